```python
import jax, jax.numpy as jnp
from jax import lax
import numpy as np

D_MODEL = 2048
BATCH = 8
SEQ = 4096
DEPTH = 2

HEAD_DIM = 64
A_HEADS = 8
A_WIDTH = A_HEADS * HEAD_DIM
CHUNK = 128
B_WIDTH = 768
CONV_WIDTH = 3
DILATION_PATTERNS = ((128, 1), (512, 4), (2048, 16))
C_HEADS_PER_PATTERN = 4
C_HEADS = C_HEADS_PER_PATTERN * len(DILATION_PATTERNS)
C_WIDTH = C_HEADS * HEAD_DIM
D_MIX = A_WIDTH + B_WIDTH + C_WIDTH
PROJ_SIZES = (A_WIDTH, A_WIDTH, B_WIDTH, B_WIDTH, B_WIDTH, C_WIDTH, C_WIDTH, C_WIDTH)
PROJ_SPLITS = tuple(int(s) for s in np.cumsum(PROJ_SIZES)[:-1])
D_IN_PROJ = sum(PROJ_SIZES)
D_FF = 4 * D_MODEL
EPS = 1e-6

kernel_name = 'hymba_style_sgu_shortconv_dilated_attn'


def rms_norm(x, g):
    x32 = x.astype(jnp.float32)
    y = x32 * lax.rsqrt(jnp.mean(x32 * x32, axis=-1, keepdims=True) + EPS)
    return (y * g.astype(jnp.float32)).astype(x.dtype)


def spatial_gating(u, v, w_s, b_s):
    bsz, s = u.shape[:2]
    n_chunks = s // CHUNK
    vc = v.reshape(bsz, n_chunks, CHUNK, A_HEADS, HEAD_DIM)
    w_causal = jnp.tril(w_s)
    mixed = jnp.einsum('hij,bcjhd->bcihd', w_causal, vc) + b_s.T[None, None, :, :, None]
    return u * mixed.reshape(bsz, s, A_WIDTH)


def short_gated_conv(b_gate, c_gate, xb, w_conv):
    z = c_gate * xb
    zp = jnp.pad(z, ((0, 0), (CONV_WIDTH - 1, 0), (0, 0)))
    s = z.shape[1]
    conv = sum(w_conv[i] * zp[:, i:i + s] for i in range(CONV_WIDTH))
    return b_gate * conv


def head_rms_norm(x, g):
    x32 = x.astype(jnp.float32)
    return x32 * lax.rsqrt(jnp.mean(x32 * x32, axis=-1, keepdims=True) + EPS) * g.astype(jnp.float32)


def dilated_window_attention(q, k, v, window, dilation):
    bsz, s, h, d = q.shape
    length = s // dilation
    blk = window // dilation
    nb = -(-length // blk)
    lp = nb * blk

    def to_sub(t):
        t = t.reshape(bsz, length, dilation, h, d).transpose(0, 2, 1, 3, 4)
        t = jnp.pad(t, ((0, 0), (0, 0), (0, lp - length), (0, 0), (0, 0)))
        return t.reshape(bsz, dilation, nb, blk, h, d)

    qb = to_sub(q)
    kb = to_sub(k)
    vb = to_sub(v.astype(jnp.float32))
    pad_prev = ((0, 0), (0, 0), (1, 0), (0, 0), (0, 0), (0, 0))
    kcat = jnp.concatenate([jnp.pad(kb, pad_prev)[:, :, :-1], kb], axis=3)
    vcat = jnp.concatenate([jnp.pad(vb, pad_prev)[:, :, :-1], vb], axis=3)

    scores = jnp.einsum('brnqhd,brnkhd->brnhqk', qb, kcat) * (HEAD_DIM ** -0.5)
    qi = jnp.arange(blk)[:, None]
    kj = jnp.arange(2 * blk)[None, :]
    band = (kj >= qi) & (kj <= qi + blk)
    has_prev = jnp.arange(nb)[:, None, None] > 0
    mask = band[None] & (has_prev | (kj >= blk)[None])
    scores = jnp.where(mask[None, None, :, None], scores, -jnp.inf)
    m = jnp.max(scores, axis=-1, keepdims=True)
    e = jnp.exp(scores - m)
    den = jnp.sum(e, axis=-1, keepdims=True)
    o = jnp.einsum('brnhqk,brnkhd->brnhqd', e, vcat) / den
    lse = (m + jnp.log(den))[..., 0]

    o = o.transpose(0, 1, 2, 4, 3, 5).reshape(bsz, dilation, lp, h, d)[:, :, :length]
    o = o.transpose(0, 2, 1, 3, 4).reshape(bsz, s, h, d)
    lse = lse.transpose(0, 1, 2, 4, 3).reshape(bsz, dilation, lp, h)[:, :, :length]
    lse = lse.transpose(0, 2, 1, 3).reshape(bsz, s, h)
    return o, lse


def dilated_mixture(q, k, v, q_g, k_g):
    bsz, s = q.shape[:2]
    q = head_rms_norm(q.reshape(bsz, s, C_HEADS, HEAD_DIM), q_g)
    k = head_rms_norm(k.reshape(bsz, s, C_HEADS, HEAD_DIM), k_g)
    v = v.reshape(bsz, s, C_HEADS, HEAD_DIM)
    outs, lses = [], []
    for g, (window, dilation) in enumerate(DILATION_PATTERNS):
        sl = slice(g * C_HEADS_PER_PATTERN, (g + 1) * C_HEADS_PER_PATTERN)
        o, lse = dilated_window_attention(q[:, :, sl], k[:, :, sl], v[:, :, sl], window, dilation)
        outs.append(o)
        lses.append(lse)
    alpha = jax.nn.softmax(jnp.stack(lses, axis=0), axis=0)
    y = jnp.stack(outs, axis=0) * alpha[..., None]
    return y.transpose(1, 2, 0, 3, 4).reshape(bsz, s, C_WIDTH).astype(v.dtype)


def _fwd_setup_inputs(seed: int = 0) -> dict:
    key = jax.random.key(seed)
    ks = jax.random.split(key, 12)
    n = jax.random.normal
    f32 = jnp.float32
    return {
        'x': n(ks[0], (BATCH, SEQ, D_MODEL), f32),
        'attn_norm': 1.0 + 0.02 * n(ks[1], (DEPTH, D_MODEL), f32),
        'w_in': n(ks[2], (DEPTH, D_MODEL, D_IN_PROJ), f32) * D_MODEL ** -0.5,
        'sgu_w': n(ks[3], (DEPTH, A_HEADS, CHUNK, CHUNK), f32) * CHUNK ** -0.5,
        'sgu_b': 1.0 + 0.1 * n(ks[4], (DEPTH, A_HEADS, CHUNK), f32),
        'conv_w': n(ks[5], (DEPTH, CONV_WIDTH, B_WIDTH), f32) * CONV_WIDTH ** -0.5,
        'q_norm': 1.0 + 0.02 * n(ks[6], (DEPTH, HEAD_DIM), f32),
        'k_norm': 1.0 + 0.02 * n(ks[7], (DEPTH, HEAD_DIM), f32),
        'w_out': n(ks[8], (DEPTH, D_MIX, D_MODEL), f32) * D_MIX ** -0.5,
        'mlp_norm': 1.0 + 0.02 * n(ks[9], (DEPTH, D_MODEL), f32),
        'w_mlp_in': n(ks[10], (DEPTH, D_MODEL, D_FF), f32) * D_MODEL ** -0.5,
        'w_mlp_out': n(ks[11], (DEPTH, D_FF, D_MODEL), f32) * D_FF ** -0.5,
    }


def _fwd_reference(x, attn_norm, w_in, sgu_w, sgu_b, conv_w, q_norm, k_norm, w_out,
              mlp_norm, w_mlp_in, w_mlp_out):
    for l in range(DEPTH):
        h = rms_norm(x, attn_norm[l])
        p = h @ w_in[l]
        a_u, a_v, b_b, b_c, b_x, q, k, v = jnp.split(p, PROJ_SPLITS, axis=-1)
        y_a = spatial_gating(a_u, a_v, sgu_w[l], sgu_b[l])
        y_b = short_gated_conv(b_b, b_c, b_x, conv_w[l])
        y_c = dilated_mixture(q, k, v, q_norm[l], k_norm[l])
        x = x + jnp.concatenate([y_a, y_b, y_c], axis=-1) @ w_out[l]
        h = rms_norm(x, mlp_norm[l])
        x = x + jnp.square(jax.nn.relu(h @ w_mlp_in[l])) @ w_mlp_out[l]
    return x


import jax as _jax
import jax.numpy as _jnp

TWIN_FORMAT = 'train_step'
FWD_PARAMS = ['x', 'attn_norm', 'w_in', 'sgu_w', 'sgu_b', 'conv_w', 'q_norm', 'k_norm', 'w_out', 'mlp_norm', 'w_mlp_in', 'w_mlp_out']
TWIN_WEIGHTS = ['attn_norm', 'w_in', 'sgu_w', 'sgu_b', 'conv_w', 'q_norm', 'k_norm', 'w_out', 'mlp_norm', 'w_mlp_in', 'w_mlp_out']
TWIN_DIFF_INPUT = 'x'
TWIN_INPUTS = ['x', 'attn_norm', 'w_in', 'sgu_w', 'sgu_b', 'conv_w', 'q_norm', 'k_norm', 'w_out', 'mlp_norm', 'w_mlp_in', 'w_mlp_out', 'loss_target', 'm_attn_norm', 'm_w_in', 'm_sgu_w', 'm_sgu_b', 'm_conv_w', 'm_q_norm', 'm_k_norm', 'm_w_out', 'm_mlp_norm', 'm_w_mlp_in', 'm_w_mlp_out', 'v_attn_norm', 'v_w_in', 'v_sgu_w', 'v_sgu_b', 'v_conv_w', 'v_q_norm', 'v_k_norm', 'v_w_out', 'v_mlp_norm', 'v_w_mlp_in', 'v_w_mlp_out']
TWIN_OUTPUTS = ['loss', 'grad_x', 'grad_attn_norm', 'grad_w_in', 'grad_sgu_w', 'grad_sgu_b', 'grad_conv_w', 'grad_q_norm', 'grad_k_norm', 'grad_w_out', 'grad_mlp_norm', 'grad_w_mlp_in', 'grad_w_mlp_out', 'delta_attn_norm', 'delta_w_in', 'delta_sgu_w', 'delta_sgu_b', 'delta_conv_w', 'delta_q_norm', 'delta_k_norm', 'delta_w_out', 'delta_mlp_norm', 'delta_w_mlp_in', 'delta_w_mlp_out', 'new_m_attn_norm', 'new_m_w_in', 'new_m_sgu_w', 'new_m_sgu_b', 'new_m_conv_w', 'new_m_q_norm', 'new_m_k_norm', 'new_m_w_out', 'new_m_mlp_norm', 'new_m_w_mlp_in', 'new_m_w_mlp_out', 'new_v_attn_norm', 'new_v_w_in', 'new_v_sgu_w', 'new_v_sgu_b', 'new_v_conv_w', 'new_v_q_norm', 'new_v_k_norm', 'new_v_w_out', 'new_v_mlp_norm', 'new_v_w_mlp_in', 'new_v_w_mlp_out']
TWIN_LEAF_KINDS = {'loss': 'loss', 'grad_x': 'grad_x', 'grad_attn_norm': 'grad_w', 'grad_w_in': 'grad_w', 'grad_sgu_w': 'grad_w', 'grad_sgu_b': 'grad_w', 'grad_conv_w': 'grad_w', 'grad_q_norm': 'grad_w', 'grad_k_norm': 'grad_w', 'grad_w_out': 'grad_w', 'grad_mlp_norm': 'grad_w', 'grad_w_mlp_in': 'grad_w', 'grad_w_mlp_out': 'grad_w', 'delta_attn_norm': 'delta_w', 'delta_w_in': 'delta_w', 'delta_sgu_w': 'delta_w', 'delta_sgu_b': 'delta_w', 'delta_conv_w': 'delta_w', 'delta_q_norm': 'delta_w', 'delta_k_norm': 'delta_w', 'delta_w_out': 'delta_w', 'delta_mlp_norm': 'delta_w', 'delta_w_mlp_in': 'delta_w', 'delta_w_mlp_out': 'delta_w', 'new_m_attn_norm': 'new_m', 'new_m_w_in': 'new_m', 'new_m_sgu_w': 'new_m', 'new_m_sgu_b': 'new_m', 'new_m_conv_w': 'new_m', 'new_m_q_norm': 'new_m', 'new_m_k_norm': 'new_m', 'new_m_w_out': 'new_m', 'new_m_mlp_norm': 'new_m', 'new_m_w_mlp_in': 'new_m', 'new_m_w_mlp_out': 'new_m', 'new_v_attn_norm': 'new_v', 'new_v_w_in': 'new_v', 'new_v_sgu_w': 'new_v', 'new_v_sgu_b': 'new_v', 'new_v_conv_w': 'new_v', 'new_v_q_norm': 'new_v', 'new_v_k_norm': 'new_v', 'new_v_w_out': 'new_v', 'new_v_mlp_norm': 'new_v', 'new_v_w_mlp_in': 'new_v', 'new_v_w_mlp_out': 'new_v'}


def _forward(args):
    return _fwd_reference(*[args[k] for k in FWD_PARAMS])


def _output_shape():
    def fwd():
        inp = _fwd_setup_inputs(0)
        return _fwd_reference(*[inp[k] for k in FWD_PARAMS])
    out = _jax.eval_shape(fwd)
    return out.shape, out.dtype

N_MICROBATCH = 1
ADAM_LR = 0.001
ADAM_B1 = 0.9
ADAM_B2 = 0.999
ADAM_EPS = 1e-08
ADAM_WD = 0.01
ADAM_STEP = 10
PER_EXAMPLE_BATCH_AXIS = {'x': 0, 'loss_target': 0}
SHARED_INPUTS = []
_WEIGHT_DTYPES = {'attn_norm': _jnp.float32, 'w_in': _jnp.float32, 'sgu_w': _jnp.float32, 'sgu_b': _jnp.float32, 'conv_w': _jnp.float32, 'q_norm': _jnp.float32, 'k_norm': _jnp.float32, 'w_out': _jnp.float32, 'mlp_norm': _jnp.float32, 'w_mlp_in': _jnp.float32, 'w_mlp_out': _jnp.float32}
MOMENT_SCALE = {'attn_norm': 2.617974e+01, 'w_in': 1.209233e+00, 'sgu_w': 5.103888e-01, 'sgu_b': 7.869251e+00, 'conv_w': 9.199815e+00, 'q_norm': 4.158016e-01, 'k_norm': 4.153924e-01, 'w_out': 1.615313e+00, 'mlp_norm': 4.854935e+01, 'w_mlp_in': 1.634956e+00, 'w_mlp_out': 7.368847e+00}


def _to_microbatches(a, axis):
    t = _jnp.moveaxis(a, axis, 0)
    t = t.reshape((N_MICROBATCH, t.shape[0] // N_MICROBATCH) + t.shape[1:])
    return _jnp.moveaxis(t, 1, axis + 1)


def setup_inputs(seed: int = 0) -> dict:
    inp = _fwd_setup_inputs(seed)
    key = _jax.random.fold_in(_jax.random.key(seed), 7919)
    shape, _ = _output_shape()
    out = dict(inp)
    out["loss_target"] = _jax.random.normal(_jax.random.fold_in(key, 0), shape, _jnp.float32)
    for i, name in enumerate(TWIN_WEIGHTS):
        w = inp[name].astype(_jnp.float32)
        if MOMENT_SCALE is None:
            s = _jnp.sqrt(_jnp.mean(_jnp.square(w)) + 1e-30)
        else:
            s = MOMENT_SCALE[name]
        km, kv = _jax.random.split(_jax.random.fold_in(key, i + 1))
        out[name] = w
        out["m_" + name] = s * _jax.random.normal(km, w.shape, _jnp.float32)
        out["v_" + name] = (s * s) * _jax.random.uniform(kv, w.shape, _jnp.float32, 0.5, 1.5)
    if N_MICROBATCH > 1:
        for name, axis in PER_EXAMPLE_BATCH_AXIS.items():
            out[name] = _to_microbatches(out[name], axis)
    return {'x': out['x'], 'attn_norm': out['attn_norm'], 'w_in': out['w_in'], 'sgu_w': out['sgu_w'], 'sgu_b': out['sgu_b'], 'conv_w': out['conv_w'], 'q_norm': out['q_norm'], 'k_norm': out['k_norm'], 'w_out': out['w_out'], 'mlp_norm': out['mlp_norm'], 'w_mlp_in': out['w_mlp_in'], 'w_mlp_out': out['w_mlp_out'], 'loss_target': out['loss_target'], 'm_attn_norm': out['m_attn_norm'], 'm_w_in': out['m_w_in'], 'm_sgu_w': out['m_sgu_w'], 'm_sgu_b': out['m_sgu_b'], 'm_conv_w': out['m_conv_w'], 'm_q_norm': out['m_q_norm'], 'm_k_norm': out['m_k_norm'], 'm_w_out': out['m_w_out'], 'm_mlp_norm': out['m_mlp_norm'], 'm_w_mlp_in': out['m_w_mlp_in'], 'm_w_mlp_out': out['m_w_mlp_out'], 'v_attn_norm': out['v_attn_norm'], 'v_w_in': out['v_w_in'], 'v_sgu_w': out['v_sgu_w'], 'v_sgu_b': out['v_sgu_b'], 'v_conv_w': out['v_conv_w'], 'v_q_norm': out['v_q_norm'], 'v_k_norm': out['v_k_norm'], 'v_w_out': out['v_w_out'], 'v_mlp_norm': out['v_mlp_norm'], 'v_w_mlp_in': out['v_w_mlp_in'], 'v_w_mlp_out': out['v_w_mlp_out']}


def _loss(weights, diff, rest, loss_target):
    with _jax.named_scope("forward"):
        args = {**rest, TWIN_DIFF_INPUT: diff, **{k: w.astype(_WEIGHT_DTYPES[k]) for k, w in weights.items()}}
        y = _forward(args)
    with _jax.named_scope("loss_head"):
        err = _jnp.square(y.astype(_jnp.float32) - loss_target)
        return 0.5 * _jnp.sum(_jnp.mean(err, axis=-1)) if err.ndim else 0.5 * err


def _adamw(w, g, m, v):
    m = ADAM_B1 * m + (1.0 - ADAM_B1) * g
    v = ADAM_B2 * v + (1.0 - ADAM_B2) * _jnp.square(g)
    m_hat = m / (1.0 - ADAM_B1 ** ADAM_STEP)
    v_hat = v / (1.0 - ADAM_B2 ** ADAM_STEP)
    delta = -ADAM_LR * (m_hat / (_jnp.sqrt(v_hat) + ADAM_EPS) + ADAM_WD * w)
    return delta, m, v


def reference(x, attn_norm, w_in, sgu_w, sgu_b, conv_w, q_norm, k_norm, w_out, mlp_norm, w_mlp_in, w_mlp_out, loss_target, m_attn_norm, m_w_in, m_sgu_w, m_sgu_b, m_conv_w, m_q_norm, m_k_norm, m_w_out, m_mlp_norm, m_w_mlp_in, m_w_mlp_out, v_attn_norm, v_w_in, v_sgu_w, v_sgu_b, v_conv_w, v_q_norm, v_k_norm, v_w_out, v_mlp_norm, v_w_mlp_in, v_w_mlp_out):
    given = dict(x=x, attn_norm=attn_norm, w_in=w_in, sgu_w=sgu_w, sgu_b=sgu_b, conv_w=conv_w, q_norm=q_norm, k_norm=k_norm, w_out=w_out, mlp_norm=mlp_norm, w_mlp_in=w_mlp_in, w_mlp_out=w_mlp_out, loss_target=loss_target, m_attn_norm=m_attn_norm, m_w_in=m_w_in, m_sgu_w=m_sgu_w, m_sgu_b=m_sgu_b, m_conv_w=m_conv_w, m_q_norm=m_q_norm, m_k_norm=m_k_norm, m_w_out=m_w_out, m_mlp_norm=m_mlp_norm, m_w_mlp_in=m_w_mlp_in, m_w_mlp_out=m_w_mlp_out, v_attn_norm=v_attn_norm, v_w_in=v_w_in, v_sgu_w=v_sgu_w, v_sgu_b=v_sgu_b, v_conv_w=v_conv_w, v_q_norm=v_q_norm, v_k_norm=v_k_norm, v_w_out=v_w_out, v_mlp_norm=v_mlp_norm, v_w_mlp_in=v_w_mlp_in, v_w_mlp_out=v_w_mlp_out)
    weights = {n: given[n] for n in TWIN_WEIGHTS}
    shared = {n: given[n] for n in SHARED_INPUTS}
    per_example = {n: given[n] for n in ['x']}
    grad_fn = _jax.value_and_grad(_loss, argnums=(0, 1))

    def one_microbatch(ex, loss_target):
        ex = dict(ex)
        diff = ex.pop(TWIN_DIFF_INPUT)
        return grad_fn(weights, diff, {**shared, **ex}, loss_target)

    if N_MICROBATCH == 1:
        loss, (grad_w, grad_x) = one_microbatch(per_example, given["loss_target"])
    else:
        def body(carry, xs):
            loss_sum, grad_sum = carry
            l_k, (gw_k, gx_k) = one_microbatch(xs[0], xs[1])
            with _jax.named_scope("update"):
                return (loss_sum + l_k, _jax.tree.map(_jnp.add, grad_sum, gw_k)), gx_k

        init = (_jnp.zeros((), _jnp.float32), _jax.tree.map(_jnp.zeros_like, weights))
        (loss, grad_w), grad_x = _jax.lax.scan(body, init, (per_example, given["loss_target"]))
    with _jax.named_scope("update"):
        delta_w, new_m, new_v = {}, {}, {}
        for n in TWIN_WEIGHTS:
            delta_w[n], new_m[n], new_v[n] = _adamw(weights[n], grad_w[n], given["m_" + n], given["v_" + n])
    return (loss, grad_x, *[grad_w[n] for n in TWIN_WEIGHTS], *[delta_w[n] for n in TWIN_WEIGHTS],
            *[new_m[n] for n in TWIN_WEIGHTS], *[new_v[n] for n in TWIN_WEIGHTS])
```

```python
import jax
import jax.numpy as jnp
from jax import lax
from jax.experimental import pallas as pl
from jax.experimental.pallas import tpu as pltpu

F32 = jnp.float32
BF16 = jnp.bfloat16
SDS = jax.ShapeDtypeStruct

EPS = 1e-6
HEAD_DIM = 64
A_HEADS = 8
A_WIDTH = 512
CHUNK = 128
B_WIDTH = 768
C_WIDTH = 768
N_PATTERNS = 3
PATTERN_DILATION = (1, 4, 16)
PW = 256
D_IN_PROJ = 5632
OFF_AU, OFF_AV, OFF_BB, OFF_BC, OFF_BX, OFF_Q, OFF_K, OFF_V = 0, 512, 1024, 1792, 2560, 3328, 4096, 4864
N_CHIPS = 4
N_DEV = 8
BLK = 128

ADAM_LR, ADAM_B1, ADAM_B2, ADAM_EPS, ADAM_WD, ADAM_STEP = 0.001, 0.9, 0.999, 1e-08, 0.01, 10

V7X_VMEM_LIMIT = 56 * 1024 * 1024
MESH = pl.DeviceIdType.MESH
NEG = -1e30


def _cp(n_axes):
    return pltpu.CompilerParams(dimension_semantics=("arbitrary",) * n_axes, vmem_limit_bytes=V7X_VMEM_LIMIT)


def _hbm_spec():
    return pl.BlockSpec(memory_space=pl.ANY)


def _norm_matmul(name, x, g, wg, layer, out_dtype):
    S, D = x.shape
    ns, _, _, Ns = wg.shape
    tm = min(512, S)

    def body(x_ref, g_ref, w_ref, o_ref, h_ref, hs_ref):
        @pl.when(pl.program_id(1) == 0)
        def _():
            xv = x_ref[...]
            y = xv * lax.rsqrt(jnp.mean(xv * xv, axis=-1, keepdims=True) + EPS) * g_ref[...]
            hb = y.astype(BF16)
            hs_ref[...] = hb
            h_ref[...] = hb
        o_ref[...] = jnp.dot(hs_ref[...], w_ref[...], preferred_element_type=F32).astype(o_ref.dtype)

    return pl.pallas_call(
        body, name=name, grid=(S // tm, ns),
        in_specs=[pl.BlockSpec((tm, D), lambda i, s: (i, 0)),
                  pl.BlockSpec((1, D), lambda i, s: (0, 0)),
                  pl.BlockSpec((None, None, D, Ns), lambda i, s: (s, layer, 0, 0))],
        out_specs=[pl.BlockSpec((tm, Ns), lambda i, s: (i, s)),
                   pl.BlockSpec((tm, D), lambda i, s: (i, 0))],
        out_shape=[SDS((S, ns * Ns), out_dtype), SDS((S, D), BF16)],
        scratch_shapes=[pltpu.VMEM((tm, D), BF16)],
        compiler_params=_cp(2),
    )(x, g, wg)


def _matmul(name, a, b, out_shape, out_dtype, *, grid, a_spec, b_spec, o_spec, contract, acc_shape,
            extras=(), extra_specs=(), a_pre=None, epi=None):
    nk = grid[2]
    n_ex = len(extras)
    dims = (((contract[0],), (contract[1],)), ((), ()))

    def body(a_ref, b_ref, *rest):
        ex = rest[:n_ex]
        o_ref = rest[n_ex]
        acc_ref = rest[n_ex + 1]
        k = pl.program_id(2)

        @pl.when(k == 0)
        def _():
            acc_ref[...] = jnp.zeros_like(acc_ref)

        av = a_ref[...]
        if a_pre is not None:
            av = a_pre(av)
        acc_ref[...] += lax.dot_general(av, b_ref[...], dims, preferred_element_type=F32)

        @pl.when(k == nk - 1)
        def _():
            r = acc_ref[...]
            if epi is not None:
                r = epi(r, *[e[...] for e in ex])
            o_ref[...] = r.astype(o_ref.dtype)

    return pl.pallas_call(
        body, name=name, grid=grid,
        in_specs=[a_spec, b_spec, *extra_specs],
        out_specs=o_spec,
        out_shape=SDS(out_shape, out_dtype),
        scratch_shapes=[pltpu.VMEM(acc_shape, F32)],
        compiler_params=_cp(3),
    )(a, b, *extras)


def _relu2_bf16(t):
    r = jnp.maximum(t.astype(F32), 0.0)
    return (r * r).astype(BF16)


def _loss_kernel(y, t):
    S, D = y.shape
    tm = min(256, S)

    def body(y_ref, t_ref, dy_ref, dyb_ref, l_ref):
        @pl.when(pl.program_id(0) == 0)
        def _():
            l_ref[...] = jnp.zeros_like(l_ref)
        e = y_ref[...] - t_ref[...]
        l_ref[...] += jnp.sum(e * e, axis=0, keepdims=True)
        dy = e * (1.0 / D)
        dy_ref[...] = dy
        dyb_ref[...] = dy.astype(BF16)

    row = pl.BlockSpec((tm, D), lambda i: (i, 0))
    return pl.pallas_call(
        body, name="loss_head", grid=(S // tm,),
        in_specs=[row, row],
        out_specs=[row, row, pl.BlockSpec((1, D), lambda i: (0, 0))],
        out_shape=[SDS((S, D), F32), SDS((S, D), BF16), SDS((1, D), F32)],
        compiler_params=_cp(1),
    )(y, t)


def _rmsnorm_bwd(name, dh, x, g, dres):
    S, D = x.shape
    tm = min(256, S)

    def body(dh_ref, x_ref, g_ref, dres_ref, dx_ref, dxb_ref, dg_ref):
        @pl.when(pl.program_id(0) == 0)
        def _():
            dg_ref[...] = jnp.zeros_like(dg_ref)
        xv = x_ref[...]
        dhv = dh_ref[...]
        rstd = lax.rsqrt(jnp.mean(xv * xv, axis=-1, keepdims=True) + EPS)
        xhat = xv * rstd
        dg_ref[...] += jnp.sum(dhv * xhat, axis=0, keepdims=True)
        dxn = dhv * g_ref[...]
        dx = dres_ref[...] + rstd * (dxn - xhat * jnp.mean(dxn * xhat, axis=-1, keepdims=True))
        dx_ref[...] = dx
        dxb_ref[...] = dx.astype(BF16)

    row = pl.BlockSpec((tm, D), lambda i: (i, 0))
    vec = pl.BlockSpec((1, D), lambda i: (0, 0))
    return pl.pallas_call(
        body, name=name, grid=(S // tm,),
        in_specs=[row, row, vec, row],
        out_specs=[row, row, vec],
        out_shape=[SDS((S, D), F32), SDS((S, D), BF16), SDS((1, D), F32)],
        compiler_params=_cp(1),
    )(dh, x, g, dres)


def _adamw(name, w, g, m, v):
    R, C = w.shape
    tr = 256 if R % 256 == 0 else R
    c1 = 1.0 - ADAM_B1 ** ADAM_STEP
    c2 = 1.0 - ADAM_B2 ** ADAM_STEP

    def body(w_ref, g_ref, m_ref, v_ref, d_ref, nm_ref, nv_ref):
        gv = g_ref[...]
        nm = ADAM_B1 * m_ref[...] + (1.0 - ADAM_B1) * gv
        nv = ADAM_B2 * v_ref[...] + (1.0 - ADAM_B2) * (gv * gv)
        m_hat = nm / c1
        v_hat = nv / c2
        d_ref[...] = -ADAM_LR * (m_hat / (jnp.sqrt(v_hat) + ADAM_EPS) + ADAM_WD * w_ref[...])
        nm_ref[...] = nm
        nv_ref[...] = nv

    blk = pl.BlockSpec((tr, C), lambda i: (i, 0))
    return pl.pallas_call(
        body, name=name, grid=(R // tr,),
        in_specs=[blk] * 4, out_specs=[blk] * 3,
        out_shape=[SDS((R, C), F32)] * 3,
        compiler_params=_cp(1),
    )(w, g, m, v)


def _pair_select(lane, lo, hi):
    return jnp.where(lane < HEAD_DIM, lo, hi)


def _sgu_fwd(name, p, wt, bb):
    S = p.shape[0]

    def body(u_ref, v_ref, wt_ref, bb_ref, o_ref):
        lane = lax.broadcasted_iota(jnp.int32, (CHUNK, 128), 1)
        for pp in range(A_HEADS // 2):
            cs = slice(128 * pp, 128 * (pp + 1))
            vb = v_ref[:, cs].astype(BF16)
            mixed = _pair_select(lane,
                                 jnp.dot(wt_ref[2 * pp], vb, preferred_element_type=F32),
                                 jnp.dot(wt_ref[2 * pp + 1], vb, preferred_element_type=F32)) + bb_ref[:, cs]
            o_ref[:, cs] = (u_ref[:, cs] * mixed).astype(o_ref.dtype)

    return pl.pallas_call(
        body, name=name, grid=(S // CHUNK,),
        in_specs=[pl.BlockSpec((CHUNK, A_WIDTH), lambda c: (c, OFF_AU // A_WIDTH)),
                  pl.BlockSpec((CHUNK, A_WIDTH), lambda c: (c, OFF_AV // A_WIDTH)),
                  pl.BlockSpec((A_HEADS, CHUNK, CHUNK), lambda c: (0, 0, 0)),
                  pl.BlockSpec((CHUNK, A_WIDTH), lambda c: (0, 0))],
        out_specs=pl.BlockSpec((CHUNK, A_WIDTH), lambda c: (c, 0)),
        out_shape=SDS((S, A_WIDTH), BF16),
        compiler_params=_cp(1),
    )(p, p, wt, bb)


def _sgu_bwd(name, p, dycat, wt, wtt, bb):
    S = p.shape[0]

    def body(u_ref, v_ref, dy_ref, wt_ref, wtt_ref, bb_ref, du_ref, dv_ref, dw_ref, db_ref, dbacc_ref):
        c = pl.program_id(0)

        @pl.when(c == 0)
        def _():
            dw_ref[...] = jnp.zeros_like(dw_ref)
            dbacc_ref[...] = jnp.zeros_like(dbacc_ref)

        lane = lax.broadcasted_iota(jnp.int32, (CHUNK, 128), 1)
        row = lax.broadcasted_iota(jnp.int32, (CHUNK, 128), 0)
        causal = row >= lane
        for pp in range(A_HEADS // 2):
            cs = slice(128 * pp, 128 * (pp + 1))
            v = v_ref[:, cs]
            vb = v.astype(BF16)
            u = u_ref[:, cs]
            dy = dy_ref[:, cs]
            mixed = _pair_select(lane,
                                 jnp.dot(wt_ref[2 * pp], vb, preferred_element_type=F32),
                                 jnp.dot(wt_ref[2 * pp + 1], vb, preferred_element_type=F32)) + bb_ref[:, cs]
            du_ref[:, cs] = (dy * mixed).astype(du_ref.dtype)
            dm = dy * u
            dmb = dm.astype(BF16)
            dv = _pair_select(lane,
                              jnp.dot(wtt_ref[2 * pp], dmb, preferred_element_type=F32),
                              jnp.dot(wtt_ref[2 * pp + 1], dmb, preferred_element_type=F32))
            dv_ref[:, cs] = dv.astype(dv_ref.dtype)
            dbacc_ref[:, cs] += dm
            nt = (((1,), (1,)), ((), ()))
            dm_lo = jnp.where(lane < HEAD_DIM, dm, 0.0).astype(BF16)
            dm_hi = jnp.where(lane >= HEAD_DIM, dm, 0.0).astype(BF16)
            dw_ref[2 * pp] += jnp.where(causal, lax.dot_general(dm_lo, vb, nt, preferred_element_type=F32), 0.0)
            dw_ref[2 * pp + 1] += jnp.where(causal, lax.dot_general(dm_hi, vb, nt, preferred_element_type=F32), 0.0)

        @pl.when(c == S // CHUNK - 1)
        def _():
            out = jnp.zeros((CHUNK, 128), F32)
            for pp in range(A_HEADS // 2):
                acc = dbacc_ref[:, 128 * pp:128 * (pp + 1)]
                s_lo = jnp.sum(jnp.where(lane < HEAD_DIM, acc, 0.0), axis=1, keepdims=True)
                s_hi = jnp.sum(jnp.where(lane >= HEAD_DIM, acc, 0.0), axis=1, keepdims=True)
                out = jnp.where(lane == 2 * pp, s_lo, out)
                out = jnp.where(lane == 2 * pp + 1, s_hi, out)
            db_ref[...] = out

    chunk = lambda col: pl.BlockSpec((CHUNK, A_WIDTH), lambda c: (c, col))
    wspec = pl.BlockSpec((A_HEADS, CHUNK, CHUNK), lambda c: (0, 0, 0))
    return pl.pallas_call(
        body, name=name, grid=(S // CHUNK,),
        in_specs=[chunk(OFF_AU // A_WIDTH), chunk(OFF_AV // A_WIDTH), chunk(0), wspec, wspec,
                  pl.BlockSpec((CHUNK, A_WIDTH), lambda c: (0, 0))],
        out_specs=[chunk(0), chunk(0), wspec, pl.BlockSpec((CHUNK, 128), lambda c: (0, 0))],
        out_shape=[SDS((S, A_WIDTH), BF16), SDS((S, A_WIDTH), BF16),
                   SDS((A_HEADS, CHUNK, CHUNK), F32), SDS((CHUNK, 128), F32)],
        scratch_shapes=[pltpu.VMEM((CHUNK, A_WIDTH), F32)],
        compiler_params=_cp(1),
    )(p, p, dycat, wt, wtt, bb)


CONV_HALO = 8


def _shift_down(a, halo, k):
    T = a.shape[0]
    row = lax.broadcasted_iota(jnp.int32, a.shape, 0)
    out = pltpu.roll(a, k, 0)
    for r in range(k):
        out = jnp.where(row == r, halo[CONV_HALO - k + r:CONV_HALO - k + r + 1, :], out)
    return out


def _shift_up(a, halo, k):
    T = a.shape[0]
    row = lax.broadcasted_iota(jnp.int32, a.shape, 0)
    out = pltpu.roll(a, T - k, 0)
    for r in range(k):
        out = jnp.where(row == T - k + r, halo[r:r + 1, :], out)
    return out


def _conv_specs(S, T):
    hb = T // CONV_HALO
    last = S // CONV_HALO - 1
    tile = lambda col0: pl.BlockSpec((T, 128), lambda j, i: (i, col0 + j))
    prev = lambda col0: pl.BlockSpec((CONV_HALO, 128), lambda j, i: (jnp.maximum(i * hb - 1, 0), col0 + j))
    nxt = lambda col0: pl.BlockSpec((CONV_HALO, 128), lambda j, i: (jnp.minimum((i + 1) * hb, last), col0 + j))
    return tile, prev, nxt


def _conv_fwd(name, p, w):
    S = p.shape[0]
    T = min(512, S)
    tile, prev, _ = _conv_specs(S, T)
    cb, cc, cx = OFF_BB // 128, OFF_BC // 128, OFF_BX // 128

    def body(b_ref, c_ref, x_ref, ch_ref, xh_ref, w_ref, o_ref):
        i = pl.program_id(1)
        z = c_ref[...] * x_ref[...]
        zh = jnp.where(i > 0, ch_ref[...] * xh_ref[...], 0.0)
        z1 = _shift_down(z, zh, 1)
        z2 = _shift_down(z, zh, 2)
        conv = w_ref[0:1, :] * z2 + w_ref[1:2, :] * z1 + w_ref[2:3, :] * z
        o_ref[...] = (b_ref[...] * conv).astype(o_ref.dtype)

    return pl.pallas_call(
        body, name=name, grid=(B_WIDTH // 128, S // T),
        in_specs=[tile(cb), tile(cc), tile(cx), prev(cc), prev(cx),
                  pl.BlockSpec((3, 128), lambda j, i: (0, j))],
        out_specs=tile(0),
        out_shape=SDS((S, B_WIDTH), BF16),
        compiler_params=_cp(2),
    )(p, p, p, p, p, w)


def _conv_bwd(name, p, dycat, w):
    S = p.shape[0]
    T = min(512, S)
    tile, prev, nxt = _conv_specs(S, T)
    cb, cc, cx = OFF_BB // 128, OFF_BC // 128, OFF_BX // 128
    cdy = A_WIDTH // 128
    n_i = S // T

    def body(b_ref, c_ref, x_ref, dy_ref, ch_ref, xh_ref, bn_ref, dyn_ref, w_ref,
             db_ref, dc_ref, dx_ref, dw_ref):
        i = pl.program_id(1)

        @pl.when(i == 0)
        def _():
            dw_ref[...] = jnp.zeros_like(dw_ref)

        cv = c_ref[...]
        xv = x_ref[...]
        z = cv * xv
        zh = jnp.where(i > 0, ch_ref[...] * xh_ref[...], 0.0)
        z1 = _shift_down(z, zh, 1)
        z2 = _shift_down(z, zh, 2)
        w0, w1, w2 = w_ref[0:1, :], w_ref[1:2, :], w_ref[2:3, :]
        conv = w0 * z2 + w1 * z1 + w2 * z
        dy = dy_ref[...]
        db_ref[...] = (dy * conv).astype(db_ref.dtype)
        dconv = dy * b_ref[...]
        dconv_n = jnp.where(i < n_i - 1, dyn_ref[...] * bn_ref[...], 0.0)
        dz = w2 * dconv + w1 * _shift_up(dconv, dconv_n, 1) + w0 * _shift_up(dconv, dconv_n, 2)
        dc_ref[...] = (dz * xv).astype(dc_ref.dtype)
        dx_ref[...] = (dz * cv).astype(dx_ref.dtype)
        dw_ref[0:1, :] += jnp.sum(dconv * z2, axis=0, keepdims=True)
        dw_ref[1:2, :] += jnp.sum(dconv * z1, axis=0, keepdims=True)
        dw_ref[2:3, :] += jnp.sum(dconv * z, axis=0, keepdims=True)

    wspec = pl.BlockSpec((3, 128), lambda j, i: (0, j))
    return pl.pallas_call(
        body, name=name, grid=(B_WIDTH // 128, n_i),
        in_specs=[tile(cb), tile(cc), tile(cx), tile(cdy), prev(cc), prev(cx), nxt(cb), nxt(cdy), wspec],
        out_specs=[tile(0), tile(0), tile(0), wspec],
        out_shape=[SDS((S, B_WIDTH), BF16)] * 3 + [SDS((3, B_WIDTH), F32)],
        compiler_params=_cp(2),
    )(p, p, p, dycat, p, p, p, dycat, w)


def _seg_sum(t, bd):
    hi = t.astype(BF16)
    lo = (t - hi.astype(F32)).astype(BF16)
    return jnp.dot(hi, bd, preferred_element_type=F32) + jnp.dot(lo, bd, preferred_element_type=F32)


def _head_norm(x, g, bd):
    rstd = lax.rsqrt(_seg_sum(x * x, bd) * (1.0 / HEAD_DIM) + EPS)
    xhat = x * rstd
    return xhat * g, xhat, rstd


def _head_norm_bwd(dy, g, xhat, rstd, bd):
    dxh = dy * g
    return rstd * (dxh - xhat * (_seg_sum(dxh * xhat, bd) * (1.0 / HEAD_DIM)))


def _band_mask(has_prev):
    row = lax.broadcasted_iota(jnp.int32, (BLK, 2 * BLK), 0)
    col = lax.broadcasted_iota(jnp.int32, (BLK, 2 * BLK), 1)
    first_key = jnp.where(has_prev, 0, BLK)
    return (col >= row) & (col <= row + BLK) & (col >= first_key)


def _first_of_segment(g, n, n_blocks):
    per_seg = lax.shift_right_logical(jnp.int32(n_blocks), 2 * g)
    return (n & (per_seg - 1)) == 0


def _attn_fwd(name, q3, k3, v3, gq, gk, bd):
    _, S, _ = q3.shape
    nblk = S // BLK
    nt = (((1,), (1,)), ((), ()))

    def body(q_ref, kc_ref, kp_ref, vc_ref, vp_ref, gq_ref, gk_ref, bd_ref, o_ref, lse_ref):
        g = pl.program_id(0)
        n = pl.program_id(1)
        has_prev = jnp.logical_not(_first_of_segment(g, n, nblk))
        bdv = bd_ref[...]
        qn, _, _ = _head_norm(q_ref[...], gq_ref[...], bdv)
        kn, _, _ = _head_norm(jnp.concatenate([kp_ref[...], kc_ref[...]], axis=0), gk_ref[...], bdv)
        knb = kn.astype(BF16)
        vb = jnp.concatenate([vp_ref[...], vc_ref[...]], axis=0).astype(BF16)
        band = _band_mask(has_prev)
        lane = lax.broadcasted_iota(jnp.int32, (1, PW), 1)
        o_acc = jnp.zeros((BLK, PW), F32)
        l_acc = jnp.zeros((BLK, PW), F32)
        for j in range(PW // HEAD_DIM):
            hm = (lane >= HEAD_DIM * j) & (lane < HEAD_DIM * (j + 1))
            qj = jnp.where(hm, qn, 0.0).astype(BF16)
            s = lax.dot_general(qj, knb, nt, preferred_element_type=F32) * (HEAD_DIM ** -0.5)
            s = jnp.where(band, s, NEG)
            m = jnp.max(s, axis=1, keepdims=True)
            e = jnp.exp(s - m)
            den = jnp.sum(e, axis=1, keepdims=True)
            pv = jnp.dot(e.astype(BF16), vb, preferred_element_type=F32)
            o_acc = jnp.where(hm, pv / den, o_acc)
            l_acc = jnp.where(hm, m + jnp.log(den), l_acc)
        o_ref[...] = o_acc
        lse_ref[...] = l_acc

    cur = pl.BlockSpec((None, BLK, PW), lambda g, n: (g, n, 0))
    prv = pl.BlockSpec((None, BLK, PW), lambda g, n: (g, jnp.maximum(n - 1, 0), 0))
    vec = pl.BlockSpec((1, PW), lambda g, n: (0, 0))
    return pl.pallas_call(
        body, name=name, grid=(N_PATTERNS, nblk),
        in_specs=[cur, cur, prv, cur, prv, vec, vec, pl.BlockSpec((PW, PW), lambda g, n: (0, 0))],
        out_specs=[cur, cur],
        out_shape=[SDS((N_PATTERNS, S, PW), F32)] * 2,
        compiler_params=_cp(2),
    )(q3, k3, k3, v3, v3, gq, gk, bd)


def _attn_bwd(name, q3, k3, v3, lse3, do3, c3, gq, gk, bd):
    _, S, _ = q3.shape
    nblk = S // BLK
    nt = (((1,), (1,)), ((), ()))
    tn = (((0,), (0,)), ((), ()))

    def body(q_ref, kc_ref, kp_ref, vc_ref, vp_ref, lse_ref, do_ref, c_ref, gq_ref, gk_ref, bd_ref,
             dq_ref, dk_ref, dv_ref, dgq_ref, dgk_ref, ck_ref, cv_ref):
        g = pl.program_id(0)
        n = pl.program_id(1)
        live = n < nblk
        ne = jnp.minimum(n, nblk - 1)
        has_prev = jnp.logical_not(_first_of_segment(g, ne, nblk))

        @pl.when(n == 0)
        def _():
            ck_ref[...] = jnp.zeros_like(ck_ref)
            cv_ref[...] = jnp.zeros_like(cv_ref)
            dgq_ref[...] = jnp.zeros_like(dgq_ref)
            dgk_ref[...] = jnp.zeros_like(dgk_ref)

        bdv = bd_ref[...]
        gqv = gq_ref[...]
        gkv = gk_ref[...]
        qn, qhat, qrstd = _head_norm(q_ref[...], gqv, bdv)
        kn, khat, krstd = _head_norm(jnp.concatenate([kp_ref[...], kc_ref[...]], axis=0), gkv, bdv)
        knb = kn.astype(BF16)
        vb = jnp.concatenate([vp_ref[...], vc_ref[...]], axis=0).astype(BF16)
        band = _band_mask(has_prev)
        lane = lax.broadcasted_iota(jnp.int32, (1, PW), 1)
        lse = lse_ref[...]
        do = do_ref[...]
        cc = c_ref[...]
        dqn = jnp.zeros((BLK, PW), F32)
        dkn = jnp.zeros((2 * BLK, PW), F32)
        dvv = jnp.zeros((2 * BLK, PW), F32)
        for j in range(PW // HEAD_DIM):
            hm = (lane >= HEAD_DIM * j) & (lane < HEAD_DIM * (j + 1))
            qj = jnp.where(hm, qn, 0.0).astype(BF16)
            doj = jnp.where(hm, do, 0.0).astype(BF16)
            s = lax.dot_general(qj, knb, nt, preferred_element_type=F32) * (HEAD_DIM ** -0.5)
            lse_j = jnp.max(jnp.where(hm, lse, NEG), axis=1, keepdims=True)
            c_j = jnp.max(jnp.where(hm, cc, NEG), axis=1, keepdims=True)
            prob = jnp.where(band, jnp.exp(s - lse_j), 0.0)
            dp = lax.dot_general(doj, vb, nt, preferred_element_type=F32)
            ds = (prob * (dp + c_j) * (HEAD_DIM ** -0.5)).astype(BF16)
            dqn = jnp.where(hm, jnp.dot(ds, knb, preferred_element_type=F32), dqn)
            dkn += lax.dot_general(ds, qj, tn, preferred_element_type=F32)
            dvv += lax.dot_general(prob.astype(BF16), doj, tn, preferred_element_type=F32)

        dq_ref[...] = _head_norm_bwd(dqn, gqv, qhat, qrstd, bdv).astype(dq_ref.dtype)
        dk2 = _head_norm_bwd(dkn, gkv, khat, krstd, bdv)
        keep = jnp.where(live, 1.0, 0.0)
        dgq_ref[...] += keep * jnp.sum(dqn * qhat, axis=0, keepdims=True)
        dgk_ref[...] += keep * jnp.sum(dkn * khat, axis=0, keepdims=True)
        dk_ref[...] = (ck_ref[...] + keep * dk2[:BLK]).astype(dk_ref.dtype)
        dv_ref[...] = (cv_ref[...] + keep * dvv[:BLK]).astype(dv_ref.dtype)
        ck_ref[...] = dk2[BLK:]
        cv_ref[...] = dvv[BLK:]

    last = nblk - 1
    cur = pl.BlockSpec((None, BLK, PW), lambda g, n: (g, jnp.minimum(n, last), 0))
    prv = pl.BlockSpec((None, BLK, PW), lambda g, n: (g, jnp.maximum(jnp.minimum(n, last) - 1, 0), 0))
    done = pl.BlockSpec((None, BLK, PW), lambda g, n: (g, jnp.maximum(n - 1, 0), 0))
    vec = pl.BlockSpec((1, PW), lambda g, n: (0, 0))
    gvec = pl.BlockSpec((None, 1, PW), lambda g, n: (g, 0, 0))
    return pl.pallas_call(
        body, name=name, grid=(N_PATTERNS, nblk + 1),
        in_specs=[cur, cur, prv, cur, prv, cur, cur, cur, vec, vec, pl.BlockSpec((PW, PW), lambda g, n: (0, 0))],
        out_specs=[cur, done, done, gvec, gvec],
        out_shape=[SDS((N_PATTERNS, S, PW), BF16)] * 3 + [SDS((N_PATTERNS, 1, PW), F32)] * 2,
        scratch_shapes=[pltpu.VMEM((BLK, PW), F32), pltpu.VMEM((BLK, PW), F32)],
        compiler_params=_cp(2),
    )(q3, k3, k3, v3, v3, lse3, do3, c3, gq, gk, bd)


def _mix_fwd(name, o3, lse3):
    _, S, _ = o3.shape
    tm = min(512, S)

    def body(o_ref, l_ref, y_ref):
        l = [l_ref[g] for g in range(N_PATTERNS)]
        m = jnp.maximum(jnp.maximum(l[0], l[1]), l[2])
        e = [jnp.exp(t - m) for t in l]
        inv = 1.0 / (e[0] + e[1] + e[2])
        for g in range(N_PATTERNS):
            y_ref[:, PW * g:PW * (g + 1)] = (o_ref[g] * (e[g] * inv)).astype(y_ref.dtype)

    blk3 = pl.BlockSpec((N_PATTERNS, tm, PW), lambda i: (0, i, 0))
    return pl.pallas_call(
        body, name=name, grid=(S // tm,),
        in_specs=[blk3, blk3],
        out_specs=pl.BlockSpec((tm, C_WIDTH), lambda i: (i, 0)),
        out_shape=SDS((S, C_WIDTH), BF16),
        compiler_params=_cp(1),
    )(o3, lse3)


def _mix_bwd(name, o3, lse3, dycat, bd):
    _, S, _ = o3.shape
    tm = min(512, S)
    c0 = (A_WIDTH + B_WIDTH) // PW

    def body(o_ref, l_ref, dy0_ref, dy1_ref, dy2_ref, bd_ref, do_ref, c_ref):
        bdv = bd_ref[...]
        dys = [dy0_ref[...], dy1_ref[...], dy2_ref[...]]
        l = [l_ref[g] for g in range(N_PATTERNS)]
        m = jnp.maximum(jnp.maximum(l[0], l[1]), l[2])
        e = [jnp.exp(t - m) for t in l]
        inv = 1.0 / (e[0] + e[1] + e[2])
        alpha = [t * inv for t in e]
        da = [_seg_sum(dys[g] * o_ref[g], bdv) for g in range(N_PATTERNS)]
        mean_da = alpha[0] * da[0] + alpha[1] * da[1] + alpha[2] * da[2]
        for g in range(N_PATTERNS):
            do_ref[g] = dys[g] * alpha[g]
            c_ref[g] = -alpha[g] * mean_da

    blk3 = pl.BlockSpec((N_PATTERNS, tm, PW), lambda i: (0, i, 0))
    dyspec = lambda g: pl.BlockSpec((tm, PW), lambda i: (i, c0 + g))
    return pl.pallas_call(
        body, name=name, grid=(S // tm,),
        in_specs=[blk3, blk3, dyspec(0), dyspec(1), dyspec(2), pl.BlockSpec((PW, PW), lambda i: (0, 0))],
        out_specs=[blk3, blk3],
        out_shape=[SDS((N_PATTERNS, S, PW), F32)] * 2,
        compiler_params=_cp(1),
    )(o3, lse3, dycat, dycat, dycat, bd)


def _mesh_pos():
    x, y, c = lax.axis_index("x"), lax.axis_index("y"), lax.axis_index("c")
    chips = [(1 - x, y), (x, 1 - y), (1 - x, 1 - y)]
    chip_idx = [2 * cx + cy for cx, cy in chips]
    return x, y, c, 2 * x + y, chips, chip_idx


def _half(ref_or_n, c):
    return pl.ds(c * ref_or_n, ref_or_n)


def _gather_weights(shards):
    T = len(shards)

    def body(*refs):
        ins, outs = refs[:T], refs[T:2 * T]
        send_sems, recv_sems, local_sems = refs[2 * T:]
        x, y, c, me, chips, chip_idx = _mesh_pos()
        sibling = (x, y, 1 - c)

        def slab(t, chip, half):
            hr = ins[t].shape[1] // 2
            return outs[t].at[chip, :, pl.ds(half * hr, hr), :]

        def copy(t, k, src, dst, to):
            return pltpu.make_async_remote_copy(src_ref=src, dst_ref=dst, send_sem=send_sems.at[6 * t + k],
                                                recv_sem=recv_sems.at[6 * t + k], device_id=to, device_id_type=MESH)

        started = []
        local = []
        for t in range(T):
            hr = ins[t].shape[1] // 2
            lc = pltpu.make_async_copy(ins[t], outs[t].at[me], local_sems.at[t])
            lc.start()
            local.append(lc)
            for j in range(3):
                cp = copy(t, j, ins[t].at[:, pl.ds(c * hr, hr), :], slab(t, me, c), (*chips[j], c))
                cp.start()
                started.append(cp)
        for t in range(T):
            for j in range(3):
                got = slab(t, chip_idx[j], c)
                copy(t, j, got, got, (*chips[j], c)).wait_recv()
                fw = copy(t, 3 + j, got, got, sibling)
                fw.start()
                started.append(fw)
        for t in range(T):
            for j in range(3):
                got = slab(t, chip_idx[j], 1 - c)
                copy(t, 3 + j, got, got, sibling).wait_recv()
        for cp in started:
            cp.wait_send()
        for lc in local:
            lc.wait()

    return pl.pallas_call(
        body, name="gather_weights",
        in_specs=[_hbm_spec()] * T, out_specs=[_hbm_spec()] * T,
        out_shape=[SDS((N_CHIPS,) + s.shape, s.dtype) for s in shards],
        scratch_shapes=[pltpu.SemaphoreType.DMA((6 * T,)), pltpu.SemaphoreType.DMA((6 * T,)),
                        pltpu.SemaphoreType.DMA((T,))],
    )(*shards)


def _swap_halves(grads):
    T = len(grads)

    def body(*refs):
        ins, outs = refs[:T], refs[T:2 * T]
        send_sems, recv_sems = refs[2 * T:]
        x, y, c, _, _, _ = _mesh_pos()
        cps = []
        for t in range(T):
            hr = ins[t].shape[1] // 2
            cp = pltpu.make_async_remote_copy(
                src_ref=ins[t].at[:, pl.ds((1 - c) * hr, hr), :], dst_ref=outs[t],
                send_sem=send_sems.at[t], recv_sem=recv_sems.at[t],
                device_id=(x, y, 1 - c), device_id_type=MESH)
            cp.start()
            cps.append(cp)
        for cp in cps:
            cp.wait()

    return pl.pallas_call(
        body, name="rs_swap_halves",
        in_specs=[_hbm_spec()] * T, out_specs=[_hbm_spec()] * T,
        out_shape=[SDS((g.shape[0], g.shape[1] // 2, g.shape[2]), g.dtype) for g in grads],
        scratch_shapes=[pltpu.SemaphoreType.DMA((T,)), pltpu.SemaphoreType.DMA((T,))],
    )(*grads)


def _add_my_half(name, g, r, c_arr):
    ns, R, C = g.shape
    hr = R // 2
    tr = min(256, hr)
    nt = hr // tr

    def body(c_ref, g_ref, r_ref, o_ref):
        o_ref[...] = g_ref[...] + r_ref[...]

    return pl.pallas_call(
        body, name=name,
        grid_spec=pltpu.PrefetchScalarGridSpec(
            num_scalar_prefetch=1, grid=(ns, nt),
            in_specs=[pl.BlockSpec((None, tr, C), lambda s, i, c_ref: (s, c_ref[0] * nt + i, 0)),
                      pl.BlockSpec((None, tr, C), lambda s, i, c_ref: (s, i, 0))],
            out_specs=pl.BlockSpec((None, tr, C), lambda s, i, c_ref: (s, i, 0))),
        out_shape=SDS((ns, hr, C), g.dtype),
        compiler_params=_cp(2),
    )(c_arr, g, r)


def _chip_exchange(parts):
    T = len(parts)

    def body(*refs):
        ins, outs = refs[:T], refs[T:2 * T]
        send_sems, recv_sems, local_sems = refs[2 * T:]
        x, y, c, me, chips, chip_idx = _mesh_pos()
        started, local = [], []

        def copy(t, j, src, dst):
            return pltpu.make_async_remote_copy(src_ref=src, dst_ref=dst, send_sem=send_sems.at[3 * t + j],
                                                recv_sem=recv_sems.at[3 * t + j],
                                                device_id=(*chips[j], c), device_id_type=MESH)

        for t in range(T):
            lc = pltpu.make_async_copy(ins[t].at[me], outs[t].at[me], local_sems.at[t])
            lc.start()
            local.append(lc)
            for j in range(3):
                cp = copy(t, j, ins[t].at[chip_idx[j]], outs[t].at[me])
                cp.start()
                started.append(cp)
        for t in range(T):
            for j in range(3):
                got = outs[t].at[chip_idx[j]]
                copy(t, j, got, got).wait_recv()
        for cp in started:
            cp.wait_send()
        for lc in local:
            lc.wait()

    return pl.pallas_call(
        body, name="rs_chip_exchange",
        in_specs=[_hbm_spec()] * T, out_specs=[_hbm_spec()] * T,
        out_shape=[SDS(p.shape, p.dtype) for p in parts],
        scratch_shapes=[pltpu.SemaphoreType.DMA((3 * T,)), pltpu.SemaphoreType.DMA((3 * T,)),
                        pltpu.SemaphoreType.DMA((T,))],
    )(*parts)


def _sum_chips(name, r):
    ns, H, C = r.shape
    tr = min(256, H)

    def body(r_ref, o_ref):
        o_ref[...] = ((r_ref[0] + r_ref[1]) + r_ref[2]) + r_ref[3]

    return pl.pallas_call(
        body, name=name, grid=(H // tr,),
        in_specs=[pl.BlockSpec((ns, tr, C), lambda i: (0, i, 0))],
        out_specs=pl.BlockSpec((tr, C), lambda i: (i, 0)),
        out_shape=SDS((H, C), r.dtype),
        compiler_params=_cp(1),
    )(r)


def _join_halves(halves, n_layers):
    T = len(halves)
    n_out = T // n_layers

    def body(*refs):
        ins, outs = refs[:T], refs[T:T + n_out]
        send_sems, recv_sems, local_sems = refs[T + n_out:]
        x, y, c, _, _, _ = _mesh_pos()
        cps, local = [], []
        for t in range(T):
            hr = ins[t].shape[0]
            o = outs[t // n_layers]
            layer = t % n_layers
            mine = o.at[layer, pl.ds(c * hr, hr), :]
            lc = pltpu.make_async_copy(ins[t], mine, local_sems.at[t])
            lc.start()
            local.append(lc)
            cp = pltpu.make_async_remote_copy(src_ref=ins[t], dst_ref=mine, send_sem=send_sems.at[t],
                                              recv_sem=recv_sems.at[t], device_id=(x, y, 1 - c), device_id_type=MESH)
            cp.start()
            cps.append(cp)
        for t in range(T):
            hr = ins[t].shape[0]
            theirs = outs[t // n_layers].at[t % n_layers, pl.ds((1 - c) * hr, hr), :]
            pltpu.make_async_remote_copy(src_ref=theirs, dst_ref=theirs, send_sem=send_sems.at[t],
                                         recv_sem=recv_sems.at[t], device_id=(x, y, 1 - c),
                                         device_id_type=MESH).wait_recv()
        for cp in cps:
            cp.wait_send()
        for lc in local:
            lc.wait()

    out_shape = []
    for o in range(n_out):
        h = halves[o * n_layers]
        out_shape.append(SDS((n_layers, 2 * h.shape[0], h.shape[1]), h.dtype))
    return pl.pallas_call(
        body, name="rs_join_halves",
        in_specs=[_hbm_spec()] * T, out_specs=[_hbm_spec()] * n_out,
        out_shape=out_shape,
        scratch_shapes=[pltpu.SemaphoreType.DMA((T,)), pltpu.SemaphoreType.DMA((T,)),
                        pltpu.SemaphoreType.DMA((T,))],
    )(*halves)


def _allreduce_small(buf):
    R, C = buf.shape

    def body(in_ref, out_ref, land_ref, send_sems, recv_sems):
        x, y, c = lax.axis_index("x"), lax.axis_index("y"), lax.axis_index("c")
        me = 4 * x + 2 * y + c
        land_ref[me] = in_ref[...]
        cps = []
        for k in range(1, N_DEV):
            kx, ky, kc = (k >> 2) & 1, (k >> 1) & 1, k & 1
            peer = (x ^ kx, y ^ ky, c ^ kc)
            cp = pltpu.make_async_remote_copy(src_ref=in_ref, dst_ref=land_ref.at[me],
                                              send_sem=send_sems.at[k - 1], recv_sem=recv_sems.at[k - 1],
                                              device_id=peer, device_id_type=MESH)
            cp.start()
            cps.append(cp)
        for k in range(1, N_DEV):
            src = me ^ k
            slot = land_ref.at[src]
            pltpu.make_async_remote_copy(src_ref=slot, dst_ref=slot, send_sem=send_sems.at[k - 1],
                                         recv_sem=recv_sems.at[k - 1], device_id=(x, y, c),
                                         device_id_type=MESH).wait_recv()
        for cp in cps:
            cp.wait_send()
        acc = land_ref[0]
        for d in range(1, N_DEV):
            acc = acc + land_ref[d]
        out_ref[...] = acc

    return pl.pallas_call(
        body, name="allreduce_small",
        in_specs=[pl.BlockSpec(memory_space=pltpu.VMEM)],
        out_specs=pl.BlockSpec(memory_space=pltpu.VMEM),
        out_shape=SDS((R, C), buf.dtype),
        scratch_shapes=[pltpu.VMEM((N_DEV, R, C), buf.dtype),
                        pltpu.SemaphoreType.DMA((N_DEV - 1,)), pltpu.SemaphoreType.DMA((N_DEV - 1,))],
        compiler_params=pltpu.CompilerParams(vmem_limit_bytes=V7X_VMEM_LIMIT),
    )(buf)


def _deinterleave(t, d):
    if d == 1:
        return t
    S, W = t.shape
    return t.reshape(S // d, d, W).transpose(1, 0, 2).reshape(S, W)


def _interleave(t, d):
    if d == 1:
        return t
    S, W = t.shape
    return t.reshape(d, S // d, W).transpose(1, 0, 2).reshape(S, W)


def _to_patterns(t, off):
    return jnp.stack([_deinterleave(t[:, off + PW * g:off + PW * (g + 1)], PATTERN_DILATION[g])
                      for g in range(N_PATTERNS)])


def _from_patterns(t3):
    return jnp.stack([_interleave(t3[g], PATTERN_DILATION[g]) for g in range(N_PATTERNS)])


def _pack_rows(vectors):
    flat = jnp.concatenate([v.reshape(-1) for v in vectors])
    n = flat.shape[0]
    padded = -(-n // 1024) * 1024
    return jnp.pad(flat, (0, padded - n)).reshape(padded // 128, 128)


def _unpack_rows(buf, shapes):
    flat = buf.reshape(-1)
    out, off = [], 0
    for s in shapes:
        n = 1
        for dim in s:
            n *= dim
        out.append(flat[off:off + n].reshape(s))
        off += n
    return out


def _layer_forward(l, x, prm, wg):
    S, D = x.shape
    dff4 = wg["w_mlp_in"].shape[-1]
    p, h = _norm_matmul(f"in_proj_{l}", x, prm["attn_norm"][l], wg["w_in"], l, F32)
    y_a = _sgu_fwd(f"sgu_fwd_{l}", p, prm["sgu_wt"][l], prm["sgu_bb"][l])
    y_b = _conv_fwd(f"conv_fwd_{l}", p, prm["conv_w"][l])
    q3, k3, v3 = _to_patterns(p, OFF_Q), _to_patterns(p, OFF_K), _to_patterns(p, OFF_V)
    o3d, lse3d = _attn_fwd(f"attn_fwd_{l}", q3, k3, v3, prm["q_gain"][l], prm["k_gain"][l], prm["bd"])
    o3, lse3 = _from_patterns(o3d), _from_patterns(lse3d)
    y_c = _mix_fwd(f"mix_fwd_{l}", o3, lse3)
    ycat = jnp.concatenate([y_a, y_b, y_c], axis=1)
    tm = min(512, S)
    rq = wg["w_out"].shape[2]
    x1 = _matmul(
        f"out_proj_{l}", ycat, wg["w_out"], (S, D), F32, grid=(S // tm, 1, N_CHIPS),
        a_spec=pl.BlockSpec((tm, rq), lambda i, j, k: (i, k)),
        b_spec=pl.BlockSpec((None, None, rq, D), lambda i, j, k: (k, l, 0, 0)),
        o_spec=pl.BlockSpec((tm, D), lambda i, j, k: (i, 0)),
        contract=(1, 0), acc_shape=(tm, D),
        extras=(x,), extra_specs=(pl.BlockSpec((tm, D), lambda i, j, k: (i, 0)),),
        epi=lambda r, res: r + res)
    a, h2 = _norm_matmul(f"mlp_in_{l}", x1, prm["mlp_norm"][l], wg["w_mlp_in"], l, BF16)
    tk = min(1024, dff4)
    kpc = dff4 // tk
    x2 = _matmul(
        f"mlp_out_{l}", a, wg["w_mlp_out"], (S, D), F32, grid=(S // tm, 1, N_CHIPS * kpc),
        a_spec=pl.BlockSpec((tm, tk), lambda i, j, k: (i, k)),
        b_spec=pl.BlockSpec((None, None, tk, D), lambda i, j, k: (k // kpc, l, k % kpc, 0)),
        o_spec=pl.BlockSpec((tm, D), lambda i, j, k: (i, 0)),
        contract=(1, 0), acc_shape=(tm, D), a_pre=_relu2_bf16,
        extras=(x1,), extra_specs=(pl.BlockSpec((tm, D), lambda i, j, k: (i, 0)),),
        epi=lambda r, res: r + res)
    saved = dict(x=x, p=p, h=h, q3=q3, k3=k3, v3=v3, o3=o3, lse3=lse3, lse3d=lse3d, ycat=ycat, x1=x1, a=a, h2=h2)
    return x2, saved


def _layer_backward(l, dx2, dx2b, sv, prm, wg):
    S, D = dx2.shape
    dff4 = wg["w_mlp_in"].shape[-1]
    dff = N_CHIPS * dff4
    tm = min(512, S)
    tk = min(1024, S)
    nks = S // tk

    da = _matmul(
        f"mlp_out_bwd_{l}", dx2b, wg["w_mlp_out"], (S, dff), BF16, grid=(S // tm, N_CHIPS, 1),
        a_spec=pl.BlockSpec((tm, D), lambda i, j, k: (i, 0)),
        b_spec=pl.BlockSpec((None, None, dff4, D), lambda i, j, k: (j, l, 0, 0)),
        o_spec=pl.BlockSpec((tm, dff4), lambda i, j, k: (i, j)),
        contract=(1, 1), acc_shape=(tm, dff4),
        extras=(sv["a"],), extra_specs=(pl.BlockSpec((tm, dff4), lambda i, j, k: (i, j)),),
        epi=lambda r, act: r * (2.0 * jnp.maximum(act.astype(F32), 0.0)))
    tmw = min(1024, dff4)
    mpc = dff4 // tmw
    g_w2 = _matmul(
        f"mlp_out_dw_{l}", sv["a"], dx2b, (N_CHIPS, dff4, D), F32, grid=(N_CHIPS * mpc, 1, nks),
        a_spec=pl.BlockSpec((tk, tmw), lambda i, j, k: (k, i)),
        b_spec=pl.BlockSpec((tk, D), lambda i, j, k: (k, 0)),
        o_spec=pl.BlockSpec((None, tmw, D), lambda i, j, k: (i // mpc, i % mpc, 0)),
        contract=(0, 0), acc_shape=(tmw, D), a_pre=_relu2_bf16)
    dh2 = _matmul(
        f"mlp_in_bwd_{l}", da, wg["w_mlp_in"], (S, D), F32, grid=(S // tm, 1, N_CHIPS),
        a_spec=pl.BlockSpec((tm, dff4), lambda i, j, k: (i, k)),
        b_spec=pl.BlockSpec((None, None, D, dff4), lambda i, j, k: (k, l, 0, 0)),
        o_spec=pl.BlockSpec((tm, D), lambda i, j, k: (i, 0)),
        contract=(1, 1), acc_shape=(tm, D))
    tmd = min(1024, D)
    g_w1 = _matmul(
        f"mlp_in_dw_{l}", sv["h2"], da, (N_CHIPS, D, dff4), F32, grid=(N_CHIPS, D // tmd, nks),
        a_spec=pl.BlockSpec((tk, tmd), lambda i, j, k: (k, j)),
        b_spec=pl.BlockSpec((tk, dff4), lambda i, j, k: (k, i)),
        o_spec=pl.BlockSpec((None, tmd, dff4), lambda i, j, k: (i, j, 0)),
        contract=(0, 0), acc_shape=(tmd, dff4))
    dx1, dx1b, g_mlp_norm = _rmsnorm_bwd(f"mlp_norm_bwd_{l}", dh2, sv["x1"], prm["mlp_norm"][l], dx2)

    rq = wg["w_out"].shape[2]
    dycat = _matmul(
        f"out_proj_bwd_{l}", dx1b, wg["w_out"], (S, N_CHIPS * rq), F32, grid=(S // tm, N_CHIPS, 1),
        a_spec=pl.BlockSpec((tm, D), lambda i, j, k: (i, 0)),
        b_spec=pl.BlockSpec((None, None, rq, D), lambda i, j, k: (j, l, 0, 0)),
        o_spec=pl.BlockSpec((tm, rq), lambda i, j, k: (i, j)),
        contract=(1, 1), acc_shape=(tm, rq))
    g_wout = _matmul(
        f"out_proj_dw_{l}", sv["ycat"], dx1b, (N_CHIPS, rq, D), F32, grid=(N_CHIPS, 1, nks),
        a_spec=pl.BlockSpec((tk, rq), lambda i, j, k: (k, i)),
        b_spec=pl.BlockSpec((tk, D), lambda i, j, k: (k, 0)),
        o_spec=pl.BlockSpec((None, rq, D), lambda i, j, k: (i, 0, 0)),
        contract=(0, 0), acc_shape=(rq, D))

    p = sv["p"]
    du, dv_a, g_sgu_w, db_lanes = _sgu_bwd(f"sgu_bwd_{l}", p, dycat, prm["sgu_wt"][l], prm["sgu_wtt"][l],
                                           prm["sgu_bb"][l])
    g_sgu_b = db_lanes[:, :A_HEADS].T
    db, dc, dxb, g_conv = _conv_bwd(f"conv_bwd_{l}", p, dycat, prm["conv_w"][l])
    do3, c3 = _mix_bwd(f"mix_bwd_{l}", sv["o3"], sv["lse3"], dycat, prm["bd"])
    do3d = jnp.stack([_deinterleave(do3[g], PATTERN_DILATION[g]) for g in range(N_PATTERNS)])
    c3d = jnp.stack([_deinterleave(c3[g], PATTERN_DILATION[g]) for g in range(N_PATTERNS)])
    dq3, dk3, dv3, dgq, dgk = _attn_bwd(f"attn_bwd_{l}", sv["q3"], sv["k3"], sv["v3"], sv["lse3d"], do3d, c3d,
                                        prm["q_gain"][l], prm["k_gain"][l], prm["bd"])
    g_q = dgq.reshape(N_PATTERNS * PW // HEAD_DIM, HEAD_DIM).sum(axis=0)
    g_k = dgk.reshape(N_PATTERNS * PW // HEAD_DIM, HEAD_DIM).sum(axis=0)
    nat = lambda t3: jnp.concatenate([_interleave(t3[g], PATTERN_DILATION[g]) for g in range(N_PATTERNS)], axis=1)
    dp = jnp.concatenate([du, dv_a, db, dc, dxb, nat(dq3), nat(dk3), nat(dv3)], axis=1)

    ns_in = wg["w_in"].shape[-1]
    dh = _matmul(
        f"in_proj_bwd_{l}", dp, wg["w_in"], (S, D), F32, grid=(S // tm, 1, N_CHIPS),
        a_spec=pl.BlockSpec((tm, ns_in), lambda i, j, k: (i, k)),
        b_spec=pl.BlockSpec((None, None, D, ns_in), lambda i, j, k: (k, l, 0, 0)),
        o_spec=pl.BlockSpec((tm, D), lambda i, j, k: (i, 0)),
        contract=(1, 1), acc_shape=(tm, D))
    g_win = _matmul(
        f"in_proj_dw_{l}", sv["h"], dp, (N_CHIPS, D, ns_in), F32, grid=(N_CHIPS, D // tmd, nks),
        a_spec=pl.BlockSpec((tk, tmd), lambda i, j, k: (k, j)),
        b_spec=pl.BlockSpec((tk, ns_in), lambda i, j, k: (k, i)),
        o_spec=pl.BlockSpec((None, tmd, ns_in), lambda i, j, k: (i, j, 0)),
        contract=(0, 0), acc_shape=(tmd, ns_in))
    dx0, dx0b, g_attn_norm = _rmsnorm_bwd(f"attn_norm_bwd_{l}", dh, sv["x"], prm["attn_norm"][l], dx1)

    big = dict(w_in=g_win, w_out=g_wout, w_mlp_in=g_w1, w_mlp_out=g_w2)
    small = dict(attn_norm=g_attn_norm.reshape(-1), sgu_w=g_sgu_w, sgu_b=g_sgu_b, conv_w=g_conv,
                 q_norm=g_q, k_norm=g_k, mlp_norm=g_mlp_norm.reshape(-1))
    return dx0, dx0b, big, small


BIG = ("w_in", "w_out", "w_mlp_in", "w_mlp_out")
SMALL_REPLICATED = ("attn_norm", "sgu_w", "sgu_b", "q_norm", "k_norm", "mlp_norm")


def _local_step(x, target, prm, wg, n_layers):
    saved = []
    h = x
    for l in range(n_layers):
        h, sv = _layer_forward(l, h, prm, wg)
        saved.append(sv)
    dy, dyb, colsq = _loss_kernel(h, target)
    loss = 0.5 * jnp.sum(colsq) / x.shape[1]
    bigs, smalls = [None] * n_layers, [None] * n_layers
    for l in reversed(range(n_layers)):
        dy, dyb, bigs[l], smalls[l] = _layer_backward(l, dy, dyb, saved[l], prm, wg)
    return loss, dy, bigs, smalls


def _prepare_params(attn_norm, sgu_w, sgu_b, conv_full, q_norm, k_norm, mlp_norm):
    n_layers = attn_norm.shape[0]
    tri = jnp.tril(sgu_w)
    idx = jnp.arange(PW)
    bd = (idx[:, None] // HEAD_DIM == idx[None, :] // HEAD_DIM).astype(BF16)
    return dict(
        attn_norm=[attn_norm[l][None, :] for l in range(n_layers)],
        mlp_norm=[mlp_norm[l][None, :] for l in range(n_layers)],
        sgu_wt=[tri[l].astype(BF16) for l in range(n_layers)],
        sgu_wtt=[tri[l].transpose(0, 2, 1).astype(BF16) for l in range(n_layers)],
        sgu_bb=[jnp.repeat(sgu_b[l].T, HEAD_DIM, axis=1) for l in range(n_layers)],
        conv_w=[conv_full[l] for l in range(n_layers)],
        q_gain=[jnp.tile(q_norm[l], PW // HEAD_DIM)[None, :] for l in range(n_layers)],
        k_gain=[jnp.tile(k_norm[l], PW // HEAD_DIM)[None, :] for l in range(n_layers)],
        bd=bd,
    )


def kernel(x, attn_norm, w_in, sgu_w, sgu_b, conv_w, q_norm, k_norm, w_out, mlp_norm, w_mlp_in, w_mlp_out, loss_target, m_attn_norm, m_w_in, m_sgu_w, m_sgu_b, m_conv_w, m_q_norm, m_k_norm, m_w_out, m_mlp_norm, m_w_mlp_in, m_w_mlp_out, v_attn_norm, v_w_in, v_sgu_w, v_sgu_b, v_conv_w, v_q_norm, v_k_norm, v_w_out, v_mlp_norm, v_w_mlp_in, v_w_mlp_out):
    n_layers = attn_norm.shape[0]
    weights = dict(attn_norm=attn_norm, w_in=w_in, sgu_w=sgu_w, sgu_b=sgu_b, conv_w=conv_w, q_norm=q_norm,
                   k_norm=k_norm, w_out=w_out, mlp_norm=mlp_norm, w_mlp_in=w_mlp_in, w_mlp_out=w_mlp_out)
    mom_m = dict(attn_norm=m_attn_norm, w_in=m_w_in, sgu_w=m_sgu_w, sgu_b=m_sgu_b, conv_w=m_conv_w,
                 q_norm=m_q_norm, k_norm=m_k_norm, w_out=m_w_out, mlp_norm=m_mlp_norm, w_mlp_in=m_w_mlp_in,
                 w_mlp_out=m_w_mlp_out)
    mom_v = dict(attn_norm=v_attn_norm, w_in=v_w_in, sgu_w=v_sgu_w, sgu_b=v_sgu_b, conv_w=v_conv_w,
                 q_norm=v_q_norm, k_norm=v_k_norm, w_out=v_w_out, mlp_norm=v_mlp_norm, w_mlp_in=v_w_mlp_in,
                 w_mlp_out=v_w_mlp_out)
    order = ("attn_norm", "w_in", "sgu_w", "sgu_b", "conv_w", "q_norm", "k_norm", "w_out", "mlp_norm",
             "w_mlp_in", "w_mlp_out")
    chip = 2 * lax.axis_index("x") + lax.axis_index("y")
    c_arr = lax.axis_index("c").astype(jnp.int32).reshape(1)

    conv_cols = conv_w.shape[-1]
    conv_pack = jnp.pad(conv_w.reshape(-1), (0, 2048 - conv_w.size)).reshape(1, 16, 128)
    gathered = _gather_weights([weights[n].astype(BF16) for n in BIG] + [conv_pack])
    wg = dict(zip(BIG, gathered[:4]))
    conv_full = gathered[4].reshape(N_CHIPS, 2048)[:, :conv_w.size].reshape(N_CHIPS, n_layers, 3, conv_cols)
    conv_full = conv_full.transpose(1, 2, 0, 3).reshape(n_layers, 3, N_CHIPS * conv_cols)
    prm = _prepare_params(attn_norm, sgu_w, sgu_b, conv_full, q_norm, k_norm, mlp_norm)

    loss_local, grad_x, bigs, smalls = _local_step(x[0], loss_target[0], prm, wg, n_layers)
    loss = lax.psum(loss_local, ("x", "y", "c"))

    flat = [bigs[l][n] for n in BIG for l in range(n_layers)]
    theirs = _swap_halves(flat)
    partial = [_add_my_half(f"rs_add_{i}", g, r, c_arr) for i, (g, r) in enumerate(zip(flat, theirs))]
    landed = _chip_exchange(partial)
    reduced = [_sum_chips(f"rs_sum_{i}", r) for i, r in enumerate(landed)]
    joined = dict(zip(BIG, _join_halves(reduced, n_layers)))

    small_names = SMALL_REPLICATED + ("conv_w",)
    small_shapes = [(n_layers,) + tuple(smalls[0][n].shape) for n in small_names]
    packed = _pack_rows([jnp.stack([smalls[l][n] for l in range(n_layers)]) for n in small_names])
    summed = _unpack_rows(_allreduce_small(packed), small_shapes)
    grads = dict(zip(small_names, summed))
    grads["conv_w"] = lax.dynamic_slice_in_dim(grads["conv_w"], chip * conv_cols, conv_cols, axis=2)
    for n in BIG:
        grads[n] = joined[n].reshape(weights[n].shape)

    delta, new_m, new_v = {}, {}, {}
    for n in BIG:
        shp = weights[n].shape
        two_d = (shp[0] * shp[1], shp[2])
        d, nm, nv = _adamw(f"adamw_{n}", weights[n].reshape(two_d), grads[n].reshape(two_d),
                           mom_m[n].reshape(two_d), mom_v[n].reshape(two_d))
        delta[n], new_m[n], new_v[n] = d.reshape(shp), nm.reshape(shp), nv.reshape(shp)
    smalls_all = SMALL_REPLICATED + ("conv_w",)
    shapes = [weights[n].shape for n in smalls_all]
    d, nm, nv = _adamw("adamw_small",
                       _pack_rows([weights[n] for n in smalls_all]), _pack_rows([grads[n] for n in smalls_all]),
                       _pack_rows([mom_m[n] for n in smalls_all]), _pack_rows([mom_v[n] for n in smalls_all]))
    for n, dd, mm, vv in zip(smalls_all, _unpack_rows(d, shapes), _unpack_rows(nm, shapes), _unpack_rows(nv, shapes)):
        delta[n], new_m[n], new_v[n] = dd, mm, vv

    return (loss, grad_x[None], *[grads[n] for n in order], *[delta[n] for n in order],
            *[new_m[n] for n in order], *[new_v[n] for n in order])
```

```python
import jax
import jax.numpy as jnp
from jax import lax
from jax.experimental import pallas as pl
from jax.experimental.pallas import tpu as pltpu

F32 = jnp.float32
BF16 = jnp.bfloat16
SDS = jax.ShapeDtypeStruct

EPS = 1e-6
HEAD_DIM = 64
A_HEADS = 8
A_WIDTH = 512
CHUNK = 128
B_WIDTH = 768
C_WIDTH = 768
N_PATTERNS = 3
PATTERN_DILATION = (1, 4, 16)
PW = 256
D_IN_PROJ = 5632
OFF_AU, OFF_AV, OFF_BB, OFF_BC, OFF_BX, OFF_Q, OFF_K, OFF_V = 0, 512, 1024, 1792, 2560, 3328, 4096, 4864
N_CHIPS = 4
N_DEV = 8
BLK = 128

ADAM_LR, ADAM_B1, ADAM_B2, ADAM_EPS, ADAM_WD, ADAM_STEP = 0.001, 0.9, 0.999, 1e-08, 0.01, 10

V7X_VMEM_LIMIT = 56 * 1024 * 1024
MESH = pl.DeviceIdType.MESH
NEG = -1e30


def _cp(n_axes):
    return pltpu.CompilerParams(dimension_semantics=("arbitrary",) * n_axes, vmem_limit_bytes=V7X_VMEM_LIMIT)


def _hbm_spec():
    return pl.BlockSpec(memory_space=pl.ANY)


def _norm_matmul(name, x, g, wg, layer, out_dtype):
    S, D = x.shape
    ns, _, _, Ns = wg.shape
    tm = min(512, S)

    def body(x_ref, g_ref, w_ref, o_ref, h_ref, hs_ref):
        @pl.when(pl.program_id(1) == 0)
        def _():
            xv = x_ref[...]
            y = xv * lax.rsqrt(jnp.mean(xv * xv, axis=-1, keepdims=True) + EPS) * g_ref[...]
            hb = y.astype(BF16)
            hs_ref[...] = hb
            h_ref[...] = hb
        o_ref[...] = jnp.dot(hs_ref[...], w_ref[...], preferred_element_type=F32).astype(o_ref.dtype)

    return pl.pallas_call(
        body, name=name, grid=(S // tm, ns),
        in_specs=[pl.BlockSpec((tm, D), lambda i, s: (i, 0)),
                  pl.BlockSpec((1, D), lambda i, s: (0, 0)),
                  pl.BlockSpec((None, None, D, Ns), lambda i, s: (s, layer, 0, 0))],
        out_specs=[pl.BlockSpec((tm, Ns), lambda i, s: (i, s)),
                   pl.BlockSpec((tm, D), lambda i, s: (i, 0))],
        out_shape=[SDS((S, ns * Ns), out_dtype), SDS((S, D), BF16)],
        scratch_shapes=[pltpu.VMEM((tm, D), BF16)],
        compiler_params=_cp(2),
    )(x, g, wg)


def _matmul(name, a, b, out_shape, out_dtype, *, grid, a_spec, b_spec, o_spec, contract, acc_shape,
            extras=(), extra_specs=(), a_pre=None, epi=None):
    nk = grid[2]
    n_ex = len(extras)
    dims = (((contract[0],), (contract[1],)), ((), ()))

    def body(a_ref, b_ref, *rest):
        ex = rest[:n_ex]
        o_ref = rest[n_ex]
        acc_ref = rest[n_ex + 1]
        k = pl.program_id(2)

        @pl.when(k == 0)
        def _():
            acc_ref[...] = jnp.zeros_like(acc_ref)

        av = a_ref[...]
        if a_pre is not None:
            av = a_pre(av)
        acc_ref[...] += lax.dot_general(av, b_ref[...], dims, preferred_element_type=F32)

        @pl.when(k == nk - 1)
        def _():
            r = acc_ref[...]
            if epi is not None:
                r = epi(r, *[e[...] for e in ex])
            o_ref[...] = r.astype(o_ref.dtype)

    return pl.pallas_call(
        body, name=name, grid=grid,
        in_specs=[a_spec, b_spec, *extra_specs],
        out_specs=o_spec,
        out_shape=SDS(out_shape, out_dtype),
        scratch_shapes=[pltpu.VMEM(acc_shape, F32)],
        compiler_params=_cp(3),
    )(a, b, *extras)


def _relu2_bf16(t):
    r = jnp.maximum(t.astype(F32), 0.0)
    return (r * r).astype(BF16)


def _loss_kernel(y, t):
    S, D = y.shape
    tm = min(256, S)

    def body(y_ref, t_ref, dy_ref, dyb_ref, l_ref):
        @pl.when(pl.program_id(0) == 0)
        def _():
            l_ref[...] = jnp.zeros_like(l_ref)
        e = y_ref[...] - t_ref[...]
        l_ref[...] += jnp.sum(e * e, axis=0, keepdims=True)
        dy = e * (1.0 / D)
        dy_ref[...] = dy
        dyb_ref[...] = dy.astype(BF16)

    row = pl.BlockSpec((tm, D), lambda i: (i, 0))
    return pl.pallas_call(
        body, name="loss_head", grid=(S // tm,),
        in_specs=[row, row],
        out_specs=[row, row, pl.BlockSpec((1, D), lambda i: (0, 0))],
        out_shape=[SDS((S, D), F32), SDS((S, D), BF16), SDS((1, D), F32)],
        compiler_params=_cp(1),
    )(y, t)


def _rmsnorm_bwd(name, dh, x, g, dres):
    S, D = x.shape
    tm = min(256, S)

    def body(dh_ref, x_ref, g_ref, dres_ref, dx_ref, dxb_ref, dg_ref):
        @pl.when(pl.program_id(0) == 0)
        def _():
            dg_ref[...] = jnp.zeros_like(dg_ref)
        xv = x_ref[...]
        dhv = dh_ref[...]
        rstd = lax.rsqrt(jnp.mean(xv * xv, axis=-1, keepdims=True) + EPS)
        xhat = xv * rstd
        dg_ref[...] += jnp.sum(dhv * xhat, axis=0, keepdims=True)
        dxn = dhv * g_ref[...]
        dx = dres_ref[...] + rstd * (dxn - xhat * jnp.mean(dxn * xhat, axis=-1, keepdims=True))
        dx_ref[...] = dx
        dxb_ref[...] = dx.astype(BF16)

    row = pl.BlockSpec((tm, D), lambda i: (i, 0))
    vec = pl.BlockSpec((1, D), lambda i: (0, 0))
    return pl.pallas_call(
        body, name=name, grid=(S // tm,),
        in_specs=[row, row, vec, row],
        out_specs=[row, row, vec],
        out_shape=[SDS((S, D), F32), SDS((S, D), BF16), SDS((1, D), F32)],
        compiler_params=_cp(1),
    )(dh, x, g, dres)


def _adamw(name, w, g, m, v):
    R, C = w.shape
    tr = 256 if R % 256 == 0 else R
    c1 = 1.0 - ADAM_B1 ** ADAM_STEP
    c2 = 1.0 - ADAM_B2 ** ADAM_STEP

    def body(w_ref, g_ref, m_ref, v_ref, d_ref, nm_ref, nv_ref):
        gv = g_ref[...]
        nm = ADAM_B1 * m_ref[...] + (1.0 - ADAM_B1) * gv
        nv = ADAM_B2 * v_ref[...] + (1.0 - ADAM_B2) * (gv * gv)
        m_hat = nm / c1
        v_hat = nv / c2
        d_ref[...] = -ADAM_LR * (m_hat / (jnp.sqrt(v_hat) + ADAM_EPS) + ADAM_WD * w_ref[...])
        nm_ref[...] = nm
        nv_ref[...] = nv

    blk = pl.BlockSpec((tr, C), lambda i: (i, 0))
    return pl.pallas_call(
        body, name=name, grid=(R // tr,),
        in_specs=[blk] * 4, out_specs=[blk] * 3,
        out_shape=[SDS((R, C), F32)] * 3,
        compiler_params=_cp(1),
    )(w, g, m, v)


def _pair_select(lane, lo, hi):
    return jnp.where(lane < HEAD_DIM, lo, hi)


def _sgu_fwd(name, p, wt, bb):
    S = p.shape[0]

    def body(u_ref, v_ref, wt_ref, bb_ref, o_ref):
        lane = lax.broadcasted_iota(jnp.int32, (CHUNK, 128), 1)
        for pp in range(A_HEADS // 2):
            cs = slice(128 * pp, 128 * (pp + 1))
            vb = v_ref[:, cs].astype(BF16)
            mixed = _pair_select(lane,
                                 jnp.dot(wt_ref[2 * pp], vb, preferred_element_type=F32),
                                 jnp.dot(wt_ref[2 * pp + 1], vb, preferred_element_type=F32)) + bb_ref[:, cs]
            o_ref[:, cs] = (u_ref[:, cs] * mixed).astype(o_ref.dtype)

    return pl.pallas_call(
        body, name=name, grid=(S // CHUNK,),
        in_specs=[pl.BlockSpec((CHUNK, A_WIDTH), lambda c: (c, OFF_AU // A_WIDTH)),
                  pl.BlockSpec((CHUNK, A_WIDTH), lambda c: (c, OFF_AV // A_WIDTH)),
                  pl.BlockSpec((A_HEADS, CHUNK, CHUNK), lambda c: (0, 0, 0)),
                  pl.BlockSpec((CHUNK, A_WIDTH), lambda c: (0, 0))],
        out_specs=pl.BlockSpec((CHUNK, A_WIDTH), lambda c: (c, 0)),
        out_shape=SDS((S, A_WIDTH), BF16),
        compiler_params=_cp(1),
    )(p, p, wt, bb)


def _sgu_bwd(name, p, dycat, wt, wtt, bb):
    S = p.shape[0]

    def body(u_ref, v_ref, dy_ref, wt_ref, wtt_ref, bb_ref, du_ref, dv_ref, dw_ref, db_ref, dbacc_ref):
        c = pl.program_id(0)

        @pl.when(c == 0)
        def _():
            dw_ref[...] = jnp.zeros_like(dw_ref)
            dbacc_ref[...] = jnp.zeros_like(dbacc_ref)

        lane = lax.broadcasted_iota(jnp.int32, (CHUNK, 128), 1)
        row = lax.broadcasted_iota(jnp.int32, (CHUNK, 128), 0)
        causal = row >= lane
        for pp in range(A_HEADS // 2):
            cs = slice(128 * pp, 128 * (pp + 1))
            v = v_ref[:, cs]
            vb = v.astype(BF16)
            u = u_ref[:, cs]
            dy = dy_ref[:, cs]
            mixed = _pair_select(lane,
                                 jnp.dot(wt_ref[2 * pp], vb, preferred_element_type=F32),
                                 jnp.dot(wt_ref[2 * pp + 1], vb, preferred_element_type=F32)) + bb_ref[:, cs]
            du_ref[:, cs] = (dy * mixed).astype(du_ref.dtype)
            dm = dy * u
            dmb = dm.astype(BF16)
            dv = _pair_select(lane,
                              jnp.dot(wtt_ref[2 * pp], dmb, preferred_element_type=F32),
                              jnp.dot(wtt_ref[2 * pp + 1], dmb, preferred_element_type=F32))
            dv_ref[:, cs] = dv.astype(dv_ref.dtype)
            dbacc_ref[:, cs] += dm
            nt = (((1,), (1,)), ((), ()))
            dm_lo = jnp.where(lane < HEAD_DIM, dm, 0.0).astype(BF16)
            dm_hi = jnp.where(lane >= HEAD_DIM, dm, 0.0).astype(BF16)
            dw_ref[2 * pp] += jnp.where(causal, lax.dot_general(dm_lo, vb, nt, preferred_element_type=F32), 0.0)
            dw_ref[2 * pp + 1] += jnp.where(causal, lax.dot_general(dm_hi, vb, nt, preferred_element_type=F32), 0.0)

        @pl.when(c == S // CHUNK - 1)
        def _():
            out = jnp.zeros((CHUNK, 128), F32)
            for pp in range(A_HEADS // 2):
                acc = dbacc_ref[:, 128 * pp:128 * (pp + 1)]
                s_lo = jnp.sum(jnp.where(lane < HEAD_DIM, acc, 0.0), axis=1, keepdims=True)
                s_hi = jnp.sum(jnp.where(lane >= HEAD_DIM, acc, 0.0), axis=1, keepdims=True)
                out = jnp.where(lane == 2 * pp, s_lo, out)
                out = jnp.where(lane == 2 * pp + 1, s_hi, out)
            db_ref[...] = out

    chunk = lambda col: pl.BlockSpec((CHUNK, A_WIDTH), lambda c: (c, col))
    wspec = pl.BlockSpec((A_HEADS, CHUNK, CHUNK), lambda c: (0, 0, 0))
    return pl.pallas_call(
        body, name=name, grid=(S // CHUNK,),
        in_specs=[chunk(OFF_AU // A_WIDTH), chunk(OFF_AV // A_WIDTH), chunk(0), wspec, wspec,
                  pl.BlockSpec((CHUNK, A_WIDTH), lambda c: (0, 0))],
        out_specs=[chunk(0), chunk(0), wspec, pl.BlockSpec((CHUNK, 128), lambda c: (0, 0))],
        out_shape=[SDS((S, A_WIDTH), BF16), SDS((S, A_WIDTH), BF16),
                   SDS((A_HEADS, CHUNK, CHUNK), F32), SDS((CHUNK, 128), F32)],
        scratch_shapes=[pltpu.VMEM((CHUNK, A_WIDTH), F32)],
        compiler_params=_cp(1),
    )(p, p, dycat, wt, wtt, bb)


CONV_HALO = 8


def _shift_down(a, halo, k):
    T = a.shape[0]
    row = lax.broadcasted_iota(jnp.int32, a.shape, 0)
    out = pltpu.roll(a, k, 0)
    for r in range(k):
        out = jnp.where(row == r, halo[CONV_HALO - k + r:CONV_HALO - k + r + 1, :], out)
    return out


def _shift_up(a, halo, k):
    T = a.shape[0]
    row = lax.broadcasted_iota(jnp.int32, a.shape, 0)
    out = pltpu.roll(a, T - k, 0)
    for r in range(k):
        out = jnp.where(row == T - k + r, halo[r:r + 1, :], out)
    return out


def _conv_specs(S, T):
    hb = T // CONV_HALO
    last = S // CONV_HALO - 1
    tile = lambda col0: pl.BlockSpec((T, 128), lambda j, i: (i, col0 + j))
    prev = lambda col0: pl.BlockSpec((CONV_HALO, 128), lambda j, i: (jnp.maximum(i * hb - 1, 0), col0 + j))
    nxt = lambda col0: pl.BlockSpec((CONV_HALO, 128), lambda j, i: (jnp.minimum((i + 1) * hb, last), col0 + j))
    return tile, prev, nxt


def _conv_fwd(name, p, w):
    S = p.shape[0]
    T = min(512, S)
    tile, prev, _ = _conv_specs(S, T)
    cb, cc, cx = OFF_BB // 128, OFF_BC // 128, OFF_BX // 128

    def body(b_ref, c_ref, x_ref, ch_ref, xh_ref, w_ref, o_ref):
        i = pl.program_id(1)
        z = c_ref[...] * x_ref[...]
        zh = jnp.where(i > 0, ch_ref[...] * xh_ref[...], 0.0)
        z1 = _shift_down(z, zh, 1)
        z2 = _shift_down(z, zh, 2)
        conv = w_ref[0:1, :] * z2 + w_ref[1:2, :] * z1 + w_ref[2:3, :] * z
        o_ref[...] = (b_ref[...] * conv).astype(o_ref.dtype)

    return pl.pallas_call(
        body, name=name, grid=(B_WIDTH // 128, S // T),
        in_specs=[tile(cb), tile(cc), tile(cx), prev(cc), prev(cx),
                  pl.BlockSpec((3, 128), lambda j, i: (0, j))],
        out_specs=tile(0),
        out_shape=SDS((S, B_WIDTH), BF16),
        compiler_params=_cp(2),
    )(p, p, p, p, p, w)


def _conv_bwd(name, p, dycat, w):
    S = p.shape[0]
    T = min(512, S)
    tile, prev, nxt = _conv_specs(S, T)
    cb, cc, cx = OFF_BB // 128, OFF_BC // 128, OFF_BX // 128
    cdy = A_WIDTH // 128
    n_i = S // T

    def body(b_ref, c_ref, x_ref, dy_ref, ch_ref, xh_ref, bn_ref, dyn_ref, w_ref,
             db_ref, dc_ref, dx_ref, dw_ref):
        i = pl.program_id(1)

        @pl.when(i == 0)
        def _():
            dw_ref[...] = jnp.zeros_like(dw_ref)

        cv = c_ref[...]
        xv = x_ref[...]
        z = cv * xv
        zh = jnp.where(i > 0, ch_ref[...] * xh_ref[...], 0.0)
        z1 = _shift_down(z, zh, 1)
        z2 = _shift_down(z, zh, 2)
        w0, w1, w2 = w_ref[0:1, :], w_ref[1:2, :], w_ref[2:3, :]
        conv = w0 * z2 + w1 * z1 + w2 * z
        dy = dy_ref[...]
        db_ref[...] = (dy * conv).astype(db_ref.dtype)
        dconv = dy * b_ref[...]
        dconv_n = jnp.where(i < n_i - 1, dyn_ref[...] * bn_ref[...], 0.0)
        dz = w2 * dconv + w1 * _shift_up(dconv, dconv_n, 1) + w0 * _shift_up(dconv, dconv_n, 2)
        dc_ref[...] = (dz * xv).astype(dc_ref.dtype)
        dx_ref[...] = (dz * cv).astype(dx_ref.dtype)
        dw_ref[0:1, :] += jnp.sum(dconv * z2, axis=0, keepdims=True)
        dw_ref[1:2, :] += jnp.sum(dconv * z1, axis=0, keepdims=True)
        dw_ref[2:3, :] += jnp.sum(dconv * z, axis=0, keepdims=True)

    wspec = pl.BlockSpec((3, 128), lambda j, i: (0, j))
    return pl.pallas_call(
        body, name=name, grid=(B_WIDTH // 128, n_i),
        in_specs=[tile(cb), tile(cc), tile(cx), tile(cdy), prev(cc), prev(cx), nxt(cb), nxt(cdy), wspec],
        out_specs=[tile(0), tile(0), tile(0), wspec],
        out_shape=[SDS((S, B_WIDTH), BF16)] * 3 + [SDS((3, B_WIDTH), F32)],
        compiler_params=_cp(2),
    )(p, p, p, dycat, p, p, p, dycat, w)


def _seg_sum(t, bd):
    hi = t.astype(BF16)
    lo = (t - hi.astype(F32)).astype(BF16)
    return jnp.dot(hi, bd, preferred_element_type=F32) + jnp.dot(lo, bd, preferred_element_type=F32)


def _head_norm(x, g, bd):
    rstd = lax.rsqrt(_seg_sum(x * x, bd) * (1.0 / HEAD_DIM) + EPS)
    xhat = x * rstd
    return xhat * g, xhat, rstd


def _head_norm_bwd(dy, g, xhat, rstd, bd):
    dxh = dy * g
    return rstd * (dxh - xhat * (_seg_sum(dxh * xhat, bd) * (1.0 / HEAD_DIM)))


def _band_mask(has_prev):
    row = lax.broadcasted_iota(jnp.int32, (BLK, 2 * BLK), 0)
    col = lax.broadcasted_iota(jnp.int32, (BLK, 2 * BLK), 1)
    first_key = jnp.where(has_prev, 0, BLK)
    return (col >= row) & (col <= row + BLK) & (col >= first_key)


def _first_of_segment(g, n, n_blocks):
    per_seg = lax.shift_right_logical(jnp.int32(n_blocks), 2 * g)
    return (n & (per_seg - 1)) == 0


def _attn_fwd(name, q3, k3, v3, gq, gk, bd):
    _, S, _ = q3.shape
    nblk = S // BLK
    nt = (((1,), (1,)), ((), ()))

    def body(q_ref, kc_ref, kp_ref, vc_ref, vp_ref, gq_ref, gk_ref, bd_ref, o_ref, lse_ref):
        g = pl.program_id(0)
        n = pl.program_id(1)
        has_prev = jnp.logical_not(_first_of_segment(g, n, nblk))
        bdv = bd_ref[...]
        qn, _, _ = _head_norm(q_ref[...], gq_ref[...], bdv)
        kn, _, _ = _head_norm(jnp.concatenate([kp_ref[...], kc_ref[...]], axis=0), gk_ref[...], bdv)
        knb = kn.astype(BF16)
        vb = jnp.concatenate([vp_ref[...], vc_ref[...]], axis=0).astype(BF16)
        band = _band_mask(has_prev)
        lane = lax.broadcasted_iota(jnp.int32, (1, PW), 1)
        o_acc = jnp.zeros((BLK, PW), F32)
        l_acc = jnp.zeros((BLK, PW), F32)
        for j in range(PW // HEAD_DIM):
            hm = (lane >= HEAD_DIM * j) & (lane < HEAD_DIM * (j + 1))
            qj = jnp.where(hm, qn, 0.0).astype(BF16)
            s = lax.dot_general(qj, knb, nt, preferred_element_type=F32) * (HEAD_DIM ** -0.5)
            s = jnp.where(band, s, NEG)
            m = jnp.max(s, axis=1, keepdims=True)
            e = jnp.exp(s - m)
            den = jnp.sum(e, axis=1, keepdims=True)
            pv = jnp.dot(e.astype(BF16), vb, preferred_element_type=F32)
            o_acc = jnp.where(hm, pv / den, o_acc)
            l_acc = jnp.where(hm, m + jnp.log(den), l_acc)
        o_ref[...] = o_acc
        lse_ref[...] = l_acc

    cur = pl.BlockSpec((None, BLK, PW), lambda g, n: (g, n, 0))
    prv = pl.BlockSpec((None, BLK, PW), lambda g, n: (g, jnp.maximum(n - 1, 0), 0))
    vec = pl.BlockSpec((1, PW), lambda g, n: (0, 0))
    return pl.pallas_call(
        body, name=name, grid=(N_PATTERNS, nblk),
        in_specs=[cur, cur, prv, cur, prv, vec, vec, pl.BlockSpec((PW, PW), lambda g, n: (0, 0))],
        out_specs=[cur, cur],
        out_shape=[SDS((N_PATTERNS, S, PW), F32)] * 2,
        compiler_params=_cp(2),
    )(q3, k3, k3, v3, v3, gq, gk, bd)


def _attn_bwd(name, q3, k3, v3, lse3, do3, c3, gq, gk, bd):
    _, S, _ = q3.shape
    nblk = S // BLK
    nt = (((1,), (1,)), ((), ()))
    tn = (((0,), (0,)), ((), ()))

    def body(q_ref, kc_ref, kp_ref, vc_ref, vp_ref, lse_ref, do_ref, c_ref, gq_ref, gk_ref, bd_ref,
             dq_ref, dk_ref, dv_ref, dgq_ref, dgk_ref, ck_ref, cv_ref):
        g = pl.program_id(0)
        n = pl.program_id(1)
        live = n < nblk
        ne = jnp.minimum(n, nblk - 1)
        has_prev = jnp.logical_not(_first_of_segment(g, ne, nblk))

        @pl.when(n == 0)
        def _():
            ck_ref[...] = jnp.zeros_like(ck_ref)
            cv_ref[...] = jnp.zeros_like(cv_ref)
            dgq_ref[...] = jnp.zeros_like(dgq_ref)
            dgk_ref[...] = jnp.zeros_like(dgk_ref)

        bdv = bd_ref[...]
        gqv = gq_ref[...]
        gkv = gk_ref[...]
        qn, qhat, qrstd = _head_norm(q_ref[...], gqv, bdv)
        kn, khat, krstd = _head_norm(jnp.concatenate([kp_ref[...], kc_ref[...]], axis=0), gkv, bdv)
        knb = kn.astype(BF16)
        vb = jnp.concatenate([vp_ref[...], vc_ref[...]], axis=0).astype(BF16)
        band = _band_mask(has_prev)
        lane = lax.broadcasted_iota(jnp.int32, (1, PW), 1)
        lse = lse_ref[...]
        do = do_ref[...]
        cc = c_ref[...]
        dqn = jnp.zeros((BLK, PW), F32)
        dkn = jnp.zeros((2 * BLK, PW), F32)
        dvv = jnp.zeros((2 * BLK, PW), F32)
        for j in range(PW // HEAD_DIM):
            hm = (lane >= HEAD_DIM * j) & (lane < HEAD_DIM * (j + 1))
            qj = jnp.where(hm, qn, 0.0).astype(BF16)
            doj = jnp.where(hm, do, 0.0).astype(BF16)
            s = lax.dot_general(qj, knb, nt, preferred_element_type=F32) * (HEAD_DIM ** -0.5)
            lse_j = jnp.max(jnp.where(hm, lse, NEG), axis=1, keepdims=True)
            c_j = jnp.max(jnp.where(hm, cc, NEG), axis=1, keepdims=True)
            prob = jnp.where(band, jnp.exp(s - lse_j), 0.0)
            dp = lax.dot_general(doj, vb, nt, preferred_element_type=F32)
            ds = (prob * (dp + c_j) * (HEAD_DIM ** -0.5)).astype(BF16)
            dqn = jnp.where(hm, jnp.dot(ds, knb, preferred_element_type=F32), dqn)
            dkn += lax.dot_general(ds, qj, tn, preferred_element_type=F32)
            dvv += lax.dot_general(prob.astype(BF16), doj, tn, preferred_element_type=F32)

        dq_ref[...] = _head_norm_bwd(dqn, gqv, qhat, qrstd, bdv).astype(dq_ref.dtype)
        dk2 = _head_norm_bwd(dkn, gkv, khat, krstd, bdv)
        keep = jnp.where(live, 1.0, 0.0)
        dgq_ref[...] += keep * jnp.sum(dqn * qhat, axis=0, keepdims=True)
        dgk_ref[...] += keep * jnp.sum(dkn * khat, axis=0, keepdims=True)
        dk_ref[...] = (ck_ref[...] + keep * dk2[:BLK]).astype(dk_ref.dtype)
        dv_ref[...] = (cv_ref[...] + keep * dvv[:BLK]).astype(dv_ref.dtype)
        ck_ref[...] = dk2[BLK:]
        cv_ref[...] = dvv[BLK:]

    last = nblk - 1
    cur = pl.BlockSpec((None, BLK, PW), lambda g, n: (g, jnp.minimum(n, last), 0))
    prv = pl.BlockSpec((None, BLK, PW), lambda g, n: (g, jnp.maximum(jnp.minimum(n, last) - 1, 0), 0))
    done = pl.BlockSpec((None, BLK, PW), lambda g, n: (g, jnp.maximum(n - 1, 0), 0))
    vec = pl.BlockSpec((1, PW), lambda g, n: (0, 0))
    gvec = pl.BlockSpec((None, 1, PW), lambda g, n: (g, 0, 0))
    return pl.pallas_call(
        body, name=name, grid=(N_PATTERNS, nblk + 1),
        in_specs=[cur, cur, prv, cur, prv, cur, cur, cur, vec, vec, pl.BlockSpec((PW, PW), lambda g, n: (0, 0))],
        out_specs=[cur, done, done, gvec, gvec],
        out_shape=[SDS((N_PATTERNS, S, PW), BF16)] * 3 + [SDS((N_PATTERNS, 1, PW), F32)] * 2,
        scratch_shapes=[pltpu.VMEM((BLK, PW), F32), pltpu.VMEM((BLK, PW), F32)],
        compiler_params=_cp(2),
    )(q3, k3, k3, v3, v3, lse3, do3, c3, gq, gk, bd)


def _mix_fwd(name, o3, lse3):
    _, S, _ = o3.shape
    tm = min(512, S)

    def body(o_ref, l_ref, y_ref):
        l = [l_ref[g] for g in range(N_PATTERNS)]
        m = jnp.maximum(jnp.maximum(l[0], l[1]), l[2])
        e = [jnp.exp(t - m) for t in l]
        inv = 1.0 / (e[0] + e[1] + e[2])
        for g in range(N_PATTERNS):
            y_ref[:, PW * g:PW * (g + 1)] = (o_ref[g] * (e[g] * inv)).astype(y_ref.dtype)

    blk3 = pl.BlockSpec((N_PATTERNS, tm, PW), lambda i: (0, i, 0))
    return pl.pallas_call(
        body, name=name, grid=(S // tm,),
        in_specs=[blk3, blk3],
        out_specs=pl.BlockSpec((tm, C_WIDTH), lambda i: (i, 0)),
        out_shape=SDS((S, C_WIDTH), BF16),
        compiler_params=_cp(1),
    )(o3, lse3)


def _mix_bwd(name, o3, lse3, dycat, bd):
    _, S, _ = o3.shape
    tm = min(512, S)
    c0 = (A_WIDTH + B_WIDTH) // PW

    def body(o_ref, l_ref, dy0_ref, dy1_ref, dy2_ref, bd_ref, do_ref, c_ref):
        bdv = bd_ref[...]
        dys = [dy0_ref[...], dy1_ref[...], dy2_ref[...]]
        l = [l_ref[g] for g in range(N_PATTERNS)]
        m = jnp.maximum(jnp.maximum(l[0], l[1]), l[2])
        e = [jnp.exp(t - m) for t in l]
        inv = 1.0 / (e[0] + e[1] + e[2])
        alpha = [t * inv for t in e]
        da = [_seg_sum(dys[g] * o_ref[g], bdv) for g in range(N_PATTERNS)]
        mean_da = alpha[0] * da[0] + alpha[1] * da[1] + alpha[2] * da[2]
        for g in range(N_PATTERNS):
            do_ref[g] = dys[g] * alpha[g]
            c_ref[g] = -alpha[g] * mean_da

    blk3 = pl.BlockSpec((N_PATTERNS, tm, PW), lambda i: (0, i, 0))
    dyspec = lambda g: pl.BlockSpec((tm, PW), lambda i: (i, c0 + g))
    return pl.pallas_call(
        body, name=name, grid=(S // tm,),
        in_specs=[blk3, blk3, dyspec(0), dyspec(1), dyspec(2), pl.BlockSpec((PW, PW), lambda i: (0, 0))],
        out_specs=[blk3, blk3],
        out_shape=[SDS((N_PATTERNS, S, PW), F32)] * 2,
        compiler_params=_cp(1),
    )(o3, lse3, dycat, dycat, dycat, bd)


def _mesh_pos():
    x, y, c = lax.axis_index("x"), lax.axis_index("y"), lax.axis_index("c")
    chips = [(1 - x, y), (x, 1 - y), (1 - x, 1 - y)]
    chip_idx = [2 * cx + cy for cx, cy in chips]
    return x, y, c, 2 * x + y, chips, chip_idx


def _half(ref_or_n, c):
    return pl.ds(c * ref_or_n, ref_or_n)


def _place_shard(name, w, chip_arr, out_dtype):
    L, R, C = w.shape
    tr = min(256, R)

    def body(chip_ref, w_ref, o_ref):
        o_ref[...] = w_ref[...].astype(o_ref.dtype)

    return pl.pallas_call(
        body, name=name,
        grid_spec=pltpu.PrefetchScalarGridSpec(
            num_scalar_prefetch=1, grid=(L, R // tr),
            in_specs=[pl.BlockSpec((None, tr, C), lambda l, i, chip_ref: (l, i, 0))],
            out_specs=pl.BlockSpec((None, None, tr, C), lambda l, i, chip_ref: (chip_ref[0], l, i, 0))),
        out_shape=SDS((N_CHIPS, L, R, C), out_dtype),
        compiler_params=_cp(2),
    )(chip_arr, w)


def _gather_weights(bufs):
    T = len(bufs)

    def body(*refs):
        outs = refs[T:2 * T]
        send_sems, recv_sems = refs[2 * T:]
        x, y, c, me, chips, chip_idx = _mesh_pos()
        sibling = (x, y, 1 - c)

        def slab(t, chip, half):
            hr = outs[t].shape[2] // 2
            return outs[t].at[chip, :, pl.ds(half * hr, hr), :]

        def copy(t, k, ref, to):
            return pltpu.make_async_remote_copy(src_ref=ref, dst_ref=ref, send_sem=send_sems.at[6 * t + k],
                                                recv_sem=recv_sems.at[6 * t + k], device_id=to, device_id_type=MESH)

        started = []
        for t in range(T):
            for j in range(3):
                cp = copy(t, j, slab(t, me, c), (*chips[j], c))
                cp.start()
                started.append(cp)
        for t in range(T):
            for j in range(3):
                got = slab(t, chip_idx[j], c)
                copy(t, j, got, (*chips[j], c)).wait_recv()
                fw = copy(t, 3 + j, got, sibling)
                fw.start()
                started.append(fw)
        for t in range(T):
            for j in range(3):
                copy(t, 3 + j, slab(t, chip_idx[j], 1 - c), sibling).wait_recv()
        for cp in started:
            cp.wait_send()

    return pl.pallas_call(
        body, name="gather_weights",
        in_specs=[_hbm_spec()] * T, out_specs=[_hbm_spec()] * T,
        out_shape=[SDS(b.shape, b.dtype) for b in bufs],
        input_output_aliases={t: t for t in range(T)},
        scratch_shapes=[pltpu.SemaphoreType.DMA((6 * T,)), pltpu.SemaphoreType.DMA((6 * T,))],
    )(*bufs)


def _swap_halves(grads):
    T = len(grads)

    def body(*refs):
        ins, outs = refs[:T], refs[T:2 * T]
        send_sems, recv_sems = refs[2 * T:]
        x, y, c, _, _, _ = _mesh_pos()
        cps = []
        for t in range(T):
            hr = ins[t].shape[1] // 2
            cp = pltpu.make_async_remote_copy(
                src_ref=ins[t].at[:, pl.ds((1 - c) * hr, hr), :], dst_ref=outs[t],
                send_sem=send_sems.at[t], recv_sem=recv_sems.at[t],
                device_id=(x, y, 1 - c), device_id_type=MESH)
            cp.start()
            cps.append(cp)
        for cp in cps:
            cp.wait()

    return pl.pallas_call(
        body, name="rs_swap_halves",
        in_specs=[_hbm_spec()] * T, out_specs=[_hbm_spec()] * T,
        out_shape=[SDS((g.shape[0], g.shape[1] // 2, g.shape[2]), g.dtype) for g in grads],
        scratch_shapes=[pltpu.SemaphoreType.DMA((T,)), pltpu.SemaphoreType.DMA((T,))],
    )(*grads)


def _add_my_half(name, g, r, c_arr):
    ns, R, C = g.shape
    hr = R // 2
    tr = min(256, hr)
    nt = hr // tr

    def body(c_ref, g_ref, r_ref, o_ref, land_ref):
        t = (g_ref[...] + r_ref[...]).astype(o_ref.dtype)
        o_ref[...] = t
        land_ref[...] = t

    out = pl.BlockSpec((None, tr, C), lambda s, i, c_ref: (s, i, 0))
    return pl.pallas_call(
        body, name=name,
        grid_spec=pltpu.PrefetchScalarGridSpec(
            num_scalar_prefetch=1, grid=(ns, nt),
            in_specs=[pl.BlockSpec((None, tr, C), lambda s, i, c_ref: (s, c_ref[0] * nt + i, 0)), out],
            out_specs=[out, out]),
        out_shape=[SDS((ns, hr, C), BF16)] * 2,
        compiler_params=_cp(2),
    )(c_arr, g, r)


def _chip_exchange(parts, lands):
    T = len(parts)

    def body(*refs):
        ins, outs = refs[:T], refs[2 * T:3 * T]
        send_sems, recv_sems = refs[3 * T:]
        x, y, c, me, chips, chip_idx = _mesh_pos()
        started = []

        def copy(t, j, src, dst):
            return pltpu.make_async_remote_copy(src_ref=src, dst_ref=dst, send_sem=send_sems.at[3 * t + j],
                                                recv_sem=recv_sems.at[3 * t + j],
                                                device_id=(*chips[j], c), device_id_type=MESH)

        for t in range(T):
            for j in range(3):
                cp = copy(t, j, ins[t].at[chip_idx[j]], outs[t].at[me])
                cp.start()
                started.append(cp)
        for t in range(T):
            for j in range(3):
                got = outs[t].at[chip_idx[j]]
                copy(t, j, got, got).wait_recv()
        for cp in started:
            cp.wait_send()

    return pl.pallas_call(
        body, name="rs_chip_exchange",
        in_specs=[_hbm_spec()] * (2 * T), out_specs=[_hbm_spec()] * T,
        out_shape=[SDS(p.shape, p.dtype) for p in lands],
        input_output_aliases={T + t: t for t in range(T)},
        scratch_shapes=[pltpu.SemaphoreType.DMA((3 * T,)), pltpu.SemaphoreType.DMA((3 * T,))],
    )(*parts, *lands)


def _sum_chips(name, r, c_arr, layer, n_layers, prev):
    ns, H, C = r.shape
    tr = min(256, H)
    nt = H // tr

    def body(c_ref, r_ref, *rest):
        o_ref = rest[-1]
        o_ref[...] = ((r_ref[0].astype(F32) + r_ref[1].astype(F32)) + r_ref[2].astype(F32)) + r_ref[3].astype(F32)

    in_specs = [pl.BlockSpec((ns, tr, C), lambda i, c_ref: (0, i, 0))]
    args = [c_arr, r]
    aliases = {}
    if prev is not None:
        in_specs.append(_hbm_spec())
        args.append(prev)
        aliases = {2: 0}
    return pl.pallas_call(
        body, name=name,
        grid_spec=pltpu.PrefetchScalarGridSpec(
            num_scalar_prefetch=1, grid=(nt,), in_specs=in_specs,
            out_specs=pl.BlockSpec((None, tr, C), lambda i, c_ref: (layer, c_ref[0] * nt + i, 0))),
        out_shape=SDS((n_layers, 2 * H, C), F32),
        input_output_aliases=aliases,
        compiler_params=_cp(1),
    )(*args)


def _join_halves(bufs):
    T = len(bufs)

    def body(*refs):
        outs = refs[T:2 * T]
        send_sems, recv_sems = refs[2 * T:]
        x, y, c, _, _, _ = _mesh_pos()
        cps = []
        for t in range(T):
            hr = outs[t].shape[1] // 2
            mine = outs[t].at[:, pl.ds(c * hr, hr), :]
            cp = pltpu.make_async_remote_copy(src_ref=mine, dst_ref=mine, send_sem=send_sems.at[t],
                                              recv_sem=recv_sems.at[t], device_id=(x, y, 1 - c), device_id_type=MESH)
            cp.start()
            cps.append(cp)
        for t in range(T):
            hr = outs[t].shape[1] // 2
            theirs = outs[t].at[:, pl.ds((1 - c) * hr, hr), :]
            pltpu.make_async_remote_copy(src_ref=theirs, dst_ref=theirs, send_sem=send_sems.at[t],
                                         recv_sem=recv_sems.at[t], device_id=(x, y, 1 - c),
                                         device_id_type=MESH).wait_recv()
        for cp in cps:
            cp.wait_send()

    return pl.pallas_call(
        body, name="rs_join_halves",
        in_specs=[_hbm_spec()] * T, out_specs=[_hbm_spec()] * T,
        out_shape=[SDS(b.shape, b.dtype) for b in bufs],
        input_output_aliases={t: t for t in range(T)},
        scratch_shapes=[pltpu.SemaphoreType.DMA((T,)), pltpu.SemaphoreType.DMA((T,))],
    )(*bufs)


def _allreduce_small(buf):
    R, C = buf.shape

    def body(in_ref, out_ref, land_ref, send_sems, recv_sems):
        x, y, c = lax.axis_index("x"), lax.axis_index("y"), lax.axis_index("c")
        me = 4 * x + 2 * y + c
        land_ref[me] = in_ref[...]
        cps = []
        for k in range(1, N_DEV):
            kx, ky, kc = (k >> 2) & 1, (k >> 1) & 1, k & 1
            peer = (x ^ kx, y ^ ky, c ^ kc)
            cp = pltpu.make_async_remote_copy(src_ref=in_ref, dst_ref=land_ref.at[me],
                                              send_sem=send_sems.at[k - 1], recv_sem=recv_sems.at[k - 1],
                                              device_id=peer, device_id_type=MESH)
            cp.start()
            cps.append(cp)
        for k in range(1, N_DEV):
            src = me ^ k
            slot = land_ref.at[src]
            pltpu.make_async_remote_copy(src_ref=slot, dst_ref=slot, send_sem=send_sems.at[k - 1],
                                         recv_sem=recv_sems.at[k - 1], device_id=(x, y, c),
                                         device_id_type=MESH).wait_recv()
        for cp in cps:
            cp.wait_send()
        acc = land_ref[0]
        for d in range(1, N_DEV):
            acc = acc + land_ref[d]
        out_ref[...] = acc

    return pl.pallas_call(
        body, name="allreduce_small",
        in_specs=[pl.BlockSpec(memory_space=pltpu.VMEM)],
        out_specs=pl.BlockSpec(memory_space=pltpu.VMEM),
        out_shape=SDS((R, C), buf.dtype),
        scratch_shapes=[pltpu.VMEM((N_DEV, R, C), buf.dtype),
                        pltpu.SemaphoreType.DMA((N_DEV - 1,)), pltpu.SemaphoreType.DMA((N_DEV - 1,))],
        compiler_params=pltpu.CompilerParams(vmem_limit_bytes=V7X_VMEM_LIMIT),
    )(buf)


def _deinterleave(t, d):
    if d == 1:
        return t
    S, W = t.shape
    return t.reshape(S // d, d, W).transpose(1, 0, 2).reshape(S, W)


def _interleave(t, d):
    if d == 1:
        return t
    S, W = t.shape
    return t.reshape(d, S // d, W).transpose(1, 0, 2).reshape(S, W)


def _to_patterns(t, off):
    return jnp.stack([_deinterleave(t[:, off + PW * g:off + PW * (g + 1)], PATTERN_DILATION[g])
                      for g in range(N_PATTERNS)])


def _from_patterns(t3):
    return jnp.stack([_interleave(t3[g], PATTERN_DILATION[g]) for g in range(N_PATTERNS)])


def _pack_rows(vectors):
    flat = jnp.concatenate([v.reshape(-1) for v in vectors])
    n = flat.shape[0]
    padded = -(-n // 1024) * 1024
    return jnp.pad(flat, (0, padded - n)).reshape(padded // 128, 128)


def _unpack_rows(buf, shapes):
    flat = buf.reshape(-1)
    out, off = [], 0
    for s in shapes:
        n = 1
        for dim in s:
            n *= dim
        out.append(flat[off:off + n].reshape(s))
        off += n
    return out


def _layer_forward(l, x, prm, wg):
    S, D = x.shape
    dff4 = wg["w_mlp_in"].shape[-1]
    p, h = _norm_matmul(f"in_proj_{l}", x, prm["attn_norm"][l], wg["w_in"], l, F32)
    y_a = _sgu_fwd(f"sgu_fwd_{l}", p, prm["sgu_wt"][l], prm["sgu_bb"][l])
    y_b = _conv_fwd(f"conv_fwd_{l}", p, prm["conv_w"][l])
    q3, k3, v3 = _to_patterns(p, OFF_Q), _to_patterns(p, OFF_K), _to_patterns(p, OFF_V)
    o3d, lse3d = _attn_fwd(f"attn_fwd_{l}", q3, k3, v3, prm["q_gain"][l], prm["k_gain"][l], prm["bd"])
    o3, lse3 = _from_patterns(o3d), _from_patterns(lse3d)
    y_c = _mix_fwd(f"mix_fwd_{l}", o3, lse3)
    ycat = jnp.concatenate([y_a, y_b, y_c], axis=1)
    tm = min(512, S)
    rq = wg["w_out"].shape[2]
    x1 = _matmul(
        f"out_proj_{l}", ycat, wg["w_out"], (S, D), F32, grid=(S // tm, 1, N_CHIPS),
        a_spec=pl.BlockSpec((tm, rq), lambda i, j, k: (i, k)),
        b_spec=pl.BlockSpec((None, None, rq, D), lambda i, j, k: (k, l, 0, 0)),
        o_spec=pl.BlockSpec((tm, D), lambda i, j, k: (i, 0)),
        contract=(1, 0), acc_shape=(tm, D),
        extras=(x,), extra_specs=(pl.BlockSpec((tm, D), lambda i, j, k: (i, 0)),),
        epi=lambda r, res: r + res)
    a, h2 = _norm_matmul(f"mlp_in_{l}", x1, prm["mlp_norm"][l], wg["w_mlp_in"], l, BF16)
    tk = min(1024, dff4)
    kpc = dff4 // tk
    x2 = _matmul(
        f"mlp_out_{l}", a, wg["w_mlp_out"], (S, D), F32, grid=(S // tm, 1, N_CHIPS * kpc),
        a_spec=pl.BlockSpec((tm, tk), lambda i, j, k: (i, k)),
        b_spec=pl.BlockSpec((None, None, tk, D), lambda i, j, k: (k // kpc, l, k % kpc, 0)),
        o_spec=pl.BlockSpec((tm, D), lambda i, j, k: (i, 0)),
        contract=(1, 0), acc_shape=(tm, D), a_pre=_relu2_bf16,
        extras=(x1,), extra_specs=(pl.BlockSpec((tm, D), lambda i, j, k: (i, 0)),),
        epi=lambda r, res: r + res)
    saved = dict(x=x, p=p, h=h, q3=q3, k3=k3, v3=v3, o3=o3, lse3=lse3, lse3d=lse3d, ycat=ycat, x1=x1, a=a, h2=h2)
    return x2, saved


def _layer_backward(l, dx2, dx2b, sv, prm, wg):
    S, D = dx2.shape
    dff4 = wg["w_mlp_in"].shape[-1]
    dff = N_CHIPS * dff4
    tm = min(512, S)
    tk = min(1024, S)
    nks = S // tk

    da = _matmul(
        f"mlp_out_bwd_{l}", dx2b, wg["w_mlp_out"], (S, dff), BF16, grid=(S // tm, N_CHIPS, 1),
        a_spec=pl.BlockSpec((tm, D), lambda i, j, k: (i, 0)),
        b_spec=pl.BlockSpec((None, None, dff4, D), lambda i, j, k: (j, l, 0, 0)),
        o_spec=pl.BlockSpec((tm, dff4), lambda i, j, k: (i, j)),
        contract=(1, 1), acc_shape=(tm, dff4),
        extras=(sv["a"],), extra_specs=(pl.BlockSpec((tm, dff4), lambda i, j, k: (i, j)),),
        epi=lambda r, act: r * (2.0 * jnp.maximum(act.astype(F32), 0.0)))
    tmw = min(1024, dff4)
    mpc = dff4 // tmw
    g_w2 = _matmul(
        f"mlp_out_dw_{l}", sv["a"], dx2b, (N_CHIPS, dff4, D), F32, grid=(N_CHIPS * mpc, 1, nks),
        a_spec=pl.BlockSpec((tk, tmw), lambda i, j, k: (k, i)),
        b_spec=pl.BlockSpec((tk, D), lambda i, j, k: (k, 0)),
        o_spec=pl.BlockSpec((None, tmw, D), lambda i, j, k: (i // mpc, i % mpc, 0)),
        contract=(0, 0), acc_shape=(tmw, D), a_pre=_relu2_bf16)
    dh2 = _matmul(
        f"mlp_in_bwd_{l}", da, wg["w_mlp_in"], (S, D), F32, grid=(S // tm, 1, N_CHIPS),
        a_spec=pl.BlockSpec((tm, dff4), lambda i, j, k: (i, k)),
        b_spec=pl.BlockSpec((None, None, D, dff4), lambda i, j, k: (k, l, 0, 0)),
        o_spec=pl.BlockSpec((tm, D), lambda i, j, k: (i, 0)),
        contract=(1, 1), acc_shape=(tm, D))
    tmd = min(1024, D)
    g_w1 = _matmul(
        f"mlp_in_dw_{l}", sv["h2"], da, (N_CHIPS, D, dff4), F32, grid=(N_CHIPS, D // tmd, nks),
        a_spec=pl.BlockSpec((tk, tmd), lambda i, j, k: (k, j)),
        b_spec=pl.BlockSpec((tk, dff4), lambda i, j, k: (k, i)),
        o_spec=pl.BlockSpec((None, tmd, dff4), lambda i, j, k: (i, j, 0)),
        contract=(0, 0), acc_shape=(tmd, dff4))
    dx1, dx1b, g_mlp_norm = _rmsnorm_bwd(f"mlp_norm_bwd_{l}", dh2, sv["x1"], prm["mlp_norm"][l], dx2)

    rq = wg["w_out"].shape[2]
    dycat = _matmul(
        f"out_proj_bwd_{l}", dx1b, wg["w_out"], (S, N_CHIPS * rq), F32, grid=(S // tm, N_CHIPS, 1),
        a_spec=pl.BlockSpec((tm, D), lambda i, j, k: (i, 0)),
        b_spec=pl.BlockSpec((None, None, rq, D), lambda i, j, k: (j, l, 0, 0)),
        o_spec=pl.BlockSpec((tm, rq), lambda i, j, k: (i, j)),
        contract=(1, 1), acc_shape=(tm, rq))
    g_wout = _matmul(
        f"out_proj_dw_{l}", sv["ycat"], dx1b, (N_CHIPS, rq, D), F32, grid=(N_CHIPS, 1, nks),
        a_spec=pl.BlockSpec((tk, rq), lambda i, j, k: (k, i)),
        b_spec=pl.BlockSpec((tk, D), lambda i, j, k: (k, 0)),
        o_spec=pl.BlockSpec((None, rq, D), lambda i, j, k: (i, 0, 0)),
        contract=(0, 0), acc_shape=(rq, D))

    p = sv["p"]
    du, dv_a, g_sgu_w, db_lanes = _sgu_bwd(f"sgu_bwd_{l}", p, dycat, prm["sgu_wt"][l], prm["sgu_wtt"][l],
                                           prm["sgu_bb"][l])
    g_sgu_b = db_lanes[:, :A_HEADS].T
    db, dc, dxb, g_conv = _conv_bwd(f"conv_bwd_{l}", p, dycat, prm["conv_w"][l])
    do3, c3 = _mix_bwd(f"mix_bwd_{l}", sv["o3"], sv["lse3"], dycat, prm["bd"])
    do3d = jnp.stack([_deinterleave(do3[g], PATTERN_DILATION[g]) for g in range(N_PATTERNS)])
    c3d = jnp.stack([_deinterleave(c3[g], PATTERN_DILATION[g]) for g in range(N_PATTERNS)])
    dq3, dk3, dv3, dgq, dgk = _attn_bwd(f"attn_bwd_{l}", sv["q3"], sv["k3"], sv["v3"], sv["lse3d"], do3d, c3d,
                                        prm["q_gain"][l], prm["k_gain"][l], prm["bd"])
    g_q = dgq.reshape(N_PATTERNS * PW // HEAD_DIM, HEAD_DIM).sum(axis=0)
    g_k = dgk.reshape(N_PATTERNS * PW // HEAD_DIM, HEAD_DIM).sum(axis=0)
    nat = lambda t3: jnp.concatenate([_interleave(t3[g], PATTERN_DILATION[g]) for g in range(N_PATTERNS)], axis=1)
    dp = jnp.concatenate([du, dv_a, db, dc, dxb, nat(dq3), nat(dk3), nat(dv3)], axis=1)

    ns_in = wg["w_in"].shape[-1]
    dh = _matmul(
        f"in_proj_bwd_{l}", dp, wg["w_in"], (S, D), F32, grid=(S // tm, 1, N_CHIPS),
        a_spec=pl.BlockSpec((tm, ns_in), lambda i, j, k: (i, k)),
        b_spec=pl.BlockSpec((None, None, D, ns_in), lambda i, j, k: (k, l, 0, 0)),
        o_spec=pl.BlockSpec((tm, D), lambda i, j, k: (i, 0)),
        contract=(1, 1), acc_shape=(tm, D))
    g_win = _matmul(
        f"in_proj_dw_{l}", sv["h"], dp, (N_CHIPS, D, ns_in), F32, grid=(N_CHIPS, D // tmd, nks),
        a_spec=pl.BlockSpec((tk, tmd), lambda i, j, k: (k, j)),
        b_spec=pl.BlockSpec((tk, ns_in), lambda i, j, k: (k, i)),
        o_spec=pl.BlockSpec((None, tmd, ns_in), lambda i, j, k: (i, j, 0)),
        contract=(0, 0), acc_shape=(tmd, ns_in))
    dx0, dx0b, g_attn_norm = _rmsnorm_bwd(f"attn_norm_bwd_{l}", dh, sv["x"], prm["attn_norm"][l], dx1)

    big = dict(w_in=g_win, w_out=g_wout, w_mlp_in=g_w1, w_mlp_out=g_w2)
    small = dict(attn_norm=g_attn_norm.reshape(-1), sgu_w=g_sgu_w, sgu_b=g_sgu_b, conv_w=g_conv,
                 q_norm=g_q, k_norm=g_k, mlp_norm=g_mlp_norm.reshape(-1))
    return dx0, dx0b, big, small


BIG = ("w_in", "w_out", "w_mlp_in", "w_mlp_out")
SMALL_REPLICATED = ("attn_norm", "sgu_w", "sgu_b", "q_norm", "k_norm", "mlp_norm")


def _local_step(x, target, prm, wg, n_layers):
    saved = []
    h = x
    for l in range(n_layers):
        h, sv = _layer_forward(l, h, prm, wg)
        saved.append(sv)
    dy, dyb, colsq = _loss_kernel(h, target)
    loss = 0.5 * jnp.sum(colsq) / x.shape[1]
    bigs, smalls = [None] * n_layers, [None] * n_layers
    for l in reversed(range(n_layers)):
        dy, dyb, bigs[l], smalls[l] = _layer_backward(l, dy, dyb, saved[l], prm, wg)
    return loss, dy, bigs, smalls


def _prepare_params(attn_norm, sgu_w, sgu_b, conv_full, q_norm, k_norm, mlp_norm):
    n_layers = attn_norm.shape[0]
    tri = jnp.tril(sgu_w)
    idx = jnp.arange(PW)
    bd = (idx[:, None] // HEAD_DIM == idx[None, :] // HEAD_DIM).astype(BF16)
    return dict(
        attn_norm=[attn_norm[l][None, :] for l in range(n_layers)],
        mlp_norm=[mlp_norm[l][None, :] for l in range(n_layers)],
        sgu_wt=[tri[l].astype(BF16) for l in range(n_layers)],
        sgu_wtt=[tri[l].transpose(0, 2, 1).astype(BF16) for l in range(n_layers)],
        sgu_bb=[jnp.repeat(sgu_b[l].T, HEAD_DIM, axis=1) for l in range(n_layers)],
        conv_w=[conv_full[l] for l in range(n_layers)],
        q_gain=[jnp.tile(q_norm[l], PW // HEAD_DIM)[None, :] for l in range(n_layers)],
        k_gain=[jnp.tile(k_norm[l], PW // HEAD_DIM)[None, :] for l in range(n_layers)],
        bd=bd,
    )


def kernel(x, attn_norm, w_in, sgu_w, sgu_b, conv_w, q_norm, k_norm, w_out, mlp_norm, w_mlp_in, w_mlp_out, loss_target, m_attn_norm, m_w_in, m_sgu_w, m_sgu_b, m_conv_w, m_q_norm, m_k_norm, m_w_out, m_mlp_norm, m_w_mlp_in, m_w_mlp_out, v_attn_norm, v_w_in, v_sgu_w, v_sgu_b, v_conv_w, v_q_norm, v_k_norm, v_w_out, v_mlp_norm, v_w_mlp_in, v_w_mlp_out):
    n_layers = attn_norm.shape[0]
    weights = dict(attn_norm=attn_norm, w_in=w_in, sgu_w=sgu_w, sgu_b=sgu_b, conv_w=conv_w, q_norm=q_norm,
                   k_norm=k_norm, w_out=w_out, mlp_norm=mlp_norm, w_mlp_in=w_mlp_in, w_mlp_out=w_mlp_out)
    mom_m = dict(attn_norm=m_attn_norm, w_in=m_w_in, sgu_w=m_sgu_w, sgu_b=m_sgu_b, conv_w=m_conv_w,
                 q_norm=m_q_norm, k_norm=m_k_norm, w_out=m_w_out, mlp_norm=m_mlp_norm, w_mlp_in=m_w_mlp_in,
                 w_mlp_out=m_w_mlp_out)
    mom_v = dict(attn_norm=v_attn_norm, w_in=v_w_in, sgu_w=v_sgu_w, sgu_b=v_sgu_b, conv_w=v_conv_w,
                 q_norm=v_q_norm, k_norm=v_k_norm, w_out=v_w_out, mlp_norm=v_mlp_norm, w_mlp_in=v_w_mlp_in,
                 w_mlp_out=v_w_mlp_out)
    order = ("attn_norm", "w_in", "sgu_w", "sgu_b", "conv_w", "q_norm", "k_norm", "w_out", "mlp_norm",
             "w_mlp_in", "w_mlp_out")
    chip = 2 * lax.axis_index("x") + lax.axis_index("y")
    c_arr = lax.axis_index("c").astype(jnp.int32).reshape(1)

    conv_cols = conv_w.shape[-1]
    chip_arr = chip.astype(jnp.int32).reshape(1)
    conv_pack = jnp.pad(conv_w.reshape(-1), (0, 2048 - conv_w.size)).reshape(1, 16, 128)
    placed = [_place_shard(f"place_{n}", weights[n], chip_arr, BF16) for n in BIG]
    placed.append(_place_shard("place_conv_w", conv_pack, chip_arr, F32))
    gathered = _gather_weights(placed)
    wg = dict(zip(BIG, gathered[:4]))
    conv_full = gathered[4].reshape(N_CHIPS, 2048)[:, :conv_w.size].reshape(N_CHIPS, n_layers, 3, conv_cols)
    conv_full = conv_full.transpose(1, 2, 0, 3).reshape(n_layers, 3, N_CHIPS * conv_cols)
    prm = _prepare_params(attn_norm, sgu_w, sgu_b, conv_full, q_norm, k_norm, mlp_norm)

    loss_local, grad_x, bigs, smalls = _local_step(x[0], loss_target[0], prm, wg, n_layers)
    loss = lax.psum(loss_local, ("x", "y", "c"))

    flat = [bigs[l][n] for n in BIG for l in range(n_layers)]
    theirs = _swap_halves(flat)
    added = [_add_my_half(f"rs_add_{i}", g, r, c_arr) for i, (g, r) in enumerate(zip(flat, theirs))]
    landed = _chip_exchange([a[0] for a in added], [a[1] for a in added])
    reduced = []
    for ti in range(len(BIG)):
        buf = None
        for l in range(n_layers):
            i = ti * n_layers + l
            buf = _sum_chips(f"rs_sum_{i}", landed[i], c_arr, l, n_layers, buf)
        reduced.append(buf)
    joined = dict(zip(BIG, _join_halves(reduced)))

    small_names = SMALL_REPLICATED + ("conv_w",)
    small_shapes = [(n_layers,) + tuple(smalls[0][n].shape) for n in small_names]
    packed = _pack_rows([jnp.stack([smalls[l][n] for l in range(n_layers)]) for n in small_names])
    summed = _unpack_rows(_allreduce_small(packed), small_shapes)
    grads = dict(zip(small_names, summed))
    grads["conv_w"] = lax.dynamic_slice_in_dim(grads["conv_w"], chip * conv_cols, conv_cols, axis=2)
    for n in BIG:
        grads[n] = joined[n].reshape(weights[n].shape)

    delta, new_m, new_v = {}, {}, {}
    for n in BIG:
        shp = weights[n].shape
        two_d = (shp[0] * shp[1], shp[2])
        d, nm, nv = _adamw(f"adamw_{n}", weights[n].reshape(two_d), grads[n].reshape(two_d),
                           mom_m[n].reshape(two_d), mom_v[n].reshape(two_d))
        delta[n], new_m[n], new_v[n] = d.reshape(shp), nm.reshape(shp), nv.reshape(shp)
    smalls_all = SMALL_REPLICATED + ("conv_w",)
    shapes = [weights[n].shape for n in smalls_all]
    d, nm, nv = _adamw("adamw_small",
                       _pack_rows([weights[n] for n in smalls_all]), _pack_rows([grads[n] for n in smalls_all]),
                       _pack_rows([mom_m[n] for n in smalls_all]), _pack_rows([mom_v[n] for n in smalls_all]))
    for n, dd, mm, vv in zip(smalls_all, _unpack_rows(d, shapes), _unpack_rows(nm, shapes), _unpack_rows(nv, shapes)):
        delta[n], new_m[n], new_v[n] = dd, mm, vv

    return (loss, grad_x[None], *[grads[n] for n in order], *[delta[n] for n in order],
            *[new_m[n] for n in order], *[new_v[n] for n in order])
```

```python
import jax
import jax.numpy as jnp
from jax import lax
from jax.experimental import pallas as pl
from jax.experimental.pallas import tpu as pltpu

F32 = jnp.float32
BF16 = jnp.bfloat16
SDS = jax.ShapeDtypeStruct

EPS = 1e-6
HEAD_DIM = 64
A_HEADS = 8
A_WIDTH = 512
CHUNK = 128
B_WIDTH = 768
C_WIDTH = 768
N_PATTERNS = 3
PATTERN_DILATION = (1, 4, 16)
PW = 256
D_IN_PROJ = 5632
OFF_AU, OFF_AV, OFF_BB, OFF_BC, OFF_BX, OFF_Q, OFF_K, OFF_V = 0, 512, 1024, 1792, 2560, 3328, 4096, 4864
N_CHIPS = 4
N_DEV = 8
BLK = 128

ADAM_LR, ADAM_B1, ADAM_B2, ADAM_EPS, ADAM_WD, ADAM_STEP = 0.001, 0.9, 0.999, 1e-08, 0.01, 10

V7X_VMEM_LIMIT = 56 * 1024 * 1024
MESH = pl.DeviceIdType.MESH
NEG = -1e30


def _cp(n_axes):
    return pltpu.CompilerParams(dimension_semantics=("arbitrary",) * n_axes, vmem_limit_bytes=V7X_VMEM_LIMIT)


def _hbm_spec():
    return pl.BlockSpec(memory_space=pl.ANY)


def _norm_matmul(name, x, g, wg, out_dtype):
    S, D = x.shape
    ns, _, Ns = wg.shape
    tm = min(512, S)

    def body(x_ref, g_ref, w_ref, o_ref, h_ref, hs_ref):
        @pl.when(pl.program_id(1) == 0)
        def _():
            xv = x_ref[...]
            y = xv * lax.rsqrt(jnp.mean(xv * xv, axis=-1, keepdims=True) + EPS) * g_ref[...]
            hb = y.astype(BF16)
            hs_ref[...] = hb
            h_ref[...] = hb
        o_ref[...] = jnp.dot(hs_ref[...], w_ref[...], preferred_element_type=F32).astype(o_ref.dtype)

    return pl.pallas_call(
        body, name=name, grid=(S // tm, ns),
        in_specs=[pl.BlockSpec((tm, D), lambda i, s: (i, 0)),
                  pl.BlockSpec((1, D), lambda i, s: (0, 0)),
                  pl.BlockSpec((None, D, Ns), lambda i, s: (s, 0, 0))],
        out_specs=[pl.BlockSpec((tm, Ns), lambda i, s: (i, s)),
                   pl.BlockSpec((tm, D), lambda i, s: (i, 0))],
        out_shape=[SDS((S, ns * Ns), out_dtype), SDS((S, D), BF16)],
        scratch_shapes=[pltpu.VMEM((tm, D), BF16)],
        compiler_params=_cp(2),
    )(x, g, wg)


def _matmul(name, a, b, out_shape, out_dtype, *, grid, a_spec, b_spec, o_spec, contract, acc_shape,
            extras=(), extra_specs=(), a_pre=None, epi=None):
    nk = grid[2]
    n_ex = len(extras)
    dims = (((contract[0],), (contract[1],)), ((), ()))

    def body(a_ref, b_ref, *rest):
        ex = rest[:n_ex]
        o_ref = rest[n_ex]
        acc_ref = rest[n_ex + 1]
        k = pl.program_id(2)

        @pl.when(k == 0)
        def _():
            acc_ref[...] = jnp.zeros_like(acc_ref)

        av = a_ref[...]
        if a_pre is not None:
            av = a_pre(av)
        acc_ref[...] += lax.dot_general(av, b_ref[...], dims, preferred_element_type=F32)

        @pl.when(k == nk - 1)
        def _():
            r = acc_ref[...]
            if epi is not None:
                r = epi(r, *[e[...] for e in ex])
            o_ref[...] = r.astype(o_ref.dtype)

    return pl.pallas_call(
        body, name=name, grid=grid,
        in_specs=[a_spec, b_spec, *extra_specs],
        out_specs=o_spec,
        out_shape=SDS(out_shape, out_dtype),
        scratch_shapes=[pltpu.VMEM(acc_shape, F32)],
        compiler_params=_cp(3),
    )(a, b, *extras)


def _relu2_bf16(t):
    r = jnp.maximum(t.astype(F32), 0.0)
    return (r * r).astype(BF16)


def _loss_kernel(y, t):
    S, D = y.shape
    tm = min(256, S)

    def body(y_ref, t_ref, dy_ref, dyb_ref, l_ref):
        @pl.when(pl.program_id(0) == 0)
        def _():
            l_ref[...] = jnp.zeros_like(l_ref)
        e = y_ref[...] - t_ref[...]
        l_ref[...] += jnp.sum(e * e, axis=0, keepdims=True)
        dy = e * (1.0 / D)
        dy_ref[...] = dy
        dyb_ref[...] = dy.astype(BF16)

    row = pl.BlockSpec((tm, D), lambda i: (i, 0))
    return pl.pallas_call(
        body, name="loss_head", grid=(S // tm,),
        in_specs=[row, row],
        out_specs=[row, row, pl.BlockSpec((1, D), lambda i: (0, 0))],
        out_shape=[SDS((S, D), F32), SDS((S, D), BF16), SDS((1, D), F32)],
        compiler_params=_cp(1),
    )(y, t)


def _rmsnorm_bwd(name, dh, x, g, dres):
    S, D = x.shape
    tm = min(256, S)

    def body(dh_ref, x_ref, g_ref, dres_ref, dx_ref, dxb_ref, dg_ref):
        @pl.when(pl.program_id(0) == 0)
        def _():
            dg_ref[...] = jnp.zeros_like(dg_ref)
        xv = x_ref[...]
        dhv = dh_ref[...]
        rstd = lax.rsqrt(jnp.mean(xv * xv, axis=-1, keepdims=True) + EPS)
        xhat = xv * rstd
        dg_ref[...] += jnp.sum(dhv * xhat, axis=0, keepdims=True)
        dxn = dhv * g_ref[...]
        dx = dres_ref[...] + rstd * (dxn - xhat * jnp.mean(dxn * xhat, axis=-1, keepdims=True))
        dx_ref[...] = dx
        dxb_ref[...] = dx.astype(BF16)

    row = pl.BlockSpec((tm, D), lambda i: (i, 0))
    vec = pl.BlockSpec((1, D), lambda i: (0, 0))
    return pl.pallas_call(
        body, name=name, grid=(S // tm,),
        in_specs=[row, row, vec, row],
        out_specs=[row, row, vec],
        out_shape=[SDS((S, D), F32), SDS((S, D), BF16), SDS((1, D), F32)],
        compiler_params=_cp(1),
    )(dh, x, g, dres)


def _adamw(name, w, g, m, v):
    R, C = w.shape
    tr = 256 if R % 256 == 0 else R
    c1 = 1.0 - ADAM_B1 ** ADAM_STEP
    c2 = 1.0 - ADAM_B2 ** ADAM_STEP

    def body(w_ref, g_ref, m_ref, v_ref, d_ref, nm_ref, nv_ref):
        gv = g_ref[...]
        nm = ADAM_B1 * m_ref[...] + (1.0 - ADAM_B1) * gv
        nv = ADAM_B2 * v_ref[...] + (1.0 - ADAM_B2) * (gv * gv)
        m_hat = nm / c1
        v_hat = nv / c2
        d_ref[...] = -ADAM_LR * (m_hat / (jnp.sqrt(v_hat) + ADAM_EPS) + ADAM_WD * w_ref[...])
        nm_ref[...] = nm
        nv_ref[...] = nv

    blk = pl.BlockSpec((tr, C), lambda i: (i, 0))
    return pl.pallas_call(
        body, name=name, grid=(R // tr,),
        in_specs=[blk] * 4, out_specs=[blk] * 3,
        out_shape=[SDS((R, C), F32)] * 3,
        compiler_params=_cp(1),
    )(w, g, m, v)


def _pair_select(lane, lo, hi):
    return jnp.where(lane < HEAD_DIM, lo, hi)


def _sgu_fwd(name, p, wt, bb):
    S = p.shape[0]

    def body(u_ref, v_ref, wt_ref, bb_ref, o_ref):
        lane = lax.broadcasted_iota(jnp.int32, (CHUNK, 128), 1)
        for pp in range(A_HEADS // 2):
            cs = slice(128 * pp, 128 * (pp + 1))
            vb = v_ref[:, cs].astype(BF16)
            mixed = _pair_select(lane,
                                 jnp.dot(wt_ref[2 * pp], vb, preferred_element_type=F32),
                                 jnp.dot(wt_ref[2 * pp + 1], vb, preferred_element_type=F32)) + bb_ref[:, cs]
            o_ref[:, cs] = (u_ref[:, cs] * mixed).astype(o_ref.dtype)

    return pl.pallas_call(
        body, name=name, grid=(S // CHUNK,),
        in_specs=[pl.BlockSpec((CHUNK, A_WIDTH), lambda c: (c, OFF_AU // A_WIDTH)),
                  pl.BlockSpec((CHUNK, A_WIDTH), lambda c: (c, OFF_AV // A_WIDTH)),
                  pl.BlockSpec((A_HEADS, CHUNK, CHUNK), lambda c: (0, 0, 0)),
                  pl.BlockSpec((CHUNK, A_WIDTH), lambda c: (0, 0))],
        out_specs=pl.BlockSpec((CHUNK, A_WIDTH), lambda c: (c, 0)),
        out_shape=SDS((S, A_WIDTH), BF16),
        compiler_params=_cp(1),
    )(p, p, wt, bb)


def _sgu_bwd(name, p, dycat, wt, wtt, bb):
    S = p.shape[0]

    def body(u_ref, v_ref, dy_ref, wt_ref, wtt_ref, bb_ref, du_ref, dv_ref, dw_ref, db_ref, dbacc_ref):
        c = pl.program_id(0)

        @pl.when(c == 0)
        def _():
            dw_ref[...] = jnp.zeros_like(dw_ref)
            dbacc_ref[...] = jnp.zeros_like(dbacc_ref)

        lane = lax.broadcasted_iota(jnp.int32, (CHUNK, 128), 1)
        row = lax.broadcasted_iota(jnp.int32, (CHUNK, 128), 0)
        causal = row >= lane
        for pp in range(A_HEADS // 2):
            cs = slice(128 * pp, 128 * (pp + 1))
            v = v_ref[:, cs]
            vb = v.astype(BF16)
            u = u_ref[:, cs]
            dy = dy_ref[:, cs]
            mixed = _pair_select(lane,
                                 jnp.dot(wt_ref[2 * pp], vb, preferred_element_type=F32),
                                 jnp.dot(wt_ref[2 * pp + 1], vb, preferred_element_type=F32)) + bb_ref[:, cs]
            du_ref[:, cs] = (dy * mixed).astype(du_ref.dtype)
            dm = dy * u
            dmb = dm.astype(BF16)
            dv = _pair_select(lane,
                              jnp.dot(wtt_ref[2 * pp], dmb, preferred_element_type=F32),
                              jnp.dot(wtt_ref[2 * pp + 1], dmb, preferred_element_type=F32))
            dv_ref[:, cs] = dv.astype(dv_ref.dtype)
            dbacc_ref[:, cs] += dm
            nt = (((1,), (1,)), ((), ()))
            dm_lo = jnp.where(lane < HEAD_DIM, dm, 0.0).astype(BF16)
            dm_hi = jnp.where(lane >= HEAD_DIM, dm, 0.0).astype(BF16)
            dw_ref[2 * pp] += jnp.where(causal, lax.dot_general(dm_lo, vb, nt, preferred_element_type=F32), 0.0)
            dw_ref[2 * pp + 1] += jnp.where(causal, lax.dot_general(dm_hi, vb, nt, preferred_element_type=F32), 0.0)

        @pl.when(c == S // CHUNK - 1)
        def _():
            out = jnp.zeros((CHUNK, 128), F32)
            for pp in range(A_HEADS // 2):
                acc = dbacc_ref[:, 128 * pp:128 * (pp + 1)]
                s_lo = jnp.sum(jnp.where(lane < HEAD_DIM, acc, 0.0), axis=1, keepdims=True)
                s_hi = jnp.sum(jnp.where(lane >= HEAD_DIM, acc, 0.0), axis=1, keepdims=True)
                out = jnp.where(lane == 2 * pp, s_lo, out)
                out = jnp.where(lane == 2 * pp + 1, s_hi, out)
            db_ref[...] = out

    chunk = lambda col: pl.BlockSpec((CHUNK, A_WIDTH), lambda c: (c, col))
    wspec = pl.BlockSpec((A_HEADS, CHUNK, CHUNK), lambda c: (0, 0, 0))
    return pl.pallas_call(
        body, name=name, grid=(S // CHUNK,),
        in_specs=[chunk(OFF_AU // A_WIDTH), chunk(OFF_AV // A_WIDTH), chunk(0), wspec, wspec,
                  pl.BlockSpec((CHUNK, A_WIDTH), lambda c: (0, 0))],
        out_specs=[chunk(0), chunk(0), wspec, pl.BlockSpec((CHUNK, 128), lambda c: (0, 0))],
        out_shape=[SDS((S, A_WIDTH), BF16), SDS((S, A_WIDTH), BF16),
                   SDS((A_HEADS, CHUNK, CHUNK), F32), SDS((CHUNK, 128), F32)],
        scratch_shapes=[pltpu.VMEM((CHUNK, A_WIDTH), F32)],
        compiler_params=_cp(1),
    )(p, p, dycat, wt, wtt, bb)


CONV_HALO = 8


def _shift_down(a, halo, k):
    T = a.shape[0]
    row = lax.broadcasted_iota(jnp.int32, a.shape, 0)
    out = pltpu.roll(a, k, 0)
    for r in range(k):
        out = jnp.where(row == r, halo[CONV_HALO - k + r:CONV_HALO - k + r + 1, :], out)
    return out


def _shift_up(a, halo, k):
    T = a.shape[0]
    row = lax.broadcasted_iota(jnp.int32, a.shape, 0)
    out = pltpu.roll(a, T - k, 0)
    for r in range(k):
        out = jnp.where(row == T - k + r, halo[r:r + 1, :], out)
    return out


def _conv_specs(S, T):
    hb = T // CONV_HALO
    last = S // CONV_HALO - 1
    tile = lambda col0: pl.BlockSpec((T, 128), lambda j, i: (i, col0 + j))
    prev = lambda col0: pl.BlockSpec((CONV_HALO, 128), lambda j, i: (jnp.maximum(i * hb - 1, 0), col0 + j))
    nxt = lambda col0: pl.BlockSpec((CONV_HALO, 128), lambda j, i: (jnp.minimum((i + 1) * hb, last), col0 + j))
    return tile, prev, nxt


def _conv_fwd(name, p, w):
    S = p.shape[0]
    T = min(512, S)
    tile, prev, _ = _conv_specs(S, T)
    cb, cc, cx = OFF_BB // 128, OFF_BC // 128, OFF_BX // 128

    def body(b_ref, c_ref, x_ref, ch_ref, xh_ref, w_ref, o_ref):
        i = pl.program_id(1)
        z = c_ref[...] * x_ref[...]
        zh = jnp.where(i > 0, ch_ref[...] * xh_ref[...], 0.0)
        z1 = _shift_down(z, zh, 1)
        z2 = _shift_down(z, zh, 2)
        conv = w_ref[0:1, :] * z2 + w_ref[1:2, :] * z1 + w_ref[2:3, :] * z
        o_ref[...] = (b_ref[...] * conv).astype(o_ref.dtype)

    return pl.pallas_call(
        body, name=name, grid=(B_WIDTH // 128, S // T),
        in_specs=[tile(cb), tile(cc), tile(cx), prev(cc), prev(cx),
                  pl.BlockSpec((3, 128), lambda j, i: (0, j))],
        out_specs=tile(0),
        out_shape=SDS((S, B_WIDTH), BF16),
        compiler_params=_cp(2),
    )(p, p, p, p, p, w)


def _conv_bwd(name, p, dycat, w):
    S = p.shape[0]
    T = min(512, S)
    tile, prev, nxt = _conv_specs(S, T)
    cb, cc, cx = OFF_BB // 128, OFF_BC // 128, OFF_BX // 128
    cdy = A_WIDTH // 128
    n_i = S // T

    def body(b_ref, c_ref, x_ref, dy_ref, ch_ref, xh_ref, bn_ref, dyn_ref, w_ref,
             db_ref, dc_ref, dx_ref, dw_ref):
        i = pl.program_id(1)

        @pl.when(i == 0)
        def _():
            dw_ref[...] = jnp.zeros_like(dw_ref)

        cv = c_ref[...]
        xv = x_ref[...]
        z = cv * xv
        zh = jnp.where(i > 0, ch_ref[...] * xh_ref[...], 0.0)
        z1 = _shift_down(z, zh, 1)
        z2 = _shift_down(z, zh, 2)
        w0, w1, w2 = w_ref[0:1, :], w_ref[1:2, :], w_ref[2:3, :]
        conv = w0 * z2 + w1 * z1 + w2 * z
        dy = dy_ref[...]
        db_ref[...] = (dy * conv).astype(db_ref.dtype)
        dconv = dy * b_ref[...]
        dconv_n = jnp.where(i < n_i - 1, dyn_ref[...] * bn_ref[...], 0.0)
        dz = w2 * dconv + w1 * _shift_up(dconv, dconv_n, 1) + w0 * _shift_up(dconv, dconv_n, 2)
        dc_ref[...] = (dz * xv).astype(dc_ref.dtype)
        dx_ref[...] = (dz * cv).astype(dx_ref.dtype)
        dw_ref[0:1, :] += jnp.sum(dconv * z2, axis=0, keepdims=True)
        dw_ref[1:2, :] += jnp.sum(dconv * z1, axis=0, keepdims=True)
        dw_ref[2:3, :] += jnp.sum(dconv * z, axis=0, keepdims=True)

    wspec = pl.BlockSpec((3, 128), lambda j, i: (0, j))
    return pl.pallas_call(
        body, name=name, grid=(B_WIDTH // 128, n_i),
        in_specs=[tile(cb), tile(cc), tile(cx), tile(cdy), prev(cc), prev(cx), nxt(cb), nxt(cdy), wspec],
        out_specs=[tile(0), tile(0), tile(0), wspec],
        out_shape=[SDS((S, B_WIDTH), BF16)] * 3 + [SDS((3, B_WIDTH), F32)],
        compiler_params=_cp(2),
    )(p, p, p, dycat, p, p, p, dycat, w)


def _seg_sum(t, bd):
    hi = t.astype(BF16)
    lo = (t - hi.astype(F32)).astype(BF16)
    return jnp.dot(hi, bd, preferred_element_type=F32) + jnp.dot(lo, bd, preferred_element_type=F32)


def _head_norm(x, g, bd):
    rstd = lax.rsqrt(_seg_sum(x * x, bd) * (1.0 / HEAD_DIM) + EPS)
    xhat = x * rstd
    return xhat * g, xhat, rstd


def _head_norm_bwd(dy, g, xhat, rstd, bd):
    dxh = dy * g
    return rstd * (dxh - xhat * (_seg_sum(dxh * xhat, bd) * (1.0 / HEAD_DIM)))


def _band_mask(has_prev):
    row = lax.broadcasted_iota(jnp.int32, (BLK, 2 * BLK), 0)
    col = lax.broadcasted_iota(jnp.int32, (BLK, 2 * BLK), 1)
    first_key = jnp.where(has_prev, 0, BLK)
    return (col >= row) & (col <= row + BLK) & (col >= first_key)


def _first_of_segment(g, n, n_blocks):
    per_seg = lax.shift_right_logical(jnp.int32(n_blocks), 2 * g)
    return (n & (per_seg - 1)) == 0


def _attn_fwd(name, q3, k3, v3, gq, gk, bd):
    _, S, _ = q3.shape
    nblk = S // BLK
    nt = (((1,), (1,)), ((), ()))

    def body(q_ref, kc_ref, kp_ref, vc_ref, vp_ref, gq_ref, gk_ref, bd_ref, o_ref, lse_ref):
        g = pl.program_id(0)
        n = pl.program_id(1)
        has_prev = jnp.logical_not(_first_of_segment(g, n, nblk))
        bdv = bd_ref[...]
        qn, _, _ = _head_norm(q_ref[...], gq_ref[...], bdv)
        kn, _, _ = _head_norm(jnp.concatenate([kp_ref[...], kc_ref[...]], axis=0), gk_ref[...], bdv)
        knb = kn.astype(BF16)
        vb = jnp.concatenate([vp_ref[...], vc_ref[...]], axis=0).astype(BF16)
        band = _band_mask(has_prev)
        lane = lax.broadcasted_iota(jnp.int32, (1, PW), 1)
        o_acc = jnp.zeros((BLK, PW), F32)
        l_acc = jnp.zeros((BLK, PW), F32)
        for j in range(PW // HEAD_DIM):
            hm = (lane >= HEAD_DIM * j) & (lane < HEAD_DIM * (j + 1))
            qj = jnp.where(hm, qn, 0.0).astype(BF16)
            s = lax.dot_general(qj, knb, nt, preferred_element_type=F32) * (HEAD_DIM ** -0.5)
            s = jnp.where(band, s, NEG)
            m = jnp.max(s, axis=1, keepdims=True)
            e = jnp.exp(s - m)
            den = jnp.sum(e, axis=1, keepdims=True)
            pv = jnp.dot(e.astype(BF16), vb, preferred_element_type=F32)
            o_acc = jnp.where(hm, pv / den, o_acc)
            l_acc = jnp.where(hm, m + jnp.log(den), l_acc)
        o_ref[...] = o_acc
        lse_ref[...] = l_acc

    cur = pl.BlockSpec((None, BLK, PW), lambda g, n: (g, n, 0))
    prv = pl.BlockSpec((None, BLK, PW), lambda g, n: (g, jnp.maximum(n - 1, 0), 0))
    vec = pl.BlockSpec((1, PW), lambda g, n: (0, 0))
    return pl.pallas_call(
        body, name=name, grid=(N_PATTERNS, nblk),
        in_specs=[cur, cur, prv, cur, prv, vec, vec, pl.BlockSpec((PW, PW), lambda g, n: (0, 0))],
        out_specs=[cur, cur],
        out_shape=[SDS((N_PATTERNS, S, PW), F32)] * 2,
        compiler_params=_cp(2),
    )(q3, k3, k3, v3, v3, gq, gk, bd)


def _attn_bwd(name, q3, k3, v3, lse3, do3, c3, gq, gk, bd):
    _, S, _ = q3.shape
    nblk = S // BLK
    nt = (((1,), (1,)), ((), ()))
    tn = (((0,), (0,)), ((), ()))

    def body(q_ref, kc_ref, kp_ref, vc_ref, vp_ref, lse_ref, do_ref, c_ref, gq_ref, gk_ref, bd_ref,
             dq_ref, dk_ref, dv_ref, dgq_ref, dgk_ref, ck_ref, cv_ref):
        g = pl.program_id(0)
        n = pl.program_id(1)
        live = n < nblk
        ne = jnp.minimum(n, nblk - 1)
        has_prev = jnp.logical_not(_first_of_segment(g, ne, nblk))

        @pl.when(n == 0)
        def _():
            ck_ref[...] = jnp.zeros_like(ck_ref)
            cv_ref[...] = jnp.zeros_like(cv_ref)
            dgq_ref[...] = jnp.zeros_like(dgq_ref)
            dgk_ref[...] = jnp.zeros_like(dgk_ref)

        bdv = bd_ref[...]
        gqv = gq_ref[...]
        gkv = gk_ref[...]
        qn, qhat, qrstd = _head_norm(q_ref[...], gqv, bdv)
        kn, khat, krstd = _head_norm(jnp.concatenate([kp_ref[...], kc_ref[...]], axis=0), gkv, bdv)
        knb = kn.astype(BF16)
        vb = jnp.concatenate([vp_ref[...], vc_ref[...]], axis=0).astype(BF16)
        band = _band_mask(has_prev)
        lane = lax.broadcasted_iota(jnp.int32, (1, PW), 1)
        lse = lse_ref[...]
        do = do_ref[...]
        cc = c_ref[...]
        dqn = jnp.zeros((BLK, PW), F32)
        dkn = jnp.zeros((2 * BLK, PW), F32)
        dvv = jnp.zeros((2 * BLK, PW), F32)
        for j in range(PW // HEAD_DIM):
            hm = (lane >= HEAD_DIM * j) & (lane < HEAD_DIM * (j + 1))
            qj = jnp.where(hm, qn, 0.0).astype(BF16)
            doj = jnp.where(hm, do, 0.0).astype(BF16)
            s = lax.dot_general(qj, knb, nt, preferred_element_type=F32) * (HEAD_DIM ** -0.5)
            lse_j = jnp.max(jnp.where(hm, lse, NEG), axis=1, keepdims=True)
            c_j = jnp.max(jnp.where(hm, cc, NEG), axis=1, keepdims=True)
            prob = jnp.where(band, jnp.exp(s - lse_j), 0.0)
            dp = lax.dot_general(doj, vb, nt, preferred_element_type=F32)
            ds = (prob * (dp + c_j) * (HEAD_DIM ** -0.5)).astype(BF16)
            dqn = jnp.where(hm, jnp.dot(ds, knb, preferred_element_type=F32), dqn)
            dkn += lax.dot_general(ds, qj, tn, preferred_element_type=F32)
            dvv += lax.dot_general(prob.astype(BF16), doj, tn, preferred_element_type=F32)

        dq_ref[...] = _head_norm_bwd(dqn, gqv, qhat, qrstd, bdv).astype(dq_ref.dtype)
        dk2 = _head_norm_bwd(dkn, gkv, khat, krstd, bdv)
        keep = jnp.where(live, 1.0, 0.0)
        dgq_ref[...] += keep * jnp.sum(dqn * qhat, axis=0, keepdims=True)
        dgk_ref[...] += keep * jnp.sum(dkn * khat, axis=0, keepdims=True)
        dk_ref[...] = (ck_ref[...] + keep * dk2[:BLK]).astype(dk_ref.dtype)
        dv_ref[...] = (cv_ref[...] + keep * dvv[:BLK]).astype(dv_ref.dtype)
        ck_ref[...] = dk2[BLK:]
        cv_ref[...] = dvv[BLK:]

    last = nblk - 1
    cur = pl.BlockSpec((None, BLK, PW), lambda g, n: (g, jnp.minimum(n, last), 0))
    prv = pl.BlockSpec((None, BLK, PW), lambda g, n: (g, jnp.maximum(jnp.minimum(n, last) - 1, 0), 0))
    done = pl.BlockSpec((None, BLK, PW), lambda g, n: (g, jnp.maximum(n - 1, 0), 0))
    vec = pl.BlockSpec((1, PW), lambda g, n: (0, 0))
    gvec = pl.BlockSpec((None, 1, PW), lambda g, n: (g, 0, 0))
    return pl.pallas_call(
        body, name=name, grid=(N_PATTERNS, nblk + 1),
        in_specs=[cur, cur, prv, cur, prv, cur, cur, cur, vec, vec, pl.BlockSpec((PW, PW), lambda g, n: (0, 0))],
        out_specs=[cur, done, done, gvec, gvec],
        out_shape=[SDS((N_PATTERNS, S, PW), BF16)] * 3 + [SDS((N_PATTERNS, 1, PW), F32)] * 2,
        scratch_shapes=[pltpu.VMEM((BLK, PW), F32), pltpu.VMEM((BLK, PW), F32)],
        compiler_params=_cp(2),
    )(q3, k3, k3, v3, v3, lse3, do3, c3, gq, gk, bd)


def _mix_fwd(name, o3, lse3):
    _, S, _ = o3.shape
    tm = min(512, S)

    def body(o_ref, l_ref, y_ref):
        l = [l_ref[g] for g in range(N_PATTERNS)]
        m = jnp.maximum(jnp.maximum(l[0], l[1]), l[2])
        e = [jnp.exp(t - m) for t in l]
        inv = 1.0 / (e[0] + e[1] + e[2])
        for g in range(N_PATTERNS):
            y_ref[:, PW * g:PW * (g + 1)] = (o_ref[g] * (e[g] * inv)).astype(y_ref.dtype)

    blk3 = pl.BlockSpec((N_PATTERNS, tm, PW), lambda i: (0, i, 0))
    return pl.pallas_call(
        body, name=name, grid=(S // tm,),
        in_specs=[blk3, blk3],
        out_specs=pl.BlockSpec((tm, C_WIDTH), lambda i: (i, 0)),
        out_shape=SDS((S, C_WIDTH), BF16),
        compiler_params=_cp(1),
    )(o3, lse3)


def _mix_bwd(name, o3, lse3, dycat, bd):
    _, S, _ = o3.shape
    tm = min(512, S)
    c0 = (A_WIDTH + B_WIDTH) // PW

    def body(o_ref, l_ref, dy0_ref, dy1_ref, dy2_ref, bd_ref, do_ref, c_ref):
        bdv = bd_ref[...]
        dys = [dy0_ref[...], dy1_ref[...], dy2_ref[...]]
        l = [l_ref[g] for g in range(N_PATTERNS)]
        m = jnp.maximum(jnp.maximum(l[0], l[1]), l[2])
        e = [jnp.exp(t - m) for t in l]
        inv = 1.0 / (e[0] + e[1] + e[2])
        alpha = [t * inv for t in e]
        da = [_seg_sum(dys[g] * o_ref[g], bdv) for g in range(N_PATTERNS)]
        mean_da = alpha[0] * da[0] + alpha[1] * da[1] + alpha[2] * da[2]
        for g in range(N_PATTERNS):
            do_ref[g] = dys[g] * alpha[g]
            c_ref[g] = -alpha[g] * mean_da

    blk3 = pl.BlockSpec((N_PATTERNS, tm, PW), lambda i: (0, i, 0))
    dyspec = lambda g: pl.BlockSpec((tm, PW), lambda i: (i, c0 + g))
    return pl.pallas_call(
        body, name=name, grid=(S // tm,),
        in_specs=[blk3, blk3, dyspec(0), dyspec(1), dyspec(2), pl.BlockSpec((PW, PW), lambda i: (0, 0))],
        out_specs=[blk3, blk3],
        out_shape=[SDS((N_PATTERNS, S, PW), F32)] * 2,
        compiler_params=_cp(1),
    )(o3, lse3, dycat, dycat, dycat, bd)


def _mesh_pos():
    x, y, c = lax.axis_index("x"), lax.axis_index("y"), lax.axis_index("c")
    chips = [(1 - x, y), (x, 1 - y), (1 - x, 1 - y)]
    chip_idx = [2 * cx + cy for cx, cy in chips]
    return x, y, c, 2 * x + y, chips, chip_idx


def _place_shard(name, w, layer, chip_arr, out_dtype):
    _, R, C = w.shape
    tr = min(256, R)

    def body(chip_ref, w_ref, o_ref):
        o_ref[...] = w_ref[...].astype(o_ref.dtype)

    return pl.pallas_call(
        body, name=name,
        grid_spec=pltpu.PrefetchScalarGridSpec(
            num_scalar_prefetch=1, grid=(R // tr,),
            in_specs=[pl.BlockSpec((None, tr, C), lambda i, chip_ref: (layer, i, 0))],
            out_specs=pl.BlockSpec((None, tr, C), lambda i, chip_ref: (chip_ref[0], i, 0))),
        out_shape=SDS((N_CHIPS, R, C), out_dtype),
        compiler_params=_cp(1),
    )(chip_arr, w)


HBM_SPEC = pl.BlockSpec(memory_space=pltpu.HBM)
SEM_SPEC = pl.BlockSpec(memory_space=pltpu.SEMAPHORE)
SPLIT_COPY = pltpu.SideEffectType.DATAFLOW_SIDE_EFFECTING
N_PEER_CHIPS = N_CHIPS - 1


def _in_hbm(a):
    return pltpu.with_memory_space_constraint(a, pltpu.HBM)


def _gather_start(bufs):
    T = len(bufs)

    def body(*refs):
        ins = refs[:T]
        send_sems, recv_sems = refs[T:2 * T], refs[2 * T:3 * T]
        x, y, c, me, chips, chip_idx = _mesh_pos()
        for t in range(T):
            hr = ins[t].shape[1] // 2
            mine = ins[t].at[me, pl.ds(c * hr, hr), :]
            for j in range(N_PEER_CHIPS):
                pltpu.make_async_remote_copy(src_ref=mine, dst_ref=mine, send_sem=send_sems[t].at[j],
                                             recv_sem=recv_sems[t].at[j], device_id=(*chips[j], c),
                                             device_id_type=MESH).start()

    sems = [pltpu.SemaphoreType.DMA((N_PEER_CHIPS,))] * T
    out = pl.pallas_call(
        body, name="gather_start",
        in_specs=[HBM_SPEC] * T,
        out_specs=[SEM_SPEC] * (2 * T) + [HBM_SPEC] * T,
        out_shape=sems + sems + [pltpu.HBM(b.shape, b.dtype) for b in bufs],
        input_output_aliases={t: 2 * T + t for t in range(T)},
        compiler_params=pltpu.CompilerParams(has_side_effects=SPLIT_COPY),
    )(*[_in_hbm(b) for b in bufs])
    return out[:T], out[T:2 * T], out[2 * T:]


def _gather_wait(name, buf, send_sem, recv_sem, after):
    n_in = 3 if after is None else 4

    def body(*refs):
        buf_ref, ssem, rsem = refs[:3]
        x, y, c, me, chips, chip_idx = _mesh_pos()
        hr = buf_ref.shape[1] // 2
        mine = buf_ref.at[me, pl.ds(c * hr, hr), :]
        for j in range(N_PEER_CHIPS):
            got = buf_ref.at[chip_idx[j], pl.ds(c * hr, hr), :]
            cp = pltpu.make_async_remote_copy(src_ref=mine, dst_ref=got, send_sem=ssem.at[j], recv_sem=rsem.at[j],
                                              device_id=(*chips[j], c), device_id_type=MESH)
            cp.wait_send()
            cp.wait_recv()

    args = [buf, send_sem, recv_sem] + ([] if after is None else [after])
    return pl.pallas_call(
        body, name=name,
        in_specs=[HBM_SPEC, SEM_SPEC, SEM_SPEC] + [_hbm_spec()] * (n_in - 3),
        out_specs=HBM_SPEC,
        out_shape=pltpu.HBM(buf.shape, buf.dtype),
        input_output_aliases={0: 0},
        compiler_params=pltpu.CompilerParams(has_side_effects=SPLIT_COPY),
    )(*args)


def _gather_forward(name, buf):
    def body(in_ref, out_ref, send_sems, recv_sems):
        x, y, c, me, chips, chip_idx = _mesh_pos()
        hr = out_ref.shape[1] // 2
        cps = []
        for j in range(N_PEER_CHIPS):
            got = out_ref.at[chip_idx[j], pl.ds(c * hr, hr), :]
            cp = pltpu.make_async_remote_copy(src_ref=got, dst_ref=got, send_sem=send_sems.at[j],
                                              recv_sem=recv_sems.at[j], device_id=(x, y, 1 - c), device_id_type=MESH)
            cp.start()
            cps.append(cp)
        for j in range(N_PEER_CHIPS):
            theirs = out_ref.at[chip_idx[j], pl.ds((1 - c) * hr, hr), :]
            pltpu.make_async_remote_copy(src_ref=theirs, dst_ref=theirs, send_sem=send_sems.at[j],
                                         recv_sem=recv_sems.at[j], device_id=(x, y, 1 - c),
                                         device_id_type=MESH).wait_recv()
        for cp in cps:
            cp.wait_send()

    return pl.pallas_call(
        body, name=name,
        in_specs=[_hbm_spec()], out_specs=_hbm_spec(),
        out_shape=SDS(buf.shape, buf.dtype),
        input_output_aliases={0: 0},
        scratch_shapes=[pltpu.SemaphoreType.DMA((N_PEER_CHIPS,)), pltpu.SemaphoreType.DMA((N_PEER_CHIPS,))],
    )(buf)


class _GatheredWeights:
    def __init__(self, keys, bufs):
        send_sems, recv_sems, thru = _gather_start(bufs)
        self._pending = {k: (b, s, r) for k, b, s, r in zip(keys, thru, send_sems, recv_sems)}
        self._ready = {}

    def get(self, name, layer, after=None):
        key = (name, layer)
        if key not in self._ready:
            buf, ssem, rsem = self._pending.pop(key)
            buf = _gather_wait(f"gather_wait_{name}_{layer}", buf, ssem, rsem, after)
            self._ready[key] = _gather_forward(f"gather_fwd_{name}_{layer}", buf)
        return self._ready[key]


def _swap_halves(grads):
    T = len(grads)

    def body(*refs):
        ins, outs = refs[:T], refs[T:2 * T]
        send_sems, recv_sems = refs[2 * T:]
        x, y, c, _, _, _ = _mesh_pos()
        cps = []
        for t in range(T):
            hr = ins[t].shape[1] // 2
            cp = pltpu.make_async_remote_copy(
                src_ref=ins[t].at[:, pl.ds((1 - c) * hr, hr), :], dst_ref=outs[t],
                send_sem=send_sems.at[t], recv_sem=recv_sems.at[t],
                device_id=(x, y, 1 - c), device_id_type=MESH)
            cp.start()
            cps.append(cp)
        for cp in cps:
            cp.wait()

    return pl.pallas_call(
        body, name="rs_swap_halves",
        in_specs=[_hbm_spec()] * T, out_specs=[_hbm_spec()] * T,
        out_shape=[SDS((g.shape[0], g.shape[1] // 2, g.shape[2]), g.dtype) for g in grads],
        scratch_shapes=[pltpu.SemaphoreType.DMA((T,)), pltpu.SemaphoreType.DMA((T,))],
    )(*grads)


def _add_my_half(name, g, r, c_arr):
    ns, R, C = g.shape
    hr = R // 2
    tr = min(256, hr)
    nt = hr // tr

    def body(c_ref, g_ref, r_ref, o_ref, land_ref):
        t = (g_ref[...] + r_ref[...]).astype(o_ref.dtype)
        o_ref[...] = t
        land_ref[...] = t

    out = pl.BlockSpec((None, tr, C), lambda s, i, c_ref: (s, i, 0))
    return pl.pallas_call(
        body, name=name,
        grid_spec=pltpu.PrefetchScalarGridSpec(
            num_scalar_prefetch=1, grid=(ns, nt),
            in_specs=[pl.BlockSpec((None, tr, C), lambda s, i, c_ref: (s, c_ref[0] * nt + i, 0)), out],
            out_specs=[out, out]),
        out_shape=[SDS((ns, hr, C), BF16)] * 2,
        compiler_params=_cp(2),
    )(c_arr, g, r)


def _chip_exchange(parts, lands):
    T = len(parts)

    def body(*refs):
        ins, outs = refs[:T], refs[2 * T:3 * T]
        send_sems, recv_sems = refs[3 * T:]
        x, y, c, me, chips, chip_idx = _mesh_pos()
        started = []

        def copy(t, j, src, dst):
            return pltpu.make_async_remote_copy(src_ref=src, dst_ref=dst, send_sem=send_sems.at[3 * t + j],
                                                recv_sem=recv_sems.at[3 * t + j],
                                                device_id=(*chips[j], c), device_id_type=MESH)

        for t in range(T):
            for j in range(3):
                cp = copy(t, j, ins[t].at[chip_idx[j]], outs[t].at[me])
                cp.start()
                started.append(cp)
        for t in range(T):
            for j in range(3):
                got = outs[t].at[chip_idx[j]]
                copy(t, j, got, got).wait_recv()
        for cp in started:
            cp.wait_send()

    return pl.pallas_call(
        body, name="rs_chip_exchange",
        in_specs=[_hbm_spec()] * (2 * T), out_specs=[_hbm_spec()] * T,
        out_shape=[SDS(p.shape, p.dtype) for p in lands],
        input_output_aliases={T + t: t for t in range(T)},
        scratch_shapes=[pltpu.SemaphoreType.DMA((3 * T,)), pltpu.SemaphoreType.DMA((3 * T,))],
    )(*parts, *lands)


def _sum_chips(name, r, c_arr, layer, n_layers, prev):
    ns, H, C = r.shape
    tr = min(256, H)
    nt = H // tr

    def body(c_ref, r_ref, *rest):
        o_ref = rest[-1]
        o_ref[...] = ((r_ref[0].astype(F32) + r_ref[1].astype(F32)) + r_ref[2].astype(F32)) + r_ref[3].astype(F32)

    in_specs = [pl.BlockSpec((ns, tr, C), lambda i, c_ref: (0, i, 0))]
    args = [c_arr, r]
    aliases = {}
    if prev is not None:
        in_specs.append(_hbm_spec())
        args.append(prev)
        aliases = {2: 0}
    return pl.pallas_call(
        body, name=name,
        grid_spec=pltpu.PrefetchScalarGridSpec(
            num_scalar_prefetch=1, grid=(nt,), in_specs=in_specs,
            out_specs=pl.BlockSpec((None, tr, C), lambda i, c_ref: (layer, c_ref[0] * nt + i, 0))),
        out_shape=SDS((n_layers, 2 * H, C), F32),
        input_output_aliases=aliases,
        compiler_params=_cp(1),
    )(*args)


def _join_halves(bufs):
    T = len(bufs)

    def body(*refs):
        outs = refs[T:2 * T]
        send_sems, recv_sems = refs[2 * T:]
        x, y, c, _, _, _ = _mesh_pos()
        cps = []
        for t in range(T):
            hr = outs[t].shape[1] // 2
            mine = outs[t].at[:, pl.ds(c * hr, hr), :]
            cp = pltpu.make_async_remote_copy(src_ref=mine, dst_ref=mine, send_sem=send_sems.at[t],
                                              recv_sem=recv_sems.at[t], device_id=(x, y, 1 - c), device_id_type=MESH)
            cp.start()
            cps.append(cp)
        for t in range(T):
            hr = outs[t].shape[1] // 2
            theirs = outs[t].at[:, pl.ds((1 - c) * hr, hr), :]
            pltpu.make_async_remote_copy(src_ref=theirs, dst_ref=theirs, send_sem=send_sems.at[t],
                                         recv_sem=recv_sems.at[t], device_id=(x, y, 1 - c),
                                         device_id_type=MESH).wait_recv()
        for cp in cps:
            cp.wait_send()

    return pl.pallas_call(
        body, name="rs_join_halves",
        in_specs=[_hbm_spec()] * T, out_specs=[_hbm_spec()] * T,
        out_shape=[SDS(b.shape, b.dtype) for b in bufs],
        input_output_aliases={t: t for t in range(T)},
        scratch_shapes=[pltpu.SemaphoreType.DMA((T,)), pltpu.SemaphoreType.DMA((T,))],
    )(*bufs)


def _allreduce_small(buf):
    R, C = buf.shape

    def body(in_ref, out_ref, land_ref, send_sems, recv_sems):
        x, y, c = lax.axis_index("x"), lax.axis_index("y"), lax.axis_index("c")
        me = 4 * x + 2 * y + c
        land_ref[me] = in_ref[...]
        cps = []
        for k in range(1, N_DEV):
            kx, ky, kc = (k >> 2) & 1, (k >> 1) & 1, k & 1
            peer = (x ^ kx, y ^ ky, c ^ kc)
            cp = pltpu.make_async_remote_copy(src_ref=in_ref, dst_ref=land_ref.at[me],
                                              send_sem=send_sems.at[k - 1], recv_sem=recv_sems.at[k - 1],
                                              device_id=peer, device_id_type=MESH)
            cp.start()
            cps.append(cp)
        for k in range(1, N_DEV):
            src = me ^ k
            slot = land_ref.at[src]
            pltpu.make_async_remote_copy(src_ref=slot, dst_ref=slot, send_sem=send_sems.at[k - 1],
                                         recv_sem=recv_sems.at[k - 1], device_id=(x, y, c),
                                         device_id_type=MESH).wait_recv()
        for cp in cps:
            cp.wait_send()
        acc = land_ref[0]
        for d in range(1, N_DEV):
            acc = acc + land_ref[d]
        out_ref[...] = acc

    return pl.pallas_call(
        body, name="allreduce_small",
        in_specs=[pl.BlockSpec(memory_space=pltpu.VMEM)],
        out_specs=pl.BlockSpec(memory_space=pltpu.VMEM),
        out_shape=SDS((R, C), buf.dtype),
        scratch_shapes=[pltpu.VMEM((N_DEV, R, C), buf.dtype),
                        pltpu.SemaphoreType.DMA((N_DEV - 1,)), pltpu.SemaphoreType.DMA((N_DEV - 1,))],
        compiler_params=pltpu.CompilerParams(vmem_limit_bytes=V7X_VMEM_LIMIT),
    )(buf)


def _deinterleave(t, d):
    if d == 1:
        return t
    S, W = t.shape
    return t.reshape(S // d, d, W).transpose(1, 0, 2).reshape(S, W)


def _interleave(t, d):
    if d == 1:
        return t
    S, W = t.shape
    return t.reshape(d, S // d, W).transpose(1, 0, 2).reshape(S, W)


def _to_patterns(t, off):
    return jnp.stack([_deinterleave(t[:, off + PW * g:off + PW * (g + 1)], PATTERN_DILATION[g])
                      for g in range(N_PATTERNS)])


def _from_patterns(t3):
    return jnp.stack([_interleave(t3[g], PATTERN_DILATION[g]) for g in range(N_PATTERNS)])


def _pack_rows(vectors):
    flat = jnp.concatenate([v.reshape(-1) for v in vectors])
    n = flat.shape[0]
    padded = -(-n // 1024) * 1024
    return jnp.pad(flat, (0, padded - n)).reshape(padded // 128, 128)


def _unpack_rows(buf, shapes):
    flat = buf.reshape(-1)
    out, off = [], 0
    for s in shapes:
        n = 1
        for dim in s:
            n *= dim
        out.append(flat[off:off + n].reshape(s))
        off += n
    return out


def _layer_forward(l, x, prm, wg):
    S, D = x.shape
    p, h = _norm_matmul(f"in_proj_{l}", x, prm["attn_norm"][l], wg.get("w_in", l, x), F32)
    y_a = _sgu_fwd(f"sgu_fwd_{l}", p, prm["sgu_wt"][l], prm["sgu_bb"][l])
    y_b = _conv_fwd(f"conv_fwd_{l}", p, prm["conv_w"][l])
    q3, k3, v3 = _to_patterns(p, OFF_Q), _to_patterns(p, OFF_K), _to_patterns(p, OFF_V)
    o3d, lse3d = _attn_fwd(f"attn_fwd_{l}", q3, k3, v3, prm["q_gain"][l], prm["k_gain"][l], prm["bd"])
    o3, lse3 = _from_patterns(o3d), _from_patterns(lse3d)
    y_c = _mix_fwd(f"mix_fwd_{l}", o3, lse3)
    ycat = jnp.concatenate([y_a, y_b, y_c], axis=1)
    tm = min(512, S)
    w_out = wg.get("w_out", l, ycat)
    rq = w_out.shape[1]
    x1 = _matmul(
        f"out_proj_{l}", ycat, w_out, (S, D), F32, grid=(S // tm, 1, N_CHIPS),
        a_spec=pl.BlockSpec((tm, rq), lambda i, j, k: (i, k)),
        b_spec=pl.BlockSpec((None, rq, D), lambda i, j, k: (k, 0, 0)),
        o_spec=pl.BlockSpec((tm, D), lambda i, j, k: (i, 0)),
        contract=(1, 0), acc_shape=(tm, D),
        extras=(x,), extra_specs=(pl.BlockSpec((tm, D), lambda i, j, k: (i, 0)),),
        epi=lambda r, res: r + res)
    a, h2 = _norm_matmul(f"mlp_in_{l}", x1, prm["mlp_norm"][l], wg.get("w_mlp_in", l, x1), BF16)
    w_mlp_out = wg.get("w_mlp_out", l, a)
    dff4 = w_mlp_out.shape[1]
    tk = min(1024, dff4)
    kpc = dff4 // tk
    x2 = _matmul(
        f"mlp_out_{l}", a, w_mlp_out, (S, D), F32, grid=(S // tm, 1, N_CHIPS * kpc),
        a_spec=pl.BlockSpec((tm, tk), lambda i, j, k: (i, k)),
        b_spec=pl.BlockSpec((None, tk, D), lambda i, j, k: (k // kpc, k % kpc, 0)),
        o_spec=pl.BlockSpec((tm, D), lambda i, j, k: (i, 0)),
        contract=(1, 0), acc_shape=(tm, D), a_pre=_relu2_bf16,
        extras=(x1,), extra_specs=(pl.BlockSpec((tm, D), lambda i, j, k: (i, 0)),),
        epi=lambda r, res: r + res)
    saved = dict(x=x, p=p, h=h, q3=q3, k3=k3, v3=v3, o3=o3, lse3=lse3, lse3d=lse3d, ycat=ycat, x1=x1, a=a, h2=h2)
    return x2, saved


def _layer_backward(l, dx2, dx2b, sv, prm, wg):
    S, D = dx2.shape
    w_in, w_out = wg.get("w_in", l), wg.get("w_out", l)
    w_mlp_in, w_mlp_out = wg.get("w_mlp_in", l), wg.get("w_mlp_out", l)
    dff4 = w_mlp_in.shape[-1]
    dff = N_CHIPS * dff4
    tm = min(512, S)
    tk = min(1024, S)
    nks = S // tk

    da = _matmul(
        f"mlp_out_bwd_{l}", dx2b, w_mlp_out, (S, dff), BF16, grid=(S // tm, N_CHIPS, 1),
        a_spec=pl.BlockSpec((tm, D), lambda i, j, k: (i, 0)),
        b_spec=pl.BlockSpec((None, dff4, D), lambda i, j, k: (j, 0, 0)),
        o_spec=pl.BlockSpec((tm, dff4), lambda i, j, k: (i, j)),
        contract=(1, 1), acc_shape=(tm, dff4),
        extras=(sv["a"],), extra_specs=(pl.BlockSpec((tm, dff4), lambda i, j, k: (i, j)),),
        epi=lambda r, act: r * (2.0 * jnp.maximum(act.astype(F32), 0.0)))
    tmw = min(1024, dff4)
    mpc = dff4 // tmw
    g_w2 = _matmul(
        f"mlp_out_dw_{l}", sv["a"], dx2b, (N_CHIPS, dff4, D), F32, grid=(N_CHIPS * mpc, 1, nks),
        a_spec=pl.BlockSpec((tk, tmw), lambda i, j, k: (k, i)),
        b_spec=pl.BlockSpec((tk, D), lambda i, j, k: (k, 0)),
        o_spec=pl.BlockSpec((None, tmw, D), lambda i, j, k: (i // mpc, i % mpc, 0)),
        contract=(0, 0), acc_shape=(tmw, D), a_pre=_relu2_bf16)
    dh2 = _matmul(
        f"mlp_in_bwd_{l}", da, w_mlp_in, (S, D), F32, grid=(S // tm, 1, N_CHIPS),
        a_spec=pl.BlockSpec((tm, dff4), lambda i, j, k: (i, k)),
        b_spec=pl.BlockSpec((None, D, dff4), lambda i, j, k: (k, 0, 0)),
        o_spec=pl.BlockSpec((tm, D), lambda i, j, k: (i, 0)),
        contract=(1, 1), acc_shape=(tm, D))
    tmd = min(1024, D)
    g_w1 = _matmul(
        f"mlp_in_dw_{l}", sv["h2"], da, (N_CHIPS, D, dff4), F32, grid=(N_CHIPS, D // tmd, nks),
        a_spec=pl.BlockSpec((tk, tmd), lambda i, j, k: (k, j)),
        b_spec=pl.BlockSpec((tk, dff4), lambda i, j, k: (k, i)),
        o_spec=pl.BlockSpec((None, tmd, dff4), lambda i, j, k: (i, j, 0)),
        contract=(0, 0), acc_shape=(tmd, dff4))
    dx1, dx1b, g_mlp_norm = _rmsnorm_bwd(f"mlp_norm_bwd_{l}", dh2, sv["x1"], prm["mlp_norm"][l], dx2)

    rq = w_out.shape[1]
    dycat = _matmul(
        f"out_proj_bwd_{l}", dx1b, w_out, (S, N_CHIPS * rq), F32, grid=(S // tm, N_CHIPS, 1),
        a_spec=pl.BlockSpec((tm, D), lambda i, j, k: (i, 0)),
        b_spec=pl.BlockSpec((None, rq, D), lambda i, j, k: (j, 0, 0)),
        o_spec=pl.BlockSpec((tm, rq), lambda i, j, k: (i, j)),
        contract=(1, 1), acc_shape=(tm, rq))
    g_wout = _matmul(
        f"out_proj_dw_{l}", sv["ycat"], dx1b, (N_CHIPS, rq, D), F32, grid=(N_CHIPS, 1, nks),
        a_spec=pl.BlockSpec((tk, rq), lambda i, j, k: (k, i)),
        b_spec=pl.BlockSpec((tk, D), lambda i, j, k: (k, 0)),
        o_spec=pl.BlockSpec((None, rq, D), lambda i, j, k: (i, 0, 0)),
        contract=(0, 0), acc_shape=(rq, D))

    p = sv["p"]
    du, dv_a, g_sgu_w, db_lanes = _sgu_bwd(f"sgu_bwd_{l}", p, dycat, prm["sgu_wt"][l], prm["sgu_wtt"][l],
                                           prm["sgu_bb"][l])
    g_sgu_b = db_lanes[:, :A_HEADS].T
    db, dc, dxb, g_conv = _conv_bwd(f"conv_bwd_{l}", p, dycat, prm["conv_w"][l])
    do3, c3 = _mix_bwd(f"mix_bwd_{l}", sv["o3"], sv["lse3"], dycat, prm["bd"])
    do3d = jnp.stack([_deinterleave(do3[g], PATTERN_DILATION[g]) for g in range(N_PATTERNS)])
    c3d = jnp.stack([_deinterleave(c3[g], PATTERN_DILATION[g]) for g in range(N_PATTERNS)])
    dq3, dk3, dv3, dgq, dgk = _attn_bwd(f"attn_bwd_{l}", sv["q3"], sv["k3"], sv["v3"], sv["lse3d"], do3d, c3d,
                                        prm["q_gain"][l], prm["k_gain"][l], prm["bd"])
    g_q = dgq.reshape(N_PATTERNS * PW // HEAD_DIM, HEAD_DIM).sum(axis=0)
    g_k = dgk.reshape(N_PATTERNS * PW // HEAD_DIM, HEAD_DIM).sum(axis=0)
    nat = lambda t3: jnp.concatenate([_interleave(t3[g], PATTERN_DILATION[g]) for g in range(N_PATTERNS)], axis=1)
    dp = jnp.concatenate([du, dv_a, db, dc, dxb, nat(dq3), nat(dk3), nat(dv3)], axis=1)

    ns_in = w_in.shape[-1]
    dh = _matmul(
        f"in_proj_bwd_{l}", dp, w_in, (S, D), F32, grid=(S // tm, 1, N_CHIPS),
        a_spec=pl.BlockSpec((tm, ns_in), lambda i, j, k: (i, k)),
        b_spec=pl.BlockSpec((None, D, ns_in), lambda i, j, k: (k, 0, 0)),
        o_spec=pl.BlockSpec((tm, D), lambda i, j, k: (i, 0)),
        contract=(1, 1), acc_shape=(tm, D))
    g_win = _matmul(
        f"in_proj_dw_{l}", sv["h"], dp, (N_CHIPS, D, ns_in), F32, grid=(N_CHIPS, D // tmd, nks),
        a_spec=pl.BlockSpec((tk, tmd), lambda i, j, k: (k, j)),
        b_spec=pl.BlockSpec((tk, ns_in), lambda i, j, k: (k, i)),
        o_spec=pl.BlockSpec((None, tmd, ns_in), lambda i, j, k: (i, j, 0)),
        contract=(0, 0), acc_shape=(tmd, ns_in))
    dx0, dx0b, g_attn_norm = _rmsnorm_bwd(f"attn_norm_bwd_{l}", dh, sv["x"], prm["attn_norm"][l], dx1)

    big = dict(w_in=g_win, w_out=g_wout, w_mlp_in=g_w1, w_mlp_out=g_w2)
    small = dict(attn_norm=g_attn_norm.reshape(-1), sgu_w=g_sgu_w, sgu_b=g_sgu_b, conv_w=g_conv,
                 q_norm=g_q, k_norm=g_k, mlp_norm=g_mlp_norm.reshape(-1))
    return dx0, dx0b, big, small


BIG = ("w_in", "w_out", "w_mlp_in", "w_mlp_out")
SMALL_REPLICATED = ("attn_norm", "sgu_w", "sgu_b", "q_norm", "k_norm", "mlp_norm")


def _local_step(x, target, prm, wg, n_layers):
    saved = []
    h = x
    for l in range(n_layers):
        h, sv = _layer_forward(l, h, prm, wg)
        saved.append(sv)
    dy, dyb, colsq = _loss_kernel(h, target)
    loss = 0.5 * jnp.sum(colsq) / x.shape[1]
    bigs, smalls = [None] * n_layers, [None] * n_layers
    for l in reversed(range(n_layers)):
        dy, dyb, bigs[l], smalls[l] = _layer_backward(l, dy, dyb, saved[l], prm, wg)
    return loss, dy, bigs, smalls


def _prepare_params(attn_norm, sgu_w, sgu_b, conv_full, q_norm, k_norm, mlp_norm):
    n_layers = attn_norm.shape[0]
    tri = jnp.tril(sgu_w)
    idx = jnp.arange(PW)
    bd = (idx[:, None] // HEAD_DIM == idx[None, :] // HEAD_DIM).astype(BF16)
    return dict(
        attn_norm=[attn_norm[l][None, :] for l in range(n_layers)],
        mlp_norm=[mlp_norm[l][None, :] for l in range(n_layers)],
        sgu_wt=[tri[l].astype(BF16) for l in range(n_layers)],
        sgu_wtt=[tri[l].transpose(0, 2, 1).astype(BF16) for l in range(n_layers)],
        sgu_bb=[jnp.repeat(sgu_b[l].T, HEAD_DIM, axis=1) for l in range(n_layers)],
        conv_w=[conv_full[l] for l in range(n_layers)],
        q_gain=[jnp.tile(q_norm[l], PW // HEAD_DIM)[None, :] for l in range(n_layers)],
        k_gain=[jnp.tile(k_norm[l], PW // HEAD_DIM)[None, :] for l in range(n_layers)],
        bd=bd,
    )


def kernel(x, attn_norm, w_in, sgu_w, sgu_b, conv_w, q_norm, k_norm, w_out, mlp_norm, w_mlp_in, w_mlp_out, loss_target, m_attn_norm, m_w_in, m_sgu_w, m_sgu_b, m_conv_w, m_q_norm, m_k_norm, m_w_out, m_mlp_norm, m_w_mlp_in, m_w_mlp_out, v_attn_norm, v_w_in, v_sgu_w, v_sgu_b, v_conv_w, v_q_norm, v_k_norm, v_w_out, v_mlp_norm, v_w_mlp_in, v_w_mlp_out):
    n_layers = attn_norm.shape[0]
    weights = dict(attn_norm=attn_norm, w_in=w_in, sgu_w=sgu_w, sgu_b=sgu_b, conv_w=conv_w, q_norm=q_norm,
                   k_norm=k_norm, w_out=w_out, mlp_norm=mlp_norm, w_mlp_in=w_mlp_in, w_mlp_out=w_mlp_out)
    mom_m = dict(attn_norm=m_attn_norm, w_in=m_w_in, sgu_w=m_sgu_w, sgu_b=m_sgu_b, conv_w=m_conv_w,
                 q_norm=m_q_norm, k_norm=m_k_norm, w_out=m_w_out, mlp_norm=m_mlp_norm, w_mlp_in=m_w_mlp_in,
                 w_mlp_out=m_w_mlp_out)
    mom_v = dict(attn_norm=v_attn_norm, w_in=v_w_in, sgu_w=v_sgu_w, sgu_b=v_sgu_b, conv_w=v_conv_w,
                 q_norm=v_q_norm, k_norm=v_k_norm, w_out=v_w_out, mlp_norm=v_mlp_norm, w_mlp_in=v_w_mlp_in,
                 w_mlp_out=v_w_mlp_out)
    order = ("attn_norm", "w_in", "sgu_w", "sgu_b", "conv_w", "q_norm", "k_norm", "w_out", "mlp_norm",
             "w_mlp_in", "w_mlp_out")
    chip = 2 * lax.axis_index("x") + lax.axis_index("y")
    c_arr = lax.axis_index("c").astype(jnp.int32).reshape(1)

    conv_cols = conv_w.shape[-1]
    chip_arr = chip.astype(jnp.int32).reshape(1)
    conv_pack = jnp.pad(conv_w.reshape(-1), (0, 2048 - conv_w.size)).reshape(1, 16, 128)
    keys = [("conv_w", 0)]
    placed = [_place_shard("place_conv_w", conv_pack, 0, chip_arr, F32)]
    for l in range(n_layers):
        for n in BIG:
            keys.append((n, l))
            placed.append(_place_shard(f"place_{n}_{l}", weights[n], l, chip_arr, BF16))
    wg = _GatheredWeights(keys, placed)
    conv_full = wg.get("conv_w", 0).reshape(N_CHIPS, 2048)[:, :conv_w.size].reshape(N_CHIPS, n_layers, 3, conv_cols)
    conv_full = conv_full.transpose(1, 2, 0, 3).reshape(n_layers, 3, N_CHIPS * conv_cols)
    prm = _prepare_params(attn_norm, sgu_w, sgu_b, conv_full, q_norm, k_norm, mlp_norm)

    loss_local, grad_x, bigs, smalls = _local_step(x[0], loss_target[0], prm, wg, n_layers)
    loss = lax.psum(loss_local, ("x", "y", "c"))

    flat = [bigs[l][n] for n in BIG for l in range(n_layers)]
    theirs = _swap_halves(flat)
    added = [_add_my_half(f"rs_add_{i}", g, r, c_arr) for i, (g, r) in enumerate(zip(flat, theirs))]
    landed = _chip_exchange([a[0] for a in added], [a[1] for a in added])
    reduced = []
    for ti in range(len(BIG)):
        buf = None
        for l in range(n_layers):
            i = ti * n_layers + l
            buf = _sum_chips(f"rs_sum_{i}", landed[i], c_arr, l, n_layers, buf)
        reduced.append(buf)
    joined = dict(zip(BIG, _join_halves(reduced)))

    small_names = SMALL_REPLICATED + ("conv_w",)
    small_shapes = [(n_layers,) + tuple(smalls[0][n].shape) for n in small_names]
    packed = _pack_rows([jnp.stack([smalls[l][n] for l in range(n_layers)]) for n in small_names])
    summed = _unpack_rows(_allreduce_small(packed), small_shapes)
    grads = dict(zip(small_names, summed))
    grads["conv_w"] = lax.dynamic_slice_in_dim(grads["conv_w"], chip * conv_cols, conv_cols, axis=2)
    for n in BIG:
        grads[n] = joined[n].reshape(weights[n].shape)

    delta, new_m, new_v = {}, {}, {}
    for n in BIG:
        shp = weights[n].shape
        two_d = (shp[0] * shp[1], shp[2])
        d, nm, nv = _adamw(f"adamw_{n}", weights[n].reshape(two_d), grads[n].reshape(two_d),
                           mom_m[n].reshape(two_d), mom_v[n].reshape(two_d))
        delta[n], new_m[n], new_v[n] = d.reshape(shp), nm.reshape(shp), nv.reshape(shp)
    smalls_all = SMALL_REPLICATED + ("conv_w",)
    shapes = [weights[n].shape for n in smalls_all]
    d, nm, nv = _adamw("adamw_small",
                       _pack_rows([weights[n] for n in smalls_all]), _pack_rows([grads[n] for n in smalls_all]),
                       _pack_rows([mom_m[n] for n in smalls_all]), _pack_rows([mom_v[n] for n in smalls_all]))
    for n, dd, mm, vv in zip(smalls_all, _unpack_rows(d, shapes), _unpack_rows(nm, shapes), _unpack_rows(nv, shapes)):
        delta[n], new_m[n], new_v[n] = dd, mm, vv

    return (loss, grad_x[None], *[grads[n] for n in order], *[delta[n] for n in order],
            *[new_m[n] for n in order], *[new_v[n] for n in order])
```

```python
import jax
import jax.numpy as jnp
from jax import lax
from jax.experimental import pallas as pl
from jax.experimental.pallas import tpu as pltpu

F32 = jnp.float32
BF16 = jnp.bfloat16
SDS = jax.ShapeDtypeStruct

EPS = 1e-6
HEAD_DIM = 64
A_HEADS = 8
A_WIDTH = 512
CHUNK = 128
B_WIDTH = 768
C_WIDTH = 768
N_PATTERNS = 3
PATTERN_DILATION = (1, 4, 16)
PW = 256
D_IN_PROJ = 5632
OFF_AU, OFF_AV, OFF_BB, OFF_BC, OFF_BX, OFF_Q, OFF_K, OFF_V = 0, 512, 1024, 1792, 2560, 3328, 4096, 4864
N_CHIPS = 4
N_DEV = 8
BLK = 128

ADAM_LR, ADAM_B1, ADAM_B2, ADAM_EPS, ADAM_WD, ADAM_STEP = 0.001, 0.9, 0.999, 1e-08, 0.01, 10

V7X_VMEM_LIMIT = 56 * 1024 * 1024
MESH = pl.DeviceIdType.MESH
NEG = -1e30


def _cp(n_axes):
    return pltpu.CompilerParams(dimension_semantics=("arbitrary",) * n_axes, vmem_limit_bytes=V7X_VMEM_LIMIT)


def _hbm_spec():
    return pl.BlockSpec(memory_space=pl.ANY)


def _norm_matmul(name, x, g, wg, out_dtype):
    S, D = x.shape
    ns, _, Ns = wg.shape
    tm = min(512, S)

    def body(x_ref, g_ref, w_ref, o_ref, h_ref, hs_ref):
        @pl.when(pl.program_id(1) == 0)
        def _():
            xv = x_ref[...]
            y = xv * lax.rsqrt(jnp.mean(xv * xv, axis=-1, keepdims=True) + EPS) * g_ref[...]
            hb = y.astype(BF16)
            hs_ref[...] = hb
            h_ref[...] = hb
        o_ref[...] = jnp.dot(hs_ref[...], w_ref[...], preferred_element_type=F32).astype(o_ref.dtype)

    return pl.pallas_call(
        body, name=name, grid=(S // tm, ns),
        in_specs=[pl.BlockSpec((tm, D), lambda i, s: (i, 0)),
                  pl.BlockSpec((1, D), lambda i, s: (0, 0)),
                  pl.BlockSpec((None, D, Ns), lambda i, s: (s, 0, 0))],
        out_specs=[pl.BlockSpec((tm, Ns), lambda i, s: (i, s)),
                   pl.BlockSpec((tm, D), lambda i, s: (i, 0))],
        out_shape=[SDS((S, ns * Ns), out_dtype), SDS((S, D), BF16)],
        scratch_shapes=[pltpu.VMEM((tm, D), BF16)],
        compiler_params=_cp(2),
    )(x, g, wg)


def _matmul(name, a, b, out_shape, out_dtype, *, grid, a_spec, b_spec, o_spec, contract, acc_shape,
            extras=(), extra_specs=(), a_pre=None, epi=None):
    nk = grid[2]
    n_ex = len(extras)
    dims = (((contract[0],), (contract[1],)), ((), ()))

    def body(a_ref, b_ref, *rest):
        ex = rest[:n_ex]
        o_ref = rest[n_ex]
        acc_ref = rest[n_ex + 1]
        k = pl.program_id(2)

        @pl.when(k == 0)
        def _():
            acc_ref[...] = jnp.zeros_like(acc_ref)

        av = a_ref[...]
        if a_pre is not None:
            av = a_pre(av)
        acc_ref[...] += lax.dot_general(av, b_ref[...], dims, preferred_element_type=F32)

        @pl.when(k == nk - 1)
        def _():
            r = acc_ref[...]
            if epi is not None:
                r = epi(r, *[e[...] for e in ex])
            o_ref[...] = r.astype(o_ref.dtype)

    return pl.pallas_call(
        body, name=name, grid=grid,
        in_specs=[a_spec, b_spec, *extra_specs],
        out_specs=o_spec,
        out_shape=SDS(out_shape, out_dtype),
        scratch_shapes=[pltpu.VMEM(acc_shape, F32)],
        compiler_params=_cp(3),
    )(a, b, *extras)


def _relu2_bf16(t):
    r = jnp.maximum(t.astype(F32), 0.0)
    return (r * r).astype(BF16)


def _loss_kernel(y, t):
    S, D = y.shape
    tm = min(256, S)

    def body(y_ref, t_ref, dy_ref, dyb_ref, l_ref):
        @pl.when(pl.program_id(0) == 0)
        def _():
            l_ref[...] = jnp.zeros_like(l_ref)
        e = y_ref[...] - t_ref[...]
        l_ref[...] += jnp.sum(e * e, axis=0, keepdims=True)
        dy = e * (1.0 / D)
        dy_ref[...] = dy
        dyb_ref[...] = dy.astype(BF16)

    row = pl.BlockSpec((tm, D), lambda i: (i, 0))
    return pl.pallas_call(
        body, name="loss_head", grid=(S // tm,),
        in_specs=[row, row],
        out_specs=[row, row, pl.BlockSpec((1, D), lambda i: (0, 0))],
        out_shape=[SDS((S, D), F32), SDS((S, D), BF16), SDS((1, D), F32)],
        compiler_params=_cp(1),
    )(y, t)


def _rmsnorm_bwd(name, dh, x, g, dres):
    S, D = x.shape
    tm = min(256, S)

    def body(dh_ref, x_ref, g_ref, dres_ref, dx_ref, dxb_ref, dg_ref):
        @pl.when(pl.program_id(0) == 0)
        def _():
            dg_ref[...] = jnp.zeros_like(dg_ref)
        xv = x_ref[...]
        dhv = dh_ref[...]
        rstd = lax.rsqrt(jnp.mean(xv * xv, axis=-1, keepdims=True) + EPS)
        xhat = xv * rstd
        dg_ref[...] += jnp.sum(dhv * xhat, axis=0, keepdims=True)
        dxn = dhv * g_ref[...]
        dx = dres_ref[...] + rstd * (dxn - xhat * jnp.mean(dxn * xhat, axis=-1, keepdims=True))
        dx_ref[...] = dx
        dxb_ref[...] = dx.astype(BF16)

    row = pl.BlockSpec((tm, D), lambda i: (i, 0))
    vec = pl.BlockSpec((1, D), lambda i: (0, 0))
    return pl.pallas_call(
        body, name=name, grid=(S // tm,),
        in_specs=[row, row, vec, row],
        out_specs=[row, row, vec],
        out_shape=[SDS((S, D), F32), SDS((S, D), BF16), SDS((1, D), F32)],
        compiler_params=_cp(1),
    )(dh, x, g, dres)


def _adamw(name, w, g, m, v):
    R, C = w.shape
    tr = 256 if R % 256 == 0 else R
    c1 = 1.0 - ADAM_B1 ** ADAM_STEP
    c2 = 1.0 - ADAM_B2 ** ADAM_STEP

    def body(w_ref, g_ref, m_ref, v_ref, d_ref, nm_ref, nv_ref):
        gv = g_ref[...]
        nm = ADAM_B1 * m_ref[...] + (1.0 - ADAM_B1) * gv
        nv = ADAM_B2 * v_ref[...] + (1.0 - ADAM_B2) * (gv * gv)
        m_hat = nm / c1
        v_hat = nv / c2
        d_ref[...] = -ADAM_LR * (m_hat / (jnp.sqrt(v_hat) + ADAM_EPS) + ADAM_WD * w_ref[...])
        nm_ref[...] = nm
        nv_ref[...] = nv

    blk = pl.BlockSpec((tr, C), lambda i: (i, 0))
    return pl.pallas_call(
        body, name=name, grid=(R // tr,),
        in_specs=[blk] * 4, out_specs=[blk] * 3,
        out_shape=[SDS((R, C), F32)] * 3,
        compiler_params=_cp(1),
    )(w, g, m, v)


def _pair_select(lane, lo, hi):
    return jnp.where(lane < HEAD_DIM, lo, hi)


def _sgu_fwd(name, p, wt, bb):
    S = p.shape[0]

    def body(u_ref, v_ref, wt_ref, bb_ref, o_ref):
        lane = lax.broadcasted_iota(jnp.int32, (CHUNK, 128), 1)
        for pp in range(A_HEADS // 2):
            cs = slice(128 * pp, 128 * (pp + 1))
            vb = v_ref[:, cs].astype(BF16)
            mixed = _pair_select(lane,
                                 jnp.dot(wt_ref[2 * pp], vb, preferred_element_type=F32),
                                 jnp.dot(wt_ref[2 * pp + 1], vb, preferred_element_type=F32)) + bb_ref[:, cs]
            o_ref[:, cs] = (u_ref[:, cs] * mixed).astype(o_ref.dtype)

    return pl.pallas_call(
        body, name=name, grid=(S // CHUNK,),
        in_specs=[pl.BlockSpec((CHUNK, A_WIDTH), lambda c: (c, OFF_AU // A_WIDTH)),
                  pl.BlockSpec((CHUNK, A_WIDTH), lambda c: (c, OFF_AV // A_WIDTH)),
                  pl.BlockSpec((A_HEADS, CHUNK, CHUNK), lambda c: (0, 0, 0)),
                  pl.BlockSpec((CHUNK, A_WIDTH), lambda c: (0, 0))],
        out_specs=pl.BlockSpec((CHUNK, A_WIDTH), lambda c: (c, 0)),
        out_shape=SDS((S, A_WIDTH), BF16),
        compiler_params=_cp(1),
    )(p, p, wt, bb)


def _sgu_bwd(name, p, dycat, wt, wtt, bb):
    S = p.shape[0]

    def body(u_ref, v_ref, dy_ref, wt_ref, wtt_ref, bb_ref, du_ref, dv_ref, dw_ref, db_ref, dbacc_ref):
        c = pl.program_id(0)

        @pl.when(c == 0)
        def _():
            dw_ref[...] = jnp.zeros_like(dw_ref)
            dbacc_ref[...] = jnp.zeros_like(dbacc_ref)

        lane = lax.broadcasted_iota(jnp.int32, (CHUNK, 128), 1)
        row = lax.broadcasted_iota(jnp.int32, (CHUNK, 128), 0)
        causal = row >= lane
        for pp in range(A_HEADS // 2):
            cs = slice(128 * pp, 128 * (pp + 1))
            v = v_ref[:, cs]
            vb = v.astype(BF16)
            u = u_ref[:, cs]
            dy = dy_ref[:, cs]
            mixed = _pair_select(lane,
                                 jnp.dot(wt_ref[2 * pp], vb, preferred_element_type=F32),
                                 jnp.dot(wt_ref[2 * pp + 1], vb, preferred_element_type=F32)) + bb_ref[:, cs]
            du_ref[:, cs] = (dy * mixed).astype(du_ref.dtype)
            dm = dy * u
            dmb = dm.astype(BF16)
            dv = _pair_select(lane,
                              jnp.dot(wtt_ref[2 * pp], dmb, preferred_element_type=F32),
                              jnp.dot(wtt_ref[2 * pp + 1], dmb, preferred_element_type=F32))
            dv_ref[:, cs] = dv.astype(dv_ref.dtype)
            dbacc_ref[:, cs] += dm
            nt = (((1,), (1,)), ((), ()))
            dm_lo = jnp.where(lane < HEAD_DIM, dm, 0.0).astype(BF16)
            dm_hi = jnp.where(lane >= HEAD_DIM, dm, 0.0).astype(BF16)
            dw_ref[2 * pp] += jnp.where(causal, lax.dot_general(dm_lo, vb, nt, preferred_element_type=F32), 0.0)
            dw_ref[2 * pp + 1] += jnp.where(causal, lax.dot_general(dm_hi, vb, nt, preferred_element_type=F32), 0.0)

        @pl.when(c == S // CHUNK - 1)
        def _():
            out = jnp.zeros((CHUNK, 128), F32)
            for pp in range(A_HEADS // 2):
                acc = dbacc_ref[:, 128 * pp:128 * (pp + 1)]
                s_lo = jnp.sum(jnp.where(lane < HEAD_DIM, acc, 0.0), axis=1, keepdims=True)
                s_hi = jnp.sum(jnp.where(lane >= HEAD_DIM, acc, 0.0), axis=1, keepdims=True)
                out = jnp.where(lane == 2 * pp, s_lo, out)
                out = jnp.where(lane == 2 * pp + 1, s_hi, out)
            db_ref[...] = out

    chunk = lambda col: pl.BlockSpec((CHUNK, A_WIDTH), lambda c: (c, col))
    wspec = pl.BlockSpec((A_HEADS, CHUNK, CHUNK), lambda c: (0, 0, 0))
    return pl.pallas_call(
        body, name=name, grid=(S // CHUNK,),
        in_specs=[chunk(OFF_AU // A_WIDTH), chunk(OFF_AV // A_WIDTH), chunk(0), wspec, wspec,
                  pl.BlockSpec((CHUNK, A_WIDTH), lambda c: (0, 0))],
        out_specs=[chunk(0), chunk(0), wspec, pl.BlockSpec((CHUNK, 128), lambda c: (0, 0))],
        out_shape=[SDS((S, A_WIDTH), BF16), SDS((S, A_WIDTH), BF16),
                   SDS((A_HEADS, CHUNK, CHUNK), F32), SDS((CHUNK, 128), F32)],
        scratch_shapes=[pltpu.VMEM((CHUNK, A_WIDTH), F32)],
        compiler_params=_cp(1),
    )(p, p, dycat, wt, wtt, bb)


CONV_HALO = 8


def _shift_down(a, halo, k):
    T = a.shape[0]
    row = lax.broadcasted_iota(jnp.int32, a.shape, 0)
    out = pltpu.roll(a, k, 0)
    for r in range(k):
        out = jnp.where(row == r, halo[CONV_HALO - k + r:CONV_HALO - k + r + 1, :], out)
    return out


def _shift_up(a, halo, k):
    T = a.shape[0]
    row = lax.broadcasted_iota(jnp.int32, a.shape, 0)
    out = pltpu.roll(a, T - k, 0)
    for r in range(k):
        out = jnp.where(row == T - k + r, halo[r:r + 1, :], out)
    return out


def _conv_specs(S, T):
    hb = T // CONV_HALO
    last = S // CONV_HALO - 1
    tile = lambda col0: pl.BlockSpec((T, 128), lambda j, i: (i, col0 + j))
    prev = lambda col0: pl.BlockSpec((CONV_HALO, 128), lambda j, i: (jnp.maximum(i * hb - 1, 0), col0 + j))
    nxt = lambda col0: pl.BlockSpec((CONV_HALO, 128), lambda j, i: (jnp.minimum((i + 1) * hb, last), col0 + j))
    return tile, prev, nxt


def _conv_fwd(name, p, w):
    S = p.shape[0]
    T = min(512, S)
    tile, prev, _ = _conv_specs(S, T)
    cb, cc, cx = OFF_BB // 128, OFF_BC // 128, OFF_BX // 128

    def body(b_ref, c_ref, x_ref, ch_ref, xh_ref, w_ref, o_ref):
        i = pl.program_id(1)
        z = c_ref[...] * x_ref[...]
        zh = jnp.where(i > 0, ch_ref[...] * xh_ref[...], 0.0)
        z1 = _shift_down(z, zh, 1)
        z2 = _shift_down(z, zh, 2)
        conv = w_ref[0:1, :] * z2 + w_ref[1:2, :] * z1 + w_ref[2:3, :] * z
        o_ref[...] = (b_ref[...] * conv).astype(o_ref.dtype)

    return pl.pallas_call(
        body, name=name, grid=(B_WIDTH // 128, S // T),
        in_specs=[tile(cb), tile(cc), tile(cx), prev(cc), prev(cx),
                  pl.BlockSpec((3, 128), lambda j, i: (0, j))],
        out_specs=tile(0),
        out_shape=SDS((S, B_WIDTH), BF16),
        compiler_params=_cp(2),
    )(p, p, p, p, p, w)


def _conv_bwd(name, p, dycat, w):
    S = p.shape[0]
    T = min(512, S)
    tile, prev, nxt = _conv_specs(S, T)
    cb, cc, cx = OFF_BB // 128, OFF_BC // 128, OFF_BX // 128
    cdy = A_WIDTH // 128
    n_i = S // T

    def body(b_ref, c_ref, x_ref, dy_ref, ch_ref, xh_ref, bn_ref, dyn_ref, w_ref,
             db_ref, dc_ref, dx_ref, dw_ref):
        i = pl.program_id(1)

        @pl.when(i == 0)
        def _():
            dw_ref[...] = jnp.zeros_like(dw_ref)

        cv = c_ref[...]
        xv = x_ref[...]
        z = cv * xv
        zh = jnp.where(i > 0, ch_ref[...] * xh_ref[...], 0.0)
        z1 = _shift_down(z, zh, 1)
        z2 = _shift_down(z, zh, 2)
        w0, w1, w2 = w_ref[0:1, :], w_ref[1:2, :], w_ref[2:3, :]
        conv = w0 * z2 + w1 * z1 + w2 * z
        dy = dy_ref[...]
        db_ref[...] = (dy * conv).astype(db_ref.dtype)
        dconv = dy * b_ref[...]
        dconv_n = jnp.where(i < n_i - 1, dyn_ref[...] * bn_ref[...], 0.0)
        dz = w2 * dconv + w1 * _shift_up(dconv, dconv_n, 1) + w0 * _shift_up(dconv, dconv_n, 2)
        dc_ref[...] = (dz * xv).astype(dc_ref.dtype)
        dx_ref[...] = (dz * cv).astype(dx_ref.dtype)
        dw_ref[0:1, :] += jnp.sum(dconv * z2, axis=0, keepdims=True)
        dw_ref[1:2, :] += jnp.sum(dconv * z1, axis=0, keepdims=True)
        dw_ref[2:3, :] += jnp.sum(dconv * z, axis=0, keepdims=True)

    wspec = pl.BlockSpec((3, 128), lambda j, i: (0, j))
    return pl.pallas_call(
        body, name=name, grid=(B_WIDTH // 128, n_i),
        in_specs=[tile(cb), tile(cc), tile(cx), tile(cdy), prev(cc), prev(cx), nxt(cb), nxt(cdy), wspec],
        out_specs=[tile(0), tile(0), tile(0), wspec],
        out_shape=[SDS((S, B_WIDTH), BF16)] * 3 + [SDS((3, B_WIDTH), F32)],
        compiler_params=_cp(2),
    )(p, p, p, dycat, p, p, p, dycat, w)


def _seg_sum(t, bd):
    hi = t.astype(BF16)
    lo = (t - hi.astype(F32)).astype(BF16)
    return jnp.dot(hi, bd, preferred_element_type=F32) + jnp.dot(lo, bd, preferred_element_type=F32)


def _head_norm(x, g, bd):
    rstd = lax.rsqrt(_seg_sum(x * x, bd) * (1.0 / HEAD_DIM) + EPS)
    xhat = x * rstd
    return xhat * g, xhat, rstd


def _head_norm_bwd(dy, g, xhat, rstd, bd):
    dxh = dy * g
    return rstd * (dxh - xhat * (_seg_sum(dxh * xhat, bd) * (1.0 / HEAD_DIM)))


def _band_mask(has_prev):
    row = lax.broadcasted_iota(jnp.int32, (BLK, 2 * BLK), 0)
    col = lax.broadcasted_iota(jnp.int32, (BLK, 2 * BLK), 1)
    first_key = jnp.where(has_prev, 0, BLK)
    return (col >= row) & (col <= row + BLK) & (col >= first_key)


def _first_of_segment(g, n, n_blocks):
    per_seg = lax.shift_right_logical(jnp.int32(n_blocks), 2 * g)
    return (n & (per_seg - 1)) == 0


def _attn_fwd(name, q3, k3, v3, gq, gk, bd):
    _, S, _ = q3.shape
    nblk = S // BLK
    nt = (((1,), (1,)), ((), ()))

    def body(q_ref, kc_ref, kp_ref, vc_ref, vp_ref, gq_ref, gk_ref, bd_ref, o_ref, lse_ref):
        g = pl.program_id(0)
        n = pl.program_id(1)
        has_prev = jnp.logical_not(_first_of_segment(g, n, nblk))
        bdv = bd_ref[...]
        qn, _, _ = _head_norm(q_ref[...], gq_ref[...], bdv)
        kn, _, _ = _head_norm(jnp.concatenate([kp_ref[...], kc_ref[...]], axis=0), gk_ref[...], bdv)
        knb = kn.astype(BF16)
        vb = jnp.concatenate([vp_ref[...], vc_ref[...]], axis=0).astype(BF16)
        band = _band_mask(has_prev)
        lane = lax.broadcasted_iota(jnp.int32, (1, PW), 1)
        o_acc = jnp.zeros((BLK, PW), F32)
        l_acc = jnp.zeros((BLK, PW), F32)
        for j in range(PW // HEAD_DIM):
            hm = (lane >= HEAD_DIM * j) & (lane < HEAD_DIM * (j + 1))
            qj = jnp.where(hm, qn, 0.0).astype(BF16)
            s = lax.dot_general(qj, knb, nt, preferred_element_type=F32) * (HEAD_DIM ** -0.5)
            s = jnp.where(band, s, NEG)
            m = jnp.max(s, axis=1, keepdims=True)
            e = jnp.exp(s - m)
            den = jnp.sum(e, axis=1, keepdims=True)
            pv = jnp.dot(e.astype(BF16), vb, preferred_element_type=F32)
            o_acc = jnp.where(hm, pv / den, o_acc)
            l_acc = jnp.where(hm, m + jnp.log(den), l_acc)
        o_ref[...] = o_acc
        lse_ref[...] = l_acc

    cur = pl.BlockSpec((None, BLK, PW), lambda g, n: (g, n, 0))
    prv = pl.BlockSpec((None, BLK, PW), lambda g, n: (g, jnp.maximum(n - 1, 0), 0))
    vec = pl.BlockSpec((1, PW), lambda g, n: (0, 0))
    return pl.pallas_call(
        body, name=name, grid=(N_PATTERNS, nblk),
        in_specs=[cur, cur, prv, cur, prv, vec, vec, pl.BlockSpec((PW, PW), lambda g, n: (0, 0))],
        out_specs=[cur, cur],
        out_shape=[SDS((N_PATTERNS, S, PW), F32)] * 2,
        compiler_params=_cp(2),
    )(q3, k3, k3, v3, v3, gq, gk, bd)


def _attn_bwd(name, q3, k3, v3, lse3, do3, c3, gq, gk, bd):
    _, S, _ = q3.shape
    nblk = S // BLK
    nt = (((1,), (1,)), ((), ()))
    tn = (((0,), (0,)), ((), ()))

    def body(q_ref, kc_ref, kp_ref, vc_ref, vp_ref, lse_ref, do_ref, c_ref, gq_ref, gk_ref, bd_ref,
             dq_ref, dk_ref, dv_ref, dgq_ref, dgk_ref, ck_ref, cv_ref):
        g = pl.program_id(0)
        n = pl.program_id(1)
        live = n < nblk
        ne = jnp.minimum(n, nblk - 1)
        has_prev = jnp.logical_not(_first_of_segment(g, ne, nblk))

        @pl.when(n == 0)
        def _():
            ck_ref[...] = jnp.zeros_like(ck_ref)
            cv_ref[...] = jnp.zeros_like(cv_ref)
            dgq_ref[...] = jnp.zeros_like(dgq_ref)
            dgk_ref[...] = jnp.zeros_like(dgk_ref)

        bdv = bd_ref[...]
        gqv = gq_ref[...]
        gkv = gk_ref[...]
        qn, qhat, qrstd = _head_norm(q_ref[...], gqv, bdv)
        kn, khat, krstd = _head_norm(jnp.concatenate([kp_ref[...], kc_ref[...]], axis=0), gkv, bdv)
        knb = kn.astype(BF16)
        vb = jnp.concatenate([vp_ref[...], vc_ref[...]], axis=0).astype(BF16)
        band = _band_mask(has_prev)
        lane = lax.broadcasted_iota(jnp.int32, (1, PW), 1)
        lse = lse_ref[...]
        do = do_ref[...]
        cc = c_ref[...]
        dqn = jnp.zeros((BLK, PW), F32)
        dkn = jnp.zeros((2 * BLK, PW), F32)
        dvv = jnp.zeros((2 * BLK, PW), F32)
        for j in range(PW // HEAD_DIM):
            hm = (lane >= HEAD_DIM * j) & (lane < HEAD_DIM * (j + 1))
            qj = jnp.where(hm, qn, 0.0).astype(BF16)
            doj = jnp.where(hm, do, 0.0).astype(BF16)
            s = lax.dot_general(qj, knb, nt, preferred_element_type=F32) * (HEAD_DIM ** -0.5)
            lse_j = jnp.max(jnp.where(hm, lse, NEG), axis=1, keepdims=True)
            c_j = jnp.max(jnp.where(hm, cc, NEG), axis=1, keepdims=True)
            prob = jnp.where(band, jnp.exp(s - lse_j), 0.0)
            dp = lax.dot_general(doj, vb, nt, preferred_element_type=F32)
            ds = (prob * (dp + c_j) * (HEAD_DIM ** -0.5)).astype(BF16)
            dqn = jnp.where(hm, jnp.dot(ds, knb, preferred_element_type=F32), dqn)
            dkn += lax.dot_general(ds, qj, tn, preferred_element_type=F32)
            dvv += lax.dot_general(prob.astype(BF16), doj, tn, preferred_element_type=F32)

        dq_ref[...] = _head_norm_bwd(dqn, gqv, qhat, qrstd, bdv).astype(dq_ref.dtype)
        dk2 = _head_norm_bwd(dkn, gkv, khat, krstd, bdv)
        keep = jnp.where(live, 1.0, 0.0)
        dgq_ref[...] += keep * jnp.sum(dqn * qhat, axis=0, keepdims=True)
        dgk_ref[...] += keep * jnp.sum(dkn * khat, axis=0, keepdims=True)
        dk_ref[...] = (ck_ref[...] + keep * dk2[:BLK]).astype(dk_ref.dtype)
        dv_ref[...] = (cv_ref[...] + keep * dvv[:BLK]).astype(dv_ref.dtype)
        ck_ref[...] = dk2[BLK:]
        cv_ref[...] = dvv[BLK:]

    last = nblk - 1
    cur = pl.BlockSpec((None, BLK, PW), lambda g, n: (g, jnp.minimum(n, last), 0))
    prv = pl.BlockSpec((None, BLK, PW), lambda g, n: (g, jnp.maximum(jnp.minimum(n, last) - 1, 0), 0))
    done = pl.BlockSpec((None, BLK, PW), lambda g, n: (g, jnp.maximum(n - 1, 0), 0))
    vec = pl.BlockSpec((1, PW), lambda g, n: (0, 0))
    gvec = pl.BlockSpec((None, 1, PW), lambda g, n: (g, 0, 0))
    return pl.pallas_call(
        body, name=name, grid=(N_PATTERNS, nblk + 1),
        in_specs=[cur, cur, prv, cur, prv, cur, cur, cur, vec, vec, pl.BlockSpec((PW, PW), lambda g, n: (0, 0))],
        out_specs=[cur, done, done, gvec, gvec],
        out_shape=[SDS((N_PATTERNS, S, PW), BF16)] * 3 + [SDS((N_PATTERNS, 1, PW), F32)] * 2,
        scratch_shapes=[pltpu.VMEM((BLK, PW), F32), pltpu.VMEM((BLK, PW), F32)],
        compiler_params=_cp(2),
    )(q3, k3, k3, v3, v3, lse3, do3, c3, gq, gk, bd)


def _mix_fwd(name, o3, lse3):
    _, S, _ = o3.shape
    tm = min(512, S)

    def body(o_ref, l_ref, y_ref):
        l = [l_ref[g] for g in range(N_PATTERNS)]
        m = jnp.maximum(jnp.maximum(l[0], l[1]), l[2])
        e = [jnp.exp(t - m) for t in l]
        inv = 1.0 / (e[0] + e[1] + e[2])
        for g in range(N_PATTERNS):
            y_ref[:, PW * g:PW * (g + 1)] = (o_ref[g] * (e[g] * inv)).astype(y_ref.dtype)

    blk3 = pl.BlockSpec((N_PATTERNS, tm, PW), lambda i: (0, i, 0))
    return pl.pallas_call(
        body, name=name, grid=(S // tm,),
        in_specs=[blk3, blk3],
        out_specs=pl.BlockSpec((tm, C_WIDTH), lambda i: (i, 0)),
        out_shape=SDS((S, C_WIDTH), BF16),
        compiler_params=_cp(1),
    )(o3, lse3)


def _mix_bwd(name, o3, lse3, dycat, bd):
    _, S, _ = o3.shape
    tm = min(512, S)
    c0 = (A_WIDTH + B_WIDTH) // PW

    def body(o_ref, l_ref, dy0_ref, dy1_ref, dy2_ref, bd_ref, do_ref, c_ref):
        bdv = bd_ref[...]
        dys = [dy0_ref[...], dy1_ref[...], dy2_ref[...]]
        l = [l_ref[g] for g in range(N_PATTERNS)]
        m = jnp.maximum(jnp.maximum(l[0], l[1]), l[2])
        e = [jnp.exp(t - m) for t in l]
        inv = 1.0 / (e[0] + e[1] + e[2])
        alpha = [t * inv for t in e]
        da = [_seg_sum(dys[g] * o_ref[g], bdv) for g in range(N_PATTERNS)]
        mean_da = alpha[0] * da[0] + alpha[1] * da[1] + alpha[2] * da[2]
        for g in range(N_PATTERNS):
            do_ref[g] = dys[g] * alpha[g]
            c_ref[g] = -alpha[g] * mean_da

    blk3 = pl.BlockSpec((N_PATTERNS, tm, PW), lambda i: (0, i, 0))
    dyspec = lambda g: pl.BlockSpec((tm, PW), lambda i: (i, c0 + g))
    return pl.pallas_call(
        body, name=name, grid=(S // tm,),
        in_specs=[blk3, blk3, dyspec(0), dyspec(1), dyspec(2), pl.BlockSpec((PW, PW), lambda i: (0, 0))],
        out_specs=[blk3, blk3],
        out_shape=[SDS((N_PATTERNS, S, PW), F32)] * 2,
        compiler_params=_cp(1),
    )(o3, lse3, dycat, dycat, dycat, bd)


def _mesh_pos():
    x, y, c = lax.axis_index("x"), lax.axis_index("y"), lax.axis_index("c")
    chips = [(1 - x, y), (x, 1 - y), (1 - x, 1 - y)]
    chip_idx = [2 * cx + cy for cx, cy in chips]
    return x, y, c, 2 * x + y, chips, chip_idx


def _place_shard(name, w, layer, chip_arr, out_dtype):
    _, R, C = w.shape
    tr = min(256, R)

    def body(chip_ref, w_ref, o_ref):
        o_ref[...] = w_ref[...].astype(o_ref.dtype)

    return pl.pallas_call(
        body, name=name,
        grid_spec=pltpu.PrefetchScalarGridSpec(
            num_scalar_prefetch=1, grid=(R // tr,),
            in_specs=[pl.BlockSpec((None, tr, C), lambda i, chip_ref: (layer, i, 0))],
            out_specs=pl.BlockSpec((None, tr, C), lambda i, chip_ref: (chip_ref[0], i, 0))),
        out_shape=SDS((N_CHIPS, R, C), out_dtype),
        compiler_params=_cp(1),
    )(chip_arr, w)


HBM_SPEC = pl.BlockSpec(memory_space=pltpu.HBM)
SEM_SPEC = pl.BlockSpec(memory_space=pltpu.SEMAPHORE)
SPLIT_COPY = pltpu.SideEffectType.DATAFLOW_SIDE_EFFECTING
N_PEER_CHIPS = N_CHIPS - 1


def _in_hbm(a):
    return pltpu.with_memory_space_constraint(a, pltpu.HBM)


def _gather_start(bufs):
    T = len(bufs)

    def body(*refs):
        ins = refs[:T]
        send_sems, recv_sems = refs[T:2 * T], refs[2 * T:3 * T]
        x, y, c, me, chips, chip_idx = _mesh_pos()
        for t in range(T):
            hr = ins[t].shape[1] // 2
            mine = ins[t].at[me, pl.ds(c * hr, hr), :]
            for j in range(N_PEER_CHIPS):
                pltpu.make_async_remote_copy(src_ref=mine, dst_ref=mine, send_sem=send_sems[t].at[j],
                                             recv_sem=recv_sems[t].at[j], device_id=(*chips[j], c),
                                             device_id_type=MESH).start()

    sems = [pltpu.SemaphoreType.DMA((N_PEER_CHIPS,))] * T
    out = pl.pallas_call(
        body, name="gather_start",
        in_specs=[HBM_SPEC] * T,
        out_specs=[SEM_SPEC] * (2 * T) + [HBM_SPEC] * T,
        out_shape=sems + sems + [pltpu.HBM(b.shape, b.dtype) for b in bufs],
        input_output_aliases={t: 2 * T + t for t in range(T)},
        compiler_params=pltpu.CompilerParams(has_side_effects=SPLIT_COPY),
    )(*[_in_hbm(b) for b in bufs])
    return out[:T], out[T:2 * T], out[2 * T:]


def _gather_wait(name, buf, send_sem, recv_sem, after):
    n_in = 3 if after is None else 4

    def body(*refs):
        buf_ref, ssem, rsem = refs[:3]
        x, y, c, me, chips, chip_idx = _mesh_pos()
        hr = buf_ref.shape[1] // 2
        mine = buf_ref.at[me, pl.ds(c * hr, hr), :]
        for j in range(N_PEER_CHIPS):
            got = buf_ref.at[chip_idx[j], pl.ds(c * hr, hr), :]
            cp = pltpu.make_async_remote_copy(src_ref=mine, dst_ref=got, send_sem=ssem.at[j], recv_sem=rsem.at[j],
                                              device_id=(*chips[j], c), device_id_type=MESH)
            cp.wait_send()
            cp.wait_recv()

    args = [buf, send_sem, recv_sem] + ([] if after is None else [after])
    return pl.pallas_call(
        body, name=name,
        in_specs=[HBM_SPEC, SEM_SPEC, SEM_SPEC] + [_hbm_spec()] * (n_in - 3),
        out_specs=HBM_SPEC,
        out_shape=pltpu.HBM(buf.shape, buf.dtype),
        input_output_aliases={0: 0},
        compiler_params=pltpu.CompilerParams(has_side_effects=SPLIT_COPY),
    )(*args)


def _gather_forward(name, buf):
    def body(in_ref, out_ref, send_sems, recv_sems):
        x, y, c, me, chips, chip_idx = _mesh_pos()
        hr = out_ref.shape[1] // 2
        cps = []
        for j in range(N_PEER_CHIPS):
            got = out_ref.at[chip_idx[j], pl.ds(c * hr, hr), :]
            cp = pltpu.make_async_remote_copy(src_ref=got, dst_ref=got, send_sem=send_sems.at[j],
                                              recv_sem=recv_sems.at[j], device_id=(x, y, 1 - c), device_id_type=MESH)
            cp.start()
            cps.append(cp)
        for j in range(N_PEER_CHIPS):
            theirs = out_ref.at[chip_idx[j], pl.ds((1 - c) * hr, hr), :]
            pltpu.make_async_remote_copy(src_ref=theirs, dst_ref=theirs, send_sem=send_sems.at[j],
                                         recv_sem=recv_sems.at[j], device_id=(x, y, 1 - c),
                                         device_id_type=MESH).wait_recv()
        for cp in cps:
            cp.wait_send()

    return pl.pallas_call(
        body, name=name,
        in_specs=[_hbm_spec()], out_specs=_hbm_spec(),
        out_shape=SDS(buf.shape, buf.dtype),
        input_output_aliases={0: 0},
        scratch_shapes=[pltpu.SemaphoreType.DMA((N_PEER_CHIPS,)), pltpu.SemaphoreType.DMA((N_PEER_CHIPS,))],
    )(buf)


class _GatheredWeights:
    def __init__(self, keys, bufs):
        send_sems, recv_sems, thru = _gather_start(bufs)
        self._pending = {k: (b, s, r) for k, b, s, r in zip(keys, thru, send_sems, recv_sems)}
        self._ready = {}

    def get(self, name, layer, after=None):
        key = (name, layer)
        if key not in self._ready:
            buf, ssem, rsem = self._pending.pop(key)
            buf = _gather_wait(f"gather_wait_{name}_{layer}", buf, ssem, rsem, after)
            self._ready[key] = _gather_forward(f"gather_fwd_{name}_{layer}", buf)
        return self._ready[key]


def _swap_copy(g_ref, land_ref, send_sem, recv_sem):
    x, y, c, _, _, _ = _mesh_pos()
    hr = g_ref.shape[1] // 2
    return pltpu.make_async_remote_copy(src_ref=g_ref.at[:, pl.ds((1 - c) * hr, hr), :], dst_ref=land_ref,
                                        send_sem=send_sem, recv_sem=recv_sem, device_id=(x, y, 1 - c),
                                        device_id_type=MESH)


def _swap_start(name, g):
    land_shape = (g.shape[0], g.shape[1] // 2, g.shape[2])

    def body(g_ref, land_ref, send_sem, recv_sem, land_thru):
        _swap_copy(g_ref, land_ref, send_sem, recv_sem).start()

    return pl.pallas_call(
        body, name=name,
        in_specs=[HBM_SPEC, HBM_SPEC],
        out_specs=[SEM_SPEC, SEM_SPEC, HBM_SPEC],
        out_shape=[pltpu.SemaphoreType.DMA(()), pltpu.SemaphoreType.DMA(()), pltpu.HBM(land_shape, g.dtype)],
        input_output_aliases={1: 2},
        compiler_params=pltpu.CompilerParams(has_side_effects=SPLIT_COPY),
    )(_in_hbm(g), _in_hbm(lax.empty(land_shape, g.dtype)))


def _swap_wait(name, g, land, send_sem, recv_sem, after):
    def body(g_ref, land_ref, send_sem, recv_sem, after_ref, land_out):
        cp = _swap_copy(g_ref, land_ref, send_sem, recv_sem)
        cp.wait_send()
        cp.wait_recv()

    return pl.pallas_call(
        body, name=name,
        in_specs=[HBM_SPEC, HBM_SPEC, SEM_SPEC, SEM_SPEC, _hbm_spec()],
        out_specs=HBM_SPEC,
        out_shape=pltpu.HBM(land.shape, land.dtype),
        input_output_aliases={1: 0},
        compiler_params=pltpu.CompilerParams(has_side_effects=SPLIT_COPY),
    )(_in_hbm(g), land, send_sem, recv_sem, after)


def _add_my_half(name, g, r, c_arr):
    ns, R, C = g.shape
    hr = R // 2
    tr = min(256, hr)
    nt = hr // tr

    def body(c_ref, g_ref, r_ref, o_ref, land_ref):
        t = (g_ref[...] + r_ref[...]).astype(o_ref.dtype)
        o_ref[...] = t
        land_ref[...] = t

    out = pl.BlockSpec((None, tr, C), lambda s, i, c_ref: (s, i, 0))
    return pl.pallas_call(
        body, name=name,
        grid_spec=pltpu.PrefetchScalarGridSpec(
            num_scalar_prefetch=1, grid=(ns, nt),
            in_specs=[pl.BlockSpec((None, tr, C), lambda s, i, c_ref: (s, c_ref[0] * nt + i, 0)), out],
            out_specs=[out, out]),
        out_shape=[SDS((ns, hr, C), BF16)] * 2,
        compiler_params=_cp(2),
    )(c_arr, g, r)


def _exchange_start(name, part, land):
    def body(part_ref, land_ref, send_sems, recv_sems, land_thru):
        x, y, c, me, chips, chip_idx = _mesh_pos()
        for j in range(N_PEER_CHIPS):
            pltpu.make_async_remote_copy(src_ref=part_ref.at[chip_idx[j]], dst_ref=land_ref.at[me],
                                         send_sem=send_sems.at[j], recv_sem=recv_sems.at[j],
                                         device_id=(*chips[j], c), device_id_type=MESH).start()

    sems = pltpu.SemaphoreType.DMA((N_PEER_CHIPS,))
    return pl.pallas_call(
        body, name=name,
        in_specs=[HBM_SPEC, HBM_SPEC],
        out_specs=[SEM_SPEC, SEM_SPEC, HBM_SPEC],
        out_shape=[sems, sems, pltpu.HBM(land.shape, land.dtype)],
        input_output_aliases={1: 2},
        compiler_params=pltpu.CompilerParams(has_side_effects=SPLIT_COPY),
    )(_in_hbm(part), _in_hbm(land))


def _exchange_wait(name, part, land, send_sems, recv_sems, after):
    def body(part_ref, land_ref, send_sems, recv_sems, after_ref, land_out):
        x, y, c, me, chips, chip_idx = _mesh_pos()
        for j in range(N_PEER_CHIPS):
            cp = pltpu.make_async_remote_copy(src_ref=part_ref.at[chip_idx[j]], dst_ref=land_ref.at[chip_idx[j]],
                                              send_sem=send_sems.at[j], recv_sem=recv_sems.at[j],
                                              device_id=(*chips[j], c), device_id_type=MESH)
            cp.wait_send()
            cp.wait_recv()

    return pl.pallas_call(
        body, name=name,
        in_specs=[HBM_SPEC, HBM_SPEC, SEM_SPEC, SEM_SPEC, _hbm_spec()],
        out_specs=HBM_SPEC,
        out_shape=pltpu.HBM(land.shape, land.dtype),
        input_output_aliases={1: 0},
        compiler_params=pltpu.CompilerParams(has_side_effects=SPLIT_COPY),
    )(_in_hbm(part), land, send_sems, recv_sems, after)


class _GradReducer:
    def __init__(self, c_arr):
        self._c_arr = c_arr
        self._swapping = []
        self._exchanging = {}

    def begin(self, name, layer, g):
        tag = f"{name}_{layer}"
        self._swapping.append((name, layer, g, _swap_start(f"rs_swap_start_{tag}", g)))

    def advance(self, after):
        for name, layer, g, (ssem, rsem, land) in self._swapping:
            tag = f"{name}_{layer}"
            theirs = _swap_wait(f"rs_swap_wait_{tag}", g, land, ssem, rsem, after)
            part, own = _add_my_half(f"rs_add_{tag}", g, theirs, self._c_arr)
            self._exchanging[(name, layer)] = (part,) + tuple(_exchange_start(f"rs_xchg_start_{tag}", part, own))
        self._swapping = []

    def finish(self, names, n_layers, after):
        bufs = []
        for name in names:
            buf = None
            for layer in range(n_layers):
                part, ssems, rsems, land = self._exchanging.pop((name, layer))
                tag = f"{name}_{layer}"
                landed = _exchange_wait(f"rs_xchg_wait_{tag}", part, land, ssems, rsems, after)
                buf = _sum_chips(f"rs_sum_{tag}", landed, self._c_arr, layer, n_layers, buf)
            bufs.append(buf)
        return dict(zip(names, _join_halves(f"rs_join_{names[0]}", bufs)))


def _sum_chips(name, r, c_arr, layer, n_layers, prev):
    ns, H, C = r.shape
    tr = min(256, H)
    nt = H // tr

    def body(c_ref, r_ref, *rest):
        o_ref = rest[-1]
        o_ref[...] = ((r_ref[0].astype(F32) + r_ref[1].astype(F32)) + r_ref[2].astype(F32)) + r_ref[3].astype(F32)

    in_specs = [pl.BlockSpec((ns, tr, C), lambda i, c_ref: (0, i, 0))]
    args = [c_arr, r]
    aliases = {}
    if prev is not None:
        in_specs.append(_hbm_spec())
        args.append(prev)
        aliases = {2: 0}
    return pl.pallas_call(
        body, name=name,
        grid_spec=pltpu.PrefetchScalarGridSpec(
            num_scalar_prefetch=1, grid=(nt,), in_specs=in_specs,
            out_specs=pl.BlockSpec((None, tr, C), lambda i, c_ref: (layer, c_ref[0] * nt + i, 0))),
        out_shape=SDS((n_layers, 2 * H, C), F32),
        input_output_aliases=aliases,
        compiler_params=_cp(1),
    )(*args)


def _join_halves(name, bufs):
    T = len(bufs)

    def body(*refs):
        outs = refs[T:2 * T]
        send_sems, recv_sems = refs[2 * T:]
        x, y, c, _, _, _ = _mesh_pos()
        cps = []
        for t in range(T):
            hr = outs[t].shape[1] // 2
            mine = outs[t].at[:, pl.ds(c * hr, hr), :]
            cp = pltpu.make_async_remote_copy(src_ref=mine, dst_ref=mine, send_sem=send_sems.at[t],
                                              recv_sem=recv_sems.at[t], device_id=(x, y, 1 - c), device_id_type=MESH)
            cp.start()
            cps.append(cp)
        for t in range(T):
            hr = outs[t].shape[1] // 2
            theirs = outs[t].at[:, pl.ds((1 - c) * hr, hr), :]
            pltpu.make_async_remote_copy(src_ref=theirs, dst_ref=theirs, send_sem=send_sems.at[t],
                                         recv_sem=recv_sems.at[t], device_id=(x, y, 1 - c),
                                         device_id_type=MESH).wait_recv()
        for cp in cps:
            cp.wait_send()

    return pl.pallas_call(
        body, name=name,
        in_specs=[_hbm_spec()] * T, out_specs=[_hbm_spec()] * T,
        out_shape=[SDS(b.shape, b.dtype) for b in bufs],
        input_output_aliases={t: t for t in range(T)},
        scratch_shapes=[pltpu.SemaphoreType.DMA((T,)), pltpu.SemaphoreType.DMA((T,))],
    )(*bufs)


def _allreduce_small(buf):
    R, C = buf.shape

    def body(in_ref, out_ref, land_ref, send_sems, recv_sems):
        x, y, c = lax.axis_index("x"), lax.axis_index("y"), lax.axis_index("c")
        me = 4 * x + 2 * y + c
        land_ref[me] = in_ref[...]
        cps = []
        for k in range(1, N_DEV):
            kx, ky, kc = (k >> 2) & 1, (k >> 1) & 1, k & 1
            peer = (x ^ kx, y ^ ky, c ^ kc)
            cp = pltpu.make_async_remote_copy(src_ref=in_ref, dst_ref=land_ref.at[me],
                                              send_sem=send_sems.at[k - 1], recv_sem=recv_sems.at[k - 1],
                                              device_id=peer, device_id_type=MESH)
            cp.start()
            cps.append(cp)
        for k in range(1, N_DEV):
            src = me ^ k
            slot = land_ref.at[src]
            pltpu.make_async_remote_copy(src_ref=slot, dst_ref=slot, send_sem=send_sems.at[k - 1],
                                         recv_sem=recv_sems.at[k - 1], device_id=(x, y, c),
                                         device_id_type=MESH).wait_recv()
        for cp in cps:
            cp.wait_send()
        acc = land_ref[0]
        for d in range(1, N_DEV):
            acc = acc + land_ref[d]
        out_ref[...] = acc

    return pl.pallas_call(
        body, name="allreduce_small",
        in_specs=[pl.BlockSpec(memory_space=pltpu.VMEM)],
        out_specs=pl.BlockSpec(memory_space=pltpu.VMEM),
        out_shape=SDS((R, C), buf.dtype),
        scratch_shapes=[pltpu.VMEM((N_DEV, R, C), buf.dtype),
                        pltpu.SemaphoreType.DMA((N_DEV - 1,)), pltpu.SemaphoreType.DMA((N_DEV - 1,))],
        compiler_params=pltpu.CompilerParams(vmem_limit_bytes=V7X_VMEM_LIMIT),
    )(buf)


def _deinterleave(t, d):
    if d == 1:
        return t
    S, W = t.shape
    return t.reshape(S // d, d, W).transpose(1, 0, 2).reshape(S, W)


def _interleave(t, d):
    if d == 1:
        return t
    S, W = t.shape
    return t.reshape(d, S // d, W).transpose(1, 0, 2).reshape(S, W)


def _to_patterns(t, off):
    return jnp.stack([_deinterleave(t[:, off + PW * g:off + PW * (g + 1)], PATTERN_DILATION[g])
                      for g in range(N_PATTERNS)])


def _from_patterns(t3):
    return jnp.stack([_interleave(t3[g], PATTERN_DILATION[g]) for g in range(N_PATTERNS)])


def _pack_rows(vectors):
    flat = jnp.concatenate([v.reshape(-1) for v in vectors])
    n = flat.shape[0]
    padded = -(-n // 1024) * 1024
    return jnp.pad(flat, (0, padded - n)).reshape(padded // 128, 128)


def _unpack_rows(buf, shapes):
    flat = buf.reshape(-1)
    out, off = [], 0
    for s in shapes:
        n = 1
        for dim in s:
            n *= dim
        out.append(flat[off:off + n].reshape(s))
        off += n
    return out


def _layer_forward(l, x, prm, wg):
    S, D = x.shape
    p, h = _norm_matmul(f"in_proj_{l}", x, prm["attn_norm"][l], wg.get("w_in", l, x), F32)
    y_a = _sgu_fwd(f"sgu_fwd_{l}", p, prm["sgu_wt"][l], prm["sgu_bb"][l])
    y_b = _conv_fwd(f"conv_fwd_{l}", p, prm["conv_w"][l])
    q3, k3, v3 = _to_patterns(p, OFF_Q), _to_patterns(p, OFF_K), _to_patterns(p, OFF_V)
    o3d, lse3d = _attn_fwd(f"attn_fwd_{l}", q3, k3, v3, prm["q_gain"][l], prm["k_gain"][l], prm["bd"])
    o3, lse3 = _from_patterns(o3d), _from_patterns(lse3d)
    y_c = _mix_fwd(f"mix_fwd_{l}", o3, lse3)
    ycat = jnp.concatenate([y_a, y_b, y_c], axis=1)
    tm = min(512, S)
    w_out = wg.get("w_out", l, ycat)
    rq = w_out.shape[1]
    x1 = _matmul(
        f"out_proj_{l}", ycat, w_out, (S, D), F32, grid=(S // tm, 1, N_CHIPS),
        a_spec=pl.BlockSpec((tm, rq), lambda i, j, k: (i, k)),
        b_spec=pl.BlockSpec((None, rq, D), lambda i, j, k: (k, 0, 0)),
        o_spec=pl.BlockSpec((tm, D), lambda i, j, k: (i, 0)),
        contract=(1, 0), acc_shape=(tm, D),
        extras=(x,), extra_specs=(pl.BlockSpec((tm, D), lambda i, j, k: (i, 0)),),
        epi=lambda r, res: r + res)
    a, h2 = _norm_matmul(f"mlp_in_{l}", x1, prm["mlp_norm"][l], wg.get("w_mlp_in", l, x1), BF16)
    w_mlp_out = wg.get("w_mlp_out", l, a)
    dff4 = w_mlp_out.shape[1]
    tk = min(1024, dff4)
    kpc = dff4 // tk
    x2 = _matmul(
        f"mlp_out_{l}", a, w_mlp_out, (S, D), F32, grid=(S // tm, 1, N_CHIPS * kpc),
        a_spec=pl.BlockSpec((tm, tk), lambda i, j, k: (i, k)),
        b_spec=pl.BlockSpec((None, tk, D), lambda i, j, k: (k // kpc, k % kpc, 0)),
        o_spec=pl.BlockSpec((tm, D), lambda i, j, k: (i, 0)),
        contract=(1, 0), acc_shape=(tm, D), a_pre=_relu2_bf16,
        extras=(x1,), extra_specs=(pl.BlockSpec((tm, D), lambda i, j, k: (i, 0)),),
        epi=lambda r, res: r + res)
    saved = dict(x=x, p=p, h=h, q3=q3, k3=k3, v3=v3, o3=o3, lse3=lse3, lse3d=lse3d, ycat=ycat, x1=x1, a=a, h2=h2)
    return x2, saved


def _layer_backward(l, dx2, dx2b, sv, prm, wg, sink):
    S, D = dx2.shape
    w_in, w_out = wg.get("w_in", l), wg.get("w_out", l)
    w_mlp_in, w_mlp_out = wg.get("w_mlp_in", l), wg.get("w_mlp_out", l)
    dff4 = w_mlp_in.shape[-1]
    dff = N_CHIPS * dff4
    tm = min(512, S)
    tk = min(1024, S)
    nks = S // tk

    da = _matmul(
        f"mlp_out_bwd_{l}", dx2b, w_mlp_out, (S, dff), BF16, grid=(S // tm, N_CHIPS, 1),
        a_spec=pl.BlockSpec((tm, D), lambda i, j, k: (i, 0)),
        b_spec=pl.BlockSpec((None, dff4, D), lambda i, j, k: (j, 0, 0)),
        o_spec=pl.BlockSpec((tm, dff4), lambda i, j, k: (i, j)),
        contract=(1, 1), acc_shape=(tm, dff4),
        extras=(sv["a"],), extra_specs=(pl.BlockSpec((tm, dff4), lambda i, j, k: (i, j)),),
        epi=lambda r, act: r * (2.0 * jnp.maximum(act.astype(F32), 0.0)))
    tmw = min(1024, dff4)
    mpc = dff4 // tmw
    g_w2 = _matmul(
        f"mlp_out_dw_{l}", sv["a"], dx2b, (N_CHIPS, dff4, D), F32, grid=(N_CHIPS * mpc, 1, nks),
        a_spec=pl.BlockSpec((tk, tmw), lambda i, j, k: (k, i)),
        b_spec=pl.BlockSpec((tk, D), lambda i, j, k: (k, 0)),
        o_spec=pl.BlockSpec((None, tmw, D), lambda i, j, k: (i // mpc, i % mpc, 0)),
        contract=(0, 0), acc_shape=(tmw, D), a_pre=_relu2_bf16)
    sink.begin("w_mlp_out", l, g_w2)
    dh2 = _matmul(
        f"mlp_in_bwd_{l}", da, w_mlp_in, (S, D), F32, grid=(S // tm, 1, N_CHIPS),
        a_spec=pl.BlockSpec((tm, dff4), lambda i, j, k: (i, k)),
        b_spec=pl.BlockSpec((None, D, dff4), lambda i, j, k: (k, 0, 0)),
        o_spec=pl.BlockSpec((tm, D), lambda i, j, k: (i, 0)),
        contract=(1, 1), acc_shape=(tm, D))
    sink.advance(dh2)
    tmd = min(1024, D)
    g_w1 = _matmul(
        f"mlp_in_dw_{l}", sv["h2"], da, (N_CHIPS, D, dff4), F32, grid=(N_CHIPS, D // tmd, nks),
        a_spec=pl.BlockSpec((tk, tmd), lambda i, j, k: (k, j)),
        b_spec=pl.BlockSpec((tk, dff4), lambda i, j, k: (k, i)),
        o_spec=pl.BlockSpec((None, tmd, dff4), lambda i, j, k: (i, j, 0)),
        contract=(0, 0), acc_shape=(tmd, dff4))
    sink.begin("w_mlp_in", l, g_w1)
    dx1, dx1b, g_mlp_norm = _rmsnorm_bwd(f"mlp_norm_bwd_{l}", dh2, sv["x1"], prm["mlp_norm"][l], dx2)

    rq = w_out.shape[1]
    dycat = _matmul(
        f"out_proj_bwd_{l}", dx1b, w_out, (S, N_CHIPS * rq), F32, grid=(S // tm, N_CHIPS, 1),
        a_spec=pl.BlockSpec((tm, D), lambda i, j, k: (i, 0)),
        b_spec=pl.BlockSpec((None, rq, D), lambda i, j, k: (j, 0, 0)),
        o_spec=pl.BlockSpec((tm, rq), lambda i, j, k: (i, j)),
        contract=(1, 1), acc_shape=(tm, rq))
    sink.advance(dycat)
    g_wout = _matmul(
        f"out_proj_dw_{l}", sv["ycat"], dx1b, (N_CHIPS, rq, D), F32, grid=(N_CHIPS, 1, nks),
        a_spec=pl.BlockSpec((tk, rq), lambda i, j, k: (k, i)),
        b_spec=pl.BlockSpec((tk, D), lambda i, j, k: (k, 0)),
        o_spec=pl.BlockSpec((None, rq, D), lambda i, j, k: (i, 0, 0)),
        contract=(0, 0), acc_shape=(rq, D))
    sink.begin("w_out", l, g_wout)

    p = sv["p"]
    du, dv_a, g_sgu_w, db_lanes = _sgu_bwd(f"sgu_bwd_{l}", p, dycat, prm["sgu_wt"][l], prm["sgu_wtt"][l],
                                           prm["sgu_bb"][l])
    sink.advance(du)
    g_sgu_b = db_lanes[:, :A_HEADS].T
    db, dc, dxb, g_conv = _conv_bwd(f"conv_bwd_{l}", p, dycat, prm["conv_w"][l])
    do3, c3 = _mix_bwd(f"mix_bwd_{l}", sv["o3"], sv["lse3"], dycat, prm["bd"])
    do3d = jnp.stack([_deinterleave(do3[g], PATTERN_DILATION[g]) for g in range(N_PATTERNS)])
    c3d = jnp.stack([_deinterleave(c3[g], PATTERN_DILATION[g]) for g in range(N_PATTERNS)])
    dq3, dk3, dv3, dgq, dgk = _attn_bwd(f"attn_bwd_{l}", sv["q3"], sv["k3"], sv["v3"], sv["lse3d"], do3d, c3d,
                                        prm["q_gain"][l], prm["k_gain"][l], prm["bd"])
    g_q = dgq.reshape(N_PATTERNS * PW // HEAD_DIM, HEAD_DIM).sum(axis=0)
    g_k = dgk.reshape(N_PATTERNS * PW // HEAD_DIM, HEAD_DIM).sum(axis=0)
    nat = lambda t3: jnp.concatenate([_interleave(t3[g], PATTERN_DILATION[g]) for g in range(N_PATTERNS)], axis=1)
    dp = jnp.concatenate([du, dv_a, db, dc, dxb, nat(dq3), nat(dk3), nat(dv3)], axis=1)

    ns_in = w_in.shape[-1]
    dh = _matmul(
        f"in_proj_bwd_{l}", dp, w_in, (S, D), F32, grid=(S // tm, 1, N_CHIPS),
        a_spec=pl.BlockSpec((tm, ns_in), lambda i, j, k: (i, k)),
        b_spec=pl.BlockSpec((None, D, ns_in), lambda i, j, k: (k, 0, 0)),
        o_spec=pl.BlockSpec((tm, D), lambda i, j, k: (i, 0)),
        contract=(1, 1), acc_shape=(tm, D))
    g_win = _matmul(
        f"in_proj_dw_{l}", sv["h"], dp, (N_CHIPS, D, ns_in), F32, grid=(N_CHIPS, D // tmd, nks),
        a_spec=pl.BlockSpec((tk, tmd), lambda i, j, k: (k, j)),
        b_spec=pl.BlockSpec((tk, ns_in), lambda i, j, k: (k, i)),
        o_spec=pl.BlockSpec((None, tmd, ns_in), lambda i, j, k: (i, j, 0)),
        contract=(0, 0), acc_shape=(tmd, ns_in))
    sink.begin("w_in", l, g_win)
    dx0, dx0b, g_attn_norm = _rmsnorm_bwd(f"attn_norm_bwd_{l}", dh, sv["x"], prm["attn_norm"][l], dx1)
    sink.advance(dx0)

    big = dict(w_in=g_win, w_out=g_wout, w_mlp_in=g_w1, w_mlp_out=g_w2)
    small = dict(attn_norm=g_attn_norm.reshape(-1), sgu_w=g_sgu_w, sgu_b=g_sgu_b, conv_w=g_conv,
                 q_norm=g_q, k_norm=g_k, mlp_norm=g_mlp_norm.reshape(-1))
    return dx0, dx0b, big, small


BIG = ("w_in", "w_out", "w_mlp_in", "w_mlp_out")
SMALL_REPLICATED = ("attn_norm", "sgu_w", "sgu_b", "q_norm", "k_norm", "mlp_norm")


def _local_step(x, target, prm, wg, n_layers, sink):
    saved = []
    h = x
    for l in range(n_layers):
        h, sv = _layer_forward(l, h, prm, wg)
        saved.append(sv)
    dy, dyb, colsq = _loss_kernel(h, target)
    loss = 0.5 * jnp.sum(colsq) / x.shape[1]
    bigs, smalls = [None] * n_layers, [None] * n_layers
    for l in reversed(range(n_layers)):
        dy, dyb, bigs[l], smalls[l] = _layer_backward(l, dy, dyb, saved[l], prm, wg, sink)
    return loss, dy, bigs, smalls


def _prepare_params(attn_norm, sgu_w, sgu_b, conv_full, q_norm, k_norm, mlp_norm):
    n_layers = attn_norm.shape[0]
    tri = jnp.tril(sgu_w)
    idx = jnp.arange(PW)
    bd = (idx[:, None] // HEAD_DIM == idx[None, :] // HEAD_DIM).astype(BF16)
    return dict(
        attn_norm=[attn_norm[l][None, :] for l in range(n_layers)],
        mlp_norm=[mlp_norm[l][None, :] for l in range(n_layers)],
        sgu_wt=[tri[l].astype(BF16) for l in range(n_layers)],
        sgu_wtt=[tri[l].transpose(0, 2, 1).astype(BF16) for l in range(n_layers)],
        sgu_bb=[jnp.repeat(sgu_b[l].T, HEAD_DIM, axis=1) for l in range(n_layers)],
        conv_w=[conv_full[l] for l in range(n_layers)],
        q_gain=[jnp.tile(q_norm[l], PW // HEAD_DIM)[None, :] for l in range(n_layers)],
        k_gain=[jnp.tile(k_norm[l], PW // HEAD_DIM)[None, :] for l in range(n_layers)],
        bd=bd,
    )


def kernel(x, attn_norm, w_in, sgu_w, sgu_b, conv_w, q_norm, k_norm, w_out, mlp_norm, w_mlp_in, w_mlp_out, loss_target, m_attn_norm, m_w_in, m_sgu_w, m_sgu_b, m_conv_w, m_q_norm, m_k_norm, m_w_out, m_mlp_norm, m_w_mlp_in, m_w_mlp_out, v_attn_norm, v_w_in, v_sgu_w, v_sgu_b, v_conv_w, v_q_norm, v_k_norm, v_w_out, v_mlp_norm, v_w_mlp_in, v_w_mlp_out):
    n_layers = attn_norm.shape[0]
    weights = dict(attn_norm=attn_norm, w_in=w_in, sgu_w=sgu_w, sgu_b=sgu_b, conv_w=conv_w, q_norm=q_norm,
                   k_norm=k_norm, w_out=w_out, mlp_norm=mlp_norm, w_mlp_in=w_mlp_in, w_mlp_out=w_mlp_out)
    mom_m = dict(attn_norm=m_attn_norm, w_in=m_w_in, sgu_w=m_sgu_w, sgu_b=m_sgu_b, conv_w=m_conv_w,
                 q_norm=m_q_norm, k_norm=m_k_norm, w_out=m_w_out, mlp_norm=m_mlp_norm, w_mlp_in=m_w_mlp_in,
                 w_mlp_out=m_w_mlp_out)
    mom_v = dict(attn_norm=v_attn_norm, w_in=v_w_in, sgu_w=v_sgu_w, sgu_b=v_sgu_b, conv_w=v_conv_w,
                 q_norm=v_q_norm, k_norm=v_k_norm, w_out=v_w_out, mlp_norm=v_mlp_norm, w_mlp_in=v_w_mlp_in,
                 w_mlp_out=v_w_mlp_out)
    order = ("attn_norm", "w_in", "sgu_w", "sgu_b", "conv_w", "q_norm", "k_norm", "w_out", "mlp_norm",
             "w_mlp_in", "w_mlp_out")
    chip = 2 * lax.axis_index("x") + lax.axis_index("y")
    c_arr = lax.axis_index("c").astype(jnp.int32).reshape(1)

    conv_cols = conv_w.shape[-1]
    chip_arr = chip.astype(jnp.int32).reshape(1)
    conv_pack = jnp.pad(conv_w.reshape(-1), (0, 2048 - conv_w.size)).reshape(1, 16, 128)
    keys = [("conv_w", 0)]
    placed = [_place_shard("place_conv_w", conv_pack, 0, chip_arr, F32)]
    for l in range(n_layers):
        for n in BIG:
            keys.append((n, l))
            placed.append(_place_shard(f"place_{n}_{l}", weights[n], l, chip_arr, BF16))
    wg = _GatheredWeights(keys, placed)
    conv_full = wg.get("conv_w", 0).reshape(N_CHIPS, 2048)[:, :conv_w.size].reshape(N_CHIPS, n_layers, 3, conv_cols)
    conv_full = conv_full.transpose(1, 2, 0, 3).reshape(n_layers, 3, N_CHIPS * conv_cols)
    prm = _prepare_params(attn_norm, sgu_w, sgu_b, conv_full, q_norm, k_norm, mlp_norm)

    sink = _GradReducer(c_arr)
    loss_local, grad_x, _, smalls = _local_step(x[0], loss_target[0], prm, wg, n_layers, sink)
    loss = lax.psum(loss_local, ("x", "y", "c"))

    small_names = SMALL_REPLICATED + ("conv_w",)
    small_shapes = [(n_layers,) + tuple(smalls[0][n].shape) for n in small_names]
    packed = _pack_rows([jnp.stack([smalls[l][n] for l in range(n_layers)]) for n in small_names])
    summed = _unpack_rows(_allreduce_small(packed), small_shapes)
    grads = dict(zip(small_names, summed))
    grads["conv_w"] = lax.dynamic_slice_in_dim(grads["conv_w"], chip * conv_cols, conv_cols, axis=2)

    delta, new_m, new_v = {}, {}, {}

    def update(names, after):
        joined = sink.finish(names, n_layers, after)
        for n in names:
            shp = weights[n].shape
            two_d = (shp[0] * shp[1], shp[2])
            grads[n] = joined[n]
            d, nm, nv = _adamw(f"adamw_{n}", weights[n].reshape(two_d), joined[n].reshape(two_d),
                               mom_m[n].reshape(two_d), mom_v[n].reshape(two_d))
            delta[n], new_m[n], new_v[n] = d.reshape(shp), nm.reshape(shp), nv.reshape(shp)

    update(("w_mlp_out", "w_mlp_in", "w_out"), summed[0])
    update(("w_in",), delta["w_out"])
    smalls_all = SMALL_REPLICATED + ("conv_w",)
    shapes = [weights[n].shape for n in smalls_all]
    d, nm, nv = _adamw("adamw_small",
                       _pack_rows([weights[n] for n in smalls_all]), _pack_rows([grads[n] for n in smalls_all]),
                       _pack_rows([mom_m[n] for n in smalls_all]), _pack_rows([mom_v[n] for n in smalls_all]))
    for n, dd, mm, vv in zip(smalls_all, _unpack_rows(d, shapes), _unpack_rows(nm, shapes), _unpack_rows(nv, shapes)):
        delta[n], new_m[n], new_v[n] = dd, mm, vv

    return (loss, grad_x[None], *[grads[n] for n in order], *[delta[n] for n in order],
            *[new_m[n] for n in order], *[new_v[n] for n in order])
```

```python
import jax
import jax.numpy as jnp
from jax import lax
from jax.experimental import pallas as pl
from jax.experimental.pallas import tpu as pltpu

F32 = jnp.float32
BF16 = jnp.bfloat16
SDS = jax.ShapeDtypeStruct

EPS = 1e-6
HEAD_DIM = 64
A_HEADS = 8
A_WIDTH = 512
CHUNK = 128
B_WIDTH = 768
C_WIDTH = 768
N_PATTERNS = 3
PATTERN_DILATION = (1, 4, 16)
PW = 256
D_IN_PROJ = 5632
OFF_AU, OFF_AV, OFF_BB, OFF_BC, OFF_BX, OFF_Q, OFF_K, OFF_V = 0, 512, 1024, 1792, 2560, 3328, 4096, 4864
N_CHIPS = 4
N_DEV = 8
BLK = 128

ADAM_LR, ADAM_B1, ADAM_B2, ADAM_EPS, ADAM_WD, ADAM_STEP = 0.001, 0.9, 0.999, 1e-08, 0.01, 10

V7X_VMEM_LIMIT = 56 * 1024 * 1024
MESH = pl.DeviceIdType.MESH
NEG = -1e30


def _cp(n_axes):
    return pltpu.CompilerParams(dimension_semantics=("arbitrary",) * n_axes, vmem_limit_bytes=V7X_VMEM_LIMIT)


def _hbm_spec():
    return pl.BlockSpec(memory_space=pl.ANY)


def _norm_matmul(name, x, g, wg, out_dtype):
    S, D = x.shape
    ns, _, Ns = wg.shape
    tm = min(512, S)

    def body(x_ref, g_ref, w_ref, o_ref, h_ref, hs_ref):
        @pl.when(pl.program_id(1) == 0)
        def _():
            xv = x_ref[...]
            y = xv * lax.rsqrt(jnp.mean(xv * xv, axis=-1, keepdims=True) + EPS) * g_ref[...]
            hb = y.astype(BF16)
            hs_ref[...] = hb
            h_ref[...] = hb
        o_ref[...] = jnp.dot(hs_ref[...], w_ref[...], preferred_element_type=F32).astype(o_ref.dtype)

    return pl.pallas_call(
        body, name=name, grid=(S // tm, ns),
        in_specs=[pl.BlockSpec((tm, D), lambda i, s: (i, 0)),
                  pl.BlockSpec((1, D), lambda i, s: (0, 0)),
                  pl.BlockSpec((None, D, Ns), lambda i, s: (s, 0, 0))],
        out_specs=[pl.BlockSpec((tm, Ns), lambda i, s: (i, s)),
                   pl.BlockSpec((tm, D), lambda i, s: (i, 0))],
        out_shape=[SDS((S, ns * Ns), out_dtype), SDS((S, D), BF16)],
        scratch_shapes=[pltpu.VMEM((tm, D), BF16)],
        compiler_params=_cp(2),
    )(x, g, wg)


def _matmul(name, a, b, out_shape, out_dtype, *, grid, a_spec, b_spec, o_spec, contract, acc_shape,
            extras=(), extra_specs=(), a_pre=None, epi=None, deps=()):
    nk = grid[2]
    n_ex = len(extras)
    n_dep = len(deps)
    dims = (((contract[0],), (contract[1],)), ((), ()))

    def body(a_ref, b_ref, *rest):
        ex = rest[:n_ex]
        o_ref = rest[n_ex + n_dep]
        acc_ref = rest[n_ex + n_dep + 1]
        k = pl.program_id(2)

        @pl.when(k == 0)
        def _():
            acc_ref[...] = jnp.zeros_like(acc_ref)

        av = a_ref[...]
        if a_pre is not None:
            av = a_pre(av)
        acc_ref[...] += lax.dot_general(av, b_ref[...], dims, preferred_element_type=F32)

        @pl.when(k == nk - 1)
        def _():
            r = acc_ref[...]
            if epi is not None:
                r = epi(r, *[e[...] for e in ex])
            o_ref[...] = r.astype(o_ref.dtype)

    return pl.pallas_call(
        body, name=name, grid=grid,
        in_specs=[a_spec, b_spec, *extra_specs] + [_hbm_spec()] * n_dep,
        out_specs=o_spec,
        out_shape=SDS(out_shape, out_dtype),
        scratch_shapes=[pltpu.VMEM(acc_shape, F32)],
        compiler_params=_cp(3),
    )(a, b, *extras, *deps)


def _relu2_bf16(t):
    r = jnp.maximum(t.astype(F32), 0.0)
    return (r * r).astype(BF16)


def _loss_kernel(y, t):
    S, D = y.shape
    tm = min(256, S)

    def body(y_ref, t_ref, dy_ref, dyb_ref, l_ref):
        @pl.when(pl.program_id(0) == 0)
        def _():
            l_ref[...] = jnp.zeros_like(l_ref)
        e = y_ref[...] - t_ref[...]
        l_ref[...] += jnp.sum(e * e, axis=0, keepdims=True)
        dy = e * (1.0 / D)
        dy_ref[...] = dy
        dyb_ref[...] = dy.astype(BF16)

    row = pl.BlockSpec((tm, D), lambda i: (i, 0))
    return pl.pallas_call(
        body, name="loss_head", grid=(S // tm,),
        in_specs=[row, row],
        out_specs=[row, row, pl.BlockSpec((1, D), lambda i: (0, 0))],
        out_shape=[SDS((S, D), F32), SDS((S, D), BF16), SDS((1, D), F32)],
        compiler_params=_cp(1),
    )(y, t)


def _rmsnorm_bwd(name, dh, x, g, dres, deps=()):
    S, D = x.shape
    tm = min(256, S)
    n_dep = len(deps)

    def body(dh_ref, x_ref, g_ref, dres_ref, *rest):
        dx_ref, dxb_ref, dg_ref = rest[n_dep:]
        @pl.when(pl.program_id(0) == 0)
        def _():
            dg_ref[...] = jnp.zeros_like(dg_ref)
        xv = x_ref[...]
        dhv = dh_ref[...]
        rstd = lax.rsqrt(jnp.mean(xv * xv, axis=-1, keepdims=True) + EPS)
        xhat = xv * rstd
        dg_ref[...] += jnp.sum(dhv * xhat, axis=0, keepdims=True)
        dxn = dhv * g_ref[...]
        dx = dres_ref[...] + rstd * (dxn - xhat * jnp.mean(dxn * xhat, axis=-1, keepdims=True))
        dx_ref[...] = dx
        dxb_ref[...] = dx.astype(BF16)

    row = pl.BlockSpec((tm, D), lambda i: (i, 0))
    vec = pl.BlockSpec((1, D), lambda i: (0, 0))
    return pl.pallas_call(
        body, name=name, grid=(S // tm,),
        in_specs=[row, row, vec, row] + [_hbm_spec()] * n_dep,
        out_specs=[row, row, vec],
        out_shape=[SDS((S, D), F32), SDS((S, D), BF16), SDS((1, D), F32)],
        compiler_params=_cp(1),
    )(dh, x, g, dres, *deps)


def _adamw(name, w, g, m, v):
    R, C = w.shape
    tr = 256 if R % 256 == 0 else R
    c1 = 1.0 - ADAM_B1 ** ADAM_STEP
    c2 = 1.0 - ADAM_B2 ** ADAM_STEP

    def body(w_ref, g_ref, m_ref, v_ref, d_ref, nm_ref, nv_ref):
        gv = g_ref[...]
        nm = ADAM_B1 * m_ref[...] + (1.0 - ADAM_B1) * gv
        nv = ADAM_B2 * v_ref[...] + (1.0 - ADAM_B2) * (gv * gv)
        m_hat = nm / c1
        v_hat = nv / c2
        d_ref[...] = -ADAM_LR * (m_hat / (jnp.sqrt(v_hat) + ADAM_EPS) + ADAM_WD * w_ref[...])
        nm_ref[...] = nm
        nv_ref[...] = nv

    blk = pl.BlockSpec((tr, C), lambda i: (i, 0))
    return pl.pallas_call(
        body, name=name, grid=(R // tr,),
        in_specs=[blk] * 4, out_specs=[blk] * 3,
        out_shape=[SDS((R, C), F32)] * 3,
        compiler_params=_cp(1),
    )(w, g, m, v)


def _pair_select(lane, lo, hi):
    return jnp.where(lane < HEAD_DIM, lo, hi)


def _sgu_fwd(name, p, wt, bb):
    S = p.shape[0]

    def body(u_ref, v_ref, wt_ref, bb_ref, o_ref):
        lane = lax.broadcasted_iota(jnp.int32, (CHUNK, 128), 1)
        for pp in range(A_HEADS // 2):
            cs = slice(128 * pp, 128 * (pp + 1))
            vb = v_ref[:, cs].astype(BF16)
            mixed = _pair_select(lane,
                                 jnp.dot(wt_ref[2 * pp], vb, preferred_element_type=F32),
                                 jnp.dot(wt_ref[2 * pp + 1], vb, preferred_element_type=F32)) + bb_ref[:, cs]
            o_ref[:, cs] = (u_ref[:, cs] * mixed).astype(o_ref.dtype)

    return pl.pallas_call(
        body, name=name, grid=(S // CHUNK,),
        in_specs=[pl.BlockSpec((CHUNK, A_WIDTH), lambda c: (c, OFF_AU // A_WIDTH)),
                  pl.BlockSpec((CHUNK, A_WIDTH), lambda c: (c, OFF_AV // A_WIDTH)),
                  pl.BlockSpec((A_HEADS, CHUNK, CHUNK), lambda c: (0, 0, 0)),
                  pl.BlockSpec((CHUNK, A_WIDTH), lambda c: (0, 0))],
        out_specs=pl.BlockSpec((CHUNK, A_WIDTH), lambda c: (c, 0)),
        out_shape=SDS((S, A_WIDTH), BF16),
        compiler_params=_cp(1),
    )(p, p, wt, bb)


def _sgu_bwd(name, p, dycat, wt, wtt, bb):
    S = p.shape[0]

    def body(u_ref, v_ref, dy_ref, wt_ref, wtt_ref, bb_ref, du_ref, dv_ref, dw_ref, db_ref, dbacc_ref):
        c = pl.program_id(0)

        @pl.when(c == 0)
        def _():
            dw_ref[...] = jnp.zeros_like(dw_ref)
            dbacc_ref[...] = jnp.zeros_like(dbacc_ref)

        lane = lax.broadcasted_iota(jnp.int32, (CHUNK, 128), 1)
        row = lax.broadcasted_iota(jnp.int32, (CHUNK, 128), 0)
        causal = row >= lane
        for pp in range(A_HEADS // 2):
            cs = slice(128 * pp, 128 * (pp + 1))
            v = v_ref[:, cs]
            vb = v.astype(BF16)
            u = u_ref[:, cs]
            dy = dy_ref[:, cs]
            mixed = _pair_select(lane,
                                 jnp.dot(wt_ref[2 * pp], vb, preferred_element_type=F32),
                                 jnp.dot(wt_ref[2 * pp + 1], vb, preferred_element_type=F32)) + bb_ref[:, cs]
            du_ref[:, cs] = (dy * mixed).astype(du_ref.dtype)
            dm = dy * u
            dmb = dm.astype(BF16)
            dv = _pair_select(lane,
                              jnp.dot(wtt_ref[2 * pp], dmb, preferred_element_type=F32),
                              jnp.dot(wtt_ref[2 * pp + 1], dmb, preferred_element_type=F32))
            dv_ref[:, cs] = dv.astype(dv_ref.dtype)
            dbacc_ref[:, cs] += dm
            nt = (((1,), (1,)), ((), ()))
            dm_lo = jnp.where(lane < HEAD_DIM, dm, 0.0).astype(BF16)
            dm_hi = jnp.where(lane >= HEAD_DIM, dm, 0.0).astype(BF16)
            dw_ref[2 * pp] += jnp.where(causal, lax.dot_general(dm_lo, vb, nt, preferred_element_type=F32), 0.0)
            dw_ref[2 * pp + 1] += jnp.where(causal, lax.dot_general(dm_hi, vb, nt, preferred_element_type=F32), 0.0)

        @pl.when(c == S // CHUNK - 1)
        def _():
            out = jnp.zeros((CHUNK, 128), F32)
            for pp in range(A_HEADS // 2):
                acc = dbacc_ref[:, 128 * pp:128 * (pp + 1)]
                s_lo = jnp.sum(jnp.where(lane < HEAD_DIM, acc, 0.0), axis=1, keepdims=True)
                s_hi = jnp.sum(jnp.where(lane >= HEAD_DIM, acc, 0.0), axis=1, keepdims=True)
                out = jnp.where(lane == 2 * pp, s_lo, out)
                out = jnp.where(lane == 2 * pp + 1, s_hi, out)
            db_ref[...] = out

    chunk = lambda col: pl.BlockSpec((CHUNK, A_WIDTH), lambda c: (c, col))
    wspec = pl.BlockSpec((A_HEADS, CHUNK, CHUNK), lambda c: (0, 0, 0))
    return pl.pallas_call(
        body, name=name, grid=(S // CHUNK,),
        in_specs=[chunk(OFF_AU // A_WIDTH), chunk(OFF_AV // A_WIDTH), chunk(0), wspec, wspec,
                  pl.BlockSpec((CHUNK, A_WIDTH), lambda c: (0, 0))],
        out_specs=[chunk(0), chunk(0), wspec, pl.BlockSpec((CHUNK, 128), lambda c: (0, 0))],
        out_shape=[SDS((S, A_WIDTH), BF16), SDS((S, A_WIDTH), BF16),
                   SDS((A_HEADS, CHUNK, CHUNK), F32), SDS((CHUNK, 128), F32)],
        scratch_shapes=[pltpu.VMEM((CHUNK, A_WIDTH), F32)],
        compiler_params=_cp(1),
    )(p, p, dycat, wt, wtt, bb)


CONV_HALO = 8


def _shift_down(a, halo, k):
    T = a.shape[0]
    row = lax.broadcasted_iota(jnp.int32, a.shape, 0)
    out = pltpu.roll(a, k, 0)
    for r in range(k):
        out = jnp.where(row == r, halo[CONV_HALO - k + r:CONV_HALO - k + r + 1, :], out)
    return out


def _shift_up(a, halo, k):
    T = a.shape[0]
    row = lax.broadcasted_iota(jnp.int32, a.shape, 0)
    out = pltpu.roll(a, T - k, 0)
    for r in range(k):
        out = jnp.where(row == T - k + r, halo[r:r + 1, :], out)
    return out


def _conv_specs(S, T):
    hb = T // CONV_HALO
    last = S // CONV_HALO - 1
    tile = lambda col0: pl.BlockSpec((T, 128), lambda j, i: (i, col0 + j))
    prev = lambda col0: pl.BlockSpec((CONV_HALO, 128), lambda j, i: (jnp.maximum(i * hb - 1, 0), col0 + j))
    nxt = lambda col0: pl.BlockSpec((CONV_HALO, 128), lambda j, i: (jnp.minimum((i + 1) * hb, last), col0 + j))
    return tile, prev, nxt


def _conv_fwd(name, p, w):
    S = p.shape[0]
    T = min(512, S)
    tile, prev, _ = _conv_specs(S, T)
    cb, cc, cx = OFF_BB // 128, OFF_BC // 128, OFF_BX // 128

    def body(b_ref, c_ref, x_ref, ch_ref, xh_ref, w_ref, o_ref):
        i = pl.program_id(1)
        z = c_ref[...] * x_ref[...]
        zh = jnp.where(i > 0, ch_ref[...] * xh_ref[...], 0.0)
        z1 = _shift_down(z, zh, 1)
        z2 = _shift_down(z, zh, 2)
        conv = w_ref[0:1, :] * z2 + w_ref[1:2, :] * z1 + w_ref[2:3, :] * z
        o_ref[...] = (b_ref[...] * conv).astype(o_ref.dtype)

    return pl.pallas_call(
        body, name=name, grid=(B_WIDTH // 128, S // T),
        in_specs=[tile(cb), tile(cc), tile(cx), prev(cc), prev(cx),
                  pl.BlockSpec((3, 128), lambda j, i: (0, j))],
        out_specs=tile(0),
        out_shape=SDS((S, B_WIDTH), BF16),
        compiler_params=_cp(2),
    )(p, p, p, p, p, w)


def _conv_bwd(name, p, dycat, w):
    S = p.shape[0]
    T = min(512, S)
    tile, prev, nxt = _conv_specs(S, T)
    cb, cc, cx = OFF_BB // 128, OFF_BC // 128, OFF_BX // 128
    cdy = A_WIDTH // 128
    n_i = S // T

    def body(b_ref, c_ref, x_ref, dy_ref, ch_ref, xh_ref, bn_ref, dyn_ref, w_ref,
             db_ref, dc_ref, dx_ref, dw_ref):
        i = pl.program_id(1)

        @pl.when(i == 0)
        def _():
            dw_ref[...] = jnp.zeros_like(dw_ref)

        cv = c_ref[...]
        xv = x_ref[...]
        z = cv * xv
        zh = jnp.where(i > 0, ch_ref[...] * xh_ref[...], 0.0)
        z1 = _shift_down(z, zh, 1)
        z2 = _shift_down(z, zh, 2)
        w0, w1, w2 = w_ref[0:1, :], w_ref[1:2, :], w_ref[2:3, :]
        conv = w0 * z2 + w1 * z1 + w2 * z
        dy = dy_ref[...]
        db_ref[...] = (dy * conv).astype(db_ref.dtype)
        dconv = dy * b_ref[...]
        dconv_n = jnp.where(i < n_i - 1, dyn_ref[...] * bn_ref[...], 0.0)
        dz = w2 * dconv + w1 * _shift_up(dconv, dconv_n, 1) + w0 * _shift_up(dconv, dconv_n, 2)
        dc_ref[...] = (dz * xv).astype(dc_ref.dtype)
        dx_ref[...] = (dz * cv).astype(dx_ref.dtype)
        dw_ref[0:1, :] += jnp.sum(dconv * z2, axis=0, keepdims=True)
        dw_ref[1:2, :] += jnp.sum(dconv * z1, axis=0, keepdims=True)
        dw_ref[2:3, :] += jnp.sum(dconv * z, axis=0, keepdims=True)

    wspec = pl.BlockSpec((3, 128), lambda j, i: (0, j))
    return pl.pallas_call(
        body, name=name, grid=(B_WIDTH // 128, n_i),
        in_specs=[tile(cb), tile(cc), tile(cx), tile(cdy), prev(cc), prev(cx), nxt(cb), nxt(cdy), wspec],
        out_specs=[tile(0), tile(0), tile(0), wspec],
        out_shape=[SDS((S, B_WIDTH), BF16)] * 3 + [SDS((3, B_WIDTH), F32)],
        compiler_params=_cp(2),
    )(p, p, p, dycat, p, p, p, dycat, w)


def _seg_sum(t, bd):
    hi = t.astype(BF16)
    lo = (t - hi.astype(F32)).astype(BF16)
    return jnp.dot(hi, bd, preferred_element_type=F32) + jnp.dot(lo, bd, preferred_element_type=F32)


def _head_norm(x, g, bd):
    rstd = lax.rsqrt(_seg_sum(x * x, bd) * (1.0 / HEAD_DIM) + EPS)
    xhat = x * rstd
    return xhat * g, xhat, rstd


def _head_norm_bwd(dy, g, xhat, rstd, bd):
    dxh = dy * g
    return rstd * (dxh - xhat * (_seg_sum(dxh * xhat, bd) * (1.0 / HEAD_DIM)))


def _band_mask(has_prev):
    row = lax.broadcasted_iota(jnp.int32, (BLK, 2 * BLK), 0)
    col = lax.broadcasted_iota(jnp.int32, (BLK, 2 * BLK), 1)
    first_key = jnp.where(has_prev, 0, BLK)
    return (col >= row) & (col <= row + BLK) & (col >= first_key)


def _first_of_segment(g, n, n_blocks):
    per_seg = lax.shift_right_logical(jnp.int32(n_blocks), 2 * g)
    return (n & (per_seg - 1)) == 0


def _attn_fwd(name, q3, k3, v3, gq, gk, bd):
    _, S, _ = q3.shape
    nblk = S // BLK
    nt = (((1,), (1,)), ((), ()))

    def body(q_ref, kc_ref, kp_ref, vc_ref, vp_ref, gq_ref, gk_ref, bd_ref, o_ref, lse_ref):
        g = pl.program_id(0)
        n = pl.program_id(1)
        has_prev = jnp.logical_not(_first_of_segment(g, n, nblk))
        bdv = bd_ref[...]
        qn, _, _ = _head_norm(q_ref[...], gq_ref[...], bdv)
        kn, _, _ = _head_norm(jnp.concatenate([kp_ref[...], kc_ref[...]], axis=0), gk_ref[...], bdv)
        knb = kn.astype(BF16)
        vb = jnp.concatenate([vp_ref[...], vc_ref[...]], axis=0).astype(BF16)
        band = _band_mask(has_prev)
        lane = lax.broadcasted_iota(jnp.int32, (1, PW), 1)
        o_acc = jnp.zeros((BLK, PW), F32)
        l_acc = jnp.zeros((BLK, PW), F32)
        for j in range(PW // HEAD_DIM):
            hm = (lane >= HEAD_DIM * j) & (lane < HEAD_DIM * (j + 1))
            qj = jnp.where(hm, qn, 0.0).astype(BF16)
            s = lax.dot_general(qj, knb, nt, preferred_element_type=F32) * (HEAD_DIM ** -0.5)
            s = jnp.where(band, s, NEG)
            m = jnp.max(s, axis=1, keepdims=True)
            e = jnp.exp(s - m)
            den = jnp.sum(e, axis=1, keepdims=True)
            pv = jnp.dot(e.astype(BF16), vb, preferred_element_type=F32)
            o_acc = jnp.where(hm, pv / den, o_acc)
            l_acc = jnp.where(hm, m + jnp.log(den), l_acc)
        o_ref[...] = o_acc
        lse_ref[...] = l_acc

    cur = pl.BlockSpec((None, BLK, PW), lambda g, n: (g, n, 0))
    prv = pl.BlockSpec((None, BLK, PW), lambda g, n: (g, jnp.maximum(n - 1, 0), 0))
    vec = pl.BlockSpec((1, PW), lambda g, n: (0, 0))
    return pl.pallas_call(
        body, name=name, grid=(N_PATTERNS, nblk),
        in_specs=[cur, cur, prv, cur, prv, vec, vec, pl.BlockSpec((PW, PW), lambda g, n: (0, 0))],
        out_specs=[cur, cur],
        out_shape=[SDS((N_PATTERNS, S, PW), F32)] * 2,
        compiler_params=_cp(2),
    )(q3, k3, k3, v3, v3, gq, gk, bd)


def _attn_bwd(name, q3, k3, v3, lse3, do3, c3, gq, gk, bd):
    _, S, _ = q3.shape
    nblk = S // BLK
    nt = (((1,), (1,)), ((), ()))
    tn = (((0,), (0,)), ((), ()))

    def body(q_ref, kc_ref, kp_ref, vc_ref, vp_ref, lse_ref, do_ref, c_ref, gq_ref, gk_ref, bd_ref,
             dq_ref, dk_ref, dv_ref, dgq_ref, dgk_ref, ck_ref, cv_ref):
        g = pl.program_id(0)
        n = pl.program_id(1)
        live = n < nblk
        ne = jnp.minimum(n, nblk - 1)
        has_prev = jnp.logical_not(_first_of_segment(g, ne, nblk))

        @pl.when(n == 0)
        def _():
            ck_ref[...] = jnp.zeros_like(ck_ref)
            cv_ref[...] = jnp.zeros_like(cv_ref)
            dgq_ref[...] = jnp.zeros_like(dgq_ref)
            dgk_ref[...] = jnp.zeros_like(dgk_ref)

        bdv = bd_ref[...]
        gqv = gq_ref[...]
        gkv = gk_ref[...]
        qn, qhat, qrstd = _head_norm(q_ref[...], gqv, bdv)
        kn, khat, krstd = _head_norm(jnp.concatenate([kp_ref[...], kc_ref[...]], axis=0), gkv, bdv)
        knb = kn.astype(BF16)
        vb = jnp.concatenate([vp_ref[...], vc_ref[...]], axis=0).astype(BF16)
        band = _band_mask(has_prev)
        lane = lax.broadcasted_iota(jnp.int32, (1, PW), 1)
        lse = lse_ref[...]
        do = do_ref[...]
        cc = c_ref[...]
        dqn = jnp.zeros((BLK, PW), F32)
        dkn = jnp.zeros((2 * BLK, PW), F32)
        dvv = jnp.zeros((2 * BLK, PW), F32)
        for j in range(PW // HEAD_DIM):
            hm = (lane >= HEAD_DIM * j) & (lane < HEAD_DIM * (j + 1))
            qj = jnp.where(hm, qn, 0.0).astype(BF16)
            doj = jnp.where(hm, do, 0.0).astype(BF16)
            s = lax.dot_general(qj, knb, nt, preferred_element_type=F32) * (HEAD_DIM ** -0.5)
            lse_j = jnp.max(jnp.where(hm, lse, NEG), axis=1, keepdims=True)
            c_j = jnp.max(jnp.where(hm, cc, NEG), axis=1, keepdims=True)
            prob = jnp.where(band, jnp.exp(s - lse_j), 0.0)
            dp = lax.dot_general(doj, vb, nt, preferred_element_type=F32)
            ds = (prob * (dp + c_j) * (HEAD_DIM ** -0.5)).astype(BF16)
            dqn = jnp.where(hm, jnp.dot(ds, knb, preferred_element_type=F32), dqn)
            dkn += lax.dot_general(ds, qj, tn, preferred_element_type=F32)
            dvv += lax.dot_general(prob.astype(BF16), doj, tn, preferred_element_type=F32)

        dq_ref[...] = _head_norm_bwd(dqn, gqv, qhat, qrstd, bdv).astype(dq_ref.dtype)
        dk2 = _head_norm_bwd(dkn, gkv, khat, krstd, bdv)
        keep = jnp.where(live, 1.0, 0.0)
        dgq_ref[...] += keep * jnp.sum(dqn * qhat, axis=0, keepdims=True)
        dgk_ref[...] += keep * jnp.sum(dkn * khat, axis=0, keepdims=True)
        dk_ref[...] = (ck_ref[...] + keep * dk2[:BLK]).astype(dk_ref.dtype)
        dv_ref[...] = (cv_ref[...] + keep * dvv[:BLK]).astype(dv_ref.dtype)
        ck_ref[...] = dk2[BLK:]
        cv_ref[...] = dvv[BLK:]

    last = nblk - 1
    cur = pl.BlockSpec((None, BLK, PW), lambda g, n: (g, jnp.minimum(n, last), 0))
    prv = pl.BlockSpec((None, BLK, PW), lambda g, n: (g, jnp.maximum(jnp.minimum(n, last) - 1, 0), 0))
    done = pl.BlockSpec((None, BLK, PW), lambda g, n: (g, jnp.maximum(n - 1, 0), 0))
    vec = pl.BlockSpec((1, PW), lambda g, n: (0, 0))
    gvec = pl.BlockSpec((None, 1, PW), lambda g, n: (g, 0, 0))
    return pl.pallas_call(
        body, name=name, grid=(N_PATTERNS, nblk + 1),
        in_specs=[cur, cur, prv, cur, prv, cur, cur, cur, vec, vec, pl.BlockSpec((PW, PW), lambda g, n: (0, 0))],
        out_specs=[cur, done, done, gvec, gvec],
        out_shape=[SDS((N_PATTERNS, S, PW), BF16)] * 3 + [SDS((N_PATTERNS, 1, PW), F32)] * 2,
        scratch_shapes=[pltpu.VMEM((BLK, PW), F32), pltpu.VMEM((BLK, PW), F32)],
        compiler_params=_cp(2),
    )(q3, k3, k3, v3, v3, lse3, do3, c3, gq, gk, bd)


def _mix_fwd(name, o3, lse3):
    _, S, _ = o3.shape
    tm = min(512, S)

    def body(o_ref, l_ref, y_ref):
        l = [l_ref[g] for g in range(N_PATTERNS)]
        m = jnp.maximum(jnp.maximum(l[0], l[1]), l[2])
        e = [jnp.exp(t - m) for t in l]
        inv = 1.0 / (e[0] + e[1] + e[2])
        for g in range(N_PATTERNS):
            y_ref[:, PW * g:PW * (g + 1)] = (o_ref[g] * (e[g] * inv)).astype(y_ref.dtype)

    blk3 = pl.BlockSpec((N_PATTERNS, tm, PW), lambda i: (0, i, 0))
    return pl.pallas_call(
        body, name=name, grid=(S // tm,),
        in_specs=[blk3, blk3],
        out_specs=pl.BlockSpec((tm, C_WIDTH), lambda i: (i, 0)),
        out_shape=SDS((S, C_WIDTH), BF16),
        compiler_params=_cp(1),
    )(o3, lse3)


def _mix_bwd(name, o3, lse3, dycat, bd):
    _, S, _ = o3.shape
    tm = min(512, S)
    c0 = (A_WIDTH + B_WIDTH) // PW

    def body(o_ref, l_ref, dy0_ref, dy1_ref, dy2_ref, bd_ref, do_ref, c_ref):
        bdv = bd_ref[...]
        dys = [dy0_ref[...], dy1_ref[...], dy2_ref[...]]
        l = [l_ref[g] for g in range(N_PATTERNS)]
        m = jnp.maximum(jnp.maximum(l[0], l[1]), l[2])
        e = [jnp.exp(t - m) for t in l]
        inv = 1.0 / (e[0] + e[1] + e[2])
        alpha = [t * inv for t in e]
        da = [_seg_sum(dys[g] * o_ref[g], bdv) for g in range(N_PATTERNS)]
        mean_da = alpha[0] * da[0] + alpha[1] * da[1] + alpha[2] * da[2]
        for g in range(N_PATTERNS):
            do_ref[g] = dys[g] * alpha[g]
            c_ref[g] = -alpha[g] * mean_da

    blk3 = pl.BlockSpec((N_PATTERNS, tm, PW), lambda i: (0, i, 0))
    dyspec = lambda g: pl.BlockSpec((tm, PW), lambda i: (i, c0 + g))
    return pl.pallas_call(
        body, name=name, grid=(S // tm,),
        in_specs=[blk3, blk3, dyspec(0), dyspec(1), dyspec(2), pl.BlockSpec((PW, PW), lambda i: (0, 0))],
        out_specs=[blk3, blk3],
        out_shape=[SDS((N_PATTERNS, S, PW), F32)] * 2,
        compiler_params=_cp(1),
    )(o3, lse3, dycat, dycat, dycat, bd)


def _mesh_pos():
    x, y, c = lax.axis_index("x"), lax.axis_index("y"), lax.axis_index("c")
    chips = [(1 - x, y), (x, 1 - y), (1 - x, 1 - y)]
    chip_idx = [2 * cx + cy for cx, cy in chips]
    return x, y, c, 2 * x + y, chips, chip_idx


def _place_shard(name, w, layer, chip_arr, out_dtype):
    _, R, C = w.shape
    tr = min(256, R)

    def body(chip_ref, w_ref, o_ref):
        o_ref[...] = w_ref[...].astype(o_ref.dtype)

    return pl.pallas_call(
        body, name=name,
        grid_spec=pltpu.PrefetchScalarGridSpec(
            num_scalar_prefetch=1, grid=(R // tr,),
            in_specs=[pl.BlockSpec((None, tr, C), lambda i, chip_ref: (layer, i, 0))],
            out_specs=pl.BlockSpec((None, tr, C), lambda i, chip_ref: (chip_ref[0], i, 0))),
        out_shape=SDS((N_CHIPS, R, C), out_dtype),
        compiler_params=_cp(1),
    )(chip_arr, w)


HBM_SPEC = pl.BlockSpec(memory_space=pltpu.HBM)
SEM_SPEC = pl.BlockSpec(memory_space=pltpu.SEMAPHORE)
SPLIT_COPY = pltpu.SideEffectType.DATAFLOW_SIDE_EFFECTING
N_PEER_CHIPS = N_CHIPS - 1
TOKEN_SHAPE = SDS((8, 128), F32)
TOKEN_SPEC = pl.BlockSpec(memory_space=pltpu.VMEM)


def _in_hbm(a):
    return pltpu.with_memory_space_constraint(a, pltpu.HBM)


def _gather_start(bufs):
    T = len(bufs)

    def body(*refs):
        ins = refs[:T]
        send_sems, recv_sems = refs[T:2 * T], refs[2 * T:3 * T]
        x, y, c, me, chips, chip_idx = _mesh_pos()
        for t in range(T):
            hr = ins[t].shape[1] // 2
            mine = ins[t].at[me, pl.ds(c * hr, hr), :]
            for j in range(N_PEER_CHIPS):
                pltpu.make_async_remote_copy(src_ref=mine, dst_ref=mine, send_sem=send_sems[t].at[j],
                                             recv_sem=recv_sems[t].at[j], device_id=(*chips[j], c),
                                             device_id_type=MESH).start()

    sems = [pltpu.SemaphoreType.DMA((N_PEER_CHIPS,))] * T
    out = pl.pallas_call(
        body, name="gather_start",
        in_specs=[HBM_SPEC] * T,
        out_specs=[SEM_SPEC] * (2 * T) + [HBM_SPEC] * T,
        out_shape=sems + sems + [pltpu.HBM(b.shape, b.dtype) for b in bufs],
        input_output_aliases={t: 2 * T + t for t in range(T)},
        compiler_params=pltpu.CompilerParams(has_side_effects=SPLIT_COPY),
    )(*[_in_hbm(b) for b in bufs])
    return out[:T], out[T:2 * T], out[2 * T:]


def _gather_wait(name, buf, send_sem, recv_sem, after):
    n_in = 3 if after is None else 4

    def body(*refs):
        buf_ref, ssem, rsem = refs[:3]
        x, y, c, me, chips, chip_idx = _mesh_pos()
        hr = buf_ref.shape[1] // 2
        mine = buf_ref.at[me, pl.ds(c * hr, hr), :]
        for j in range(N_PEER_CHIPS):
            got = buf_ref.at[chip_idx[j], pl.ds(c * hr, hr), :]
            cp = pltpu.make_async_remote_copy(src_ref=mine, dst_ref=got, send_sem=ssem.at[j], recv_sem=rsem.at[j],
                                              device_id=(*chips[j], c), device_id_type=MESH)
            cp.wait_send()
            cp.wait_recv()

    args = [buf, send_sem, recv_sem] + ([] if after is None else [after])
    return pl.pallas_call(
        body, name=name,
        in_specs=[HBM_SPEC, SEM_SPEC, SEM_SPEC] + [_hbm_spec()] * (n_in - 3),
        out_specs=HBM_SPEC,
        out_shape=pltpu.HBM(buf.shape, buf.dtype),
        input_output_aliases={0: 0},
        compiler_params=pltpu.CompilerParams(has_side_effects=SPLIT_COPY),
    )(*args)


def _gather_forward(name, buf):
    def body(in_ref, out_ref, send_sems, recv_sems):
        x, y, c, me, chips, chip_idx = _mesh_pos()
        hr = out_ref.shape[1] // 2
        cps = []
        for j in range(N_PEER_CHIPS):
            got = out_ref.at[chip_idx[j], pl.ds(c * hr, hr), :]
            cp = pltpu.make_async_remote_copy(src_ref=got, dst_ref=got, send_sem=send_sems.at[j],
                                              recv_sem=recv_sems.at[j], device_id=(x, y, 1 - c), device_id_type=MESH)
            cp.start()
            cps.append(cp)
        for j in range(N_PEER_CHIPS):
            theirs = out_ref.at[chip_idx[j], pl.ds((1 - c) * hr, hr), :]
            pltpu.make_async_remote_copy(src_ref=theirs, dst_ref=theirs, send_sem=send_sems.at[j],
                                         recv_sem=recv_sems.at[j], device_id=(x, y, 1 - c),
                                         device_id_type=MESH).wait_recv()
        for cp in cps:
            cp.wait_send()

    return pl.pallas_call(
        body, name=name,
        in_specs=[_hbm_spec()], out_specs=_hbm_spec(),
        out_shape=SDS(buf.shape, buf.dtype),
        input_output_aliases={0: 0},
        scratch_shapes=[pltpu.SemaphoreType.DMA((N_PEER_CHIPS,)), pltpu.SemaphoreType.DMA((N_PEER_CHIPS,))],
    )(buf)


class _GatheredWeights:
    def __init__(self, keys, bufs):
        send_sems, recv_sems, thru = _gather_start(bufs)
        self._pending = {k: (b, s, r) for k, b, s, r in zip(keys, thru, send_sems, recv_sems)}
        self._ready = {}

    def get(self, name, layer, after=None):
        key = (name, layer)
        if key not in self._ready:
            buf, ssem, rsem = self._pending.pop(key)
            buf = _gather_wait(f"gather_wait_{name}_{layer}", buf, ssem, rsem, after)
            self._ready[key] = _gather_forward(f"gather_fwd_{name}_{layer}", buf)
        return self._ready[key]


def _swap_copy(g_ref, land_ref, send_sem, recv_sem):
    x, y, c, _, _, _ = _mesh_pos()
    hr = g_ref.shape[1] // 2
    return pltpu.make_async_remote_copy(src_ref=g_ref.at[:, pl.ds((1 - c) * hr, hr), :], dst_ref=land_ref,
                                        send_sem=send_sem, recv_sem=recv_sem, device_id=(x, y, 1 - c),
                                        device_id_type=MESH)


def _swap_start(name, g):
    land_shape = (g.shape[0], g.shape[1] // 2, g.shape[2])

    def body(g_ref, land_ref, send_sem, recv_sem, land_thru, token):
        _swap_copy(g_ref, land_ref, send_sem, recv_sem).start()
        token[...] = jnp.zeros_like(token)

    return pl.pallas_call(
        body, name=name,
        in_specs=[HBM_SPEC, HBM_SPEC],
        out_specs=[SEM_SPEC, SEM_SPEC, HBM_SPEC, TOKEN_SPEC],
        out_shape=[pltpu.SemaphoreType.DMA(()), pltpu.SemaphoreType.DMA(()), pltpu.HBM(land_shape, g.dtype),
                   TOKEN_SHAPE],
        input_output_aliases={1: 2},
        compiler_params=pltpu.CompilerParams(has_side_effects=SPLIT_COPY),
    )(_in_hbm(g), _in_hbm(lax.empty(land_shape, g.dtype)))


def _swap_wait(name, g, land, send_sem, recv_sem, after):
    def body(g_ref, land_ref, send_sem, recv_sem, after_ref, land_out):
        cp = _swap_copy(g_ref, land_ref, send_sem, recv_sem)
        cp.wait_send()
        cp.wait_recv()

    return pl.pallas_call(
        body, name=name,
        in_specs=[HBM_SPEC, HBM_SPEC, SEM_SPEC, SEM_SPEC, _hbm_spec()],
        out_specs=HBM_SPEC,
        out_shape=pltpu.HBM(land.shape, land.dtype),
        input_output_aliases={1: 0},
        compiler_params=pltpu.CompilerParams(has_side_effects=SPLIT_COPY),
    )(_in_hbm(g), land, send_sem, recv_sem, after)


def _add_my_half(name, g, r, c_arr):
    ns, R, C = g.shape
    hr = R // 2
    tr = min(256, hr)
    nt = hr // tr

    def body(c_ref, g_ref, r_ref, o_ref, land_ref):
        t = (g_ref[...] + r_ref[...]).astype(o_ref.dtype)
        o_ref[...] = t
        land_ref[...] = t

    out = pl.BlockSpec((None, tr, C), lambda s, i, c_ref: (s, i, 0))
    return pl.pallas_call(
        body, name=name,
        grid_spec=pltpu.PrefetchScalarGridSpec(
            num_scalar_prefetch=1, grid=(ns, nt),
            in_specs=[pl.BlockSpec((None, tr, C), lambda s, i, c_ref: (s, c_ref[0] * nt + i, 0)), out],
            out_specs=[out, out]),
        out_shape=[SDS((ns, hr, C), BF16)] * 2,
        compiler_params=_cp(2),
    )(c_arr, g, r)


def _exchange_start(name, part, land):
    def body(part_ref, land_ref, send_sems, recv_sems, land_thru, token):
        x, y, c, me, chips, chip_idx = _mesh_pos()
        for j in range(N_PEER_CHIPS):
            pltpu.make_async_remote_copy(src_ref=part_ref.at[chip_idx[j]], dst_ref=land_ref.at[me],
                                         send_sem=send_sems.at[j], recv_sem=recv_sems.at[j],
                                         device_id=(*chips[j], c), device_id_type=MESH).start()
        token[...] = jnp.zeros_like(token)

    sems = pltpu.SemaphoreType.DMA((N_PEER_CHIPS,))
    return pl.pallas_call(
        body, name=name,
        in_specs=[HBM_SPEC, HBM_SPEC],
        out_specs=[SEM_SPEC, SEM_SPEC, HBM_SPEC, TOKEN_SPEC],
        out_shape=[sems, sems, pltpu.HBM(land.shape, land.dtype), TOKEN_SHAPE],
        input_output_aliases={1: 2},
        compiler_params=pltpu.CompilerParams(has_side_effects=SPLIT_COPY),
    )(_in_hbm(part), _in_hbm(land))


def _exchange_wait(name, part, land, send_sems, recv_sems, after):
    def body(part_ref, land_ref, send_sems, recv_sems, after_ref, land_out):
        x, y, c, me, chips, chip_idx = _mesh_pos()
        for j in range(N_PEER_CHIPS):
            cp = pltpu.make_async_remote_copy(src_ref=part_ref.at[chip_idx[j]], dst_ref=land_ref.at[chip_idx[j]],
                                              send_sem=send_sems.at[j], recv_sem=recv_sems.at[j],
                                              device_id=(*chips[j], c), device_id_type=MESH)
            cp.wait_send()
            cp.wait_recv()

    return pl.pallas_call(
        body, name=name,
        in_specs=[HBM_SPEC, HBM_SPEC, SEM_SPEC, SEM_SPEC, _hbm_spec()],
        out_specs=HBM_SPEC,
        out_shape=pltpu.HBM(land.shape, land.dtype),
        input_output_aliases={1: 0},
        compiler_params=pltpu.CompilerParams(has_side_effects=SPLIT_COPY),
    )(_in_hbm(part), land, send_sems, recv_sems, after)


class _GradReducer:
    def __init__(self, c_arr):
        self._c_arr = c_arr
        self._swapping = []
        self._exchanging = {}
        self._tokens = []

    def begin(self, name, layer, g):
        tag = f"{name}_{layer}"
        ssem, rsem, land, token = _swap_start(f"rs_swap_start_{tag}", g)
        self._swapping.append((name, layer, g, ssem, rsem, land))
        self._tokens.append(token)

    def advance(self, after):
        for name, layer, g, ssem, rsem, land in self._swapping:
            tag = f"{name}_{layer}"
            theirs = _swap_wait(f"rs_swap_wait_{tag}", g, land, ssem, rsem, after)
            part, own = _add_my_half(f"rs_add_{tag}", g, theirs, self._c_arr)
            ssems, rsems, land2, token = _exchange_start(f"rs_xchg_start_{tag}", part, own)
            self._exchanging[(name, layer)] = (part, ssems, rsems, land2)
            self._tokens.append(token)
        self._swapping = []

    def deps(self):
        tokens, self._tokens = self._tokens, []
        return tokens

    def finish(self, names, n_layers, after):
        bufs = []
        for name in names:
            buf = None
            for layer in range(n_layers):
                part, ssems, rsems, land = self._exchanging.pop((name, layer))
                tag = f"{name}_{layer}"
                landed = _exchange_wait(f"rs_xchg_wait_{tag}", part, land, ssems, rsems, after)
                buf = _sum_chips(f"rs_sum_{tag}", landed, self._c_arr, layer, n_layers, buf)
            bufs.append(buf)
        return dict(zip(names, _join_halves(f"rs_join_{names[0]}", bufs)))


def _sum_chips(name, r, c_arr, layer, n_layers, prev):
    ns, H, C = r.shape
    tr = min(256, H)
    nt = H // tr

    def body(c_ref, r_ref, *rest):
        o_ref = rest[-1]
        o_ref[...] = ((r_ref[0].astype(F32) + r_ref[1].astype(F32)) + r_ref[2].astype(F32)) + r_ref[3].astype(F32)

    in_specs = [pl.BlockSpec((ns, tr, C), lambda i, c_ref: (0, i, 0))]
    args = [c_arr, r]
    aliases = {}
    if prev is not None:
        in_specs.append(_hbm_spec())
        args.append(prev)
        aliases = {2: 0}
    return pl.pallas_call(
        body, name=name,
        grid_spec=pltpu.PrefetchScalarGridSpec(
            num_scalar_prefetch=1, grid=(nt,), in_specs=in_specs,
            out_specs=pl.BlockSpec((None, tr, C), lambda i, c_ref: (layer, c_ref[0] * nt + i, 0))),
        out_shape=SDS((n_layers, 2 * H, C), F32),
        input_output_aliases=aliases,
        compiler_params=_cp(1),
    )(*args)


def _join_halves(name, bufs):
    T = len(bufs)

    def body(*refs):
        outs = refs[T:2 * T]
        send_sems, recv_sems = refs[2 * T:]
        x, y, c, _, _, _ = _mesh_pos()
        cps = []
        for t in range(T):
            hr = outs[t].shape[1] // 2
            mine = outs[t].at[:, pl.ds(c * hr, hr), :]
            cp = pltpu.make_async_remote_copy(src_ref=mine, dst_ref=mine, send_sem=send_sems.at[t],
                                              recv_sem=recv_sems.at[t], device_id=(x, y, 1 - c), device_id_type=MESH)
            cp.start()
            cps.append(cp)
        for t in range(T):
            hr = outs[t].shape[1] // 2
            theirs = outs[t].at[:, pl.ds((1 - c) * hr, hr), :]
            pltpu.make_async_remote_copy(src_ref=theirs, dst_ref=theirs, send_sem=send_sems.at[t],
                                         recv_sem=recv_sems.at[t], device_id=(x, y, 1 - c),
                                         device_id_type=MESH).wait_recv()
        for cp in cps:
            cp.wait_send()

    return pl.pallas_call(
        body, name=name,
        in_specs=[_hbm_spec()] * T, out_specs=[_hbm_spec()] * T,
        out_shape=[SDS(b.shape, b.dtype) for b in bufs],
        input_output_aliases={t: t for t in range(T)},
        scratch_shapes=[pltpu.SemaphoreType.DMA((T,)), pltpu.SemaphoreType.DMA((T,))],
    )(*bufs)


def _allreduce_small(buf):
    R, C = buf.shape

    def body(in_ref, out_ref, land_ref, send_sems, recv_sems):
        x, y, c = lax.axis_index("x"), lax.axis_index("y"), lax.axis_index("c")
        me = 4 * x + 2 * y + c
        land_ref[me] = in_ref[...]
        cps = []
        for k in range(1, N_DEV):
            kx, ky, kc = (k >> 2) & 1, (k >> 1) & 1, k & 1
            peer = (x ^ kx, y ^ ky, c ^ kc)
            cp = pltpu.make_async_remote_copy(src_ref=in_ref, dst_ref=land_ref.at[me],
                                              send_sem=send_sems.at[k - 1], recv_sem=recv_sems.at[k - 1],
                                              device_id=peer, device_id_type=MESH)
            cp.start()
            cps.append(cp)
        for k in range(1, N_DEV):
            src = me ^ k
            slot = land_ref.at[src]
            pltpu.make_async_remote_copy(src_ref=slot, dst_ref=slot, send_sem=send_sems.at[k - 1],
                                         recv_sem=recv_sems.at[k - 1], device_id=(x, y, c),
                                         device_id_type=MESH).wait_recv()
        for cp in cps:
            cp.wait_send()
        acc = land_ref[0]
        for d in range(1, N_DEV):
            acc = acc + land_ref[d]
        out_ref[...] = acc

    return pl.pallas_call(
        body, name="allreduce_small",
        in_specs=[pl.BlockSpec(memory_space=pltpu.VMEM)],
        out_specs=pl.BlockSpec(memory_space=pltpu.VMEM),
        out_shape=SDS((R, C), buf.dtype),
        scratch_shapes=[pltpu.VMEM((N_DEV, R, C), buf.dtype),
                        pltpu.SemaphoreType.DMA((N_DEV - 1,)), pltpu.SemaphoreType.DMA((N_DEV - 1,))],
        compiler_params=pltpu.CompilerParams(vmem_limit_bytes=V7X_VMEM_LIMIT),
    )(buf)


def _deinterleave(t, d):
    if d == 1:
        return t
    S, W = t.shape
    return t.reshape(S // d, d, W).transpose(1, 0, 2).reshape(S, W)


def _interleave(t, d):
    if d == 1:
        return t
    S, W = t.shape
    return t.reshape(d, S // d, W).transpose(1, 0, 2).reshape(S, W)


def _to_patterns(t, off):
    return jnp.stack([_deinterleave(t[:, off + PW * g:off + PW * (g + 1)], PATTERN_DILATION[g])
                      for g in range(N_PATTERNS)])


def _from_patterns(t3):
    return jnp.stack([_interleave(t3[g], PATTERN_DILATION[g]) for g in range(N_PATTERNS)])


def _pack_rows(vectors):
    flat = jnp.concatenate([v.reshape(-1) for v in vectors])
    n = flat.shape[0]
    padded = -(-n // 1024) * 1024
    return jnp.pad(flat, (0, padded - n)).reshape(padded // 128, 128)


def _unpack_rows(buf, shapes):
    flat = buf.reshape(-1)
    out, off = [], 0
    for s in shapes:
        n = 1
        for dim in s:
            n *= dim
        out.append(flat[off:off + n].reshape(s))
        off += n
    return out


def _layer_forward(l, x, prm, wg):
    S, D = x.shape
    p, h = _norm_matmul(f"in_proj_{l}", x, prm["attn_norm"][l], wg.get("w_in", l, x), F32)
    y_a = _sgu_fwd(f"sgu_fwd_{l}", p, prm["sgu_wt"][l], prm["sgu_bb"][l])
    y_b = _conv_fwd(f"conv_fwd_{l}", p, prm["conv_w"][l])
    q3, k3, v3 = _to_patterns(p, OFF_Q), _to_patterns(p, OFF_K), _to_patterns(p, OFF_V)
    o3d, lse3d = _attn_fwd(f"attn_fwd_{l}", q3, k3, v3, prm["q_gain"][l], prm["k_gain"][l], prm["bd"])
    o3, lse3 = _from_patterns(o3d), _from_patterns(lse3d)
    y_c = _mix_fwd(f"mix_fwd_{l}", o3, lse3)
    ycat = jnp.concatenate([y_a, y_b, y_c], axis=1)
    tm = min(512, S)
    w_out = wg.get("w_out", l, ycat)
    rq = w_out.shape[1]
    x1 = _matmul(
        f"out_proj_{l}", ycat, w_out, (S, D), F32, grid=(S // tm, 1, N_CHIPS),
        a_spec=pl.BlockSpec((tm, rq), lambda i, j, k: (i, k)),
        b_spec=pl.BlockSpec((None, rq, D), lambda i, j, k: (k, 0, 0)),
        o_spec=pl.BlockSpec((tm, D), lambda i, j, k: (i, 0)),
        contract=(1, 0), acc_shape=(tm, D),
        extras=(x,), extra_specs=(pl.BlockSpec((tm, D), lambda i, j, k: (i, 0)),),
        epi=lambda r, res: r + res)
    a, h2 = _norm_matmul(f"mlp_in_{l}", x1, prm["mlp_norm"][l], wg.get("w_mlp_in", l, x1), BF16)
    w_mlp_out = wg.get("w_mlp_out", l, a)
    dff4 = w_mlp_out.shape[1]
    tk = min(1024, dff4)
    kpc = dff4 // tk
    x2 = _matmul(
        f"mlp_out_{l}", a, w_mlp_out, (S, D), F32, grid=(S // tm, 1, N_CHIPS * kpc),
        a_spec=pl.BlockSpec((tm, tk), lambda i, j, k: (i, k)),
        b_spec=pl.BlockSpec((None, tk, D), lambda i, j, k: (k // kpc, k % kpc, 0)),
        o_spec=pl.BlockSpec((tm, D), lambda i, j, k: (i, 0)),
        contract=(1, 0), acc_shape=(tm, D), a_pre=_relu2_bf16,
        extras=(x1,), extra_specs=(pl.BlockSpec((tm, D), lambda i, j, k: (i, 0)),),
        epi=lambda r, res: r + res)
    saved = dict(x=x, p=p, h=h, q3=q3, k3=k3, v3=v3, o3=o3, lse3=lse3, lse3d=lse3d, ycat=ycat, x1=x1, a=a, h2=h2)
    return x2, saved


def _layer_backward(l, dx2, dx2b, sv, prm, wg, sink):
    S, D = dx2.shape
    w_in, w_out = wg.get("w_in", l), wg.get("w_out", l)
    w_mlp_in, w_mlp_out = wg.get("w_mlp_in", l), wg.get("w_mlp_out", l)
    dff4 = w_mlp_in.shape[-1]
    dff = N_CHIPS * dff4
    tm = min(512, S)
    tk = min(1024, S)
    nks = S // tk

    da = _matmul(
        f"mlp_out_bwd_{l}", dx2b, w_mlp_out, (S, dff), BF16, grid=(S // tm, N_CHIPS, 1),
        a_spec=pl.BlockSpec((tm, D), lambda i, j, k: (i, 0)),
        b_spec=pl.BlockSpec((None, dff4, D), lambda i, j, k: (j, 0, 0)),
        o_spec=pl.BlockSpec((tm, dff4), lambda i, j, k: (i, j)),
        contract=(1, 1), acc_shape=(tm, dff4),
        extras=(sv["a"],), extra_specs=(pl.BlockSpec((tm, dff4), lambda i, j, k: (i, j)),),
        epi=lambda r, act: r * (2.0 * jnp.maximum(act.astype(F32), 0.0)), deps=sink.deps())
    tmw = min(1024, dff4)
    mpc = dff4 // tmw
    g_w2 = _matmul(
        f"mlp_out_dw_{l}", sv["a"], dx2b, (N_CHIPS, dff4, D), F32, grid=(N_CHIPS * mpc, 1, nks),
        a_spec=pl.BlockSpec((tk, tmw), lambda i, j, k: (k, i)),
        b_spec=pl.BlockSpec((tk, D), lambda i, j, k: (k, 0)),
        o_spec=pl.BlockSpec((None, tmw, D), lambda i, j, k: (i // mpc, i % mpc, 0)),
        contract=(0, 0), acc_shape=(tmw, D), a_pre=_relu2_bf16)
    sink.begin("w_mlp_out", l, g_w2)
    dh2 = _matmul(
        f"mlp_in_bwd_{l}", da, w_mlp_in, (S, D), F32, grid=(S // tm, 1, N_CHIPS),
        a_spec=pl.BlockSpec((tm, dff4), lambda i, j, k: (i, k)),
        b_spec=pl.BlockSpec((None, D, dff4), lambda i, j, k: (k, 0, 0)),
        o_spec=pl.BlockSpec((tm, D), lambda i, j, k: (i, 0)),
        contract=(1, 1), acc_shape=(tm, D), deps=sink.deps())
    sink.advance(dh2)
    tmd = min(1024, D)
    g_w1 = _matmul(
        f"mlp_in_dw_{l}", sv["h2"], da, (N_CHIPS, D, dff4), F32, grid=(N_CHIPS, D // tmd, nks),
        a_spec=pl.BlockSpec((tk, tmd), lambda i, j, k: (k, j)),
        b_spec=pl.BlockSpec((tk, dff4), lambda i, j, k: (k, i)),
        o_spec=pl.BlockSpec((None, tmd, dff4), lambda i, j, k: (i, j, 0)),
        contract=(0, 0), acc_shape=(tmd, dff4))
    sink.begin("w_mlp_in", l, g_w1)
    dx1, dx1b, g_mlp_norm = _rmsnorm_bwd(f"mlp_norm_bwd_{l}", dh2, sv["x1"], prm["mlp_norm"][l], dx2,
                                         deps=sink.deps())

    rq = w_out.shape[1]
    dycat = _matmul(
        f"out_proj_bwd_{l}", dx1b, w_out, (S, N_CHIPS * rq), F32, grid=(S // tm, N_CHIPS, 1),
        a_spec=pl.BlockSpec((tm, D), lambda i, j, k: (i, 0)),
        b_spec=pl.BlockSpec((None, rq, D), lambda i, j, k: (j, 0, 0)),
        o_spec=pl.BlockSpec((tm, rq), lambda i, j, k: (i, j)),
        contract=(1, 1), acc_shape=(tm, rq))
    sink.advance(dycat)
    g_wout = _matmul(
        f"out_proj_dw_{l}", sv["ycat"], dx1b, (N_CHIPS, rq, D), F32, grid=(N_CHIPS, 1, nks),
        a_spec=pl.BlockSpec((tk, rq), lambda i, j, k: (k, i)),
        b_spec=pl.BlockSpec((tk, D), lambda i, j, k: (k, 0)),
        o_spec=pl.BlockSpec((None, rq, D), lambda i, j, k: (i, 0, 0)),
        contract=(0, 0), acc_shape=(rq, D))
    sink.begin("w_out", l, g_wout)

    p = sv["p"]
    du, dv_a, g_sgu_w, db_lanes = _sgu_bwd(f"sgu_bwd_{l}", p, dycat, prm["sgu_wt"][l], prm["sgu_wtt"][l],
                                           prm["sgu_bb"][l])
    sink.advance(du)
    g_sgu_b = db_lanes[:, :A_HEADS].T
    db, dc, dxb, g_conv = _conv_bwd(f"conv_bwd_{l}", p, dycat, prm["conv_w"][l])
    do3, c3 = _mix_bwd(f"mix_bwd_{l}", sv["o3"], sv["lse3"], dycat, prm["bd"])
    do3d = jnp.stack([_deinterleave(do3[g], PATTERN_DILATION[g]) for g in range(N_PATTERNS)])
    c3d = jnp.stack([_deinterleave(c3[g], PATTERN_DILATION[g]) for g in range(N_PATTERNS)])
    dq3, dk3, dv3, dgq, dgk = _attn_bwd(f"attn_bwd_{l}", sv["q3"], sv["k3"], sv["v3"], sv["lse3d"], do3d, c3d,
                                        prm["q_gain"][l], prm["k_gain"][l], prm["bd"])
    g_q = dgq.reshape(N_PATTERNS * PW // HEAD_DIM, HEAD_DIM).sum(axis=0)
    g_k = dgk.reshape(N_PATTERNS * PW // HEAD_DIM, HEAD_DIM).sum(axis=0)
    nat = lambda t3: jnp.concatenate([_interleave(t3[g], PATTERN_DILATION[g]) for g in range(N_PATTERNS)], axis=1)
    dp = jnp.concatenate([du, dv_a, db, dc, dxb, nat(dq3), nat(dk3), nat(dv3)], axis=1)

    ns_in = w_in.shape[-1]
    dh = _matmul(
        f"in_proj_bwd_{l}", dp, w_in, (S, D), F32, grid=(S // tm, 1, N_CHIPS),
        a_spec=pl.BlockSpec((tm, ns_in), lambda i, j, k: (i, k)),
        b_spec=pl.BlockSpec((None, D, ns_in), lambda i, j, k: (k, 0, 0)),
        o_spec=pl.BlockSpec((tm, D), lambda i, j, k: (i, 0)),
        contract=(1, 1), acc_shape=(tm, D), deps=sink.deps())
    g_win = _matmul(
        f"in_proj_dw_{l}", sv["h"], dp, (N_CHIPS, D, ns_in), F32, grid=(N_CHIPS, D // tmd, nks),
        a_spec=pl.BlockSpec((tk, tmd), lambda i, j, k: (k, j)),
        b_spec=pl.BlockSpec((tk, ns_in), lambda i, j, k: (k, i)),
        o_spec=pl.BlockSpec((None, tmd, ns_in), lambda i, j, k: (i, j, 0)),
        contract=(0, 0), acc_shape=(tmd, ns_in))
    sink.begin("w_in", l, g_win)
    dx0, dx0b, g_attn_norm = _rmsnorm_bwd(f"attn_norm_bwd_{l}", dh, sv["x"], prm["attn_norm"][l], dx1,
                                          deps=sink.deps())
    sink.advance(dx0)

    big = dict(w_in=g_win, w_out=g_wout, w_mlp_in=g_w1, w_mlp_out=g_w2)
    small = dict(attn_norm=g_attn_norm.reshape(-1), sgu_w=g_sgu_w, sgu_b=g_sgu_b, conv_w=g_conv,
                 q_norm=g_q, k_norm=g_k, mlp_norm=g_mlp_norm.reshape(-1))
    return dx0, dx0b, big, small


BIG = ("w_in", "w_out", "w_mlp_in", "w_mlp_out")
SMALL_REPLICATED = ("attn_norm", "sgu_w", "sgu_b", "q_norm", "k_norm", "mlp_norm")


def _local_step(x, target, prm, wg, n_layers, sink):
    saved = []
    h = x
    for l in range(n_layers):
        h, sv = _layer_forward(l, h, prm, wg)
        saved.append(sv)
    dy, dyb, colsq = _loss_kernel(h, target)
    loss = 0.5 * jnp.sum(colsq) / x.shape[1]
    bigs, smalls = [None] * n_layers, [None] * n_layers
    for l in reversed(range(n_layers)):
        dy, dyb, bigs[l], smalls[l] = _layer_backward(l, dy, dyb, saved[l], prm, wg, sink)
    return loss, dy, bigs, smalls


def _prepare_params(attn_norm, sgu_w, sgu_b, conv_full, q_norm, k_norm, mlp_norm):
    n_layers = attn_norm.shape[0]
    tri = jnp.tril(sgu_w)
    idx = jnp.arange(PW)
    bd = (idx[:, None] // HEAD_DIM == idx[None, :] // HEAD_DIM).astype(BF16)
    return dict(
        attn_norm=[attn_norm[l][None, :] for l in range(n_layers)],
        mlp_norm=[mlp_norm[l][None, :] for l in range(n_layers)],
        sgu_wt=[tri[l].astype(BF16) for l in range(n_layers)],
        sgu_wtt=[tri[l].transpose(0, 2, 1).astype(BF16) for l in range(n_layers)],
        sgu_bb=[jnp.repeat(sgu_b[l].T, HEAD_DIM, axis=1) for l in range(n_layers)],
        conv_w=[conv_full[l] for l in range(n_layers)],
        q_gain=[jnp.tile(q_norm[l], PW // HEAD_DIM)[None, :] for l in range(n_layers)],
        k_gain=[jnp.tile(k_norm[l], PW // HEAD_DIM)[None, :] for l in range(n_layers)],
        bd=bd,
    )


def kernel(x, attn_norm, w_in, sgu_w, sgu_b, conv_w, q_norm, k_norm, w_out, mlp_norm, w_mlp_in, w_mlp_out, loss_target, m_attn_norm, m_w_in, m_sgu_w, m_sgu_b, m_conv_w, m_q_norm, m_k_norm, m_w_out, m_mlp_norm, m_w_mlp_in, m_w_mlp_out, v_attn_norm, v_w_in, v_sgu_w, v_sgu_b, v_conv_w, v_q_norm, v_k_norm, v_w_out, v_mlp_norm, v_w_mlp_in, v_w_mlp_out):
    n_layers = attn_norm.shape[0]
    weights = dict(attn_norm=attn_norm, w_in=w_in, sgu_w=sgu_w, sgu_b=sgu_b, conv_w=conv_w, q_norm=q_norm,
                   k_norm=k_norm, w_out=w_out, mlp_norm=mlp_norm, w_mlp_in=w_mlp_in, w_mlp_out=w_mlp_out)
    mom_m = dict(attn_norm=m_attn_norm, w_in=m_w_in, sgu_w=m_sgu_w, sgu_b=m_sgu_b, conv_w=m_conv_w,
                 q_norm=m_q_norm, k_norm=m_k_norm, w_out=m_w_out, mlp_norm=m_mlp_norm, w_mlp_in=m_w_mlp_in,
                 w_mlp_out=m_w_mlp_out)
    mom_v = dict(attn_norm=v_attn_norm, w_in=v_w_in, sgu_w=v_sgu_w, sgu_b=v_sgu_b, conv_w=v_conv_w,
                 q_norm=v_q_norm, k_norm=v_k_norm, w_out=v_w_out, mlp_norm=v_mlp_norm, w_mlp_in=v_w_mlp_in,
                 w_mlp_out=v_w_mlp_out)
    order = ("attn_norm", "w_in", "sgu_w", "sgu_b", "conv_w", "q_norm", "k_norm", "w_out", "mlp_norm",
             "w_mlp_in", "w_mlp_out")
    chip = 2 * lax.axis_index("x") + lax.axis_index("y")
    c_arr = lax.axis_index("c").astype(jnp.int32).reshape(1)

    conv_cols = conv_w.shape[-1]
    chip_arr = chip.astype(jnp.int32).reshape(1)
    conv_pack = jnp.pad(conv_w.reshape(-1), (0, 2048 - conv_w.size)).reshape(1, 16, 128)
    keys = [("conv_w", 0)]
    placed = [_place_shard("place_conv_w", conv_pack, 0, chip_arr, F32)]
    for l in range(n_layers):
        for n in BIG:
            keys.append((n, l))
            placed.append(_place_shard(f"place_{n}_{l}", weights[n], l, chip_arr, BF16))
    wg = _GatheredWeights(keys, placed)
    conv_full = wg.get("conv_w", 0).reshape(N_CHIPS, 2048)[:, :conv_w.size].reshape(N_CHIPS, n_layers, 3, conv_cols)
    conv_full = conv_full.transpose(1, 2, 0, 3).reshape(n_layers, 3, N_CHIPS * conv_cols)
    prm = _prepare_params(attn_norm, sgu_w, sgu_b, conv_full, q_norm, k_norm, mlp_norm)

    sink = _GradReducer(c_arr)
    loss_local, grad_x, _, smalls = _local_step(x[0], loss_target[0], prm, wg, n_layers, sink)
    loss = lax.psum(loss_local, ("x", "y", "c"))

    small_names = SMALL_REPLICATED + ("conv_w",)
    small_shapes = [(n_layers,) + tuple(smalls[0][n].shape) for n in small_names]
    packed = _pack_rows([jnp.stack([smalls[l][n] for l in range(n_layers)]) for n in small_names])
    summed = _unpack_rows(_allreduce_small(packed), small_shapes)
    grads = dict(zip(small_names, summed))
    grads["conv_w"] = lax.dynamic_slice_in_dim(grads["conv_w"], chip * conv_cols, conv_cols, axis=2)

    delta, new_m, new_v = {}, {}, {}

    def update(names, after):
        joined = sink.finish(names, n_layers, after)
        for n in names:
            shp = weights[n].shape
            two_d = (shp[0] * shp[1], shp[2])
            grads[n] = joined[n]
            d, nm, nv = _adamw(f"adamw_{n}", weights[n].reshape(two_d), joined[n].reshape(two_d),
                               mom_m[n].reshape(two_d), mom_v[n].reshape(two_d))
            delta[n], new_m[n], new_v[n] = d.reshape(shp), nm.reshape(shp), nv.reshape(shp)

    update(("w_mlp_out", "w_mlp_in", "w_out"), sink.deps()[-1])
    update(("w_in",), delta["w_out"])
    smalls_all = SMALL_REPLICATED + ("conv_w",)
    shapes = [weights[n].shape for n in smalls_all]
    d, nm, nv = _adamw("adamw_small",
                       _pack_rows([weights[n] for n in smalls_all]), _pack_rows([grads[n] for n in smalls_all]),
                       _pack_rows([mom_m[n] for n in smalls_all]), _pack_rows([mom_v[n] for n in smalls_all]))
    for n, dd, mm, vv in zip(smalls_all, _unpack_rows(d, shapes), _unpack_rows(nm, shapes), _unpack_rows(nv, shapes)):
        delta[n], new_m[n], new_v[n] = dd, mm, vv

    return (loss, grad_x[None], *[grads[n] for n in order], *[delta[n] for n in order],
            *[new_m[n] for n in order], *[new_v[n] for n in order])
```

```python
import jax
import jax.numpy as jnp
from jax import lax
from jax.experimental import pallas as pl
from jax.experimental.pallas import tpu as pltpu

F32 = jnp.float32
BF16 = jnp.bfloat16
SDS = jax.ShapeDtypeStruct

EPS = 1e-6
HEAD_DIM = 64
A_HEADS = 8
A_WIDTH = 512
CHUNK = 128
B_WIDTH = 768
C_WIDTH = 768
N_PATTERNS = 3
PATTERN_DILATION = (1, 4, 16)
PW = 256
D_IN_PROJ = 5632
OFF_AU, OFF_AV, OFF_BB, OFF_BC, OFF_BX, OFF_Q, OFF_K, OFF_V = 0, 512, 1024, 1792, 2560, 3328, 4096, 4864
N_CHIPS = 4
N_DEV = 8
BLK = 128

ADAM_LR, ADAM_B1, ADAM_B2, ADAM_EPS, ADAM_WD, ADAM_STEP = 0.001, 0.9, 0.999, 1e-08, 0.01, 10

V7X_VMEM_LIMIT = 56 * 1024 * 1024
MESH = pl.DeviceIdType.MESH
NEG = -1e30


def _cp(n_axes):
    return pltpu.CompilerParams(dimension_semantics=("arbitrary",) * n_axes, vmem_limit_bytes=V7X_VMEM_LIMIT)


def _hbm_spec():
    return pl.BlockSpec(memory_space=pl.ANY)


def _norm_matmul(name, x, g, wg, out_dtype):
    S, D = x.shape
    ns, _, Ns = wg.shape
    tm = min(512, S)

    def body(x_ref, g_ref, w_ref, o_ref, h_ref, hs_ref):
        @pl.when(pl.program_id(1) == 0)
        def _():
            xv = x_ref[...]
            y = xv * lax.rsqrt(jnp.mean(xv * xv, axis=-1, keepdims=True) + EPS) * g_ref[...]
            hb = y.astype(BF16)
            hs_ref[...] = hb
            h_ref[...] = hb
        o_ref[...] = jnp.dot(hs_ref[...], w_ref[...], preferred_element_type=F32).astype(o_ref.dtype)

    return pl.pallas_call(
        body, name=name, grid=(S // tm, ns),
        in_specs=[pl.BlockSpec((tm, D), lambda i, s: (i, 0)),
                  pl.BlockSpec((1, D), lambda i, s: (0, 0)),
                  pl.BlockSpec((None, D, Ns), lambda i, s: (s, 0, 0))],
        out_specs=[pl.BlockSpec((tm, Ns), lambda i, s: (i, s)),
                   pl.BlockSpec((tm, D), lambda i, s: (i, 0))],
        out_shape=[SDS((S, ns * Ns), out_dtype), SDS((S, D), BF16)],
        scratch_shapes=[pltpu.VMEM((tm, D), BF16)],
        compiler_params=_cp(2),
    )(x, g, wg)


def _matmul(name, a, b, out_shape, out_dtype, *, grid, a_spec, b_spec, o_spec, contract, acc_shape,
            extras=(), extra_specs=(), a_pre=None, epi=None, deps=()):
    nk = grid[2]
    n_ex = len(extras)
    n_dep = len(deps)
    dims = (((contract[0],), (contract[1],)), ((), ()))

    def body(a_ref, b_ref, *rest):
        ex = rest[:n_ex]
        o_ref = rest[n_ex + n_dep]
        acc_ref = rest[n_ex + n_dep + 1]
        k = pl.program_id(2)

        @pl.when(k == 0)
        def _():
            acc_ref[...] = jnp.zeros_like(acc_ref)

        av = a_ref[...]
        if a_pre is not None:
            av = a_pre(av)
        acc_ref[...] += lax.dot_general(av, b_ref[...], dims, preferred_element_type=F32)

        @pl.when(k == nk - 1)
        def _():
            r = acc_ref[...]
            if epi is not None:
                r = epi(r, *[e[...] for e in ex])
            o_ref[...] = r.astype(o_ref.dtype)

    return pl.pallas_call(
        body, name=name, grid=grid,
        in_specs=[a_spec, b_spec, *extra_specs] + [_hbm_spec()] * n_dep,
        out_specs=o_spec,
        out_shape=SDS(out_shape, out_dtype),
        scratch_shapes=[pltpu.VMEM(acc_shape, F32)],
        compiler_params=_cp(3),
    )(a, b, *extras, *deps)


def _relu2_bf16(t):
    r = jnp.maximum(t.astype(F32), 0.0)
    return (r * r).astype(BF16)


def _loss_kernel(y, t):
    S, D = y.shape
    tm = min(256, S)

    def body(y_ref, t_ref, dy_ref, dyb_ref, l_ref):
        @pl.when(pl.program_id(0) == 0)
        def _():
            l_ref[...] = jnp.zeros_like(l_ref)
        e = y_ref[...] - t_ref[...]
        l_ref[...] += jnp.sum(e * e, axis=0, keepdims=True)
        dy = e * (1.0 / D)
        dy_ref[...] = dy
        dyb_ref[...] = dy.astype(BF16)

    row = pl.BlockSpec((tm, D), lambda i: (i, 0))
    return pl.pallas_call(
        body, name="loss_head", grid=(S // tm,),
        in_specs=[row, row],
        out_specs=[row, row, pl.BlockSpec((1, D), lambda i: (0, 0))],
        out_shape=[SDS((S, D), F32), SDS((S, D), BF16), SDS((1, D), F32)],
        compiler_params=_cp(1),
    )(y, t)


def _rmsnorm_bwd(name, dh, x, g, dres, deps=()):
    S, D = x.shape
    tm = min(256, S)
    n_dep = len(deps)

    def body(dh_ref, x_ref, g_ref, dres_ref, *rest):
        dx_ref, dxb_ref, dg_ref = rest[n_dep:]
        @pl.when(pl.program_id(0) == 0)
        def _():
            dg_ref[...] = jnp.zeros_like(dg_ref)
        xv = x_ref[...]
        dhv = dh_ref[...]
        rstd = lax.rsqrt(jnp.mean(xv * xv, axis=-1, keepdims=True) + EPS)
        xhat = xv * rstd
        dg_ref[...] += jnp.sum(dhv * xhat, axis=0, keepdims=True)
        dxn = dhv * g_ref[...]
        dx = dres_ref[...] + rstd * (dxn - xhat * jnp.mean(dxn * xhat, axis=-1, keepdims=True))
        dx_ref[...] = dx
        dxb_ref[...] = dx.astype(BF16)

    row = pl.BlockSpec((tm, D), lambda i: (i, 0))
    vec = pl.BlockSpec((1, D), lambda i: (0, 0))
    return pl.pallas_call(
        body, name=name, grid=(S // tm,),
        in_specs=[row, row, vec, row] + [_hbm_spec()] * n_dep,
        out_specs=[row, row, vec],
        out_shape=[SDS((S, D), F32), SDS((S, D), BF16), SDS((1, D), F32)],
        compiler_params=_cp(1),
    )(dh, x, g, dres, *deps)


def _adamw(name, w, g, m, v):
    R, C = w.shape
    tr = 256 if R % 256 == 0 else R
    c1 = 1.0 - ADAM_B1 ** ADAM_STEP
    c2 = 1.0 - ADAM_B2 ** ADAM_STEP

    def body(w_ref, g_ref, m_ref, v_ref, d_ref, nm_ref, nv_ref):
        gv = g_ref[...]
        nm = ADAM_B1 * m_ref[...] + (1.0 - ADAM_B1) * gv
        nv = ADAM_B2 * v_ref[...] + (1.0 - ADAM_B2) * (gv * gv)
        m_hat = nm / c1
        v_hat = nv / c2
        d_ref[...] = -ADAM_LR * (m_hat / (jnp.sqrt(v_hat) + ADAM_EPS) + ADAM_WD * w_ref[...])
        nm_ref[...] = nm
        nv_ref[...] = nv

    blk = pl.BlockSpec((tr, C), lambda i: (i, 0))
    return pl.pallas_call(
        body, name=name, grid=(R // tr,),
        in_specs=[blk] * 4, out_specs=[blk] * 3,
        out_shape=[SDS((R, C), F32)] * 3,
        compiler_params=_cp(1),
    )(w, g, m, v)


def _pair_select(lane, lo, hi):
    return jnp.where(lane < HEAD_DIM, lo, hi)


def _sgu_fwd(name, p, wt, bb):
    S = p.shape[0]

    def body(u_ref, v_ref, wt_ref, bb_ref, o_ref):
        lane = lax.broadcasted_iota(jnp.int32, (CHUNK, 128), 1)
        for pp in range(A_HEADS // 2):
            cs = slice(128 * pp, 128 * (pp + 1))
            vb = v_ref[:, cs].astype(BF16)
            mixed = _pair_select(lane,
                                 jnp.dot(wt_ref[2 * pp], vb, preferred_element_type=F32),
                                 jnp.dot(wt_ref[2 * pp + 1], vb, preferred_element_type=F32)) + bb_ref[:, cs]
            o_ref[:, cs] = (u_ref[:, cs] * mixed).astype(o_ref.dtype)

    return pl.pallas_call(
        body, name=name, grid=(S // CHUNK,),
        in_specs=[pl.BlockSpec((CHUNK, A_WIDTH), lambda c: (c, OFF_AU // A_WIDTH)),
                  pl.BlockSpec((CHUNK, A_WIDTH), lambda c: (c, OFF_AV // A_WIDTH)),
                  pl.BlockSpec((A_HEADS, CHUNK, CHUNK), lambda c: (0, 0, 0)),
                  pl.BlockSpec((CHUNK, A_WIDTH), lambda c: (0, 0))],
        out_specs=pl.BlockSpec((CHUNK, A_WIDTH), lambda c: (c, 0)),
        out_shape=SDS((S, A_WIDTH), BF16),
        compiler_params=_cp(1),
    )(p, p, wt, bb)


def _sgu_bwd(name, p, dycat, wt, wtt, bb):
    S = p.shape[0]

    def body(u_ref, v_ref, dy_ref, wt_ref, wtt_ref, bb_ref, du_ref, dv_ref, dw_ref, db_ref, dbacc_ref):
        c = pl.program_id(0)

        @pl.when(c == 0)
        def _():
            dw_ref[...] = jnp.zeros_like(dw_ref)
            dbacc_ref[...] = jnp.zeros_like(dbacc_ref)

        lane = lax.broadcasted_iota(jnp.int32, (CHUNK, 128), 1)
        row = lax.broadcasted_iota(jnp.int32, (CHUNK, 128), 0)
        causal = row >= lane
        for pp in range(A_HEADS // 2):
            cs = slice(128 * pp, 128 * (pp + 1))
            v = v_ref[:, cs]
            vb = v.astype(BF16)
            u = u_ref[:, cs]
            dy = dy_ref[:, cs]
            mixed = _pair_select(lane,
                                 jnp.dot(wt_ref[2 * pp], vb, preferred_element_type=F32),
                                 jnp.dot(wt_ref[2 * pp + 1], vb, preferred_element_type=F32)) + bb_ref[:, cs]
            du_ref[:, cs] = (dy * mixed).astype(du_ref.dtype)
            dm = dy * u
            dmb = dm.astype(BF16)
            dv = _pair_select(lane,
                              jnp.dot(wtt_ref[2 * pp], dmb, preferred_element_type=F32),
                              jnp.dot(wtt_ref[2 * pp + 1], dmb, preferred_element_type=F32))
            dv_ref[:, cs] = dv.astype(dv_ref.dtype)
            dbacc_ref[:, cs] += dm
            nt = (((1,), (1,)), ((), ()))
            dm_lo = jnp.where(lane < HEAD_DIM, dm, 0.0).astype(BF16)
            dm_hi = jnp.where(lane >= HEAD_DIM, dm, 0.0).astype(BF16)
            dw_ref[2 * pp] += jnp.where(causal, lax.dot_general(dm_lo, vb, nt, preferred_element_type=F32), 0.0)
            dw_ref[2 * pp + 1] += jnp.where(causal, lax.dot_general(dm_hi, vb, nt, preferred_element_type=F32), 0.0)

        @pl.when(c == S // CHUNK - 1)
        def _():
            out = jnp.zeros((CHUNK, 128), F32)
            for pp in range(A_HEADS // 2):
                acc = dbacc_ref[:, 128 * pp:128 * (pp + 1)]
                s_lo = jnp.sum(jnp.where(lane < HEAD_DIM, acc, 0.0), axis=1, keepdims=True)
                s_hi = jnp.sum(jnp.where(lane >= HEAD_DIM, acc, 0.0), axis=1, keepdims=True)
                out = jnp.where(lane == 2 * pp, s_lo, out)
                out = jnp.where(lane == 2 * pp + 1, s_hi, out)
            db_ref[...] = out

    chunk = lambda col: pl.BlockSpec((CHUNK, A_WIDTH), lambda c: (c, col))
    wspec = pl.BlockSpec((A_HEADS, CHUNK, CHUNK), lambda c: (0, 0, 0))
    return pl.pallas_call(
        body, name=name, grid=(S // CHUNK,),
        in_specs=[chunk(OFF_AU // A_WIDTH), chunk(OFF_AV // A_WIDTH), chunk(0), wspec, wspec,
                  pl.BlockSpec((CHUNK, A_WIDTH), lambda c: (0, 0))],
        out_specs=[chunk(0), chunk(0), wspec, pl.BlockSpec((CHUNK, 128), lambda c: (0, 0))],
        out_shape=[SDS((S, A_WIDTH), BF16), SDS((S, A_WIDTH), BF16),
                   SDS((A_HEADS, CHUNK, CHUNK), F32), SDS((CHUNK, 128), F32)],
        scratch_shapes=[pltpu.VMEM((CHUNK, A_WIDTH), F32)],
        compiler_params=_cp(1),
    )(p, p, dycat, wt, wtt, bb)


CONV_HALO = 8


def _shift_down(a, halo, k):
    T = a.shape[0]
    row = lax.broadcasted_iota(jnp.int32, a.shape, 0)
    out = pltpu.roll(a, k, 0)
    for r in range(k):
        out = jnp.where(row == r, halo[CONV_HALO - k + r:CONV_HALO - k + r + 1, :], out)
    return out


def _shift_up(a, halo, k):
    T = a.shape[0]
    row = lax.broadcasted_iota(jnp.int32, a.shape, 0)
    out = pltpu.roll(a, T - k, 0)
    for r in range(k):
        out = jnp.where(row == T - k + r, halo[r:r + 1, :], out)
    return out


def _conv_specs(S, T):
    hb = T // CONV_HALO
    last = S // CONV_HALO - 1
    tile = lambda col0: pl.BlockSpec((T, 128), lambda j, i: (i, col0 + j))
    prev = lambda col0: pl.BlockSpec((CONV_HALO, 128), lambda j, i: (jnp.maximum(i * hb - 1, 0), col0 + j))
    nxt = lambda col0: pl.BlockSpec((CONV_HALO, 128), lambda j, i: (jnp.minimum((i + 1) * hb, last), col0 + j))
    return tile, prev, nxt


def _conv_fwd(name, p, w):
    S = p.shape[0]
    T = min(512, S)
    tile, prev, _ = _conv_specs(S, T)
    cb, cc, cx = OFF_BB // 128, OFF_BC // 128, OFF_BX // 128

    def body(b_ref, c_ref, x_ref, ch_ref, xh_ref, w_ref, o_ref):
        i = pl.program_id(1)
        z = c_ref[...] * x_ref[...]
        zh = jnp.where(i > 0, ch_ref[...] * xh_ref[...], 0.0)
        z1 = _shift_down(z, zh, 1)
        z2 = _shift_down(z, zh, 2)
        conv = w_ref[0:1, :] * z2 + w_ref[1:2, :] * z1 + w_ref[2:3, :] * z
        o_ref[...] = (b_ref[...] * conv).astype(o_ref.dtype)

    return pl.pallas_call(
        body, name=name, grid=(B_WIDTH // 128, S // T),
        in_specs=[tile(cb), tile(cc), tile(cx), prev(cc), prev(cx),
                  pl.BlockSpec((3, 128), lambda j, i: (0, j))],
        out_specs=tile(0),
        out_shape=SDS((S, B_WIDTH), BF16),
        compiler_params=_cp(2),
    )(p, p, p, p, p, w)


def _conv_bwd(name, p, dycat, w):
    S = p.shape[0]
    T = min(512, S)
    tile, prev, nxt = _conv_specs(S, T)
    cb, cc, cx = OFF_BB // 128, OFF_BC // 128, OFF_BX // 128
    cdy = A_WIDTH // 128
    n_i = S // T

    def body(b_ref, c_ref, x_ref, dy_ref, ch_ref, xh_ref, bn_ref, dyn_ref, w_ref,
             db_ref, dc_ref, dx_ref, dw_ref):
        i = pl.program_id(1)

        @pl.when(i == 0)
        def _():
            dw_ref[...] = jnp.zeros_like(dw_ref)

        cv = c_ref[...]
        xv = x_ref[...]
        z = cv * xv
        zh = jnp.where(i > 0, ch_ref[...] * xh_ref[...], 0.0)
        z1 = _shift_down(z, zh, 1)
        z2 = _shift_down(z, zh, 2)
        w0, w1, w2 = w_ref[0:1, :], w_ref[1:2, :], w_ref[2:3, :]
        conv = w0 * z2 + w1 * z1 + w2 * z
        dy = dy_ref[...]
        db_ref[...] = (dy * conv).astype(db_ref.dtype)
        dconv = dy * b_ref[...]
        dconv_n = jnp.where(i < n_i - 1, dyn_ref[...] * bn_ref[...], 0.0)
        dz = w2 * dconv + w1 * _shift_up(dconv, dconv_n, 1) + w0 * _shift_up(dconv, dconv_n, 2)
        dc_ref[...] = (dz * xv).astype(dc_ref.dtype)
        dx_ref[...] = (dz * cv).astype(dx_ref.dtype)
        dw_ref[0:1, :] += jnp.sum(dconv * z2, axis=0, keepdims=True)
        dw_ref[1:2, :] += jnp.sum(dconv * z1, axis=0, keepdims=True)
        dw_ref[2:3, :] += jnp.sum(dconv * z, axis=0, keepdims=True)

    wspec = pl.BlockSpec((3, 128), lambda j, i: (0, j))
    return pl.pallas_call(
        body, name=name, grid=(B_WIDTH // 128, n_i),
        in_specs=[tile(cb), tile(cc), tile(cx), tile(cdy), prev(cc), prev(cx), nxt(cb), nxt(cdy), wspec],
        out_specs=[tile(0), tile(0), tile(0), wspec],
        out_shape=[SDS((S, B_WIDTH), BF16)] * 3 + [SDS((3, B_WIDTH), F32)],
        compiler_params=_cp(2),
    )(p, p, p, dycat, p, p, p, dycat, w)


def _seg_sum(t, bd):
    hi = t.astype(BF16)
    lo = (t - hi.astype(F32)).astype(BF16)
    return jnp.dot(hi, bd, preferred_element_type=F32) + jnp.dot(lo, bd, preferred_element_type=F32)


def _head_norm(x, g, bd):
    rstd = lax.rsqrt(_seg_sum(x * x, bd) * (1.0 / HEAD_DIM) + EPS)
    xhat = x * rstd
    return xhat * g, xhat, rstd


def _head_norm_bwd(dy, g, xhat, rstd, bd):
    dxh = dy * g
    return rstd * (dxh - xhat * (_seg_sum(dxh * xhat, bd) * (1.0 / HEAD_DIM)))


def _band_mask(has_prev):
    row = lax.broadcasted_iota(jnp.int32, (BLK, 2 * BLK), 0)
    col = lax.broadcasted_iota(jnp.int32, (BLK, 2 * BLK), 1)
    first_key = jnp.where(has_prev, 0, BLK)
    return (col >= row) & (col <= row + BLK) & (col >= first_key)


def _first_of_segment(g, n, n_blocks):
    per_seg = lax.shift_right_logical(jnp.int32(n_blocks), 2 * g)
    return (n & (per_seg - 1)) == 0


def _residue_rows(r, d):
    return slice(None) if d == 1 else pl.ds(r, BLK, stride=d)


HW = 128


def _for_residues(d, fn):
    if d == 1:
        fn(0)
    else:
        lax.fori_loop(0, d, lambda r, carry: (fn(r), carry)[1], 0)


def _attn_fwd(name, p, g, gq, gk, bd):
    S = p.shape[0]
    d = PATTERN_DILATION[g]
    rows = BLK * d
    nt = (((1,), (1,)), ((), ()))

    def body(q_ref, kc_ref, kp_ref, vc_ref, vp_ref, gq_ref, gk_ref, bd_ref, o_ref, lse_ref):
        has_prev = pl.program_id(1) > 0
        bdv = bd_ref[...]
        band = _band_mask(has_prev)
        lane = lax.broadcasted_iota(jnp.int32, (1, HW), 1)

        def residue(r):
            rr = _residue_rows(r, d)
            qn, _, _ = _head_norm(q_ref[rr, :], gq_ref[...], bdv)
            kn, _, _ = _head_norm(jnp.concatenate([kp_ref[rr, :], kc_ref[rr, :]], axis=0), gk_ref[...], bdv)
            knb = kn.astype(BF16)
            vb = jnp.concatenate([vp_ref[rr, :], vc_ref[rr, :]], axis=0).astype(BF16)
            o_acc = jnp.zeros((BLK, HW), F32)
            l_acc = jnp.zeros((BLK, HW), F32)
            for j in range(HW // HEAD_DIM):
                hm = (lane >= HEAD_DIM * j) & (lane < HEAD_DIM * (j + 1))
                qj = jnp.where(hm, qn, 0.0).astype(BF16)
                s = lax.dot_general(qj, knb, nt, preferred_element_type=F32) * (HEAD_DIM ** -0.5)
                s = jnp.where(band, s, NEG)
                m = jnp.max(s, axis=1, keepdims=True)
                e = jnp.exp(s - m)
                den = jnp.sum(e, axis=1, keepdims=True)
                pv = jnp.dot(e.astype(BF16), vb, preferred_element_type=F32)
                o_acc = jnp.where(hm, pv / den, o_acc)
                l_acc = jnp.where(hm, m + jnp.log(den), l_acc)
            o_ref[rr, :] = o_acc
            lse_ref[rr, :] = l_acc

        _for_residues(d, residue)

    per = PW // HW
    cq, ck, cv = (OFF_Q + PW * g) // HW, (OFF_K + PW * g) // HW, (OFF_V + PW * g) // HW
    cur = lambda col: pl.BlockSpec((rows, HW), lambda h, n: (n, col + h))
    prv = lambda col: pl.BlockSpec((rows, HW), lambda h, n: (jnp.maximum(n - 1, 0), col + h))
    vec = pl.BlockSpec((1, HW), lambda h, n: (0, h))
    return pl.pallas_call(
        body, name=name, grid=(per, S // rows),
        in_specs=[cur(cq), cur(ck), prv(ck), cur(cv), prv(cv), vec, vec, pl.BlockSpec((HW, HW), lambda h, n: (0, 0))],
        out_specs=[cur(0), cur(0)],
        out_shape=[SDS((S, PW), F32)] * 2,
        compiler_params=_cp(2),
    )(p, p, p, p, p, gq, gk, bd)


def _attn_bwd(name, p, g, lse, do3, c3, gq, gk, bd):
    S = p.shape[0]
    d = PATTERN_DILATION[g]
    rows = BLK * d
    nblk = S // rows
    nt = (((1,), (1,)), ((), ()))
    tn = (((0,), (0,)), ((), ()))

    def body(q_ref, kc_ref, kp_ref, vc_ref, vp_ref, lse_ref, do_ref, c_ref, gq_ref, gk_ref, bd_ref,
             dq_ref, dk_ref, dv_ref, dgq_ref, dgk_ref, ck_ref, cv_ref):
        n = pl.program_id(1)
        keep = jnp.where(n < nblk, 1.0, 0.0)
        has_prev = jnp.minimum(n, nblk - 1) > 0

        @pl.when(n == 0)
        def _():
            ck_ref[...] = jnp.zeros_like(ck_ref)
            cv_ref[...] = jnp.zeros_like(cv_ref)
            dgq_ref[...] = jnp.zeros_like(dgq_ref)
            dgk_ref[...] = jnp.zeros_like(dgk_ref)

        bdv = bd_ref[...]
        gqv = gq_ref[...]
        gkv = gk_ref[...]
        band = _band_mask(has_prev)
        lane = lax.broadcasted_iota(jnp.int32, (1, HW), 1)

        def residue(r):
            rr = _residue_rows(r, d)
            qn, qhat, qrstd = _head_norm(q_ref[rr, :], gqv, bdv)
            kn, khat, krstd = _head_norm(jnp.concatenate([kp_ref[rr, :], kc_ref[rr, :]], axis=0), gkv, bdv)
            knb = kn.astype(BF16)
            vb = jnp.concatenate([vp_ref[rr, :], vc_ref[rr, :]], axis=0).astype(BF16)
            lse_v = lse_ref[rr, :]
            do = do_ref[rr, :]
            cc = c_ref[rr, :]
            dqn = jnp.zeros((BLK, HW), F32)
            dkn = jnp.zeros((2 * BLK, HW), F32)
            dvv = jnp.zeros((2 * BLK, HW), F32)
            for j in range(HW // HEAD_DIM):
                hm = (lane >= HEAD_DIM * j) & (lane < HEAD_DIM * (j + 1))
                qj = jnp.where(hm, qn, 0.0).astype(BF16)
                doj = jnp.where(hm, do, 0.0).astype(BF16)
                s = lax.dot_general(qj, knb, nt, preferred_element_type=F32) * (HEAD_DIM ** -0.5)
                lse_j = jnp.max(jnp.where(hm, lse_v, NEG), axis=1, keepdims=True)
                c_j = jnp.max(jnp.where(hm, cc, NEG), axis=1, keepdims=True)
                prob = jnp.where(band, jnp.exp(s - lse_j), 0.0)
                dp = lax.dot_general(doj, vb, nt, preferred_element_type=F32)
                ds = (prob * (dp + c_j) * (HEAD_DIM ** -0.5)).astype(BF16)
                dqn = jnp.where(hm, jnp.dot(ds, knb, preferred_element_type=F32), dqn)
                dkn += lax.dot_general(ds, qj, tn, preferred_element_type=F32)
                dvv += lax.dot_general(prob.astype(BF16), doj, tn, preferred_element_type=F32)

            dq_ref[rr, :] = _head_norm_bwd(dqn, gqv, qhat, qrstd, bdv)
            dk2 = _head_norm_bwd(dkn, gkv, khat, krstd, bdv)
            dgq_ref[...] += keep * jnp.sum(dqn * qhat, axis=0, keepdims=True)
            dgk_ref[...] += keep * jnp.sum(dkn * khat, axis=0, keepdims=True)
            dk_ref[rr, :] = ck_ref[rr, :] + keep * dk2[:BLK]
            dv_ref[rr, :] = cv_ref[rr, :] + keep * dvv[:BLK]
            ck_ref[rr, :] = dk2[BLK:]
            cv_ref[rr, :] = dvv[BLK:]

        _for_residues(d, residue)

    last = nblk - 1
    per = PW // HW
    cq, ck, cv = (OFF_Q + PW * g) // HW, (OFF_K + PW * g) // HW, (OFF_V + PW * g) // HW
    cur = lambda col: pl.BlockSpec((rows, HW), lambda h, n: (jnp.minimum(n, last), col + h))
    prv = lambda col: pl.BlockSpec((rows, HW), lambda h, n: (jnp.maximum(jnp.minimum(n, last) - 1, 0), col + h))
    cur3 = pl.BlockSpec((None, rows, HW), lambda h, n: (g, jnp.minimum(n, last), h))
    done = pl.BlockSpec((rows, HW), lambda h, n: (jnp.maximum(n - 1, 0), h))
    vec = pl.BlockSpec((1, HW), lambda h, n: (0, h))
    return pl.pallas_call(
        body, name=name, grid=(per, nblk + 1),
        in_specs=[cur(cq), cur(ck), prv(ck), cur(cv), prv(cv), cur(0), cur3, cur3, vec, vec,
                  pl.BlockSpec((HW, HW), lambda h, n: (0, 0))],
        out_specs=[cur(0), done, done, vec, vec],
        out_shape=[SDS((S, PW), F32)] * 3 + [SDS((1, PW), F32)] * 2,
        scratch_shapes=[pltpu.VMEM((rows, HW), F32), pltpu.VMEM((rows, HW), F32)],
        compiler_params=_cp(2),
    )(p, p, p, p, p, lse, do3, c3, gq, gk, bd)


def _mix_fwd(name, os, lses):
    S = os[0].shape[0]
    tm = min(512, S)

    def body(o0, o1, o2, l0, l1, l2, y_ref):
        o = [o0[...], o1[...], o2[...]]
        l = [l0[...], l1[...], l2[...]]
        m = jnp.maximum(jnp.maximum(l[0], l[1]), l[2])
        e = [jnp.exp(t - m) for t in l]
        inv = 1.0 / (e[0] + e[1] + e[2])
        for g in range(N_PATTERNS):
            y_ref[:, PW * g:PW * (g + 1)] = (o[g] * (e[g] * inv)).astype(y_ref.dtype)

    blk = pl.BlockSpec((tm, PW), lambda i: (i, 0))
    return pl.pallas_call(
        body, name=name, grid=(S // tm,),
        in_specs=[blk] * 6,
        out_specs=pl.BlockSpec((tm, C_WIDTH), lambda i: (i, 0)),
        out_shape=SDS((S, C_WIDTH), BF16),
        compiler_params=_cp(1),
    )(*os, *lses)


def _mix_bwd(name, os, lses, dycat, bd):
    S = os[0].shape[0]
    tm = min(512, S)
    c0 = (A_WIDTH + B_WIDTH) // PW

    def body(o0, o1, o2, l0, l1, l2, dy0_ref, dy1_ref, dy2_ref, bd_ref, do_ref, c_ref):
        bdv = bd_ref[...]
        o = [o0[...], o1[...], o2[...]]
        l = [l0[...], l1[...], l2[...]]
        dys = [dy0_ref[...], dy1_ref[...], dy2_ref[...]]
        m = jnp.maximum(jnp.maximum(l[0], l[1]), l[2])
        e = [jnp.exp(t - m) for t in l]
        inv = 1.0 / (e[0] + e[1] + e[2])
        alpha = [t * inv for t in e]
        da = [_seg_sum(dys[g] * o[g], bdv) for g in range(N_PATTERNS)]
        mean_da = alpha[0] * da[0] + alpha[1] * da[1] + alpha[2] * da[2]
        for g in range(N_PATTERNS):
            do_ref[g] = dys[g] * alpha[g]
            c_ref[g] = -alpha[g] * mean_da

    blk = pl.BlockSpec((tm, PW), lambda i: (i, 0))
    blk3 = pl.BlockSpec((N_PATTERNS, tm, PW), lambda i: (0, i, 0))
    dyspec = lambda g: pl.BlockSpec((tm, PW), lambda i: (i, c0 + g))
    return pl.pallas_call(
        body, name=name, grid=(S // tm,),
        in_specs=[blk] * 6 + [dyspec(0), dyspec(1), dyspec(2), pl.BlockSpec((PW, PW), lambda i: (0, 0))],
        out_specs=[blk3, blk3],
        out_shape=[SDS((N_PATTERNS, S, PW), F32)] * 2,
        compiler_params=_cp(1),
    )(*os, *lses, dycat, dycat, dycat, bd)


def _mesh_pos():
    x, y, c = lax.axis_index("x"), lax.axis_index("y"), lax.axis_index("c")
    chips = [(1 - x, y), (x, 1 - y), (1 - x, 1 - y)]
    chip_idx = [2 * cx + cy for cx, cy in chips]
    return x, y, c, 2 * x + y, chips, chip_idx


def _place_shard(name, w, layer, chip_arr, out_dtype):
    _, R, C = w.shape
    tr = min(256, R)

    def body(chip_ref, w_ref, o_ref):
        o_ref[...] = w_ref[...].astype(o_ref.dtype)

    return pl.pallas_call(
        body, name=name,
        grid_spec=pltpu.PrefetchScalarGridSpec(
            num_scalar_prefetch=1, grid=(R // tr,),
            in_specs=[pl.BlockSpec((None, tr, C), lambda i, chip_ref: (layer, i, 0))],
            out_specs=pl.BlockSpec((None, tr, C), lambda i, chip_ref: (chip_ref[0], i, 0))),
        out_shape=SDS((N_CHIPS, R, C), out_dtype),
        compiler_params=_cp(1),
    )(chip_arr, w)


HBM_SPEC = pl.BlockSpec(memory_space=pltpu.HBM)
SEM_SPEC = pl.BlockSpec(memory_space=pltpu.SEMAPHORE)
SPLIT_COPY = pltpu.SideEffectType.DATAFLOW_SIDE_EFFECTING
N_PEER_CHIPS = N_CHIPS - 1
TOKEN_SHAPE = SDS((8, 128), F32)
TOKEN_SPEC = pl.BlockSpec(memory_space=pltpu.VMEM)


def _in_hbm(a):
    return pltpu.with_memory_space_constraint(a, pltpu.HBM)


def _gather_start(bufs):
    T = len(bufs)

    def body(*refs):
        ins = refs[:T]
        send_sems, recv_sems = refs[T:2 * T], refs[2 * T:3 * T]
        x, y, c, me, chips, chip_idx = _mesh_pos()
        for t in range(T):
            hr = ins[t].shape[1] // 2
            mine = ins[t].at[me, pl.ds(c * hr, hr), :]
            for j in range(N_PEER_CHIPS):
                pltpu.make_async_remote_copy(src_ref=mine, dst_ref=mine, send_sem=send_sems[t].at[j],
                                             recv_sem=recv_sems[t].at[j], device_id=(*chips[j], c),
                                             device_id_type=MESH).start()

    sems = [pltpu.SemaphoreType.DMA((N_PEER_CHIPS,))] * T
    out = pl.pallas_call(
        body, name="gather_start",
        in_specs=[HBM_SPEC] * T,
        out_specs=[SEM_SPEC] * (2 * T) + [HBM_SPEC] * T,
        out_shape=sems + sems + [pltpu.HBM(b.shape, b.dtype) for b in bufs],
        input_output_aliases={t: 2 * T + t for t in range(T)},
        compiler_params=pltpu.CompilerParams(has_side_effects=SPLIT_COPY),
    )(*[_in_hbm(b) for b in bufs])
    return out[:T], out[T:2 * T], out[2 * T:]


def _gather_wait(name, buf, send_sem, recv_sem, after):
    n_in = 3 if after is None else 4

    def body(*refs):
        buf_ref, ssem, rsem = refs[:3]
        x, y, c, me, chips, chip_idx = _mesh_pos()
        hr = buf_ref.shape[1] // 2
        mine = buf_ref.at[me, pl.ds(c * hr, hr), :]
        for j in range(N_PEER_CHIPS):
            got = buf_ref.at[chip_idx[j], pl.ds(c * hr, hr), :]
            cp = pltpu.make_async_remote_copy(src_ref=mine, dst_ref=got, send_sem=ssem.at[j], recv_sem=rsem.at[j],
                                              device_id=(*chips[j], c), device_id_type=MESH)
            cp.wait_send()
            cp.wait_recv()

    args = [buf, send_sem, recv_sem] + ([] if after is None else [after])
    return pl.pallas_call(
        body, name=name,
        in_specs=[HBM_SPEC, SEM_SPEC, SEM_SPEC] + [_hbm_spec()] * (n_in - 3),
        out_specs=HBM_SPEC,
        out_shape=pltpu.HBM(buf.shape, buf.dtype),
        input_output_aliases={0: 0},
        compiler_params=pltpu.CompilerParams(has_side_effects=SPLIT_COPY),
    )(*args)


def _gather_forward(name, buf):
    def body(in_ref, out_ref, send_sems, recv_sems):
        x, y, c, me, chips, chip_idx = _mesh_pos()
        hr = out_ref.shape[1] // 2
        cps = []
        for j in range(N_PEER_CHIPS):
            got = out_ref.at[chip_idx[j], pl.ds(c * hr, hr), :]
            cp = pltpu.make_async_remote_copy(src_ref=got, dst_ref=got, send_sem=send_sems.at[j],
                                              recv_sem=recv_sems.at[j], device_id=(x, y, 1 - c), device_id_type=MESH)
            cp.start()
            cps.append(cp)
        for j in range(N_PEER_CHIPS):
            theirs = out_ref.at[chip_idx[j], pl.ds((1 - c) * hr, hr), :]
            pltpu.make_async_remote_copy(src_ref=theirs, dst_ref=theirs, send_sem=send_sems.at[j],
                                         recv_sem=recv_sems.at[j], device_id=(x, y, 1 - c),
                                         device_id_type=MESH).wait_recv()
        for cp in cps:
            cp.wait_send()

    return pl.pallas_call(
        body, name=name,
        in_specs=[_hbm_spec()], out_specs=_hbm_spec(),
        out_shape=SDS(buf.shape, buf.dtype),
        input_output_aliases={0: 0},
        scratch_shapes=[pltpu.SemaphoreType.DMA((N_PEER_CHIPS,)), pltpu.SemaphoreType.DMA((N_PEER_CHIPS,))],
    )(buf)


class _GatheredWeights:
    def __init__(self, keys, bufs):
        send_sems, recv_sems, thru = _gather_start(bufs)
        self._pending = {k: (b, s, r) for k, b, s, r in zip(keys, thru, send_sems, recv_sems)}
        self._ready = {}

    def get(self, name, layer, after=None):
        key = (name, layer)
        if key not in self._ready:
            buf, ssem, rsem = self._pending.pop(key)
            buf = _gather_wait(f"gather_wait_{name}_{layer}", buf, ssem, rsem, after)
            self._ready[key] = _gather_forward(f"gather_fwd_{name}_{layer}", buf)
        return self._ready[key]


def _swap_copy(g_ref, land_ref, send_sem, recv_sem):
    x, y, c, _, _, _ = _mesh_pos()
    hr = g_ref.shape[1] // 2
    return pltpu.make_async_remote_copy(src_ref=g_ref.at[:, pl.ds((1 - c) * hr, hr), :], dst_ref=land_ref,
                                        send_sem=send_sem, recv_sem=recv_sem, device_id=(x, y, 1 - c),
                                        device_id_type=MESH)


def _swap_start(name, g):
    land_shape = (g.shape[0], g.shape[1] // 2, g.shape[2])

    def body(g_ref, land_ref, send_sem, recv_sem, land_thru, token):
        _swap_copy(g_ref, land_ref, send_sem, recv_sem).start()
        token[...] = jnp.zeros_like(token)

    return pl.pallas_call(
        body, name=name,
        in_specs=[HBM_SPEC, HBM_SPEC],
        out_specs=[SEM_SPEC, SEM_SPEC, HBM_SPEC, TOKEN_SPEC],
        out_shape=[pltpu.SemaphoreType.DMA(()), pltpu.SemaphoreType.DMA(()), pltpu.HBM(land_shape, g.dtype),
                   TOKEN_SHAPE],
        input_output_aliases={1: 2},
        compiler_params=pltpu.CompilerParams(has_side_effects=SPLIT_COPY),
    )(_in_hbm(g), _in_hbm(lax.empty(land_shape, g.dtype)))


def _swap_wait(name, g, land, send_sem, recv_sem, after):
    def body(g_ref, land_ref, send_sem, recv_sem, after_ref, land_out):
        cp = _swap_copy(g_ref, land_ref, send_sem, recv_sem)
        cp.wait_send()
        cp.wait_recv()

    return pl.pallas_call(
        body, name=name,
        in_specs=[HBM_SPEC, HBM_SPEC, SEM_SPEC, SEM_SPEC, _hbm_spec()],
        out_specs=HBM_SPEC,
        out_shape=pltpu.HBM(land.shape, land.dtype),
        input_output_aliases={1: 0},
        compiler_params=pltpu.CompilerParams(has_side_effects=SPLIT_COPY),
    )(_in_hbm(g), land, send_sem, recv_sem, after)


def _add_my_half(name, g, r, c_arr):
    ns, R, C = g.shape
    hr = R // 2
    tr = min(256, hr)
    nt = hr // tr

    def body(c_ref, g_ref, r_ref, o_ref, land_ref):
        t = (g_ref[...] + r_ref[...]).astype(o_ref.dtype)
        o_ref[...] = t
        land_ref[...] = t

    out = pl.BlockSpec((None, tr, C), lambda s, i, c_ref: (s, i, 0))
    return pl.pallas_call(
        body, name=name,
        grid_spec=pltpu.PrefetchScalarGridSpec(
            num_scalar_prefetch=1, grid=(ns, nt),
            in_specs=[pl.BlockSpec((None, tr, C), lambda s, i, c_ref: (s, c_ref[0] * nt + i, 0)), out],
            out_specs=[out, out]),
        out_shape=[SDS((ns, hr, C), BF16)] * 2,
        compiler_params=_cp(2),
    )(c_arr, g, r)


def _exchange_start(name, part, land):
    def body(part_ref, land_ref, send_sems, recv_sems, land_thru, token):
        x, y, c, me, chips, chip_idx = _mesh_pos()
        for j in range(N_PEER_CHIPS):
            pltpu.make_async_remote_copy(src_ref=part_ref.at[chip_idx[j]], dst_ref=land_ref.at[me],
                                         send_sem=send_sems.at[j], recv_sem=recv_sems.at[j],
                                         device_id=(*chips[j], c), device_id_type=MESH).start()
        token[...] = jnp.zeros_like(token)

    sems = pltpu.SemaphoreType.DMA((N_PEER_CHIPS,))
    return pl.pallas_call(
        body, name=name,
        in_specs=[HBM_SPEC, HBM_SPEC],
        out_specs=[SEM_SPEC, SEM_SPEC, HBM_SPEC, TOKEN_SPEC],
        out_shape=[sems, sems, pltpu.HBM(land.shape, land.dtype), TOKEN_SHAPE],
        input_output_aliases={1: 2},
        compiler_params=pltpu.CompilerParams(has_side_effects=SPLIT_COPY),
    )(_in_hbm(part), _in_hbm(land))


def _exchange_wait(name, part, land, send_sems, recv_sems, after):
    def body(part_ref, land_ref, send_sems, recv_sems, after_ref, land_out):
        x, y, c, me, chips, chip_idx = _mesh_pos()
        for j in range(N_PEER_CHIPS):
            cp = pltpu.make_async_remote_copy(src_ref=part_ref.at[chip_idx[j]], dst_ref=land_ref.at[chip_idx[j]],
                                              send_sem=send_sems.at[j], recv_sem=recv_sems.at[j],
                                              device_id=(*chips[j], c), device_id_type=MESH)
            cp.wait_send()
            cp.wait_recv()

    return pl.pallas_call(
        body, name=name,
        in_specs=[HBM_SPEC, HBM_SPEC, SEM_SPEC, SEM_SPEC, _hbm_spec()],
        out_specs=HBM_SPEC,
        out_shape=pltpu.HBM(land.shape, land.dtype),
        input_output_aliases={1: 0},
        compiler_params=pltpu.CompilerParams(has_side_effects=SPLIT_COPY),
    )(_in_hbm(part), land, send_sems, recv_sems, after)


class _GradReducer:
    def __init__(self, c_arr):
        self._c_arr = c_arr
        self._swapping = []
        self._exchanging = {}
        self._tokens = []

    def begin(self, name, layer, g):
        tag = f"{name}_{layer}"
        ssem, rsem, land, token = _swap_start(f"rs_swap_start_{tag}", g)
        self._swapping.append((name, layer, g, ssem, rsem, land))
        self._tokens.append(token)

    def advance(self, after):
        for name, layer, g, ssem, rsem, land in self._swapping:
            tag = f"{name}_{layer}"
            theirs = _swap_wait(f"rs_swap_wait_{tag}", g, land, ssem, rsem, after)
            part, own = _add_my_half(f"rs_add_{tag}", g, theirs, self._c_arr)
            ssems, rsems, land2, token = _exchange_start(f"rs_xchg_start_{tag}", part, own)
            self._exchanging[(name, layer)] = (part, ssems, rsems, land2)
            self._tokens.append(token)
        self._swapping = []

    def deps(self):
        tokens, self._tokens = self._tokens, []
        return tokens

    def finish(self, names, n_layers, after):
        bufs = []
        for name in names:
            buf = None
            for layer in range(n_layers):
                part, ssems, rsems, land = self._exchanging.pop((name, layer))
                tag = f"{name}_{layer}"
                landed = _exchange_wait(f"rs_xchg_wait_{tag}", part, land, ssems, rsems, after)
                buf = _sum_chips(f"rs_sum_{tag}", landed, self._c_arr, layer, n_layers, buf)
            bufs.append(buf)
        return dict(zip(names, _join_halves(f"rs_join_{names[0]}", bufs)))


def _sum_chips(name, r, c_arr, layer, n_layers, prev):
    ns, H, C = r.shape
    tr = min(256, H)
    nt = H // tr

    def body(c_ref, r_ref, *rest):
        o_ref = rest[-1]
        o_ref[...] = ((r_ref[0].astype(F32) + r_ref[1].astype(F32)) + r_ref[2].astype(F32)) + r_ref[3].astype(F32)

    in_specs = [pl.BlockSpec((ns, tr, C), lambda i, c_ref: (0, i, 0))]
    args = [c_arr, r]
    aliases = {}
    if prev is not None:
        in_specs.append(_hbm_spec())
        args.append(prev)
        aliases = {2: 0}
    return pl.pallas_call(
        body, name=name,
        grid_spec=pltpu.PrefetchScalarGridSpec(
            num_scalar_prefetch=1, grid=(nt,), in_specs=in_specs,
            out_specs=pl.BlockSpec((None, tr, C), lambda i, c_ref: (layer, c_ref[0] * nt + i, 0))),
        out_shape=SDS((n_layers, 2 * H, C), F32),
        input_output_aliases=aliases,
        compiler_params=_cp(1),
    )(*args)


def _join_halves(name, bufs):
    T = len(bufs)

    def body(*refs):
        outs = refs[T:2 * T]
        send_sems, recv_sems = refs[2 * T:]
        x, y, c, _, _, _ = _mesh_pos()
        cps = []
        for t in range(T):
            hr = outs[t].shape[1] // 2
            mine = outs[t].at[:, pl.ds(c * hr, hr), :]
            cp = pltpu.make_async_remote_copy(src_ref=mine, dst_ref=mine, send_sem=send_sems.at[t],
                                              recv_sem=recv_sems.at[t], device_id=(x, y, 1 - c), device_id_type=MESH)
            cp.start()
            cps.append(cp)
        for t in range(T):
            hr = outs[t].shape[1] // 2
            theirs = outs[t].at[:, pl.ds((1 - c) * hr, hr), :]
            pltpu.make_async_remote_copy(src_ref=theirs, dst_ref=theirs, send_sem=send_sems.at[t],
                                         recv_sem=recv_sems.at[t], device_id=(x, y, 1 - c),
                                         device_id_type=MESH).wait_recv()
        for cp in cps:
            cp.wait_send()

    return pl.pallas_call(
        body, name=name,
        in_specs=[_hbm_spec()] * T, out_specs=[_hbm_spec()] * T,
        out_shape=[SDS(b.shape, b.dtype) for b in bufs],
        input_output_aliases={t: t for t in range(T)},
        scratch_shapes=[pltpu.SemaphoreType.DMA((T,)), pltpu.SemaphoreType.DMA((T,))],
    )(*bufs)


def _allreduce_small(buf):
    R, C = buf.shape

    def body(in_ref, out_ref, land_ref, send_sems, recv_sems):
        x, y, c = lax.axis_index("x"), lax.axis_index("y"), lax.axis_index("c")
        me = 4 * x + 2 * y + c
        land_ref[me] = in_ref[...]
        cps = []
        for k in range(1, N_DEV):
            kx, ky, kc = (k >> 2) & 1, (k >> 1) & 1, k & 1
            peer = (x ^ kx, y ^ ky, c ^ kc)
            cp = pltpu.make_async_remote_copy(src_ref=in_ref, dst_ref=land_ref.at[me],
                                              send_sem=send_sems.at[k - 1], recv_sem=recv_sems.at[k - 1],
                                              device_id=peer, device_id_type=MESH)
            cp.start()
            cps.append(cp)
        for k in range(1, N_DEV):
            src = me ^ k
            slot = land_ref.at[src]
            pltpu.make_async_remote_copy(src_ref=slot, dst_ref=slot, send_sem=send_sems.at[k - 1],
                                         recv_sem=recv_sems.at[k - 1], device_id=(x, y, c),
                                         device_id_type=MESH).wait_recv()
        for cp in cps:
            cp.wait_send()
        acc = land_ref[0]
        for d in range(1, N_DEV):
            acc = acc + land_ref[d]
        out_ref[...] = acc

    return pl.pallas_call(
        body, name="allreduce_small",
        in_specs=[pl.BlockSpec(memory_space=pltpu.VMEM)],
        out_specs=pl.BlockSpec(memory_space=pltpu.VMEM),
        out_shape=SDS((R, C), buf.dtype),
        scratch_shapes=[pltpu.VMEM((N_DEV, R, C), buf.dtype),
                        pltpu.SemaphoreType.DMA((N_DEV - 1,)), pltpu.SemaphoreType.DMA((N_DEV - 1,))],
        compiler_params=pltpu.CompilerParams(vmem_limit_bytes=V7X_VMEM_LIMIT),
    )(buf)


def _deinterleave(t, d):
    if d == 1:
        return t
    S, W = t.shape
    return t.reshape(S // d, d, W).transpose(1, 0, 2).reshape(S, W)


def _interleave(t, d):
    if d == 1:
        return t
    S, W = t.shape
    return t.reshape(d, S // d, W).transpose(1, 0, 2).reshape(S, W)


def _to_patterns(t, off):
    return jnp.stack([_deinterleave(t[:, off + PW * g:off + PW * (g + 1)], PATTERN_DILATION[g])
                      for g in range(N_PATTERNS)])


def _from_patterns(t3):
    return jnp.stack([_interleave(t3[g], PATTERN_DILATION[g]) for g in range(N_PATTERNS)])


def _pack_rows(vectors):
    flat = jnp.concatenate([v.reshape(-1) for v in vectors])
    n = flat.shape[0]
    padded = -(-n // 1024) * 1024
    return jnp.pad(flat, (0, padded - n)).reshape(padded // 128, 128)


def _unpack_rows(buf, shapes):
    flat = buf.reshape(-1)
    out, off = [], 0
    for s in shapes:
        n = 1
        for dim in s:
            n *= dim
        out.append(flat[off:off + n].reshape(s))
        off += n
    return out


def _layer_forward(l, x, prm, wg):
    S, D = x.shape
    p, h = _norm_matmul(f"in_proj_{l}", x, prm["attn_norm"][l], wg.get("w_in", l, x), F32)
    y_a = _sgu_fwd(f"sgu_fwd_{l}", p, prm["sgu_wt"][l], prm["sgu_bb"][l])
    y_b = _conv_fwd(f"conv_fwd_{l}", p, prm["conv_w"][l])
    os, lses = [], []
    for g in range(N_PATTERNS):
        o_g, lse_g = _attn_fwd(f"attn_fwd_{l}_{g}", p, g, prm["q_gain"][l], prm["k_gain"][l], prm["bd"])
        os.append(o_g)
        lses.append(lse_g)
    y_c = _mix_fwd(f"mix_fwd_{l}", os, lses)
    ycat = jnp.concatenate([y_a, y_b, y_c], axis=1)
    tm = min(512, S)
    w_out = wg.get("w_out", l, ycat)
    rq = w_out.shape[1]
    x1 = _matmul(
        f"out_proj_{l}", ycat, w_out, (S, D), F32, grid=(S // tm, 1, N_CHIPS),
        a_spec=pl.BlockSpec((tm, rq), lambda i, j, k: (i, k)),
        b_spec=pl.BlockSpec((None, rq, D), lambda i, j, k: (k, 0, 0)),
        o_spec=pl.BlockSpec((tm, D), lambda i, j, k: (i, 0)),
        contract=(1, 0), acc_shape=(tm, D),
        extras=(x,), extra_specs=(pl.BlockSpec((tm, D), lambda i, j, k: (i, 0)),),
        epi=lambda r, res: r + res)
    a, h2 = _norm_matmul(f"mlp_in_{l}", x1, prm["mlp_norm"][l], wg.get("w_mlp_in", l, x1), BF16)
    w_mlp_out = wg.get("w_mlp_out", l, a)
    dff4 = w_mlp_out.shape[1]
    tk = min(1024, dff4)
    kpc = dff4 // tk
    x2 = _matmul(
        f"mlp_out_{l}", a, w_mlp_out, (S, D), F32, grid=(S // tm, 1, N_CHIPS * kpc),
        a_spec=pl.BlockSpec((tm, tk), lambda i, j, k: (i, k)),
        b_spec=pl.BlockSpec((None, tk, D), lambda i, j, k: (k // kpc, k % kpc, 0)),
        o_spec=pl.BlockSpec((tm, D), lambda i, j, k: (i, 0)),
        contract=(1, 0), acc_shape=(tm, D), a_pre=_relu2_bf16,
        extras=(x1,), extra_specs=(pl.BlockSpec((tm, D), lambda i, j, k: (i, 0)),),
        epi=lambda r, res: r + res)
    saved = dict(x=x, p=p, h=h, os=os, lses=lses, ycat=ycat, x1=x1, a=a, h2=h2)
    return x2, saved


def _layer_backward(l, dx2, dx2b, sv, prm, wg, sink):
    S, D = dx2.shape
    w_in, w_out = wg.get("w_in", l), wg.get("w_out", l)
    w_mlp_in, w_mlp_out = wg.get("w_mlp_in", l), wg.get("w_mlp_out", l)
    dff4 = w_mlp_in.shape[-1]
    dff = N_CHIPS * dff4
    tm = min(512, S)
    tk = min(1024, S)
    nks = S // tk

    da = _matmul(
        f"mlp_out_bwd_{l}", dx2b, w_mlp_out, (S, dff), BF16, grid=(S // tm, N_CHIPS, 1),
        a_spec=pl.BlockSpec((tm, D), lambda i, j, k: (i, 0)),
        b_spec=pl.BlockSpec((None, dff4, D), lambda i, j, k: (j, 0, 0)),
        o_spec=pl.BlockSpec((tm, dff4), lambda i, j, k: (i, j)),
        contract=(1, 1), acc_shape=(tm, dff4),
        extras=(sv["a"],), extra_specs=(pl.BlockSpec((tm, dff4), lambda i, j, k: (i, j)),),
        epi=lambda r, act: r * (2.0 * jnp.maximum(act.astype(F32), 0.0)), deps=sink.deps())
    tmw = min(1024, dff4)
    mpc = dff4 // tmw
    g_w2 = _matmul(
        f"mlp_out_dw_{l}", sv["a"], dx2b, (N_CHIPS, dff4, D), F32, grid=(N_CHIPS * mpc, 1, nks),
        a_spec=pl.BlockSpec((tk, tmw), lambda i, j, k: (k, i)),
        b_spec=pl.BlockSpec((tk, D), lambda i, j, k: (k, 0)),
        o_spec=pl.BlockSpec((None, tmw, D), lambda i, j, k: (i // mpc, i % mpc, 0)),
        contract=(0, 0), acc_shape=(tmw, D), a_pre=_relu2_bf16)
    sink.begin("w_mlp_out", l, g_w2)
    dh2 = _matmul(
        f"mlp_in_bwd_{l}", da, w_mlp_in, (S, D), F32, grid=(S // tm, 1, N_CHIPS),
        a_spec=pl.BlockSpec((tm, dff4), lambda i, j, k: (i, k)),
        b_spec=pl.BlockSpec((None, D, dff4), lambda i, j, k: (k, 0, 0)),
        o_spec=pl.BlockSpec((tm, D), lambda i, j, k: (i, 0)),
        contract=(1, 1), acc_shape=(tm, D), deps=sink.deps())
    sink.advance(dh2)
    tmd = min(1024, D)
    g_w1 = _matmul(
        f"mlp_in_dw_{l}", sv["h2"], da, (N_CHIPS, D, dff4), F32, grid=(N_CHIPS, D // tmd, nks),
        a_spec=pl.BlockSpec((tk, tmd), lambda i, j, k: (k, j)),
        b_spec=pl.BlockSpec((tk, dff4), lambda i, j, k: (k, i)),
        o_spec=pl.BlockSpec((None, tmd, dff4), lambda i, j, k: (i, j, 0)),
        contract=(0, 0), acc_shape=(tmd, dff4))
    sink.begin("w_mlp_in", l, g_w1)
    dx1, dx1b, g_mlp_norm = _rmsnorm_bwd(f"mlp_norm_bwd_{l}", dh2, sv["x1"], prm["mlp_norm"][l], dx2,
                                         deps=sink.deps())

    rq = w_out.shape[1]
    dycat = _matmul(
        f"out_proj_bwd_{l}", dx1b, w_out, (S, N_CHIPS * rq), F32, grid=(S // tm, N_CHIPS, 1),
        a_spec=pl.BlockSpec((tm, D), lambda i, j, k: (i, 0)),
        b_spec=pl.BlockSpec((None, rq, D), lambda i, j, k: (j, 0, 0)),
        o_spec=pl.BlockSpec((tm, rq), lambda i, j, k: (i, j)),
        contract=(1, 1), acc_shape=(tm, rq))
    sink.advance(dycat)
    g_wout = _matmul(
        f"out_proj_dw_{l}", sv["ycat"], dx1b, (N_CHIPS, rq, D), F32, grid=(N_CHIPS, 1, nks),
        a_spec=pl.BlockSpec((tk, rq), lambda i, j, k: (k, i)),
        b_spec=pl.BlockSpec((tk, D), lambda i, j, k: (k, 0)),
        o_spec=pl.BlockSpec((None, rq, D), lambda i, j, k: (i, 0, 0)),
        contract=(0, 0), acc_shape=(rq, D))
    sink.begin("w_out", l, g_wout)

    p = sv["p"]
    du, dv_a, g_sgu_w, db_lanes = _sgu_bwd(f"sgu_bwd_{l}", p, dycat, prm["sgu_wt"][l], prm["sgu_wtt"][l],
                                           prm["sgu_bb"][l])
    sink.advance(du)
    g_sgu_b = db_lanes[:, :A_HEADS].T
    db, dc, dxb, g_conv = _conv_bwd(f"conv_bwd_{l}", p, dycat, prm["conv_w"][l])
    do3, c3 = _mix_bwd(f"mix_bwd_{l}", sv["os"], sv["lses"], dycat, prm["bd"])
    dqs, dks, dvs, dgqs, dgks = [], [], [], [], []
    for g in range(N_PATTERNS):
        dq, dk, dv, dgq, dgk = _attn_bwd(f"attn_bwd_{l}_{g}", p, g, sv["lses"][g], do3, c3,
                                         prm["q_gain"][l], prm["k_gain"][l], prm["bd"])
        dqs.append(dq)
        dks.append(dk)
        dvs.append(dv)
        dgqs.append(dgq)
        dgks.append(dgk)
    g_q = jnp.concatenate(dgqs, axis=1).reshape(N_PATTERNS * PW // HEAD_DIM, HEAD_DIM).sum(axis=0)
    g_k = jnp.concatenate(dgks, axis=1).reshape(N_PATTERNS * PW // HEAD_DIM, HEAD_DIM).sum(axis=0)
    dp = jnp.concatenate([du, dv_a, db, dc, dxb] + [t.astype(BF16) for t in dqs + dks + dvs], axis=1)

    ns_in = w_in.shape[-1]
    dh = _matmul(
        f"in_proj_bwd_{l}", dp, w_in, (S, D), F32, grid=(S // tm, 1, N_CHIPS),
        a_spec=pl.BlockSpec((tm, ns_in), lambda i, j, k: (i, k)),
        b_spec=pl.BlockSpec((None, D, ns_in), lambda i, j, k: (k, 0, 0)),
        o_spec=pl.BlockSpec((tm, D), lambda i, j, k: (i, 0)),
        contract=(1, 1), acc_shape=(tm, D), deps=sink.deps())
    g_win = _matmul(
        f"in_proj_dw_{l}", sv["h"], dp, (N_CHIPS, D, ns_in), F32, grid=(N_CHIPS, D // tmd, nks),
        a_spec=pl.BlockSpec((tk, tmd), lambda i, j, k: (k, j)),
        b_spec=pl.BlockSpec((tk, ns_in), lambda i, j, k: (k, i)),
        o_spec=pl.BlockSpec((None, tmd, ns_in), lambda i, j, k: (i, j, 0)),
        contract=(0, 0), acc_shape=(tmd, ns_in))
    sink.begin("w_in", l, g_win)
    dx0, dx0b, g_attn_norm = _rmsnorm_bwd(f"attn_norm_bwd_{l}", dh, sv["x"], prm["attn_norm"][l], dx1,
                                          deps=sink.deps())
    sink.advance(dx0)

    big = dict(w_in=g_win, w_out=g_wout, w_mlp_in=g_w1, w_mlp_out=g_w2)
    small = dict(attn_norm=g_attn_norm.reshape(-1), sgu_w=g_sgu_w, sgu_b=g_sgu_b, conv_w=g_conv,
                 q_norm=g_q, k_norm=g_k, mlp_norm=g_mlp_norm.reshape(-1))
    return dx0, dx0b, big, small


BIG = ("w_in", "w_out", "w_mlp_in", "w_mlp_out")
SMALL_REPLICATED = ("attn_norm", "sgu_w", "sgu_b", "q_norm", "k_norm", "mlp_norm")


def _local_step(x, target, prm, wg, n_layers, sink):
    saved = []
    h = x
    for l in range(n_layers):
        h, sv = _layer_forward(l, h, prm, wg)
        saved.append(sv)
    dy, dyb, colsq = _loss_kernel(h, target)
    loss = 0.5 * jnp.sum(colsq) / x.shape[1]
    bigs, smalls = [None] * n_layers, [None] * n_layers
    for l in reversed(range(n_layers)):
        dy, dyb, bigs[l], smalls[l] = _layer_backward(l, dy, dyb, saved[l], prm, wg, sink)
    return loss, dy, bigs, smalls


def _prepare_params(attn_norm, sgu_w, sgu_b, conv_full, q_norm, k_norm, mlp_norm):
    n_layers = attn_norm.shape[0]
    tri = jnp.tril(sgu_w)
    idx = jnp.arange(PW)
    bd = (idx[:, None] // HEAD_DIM == idx[None, :] // HEAD_DIM).astype(BF16)
    return dict(
        attn_norm=[attn_norm[l][None, :] for l in range(n_layers)],
        mlp_norm=[mlp_norm[l][None, :] for l in range(n_layers)],
        sgu_wt=[tri[l].astype(BF16) for l in range(n_layers)],
        sgu_wtt=[tri[l].transpose(0, 2, 1).astype(BF16) for l in range(n_layers)],
        sgu_bb=[jnp.repeat(sgu_b[l].T, HEAD_DIM, axis=1) for l in range(n_layers)],
        conv_w=[conv_full[l] for l in range(n_layers)],
        q_gain=[jnp.tile(q_norm[l], PW // HEAD_DIM)[None, :] for l in range(n_layers)],
        k_gain=[jnp.tile(k_norm[l], PW // HEAD_DIM)[None, :] for l in range(n_layers)],
        bd=bd,
    )


def kernel(x, attn_norm, w_in, sgu_w, sgu_b, conv_w, q_norm, k_norm, w_out, mlp_norm, w_mlp_in, w_mlp_out, loss_target, m_attn_norm, m_w_in, m_sgu_w, m_sgu_b, m_conv_w, m_q_norm, m_k_norm, m_w_out, m_mlp_norm, m_w_mlp_in, m_w_mlp_out, v_attn_norm, v_w_in, v_sgu_w, v_sgu_b, v_conv_w, v_q_norm, v_k_norm, v_w_out, v_mlp_norm, v_w_mlp_in, v_w_mlp_out):
    n_layers = attn_norm.shape[0]
    weights = dict(attn_norm=attn_norm, w_in=w_in, sgu_w=sgu_w, sgu_b=sgu_b, conv_w=conv_w, q_norm=q_norm,
                   k_norm=k_norm, w_out=w_out, mlp_norm=mlp_norm, w_mlp_in=w_mlp_in, w_mlp_out=w_mlp_out)
    mom_m = dict(attn_norm=m_attn_norm, w_in=m_w_in, sgu_w=m_sgu_w, sgu_b=m_sgu_b, conv_w=m_conv_w,
                 q_norm=m_q_norm, k_norm=m_k_norm, w_out=m_w_out, mlp_norm=m_mlp_norm, w_mlp_in=m_w_mlp_in,
                 w_mlp_out=m_w_mlp_out)
    mom_v = dict(attn_norm=v_attn_norm, w_in=v_w_in, sgu_w=v_sgu_w, sgu_b=v_sgu_b, conv_w=v_conv_w,
                 q_norm=v_q_norm, k_norm=v_k_norm, w_out=v_w_out, mlp_norm=v_mlp_norm, w_mlp_in=v_w_mlp_in,
                 w_mlp_out=v_w_mlp_out)
    order = ("attn_norm", "w_in", "sgu_w", "sgu_b", "conv_w", "q_norm", "k_norm", "w_out", "mlp_norm",
             "w_mlp_in", "w_mlp_out")
    chip = 2 * lax.axis_index("x") + lax.axis_index("y")
    c_arr = lax.axis_index("c").astype(jnp.int32).reshape(1)

    conv_cols = conv_w.shape[-1]
    chip_arr = chip.astype(jnp.int32).reshape(1)
    conv_pack = jnp.pad(conv_w.reshape(-1), (0, 2048 - conv_w.size)).reshape(1, 16, 128)
    keys = [("conv_w", 0)]
    placed = [_place_shard("place_conv_w", conv_pack, 0, chip_arr, F32)]
    for l in range(n_layers):
        for n in BIG:
            keys.append((n, l))
            placed.append(_place_shard(f"place_{n}_{l}", weights[n], l, chip_arr, BF16))
    wg = _GatheredWeights(keys, placed)
    conv_full = wg.get("conv_w", 0).reshape(N_CHIPS, 2048)[:, :conv_w.size].reshape(N_CHIPS, n_layers, 3, conv_cols)
    conv_full = conv_full.transpose(1, 2, 0, 3).reshape(n_layers, 3, N_CHIPS * conv_cols)
    prm = _prepare_params(attn_norm, sgu_w, sgu_b, conv_full, q_norm, k_norm, mlp_norm)

    sink = _GradReducer(c_arr)
    loss_local, grad_x, _, smalls = _local_step(x[0], loss_target[0], prm, wg, n_layers, sink)
    loss = lax.psum(loss_local, ("x", "y", "c"))

    small_names = SMALL_REPLICATED + ("conv_w",)
    small_shapes = [(n_layers,) + tuple(smalls[0][n].shape) for n in small_names]
    packed = _pack_rows([jnp.stack([smalls[l][n] for l in range(n_layers)]) for n in small_names])
    summed = _unpack_rows(_allreduce_small(packed), small_shapes)
    grads = dict(zip(small_names, summed))
    grads["conv_w"] = lax.dynamic_slice_in_dim(grads["conv_w"], chip * conv_cols, conv_cols, axis=2)

    delta, new_m, new_v = {}, {}, {}

    def update(names, after):
        joined = sink.finish(names, n_layers, after)
        for n in names:
            shp = weights[n].shape
            two_d = (shp[0] * shp[1], shp[2])
            grads[n] = joined[n]
            d, nm, nv = _adamw(f"adamw_{n}", weights[n].reshape(two_d), joined[n].reshape(two_d),
                               mom_m[n].reshape(two_d), mom_v[n].reshape(two_d))
            delta[n], new_m[n], new_v[n] = d.reshape(shp), nm.reshape(shp), nv.reshape(shp)

    update(("w_mlp_out", "w_mlp_in", "w_out"), sink.deps()[-1])
    update(("w_in",), delta["w_out"])
    smalls_all = SMALL_REPLICATED + ("conv_w",)
    shapes = [weights[n].shape for n in smalls_all]
    d, nm, nv = _adamw("adamw_small",
                       _pack_rows([weights[n] for n in smalls_all]), _pack_rows([grads[n] for n in smalls_all]),
                       _pack_rows([mom_m[n] for n in smalls_all]), _pack_rows([mom_v[n] for n in smalls_all]))
    for n, dd, mm, vv in zip(smalls_all, _unpack_rows(d, shapes), _unpack_rows(nm, shapes), _unpack_rows(nv, shapes)):
        delta[n], new_m[n], new_v[n] = dd, mm, vv

    return (loss, grad_x[None], *[grads[n] for n in order], *[delta[n] for n in order],
            *[new_m[n] for n in order], *[new_v[n] for n in order])
```

```python
import jax
import jax.numpy as jnp
from jax import lax
from jax.experimental import pallas as pl
from jax.experimental.pallas import tpu as pltpu

F32 = jnp.float32
BF16 = jnp.bfloat16
SDS = jax.ShapeDtypeStruct

EPS = 1e-6
HEAD_DIM = 64
A_HEADS = 8
A_WIDTH = 512
CHUNK = 128
B_WIDTH = 768
C_WIDTH = 768
N_PATTERNS = 3
PATTERN_DILATION = (1, 4, 16)
PW = 256
D_IN_PROJ = 5632
OFF_AU, OFF_AV, OFF_BB, OFF_BC, OFF_BX, OFF_Q, OFF_K, OFF_V = 0, 512, 1024, 1792, 2560, 3328, 4096, 4864
N_CHIPS = 4
N_DEV = 8
BLK = 128

ADAM_LR, ADAM_B1, ADAM_B2, ADAM_EPS, ADAM_WD, ADAM_STEP = 0.001, 0.9, 0.999, 1e-08, 0.01, 10

V7X_VMEM_LIMIT = 56 * 1024 * 1024
MESH = pl.DeviceIdType.MESH
NEG = -1e30


def _cp(n_axes):
    return pltpu.CompilerParams(dimension_semantics=("arbitrary",) * n_axes, vmem_limit_bytes=V7X_VMEM_LIMIT)


def _hbm_spec():
    return pl.BlockSpec(memory_space=pl.ANY)


def _norm_matmul(name, x, g, wg, out_dtype, deps=()):
    S, D = x.shape
    ns, _, Ns = wg.shape
    tm = min(512, S)
    n_dep = len(deps)

    def body(x_ref, g_ref, w_ref, *rest):
        o_ref, h_ref, hs_ref = rest[n_dep:]
        @pl.when(pl.program_id(1) == 0)
        def _():
            xv = x_ref[...]
            y = xv * lax.rsqrt(jnp.mean(xv * xv, axis=-1, keepdims=True) + EPS) * g_ref[...]
            hb = y.astype(BF16)
            hs_ref[...] = hb
            h_ref[...] = hb
        o_ref[...] = jnp.dot(hs_ref[...], w_ref[...], preferred_element_type=F32).astype(o_ref.dtype)

    return pl.pallas_call(
        body, name=name, grid=(S // tm, ns),
        in_specs=[pl.BlockSpec((tm, D), lambda i, s: (i, 0)),
                  pl.BlockSpec((1, D), lambda i, s: (0, 0)),
                  pl.BlockSpec((None, D, Ns), lambda i, s: (s, 0, 0))] + [_hbm_spec()] * n_dep,
        out_specs=[pl.BlockSpec((tm, Ns), lambda i, s: (i, s)),
                   pl.BlockSpec((tm, D), lambda i, s: (i, 0))],
        out_shape=[SDS((S, ns * Ns), out_dtype), SDS((S, D), BF16)],
        scratch_shapes=[pltpu.VMEM((tm, D), BF16)],
        compiler_params=_cp(2),
    )(x, g, wg, *deps)


def _matmul(name, a, b, out_shape, out_dtype, *, grid, a_spec, b_spec, o_spec, contract, acc_shape,
            extras=(), extra_specs=(), a_pre=None, epi=None, deps=()):
    nk = grid[2]
    n_ex = len(extras)
    n_dep = len(deps)
    dims = (((contract[0],), (contract[1],)), ((), ()))

    def body(a_ref, b_ref, *rest):
        ex = rest[:n_ex]
        o_ref = rest[n_ex + n_dep]
        acc_ref = rest[n_ex + n_dep + 1]
        k = pl.program_id(2)

        @pl.when(k == 0)
        def _():
            acc_ref[...] = jnp.zeros_like(acc_ref)

        av = a_ref[...]
        if a_pre is not None:
            av = a_pre(av)
        acc_ref[...] += lax.dot_general(av, b_ref[...], dims, preferred_element_type=F32)

        @pl.when(k == nk - 1)
        def _():
            r = acc_ref[...]
            if epi is not None:
                r = epi(r, *[e[...] for e in ex])
            o_ref[...] = r.astype(o_ref.dtype)

    return pl.pallas_call(
        body, name=name, grid=grid,
        in_specs=[a_spec, b_spec, *extra_specs] + [_hbm_spec()] * n_dep,
        out_specs=o_spec,
        out_shape=SDS(out_shape, out_dtype),
        scratch_shapes=[pltpu.VMEM(acc_shape, F32)],
        compiler_params=_cp(3),
    )(a, b, *extras, *deps)


def _relu2_bf16(t):
    r = jnp.maximum(t.astype(F32), 0.0)
    return (r * r).astype(BF16)


def _loss_kernel(y, t):
    S, D = y.shape
    tm = min(256, S)

    def body(y_ref, t_ref, dy_ref, dyb_ref, l_ref):
        @pl.when(pl.program_id(0) == 0)
        def _():
            l_ref[...] = jnp.zeros_like(l_ref)
        e = y_ref[...] - t_ref[...]
        l_ref[...] += jnp.sum(e * e, axis=0, keepdims=True)
        dy = e * (1.0 / D)
        dy_ref[...] = dy
        dyb_ref[...] = dy.astype(BF16)

    row = pl.BlockSpec((tm, D), lambda i: (i, 0))
    return pl.pallas_call(
        body, name="loss_head", grid=(S // tm,),
        in_specs=[row, row],
        out_specs=[row, row, pl.BlockSpec((1, D), lambda i: (0, 0))],
        out_shape=[SDS((S, D), F32), SDS((S, D), BF16), SDS((1, D), F32)],
        compiler_params=_cp(1),
    )(y, t)


def _rmsnorm_bwd(name, dh, x, g, dres, deps=()):
    S, D = x.shape
    tm = min(256, S)
    n_dep = len(deps)

    def body(dh_ref, x_ref, g_ref, dres_ref, *rest):
        dx_ref, dxb_ref, dg_ref = rest[n_dep:]
        @pl.when(pl.program_id(0) == 0)
        def _():
            dg_ref[...] = jnp.zeros_like(dg_ref)
        xv = x_ref[...]
        dhv = dh_ref[...]
        rstd = lax.rsqrt(jnp.mean(xv * xv, axis=-1, keepdims=True) + EPS)
        xhat = xv * rstd
        dg_ref[...] += jnp.sum(dhv * xhat, axis=0, keepdims=True)
        dxn = dhv * g_ref[...]
        dx = dres_ref[...] + rstd * (dxn - xhat * jnp.mean(dxn * xhat, axis=-1, keepdims=True))
        dx_ref[...] = dx
        dxb_ref[...] = dx.astype(BF16)

    row = pl.BlockSpec((tm, D), lambda i: (i, 0))
    vec = pl.BlockSpec((1, D), lambda i: (0, 0))
    return pl.pallas_call(
        body, name=name, grid=(S // tm,),
        in_specs=[row, row, vec, row] + [_hbm_spec()] * n_dep,
        out_specs=[row, row, vec],
        out_shape=[SDS((S, D), F32), SDS((S, D), BF16), SDS((1, D), F32)],
        compiler_params=_cp(1),
    )(dh, x, g, dres, *deps)


def _adamw(name, w, g, m, v):
    R, C = w.shape
    tr = 256 if R % 256 == 0 else R
    c1 = 1.0 - ADAM_B1 ** ADAM_STEP
    c2 = 1.0 - ADAM_B2 ** ADAM_STEP

    def body(w_ref, g_ref, m_ref, v_ref, d_ref, nm_ref, nv_ref):
        gv = g_ref[...]
        nm = ADAM_B1 * m_ref[...] + (1.0 - ADAM_B1) * gv
        nv = ADAM_B2 * v_ref[...] + (1.0 - ADAM_B2) * (gv * gv)
        m_hat = nm / c1
        v_hat = nv / c2
        d_ref[...] = -ADAM_LR * (m_hat / (jnp.sqrt(v_hat) + ADAM_EPS) + ADAM_WD * w_ref[...])
        nm_ref[...] = nm
        nv_ref[...] = nv

    blk = pl.BlockSpec((tr, C), lambda i: (i, 0))
    return pl.pallas_call(
        body, name=name, grid=(R // tr,),
        in_specs=[blk] * 4, out_specs=[blk] * 3,
        out_shape=[SDS((R, C), F32)] * 3,
        compiler_params=_cp(1),
    )(w, g, m, v)


def _pair_select(lane, lo, hi):
    return jnp.where(lane < HEAD_DIM, lo, hi)


def _sgu_fwd(name, p, wt, bb):
    S = p.shape[0]

    def body(u_ref, v_ref, wt_ref, bb_ref, o_ref):
        lane = lax.broadcasted_iota(jnp.int32, (CHUNK, 128), 1)
        for pp in range(A_HEADS // 2):
            cs = slice(128 * pp, 128 * (pp + 1))
            vb = v_ref[:, cs].astype(BF16)
            mixed = _pair_select(lane,
                                 jnp.dot(wt_ref[2 * pp], vb, preferred_element_type=F32),
                                 jnp.dot(wt_ref[2 * pp + 1], vb, preferred_element_type=F32)) + bb_ref[:, cs]
            o_ref[:, cs] = (u_ref[:, cs] * mixed).astype(o_ref.dtype)

    return pl.pallas_call(
        body, name=name, grid=(S // CHUNK,),
        in_specs=[pl.BlockSpec((CHUNK, A_WIDTH), lambda c: (c, OFF_AU // A_WIDTH)),
                  pl.BlockSpec((CHUNK, A_WIDTH), lambda c: (c, OFF_AV // A_WIDTH)),
                  pl.BlockSpec((A_HEADS, CHUNK, CHUNK), lambda c: (0, 0, 0)),
                  pl.BlockSpec((CHUNK, A_WIDTH), lambda c: (0, 0))],
        out_specs=pl.BlockSpec((CHUNK, A_WIDTH), lambda c: (c, 0)),
        out_shape=SDS((S, A_WIDTH), BF16),
        compiler_params=_cp(1),
    )(p, p, wt, bb)


def _sgu_bwd(name, p, dycat, wt, wtt, bb):
    S = p.shape[0]

    def body(u_ref, v_ref, dy_ref, wt_ref, wtt_ref, bb_ref, du_ref, dv_ref, dw_ref, db_ref, dbacc_ref):
        c = pl.program_id(0)

        @pl.when(c == 0)
        def _():
            dw_ref[...] = jnp.zeros_like(dw_ref)
            dbacc_ref[...] = jnp.zeros_like(dbacc_ref)

        lane = lax.broadcasted_iota(jnp.int32, (CHUNK, 128), 1)
        row = lax.broadcasted_iota(jnp.int32, (CHUNK, 128), 0)
        causal = row >= lane
        for pp in range(A_HEADS // 2):
            cs = slice(128 * pp, 128 * (pp + 1))
            v = v_ref[:, cs]
            vb = v.astype(BF16)
            u = u_ref[:, cs]
            dy = dy_ref[:, cs]
            mixed = _pair_select(lane,
                                 jnp.dot(wt_ref[2 * pp], vb, preferred_element_type=F32),
                                 jnp.dot(wt_ref[2 * pp + 1], vb, preferred_element_type=F32)) + bb_ref[:, cs]
            du_ref[:, cs] = (dy * mixed).astype(du_ref.dtype)
            dm = dy * u
            dmb = dm.astype(BF16)
            dv = _pair_select(lane,
                              jnp.dot(wtt_ref[2 * pp], dmb, preferred_element_type=F32),
                              jnp.dot(wtt_ref[2 * pp + 1], dmb, preferred_element_type=F32))
            dv_ref[:, cs] = dv.astype(dv_ref.dtype)
            dbacc_ref[:, cs] += dm
            nt = (((1,), (1,)), ((), ()))
            dm_lo = jnp.where(lane < HEAD_DIM, dm, 0.0).astype(BF16)
            dm_hi = jnp.where(lane >= HEAD_DIM, dm, 0.0).astype(BF16)
            dw_ref[2 * pp] += jnp.where(causal, lax.dot_general(dm_lo, vb, nt, preferred_element_type=F32), 0.0)
            dw_ref[2 * pp + 1] += jnp.where(causal, lax.dot_general(dm_hi, vb, nt, preferred_element_type=F32), 0.0)

        @pl.when(c == S // CHUNK - 1)
        def _():
            out = jnp.zeros((CHUNK, 128), F32)
            for pp in range(A_HEADS // 2):
                acc = dbacc_ref[:, 128 * pp:128 * (pp + 1)]
                s_lo = jnp.sum(jnp.where(lane < HEAD_DIM, acc, 0.0), axis=1, keepdims=True)
                s_hi = jnp.sum(jnp.where(lane >= HEAD_DIM, acc, 0.0), axis=1, keepdims=True)
                out = jnp.where(lane == 2 * pp, s_lo, out)
                out = jnp.where(lane == 2 * pp + 1, s_hi, out)
            db_ref[...] = out

    chunk = lambda col: pl.BlockSpec((CHUNK, A_WIDTH), lambda c: (c, col))
    wspec = pl.BlockSpec((A_HEADS, CHUNK, CHUNK), lambda c: (0, 0, 0))
    return pl.pallas_call(
        body, name=name, grid=(S // CHUNK,),
        in_specs=[chunk(OFF_AU // A_WIDTH), chunk(OFF_AV // A_WIDTH), chunk(0), wspec, wspec,
                  pl.BlockSpec((CHUNK, A_WIDTH), lambda c: (0, 0))],
        out_specs=[chunk(0), chunk(0), wspec, pl.BlockSpec((CHUNK, 128), lambda c: (0, 0))],
        out_shape=[SDS((S, A_WIDTH), BF16), SDS((S, A_WIDTH), BF16),
                   SDS((A_HEADS, CHUNK, CHUNK), F32), SDS((CHUNK, 128), F32)],
        scratch_shapes=[pltpu.VMEM((CHUNK, A_WIDTH), F32)],
        compiler_params=_cp(1),
    )(p, p, dycat, wt, wtt, bb)


CONV_HALO = 8


def _shift_down(a, halo, k):
    T = a.shape[0]
    row = lax.broadcasted_iota(jnp.int32, a.shape, 0)
    out = pltpu.roll(a, k, 0)
    for r in range(k):
        out = jnp.where(row == r, halo[CONV_HALO - k + r:CONV_HALO - k + r + 1, :], out)
    return out


def _shift_up(a, halo, k):
    T = a.shape[0]
    row = lax.broadcasted_iota(jnp.int32, a.shape, 0)
    out = pltpu.roll(a, T - k, 0)
    for r in range(k):
        out = jnp.where(row == T - k + r, halo[r:r + 1, :], out)
    return out


def _conv_specs(S, T):
    hb = T // CONV_HALO
    last = S // CONV_HALO - 1
    tile = lambda col0: pl.BlockSpec((T, 128), lambda j, i: (i, col0 + j))
    prev = lambda col0: pl.BlockSpec((CONV_HALO, 128), lambda j, i: (jnp.maximum(i * hb - 1, 0), col0 + j))
    nxt = lambda col0: pl.BlockSpec((CONV_HALO, 128), lambda j, i: (jnp.minimum((i + 1) * hb, last), col0 + j))
    return tile, prev, nxt


def _conv_fwd(name, p, w):
    S = p.shape[0]
    T = min(512, S)
    tile, prev, _ = _conv_specs(S, T)
    cb, cc, cx = OFF_BB // 128, OFF_BC // 128, OFF_BX // 128

    def body(b_ref, c_ref, x_ref, ch_ref, xh_ref, w_ref, o_ref):
        i = pl.program_id(1)
        z = c_ref[...] * x_ref[...]
        zh = jnp.where(i > 0, ch_ref[...] * xh_ref[...], 0.0)
        z1 = _shift_down(z, zh, 1)
        z2 = _shift_down(z, zh, 2)
        conv = w_ref[0:1, :] * z2 + w_ref[1:2, :] * z1 + w_ref[2:3, :] * z
        o_ref[...] = (b_ref[...] * conv).astype(o_ref.dtype)

    return pl.pallas_call(
        body, name=name, grid=(B_WIDTH // 128, S // T),
        in_specs=[tile(cb), tile(cc), tile(cx), prev(cc), prev(cx),
                  pl.BlockSpec((3, 128), lambda j, i: (0, j))],
        out_specs=tile(0),
        out_shape=SDS((S, B_WIDTH), BF16),
        compiler_params=_cp(2),
    )(p, p, p, p, p, w)


def _conv_bwd(name, p, dycat, w):
    S = p.shape[0]
    T = min(512, S)
    tile, prev, nxt = _conv_specs(S, T)
    cb, cc, cx = OFF_BB // 128, OFF_BC // 128, OFF_BX // 128
    cdy = A_WIDTH // 128
    n_i = S // T

    def body(b_ref, c_ref, x_ref, dy_ref, ch_ref, xh_ref, bn_ref, dyn_ref, w_ref,
             db_ref, dc_ref, dx_ref, dw_ref):
        i = pl.program_id(1)

        @pl.when(i == 0)
        def _():
            dw_ref[...] = jnp.zeros_like(dw_ref)

        cv = c_ref[...]
        xv = x_ref[...]
        z = cv * xv
        zh = jnp.where(i > 0, ch_ref[...] * xh_ref[...], 0.0)
        z1 = _shift_down(z, zh, 1)
        z2 = _shift_down(z, zh, 2)
        w0, w1, w2 = w_ref[0:1, :], w_ref[1:2, :], w_ref[2:3, :]
        conv = w0 * z2 + w1 * z1 + w2 * z
        dy = dy_ref[...]
        db_ref[...] = (dy * conv).astype(db_ref.dtype)
        dconv = dy * b_ref[...]
        dconv_n = jnp.where(i < n_i - 1, dyn_ref[...] * bn_ref[...], 0.0)
        dz = w2 * dconv + w1 * _shift_up(dconv, dconv_n, 1) + w0 * _shift_up(dconv, dconv_n, 2)
        dc_ref[...] = (dz * xv).astype(dc_ref.dtype)
        dx_ref[...] = (dz * cv).astype(dx_ref.dtype)
        dw_ref[0:1, :] += jnp.sum(dconv * z2, axis=0, keepdims=True)
        dw_ref[1:2, :] += jnp.sum(dconv * z1, axis=0, keepdims=True)
        dw_ref[2:3, :] += jnp.sum(dconv * z, axis=0, keepdims=True)

    wspec = pl.BlockSpec((3, 128), lambda j, i: (0, j))
    return pl.pallas_call(
        body, name=name, grid=(B_WIDTH // 128, n_i),
        in_specs=[tile(cb), tile(cc), tile(cx), tile(cdy), prev(cc), prev(cx), nxt(cb), nxt(cdy), wspec],
        out_specs=[tile(0), tile(0), tile(0), wspec],
        out_shape=[SDS((S, B_WIDTH), BF16)] * 3 + [SDS((3, B_WIDTH), F32)],
        compiler_params=_cp(2),
    )(p, p, p, dycat, p, p, p, dycat, w)


def _seg_sum(t, bd):
    hi = t.astype(BF16)
    lo = (t - hi.astype(F32)).astype(BF16)
    return jnp.dot(hi, bd, preferred_element_type=F32) + jnp.dot(lo, bd, preferred_element_type=F32)


def _head_norm(x, g, bd):
    rstd = lax.rsqrt(_seg_sum(x * x, bd) * (1.0 / HEAD_DIM) + EPS)
    xhat = x * rstd
    return xhat * g, xhat, rstd


def _head_norm_bwd(dy, g, xhat, rstd, bd):
    dxh = dy * g
    return rstd * (dxh - xhat * (_seg_sum(dxh * xhat, bd) * (1.0 / HEAD_DIM)))


def _band_mask(has_prev):
    row = lax.broadcasted_iota(jnp.int32, (BLK, 2 * BLK), 0)
    col = lax.broadcasted_iota(jnp.int32, (BLK, 2 * BLK), 1)
    first_key = jnp.where(has_prev, 0, BLK)
    return (col >= row) & (col <= row + BLK) & (col >= first_key)


def _first_of_segment(g, n, n_blocks):
    per_seg = lax.shift_right_logical(jnp.int32(n_blocks), 2 * g)
    return (n & (per_seg - 1)) == 0


def _residue_rows(r, d):
    return slice(None) if d == 1 else pl.ds(r, BLK, stride=d)


HW = 128


def _for_residues(d, fn):
    if d == 1:
        fn(0)
    else:
        lax.fori_loop(0, d, lambda r, carry: (fn(r), carry)[1], 0)


def _attn_fwd(name, p, g, gq, gk, bd):
    S = p.shape[0]
    d = PATTERN_DILATION[g]
    rows = BLK * d
    nt = (((1,), (1,)), ((), ()))

    def body(q_ref, kc_ref, kp_ref, vc_ref, vp_ref, gq_ref, gk_ref, bd_ref, o_ref, lse_ref):
        has_prev = pl.program_id(1) > 0
        bdv = bd_ref[...]
        band = _band_mask(has_prev)
        lane = lax.broadcasted_iota(jnp.int32, (1, HW), 1)

        def residue(r):
            rr = _residue_rows(r, d)
            qn, _, _ = _head_norm(q_ref[rr, :], gq_ref[...], bdv)
            kn, _, _ = _head_norm(jnp.concatenate([kp_ref[rr, :], kc_ref[rr, :]], axis=0), gk_ref[...], bdv)
            knb = kn.astype(BF16)
            vb = jnp.concatenate([vp_ref[rr, :], vc_ref[rr, :]], axis=0).astype(BF16)
            o_acc = jnp.zeros((BLK, HW), F32)
            l_acc = jnp.zeros((BLK, HW), F32)
            for j in range(HW // HEAD_DIM):
                hm = (lane >= HEAD_DIM * j) & (lane < HEAD_DIM * (j + 1))
                qj = jnp.where(hm, qn, 0.0).astype(BF16)
                s = lax.dot_general(qj, knb, nt, preferred_element_type=F32) * (HEAD_DIM ** -0.5)
                s = jnp.where(band, s, NEG)
                m = jnp.max(s, axis=1, keepdims=True)
                e = jnp.exp(s - m)
                den = jnp.sum(e, axis=1, keepdims=True)
                pv = jnp.dot(e.astype(BF16), vb, preferred_element_type=F32)
                o_acc = jnp.where(hm, pv / den, o_acc)
                l_acc = jnp.where(hm, m + jnp.log(den), l_acc)
            o_ref[rr, :] = o_acc
            lse_ref[rr, :] = l_acc

        _for_residues(d, residue)

    per = PW // HW
    cq, ck, cv = (OFF_Q + PW * g) // HW, (OFF_K + PW * g) // HW, (OFF_V + PW * g) // HW
    cur = lambda col: pl.BlockSpec((rows, HW), lambda h, n: (n, col + h))
    prv = lambda col: pl.BlockSpec((rows, HW), lambda h, n: (jnp.maximum(n - 1, 0), col + h))
    vec = pl.BlockSpec((1, HW), lambda h, n: (0, h))
    return pl.pallas_call(
        body, name=name, grid=(per, S // rows),
        in_specs=[cur(cq), cur(ck), prv(ck), cur(cv), prv(cv), vec, vec, pl.BlockSpec((HW, HW), lambda h, n: (0, 0))],
        out_specs=[cur(0), cur(0)],
        out_shape=[SDS((S, PW), F32)] * 2,
        compiler_params=_cp(2),
    )(p, p, p, p, p, gq, gk, bd)


def _attn_bwd(name, p, g, lse, do3, c3, gq, gk, bd):
    S = p.shape[0]
    d = PATTERN_DILATION[g]
    rows = BLK * d
    nblk = S // rows
    nt = (((1,), (1,)), ((), ()))
    tn = (((0,), (0,)), ((), ()))

    def body(q_ref, kc_ref, kp_ref, vc_ref, vp_ref, lse_ref, do_ref, c_ref, gq_ref, gk_ref, bd_ref,
             dq_ref, dk_ref, dv_ref, dgq_ref, dgk_ref, ck_ref, cv_ref):
        n = pl.program_id(1)
        keep = jnp.where(n < nblk, 1.0, 0.0)
        has_prev = jnp.minimum(n, nblk - 1) > 0

        @pl.when(n == 0)
        def _():
            ck_ref[...] = jnp.zeros_like(ck_ref)
            cv_ref[...] = jnp.zeros_like(cv_ref)
            dgq_ref[...] = jnp.zeros_like(dgq_ref)
            dgk_ref[...] = jnp.zeros_like(dgk_ref)

        bdv = bd_ref[...]
        gqv = gq_ref[...]
        gkv = gk_ref[...]
        band = _band_mask(has_prev)
        lane = lax.broadcasted_iota(jnp.int32, (1, HW), 1)

        def residue(r):
            rr = _residue_rows(r, d)
            qn, qhat, qrstd = _head_norm(q_ref[rr, :], gqv, bdv)
            kn, khat, krstd = _head_norm(jnp.concatenate([kp_ref[rr, :], kc_ref[rr, :]], axis=0), gkv, bdv)
            knb = kn.astype(BF16)
            vb = jnp.concatenate([vp_ref[rr, :], vc_ref[rr, :]], axis=0).astype(BF16)
            lse_v = lse_ref[rr, :]
            do = do_ref[rr, :]
            cc = c_ref[rr, :]
            dqn = jnp.zeros((BLK, HW), F32)
            dkn = jnp.zeros((2 * BLK, HW), F32)
            dvv = jnp.zeros((2 * BLK, HW), F32)
            for j in range(HW // HEAD_DIM):
                hm = (lane >= HEAD_DIM * j) & (lane < HEAD_DIM * (j + 1))
                qj = jnp.where(hm, qn, 0.0).astype(BF16)
                doj = jnp.where(hm, do, 0.0).astype(BF16)
                s = lax.dot_general(qj, knb, nt, preferred_element_type=F32) * (HEAD_DIM ** -0.5)
                lse_j = jnp.max(jnp.where(hm, lse_v, NEG), axis=1, keepdims=True)
                c_j = jnp.max(jnp.where(hm, cc, NEG), axis=1, keepdims=True)
                prob = jnp.where(band, jnp.exp(s - lse_j), 0.0)
                dp = lax.dot_general(doj, vb, nt, preferred_element_type=F32)
                ds = (prob * (dp + c_j) * (HEAD_DIM ** -0.5)).astype(BF16)
                dqn = jnp.where(hm, jnp.dot(ds, knb, preferred_element_type=F32), dqn)
                dkn += lax.dot_general(ds, qj, tn, preferred_element_type=F32)
                dvv += lax.dot_general(prob.astype(BF16), doj, tn, preferred_element_type=F32)

            dq_ref[rr, :] = _head_norm_bwd(dqn, gqv, qhat, qrstd, bdv)
            dk2 = _head_norm_bwd(dkn, gkv, khat, krstd, bdv)
            dgq_ref[...] += keep * jnp.sum(dqn * qhat, axis=0, keepdims=True)
            dgk_ref[...] += keep * jnp.sum(dkn * khat, axis=0, keepdims=True)
            dk_ref[rr, :] = ck_ref[rr, :] + keep * dk2[:BLK]
            dv_ref[rr, :] = cv_ref[rr, :] + keep * dvv[:BLK]
            ck_ref[rr, :] = dk2[BLK:]
            cv_ref[rr, :] = dvv[BLK:]

        _for_residues(d, residue)

    last = nblk - 1
    per = PW // HW
    cq, ck, cv = (OFF_Q + PW * g) // HW, (OFF_K + PW * g) // HW, (OFF_V + PW * g) // HW
    cur = lambda col: pl.BlockSpec((rows, HW), lambda h, n: (jnp.minimum(n, last), col + h))
    prv = lambda col: pl.BlockSpec((rows, HW), lambda h, n: (jnp.maximum(jnp.minimum(n, last) - 1, 0), col + h))
    cur3 = pl.BlockSpec((None, rows, HW), lambda h, n: (g, jnp.minimum(n, last), h))
    done = pl.BlockSpec((rows, HW), lambda h, n: (jnp.maximum(n - 1, 0), h))
    vec = pl.BlockSpec((1, HW), lambda h, n: (0, h))
    return pl.pallas_call(
        body, name=name, grid=(per, nblk + 1),
        in_specs=[cur(cq), cur(ck), prv(ck), cur(cv), prv(cv), cur(0), cur3, cur3, vec, vec,
                  pl.BlockSpec((HW, HW), lambda h, n: (0, 0))],
        out_specs=[cur(0), done, done, vec, vec],
        out_shape=[SDS((S, PW), F32)] * 3 + [SDS((1, PW), F32)] * 2,
        scratch_shapes=[pltpu.VMEM((rows, HW), F32), pltpu.VMEM((rows, HW), F32)],
        compiler_params=_cp(2),
    )(p, p, p, p, p, lse, do3, c3, gq, gk, bd)


def _mix_fwd(name, os, lses):
    S = os[0].shape[0]
    tm = min(512, S)

    def body(o0, o1, o2, l0, l1, l2, y_ref):
        o = [o0[...], o1[...], o2[...]]
        l = [l0[...], l1[...], l2[...]]
        m = jnp.maximum(jnp.maximum(l[0], l[1]), l[2])
        e = [jnp.exp(t - m) for t in l]
        inv = 1.0 / (e[0] + e[1] + e[2])
        for g in range(N_PATTERNS):
            y_ref[:, PW * g:PW * (g + 1)] = (o[g] * (e[g] * inv)).astype(y_ref.dtype)

    blk = pl.BlockSpec((tm, PW), lambda i: (i, 0))
    return pl.pallas_call(
        body, name=name, grid=(S // tm,),
        in_specs=[blk] * 6,
        out_specs=pl.BlockSpec((tm, C_WIDTH), lambda i: (i, 0)),
        out_shape=SDS((S, C_WIDTH), BF16),
        compiler_params=_cp(1),
    )(*os, *lses)


def _mix_bwd(name, os, lses, dycat, bd):
    S = os[0].shape[0]
    tm = min(512, S)
    c0 = (A_WIDTH + B_WIDTH) // PW

    def body(o0, o1, o2, l0, l1, l2, dy0_ref, dy1_ref, dy2_ref, bd_ref, do_ref, c_ref):
        bdv = bd_ref[...]
        o = [o0[...], o1[...], o2[...]]
        l = [l0[...], l1[...], l2[...]]
        dys = [dy0_ref[...], dy1_ref[...], dy2_ref[...]]
        m = jnp.maximum(jnp.maximum(l[0], l[1]), l[2])
        e = [jnp.exp(t - m) for t in l]
        inv = 1.0 / (e[0] + e[1] + e[2])
        alpha = [t * inv for t in e]
        da = [_seg_sum(dys[g] * o[g], bdv) for g in range(N_PATTERNS)]
        mean_da = alpha[0] * da[0] + alpha[1] * da[1] + alpha[2] * da[2]
        for g in range(N_PATTERNS):
            do_ref[g] = dys[g] * alpha[g]
            c_ref[g] = -alpha[g] * mean_da

    blk = pl.BlockSpec((tm, PW), lambda i: (i, 0))
    blk3 = pl.BlockSpec((N_PATTERNS, tm, PW), lambda i: (0, i, 0))
    dyspec = lambda g: pl.BlockSpec((tm, PW), lambda i: (i, c0 + g))
    return pl.pallas_call(
        body, name=name, grid=(S // tm,),
        in_specs=[blk] * 6 + [dyspec(0), dyspec(1), dyspec(2), pl.BlockSpec((PW, PW), lambda i: (0, 0))],
        out_specs=[blk3, blk3],
        out_shape=[SDS((N_PATTERNS, S, PW), F32)] * 2,
        compiler_params=_cp(1),
    )(*os, *lses, dycat, dycat, dycat, bd)


def _mesh_pos():
    x, y, c = lax.axis_index("x"), lax.axis_index("y"), lax.axis_index("c")
    chips = [(1 - x, y), (x, 1 - y), (1 - x, 1 - y)]
    chip_idx = [2 * cx + cy for cx, cy in chips]
    return x, y, c, 2 * x + y, chips, chip_idx


def _place_shard(name, w, layer, chip_arr, out_dtype, deps=()):
    _, R, C = w.shape
    tr = min(256, R)

    def body(chip_ref, w_ref, *rest):
        o_ref = rest[-1]
        o_ref[...] = w_ref[...].astype(o_ref.dtype)

    return pl.pallas_call(
        body, name=name,
        grid_spec=pltpu.PrefetchScalarGridSpec(
            num_scalar_prefetch=1, grid=(R // tr,),
            in_specs=[pl.BlockSpec((None, tr, C), lambda i, chip_ref: (layer, i, 0))] + [_hbm_spec()] * len(deps),
            out_specs=pl.BlockSpec((None, tr, C), lambda i, chip_ref: (chip_ref[0], i, 0))),
        out_shape=SDS((N_CHIPS, R, C), out_dtype),
        compiler_params=_cp(1),
    )(chip_arr, w, *deps)


HBM_SPEC = pl.BlockSpec(memory_space=pltpu.HBM)
SEM_SPEC = pl.BlockSpec(memory_space=pltpu.SEMAPHORE)
SPLIT_COPY = pltpu.SideEffectType.DATAFLOW_SIDE_EFFECTING
N_PEER_CHIPS = N_CHIPS - 1
TOKEN_SHAPE = SDS((8, 128), F32)
TOKEN_SPEC = pl.BlockSpec(memory_space=pltpu.VMEM)


def _in_hbm(a):
    return pltpu.with_memory_space_constraint(a, pltpu.HBM)


def _gather_start(name, bufs):
    T = len(bufs)

    def body(*refs):
        ins = refs[:T]
        send_sems, recv_sems = refs[T:2 * T], refs[2 * T:3 * T]
        token = refs[4 * T]
        x, y, c, me, chips, chip_idx = _mesh_pos()
        for t in range(T):
            hr = ins[t].shape[1] // 2
            mine = ins[t].at[me, pl.ds(c * hr, hr), :]
            for j in range(N_PEER_CHIPS):
                pltpu.make_async_remote_copy(src_ref=mine, dst_ref=mine, send_sem=send_sems[t].at[j],
                                             recv_sem=recv_sems[t].at[j], device_id=(*chips[j], c),
                                             device_id_type=MESH).start()
        token[...] = jnp.zeros_like(token)

    sems = [pltpu.SemaphoreType.DMA((N_PEER_CHIPS,))] * T
    out = pl.pallas_call(
        body, name=name,
        in_specs=[HBM_SPEC] * T,
        out_specs=[SEM_SPEC] * (2 * T) + [HBM_SPEC] * T + [TOKEN_SPEC],
        out_shape=sems + sems + [pltpu.HBM(b.shape, b.dtype) for b in bufs] + [TOKEN_SHAPE],
        input_output_aliases={t: 2 * T + t for t in range(T)},
        compiler_params=pltpu.CompilerParams(has_side_effects=SPLIT_COPY),
    )(*[_in_hbm(b) for b in bufs])
    return out[:T], out[T:2 * T], out[2 * T:3 * T], out[3 * T]


def _gather_wait(name, buf, send_sem, recv_sem, after):
    n_in = 3 if after is None else 4

    def body(*refs):
        buf_ref, ssem, rsem = refs[:3]
        x, y, c, me, chips, chip_idx = _mesh_pos()
        hr = buf_ref.shape[1] // 2
        mine = buf_ref.at[me, pl.ds(c * hr, hr), :]
        for j in range(N_PEER_CHIPS):
            got = buf_ref.at[chip_idx[j], pl.ds(c * hr, hr), :]
            cp = pltpu.make_async_remote_copy(src_ref=mine, dst_ref=got, send_sem=ssem.at[j], recv_sem=rsem.at[j],
                                              device_id=(*chips[j], c), device_id_type=MESH)
            cp.wait_send()
            cp.wait_recv()

    args = [buf, send_sem, recv_sem] + ([] if after is None else [after])
    return pl.pallas_call(
        body, name=name,
        in_specs=[HBM_SPEC, SEM_SPEC, SEM_SPEC] + [_hbm_spec()] * (n_in - 3),
        out_specs=HBM_SPEC,
        out_shape=pltpu.HBM(buf.shape, buf.dtype),
        input_output_aliases={0: 0},
        compiler_params=pltpu.CompilerParams(has_side_effects=SPLIT_COPY),
    )(*args)


def _forward_start(name, buf):
    def body(buf_ref, send_sems, recv_sems, buf_thru, token):
        x, y, c, me, chips, chip_idx = _mesh_pos()
        hr = buf_ref.shape[1] // 2
        for j in range(N_PEER_CHIPS):
            got = buf_ref.at[chip_idx[j], pl.ds(c * hr, hr), :]
            pltpu.make_async_remote_copy(src_ref=got, dst_ref=got, send_sem=send_sems.at[j], recv_sem=recv_sems.at[j],
                                         device_id=(x, y, 1 - c), device_id_type=MESH).start()
        token[...] = jnp.zeros_like(token)

    sems = pltpu.SemaphoreType.DMA((N_PEER_CHIPS,))
    return pl.pallas_call(
        body, name=name,
        in_specs=[HBM_SPEC],
        out_specs=[SEM_SPEC, SEM_SPEC, HBM_SPEC, TOKEN_SPEC],
        out_shape=[sems, sems, pltpu.HBM(buf.shape, buf.dtype), TOKEN_SHAPE],
        input_output_aliases={0: 2},
        compiler_params=pltpu.CompilerParams(has_side_effects=SPLIT_COPY),
    )(_in_hbm(buf))


def _forward_wait(name, buf, send_sems, recv_sems, after):
    n_in = 3 if after is None else 4

    def body(*refs):
        buf_ref, ssems, rsems = refs[:3]
        x, y, c, me, chips, chip_idx = _mesh_pos()
        hr = buf_ref.shape[1] // 2
        for j in range(N_PEER_CHIPS):
            sent = buf_ref.at[chip_idx[j], pl.ds(c * hr, hr), :]
            theirs = buf_ref.at[chip_idx[j], pl.ds((1 - c) * hr, hr), :]
            cp = pltpu.make_async_remote_copy(src_ref=sent, dst_ref=theirs, send_sem=ssems.at[j],
                                              recv_sem=rsems.at[j], device_id=(x, y, 1 - c), device_id_type=MESH)
            cp.wait_send()
            cp.wait_recv()

    args = [buf, send_sems, recv_sems] + ([] if after is None else [after])
    return pl.pallas_call(
        body, name=name,
        in_specs=[HBM_SPEC, SEM_SPEC, SEM_SPEC] + [_hbm_spec()] * (n_in - 3),
        out_specs=HBM_SPEC,
        out_shape=pltpu.HBM(buf.shape, buf.dtype),
        input_output_aliases={0: 0},
        compiler_params=pltpu.CompilerParams(has_side_effects=SPLIT_COPY),
    )(*args)


class _GatheredWeights:
    def __init__(self):
        self._order = []
        self._pending = {}
        self._forwarding = {}
        self._ready = {}
        self._tokens = []

    def start(self, keys, bufs):
        send_sems, recv_sems, thru, token = _gather_start(f"gather_start_{len(self._order)}", bufs)
        self._tokens.append(token)
        self._order.extend(keys)
        self._pending.update({k: (b, s, r) for k, b, s, r in zip(keys, thru, send_sems, recv_sems)})

    def _prefetch(self, key, after):
        if key in self._pending:
            buf, ssem, rsem = self._pending.pop(key)
            tag = f"{key[0]}_{key[1]}"
            buf = _gather_wait(f"gather_wait_{tag}", buf, ssem, rsem, after)
            ssems, rsems, buf, token = _forward_start(f"gather_fwd_start_{tag}", buf)
            self._forwarding[key] = (buf, ssems, rsems)
            self._tokens.append(token)

    def get(self, name, layer, after=None):
        key = (name, layer)
        if key not in self._ready:
            self._prefetch(key, after)
            buf, ssems, rsems = self._forwarding.pop(key)
            self._ready[key] = _forward_wait(f"gather_fwd_wait_{name}_{layer}", buf, ssems, rsems, after)
            nxt = self._order.index(key) + 1
            if nxt < len(self._order):
                self._prefetch(self._order[nxt], after)
        return self._ready[key]

    def deps(self):
        tokens, self._tokens = self._tokens, []
        return tokens


def _swap_copy(g_ref, land_ref, send_sem, recv_sem):
    x, y, c, _, _, _ = _mesh_pos()
    hr = g_ref.shape[1] // 2
    return pltpu.make_async_remote_copy(src_ref=g_ref.at[:, pl.ds((1 - c) * hr, hr), :], dst_ref=land_ref,
                                        send_sem=send_sem, recv_sem=recv_sem, device_id=(x, y, 1 - c),
                                        device_id_type=MESH)


def _swap_start(name, g):
    land_shape = (g.shape[0], g.shape[1] // 2, g.shape[2])

    def body(g_ref, land_ref, send_sem, recv_sem, land_thru, token):
        _swap_copy(g_ref, land_ref, send_sem, recv_sem).start()
        token[...] = jnp.zeros_like(token)

    return pl.pallas_call(
        body, name=name,
        in_specs=[HBM_SPEC, HBM_SPEC],
        out_specs=[SEM_SPEC, SEM_SPEC, HBM_SPEC, TOKEN_SPEC],
        out_shape=[pltpu.SemaphoreType.DMA(()), pltpu.SemaphoreType.DMA(()), pltpu.HBM(land_shape, g.dtype),
                   TOKEN_SHAPE],
        input_output_aliases={1: 2},
        compiler_params=pltpu.CompilerParams(has_side_effects=SPLIT_COPY),
    )(_in_hbm(g), _in_hbm(lax.empty(land_shape, g.dtype)))


def _swap_wait(name, g, land, send_sem, recv_sem, after):
    def body(g_ref, land_ref, send_sem, recv_sem, after_ref, land_out):
        cp = _swap_copy(g_ref, land_ref, send_sem, recv_sem)
        cp.wait_send()
        cp.wait_recv()

    return pl.pallas_call(
        body, name=name,
        in_specs=[HBM_SPEC, HBM_SPEC, SEM_SPEC, SEM_SPEC, _hbm_spec()],
        out_specs=HBM_SPEC,
        out_shape=pltpu.HBM(land.shape, land.dtype),
        input_output_aliases={1: 0},
        compiler_params=pltpu.CompilerParams(has_side_effects=SPLIT_COPY),
    )(_in_hbm(g), land, send_sem, recv_sem, after)


def _add_my_half(name, g, r, c_arr):
    ns, R, C = g.shape
    hr = R // 2
    tr = min(256, hr)
    nt = hr // tr

    def body(c_ref, g_ref, r_ref, o_ref, land_ref):
        t = (g_ref[...] + r_ref[...]).astype(o_ref.dtype)
        o_ref[...] = t
        land_ref[...] = t

    out = pl.BlockSpec((None, tr, C), lambda s, i, c_ref: (s, i, 0))
    return pl.pallas_call(
        body, name=name,
        grid_spec=pltpu.PrefetchScalarGridSpec(
            num_scalar_prefetch=1, grid=(ns, nt),
            in_specs=[pl.BlockSpec((None, tr, C), lambda s, i, c_ref: (s, c_ref[0] * nt + i, 0)), out],
            out_specs=[out, out]),
        out_shape=[SDS((ns, hr, C), BF16)] * 2,
        compiler_params=_cp(2),
    )(c_arr, g, r)


def _exchange_start(name, part, land):
    def body(part_ref, land_ref, send_sems, recv_sems, land_thru, token):
        x, y, c, me, chips, chip_idx = _mesh_pos()
        for j in range(N_PEER_CHIPS):
            pltpu.make_async_remote_copy(src_ref=part_ref.at[chip_idx[j]], dst_ref=land_ref.at[me],
                                         send_sem=send_sems.at[j], recv_sem=recv_sems.at[j],
                                         device_id=(*chips[j], c), device_id_type=MESH).start()
        token[...] = jnp.zeros_like(token)

    sems = pltpu.SemaphoreType.DMA((N_PEER_CHIPS,))
    return pl.pallas_call(
        body, name=name,
        in_specs=[HBM_SPEC, HBM_SPEC],
        out_specs=[SEM_SPEC, SEM_SPEC, HBM_SPEC, TOKEN_SPEC],
        out_shape=[sems, sems, pltpu.HBM(land.shape, land.dtype), TOKEN_SHAPE],
        input_output_aliases={1: 2},
        compiler_params=pltpu.CompilerParams(has_side_effects=SPLIT_COPY),
    )(_in_hbm(part), _in_hbm(land))


def _exchange_wait(name, part, land, send_sems, recv_sems, after):
    def body(part_ref, land_ref, send_sems, recv_sems, after_ref, land_out):
        x, y, c, me, chips, chip_idx = _mesh_pos()
        for j in range(N_PEER_CHIPS):
            cp = pltpu.make_async_remote_copy(src_ref=part_ref.at[chip_idx[j]], dst_ref=land_ref.at[chip_idx[j]],
                                              send_sem=send_sems.at[j], recv_sem=recv_sems.at[j],
                                              device_id=(*chips[j], c), device_id_type=MESH)
            cp.wait_send()
            cp.wait_recv()

    return pl.pallas_call(
        body, name=name,
        in_specs=[HBM_SPEC, HBM_SPEC, SEM_SPEC, SEM_SPEC, _hbm_spec()],
        out_specs=HBM_SPEC,
        out_shape=pltpu.HBM(land.shape, land.dtype),
        input_output_aliases={1: 0},
        compiler_params=pltpu.CompilerParams(has_side_effects=SPLIT_COPY),
    )(_in_hbm(part), land, send_sems, recv_sems, after)


class _GradReducer:
    def __init__(self, c_arr):
        self._c_arr = c_arr
        self._swapping = []
        self._exchanging = {}
        self._tokens = []

    def begin(self, name, layer, g):
        tag = f"{name}_{layer}"
        ssem, rsem, land, token = _swap_start(f"rs_swap_start_{tag}", g)
        self._swapping.append((name, layer, g, ssem, rsem, land))
        self._tokens.append(token)

    def advance(self, after):
        for name, layer, g, ssem, rsem, land in self._swapping:
            tag = f"{name}_{layer}"
            theirs = _swap_wait(f"rs_swap_wait_{tag}", g, land, ssem, rsem, after)
            part, own = _add_my_half(f"rs_add_{tag}", g, theirs, self._c_arr)
            ssems, rsems, land2, token = _exchange_start(f"rs_xchg_start_{tag}", part, own)
            self._exchanging[(name, layer)] = (part, ssems, rsems, land2)
            self._tokens.append(token)
        self._swapping = []

    def deps(self):
        tokens, self._tokens = self._tokens, []
        return tokens

    def finish(self, names, n_layers, after):
        bufs = []
        for name in names:
            buf = None
            for layer in range(n_layers):
                part, ssems, rsems, land = self._exchanging.pop((name, layer))
                tag = f"{name}_{layer}"
                landed = _exchange_wait(f"rs_xchg_wait_{tag}", part, land, ssems, rsems, after)
                buf = _sum_chips(f"rs_sum_{tag}", landed, self._c_arr, layer, n_layers, buf)
            bufs.append(buf)
        return dict(zip(names, _join_halves(f"rs_join_{names[0]}", bufs)))


def _sum_chips(name, r, c_arr, layer, n_layers, prev):
    ns, H, C = r.shape
    tr = min(256, H)
    nt = H // tr

    def body(c_ref, r_ref, *rest):
        o_ref = rest[-1]
        o_ref[...] = ((r_ref[0].astype(F32) + r_ref[1].astype(F32)) + r_ref[2].astype(F32)) + r_ref[3].astype(F32)

    in_specs = [pl.BlockSpec((ns, tr, C), lambda i, c_ref: (0, i, 0))]
    args = [c_arr, r]
    aliases = {}
    if prev is not None:
        in_specs.append(_hbm_spec())
        args.append(prev)
        aliases = {2: 0}
    return pl.pallas_call(
        body, name=name,
        grid_spec=pltpu.PrefetchScalarGridSpec(
            num_scalar_prefetch=1, grid=(nt,), in_specs=in_specs,
            out_specs=pl.BlockSpec((None, tr, C), lambda i, c_ref: (layer, c_ref[0] * nt + i, 0))),
        out_shape=SDS((n_layers, 2 * H, C), F32),
        input_output_aliases=aliases,
        compiler_params=_cp(1),
    )(*args)


def _join_halves(name, bufs):
    T = len(bufs)

    def body(*refs):
        outs = refs[T:2 * T]
        send_sems, recv_sems = refs[2 * T:]
        x, y, c, _, _, _ = _mesh_pos()
        cps = []
        for t in range(T):
            hr = outs[t].shape[1] // 2
            mine = outs[t].at[:, pl.ds(c * hr, hr), :]
            cp = pltpu.make_async_remote_copy(src_ref=mine, dst_ref=mine, send_sem=send_sems.at[t],
                                              recv_sem=recv_sems.at[t], device_id=(x, y, 1 - c), device_id_type=MESH)
            cp.start()
            cps.append(cp)
        for t in range(T):
            hr = outs[t].shape[1] // 2
            theirs = outs[t].at[:, pl.ds((1 - c) * hr, hr), :]
            pltpu.make_async_remote_copy(src_ref=theirs, dst_ref=theirs, send_sem=send_sems.at[t],
                                         recv_sem=recv_sems.at[t], device_id=(x, y, 1 - c),
                                         device_id_type=MESH).wait_recv()
        for cp in cps:
            cp.wait_send()

    return pl.pallas_call(
        body, name=name,
        in_specs=[_hbm_spec()] * T, out_specs=[_hbm_spec()] * T,
        out_shape=[SDS(b.shape, b.dtype) for b in bufs],
        input_output_aliases={t: t for t in range(T)},
        scratch_shapes=[pltpu.SemaphoreType.DMA((T,)), pltpu.SemaphoreType.DMA((T,))],
    )(*bufs)


def _small_copy(k, buf_ref, land_ref, send_sems, recv_sems):
    x, y, c = lax.axis_index("x"), lax.axis_index("y"), lax.axis_index("c")
    me = 4 * x + 2 * y + c
    peer = (x ^ ((k >> 2) & 1), y ^ ((k >> 1) & 1), c ^ (k & 1))
    cp = pltpu.make_async_remote_copy(src_ref=buf_ref, dst_ref=land_ref.at[me], send_sem=send_sems.at[k - 1],
                                      recv_sem=recv_sems.at[k - 1], device_id=peer, device_id_type=MESH)
    return me, peer, cp


def _small_start(buf, deps):
    land = jnp.broadcast_to(buf[None], (N_DEV,) + buf.shape)
    n_dep = len(deps)

    def body(buf_ref, land_ref, *rest):
        send_sems, recv_sems, _, token = rest[n_dep:]
        for k in range(1, N_DEV):
            _small_copy(k, buf_ref, land_ref, send_sems, recv_sems)[2].start()
        token[...] = jnp.zeros_like(token)

    sems = pltpu.SemaphoreType.DMA((N_DEV - 1,))
    return pl.pallas_call(
        body, name="small_gather_start",
        in_specs=[HBM_SPEC, HBM_SPEC] + [_hbm_spec()] * n_dep,
        out_specs=[SEM_SPEC, SEM_SPEC, HBM_SPEC, TOKEN_SPEC],
        out_shape=[sems, sems, pltpu.HBM(land.shape, land.dtype), TOKEN_SHAPE],
        input_output_aliases={1: 2},
        compiler_params=pltpu.CompilerParams(has_side_effects=SPLIT_COPY),
    )(_in_hbm(buf), _in_hbm(land), *deps)


def _small_wait(buf, land, send_sems, recv_sems, after):
    def body(buf_ref, land_ref, send_sems, recv_sems, after_ref, land_out):
        for k in range(1, N_DEV):
            me, peer, cp = _small_copy(k, buf_ref, land_ref, send_sems, recv_sems)
            cp.wait_send()
            got = land_ref.at[me ^ k]
            pltpu.make_async_remote_copy(src_ref=got, dst_ref=got, send_sem=send_sems.at[k - 1],
                                         recv_sem=recv_sems.at[k - 1], device_id=peer,
                                         device_id_type=MESH).wait_recv()

    return pl.pallas_call(
        body, name="small_gather_wait",
        in_specs=[HBM_SPEC, HBM_SPEC, SEM_SPEC, SEM_SPEC, _hbm_spec()],
        out_specs=HBM_SPEC,
        out_shape=pltpu.HBM(land.shape, land.dtype),
        input_output_aliases={1: 0},
        compiler_params=pltpu.CompilerParams(has_side_effects=SPLIT_COPY),
    )(_in_hbm(buf), land, send_sems, recv_sems, after)


def _sum_devices(land):
    n, R, C = land.shape

    def body(land_ref, out_ref):
        acc = land_ref[0]
        for d in range(1, n):
            acc = acc + land_ref[d]
        out_ref[...] = acc

    return pl.pallas_call(
        body, name="small_sum",
        in_specs=[pl.BlockSpec(memory_space=pltpu.VMEM)],
        out_specs=pl.BlockSpec(memory_space=pltpu.VMEM),
        out_shape=SDS((R, C), land.dtype),
        compiler_params=pltpu.CompilerParams(vmem_limit_bytes=V7X_VMEM_LIMIT),
    )(land)


def _deinterleave(t, d):
    if d == 1:
        return t
    S, W = t.shape
    return t.reshape(S // d, d, W).transpose(1, 0, 2).reshape(S, W)


def _interleave(t, d):
    if d == 1:
        return t
    S, W = t.shape
    return t.reshape(d, S // d, W).transpose(1, 0, 2).reshape(S, W)


def _to_patterns(t, off):
    return jnp.stack([_deinterleave(t[:, off + PW * g:off + PW * (g + 1)], PATTERN_DILATION[g])
                      for g in range(N_PATTERNS)])


def _from_patterns(t3):
    return jnp.stack([_interleave(t3[g], PATTERN_DILATION[g]) for g in range(N_PATTERNS)])


def _pack_rows(vectors):
    flat = jnp.concatenate([v.reshape(-1) for v in vectors])
    n = flat.shape[0]
    padded = -(-n // 1024) * 1024
    return jnp.pad(flat, (0, padded - n)).reshape(padded // 128, 128)


def _unpack_rows(buf, shapes):
    flat = buf.reshape(-1)
    out, off = [], 0
    for s in shapes:
        n = 1
        for dim in s:
            n *= dim
        out.append(flat[off:off + n].reshape(s))
        off += n
    return out


def _layer_forward(l, x, prm, wg):
    S, D = x.shape
    w_in = wg.get("w_in", l, x)
    p, h = _norm_matmul(f"in_proj_{l}", x, prm["attn_norm"][l], w_in, F32, deps=wg.deps())
    y_a = _sgu_fwd(f"sgu_fwd_{l}", p, prm["sgu_wt"][l], prm["sgu_bb"][l])
    y_b = _conv_fwd(f"conv_fwd_{l}", p, prm["conv_w"][l])
    os, lses = [], []
    for g in range(N_PATTERNS):
        o_g, lse_g = _attn_fwd(f"attn_fwd_{l}_{g}", p, g, prm["q_gain"][l], prm["k_gain"][l], prm["bd"])
        os.append(o_g)
        lses.append(lse_g)
    y_c = _mix_fwd(f"mix_fwd_{l}", os, lses)
    ycat = jnp.concatenate([y_a, y_b, y_c], axis=1)
    tm = min(512, S)
    w_out = wg.get("w_out", l, ycat)
    rq = w_out.shape[1]
    x1 = _matmul(
        f"out_proj_{l}", ycat, w_out, (S, D), F32, grid=(S // tm, 1, N_CHIPS),
        a_spec=pl.BlockSpec((tm, rq), lambda i, j, k: (i, k)),
        b_spec=pl.BlockSpec((None, rq, D), lambda i, j, k: (k, 0, 0)),
        o_spec=pl.BlockSpec((tm, D), lambda i, j, k: (i, 0)),
        contract=(1, 0), acc_shape=(tm, D),
        extras=(x,), extra_specs=(pl.BlockSpec((tm, D), lambda i, j, k: (i, 0)),),
        epi=lambda r, res: r + res, deps=wg.deps())
    w_mlp_in = wg.get("w_mlp_in", l, x1)
    a, h2 = _norm_matmul(f"mlp_in_{l}", x1, prm["mlp_norm"][l], w_mlp_in, BF16, deps=wg.deps())
    w_mlp_out = wg.get("w_mlp_out", l, a)
    dff4 = w_mlp_out.shape[1]
    tk = min(1024, dff4)
    kpc = dff4 // tk
    x2 = _matmul(
        f"mlp_out_{l}", a, w_mlp_out, (S, D), F32, grid=(S // tm, 1, N_CHIPS * kpc),
        a_spec=pl.BlockSpec((tm, tk), lambda i, j, k: (i, k)),
        b_spec=pl.BlockSpec((None, tk, D), lambda i, j, k: (k // kpc, k % kpc, 0)),
        o_spec=pl.BlockSpec((tm, D), lambda i, j, k: (i, 0)),
        contract=(1, 0), acc_shape=(tm, D), a_pre=_relu2_bf16,
        extras=(x1,), extra_specs=(pl.BlockSpec((tm, D), lambda i, j, k: (i, 0)),),
        epi=lambda r, res: r + res, deps=wg.deps())
    saved = dict(x=x, p=p, h=h, os=os, lses=lses, ycat=ycat, x1=x1, a=a, h2=h2)
    return x2, saved


def _layer_backward(l, dx2, dx2b, sv, prm, wg, sink):
    S, D = dx2.shape
    w_in, w_out = wg.get("w_in", l), wg.get("w_out", l)
    w_mlp_in, w_mlp_out = wg.get("w_mlp_in", l), wg.get("w_mlp_out", l)
    dff4 = w_mlp_in.shape[-1]
    dff = N_CHIPS * dff4
    tm = min(512, S)
    tk = min(1024, S)
    nks = S // tk

    da = _matmul(
        f"mlp_out_bwd_{l}", dx2b, w_mlp_out, (S, dff), BF16, grid=(S // tm, N_CHIPS, 1),
        a_spec=pl.BlockSpec((tm, D), lambda i, j, k: (i, 0)),
        b_spec=pl.BlockSpec((None, dff4, D), lambda i, j, k: (j, 0, 0)),
        o_spec=pl.BlockSpec((tm, dff4), lambda i, j, k: (i, j)),
        contract=(1, 1), acc_shape=(tm, dff4),
        extras=(sv["a"],), extra_specs=(pl.BlockSpec((tm, dff4), lambda i, j, k: (i, j)),),
        epi=lambda r, act: r * (2.0 * jnp.maximum(act.astype(F32), 0.0)), deps=sink.deps())
    tmw = min(1024, dff4)
    mpc = dff4 // tmw
    g_w2 = _matmul(
        f"mlp_out_dw_{l}", sv["a"], dx2b, (N_CHIPS, dff4, D), F32, grid=(N_CHIPS * mpc, 1, nks),
        a_spec=pl.BlockSpec((tk, tmw), lambda i, j, k: (k, i)),
        b_spec=pl.BlockSpec((tk, D), lambda i, j, k: (k, 0)),
        o_spec=pl.BlockSpec((None, tmw, D), lambda i, j, k: (i // mpc, i % mpc, 0)),
        contract=(0, 0), acc_shape=(tmw, D), a_pre=_relu2_bf16)
    sink.begin("w_mlp_out", l, g_w2)
    dh2 = _matmul(
        f"mlp_in_bwd_{l}", da, w_mlp_in, (S, D), F32, grid=(S // tm, 1, N_CHIPS),
        a_spec=pl.BlockSpec((tm, dff4), lambda i, j, k: (i, k)),
        b_spec=pl.BlockSpec((None, D, dff4), lambda i, j, k: (k, 0, 0)),
        o_spec=pl.BlockSpec((tm, D), lambda i, j, k: (i, 0)),
        contract=(1, 1), acc_shape=(tm, D), deps=sink.deps())
    sink.advance(dh2)
    tmd = min(1024, D)
    g_w1 = _matmul(
        f"mlp_in_dw_{l}", sv["h2"], da, (N_CHIPS, D, dff4), F32, grid=(N_CHIPS, D // tmd, nks),
        a_spec=pl.BlockSpec((tk, tmd), lambda i, j, k: (k, j)),
        b_spec=pl.BlockSpec((tk, dff4), lambda i, j, k: (k, i)),
        o_spec=pl.BlockSpec((None, tmd, dff4), lambda i, j, k: (i, j, 0)),
        contract=(0, 0), acc_shape=(tmd, dff4))
    sink.begin("w_mlp_in", l, g_w1)
    dx1, dx1b, g_mlp_norm = _rmsnorm_bwd(f"mlp_norm_bwd_{l}", dh2, sv["x1"], prm["mlp_norm"][l], dx2,
                                         deps=sink.deps())

    rq = w_out.shape[1]
    dycat = _matmul(
        f"out_proj_bwd_{l}", dx1b, w_out, (S, N_CHIPS * rq), F32, grid=(S // tm, N_CHIPS, 1),
        a_spec=pl.BlockSpec((tm, D), lambda i, j, k: (i, 0)),
        b_spec=pl.BlockSpec((None, rq, D), lambda i, j, k: (j, 0, 0)),
        o_spec=pl.BlockSpec((tm, rq), lambda i, j, k: (i, j)),
        contract=(1, 1), acc_shape=(tm, rq))
    sink.advance(dycat)
    g_wout = _matmul(
        f"out_proj_dw_{l}", sv["ycat"], dx1b, (N_CHIPS, rq, D), F32, grid=(N_CHIPS, 1, nks),
        a_spec=pl.BlockSpec((tk, rq), lambda i, j, k: (k, i)),
        b_spec=pl.BlockSpec((tk, D), lambda i, j, k: (k, 0)),
        o_spec=pl.BlockSpec((None, rq, D), lambda i, j, k: (i, 0, 0)),
        contract=(0, 0), acc_shape=(rq, D))
    sink.begin("w_out", l, g_wout)

    p = sv["p"]
    du, dv_a, g_sgu_w, db_lanes = _sgu_bwd(f"sgu_bwd_{l}", p, dycat, prm["sgu_wt"][l], prm["sgu_wtt"][l],
                                           prm["sgu_bb"][l])
    sink.advance(du)
    g_sgu_b = db_lanes[:, :A_HEADS].T
    db, dc, dxb, g_conv = _conv_bwd(f"conv_bwd_{l}", p, dycat, prm["conv_w"][l])
    do3, c3 = _mix_bwd(f"mix_bwd_{l}", sv["os"], sv["lses"], dycat, prm["bd"])
    dqs, dks, dvs, dgqs, dgks = [], [], [], [], []
    for g in range(N_PATTERNS):
        dq, dk, dv, dgq, dgk = _attn_bwd(f"attn_bwd_{l}_{g}", p, g, sv["lses"][g], do3, c3,
                                         prm["q_gain"][l], prm["k_gain"][l], prm["bd"])
        dqs.append(dq)
        dks.append(dk)
        dvs.append(dv)
        dgqs.append(dgq)
        dgks.append(dgk)
    g_q = jnp.concatenate(dgqs, axis=1).reshape(N_PATTERNS * PW // HEAD_DIM, HEAD_DIM).sum(axis=0)
    g_k = jnp.concatenate(dgks, axis=1).reshape(N_PATTERNS * PW // HEAD_DIM, HEAD_DIM).sum(axis=0)
    dp = jnp.concatenate([du, dv_a, db, dc, dxb] + [t.astype(BF16) for t in dqs + dks + dvs], axis=1)

    ns_in = w_in.shape[-1]
    dh = _matmul(
        f"in_proj_bwd_{l}", dp, w_in, (S, D), F32, grid=(S // tm, 1, N_CHIPS),
        a_spec=pl.BlockSpec((tm, ns_in), lambda i, j, k: (i, k)),
        b_spec=pl.BlockSpec((None, D, ns_in), lambda i, j, k: (k, 0, 0)),
        o_spec=pl.BlockSpec((tm, D), lambda i, j, k: (i, 0)),
        contract=(1, 1), acc_shape=(tm, D), deps=sink.deps())
    g_win = _matmul(
        f"in_proj_dw_{l}", sv["h"], dp, (N_CHIPS, D, ns_in), F32, grid=(N_CHIPS, D // tmd, nks),
        a_spec=pl.BlockSpec((tk, tmd), lambda i, j, k: (k, j)),
        b_spec=pl.BlockSpec((tk, ns_in), lambda i, j, k: (k, i)),
        o_spec=pl.BlockSpec((None, tmd, ns_in), lambda i, j, k: (i, j, 0)),
        contract=(0, 0), acc_shape=(tmd, ns_in))
    sink.begin("w_in", l, g_win)
    dx0, dx0b, g_attn_norm = _rmsnorm_bwd(f"attn_norm_bwd_{l}", dh, sv["x"], prm["attn_norm"][l], dx1,
                                          deps=sink.deps())
    sink.advance(dx0)

    big = dict(w_in=g_win, w_out=g_wout, w_mlp_in=g_w1, w_mlp_out=g_w2)
    small = dict(attn_norm=g_attn_norm.reshape(-1), sgu_w=g_sgu_w, sgu_b=g_sgu_b, conv_w=g_conv,
                 q_norm=g_q, k_norm=g_k, mlp_norm=g_mlp_norm.reshape(-1))
    return dx0, dx0b, big, small


BIG = ("w_in", "w_out", "w_mlp_in", "w_mlp_out")
SMALL_REPLICATED = ("attn_norm", "sgu_w", "sgu_b", "q_norm", "k_norm", "mlp_norm")


def _local_step(x, target, prm, wg, n_layers, sink):
    saved = []
    h = x
    for l in range(n_layers):
        h, sv = _layer_forward(l, h, prm, wg)
        saved.append(sv)
    dy, dyb, colsq = _loss_kernel(h, target)
    loss = 0.5 * jnp.sum(colsq) / x.shape[1]
    bigs, smalls = [None] * n_layers, [None] * n_layers
    for l in reversed(range(n_layers)):
        dy, dyb, bigs[l], smalls[l] = _layer_backward(l, dy, dyb, saved[l], prm, wg, sink)
    return loss, dy, bigs, smalls


def _prepare_params(attn_norm, sgu_w, sgu_b, conv_full, q_norm, k_norm, mlp_norm):
    n_layers = attn_norm.shape[0]
    tri = jnp.tril(sgu_w)
    idx = jnp.arange(PW)
    bd = (idx[:, None] // HEAD_DIM == idx[None, :] // HEAD_DIM).astype(BF16)
    return dict(
        attn_norm=[attn_norm[l][None, :] for l in range(n_layers)],
        mlp_norm=[mlp_norm[l][None, :] for l in range(n_layers)],
        sgu_wt=[tri[l].astype(BF16) for l in range(n_layers)],
        sgu_wtt=[tri[l].transpose(0, 2, 1).astype(BF16) for l in range(n_layers)],
        sgu_bb=[jnp.repeat(sgu_b[l].T, HEAD_DIM, axis=1) for l in range(n_layers)],
        conv_w=[conv_full[l] for l in range(n_layers)],
        q_gain=[jnp.tile(q_norm[l], PW // HEAD_DIM)[None, :] for l in range(n_layers)],
        k_gain=[jnp.tile(k_norm[l], PW // HEAD_DIM)[None, :] for l in range(n_layers)],
        bd=bd,
    )


def kernel(x, attn_norm, w_in, sgu_w, sgu_b, conv_w, q_norm, k_norm, w_out, mlp_norm, w_mlp_in, w_mlp_out, loss_target, m_attn_norm, m_w_in, m_sgu_w, m_sgu_b, m_conv_w, m_q_norm, m_k_norm, m_w_out, m_mlp_norm, m_w_mlp_in, m_w_mlp_out, v_attn_norm, v_w_in, v_sgu_w, v_sgu_b, v_conv_w, v_q_norm, v_k_norm, v_w_out, v_mlp_norm, v_w_mlp_in, v_w_mlp_out):
    n_layers = attn_norm.shape[0]
    weights = dict(attn_norm=attn_norm, w_in=w_in, sgu_w=sgu_w, sgu_b=sgu_b, conv_w=conv_w, q_norm=q_norm,
                   k_norm=k_norm, w_out=w_out, mlp_norm=mlp_norm, w_mlp_in=w_mlp_in, w_mlp_out=w_mlp_out)
    mom_m = dict(attn_norm=m_attn_norm, w_in=m_w_in, sgu_w=m_sgu_w, sgu_b=m_sgu_b, conv_w=m_conv_w,
                 q_norm=m_q_norm, k_norm=m_k_norm, w_out=m_w_out, mlp_norm=m_mlp_norm, w_mlp_in=m_w_mlp_in,
                 w_mlp_out=m_w_mlp_out)
    mom_v = dict(attn_norm=v_attn_norm, w_in=v_w_in, sgu_w=v_sgu_w, sgu_b=v_sgu_b, conv_w=v_conv_w,
                 q_norm=v_q_norm, k_norm=v_k_norm, w_out=v_w_out, mlp_norm=v_mlp_norm, w_mlp_in=v_w_mlp_in,
                 w_mlp_out=v_w_mlp_out)
    order = ("attn_norm", "w_in", "sgu_w", "sgu_b", "conv_w", "q_norm", "k_norm", "w_out", "mlp_norm",
             "w_mlp_in", "w_mlp_out")
    chip = 2 * lax.axis_index("x") + lax.axis_index("y")
    c_arr = lax.axis_index("c").astype(jnp.int32).reshape(1)

    conv_cols = conv_w.shape[-1]
    chip_arr = chip.astype(jnp.int32).reshape(1)
    conv_pack = jnp.pad(conv_w.reshape(-1), (0, 2048 - conv_w.size)).reshape(1, 16, 128)
    wg = _GatheredWeights()
    wg.start([("conv_w", 0), ("w_in", 0)],
             [_place_shard("place_conv_w", conv_pack, 0, chip_arr, F32),
              _place_shard("place_w_in_0", weights["w_in"], 0, chip_arr, BF16)])
    keys = [(n, l) for l in range(n_layers) for n in BIG if (n, l) != ("w_in", 0)]
    first = wg.deps()
    wg.start(keys, [_place_shard(f"place_{n}_{l}", weights[n], l, chip_arr, BF16, deps=first) for n, l in keys])
    conv_full = wg.get("conv_w", 0, wg.deps()[-1]).reshape(N_CHIPS, 2048)[:, :conv_w.size].reshape(N_CHIPS, n_layers, 3, conv_cols)
    conv_full = conv_full.transpose(1, 2, 0, 3).reshape(n_layers, 3, N_CHIPS * conv_cols)
    prm = _prepare_params(attn_norm, sgu_w, sgu_b, conv_full, q_norm, k_norm, mlp_norm)

    sink = _GradReducer(c_arr)
    loss_local, grad_x, _, smalls = _local_step(x[0], loss_target[0], prm, wg, n_layers, sink)
    loss = lax.psum(loss_local, ("x", "y", "c"))

    small_names = SMALL_REPLICATED + ("conv_w",)
    small_shapes = [(n_layers,) + tuple(smalls[0][n].shape) for n in small_names]
    packed = _pack_rows([jnp.stack([smalls[l][n] for l in range(n_layers)]) for n in small_names])
    small_send, small_recv, small_land, small_token = _small_start(packed, sink.deps())

    grads, delta, new_m, new_v = {}, {}, {}, {}

    def update(names, after):
        joined = sink.finish(names, n_layers, after)
        for n in names:
            shp = weights[n].shape
            two_d = (shp[0] * shp[1], shp[2])
            grads[n] = joined[n]
            d, nm, nv = _adamw(f"adamw_{n}", weights[n].reshape(two_d), joined[n].reshape(two_d),
                               mom_m[n].reshape(two_d), mom_v[n].reshape(two_d))
            delta[n], new_m[n], new_v[n] = d.reshape(shp), nm.reshape(shp), nv.reshape(shp)

    update(("w_mlp_out", "w_mlp_in", "w_out"), small_token)
    update(("w_in",), delta["w_out"])
    small_land = _small_wait(packed, small_land, small_send, small_recv, delta["w_in"])
    grads.update(zip(small_names, _unpack_rows(_sum_devices(small_land), small_shapes)))
    grads["conv_w"] = lax.dynamic_slice_in_dim(grads["conv_w"], chip * conv_cols, conv_cols, axis=2)
    smalls_all = SMALL_REPLICATED + ("conv_w",)
    shapes = [weights[n].shape for n in smalls_all]
    d, nm, nv = _adamw("adamw_small",
                       _pack_rows([weights[n] for n in smalls_all]), _pack_rows([grads[n] for n in smalls_all]),
                       _pack_rows([mom_m[n] for n in smalls_all]), _pack_rows([mom_v[n] for n in smalls_all]))
    for n, dd, mm, vv in zip(smalls_all, _unpack_rows(d, shapes), _unpack_rows(nm, shapes), _unpack_rows(nv, shapes)):
        delta[n], new_m[n], new_v[n] = dd, mm, vv

    return (loss, grad_x[None], *[grads[n] for n in order], *[delta[n] for n in order],
            *[new_m[n] for n in order], *[new_v[n] for n in order])
```

```python
import jax
import jax.numpy as jnp
from jax import lax
from jax.experimental import pallas as pl
from jax.experimental.pallas import tpu as pltpu

F32 = jnp.float32
BF16 = jnp.bfloat16
SDS = jax.ShapeDtypeStruct

EPS = 1e-6
HEAD_DIM = 64
A_HEADS = 8
A_WIDTH = 512
CHUNK = 128
B_WIDTH = 768
C_WIDTH = 768
N_PATTERNS = 3
PATTERN_DILATION = (1, 4, 16)
PW = 256
D_IN_PROJ = 5632
OFF_AU, OFF_AV, OFF_BB, OFF_BC, OFF_BX, OFF_Q, OFF_K, OFF_V = 0, 512, 1024, 1792, 2560, 3328, 4096, 4864
N_CHIPS = 4
N_DEV = 8
BLK = 128

ADAM_LR, ADAM_B1, ADAM_B2, ADAM_EPS, ADAM_WD, ADAM_STEP = 0.001, 0.9, 0.999, 1e-08, 0.01, 10

V7X_VMEM_LIMIT = 56 * 1024 * 1024
MESH = pl.DeviceIdType.MESH
NEG = -1e30


def _cp(n_axes):
    return pltpu.CompilerParams(dimension_semantics=("arbitrary",) * n_axes, vmem_limit_bytes=V7X_VMEM_LIMIT)


def _hbm_spec():
    return pl.BlockSpec(memory_space=pl.ANY)


def _norm_matmul(name, x, g, wg, out_dtype, deps=()):
    S, D = x.shape
    ns, _, Ns = wg.shape
    tm = min(512, S)
    n_dep = len(deps)

    def body(x_ref, g_ref, w_ref, *rest):
        o_ref, h_ref, hs_ref = rest[n_dep:]
        @pl.when(pl.program_id(1) == 0)
        def _():
            xv = x_ref[...]
            y = xv * lax.rsqrt(jnp.mean(xv * xv, axis=-1, keepdims=True) + EPS) * g_ref[...]
            hb = y.astype(BF16)
            hs_ref[...] = hb
            h_ref[...] = hb
        o_ref[...] = jnp.dot(hs_ref[...], w_ref[...], preferred_element_type=F32).astype(o_ref.dtype)

    return pl.pallas_call(
        body, name=name, grid=(S // tm, ns),
        in_specs=[pl.BlockSpec((tm, D), lambda i, s: (i, 0)),
                  pl.BlockSpec((1, D), lambda i, s: (0, 0)),
                  pl.BlockSpec((None, D, Ns), lambda i, s: (s, 0, 0))] + [_hbm_spec()] * n_dep,
        out_specs=[pl.BlockSpec((tm, Ns), lambda i, s: (i, s)),
                   pl.BlockSpec((tm, D), lambda i, s: (i, 0))],
        out_shape=[SDS((S, ns * Ns), out_dtype), SDS((S, D), BF16)],
        scratch_shapes=[pltpu.VMEM((tm, D), BF16)],
        compiler_params=_cp(2),
    )(x, g, wg, *deps)


def _matmul(name, a, b, out_shape, out_dtype, *, grid, a_spec, b_spec, o_spec, contract, acc_shape,
            extras=(), extra_specs=(), a_pre=None, epi=None, deps=()):
    nk = grid[2]
    n_ex = len(extras)
    n_dep = len(deps)
    dims = (((contract[0],), (contract[1],)), ((), ()))

    def product(a_ref, b_ref):
        av = a_ref[...]
        if a_pre is not None:
            av = a_pre(av)
        return lax.dot_general(av, b_ref[...], dims, preferred_element_type=F32)

    def finish(r, ex, o_ref):
        if epi is not None:
            r = epi(r, *[e[...] for e in ex])
        o_ref[...] = r.astype(o_ref.dtype)

    def body_single(a_ref, b_ref, *rest):
        finish(product(a_ref, b_ref), rest[:n_ex], rest[n_ex + n_dep])

    def body(a_ref, b_ref, *rest):
        ex = rest[:n_ex]
        o_ref = rest[n_ex + n_dep]
        acc_ref = rest[n_ex + n_dep + 1]
        k = pl.program_id(2)

        @pl.when(k == 0)
        def _():
            acc_ref[...] = product(a_ref, b_ref)

        @pl.when((k > 0) & (k < nk - 1))
        def _():
            acc_ref[...] += product(a_ref, b_ref)

        @pl.when(k == nk - 1)
        def _():
            finish(acc_ref[...] + product(a_ref, b_ref), ex, o_ref)

    return pl.pallas_call(
        body_single if nk == 1 else body, name=name, grid=grid,
        in_specs=[a_spec, b_spec, *extra_specs] + [_hbm_spec()] * n_dep,
        out_specs=o_spec,
        out_shape=SDS(out_shape, out_dtype),
        scratch_shapes=[] if nk == 1 else [pltpu.VMEM(acc_shape, F32)],
        compiler_params=_cp(3),
    )(a, b, *extras, *deps)


def _relu2_bf16(t):
    r = jnp.maximum(t.astype(F32), 0.0)
    return (r * r).astype(BF16)


def _loss_kernel(y, t):
    S, D = y.shape
    tm = min(256, S)

    def body(y_ref, t_ref, dy_ref, dyb_ref, l_ref):
        @pl.when(pl.program_id(0) == 0)
        def _():
            l_ref[...] = jnp.zeros_like(l_ref)
        e = y_ref[...] - t_ref[...]
        l_ref[...] += jnp.sum(e * e, axis=0, keepdims=True)
        dy = e * (1.0 / D)
        dy_ref[...] = dy
        dyb_ref[...] = dy.astype(BF16)

    row = pl.BlockSpec((tm, D), lambda i: (i, 0))
    return pl.pallas_call(
        body, name="loss_head", grid=(S // tm,),
        in_specs=[row, row],
        out_specs=[row, row, pl.BlockSpec((1, D), lambda i: (0, 0))],
        out_shape=[SDS((S, D), F32), SDS((S, D), BF16), SDS((1, D), F32)],
        compiler_params=_cp(1),
    )(y, t)


def _rmsnorm_bwd(name, dh, x, g, dres, deps=()):
    S, D = x.shape
    tm = min(256, S)
    n_dep = len(deps)

    def body(dh_ref, x_ref, g_ref, dres_ref, *rest):
        dx_ref, dxb_ref, dg_ref = rest[n_dep:]
        @pl.when(pl.program_id(0) == 0)
        def _():
            dg_ref[...] = jnp.zeros_like(dg_ref)
        xv = x_ref[...]
        dhv = dh_ref[...]
        rstd = lax.rsqrt(jnp.mean(xv * xv, axis=-1, keepdims=True) + EPS)
        xhat = xv * rstd
        dg_ref[...] += jnp.sum(dhv * xhat, axis=0, keepdims=True)
        dxn = dhv * g_ref[...]
        dx = dres_ref[...] + rstd * (dxn - xhat * jnp.mean(dxn * xhat, axis=-1, keepdims=True))
        dx_ref[...] = dx
        dxb_ref[...] = dx.astype(BF16)

    row = pl.BlockSpec((tm, D), lambda i: (i, 0))
    vec = pl.BlockSpec((1, D), lambda i: (0, 0))
    return pl.pallas_call(
        body, name=name, grid=(S // tm,),
        in_specs=[row, row, vec, row] + [_hbm_spec()] * n_dep,
        out_specs=[row, row, vec],
        out_shape=[SDS((S, D), F32), SDS((S, D), BF16), SDS((1, D), F32)],
        compiler_params=_cp(1),
    )(dh, x, g, dres, *deps)


def _adamw(name, w, g, m, v):
    R, C = w.shape
    tr = 256 if R % 256 == 0 else R
    c1 = 1.0 - ADAM_B1 ** ADAM_STEP
    c2 = 1.0 - ADAM_B2 ** ADAM_STEP

    def body(w_ref, g_ref, m_ref, v_ref, d_ref, nm_ref, nv_ref, g_out_ref):
        gv = g_ref[...]
        nm = ADAM_B1 * m_ref[...] + (1.0 - ADAM_B1) * gv
        nv = ADAM_B2 * v_ref[...] + (1.0 - ADAM_B2) * (gv * gv)
        m_hat = nm / c1
        v_hat = nv / c2
        d_ref[...] = -ADAM_LR * (m_hat / (jnp.sqrt(v_hat) + ADAM_EPS) + ADAM_WD * w_ref[...])
        nm_ref[...] = nm
        nv_ref[...] = nv
        g_out_ref[...] = gv

    blk = pl.BlockSpec((tr, C), lambda i: (i, 0))
    return pl.pallas_call(
        body, name=name, grid=(R // tr,),
        in_specs=[blk] * 4, out_specs=[blk] * 4,
        out_shape=[SDS((R, C), F32)] * 4,
        compiler_params=_cp(1),
    )(w, g, m, v)


def _pair_select(lane, lo, hi):
    return jnp.where(lane < HEAD_DIM, lo, hi)


def _sgu_fwd(name, p, wt, bb):
    S = p.shape[0]

    def body(u_ref, v_ref, wt_ref, bb_ref, o_ref):
        lane = lax.broadcasted_iota(jnp.int32, (CHUNK, 128), 1)
        for pp in range(A_HEADS // 2):
            cs = slice(128 * pp, 128 * (pp + 1))
            vb = v_ref[:, cs].astype(BF16)
            mixed = _pair_select(lane,
                                 jnp.dot(wt_ref[2 * pp], vb, preferred_element_type=F32),
                                 jnp.dot(wt_ref[2 * pp + 1], vb, preferred_element_type=F32)) + bb_ref[:, cs]
            o_ref[:, cs] = (u_ref[:, cs] * mixed).astype(o_ref.dtype)

    return pl.pallas_call(
        body, name=name, grid=(S // CHUNK,),
        in_specs=[pl.BlockSpec((CHUNK, A_WIDTH), lambda c: (c, OFF_AU // A_WIDTH)),
                  pl.BlockSpec((CHUNK, A_WIDTH), lambda c: (c, OFF_AV // A_WIDTH)),
                  pl.BlockSpec((A_HEADS, CHUNK, CHUNK), lambda c: (0, 0, 0)),
                  pl.BlockSpec((CHUNK, A_WIDTH), lambda c: (0, 0))],
        out_specs=pl.BlockSpec((CHUNK, A_WIDTH), lambda c: (c, 0)),
        out_shape=SDS((S, A_WIDTH), BF16),
        compiler_params=_cp(1),
    )(p, p, wt, bb)


def _sgu_bwd(name, p, dycat, wt, wtt, bb):
    S = p.shape[0]

    def body(u_ref, v_ref, dy_ref, wt_ref, wtt_ref, bb_ref, du_ref, dv_ref, dw_ref, db_ref, dbacc_ref):
        c = pl.program_id(0)

        @pl.when(c == 0)
        def _():
            dw_ref[...] = jnp.zeros_like(dw_ref)
            dbacc_ref[...] = jnp.zeros_like(dbacc_ref)

        lane = lax.broadcasted_iota(jnp.int32, (CHUNK, 128), 1)
        row = lax.broadcasted_iota(jnp.int32, (CHUNK, 128), 0)
        causal = row >= lane
        for pp in range(A_HEADS // 2):
            cs = slice(128 * pp, 128 * (pp + 1))
            v = v_ref[:, cs]
            vb = v.astype(BF16)
            u = u_ref[:, cs]
            dy = dy_ref[:, cs]
            mixed = _pair_select(lane,
                                 jnp.dot(wt_ref[2 * pp], vb, preferred_element_type=F32),
                                 jnp.dot(wt_ref[2 * pp + 1], vb, preferred_element_type=F32)) + bb_ref[:, cs]
            du_ref[:, cs] = (dy * mixed).astype(du_ref.dtype)
            dm = dy * u
            dmb = dm.astype(BF16)
            dv = _pair_select(lane,
                              jnp.dot(wtt_ref[2 * pp], dmb, preferred_element_type=F32),
                              jnp.dot(wtt_ref[2 * pp + 1], dmb, preferred_element_type=F32))
            dv_ref[:, cs] = dv.astype(dv_ref.dtype)
            dbacc_ref[:, cs] += dm
            nt = (((1,), (1,)), ((), ()))
            dm_lo = jnp.where(lane < HEAD_DIM, dm, 0.0).astype(BF16)
            dm_hi = jnp.where(lane >= HEAD_DIM, dm, 0.0).astype(BF16)
            dw_ref[2 * pp] += jnp.where(causal, lax.dot_general(dm_lo, vb, nt, preferred_element_type=F32), 0.0)
            dw_ref[2 * pp + 1] += jnp.where(causal, lax.dot_general(dm_hi, vb, nt, preferred_element_type=F32), 0.0)

        @pl.when(c == S // CHUNK - 1)
        def _():
            out = jnp.zeros((CHUNK, 128), F32)
            for pp in range(A_HEADS // 2):
                acc = dbacc_ref[:, 128 * pp:128 * (pp + 1)]
                s_lo = jnp.sum(jnp.where(lane < HEAD_DIM, acc, 0.0), axis=1, keepdims=True)
                s_hi = jnp.sum(jnp.where(lane >= HEAD_DIM, acc, 0.0), axis=1, keepdims=True)
                out = jnp.where(lane == 2 * pp, s_lo, out)
                out = jnp.where(lane == 2 * pp + 1, s_hi, out)
            db_ref[...] = out

    chunk = lambda col: pl.BlockSpec((CHUNK, A_WIDTH), lambda c: (c, col))
    wspec = pl.BlockSpec((A_HEADS, CHUNK, CHUNK), lambda c: (0, 0, 0))
    return pl.pallas_call(
        body, name=name, grid=(S // CHUNK,),
        in_specs=[chunk(OFF_AU // A_WIDTH), chunk(OFF_AV // A_WIDTH), chunk(0), wspec, wspec,
                  pl.BlockSpec((CHUNK, A_WIDTH), lambda c: (0, 0))],
        out_specs=[chunk(0), chunk(0), wspec, pl.BlockSpec((CHUNK, 128), lambda c: (0, 0))],
        out_shape=[SDS((S, A_WIDTH), BF16), SDS((S, A_WIDTH), BF16),
                   SDS((A_HEADS, CHUNK, CHUNK), F32), SDS((CHUNK, 128), F32)],
        scratch_shapes=[pltpu.VMEM((CHUNK, A_WIDTH), F32)],
        compiler_params=_cp(1),
    )(p, p, dycat, wt, wtt, bb)


CONV_HALO = 8


def _shift_down(a, halo, k):
    T = a.shape[0]
    row = lax.broadcasted_iota(jnp.int32, a.shape, 0)
    out = pltpu.roll(a, k, 0)
    for r in range(k):
        out = jnp.where(row == r, halo[CONV_HALO - k + r:CONV_HALO - k + r + 1, :], out)
    return out


def _shift_up(a, halo, k):
    T = a.shape[0]
    row = lax.broadcasted_iota(jnp.int32, a.shape, 0)
    out = pltpu.roll(a, T - k, 0)
    for r in range(k):
        out = jnp.where(row == T - k + r, halo[r:r + 1, :], out)
    return out


def _conv_specs(S, T):
    hb = T // CONV_HALO
    last = S // CONV_HALO - 1
    tile = lambda col0: pl.BlockSpec((T, 128), lambda j, i: (i, col0 + j))
    prev = lambda col0: pl.BlockSpec((CONV_HALO, 128), lambda j, i: (jnp.maximum(i * hb - 1, 0), col0 + j))
    nxt = lambda col0: pl.BlockSpec((CONV_HALO, 128), lambda j, i: (jnp.minimum((i + 1) * hb, last), col0 + j))
    return tile, prev, nxt


def _conv_fwd(name, p, w):
    S = p.shape[0]
    T = min(512, S)
    tile, prev, _ = _conv_specs(S, T)
    cb, cc, cx = OFF_BB // 128, OFF_BC // 128, OFF_BX // 128

    def body(b_ref, c_ref, x_ref, ch_ref, xh_ref, w_ref, o_ref):
        i = pl.program_id(1)
        z = c_ref[...] * x_ref[...]
        zh = jnp.where(i > 0, ch_ref[...] * xh_ref[...], 0.0)
        z1 = _shift_down(z, zh, 1)
        z2 = _shift_down(z, zh, 2)
        conv = w_ref[0:1, :] * z2 + w_ref[1:2, :] * z1 + w_ref[2:3, :] * z
        o_ref[...] = (b_ref[...] * conv).astype(o_ref.dtype)

    return pl.pallas_call(
        body, name=name, grid=(B_WIDTH // 128, S // T),
        in_specs=[tile(cb), tile(cc), tile(cx), prev(cc), prev(cx),
                  pl.BlockSpec((3, 128), lambda j, i: (0, j))],
        out_specs=tile(0),
        out_shape=SDS((S, B_WIDTH), BF16),
        compiler_params=_cp(2),
    )(p, p, p, p, p, w)


def _conv_bwd(name, p, dycat, w):
    S = p.shape[0]
    T = min(512, S)
    tile, prev, nxt = _conv_specs(S, T)
    cb, cc, cx = OFF_BB // 128, OFF_BC // 128, OFF_BX // 128
    cdy = A_WIDTH // 128
    n_i = S // T

    def body(b_ref, c_ref, x_ref, dy_ref, ch_ref, xh_ref, bn_ref, dyn_ref, w_ref,
             db_ref, dc_ref, dx_ref, dw_ref):
        i = pl.program_id(1)

        @pl.when(i == 0)
        def _():
            dw_ref[...] = jnp.zeros_like(dw_ref)

        cv = c_ref[...]
        xv = x_ref[...]
        z = cv * xv
        zh = jnp.where(i > 0, ch_ref[...] * xh_ref[...], 0.0)
        z1 = _shift_down(z, zh, 1)
        z2 = _shift_down(z, zh, 2)
        w0, w1, w2 = w_ref[0:1, :], w_ref[1:2, :], w_ref[2:3, :]
        conv = w0 * z2 + w1 * z1 + w2 * z
        dy = dy_ref[...]
        db_ref[...] = (dy * conv).astype(db_ref.dtype)
        dconv = dy * b_ref[...]
        dconv_n = jnp.where(i < n_i - 1, dyn_ref[...] * bn_ref[...], 0.0)
        dz = w2 * dconv + w1 * _shift_up(dconv, dconv_n, 1) + w0 * _shift_up(dconv, dconv_n, 2)
        dc_ref[...] = (dz * xv).astype(dc_ref.dtype)
        dx_ref[...] = (dz * cv).astype(dx_ref.dtype)
        dw_ref[0:1, :] += jnp.sum(dconv * z2, axis=0, keepdims=True)
        dw_ref[1:2, :] += jnp.sum(dconv * z1, axis=0, keepdims=True)
        dw_ref[2:3, :] += jnp.sum(dconv * z, axis=0, keepdims=True)

    wspec = pl.BlockSpec((3, 128), lambda j, i: (0, j))
    return pl.pallas_call(
        body, name=name, grid=(B_WIDTH // 128, n_i),
        in_specs=[tile(cb), tile(cc), tile(cx), tile(cdy), prev(cc), prev(cx), nxt(cb), nxt(cdy), wspec],
        out_specs=[tile(0), tile(0), tile(0), wspec],
        out_shape=[SDS((S, B_WIDTH), BF16)] * 3 + [SDS((3, B_WIDTH), F32)],
        compiler_params=_cp(2),
    )(p, p, p, dycat, p, p, p, dycat, w)


def _seg_sum(t, bd):
    hi = t.astype(BF16)
    lo = (t - hi.astype(F32)).astype(BF16)
    return jnp.dot(hi, bd, preferred_element_type=F32) + jnp.dot(lo, bd, preferred_element_type=F32)


def _head_norm(x, g, bd):
    rstd = lax.rsqrt(_seg_sum(x * x, bd) * (1.0 / HEAD_DIM) + EPS)
    xhat = x * rstd
    return xhat * g, xhat, rstd


def _head_norm_bwd(dy, g, xhat, rstd, bd):
    dxh = dy * g
    return rstd * (dxh - xhat * (_seg_sum(dxh * xhat, bd) * (1.0 / HEAD_DIM)))


def _band_mask(has_prev):
    row = lax.broadcasted_iota(jnp.int32, (BLK, 2 * BLK), 0)
    col = lax.broadcasted_iota(jnp.int32, (BLK, 2 * BLK), 1)
    first_key = jnp.where(has_prev, 0, BLK)
    return (col >= row) & (col <= row + BLK) & (col >= first_key)


def _first_of_segment(g, n, n_blocks):
    per_seg = lax.shift_right_logical(jnp.int32(n_blocks), 2 * g)
    return (n & (per_seg - 1)) == 0


def _residue_rows(r, d):
    return slice(None) if d == 1 else pl.ds(r, BLK, stride=d)


HW = 128


def _for_residues(d, fn):
    if d == 1:
        fn(0)
    else:
        lax.fori_loop(0, d, lambda r, carry: (fn(r), carry)[1], 0)


def _attn_fwd(name, p, g, gq, gk, bd):
    S = p.shape[0]
    d = PATTERN_DILATION[g]
    rows = BLK * d
    nt = (((1,), (1,)), ((), ()))

    def body(q_ref, kc_ref, kp_ref, vc_ref, vp_ref, gq_ref, gk_ref, bd_ref, o_ref, lse_ref):
        has_prev = pl.program_id(1) > 0
        bdv = bd_ref[...]
        band = _band_mask(has_prev)
        lane = lax.broadcasted_iota(jnp.int32, (1, HW), 1)

        def residue(r):
            rr = _residue_rows(r, d)
            qn, _, _ = _head_norm(q_ref[rr, :], gq_ref[...], bdv)
            kn, _, _ = _head_norm(jnp.concatenate([kp_ref[rr, :], kc_ref[rr, :]], axis=0), gk_ref[...], bdv)
            knb = kn.astype(BF16)
            vb = jnp.concatenate([vp_ref[rr, :], vc_ref[rr, :]], axis=0).astype(BF16)
            o_acc = jnp.zeros((BLK, HW), F32)
            l_acc = jnp.zeros((BLK, HW), F32)
            for j in range(HW // HEAD_DIM):
                hm = (lane >= HEAD_DIM * j) & (lane < HEAD_DIM * (j + 1))
                qj = jnp.where(hm, qn, 0.0).astype(BF16)
                s = lax.dot_general(qj, knb, nt, preferred_element_type=F32) * (HEAD_DIM ** -0.5)
                s = jnp.where(band, s, NEG)
                m = jnp.max(s, axis=1, keepdims=True)
                e = jnp.exp(s - m)
                den = jnp.sum(e, axis=1, keepdims=True)
                pv = jnp.dot(e.astype(BF16), vb, preferred_element_type=F32)
                o_acc = jnp.where(hm, pv / den, o_acc)
                l_acc = jnp.where(hm, m + jnp.log(den), l_acc)
            o_ref[rr, :] = o_acc
            lse_ref[rr, :] = l_acc

        _for_residues(d, residue)

    per = PW // HW
    cq, ck, cv = (OFF_Q + PW * g) // HW, (OFF_K + PW * g) // HW, (OFF_V + PW * g) // HW
    cur = lambda col: pl.BlockSpec((rows, HW), lambda h, n: (n, col + h))
    prv = lambda col: pl.BlockSpec((rows, HW), lambda h, n: (jnp.maximum(n - 1, 0), col + h))
    vec = pl.BlockSpec((1, HW), lambda h, n: (0, h))
    return pl.pallas_call(
        body, name=name, grid=(per, S // rows),
        in_specs=[cur(cq), cur(ck), prv(ck), cur(cv), prv(cv), vec, vec, pl.BlockSpec((HW, HW), lambda h, n: (0, 0))],
        out_specs=[cur(0), cur(0)],
        out_shape=[SDS((S, PW), F32)] * 2,
        compiler_params=_cp(2),
    )(p, p, p, p, p, gq, gk, bd)


def _attn_bwd(name, p, g, lse, do3, c3, gq, gk, bd):
    S = p.shape[0]
    d = PATTERN_DILATION[g]
    rows = BLK * d
    nblk = S // rows
    nt = (((1,), (1,)), ((), ()))
    tn = (((0,), (0,)), ((), ()))

    def body(q_ref, kc_ref, kp_ref, vc_ref, vp_ref, lse_ref, do_ref, c_ref, gq_ref, gk_ref, bd_ref,
             dq_ref, dk_ref, dv_ref, dgq_ref, dgk_ref, ck_ref, cv_ref):
        n = pl.program_id(1)
        keep = jnp.where(n < nblk, 1.0, 0.0)
        has_prev = jnp.minimum(n, nblk - 1) > 0

        @pl.when(n == 0)
        def _():
            ck_ref[...] = jnp.zeros_like(ck_ref)
            cv_ref[...] = jnp.zeros_like(cv_ref)
            dgq_ref[...] = jnp.zeros_like(dgq_ref)
            dgk_ref[...] = jnp.zeros_like(dgk_ref)

        bdv = bd_ref[...]
        gqv = gq_ref[...]
        gkv = gk_ref[...]
        band = _band_mask(has_prev)
        lane = lax.broadcasted_iota(jnp.int32, (1, HW), 1)

        def residue(r):
            rr = _residue_rows(r, d)
            qn, qhat, qrstd = _head_norm(q_ref[rr, :], gqv, bdv)
            kn, khat, krstd = _head_norm(jnp.concatenate([kp_ref[rr, :], kc_ref[rr, :]], axis=0), gkv, bdv)
            knb = kn.astype(BF16)
            vb = jnp.concatenate([vp_ref[rr, :], vc_ref[rr, :]], axis=0).astype(BF16)
            lse_v = lse_ref[rr, :]
            do = do_ref[rr, :]
            cc = c_ref[rr, :]
            dqn = jnp.zeros((BLK, HW), F32)
            dkn = jnp.zeros((2 * BLK, HW), F32)
            dvv = jnp.zeros((2 * BLK, HW), F32)
            for j in range(HW // HEAD_DIM):
                hm = (lane >= HEAD_DIM * j) & (lane < HEAD_DIM * (j + 1))
                qj = jnp.where(hm, qn, 0.0).astype(BF16)
                doj = jnp.where(hm, do, 0.0).astype(BF16)
                s = lax.dot_general(qj, knb, nt, preferred_element_type=F32) * (HEAD_DIM ** -0.5)
                lse_j = jnp.max(jnp.where(hm, lse_v, NEG), axis=1, keepdims=True)
                c_j = jnp.max(jnp.where(hm, cc, NEG), axis=1, keepdims=True)
                prob = jnp.where(band, jnp.exp(s - lse_j), 0.0)
                dp = lax.dot_general(doj, vb, nt, preferred_element_type=F32)
                ds = (prob * (dp + c_j) * (HEAD_DIM ** -0.5)).astype(BF16)
                dqn = jnp.where(hm, jnp.dot(ds, knb, preferred_element_type=F32), dqn)
                dkn += lax.dot_general(ds, qj, tn, preferred_element_type=F32)
                dvv += lax.dot_general(prob.astype(BF16), doj, tn, preferred_element_type=F32)

            dq_ref[rr, :] = _head_norm_bwd(dqn, gqv, qhat, qrstd, bdv)
            dk2 = _head_norm_bwd(dkn, gkv, khat, krstd, bdv)
            dgq_ref[...] += keep * jnp.sum(dqn * qhat, axis=0, keepdims=True)
            dgk_ref[...] += keep * jnp.sum(dkn * khat, axis=0, keepdims=True)
            dk_ref[rr, :] = ck_ref[rr, :] + keep * dk2[:BLK]
            dv_ref[rr, :] = cv_ref[rr, :] + keep * dvv[:BLK]
            ck_ref[rr, :] = dk2[BLK:]
            cv_ref[rr, :] = dvv[BLK:]

        _for_residues(d, residue)

    last = nblk - 1
    per = PW // HW
    cq, ck, cv = (OFF_Q + PW * g) // HW, (OFF_K + PW * g) // HW, (OFF_V + PW * g) // HW
    cur = lambda col: pl.BlockSpec((rows, HW), lambda h, n: (jnp.minimum(n, last), col + h))
    prv = lambda col: pl.BlockSpec((rows, HW), lambda h, n: (jnp.maximum(jnp.minimum(n, last) - 1, 0), col + h))
    cur3 = pl.BlockSpec((None, rows, HW), lambda h, n: (g, jnp.minimum(n, last), h))
    done = pl.BlockSpec((rows, HW), lambda h, n: (jnp.maximum(n - 1, 0), h))
    vec = pl.BlockSpec((1, HW), lambda h, n: (0, h))
    return pl.pallas_call(
        body, name=name, grid=(per, nblk + 1),
        in_specs=[cur(cq), cur(ck), prv(ck), cur(cv), prv(cv), cur(0), cur3, cur3, vec, vec,
                  pl.BlockSpec((HW, HW), lambda h, n: (0, 0))],
        out_specs=[cur(0), done, done, vec, vec],
        out_shape=[SDS((S, PW), F32)] * 3 + [SDS((1, PW), F32)] * 2,
        scratch_shapes=[pltpu.VMEM((rows, HW), F32), pltpu.VMEM((rows, HW), F32)],
        compiler_params=_cp(2),
    )(p, p, p, p, p, lse, do3, c3, gq, gk, bd)


def _mix_fwd(name, os, lses):
    S = os[0].shape[0]
    tm = min(512, S)

    def body(o0, o1, o2, l0, l1, l2, y_ref):
        o = [o0[...], o1[...], o2[...]]
        l = [l0[...], l1[...], l2[...]]
        m = jnp.maximum(jnp.maximum(l[0], l[1]), l[2])
        e = [jnp.exp(t - m) for t in l]
        inv = 1.0 / (e[0] + e[1] + e[2])
        for g in range(N_PATTERNS):
            y_ref[:, PW * g:PW * (g + 1)] = (o[g] * (e[g] * inv)).astype(y_ref.dtype)

    blk = pl.BlockSpec((tm, PW), lambda i: (i, 0))
    return pl.pallas_call(
        body, name=name, grid=(S // tm,),
        in_specs=[blk] * 6,
        out_specs=pl.BlockSpec((tm, C_WIDTH), lambda i: (i, 0)),
        out_shape=SDS((S, C_WIDTH), BF16),
        compiler_params=_cp(1),
    )(*os, *lses)


def _mix_bwd(name, os, lses, dycat, bd):
    S = os[0].shape[0]
    tm = min(512, S)
    c0 = (A_WIDTH + B_WIDTH) // PW

    def body(o0, o1, o2, l0, l1, l2, dy0_ref, dy1_ref, dy2_ref, bd_ref, do_ref, c_ref):
        bdv = bd_ref[...]
        o = [o0[...], o1[...], o2[...]]
        l = [l0[...], l1[...], l2[...]]
        dys = [dy0_ref[...], dy1_ref[...], dy2_ref[...]]
        m = jnp.maximum(jnp.maximum(l[0], l[1]), l[2])
        e = [jnp.exp(t - m) for t in l]
        inv = 1.0 / (e[0] + e[1] + e[2])
        alpha = [t * inv for t in e]
        da = [_seg_sum(dys[g] * o[g], bdv) for g in range(N_PATTERNS)]
        mean_da = alpha[0] * da[0] + alpha[1] * da[1] + alpha[2] * da[2]
        for g in range(N_PATTERNS):
            do_ref[g] = dys[g] * alpha[g]
            c_ref[g] = -alpha[g] * mean_da

    blk = pl.BlockSpec((tm, PW), lambda i: (i, 0))
    blk3 = pl.BlockSpec((N_PATTERNS, tm, PW), lambda i: (0, i, 0))
    dyspec = lambda g: pl.BlockSpec((tm, PW), lambda i: (i, c0 + g))
    return pl.pallas_call(
        body, name=name, grid=(S // tm,),
        in_specs=[blk] * 6 + [dyspec(0), dyspec(1), dyspec(2), pl.BlockSpec((PW, PW), lambda i: (0, 0))],
        out_specs=[blk3, blk3],
        out_shape=[SDS((N_PATTERNS, S, PW), F32)] * 2,
        compiler_params=_cp(1),
    )(*os, *lses, dycat, dycat, dycat, bd)


def _mesh_pos():
    x, y, c = lax.axis_index("x"), lax.axis_index("y"), lax.axis_index("c")
    chips = [(1 - x, y), (x, 1 - y), (1 - x, 1 - y)]
    chip_idx = [2 * cx + cy for cx, cy in chips]
    return x, y, c, 2 * x + y, chips, chip_idx


def _place_shard(name, w, layer, chip_arr, out_dtype, deps=()):
    _, R, C = w.shape
    tr = min(256, R)

    def body(chip_ref, w_ref, *rest):
        o_ref = rest[-1]
        o_ref[...] = w_ref[...].astype(o_ref.dtype)

    return pl.pallas_call(
        body, name=name,
        grid_spec=pltpu.PrefetchScalarGridSpec(
            num_scalar_prefetch=1, grid=(R // tr,),
            in_specs=[pl.BlockSpec((None, tr, C), lambda i, chip_ref: (layer, i, 0))] + [_hbm_spec()] * len(deps),
            out_specs=pl.BlockSpec((None, tr, C), lambda i, chip_ref: (chip_ref[0], i, 0))),
        out_shape=SDS((N_CHIPS, R, C), out_dtype),
        compiler_params=_cp(1),
    )(chip_arr, w, *deps)


HBM_SPEC = pl.BlockSpec(memory_space=pltpu.HBM)
SEM_SPEC = pl.BlockSpec(memory_space=pltpu.SEMAPHORE)
SPLIT_COPY = pltpu.SideEffectType.DATAFLOW_SIDE_EFFECTING
N_PEER_CHIPS = N_CHIPS - 1
TOKEN_SHAPE = SDS((8, 128), F32)
TOKEN_SPEC = pl.BlockSpec(memory_space=pltpu.VMEM)


def _in_hbm(a):
    return pltpu.with_memory_space_constraint(a, pltpu.HBM)


def _gather_start(name, bufs):
    T = len(bufs)

    def body(*refs):
        ins = refs[:T]
        send_sems, recv_sems = refs[T:2 * T], refs[2 * T:3 * T]
        token = refs[4 * T]
        x, y, c, me, chips, chip_idx = _mesh_pos()
        for t in range(T):
            hr = ins[t].shape[1] // 2
            mine = ins[t].at[me, pl.ds(c * hr, hr), :]
            for j in range(N_PEER_CHIPS):
                pltpu.make_async_remote_copy(src_ref=mine, dst_ref=mine, send_sem=send_sems[t].at[j],
                                             recv_sem=recv_sems[t].at[j], device_id=(*chips[j], c),
                                             device_id_type=MESH).start()
        token[...] = jnp.zeros_like(token)

    sems = [pltpu.SemaphoreType.DMA((N_PEER_CHIPS,))] * T
    out = pl.pallas_call(
        body, name=name,
        in_specs=[HBM_SPEC] * T,
        out_specs=[SEM_SPEC] * (2 * T) + [HBM_SPEC] * T + [TOKEN_SPEC],
        out_shape=sems + sems + [pltpu.HBM(b.shape, b.dtype) for b in bufs] + [TOKEN_SHAPE],
        input_output_aliases={t: 2 * T + t for t in range(T)},
        compiler_params=pltpu.CompilerParams(has_side_effects=SPLIT_COPY),
    )(*[_in_hbm(b) for b in bufs])
    return out[:T], out[T:2 * T], out[2 * T:3 * T], out[3 * T]


def _gather_wait(name, buf, send_sem, recv_sem, after):
    n_in = 3 if after is None else 4

    def body(*refs):
        buf_ref, ssem, rsem = refs[:3]
        x, y, c, me, chips, chip_idx = _mesh_pos()
        hr = buf_ref.shape[1] // 2
        mine = buf_ref.at[me, pl.ds(c * hr, hr), :]
        for j in range(N_PEER_CHIPS):
            got = buf_ref.at[chip_idx[j], pl.ds(c * hr, hr), :]
            cp = pltpu.make_async_remote_copy(src_ref=mine, dst_ref=got, send_sem=ssem.at[j], recv_sem=rsem.at[j],
                                              device_id=(*chips[j], c), device_id_type=MESH)
            cp.wait_send()
            cp.wait_recv()

    args = [buf, send_sem, recv_sem] + ([] if after is None else [after])
    return pl.pallas_call(
        body, name=name,
        in_specs=[HBM_SPEC, SEM_SPEC, SEM_SPEC] + [_hbm_spec()] * (n_in - 3),
        out_specs=HBM_SPEC,
        out_shape=pltpu.HBM(buf.shape, buf.dtype),
        input_output_aliases={0: 0},
        compiler_params=pltpu.CompilerParams(has_side_effects=SPLIT_COPY),
    )(*args)


def _forward_start(name, buf):
    def body(buf_ref, send_sems, recv_sems, buf_thru, token):
        x, y, c, me, chips, chip_idx = _mesh_pos()
        hr = buf_ref.shape[1] // 2
        for j in range(N_PEER_CHIPS):
            got = buf_ref.at[chip_idx[j], pl.ds(c * hr, hr), :]
            pltpu.make_async_remote_copy(src_ref=got, dst_ref=got, send_sem=send_sems.at[j], recv_sem=recv_sems.at[j],
                                         device_id=(x, y, 1 - c), device_id_type=MESH).start()
        token[...] = jnp.zeros_like(token)

    sems = pltpu.SemaphoreType.DMA((N_PEER_CHIPS,))
    return pl.pallas_call(
        body, name=name,
        in_specs=[HBM_SPEC],
        out_specs=[SEM_SPEC, SEM_SPEC, HBM_SPEC, TOKEN_SPEC],
        out_shape=[sems, sems, pltpu.HBM(buf.shape, buf.dtype), TOKEN_SHAPE],
        input_output_aliases={0: 2},
        compiler_params=pltpu.CompilerParams(has_side_effects=SPLIT_COPY),
    )(_in_hbm(buf))


def _forward_wait(name, buf, send_sems, recv_sems, after):
    n_in = 3 if after is None else 4

    def body(*refs):
        buf_ref, ssems, rsems = refs[:3]
        x, y, c, me, chips, chip_idx = _mesh_pos()
        hr = buf_ref.shape[1] // 2
        for j in range(N_PEER_CHIPS):
            sent = buf_ref.at[chip_idx[j], pl.ds(c * hr, hr), :]
            theirs = buf_ref.at[chip_idx[j], pl.ds((1 - c) * hr, hr), :]
            cp = pltpu.make_async_remote_copy(src_ref=sent, dst_ref=theirs, send_sem=ssems.at[j],
                                              recv_sem=rsems.at[j], device_id=(x, y, 1 - c), device_id_type=MESH)
            cp.wait_send()
            cp.wait_recv()

    args = [buf, send_sems, recv_sems] + ([] if after is None else [after])
    return pl.pallas_call(
        body, name=name,
        in_specs=[HBM_SPEC, SEM_SPEC, SEM_SPEC] + [_hbm_spec()] * (n_in - 3),
        out_specs=HBM_SPEC,
        out_shape=pltpu.HBM(buf.shape, buf.dtype),
        input_output_aliases={0: 0},
        compiler_params=pltpu.CompilerParams(has_side_effects=SPLIT_COPY),
    )(*args)


class _GatheredWeights:
    def __init__(self):
        self._order = []
        self._pending = {}
        self._forwarding = {}
        self._ready = {}
        self._tokens = []

    def start(self, keys, bufs):
        send_sems, recv_sems, thru, token = _gather_start(f"gather_start_{len(self._order)}", bufs)
        self._tokens.append(token)
        self._order.extend(keys)
        self._pending.update({k: (b, s, r) for k, b, s, r in zip(keys, thru, send_sems, recv_sems)})

    def _prefetch(self, key, after):
        if key in self._pending:
            buf, ssem, rsem = self._pending.pop(key)
            tag = f"{key[0]}_{key[1]}"
            buf = _gather_wait(f"gather_wait_{tag}", buf, ssem, rsem, after)
            ssems, rsems, buf, token = _forward_start(f"gather_fwd_start_{tag}", buf)
            self._forwarding[key] = (buf, ssems, rsems)
            self._tokens.append(token)

    def get(self, name, layer, after=None):
        key = (name, layer)
        if key not in self._ready:
            self._prefetch(key, after)
            buf, ssems, rsems = self._forwarding.pop(key)
            self._ready[key] = _forward_wait(f"gather_fwd_wait_{name}_{layer}", buf, ssems, rsems, after)
            nxt = self._order.index(key) + 1
            if nxt < len(self._order):
                self._prefetch(self._order[nxt], after)
        return self._ready[key]

    def deps(self):
        tokens, self._tokens = self._tokens, []
        return tokens


def _swap_copy(g_ref, land_ref, send_sem, recv_sem):
    x, y, c, _, _, _ = _mesh_pos()
    hr = g_ref.shape[1] // 2
    return pltpu.make_async_remote_copy(src_ref=g_ref.at[:, pl.ds((1 - c) * hr, hr), :], dst_ref=land_ref,
                                        send_sem=send_sem, recv_sem=recv_sem, device_id=(x, y, 1 - c),
                                        device_id_type=MESH)


def _swap_start(name, g):
    land_shape = (g.shape[0], g.shape[1] // 2, g.shape[2])

    def body(g_ref, land_ref, send_sem, recv_sem, land_thru, token):
        _swap_copy(g_ref, land_ref, send_sem, recv_sem).start()
        token[...] = jnp.zeros_like(token)

    return pl.pallas_call(
        body, name=name,
        in_specs=[HBM_SPEC, HBM_SPEC],
        out_specs=[SEM_SPEC, SEM_SPEC, HBM_SPEC, TOKEN_SPEC],
        out_shape=[pltpu.SemaphoreType.DMA(()), pltpu.SemaphoreType.DMA(()), pltpu.HBM(land_shape, g.dtype),
                   TOKEN_SHAPE],
        input_output_aliases={1: 2},
        compiler_params=pltpu.CompilerParams(has_side_effects=SPLIT_COPY),
    )(_in_hbm(g), _in_hbm(lax.empty(land_shape, g.dtype)))


def _swap_wait(name, g, land, send_sem, recv_sem, after):
    def body(g_ref, land_ref, send_sem, recv_sem, after_ref, land_out):
        cp = _swap_copy(g_ref, land_ref, send_sem, recv_sem)
        cp.wait_send()
        cp.wait_recv()

    return pl.pallas_call(
        body, name=name,
        in_specs=[HBM_SPEC, HBM_SPEC, SEM_SPEC, SEM_SPEC, _hbm_spec()],
        out_specs=HBM_SPEC,
        out_shape=pltpu.HBM(land.shape, land.dtype),
        input_output_aliases={1: 0},
        compiler_params=pltpu.CompilerParams(has_side_effects=SPLIT_COPY),
    )(_in_hbm(g), land, send_sem, recv_sem, after)


def _add_my_half(name, g, r, c_arr):
    ns, R, C = g.shape
    hr = R // 2
    tr = min(256, hr)
    nt = hr // tr

    def body(c_ref, g_ref, r_ref, o_ref, land_ref):
        t = (g_ref[...] + r_ref[...]).astype(o_ref.dtype)
        o_ref[...] = t
        land_ref[...] = t

    out = pl.BlockSpec((None, tr, C), lambda s, i, c_ref: (s, i, 0))
    return pl.pallas_call(
        body, name=name,
        grid_spec=pltpu.PrefetchScalarGridSpec(
            num_scalar_prefetch=1, grid=(ns, nt),
            in_specs=[pl.BlockSpec((None, tr, C), lambda s, i, c_ref: (s, c_ref[0] * nt + i, 0)), out],
            out_specs=[out, out]),
        out_shape=[SDS((ns, hr, C), BF16)] * 2,
        compiler_params=_cp(2),
    )(c_arr, g, r)


def _exchange_start(name, part, land):
    def body(part_ref, land_ref, send_sems, recv_sems, land_thru, token):
        x, y, c, me, chips, chip_idx = _mesh_pos()
        for j in range(N_PEER_CHIPS):
            pltpu.make_async_remote_copy(src_ref=part_ref.at[chip_idx[j]], dst_ref=land_ref.at[me],
                                         send_sem=send_sems.at[j], recv_sem=recv_sems.at[j],
                                         device_id=(*chips[j], c), device_id_type=MESH).start()
        token[...] = jnp.zeros_like(token)

    sems = pltpu.SemaphoreType.DMA((N_PEER_CHIPS,))
    return pl.pallas_call(
        body, name=name,
        in_specs=[HBM_SPEC, HBM_SPEC],
        out_specs=[SEM_SPEC, SEM_SPEC, HBM_SPEC, TOKEN_SPEC],
        out_shape=[sems, sems, pltpu.HBM(land.shape, land.dtype), TOKEN_SHAPE],
        input_output_aliases={1: 2},
        compiler_params=pltpu.CompilerParams(has_side_effects=SPLIT_COPY),
    )(_in_hbm(part), _in_hbm(land))


def _exchange_wait(name, part, land, send_sems, recv_sems, after):
    def body(part_ref, land_ref, send_sems, recv_sems, after_ref, land_out):
        x, y, c, me, chips, chip_idx = _mesh_pos()
        for j in range(N_PEER_CHIPS):
            cp = pltpu.make_async_remote_copy(src_ref=part_ref.at[chip_idx[j]], dst_ref=land_ref.at[chip_idx[j]],
                                              send_sem=send_sems.at[j], recv_sem=recv_sems.at[j],
                                              device_id=(*chips[j], c), device_id_type=MESH)
            cp.wait_send()
            cp.wait_recv()

    return pl.pallas_call(
        body, name=name,
        in_specs=[HBM_SPEC, HBM_SPEC, SEM_SPEC, SEM_SPEC, _hbm_spec()],
        out_specs=HBM_SPEC,
        out_shape=pltpu.HBM(land.shape, land.dtype),
        input_output_aliases={1: 0},
        compiler_params=pltpu.CompilerParams(has_side_effects=SPLIT_COPY),
    )(_in_hbm(part), land, send_sems, recv_sems, after)


class _GradReducer:
    def __init__(self, c_arr):
        self._c_arr = c_arr
        self._swapping = []
        self._exchanging = {}
        self._tokens = []

    def begin(self, name, layer, g):
        tag = f"{name}_{layer}"
        ssem, rsem, land, token = _swap_start(f"rs_swap_start_{tag}", g)
        self._swapping.append((name, layer, g, ssem, rsem, land))
        self._tokens.append(token)

    def advance(self, after):
        for name, layer, g, ssem, rsem, land in self._swapping:
            tag = f"{name}_{layer}"
            theirs = _swap_wait(f"rs_swap_wait_{tag}", g, land, ssem, rsem, after)
            part, own = _add_my_half(f"rs_add_{tag}", g, theirs, self._c_arr)
            ssems, rsems, land2, token = _exchange_start(f"rs_xchg_start_{tag}", part, own)
            self._exchanging[(name, layer)] = (part, ssems, rsems, land2)
            self._tokens.append(token)
        self._swapping = []

    def deps(self):
        tokens, self._tokens = self._tokens, []
        return tokens

    def finish(self, names, n_layers, after):
        bufs = []
        for name in names:
            buf = None
            for layer in range(n_layers):
                part, ssems, rsems, land = self._exchanging.pop((name, layer))
                tag = f"{name}_{layer}"
                landed = _exchange_wait(f"rs_xchg_wait_{tag}", part, land, ssems, rsems, after)
                buf = _sum_chips(f"rs_sum_{tag}", landed, self._c_arr, layer, n_layers, buf)
            bufs.append(buf)
        return dict(zip(names, _join_halves(f"rs_join_{names[0]}", bufs)))


def _sum_chips(name, r, c_arr, layer, n_layers, prev):
    ns, H, C = r.shape
    tr = min(256, H)
    nt = H // tr

    def body(c_ref, r_ref, *rest):
        o_ref = rest[-1]
        o_ref[...] = ((r_ref[0].astype(F32) + r_ref[1].astype(F32)) + r_ref[2].astype(F32)) + r_ref[3].astype(F32)

    in_specs = [pl.BlockSpec((ns, tr, C), lambda i, c_ref: (0, i, 0))]
    args = [c_arr, r]
    aliases = {}
    if prev is not None:
        in_specs.append(_hbm_spec())
        args.append(prev)
        aliases = {2: 0}
    return pl.pallas_call(
        body, name=name,
        grid_spec=pltpu.PrefetchScalarGridSpec(
            num_scalar_prefetch=1, grid=(nt,), in_specs=in_specs,
            out_specs=pl.BlockSpec((None, tr, C), lambda i, c_ref: (layer, c_ref[0] * nt + i, 0))),
        out_shape=SDS((n_layers, 2 * H, C), F32),
        input_output_aliases=aliases,
        compiler_params=_cp(1),
    )(*args)


def _join_halves(name, bufs):
    T = len(bufs)

    def body(*refs):
        outs = refs[T:2 * T]
        send_sems, recv_sems = refs[2 * T:]
        x, y, c, _, _, _ = _mesh_pos()
        cps = []
        for t in range(T):
            hr = outs[t].shape[1] // 2
            mine = outs[t].at[:, pl.ds(c * hr, hr), :]
            cp = pltpu.make_async_remote_copy(src_ref=mine, dst_ref=mine, send_sem=send_sems.at[t],
                                              recv_sem=recv_sems.at[t], device_id=(x, y, 1 - c), device_id_type=MESH)
            cp.start()
            cps.append(cp)
        for t in range(T):
            hr = outs[t].shape[1] // 2
            theirs = outs[t].at[:, pl.ds((1 - c) * hr, hr), :]
            pltpu.make_async_remote_copy(src_ref=theirs, dst_ref=theirs, send_sem=send_sems.at[t],
                                         recv_sem=recv_sems.at[t], device_id=(x, y, 1 - c),
                                         device_id_type=MESH).wait_recv()
        for cp in cps:
            cp.wait_send()

    return pl.pallas_call(
        body, name=name,
        in_specs=[_hbm_spec()] * T, out_specs=[_hbm_spec()] * T,
        out_shape=[SDS(b.shape, b.dtype) for b in bufs],
        input_output_aliases={t: t for t in range(T)},
        scratch_shapes=[pltpu.SemaphoreType.DMA((T,)), pltpu.SemaphoreType.DMA((T,))],
    )(*bufs)


def _small_copy(k, buf_ref, land_ref, send_sems, recv_sems):
    x, y, c = lax.axis_index("x"), lax.axis_index("y"), lax.axis_index("c")
    me = 4 * x + 2 * y + c
    peer = (x ^ ((k >> 2) & 1), y ^ ((k >> 1) & 1), c ^ (k & 1))
    cp = pltpu.make_async_remote_copy(src_ref=buf_ref, dst_ref=land_ref.at[me], send_sem=send_sems.at[k - 1],
                                      recv_sem=recv_sems.at[k - 1], device_id=peer, device_id_type=MESH)
    return me, peer, cp


def _small_start(buf, deps):
    land = jnp.broadcast_to(buf[None], (N_DEV,) + buf.shape)
    n_dep = len(deps)

    def body(buf_ref, land_ref, *rest):
        send_sems, recv_sems, _, token = rest[n_dep:]
        for k in range(1, N_DEV):
            _small_copy(k, buf_ref, land_ref, send_sems, recv_sems)[2].start()
        token[...] = jnp.zeros_like(token)

    sems = pltpu.SemaphoreType.DMA((N_DEV - 1,))
    return pl.pallas_call(
        body, name="small_gather_start",
        in_specs=[HBM_SPEC, HBM_SPEC] + [_hbm_spec()] * n_dep,
        out_specs=[SEM_SPEC, SEM_SPEC, HBM_SPEC, TOKEN_SPEC],
        out_shape=[sems, sems, pltpu.HBM(land.shape, land.dtype), TOKEN_SHAPE],
        input_output_aliases={1: 2},
        compiler_params=pltpu.CompilerParams(has_side_effects=SPLIT_COPY),
    )(_in_hbm(buf), _in_hbm(land), *deps)


def _small_wait(buf, land, send_sems, recv_sems, after):
    def body(buf_ref, land_ref, send_sems, recv_sems, after_ref, land_out):
        for k in range(1, N_DEV):
            me, peer, cp = _small_copy(k, buf_ref, land_ref, send_sems, recv_sems)
            cp.wait_send()
            got = land_ref.at[me ^ k]
            pltpu.make_async_remote_copy(src_ref=got, dst_ref=got, send_sem=send_sems.at[k - 1],
                                         recv_sem=recv_sems.at[k - 1], device_id=peer,
                                         device_id_type=MESH).wait_recv()

    return pl.pallas_call(
        body, name="small_gather_wait",
        in_specs=[HBM_SPEC, HBM_SPEC, SEM_SPEC, SEM_SPEC, _hbm_spec()],
        out_specs=HBM_SPEC,
        out_shape=pltpu.HBM(land.shape, land.dtype),
        input_output_aliases={1: 0},
        compiler_params=pltpu.CompilerParams(has_side_effects=SPLIT_COPY),
    )(_in_hbm(buf), land, send_sems, recv_sems, after)


def _sum_devices(land):
    n, R, C = land.shape

    def body(land_ref, out_ref):
        acc = land_ref[0]
        for d in range(1, n):
            acc = acc + land_ref[d]
        out_ref[...] = acc

    return pl.pallas_call(
        body, name="small_sum",
        in_specs=[pl.BlockSpec(memory_space=pltpu.VMEM)],
        out_specs=pl.BlockSpec(memory_space=pltpu.VMEM),
        out_shape=SDS((R, C), land.dtype),
        compiler_params=pltpu.CompilerParams(vmem_limit_bytes=V7X_VMEM_LIMIT),
    )(land)


def _deinterleave(t, d):
    if d == 1:
        return t
    S, W = t.shape
    return t.reshape(S // d, d, W).transpose(1, 0, 2).reshape(S, W)


def _interleave(t, d):
    if d == 1:
        return t
    S, W = t.shape
    return t.reshape(d, S // d, W).transpose(1, 0, 2).reshape(S, W)


def _to_patterns(t, off):
    return jnp.stack([_deinterleave(t[:, off + PW * g:off + PW * (g + 1)], PATTERN_DILATION[g])
                      for g in range(N_PATTERNS)])


def _from_patterns(t3):
    return jnp.stack([_interleave(t3[g], PATTERN_DILATION[g]) for g in range(N_PATTERNS)])


def _pack_rows(vectors):
    flat = jnp.concatenate([v.reshape(-1) for v in vectors])
    n = flat.shape[0]
    padded = -(-n // 1024) * 1024
    return jnp.pad(flat, (0, padded - n)).reshape(padded // 128, 128)


def _unpack_rows(buf, shapes):
    flat = buf.reshape(-1)
    out, off = [], 0
    for s in shapes:
        n = 1
        for dim in s:
            n *= dim
        out.append(flat[off:off + n].reshape(s))
        off += n
    return out


def _layer_forward(l, x, prm, wg):
    S, D = x.shape
    w_in = wg.get("w_in", l, x)
    p, h = _norm_matmul(f"in_proj_{l}", x, prm["attn_norm"][l], w_in, F32, deps=wg.deps())
    y_a = _sgu_fwd(f"sgu_fwd_{l}", p, prm["sgu_wt"][l], prm["sgu_bb"][l])
    y_b = _conv_fwd(f"conv_fwd_{l}", p, prm["conv_w"][l])
    os, lses = [], []
    for g in range(N_PATTERNS):
        o_g, lse_g = _attn_fwd(f"attn_fwd_{l}_{g}", p, g, prm["q_gain"][l], prm["k_gain"][l], prm["bd"])
        os.append(o_g)
        lses.append(lse_g)
    y_c = _mix_fwd(f"mix_fwd_{l}", os, lses)
    ycat = jnp.concatenate([y_a, y_b, y_c], axis=1)
    tmb, tnb = min(1024, S), min(1024, D)
    w_out = wg.get("w_out", l, ycat)
    rq = w_out.shape[1]
    x1 = _matmul(
        f"out_proj_{l}", ycat, w_out, (S, D), F32, grid=(S // tmb, D // tnb, N_CHIPS),
        a_spec=pl.BlockSpec((tmb, rq), lambda i, j, k: (i, k)),
        b_spec=pl.BlockSpec((None, rq, tnb), lambda i, j, k: (k, 0, j)),
        o_spec=pl.BlockSpec((tmb, tnb), lambda i, j, k: (i, j)),
        contract=(1, 0), acc_shape=(tmb, tnb),
        extras=(x,), extra_specs=(pl.BlockSpec((tmb, tnb), lambda i, j, k: (i, j)),),
        epi=lambda r, res: r + res, deps=wg.deps())
    w_mlp_in = wg.get("w_mlp_in", l, x1)
    a, h2 = _norm_matmul(f"mlp_in_{l}", x1, prm["mlp_norm"][l], w_mlp_in, BF16, deps=wg.deps())
    w_mlp_out = wg.get("w_mlp_out", l, a)
    dff4 = w_mlp_out.shape[1]
    tk = min(1024, dff4)
    kpc = dff4 // tk
    x2 = _matmul(
        f"mlp_out_{l}", a, w_mlp_out, (S, D), F32, grid=(S // tmb, D // tnb, N_CHIPS * kpc),
        a_spec=pl.BlockSpec((tmb, tk), lambda i, j, k: (i, k)),
        b_spec=pl.BlockSpec((None, tk, tnb), lambda i, j, k: (k // kpc, k % kpc, j)),
        o_spec=pl.BlockSpec((tmb, tnb), lambda i, j, k: (i, j)),
        contract=(1, 0), acc_shape=(tmb, tnb), a_pre=_relu2_bf16,
        extras=(x1,), extra_specs=(pl.BlockSpec((tmb, tnb), lambda i, j, k: (i, j)),),
        epi=lambda r, res: r + res, deps=wg.deps())
    saved = dict(x=x, p=p, h=h, os=os, lses=lses, ycat=ycat, x1=x1, a=a, h2=h2)
    return x2, saved


def _layer_backward(l, dx2, dx2b, sv, prm, wg, sink):
    S, D = dx2.shape
    w_in, w_out = wg.get("w_in", l), wg.get("w_out", l)
    w_mlp_in, w_mlp_out = wg.get("w_mlp_in", l), wg.get("w_mlp_out", l)
    dff4 = w_mlp_in.shape[-1]
    dff = N_CHIPS * dff4
    tm = min(512, S)
    tk = min(1024, S)
    nks = S // tk

    tmb, tnb = min(1024, S), min(1024, D)
    da = _matmul(
        f"mlp_out_bwd_{l}", dx2b, w_mlp_out, (S, dff), BF16, grid=(S // tmb, N_CHIPS, 1),
        a_spec=pl.BlockSpec((tmb, D), lambda i, j, k: (i, 0)),
        b_spec=pl.BlockSpec((None, dff4, D), lambda i, j, k: (j, 0, 0)),
        o_spec=pl.BlockSpec((tmb, dff4), lambda i, j, k: (i, j)),
        contract=(1, 1), acc_shape=(tmb, dff4),
        extras=(sv["a"],), extra_specs=(pl.BlockSpec((tmb, dff4), lambda i, j, k: (i, j)),),
        epi=lambda r, act: r * (2.0 * jnp.maximum(act.astype(F32), 0.0)), deps=sink.deps())
    tmw = min(1024, dff4)
    mpc = dff4 // tmw
    g_w2 = _matmul(
        f"mlp_out_dw_{l}", sv["a"], dx2b, (N_CHIPS, dff4, D), F32, grid=(N_CHIPS * mpc, 1, nks),
        a_spec=pl.BlockSpec((tk, tmw), lambda i, j, k: (k, i)),
        b_spec=pl.BlockSpec((tk, D), lambda i, j, k: (k, 0)),
        o_spec=pl.BlockSpec((None, tmw, D), lambda i, j, k: (i // mpc, i % mpc, 0)),
        contract=(0, 0), acc_shape=(tmw, D), a_pre=_relu2_bf16)
    sink.begin("w_mlp_out", l, g_w2)
    dh2 = _matmul(
        f"mlp_in_bwd_{l}", da, w_mlp_in, (S, D), F32, grid=(S // tmb, D // tnb, N_CHIPS),
        a_spec=pl.BlockSpec((tmb, dff4), lambda i, j, k: (i, k)),
        b_spec=pl.BlockSpec((None, tnb, dff4), lambda i, j, k: (k, j, 0)),
        o_spec=pl.BlockSpec((tmb, tnb), lambda i, j, k: (i, j)),
        contract=(1, 1), acc_shape=(tmb, tnb), deps=sink.deps())
    sink.advance(dh2)
    tmd = min(1024, D)
    g_w1 = _matmul(
        f"mlp_in_dw_{l}", sv["h2"], da, (N_CHIPS, D, dff4), F32, grid=(N_CHIPS, D // tmd, nks),
        a_spec=pl.BlockSpec((tk, tmd), lambda i, j, k: (k, j)),
        b_spec=pl.BlockSpec((tk, dff4), lambda i, j, k: (k, i)),
        o_spec=pl.BlockSpec((None, tmd, dff4), lambda i, j, k: (i, j, 0)),
        contract=(0, 0), acc_shape=(tmd, dff4))
    sink.begin("w_mlp_in", l, g_w1)
    dx1, dx1b, g_mlp_norm = _rmsnorm_bwd(f"mlp_norm_bwd_{l}", dh2, sv["x1"], prm["mlp_norm"][l], dx2,
                                         deps=sink.deps())

    rq = w_out.shape[1]
    dycat = _matmul(
        f"out_proj_bwd_{l}", dx1b, w_out, (S, N_CHIPS * rq), F32, grid=(S // tmb, N_CHIPS, 1),
        a_spec=pl.BlockSpec((tmb, D), lambda i, j, k: (i, 0)),
        b_spec=pl.BlockSpec((None, rq, D), lambda i, j, k: (j, 0, 0)),
        o_spec=pl.BlockSpec((tmb, rq), lambda i, j, k: (i, j)),
        contract=(1, 1), acc_shape=(tmb, rq))
    sink.advance(dycat)
    g_wout = _matmul(
        f"out_proj_dw_{l}", sv["ycat"], dx1b, (N_CHIPS, rq, D), F32, grid=(N_CHIPS, 1, nks),
        a_spec=pl.BlockSpec((tk, rq), lambda i, j, k: (k, i)),
        b_spec=pl.BlockSpec((tk, D), lambda i, j, k: (k, 0)),
        o_spec=pl.BlockSpec((None, rq, D), lambda i, j, k: (i, 0, 0)),
        contract=(0, 0), acc_shape=(rq, D))
    sink.begin("w_out", l, g_wout)

    p = sv["p"]
    du, dv_a, g_sgu_w, db_lanes = _sgu_bwd(f"sgu_bwd_{l}", p, dycat, prm["sgu_wt"][l], prm["sgu_wtt"][l],
                                           prm["sgu_bb"][l])
    sink.advance(du)
    g_sgu_b = db_lanes[:, :A_HEADS].T
    db, dc, dxb, g_conv = _conv_bwd(f"conv_bwd_{l}", p, dycat, prm["conv_w"][l])
    do3, c3 = _mix_bwd(f"mix_bwd_{l}", sv["os"], sv["lses"], dycat, prm["bd"])
    dqs, dks, dvs, dgqs, dgks = [], [], [], [], []
    for g in range(N_PATTERNS):
        dq, dk, dv, dgq, dgk = _attn_bwd(f"attn_bwd_{l}_{g}", p, g, sv["lses"][g], do3, c3,
                                         prm["q_gain"][l], prm["k_gain"][l], prm["bd"])
        dqs.append(dq)
        dks.append(dk)
        dvs.append(dv)
        dgqs.append(dgq)
        dgks.append(dgk)
    g_q = jnp.concatenate(dgqs, axis=1).reshape(N_PATTERNS * PW // HEAD_DIM, HEAD_DIM).sum(axis=0)
    g_k = jnp.concatenate(dgks, axis=1).reshape(N_PATTERNS * PW // HEAD_DIM, HEAD_DIM).sum(axis=0)
    dp = jnp.concatenate([du, dv_a, db, dc, dxb] + [t.astype(BF16) for t in dqs + dks + dvs], axis=1)

    ns_in = w_in.shape[-1]
    g_win = _matmul(
        f"in_proj_dw_{l}", sv["h"], dp, (N_CHIPS, D, ns_in), F32, grid=(N_CHIPS, D // tmd, nks),
        a_spec=pl.BlockSpec((tk, tmd), lambda i, j, k: (k, j)),
        b_spec=pl.BlockSpec((tk, ns_in), lambda i, j, k: (k, i)),
        o_spec=pl.BlockSpec((None, tmd, ns_in), lambda i, j, k: (i, j, 0)),
        contract=(0, 0), acc_shape=(tmd, ns_in))
    sink.begin("w_in", l, g_win)
    dh = _matmul(
        f"in_proj_bwd_{l}", dp, w_in, (S, D), F32, grid=(S // tmb, D // tnb, N_CHIPS),
        a_spec=pl.BlockSpec((tmb, ns_in), lambda i, j, k: (i, k)),
        b_spec=pl.BlockSpec((None, tnb, ns_in), lambda i, j, k: (k, j, 0)),
        o_spec=pl.BlockSpec((tmb, tnb), lambda i, j, k: (i, j)),
        contract=(1, 1), acc_shape=(tmb, tnb), deps=sink.deps())
    sink.advance(dh)
    dx0, dx0b, g_attn_norm = _rmsnorm_bwd(f"attn_norm_bwd_{l}", dh, sv["x"], prm["attn_norm"][l], dx1,
                                          deps=sink.deps())

    big = dict(w_in=g_win, w_out=g_wout, w_mlp_in=g_w1, w_mlp_out=g_w2)
    small = dict(attn_norm=g_attn_norm.reshape(-1), sgu_w=g_sgu_w, sgu_b=g_sgu_b, conv_w=g_conv,
                 q_norm=g_q, k_norm=g_k, mlp_norm=g_mlp_norm.reshape(-1))
    return dx0, dx0b, big, small


BIG = ("w_in", "w_out", "w_mlp_in", "w_mlp_out")
SMALL_REPLICATED = ("attn_norm", "sgu_w", "sgu_b", "q_norm", "k_norm", "mlp_norm")


def _local_step(x, target, prm, wg, n_layers, sink):
    saved = []
    h = x
    for l in range(n_layers):
        h, sv = _layer_forward(l, h, prm, wg)
        saved.append(sv)
    dy, dyb, colsq = _loss_kernel(h, target)
    loss = 0.5 * jnp.sum(colsq) / x.shape[1]
    bigs, smalls = [None] * n_layers, [None] * n_layers
    for l in reversed(range(n_layers)):
        dy, dyb, bigs[l], smalls[l] = _layer_backward(l, dy, dyb, saved[l], prm, wg, sink)
    return loss, dy, bigs, smalls


def _prepare_params(attn_norm, sgu_w, sgu_b, conv_full, q_norm, k_norm, mlp_norm):
    n_layers = attn_norm.shape[0]
    tri = jnp.tril(sgu_w)
    idx = jnp.arange(PW)
    bd = (idx[:, None] // HEAD_DIM == idx[None, :] // HEAD_DIM).astype(BF16)
    return dict(
        attn_norm=[attn_norm[l][None, :] for l in range(n_layers)],
        mlp_norm=[mlp_norm[l][None, :] for l in range(n_layers)],
        sgu_wt=[tri[l].astype(BF16) for l in range(n_layers)],
        sgu_wtt=[tri[l].transpose(0, 2, 1).astype(BF16) for l in range(n_layers)],
        sgu_bb=[jnp.repeat(sgu_b[l].T, HEAD_DIM, axis=1) for l in range(n_layers)],
        conv_w=[conv_full[l] for l in range(n_layers)],
        q_gain=[jnp.tile(q_norm[l], PW // HEAD_DIM)[None, :] for l in range(n_layers)],
        k_gain=[jnp.tile(k_norm[l], PW // HEAD_DIM)[None, :] for l in range(n_layers)],
        bd=bd,
    )


def kernel(x, attn_norm, w_in, sgu_w, sgu_b, conv_w, q_norm, k_norm, w_out, mlp_norm, w_mlp_in, w_mlp_out, loss_target, m_attn_norm, m_w_in, m_sgu_w, m_sgu_b, m_conv_w, m_q_norm, m_k_norm, m_w_out, m_mlp_norm, m_w_mlp_in, m_w_mlp_out, v_attn_norm, v_w_in, v_sgu_w, v_sgu_b, v_conv_w, v_q_norm, v_k_norm, v_w_out, v_mlp_norm, v_w_mlp_in, v_w_mlp_out):
    n_layers = attn_norm.shape[0]
    weights = dict(attn_norm=attn_norm, w_in=w_in, sgu_w=sgu_w, sgu_b=sgu_b, conv_w=conv_w, q_norm=q_norm,
                   k_norm=k_norm, w_out=w_out, mlp_norm=mlp_norm, w_mlp_in=w_mlp_in, w_mlp_out=w_mlp_out)
    mom_m = dict(attn_norm=m_attn_norm, w_in=m_w_in, sgu_w=m_sgu_w, sgu_b=m_sgu_b, conv_w=m_conv_w,
                 q_norm=m_q_norm, k_norm=m_k_norm, w_out=m_w_out, mlp_norm=m_mlp_norm, w_mlp_in=m_w_mlp_in,
                 w_mlp_out=m_w_mlp_out)
    mom_v = dict(attn_norm=v_attn_norm, w_in=v_w_in, sgu_w=v_sgu_w, sgu_b=v_sgu_b, conv_w=v_conv_w,
                 q_norm=v_q_norm, k_norm=v_k_norm, w_out=v_w_out, mlp_norm=v_mlp_norm, w_mlp_in=v_w_mlp_in,
                 w_mlp_out=v_w_mlp_out)
    order = ("attn_norm", "w_in", "sgu_w", "sgu_b", "conv_w", "q_norm", "k_norm", "w_out", "mlp_norm",
             "w_mlp_in", "w_mlp_out")
    chip = 2 * lax.axis_index("x") + lax.axis_index("y")
    c_arr = lax.axis_index("c").astype(jnp.int32).reshape(1)

    conv_cols = conv_w.shape[-1]
    chip_arr = chip.astype(jnp.int32).reshape(1)
    conv_pack = jnp.pad(conv_w.reshape(-1), (0, 2048 - conv_w.size)).reshape(1, 16, 128)
    wg = _GatheredWeights()
    wg.start([("conv_w", 0), ("w_in", 0)],
             [_place_shard("place_conv_w", conv_pack, 0, chip_arr, F32),
              _place_shard("place_w_in_0", weights["w_in"], 0, chip_arr, BF16)])
    keys = [(n, l) for l in range(n_layers) for n in BIG if (n, l) != ("w_in", 0)]
    first = wg.deps()
    wg.start(keys, [_place_shard(f"place_{n}_{l}", weights[n], l, chip_arr, BF16, deps=first) for n, l in keys])
    conv_full = wg.get("conv_w", 0, wg.deps()[-1]).reshape(N_CHIPS, 2048)[:, :conv_w.size].reshape(N_CHIPS, n_layers, 3, conv_cols)
    conv_full = conv_full.transpose(1, 2, 0, 3).reshape(n_layers, 3, N_CHIPS * conv_cols)
    prm = _prepare_params(attn_norm, sgu_w, sgu_b, conv_full, q_norm, k_norm, mlp_norm)

    sink = _GradReducer(c_arr)
    loss_local, grad_x, _, smalls = _local_step(x[0], loss_target[0], prm, wg, n_layers, sink)
    loss = lax.psum(loss_local, ("x", "y", "c"))

    small_names = SMALL_REPLICATED + ("conv_w",)
    small_shapes = [(n_layers,) + tuple(smalls[0][n].shape) for n in small_names]
    packed = _pack_rows([jnp.stack([smalls[l][n] for l in range(n_layers)]) for n in small_names])
    small_send, small_recv, small_land, small_token = _small_start(packed, sink.deps())

    grads, delta, new_m, new_v = {}, {}, {}, {}

    def update(names, after):
        joined = sink.finish(names, n_layers, after)
        for n in names:
            shp = weights[n].shape
            two_d = (shp[0] * shp[1], shp[2])
            d, nm, nv, g = _adamw(f"adamw_{n}", weights[n].reshape(two_d), joined[n].reshape(two_d),
                                  mom_m[n].reshape(two_d), mom_v[n].reshape(two_d))
            grads[n], delta[n], new_m[n], new_v[n] = g.reshape(shp), d.reshape(shp), nm.reshape(shp), nv.reshape(shp)

    update(("w_mlp_out", "w_mlp_in", "w_out"), small_token)
    update(("w_in",), delta["w_out"])
    small_land = _small_wait(packed, small_land, small_send, small_recv, delta["w_in"])
    grads.update(zip(small_names, _unpack_rows(_sum_devices(small_land), small_shapes)))
    grads["conv_w"] = lax.dynamic_slice_in_dim(grads["conv_w"], chip * conv_cols, conv_cols, axis=2)
    smalls_all = SMALL_REPLICATED + ("conv_w",)
    shapes = [weights[n].shape for n in smalls_all]
    d, nm, nv, _ = _adamw("adamw_small",
                       _pack_rows([weights[n] for n in smalls_all]), _pack_rows([grads[n] for n in smalls_all]),
                       _pack_rows([mom_m[n] for n in smalls_all]), _pack_rows([mom_v[n] for n in smalls_all]))
    for n, dd, mm, vv in zip(smalls_all, _unpack_rows(d, shapes), _unpack_rows(nm, shapes), _unpack_rows(nv, shapes)):
        delta[n], new_m[n], new_v[n] = dd, mm, vv

    return (loss, grad_x[None], *[grads[n] for n in order], *[delta[n] for n in order],
            *[new_m[n] for n in order], *[new_v[n] for n in order])
```

```python
import jax
import jax.numpy as jnp
from jax import lax
from jax.experimental import pallas as pl
from jax.experimental.pallas import tpu as pltpu

F32 = jnp.float32
BF16 = jnp.bfloat16
SDS = jax.ShapeDtypeStruct

EPS = 1e-6
HEAD_DIM = 64
A_HEADS = 8
A_WIDTH = 512
CHUNK = 128
B_WIDTH = 768
C_WIDTH = 768
N_PATTERNS = 3
PATTERN_DILATION = (1, 4, 16)
PW = 256
D_IN_PROJ = 5632
OFF_AU, OFF_AV, OFF_BB, OFF_BC, OFF_BX, OFF_Q, OFF_K, OFF_V = 0, 512, 1024, 1792, 2560, 3328, 4096, 4864
N_CHIPS = 4
N_DEV = 8
BLK = 128

ADAM_LR, ADAM_B1, ADAM_B2, ADAM_EPS, ADAM_WD, ADAM_STEP = 0.001, 0.9, 0.999, 1e-08, 0.01, 10

V7X_VMEM_LIMIT = 56 * 1024 * 1024
MESH = pl.DeviceIdType.MESH
NEG = -1e30


def _cp(n_axes):
    return pltpu.CompilerParams(dimension_semantics=("arbitrary",) * n_axes, vmem_limit_bytes=V7X_VMEM_LIMIT)


def _hbm_spec():
    return pl.BlockSpec(memory_space=pl.ANY)


def _norm_matmul(name, x, g, wg, out_dtype, deps=()):
    S, D = x.shape
    ns, _, Ns = wg.shape
    tm = min(512, S)
    n_dep = len(deps)

    def body(x_ref, g_ref, w_ref, *rest):
        o_ref, h_ref, hs_ref = rest[n_dep:]
        @pl.when(pl.program_id(1) == 0)
        def _():
            xv = x_ref[...]
            y = xv * lax.rsqrt(jnp.mean(xv * xv, axis=-1, keepdims=True) + EPS) * g_ref[...]
            hb = y.astype(BF16)
            hs_ref[...] = hb
            h_ref[...] = hb
        o_ref[...] = jnp.dot(hs_ref[...], w_ref[...], preferred_element_type=F32).astype(o_ref.dtype)

    return pl.pallas_call(
        body, name=name, grid=(S // tm, ns),
        in_specs=[pl.BlockSpec((tm, D), lambda i, s: (i, 0)),
                  pl.BlockSpec((1, D), lambda i, s: (0, 0)),
                  pl.BlockSpec((None, D, Ns), lambda i, s: (s, 0, 0))] + [_hbm_spec()] * n_dep,
        out_specs=[pl.BlockSpec((tm, Ns), lambda i, s: (i, s)),
                   pl.BlockSpec((tm, D), lambda i, s: (i, 0))],
        out_shape=[SDS((S, ns * Ns), out_dtype), SDS((S, D), BF16)],
        scratch_shapes=[pltpu.VMEM((tm, D), BF16)],
        compiler_params=_cp(2),
    )(x, g, wg, *deps)


def _matmul(name, a, b, out_shape, out_dtype, *, grid, a_spec, b_spec, o_spec, contract, acc_shape,
            extras=(), extra_specs=(), a_pre=None, epi=None, deps=()):
    nk = grid[2]
    n_ex = len(extras)
    n_dep = len(deps)
    dims = (((contract[0],), (contract[1],)), ((), ()))

    def product(a_ref, b_ref):
        av = a_ref[...]
        if a_pre is not None:
            av = a_pre(av)
        return lax.dot_general(av, b_ref[...], dims, preferred_element_type=F32)

    def finish(r, ex, o_ref):
        if epi is not None:
            r = epi(r, *[e[...] for e in ex])
        o_ref[...] = r.astype(o_ref.dtype)

    def body_single(a_ref, b_ref, *rest):
        finish(product(a_ref, b_ref), rest[:n_ex], rest[n_ex + n_dep])

    def body(a_ref, b_ref, *rest):
        ex = rest[:n_ex]
        o_ref = rest[n_ex + n_dep]
        acc_ref = rest[n_ex + n_dep + 1]
        k = pl.program_id(2)

        @pl.when(k == 0)
        def _():
            acc_ref[...] = product(a_ref, b_ref)

        @pl.when((k > 0) & (k < nk - 1))
        def _():
            acc_ref[...] += product(a_ref, b_ref)

        @pl.when(k == nk - 1)
        def _():
            finish(acc_ref[...] + product(a_ref, b_ref), ex, o_ref)

    return pl.pallas_call(
        body_single if nk == 1 else body, name=name, grid=grid,
        in_specs=[a_spec, b_spec, *extra_specs] + [_hbm_spec()] * n_dep,
        out_specs=o_spec,
        out_shape=SDS(out_shape, out_dtype),
        scratch_shapes=[] if nk == 1 else [pltpu.VMEM(acc_shape, F32)],
        compiler_params=_cp(3),
    )(a, b, *extras, *deps)


def _relu2_bf16(t):
    r = jnp.maximum(t.astype(F32), 0.0)
    return (r * r).astype(BF16)


def _loss_kernel(y, t):
    S, D = y.shape
    tm = min(256, S)

    def body(y_ref, t_ref, dy_ref, dyb_ref, l_ref):
        @pl.when(pl.program_id(0) == 0)
        def _():
            l_ref[...] = jnp.zeros_like(l_ref)
        e = y_ref[...] - t_ref[...]
        l_ref[...] += jnp.sum(e * e, axis=0, keepdims=True)
        dy = e * (1.0 / D)
        dy_ref[...] = dy
        dyb_ref[...] = dy.astype(BF16)

    row = pl.BlockSpec((tm, D), lambda i: (i, 0))
    return pl.pallas_call(
        body, name="loss_head", grid=(S // tm,),
        in_specs=[row, row],
        out_specs=[row, row, pl.BlockSpec((1, D), lambda i: (0, 0))],
        out_shape=[SDS((S, D), F32), SDS((S, D), BF16), SDS((1, D), F32)],
        compiler_params=_cp(1),
    )(y, t)


def _rmsnorm_bwd(name, dh, x, g, dres, deps=()):
    S, D = x.shape
    tm = min(256, S)
    n_dep = len(deps)

    def body(dh_ref, x_ref, g_ref, dres_ref, *rest):
        dx_ref, dxb_ref, dg_ref = rest[n_dep:]
        @pl.when(pl.program_id(0) == 0)
        def _():
            dg_ref[...] = jnp.zeros_like(dg_ref)
        xv = x_ref[...]
        dhv = dh_ref[...]
        rstd = lax.rsqrt(jnp.mean(xv * xv, axis=-1, keepdims=True) + EPS)
        xhat = xv * rstd
        dg_ref[...] += jnp.sum(dhv * xhat, axis=0, keepdims=True)
        dxn = dhv * g_ref[...]
        dx = dres_ref[...] + rstd * (dxn - xhat * jnp.mean(dxn * xhat, axis=-1, keepdims=True))
        dx_ref[...] = dx
        dxb_ref[...] = dx.astype(BF16)

    row = pl.BlockSpec((tm, D), lambda i: (i, 0))
    vec = pl.BlockSpec((1, D), lambda i: (0, 0))
    return pl.pallas_call(
        body, name=name, grid=(S // tm,),
        in_specs=[row, row, vec, row] + [_hbm_spec()] * n_dep,
        out_specs=[row, row, vec],
        out_shape=[SDS((S, D), F32), SDS((S, D), BF16), SDS((1, D), F32)],
        compiler_params=_cp(1),
    )(dh, x, g, dres, *deps)


def _adamw(name, w, g, m, v):
    R, C = w.shape
    tr = 256 if R % 256 == 0 else R
    c1 = 1.0 - ADAM_B1 ** ADAM_STEP
    c2 = 1.0 - ADAM_B2 ** ADAM_STEP

    def body(w_ref, g_ref, m_ref, v_ref, d_ref, nm_ref, nv_ref, g_out_ref):
        gv = g_ref[...]
        nm = ADAM_B1 * m_ref[...] + (1.0 - ADAM_B1) * gv
        nv = ADAM_B2 * v_ref[...] + (1.0 - ADAM_B2) * (gv * gv)
        m_hat = nm / c1
        v_hat = nv / c2
        d_ref[...] = -ADAM_LR * (m_hat / (jnp.sqrt(v_hat) + ADAM_EPS) + ADAM_WD * w_ref[...])
        nm_ref[...] = nm
        nv_ref[...] = nv
        g_out_ref[...] = gv

    blk = pl.BlockSpec((tr, C), lambda i: (i, 0))
    return pl.pallas_call(
        body, name=name, grid=(R // tr,),
        in_specs=[blk] * 4, out_specs=[blk] * 4,
        out_shape=[SDS((R, C), F32)] * 4,
        compiler_params=_cp(1),
    )(w, g, m, v)


def _pair_select(lane, lo, hi):
    return jnp.where(lane < HEAD_DIM, lo, hi)


def _sgu_fwd(name, p, wt, bb):
    S = p.shape[0]

    def body(u_ref, v_ref, wt_ref, bb_ref, o_ref):
        lane = lax.broadcasted_iota(jnp.int32, (CHUNK, 128), 1)
        for pp in range(A_HEADS // 2):
            cs = slice(128 * pp, 128 * (pp + 1))
            vb = v_ref[:, cs].astype(BF16)
            mixed = _pair_select(lane,
                                 jnp.dot(wt_ref[2 * pp], vb, preferred_element_type=F32),
                                 jnp.dot(wt_ref[2 * pp + 1], vb, preferred_element_type=F32)) + bb_ref[:, cs]
            o_ref[:, cs] = (u_ref[:, cs] * mixed).astype(o_ref.dtype)

    return pl.pallas_call(
        body, name=name, grid=(S // CHUNK,),
        in_specs=[pl.BlockSpec((CHUNK, A_WIDTH), lambda c: (c, OFF_AU // A_WIDTH)),
                  pl.BlockSpec((CHUNK, A_WIDTH), lambda c: (c, OFF_AV // A_WIDTH)),
                  pl.BlockSpec((A_HEADS, CHUNK, CHUNK), lambda c: (0, 0, 0)),
                  pl.BlockSpec((CHUNK, A_WIDTH), lambda c: (0, 0))],
        out_specs=pl.BlockSpec((CHUNK, A_WIDTH), lambda c: (c, 0)),
        out_shape=SDS((S, A_WIDTH), BF16),
        compiler_params=_cp(1),
    )(p, p, wt, bb)


def _sgu_bwd(name, p, dycat, wt, wtt, bb):
    S = p.shape[0]

    def body(u_ref, v_ref, dy_ref, wt_ref, wtt_ref, bb_ref, du_ref, dv_ref, dw_ref, db_ref, dbacc_ref):
        c = pl.program_id(0)

        @pl.when(c == 0)
        def _():
            dw_ref[...] = jnp.zeros_like(dw_ref)
            dbacc_ref[...] = jnp.zeros_like(dbacc_ref)

        lane = lax.broadcasted_iota(jnp.int32, (CHUNK, 128), 1)
        row = lax.broadcasted_iota(jnp.int32, (CHUNK, 128), 0)
        causal = row >= lane
        for pp in range(A_HEADS // 2):
            cs = slice(128 * pp, 128 * (pp + 1))
            v = v_ref[:, cs]
            vb = v.astype(BF16)
            u = u_ref[:, cs]
            dy = dy_ref[:, cs]
            mixed = _pair_select(lane,
                                 jnp.dot(wt_ref[2 * pp], vb, preferred_element_type=F32),
                                 jnp.dot(wt_ref[2 * pp + 1], vb, preferred_element_type=F32)) + bb_ref[:, cs]
            du_ref[:, cs] = (dy * mixed).astype(du_ref.dtype)
            dm = dy * u
            dmb = dm.astype(BF16)
            dv = _pair_select(lane,
                              jnp.dot(wtt_ref[2 * pp], dmb, preferred_element_type=F32),
                              jnp.dot(wtt_ref[2 * pp + 1], dmb, preferred_element_type=F32))
            dv_ref[:, cs] = dv.astype(dv_ref.dtype)
            dbacc_ref[:, cs] += dm
            nt = (((1,), (1,)), ((), ()))
            dm_lo = jnp.where(lane < HEAD_DIM, dm, 0.0).astype(BF16)
            dm_hi = jnp.where(lane >= HEAD_DIM, dm, 0.0).astype(BF16)
            dw_ref[2 * pp] += jnp.where(causal, lax.dot_general(dm_lo, vb, nt, preferred_element_type=F32), 0.0)
            dw_ref[2 * pp + 1] += jnp.where(causal, lax.dot_general(dm_hi, vb, nt, preferred_element_type=F32), 0.0)

        @pl.when(c == S // CHUNK - 1)
        def _():
            out = jnp.zeros((CHUNK, 128), F32)
            for pp in range(A_HEADS // 2):
                acc = dbacc_ref[:, 128 * pp:128 * (pp + 1)]
                s_lo = jnp.sum(jnp.where(lane < HEAD_DIM, acc, 0.0), axis=1, keepdims=True)
                s_hi = jnp.sum(jnp.where(lane >= HEAD_DIM, acc, 0.0), axis=1, keepdims=True)
                out = jnp.where(lane == 2 * pp, s_lo, out)
                out = jnp.where(lane == 2 * pp + 1, s_hi, out)
            db_ref[...] = out

    chunk = lambda col: pl.BlockSpec((CHUNK, A_WIDTH), lambda c: (c, col))
    wspec = pl.BlockSpec((A_HEADS, CHUNK, CHUNK), lambda c: (0, 0, 0))
    return pl.pallas_call(
        body, name=name, grid=(S // CHUNK,),
        in_specs=[chunk(OFF_AU // A_WIDTH), chunk(OFF_AV // A_WIDTH), chunk(0), wspec, wspec,
                  pl.BlockSpec((CHUNK, A_WIDTH), lambda c: (0, 0))],
        out_specs=[chunk(0), chunk(0), wspec, pl.BlockSpec((CHUNK, 128), lambda c: (0, 0))],
        out_shape=[SDS((S, A_WIDTH), BF16), SDS((S, A_WIDTH), BF16),
                   SDS((A_HEADS, CHUNK, CHUNK), F32), SDS((CHUNK, 128), F32)],
        scratch_shapes=[pltpu.VMEM((CHUNK, A_WIDTH), F32)],
        compiler_params=_cp(1),
    )(p, p, dycat, wt, wtt, bb)


CONV_HALO = 8


def _shift_down(a, halo, k):
    T = a.shape[0]
    row = lax.broadcasted_iota(jnp.int32, a.shape, 0)
    out = pltpu.roll(a, k, 0)
    for r in range(k):
        out = jnp.where(row == r, halo[CONV_HALO - k + r:CONV_HALO - k + r + 1, :], out)
    return out


def _shift_up(a, halo, k):
    T = a.shape[0]
    row = lax.broadcasted_iota(jnp.int32, a.shape, 0)
    out = pltpu.roll(a, T - k, 0)
    for r in range(k):
        out = jnp.where(row == T - k + r, halo[r:r + 1, :], out)
    return out


def _conv_specs(S, T):
    hb = T // CONV_HALO
    last = S // CONV_HALO - 1
    tile = lambda col0: pl.BlockSpec((T, 128), lambda j, i: (i, col0 + j))
    prev = lambda col0: pl.BlockSpec((CONV_HALO, 128), lambda j, i: (jnp.maximum(i * hb - 1, 0), col0 + j))
    nxt = lambda col0: pl.BlockSpec((CONV_HALO, 128), lambda j, i: (jnp.minimum((i + 1) * hb, last), col0 + j))
    return tile, prev, nxt


def _conv_fwd(name, p, w):
    S = p.shape[0]
    T = min(512, S)
    tile, prev, _ = _conv_specs(S, T)
    cb, cc, cx = OFF_BB // 128, OFF_BC // 128, OFF_BX // 128

    def body(b_ref, c_ref, x_ref, ch_ref, xh_ref, w_ref, o_ref):
        i = pl.program_id(1)
        z = c_ref[...] * x_ref[...]
        zh = jnp.where(i > 0, ch_ref[...] * xh_ref[...], 0.0)
        z1 = _shift_down(z, zh, 1)
        z2 = _shift_down(z, zh, 2)
        conv = w_ref[0:1, :] * z2 + w_ref[1:2, :] * z1 + w_ref[2:3, :] * z
        o_ref[...] = (b_ref[...] * conv).astype(o_ref.dtype)

    return pl.pallas_call(
        body, name=name, grid=(B_WIDTH // 128, S // T),
        in_specs=[tile(cb), tile(cc), tile(cx), prev(cc), prev(cx),
                  pl.BlockSpec((3, 128), lambda j, i: (0, j))],
        out_specs=tile(0),
        out_shape=SDS((S, B_WIDTH), BF16),
        compiler_params=_cp(2),
    )(p, p, p, p, p, w)


def _conv_bwd(name, p, dycat, w):
    S = p.shape[0]
    T = min(512, S)
    tile, prev, nxt = _conv_specs(S, T)
    cb, cc, cx = OFF_BB // 128, OFF_BC // 128, OFF_BX // 128
    cdy = A_WIDTH // 128
    n_i = S // T

    def body(b_ref, c_ref, x_ref, dy_ref, ch_ref, xh_ref, bn_ref, dyn_ref, w_ref,
             db_ref, dc_ref, dx_ref, dw_ref):
        i = pl.program_id(1)

        @pl.when(i == 0)
        def _():
            dw_ref[...] = jnp.zeros_like(dw_ref)

        cv = c_ref[...]
        xv = x_ref[...]
        z = cv * xv
        zh = jnp.where(i > 0, ch_ref[...] * xh_ref[...], 0.0)
        z1 = _shift_down(z, zh, 1)
        z2 = _shift_down(z, zh, 2)
        w0, w1, w2 = w_ref[0:1, :], w_ref[1:2, :], w_ref[2:3, :]
        conv = w0 * z2 + w1 * z1 + w2 * z
        dy = dy_ref[...]
        db_ref[...] = (dy * conv).astype(db_ref.dtype)
        dconv = dy * b_ref[...]
        dconv_n = jnp.where(i < n_i - 1, dyn_ref[...] * bn_ref[...], 0.0)
        dz = w2 * dconv + w1 * _shift_up(dconv, dconv_n, 1) + w0 * _shift_up(dconv, dconv_n, 2)
        dc_ref[...] = (dz * xv).astype(dc_ref.dtype)
        dx_ref[...] = (dz * cv).astype(dx_ref.dtype)
        dw_ref[0:1, :] += jnp.sum(dconv * z2, axis=0, keepdims=True)
        dw_ref[1:2, :] += jnp.sum(dconv * z1, axis=0, keepdims=True)
        dw_ref[2:3, :] += jnp.sum(dconv * z, axis=0, keepdims=True)

    wspec = pl.BlockSpec((3, 128), lambda j, i: (0, j))
    return pl.pallas_call(
        body, name=name, grid=(B_WIDTH // 128, n_i),
        in_specs=[tile(cb), tile(cc), tile(cx), tile(cdy), prev(cc), prev(cx), nxt(cb), nxt(cdy), wspec],
        out_specs=[tile(0), tile(0), tile(0), wspec],
        out_shape=[SDS((S, B_WIDTH), BF16)] * 3 + [SDS((3, B_WIDTH), F32)],
        compiler_params=_cp(2),
    )(p, p, p, dycat, p, p, p, dycat, w)


def _seg_sum(t, bd):
    hi = t.astype(BF16)
    lo = (t - hi.astype(F32)).astype(BF16)
    return jnp.dot(hi, bd, preferred_element_type=F32) + jnp.dot(lo, bd, preferred_element_type=F32)


def _head_norm(x, g, bd):
    rstd = lax.rsqrt(_seg_sum(x * x, bd) * (1.0 / HEAD_DIM) + EPS)
    xhat = x * rstd
    return xhat * g, xhat, rstd


def _head_norm_bwd(dy, g, xhat, rstd, bd):
    dxh = dy * g
    return rstd * (dxh - xhat * (_seg_sum(dxh * xhat, bd) * (1.0 / HEAD_DIM)))


def _band_mask(has_prev):
    row = lax.broadcasted_iota(jnp.int32, (BLK, 2 * BLK), 0)
    col = lax.broadcasted_iota(jnp.int32, (BLK, 2 * BLK), 1)
    first_key = jnp.where(has_prev, 0, BLK)
    return (col >= row) & (col <= row + BLK) & (col >= first_key)


def _first_of_segment(g, n, n_blocks):
    per_seg = lax.shift_right_logical(jnp.int32(n_blocks), 2 * g)
    return (n & (per_seg - 1)) == 0


def _residue_rows(r, d):
    return slice(None) if d == 1 else pl.ds(r, BLK, stride=d)


HW = 128
N_STACK = HW // HEAD_DIM


def _for_residues(d, fn):
    if d == 1:
        fn(0)
    else:
        def two(i, carry):
            fn(2 * i)
            fn(2 * i + 1)
            return carry
        lax.fori_loop(0, d // 2, two, 0)


def _head_mask(lane, j):
    return (lane >= HEAD_DIM * j) & (lane < HEAD_DIM * (j + 1))


def _stack_heads(x, lane):
    return jnp.concatenate([jnp.where(_head_mask(lane, j), x, 0.0) for j in range(N_STACK)], axis=0)


def _unstack_heads(y, lane):
    out = y[:BLK]
    for j in range(1, N_STACK):
        out = jnp.where(lane >= HEAD_DIM * j, y[BLK * j:BLK * (j + 1)], out)
    return out


def _head_columns(v, lane):
    return jnp.concatenate([jnp.max(jnp.where(_head_mask(lane, j), v, NEG), axis=1, keepdims=True)
                            for j in range(N_STACK)], axis=0)


def _attn_fwd(name, p, g, gq, gk, bd):
    S = p.shape[0]
    d = PATTERN_DILATION[g]
    rows = BLK * d
    nt = (((1,), (1,)), ((), ()))

    def body(q_ref, kc_ref, kp_ref, vc_ref, vp_ref, gq_ref, gk_ref, bd_ref, o_ref, lse_ref):
        has_prev = pl.program_id(1) > 0
        bdv = bd_ref[...]
        band = jnp.concatenate([_band_mask(has_prev)] * N_STACK, axis=0)
        lane = lax.broadcasted_iota(jnp.int32, (1, HW), 1)

        def residue(r):
            rr = _residue_rows(r, d)
            qn, _, _ = _head_norm(q_ref[rr, :], gq_ref[...], bdv)
            kn, _, _ = _head_norm(jnp.concatenate([kp_ref[rr, :], kc_ref[rr, :]], axis=0), gk_ref[...], bdv)
            knb = kn.astype(BF16)
            vb = jnp.concatenate([vp_ref[rr, :], vc_ref[rr, :]], axis=0).astype(BF16)
            qs = _stack_heads(qn, lane).astype(BF16)
            s = lax.dot_general(qs, knb, nt, preferred_element_type=F32) * (HEAD_DIM ** -0.5)
            s = jnp.where(band, s, NEG)
            m = jnp.max(s, axis=1, keepdims=True)
            e = jnp.exp(s - m)
            den = jnp.sum(e, axis=1, keepdims=True)
            pv = jnp.dot(e.astype(BF16), vb, preferred_element_type=F32)
            o_ref[rr, :] = _unstack_heads(pv / den, lane)
            lse_ref[rr, :] = _unstack_heads(jnp.broadcast_to(m + jnp.log(den), pv.shape), lane)

        _for_residues(d, residue)

    per = PW // HW
    cq, ck, cv = (OFF_Q + PW * g) // HW, (OFF_K + PW * g) // HW, (OFF_V + PW * g) // HW
    cur = lambda col: pl.BlockSpec((rows, HW), lambda h, n: (n, col + h))
    prv = lambda col: pl.BlockSpec((rows, HW), lambda h, n: (jnp.maximum(n - 1, 0), col + h))
    vec = pl.BlockSpec((1, HW), lambda h, n: (0, h))
    return pl.pallas_call(
        body, name=name, grid=(per, S // rows),
        in_specs=[cur(cq), cur(ck), prv(ck), cur(cv), prv(cv), vec, vec, pl.BlockSpec((HW, HW), lambda h, n: (0, 0))],
        out_specs=[cur(0), cur(0)],
        out_shape=[SDS((S, PW), F32)] * 2,
        compiler_params=_cp(2),
    )(p, p, p, p, p, gq, gk, bd)


def _attn_bwd(name, p, g, lse, do3, c3, gq, gk, bd):
    S = p.shape[0]
    d = PATTERN_DILATION[g]
    rows = BLK * d
    nblk = S // rows
    nt = (((1,), (1,)), ((), ()))
    tn = (((0,), (0,)), ((), ()))

    def body(q_ref, kc_ref, kp_ref, vc_ref, vp_ref, lse_ref, do_ref, c_ref, gq_ref, gk_ref, bd_ref,
             dq_ref, dk_ref, dv_ref, dgq_ref, dgk_ref, ck_ref, cv_ref):
        n = pl.program_id(1)
        keep = jnp.where(n < nblk, 1.0, 0.0)
        has_prev = jnp.minimum(n, nblk - 1) > 0

        @pl.when(n == 0)
        def _():
            ck_ref[...] = jnp.zeros_like(ck_ref)
            cv_ref[...] = jnp.zeros_like(cv_ref)
            dgq_ref[...] = jnp.zeros_like(dgq_ref)
            dgk_ref[...] = jnp.zeros_like(dgk_ref)

        bdv = bd_ref[...]
        gqv = gq_ref[...]
        gkv = gk_ref[...]
        band = jnp.concatenate([_band_mask(has_prev)] * N_STACK, axis=0)
        lane = lax.broadcasted_iota(jnp.int32, (1, HW), 1)

        def residue(r):
            rr = _residue_rows(r, d)
            qn, qhat, qrstd = _head_norm(q_ref[rr, :], gqv, bdv)
            kn, khat, krstd = _head_norm(jnp.concatenate([kp_ref[rr, :], kc_ref[rr, :]], axis=0), gkv, bdv)
            knb = kn.astype(BF16)
            vb = jnp.concatenate([vp_ref[rr, :], vc_ref[rr, :]], axis=0).astype(BF16)
            qs = _stack_heads(qn, lane).astype(BF16)
            dos = _stack_heads(do_ref[rr, :], lane).astype(BF16)
            s = lax.dot_general(qs, knb, nt, preferred_element_type=F32) * (HEAD_DIM ** -0.5)
            prob = jnp.where(band, jnp.exp(s - _head_columns(lse_ref[rr, :], lane)), 0.0)
            dp = lax.dot_general(dos, vb, nt, preferred_element_type=F32)
            ds = (prob * (dp + _head_columns(c_ref[rr, :], lane)) * (HEAD_DIM ** -0.5)).astype(BF16)
            dqn = _unstack_heads(jnp.dot(ds, knb, preferred_element_type=F32), lane)
            dkn = lax.dot_general(ds, qs, tn, preferred_element_type=F32)
            dvv = lax.dot_general(prob.astype(BF16), dos, tn, preferred_element_type=F32)

            dq_ref[rr, :] = _head_norm_bwd(dqn, gqv, qhat, qrstd, bdv)
            dk2 = _head_norm_bwd(dkn, gkv, khat, krstd, bdv)
            dgq_ref[...] += keep * jnp.sum(dqn * qhat, axis=0, keepdims=True)
            dgk_ref[...] += keep * jnp.sum(dkn * khat, axis=0, keepdims=True)
            dk_ref[rr, :] = ck_ref[rr, :] + keep * dk2[:BLK]
            dv_ref[rr, :] = cv_ref[rr, :] + keep * dvv[:BLK]
            ck_ref[rr, :] = dk2[BLK:]
            cv_ref[rr, :] = dvv[BLK:]

        _for_residues(d, residue)

    last = nblk - 1
    per = PW // HW
    cq, ck, cv = (OFF_Q + PW * g) // HW, (OFF_K + PW * g) // HW, (OFF_V + PW * g) // HW
    cur = lambda col: pl.BlockSpec((rows, HW), lambda h, n: (jnp.minimum(n, last), col + h))
    prv = lambda col: pl.BlockSpec((rows, HW), lambda h, n: (jnp.maximum(jnp.minimum(n, last) - 1, 0), col + h))
    cur3 = pl.BlockSpec((None, rows, HW), lambda h, n: (g, jnp.minimum(n, last), h))
    done = pl.BlockSpec((rows, HW), lambda h, n: (jnp.maximum(n - 1, 0), h))
    vec = pl.BlockSpec((1, HW), lambda h, n: (0, h))
    return pl.pallas_call(
        body, name=name, grid=(per, nblk + 1),
        in_specs=[cur(cq), cur(ck), prv(ck), cur(cv), prv(cv), cur(0), cur3, cur3, vec, vec,
                  pl.BlockSpec((HW, HW), lambda h, n: (0, 0))],
        out_specs=[cur(0), done, done, vec, vec],
        out_shape=[SDS((S, PW), F32)] * 3 + [SDS((1, PW), F32)] * 2,
        scratch_shapes=[pltpu.VMEM((rows, HW), F32), pltpu.VMEM((rows, HW), F32)],
        compiler_params=_cp(2),
    )(p, p, p, p, p, lse, do3, c3, gq, gk, bd)


def _mix_fwd(name, os, lses):
    S = os[0].shape[0]
    tm = min(512, S)

    def body(o0, o1, o2, l0, l1, l2, y_ref):
        o = [o0[...], o1[...], o2[...]]
        l = [l0[...], l1[...], l2[...]]
        m = jnp.maximum(jnp.maximum(l[0], l[1]), l[2])
        e = [jnp.exp(t - m) for t in l]
        inv = 1.0 / (e[0] + e[1] + e[2])
        for g in range(N_PATTERNS):
            y_ref[:, PW * g:PW * (g + 1)] = (o[g] * (e[g] * inv)).astype(y_ref.dtype)

    blk = pl.BlockSpec((tm, PW), lambda i: (i, 0))
    return pl.pallas_call(
        body, name=name, grid=(S // tm,),
        in_specs=[blk] * 6,
        out_specs=pl.BlockSpec((tm, C_WIDTH), lambda i: (i, 0)),
        out_shape=SDS((S, C_WIDTH), BF16),
        compiler_params=_cp(1),
    )(*os, *lses)


def _mix_bwd(name, os, lses, dycat, bd):
    S = os[0].shape[0]
    tm = min(512, S)
    c0 = (A_WIDTH + B_WIDTH) // PW

    def body(o0, o1, o2, l0, l1, l2, dy0_ref, dy1_ref, dy2_ref, bd_ref, do_ref, c_ref):
        bdv = bd_ref[...]
        o = [o0[...], o1[...], o2[...]]
        l = [l0[...], l1[...], l2[...]]
        dys = [dy0_ref[...], dy1_ref[...], dy2_ref[...]]
        m = jnp.maximum(jnp.maximum(l[0], l[1]), l[2])
        e = [jnp.exp(t - m) for t in l]
        inv = 1.0 / (e[0] + e[1] + e[2])
        alpha = [t * inv for t in e]
        da = [_seg_sum(dys[g] * o[g], bdv) for g in range(N_PATTERNS)]
        mean_da = alpha[0] * da[0] + alpha[1] * da[1] + alpha[2] * da[2]
        for g in range(N_PATTERNS):
            do_ref[g] = dys[g] * alpha[g]
            c_ref[g] = -alpha[g] * mean_da

    blk = pl.BlockSpec((tm, PW), lambda i: (i, 0))
    blk3 = pl.BlockSpec((N_PATTERNS, tm, PW), lambda i: (0, i, 0))
    dyspec = lambda g: pl.BlockSpec((tm, PW), lambda i: (i, c0 + g))
    return pl.pallas_call(
        body, name=name, grid=(S // tm,),
        in_specs=[blk] * 6 + [dyspec(0), dyspec(1), dyspec(2), pl.BlockSpec((PW, PW), lambda i: (0, 0))],
        out_specs=[blk3, blk3],
        out_shape=[SDS((N_PATTERNS, S, PW), F32)] * 2,
        compiler_params=_cp(1),
    )(*os, *lses, dycat, dycat, dycat, bd)


def _mesh_pos():
    x, y, c = lax.axis_index("x"), lax.axis_index("y"), lax.axis_index("c")
    chips = [(1 - x, y), (x, 1 - y), (1 - x, 1 - y)]
    chip_idx = [2 * cx + cy for cx, cy in chips]
    return x, y, c, 2 * x + y, chips, chip_idx


def _place_shard(name, w, layer, chip_arr, out_dtype, deps=()):
    _, R, C = w.shape
    tr = min(256, R)

    def body(chip_ref, w_ref, *rest):
        o_ref = rest[-1]
        o_ref[...] = w_ref[...].astype(o_ref.dtype)

    return pl.pallas_call(
        body, name=name,
        grid_spec=pltpu.PrefetchScalarGridSpec(
            num_scalar_prefetch=1, grid=(R // tr,),
            in_specs=[pl.BlockSpec((None, tr, C), lambda i, chip_ref: (layer, i, 0))] + [_hbm_spec()] * len(deps),
            out_specs=pl.BlockSpec((None, tr, C), lambda i, chip_ref: (chip_ref[0], i, 0))),
        out_shape=SDS((N_CHIPS, R, C), out_dtype),
        compiler_params=_cp(1),
    )(chip_arr, w, *deps)


HBM_SPEC = pl.BlockSpec(memory_space=pltpu.HBM)
SEM_SPEC = pl.BlockSpec(memory_space=pltpu.SEMAPHORE)
SPLIT_COPY = pltpu.SideEffectType.DATAFLOW_SIDE_EFFECTING
N_PEER_CHIPS = N_CHIPS - 1
TOKEN_SHAPE = SDS((8, 128), F32)
TOKEN_SPEC = pl.BlockSpec(memory_space=pltpu.VMEM)


def _in_hbm(a):
    return pltpu.with_memory_space_constraint(a, pltpu.HBM)


def _gather_start(name, bufs):
    T = len(bufs)

    def body(*refs):
        ins = refs[:T]
        send_sems, recv_sems = refs[T:2 * T], refs[2 * T:3 * T]
        token = refs[4 * T]
        x, y, c, me, chips, chip_idx = _mesh_pos()
        for t in range(T):
            hr = ins[t].shape[1] // 2
            mine = ins[t].at[me, pl.ds(c * hr, hr), :]
            for j in range(N_PEER_CHIPS):
                pltpu.make_async_remote_copy(src_ref=mine, dst_ref=mine, send_sem=send_sems[t].at[j],
                                             recv_sem=recv_sems[t].at[j], device_id=(*chips[j], c),
                                             device_id_type=MESH).start()
        token[...] = jnp.zeros_like(token)

    sems = [pltpu.SemaphoreType.DMA((N_PEER_CHIPS,))] * T
    out = pl.pallas_call(
        body, name=name,
        in_specs=[HBM_SPEC] * T,
        out_specs=[SEM_SPEC] * (2 * T) + [HBM_SPEC] * T + [TOKEN_SPEC],
        out_shape=sems + sems + [pltpu.HBM(b.shape, b.dtype) for b in bufs] + [TOKEN_SHAPE],
        input_output_aliases={t: 2 * T + t for t in range(T)},
        compiler_params=pltpu.CompilerParams(has_side_effects=SPLIT_COPY),
    )(*[_in_hbm(b) for b in bufs])
    return out[:T], out[T:2 * T], out[2 * T:3 * T], out[3 * T]


def _gather_wait(name, buf, send_sem, recv_sem, after):
    n_in = 3 if after is None else 4

    def body(*refs):
        buf_ref, ssem, rsem = refs[:3]
        x, y, c, me, chips, chip_idx = _mesh_pos()
        hr = buf_ref.shape[1] // 2
        mine = buf_ref.at[me, pl.ds(c * hr, hr), :]
        for j in range(N_PEER_CHIPS):
            got = buf_ref.at[chip_idx[j], pl.ds(c * hr, hr), :]
            cp = pltpu.make_async_remote_copy(src_ref=mine, dst_ref=got, send_sem=ssem.at[j], recv_sem=rsem.at[j],
                                              device_id=(*chips[j], c), device_id_type=MESH)
            cp.wait_send()
            cp.wait_recv()

    args = [buf, send_sem, recv_sem] + ([] if after is None else [after])
    return pl.pallas_call(
        body, name=name,
        in_specs=[HBM_SPEC, SEM_SPEC, SEM_SPEC] + [_hbm_spec()] * (n_in - 3),
        out_specs=HBM_SPEC,
        out_shape=pltpu.HBM(buf.shape, buf.dtype),
        input_output_aliases={0: 0},
        compiler_params=pltpu.CompilerParams(has_side_effects=SPLIT_COPY),
    )(*args)


def _forward_start(name, buf):
    def body(buf_ref, send_sems, recv_sems, buf_thru, token):
        x, y, c, me, chips, chip_idx = _mesh_pos()
        hr = buf_ref.shape[1] // 2
        for j in range(N_PEER_CHIPS):
            got = buf_ref.at[chip_idx[j], pl.ds(c * hr, hr), :]
            pltpu.make_async_remote_copy(src_ref=got, dst_ref=got, send_sem=send_sems.at[j], recv_sem=recv_sems.at[j],
                                         device_id=(x, y, 1 - c), device_id_type=MESH).start()
        token[...] = jnp.zeros_like(token)

    sems = pltpu.SemaphoreType.DMA((N_PEER_CHIPS,))
    return pl.pallas_call(
        body, name=name,
        in_specs=[HBM_SPEC],
        out_specs=[SEM_SPEC, SEM_SPEC, HBM_SPEC, TOKEN_SPEC],
        out_shape=[sems, sems, pltpu.HBM(buf.shape, buf.dtype), TOKEN_SHAPE],
        input_output_aliases={0: 2},
        compiler_params=pltpu.CompilerParams(has_side_effects=SPLIT_COPY),
    )(_in_hbm(buf))


def _forward_wait(name, buf, send_sems, recv_sems, after):
    n_in = 3 if after is None else 4

    def body(*refs):
        buf_ref, ssems, rsems = refs[:3]
        x, y, c, me, chips, chip_idx = _mesh_pos()
        hr = buf_ref.shape[1] // 2
        for j in range(N_PEER_CHIPS):
            sent = buf_ref.at[chip_idx[j], pl.ds(c * hr, hr), :]
            theirs = buf_ref.at[chip_idx[j], pl.ds((1 - c) * hr, hr), :]
            cp = pltpu.make_async_remote_copy(src_ref=sent, dst_ref=theirs, send_sem=ssems.at[j],
                                              recv_sem=rsems.at[j], device_id=(x, y, 1 - c), device_id_type=MESH)
            cp.wait_send()
            cp.wait_recv()

    args = [buf, send_sems, recv_sems] + ([] if after is None else [after])
    return pl.pallas_call(
        body, name=name,
        in_specs=[HBM_SPEC, SEM_SPEC, SEM_SPEC] + [_hbm_spec()] * (n_in - 3),
        out_specs=HBM_SPEC,
        out_shape=pltpu.HBM(buf.shape, buf.dtype),
        input_output_aliases={0: 0},
        compiler_params=pltpu.CompilerParams(has_side_effects=SPLIT_COPY),
    )(*args)


class _GatheredWeights:
    def __init__(self):
        self._order = []
        self._pending = {}
        self._forwarding = {}
        self._ready = {}
        self._tokens = []

    def start(self, keys, bufs):
        send_sems, recv_sems, thru, token = _gather_start(f"gather_start_{len(self._order)}", bufs)
        self._tokens.append(token)
        self._order.extend(keys)
        self._pending.update({k: (b, s, r) for k, b, s, r in zip(keys, thru, send_sems, recv_sems)})

    def _prefetch(self, key, after):
        if key in self._pending:
            buf, ssem, rsem = self._pending.pop(key)
            tag = f"{key[0]}_{key[1]}"
            buf = _gather_wait(f"gather_wait_{tag}", buf, ssem, rsem, after)
            ssems, rsems, buf, token = _forward_start(f"gather_fwd_start_{tag}", buf)
            self._forwarding[key] = (buf, ssems, rsems)
            self._tokens.append(token)

    def get(self, name, layer, after=None):
        key = (name, layer)
        if key not in self._ready:
            self._prefetch(key, after)
            buf, ssems, rsems = self._forwarding.pop(key)
            self._ready[key] = _forward_wait(f"gather_fwd_wait_{name}_{layer}", buf, ssems, rsems, after)
            nxt = self._order.index(key) + 1
            if nxt < len(self._order):
                self._prefetch(self._order[nxt], after)
        return self._ready[key]

    def deps(self):
        tokens, self._tokens = self._tokens, []
        return tokens


def _swap_copy(g_ref, land_ref, send_sem, recv_sem):
    x, y, c, _, _, _ = _mesh_pos()
    hr = g_ref.shape[1] // 2
    return pltpu.make_async_remote_copy(src_ref=g_ref.at[:, pl.ds((1 - c) * hr, hr), :], dst_ref=land_ref,
                                        send_sem=send_sem, recv_sem=recv_sem, device_id=(x, y, 1 - c),
                                        device_id_type=MESH)


def _swap_start(name, g):
    land_shape = (g.shape[0], g.shape[1] // 2, g.shape[2])

    def body(g_ref, land_ref, send_sem, recv_sem, land_thru, token):
        _swap_copy(g_ref, land_ref, send_sem, recv_sem).start()
        token[...] = jnp.zeros_like(token)

    return pl.pallas_call(
        body, name=name,
        in_specs=[HBM_SPEC, HBM_SPEC],
        out_specs=[SEM_SPEC, SEM_SPEC, HBM_SPEC, TOKEN_SPEC],
        out_shape=[pltpu.SemaphoreType.DMA(()), pltpu.SemaphoreType.DMA(()), pltpu.HBM(land_shape, g.dtype),
                   TOKEN_SHAPE],
        input_output_aliases={1: 2},
        compiler_params=pltpu.CompilerParams(has_side_effects=SPLIT_COPY),
    )(_in_hbm(g), _in_hbm(lax.empty(land_shape, g.dtype)))


def _swap_wait(name, g, land, send_sem, recv_sem, after):
    def body(g_ref, land_ref, send_sem, recv_sem, after_ref, land_out):
        cp = _swap_copy(g_ref, land_ref, send_sem, recv_sem)
        cp.wait_send()
        cp.wait_recv()

    return pl.pallas_call(
        body, name=name,
        in_specs=[HBM_SPEC, HBM_SPEC, SEM_SPEC, SEM_SPEC, _hbm_spec()],
        out_specs=HBM_SPEC,
        out_shape=pltpu.HBM(land.shape, land.dtype),
        input_output_aliases={1: 0},
        compiler_params=pltpu.CompilerParams(has_side_effects=SPLIT_COPY),
    )(_in_hbm(g), land, send_sem, recv_sem, after)


def _add_my_half(name, g, r, c_arr):
    ns, R, C = g.shape
    hr = R // 2
    tr = min(256, hr)
    nt = hr // tr

    def body(c_ref, g_ref, r_ref, o_ref, land_ref):
        t = (g_ref[...] + r_ref[...]).astype(o_ref.dtype)
        o_ref[...] = t
        land_ref[...] = t

    out = pl.BlockSpec((None, tr, C), lambda s, i, c_ref: (s, i, 0))
    return pl.pallas_call(
        body, name=name,
        grid_spec=pltpu.PrefetchScalarGridSpec(
            num_scalar_prefetch=1, grid=(ns, nt),
            in_specs=[pl.BlockSpec((None, tr, C), lambda s, i, c_ref: (s, c_ref[0] * nt + i, 0)), out],
            out_specs=[out, out]),
        out_shape=[SDS((ns, hr, C), BF16)] * 2,
        compiler_params=_cp(2),
    )(c_arr, g, r)


def _exchange_start(name, part, land):
    def body(part_ref, land_ref, send_sems, recv_sems, land_thru, token):
        x, y, c, me, chips, chip_idx = _mesh_pos()
        for j in range(N_PEER_CHIPS):
            pltpu.make_async_remote_copy(src_ref=part_ref.at[chip_idx[j]], dst_ref=land_ref.at[me],
                                         send_sem=send_sems.at[j], recv_sem=recv_sems.at[j],
                                         device_id=(*chips[j], c), device_id_type=MESH).start()
        token[...] = jnp.zeros_like(token)

    sems = pltpu.SemaphoreType.DMA((N_PEER_CHIPS,))
    return pl.pallas_call(
        body, name=name,
        in_specs=[HBM_SPEC, HBM_SPEC],
        out_specs=[SEM_SPEC, SEM_SPEC, HBM_SPEC, TOKEN_SPEC],
        out_shape=[sems, sems, pltpu.HBM(land.shape, land.dtype), TOKEN_SHAPE],
        input_output_aliases={1: 2},
        compiler_params=pltpu.CompilerParams(has_side_effects=SPLIT_COPY),
    )(_in_hbm(part), _in_hbm(land))


def _exchange_wait(name, part, land, send_sems, recv_sems, after):
    def body(part_ref, land_ref, send_sems, recv_sems, after_ref, land_out):
        x, y, c, me, chips, chip_idx = _mesh_pos()
        for j in range(N_PEER_CHIPS):
            cp = pltpu.make_async_remote_copy(src_ref=part_ref.at[chip_idx[j]], dst_ref=land_ref.at[chip_idx[j]],
                                              send_sem=send_sems.at[j], recv_sem=recv_sems.at[j],
                                              device_id=(*chips[j], c), device_id_type=MESH)
            cp.wait_send()
            cp.wait_recv()

    return pl.pallas_call(
        body, name=name,
        in_specs=[HBM_SPEC, HBM_SPEC, SEM_SPEC, SEM_SPEC, _hbm_spec()],
        out_specs=HBM_SPEC,
        out_shape=pltpu.HBM(land.shape, land.dtype),
        input_output_aliases={1: 0},
        compiler_params=pltpu.CompilerParams(has_side_effects=SPLIT_COPY),
    )(_in_hbm(part), land, send_sems, recv_sems, after)


class _GradReducer:
    def __init__(self, c_arr):
        self._c_arr = c_arr
        self._swapping = []
        self._exchanging = {}
        self._tokens = []

    def begin(self, name, layer, g):
        tag = f"{name}_{layer}"
        ssem, rsem, land, token = _swap_start(f"rs_swap_start_{tag}", g)
        self._swapping.append((name, layer, g, ssem, rsem, land))
        self._tokens.append(token)

    def advance(self, after):
        for name, layer, g, ssem, rsem, land in self._swapping:
            tag = f"{name}_{layer}"
            theirs = _swap_wait(f"rs_swap_wait_{tag}", g, land, ssem, rsem, after)
            part, own = _add_my_half(f"rs_add_{tag}", g, theirs, self._c_arr)
            ssems, rsems, land2, token = _exchange_start(f"rs_xchg_start_{tag}", part, own)
            self._exchanging[(name, layer)] = (part, ssems, rsems, land2)
            self._tokens.append(token)
        self._swapping = []

    def deps(self):
        tokens, self._tokens = self._tokens, []
        return tokens

    def finish(self, names, n_layers, after):
        bufs = []
        for name in names:
            buf = None
            for layer in range(n_layers):
                part, ssems, rsems, land = self._exchanging.pop((name, layer))
                tag = f"{name}_{layer}"
                landed = _exchange_wait(f"rs_xchg_wait_{tag}", part, land, ssems, rsems, after)
                buf = _sum_chips(f"rs_sum_{tag}", landed, self._c_arr, layer, n_layers, buf)
            bufs.append(buf)
        return dict(zip(names, _join_halves(f"rs_join_{names[0]}", bufs)))


def _sum_chips(name, r, c_arr, layer, n_layers, prev):
    ns, H, C = r.shape
    tr = min(256, H)
    nt = H // tr

    def body(c_ref, r_ref, *rest):
        o_ref = rest[-1]
        o_ref[...] = ((r_ref[0].astype(F32) + r_ref[1].astype(F32)) + r_ref[2].astype(F32)) + r_ref[3].astype(F32)

    in_specs = [pl.BlockSpec((ns, tr, C), lambda i, c_ref: (0, i, 0))]
    args = [c_arr, r]
    aliases = {}
    if prev is not None:
        in_specs.append(_hbm_spec())
        args.append(prev)
        aliases = {2: 0}
    return pl.pallas_call(
        body, name=name,
        grid_spec=pltpu.PrefetchScalarGridSpec(
            num_scalar_prefetch=1, grid=(nt,), in_specs=in_specs,
            out_specs=pl.BlockSpec((None, tr, C), lambda i, c_ref: (layer, c_ref[0] * nt + i, 0))),
        out_shape=SDS((n_layers, 2 * H, C), F32),
        input_output_aliases=aliases,
        compiler_params=_cp(1),
    )(*args)


def _join_halves(name, bufs):
    T = len(bufs)

    def body(*refs):
        outs = refs[T:2 * T]
        send_sems, recv_sems = refs[2 * T:]
        x, y, c, _, _, _ = _mesh_pos()
        cps = []
        for t in range(T):
            hr = outs[t].shape[1] // 2
            mine = outs[t].at[:, pl.ds(c * hr, hr), :]
            cp = pltpu.make_async_remote_copy(src_ref=mine, dst_ref=mine, send_sem=send_sems.at[t],
                                              recv_sem=recv_sems.at[t], device_id=(x, y, 1 - c), device_id_type=MESH)
            cp.start()
            cps.append(cp)
        for t in range(T):
            hr = outs[t].shape[1] // 2
            theirs = outs[t].at[:, pl.ds((1 - c) * hr, hr), :]
            pltpu.make_async_remote_copy(src_ref=theirs, dst_ref=theirs, send_sem=send_sems.at[t],
                                         recv_sem=recv_sems.at[t], device_id=(x, y, 1 - c),
                                         device_id_type=MESH).wait_recv()
        for cp in cps:
            cp.wait_send()

    return pl.pallas_call(
        body, name=name,
        in_specs=[_hbm_spec()] * T, out_specs=[_hbm_spec()] * T,
        out_shape=[SDS(b.shape, b.dtype) for b in bufs],
        input_output_aliases={t: t for t in range(T)},
        scratch_shapes=[pltpu.SemaphoreType.DMA((T,)), pltpu.SemaphoreType.DMA((T,))],
    )(*bufs)


def _small_copy(k, buf_ref, land_ref, send_sems, recv_sems):
    x, y, c = lax.axis_index("x"), lax.axis_index("y"), lax.axis_index("c")
    me = 4 * x + 2 * y + c
    peer = (x ^ ((k >> 2) & 1), y ^ ((k >> 1) & 1), c ^ (k & 1))
    cp = pltpu.make_async_remote_copy(src_ref=buf_ref, dst_ref=land_ref.at[me], send_sem=send_sems.at[k - 1],
                                      recv_sem=recv_sems.at[k - 1], device_id=peer, device_id_type=MESH)
    return me, peer, cp


def _small_start(buf, deps):
    land = jnp.broadcast_to(buf[None], (N_DEV,) + buf.shape)
    n_dep = len(deps)

    def body(buf_ref, land_ref, *rest):
        send_sems, recv_sems, _, token = rest[n_dep:]
        for k in range(1, N_DEV):
            _small_copy(k, buf_ref, land_ref, send_sems, recv_sems)[2].start()
        token[...] = jnp.zeros_like(token)

    sems = pltpu.SemaphoreType.DMA((N_DEV - 1,))
    return pl.pallas_call(
        body, name="small_gather_start",
        in_specs=[HBM_SPEC, HBM_SPEC] + [_hbm_spec()] * n_dep,
        out_specs=[SEM_SPEC, SEM_SPEC, HBM_SPEC, TOKEN_SPEC],
        out_shape=[sems, sems, pltpu.HBM(land.shape, land.dtype), TOKEN_SHAPE],
        input_output_aliases={1: 2},
        compiler_params=pltpu.CompilerParams(has_side_effects=SPLIT_COPY),
    )(_in_hbm(buf), _in_hbm(land), *deps)


def _small_wait(buf, land, send_sems, recv_sems, after):
    def body(buf_ref, land_ref, send_sems, recv_sems, after_ref, land_out):
        for k in range(1, N_DEV):
            me, peer, cp = _small_copy(k, buf_ref, land_ref, send_sems, recv_sems)
            cp.wait_send()
            got = land_ref.at[me ^ k]
            pltpu.make_async_remote_copy(src_ref=got, dst_ref=got, send_sem=send_sems.at[k - 1],
                                         recv_sem=recv_sems.at[k - 1], device_id=peer,
                                         device_id_type=MESH).wait_recv()

    return pl.pallas_call(
        body, name="small_gather_wait",
        in_specs=[HBM_SPEC, HBM_SPEC, SEM_SPEC, SEM_SPEC, _hbm_spec()],
        out_specs=HBM_SPEC,
        out_shape=pltpu.HBM(land.shape, land.dtype),
        input_output_aliases={1: 0},
        compiler_params=pltpu.CompilerParams(has_side_effects=SPLIT_COPY),
    )(_in_hbm(buf), land, send_sems, recv_sems, after)


def _sum_devices(land):
    n, R, C = land.shape

    def body(land_ref, out_ref):
        acc = land_ref[0]
        for d in range(1, n):
            acc = acc + land_ref[d]
        out_ref[...] = acc

    return pl.pallas_call(
        body, name="small_sum",
        in_specs=[pl.BlockSpec(memory_space=pltpu.VMEM)],
        out_specs=pl.BlockSpec(memory_space=pltpu.VMEM),
        out_shape=SDS((R, C), land.dtype),
        compiler_params=pltpu.CompilerParams(vmem_limit_bytes=V7X_VMEM_LIMIT),
    )(land)


def _deinterleave(t, d):
    if d == 1:
        return t
    S, W = t.shape
    return t.reshape(S // d, d, W).transpose(1, 0, 2).reshape(S, W)


def _interleave(t, d):
    if d == 1:
        return t
    S, W = t.shape
    return t.reshape(d, S // d, W).transpose(1, 0, 2).reshape(S, W)


def _to_patterns(t, off):
    return jnp.stack([_deinterleave(t[:, off + PW * g:off + PW * (g + 1)], PATTERN_DILATION[g])
                      for g in range(N_PATTERNS)])


def _from_patterns(t3):
    return jnp.stack([_interleave(t3[g], PATTERN_DILATION[g]) for g in range(N_PATTERNS)])


def _pack_rows(vectors):
    flat = jnp.concatenate([v.reshape(-1) for v in vectors])
    n = flat.shape[0]
    padded = -(-n // 1024) * 1024
    return jnp.pad(flat, (0, padded - n)).reshape(padded // 128, 128)


def _unpack_rows(buf, shapes):
    flat = buf.reshape(-1)
    out, off = [], 0
    for s in shapes:
        n = 1
        for dim in s:
            n *= dim
        out.append(flat[off:off + n].reshape(s))
        off += n
    return out


def _layer_forward(l, x, prm, wg):
    S, D = x.shape
    w_in = wg.get("w_in", l, x)
    p, h = _norm_matmul(f"in_proj_{l}", x, prm["attn_norm"][l], w_in, F32, deps=wg.deps())
    y_a = _sgu_fwd(f"sgu_fwd_{l}", p, prm["sgu_wt"][l], prm["sgu_bb"][l])
    y_b = _conv_fwd(f"conv_fwd_{l}", p, prm["conv_w"][l])
    os, lses = [], []
    for g in range(N_PATTERNS):
        o_g, lse_g = _attn_fwd(f"attn_fwd_{l}_{g}", p, g, prm["q_gain"][l], prm["k_gain"][l], prm["bd"])
        os.append(o_g)
        lses.append(lse_g)
    y_c = _mix_fwd(f"mix_fwd_{l}", os, lses)
    ycat = jnp.concatenate([y_a, y_b, y_c], axis=1)
    tmb, tnb = min(1024, S), min(1024, D)
    w_out = wg.get("w_out", l, ycat)
    rq = w_out.shape[1]
    x1 = _matmul(
        f"out_proj_{l}", ycat, w_out, (S, D), F32, grid=(S // tmb, D // tnb, N_CHIPS),
        a_spec=pl.BlockSpec((tmb, rq), lambda i, j, k: (i, k)),
        b_spec=pl.BlockSpec((None, rq, tnb), lambda i, j, k: (k, 0, j)),
        o_spec=pl.BlockSpec((tmb, tnb), lambda i, j, k: (i, j)),
        contract=(1, 0), acc_shape=(tmb, tnb),
        extras=(x,), extra_specs=(pl.BlockSpec((tmb, tnb), lambda i, j, k: (i, j)),),
        epi=lambda r, res: r + res, deps=wg.deps())
    w_mlp_in = wg.get("w_mlp_in", l, x1)
    a, h2 = _norm_matmul(f"mlp_in_{l}", x1, prm["mlp_norm"][l], w_mlp_in, BF16, deps=wg.deps())
    w_mlp_out = wg.get("w_mlp_out", l, a)
    dff4 = w_mlp_out.shape[1]
    tk = min(1024, dff4)
    kpc = dff4 // tk
    x2 = _matmul(
        f"mlp_out_{l}", a, w_mlp_out, (S, D), F32, grid=(S // tmb, D // tnb, N_CHIPS * kpc),
        a_spec=pl.BlockSpec((tmb, tk), lambda i, j, k: (i, k)),
        b_spec=pl.BlockSpec((None, tk, tnb), lambda i, j, k: (k // kpc, k % kpc, j)),
        o_spec=pl.BlockSpec((tmb, tnb), lambda i, j, k: (i, j)),
        contract=(1, 0), acc_shape=(tmb, tnb), a_pre=_relu2_bf16,
        extras=(x1,), extra_specs=(pl.BlockSpec((tmb, tnb), lambda i, j, k: (i, j)),),
        epi=lambda r, res: r + res, deps=wg.deps())
    saved = dict(x=x, p=p, h=h, os=os, lses=lses, ycat=ycat, x1=x1, a=a, h2=h2)
    return x2, saved


def _layer_backward(l, dx2, dx2b, sv, prm, wg, sink):
    S, D = dx2.shape
    w_in, w_out = wg.get("w_in", l), wg.get("w_out", l)
    w_mlp_in, w_mlp_out = wg.get("w_mlp_in", l), wg.get("w_mlp_out", l)
    dff4 = w_mlp_in.shape[-1]
    dff = N_CHIPS * dff4
    tm = min(512, S)
    tk = min(1024, S)
    nks = S // tk

    tmb, tnb = min(1024, S), min(1024, D)
    da = _matmul(
        f"mlp_out_bwd_{l}", dx2b, w_mlp_out, (S, dff), BF16, grid=(S // tmb, N_CHIPS, 1),
        a_spec=pl.BlockSpec((tmb, D), lambda i, j, k: (i, 0)),
        b_spec=pl.BlockSpec((None, dff4, D), lambda i, j, k: (j, 0, 0)),
        o_spec=pl.BlockSpec((tmb, dff4), lambda i, j, k: (i, j)),
        contract=(1, 1), acc_shape=(tmb, dff4),
        extras=(sv["a"],), extra_specs=(pl.BlockSpec((tmb, dff4), lambda i, j, k: (i, j)),),
        epi=lambda r, act: r * (2.0 * jnp.maximum(act.astype(F32), 0.0)), deps=sink.deps())
    tmw = min(1024, dff4)
    mpc = dff4 // tmw
    g_w2 = _matmul(
        f"mlp_out_dw_{l}", sv["a"], dx2b, (N_CHIPS, dff4, D), F32, grid=(N_CHIPS * mpc, 1, nks),
        a_spec=pl.BlockSpec((tk, tmw), lambda i, j, k: (k, i)),
        b_spec=pl.BlockSpec((tk, D), lambda i, j, k: (k, 0)),
        o_spec=pl.BlockSpec((None, tmw, D), lambda i, j, k: (i // mpc, i % mpc, 0)),
        contract=(0, 0), acc_shape=(tmw, D), a_pre=_relu2_bf16)
    sink.begin("w_mlp_out", l, g_w2)
    dh2 = _matmul(
        f"mlp_in_bwd_{l}", da, w_mlp_in, (S, D), F32, grid=(S // tmb, D // tnb, N_CHIPS),
        a_spec=pl.BlockSpec((tmb, dff4), lambda i, j, k: (i, k)),
        b_spec=pl.BlockSpec((None, tnb, dff4), lambda i, j, k: (k, j, 0)),
        o_spec=pl.BlockSpec((tmb, tnb), lambda i, j, k: (i, j)),
        contract=(1, 1), acc_shape=(tmb, tnb), deps=sink.deps())
    sink.advance(dh2)
    tmd = min(1024, D)
    g_w1 = _matmul(
        f"mlp_in_dw_{l}", sv["h2"], da, (N_CHIPS, D, dff4), F32, grid=(N_CHIPS, D // tmd, nks),
        a_spec=pl.BlockSpec((tk, tmd), lambda i, j, k: (k, j)),
        b_spec=pl.BlockSpec((tk, dff4), lambda i, j, k: (k, i)),
        o_spec=pl.BlockSpec((None, tmd, dff4), lambda i, j, k: (i, j, 0)),
        contract=(0, 0), acc_shape=(tmd, dff4))
    sink.begin("w_mlp_in", l, g_w1)
    dx1, dx1b, g_mlp_norm = _rmsnorm_bwd(f"mlp_norm_bwd_{l}", dh2, sv["x1"], prm["mlp_norm"][l], dx2,
                                         deps=sink.deps())

    rq = w_out.shape[1]
    dycat = _matmul(
        f"out_proj_bwd_{l}", dx1b, w_out, (S, N_CHIPS * rq), F32, grid=(S // tmb, N_CHIPS, 1),
        a_spec=pl.BlockSpec((tmb, D), lambda i, j, k: (i, 0)),
        b_spec=pl.BlockSpec((None, rq, D), lambda i, j, k: (j, 0, 0)),
        o_spec=pl.BlockSpec((tmb, rq), lambda i, j, k: (i, j)),
        contract=(1, 1), acc_shape=(tmb, rq))
    sink.advance(dycat)
    g_wout = _matmul(
        f"out_proj_dw_{l}", sv["ycat"], dx1b, (N_CHIPS, rq, D), F32, grid=(N_CHIPS, 1, nks),
        a_spec=pl.BlockSpec((tk, rq), lambda i, j, k: (k, i)),
        b_spec=pl.BlockSpec((tk, D), lambda i, j, k: (k, 0)),
        o_spec=pl.BlockSpec((None, rq, D), lambda i, j, k: (i, 0, 0)),
        contract=(0, 0), acc_shape=(rq, D))
    sink.begin("w_out", l, g_wout)

    p = sv["p"]
    du, dv_a, g_sgu_w, db_lanes = _sgu_bwd(f"sgu_bwd_{l}", p, dycat, prm["sgu_wt"][l], prm["sgu_wtt"][l],
                                           prm["sgu_bb"][l])
    g_sgu_b = db_lanes[:, :A_HEADS].T
    db, dc, dxb, g_conv = _conv_bwd(f"conv_bwd_{l}", p, dycat, prm["conv_w"][l])
    do3, c3 = _mix_bwd(f"mix_bwd_{l}", sv["os"], sv["lses"], dycat, prm["bd"])
    dqs, dks, dvs, dgqs, dgks = [], [], [], [], []
    for g in range(N_PATTERNS):
        dq, dk, dv, dgq, dgk = _attn_bwd(f"attn_bwd_{l}_{g}", p, g, sv["lses"][g], do3, c3,
                                         prm["q_gain"][l], prm["k_gain"][l], prm["bd"])
        dqs.append(dq)
        dks.append(dk)
        dvs.append(dv)
        dgqs.append(dgq)
        dgks.append(dgk)
    g_q = jnp.concatenate(dgqs, axis=1).reshape(N_PATTERNS * PW // HEAD_DIM, HEAD_DIM).sum(axis=0)
    g_k = jnp.concatenate(dgks, axis=1).reshape(N_PATTERNS * PW // HEAD_DIM, HEAD_DIM).sum(axis=0)
    dp = jnp.concatenate([du, dv_a, db, dc, dxb] + [t.astype(BF16) for t in dqs + dks + dvs], axis=1)

    ns_in = w_in.shape[-1]
    g_win = _matmul(
        f"in_proj_dw_{l}", sv["h"], dp, (N_CHIPS, D, ns_in), F32, grid=(N_CHIPS, D // tmd, nks),
        a_spec=pl.BlockSpec((tk, tmd), lambda i, j, k: (k, j)),
        b_spec=pl.BlockSpec((tk, ns_in), lambda i, j, k: (k, i)),
        o_spec=pl.BlockSpec((None, tmd, ns_in), lambda i, j, k: (i, j, 0)),
        contract=(0, 0), acc_shape=(tmd, ns_in))
    sink.begin("w_in", l, g_win)
    dh = _matmul(
        f"in_proj_bwd_{l}", dp, w_in, (S, D), F32, grid=(S // tmb, D // tnb, N_CHIPS),
        a_spec=pl.BlockSpec((tmb, ns_in), lambda i, j, k: (i, k)),
        b_spec=pl.BlockSpec((None, tnb, ns_in), lambda i, j, k: (k, j, 0)),
        o_spec=pl.BlockSpec((tmb, tnb), lambda i, j, k: (i, j)),
        contract=(1, 1), acc_shape=(tmb, tnb), deps=sink.deps())
    sink.advance(dh)
    dx0, dx0b, g_attn_norm = _rmsnorm_bwd(f"attn_norm_bwd_{l}", dh, sv["x"], prm["attn_norm"][l], dx1,
                                          deps=sink.deps())

    big = dict(w_in=g_win, w_out=g_wout, w_mlp_in=g_w1, w_mlp_out=g_w2)
    small = dict(attn_norm=g_attn_norm.reshape(-1), sgu_w=g_sgu_w, sgu_b=g_sgu_b, conv_w=g_conv,
                 q_norm=g_q, k_norm=g_k, mlp_norm=g_mlp_norm.reshape(-1))
    return dx0, dx0b, big, small


BIG = ("w_in", "w_out", "w_mlp_in", "w_mlp_out")
SMALL_REPLICATED = ("attn_norm", "sgu_w", "sgu_b", "q_norm", "k_norm", "mlp_norm")


def _local_step(x, target, prm, wg, n_layers, sink):
    saved = []
    h = x
    for l in range(n_layers):
        h, sv = _layer_forward(l, h, prm, wg)
        saved.append(sv)
    dy, dyb, colsq = _loss_kernel(h, target)
    loss = 0.5 * jnp.sum(colsq) / x.shape[1]
    bigs, smalls = [None] * n_layers, [None] * n_layers
    for l in reversed(range(n_layers)):
        dy, dyb, bigs[l], smalls[l] = _layer_backward(l, dy, dyb, saved[l], prm, wg, sink)
    return loss, dy, bigs, smalls


def _prepare_params(attn_norm, sgu_w, sgu_b, conv_full, q_norm, k_norm, mlp_norm):
    n_layers = attn_norm.shape[0]
    tri = jnp.tril(sgu_w)
    idx = jnp.arange(PW)
    bd = (idx[:, None] // HEAD_DIM == idx[None, :] // HEAD_DIM).astype(BF16)
    return dict(
        attn_norm=[attn_norm[l][None, :] for l in range(n_layers)],
        mlp_norm=[mlp_norm[l][None, :] for l in range(n_layers)],
        sgu_wt=[tri[l].astype(BF16) for l in range(n_layers)],
        sgu_wtt=[tri[l].transpose(0, 2, 1).astype(BF16) for l in range(n_layers)],
        sgu_bb=[jnp.repeat(sgu_b[l].T, HEAD_DIM, axis=1) for l in range(n_layers)],
        conv_w=[conv_full[l] for l in range(n_layers)],
        q_gain=[jnp.tile(q_norm[l], PW // HEAD_DIM)[None, :] for l in range(n_layers)],
        k_gain=[jnp.tile(k_norm[l], PW // HEAD_DIM)[None, :] for l in range(n_layers)],
        bd=bd,
    )


def kernel(x, attn_norm, w_in, sgu_w, sgu_b, conv_w, q_norm, k_norm, w_out, mlp_norm, w_mlp_in, w_mlp_out, loss_target, m_attn_norm, m_w_in, m_sgu_w, m_sgu_b, m_conv_w, m_q_norm, m_k_norm, m_w_out, m_mlp_norm, m_w_mlp_in, m_w_mlp_out, v_attn_norm, v_w_in, v_sgu_w, v_sgu_b, v_conv_w, v_q_norm, v_k_norm, v_w_out, v_mlp_norm, v_w_mlp_in, v_w_mlp_out):
    n_layers = attn_norm.shape[0]
    weights = dict(attn_norm=attn_norm, w_in=w_in, sgu_w=sgu_w, sgu_b=sgu_b, conv_w=conv_w, q_norm=q_norm,
                   k_norm=k_norm, w_out=w_out, mlp_norm=mlp_norm, w_mlp_in=w_mlp_in, w_mlp_out=w_mlp_out)
    mom_m = dict(attn_norm=m_attn_norm, w_in=m_w_in, sgu_w=m_sgu_w, sgu_b=m_sgu_b, conv_w=m_conv_w,
                 q_norm=m_q_norm, k_norm=m_k_norm, w_out=m_w_out, mlp_norm=m_mlp_norm, w_mlp_in=m_w_mlp_in,
                 w_mlp_out=m_w_mlp_out)
    mom_v = dict(attn_norm=v_attn_norm, w_in=v_w_in, sgu_w=v_sgu_w, sgu_b=v_sgu_b, conv_w=v_conv_w,
                 q_norm=v_q_norm, k_norm=v_k_norm, w_out=v_w_out, mlp_norm=v_mlp_norm, w_mlp_in=v_w_mlp_in,
                 w_mlp_out=v_w_mlp_out)
    order = ("attn_norm", "w_in", "sgu_w", "sgu_b", "conv_w", "q_norm", "k_norm", "w_out", "mlp_norm",
             "w_mlp_in", "w_mlp_out")
    chip = 2 * lax.axis_index("x") + lax.axis_index("y")
    c_arr = lax.axis_index("c").astype(jnp.int32).reshape(1)

    conv_cols = conv_w.shape[-1]
    chip_arr = chip.astype(jnp.int32).reshape(1)
    conv_pack = jnp.pad(conv_w.reshape(-1), (0, 2048 - conv_w.size)).reshape(1, 16, 128)
    wg = _GatheredWeights()
    wg.start([("conv_w", 0), ("w_in", 0)],
             [_place_shard("place_conv_w", conv_pack, 0, chip_arr, F32),
              _place_shard("place_w_in_0", weights["w_in"], 0, chip_arr, BF16)])
    keys = [(n, l) for l in range(n_layers) for n in BIG if (n, l) != ("w_in", 0)]
    first = wg.deps()
    wg.start(keys, [_place_shard(f"place_{n}_{l}", weights[n], l, chip_arr, BF16, deps=first) for n, l in keys])
    conv_full = wg.get("conv_w", 0, wg.deps()[-1]).reshape(N_CHIPS, 2048)[:, :conv_w.size].reshape(N_CHIPS, n_layers, 3, conv_cols)
    conv_full = conv_full.transpose(1, 2, 0, 3).reshape(n_layers, 3, N_CHIPS * conv_cols)
    prm = _prepare_params(attn_norm, sgu_w, sgu_b, conv_full, q_norm, k_norm, mlp_norm)

    sink = _GradReducer(c_arr)
    loss_local, grad_x, _, smalls = _local_step(x[0], loss_target[0], prm, wg, n_layers, sink)
    loss = lax.psum(loss_local, ("x", "y", "c"))

    small_names = SMALL_REPLICATED + ("conv_w",)
    small_shapes = [(n_layers,) + tuple(smalls[0][n].shape) for n in small_names]
    packed = _pack_rows([jnp.stack([smalls[l][n] for l in range(n_layers)]) for n in small_names])
    small_send, small_recv, small_land, small_token = _small_start(packed, sink.deps())

    grads, delta, new_m, new_v = {}, {}, {}, {}

    def update(names, after):
        joined = sink.finish(names, n_layers, after)
        for n in names:
            shp = weights[n].shape
            two_d = (shp[0] * shp[1], shp[2])
            d, nm, nv, g = _adamw(f"adamw_{n}", weights[n].reshape(two_d), joined[n].reshape(two_d),
                                  mom_m[n].reshape(two_d), mom_v[n].reshape(two_d))
            grads[n], delta[n], new_m[n], new_v[n] = g.reshape(shp), d.reshape(shp), nm.reshape(shp), nv.reshape(shp)

    update(("w_mlp_out", "w_mlp_in", "w_out"), small_token)
    update(("w_in",), delta["w_out"])
    small_land = _small_wait(packed, small_land, small_send, small_recv, delta["w_in"])
    grads.update(zip(small_names, _unpack_rows(_sum_devices(small_land), small_shapes)))
    grads["conv_w"] = lax.dynamic_slice_in_dim(grads["conv_w"], chip * conv_cols, conv_cols, axis=2)
    smalls_all = SMALL_REPLICATED + ("conv_w",)
    shapes = [weights[n].shape for n in smalls_all]
    d, nm, nv, _ = _adamw("adamw_small",
                       _pack_rows([weights[n] for n in smalls_all]), _pack_rows([grads[n] for n in smalls_all]),
                       _pack_rows([mom_m[n] for n in smalls_all]), _pack_rows([mom_v[n] for n in smalls_all]))
    for n, dd, mm, vv in zip(smalls_all, _unpack_rows(d, shapes), _unpack_rows(nm, shapes), _unpack_rows(nv, shapes)):
        delta[n], new_m[n], new_v[n] = dd, mm, vv

    return (loss, grad_x[None], *[grads[n] for n in order], *[delta[n] for n in order],
            *[new_m[n] for n in order], *[new_v[n] for n in order])
```

```python
import jax
import jax.numpy as jnp
from jax import lax
from jax.experimental import pallas as pl
from jax.experimental.pallas import tpu as pltpu

F32 = jnp.float32
BF16 = jnp.bfloat16
SDS = jax.ShapeDtypeStruct

EPS = 1e-6
HEAD_DIM = 64
A_HEADS = 8
A_WIDTH = 512
CHUNK = 128
B_WIDTH = 768
C_WIDTH = 768
N_PATTERNS = 3
PATTERN_DILATION = (1, 4, 16)
PW = 256
D_IN_PROJ = 5632
OFF_AU, OFF_AV, OFF_BB, OFF_BC, OFF_BX, OFF_Q, OFF_K, OFF_V = 0, 512, 1024, 1792, 2560, 3328, 4096, 4864
N_CHIPS = 4
N_DEV = 8
BLK = 128

ADAM_LR, ADAM_B1, ADAM_B2, ADAM_EPS, ADAM_WD, ADAM_STEP = 0.001, 0.9, 0.999, 1e-08, 0.01, 10

V7X_VMEM_LIMIT = 56 * 1024 * 1024
MESH = pl.DeviceIdType.MESH
NEG = -1e30


def _cp(n_axes):
    return pltpu.CompilerParams(dimension_semantics=("arbitrary",) * n_axes, vmem_limit_bytes=V7X_VMEM_LIMIT)


def _hbm_spec():
    return pl.BlockSpec(memory_space=pl.ANY)


def _norm_matmul(name, x, g, wg, out_dtype, deps=()):
    S, D = x.shape
    ns, _, Ns = wg.shape
    tm = min(512, S)
    n_dep = len(deps)

    def body(x_ref, g_ref, w_ref, *rest):
        o_ref, h_ref, hs_ref = rest[n_dep:]
        @pl.when(pl.program_id(1) == 0)
        def _():
            xv = x_ref[...]
            y = xv * lax.rsqrt(jnp.mean(xv * xv, axis=-1, keepdims=True) + EPS) * g_ref[...]
            hb = y.astype(BF16)
            hs_ref[...] = hb
            h_ref[...] = hb
        o_ref[...] = jnp.dot(hs_ref[...], w_ref[...], preferred_element_type=F32).astype(o_ref.dtype)

    return pl.pallas_call(
        body, name=name, grid=(S // tm, ns),
        in_specs=[pl.BlockSpec((tm, D), lambda i, s: (i, 0)),
                  pl.BlockSpec((1, D), lambda i, s: (0, 0)),
                  pl.BlockSpec((None, D, Ns), lambda i, s: (s, 0, 0))] + [_hbm_spec()] * n_dep,
        out_specs=[pl.BlockSpec((tm, Ns), lambda i, s: (i, s)),
                   pl.BlockSpec((tm, D), lambda i, s: (i, 0))],
        out_shape=[SDS((S, ns * Ns), out_dtype), SDS((S, D), BF16)],
        scratch_shapes=[pltpu.VMEM((tm, D), BF16)],
        compiler_params=_cp(2),
    )(x, g, wg, *deps)


def _matmul(name, a, b, out_shape, out_dtype, *, grid, a_spec, b_spec, o_spec, contract, acc_shape,
            extras=(), extra_specs=(), a_pre=None, epi=None, deps=()):
    nk = grid[2]
    n_ex = len(extras)
    n_dep = len(deps)
    dims = (((contract[0],), (contract[1],)), ((), ()))

    def product(a_ref, b_ref):
        av = a_ref[...]
        if a_pre is not None:
            av = a_pre(av)
        return lax.dot_general(av, b_ref[...], dims, preferred_element_type=F32)

    def finish(r, ex, o_ref):
        if epi is not None:
            r = epi(r, *[e[...] for e in ex])
        o_ref[...] = r.astype(o_ref.dtype)

    def body_single(a_ref, b_ref, *rest):
        finish(product(a_ref, b_ref), rest[:n_ex], rest[n_ex + n_dep])

    def body(a_ref, b_ref, *rest):
        ex = rest[:n_ex]
        o_ref = rest[n_ex + n_dep]
        acc_ref = rest[n_ex + n_dep + 1]
        k = pl.program_id(2)

        @pl.when(k == 0)
        def _():
            acc_ref[...] = product(a_ref, b_ref)

        @pl.when((k > 0) & (k < nk - 1))
        def _():
            acc_ref[...] += product(a_ref, b_ref)

        @pl.when(k == nk - 1)
        def _():
            finish(acc_ref[...] + product(a_ref, b_ref), ex, o_ref)

    return pl.pallas_call(
        body_single if nk == 1 else body, name=name, grid=grid,
        in_specs=[a_spec, b_spec, *extra_specs] + [_hbm_spec()] * n_dep,
        out_specs=o_spec,
        out_shape=SDS(out_shape, out_dtype),
        scratch_shapes=[] if nk == 1 else [pltpu.VMEM(acc_shape, F32)],
        compiler_params=_cp(3),
    )(a, b, *extras, *deps)


def _relu2_bf16(t):
    r = jnp.maximum(t.astype(F32), 0.0)
    return (r * r).astype(BF16)


def _loss_kernel(y, t):
    S, D = y.shape
    tm = min(256, S)

    def body(y_ref, t_ref, dy_ref, dyb_ref, l_ref):
        @pl.when(pl.program_id(0) == 0)
        def _():
            l_ref[...] = jnp.zeros_like(l_ref)
        e = y_ref[...] - t_ref[...]
        l_ref[...] += jnp.sum(e * e, axis=0, keepdims=True)
        dy = e * (1.0 / D)
        dy_ref[...] = dy
        dyb_ref[...] = dy.astype(BF16)

    row = pl.BlockSpec((tm, D), lambda i: (i, 0))
    return pl.pallas_call(
        body, name="loss_head", grid=(S // tm,),
        in_specs=[row, row],
        out_specs=[row, row, pl.BlockSpec((1, D), lambda i: (0, 0))],
        out_shape=[SDS((S, D), F32), SDS((S, D), BF16), SDS((1, D), F32)],
        compiler_params=_cp(1),
    )(y, t)


def _rmsnorm_bwd(name, dh, x, g, dres, deps=()):
    S, D = x.shape
    tm = min(256, S)
    n_dep = len(deps)

    def body(dh_ref, x_ref, g_ref, dres_ref, *rest):
        dx_ref, dxb_ref, dg_ref = rest[n_dep:]
        @pl.when(pl.program_id(0) == 0)
        def _():
            dg_ref[...] = jnp.zeros_like(dg_ref)
        xv = x_ref[...]
        dhv = dh_ref[...]
        rstd = lax.rsqrt(jnp.mean(xv * xv, axis=-1, keepdims=True) + EPS)
        xhat = xv * rstd
        dg_ref[...] += jnp.sum(dhv * xhat, axis=0, keepdims=True)
        dxn = dhv * g_ref[...]
        dx = dres_ref[...] + rstd * (dxn - xhat * jnp.mean(dxn * xhat, axis=-1, keepdims=True))
        dx_ref[...] = dx
        dxb_ref[...] = dx.astype(BF16)

    row = pl.BlockSpec((tm, D), lambda i: (i, 0))
    vec = pl.BlockSpec((1, D), lambda i: (0, 0))
    return pl.pallas_call(
        body, name=name, grid=(S // tm,),
        in_specs=[row, row, vec, row] + [_hbm_spec()] * n_dep,
        out_specs=[row, row, vec],
        out_shape=[SDS((S, D), F32), SDS((S, D), BF16), SDS((1, D), F32)],
        compiler_params=_cp(1),
    )(dh, x, g, dres, *deps)


def _adamw(name, w, g, m, v):
    R, C = w.shape
    tr = 256 if R % 256 == 0 else R
    c1 = 1.0 - ADAM_B1 ** ADAM_STEP
    c2 = 1.0 - ADAM_B2 ** ADAM_STEP

    def body(w_ref, g_ref, m_ref, v_ref, d_ref, nm_ref, nv_ref, g_out_ref):
        gv = g_ref[...]
        nm = ADAM_B1 * m_ref[...] + (1.0 - ADAM_B1) * gv
        nv = ADAM_B2 * v_ref[...] + (1.0 - ADAM_B2) * (gv * gv)
        m_hat = nm / c1
        v_hat = nv / c2
        d_ref[...] = -ADAM_LR * (m_hat / (jnp.sqrt(v_hat) + ADAM_EPS) + ADAM_WD * w_ref[...])
        nm_ref[...] = nm
        nv_ref[...] = nv
        g_out_ref[...] = gv

    blk = pl.BlockSpec((tr, C), lambda i: (i, 0))
    return pl.pallas_call(
        body, name=name, grid=(R // tr,),
        in_specs=[blk] * 4, out_specs=[blk] * 4,
        out_shape=[SDS((R, C), F32)] * 4,
        compiler_params=_cp(1),
    )(w, g, m, v)


def _pair_select(lane, lo, hi):
    return jnp.where(lane < HEAD_DIM, lo, hi)


def _sgu_fwd(name, p, wt, bb):
    S = p.shape[0]

    def body(u_ref, v_ref, wt_ref, bb_ref, o_ref):
        lane = lax.broadcasted_iota(jnp.int32, (CHUNK, 128), 1)
        for pp in range(A_HEADS // 2):
            cs = slice(128 * pp, 128 * (pp + 1))
            vb = v_ref[:, cs].astype(BF16)
            mixed = _pair_select(lane,
                                 jnp.dot(wt_ref[2 * pp], vb, preferred_element_type=F32),
                                 jnp.dot(wt_ref[2 * pp + 1], vb, preferred_element_type=F32)) + bb_ref[:, cs]
            o_ref[:, cs] = (u_ref[:, cs] * mixed).astype(o_ref.dtype)

    return pl.pallas_call(
        body, name=name, grid=(S // CHUNK,),
        in_specs=[pl.BlockSpec((CHUNK, A_WIDTH), lambda c: (c, OFF_AU // A_WIDTH)),
                  pl.BlockSpec((CHUNK, A_WIDTH), lambda c: (c, OFF_AV // A_WIDTH)),
                  pl.BlockSpec((A_HEADS, CHUNK, CHUNK), lambda c: (0, 0, 0)),
                  pl.BlockSpec((CHUNK, A_WIDTH), lambda c: (0, 0))],
        out_specs=pl.BlockSpec((CHUNK, A_WIDTH), lambda c: (c, 0)),
        out_shape=SDS((S, A_WIDTH), BF16),
        compiler_params=_cp(1),
    )(p, p, wt, bb)


def _sgu_bwd(name, p, dycat, wt, wtt, bb):
    S = p.shape[0]

    def body(u_ref, v_ref, dy_ref, wt_ref, wtt_ref, bb_ref, du_ref, dv_ref, dw_ref, db_ref, dbacc_ref):
        c = pl.program_id(0)

        @pl.when(c == 0)
        def _():
            dw_ref[...] = jnp.zeros_like(dw_ref)
            dbacc_ref[...] = jnp.zeros_like(dbacc_ref)

        lane = lax.broadcasted_iota(jnp.int32, (CHUNK, 128), 1)
        row = lax.broadcasted_iota(jnp.int32, (CHUNK, 128), 0)
        causal = row >= lane
        for pp in range(A_HEADS // 2):
            cs = slice(128 * pp, 128 * (pp + 1))
            v = v_ref[:, cs]
            vb = v.astype(BF16)
            u = u_ref[:, cs]
            dy = dy_ref[:, cs]
            mixed = _pair_select(lane,
                                 jnp.dot(wt_ref[2 * pp], vb, preferred_element_type=F32),
                                 jnp.dot(wt_ref[2 * pp + 1], vb, preferred_element_type=F32)) + bb_ref[:, cs]
            du_ref[:, cs] = (dy * mixed).astype(du_ref.dtype)
            dm = dy * u
            dmb = dm.astype(BF16)
            dv = _pair_select(lane,
                              jnp.dot(wtt_ref[2 * pp], dmb, preferred_element_type=F32),
                              jnp.dot(wtt_ref[2 * pp + 1], dmb, preferred_element_type=F32))
            dv_ref[:, cs] = dv.astype(dv_ref.dtype)
            dbacc_ref[:, cs] += dm
            nt = (((1,), (1,)), ((), ()))
            dm_lo = jnp.where(lane < HEAD_DIM, dm, 0.0).astype(BF16)
            dm_hi = jnp.where(lane >= HEAD_DIM, dm, 0.0).astype(BF16)
            dw_ref[2 * pp] += jnp.where(causal, lax.dot_general(dm_lo, vb, nt, preferred_element_type=F32), 0.0)
            dw_ref[2 * pp + 1] += jnp.where(causal, lax.dot_general(dm_hi, vb, nt, preferred_element_type=F32), 0.0)

        @pl.when(c == S // CHUNK - 1)
        def _():
            out = jnp.zeros((CHUNK, 128), F32)
            for pp in range(A_HEADS // 2):
                acc = dbacc_ref[:, 128 * pp:128 * (pp + 1)]
                s_lo = jnp.sum(jnp.where(lane < HEAD_DIM, acc, 0.0), axis=1, keepdims=True)
                s_hi = jnp.sum(jnp.where(lane >= HEAD_DIM, acc, 0.0), axis=1, keepdims=True)
                out = jnp.where(lane == 2 * pp, s_lo, out)
                out = jnp.where(lane == 2 * pp + 1, s_hi, out)
            db_ref[...] = out

    chunk = lambda col: pl.BlockSpec((CHUNK, A_WIDTH), lambda c: (c, col))
    wspec = pl.BlockSpec((A_HEADS, CHUNK, CHUNK), lambda c: (0, 0, 0))
    return pl.pallas_call(
        body, name=name, grid=(S // CHUNK,),
        in_specs=[chunk(OFF_AU // A_WIDTH), chunk(OFF_AV // A_WIDTH), chunk(0), wspec, wspec,
                  pl.BlockSpec((CHUNK, A_WIDTH), lambda c: (0, 0))],
        out_specs=[chunk(0), chunk(0), wspec, pl.BlockSpec((CHUNK, 128), lambda c: (0, 0))],
        out_shape=[SDS((S, A_WIDTH), BF16), SDS((S, A_WIDTH), BF16),
                   SDS((A_HEADS, CHUNK, CHUNK), F32), SDS((CHUNK, 128), F32)],
        scratch_shapes=[pltpu.VMEM((CHUNK, A_WIDTH), F32)],
        compiler_params=_cp(1),
    )(p, p, dycat, wt, wtt, bb)


CONV_HALO = 8


def _shift_down(a, halo, k):
    T = a.shape[0]
    row = lax.broadcasted_iota(jnp.int32, a.shape, 0)
    out = pltpu.roll(a, k, 0)
    for r in range(k):
        out = jnp.where(row == r, halo[CONV_HALO - k + r:CONV_HALO - k + r + 1, :], out)
    return out


def _shift_up(a, halo, k):
    T = a.shape[0]
    row = lax.broadcasted_iota(jnp.int32, a.shape, 0)
    out = pltpu.roll(a, T - k, 0)
    for r in range(k):
        out = jnp.where(row == T - k + r, halo[r:r + 1, :], out)
    return out


def _conv_specs(S, T):
    hb = T // CONV_HALO
    last = S // CONV_HALO - 1
    tile = lambda col0: pl.BlockSpec((T, 128), lambda j, i: (i, col0 + j))
    prev = lambda col0: pl.BlockSpec((CONV_HALO, 128), lambda j, i: (jnp.maximum(i * hb - 1, 0), col0 + j))
    nxt = lambda col0: pl.BlockSpec((CONV_HALO, 128), lambda j, i: (jnp.minimum((i + 1) * hb, last), col0 + j))
    return tile, prev, nxt


def _conv_fwd(name, p, w):
    S = p.shape[0]
    T = min(512, S)
    tile, prev, _ = _conv_specs(S, T)
    cb, cc, cx = OFF_BB // 128, OFF_BC // 128, OFF_BX // 128

    def body(b_ref, c_ref, x_ref, ch_ref, xh_ref, w_ref, o_ref):
        i = pl.program_id(1)
        z = c_ref[...] * x_ref[...]
        zh = jnp.where(i > 0, ch_ref[...] * xh_ref[...], 0.0)
        z1 = _shift_down(z, zh, 1)
        z2 = _shift_down(z, zh, 2)
        conv = w_ref[0:1, :] * z2 + w_ref[1:2, :] * z1 + w_ref[2:3, :] * z
        o_ref[...] = (b_ref[...] * conv).astype(o_ref.dtype)

    return pl.pallas_call(
        body, name=name, grid=(B_WIDTH // 128, S // T),
        in_specs=[tile(cb), tile(cc), tile(cx), prev(cc), prev(cx),
                  pl.BlockSpec((3, 128), lambda j, i: (0, j))],
        out_specs=tile(0),
        out_shape=SDS((S, B_WIDTH), BF16),
        compiler_params=_cp(2),
    )(p, p, p, p, p, w)


def _conv_bwd(name, p, dycat, w):
    S = p.shape[0]
    T = min(512, S)
    tile, prev, nxt = _conv_specs(S, T)
    cb, cc, cx = OFF_BB // 128, OFF_BC // 128, OFF_BX // 128
    cdy = A_WIDTH // 128
    n_i = S // T

    def body(b_ref, c_ref, x_ref, dy_ref, ch_ref, xh_ref, bn_ref, dyn_ref, w_ref,
             db_ref, dc_ref, dx_ref, dw_ref):
        i = pl.program_id(1)

        @pl.when(i == 0)
        def _():
            dw_ref[...] = jnp.zeros_like(dw_ref)

        cv = c_ref[...]
        xv = x_ref[...]
        z = cv * xv
        zh = jnp.where(i > 0, ch_ref[...] * xh_ref[...], 0.0)
        z1 = _shift_down(z, zh, 1)
        z2 = _shift_down(z, zh, 2)
        w0, w1, w2 = w_ref[0:1, :], w_ref[1:2, :], w_ref[2:3, :]
        conv = w0 * z2 + w1 * z1 + w2 * z
        dy = dy_ref[...]
        db_ref[...] = (dy * conv).astype(db_ref.dtype)
        dconv = dy * b_ref[...]
        dconv_n = jnp.where(i < n_i - 1, dyn_ref[...] * bn_ref[...], 0.0)
        dz = w2 * dconv + w1 * _shift_up(dconv, dconv_n, 1) + w0 * _shift_up(dconv, dconv_n, 2)
        dc_ref[...] = (dz * xv).astype(dc_ref.dtype)
        dx_ref[...] = (dz * cv).astype(dx_ref.dtype)
        dw_ref[0:1, :] += jnp.sum(dconv * z2, axis=0, keepdims=True)
        dw_ref[1:2, :] += jnp.sum(dconv * z1, axis=0, keepdims=True)
        dw_ref[2:3, :] += jnp.sum(dconv * z, axis=0, keepdims=True)

    wspec = pl.BlockSpec((3, 128), lambda j, i: (0, j))
    return pl.pallas_call(
        body, name=name, grid=(B_WIDTH // 128, n_i),
        in_specs=[tile(cb), tile(cc), tile(cx), tile(cdy), prev(cc), prev(cx), nxt(cb), nxt(cdy), wspec],
        out_specs=[tile(0), tile(0), tile(0), wspec],
        out_shape=[SDS((S, B_WIDTH), BF16)] * 3 + [SDS((3, B_WIDTH), F32)],
        compiler_params=_cp(2),
    )(p, p, p, dycat, p, p, p, dycat, w)


def _seg_sum(t, bd):
    hi = t.astype(BF16)
    lo = (t - hi.astype(F32)).astype(BF16)
    return jnp.dot(hi, bd, preferred_element_type=F32) + jnp.dot(lo, bd, preferred_element_type=F32)


def _head_norm(x, g, bd):
    rstd = lax.rsqrt(_seg_sum(x * x, bd) * (1.0 / HEAD_DIM) + EPS)
    xhat = x * rstd
    return xhat * g, xhat, rstd


def _head_norm_bwd(dy, g, xhat, rstd, bd):
    dxh = dy * g
    return rstd * (dxh - xhat * (_seg_sum(dxh * xhat, bd) * (1.0 / HEAD_DIM)))


def _band_mask(has_prev):
    row = lax.broadcasted_iota(jnp.int32, (BLK, 2 * BLK), 0)
    col = lax.broadcasted_iota(jnp.int32, (BLK, 2 * BLK), 1)
    first_key = jnp.where(has_prev, 0, BLK)
    return (col >= row) & (col <= row + BLK) & (col >= first_key)


def _first_of_segment(g, n, n_blocks):
    per_seg = lax.shift_right_logical(jnp.int32(n_blocks), 2 * g)
    return (n & (per_seg - 1)) == 0


def _residue_rows(r, d):
    return slice(None) if d == 1 else pl.ds(r, BLK, stride=d)


STRIDED_LANES = 128


def _step_width(d):
    return PW if d == 1 else STRIDED_LANES


def _n_stack(lane):
    return lane.shape[1] // HEAD_DIM


def _for_residues(d, fn):
    if d == 1:
        fn(0)
    else:
        def two(i, carry):
            fn(2 * i)
            fn(2 * i + 1)
            return carry
        lax.fori_loop(0, d // 2, two, 0)


def _head_mask(lane, j):
    return (lane >= HEAD_DIM * j) & (lane < HEAD_DIM * (j + 1))


def _stack_heads(x, lane):
    return jnp.concatenate([jnp.where(_head_mask(lane, j), x, 0.0) for j in range(_n_stack(lane))], axis=0)


def _unstack_heads(y, lane):
    out = y[:BLK]
    for j in range(1, _n_stack(lane)):
        out = jnp.where(lane >= HEAD_DIM * j, y[BLK * j:BLK * (j + 1)], out)
    return out


def _head_columns(v, lane):
    return jnp.concatenate([jnp.max(jnp.where(_head_mask(lane, j), v, NEG), axis=1, keepdims=True)
                            for j in range(_n_stack(lane))], axis=0)


def _attn_fwd(name, p, g, gq, gk, bd):
    S = p.shape[0]
    d = PATTERN_DILATION[g]
    rows = BLK * d
    hw = _step_width(d)
    nt = (((1,), (1,)), ((), ()))

    def body(q_ref, kc_ref, kp_ref, vc_ref, vp_ref, gq_ref, gk_ref, bd_ref, o_ref, lse_ref):
        has_prev = pl.program_id(1) > 0
        bdv = bd_ref[...]
        band = jnp.concatenate([_band_mask(has_prev)] * (hw // HEAD_DIM), axis=0)
        lane = lax.broadcasted_iota(jnp.int32, (1, hw), 1)

        def residue(r):
            rr = _residue_rows(r, d)
            qn, _, _ = _head_norm(q_ref[rr, :], gq_ref[...], bdv)
            kn, _, _ = _head_norm(jnp.concatenate([kp_ref[rr, :], kc_ref[rr, :]], axis=0), gk_ref[...], bdv)
            knb = kn.astype(BF16)
            vb = jnp.concatenate([vp_ref[rr, :], vc_ref[rr, :]], axis=0).astype(BF16)
            qs = _stack_heads(qn, lane).astype(BF16)
            s = lax.dot_general(qs, knb, nt, preferred_element_type=F32) * (HEAD_DIM ** -0.5)
            s = jnp.where(band, s, NEG)
            m = jnp.max(s, axis=1, keepdims=True)
            e = jnp.exp(s - m)
            den = jnp.sum(e, axis=1, keepdims=True)
            pv = jnp.dot(e.astype(BF16), vb, preferred_element_type=F32)
            o_ref[rr, :] = _unstack_heads(pv / den, lane)
            lse_ref[rr, :] = _unstack_heads(jnp.broadcast_to(m + jnp.log(den), pv.shape), lane)

        _for_residues(d, residue)

    per = PW // hw
    cq, ck, cv = (OFF_Q + PW * g) // hw, (OFF_K + PW * g) // hw, (OFF_V + PW * g) // hw
    cur = lambda col: pl.BlockSpec((rows, hw), lambda h, n: (n, col + h))
    prv = lambda col: pl.BlockSpec((rows, hw), lambda h, n: (jnp.maximum(n - 1, 0), col + h))
    vec = pl.BlockSpec((1, hw), lambda h, n: (0, h))
    return pl.pallas_call(
        body, name=name, grid=(per, S // rows),
        in_specs=[cur(cq), cur(ck), prv(ck), cur(cv), prv(cv), vec, vec, pl.BlockSpec((hw, hw), lambda h, n: (0, 0))],
        out_specs=[cur(0), cur(0)],
        out_shape=[SDS((S, PW), F32)] * 2,
        compiler_params=_cp(2),
    )(p, p, p, p, p, gq, gk, bd)


def _attn_bwd(name, p, g, lse, do3, c3, gq, gk, bd):
    S = p.shape[0]
    d = PATTERN_DILATION[g]
    rows = BLK * d
    nblk = S // rows
    hw = _step_width(d)
    nt = (((1,), (1,)), ((), ()))
    tn = (((0,), (0,)), ((), ()))

    def body(q_ref, kc_ref, kp_ref, vc_ref, vp_ref, lse_ref, do_ref, c_ref, gq_ref, gk_ref, bd_ref,
             dq_ref, dk_ref, dv_ref, dgq_ref, dgk_ref, ck_ref, cv_ref, dq_keep_ref):
        n = pl.program_id(1)

        @pl.when(n == 0)
        def _():
            ck_ref[...] = jnp.zeros_like(ck_ref)
            cv_ref[...] = jnp.zeros_like(cv_ref)
            dgq_ref[...] = jnp.zeros_like(dgq_ref)
            dgk_ref[...] = jnp.zeros_like(dgk_ref)

        @pl.when(n == nblk)
        def _():
            dq_ref[...] = dq_keep_ref[...]
            dk_ref[...] = ck_ref[...]
            dv_ref[...] = cv_ref[...]

        bdv = bd_ref[...]
        gqv = gq_ref[...]
        gkv = gk_ref[...]
        band = jnp.concatenate([_band_mask(n > 0)] * (hw // HEAD_DIM), axis=0)
        lane = lax.broadcasted_iota(jnp.int32, (1, hw), 1)

        def residue(r):
            rr = _residue_rows(r, d)
            qn, qhat, qrstd = _head_norm(q_ref[rr, :], gqv, bdv)
            kn, khat, krstd = _head_norm(jnp.concatenate([kp_ref[rr, :], kc_ref[rr, :]], axis=0), gkv, bdv)
            knb = kn.astype(BF16)
            vb = jnp.concatenate([vp_ref[rr, :], vc_ref[rr, :]], axis=0).astype(BF16)
            qs = _stack_heads(qn, lane).astype(BF16)
            dos = _stack_heads(do_ref[rr, :], lane).astype(BF16)
            s = lax.dot_general(qs, knb, nt, preferred_element_type=F32) * (HEAD_DIM ** -0.5)
            prob = jnp.where(band, jnp.exp(s - _head_columns(lse_ref[rr, :], lane)), 0.0)
            dp = lax.dot_general(dos, vb, nt, preferred_element_type=F32)
            ds = (prob * (dp + _head_columns(c_ref[rr, :], lane)) * (HEAD_DIM ** -0.5)).astype(BF16)
            dqn = _unstack_heads(jnp.dot(ds, knb, preferred_element_type=F32), lane)
            dkn = lax.dot_general(ds, qs, tn, preferred_element_type=F32)
            dvv = lax.dot_general(prob.astype(BF16), dos, tn, preferred_element_type=F32)

            dq = _head_norm_bwd(dqn, gqv, qhat, qrstd, bdv)
            dq_ref[rr, :] = dq
            dq_keep_ref[rr, :] = dq
            dk2 = _head_norm_bwd(dkn, gkv, khat, krstd, bdv)
            dgq_ref[...] += jnp.sum(dqn * qhat, axis=0, keepdims=True)
            dgk_ref[...] += jnp.sum(dkn * khat, axis=0, keepdims=True)
            dk_ref[rr, :] = ck_ref[rr, :] + dk2[:BLK]
            dv_ref[rr, :] = cv_ref[rr, :] + dvv[:BLK]
            ck_ref[rr, :] = dk2[BLK:]
            cv_ref[rr, :] = dvv[BLK:]

        @pl.when(n < nblk)
        def _():
            _for_residues(d, residue)

    last = nblk - 1
    per = PW // hw
    cq, ck, cv = (OFF_Q + PW * g) // hw, (OFF_K + PW * g) // hw, (OFF_V + PW * g) // hw
    cur = lambda col: pl.BlockSpec((rows, hw), lambda h, n: (jnp.minimum(n, last), col + h))
    prv = lambda col: pl.BlockSpec((rows, hw), lambda h, n: (jnp.maximum(jnp.minimum(n, last) - 1, 0), col + h))
    cur3 = pl.BlockSpec((None, rows, hw), lambda h, n: (g, jnp.minimum(n, last), h))
    done = pl.BlockSpec((rows, hw), lambda h, n: (jnp.maximum(n - 1, 0), h))
    vec = pl.BlockSpec((1, hw), lambda h, n: (0, h))
    return pl.pallas_call(
        body, name=name, grid=(per, nblk + 1),
        in_specs=[cur(cq), cur(ck), prv(ck), cur(cv), prv(cv), cur(0), cur3, cur3, vec, vec,
                  pl.BlockSpec((hw, hw), lambda h, n: (0, 0))],
        out_specs=[cur(0), done, done, vec, vec],
        out_shape=[SDS((S, PW), F32)] * 3 + [SDS((1, PW), F32)] * 2,
        scratch_shapes=[pltpu.VMEM((rows, hw), F32)] * 3,
        compiler_params=_cp(2),
    )(p, p, p, p, p, lse, do3, c3, gq, gk, bd)


def _mix_fwd(name, os, lses):
    S = os[0].shape[0]
    tm = min(512, S)

    def body(o0, o1, o2, l0, l1, l2, y_ref):
        o = [o0[...], o1[...], o2[...]]
        l = [l0[...], l1[...], l2[...]]
        m = jnp.maximum(jnp.maximum(l[0], l[1]), l[2])
        e = [jnp.exp(t - m) for t in l]
        inv = 1.0 / (e[0] + e[1] + e[2])
        for g in range(N_PATTERNS):
            y_ref[:, PW * g:PW * (g + 1)] = (o[g] * (e[g] * inv)).astype(y_ref.dtype)

    blk = pl.BlockSpec((tm, PW), lambda i: (i, 0))
    return pl.pallas_call(
        body, name=name, grid=(S // tm,),
        in_specs=[blk] * 6,
        out_specs=pl.BlockSpec((tm, C_WIDTH), lambda i: (i, 0)),
        out_shape=SDS((S, C_WIDTH), BF16),
        compiler_params=_cp(1),
    )(*os, *lses)


def _mix_bwd(name, os, lses, dycat, bd):
    S = os[0].shape[0]
    tm = min(512, S)
    c0 = (A_WIDTH + B_WIDTH) // PW

    def body(o0, o1, o2, l0, l1, l2, dy0_ref, dy1_ref, dy2_ref, bd_ref, do_ref, c_ref):
        bdv = bd_ref[...]
        o = [o0[...], o1[...], o2[...]]
        l = [l0[...], l1[...], l2[...]]
        dys = [dy0_ref[...], dy1_ref[...], dy2_ref[...]]
        m = jnp.maximum(jnp.maximum(l[0], l[1]), l[2])
        e = [jnp.exp(t - m) for t in l]
        inv = 1.0 / (e[0] + e[1] + e[2])
        alpha = [t * inv for t in e]
        da = [_seg_sum(dys[g] * o[g], bdv) for g in range(N_PATTERNS)]
        mean_da = alpha[0] * da[0] + alpha[1] * da[1] + alpha[2] * da[2]
        for g in range(N_PATTERNS):
            do_ref[g] = dys[g] * alpha[g]
            c_ref[g] = -alpha[g] * mean_da

    blk = pl.BlockSpec((tm, PW), lambda i: (i, 0))
    blk3 = pl.BlockSpec((N_PATTERNS, tm, PW), lambda i: (0, i, 0))
    dyspec = lambda g: pl.BlockSpec((tm, PW), lambda i: (i, c0 + g))
    return pl.pallas_call(
        body, name=name, grid=(S // tm,),
        in_specs=[blk] * 6 + [dyspec(0), dyspec(1), dyspec(2), pl.BlockSpec((PW, PW), lambda i: (0, 0))],
        out_specs=[blk3, blk3],
        out_shape=[SDS((N_PATTERNS, S, PW), F32)] * 2,
        compiler_params=_cp(1),
    )(*os, *lses, dycat, dycat, dycat, bd)


def _mesh_pos():
    x, y, c = lax.axis_index("x"), lax.axis_index("y"), lax.axis_index("c")
    chips = [(1 - x, y), (x, 1 - y), (1 - x, 1 - y)]
    chip_idx = [2 * cx + cy for cx, cy in chips]
    return x, y, c, 2 * x + y, chips, chip_idx


def _place_shard(name, w, layer, chip_arr, out_dtype, deps=()):
    _, R, C = w.shape
    tr = min(256, R)

    def body(chip_ref, w_ref, *rest):
        o_ref = rest[-1]
        o_ref[...] = w_ref[...].astype(o_ref.dtype)

    return pl.pallas_call(
        body, name=name,
        grid_spec=pltpu.PrefetchScalarGridSpec(
            num_scalar_prefetch=1, grid=(R // tr,),
            in_specs=[pl.BlockSpec((None, tr, C), lambda i, chip_ref: (layer, i, 0))] + [_hbm_spec()] * len(deps),
            out_specs=pl.BlockSpec((None, tr, C), lambda i, chip_ref: (chip_ref[0], i, 0))),
        out_shape=SDS((N_CHIPS, R, C), out_dtype),
        compiler_params=_cp(1),
    )(chip_arr, w, *deps)


HBM_SPEC = pl.BlockSpec(memory_space=pltpu.HBM)
SEM_SPEC = pl.BlockSpec(memory_space=pltpu.SEMAPHORE)
SPLIT_COPY = pltpu.SideEffectType.DATAFLOW_SIDE_EFFECTING
N_PEER_CHIPS = N_CHIPS - 1
TOKEN_SHAPE = SDS((8, 128), F32)
TOKEN_SPEC = pl.BlockSpec(memory_space=pltpu.VMEM)


def _in_hbm(a):
    return pltpu.with_memory_space_constraint(a, pltpu.HBM)


def _gather_start(name, bufs):
    T = len(bufs)

    def body(*refs):
        ins = refs[:T]
        send_sems, recv_sems = refs[T:2 * T], refs[2 * T:3 * T]
        token = refs[4 * T]
        x, y, c, me, chips, chip_idx = _mesh_pos()
        for t in range(T):
            hr = ins[t].shape[1] // 2
            mine = ins[t].at[me, pl.ds(c * hr, hr), :]
            for j in range(N_PEER_CHIPS):
                pltpu.make_async_remote_copy(src_ref=mine, dst_ref=mine, send_sem=send_sems[t].at[j],
                                             recv_sem=recv_sems[t].at[j], device_id=(*chips[j], c),
                                             device_id_type=MESH).start()
        token[...] = jnp.zeros_like(token)

    sems = [pltpu.SemaphoreType.DMA((N_PEER_CHIPS,))] * T
    out = pl.pallas_call(
        body, name=name,
        in_specs=[HBM_SPEC] * T,
        out_specs=[SEM_SPEC] * (2 * T) + [HBM_SPEC] * T + [TOKEN_SPEC],
        out_shape=sems + sems + [pltpu.HBM(b.shape, b.dtype) for b in bufs] + [TOKEN_SHAPE],
        input_output_aliases={t: 2 * T + t for t in range(T)},
        compiler_params=pltpu.CompilerParams(has_side_effects=SPLIT_COPY),
    )(*[_in_hbm(b) for b in bufs])
    return out[:T], out[T:2 * T], out[2 * T:3 * T], out[3 * T]


def _gather_wait(name, buf, send_sem, recv_sem, after):
    n_in = 3 if after is None else 4

    def body(*refs):
        buf_ref, ssem, rsem = refs[:3]
        x, y, c, me, chips, chip_idx = _mesh_pos()
        hr = buf_ref.shape[1] // 2
        mine = buf_ref.at[me, pl.ds(c * hr, hr), :]
        for j in range(N_PEER_CHIPS):
            got = buf_ref.at[chip_idx[j], pl.ds(c * hr, hr), :]
            cp = pltpu.make_async_remote_copy(src_ref=mine, dst_ref=got, send_sem=ssem.at[j], recv_sem=rsem.at[j],
                                              device_id=(*chips[j], c), device_id_type=MESH)
            cp.wait_send()
            cp.wait_recv()

    args = [buf, send_sem, recv_sem] + ([] if after is None else [after])
    return pl.pallas_call(
        body, name=name,
        in_specs=[HBM_SPEC, SEM_SPEC, SEM_SPEC] + [_hbm_spec()] * (n_in - 3),
        out_specs=HBM_SPEC,
        out_shape=pltpu.HBM(buf.shape, buf.dtype),
        input_output_aliases={0: 0},
        compiler_params=pltpu.CompilerParams(has_side_effects=SPLIT_COPY),
    )(*args)


def _forward_start(name, buf):
    def body(buf_ref, send_sems, recv_sems, buf_thru, token):
        x, y, c, me, chips, chip_idx = _mesh_pos()
        hr = buf_ref.shape[1] // 2
        for j in range(N_PEER_CHIPS):
            got = buf_ref.at[chip_idx[j], pl.ds(c * hr, hr), :]
            pltpu.make_async_remote_copy(src_ref=got, dst_ref=got, send_sem=send_sems.at[j], recv_sem=recv_sems.at[j],
                                         device_id=(x, y, 1 - c), device_id_type=MESH).start()
        token[...] = jnp.zeros_like(token)

    sems = pltpu.SemaphoreType.DMA((N_PEER_CHIPS,))
    return pl.pallas_call(
        body, name=name,
        in_specs=[HBM_SPEC],
        out_specs=[SEM_SPEC, SEM_SPEC, HBM_SPEC, TOKEN_SPEC],
        out_shape=[sems, sems, pltpu.HBM(buf.shape, buf.dtype), TOKEN_SHAPE],
        input_output_aliases={0: 2},
        compiler_params=pltpu.CompilerParams(has_side_effects=SPLIT_COPY),
    )(_in_hbm(buf))


def _forward_wait(name, buf, send_sems, recv_sems, after):
    n_in = 3 if after is None else 4

    def body(*refs):
        buf_ref, ssems, rsems = refs[:3]
        x, y, c, me, chips, chip_idx = _mesh_pos()
        hr = buf_ref.shape[1] // 2
        for j in range(N_PEER_CHIPS):
            sent = buf_ref.at[chip_idx[j], pl.ds(c * hr, hr), :]
            theirs = buf_ref.at[chip_idx[j], pl.ds((1 - c) * hr, hr), :]
            cp = pltpu.make_async_remote_copy(src_ref=sent, dst_ref=theirs, send_sem=ssems.at[j],
                                              recv_sem=rsems.at[j], device_id=(x, y, 1 - c), device_id_type=MESH)
            cp.wait_send()
            cp.wait_recv()

    args = [buf, send_sems, recv_sems] + ([] if after is None else [after])
    return pl.pallas_call(
        body, name=name,
        in_specs=[HBM_SPEC, SEM_SPEC, SEM_SPEC] + [_hbm_spec()] * (n_in - 3),
        out_specs=HBM_SPEC,
        out_shape=pltpu.HBM(buf.shape, buf.dtype),
        input_output_aliases={0: 0},
        compiler_params=pltpu.CompilerParams(has_side_effects=SPLIT_COPY),
    )(*args)


class _GatheredWeights:
    def __init__(self):
        self._order = []
        self._pending = {}
        self._forwarding = {}
        self._ready = {}
        self._tokens = []

    def start(self, keys, bufs):
        send_sems, recv_sems, thru, token = _gather_start(f"gather_start_{len(self._order)}", bufs)
        self._tokens.append(token)
        self._order.extend(keys)
        self._pending.update({k: (b, s, r) for k, b, s, r in zip(keys, thru, send_sems, recv_sems)})

    def _prefetch(self, key, after):
        if key in self._pending:
            buf, ssem, rsem = self._pending.pop(key)
            tag = f"{key[0]}_{key[1]}"
            buf = _gather_wait(f"gather_wait_{tag}", buf, ssem, rsem, after)
            ssems, rsems, buf, token = _forward_start(f"gather_fwd_start_{tag}", buf)
            self._forwarding[key] = (buf, ssems, rsems)
            self._tokens.append(token)

    def get(self, name, layer, after=None):
        key = (name, layer)
        if key not in self._ready:
            self._prefetch(key, after)
            buf, ssems, rsems = self._forwarding.pop(key)
            self._ready[key] = _forward_wait(f"gather_fwd_wait_{name}_{layer}", buf, ssems, rsems, after)
            nxt = self._order.index(key) + 1
            if nxt < len(self._order):
                self._prefetch(self._order[nxt], after)
        return self._ready[key]

    def deps(self):
        tokens, self._tokens = self._tokens, []
        return tokens


def _swap_copy(g_ref, land_ref, send_sem, recv_sem):
    x, y, c, _, _, _ = _mesh_pos()
    hr = g_ref.shape[1] // 2
    return pltpu.make_async_remote_copy(src_ref=g_ref.at[:, pl.ds((1 - c) * hr, hr), :], dst_ref=land_ref,
                                        send_sem=send_sem, recv_sem=recv_sem, device_id=(x, y, 1 - c),
                                        device_id_type=MESH)


def _swap_start(name, g):
    land_shape = (g.shape[0], g.shape[1] // 2, g.shape[2])

    def body(g_ref, land_ref, send_sem, recv_sem, land_thru, token):
        _swap_copy(g_ref, land_ref, send_sem, recv_sem).start()
        token[...] = jnp.zeros_like(token)

    return pl.pallas_call(
        body, name=name,
        in_specs=[HBM_SPEC, HBM_SPEC],
        out_specs=[SEM_SPEC, SEM_SPEC, HBM_SPEC, TOKEN_SPEC],
        out_shape=[pltpu.SemaphoreType.DMA(()), pltpu.SemaphoreType.DMA(()), pltpu.HBM(land_shape, g.dtype),
                   TOKEN_SHAPE],
        input_output_aliases={1: 2},
        compiler_params=pltpu.CompilerParams(has_side_effects=SPLIT_COPY),
    )(_in_hbm(g), _in_hbm(lax.empty(land_shape, g.dtype)))


def _swap_wait(name, g, land, send_sem, recv_sem, after):
    def body(g_ref, land_ref, send_sem, recv_sem, after_ref, land_out):
        cp = _swap_copy(g_ref, land_ref, send_sem, recv_sem)
        cp.wait_send()
        cp.wait_recv()

    return pl.pallas_call(
        body, name=name,
        in_specs=[HBM_SPEC, HBM_SPEC, SEM_SPEC, SEM_SPEC, _hbm_spec()],
        out_specs=HBM_SPEC,
        out_shape=pltpu.HBM(land.shape, land.dtype),
        input_output_aliases={1: 0},
        compiler_params=pltpu.CompilerParams(has_side_effects=SPLIT_COPY),
    )(_in_hbm(g), land, send_sem, recv_sem, after)


def _add_my_half(name, g, r, c_arr):
    ns, R, C = g.shape
    hr = R // 2
    tr = min(256, hr)
    nt = hr // tr

    def body(c_ref, g_ref, r_ref, o_ref, land_ref):
        t = (g_ref[...] + r_ref[...]).astype(o_ref.dtype)
        o_ref[...] = t
        land_ref[...] = t

    out = pl.BlockSpec((None, tr, C), lambda s, i, c_ref: (s, i, 0))
    return pl.pallas_call(
        body, name=name,
        grid_spec=pltpu.PrefetchScalarGridSpec(
            num_scalar_prefetch=1, grid=(ns, nt),
            in_specs=[pl.BlockSpec((None, tr, C), lambda s, i, c_ref: (s, c_ref[0] * nt + i, 0)), out],
            out_specs=[out, out]),
        out_shape=[SDS((ns, hr, C), BF16)] * 2,
        compiler_params=_cp(2),
    )(c_arr, g, r)


def _exchange_start(name, part, land):
    def body(part_ref, land_ref, send_sems, recv_sems, land_thru, token):
        x, y, c, me, chips, chip_idx = _mesh_pos()
        for j in range(N_PEER_CHIPS):
            pltpu.make_async_remote_copy(src_ref=part_ref.at[chip_idx[j]], dst_ref=land_ref.at[me],
                                         send_sem=send_sems.at[j], recv_sem=recv_sems.at[j],
                                         device_id=(*chips[j], c), device_id_type=MESH).start()
        token[...] = jnp.zeros_like(token)

    sems = pltpu.SemaphoreType.DMA((N_PEER_CHIPS,))
    return pl.pallas_call(
        body, name=name,
        in_specs=[HBM_SPEC, HBM_SPEC],
        out_specs=[SEM_SPEC, SEM_SPEC, HBM_SPEC, TOKEN_SPEC],
        out_shape=[sems, sems, pltpu.HBM(land.shape, land.dtype), TOKEN_SHAPE],
        input_output_aliases={1: 2},
        compiler_params=pltpu.CompilerParams(has_side_effects=SPLIT_COPY),
    )(_in_hbm(part), _in_hbm(land))


def _exchange_wait(name, part, land, send_sems, recv_sems, after):
    def body(part_ref, land_ref, send_sems, recv_sems, after_ref, land_out):
        x, y, c, me, chips, chip_idx = _mesh_pos()
        for j in range(N_PEER_CHIPS):
            cp = pltpu.make_async_remote_copy(src_ref=part_ref.at[chip_idx[j]], dst_ref=land_ref.at[chip_idx[j]],
                                              send_sem=send_sems.at[j], recv_sem=recv_sems.at[j],
                                              device_id=(*chips[j], c), device_id_type=MESH)
            cp.wait_send()
            cp.wait_recv()

    return pl.pallas_call(
        body, name=name,
        in_specs=[HBM_SPEC, HBM_SPEC, SEM_SPEC, SEM_SPEC, _hbm_spec()],
        out_specs=HBM_SPEC,
        out_shape=pltpu.HBM(land.shape, land.dtype),
        input_output_aliases={1: 0},
        compiler_params=pltpu.CompilerParams(has_side_effects=SPLIT_COPY),
    )(_in_hbm(part), land, send_sems, recv_sems, after)


class _GradReducer:
    def __init__(self, c_arr):
        self._c_arr = c_arr
        self._swapping = []
        self._exchanging = {}
        self._tokens = []

    def begin(self, name, layer, g):
        tag = f"{name}_{layer}"
        ssem, rsem, land, token = _swap_start(f"rs_swap_start_{tag}", g)
        self._swapping.append((name, layer, g, ssem, rsem, land))
        self._tokens.append(token)

    def advance(self, after):
        for name, layer, g, ssem, rsem, land in self._swapping:
            tag = f"{name}_{layer}"
            theirs = _swap_wait(f"rs_swap_wait_{tag}", g, land, ssem, rsem, after)
            part, own = _add_my_half(f"rs_add_{tag}", g, theirs, self._c_arr)
            ssems, rsems, land2, token = _exchange_start(f"rs_xchg_start_{tag}", part, own)
            self._exchanging[(name, layer)] = (part, ssems, rsems, land2)
            self._tokens.append(token)
        self._swapping = []

    def deps(self):
        tokens, self._tokens = self._tokens, []
        return tokens

    def finish(self, names, n_layers, after):
        bufs = []
        for name in names:
            buf = None
            for layer in range(n_layers):
                part, ssems, rsems, land = self._exchanging.pop((name, layer))
                tag = f"{name}_{layer}"
                landed = _exchange_wait(f"rs_xchg_wait_{tag}", part, land, ssems, rsems, after)
                buf = _sum_chips(f"rs_sum_{tag}", landed, self._c_arr, layer, n_layers, buf)
            bufs.append(buf)
        return dict(zip(names, _join_halves(f"rs_join_{names[0]}", bufs)))


def _sum_chips(name, r, c_arr, layer, n_layers, prev):
    ns, H, C = r.shape
    tr = min(256, H)
    nt = H // tr

    def body(c_ref, r_ref, *rest):
        o_ref = rest[-1]
        o_ref[...] = ((r_ref[0].astype(F32) + r_ref[1].astype(F32)) + r_ref[2].astype(F32)) + r_ref[3].astype(F32)

    in_specs = [pl.BlockSpec((ns, tr, C), lambda i, c_ref: (0, i, 0))]
    args = [c_arr, r]
    aliases = {}
    if prev is not None:
        in_specs.append(_hbm_spec())
        args.append(prev)
        aliases = {2: 0}
    return pl.pallas_call(
        body, name=name,
        grid_spec=pltpu.PrefetchScalarGridSpec(
            num_scalar_prefetch=1, grid=(nt,), in_specs=in_specs,
            out_specs=pl.BlockSpec((None, tr, C), lambda i, c_ref: (layer, c_ref[0] * nt + i, 0))),
        out_shape=SDS((n_layers, 2 * H, C), F32),
        input_output_aliases=aliases,
        compiler_params=_cp(1),
    )(*args)


def _join_halves(name, bufs):
    T = len(bufs)

    def body(*refs):
        outs = refs[T:2 * T]
        send_sems, recv_sems = refs[2 * T:]
        x, y, c, _, _, _ = _mesh_pos()
        cps = []
        for t in range(T):
            hr = outs[t].shape[1] // 2
            mine = outs[t].at[:, pl.ds(c * hr, hr), :]
            cp = pltpu.make_async_remote_copy(src_ref=mine, dst_ref=mine, send_sem=send_sems.at[t],
                                              recv_sem=recv_sems.at[t], device_id=(x, y, 1 - c), device_id_type=MESH)
            cp.start()
            cps.append(cp)
        for t in range(T):
            hr = outs[t].shape[1] // 2
            theirs = outs[t].at[:, pl.ds((1 - c) * hr, hr), :]
            pltpu.make_async_remote_copy(src_ref=theirs, dst_ref=theirs, send_sem=send_sems.at[t],
                                         recv_sem=recv_sems.at[t], device_id=(x, y, 1 - c),
                                         device_id_type=MESH).wait_recv()
        for cp in cps:
            cp.wait_send()

    return pl.pallas_call(
        body, name=name,
        in_specs=[_hbm_spec()] * T, out_specs=[_hbm_spec()] * T,
        out_shape=[SDS(b.shape, b.dtype) for b in bufs],
        input_output_aliases={t: t for t in range(T)},
        scratch_shapes=[pltpu.SemaphoreType.DMA((T,)), pltpu.SemaphoreType.DMA((T,))],
    )(*bufs)


def _small_copy(k, buf_ref, land_ref, send_sems, recv_sems):
    x, y, c = lax.axis_index("x"), lax.axis_index("y"), lax.axis_index("c")
    me = 4 * x + 2 * y + c
    peer = (x ^ ((k >> 2) & 1), y ^ ((k >> 1) & 1), c ^ (k & 1))
    cp = pltpu.make_async_remote_copy(src_ref=buf_ref, dst_ref=land_ref.at[me], send_sem=send_sems.at[k - 1],
                                      recv_sem=recv_sems.at[k - 1], device_id=peer, device_id_type=MESH)
    return me, peer, cp


def _small_start(buf, deps):
    land = jnp.broadcast_to(buf[None], (N_DEV,) + buf.shape)
    n_dep = len(deps)

    def body(buf_ref, land_ref, *rest):
        send_sems, recv_sems, _, token = rest[n_dep:]
        for k in range(1, N_DEV):
            _small_copy(k, buf_ref, land_ref, send_sems, recv_sems)[2].start()
        token[...] = jnp.zeros_like(token)

    sems = pltpu.SemaphoreType.DMA((N_DEV - 1,))
    return pl.pallas_call(
        body, name="small_gather_start",
        in_specs=[HBM_SPEC, HBM_SPEC] + [_hbm_spec()] * n_dep,
        out_specs=[SEM_SPEC, SEM_SPEC, HBM_SPEC, TOKEN_SPEC],
        out_shape=[sems, sems, pltpu.HBM(land.shape, land.dtype), TOKEN_SHAPE],
        input_output_aliases={1: 2},
        compiler_params=pltpu.CompilerParams(has_side_effects=SPLIT_COPY),
    )(_in_hbm(buf), _in_hbm(land), *deps)


def _small_wait(buf, land, send_sems, recv_sems, after):
    def body(buf_ref, land_ref, send_sems, recv_sems, after_ref, land_out):
        for k in range(1, N_DEV):
            me, peer, cp = _small_copy(k, buf_ref, land_ref, send_sems, recv_sems)
            cp.wait_send()
            got = land_ref.at[me ^ k]
            pltpu.make_async_remote_copy(src_ref=got, dst_ref=got, send_sem=send_sems.at[k - 1],
                                         recv_sem=recv_sems.at[k - 1], device_id=peer,
                                         device_id_type=MESH).wait_recv()

    return pl.pallas_call(
        body, name="small_gather_wait",
        in_specs=[HBM_SPEC, HBM_SPEC, SEM_SPEC, SEM_SPEC, _hbm_spec()],
        out_specs=HBM_SPEC,
        out_shape=pltpu.HBM(land.shape, land.dtype),
        input_output_aliases={1: 0},
        compiler_params=pltpu.CompilerParams(has_side_effects=SPLIT_COPY),
    )(_in_hbm(buf), land, send_sems, recv_sems, after)


def _sum_devices(land):
    n, R, C = land.shape

    def body(land_ref, out_ref):
        acc = land_ref[0]
        for d in range(1, n):
            acc = acc + land_ref[d]
        out_ref[...] = acc

    return pl.pallas_call(
        body, name="small_sum",
        in_specs=[pl.BlockSpec(memory_space=pltpu.VMEM)],
        out_specs=pl.BlockSpec(memory_space=pltpu.VMEM),
        out_shape=SDS((R, C), land.dtype),
        compiler_params=pltpu.CompilerParams(vmem_limit_bytes=V7X_VMEM_LIMIT),
    )(land)


def _deinterleave(t, d):
    if d == 1:
        return t
    S, W = t.shape
    return t.reshape(S // d, d, W).transpose(1, 0, 2).reshape(S, W)


def _interleave(t, d):
    if d == 1:
        return t
    S, W = t.shape
    return t.reshape(d, S // d, W).transpose(1, 0, 2).reshape(S, W)


def _to_patterns(t, off):
    return jnp.stack([_deinterleave(t[:, off + PW * g:off + PW * (g + 1)], PATTERN_DILATION[g])
                      for g in range(N_PATTERNS)])


def _from_patterns(t3):
    return jnp.stack([_interleave(t3[g], PATTERN_DILATION[g]) for g in range(N_PATTERNS)])


def _pack_rows(vectors):
    flat = jnp.concatenate([v.reshape(-1) for v in vectors])
    n = flat.shape[0]
    padded = -(-n // 1024) * 1024
    return jnp.pad(flat, (0, padded - n)).reshape(padded // 128, 128)


def _unpack_rows(buf, shapes):
    flat = buf.reshape(-1)
    out, off = [], 0
    for s in shapes:
        n = 1
        for dim in s:
            n *= dim
        out.append(flat[off:off + n].reshape(s))
        off += n
    return out


def _layer_forward(l, x, prm, wg):
    S, D = x.shape
    w_in = wg.get("w_in", l, x)
    p, h = _norm_matmul(f"in_proj_{l}", x, prm["attn_norm"][l], w_in, F32, deps=wg.deps())
    y_a = _sgu_fwd(f"sgu_fwd_{l}", p, prm["sgu_wt"][l], prm["sgu_bb"][l])
    y_b = _conv_fwd(f"conv_fwd_{l}", p, prm["conv_w"][l])
    os, lses = [], []
    for g in range(N_PATTERNS):
        o_g, lse_g = _attn_fwd(f"attn_fwd_{l}_{g}", p, g, prm["q_gain"][l], prm["k_gain"][l], prm["bd"])
        os.append(o_g)
        lses.append(lse_g)
    y_c = _mix_fwd(f"mix_fwd_{l}", os, lses)
    ycat = jnp.concatenate([y_a, y_b, y_c], axis=1)
    tmb, tnb = min(1024, S), min(1024, D)
    w_out = wg.get("w_out", l, ycat)
    rq = w_out.shape[1]
    x1 = _matmul(
        f"out_proj_{l}", ycat, w_out, (S, D), F32, grid=(S // tmb, D // tnb, N_CHIPS),
        a_spec=pl.BlockSpec((tmb, rq), lambda i, j, k: (i, k)),
        b_spec=pl.BlockSpec((None, rq, tnb), lambda i, j, k: (k, 0, j)),
        o_spec=pl.BlockSpec((tmb, tnb), lambda i, j, k: (i, j)),
        contract=(1, 0), acc_shape=(tmb, tnb),
        extras=(x,), extra_specs=(pl.BlockSpec((tmb, tnb), lambda i, j, k: (i, j)),),
        epi=lambda r, res: r + res, deps=wg.deps())
    w_mlp_in = wg.get("w_mlp_in", l, x1)
    a, h2 = _norm_matmul(f"mlp_in_{l}", x1, prm["mlp_norm"][l], w_mlp_in, BF16, deps=wg.deps())
    w_mlp_out = wg.get("w_mlp_out", l, a)
    dff4 = w_mlp_out.shape[1]
    tk = min(1024, dff4)
    kpc = dff4 // tk
    x2 = _matmul(
        f"mlp_out_{l}", a, w_mlp_out, (S, D), F32, grid=(S // tmb, D // tnb, N_CHIPS * kpc),
        a_spec=pl.BlockSpec((tmb, tk), lambda i, j, k: (i, k)),
        b_spec=pl.BlockSpec((None, tk, tnb), lambda i, j, k: (k // kpc, k % kpc, j)),
        o_spec=pl.BlockSpec((tmb, tnb), lambda i, j, k: (i, j)),
        contract=(1, 0), acc_shape=(tmb, tnb), a_pre=_relu2_bf16,
        extras=(x1,), extra_specs=(pl.BlockSpec((tmb, tnb), lambda i, j, k: (i, j)),),
        epi=lambda r, res: r + res, deps=wg.deps())
    saved = dict(x=x, p=p, h=h, os=os, lses=lses, ycat=ycat, x1=x1, a=a, h2=h2)
    return x2, saved


def _layer_backward(l, dx2, dx2b, sv, prm, wg, sink):
    S, D = dx2.shape
    w_in, w_out = wg.get("w_in", l), wg.get("w_out", l)
    w_mlp_in, w_mlp_out = wg.get("w_mlp_in", l), wg.get("w_mlp_out", l)
    dff4 = w_mlp_in.shape[-1]
    dff = N_CHIPS * dff4
    tm = min(512, S)
    tk = min(1024, S)
    nks = S // tk

    tmb, tnb = min(1024, S), min(1024, D)
    da = _matmul(
        f"mlp_out_bwd_{l}", dx2b, w_mlp_out, (S, dff), BF16, grid=(S // tmb, N_CHIPS, 1),
        a_spec=pl.BlockSpec((tmb, D), lambda i, j, k: (i, 0)),
        b_spec=pl.BlockSpec((None, dff4, D), lambda i, j, k: (j, 0, 0)),
        o_spec=pl.BlockSpec((tmb, dff4), lambda i, j, k: (i, j)),
        contract=(1, 1), acc_shape=(tmb, dff4),
        extras=(sv["a"],), extra_specs=(pl.BlockSpec((tmb, dff4), lambda i, j, k: (i, j)),),
        epi=lambda r, act: r * (2.0 * jnp.maximum(act.astype(F32), 0.0)), deps=sink.deps())
    tmw = min(1024, dff4)
    mpc = dff4 // tmw
    g_w2 = _matmul(
        f"mlp_out_dw_{l}", sv["a"], dx2b, (N_CHIPS, dff4, D), F32, grid=(N_CHIPS * mpc, 1, nks),
        a_spec=pl.BlockSpec((tk, tmw), lambda i, j, k: (k, i)),
        b_spec=pl.BlockSpec((tk, D), lambda i, j, k: (k, 0)),
        o_spec=pl.BlockSpec((None, tmw, D), lambda i, j, k: (i // mpc, i % mpc, 0)),
        contract=(0, 0), acc_shape=(tmw, D), a_pre=_relu2_bf16)
    sink.begin("w_mlp_out", l, g_w2)
    dh2 = _matmul(
        f"mlp_in_bwd_{l}", da, w_mlp_in, (S, D), F32, grid=(S // tmb, D // tnb, N_CHIPS),
        a_spec=pl.BlockSpec((tmb, dff4), lambda i, j, k: (i, k)),
        b_spec=pl.BlockSpec((None, tnb, dff4), lambda i, j, k: (k, j, 0)),
        o_spec=pl.BlockSpec((tmb, tnb), lambda i, j, k: (i, j)),
        contract=(1, 1), acc_shape=(tmb, tnb), deps=sink.deps())
    sink.advance(dh2)
    tmd = min(1024, D)
    g_w1 = _matmul(
        f"mlp_in_dw_{l}", sv["h2"], da, (N_CHIPS, D, dff4), F32, grid=(N_CHIPS, D // tmd, nks),
        a_spec=pl.BlockSpec((tk, tmd), lambda i, j, k: (k, j)),
        b_spec=pl.BlockSpec((tk, dff4), lambda i, j, k: (k, i)),
        o_spec=pl.BlockSpec((None, tmd, dff4), lambda i, j, k: (i, j, 0)),
        contract=(0, 0), acc_shape=(tmd, dff4))
    sink.begin("w_mlp_in", l, g_w1)
    dx1, dx1b, g_mlp_norm = _rmsnorm_bwd(f"mlp_norm_bwd_{l}", dh2, sv["x1"], prm["mlp_norm"][l], dx2,
                                         deps=sink.deps())

    rq = w_out.shape[1]
    dycat = _matmul(
        f"out_proj_bwd_{l}", dx1b, w_out, (S, N_CHIPS * rq), F32, grid=(S // tmb, N_CHIPS, 1),
        a_spec=pl.BlockSpec((tmb, D), lambda i, j, k: (i, 0)),
        b_spec=pl.BlockSpec((None, rq, D), lambda i, j, k: (j, 0, 0)),
        o_spec=pl.BlockSpec((tmb, rq), lambda i, j, k: (i, j)),
        contract=(1, 1), acc_shape=(tmb, rq))
    sink.advance(dycat)
    g_wout = _matmul(
        f"out_proj_dw_{l}", sv["ycat"], dx1b, (N_CHIPS, rq, D), F32, grid=(N_CHIPS, 1, nks),
        a_spec=pl.BlockSpec((tk, rq), lambda i, j, k: (k, i)),
        b_spec=pl.BlockSpec((tk, D), lambda i, j, k: (k, 0)),
        o_spec=pl.BlockSpec((None, rq, D), lambda i, j, k: (i, 0, 0)),
        contract=(0, 0), acc_shape=(rq, D))
    sink.begin("w_out", l, g_wout)

    p = sv["p"]
    du, dv_a, g_sgu_w, db_lanes = _sgu_bwd(f"sgu_bwd_{l}", p, dycat, prm["sgu_wt"][l], prm["sgu_wtt"][l],
                                           prm["sgu_bb"][l])
    g_sgu_b = db_lanes[:, :A_HEADS].T
    db, dc, dxb, g_conv = _conv_bwd(f"conv_bwd_{l}", p, dycat, prm["conv_w"][l])
    do3, c3 = _mix_bwd(f"mix_bwd_{l}", sv["os"], sv["lses"], dycat, prm["bd"])
    dqs, dks, dvs, dgqs, dgks = [], [], [], [], []
    for g in range(N_PATTERNS):
        dq, dk, dv, dgq, dgk = _attn_bwd(f"attn_bwd_{l}_{g}", p, g, sv["lses"][g], do3, c3,
                                         prm["q_gain"][l], prm["k_gain"][l], prm["bd"])
        dqs.append(dq)
        dks.append(dk)
        dvs.append(dv)
        dgqs.append(dgq)
        dgks.append(dgk)
    g_q = jnp.concatenate(dgqs, axis=1).reshape(N_PATTERNS * PW // HEAD_DIM, HEAD_DIM).sum(axis=0)
    g_k = jnp.concatenate(dgks, axis=1).reshape(N_PATTERNS * PW // HEAD_DIM, HEAD_DIM).sum(axis=0)
    dp = jnp.concatenate([du, dv_a, db, dc, dxb] + [t.astype(BF16) for t in dqs + dks + dvs], axis=1)

    ns_in = w_in.shape[-1]
    g_win = _matmul(
        f"in_proj_dw_{l}", sv["h"], dp, (N_CHIPS, D, ns_in), F32, grid=(N_CHIPS, D // tmd, nks),
        a_spec=pl.BlockSpec((tk, tmd), lambda i, j, k: (k, j)),
        b_spec=pl.BlockSpec((tk, ns_in), lambda i, j, k: (k, i)),
        o_spec=pl.BlockSpec((None, tmd, ns_in), lambda i, j, k: (i, j, 0)),
        contract=(0, 0), acc_shape=(tmd, ns_in))
    sink.begin("w_in", l, g_win)
    dh = _matmul(
        f"in_proj_bwd_{l}", dp, w_in, (S, D), F32, grid=(S // tmb, D // tnb, N_CHIPS),
        a_spec=pl.BlockSpec((tmb, ns_in), lambda i, j, k: (i, k)),
        b_spec=pl.BlockSpec((None, tnb, ns_in), lambda i, j, k: (k, j, 0)),
        o_spec=pl.BlockSpec((tmb, tnb), lambda i, j, k: (i, j)),
        contract=(1, 1), acc_shape=(tmb, tnb), deps=sink.deps())
    sink.advance(dh)
    dx0, dx0b, g_attn_norm = _rmsnorm_bwd(f"attn_norm_bwd_{l}", dh, sv["x"], prm["attn_norm"][l], dx1,
                                          deps=sink.deps())

    big = dict(w_in=g_win, w_out=g_wout, w_mlp_in=g_w1, w_mlp_out=g_w2)
    small = dict(attn_norm=g_attn_norm.reshape(-1), sgu_w=g_sgu_w, sgu_b=g_sgu_b, conv_w=g_conv,
                 q_norm=g_q, k_norm=g_k, mlp_norm=g_mlp_norm.reshape(-1))
    return dx0, dx0b, big, small


BIG = ("w_in", "w_out", "w_mlp_in", "w_mlp_out")
SMALL_REPLICATED = ("attn_norm", "sgu_w", "sgu_b", "q_norm", "k_norm", "mlp_norm")


def _local_step(x, target, prm, wg, n_layers, sink):
    saved = []
    h = x
    for l in range(n_layers):
        h, sv = _layer_forward(l, h, prm, wg)
        saved.append(sv)
    dy, dyb, colsq = _loss_kernel(h, target)
    loss = 0.5 * jnp.sum(colsq) / x.shape[1]
    bigs, smalls = [None] * n_layers, [None] * n_layers
    for l in reversed(range(n_layers)):
        dy, dyb, bigs[l], smalls[l] = _layer_backward(l, dy, dyb, saved[l], prm, wg, sink)
    return loss, dy, bigs, smalls


def _prepare_params(attn_norm, sgu_w, sgu_b, conv_full, q_norm, k_norm, mlp_norm):
    n_layers = attn_norm.shape[0]
    tri = jnp.tril(sgu_w)
    idx = jnp.arange(PW)
    bd = (idx[:, None] // HEAD_DIM == idx[None, :] // HEAD_DIM).astype(BF16)
    return dict(
        attn_norm=[attn_norm[l][None, :] for l in range(n_layers)],
        mlp_norm=[mlp_norm[l][None, :] for l in range(n_layers)],
        sgu_wt=[tri[l].astype(BF16) for l in range(n_layers)],
        sgu_wtt=[tri[l].transpose(0, 2, 1).astype(BF16) for l in range(n_layers)],
        sgu_bb=[jnp.repeat(sgu_b[l].T, HEAD_DIM, axis=1) for l in range(n_layers)],
        conv_w=[conv_full[l] for l in range(n_layers)],
        q_gain=[jnp.tile(q_norm[l], PW // HEAD_DIM)[None, :] for l in range(n_layers)],
        k_gain=[jnp.tile(k_norm[l], PW // HEAD_DIM)[None, :] for l in range(n_layers)],
        bd=bd,
    )


def kernel(x, attn_norm, w_in, sgu_w, sgu_b, conv_w, q_norm, k_norm, w_out, mlp_norm, w_mlp_in, w_mlp_out, loss_target, m_attn_norm, m_w_in, m_sgu_w, m_sgu_b, m_conv_w, m_q_norm, m_k_norm, m_w_out, m_mlp_norm, m_w_mlp_in, m_w_mlp_out, v_attn_norm, v_w_in, v_sgu_w, v_sgu_b, v_conv_w, v_q_norm, v_k_norm, v_w_out, v_mlp_norm, v_w_mlp_in, v_w_mlp_out):
    n_layers = attn_norm.shape[0]
    weights = dict(attn_norm=attn_norm, w_in=w_in, sgu_w=sgu_w, sgu_b=sgu_b, conv_w=conv_w, q_norm=q_norm,
                   k_norm=k_norm, w_out=w_out, mlp_norm=mlp_norm, w_mlp_in=w_mlp_in, w_mlp_out=w_mlp_out)
    mom_m = dict(attn_norm=m_attn_norm, w_in=m_w_in, sgu_w=m_sgu_w, sgu_b=m_sgu_b, conv_w=m_conv_w,
                 q_norm=m_q_norm, k_norm=m_k_norm, w_out=m_w_out, mlp_norm=m_mlp_norm, w_mlp_in=m_w_mlp_in,
                 w_mlp_out=m_w_mlp_out)
    mom_v = dict(attn_norm=v_attn_norm, w_in=v_w_in, sgu_w=v_sgu_w, sgu_b=v_sgu_b, conv_w=v_conv_w,
                 q_norm=v_q_norm, k_norm=v_k_norm, w_out=v_w_out, mlp_norm=v_mlp_norm, w_mlp_in=v_w_mlp_in,
                 w_mlp_out=v_w_mlp_out)
    order = ("attn_norm", "w_in", "sgu_w", "sgu_b", "conv_w", "q_norm", "k_norm", "w_out", "mlp_norm",
             "w_mlp_in", "w_mlp_out")
    chip = 2 * lax.axis_index("x") + lax.axis_index("y")
    c_arr = lax.axis_index("c").astype(jnp.int32).reshape(1)

    conv_cols = conv_w.shape[-1]
    chip_arr = chip.astype(jnp.int32).reshape(1)
    conv_pack = jnp.pad(conv_w.reshape(-1), (0, 2048 - conv_w.size)).reshape(1, 16, 128)
    wg = _GatheredWeights()
    wg.start([("conv_w", 0), ("w_in", 0)],
             [_place_shard("place_conv_w", conv_pack, 0, chip_arr, F32),
              _place_shard("place_w_in_0", weights["w_in"], 0, chip_arr, BF16)])
    keys = [(n, l) for l in range(n_layers) for n in BIG if (n, l) != ("w_in", 0)]
    first = wg.deps()
    wg.start(keys, [_place_shard(f"place_{n}_{l}", weights[n], l, chip_arr, BF16, deps=first) for n, l in keys])
    conv_full = wg.get("conv_w", 0, wg.deps()[-1]).reshape(N_CHIPS, 2048)[:, :conv_w.size].reshape(N_CHIPS, n_layers, 3, conv_cols)
    conv_full = conv_full.transpose(1, 2, 0, 3).reshape(n_layers, 3, N_CHIPS * conv_cols)
    prm = _prepare_params(attn_norm, sgu_w, sgu_b, conv_full, q_norm, k_norm, mlp_norm)

    sink = _GradReducer(c_arr)
    loss_local, grad_x, _, smalls = _local_step(x[0], loss_target[0], prm, wg, n_layers, sink)
    loss = lax.psum(loss_local, ("x", "y", "c"))

    small_names = SMALL_REPLICATED + ("conv_w",)
    small_shapes = [(n_layers,) + tuple(smalls[0][n].shape) for n in small_names]
    packed = _pack_rows([jnp.stack([smalls[l][n] for l in range(n_layers)]) for n in small_names])
    small_send, small_recv, small_land, small_token = _small_start(packed, sink.deps())

    grads, delta, new_m, new_v = {}, {}, {}, {}

    def update(names, after):
        joined = sink.finish(names, n_layers, after)
        for n in names:
            shp = weights[n].shape
            two_d = (shp[0] * shp[1], shp[2])
            d, nm, nv, g = _adamw(f"adamw_{n}", weights[n].reshape(two_d), joined[n].reshape(two_d),
                                  mom_m[n].reshape(two_d), mom_v[n].reshape(two_d))
            grads[n], delta[n], new_m[n], new_v[n] = g.reshape(shp), d.reshape(shp), nm.reshape(shp), nv.reshape(shp)

    update(("w_mlp_out", "w_mlp_in", "w_out"), small_token)
    update(("w_in",), delta["w_out"])
    small_land = _small_wait(packed, small_land, small_send, small_recv, delta["w_in"])
    grads.update(zip(small_names, _unpack_rows(_sum_devices(small_land), small_shapes)))
    grads["conv_w"] = lax.dynamic_slice_in_dim(grads["conv_w"], chip * conv_cols, conv_cols, axis=2)
    smalls_all = SMALL_REPLICATED + ("conv_w",)
    shapes = [weights[n].shape for n in smalls_all]
    d, nm, nv, _ = _adamw("adamw_small",
                       _pack_rows([weights[n] for n in smalls_all]), _pack_rows([grads[n] for n in smalls_all]),
                       _pack_rows([mom_m[n] for n in smalls_all]), _pack_rows([mom_v[n] for n in smalls_all]))
    for n, dd, mm, vv in zip(smalls_all, _unpack_rows(d, shapes), _unpack_rows(nm, shapes), _unpack_rows(nv, shapes)):
        delta[n], new_m[n], new_v[n] = dd, mm, vv

    return (loss, grad_x[None], *[grads[n] for n in order], *[delta[n] for n in order],
            *[new_m[n] for n in order], *[new_v[n] for n in order])
```

```python
import jax
import jax.numpy as jnp
from jax import lax
from jax.experimental import pallas as pl
from jax.experimental.pallas import tpu as pltpu

F32 = jnp.float32
BF16 = jnp.bfloat16
SDS = jax.ShapeDtypeStruct

EPS = 1e-6
HEAD_DIM = 64
A_HEADS = 8
A_WIDTH = 512
CHUNK = 128
B_WIDTH = 768
C_WIDTH = 768
N_PATTERNS = 3
PATTERN_DILATION = (1, 4, 16)
PW = 256
D_IN_PROJ = 5632
OFF_AU, OFF_AV, OFF_BB, OFF_BC, OFF_BX, OFF_Q, OFF_K, OFF_V = 0, 512, 1024, 1792, 2560, 3328, 4096, 4864
N_CHIPS = 4
N_DEV = 8
BLK = 128

ADAM_LR, ADAM_B1, ADAM_B2, ADAM_EPS, ADAM_WD, ADAM_STEP = 0.001, 0.9, 0.999, 1e-08, 0.01, 10

V7X_VMEM_LIMIT = 56 * 1024 * 1024
MESH = pl.DeviceIdType.MESH
NEG = -1e30


def _cp(n_axes):
    return pltpu.CompilerParams(dimension_semantics=("arbitrary",) * n_axes, vmem_limit_bytes=V7X_VMEM_LIMIT)


def _hbm_spec():
    return pl.BlockSpec(memory_space=pl.ANY)


def _norm_matmul(name, x, g, wg, out_dtype, deps=()):
    S, D = x.shape
    ns, _, Ns = wg.shape
    tm = min(512, S)
    n_dep = len(deps)

    def body(x_ref, g_ref, w_ref, *rest):
        o_ref, h_ref, hs_ref = rest[n_dep:]
        @pl.when(pl.program_id(1) == 0)
        def _():
            xv = x_ref[...]
            y = xv * lax.rsqrt(jnp.mean(xv * xv, axis=-1, keepdims=True) + EPS) * g_ref[...]
            hb = y.astype(BF16)
            hs_ref[...] = hb
            h_ref[...] = hb
        o_ref[...] = jnp.dot(hs_ref[...], w_ref[...], preferred_element_type=F32).astype(o_ref.dtype)

    return pl.pallas_call(
        body, name=name, grid=(S // tm, ns),
        in_specs=[pl.BlockSpec((tm, D), lambda i, s: (i, 0)),
                  pl.BlockSpec((1, D), lambda i, s: (0, 0)),
                  pl.BlockSpec((None, D, Ns), lambda i, s: (s, 0, 0))] + [_hbm_spec()] * n_dep,
        out_specs=[pl.BlockSpec((tm, Ns), lambda i, s: (i, s)),
                   pl.BlockSpec((tm, D), lambda i, s: (i, 0))],
        out_shape=[SDS((S, ns * Ns), out_dtype), SDS((S, D), BF16)],
        scratch_shapes=[pltpu.VMEM((tm, D), BF16)],
        compiler_params=_cp(2),
    )(x, g, wg, *deps)


def _matmul(name, a, b, out_shape, out_dtype, *, grid, a_spec, b_spec, o_spec, contract, acc_shape,
            extras=(), extra_specs=(), a_pre=None, epi=None, deps=()):
    nk = grid[2]
    n_ex = len(extras)
    n_dep = len(deps)
    dims = (((contract[0],), (contract[1],)), ((), ()))

    def product(a_ref, b_ref):
        av = a_ref[...]
        if a_pre is not None:
            av = a_pre(av)
        return lax.dot_general(av, b_ref[...], dims, preferred_element_type=F32)

    def finish(r, ex, o_ref):
        if epi is not None:
            r = epi(r, *[e[...] for e in ex])
        o_ref[...] = r.astype(o_ref.dtype)

    def body_single(a_ref, b_ref, *rest):
        finish(product(a_ref, b_ref), rest[:n_ex], rest[n_ex + n_dep])

    def body(a_ref, b_ref, *rest):
        ex = rest[:n_ex]
        o_ref = rest[n_ex + n_dep]
        acc_ref = rest[n_ex + n_dep + 1]
        k = pl.program_id(2)

        @pl.when(k == 0)
        def _():
            acc_ref[...] = product(a_ref, b_ref)

        @pl.when((k > 0) & (k < nk - 1))
        def _():
            acc_ref[...] += product(a_ref, b_ref)

        @pl.when(k == nk - 1)
        def _():
            finish(acc_ref[...] + product(a_ref, b_ref), ex, o_ref)

    return pl.pallas_call(
        body_single if nk == 1 else body, name=name, grid=grid,
        in_specs=[a_spec, b_spec, *extra_specs] + [_hbm_spec()] * n_dep,
        out_specs=o_spec,
        out_shape=SDS(out_shape, out_dtype),
        scratch_shapes=[] if nk == 1 else [pltpu.VMEM(acc_shape, F32)],
        compiler_params=_cp(3),
    )(a, b, *extras, *deps)


def _relu2_bf16(t):
    r = jnp.maximum(t.astype(F32), 0.0)
    return (r * r).astype(BF16)


def _loss_kernel(y, t):
    S, D = y.shape
    tm = min(256, S)

    def body(y_ref, t_ref, dy_ref, dyb_ref, l_ref):
        @pl.when(pl.program_id(0) == 0)
        def _():
            l_ref[...] = jnp.zeros_like(l_ref)
        e = y_ref[...] - t_ref[...]
        l_ref[...] += jnp.sum(e * e, axis=0, keepdims=True)
        dy = e * (1.0 / D)
        dy_ref[...] = dy
        dyb_ref[...] = dy.astype(BF16)

    row = pl.BlockSpec((tm, D), lambda i: (i, 0))
    return pl.pallas_call(
        body, name="loss_head", grid=(S // tm,),
        in_specs=[row, row],
        out_specs=[row, row, pl.BlockSpec((1, D), lambda i: (0, 0))],
        out_shape=[SDS((S, D), F32), SDS((S, D), BF16), SDS((1, D), F32)],
        compiler_params=_cp(1),
    )(y, t)


def _rmsnorm_bwd(name, dh, x, g, dres, deps=()):
    S, D = x.shape
    tm = min(256, S)
    n_dep = len(deps)

    def body(dh_ref, x_ref, g_ref, dres_ref, *rest):
        dx_ref, dxb_ref, dg_ref = rest[n_dep:]
        @pl.when(pl.program_id(0) == 0)
        def _():
            dg_ref[...] = jnp.zeros_like(dg_ref)
        xv = x_ref[...]
        dhv = dh_ref[...]
        rstd = lax.rsqrt(jnp.mean(xv * xv, axis=-1, keepdims=True) + EPS)
        xhat = xv * rstd
        dg_ref[...] += jnp.sum(dhv * xhat, axis=0, keepdims=True)
        dxn = dhv * g_ref[...]
        dx = dres_ref[...] + rstd * (dxn - xhat * jnp.mean(dxn * xhat, axis=-1, keepdims=True))
        dx_ref[...] = dx
        dxb_ref[...] = dx.astype(BF16)

    row = pl.BlockSpec((tm, D), lambda i: (i, 0))
    vec = pl.BlockSpec((1, D), lambda i: (0, 0))
    return pl.pallas_call(
        body, name=name, grid=(S // tm,),
        in_specs=[row, row, vec, row] + [_hbm_spec()] * n_dep,
        out_specs=[row, row, vec],
        out_shape=[SDS((S, D), F32), SDS((S, D), BF16), SDS((1, D), F32)],
        compiler_params=_cp(1),
    )(dh, x, g, dres, *deps)


def _adamw(name, w, g, m, v):
    R, C = w.shape
    tr = 256 if R % 256 == 0 else R
    c1 = 1.0 - ADAM_B1 ** ADAM_STEP
    c2 = 1.0 - ADAM_B2 ** ADAM_STEP

    def body(w_ref, g_ref, m_ref, v_ref, d_ref, nm_ref, nv_ref, g_out_ref):
        gv = g_ref[...]
        nm = ADAM_B1 * m_ref[...] + (1.0 - ADAM_B1) * gv
        nv = ADAM_B2 * v_ref[...] + (1.0 - ADAM_B2) * (gv * gv)
        m_hat = nm / c1
        v_hat = nv / c2
        d_ref[...] = -ADAM_LR * (m_hat / (jnp.sqrt(v_hat) + ADAM_EPS) + ADAM_WD * w_ref[...])
        nm_ref[...] = nm
        nv_ref[...] = nv
        g_out_ref[...] = gv

    blk = pl.BlockSpec((tr, C), lambda i: (i, 0))
    return pl.pallas_call(
        body, name=name, grid=(R // tr,),
        in_specs=[blk] * 4, out_specs=[blk] * 4,
        out_shape=[SDS((R, C), F32)] * 4,
        compiler_params=_cp(1),
    )(w, g, m, v)


SGU_STEP_ROWS = 512


def _pair_select(lane, lo, hi):
    return jnp.where(lane < HEAD_DIM, lo, hi)


def _sgu_fwd(name, p, wt, bb):
    S = p.shape[0]

    rows = min(SGU_STEP_ROWS, S)

    def body(u_ref, v_ref, wt_ref, bb_ref, o_ref):
        lane = lax.broadcasted_iota(jnp.int32, (CHUNK, 128), 1)
        for ci in range(rows // CHUNK):
            rs = slice(CHUNK * ci, CHUNK * (ci + 1))
            for pp in range(A_HEADS // 2):
                cs = slice(128 * pp, 128 * (pp + 1))
                vb = v_ref[rs, cs].astype(BF16)
                mixed = _pair_select(lane,
                                     jnp.dot(wt_ref[2 * pp], vb, preferred_element_type=F32),
                                     jnp.dot(wt_ref[2 * pp + 1], vb, preferred_element_type=F32)) + bb_ref[:, cs]
                o_ref[rs, cs] = (u_ref[rs, cs] * mixed).astype(o_ref.dtype)

    return pl.pallas_call(
        body, name=name, grid=(S // rows,),
        in_specs=[pl.BlockSpec((rows, A_WIDTH), lambda c: (c, OFF_AU // A_WIDTH)),
                  pl.BlockSpec((rows, A_WIDTH), lambda c: (c, OFF_AV // A_WIDTH)),
                  pl.BlockSpec((A_HEADS, CHUNK, CHUNK), lambda c: (0, 0, 0)),
                  pl.BlockSpec((CHUNK, A_WIDTH), lambda c: (0, 0))],
        out_specs=pl.BlockSpec((rows, A_WIDTH), lambda c: (c, 0)),
        out_shape=SDS((S, A_WIDTH), BF16),
        compiler_params=_cp(1),
    )(p, p, wt, bb)


def _sgu_bwd(name, p, dycat, wt, wtt, bb):
    S = p.shape[0]
    rows = min(SGU_STEP_ROWS, S)

    def body(u_ref, v_ref, dy_ref, wt_ref, wtt_ref, bb_ref, du_ref, dv_ref, dw_ref, db_ref, dbacc_ref):
        c = pl.program_id(0)

        @pl.when(c == 0)
        def _():
            dw_ref[...] = jnp.zeros_like(dw_ref)
            dbacc_ref[...] = jnp.zeros_like(dbacc_ref)

        lane = lax.broadcasted_iota(jnp.int32, (CHUNK, 128), 1)
        row = lax.broadcasted_iota(jnp.int32, (CHUNK, 128), 0)
        causal = row >= lane
        nt = (((1,), (1,)), ((), ()))
        for pp in range(A_HEADS // 2):
            cs = slice(128 * pp, 128 * (pp + 1))
            dw_lo = jnp.zeros((CHUNK, CHUNK), F32)
            dw_hi = jnp.zeros((CHUNK, CHUNK), F32)
            dm_sum = jnp.zeros((CHUNK, 128), F32)
            for ci in range(rows // CHUNK):
                rs = slice(CHUNK * ci, CHUNK * (ci + 1))
                vb = v_ref[rs, cs].astype(BF16)
                dy = dy_ref[rs, cs]
                mixed = _pair_select(lane,
                                     jnp.dot(wt_ref[2 * pp], vb, preferred_element_type=F32),
                                     jnp.dot(wt_ref[2 * pp + 1], vb, preferred_element_type=F32)) + bb_ref[:, cs]
                du_ref[rs, cs] = (dy * mixed).astype(du_ref.dtype)
                dm = dy * u_ref[rs, cs]
                dmb = dm.astype(BF16)
                dv = _pair_select(lane,
                                  jnp.dot(wtt_ref[2 * pp], dmb, preferred_element_type=F32),
                                  jnp.dot(wtt_ref[2 * pp + 1], dmb, preferred_element_type=F32))
                dv_ref[rs, cs] = dv.astype(dv_ref.dtype)
                dm_sum += dm
                dm_lo = jnp.where(lane < HEAD_DIM, dm, 0.0).astype(BF16)
                dm_hi = jnp.where(lane >= HEAD_DIM, dm, 0.0).astype(BF16)
                dw_lo += lax.dot_general(dm_lo, vb, nt, preferred_element_type=F32)
                dw_hi += lax.dot_general(dm_hi, vb, nt, preferred_element_type=F32)
            dbacc_ref[:, cs] += dm_sum
            dw_ref[2 * pp] += jnp.where(causal, dw_lo, 0.0)
            dw_ref[2 * pp + 1] += jnp.where(causal, dw_hi, 0.0)

        @pl.when(c == S // rows - 1)
        def _():
            out = jnp.zeros((CHUNK, 128), F32)
            for pp in range(A_HEADS // 2):
                acc = dbacc_ref[:, 128 * pp:128 * (pp + 1)]
                s_lo = jnp.sum(jnp.where(lane < HEAD_DIM, acc, 0.0), axis=1, keepdims=True)
                s_hi = jnp.sum(jnp.where(lane >= HEAD_DIM, acc, 0.0), axis=1, keepdims=True)
                out = jnp.where(lane == 2 * pp, s_lo, out)
                out = jnp.where(lane == 2 * pp + 1, s_hi, out)
            db_ref[...] = out

    chunk = lambda col: pl.BlockSpec((rows, A_WIDTH), lambda c: (c, col))
    wspec = pl.BlockSpec((A_HEADS, CHUNK, CHUNK), lambda c: (0, 0, 0))
    return pl.pallas_call(
        body, name=name, grid=(S // rows,),
        in_specs=[chunk(OFF_AU // A_WIDTH), chunk(OFF_AV // A_WIDTH), chunk(0), wspec, wspec,
                  pl.BlockSpec((CHUNK, A_WIDTH), lambda c: (0, 0))],
        out_specs=[chunk(0), chunk(0), wspec, pl.BlockSpec((CHUNK, 128), lambda c: (0, 0))],
        out_shape=[SDS((S, A_WIDTH), BF16), SDS((S, A_WIDTH), BF16),
                   SDS((A_HEADS, CHUNK, CHUNK), F32), SDS((CHUNK, 128), F32)],
        scratch_shapes=[pltpu.VMEM((CHUNK, A_WIDTH), F32)],
        compiler_params=_cp(1),
    )(p, p, dycat, wt, wtt, bb)


CONV_HALO = 8
CONV_COLS = 256
CONV_ROWS = 1024


def _shift_down(a, halo, k):
    T = a.shape[0]
    row = lax.broadcasted_iota(jnp.int32, a.shape, 0)
    out = pltpu.roll(a, k, 0)
    for r in range(k):
        out = jnp.where(row == r, halo[CONV_HALO - k + r:CONV_HALO - k + r + 1, :], out)
    return out


def _shift_up(a, halo, k):
    T = a.shape[0]
    row = lax.broadcasted_iota(jnp.int32, a.shape, 0)
    out = pltpu.roll(a, T - k, 0)
    for r in range(k):
        out = jnp.where(row == T - k + r, halo[r:r + 1, :], out)
    return out


def _conv_specs(S, T):
    hb = T // CONV_HALO
    last = S // CONV_HALO - 1
    tile = lambda col0: pl.BlockSpec((T, CONV_COLS), lambda j, i: (i, col0 + j))
    prev = lambda col0: pl.BlockSpec((CONV_HALO, CONV_COLS), lambda j, i: (jnp.maximum(i * hb - 1, 0), col0 + j))
    nxt = lambda col0: pl.BlockSpec((CONV_HALO, CONV_COLS), lambda j, i: (jnp.minimum((i + 1) * hb, last), col0 + j))
    return tile, prev, nxt


def _conv_fwd(name, p, w):
    S = p.shape[0]
    T = min(CONV_ROWS, S)
    tile, prev, _ = _conv_specs(S, T)
    cb, cc, cx = OFF_BB // CONV_COLS, OFF_BC // CONV_COLS, OFF_BX // CONV_COLS

    def body(b_ref, c_ref, x_ref, ch_ref, xh_ref, w_ref, o_ref):
        i = pl.program_id(1)
        z = c_ref[...] * x_ref[...]
        zh = jnp.where(i > 0, ch_ref[...] * xh_ref[...], 0.0)
        z1 = _shift_down(z, zh, 1)
        z2 = _shift_down(z, zh, 2)
        conv = w_ref[0:1, :] * z2 + w_ref[1:2, :] * z1 + w_ref[2:3, :] * z
        o_ref[...] = (b_ref[...] * conv).astype(o_ref.dtype)

    return pl.pallas_call(
        body, name=name, grid=(B_WIDTH // CONV_COLS, S // T),
        in_specs=[tile(cb), tile(cc), tile(cx), prev(cc), prev(cx),
                  pl.BlockSpec((3, CONV_COLS), lambda j, i: (0, j))],
        out_specs=tile(0),
        out_shape=SDS((S, B_WIDTH), BF16),
        compiler_params=_cp(2),
    )(p, p, p, p, p, w)


def _conv_bwd(name, p, dycat, w):
    S = p.shape[0]
    T = min(CONV_ROWS, S)
    tile, prev, nxt = _conv_specs(S, T)
    cb, cc, cx = OFF_BB // CONV_COLS, OFF_BC // CONV_COLS, OFF_BX // CONV_COLS
    cdy = A_WIDTH // CONV_COLS
    n_i = S // T

    def body(b_ref, c_ref, x_ref, dy_ref, ch_ref, xh_ref, bn_ref, dyn_ref, w_ref,
             db_ref, dc_ref, dx_ref, dw_ref):
        i = pl.program_id(1)

        @pl.when(i == 0)
        def _():
            dw_ref[...] = jnp.zeros_like(dw_ref)

        cv = c_ref[...]
        xv = x_ref[...]
        z = cv * xv
        zh = jnp.where(i > 0, ch_ref[...] * xh_ref[...], 0.0)
        z1 = _shift_down(z, zh, 1)
        z2 = _shift_down(z, zh, 2)
        w0, w1, w2 = w_ref[0:1, :], w_ref[1:2, :], w_ref[2:3, :]
        conv = w0 * z2 + w1 * z1 + w2 * z
        dy = dy_ref[...]
        db_ref[...] = (dy * conv).astype(db_ref.dtype)
        dconv = dy * b_ref[...]
        dconv_n = jnp.where(i < n_i - 1, dyn_ref[...] * bn_ref[...], 0.0)
        dz = w2 * dconv + w1 * _shift_up(dconv, dconv_n, 1) + w0 * _shift_up(dconv, dconv_n, 2)
        dc_ref[...] = (dz * xv).astype(dc_ref.dtype)
        dx_ref[...] = (dz * cv).astype(dx_ref.dtype)
        dw_ref[0:1, :] += jnp.sum(dconv * z2, axis=0, keepdims=True)
        dw_ref[1:2, :] += jnp.sum(dconv * z1, axis=0, keepdims=True)
        dw_ref[2:3, :] += jnp.sum(dconv * z, axis=0, keepdims=True)

    wspec = pl.BlockSpec((3, CONV_COLS), lambda j, i: (0, j))
    return pl.pallas_call(
        body, name=name, grid=(B_WIDTH // CONV_COLS, n_i),
        in_specs=[tile(cb), tile(cc), tile(cx), tile(cdy), prev(cc), prev(cx), nxt(cb), nxt(cdy), wspec],
        out_specs=[tile(0), tile(0), tile(0), wspec],
        out_shape=[SDS((S, B_WIDTH), BF16)] * 3 + [SDS((3, B_WIDTH), F32)],
        compiler_params=_cp(2),
    )(p, p, p, dycat, p, p, p, dycat, w)


def _seg_sum(t, bd):
    hi = t.astype(BF16)
    lo = (t - hi.astype(F32)).astype(BF16)
    return jnp.dot(hi, bd, preferred_element_type=F32) + jnp.dot(lo, bd, preferred_element_type=F32)


def _head_norm(x, g, bd):
    rstd = lax.rsqrt(_seg_sum(x * x, bd) * (1.0 / HEAD_DIM) + EPS)
    xhat = x * rstd
    return xhat * g, xhat, rstd


def _head_norm_bwd(dy, g, xhat, rstd, bd):
    dxh = dy * g
    return rstd * (dxh - xhat * (_seg_sum(dxh * xhat, bd) * (1.0 / HEAD_DIM)))


def _band_mask(has_prev):
    row = lax.broadcasted_iota(jnp.int32, (BLK, 2 * BLK), 0)
    col = lax.broadcasted_iota(jnp.int32, (BLK, 2 * BLK), 1)
    first_key = jnp.where(has_prev, 0, BLK)
    return (col >= row) & (col <= row + BLK) & (col >= first_key)


def _first_of_segment(g, n, n_blocks):
    per_seg = lax.shift_right_logical(jnp.int32(n_blocks), 2 * g)
    return (n & (per_seg - 1)) == 0


def _residue_rows(r, d):
    return slice(None) if d == 1 else pl.ds(r, BLK, stride=d)


STRIDED_LANES = 128


def _step_width(d):
    return PW if d == 1 else STRIDED_LANES


def _n_stack(lane):
    return lane.shape[1] // HEAD_DIM


def _for_residues(d, fn):
    if d == 1:
        fn(0)
    else:
        def two(i, carry):
            fn(2 * i)
            fn(2 * i + 1)
            return carry
        lax.fori_loop(0, d // 2, two, 0)


def _head_mask(lane, j):
    return (lane >= HEAD_DIM * j) & (lane < HEAD_DIM * (j + 1))


def _stack_heads(x, lane):
    return jnp.concatenate([jnp.where(_head_mask(lane, j), x, 0.0) for j in range(_n_stack(lane))], axis=0)


def _unstack_heads(y, lane):
    out = y[:BLK]
    for j in range(1, _n_stack(lane)):
        out = jnp.where(lane >= HEAD_DIM * j, y[BLK * j:BLK * (j + 1)], out)
    return out


def _head_columns(v, lane):
    return jnp.concatenate([jnp.max(jnp.where(_head_mask(lane, j), v, NEG), axis=1, keepdims=True)
                            for j in range(_n_stack(lane))], axis=0)


def _attn_fwd(name, p, g, gq, gk, bd):
    S = p.shape[0]
    d = PATTERN_DILATION[g]
    rows = BLK * d
    hw = _step_width(d)
    nt = (((1,), (1,)), ((), ()))

    def body(q_ref, kc_ref, kp_ref, vc_ref, vp_ref, gq_ref, gk_ref, bd_ref, o_ref, lse_ref):
        has_prev = pl.program_id(1) > 0
        bdv = bd_ref[...]
        band = jnp.concatenate([_band_mask(has_prev)] * (hw // HEAD_DIM), axis=0)
        lane = lax.broadcasted_iota(jnp.int32, (1, hw), 1)

        def residue(r):
            rr = _residue_rows(r, d)
            qn, _, _ = _head_norm(q_ref[rr, :], gq_ref[...], bdv)
            kn, _, _ = _head_norm(jnp.concatenate([kp_ref[rr, :], kc_ref[rr, :]], axis=0), gk_ref[...], bdv)
            knb = kn.astype(BF16)
            vb = jnp.concatenate([vp_ref[rr, :], vc_ref[rr, :]], axis=0).astype(BF16)
            qs = _stack_heads(qn, lane).astype(BF16)
            s = lax.dot_general(qs, knb, nt, preferred_element_type=F32) * (HEAD_DIM ** -0.5)
            s = jnp.where(band, s, NEG)
            m = jnp.max(s, axis=1, keepdims=True)
            e = jnp.exp(s - m)
            den = jnp.sum(e, axis=1, keepdims=True)
            pv = jnp.dot(e.astype(BF16), vb, preferred_element_type=F32)
            o_ref[rr, :] = _unstack_heads(pv / den, lane)
            lse_ref[rr, :] = _unstack_heads(jnp.broadcast_to(m + jnp.log(den), pv.shape), lane)

        _for_residues(d, residue)

    per = PW // hw
    cq, ck, cv = (OFF_Q + PW * g) // hw, (OFF_K + PW * g) // hw, (OFF_V + PW * g) // hw
    cur = lambda col: pl.BlockSpec((rows, hw), lambda h, n: (n, col + h))
    prv = lambda col: pl.BlockSpec((rows, hw), lambda h, n: (jnp.maximum(n - 1, 0), col + h))
    vec = pl.BlockSpec((1, hw), lambda h, n: (0, h))
    return pl.pallas_call(
        body, name=name, grid=(per, S // rows),
        in_specs=[cur(cq), cur(ck), prv(ck), cur(cv), prv(cv), vec, vec, pl.BlockSpec((hw, hw), lambda h, n: (0, 0))],
        out_specs=[cur(0), cur(0)],
        out_shape=[SDS((S, PW), F32)] * 2,
        compiler_params=_cp(2),
    )(p, p, p, p, p, gq, gk, bd)


def _attn_bwd(name, p, g, lse, do3, c3, gq, gk, bd):
    S = p.shape[0]
    d = PATTERN_DILATION[g]
    rows = BLK * d
    nblk = S // rows
    hw = _step_width(d)
    nt = (((1,), (1,)), ((), ()))
    tn = (((0,), (0,)), ((), ()))

    def body(q_ref, kc_ref, kp_ref, vc_ref, vp_ref, lse_ref, do_ref, c_ref, gq_ref, gk_ref, bd_ref,
             dq_ref, dk_ref, dv_ref, dgq_ref, dgk_ref, ck_ref, cv_ref, dq_keep_ref):
        n = pl.program_id(1)

        @pl.when(n == 0)
        def _():
            ck_ref[...] = jnp.zeros_like(ck_ref)
            cv_ref[...] = jnp.zeros_like(cv_ref)
            dgq_ref[...] = jnp.zeros_like(dgq_ref)
            dgk_ref[...] = jnp.zeros_like(dgk_ref)

        @pl.when(n == nblk)
        def _():
            dq_ref[...] = dq_keep_ref[...]
            dk_ref[...] = ck_ref[...]
            dv_ref[...] = cv_ref[...]

        bdv = bd_ref[...]
        gqv = gq_ref[...]
        gkv = gk_ref[...]
        band = jnp.concatenate([_band_mask(n > 0)] * (hw // HEAD_DIM), axis=0)
        lane = lax.broadcasted_iota(jnp.int32, (1, hw), 1)

        def residue(r):
            rr = _residue_rows(r, d)
            qn, qhat, qrstd = _head_norm(q_ref[rr, :], gqv, bdv)
            kn, khat, krstd = _head_norm(jnp.concatenate([kp_ref[rr, :], kc_ref[rr, :]], axis=0), gkv, bdv)
            knb = kn.astype(BF16)
            vb = jnp.concatenate([vp_ref[rr, :], vc_ref[rr, :]], axis=0).astype(BF16)
            qs = _stack_heads(qn, lane).astype(BF16)
            dos = _stack_heads(do_ref[rr, :], lane).astype(BF16)
            s = lax.dot_general(qs, knb, nt, preferred_element_type=F32) * (HEAD_DIM ** -0.5)
            prob = jnp.where(band, jnp.exp(s - _head_columns(lse_ref[rr, :], lane)), 0.0)
            dp = lax.dot_general(dos, vb, nt, preferred_element_type=F32)
            ds = (prob * (dp + _head_columns(c_ref[rr, :], lane)) * (HEAD_DIM ** -0.5)).astype(BF16)
            dqn = _unstack_heads(jnp.dot(ds, knb, preferred_element_type=F32), lane)
            dkn = lax.dot_general(ds, qs, tn, preferred_element_type=F32)
            dvv = lax.dot_general(prob.astype(BF16), dos, tn, preferred_element_type=F32)

            dq = _head_norm_bwd(dqn, gqv, qhat, qrstd, bdv)
            dq_ref[rr, :] = dq
            dq_keep_ref[rr, :] = dq
            dk2 = _head_norm_bwd(dkn, gkv, khat, krstd, bdv)
            dgq_ref[...] += jnp.sum(dqn * qhat, axis=0, keepdims=True)
            dgk_ref[...] += jnp.sum(dkn * khat, axis=0, keepdims=True)
            dk_ref[rr, :] = ck_ref[rr, :] + dk2[:BLK]
            dv_ref[rr, :] = cv_ref[rr, :] + dvv[:BLK]
            ck_ref[rr, :] = dk2[BLK:]
            cv_ref[rr, :] = dvv[BLK:]

        @pl.when(n < nblk)
        def _():
            _for_residues(d, residue)

    last = nblk - 1
    per = PW // hw
    cq, ck, cv = (OFF_Q + PW * g) // hw, (OFF_K + PW * g) // hw, (OFF_V + PW * g) // hw
    cur = lambda col: pl.BlockSpec((rows, hw), lambda h, n: (jnp.minimum(n, last), col + h))
    prv = lambda col: pl.BlockSpec((rows, hw), lambda h, n: (jnp.maximum(jnp.minimum(n, last) - 1, 0), col + h))
    cur3 = pl.BlockSpec((None, rows, hw), lambda h, n: (g, jnp.minimum(n, last), h))
    done = pl.BlockSpec((rows, hw), lambda h, n: (jnp.maximum(n - 1, 0), h))
    vec = pl.BlockSpec((1, hw), lambda h, n: (0, h))
    return pl.pallas_call(
        body, name=name, grid=(per, nblk + 1),
        in_specs=[cur(cq), cur(ck), prv(ck), cur(cv), prv(cv), cur(0), cur3, cur3, vec, vec,
                  pl.BlockSpec((hw, hw), lambda h, n: (0, 0))],
        out_specs=[cur(0), done, done, vec, vec],
        out_shape=[SDS((S, PW), F32)] * 3 + [SDS((1, PW), F32)] * 2,
        scratch_shapes=[pltpu.VMEM((rows, hw), F32)] * 3,
        compiler_params=_cp(2),
    )(p, p, p, p, p, lse, do3, c3, gq, gk, bd)


def _mix_fwd(name, os, lses):
    S = os[0].shape[0]
    tm = min(512, S)

    def body(o0, o1, o2, l0, l1, l2, y_ref):
        o = [o0[...], o1[...], o2[...]]
        l = [l0[...], l1[...], l2[...]]
        m = jnp.maximum(jnp.maximum(l[0], l[1]), l[2])
        e = [jnp.exp(t - m) for t in l]
        inv = 1.0 / (e[0] + e[1] + e[2])
        for g in range(N_PATTERNS):
            y_ref[:, PW * g:PW * (g + 1)] = (o[g] * (e[g] * inv)).astype(y_ref.dtype)

    blk = pl.BlockSpec((tm, PW), lambda i: (i, 0))
    return pl.pallas_call(
        body, name=name, grid=(S // tm,),
        in_specs=[blk] * 6,
        out_specs=pl.BlockSpec((tm, C_WIDTH), lambda i: (i, 0)),
        out_shape=SDS((S, C_WIDTH), BF16),
        compiler_params=_cp(1),
    )(*os, *lses)


def _mix_bwd(name, os, lses, dycat, bd):
    S = os[0].shape[0]
    tm = min(512, S)
    c0 = (A_WIDTH + B_WIDTH) // PW

    def body(o0, o1, o2, l0, l1, l2, dy0_ref, dy1_ref, dy2_ref, bd_ref, do_ref, c_ref):
        bdv = bd_ref[...]
        o = [o0[...], o1[...], o2[...]]
        l = [l0[...], l1[...], l2[...]]
        dys = [dy0_ref[...], dy1_ref[...], dy2_ref[...]]
        m = jnp.maximum(jnp.maximum(l[0], l[1]), l[2])
        e = [jnp.exp(t - m) for t in l]
        inv = 1.0 / (e[0] + e[1] + e[2])
        alpha = [t * inv for t in e]
        da = [_seg_sum(dys[g] * o[g], bdv) for g in range(N_PATTERNS)]
        mean_da = alpha[0] * da[0] + alpha[1] * da[1] + alpha[2] * da[2]
        for g in range(N_PATTERNS):
            do_ref[g] = dys[g] * alpha[g]
            c_ref[g] = -alpha[g] * mean_da

    blk = pl.BlockSpec((tm, PW), lambda i: (i, 0))
    blk3 = pl.BlockSpec((N_PATTERNS, tm, PW), lambda i: (0, i, 0))
    dyspec = lambda g: pl.BlockSpec((tm, PW), lambda i: (i, c0 + g))
    return pl.pallas_call(
        body, name=name, grid=(S // tm,),
        in_specs=[blk] * 6 + [dyspec(0), dyspec(1), dyspec(2), pl.BlockSpec((PW, PW), lambda i: (0, 0))],
        out_specs=[blk3, blk3],
        out_shape=[SDS((N_PATTERNS, S, PW), F32)] * 2,
        compiler_params=_cp(1),
    )(*os, *lses, dycat, dycat, dycat, bd)


def _mesh_pos():
    x, y, c = lax.axis_index("x"), lax.axis_index("y"), lax.axis_index("c")
    chips = [(1 - x, y), (x, 1 - y), (1 - x, 1 - y)]
    chip_idx = [2 * cx + cy for cx, cy in chips]
    return x, y, c, 2 * x + y, chips, chip_idx


def _place_shard(name, w, layer, chip_arr, out_dtype, deps=()):
    _, R, C = w.shape
    tr = min(256, R)

    def body(chip_ref, w_ref, *rest):
        o_ref = rest[-1]
        o_ref[...] = w_ref[...].astype(o_ref.dtype)

    return pl.pallas_call(
        body, name=name,
        grid_spec=pltpu.PrefetchScalarGridSpec(
            num_scalar_prefetch=1, grid=(R // tr,),
            in_specs=[pl.BlockSpec((None, tr, C), lambda i, chip_ref: (layer, i, 0))] + [_hbm_spec()] * len(deps),
            out_specs=pl.BlockSpec((None, tr, C), lambda i, chip_ref: (chip_ref[0], i, 0))),
        out_shape=SDS((N_CHIPS, R, C), out_dtype),
        compiler_params=_cp(1),
    )(chip_arr, w, *deps)


HBM_SPEC = pl.BlockSpec(memory_space=pltpu.HBM)
SEM_SPEC = pl.BlockSpec(memory_space=pltpu.SEMAPHORE)
SPLIT_COPY = pltpu.SideEffectType.DATAFLOW_SIDE_EFFECTING
N_PEER_CHIPS = N_CHIPS - 1
TOKEN_SHAPE = SDS((8, 128), F32)
TOKEN_SPEC = pl.BlockSpec(memory_space=pltpu.VMEM)


def _in_hbm(a):
    return pltpu.with_memory_space_constraint(a, pltpu.HBM)


def _gather_start(name, bufs):
    T = len(bufs)

    def body(*refs):
        ins = refs[:T]
        send_sems, recv_sems = refs[T:2 * T], refs[2 * T:3 * T]
        token = refs[4 * T]
        x, y, c, me, chips, chip_idx = _mesh_pos()
        for t in range(T):
            hr = ins[t].shape[1] // 2
            mine = ins[t].at[me, pl.ds(c * hr, hr), :]
            for j in range(N_PEER_CHIPS):
                pltpu.make_async_remote_copy(src_ref=mine, dst_ref=mine, send_sem=send_sems[t].at[j],
                                             recv_sem=recv_sems[t].at[j], device_id=(*chips[j], c),
                                             device_id_type=MESH).start()
        token[...] = jnp.zeros_like(token)

    sems = [pltpu.SemaphoreType.DMA((N_PEER_CHIPS,))] * T
    out = pl.pallas_call(
        body, name=name,
        in_specs=[HBM_SPEC] * T,
        out_specs=[SEM_SPEC] * (2 * T) + [HBM_SPEC] * T + [TOKEN_SPEC],
        out_shape=sems + sems + [pltpu.HBM(b.shape, b.dtype) for b in bufs] + [TOKEN_SHAPE],
        input_output_aliases={t: 2 * T + t for t in range(T)},
        compiler_params=pltpu.CompilerParams(has_side_effects=SPLIT_COPY),
    )(*[_in_hbm(b) for b in bufs])
    return out[:T], out[T:2 * T], out[2 * T:3 * T], out[3 * T]


def _gather_wait(name, buf, send_sem, recv_sem, after):
    n_in = 3 if after is None else 4

    def body(*refs):
        buf_ref, ssem, rsem = refs[:3]
        x, y, c, me, chips, chip_idx = _mesh_pos()
        hr = buf_ref.shape[1] // 2
        mine = buf_ref.at[me, pl.ds(c * hr, hr), :]
        for j in range(N_PEER_CHIPS):
            got = buf_ref.at[chip_idx[j], pl.ds(c * hr, hr), :]
            cp = pltpu.make_async_remote_copy(src_ref=mine, dst_ref=got, send_sem=ssem.at[j], recv_sem=rsem.at[j],
                                              device_id=(*chips[j], c), device_id_type=MESH)
            cp.wait_send()
            cp.wait_recv()

    args = [buf, send_sem, recv_sem] + ([] if after is None else [after])
    return pl.pallas_call(
        body, name=name,
        in_specs=[HBM_SPEC, SEM_SPEC, SEM_SPEC] + [_hbm_spec()] * (n_in - 3),
        out_specs=HBM_SPEC,
        out_shape=pltpu.HBM(buf.shape, buf.dtype),
        input_output_aliases={0: 0},
        compiler_params=pltpu.CompilerParams(has_side_effects=SPLIT_COPY),
    )(*args)


def _forward_start(name, buf):
    def body(buf_ref, send_sems, recv_sems, buf_thru, token):
        x, y, c, me, chips, chip_idx = _mesh_pos()
        hr = buf_ref.shape[1] // 2
        for j in range(N_PEER_CHIPS):
            got = buf_ref.at[chip_idx[j], pl.ds(c * hr, hr), :]
            pltpu.make_async_remote_copy(src_ref=got, dst_ref=got, send_sem=send_sems.at[j], recv_sem=recv_sems.at[j],
                                         device_id=(x, y, 1 - c), device_id_type=MESH).start()
        token[...] = jnp.zeros_like(token)

    sems = pltpu.SemaphoreType.DMA((N_PEER_CHIPS,))
    return pl.pallas_call(
        body, name=name,
        in_specs=[HBM_SPEC],
        out_specs=[SEM_SPEC, SEM_SPEC, HBM_SPEC, TOKEN_SPEC],
        out_shape=[sems, sems, pltpu.HBM(buf.shape, buf.dtype), TOKEN_SHAPE],
        input_output_aliases={0: 2},
        compiler_params=pltpu.CompilerParams(has_side_effects=SPLIT_COPY),
    )(_in_hbm(buf))


def _forward_wait(name, buf, send_sems, recv_sems, after):
    n_in = 3 if after is None else 4

    def body(*refs):
        buf_ref, ssems, rsems = refs[:3]
        x, y, c, me, chips, chip_idx = _mesh_pos()
        hr = buf_ref.shape[1] // 2
        for j in range(N_PEER_CHIPS):
            sent = buf_ref.at[chip_idx[j], pl.ds(c * hr, hr), :]
            theirs = buf_ref.at[chip_idx[j], pl.ds((1 - c) * hr, hr), :]
            cp = pltpu.make_async_remote_copy(src_ref=sent, dst_ref=theirs, send_sem=ssems.at[j],
                                              recv_sem=rsems.at[j], device_id=(x, y, 1 - c), device_id_type=MESH)
            cp.wait_send()
            cp.wait_recv()

    args = [buf, send_sems, recv_sems] + ([] if after is None else [after])
    return pl.pallas_call(
        body, name=name,
        in_specs=[HBM_SPEC, SEM_SPEC, SEM_SPEC] + [_hbm_spec()] * (n_in - 3),
        out_specs=HBM_SPEC,
        out_shape=pltpu.HBM(buf.shape, buf.dtype),
        input_output_aliases={0: 0},
        compiler_params=pltpu.CompilerParams(has_side_effects=SPLIT_COPY),
    )(*args)


class _GatheredWeights:
    def __init__(self):
        self._order = []
        self._pending = {}
        self._forwarding = {}
        self._ready = {}
        self._tokens = []

    def start(self, keys, bufs):
        send_sems, recv_sems, thru, token = _gather_start(f"gather_start_{len(self._order)}", bufs)
        self._tokens.append(token)
        self._order.extend(keys)
        self._pending.update({k: (b, s, r) for k, b, s, r in zip(keys, thru, send_sems, recv_sems)})

    def _prefetch(self, key, after):
        if key in self._pending:
            buf, ssem, rsem = self._pending.pop(key)
            tag = f"{key[0]}_{key[1]}"
            buf = _gather_wait(f"gather_wait_{tag}", buf, ssem, rsem, after)
            ssems, rsems, buf, token = _forward_start(f"gather_fwd_start_{tag}", buf)
            self._forwarding[key] = (buf, ssems, rsems)
            self._tokens.append(token)

    def get(self, name, layer, after=None):
        key = (name, layer)
        if key not in self._ready:
            self._prefetch(key, after)
            buf, ssems, rsems = self._forwarding.pop(key)
            self._ready[key] = _forward_wait(f"gather_fwd_wait_{name}_{layer}", buf, ssems, rsems, after)
            nxt = self._order.index(key) + 1
            if nxt < len(self._order):
                self._prefetch(self._order[nxt], after)
        return self._ready[key]

    def deps(self):
        tokens, self._tokens = self._tokens, []
        return tokens


def _swap_copy(g_ref, land_ref, send_sem, recv_sem):
    x, y, c, _, _, _ = _mesh_pos()
    hr = g_ref.shape[1] // 2
    return pltpu.make_async_remote_copy(src_ref=g_ref.at[:, pl.ds((1 - c) * hr, hr), :], dst_ref=land_ref,
                                        send_sem=send_sem, recv_sem=recv_sem, device_id=(x, y, 1 - c),
                                        device_id_type=MESH)


def _swap_start(name, g):
    land_shape = (g.shape[0], g.shape[1] // 2, g.shape[2])

    def body(g_ref, land_ref, send_sem, recv_sem, land_thru, token):
        _swap_copy(g_ref, land_ref, send_sem, recv_sem).start()
        token[...] = jnp.zeros_like(token)

    return pl.pallas_call(
        body, name=name,
        in_specs=[HBM_SPEC, HBM_SPEC],
        out_specs=[SEM_SPEC, SEM_SPEC, HBM_SPEC, TOKEN_SPEC],
        out_shape=[pltpu.SemaphoreType.DMA(()), pltpu.SemaphoreType.DMA(()), pltpu.HBM(land_shape, g.dtype),
                   TOKEN_SHAPE],
        input_output_aliases={1: 2},
        compiler_params=pltpu.CompilerParams(has_side_effects=SPLIT_COPY),
    )(_in_hbm(g), _in_hbm(lax.empty(land_shape, g.dtype)))


def _swap_wait(name, g, land, send_sem, recv_sem, after):
    def body(g_ref, land_ref, send_sem, recv_sem, after_ref, land_out):
        cp = _swap_copy(g_ref, land_ref, send_sem, recv_sem)
        cp.wait_send()
        cp.wait_recv()

    return pl.pallas_call(
        body, name=name,
        in_specs=[HBM_SPEC, HBM_SPEC, SEM_SPEC, SEM_SPEC, _hbm_spec()],
        out_specs=HBM_SPEC,
        out_shape=pltpu.HBM(land.shape, land.dtype),
        input_output_aliases={1: 0},
        compiler_params=pltpu.CompilerParams(has_side_effects=SPLIT_COPY),
    )(_in_hbm(g), land, send_sem, recv_sem, after)


def _add_my_half(name, g, r, pos_arr):
    ns, R, C = g.shape
    hr = R // 2
    tr = min(256, hr)
    nt = hr // tr

    def body(pos_ref, g_ref, r_ref, o_ref, land_ref):
        t = (g_ref[...] + r_ref[...]).astype(o_ref.dtype)
        o_ref[...] = t

        @pl.when(pl.program_id(1) == pos_ref[1])
        def _():
            land_ref[...] = t

    blk = pl.BlockSpec((None, tr, C), lambda i, s, pos_ref: (s, i, 0))
    return pl.pallas_call(
        body, name=name,
        grid_spec=pltpu.PrefetchScalarGridSpec(
            num_scalar_prefetch=1, grid=(nt, ns),
            in_specs=[pl.BlockSpec((None, tr, C), lambda i, s, pos_ref: (s, pos_ref[0] * nt + i, 0)), blk],
            out_specs=[blk, pl.BlockSpec((None, tr, C), lambda i, s, pos_ref: (pos_ref[1], i, 0))]),
        out_shape=[SDS((ns, hr, C), BF16)] * 2,
        compiler_params=_cp(2),
    )(pos_arr, g, r)


def _exchange_start(name, part, land):
    def body(part_ref, land_ref, send_sems, recv_sems, land_thru, token):
        x, y, c, me, chips, chip_idx = _mesh_pos()
        for j in range(N_PEER_CHIPS):
            pltpu.make_async_remote_copy(src_ref=part_ref.at[chip_idx[j]], dst_ref=land_ref.at[me],
                                         send_sem=send_sems.at[j], recv_sem=recv_sems.at[j],
                                         device_id=(*chips[j], c), device_id_type=MESH).start()
        token[...] = jnp.zeros_like(token)

    sems = pltpu.SemaphoreType.DMA((N_PEER_CHIPS,))
    return pl.pallas_call(
        body, name=name,
        in_specs=[HBM_SPEC, HBM_SPEC],
        out_specs=[SEM_SPEC, SEM_SPEC, HBM_SPEC, TOKEN_SPEC],
        out_shape=[sems, sems, pltpu.HBM(land.shape, land.dtype), TOKEN_SHAPE],
        input_output_aliases={1: 2},
        compiler_params=pltpu.CompilerParams(has_side_effects=SPLIT_COPY),
    )(_in_hbm(part), _in_hbm(land))


def _exchange_wait(name, part, land, send_sems, recv_sems, after):
    def body(part_ref, land_ref, send_sems, recv_sems, after_ref, land_out):
        x, y, c, me, chips, chip_idx = _mesh_pos()
        for j in range(N_PEER_CHIPS):
            cp = pltpu.make_async_remote_copy(src_ref=part_ref.at[chip_idx[j]], dst_ref=land_ref.at[chip_idx[j]],
                                              send_sem=send_sems.at[j], recv_sem=recv_sems.at[j],
                                              device_id=(*chips[j], c), device_id_type=MESH)
            cp.wait_send()
            cp.wait_recv()

    return pl.pallas_call(
        body, name=name,
        in_specs=[HBM_SPEC, HBM_SPEC, SEM_SPEC, SEM_SPEC, _hbm_spec()],
        out_specs=HBM_SPEC,
        out_shape=pltpu.HBM(land.shape, land.dtype),
        input_output_aliases={1: 0},
        compiler_params=pltpu.CompilerParams(has_side_effects=SPLIT_COPY),
    )(_in_hbm(part), land, send_sems, recv_sems, after)


class _GradReducer:
    def __init__(self, c_arr):
        self._c_arr = c_arr
        self._swapping = []
        self._exchanging = {}
        self._tokens = []

    def begin(self, name, layer, g):
        tag = f"{name}_{layer}"
        ssem, rsem, land, token = _swap_start(f"rs_swap_start_{tag}", g)
        self._swapping.append((name, layer, g, ssem, rsem, land))
        self._tokens.append(token)

    def advance(self, after):
        for name, layer, g, ssem, rsem, land in self._swapping:
            tag = f"{name}_{layer}"
            theirs = _swap_wait(f"rs_swap_wait_{tag}", g, land, ssem, rsem, after)
            part, own = _add_my_half(f"rs_add_{tag}", g, theirs, self._c_arr)
            ssems, rsems, land2, token = _exchange_start(f"rs_xchg_start_{tag}", part, own)
            self._exchanging[(name, layer)] = (part, ssems, rsems, land2)
            self._tokens.append(token)
        self._swapping = []

    def deps(self):
        tokens, self._tokens = self._tokens, []
        return tokens

    def finish(self, names, n_layers, after):
        bufs = []
        for name in names:
            buf = None
            for layer in range(n_layers):
                part, ssems, rsems, land = self._exchanging.pop((name, layer))
                tag = f"{name}_{layer}"
                landed = _exchange_wait(f"rs_xchg_wait_{tag}", part, land, ssems, rsems, after)
                buf = _sum_chips(f"rs_sum_{tag}", landed, self._c_arr, layer, n_layers, buf)
            bufs.append(buf)
        return dict(zip(names, _join_halves(f"rs_join_{names[0]}", bufs)))


def _sum_chips(name, r, c_arr, layer, n_layers, prev):
    ns, H, C = r.shape
    tr = min(256, H)
    nt = H // tr

    def body(c_ref, r_ref, *rest):
        o_ref = rest[-1]
        o_ref[...] = ((r_ref[0].astype(F32) + r_ref[1].astype(F32)) + r_ref[2].astype(F32)) + r_ref[3].astype(F32)

    in_specs = [pl.BlockSpec((ns, tr, C), lambda i, c_ref: (0, i, 0))]
    args = [c_arr, r]
    aliases = {}
    if prev is not None:
        in_specs.append(_hbm_spec())
        args.append(prev)
        aliases = {2: 0}
    return pl.pallas_call(
        body, name=name,
        grid_spec=pltpu.PrefetchScalarGridSpec(
            num_scalar_prefetch=1, grid=(nt,), in_specs=in_specs,
            out_specs=pl.BlockSpec((None, tr, C), lambda i, c_ref: (layer, c_ref[0] * nt + i, 0))),
        out_shape=SDS((n_layers, 2 * H, C), F32),
        input_output_aliases=aliases,
        compiler_params=_cp(1),
    )(*args)


def _join_halves(name, bufs):
    T = len(bufs)

    def body(*refs):
        outs = refs[T:2 * T]
        send_sems, recv_sems = refs[2 * T:]
        x, y, c, _, _, _ = _mesh_pos()
        cps = []
        for t in range(T):
            hr = outs[t].shape[1] // 2
            mine = outs[t].at[:, pl.ds(c * hr, hr), :]
            cp = pltpu.make_async_remote_copy(src_ref=mine, dst_ref=mine, send_sem=send_sems.at[t],
                                              recv_sem=recv_sems.at[t], device_id=(x, y, 1 - c), device_id_type=MESH)
            cp.start()
            cps.append(cp)
        for t in range(T):
            hr = outs[t].shape[1] // 2
            theirs = outs[t].at[:, pl.ds((1 - c) * hr, hr), :]
            pltpu.make_async_remote_copy(src_ref=theirs, dst_ref=theirs, send_sem=send_sems.at[t],
                                         recv_sem=recv_sems.at[t], device_id=(x, y, 1 - c),
                                         device_id_type=MESH).wait_recv()
        for cp in cps:
            cp.wait_send()

    return pl.pallas_call(
        body, name=name,
        in_specs=[_hbm_spec()] * T, out_specs=[_hbm_spec()] * T,
        out_shape=[SDS(b.shape, b.dtype) for b in bufs],
        input_output_aliases={t: t for t in range(T)},
        scratch_shapes=[pltpu.SemaphoreType.DMA((T,)), pltpu.SemaphoreType.DMA((T,))],
    )(*bufs)


def _small_copy(k, buf_ref, land_ref, send_sems, recv_sems):
    x, y, c = lax.axis_index("x"), lax.axis_index("y"), lax.axis_index("c")
    me = 4 * x + 2 * y + c
    peer = (x ^ ((k >> 2) & 1), y ^ ((k >> 1) & 1), c ^ (k & 1))
    cp = pltpu.make_async_remote_copy(src_ref=buf_ref, dst_ref=land_ref.at[me], send_sem=send_sems.at[k - 1],
                                      recv_sem=recv_sems.at[k - 1], device_id=peer, device_id_type=MESH)
    return me, peer, cp


def _small_start(buf, deps):
    land = jnp.broadcast_to(buf[None], (N_DEV,) + buf.shape)
    n_dep = len(deps)

    def body(buf_ref, land_ref, *rest):
        send_sems, recv_sems, _, token = rest[n_dep:]
        for k in range(1, N_DEV):
            _small_copy(k, buf_ref, land_ref, send_sems, recv_sems)[2].start()
        token[...] = jnp.zeros_like(token)

    sems = pltpu.SemaphoreType.DMA((N_DEV - 1,))
    return pl.pallas_call(
        body, name="small_gather_start",
        in_specs=[HBM_SPEC, HBM_SPEC] + [_hbm_spec()] * n_dep,
        out_specs=[SEM_SPEC, SEM_SPEC, HBM_SPEC, TOKEN_SPEC],
        out_shape=[sems, sems, pltpu.HBM(land.shape, land.dtype), TOKEN_SHAPE],
        input_output_aliases={1: 2},
        compiler_params=pltpu.CompilerParams(has_side_effects=SPLIT_COPY),
    )(_in_hbm(buf), _in_hbm(land), *deps)


def _small_wait(buf, land, send_sems, recv_sems, after):
    def body(buf_ref, land_ref, send_sems, recv_sems, after_ref, land_out):
        for k in range(1, N_DEV):
            me, peer, cp = _small_copy(k, buf_ref, land_ref, send_sems, recv_sems)
            cp.wait_send()
            got = land_ref.at[me ^ k]
            pltpu.make_async_remote_copy(src_ref=got, dst_ref=got, send_sem=send_sems.at[k - 1],
                                         recv_sem=recv_sems.at[k - 1], device_id=peer,
                                         device_id_type=MESH).wait_recv()

    return pl.pallas_call(
        body, name="small_gather_wait",
        in_specs=[HBM_SPEC, HBM_SPEC, SEM_SPEC, SEM_SPEC, _hbm_spec()],
        out_specs=HBM_SPEC,
        out_shape=pltpu.HBM(land.shape, land.dtype),
        input_output_aliases={1: 0},
        compiler_params=pltpu.CompilerParams(has_side_effects=SPLIT_COPY),
    )(_in_hbm(buf), land, send_sems, recv_sems, after)


def _sum_devices(land):
    n, R, C = land.shape

    def body(land_ref, out_ref):
        acc = land_ref[0]
        for d in range(1, n):
            acc = acc + land_ref[d]
        out_ref[...] = acc

    return pl.pallas_call(
        body, name="small_sum",
        in_specs=[pl.BlockSpec(memory_space=pltpu.VMEM)],
        out_specs=pl.BlockSpec(memory_space=pltpu.VMEM),
        out_shape=SDS((R, C), land.dtype),
        compiler_params=pltpu.CompilerParams(vmem_limit_bytes=V7X_VMEM_LIMIT),
    )(land)


def _deinterleave(t, d):
    if d == 1:
        return t
    S, W = t.shape
    return t.reshape(S // d, d, W).transpose(1, 0, 2).reshape(S, W)


def _interleave(t, d):
    if d == 1:
        return t
    S, W = t.shape
    return t.reshape(d, S // d, W).transpose(1, 0, 2).reshape(S, W)


def _to_patterns(t, off):
    return jnp.stack([_deinterleave(t[:, off + PW * g:off + PW * (g + 1)], PATTERN_DILATION[g])
                      for g in range(N_PATTERNS)])


def _from_patterns(t3):
    return jnp.stack([_interleave(t3[g], PATTERN_DILATION[g]) for g in range(N_PATTERNS)])


def _pack_rows(vectors):
    flat = jnp.concatenate([v.reshape(-1) for v in vectors])
    n = flat.shape[0]
    padded = -(-n // 1024) * 1024
    return jnp.pad(flat, (0, padded - n)).reshape(padded // 128, 128)


def _unpack_rows(buf, shapes):
    flat = buf.reshape(-1)
    out, off = [], 0
    for s in shapes:
        n = 1
        for dim in s:
            n *= dim
        out.append(flat[off:off + n].reshape(s))
        off += n
    return out


def _layer_forward(l, x, prm, wg):
    S, D = x.shape
    w_in = wg.get("w_in", l, x)
    p, h = _norm_matmul(f"in_proj_{l}", x, prm["attn_norm"][l], w_in, F32, deps=wg.deps())
    y_a = _sgu_fwd(f"sgu_fwd_{l}", p, prm["sgu_wt"][l], prm["sgu_bb"][l])
    y_b = _conv_fwd(f"conv_fwd_{l}", p, prm["conv_w"][l])
    os, lses = [], []
    for g in range(N_PATTERNS):
        o_g, lse_g = _attn_fwd(f"attn_fwd_{l}_{g}", p, g, prm["q_gain"][l], prm["k_gain"][l], prm["bd"])
        os.append(o_g)
        lses.append(lse_g)
    y_c = _mix_fwd(f"mix_fwd_{l}", os, lses)
    ycat = jnp.concatenate([y_a, y_b, y_c], axis=1)
    tmb, tnb = min(1024, S), min(1024, D)
    w_out = wg.get("w_out", l, ycat)
    rq = w_out.shape[1]
    x1 = _matmul(
        f"out_proj_{l}", ycat, w_out, (S, D), F32, grid=(S // tmb, D // tnb, N_CHIPS),
        a_spec=pl.BlockSpec((tmb, rq), lambda i, j, k: (i, k)),
        b_spec=pl.BlockSpec((None, rq, tnb), lambda i, j, k: (k, 0, j)),
        o_spec=pl.BlockSpec((tmb, tnb), lambda i, j, k: (i, j)),
        contract=(1, 0), acc_shape=(tmb, tnb),
        extras=(x,), extra_specs=(pl.BlockSpec((tmb, tnb), lambda i, j, k: (i, j)),),
        epi=lambda r, res: r + res, deps=wg.deps())
    w_mlp_in = wg.get("w_mlp_in", l, x1)
    a, h2 = _norm_matmul(f"mlp_in_{l}", x1, prm["mlp_norm"][l], w_mlp_in, BF16, deps=wg.deps())
    w_mlp_out = wg.get("w_mlp_out", l, a)
    dff4 = w_mlp_out.shape[1]
    tk = min(1024, dff4)
    kpc = dff4 // tk
    x2 = _matmul(
        f"mlp_out_{l}", a, w_mlp_out, (S, D), F32, grid=(S // tmb, D // tnb, N_CHIPS * kpc),
        a_spec=pl.BlockSpec((tmb, tk), lambda i, j, k: (i, k)),
        b_spec=pl.BlockSpec((None, tk, tnb), lambda i, j, k: (k // kpc, k % kpc, j)),
        o_spec=pl.BlockSpec((tmb, tnb), lambda i, j, k: (i, j)),
        contract=(1, 0), acc_shape=(tmb, tnb), a_pre=_relu2_bf16,
        extras=(x1,), extra_specs=(pl.BlockSpec((tmb, tnb), lambda i, j, k: (i, j)),),
        epi=lambda r, res: r + res, deps=wg.deps())
    saved = dict(x=x, p=p, h=h, os=os, lses=lses, ycat=ycat, x1=x1, a=a, h2=h2)
    return x2, saved


def _layer_backward(l, dx2, dx2b, sv, prm, wg, sink):
    S, D = dx2.shape
    w_in, w_out = wg.get("w_in", l), wg.get("w_out", l)
    w_mlp_in, w_mlp_out = wg.get("w_mlp_in", l), wg.get("w_mlp_out", l)
    dff4 = w_mlp_in.shape[-1]
    dff = N_CHIPS * dff4
    tm = min(512, S)
    tk = min(1024, S)
    nks = S // tk

    tmb, tnb = min(1024, S), min(1024, D)
    da = _matmul(
        f"mlp_out_bwd_{l}", dx2b, w_mlp_out, (S, dff), BF16, grid=(S // tmb, N_CHIPS, 1),
        a_spec=pl.BlockSpec((tmb, D), lambda i, j, k: (i, 0)),
        b_spec=pl.BlockSpec((None, dff4, D), lambda i, j, k: (j, 0, 0)),
        o_spec=pl.BlockSpec((tmb, dff4), lambda i, j, k: (i, j)),
        contract=(1, 1), acc_shape=(tmb, dff4),
        extras=(sv["a"],), extra_specs=(pl.BlockSpec((tmb, dff4), lambda i, j, k: (i, j)),),
        epi=lambda r, act: r * (2.0 * jnp.maximum(act.astype(F32), 0.0)), deps=sink.deps())
    tmw = min(1024, dff4)
    mpc = dff4 // tmw
    g_w2 = _matmul(
        f"mlp_out_dw_{l}", sv["a"], dx2b, (N_CHIPS, dff4, D), F32, grid=(N_CHIPS * mpc, 1, nks),
        a_spec=pl.BlockSpec((tk, tmw), lambda i, j, k: (k, i)),
        b_spec=pl.BlockSpec((tk, D), lambda i, j, k: (k, 0)),
        o_spec=pl.BlockSpec((None, tmw, D), lambda i, j, k: (i // mpc, i % mpc, 0)),
        contract=(0, 0), acc_shape=(tmw, D), a_pre=_relu2_bf16)
    sink.begin("w_mlp_out", l, g_w2)
    dh2 = _matmul(
        f"mlp_in_bwd_{l}", da, w_mlp_in, (S, D), F32, grid=(S // tmb, D // tnb, N_CHIPS),
        a_spec=pl.BlockSpec((tmb, dff4), lambda i, j, k: (i, k)),
        b_spec=pl.BlockSpec((None, tnb, dff4), lambda i, j, k: (k, j, 0)),
        o_spec=pl.BlockSpec((tmb, tnb), lambda i, j, k: (i, j)),
        contract=(1, 1), acc_shape=(tmb, tnb), deps=sink.deps())
    sink.advance(dh2)
    tmd = min(1024, D)
    g_w1 = _matmul(
        f"mlp_in_dw_{l}", sv["h2"], da, (N_CHIPS, D, dff4), F32, grid=(N_CHIPS, D // tmd, nks),
        a_spec=pl.BlockSpec((tk, tmd), lambda i, j, k: (k, j)),
        b_spec=pl.BlockSpec((tk, dff4), lambda i, j, k: (k, i)),
        o_spec=pl.BlockSpec((None, tmd, dff4), lambda i, j, k: (i, j, 0)),
        contract=(0, 0), acc_shape=(tmd, dff4))
    sink.begin("w_mlp_in", l, g_w1)
    dx1, dx1b, g_mlp_norm = _rmsnorm_bwd(f"mlp_norm_bwd_{l}", dh2, sv["x1"], prm["mlp_norm"][l], dx2,
                                         deps=sink.deps())

    rq = w_out.shape[1]
    dycat = _matmul(
        f"out_proj_bwd_{l}", dx1b, w_out, (S, N_CHIPS * rq), F32, grid=(S // tmb, N_CHIPS, 1),
        a_spec=pl.BlockSpec((tmb, D), lambda i, j, k: (i, 0)),
        b_spec=pl.BlockSpec((None, rq, D), lambda i, j, k: (j, 0, 0)),
        o_spec=pl.BlockSpec((tmb, rq), lambda i, j, k: (i, j)),
        contract=(1, 1), acc_shape=(tmb, rq))
    sink.advance(dycat)
    g_wout = _matmul(
        f"out_proj_dw_{l}", sv["ycat"], dx1b, (N_CHIPS, rq, D), F32, grid=(N_CHIPS, 1, nks),
        a_spec=pl.BlockSpec((tk, rq), lambda i, j, k: (k, i)),
        b_spec=pl.BlockSpec((tk, D), lambda i, j, k: (k, 0)),
        o_spec=pl.BlockSpec((None, rq, D), lambda i, j, k: (i, 0, 0)),
        contract=(0, 0), acc_shape=(rq, D))
    sink.begin("w_out", l, g_wout)

    p = sv["p"]
    du, dv_a, g_sgu_w, db_lanes = _sgu_bwd(f"sgu_bwd_{l}", p, dycat, prm["sgu_wt"][l], prm["sgu_wtt"][l],
                                           prm["sgu_bb"][l])
    g_sgu_b = db_lanes[:, :A_HEADS].T
    db, dc, dxb, g_conv = _conv_bwd(f"conv_bwd_{l}", p, dycat, prm["conv_w"][l])
    do3, c3 = _mix_bwd(f"mix_bwd_{l}", sv["os"], sv["lses"], dycat, prm["bd"])
    dqs, dks, dvs, dgqs, dgks = [], [], [], [], []
    for g in range(N_PATTERNS):
        dq, dk, dv, dgq, dgk = _attn_bwd(f"attn_bwd_{l}_{g}", p, g, sv["lses"][g], do3, c3,
                                         prm["q_gain"][l], prm["k_gain"][l], prm["bd"])
        dqs.append(dq)
        dks.append(dk)
        dvs.append(dv)
        dgqs.append(dgq)
        dgks.append(dgk)
    g_q = jnp.concatenate(dgqs, axis=1).reshape(N_PATTERNS * PW // HEAD_DIM, HEAD_DIM).sum(axis=0)
    g_k = jnp.concatenate(dgks, axis=1).reshape(N_PATTERNS * PW // HEAD_DIM, HEAD_DIM).sum(axis=0)
    dp = jnp.concatenate([du, dv_a, db, dc, dxb] + [t.astype(BF16) for t in dqs + dks + dvs], axis=1)

    ns_in = w_in.shape[-1]
    g_win = _matmul(
        f"in_proj_dw_{l}", sv["h"], dp, (N_CHIPS, D, ns_in), F32, grid=(N_CHIPS, D // tmd, nks),
        a_spec=pl.BlockSpec((tk, tmd), lambda i, j, k: (k, j)),
        b_spec=pl.BlockSpec((tk, ns_in), lambda i, j, k: (k, i)),
        o_spec=pl.BlockSpec((None, tmd, ns_in), lambda i, j, k: (i, j, 0)),
        contract=(0, 0), acc_shape=(tmd, ns_in))
    sink.begin("w_in", l, g_win)
    dh = _matmul(
        f"in_proj_bwd_{l}", dp, w_in, (S, D), F32, grid=(S // tmb, D // tnb, N_CHIPS),
        a_spec=pl.BlockSpec((tmb, ns_in), lambda i, j, k: (i, k)),
        b_spec=pl.BlockSpec((None, tnb, ns_in), lambda i, j, k: (k, j, 0)),
        o_spec=pl.BlockSpec((tmb, tnb), lambda i, j, k: (i, j)),
        contract=(1, 1), acc_shape=(tmb, tnb), deps=sink.deps())
    sink.advance(dh)
    dx0, dx0b, g_attn_norm = _rmsnorm_bwd(f"attn_norm_bwd_{l}", dh, sv["x"], prm["attn_norm"][l], dx1,
                                          deps=sink.deps())

    big = dict(w_in=g_win, w_out=g_wout, w_mlp_in=g_w1, w_mlp_out=g_w2)
    small = dict(attn_norm=g_attn_norm.reshape(-1), sgu_w=g_sgu_w, sgu_b=g_sgu_b, conv_w=g_conv,
                 q_norm=g_q, k_norm=g_k, mlp_norm=g_mlp_norm.reshape(-1))
    return dx0, dx0b, big, small


BIG = ("w_in", "w_out", "w_mlp_in", "w_mlp_out")
SMALL_REPLICATED = ("attn_norm", "sgu_w", "sgu_b", "q_norm", "k_norm", "mlp_norm")


def _local_step(x, target, prm, wg, n_layers, sink):
    saved = []
    h = x
    for l in range(n_layers):
        h, sv = _layer_forward(l, h, prm, wg)
        saved.append(sv)
    dy, dyb, colsq = _loss_kernel(h, target)
    loss = 0.5 * jnp.sum(colsq) / x.shape[1]
    bigs, smalls = [None] * n_layers, [None] * n_layers
    for l in reversed(range(n_layers)):
        dy, dyb, bigs[l], smalls[l] = _layer_backward(l, dy, dyb, saved[l], prm, wg, sink)
    return loss, dy, bigs, smalls


def _prepare_params(attn_norm, sgu_w, sgu_b, conv_full, q_norm, k_norm, mlp_norm):
    n_layers = attn_norm.shape[0]
    tri = jnp.tril(sgu_w)
    idx = jnp.arange(PW)
    bd = (idx[:, None] // HEAD_DIM == idx[None, :] // HEAD_DIM).astype(BF16)
    return dict(
        attn_norm=[attn_norm[l][None, :] for l in range(n_layers)],
        mlp_norm=[mlp_norm[l][None, :] for l in range(n_layers)],
        sgu_wt=[tri[l].astype(BF16) for l in range(n_layers)],
        sgu_wtt=[tri[l].transpose(0, 2, 1).astype(BF16) for l in range(n_layers)],
        sgu_bb=[jnp.repeat(sgu_b[l].T, HEAD_DIM, axis=1) for l in range(n_layers)],
        conv_w=[conv_full[l] for l in range(n_layers)],
        q_gain=[jnp.tile(q_norm[l], PW // HEAD_DIM)[None, :] for l in range(n_layers)],
        k_gain=[jnp.tile(k_norm[l], PW // HEAD_DIM)[None, :] for l in range(n_layers)],
        bd=bd,
    )


def kernel(x, attn_norm, w_in, sgu_w, sgu_b, conv_w, q_norm, k_norm, w_out, mlp_norm, w_mlp_in, w_mlp_out, loss_target, m_attn_norm, m_w_in, m_sgu_w, m_sgu_b, m_conv_w, m_q_norm, m_k_norm, m_w_out, m_mlp_norm, m_w_mlp_in, m_w_mlp_out, v_attn_norm, v_w_in, v_sgu_w, v_sgu_b, v_conv_w, v_q_norm, v_k_norm, v_w_out, v_mlp_norm, v_w_mlp_in, v_w_mlp_out):
    n_layers = attn_norm.shape[0]
    weights = dict(attn_norm=attn_norm, w_in=w_in, sgu_w=sgu_w, sgu_b=sgu_b, conv_w=conv_w, q_norm=q_norm,
                   k_norm=k_norm, w_out=w_out, mlp_norm=mlp_norm, w_mlp_in=w_mlp_in, w_mlp_out=w_mlp_out)
    mom_m = dict(attn_norm=m_attn_norm, w_in=m_w_in, sgu_w=m_sgu_w, sgu_b=m_sgu_b, conv_w=m_conv_w,
                 q_norm=m_q_norm, k_norm=m_k_norm, w_out=m_w_out, mlp_norm=m_mlp_norm, w_mlp_in=m_w_mlp_in,
                 w_mlp_out=m_w_mlp_out)
    mom_v = dict(attn_norm=v_attn_norm, w_in=v_w_in, sgu_w=v_sgu_w, sgu_b=v_sgu_b, conv_w=v_conv_w,
                 q_norm=v_q_norm, k_norm=v_k_norm, w_out=v_w_out, mlp_norm=v_mlp_norm, w_mlp_in=v_w_mlp_in,
                 w_mlp_out=v_w_mlp_out)
    order = ("attn_norm", "w_in", "sgu_w", "sgu_b", "conv_w", "q_norm", "k_norm", "w_out", "mlp_norm",
             "w_mlp_in", "w_mlp_out")
    chip = 2 * lax.axis_index("x") + lax.axis_index("y")
    c_arr = jnp.stack([lax.axis_index("c"), chip]).astype(jnp.int32)

    conv_cols = conv_w.shape[-1]
    chip_arr = chip.astype(jnp.int32).reshape(1)
    conv_pack = jnp.pad(conv_w.reshape(-1), (0, 2048 - conv_w.size)).reshape(1, 16, 128)
    wg = _GatheredWeights()
    wg.start([("conv_w", 0), ("w_in", 0)],
             [_place_shard("place_conv_w", conv_pack, 0, chip_arr, F32),
              _place_shard("place_w_in_0", weights["w_in"], 0, chip_arr, BF16)])
    keys = [(n, l) for l in range(n_layers) for n in BIG if (n, l) != ("w_in", 0)]
    first = wg.deps()
    wg.start(keys, [_place_shard(f"place_{n}_{l}", weights[n], l, chip_arr, BF16, deps=first) for n, l in keys])
    conv_full = wg.get("conv_w", 0, wg.deps()[-1]).reshape(N_CHIPS, 2048)[:, :conv_w.size].reshape(N_CHIPS, n_layers, 3, conv_cols)
    conv_full = conv_full.transpose(1, 2, 0, 3).reshape(n_layers, 3, N_CHIPS * conv_cols)
    prm = _prepare_params(attn_norm, sgu_w, sgu_b, conv_full, q_norm, k_norm, mlp_norm)

    sink = _GradReducer(c_arr)
    loss_local, grad_x, _, smalls = _local_step(x[0], loss_target[0], prm, wg, n_layers, sink)
    loss = lax.psum(loss_local, ("x", "y", "c"))

    small_names = SMALL_REPLICATED + ("conv_w",)
    small_shapes = [(n_layers,) + tuple(smalls[0][n].shape) for n in small_names]
    packed = _pack_rows([jnp.stack([smalls[l][n] for l in range(n_layers)]) for n in small_names])
    small_send, small_recv, small_land, small_token = _small_start(packed, sink.deps())

    grads, delta, new_m, new_v = {}, {}, {}, {}

    def update(names, after):
        joined = sink.finish(names, n_layers, after)
        for n in names:
            shp = weights[n].shape
            two_d = (shp[0] * shp[1], shp[2])
            d, nm, nv, g = _adamw(f"adamw_{n}", weights[n].reshape(two_d), joined[n].reshape(two_d),
                                  mom_m[n].reshape(two_d), mom_v[n].reshape(two_d))
            grads[n], delta[n], new_m[n], new_v[n] = g.reshape(shp), d.reshape(shp), nm.reshape(shp), nv.reshape(shp)

    update(("w_mlp_out", "w_mlp_in", "w_out"), small_token)
    update(("w_in",), delta["w_out"])
    small_land = _small_wait(packed, small_land, small_send, small_recv, delta["w_in"])
    grads.update(zip(small_names, _unpack_rows(_sum_devices(small_land), small_shapes)))
    grads["conv_w"] = lax.dynamic_slice_in_dim(grads["conv_w"], chip * conv_cols, conv_cols, axis=2)
    smalls_all = SMALL_REPLICATED + ("conv_w",)
    shapes = [weights[n].shape for n in smalls_all]
    d, nm, nv, _ = _adamw("adamw_small",
                       _pack_rows([weights[n] for n in smalls_all]), _pack_rows([grads[n] for n in smalls_all]),
                       _pack_rows([mom_m[n] for n in smalls_all]), _pack_rows([mom_v[n] for n in smalls_all]))
    for n, dd, mm, vv in zip(smalls_all, _unpack_rows(d, shapes), _unpack_rows(nm, shapes), _unpack_rows(nv, shapes)):
        delta[n], new_m[n], new_v[n] = dd, mm, vv

    return (loss, grad_x[None], *[grads[n] for n in order], *[delta[n] for n in order],
            *[new_m[n] for n in order], *[new_v[n] for n in order])
```

```python
import jax
import jax.numpy as jnp
from jax import lax
from jax.experimental import pallas as pl
from jax.experimental.pallas import tpu as pltpu

F32 = jnp.float32
BF16 = jnp.bfloat16
SDS = jax.ShapeDtypeStruct

EPS = 1e-6
HEAD_DIM = 64
A_HEADS = 8
A_WIDTH = 512
CHUNK = 128
B_WIDTH = 768
C_WIDTH = 768
N_PATTERNS = 3
PATTERN_DILATION = (1, 4, 16)
PW = 256
D_IN_PROJ = 5632
OFF_AU, OFF_AV, OFF_BB, OFF_BC, OFF_BX, OFF_Q, OFF_K, OFF_V = 0, 512, 1024, 1792, 2560, 3328, 4096, 4864
N_CHIPS = 4
N_DEV = 8
BLK = 128

ADAM_LR, ADAM_B1, ADAM_B2, ADAM_EPS, ADAM_WD, ADAM_STEP = 0.001, 0.9, 0.999, 1e-08, 0.01, 10

V7X_VMEM_LIMIT = 56 * 1024 * 1024
MESH = pl.DeviceIdType.MESH
NEG = -1e30


def _cp(n_axes):
    return pltpu.CompilerParams(dimension_semantics=("arbitrary",) * n_axes, vmem_limit_bytes=V7X_VMEM_LIMIT)


def _hbm_spec():
    return pl.BlockSpec(memory_space=pl.ANY)


def _norm_matmul(name, x, g, wg, out_dtype, deps=()):
    S, D = x.shape
    ns, _, Ns = wg.shape
    tm = min(512, S)
    n_dep = len(deps)

    def body(x_ref, g_ref, w_ref, *rest):
        o_ref, h_ref, hs_ref = rest[n_dep:]
        @pl.when(pl.program_id(1) == 0)
        def _():
            xv = x_ref[...]
            y = xv * lax.rsqrt(jnp.mean(xv * xv, axis=-1, keepdims=True) + EPS) * g_ref[...]
            hb = y.astype(BF16)
            hs_ref[...] = hb
            h_ref[...] = hb
        o_ref[...] = jnp.dot(hs_ref[...], w_ref[...], preferred_element_type=F32).astype(o_ref.dtype)

    return pl.pallas_call(
        body, name=name, grid=(S // tm, ns),
        in_specs=[pl.BlockSpec((tm, D), lambda i, s: (i, 0)),
                  pl.BlockSpec((1, D), lambda i, s: (0, 0)),
                  pl.BlockSpec((None, D, Ns), lambda i, s: (s, 0, 0))] + [_hbm_spec()] * n_dep,
        out_specs=[pl.BlockSpec((tm, Ns), lambda i, s: (i, s)),
                   pl.BlockSpec((tm, D), lambda i, s: (i, 0))],
        out_shape=[SDS((S, ns * Ns), out_dtype), SDS((S, D), BF16)],
        scratch_shapes=[pltpu.VMEM((tm, D), BF16)],
        compiler_params=_cp(2),
    )(x, g, wg, *deps)


def _matmul(name, a, b, out_shape, out_dtype, *, grid, a_spec, b_spec, o_spec, contract, acc_shape,
            extras=(), extra_specs=(), a_pre=None, epi=None, deps=()):
    nk = grid[2]
    n_ex = len(extras)
    n_dep = len(deps)
    dims = (((contract[0],), (contract[1],)), ((), ()))

    def product(a_ref, b_ref):
        av = a_ref[...]
        if a_pre is not None:
            av = a_pre(av)
        return lax.dot_general(av, b_ref[...], dims, preferred_element_type=F32)

    def finish(r, ex, o_ref):
        if epi is not None:
            r = epi(r, *[e[...] for e in ex])
        o_ref[...] = r.astype(o_ref.dtype)

    def body_single(a_ref, b_ref, *rest):
        finish(product(a_ref, b_ref), rest[:n_ex], rest[n_ex + n_dep])

    def body(a_ref, b_ref, *rest):
        ex = rest[:n_ex]
        o_ref = rest[n_ex + n_dep]
        acc_ref = rest[n_ex + n_dep + 1]
        k = pl.program_id(2)

        @pl.when(k == 0)
        def _():
            acc_ref[...] = product(a_ref, b_ref)

        @pl.when((k > 0) & (k < nk - 1))
        def _():
            acc_ref[...] += product(a_ref, b_ref)

        @pl.when(k == nk - 1)
        def _():
            finish(acc_ref[...] + product(a_ref, b_ref), ex, o_ref)

    return pl.pallas_call(
        body_single if nk == 1 else body, name=name, grid=grid,
        in_specs=[a_spec, b_spec, *extra_specs] + [_hbm_spec()] * n_dep,
        out_specs=o_spec,
        out_shape=SDS(out_shape, out_dtype),
        scratch_shapes=[] if nk == 1 else [pltpu.VMEM(acc_shape, F32)],
        compiler_params=_cp(3),
    )(a, b, *extras, *deps)


def _relu2_bf16(t):
    r = jnp.maximum(t.astype(F32), 0.0)
    return (r * r).astype(BF16)


def _loss_kernel(y, t):
    S, D = y.shape
    tm = min(256, S)

    def body(y_ref, t_ref, dy_ref, dyb_ref, l_ref):
        @pl.when(pl.program_id(0) == 0)
        def _():
            l_ref[...] = jnp.zeros_like(l_ref)
        e = y_ref[...] - t_ref[...]
        l_ref[...] += jnp.sum(e * e, axis=0, keepdims=True)
        dy = e * (1.0 / D)
        dy_ref[...] = dy
        dyb_ref[...] = dy.astype(BF16)

    row = pl.BlockSpec((tm, D), lambda i: (i, 0))
    return pl.pallas_call(
        body, name="loss_head", grid=(S // tm,),
        in_specs=[row, row],
        out_specs=[row, row, pl.BlockSpec((1, D), lambda i: (0, 0))],
        out_shape=[SDS((S, D), F32), SDS((S, D), BF16), SDS((1, D), F32)],
        compiler_params=_cp(1),
    )(y, t)


def _rmsnorm_bwd(name, dh, x, g, dres, deps=()):
    S, D = x.shape
    tm = min(256, S)
    n_dep = len(deps)

    def body(dh_ref, x_ref, g_ref, dres_ref, *rest):
        dx_ref, dxb_ref, dg_ref = rest[n_dep:]
        @pl.when(pl.program_id(0) == 0)
        def _():
            dg_ref[...] = jnp.zeros_like(dg_ref)
        xv = x_ref[...]
        dhv = dh_ref[...]
        rstd = lax.rsqrt(jnp.mean(xv * xv, axis=-1, keepdims=True) + EPS)
        xhat = xv * rstd
        dg_ref[...] += jnp.sum(dhv * xhat, axis=0, keepdims=True)
        dxn = dhv * g_ref[...]
        dx = dres_ref[...] + rstd * (dxn - xhat * jnp.mean(dxn * xhat, axis=-1, keepdims=True))
        dx_ref[...] = dx
        dxb_ref[...] = dx.astype(BF16)

    row = pl.BlockSpec((tm, D), lambda i: (i, 0))
    vec = pl.BlockSpec((1, D), lambda i: (0, 0))
    return pl.pallas_call(
        body, name=name, grid=(S // tm,),
        in_specs=[row, row, vec, row] + [_hbm_spec()] * n_dep,
        out_specs=[row, row, vec],
        out_shape=[SDS((S, D), F32), SDS((S, D), BF16), SDS((1, D), F32)],
        compiler_params=_cp(1),
    )(dh, x, g, dres, *deps)


def _adamw(name, w, g, m, v):
    R, C = w.shape
    tr = 256 if R % 256 == 0 else R
    c1 = 1.0 - ADAM_B1 ** ADAM_STEP
    c2 = 1.0 - ADAM_B2 ** ADAM_STEP

    def body(w_ref, g_ref, m_ref, v_ref, d_ref, nm_ref, nv_ref, g_out_ref):
        gv = g_ref[...]
        nm = ADAM_B1 * m_ref[...] + (1.0 - ADAM_B1) * gv
        nv = ADAM_B2 * v_ref[...] + (1.0 - ADAM_B2) * (gv * gv)
        m_hat = nm / c1
        v_hat = nv / c2
        d_ref[...] = -ADAM_LR * (m_hat / (jnp.sqrt(v_hat) + ADAM_EPS) + ADAM_WD * w_ref[...])
        nm_ref[...] = nm
        nv_ref[...] = nv
        g_out_ref[...] = gv

    blk = pl.BlockSpec((tr, C), lambda i: (i, 0))
    return pl.pallas_call(
        body, name=name, grid=(R // tr,),
        in_specs=[blk] * 4, out_specs=[blk] * 4,
        out_shape=[SDS((R, C), F32)] * 4,
        compiler_params=_cp(1),
    )(w, g, m, v)


SGU_STEP_ROWS = 512


def _pair_select(lane, lo, hi):
    return jnp.where(lane < HEAD_DIM, lo, hi)


def _sgu_fwd(name, p, wt, bb):
    S = p.shape[0]

    rows = min(SGU_STEP_ROWS, S)

    def body(u_ref, v_ref, wt_ref, bb_ref, o_ref):
        lane = lax.broadcasted_iota(jnp.int32, (CHUNK, 128), 1)
        for ci in range(rows // CHUNK):
            rs = slice(CHUNK * ci, CHUNK * (ci + 1))
            for pp in range(A_HEADS // 2):
                cs = slice(128 * pp, 128 * (pp + 1))
                vb = v_ref[rs, cs].astype(BF16)
                mixed = _pair_select(lane,
                                     jnp.dot(wt_ref[2 * pp], vb, preferred_element_type=F32),
                                     jnp.dot(wt_ref[2 * pp + 1], vb, preferred_element_type=F32)) + bb_ref[:, cs]
                o_ref[rs, cs] = (u_ref[rs, cs] * mixed).astype(o_ref.dtype)

    return pl.pallas_call(
        body, name=name, grid=(S // rows,),
        in_specs=[pl.BlockSpec((rows, A_WIDTH), lambda c: (c, OFF_AU // A_WIDTH)),
                  pl.BlockSpec((rows, A_WIDTH), lambda c: (c, OFF_AV // A_WIDTH)),
                  pl.BlockSpec((A_HEADS, CHUNK, CHUNK), lambda c: (0, 0, 0)),
                  pl.BlockSpec((CHUNK, A_WIDTH), lambda c: (0, 0))],
        out_specs=pl.BlockSpec((rows, A_WIDTH), lambda c: (c, 0)),
        out_shape=SDS((S, A_WIDTH), BF16),
        compiler_params=_cp(1),
    )(p, p, wt, bb)


def _sgu_bwd(name, p, dycat, wt, wtt, bb):
    S = p.shape[0]
    rows = min(SGU_STEP_ROWS, S)

    def body(u_ref, v_ref, dy_ref, wt_ref, wtt_ref, bb_ref, du_ref, dv_ref, dw_ref, db_ref, dbacc_ref):
        c = pl.program_id(0)

        @pl.when(c == 0)
        def _():
            dw_ref[...] = jnp.zeros_like(dw_ref)
            dbacc_ref[...] = jnp.zeros_like(dbacc_ref)

        lane = lax.broadcasted_iota(jnp.int32, (CHUNK, 128), 1)
        row = lax.broadcasted_iota(jnp.int32, (CHUNK, 128), 0)
        causal = row >= lane
        nt = (((1,), (1,)), ((), ()))
        for pp in range(A_HEADS // 2):
            cs = slice(128 * pp, 128 * (pp + 1))
            dw_lo = jnp.zeros((CHUNK, CHUNK), F32)
            dw_hi = jnp.zeros((CHUNK, CHUNK), F32)
            dm_sum = jnp.zeros((CHUNK, 128), F32)
            for ci in range(rows // CHUNK):
                rs = slice(CHUNK * ci, CHUNK * (ci + 1))
                vb = v_ref[rs, cs].astype(BF16)
                dy = dy_ref[rs, cs]
                mixed = _pair_select(lane,
                                     jnp.dot(wt_ref[2 * pp], vb, preferred_element_type=F32),
                                     jnp.dot(wt_ref[2 * pp + 1], vb, preferred_element_type=F32)) + bb_ref[:, cs]
                du_ref[rs, cs] = (dy * mixed).astype(du_ref.dtype)
                dm = dy * u_ref[rs, cs]
                dmb = dm.astype(BF16)
                dv = _pair_select(lane,
                                  jnp.dot(wtt_ref[2 * pp], dmb, preferred_element_type=F32),
                                  jnp.dot(wtt_ref[2 * pp + 1], dmb, preferred_element_type=F32))
                dv_ref[rs, cs] = dv.astype(dv_ref.dtype)
                dm_sum += dm
                dm_lo = jnp.where(lane < HEAD_DIM, dm, 0.0).astype(BF16)
                dm_hi = jnp.where(lane >= HEAD_DIM, dm, 0.0).astype(BF16)
                dw_lo += lax.dot_general(dm_lo, vb, nt, preferred_element_type=F32)
                dw_hi += lax.dot_general(dm_hi, vb, nt, preferred_element_type=F32)
            dbacc_ref[:, cs] += dm_sum
            dw_ref[2 * pp] += jnp.where(causal, dw_lo, 0.0)
            dw_ref[2 * pp + 1] += jnp.where(causal, dw_hi, 0.0)

        @pl.when(c == S // rows - 1)
        def _():
            out = jnp.zeros((CHUNK, 128), F32)
            for pp in range(A_HEADS // 2):
                acc = dbacc_ref[:, 128 * pp:128 * (pp + 1)]
                s_lo = jnp.sum(jnp.where(lane < HEAD_DIM, acc, 0.0), axis=1, keepdims=True)
                s_hi = jnp.sum(jnp.where(lane >= HEAD_DIM, acc, 0.0), axis=1, keepdims=True)
                out = jnp.where(lane == 2 * pp, s_lo, out)
                out = jnp.where(lane == 2 * pp + 1, s_hi, out)
            db_ref[...] = out

    chunk = lambda col: pl.BlockSpec((rows, A_WIDTH), lambda c: (c, col))
    wspec = pl.BlockSpec((A_HEADS, CHUNK, CHUNK), lambda c: (0, 0, 0))
    return pl.pallas_call(
        body, name=name, grid=(S // rows,),
        in_specs=[chunk(OFF_AU // A_WIDTH), chunk(OFF_AV // A_WIDTH), chunk(0), wspec, wspec,
                  pl.BlockSpec((CHUNK, A_WIDTH), lambda c: (0, 0))],
        out_specs=[chunk(0), chunk(0), wspec, pl.BlockSpec((CHUNK, 128), lambda c: (0, 0))],
        out_shape=[SDS((S, A_WIDTH), BF16), SDS((S, A_WIDTH), BF16),
                   SDS((A_HEADS, CHUNK, CHUNK), F32), SDS((CHUNK, 128), F32)],
        scratch_shapes=[pltpu.VMEM((CHUNK, A_WIDTH), F32)],
        compiler_params=_cp(1),
    )(p, p, dycat, wt, wtt, bb)


CONV_HALO = 8
CONV_COLS = 256
CONV_ROWS = 1024


def _shift_down(a, halo, k):
    T = a.shape[0]
    row = lax.broadcasted_iota(jnp.int32, a.shape, 0)
    out = pltpu.roll(a, k, 0)
    for r in range(k):
        out = jnp.where(row == r, halo[CONV_HALO - k + r:CONV_HALO - k + r + 1, :], out)
    return out


def _shift_up(a, halo, k):
    T = a.shape[0]
    row = lax.broadcasted_iota(jnp.int32, a.shape, 0)
    out = pltpu.roll(a, T - k, 0)
    for r in range(k):
        out = jnp.where(row == T - k + r, halo[r:r + 1, :], out)
    return out


def _conv_specs(S, T):
    hb = T // CONV_HALO
    last = S // CONV_HALO - 1
    tile = lambda col0: pl.BlockSpec((T, CONV_COLS), lambda j, i: (i, col0 + j))
    prev = lambda col0: pl.BlockSpec((CONV_HALO, CONV_COLS), lambda j, i: (jnp.maximum(i * hb - 1, 0), col0 + j))
    nxt = lambda col0: pl.BlockSpec((CONV_HALO, CONV_COLS), lambda j, i: (jnp.minimum((i + 1) * hb, last), col0 + j))
    return tile, prev, nxt


def _conv_fwd(name, p, w):
    S = p.shape[0]
    T = min(CONV_ROWS, S)
    tile, prev, _ = _conv_specs(S, T)
    cb, cc, cx = OFF_BB // CONV_COLS, OFF_BC // CONV_COLS, OFF_BX // CONV_COLS

    def body(b_ref, c_ref, x_ref, ch_ref, xh_ref, w_ref, o_ref):
        i = pl.program_id(1)
        z = c_ref[...] * x_ref[...]
        zh = jnp.where(i > 0, ch_ref[...] * xh_ref[...], 0.0)
        z1 = _shift_down(z, zh, 1)
        z2 = _shift_down(z, zh, 2)
        conv = w_ref[0:1, :] * z2 + w_ref[1:2, :] * z1 + w_ref[2:3, :] * z
        o_ref[...] = (b_ref[...] * conv).astype(o_ref.dtype)

    return pl.pallas_call(
        body, name=name, grid=(B_WIDTH // CONV_COLS, S // T),
        in_specs=[tile(cb), tile(cc), tile(cx), prev(cc), prev(cx),
                  pl.BlockSpec((3, CONV_COLS), lambda j, i: (0, j))],
        out_specs=tile(0),
        out_shape=SDS((S, B_WIDTH), BF16),
        compiler_params=_cp(2),
    )(p, p, p, p, p, w)


def _conv_bwd(name, p, dycat, w):
    S = p.shape[0]
    T = min(CONV_ROWS, S)
    tile, prev, nxt = _conv_specs(S, T)
    cb, cc, cx = OFF_BB // CONV_COLS, OFF_BC // CONV_COLS, OFF_BX // CONV_COLS
    cdy = A_WIDTH // CONV_COLS
    n_i = S // T

    def body(b_ref, c_ref, x_ref, dy_ref, ch_ref, xh_ref, bn_ref, dyn_ref, w_ref,
             db_ref, dc_ref, dx_ref, dw_ref):
        i = pl.program_id(1)

        @pl.when(i == 0)
        def _():
            dw_ref[...] = jnp.zeros_like(dw_ref)

        cv = c_ref[...]
        xv = x_ref[...]
        z = cv * xv
        zh = jnp.where(i > 0, ch_ref[...] * xh_ref[...], 0.0)
        z1 = _shift_down(z, zh, 1)
        z2 = _shift_down(z, zh, 2)
        w0, w1, w2 = w_ref[0:1, :], w_ref[1:2, :], w_ref[2:3, :]
        conv = w0 * z2 + w1 * z1 + w2 * z
        dy = dy_ref[...]
        db_ref[...] = (dy * conv).astype(db_ref.dtype)
        dconv = dy * b_ref[...]
        dconv_n = jnp.where(i < n_i - 1, dyn_ref[...] * bn_ref[...], 0.0)
        dz = w2 * dconv + w1 * _shift_up(dconv, dconv_n, 1) + w0 * _shift_up(dconv, dconv_n, 2)
        dc_ref[...] = (dz * xv).astype(dc_ref.dtype)
        dx_ref[...] = (dz * cv).astype(dx_ref.dtype)
        dw_ref[0:1, :] += jnp.sum(dconv * z2, axis=0, keepdims=True)
        dw_ref[1:2, :] += jnp.sum(dconv * z1, axis=0, keepdims=True)
        dw_ref[2:3, :] += jnp.sum(dconv * z, axis=0, keepdims=True)

    wspec = pl.BlockSpec((3, CONV_COLS), lambda j, i: (0, j))
    return pl.pallas_call(
        body, name=name, grid=(B_WIDTH // CONV_COLS, n_i),
        in_specs=[tile(cb), tile(cc), tile(cx), tile(cdy), prev(cc), prev(cx), nxt(cb), nxt(cdy), wspec],
        out_specs=[tile(0), tile(0), tile(0), wspec],
        out_shape=[SDS((S, B_WIDTH), BF16)] * 3 + [SDS((3, B_WIDTH), F32)],
        compiler_params=_cp(2),
    )(p, p, p, dycat, p, p, p, dycat, w)


def _seg_sum(t, bd):
    hi = t.astype(BF16)
    lo = (t - hi.astype(F32)).astype(BF16)
    return jnp.dot(hi, bd, preferred_element_type=F32) + jnp.dot(lo, bd, preferred_element_type=F32)


def _head_norm(x, g, bd):
    rstd = lax.rsqrt(_seg_sum(x * x, bd) * (1.0 / HEAD_DIM) + EPS)
    xhat = x * rstd
    return xhat * g, xhat, rstd


def _head_norm_bwd(dy, g, xhat, rstd, bd):
    dxh = dy * g
    return rstd * (dxh - xhat * (_seg_sum(dxh * xhat, bd) * (1.0 / HEAD_DIM)))


def _band_mask(has_prev):
    row = lax.broadcasted_iota(jnp.int32, (BLK, 2 * BLK), 0)
    col = lax.broadcasted_iota(jnp.int32, (BLK, 2 * BLK), 1)
    first_key = jnp.where(has_prev, 0, BLK)
    return (col >= row) & (col <= row + BLK) & (col >= first_key)


def _first_of_segment(g, n, n_blocks):
    per_seg = lax.shift_right_logical(jnp.int32(n_blocks), 2 * g)
    return (n & (per_seg - 1)) == 0


def _residue_rows(r, d):
    return slice(None) if d == 1 else pl.ds(r, BLK, stride=d)


STRIDED_LANES = 128


def _step_width(d):
    return PW if d == 1 else STRIDED_LANES


def _n_stack(lane):
    return lane.shape[1] // HEAD_DIM


def _for_residues(d, fn):
    if d == 1:
        fn(0)
    else:
        def two(i, carry):
            fn(2 * i)
            fn(2 * i + 1)
            return carry
        lax.fori_loop(0, d // 2, two, 0)


def _head_mask(lane, j):
    return (lane >= HEAD_DIM * j) & (lane < HEAD_DIM * (j + 1))


def _stack_heads(x, lane):
    return jnp.concatenate([jnp.where(_head_mask(lane, j), x, 0.0) for j in range(_n_stack(lane))], axis=0)


def _unstack_heads(y, lane):
    out = y[:BLK]
    for j in range(1, _n_stack(lane)):
        out = jnp.where(lane >= HEAD_DIM * j, y[BLK * j:BLK * (j + 1)], out)
    return out


def _head_columns(v, lane):
    return jnp.concatenate([jnp.max(jnp.where(_head_mask(lane, j), v, NEG), axis=1, keepdims=True)
                            for j in range(_n_stack(lane))], axis=0)


def _attn_fwd(name, p, g, gq, gk, bd):
    S = p.shape[0]
    d = PATTERN_DILATION[g]
    rows = BLK * d
    hw = _step_width(d)
    nt = (((1,), (1,)), ((), ()))

    def body(q_ref, kc_ref, kp_ref, vc_ref, vp_ref, gq_ref, gk_ref, bd_ref, o_ref, lse_ref):
        has_prev = pl.program_id(1) > 0
        bdv = bd_ref[...]
        band = jnp.concatenate([_band_mask(has_prev)] * (hw // HEAD_DIM), axis=0)
        lane = lax.broadcasted_iota(jnp.int32, (1, hw), 1)

        def residue(r):
            rr = _residue_rows(r, d)
            qn, _, _ = _head_norm(q_ref[rr, :], gq_ref[...], bdv)
            kn, _, _ = _head_norm(jnp.concatenate([kp_ref[rr, :], kc_ref[rr, :]], axis=0), gk_ref[...], bdv)
            knb = kn.astype(BF16)
            vb = jnp.concatenate([vp_ref[rr, :], vc_ref[rr, :]], axis=0).astype(BF16)
            qs = _stack_heads(qn, lane).astype(BF16)
            s = lax.dot_general(qs, knb, nt, preferred_element_type=F32) * (HEAD_DIM ** -0.5)
            s = jnp.where(band, s, NEG)
            m = jnp.max(s, axis=1, keepdims=True)
            e = jnp.exp(s - m)
            den = jnp.sum(e, axis=1, keepdims=True)
            pv = jnp.dot(e.astype(BF16), vb, preferred_element_type=F32)
            o_ref[rr, :] = _unstack_heads(pv / den, lane)
            lse_ref[rr, :] = _unstack_heads(jnp.broadcast_to(m + jnp.log(den), pv.shape), lane)

        _for_residues(d, residue)

    per = PW // hw
    cq, ck, cv = (OFF_Q + PW * g) // hw, (OFF_K + PW * g) // hw, (OFF_V + PW * g) // hw
    cur = lambda col: pl.BlockSpec((rows, hw), lambda h, n: (n, col + h))
    prv = lambda col: pl.BlockSpec((rows, hw), lambda h, n: (jnp.maximum(n - 1, 0), col + h))
    vec = pl.BlockSpec((1, hw), lambda h, n: (0, h))
    return pl.pallas_call(
        body, name=name, grid=(per, S // rows),
        in_specs=[cur(cq), cur(ck), prv(ck), cur(cv), prv(cv), vec, vec, pl.BlockSpec((hw, hw), lambda h, n: (0, 0))],
        out_specs=[cur(0), cur(0)],
        out_shape=[SDS((S, PW), F32)] * 2,
        compiler_params=_cp(2),
    )(p, p, p, p, p, gq, gk, bd)


def _attn_bwd(name, p, g, lse, do3, c3, gq, gk, bd):
    S = p.shape[0]
    d = PATTERN_DILATION[g]
    rows = BLK * d
    nblk = S // rows
    hw = _step_width(d)
    nt = (((1,), (1,)), ((), ()))
    tn = (((0,), (0,)), ((), ()))

    def body(q_ref, kc_ref, kp_ref, vc_ref, vp_ref, lse_ref, do_ref, c_ref, gq_ref, gk_ref, bd_ref,
             dq_ref, dk_ref, dv_ref, dgq_ref, dgk_ref, ck_ref, cv_ref, dq_keep_ref):
        n = pl.program_id(1)

        @pl.when(n == 0)
        def _():
            ck_ref[...] = jnp.zeros_like(ck_ref)
            cv_ref[...] = jnp.zeros_like(cv_ref)
            dgq_ref[...] = jnp.zeros_like(dgq_ref)
            dgk_ref[...] = jnp.zeros_like(dgk_ref)

        @pl.when(n == nblk)
        def _():
            dq_ref[...] = dq_keep_ref[...]
            dk_ref[...] = ck_ref[...]
            dv_ref[...] = cv_ref[...]

        bdv = bd_ref[...]
        gqv = gq_ref[...]
        gkv = gk_ref[...]
        band = jnp.concatenate([_band_mask(n > 0)] * (hw // HEAD_DIM), axis=0)
        lane = lax.broadcasted_iota(jnp.int32, (1, hw), 1)

        def residue(r):
            rr = _residue_rows(r, d)
            qn, qhat, qrstd = _head_norm(q_ref[rr, :], gqv, bdv)
            kn, khat, krstd = _head_norm(jnp.concatenate([kp_ref[rr, :], kc_ref[rr, :]], axis=0), gkv, bdv)
            knb = kn.astype(BF16)
            vb = jnp.concatenate([vp_ref[rr, :], vc_ref[rr, :]], axis=0).astype(BF16)
            qs = _stack_heads(qn, lane).astype(BF16)
            dos = _stack_heads(do_ref[rr, :], lane).astype(BF16)
            s = lax.dot_general(qs, knb, nt, preferred_element_type=F32) * (HEAD_DIM ** -0.5)
            prob = jnp.where(band, jnp.exp(s - _head_columns(lse_ref[rr, :], lane)), 0.0)
            dp = lax.dot_general(dos, vb, nt, preferred_element_type=F32)
            ds = (prob * (dp + _head_columns(c_ref[rr, :], lane)) * (HEAD_DIM ** -0.5)).astype(BF16)
            dqn = _unstack_heads(jnp.dot(ds, knb, preferred_element_type=F32), lane)
            dkn = lax.dot_general(ds, qs, tn, preferred_element_type=F32)
            dvv = lax.dot_general(prob.astype(BF16), dos, tn, preferred_element_type=F32)

            dq = _head_norm_bwd(dqn, gqv, qhat, qrstd, bdv)
            dq_ref[rr, :] = dq
            dq_keep_ref[rr, :] = dq
            dk2 = _head_norm_bwd(dkn, gkv, khat, krstd, bdv)
            dgq_ref[...] += jnp.sum(dqn * qhat, axis=0, keepdims=True)
            dgk_ref[...] += jnp.sum(dkn * khat, axis=0, keepdims=True)
            dk_ref[rr, :] = ck_ref[rr, :] + dk2[:BLK]
            dv_ref[rr, :] = cv_ref[rr, :] + dvv[:BLK]
            ck_ref[rr, :] = dk2[BLK:]
            cv_ref[rr, :] = dvv[BLK:]

        @pl.when(n < nblk)
        def _():
            _for_residues(d, residue)

    last = nblk - 1
    per = PW // hw
    cq, ck, cv = (OFF_Q + PW * g) // hw, (OFF_K + PW * g) // hw, (OFF_V + PW * g) // hw
    cur = lambda col: pl.BlockSpec((rows, hw), lambda h, n: (jnp.minimum(n, last), col + h))
    prv = lambda col: pl.BlockSpec((rows, hw), lambda h, n: (jnp.maximum(jnp.minimum(n, last) - 1, 0), col + h))
    cur3 = pl.BlockSpec((None, rows, hw), lambda h, n: (g, jnp.minimum(n, last), h))
    done = pl.BlockSpec((rows, hw), lambda h, n: (jnp.maximum(n - 1, 0), h))
    vec = pl.BlockSpec((1, hw), lambda h, n: (0, h))
    return pl.pallas_call(
        body, name=name, grid=(per, nblk + 1),
        in_specs=[cur(cq), cur(ck), prv(ck), cur(cv), prv(cv), cur(0), cur3, cur3, vec, vec,
                  pl.BlockSpec((hw, hw), lambda h, n: (0, 0))],
        out_specs=[cur(0), done, done, vec, vec],
        out_shape=[SDS((S, PW), F32)] * 3 + [SDS((1, PW), F32)] * 2,
        scratch_shapes=[pltpu.VMEM((rows, hw), F32)] * 3,
        compiler_params=_cp(2),
    )(p, p, p, p, p, lse, do3, c3, gq, gk, bd)


def _mix_fwd(name, os, lses):
    S = os[0].shape[0]
    tm = min(512, S)

    def body(o0, o1, o2, l0, l1, l2, y_ref):
        o = [o0[...], o1[...], o2[...]]
        l = [l0[...], l1[...], l2[...]]
        m = jnp.maximum(jnp.maximum(l[0], l[1]), l[2])
        e = [jnp.exp(t - m) for t in l]
        inv = 1.0 / (e[0] + e[1] + e[2])
        for g in range(N_PATTERNS):
            y_ref[:, PW * g:PW * (g + 1)] = (o[g] * (e[g] * inv)).astype(y_ref.dtype)

    blk = pl.BlockSpec((tm, PW), lambda i: (i, 0))
    return pl.pallas_call(
        body, name=name, grid=(S // tm,),
        in_specs=[blk] * 6,
        out_specs=pl.BlockSpec((tm, C_WIDTH), lambda i: (i, 0)),
        out_shape=SDS((S, C_WIDTH), BF16),
        compiler_params=_cp(1),
    )(*os, *lses)


def _mix_bwd(name, os, lses, dycat, bd):
    S = os[0].shape[0]
    tm = min(512, S)
    c0 = (A_WIDTH + B_WIDTH) // PW

    def body(o0, o1, o2, l0, l1, l2, dy0_ref, dy1_ref, dy2_ref, bd_ref, do_ref, c_ref):
        bdv = bd_ref[...]
        o = [o0[...], o1[...], o2[...]]
        l = [l0[...], l1[...], l2[...]]
        dys = [dy0_ref[...], dy1_ref[...], dy2_ref[...]]
        m = jnp.maximum(jnp.maximum(l[0], l[1]), l[2])
        e = [jnp.exp(t - m) for t in l]
        inv = 1.0 / (e[0] + e[1] + e[2])
        alpha = [t * inv for t in e]
        da = [_seg_sum(dys[g] * o[g], bdv) for g in range(N_PATTERNS)]
        mean_da = alpha[0] * da[0] + alpha[1] * da[1] + alpha[2] * da[2]
        for g in range(N_PATTERNS):
            do_ref[g] = dys[g] * alpha[g]
            c_ref[g] = -alpha[g] * mean_da

    blk = pl.BlockSpec((tm, PW), lambda i: (i, 0))
    blk3 = pl.BlockSpec((N_PATTERNS, tm, PW), lambda i: (0, i, 0))
    dyspec = lambda g: pl.BlockSpec((tm, PW), lambda i: (i, c0 + g))
    return pl.pallas_call(
        body, name=name, grid=(S // tm,),
        in_specs=[blk] * 6 + [dyspec(0), dyspec(1), dyspec(2), pl.BlockSpec((PW, PW), lambda i: (0, 0))],
        out_specs=[blk3, blk3],
        out_shape=[SDS((N_PATTERNS, S, PW), F32)] * 2,
        compiler_params=_cp(1),
    )(*os, *lses, dycat, dycat, dycat, bd)


def _mesh_pos():
    x, y, c = lax.axis_index("x"), lax.axis_index("y"), lax.axis_index("c")
    chips = [(1 - x, y), (x, 1 - y), (1 - x, 1 - y)]
    chip_idx = [2 * cx + cy for cx, cy in chips]
    return x, y, c, 2 * x + y, chips, chip_idx


def _place_shard(name, w, layer, chip_arr, out_dtype, deps=()):
    _, R, C = w.shape
    tr = min(256, R)

    def body(chip_ref, w_ref, *rest):
        o_ref = rest[-1]
        o_ref[...] = w_ref[...].astype(o_ref.dtype)

    return pl.pallas_call(
        body, name=name,
        grid_spec=pltpu.PrefetchScalarGridSpec(
            num_scalar_prefetch=1, grid=(R // tr,),
            in_specs=[pl.BlockSpec((None, tr, C), lambda i, chip_ref: (layer, i, 0))] + [_hbm_spec()] * len(deps),
            out_specs=pl.BlockSpec((None, tr, C), lambda i, chip_ref: (chip_ref[0], i, 0))),
        out_shape=SDS((N_CHIPS, R, C), out_dtype),
        compiler_params=_cp(1),
    )(chip_arr, w, *deps)


HBM_SPEC = pl.BlockSpec(memory_space=pltpu.HBM)
SEM_SPEC = pl.BlockSpec(memory_space=pltpu.SEMAPHORE)
SPLIT_COPY = pltpu.SideEffectType.DATAFLOW_SIDE_EFFECTING
N_PEER_CHIPS = N_CHIPS - 1
TOKEN_SHAPE = SDS((8, 128), F32)
TOKEN_SPEC = pl.BlockSpec(memory_space=pltpu.VMEM)


def _in_hbm(a):
    return pltpu.with_memory_space_constraint(a, pltpu.HBM)


def _gather_start(name, bufs):
    T = len(bufs)

    def body(*refs):
        ins = refs[:T]
        send_sems, recv_sems = refs[T:2 * T], refs[2 * T:3 * T]
        token = refs[4 * T]
        x, y, c, me, chips, chip_idx = _mesh_pos()
        for t in range(T):
            hr = ins[t].shape[1] // 2
            mine = ins[t].at[me, pl.ds(c * hr, hr), :]
            for j in range(N_PEER_CHIPS):
                pltpu.make_async_remote_copy(src_ref=mine, dst_ref=mine, send_sem=send_sems[t].at[j],
                                             recv_sem=recv_sems[t].at[j], device_id=(*chips[j], c),
                                             device_id_type=MESH).start()
        token[...] = jnp.zeros_like(token)

    sems = [pltpu.SemaphoreType.DMA((N_PEER_CHIPS,))] * T
    out = pl.pallas_call(
        body, name=name,
        in_specs=[HBM_SPEC] * T,
        out_specs=[SEM_SPEC] * (2 * T) + [HBM_SPEC] * T + [TOKEN_SPEC],
        out_shape=sems + sems + [pltpu.HBM(b.shape, b.dtype) for b in bufs] + [TOKEN_SHAPE],
        input_output_aliases={t: 2 * T + t for t in range(T)},
        compiler_params=pltpu.CompilerParams(has_side_effects=SPLIT_COPY),
    )(*[_in_hbm(b) for b in bufs])
    return out[:T], out[T:2 * T], out[2 * T:3 * T], out[3 * T]


def _gather_wait(name, buf, send_sem, recv_sem, after):
    n_in = 3 if after is None else 4

    def body(*refs):
        buf_ref, ssem, rsem = refs[:3]
        x, y, c, me, chips, chip_idx = _mesh_pos()
        hr = buf_ref.shape[1] // 2
        mine = buf_ref.at[me, pl.ds(c * hr, hr), :]
        for j in range(N_PEER_CHIPS):
            got = buf_ref.at[chip_idx[j], pl.ds(c * hr, hr), :]
            cp = pltpu.make_async_remote_copy(src_ref=mine, dst_ref=got, send_sem=ssem.at[j], recv_sem=rsem.at[j],
                                              device_id=(*chips[j], c), device_id_type=MESH)
            cp.wait_send()
            cp.wait_recv()

    args = [buf, send_sem, recv_sem] + ([] if after is None else [after])
    return pl.pallas_call(
        body, name=name,
        in_specs=[HBM_SPEC, SEM_SPEC, SEM_SPEC] + [_hbm_spec()] * (n_in - 3),
        out_specs=HBM_SPEC,
        out_shape=pltpu.HBM(buf.shape, buf.dtype),
        input_output_aliases={0: 0},
        compiler_params=pltpu.CompilerParams(has_side_effects=SPLIT_COPY),
    )(*args)


def _forward_start(name, buf):
    def body(buf_ref, send_sems, recv_sems, buf_thru, token):
        x, y, c, me, chips, chip_idx = _mesh_pos()
        hr = buf_ref.shape[1] // 2
        for j in range(N_PEER_CHIPS):
            got = buf_ref.at[chip_idx[j], pl.ds(c * hr, hr), :]
            pltpu.make_async_remote_copy(src_ref=got, dst_ref=got, send_sem=send_sems.at[j], recv_sem=recv_sems.at[j],
                                         device_id=(x, y, 1 - c), device_id_type=MESH).start()
        token[...] = jnp.zeros_like(token)

    sems = pltpu.SemaphoreType.DMA((N_PEER_CHIPS,))
    return pl.pallas_call(
        body, name=name,
        in_specs=[HBM_SPEC],
        out_specs=[SEM_SPEC, SEM_SPEC, HBM_SPEC, TOKEN_SPEC],
        out_shape=[sems, sems, pltpu.HBM(buf.shape, buf.dtype), TOKEN_SHAPE],
        input_output_aliases={0: 2},
        compiler_params=pltpu.CompilerParams(has_side_effects=SPLIT_COPY),
    )(_in_hbm(buf))


def _forward_wait(name, buf, send_sems, recv_sems, after):
    n_in = 3 if after is None else 4

    def body(*refs):
        buf_ref, ssems, rsems = refs[:3]
        x, y, c, me, chips, chip_idx = _mesh_pos()
        hr = buf_ref.shape[1] // 2
        for j in range(N_PEER_CHIPS):
            sent = buf_ref.at[chip_idx[j], pl.ds(c * hr, hr), :]
            theirs = buf_ref.at[chip_idx[j], pl.ds((1 - c) * hr, hr), :]
            cp = pltpu.make_async_remote_copy(src_ref=sent, dst_ref=theirs, send_sem=ssems.at[j],
                                              recv_sem=rsems.at[j], device_id=(x, y, 1 - c), device_id_type=MESH)
            cp.wait_send()
            cp.wait_recv()

    args = [buf, send_sems, recv_sems] + ([] if after is None else [after])
    return pl.pallas_call(
        body, name=name,
        in_specs=[HBM_SPEC, SEM_SPEC, SEM_SPEC] + [_hbm_spec()] * (n_in - 3),
        out_specs=HBM_SPEC,
        out_shape=pltpu.HBM(buf.shape, buf.dtype),
        input_output_aliases={0: 0},
        compiler_params=pltpu.CompilerParams(has_side_effects=SPLIT_COPY),
    )(*args)


class _GatheredWeights:
    def __init__(self):
        self._order = []
        self._pending = {}
        self._forwarding = {}
        self._ready = {}
        self._tokens = []

    def start(self, keys, bufs):
        send_sems, recv_sems, thru, token = _gather_start(f"gather_start_{len(self._order)}", bufs)
        self._tokens.append(token)
        self._order.extend(keys)
        self._pending.update({k: (b, s, r) for k, b, s, r in zip(keys, thru, send_sems, recv_sems)})

    def _prefetch(self, key, after):
        if key in self._pending:
            buf, ssem, rsem = self._pending.pop(key)
            tag = f"{key[0]}_{key[1]}"
            buf = _gather_wait(f"gather_wait_{tag}", buf, ssem, rsem, after)
            ssems, rsems, buf, token = _forward_start(f"gather_fwd_start_{tag}", buf)
            self._forwarding[key] = (buf, ssems, rsems)
            self._tokens.append(token)

    def get(self, name, layer, after=None):
        key = (name, layer)
        if key not in self._ready:
            self._prefetch(key, after)
            buf, ssems, rsems = self._forwarding.pop(key)
            self._ready[key] = _forward_wait(f"gather_fwd_wait_{name}_{layer}", buf, ssems, rsems, after)
            nxt = self._order.index(key) + 1
            if nxt < len(self._order):
                self._prefetch(self._order[nxt], after)
        return self._ready[key]

    def deps(self):
        tokens, self._tokens = self._tokens, []
        return tokens


def _swap_copy(g_ref, land_ref, send_sem, recv_sem):
    x, y, c, _, _, _ = _mesh_pos()
    hr = g_ref.shape[1] // 2
    return pltpu.make_async_remote_copy(src_ref=g_ref.at[:, pl.ds((1 - c) * hr, hr), :], dst_ref=land_ref,
                                        send_sem=send_sem, recv_sem=recv_sem, device_id=(x, y, 1 - c),
                                        device_id_type=MESH)


def _swap_start(name, g):
    land_shape = (g.shape[0], g.shape[1] // 2, g.shape[2])

    def body(g_ref, land_ref, send_sem, recv_sem, land_thru, token):
        _swap_copy(g_ref, land_ref, send_sem, recv_sem).start()
        token[...] = jnp.zeros_like(token)

    return pl.pallas_call(
        body, name=name,
        in_specs=[HBM_SPEC, HBM_SPEC],
        out_specs=[SEM_SPEC, SEM_SPEC, HBM_SPEC, TOKEN_SPEC],
        out_shape=[pltpu.SemaphoreType.DMA(()), pltpu.SemaphoreType.DMA(()), pltpu.HBM(land_shape, g.dtype),
                   TOKEN_SHAPE],
        input_output_aliases={1: 2},
        compiler_params=pltpu.CompilerParams(has_side_effects=SPLIT_COPY),
    )(_in_hbm(g), _in_hbm(lax.empty(land_shape, g.dtype)))


def _swap_wait(name, g, land, send_sem, recv_sem, after):
    def body(g_ref, land_ref, send_sem, recv_sem, after_ref, land_out):
        cp = _swap_copy(g_ref, land_ref, send_sem, recv_sem)
        cp.wait_send()
        cp.wait_recv()

    return pl.pallas_call(
        body, name=name,
        in_specs=[HBM_SPEC, HBM_SPEC, SEM_SPEC, SEM_SPEC, _hbm_spec()],
        out_specs=HBM_SPEC,
        out_shape=pltpu.HBM(land.shape, land.dtype),
        input_output_aliases={1: 0},
        compiler_params=pltpu.CompilerParams(has_side_effects=SPLIT_COPY),
    )(_in_hbm(g), land, send_sem, recv_sem, after)


def _add_my_half(name, g, r, pos_arr):
    ns, R, C = g.shape
    hr = R // 2
    tr = min(256, hr)
    nt = hr // tr

    def body(pos_ref, g_ref, r_ref, o_ref, land_ref):
        t = (g_ref[...] + r_ref[...]).astype(o_ref.dtype)
        o_ref[...] = t

        @pl.when(pl.program_id(1) == pos_ref[1])
        def _():
            land_ref[...] = t

    blk = pl.BlockSpec((None, tr, C), lambda i, s, pos_ref: (s, i, 0))
    return pl.pallas_call(
        body, name=name,
        grid_spec=pltpu.PrefetchScalarGridSpec(
            num_scalar_prefetch=1, grid=(nt, ns),
            in_specs=[pl.BlockSpec((None, tr, C), lambda i, s, pos_ref: (s, pos_ref[0] * nt + i, 0)), blk],
            out_specs=[blk, pl.BlockSpec((None, tr, C), lambda i, s, pos_ref: (pos_ref[1], i, 0))]),
        out_shape=[SDS((ns, hr, C), BF16)] * 2,
        compiler_params=_cp(2),
    )(pos_arr, g, r)


def _exchange_start(name, part, land):
    def body(part_ref, land_ref, send_sems, recv_sems, land_thru, token):
        x, y, c, me, chips, chip_idx = _mesh_pos()
        for j in range(N_PEER_CHIPS):
            pltpu.make_async_remote_copy(src_ref=part_ref.at[chip_idx[j]], dst_ref=land_ref.at[me],
                                         send_sem=send_sems.at[j], recv_sem=recv_sems.at[j],
                                         device_id=(*chips[j], c), device_id_type=MESH).start()
        token[...] = jnp.zeros_like(token)

    sems = pltpu.SemaphoreType.DMA((N_PEER_CHIPS,))
    return pl.pallas_call(
        body, name=name,
        in_specs=[HBM_SPEC, HBM_SPEC],
        out_specs=[SEM_SPEC, SEM_SPEC, HBM_SPEC, TOKEN_SPEC],
        out_shape=[sems, sems, pltpu.HBM(land.shape, land.dtype), TOKEN_SHAPE],
        input_output_aliases={1: 2},
        compiler_params=pltpu.CompilerParams(has_side_effects=SPLIT_COPY),
    )(_in_hbm(part), _in_hbm(land))


def _exchange_wait(name, part, land, send_sems, recv_sems, after):
    def body(part_ref, land_ref, send_sems, recv_sems, after_ref, land_out):
        x, y, c, me, chips, chip_idx = _mesh_pos()
        for j in range(N_PEER_CHIPS):
            cp = pltpu.make_async_remote_copy(src_ref=part_ref.at[chip_idx[j]], dst_ref=land_ref.at[chip_idx[j]],
                                              send_sem=send_sems.at[j], recv_sem=recv_sems.at[j],
                                              device_id=(*chips[j], c), device_id_type=MESH)
            cp.wait_send()
            cp.wait_recv()

    return pl.pallas_call(
        body, name=name,
        in_specs=[HBM_SPEC, HBM_SPEC, SEM_SPEC, SEM_SPEC, _hbm_spec()],
        out_specs=HBM_SPEC,
        out_shape=pltpu.HBM(land.shape, land.dtype),
        input_output_aliases={1: 0},
        compiler_params=pltpu.CompilerParams(has_side_effects=SPLIT_COPY),
    )(_in_hbm(part), land, send_sems, recv_sems, after)


class _GradReducer:
    def __init__(self, c_arr):
        self._c_arr = c_arr
        self._swapping = []
        self._exchanging = {}
        self._joining = {}
        self._tokens = []

    def begin(self, name, layer, g):
        tag = f"{name}_{layer}"
        ssem, rsem, land, token = _swap_start(f"rs_swap_start_{tag}", g)
        self._swapping.append((name, layer, g, ssem, rsem, land))
        self._tokens.append(token)

    def advance(self, after):
        for name, layer, g, ssem, rsem, land in self._swapping:
            tag = f"{name}_{layer}"
            theirs = _swap_wait(f"rs_swap_wait_{tag}", g, land, ssem, rsem, after)
            part, own = _add_my_half(f"rs_add_{tag}", g, theirs, self._c_arr)
            ssems, rsems, land2, token = _exchange_start(f"rs_xchg_start_{tag}", part, own)
            self._exchanging[(name, layer)] = (part, ssems, rsems, land2)
            self._tokens.append(token)
        self._swapping = []

    def deps(self):
        tokens, self._tokens = self._tokens, []
        return tokens

    def reduce(self, name, n_layers, after):
        buf = None
        for layer in range(n_layers):
            part, ssems, rsems, land = self._exchanging.pop((name, layer))
            tag = f"{name}_{layer}"
            landed = _exchange_wait(f"rs_xchg_wait_{tag}", part, land, ssems, rsems, after)
            buf = _sum_chips(f"rs_sum_{tag}", landed, self._c_arr, layer, n_layers, buf)
        ssem, rsem, buf, token = _join_start(f"rs_join_start_{name}", buf)
        self._joining[name] = (buf, ssem, rsem)
        return token

    def reduced(self, name, after):
        buf, ssem, rsem = self._joining.pop(name)
        return _join_wait(f"rs_join_wait_{name}", buf, ssem, rsem, after)


def _sum_chips(name, r, c_arr, layer, n_layers, prev):
    ns, H, C = r.shape
    tr = min(256, H)
    nt = H // tr

    def body(c_ref, r_ref, *rest):
        o_ref = rest[-1]
        o_ref[...] = ((r_ref[0].astype(F32) + r_ref[1].astype(F32)) + r_ref[2].astype(F32)) + r_ref[3].astype(F32)

    in_specs = [pl.BlockSpec((ns, tr, C), lambda i, c_ref: (0, i, 0))]
    args = [c_arr, r]
    aliases = {}
    if prev is not None:
        in_specs.append(_hbm_spec())
        args.append(prev)
        aliases = {2: 0}
    return pl.pallas_call(
        body, name=name,
        grid_spec=pltpu.PrefetchScalarGridSpec(
            num_scalar_prefetch=1, grid=(nt,), in_specs=in_specs,
            out_specs=pl.BlockSpec((None, tr, C), lambda i, c_ref: (layer, c_ref[0] * nt + i, 0))),
        out_shape=SDS((n_layers, 2 * H, C), F32),
        input_output_aliases=aliases,
        compiler_params=_cp(1),
    )(*args)


def _join_copy(buf_ref, send_sem, recv_sem):
    x, y, c, _, _, _ = _mesh_pos()
    hr = buf_ref.shape[1] // 2
    mine = buf_ref.at[:, pl.ds(c * hr, hr), :]
    theirs = buf_ref.at[:, pl.ds((1 - c) * hr, hr), :]
    send = pltpu.make_async_remote_copy(src_ref=mine, dst_ref=mine, send_sem=send_sem, recv_sem=recv_sem,
                                        device_id=(x, y, 1 - c), device_id_type=MESH)
    arrive = pltpu.make_async_remote_copy(src_ref=theirs, dst_ref=theirs, send_sem=send_sem, recv_sem=recv_sem,
                                          device_id=(x, y, 1 - c), device_id_type=MESH)
    return send, arrive


def _join_start(name, buf):
    def body(buf_ref, send_sem, recv_sem, buf_thru, token):
        _join_copy(buf_ref, send_sem, recv_sem)[0].start()
        token[...] = jnp.zeros_like(token)

    return pl.pallas_call(
        body, name=name,
        in_specs=[HBM_SPEC],
        out_specs=[SEM_SPEC, SEM_SPEC, HBM_SPEC, TOKEN_SPEC],
        out_shape=[pltpu.SemaphoreType.DMA(()), pltpu.SemaphoreType.DMA(()), pltpu.HBM(buf.shape, buf.dtype),
                   TOKEN_SHAPE],
        input_output_aliases={0: 2},
        compiler_params=pltpu.CompilerParams(has_side_effects=SPLIT_COPY),
    )(_in_hbm(buf))


def _join_wait(name, buf, send_sem, recv_sem, after):
    def body(buf_ref, send_sem, recv_sem, after_ref, buf_out):
        send, arrive = _join_copy(buf_ref, send_sem, recv_sem)
        send.wait_send()
        arrive.wait_recv()

    return pl.pallas_call(
        body, name=name,
        in_specs=[HBM_SPEC, SEM_SPEC, SEM_SPEC, _hbm_spec()],
        out_specs=HBM_SPEC,
        out_shape=pltpu.HBM(buf.shape, buf.dtype),
        input_output_aliases={0: 0},
        compiler_params=pltpu.CompilerParams(has_side_effects=SPLIT_COPY),
    )(buf, send_sem, recv_sem, after)


def _small_copy(k, buf_ref, land_ref, send_sems, recv_sems):
    x, y, c = lax.axis_index("x"), lax.axis_index("y"), lax.axis_index("c")
    me = 4 * x + 2 * y + c
    peer = (x ^ ((k >> 2) & 1), y ^ ((k >> 1) & 1), c ^ (k & 1))
    cp = pltpu.make_async_remote_copy(src_ref=buf_ref, dst_ref=land_ref.at[me], send_sem=send_sems.at[k - 1],
                                      recv_sem=recv_sems.at[k - 1], device_id=peer, device_id_type=MESH)
    return me, peer, cp


def _small_start(buf, deps):
    land = jnp.broadcast_to(buf[None], (N_DEV,) + buf.shape)
    n_dep = len(deps)

    def body(buf_ref, land_ref, *rest):
        send_sems, recv_sems, _, token = rest[n_dep:]
        for k in range(1, N_DEV):
            _small_copy(k, buf_ref, land_ref, send_sems, recv_sems)[2].start()
        token[...] = jnp.zeros_like(token)

    sems = pltpu.SemaphoreType.DMA((N_DEV - 1,))
    return pl.pallas_call(
        body, name="small_gather_start",
        in_specs=[HBM_SPEC, HBM_SPEC] + [_hbm_spec()] * n_dep,
        out_specs=[SEM_SPEC, SEM_SPEC, HBM_SPEC, TOKEN_SPEC],
        out_shape=[sems, sems, pltpu.HBM(land.shape, land.dtype), TOKEN_SHAPE],
        input_output_aliases={1: 2},
        compiler_params=pltpu.CompilerParams(has_side_effects=SPLIT_COPY),
    )(_in_hbm(buf), _in_hbm(land), *deps)


def _small_wait(buf, land, send_sems, recv_sems, after):
    def body(buf_ref, land_ref, send_sems, recv_sems, after_ref, land_out):
        for k in range(1, N_DEV):
            me, peer, cp = _small_copy(k, buf_ref, land_ref, send_sems, recv_sems)
            cp.wait_send()
            got = land_ref.at[me ^ k]
            pltpu.make_async_remote_copy(src_ref=got, dst_ref=got, send_sem=send_sems.at[k - 1],
                                         recv_sem=recv_sems.at[k - 1], device_id=peer,
                                         device_id_type=MESH).wait_recv()

    return pl.pallas_call(
        body, name="small_gather_wait",
        in_specs=[HBM_SPEC, HBM_SPEC, SEM_SPEC, SEM_SPEC, _hbm_spec()],
        out_specs=HBM_SPEC,
        out_shape=pltpu.HBM(land.shape, land.dtype),
        input_output_aliases={1: 0},
        compiler_params=pltpu.CompilerParams(has_side_effects=SPLIT_COPY),
    )(_in_hbm(buf), land, send_sems, recv_sems, after)


def _sum_devices(land):
    n, R, C = land.shape

    def body(land_ref, out_ref):
        acc = land_ref[0]
        for d in range(1, n):
            acc = acc + land_ref[d]
        out_ref[...] = acc

    return pl.pallas_call(
        body, name="small_sum",
        in_specs=[pl.BlockSpec(memory_space=pltpu.VMEM)],
        out_specs=pl.BlockSpec(memory_space=pltpu.VMEM),
        out_shape=SDS((R, C), land.dtype),
        compiler_params=pltpu.CompilerParams(vmem_limit_bytes=V7X_VMEM_LIMIT),
    )(land)


def _deinterleave(t, d):
    if d == 1:
        return t
    S, W = t.shape
    return t.reshape(S // d, d, W).transpose(1, 0, 2).reshape(S, W)


def _interleave(t, d):
    if d == 1:
        return t
    S, W = t.shape
    return t.reshape(d, S // d, W).transpose(1, 0, 2).reshape(S, W)


def _to_patterns(t, off):
    return jnp.stack([_deinterleave(t[:, off + PW * g:off + PW * (g + 1)], PATTERN_DILATION[g])
                      for g in range(N_PATTERNS)])


def _from_patterns(t3):
    return jnp.stack([_interleave(t3[g], PATTERN_DILATION[g]) for g in range(N_PATTERNS)])


def _pack_rows(vectors):
    flat = jnp.concatenate([v.reshape(-1) for v in vectors])
    n = flat.shape[0]
    padded = -(-n // 1024) * 1024
    return jnp.pad(flat, (0, padded - n)).reshape(padded // 128, 128)


def _unpack_rows(buf, shapes):
    flat = buf.reshape(-1)
    out, off = [], 0
    for s in shapes:
        n = 1
        for dim in s:
            n *= dim
        out.append(flat[off:off + n].reshape(s))
        off += n
    return out


def _layer_forward(l, x, prm, wg):
    S, D = x.shape
    w_in = wg.get("w_in", l, x)
    p, h = _norm_matmul(f"in_proj_{l}", x, prm["attn_norm"][l], w_in, F32, deps=wg.deps())
    y_a = _sgu_fwd(f"sgu_fwd_{l}", p, prm["sgu_wt"][l], prm["sgu_bb"][l])
    y_b = _conv_fwd(f"conv_fwd_{l}", p, prm["conv_w"][l])
    os, lses = [], []
    for g in range(N_PATTERNS):
        o_g, lse_g = _attn_fwd(f"attn_fwd_{l}_{g}", p, g, prm["q_gain"][l], prm["k_gain"][l], prm["bd"])
        os.append(o_g)
        lses.append(lse_g)
    y_c = _mix_fwd(f"mix_fwd_{l}", os, lses)
    ycat = jnp.concatenate([y_a, y_b, y_c], axis=1)
    tmb, tnb = min(1024, S), min(1024, D)
    w_out = wg.get("w_out", l, ycat)
    rq = w_out.shape[1]
    x1 = _matmul(
        f"out_proj_{l}", ycat, w_out, (S, D), F32, grid=(S // tmb, D // tnb, N_CHIPS),
        a_spec=pl.BlockSpec((tmb, rq), lambda i, j, k: (i, k)),
        b_spec=pl.BlockSpec((None, rq, tnb), lambda i, j, k: (k, 0, j)),
        o_spec=pl.BlockSpec((tmb, tnb), lambda i, j, k: (i, j)),
        contract=(1, 0), acc_shape=(tmb, tnb),
        extras=(x,), extra_specs=(pl.BlockSpec((tmb, tnb), lambda i, j, k: (i, j)),),
        epi=lambda r, res: r + res, deps=wg.deps())
    w_mlp_in = wg.get("w_mlp_in", l, x1)
    a, h2 = _norm_matmul(f"mlp_in_{l}", x1, prm["mlp_norm"][l], w_mlp_in, BF16, deps=wg.deps())
    w_mlp_out = wg.get("w_mlp_out", l, a)
    dff4 = w_mlp_out.shape[1]
    tk = min(1024, dff4)
    kpc = dff4 // tk
    x2 = _matmul(
        f"mlp_out_{l}", a, w_mlp_out, (S, D), F32, grid=(S // tmb, D // tnb, N_CHIPS * kpc),
        a_spec=pl.BlockSpec((tmb, tk), lambda i, j, k: (i, k)),
        b_spec=pl.BlockSpec((None, tk, tnb), lambda i, j, k: (k // kpc, k % kpc, j)),
        o_spec=pl.BlockSpec((tmb, tnb), lambda i, j, k: (i, j)),
        contract=(1, 0), acc_shape=(tmb, tnb), a_pre=_relu2_bf16,
        extras=(x1,), extra_specs=(pl.BlockSpec((tmb, tnb), lambda i, j, k: (i, j)),),
        epi=lambda r, res: r + res, deps=wg.deps())
    saved = dict(x=x, p=p, h=h, os=os, lses=lses, ycat=ycat, x1=x1, a=a, h2=h2)
    return x2, saved


def _layer_backward(l, dx2, dx2b, sv, prm, wg, sink):
    S, D = dx2.shape
    w_in, w_out = wg.get("w_in", l), wg.get("w_out", l)
    w_mlp_in, w_mlp_out = wg.get("w_mlp_in", l), wg.get("w_mlp_out", l)
    dff4 = w_mlp_in.shape[-1]
    dff = N_CHIPS * dff4
    tm = min(512, S)
    tk = min(1024, S)
    nks = S // tk

    tmb, tnb = min(1024, S), min(1024, D)
    da = _matmul(
        f"mlp_out_bwd_{l}", dx2b, w_mlp_out, (S, dff), BF16, grid=(S // tmb, N_CHIPS, 1),
        a_spec=pl.BlockSpec((tmb, D), lambda i, j, k: (i, 0)),
        b_spec=pl.BlockSpec((None, dff4, D), lambda i, j, k: (j, 0, 0)),
        o_spec=pl.BlockSpec((tmb, dff4), lambda i, j, k: (i, j)),
        contract=(1, 1), acc_shape=(tmb, dff4),
        extras=(sv["a"],), extra_specs=(pl.BlockSpec((tmb, dff4), lambda i, j, k: (i, j)),),
        epi=lambda r, act: r * (2.0 * jnp.maximum(act.astype(F32), 0.0)), deps=sink.deps())
    tmw = min(1024, dff4)
    mpc = dff4 // tmw
    g_w2 = _matmul(
        f"mlp_out_dw_{l}", sv["a"], dx2b, (N_CHIPS, dff4, D), F32, grid=(N_CHIPS * mpc, 1, nks),
        a_spec=pl.BlockSpec((tk, tmw), lambda i, j, k: (k, i)),
        b_spec=pl.BlockSpec((tk, D), lambda i, j, k: (k, 0)),
        o_spec=pl.BlockSpec((None, tmw, D), lambda i, j, k: (i // mpc, i % mpc, 0)),
        contract=(0, 0), acc_shape=(tmw, D), a_pre=_relu2_bf16)
    sink.begin("w_mlp_out", l, g_w2)
    dh2 = _matmul(
        f"mlp_in_bwd_{l}", da, w_mlp_in, (S, D), F32, grid=(S // tmb, D // tnb, N_CHIPS),
        a_spec=pl.BlockSpec((tmb, dff4), lambda i, j, k: (i, k)),
        b_spec=pl.BlockSpec((None, tnb, dff4), lambda i, j, k: (k, j, 0)),
        o_spec=pl.BlockSpec((tmb, tnb), lambda i, j, k: (i, j)),
        contract=(1, 1), acc_shape=(tmb, tnb), deps=sink.deps())
    sink.advance(dh2)
    tmd = min(1024, D)
    g_w1 = _matmul(
        f"mlp_in_dw_{l}", sv["h2"], da, (N_CHIPS, D, dff4), F32, grid=(N_CHIPS, D // tmd, nks),
        a_spec=pl.BlockSpec((tk, tmd), lambda i, j, k: (k, j)),
        b_spec=pl.BlockSpec((tk, dff4), lambda i, j, k: (k, i)),
        o_spec=pl.BlockSpec((None, tmd, dff4), lambda i, j, k: (i, j, 0)),
        contract=(0, 0), acc_shape=(tmd, dff4))
    sink.begin("w_mlp_in", l, g_w1)
    dx1, dx1b, g_mlp_norm = _rmsnorm_bwd(f"mlp_norm_bwd_{l}", dh2, sv["x1"], prm["mlp_norm"][l], dx2,
                                         deps=sink.deps())

    rq = w_out.shape[1]
    dycat = _matmul(
        f"out_proj_bwd_{l}", dx1b, w_out, (S, N_CHIPS * rq), F32, grid=(S // tmb, N_CHIPS, 1),
        a_spec=pl.BlockSpec((tmb, D), lambda i, j, k: (i, 0)),
        b_spec=pl.BlockSpec((None, rq, D), lambda i, j, k: (j, 0, 0)),
        o_spec=pl.BlockSpec((tmb, rq), lambda i, j, k: (i, j)),
        contract=(1, 1), acc_shape=(tmb, rq))
    sink.advance(dycat)
    g_wout = _matmul(
        f"out_proj_dw_{l}", sv["ycat"], dx1b, (N_CHIPS, rq, D), F32, grid=(N_CHIPS, 1, nks),
        a_spec=pl.BlockSpec((tk, rq), lambda i, j, k: (k, i)),
        b_spec=pl.BlockSpec((tk, D), lambda i, j, k: (k, 0)),
        o_spec=pl.BlockSpec((None, rq, D), lambda i, j, k: (i, 0, 0)),
        contract=(0, 0), acc_shape=(rq, D))
    sink.begin("w_out", l, g_wout)

    p = sv["p"]
    du, dv_a, g_sgu_w, db_lanes = _sgu_bwd(f"sgu_bwd_{l}", p, dycat, prm["sgu_wt"][l], prm["sgu_wtt"][l],
                                           prm["sgu_bb"][l])
    g_sgu_b = db_lanes[:, :A_HEADS].T
    db, dc, dxb, g_conv = _conv_bwd(f"conv_bwd_{l}", p, dycat, prm["conv_w"][l])
    do3, c3 = _mix_bwd(f"mix_bwd_{l}", sv["os"], sv["lses"], dycat, prm["bd"])
    dqs, dks, dvs, dgqs, dgks = [], [], [], [], []
    for g in range(N_PATTERNS):
        dq, dk, dv, dgq, dgk = _attn_bwd(f"attn_bwd_{l}_{g}", p, g, sv["lses"][g], do3, c3,
                                         prm["q_gain"][l], prm["k_gain"][l], prm["bd"])
        dqs.append(dq)
        dks.append(dk)
        dvs.append(dv)
        dgqs.append(dgq)
        dgks.append(dgk)
    g_q = jnp.concatenate(dgqs, axis=1).reshape(N_PATTERNS * PW // HEAD_DIM, HEAD_DIM).sum(axis=0)
    g_k = jnp.concatenate(dgks, axis=1).reshape(N_PATTERNS * PW // HEAD_DIM, HEAD_DIM).sum(axis=0)
    dp = jnp.concatenate([du, dv_a, db, dc, dxb] + [t.astype(BF16) for t in dqs + dks + dvs], axis=1)

    ns_in = w_in.shape[-1]
    g_win = _matmul(
        f"in_proj_dw_{l}", sv["h"], dp, (N_CHIPS, D, ns_in), F32, grid=(N_CHIPS, D // tmd, nks),
        a_spec=pl.BlockSpec((tk, tmd), lambda i, j, k: (k, j)),
        b_spec=pl.BlockSpec((tk, ns_in), lambda i, j, k: (k, i)),
        o_spec=pl.BlockSpec((None, tmd, ns_in), lambda i, j, k: (i, j, 0)),
        contract=(0, 0), acc_shape=(tmd, ns_in))
    sink.begin("w_in", l, g_win)
    dh = _matmul(
        f"in_proj_bwd_{l}", dp, w_in, (S, D), F32, grid=(S // tmb, D // tnb, N_CHIPS),
        a_spec=pl.BlockSpec((tmb, ns_in), lambda i, j, k: (i, k)),
        b_spec=pl.BlockSpec((None, tnb, ns_in), lambda i, j, k: (k, j, 0)),
        o_spec=pl.BlockSpec((tmb, tnb), lambda i, j, k: (i, j)),
        contract=(1, 1), acc_shape=(tmb, tnb), deps=sink.deps())
    sink.advance(dh)
    dx0, dx0b, g_attn_norm = _rmsnorm_bwd(f"attn_norm_bwd_{l}", dh, sv["x"], prm["attn_norm"][l], dx1,
                                          deps=sink.deps())

    big = dict(w_in=g_win, w_out=g_wout, w_mlp_in=g_w1, w_mlp_out=g_w2)
    small = dict(attn_norm=g_attn_norm.reshape(-1), sgu_w=g_sgu_w, sgu_b=g_sgu_b, conv_w=g_conv,
                 q_norm=g_q, k_norm=g_k, mlp_norm=g_mlp_norm.reshape(-1))
    return dx0, dx0b, big, small


BIG = ("w_in", "w_out", "w_mlp_in", "w_mlp_out")
SMALL_REPLICATED = ("attn_norm", "sgu_w", "sgu_b", "q_norm", "k_norm", "mlp_norm")


def _local_step(x, target, prm, wg, n_layers, sink):
    saved = []
    h = x
    for l in range(n_layers):
        h, sv = _layer_forward(l, h, prm, wg)
        saved.append(sv)
    dy, dyb, colsq = _loss_kernel(h, target)
    loss = 0.5 * jnp.sum(colsq) / x.shape[1]
    bigs, smalls = [None] * n_layers, [None] * n_layers
    for l in reversed(range(n_layers)):
        dy, dyb, bigs[l], smalls[l] = _layer_backward(l, dy, dyb, saved[l], prm, wg, sink)
    return loss, dy, bigs, smalls


def _prepare_params(attn_norm, sgu_w, sgu_b, conv_full, q_norm, k_norm, mlp_norm):
    n_layers = attn_norm.shape[0]
    tri = jnp.tril(sgu_w)
    idx = jnp.arange(PW)
    bd = (idx[:, None] // HEAD_DIM == idx[None, :] // HEAD_DIM).astype(BF16)
    return dict(
        attn_norm=[attn_norm[l][None, :] for l in range(n_layers)],
        mlp_norm=[mlp_norm[l][None, :] for l in range(n_layers)],
        sgu_wt=[tri[l].astype(BF16) for l in range(n_layers)],
        sgu_wtt=[tri[l].transpose(0, 2, 1).astype(BF16) for l in range(n_layers)],
        sgu_bb=[jnp.repeat(sgu_b[l].T, HEAD_DIM, axis=1) for l in range(n_layers)],
        conv_w=[conv_full[l] for l in range(n_layers)],
        q_gain=[jnp.tile(q_norm[l], PW // HEAD_DIM)[None, :] for l in range(n_layers)],
        k_gain=[jnp.tile(k_norm[l], PW // HEAD_DIM)[None, :] for l in range(n_layers)],
        bd=bd,
    )


def kernel(x, attn_norm, w_in, sgu_w, sgu_b, conv_w, q_norm, k_norm, w_out, mlp_norm, w_mlp_in, w_mlp_out, loss_target, m_attn_norm, m_w_in, m_sgu_w, m_sgu_b, m_conv_w, m_q_norm, m_k_norm, m_w_out, m_mlp_norm, m_w_mlp_in, m_w_mlp_out, v_attn_norm, v_w_in, v_sgu_w, v_sgu_b, v_conv_w, v_q_norm, v_k_norm, v_w_out, v_mlp_norm, v_w_mlp_in, v_w_mlp_out):
    n_layers = attn_norm.shape[0]
    weights = dict(attn_norm=attn_norm, w_in=w_in, sgu_w=sgu_w, sgu_b=sgu_b, conv_w=conv_w, q_norm=q_norm,
                   k_norm=k_norm, w_out=w_out, mlp_norm=mlp_norm, w_mlp_in=w_mlp_in, w_mlp_out=w_mlp_out)
    mom_m = dict(attn_norm=m_attn_norm, w_in=m_w_in, sgu_w=m_sgu_w, sgu_b=m_sgu_b, conv_w=m_conv_w,
                 q_norm=m_q_norm, k_norm=m_k_norm, w_out=m_w_out, mlp_norm=m_mlp_norm, w_mlp_in=m_w_mlp_in,
                 w_mlp_out=m_w_mlp_out)
    mom_v = dict(attn_norm=v_attn_norm, w_in=v_w_in, sgu_w=v_sgu_w, sgu_b=v_sgu_b, conv_w=v_conv_w,
                 q_norm=v_q_norm, k_norm=v_k_norm, w_out=v_w_out, mlp_norm=v_mlp_norm, w_mlp_in=v_w_mlp_in,
                 w_mlp_out=v_w_mlp_out)
    order = ("attn_norm", "w_in", "sgu_w", "sgu_b", "conv_w", "q_norm", "k_norm", "w_out", "mlp_norm",
             "w_mlp_in", "w_mlp_out")
    chip = 2 * lax.axis_index("x") + lax.axis_index("y")
    c_arr = jnp.stack([lax.axis_index("c"), chip]).astype(jnp.int32)

    conv_cols = conv_w.shape[-1]
    chip_arr = chip.astype(jnp.int32).reshape(1)
    conv_pack = jnp.pad(conv_w.reshape(-1), (0, 2048 - conv_w.size)).reshape(1, 16, 128)
    wg = _GatheredWeights()
    wg.start([("conv_w", 0), ("w_in", 0)],
             [_place_shard("place_conv_w", conv_pack, 0, chip_arr, F32),
              _place_shard("place_w_in_0", weights["w_in"], 0, chip_arr, BF16)])
    keys = [(n, l) for l in range(n_layers) for n in BIG if (n, l) != ("w_in", 0)]
    first = wg.deps()
    wg.start(keys, [_place_shard(f"place_{n}_{l}", weights[n], l, chip_arr, BF16, deps=first) for n, l in keys])
    conv_full = wg.get("conv_w", 0, wg.deps()[-1]).reshape(N_CHIPS, 2048)[:, :conv_w.size].reshape(N_CHIPS, n_layers, 3, conv_cols)
    conv_full = conv_full.transpose(1, 2, 0, 3).reshape(n_layers, 3, N_CHIPS * conv_cols)
    prm = _prepare_params(attn_norm, sgu_w, sgu_b, conv_full, q_norm, k_norm, mlp_norm)

    sink = _GradReducer(c_arr)
    loss_local, grad_x, _, smalls = _local_step(x[0], loss_target[0], prm, wg, n_layers, sink)
    loss = lax.psum(loss_local, ("x", "y", "c"))

    small_names = SMALL_REPLICATED + ("conv_w",)
    small_shapes = [(n_layers,) + tuple(smalls[0][n].shape) for n in small_names]
    packed = _pack_rows([jnp.stack([smalls[l][n] for l in range(n_layers)]) for n in small_names])
    small_send, small_recv, small_land, small_token = _small_start(packed, sink.deps())

    grads, delta, new_m, new_v = {}, {}, {}, {}

    def update(n, after):
        shp = weights[n].shape
        two_d = (shp[0] * shp[1], shp[2])
        d, nm, nv, g = _adamw(f"adamw_{n}", weights[n].reshape(two_d), sink.reduced(n, after).reshape(two_d),
                              mom_m[n].reshape(two_d), mom_v[n].reshape(two_d))
        grads[n], delta[n], new_m[n], new_v[n] = g.reshape(shp), d.reshape(shp), nm.reshape(shp), nv.reshape(shp)

    token = small_token
    for n in ("w_mlp_out", "w_mlp_in", "w_out"):
        token = sink.reduce(n, n_layers, token)
    update("w_mlp_out", token)
    token = sink.reduce("w_in", n_layers, delta["w_mlp_out"])
    update("w_mlp_in", token)
    update("w_out", delta["w_mlp_in"])
    update("w_in", delta["w_out"])
    small_land = _small_wait(packed, small_land, small_send, small_recv, delta["w_in"])
    grads.update(zip(small_names, _unpack_rows(_sum_devices(small_land), small_shapes)))
    grads["conv_w"] = lax.dynamic_slice_in_dim(grads["conv_w"], chip * conv_cols, conv_cols, axis=2)
    smalls_all = SMALL_REPLICATED + ("conv_w",)
    shapes = [weights[n].shape for n in smalls_all]
    d, nm, nv, _ = _adamw("adamw_small",
                          _pack_rows([weights[n] for n in smalls_all]), _pack_rows([grads[n] for n in smalls_all]),
                          _pack_rows([mom_m[n] for n in smalls_all]), _pack_rows([mom_v[n] for n in smalls_all]))
    for n, dd, mm, vv in zip(smalls_all, _unpack_rows(d, shapes), _unpack_rows(nm, shapes), _unpack_rows(nv, shapes)):
        delta[n], new_m[n], new_v[n] = dd, mm, vv

    return (loss, grad_x[None], *[grads[n] for n in order], *[delta[n] for n in order],
            *[new_m[n] for n in order], *[new_v[n] for n in order])
```

```python
import jax
import jax.numpy as jnp
from jax import lax
from jax.experimental import pallas as pl
from jax.experimental.pallas import tpu as pltpu

F32 = jnp.float32
BF16 = jnp.bfloat16
SDS = jax.ShapeDtypeStruct

EPS = 1e-6
HEAD_DIM = 64
A_HEADS = 8
A_WIDTH = 512
CHUNK = 128
B_WIDTH = 768
C_WIDTH = 768
N_PATTERNS = 3
PATTERN_DILATION = (1, 4, 16)
PW = 256
D_IN_PROJ = 5632
OFF_AU, OFF_AV, OFF_BB, OFF_BC, OFF_BX, OFF_Q, OFF_K, OFF_V = 0, 512, 1024, 1792, 2560, 3328, 4096, 4864
N_CHIPS = 4
N_DEV = 8
BLK = 128

ADAM_LR, ADAM_B1, ADAM_B2, ADAM_EPS, ADAM_WD, ADAM_STEP = 0.001, 0.9, 0.999, 1e-08, 0.01, 10

V7X_VMEM_LIMIT = 56 * 1024 * 1024
MESH = pl.DeviceIdType.MESH
NEG = -1e30


def _cp(n_axes):
    return pltpu.CompilerParams(dimension_semantics=("arbitrary",) * n_axes, vmem_limit_bytes=V7X_VMEM_LIMIT)


def _hbm_spec():
    return pl.BlockSpec(memory_space=pl.ANY)


def _norm_matmul(name, x, g, wg, out_dtype, deps=()):
    S, D = x.shape
    ns, _, Ns = wg.shape
    tm = min(512, S)
    n_dep = len(deps)

    def body(x_ref, g_ref, w_ref, *rest):
        o_ref, h_ref, hs_ref = rest[n_dep:]
        @pl.when(pl.program_id(1) == 0)
        def _():
            xv = x_ref[...]
            y = xv * lax.rsqrt(jnp.mean(xv * xv, axis=-1, keepdims=True) + EPS) * g_ref[...]
            hb = y.astype(BF16)
            hs_ref[...] = hb
            h_ref[...] = hb
        o_ref[...] = jnp.dot(hs_ref[...], w_ref[...], preferred_element_type=F32).astype(o_ref.dtype)

    return pl.pallas_call(
        body, name=name, grid=(S // tm, ns),
        in_specs=[pl.BlockSpec((tm, D), lambda i, s: (i, 0)),
                  pl.BlockSpec((1, D), lambda i, s: (0, 0)),
                  pl.BlockSpec((None, D, Ns), lambda i, s: (s, 0, 0))] + [_hbm_spec()] * n_dep,
        out_specs=[pl.BlockSpec((tm, Ns), lambda i, s: (i, s)),
                   pl.BlockSpec((tm, D), lambda i, s: (i, 0))],
        out_shape=[SDS((S, ns * Ns), out_dtype), SDS((S, D), BF16)],
        scratch_shapes=[pltpu.VMEM((tm, D), BF16)],
        compiler_params=_cp(2),
    )(x, g, wg, *deps)


def _matmul(name, a, b, out_shape, out_dtype, *, grid, a_spec, b_spec, o_spec, contract, acc_shape,
            extras=(), extra_specs=(), a_pre=None, epi=None, deps=()):
    nk = grid[2]
    n_ex = len(extras)
    n_dep = len(deps)
    dims = (((contract[0],), (contract[1],)), ((), ()))

    def product(a_ref, b_ref):
        av = a_ref[...]
        if a_pre is not None:
            av = a_pre(av)
        return lax.dot_general(av, b_ref[...], dims, preferred_element_type=F32)

    def finish(r, ex, o_ref):
        if epi is not None:
            r = epi(r, *[e[...] for e in ex])
        o_ref[...] = r.astype(o_ref.dtype)

    def body_single(a_ref, b_ref, *rest):
        finish(product(a_ref, b_ref), rest[:n_ex], rest[n_ex + n_dep])

    def body(a_ref, b_ref, *rest):
        ex = rest[:n_ex]
        o_ref = rest[n_ex + n_dep]
        acc_ref = rest[n_ex + n_dep + 1]
        k = pl.program_id(2)

        @pl.when(k == 0)
        def _():
            acc_ref[...] = product(a_ref, b_ref)

        @pl.when((k > 0) & (k < nk - 1))
        def _():
            acc_ref[...] += product(a_ref, b_ref)

        @pl.when(k == nk - 1)
        def _():
            finish(acc_ref[...] + product(a_ref, b_ref), ex, o_ref)

    return pl.pallas_call(
        body_single if nk == 1 else body, name=name, grid=grid,
        in_specs=[a_spec, b_spec, *extra_specs] + [_hbm_spec()] * n_dep,
        out_specs=o_spec,
        out_shape=SDS(out_shape, out_dtype),
        scratch_shapes=[] if nk == 1 else [pltpu.VMEM(acc_shape, F32)],
        compiler_params=_cp(3),
    )(a, b, *extras, *deps)


def _relu2_bf16(t):
    r = jnp.maximum(t.astype(F32), 0.0)
    return (r * r).astype(BF16)


def _loss_kernel(y, t):
    S, D = y.shape
    tm = min(256, S)

    def body(y_ref, t_ref, dy_ref, dyb_ref, l_ref):
        @pl.when(pl.program_id(0) == 0)
        def _():
            l_ref[...] = jnp.zeros_like(l_ref)
        e = y_ref[...] - t_ref[...]
        l_ref[...] += jnp.sum(e * e, axis=0, keepdims=True)
        dy = e * (1.0 / D)
        dy_ref[...] = dy
        dyb_ref[...] = dy.astype(BF16)

    row = pl.BlockSpec((tm, D), lambda i: (i, 0))
    return pl.pallas_call(
        body, name="loss_head", grid=(S // tm,),
        in_specs=[row, row],
        out_specs=[row, row, pl.BlockSpec((1, D), lambda i: (0, 0))],
        out_shape=[SDS((S, D), F32), SDS((S, D), BF16), SDS((1, D), F32)],
        compiler_params=_cp(1),
    )(y, t)


def _rmsnorm_bwd(name, dh, x, g, dres, deps=()):
    S, D = x.shape
    tm = min(256, S)
    n_dep = len(deps)

    def body(dh_ref, x_ref, g_ref, dres_ref, *rest):
        dx_ref, dxb_ref, dg_ref = rest[n_dep:]
        @pl.when(pl.program_id(0) == 0)
        def _():
            dg_ref[...] = jnp.zeros_like(dg_ref)
        xv = x_ref[...]
        dhv = dh_ref[...]
        rstd = lax.rsqrt(jnp.mean(xv * xv, axis=-1, keepdims=True) + EPS)
        xhat = xv * rstd
        dg_ref[...] += jnp.sum(dhv * xhat, axis=0, keepdims=True)
        dxn = dhv * g_ref[...]
        dx = dres_ref[...] + rstd * (dxn - xhat * jnp.mean(dxn * xhat, axis=-1, keepdims=True))
        dx_ref[...] = dx
        dxb_ref[...] = dx.astype(BF16)

    row = pl.BlockSpec((tm, D), lambda i: (i, 0))
    vec = pl.BlockSpec((1, D), lambda i: (0, 0))
    return pl.pallas_call(
        body, name=name, grid=(S // tm,),
        in_specs=[row, row, vec, row] + [_hbm_spec()] * n_dep,
        out_specs=[row, row, vec],
        out_shape=[SDS((S, D), F32), SDS((S, D), BF16), SDS((1, D), F32)],
        compiler_params=_cp(1),
    )(dh, x, g, dres, *deps)


def _adamw(name, w, g, m, v):
    R, C = w.shape
    tr = 256 if R % 256 == 0 else R
    c1 = 1.0 - ADAM_B1 ** ADAM_STEP
    c2 = 1.0 - ADAM_B2 ** ADAM_STEP

    def body(w_ref, g_ref, m_ref, v_ref, d_ref, nm_ref, nv_ref, g_out_ref):
        gv = g_ref[...]
        nm = ADAM_B1 * m_ref[...] + (1.0 - ADAM_B1) * gv
        nv = ADAM_B2 * v_ref[...] + (1.0 - ADAM_B2) * (gv * gv)
        m_hat = nm / c1
        v_hat = nv / c2
        d_ref[...] = -ADAM_LR * (m_hat / (jnp.sqrt(v_hat) + ADAM_EPS) + ADAM_WD * w_ref[...])
        nm_ref[...] = nm
        nv_ref[...] = nv
        g_out_ref[...] = gv

    blk = pl.BlockSpec((tr, C), lambda i: (i, 0))
    return pl.pallas_call(
        body, name=name, grid=(R // tr,),
        in_specs=[blk] * 4, out_specs=[blk] * 4,
        out_shape=[SDS((R, C), F32)] * 4,
        compiler_params=_cp(1),
    )(w, g, m, v)


SGU_STEP_ROWS = 512


def _pair_select(lane, lo, hi):
    return jnp.where(lane < HEAD_DIM, lo, hi)


def _sgu_fwd(name, p, wt, bb):
    S = p.shape[0]

    rows = min(SGU_STEP_ROWS, S)

    def body(u_ref, v_ref, wt_ref, bb_ref, o_ref):
        lane = lax.broadcasted_iota(jnp.int32, (CHUNK, 128), 1)
        for ci in range(rows // CHUNK):
            rs = slice(CHUNK * ci, CHUNK * (ci + 1))
            for pp in range(A_HEADS // 2):
                cs = slice(128 * pp, 128 * (pp + 1))
                vb = v_ref[rs, cs].astype(BF16)
                mixed = _pair_select(lane,
                                     jnp.dot(wt_ref[2 * pp], vb, preferred_element_type=F32),
                                     jnp.dot(wt_ref[2 * pp + 1], vb, preferred_element_type=F32)) + bb_ref[:, cs]
                o_ref[rs, cs] = (u_ref[rs, cs] * mixed).astype(o_ref.dtype)

    return pl.pallas_call(
        body, name=name, grid=(S // rows,),
        in_specs=[pl.BlockSpec((rows, A_WIDTH), lambda c: (c, OFF_AU // A_WIDTH)),
                  pl.BlockSpec((rows, A_WIDTH), lambda c: (c, OFF_AV // A_WIDTH)),
                  pl.BlockSpec((A_HEADS, CHUNK, CHUNK), lambda c: (0, 0, 0)),
                  pl.BlockSpec((CHUNK, A_WIDTH), lambda c: (0, 0))],
        out_specs=pl.BlockSpec((rows, A_WIDTH), lambda c: (c, 0)),
        out_shape=SDS((S, A_WIDTH), BF16),
        compiler_params=_cp(1),
    )(p, p, wt, bb)


def _sgu_bwd(name, p, dycat, wt, wtt, bb):
    S = p.shape[0]
    rows = min(SGU_STEP_ROWS, S)

    def body(u_ref, v_ref, dy_ref, wt_ref, wtt_ref, bb_ref, du_ref, dv_ref, dw_ref, db_ref, dbacc_ref):
        c = pl.program_id(0)

        @pl.when(c == 0)
        def _():
            dw_ref[...] = jnp.zeros_like(dw_ref)
            dbacc_ref[...] = jnp.zeros_like(dbacc_ref)

        lane = lax.broadcasted_iota(jnp.int32, (CHUNK, 128), 1)
        row = lax.broadcasted_iota(jnp.int32, (CHUNK, 128), 0)
        causal = row >= lane
        nt = (((1,), (1,)), ((), ()))
        for pp in range(A_HEADS // 2):
            cs = slice(128 * pp, 128 * (pp + 1))
            dw_lo = jnp.zeros((CHUNK, CHUNK), F32)
            dw_hi = jnp.zeros((CHUNK, CHUNK), F32)
            dm_sum = jnp.zeros((CHUNK, 128), F32)
            for ci in range(rows // CHUNK):
                rs = slice(CHUNK * ci, CHUNK * (ci + 1))
                vb = v_ref[rs, cs].astype(BF16)
                dy = dy_ref[rs, cs]
                mixed = _pair_select(lane,
                                     jnp.dot(wt_ref[2 * pp], vb, preferred_element_type=F32),
                                     jnp.dot(wt_ref[2 * pp + 1], vb, preferred_element_type=F32)) + bb_ref[:, cs]
                du_ref[rs, cs] = (dy * mixed).astype(du_ref.dtype)
                dm = dy * u_ref[rs, cs]
                dmb = dm.astype(BF16)
                dv = _pair_select(lane,
                                  jnp.dot(wtt_ref[2 * pp], dmb, preferred_element_type=F32),
                                  jnp.dot(wtt_ref[2 * pp + 1], dmb, preferred_element_type=F32))
                dv_ref[rs, cs] = dv.astype(dv_ref.dtype)
                dm_sum += dm
                dm_lo = jnp.where(lane < HEAD_DIM, dm, 0.0).astype(BF16)
                dm_hi = jnp.where(lane >= HEAD_DIM, dm, 0.0).astype(BF16)
                dw_lo += lax.dot_general(dm_lo, vb, nt, preferred_element_type=F32)
                dw_hi += lax.dot_general(dm_hi, vb, nt, preferred_element_type=F32)
            dbacc_ref[:, cs] += dm_sum
            dw_ref[2 * pp] += jnp.where(causal, dw_lo, 0.0)
            dw_ref[2 * pp + 1] += jnp.where(causal, dw_hi, 0.0)

        @pl.when(c == S // rows - 1)
        def _():
            out = jnp.zeros((CHUNK, 128), F32)
            for pp in range(A_HEADS // 2):
                acc = dbacc_ref[:, 128 * pp:128 * (pp + 1)]
                s_lo = jnp.sum(jnp.where(lane < HEAD_DIM, acc, 0.0), axis=1, keepdims=True)
                s_hi = jnp.sum(jnp.where(lane >= HEAD_DIM, acc, 0.0), axis=1, keepdims=True)
                out = jnp.where(lane == 2 * pp, s_lo, out)
                out = jnp.where(lane == 2 * pp + 1, s_hi, out)
            db_ref[...] = out

    chunk = lambda col: pl.BlockSpec((rows, A_WIDTH), lambda c: (c, col))
    wspec = pl.BlockSpec((A_HEADS, CHUNK, CHUNK), lambda c: (0, 0, 0))
    return pl.pallas_call(
        body, name=name, grid=(S // rows,),
        in_specs=[chunk(OFF_AU // A_WIDTH), chunk(OFF_AV // A_WIDTH), chunk(0), wspec, wspec,
                  pl.BlockSpec((CHUNK, A_WIDTH), lambda c: (0, 0))],
        out_specs=[chunk(0), chunk(0), wspec, pl.BlockSpec((CHUNK, 128), lambda c: (0, 0))],
        out_shape=[SDS((S, A_WIDTH), BF16), SDS((S, A_WIDTH), BF16),
                   SDS((A_HEADS, CHUNK, CHUNK), F32), SDS((CHUNK, 128), F32)],
        scratch_shapes=[pltpu.VMEM((CHUNK, A_WIDTH), F32)],
        compiler_params=_cp(1),
    )(p, p, dycat, wt, wtt, bb)


CONV_HALO = 8
CONV_COLS = 256
CONV_ROWS = 1024


def _shift_down(a, halo, k):
    T = a.shape[0]
    row = lax.broadcasted_iota(jnp.int32, a.shape, 0)
    out = pltpu.roll(a, k, 0)
    for r in range(k):
        out = jnp.where(row == r, halo[CONV_HALO - k + r:CONV_HALO - k + r + 1, :], out)
    return out


def _shift_up(a, halo, k):
    T = a.shape[0]
    row = lax.broadcasted_iota(jnp.int32, a.shape, 0)
    out = pltpu.roll(a, T - k, 0)
    for r in range(k):
        out = jnp.where(row == T - k + r, halo[r:r + 1, :], out)
    return out


def _conv_specs(S, T):
    hb = T // CONV_HALO
    last = S // CONV_HALO - 1
    tile = lambda col0: pl.BlockSpec((T, CONV_COLS), lambda j, i: (i, col0 + j))
    prev = lambda col0: pl.BlockSpec((CONV_HALO, CONV_COLS), lambda j, i: (jnp.maximum(i * hb - 1, 0), col0 + j))
    nxt = lambda col0: pl.BlockSpec((CONV_HALO, CONV_COLS), lambda j, i: (jnp.minimum((i + 1) * hb, last), col0 + j))
    return tile, prev, nxt


def _conv_fwd(name, p, w):
    S = p.shape[0]
    T = min(CONV_ROWS, S)
    tile, prev, _ = _conv_specs(S, T)
    cb, cc, cx = OFF_BB // CONV_COLS, OFF_BC // CONV_COLS, OFF_BX // CONV_COLS

    def body(b_ref, c_ref, x_ref, ch_ref, xh_ref, w_ref, o_ref):
        i = pl.program_id(1)
        z = c_ref[...] * x_ref[...]
        zh = jnp.where(i > 0, ch_ref[...] * xh_ref[...], 0.0)
        z1 = _shift_down(z, zh, 1)
        z2 = _shift_down(z, zh, 2)
        conv = w_ref[0:1, :] * z2 + w_ref[1:2, :] * z1 + w_ref[2:3, :] * z
        o_ref[...] = (b_ref[...] * conv).astype(o_ref.dtype)

    return pl.pallas_call(
        body, name=name, grid=(B_WIDTH // CONV_COLS, S // T),
        in_specs=[tile(cb), tile(cc), tile(cx), prev(cc), prev(cx),
                  pl.BlockSpec((3, CONV_COLS), lambda j, i: (0, j))],
        out_specs=tile(0),
        out_shape=SDS((S, B_WIDTH), BF16),
        compiler_params=_cp(2),
    )(p, p, p, p, p, w)


def _conv_bwd(name, p, dycat, w):
    S = p.shape[0]
    T = min(CONV_ROWS, S)
    tile, prev, nxt = _conv_specs(S, T)
    cb, cc, cx = OFF_BB // CONV_COLS, OFF_BC // CONV_COLS, OFF_BX // CONV_COLS
    cdy = A_WIDTH // CONV_COLS
    n_i = S // T

    def body(b_ref, c_ref, x_ref, dy_ref, ch_ref, xh_ref, bn_ref, dyn_ref, w_ref,
             db_ref, dc_ref, dx_ref, dw_ref):
        i = pl.program_id(1)

        @pl.when(i == 0)
        def _():
            dw_ref[...] = jnp.zeros_like(dw_ref)

        cv = c_ref[...]
        xv = x_ref[...]
        z = cv * xv
        zh = jnp.where(i > 0, ch_ref[...] * xh_ref[...], 0.0)
        z1 = _shift_down(z, zh, 1)
        z2 = _shift_down(z, zh, 2)
        w0, w1, w2 = w_ref[0:1, :], w_ref[1:2, :], w_ref[2:3, :]
        conv = w0 * z2 + w1 * z1 + w2 * z
        dy = dy_ref[...]
        db_ref[...] = (dy * conv).astype(db_ref.dtype)
        dconv = dy * b_ref[...]
        dconv_n = jnp.where(i < n_i - 1, dyn_ref[...] * bn_ref[...], 0.0)
        dz = w2 * dconv + w1 * _shift_up(dconv, dconv_n, 1) + w0 * _shift_up(dconv, dconv_n, 2)
        dc_ref[...] = (dz * xv).astype(dc_ref.dtype)
        dx_ref[...] = (dz * cv).astype(dx_ref.dtype)
        dw_ref[0:1, :] += jnp.sum(dconv * z2, axis=0, keepdims=True)
        dw_ref[1:2, :] += jnp.sum(dconv * z1, axis=0, keepdims=True)
        dw_ref[2:3, :] += jnp.sum(dconv * z, axis=0, keepdims=True)

    wspec = pl.BlockSpec((3, CONV_COLS), lambda j, i: (0, j))
    return pl.pallas_call(
        body, name=name, grid=(B_WIDTH // CONV_COLS, n_i),
        in_specs=[tile(cb), tile(cc), tile(cx), tile(cdy), prev(cc), prev(cx), nxt(cb), nxt(cdy), wspec],
        out_specs=[tile(0), tile(0), tile(0), wspec],
        out_shape=[SDS((S, B_WIDTH), BF16)] * 3 + [SDS((3, B_WIDTH), F32)],
        compiler_params=_cp(2),
    )(p, p, p, dycat, p, p, p, dycat, w)


def _seg_sum(t, bd):
    hi = t.astype(BF16)
    lo = (t - hi.astype(F32)).astype(BF16)
    return jnp.dot(hi, bd, preferred_element_type=F32) + jnp.dot(lo, bd, preferred_element_type=F32)


def _head_norm(x, g, bd):
    rstd = lax.rsqrt(_seg_sum(x * x, bd) * (1.0 / HEAD_DIM) + EPS)
    xhat = x * rstd
    return xhat * g, xhat, rstd


def _head_norm_bwd(dy, g, xhat, rstd, bd):
    dxh = dy * g
    return rstd * (dxh - xhat * (_seg_sum(dxh * xhat, bd) * (1.0 / HEAD_DIM)))


def _band_mask(has_prev):
    row = lax.broadcasted_iota(jnp.int32, (BLK, 2 * BLK), 0)
    col = lax.broadcasted_iota(jnp.int32, (BLK, 2 * BLK), 1)
    first_key = jnp.where(has_prev, 0, BLK)
    return (col >= row) & (col <= row + BLK) & (col >= first_key)


def _first_of_segment(g, n, n_blocks):
    per_seg = lax.shift_right_logical(jnp.int32(n_blocks), 2 * g)
    return (n & (per_seg - 1)) == 0


def _residue_rows(r, d):
    return slice(None) if d == 1 else pl.ds(r, BLK, stride=d)


STRIDED_LANES = 128


def _step_width(d):
    return PW if d == 1 else STRIDED_LANES


def _n_stack(lane):
    return lane.shape[1] // HEAD_DIM


def _for_residues(d, fn):
    if d == 1:
        fn(0)
    else:
        def two(i, carry):
            fn(2 * i)
            fn(2 * i + 1)
            return carry
        lax.fori_loop(0, d // 2, two, 0)


def _head_mask(lane, j):
    return (lane >= HEAD_DIM * j) & (lane < HEAD_DIM * (j + 1))


def _stack_heads(x, lane):
    return jnp.concatenate([jnp.where(_head_mask(lane, j), x, 0.0) for j in range(_n_stack(lane))], axis=0)


def _unstack_heads(y, lane):
    out = y[:BLK]
    for j in range(1, _n_stack(lane)):
        out = jnp.where(lane >= HEAD_DIM * j, y[BLK * j:BLK * (j + 1)], out)
    return out


def _head_columns(v, lane):
    return jnp.concatenate([jnp.max(jnp.where(_head_mask(lane, j), v, NEG), axis=1, keepdims=True)
                            for j in range(_n_stack(lane))], axis=0)


def _attn_fwd(name, p, g, gq, gk, bd):
    S = p.shape[0]
    d = PATTERN_DILATION[g]
    rows = BLK * d
    hw = _step_width(d)
    nt = (((1,), (1,)), ((), ()))

    def body(q_ref, kc_ref, kp_ref, vc_ref, vp_ref, gq_ref, gk_ref, bd_ref, o_ref, lse_ref):
        has_prev = pl.program_id(1) > 0
        bdv = bd_ref[...]
        band = jnp.concatenate([_band_mask(has_prev)] * (hw // HEAD_DIM), axis=0)
        lane = lax.broadcasted_iota(jnp.int32, (1, hw), 1)

        def residue(r):
            rr = _residue_rows(r, d)
            qn, _, _ = _head_norm(q_ref[rr, :], gq_ref[...], bdv)
            kn, _, _ = _head_norm(jnp.concatenate([kp_ref[rr, :], kc_ref[rr, :]], axis=0), gk_ref[...], bdv)
            knb = kn.astype(BF16)
            vb = jnp.concatenate([vp_ref[rr, :], vc_ref[rr, :]], axis=0).astype(BF16)
            qs = _stack_heads(qn, lane).astype(BF16)
            s = lax.dot_general(qs, knb, nt, preferred_element_type=F32) * (HEAD_DIM ** -0.5)
            s = jnp.where(band, s, NEG)
            m = jnp.max(s, axis=1, keepdims=True)
            e = jnp.exp(s - m)
            den = jnp.sum(e, axis=1, keepdims=True)
            pv = jnp.dot(e.astype(BF16), vb, preferred_element_type=F32)
            o_ref[rr, :] = _unstack_heads(pv / den, lane)
            lse_ref[rr, :] = _unstack_heads(jnp.broadcast_to(m + jnp.log(den), pv.shape), lane)

        _for_residues(d, residue)

    per = PW // hw
    cq, ck, cv = (OFF_Q + PW * g) // hw, (OFF_K + PW * g) // hw, (OFF_V + PW * g) // hw
    cur = lambda col: pl.BlockSpec((rows, hw), lambda h, n: (n, col + h))
    prv = lambda col: pl.BlockSpec((rows, hw), lambda h, n: (jnp.maximum(n - 1, 0), col + h))
    vec = pl.BlockSpec((1, hw), lambda h, n: (0, h))
    return pl.pallas_call(
        body, name=name, grid=(per, S // rows),
        in_specs=[cur(cq), cur(ck), prv(ck), cur(cv), prv(cv), vec, vec, pl.BlockSpec((hw, hw), lambda h, n: (0, 0))],
        out_specs=[cur(0), cur(0)],
        out_shape=[SDS((S, PW), F32)] * 2,
        compiler_params=_cp(2),
    )(p, p, p, p, p, gq, gk, bd)


def _attn_bwd(name, p, g, lse, do3, c3, gq, gk, bd):
    S = p.shape[0]
    d = PATTERN_DILATION[g]
    rows = BLK * d
    nblk = S // rows
    hw = _step_width(d)
    nt = (((1,), (1,)), ((), ()))
    tn = (((0,), (0,)), ((), ()))

    def body(q_ref, kc_ref, kp_ref, vc_ref, vp_ref, lse_ref, do_ref, c_ref, gq_ref, gk_ref, bd_ref,
             dq_ref, dk_ref, dv_ref, dgq_ref, dgk_ref, ck_ref, cv_ref, dq_keep_ref):
        n = pl.program_id(1)

        @pl.when(n == 0)
        def _():
            ck_ref[...] = jnp.zeros_like(ck_ref)
            cv_ref[...] = jnp.zeros_like(cv_ref)
            dgq_ref[...] = jnp.zeros_like(dgq_ref)
            dgk_ref[...] = jnp.zeros_like(dgk_ref)

        @pl.when(n == nblk)
        def _():
            dq_ref[...] = dq_keep_ref[...]
            dk_ref[...] = ck_ref[...]
            dv_ref[...] = cv_ref[...]

        bdv = bd_ref[...]
        gqv = gq_ref[...]
        gkv = gk_ref[...]
        band = jnp.concatenate([_band_mask(n > 0)] * (hw // HEAD_DIM), axis=0)
        lane = lax.broadcasted_iota(jnp.int32, (1, hw), 1)

        def residue(r):
            rr = _residue_rows(r, d)
            qn, qhat, qrstd = _head_norm(q_ref[rr, :], gqv, bdv)
            kn, khat, krstd = _head_norm(jnp.concatenate([kp_ref[rr, :], kc_ref[rr, :]], axis=0), gkv, bdv)
            knb = kn.astype(BF16)
            vb = jnp.concatenate([vp_ref[rr, :], vc_ref[rr, :]], axis=0).astype(BF16)
            qs = _stack_heads(qn, lane).astype(BF16)
            dos = _stack_heads(do_ref[rr, :], lane).astype(BF16)
            s = lax.dot_general(qs, knb, nt, preferred_element_type=F32) * (HEAD_DIM ** -0.5)
            prob = jnp.where(band, jnp.exp(s - _head_columns(lse_ref[rr, :], lane)), 0.0)
            dp = lax.dot_general(dos, vb, nt, preferred_element_type=F32)
            ds = (prob * (dp + _head_columns(c_ref[rr, :], lane)) * (HEAD_DIM ** -0.5)).astype(BF16)
            dqn = _unstack_heads(jnp.dot(ds, knb, preferred_element_type=F32), lane)
            dkn = lax.dot_general(ds, qs, tn, preferred_element_type=F32)
            dvv = lax.dot_general(prob.astype(BF16), dos, tn, preferred_element_type=F32)

            dq = _head_norm_bwd(dqn, gqv, qhat, qrstd, bdv)
            dq_ref[rr, :] = dq
            dq_keep_ref[rr, :] = dq
            dk2 = _head_norm_bwd(dkn, gkv, khat, krstd, bdv)
            dgq_ref[...] += jnp.sum(dqn * qhat, axis=0, keepdims=True)
            dgk_ref[...] += jnp.sum(dkn * khat, axis=0, keepdims=True)
            dk_ref[rr, :] = ck_ref[rr, :] + dk2[:BLK]
            dv_ref[rr, :] = cv_ref[rr, :] + dvv[:BLK]
            ck_ref[rr, :] = dk2[BLK:]
            cv_ref[rr, :] = dvv[BLK:]

        @pl.when(n < nblk)
        def _():
            _for_residues(d, residue)

    last = nblk - 1
    per = PW // hw
    cq, ck, cv = (OFF_Q + PW * g) // hw, (OFF_K + PW * g) // hw, (OFF_V + PW * g) // hw
    cur = lambda col: pl.BlockSpec((rows, hw), lambda h, n: (jnp.minimum(n, last), col + h))
    prv = lambda col: pl.BlockSpec((rows, hw), lambda h, n: (jnp.maximum(jnp.minimum(n, last) - 1, 0), col + h))
    cur3 = pl.BlockSpec((None, rows, hw), lambda h, n: (g, jnp.minimum(n, last), h))
    done = pl.BlockSpec((rows, hw), lambda h, n: (jnp.maximum(n - 1, 0), h))
    vec = pl.BlockSpec((1, hw), lambda h, n: (0, h))
    return pl.pallas_call(
        body, name=name, grid=(per, nblk + 1),
        in_specs=[cur(cq), cur(ck), prv(ck), cur(cv), prv(cv), cur(0), cur3, cur3, vec, vec,
                  pl.BlockSpec((hw, hw), lambda h, n: (0, 0))],
        out_specs=[cur(0), done, done, vec, vec],
        out_shape=[SDS((S, PW), F32)] * 3 + [SDS((1, PW), F32)] * 2,
        scratch_shapes=[pltpu.VMEM((rows, hw), F32)] * 3,
        compiler_params=_cp(2),
    )(p, p, p, p, p, lse, do3, c3, gq, gk, bd)


def _mix_fwd(name, os, lses):
    S = os[0].shape[0]
    tm = min(512, S)

    def body(o0, o1, o2, l0, l1, l2, y_ref):
        o = [o0[...], o1[...], o2[...]]
        l = [l0[...], l1[...], l2[...]]
        m = jnp.maximum(jnp.maximum(l[0], l[1]), l[2])
        e = [jnp.exp(t - m) for t in l]
        inv = 1.0 / (e[0] + e[1] + e[2])
        for g in range(N_PATTERNS):
            y_ref[:, PW * g:PW * (g + 1)] = (o[g] * (e[g] * inv)).astype(y_ref.dtype)

    blk = pl.BlockSpec((tm, PW), lambda i: (i, 0))
    return pl.pallas_call(
        body, name=name, grid=(S // tm,),
        in_specs=[blk] * 6,
        out_specs=pl.BlockSpec((tm, C_WIDTH), lambda i: (i, 0)),
        out_shape=SDS((S, C_WIDTH), BF16),
        compiler_params=_cp(1),
    )(*os, *lses)


def _mix_bwd(name, os, lses, dycat, bd):
    S = os[0].shape[0]
    tm = min(512, S)
    c0 = (A_WIDTH + B_WIDTH) // PW

    def body(o0, o1, o2, l0, l1, l2, dy0_ref, dy1_ref, dy2_ref, bd_ref, do_ref, c_ref):
        bdv = bd_ref[...]
        o = [o0[...], o1[...], o2[...]]
        l = [l0[...], l1[...], l2[...]]
        dys = [dy0_ref[...], dy1_ref[...], dy2_ref[...]]
        m = jnp.maximum(jnp.maximum(l[0], l[1]), l[2])
        e = [jnp.exp(t - m) for t in l]
        inv = 1.0 / (e[0] + e[1] + e[2])
        alpha = [t * inv for t in e]
        da = [_seg_sum(dys[g] * o[g], bdv) for g in range(N_PATTERNS)]
        mean_da = alpha[0] * da[0] + alpha[1] * da[1] + alpha[2] * da[2]
        for g in range(N_PATTERNS):
            do_ref[g] = dys[g] * alpha[g]
            c_ref[g] = -alpha[g] * mean_da

    blk = pl.BlockSpec((tm, PW), lambda i: (i, 0))
    blk3 = pl.BlockSpec((N_PATTERNS, tm, PW), lambda i: (0, i, 0))
    dyspec = lambda g: pl.BlockSpec((tm, PW), lambda i: (i, c0 + g))
    return pl.pallas_call(
        body, name=name, grid=(S // tm,),
        in_specs=[blk] * 6 + [dyspec(0), dyspec(1), dyspec(2), pl.BlockSpec((PW, PW), lambda i: (0, 0))],
        out_specs=[blk3, blk3],
        out_shape=[SDS((N_PATTERNS, S, PW), F32)] * 2,
        compiler_params=_cp(1),
    )(*os, *lses, dycat, dycat, dycat, bd)


def _mesh_pos():
    x, y, c = lax.axis_index("x"), lax.axis_index("y"), lax.axis_index("c")
    chips = [(1 - x, y), (x, 1 - y), (1 - x, 1 - y)]
    chip_idx = [2 * cx + cy for cx, cy in chips]
    return x, y, c, 2 * x + y, chips, chip_idx


def _place_shard(name, w, layer, chip_arr, out_dtype, deps=()):
    _, R, C = w.shape
    tr = min(256, R)

    def body(chip_ref, w_ref, *rest):
        o_ref = rest[-1]
        o_ref[...] = w_ref[...].astype(o_ref.dtype)

    return pl.pallas_call(
        body, name=name,
        grid_spec=pltpu.PrefetchScalarGridSpec(
            num_scalar_prefetch=1, grid=(R // tr,),
            in_specs=[pl.BlockSpec((None, tr, C), lambda i, chip_ref: (layer, i, 0))] + [_hbm_spec()] * len(deps),
            out_specs=pl.BlockSpec((None, tr, C), lambda i, chip_ref: (chip_ref[0], i, 0))),
        out_shape=SDS((N_CHIPS, R, C), out_dtype),
        compiler_params=_cp(1),
    )(chip_arr, w, *deps)


HBM_SPEC = pl.BlockSpec(memory_space=pltpu.HBM)
SEM_SPEC = pl.BlockSpec(memory_space=pltpu.SEMAPHORE)
SPLIT_COPY = pltpu.SideEffectType.DATAFLOW_SIDE_EFFECTING
N_PEER_CHIPS = N_CHIPS - 1
TOKEN_SHAPE = SDS((8, 128), F32)
TOKEN_SPEC = pl.BlockSpec(memory_space=pltpu.VMEM)


def _in_hbm(a):
    return pltpu.with_memory_space_constraint(a, pltpu.HBM)


def _gather_start(name, bufs):
    T = len(bufs)

    def body(*refs):
        ins = refs[:T]
        send_sems, recv_sems = refs[T:2 * T], refs[2 * T:3 * T]
        token = refs[4 * T]
        x, y, c, me, chips, chip_idx = _mesh_pos()
        for t in range(T):
            hr = ins[t].shape[1] // 2
            mine = ins[t].at[me, pl.ds(c * hr, hr), :]
            for j in range(N_PEER_CHIPS):
                pltpu.make_async_remote_copy(src_ref=mine, dst_ref=mine, send_sem=send_sems[t].at[j],
                                             recv_sem=recv_sems[t].at[j], device_id=(*chips[j], c),
                                             device_id_type=MESH).start()
        token[...] = jnp.zeros_like(token)

    sems = [pltpu.SemaphoreType.DMA((N_PEER_CHIPS,))] * T
    out = pl.pallas_call(
        body, name=name,
        in_specs=[HBM_SPEC] * T,
        out_specs=[SEM_SPEC] * (2 * T) + [HBM_SPEC] * T + [TOKEN_SPEC],
        out_shape=sems + sems + [pltpu.HBM(b.shape, b.dtype) for b in bufs] + [TOKEN_SHAPE],
        input_output_aliases={t: 2 * T + t for t in range(T)},
        compiler_params=pltpu.CompilerParams(has_side_effects=SPLIT_COPY),
    )(*[_in_hbm(b) for b in bufs])
    return out[:T], out[T:2 * T], out[2 * T:3 * T], out[3 * T]


def _gather_wait(name, buf, send_sem, recv_sem, after):
    n_in = 3 if after is None else 4

    def body(*refs):
        buf_ref, ssem, rsem = refs[:3]
        x, y, c, me, chips, chip_idx = _mesh_pos()
        hr = buf_ref.shape[1] // 2
        mine = buf_ref.at[me, pl.ds(c * hr, hr), :]
        for j in range(N_PEER_CHIPS):
            got = buf_ref.at[chip_idx[j], pl.ds(c * hr, hr), :]
            cp = pltpu.make_async_remote_copy(src_ref=mine, dst_ref=got, send_sem=ssem.at[j], recv_sem=rsem.at[j],
                                              device_id=(*chips[j], c), device_id_type=MESH)
            cp.wait_send()
            cp.wait_recv()

    args = [buf, send_sem, recv_sem] + ([] if after is None else [after])
    return pl.pallas_call(
        body, name=name,
        in_specs=[HBM_SPEC, SEM_SPEC, SEM_SPEC] + [_hbm_spec()] * (n_in - 3),
        out_specs=HBM_SPEC,
        out_shape=pltpu.HBM(buf.shape, buf.dtype),
        input_output_aliases={0: 0},
        compiler_params=pltpu.CompilerParams(has_side_effects=SPLIT_COPY),
    )(*args)


def _forward_start(name, buf):
    def body(buf_ref, send_sems, recv_sems, buf_thru, token):
        x, y, c, me, chips, chip_idx = _mesh_pos()
        hr = buf_ref.shape[1] // 2
        for j in range(N_PEER_CHIPS):
            got = buf_ref.at[chip_idx[j], pl.ds(c * hr, hr), :]
            pltpu.make_async_remote_copy(src_ref=got, dst_ref=got, send_sem=send_sems.at[j], recv_sem=recv_sems.at[j],
                                         device_id=(x, y, 1 - c), device_id_type=MESH).start()
        token[...] = jnp.zeros_like(token)

    sems = pltpu.SemaphoreType.DMA((N_PEER_CHIPS,))
    return pl.pallas_call(
        body, name=name,
        in_specs=[HBM_SPEC],
        out_specs=[SEM_SPEC, SEM_SPEC, HBM_SPEC, TOKEN_SPEC],
        out_shape=[sems, sems, pltpu.HBM(buf.shape, buf.dtype), TOKEN_SHAPE],
        input_output_aliases={0: 2},
        compiler_params=pltpu.CompilerParams(has_side_effects=SPLIT_COPY),
    )(_in_hbm(buf))


def _forward_wait(name, buf, send_sems, recv_sems, after):
    n_in = 3 if after is None else 4

    def body(*refs):
        buf_ref, ssems, rsems = refs[:3]
        x, y, c, me, chips, chip_idx = _mesh_pos()
        hr = buf_ref.shape[1] // 2
        for j in range(N_PEER_CHIPS):
            sent = buf_ref.at[chip_idx[j], pl.ds(c * hr, hr), :]
            theirs = buf_ref.at[chip_idx[j], pl.ds((1 - c) * hr, hr), :]
            cp = pltpu.make_async_remote_copy(src_ref=sent, dst_ref=theirs, send_sem=ssems.at[j],
                                              recv_sem=rsems.at[j], device_id=(x, y, 1 - c), device_id_type=MESH)
            cp.wait_send()
            cp.wait_recv()

    args = [buf, send_sems, recv_sems] + ([] if after is None else [after])
    return pl.pallas_call(
        body, name=name,
        in_specs=[HBM_SPEC, SEM_SPEC, SEM_SPEC] + [_hbm_spec()] * (n_in - 3),
        out_specs=HBM_SPEC,
        out_shape=pltpu.HBM(buf.shape, buf.dtype),
        input_output_aliases={0: 0},
        compiler_params=pltpu.CompilerParams(has_side_effects=SPLIT_COPY),
    )(*args)


class _GatheredWeights:
    def __init__(self):
        self._order = []
        self._pending = {}
        self._forwarding = {}
        self._ready = {}
        self._tokens = []

    def start(self, keys, bufs):
        send_sems, recv_sems, thru, token = _gather_start(f"gather_start_{len(self._order)}", bufs)
        self._tokens.append(token)
        self._order.extend(keys)
        self._pending.update({k: (b, s, r) for k, b, s, r in zip(keys, thru, send_sems, recv_sems)})

    def _prefetch(self, key, after):
        if key in self._pending:
            buf, ssem, rsem = self._pending.pop(key)
            tag = f"{key[0]}_{key[1]}"
            buf = _gather_wait(f"gather_wait_{tag}", buf, ssem, rsem, after)
            ssems, rsems, buf, token = _forward_start(f"gather_fwd_start_{tag}", buf)
            self._forwarding[key] = (buf, ssems, rsems)
            self._tokens.append(token)

    def get(self, name, layer, after=None, prefetch_next=True):
        key = (name, layer)
        if key not in self._ready:
            self._prefetch(key, after)
            buf, ssems, rsems = self._forwarding.pop(key)
            self._ready[key] = _forward_wait(f"gather_fwd_wait_{name}_{layer}", buf, ssems, rsems, after)
            if prefetch_next:
                self.prefetch_after(name, layer, after)
        return self._ready[key]

    def prefetch_after(self, name, layer, after):
        nxt = self._order.index((name, layer)) + 1
        if nxt < len(self._order):
            self._prefetch(self._order[nxt], after)

    def deps(self):
        tokens, self._tokens = self._tokens, []
        return tokens


def _swap_copy(g_ref, land_ref, send_sem, recv_sem):
    x, y, c, _, _, _ = _mesh_pos()
    hr = g_ref.shape[1] // 2
    return pltpu.make_async_remote_copy(src_ref=g_ref.at[:, pl.ds((1 - c) * hr, hr), :], dst_ref=land_ref,
                                        send_sem=send_sem, recv_sem=recv_sem, device_id=(x, y, 1 - c),
                                        device_id_type=MESH)


def _swap_start(name, g):
    land_shape = (g.shape[0], g.shape[1] // 2, g.shape[2])

    def body(g_ref, land_ref, send_sem, recv_sem, land_thru, token):
        _swap_copy(g_ref, land_ref, send_sem, recv_sem).start()
        token[...] = jnp.zeros_like(token)

    return pl.pallas_call(
        body, name=name,
        in_specs=[HBM_SPEC, HBM_SPEC],
        out_specs=[SEM_SPEC, SEM_SPEC, HBM_SPEC, TOKEN_SPEC],
        out_shape=[pltpu.SemaphoreType.DMA(()), pltpu.SemaphoreType.DMA(()), pltpu.HBM(land_shape, g.dtype),
                   TOKEN_SHAPE],
        input_output_aliases={1: 2},
        compiler_params=pltpu.CompilerParams(has_side_effects=SPLIT_COPY),
    )(_in_hbm(g), _in_hbm(lax.empty(land_shape, g.dtype)))


def _swap_wait(name, g, land, send_sem, recv_sem, after):
    def body(g_ref, land_ref, send_sem, recv_sem, after_ref, land_out):
        cp = _swap_copy(g_ref, land_ref, send_sem, recv_sem)
        cp.wait_send()
        cp.wait_recv()

    return pl.pallas_call(
        body, name=name,
        in_specs=[HBM_SPEC, HBM_SPEC, SEM_SPEC, SEM_SPEC, _hbm_spec()],
        out_specs=HBM_SPEC,
        out_shape=pltpu.HBM(land.shape, land.dtype),
        input_output_aliases={1: 0},
        compiler_params=pltpu.CompilerParams(has_side_effects=SPLIT_COPY),
    )(_in_hbm(g), land, send_sem, recv_sem, after)


def _add_my_half(name, g, r, pos_arr):
    ns, R, C = g.shape
    hr = R // 2
    tr = min(256, hr)
    nt = hr // tr

    def body(pos_ref, g_ref, r_ref, o_ref, land_ref):
        t = (g_ref[...] + r_ref[...]).astype(o_ref.dtype)
        o_ref[...] = t

        @pl.when(pl.program_id(1) == pos_ref[1])
        def _():
            land_ref[...] = t

    blk = pl.BlockSpec((None, tr, C), lambda i, s, pos_ref: (s, i, 0))
    return pl.pallas_call(
        body, name=name,
        grid_spec=pltpu.PrefetchScalarGridSpec(
            num_scalar_prefetch=1, grid=(nt, ns),
            in_specs=[pl.BlockSpec((None, tr, C), lambda i, s, pos_ref: (s, pos_ref[0] * nt + i, 0)), blk],
            out_specs=[blk, pl.BlockSpec((None, tr, C), lambda i, s, pos_ref: (pos_ref[1], i, 0))]),
        out_shape=[SDS((ns, hr, C), BF16)] * 2,
        compiler_params=_cp(2),
    )(pos_arr, g, r)


def _exchange_start(name, part, land):
    def body(part_ref, land_ref, send_sems, recv_sems, land_thru, token):
        x, y, c, me, chips, chip_idx = _mesh_pos()
        for j in range(N_PEER_CHIPS):
            pltpu.make_async_remote_copy(src_ref=part_ref.at[chip_idx[j]], dst_ref=land_ref.at[me],
                                         send_sem=send_sems.at[j], recv_sem=recv_sems.at[j],
                                         device_id=(*chips[j], c), device_id_type=MESH).start()
        token[...] = jnp.zeros_like(token)

    sems = pltpu.SemaphoreType.DMA((N_PEER_CHIPS,))
    return pl.pallas_call(
        body, name=name,
        in_specs=[HBM_SPEC, HBM_SPEC],
        out_specs=[SEM_SPEC, SEM_SPEC, HBM_SPEC, TOKEN_SPEC],
        out_shape=[sems, sems, pltpu.HBM(land.shape, land.dtype), TOKEN_SHAPE],
        input_output_aliases={1: 2},
        compiler_params=pltpu.CompilerParams(has_side_effects=SPLIT_COPY),
    )(_in_hbm(part), _in_hbm(land))


def _exchange_wait(name, part, land, send_sems, recv_sems, after):
    def body(part_ref, land_ref, send_sems, recv_sems, after_ref, land_out):
        x, y, c, me, chips, chip_idx = _mesh_pos()
        for j in range(N_PEER_CHIPS):
            cp = pltpu.make_async_remote_copy(src_ref=part_ref.at[chip_idx[j]], dst_ref=land_ref.at[chip_idx[j]],
                                              send_sem=send_sems.at[j], recv_sem=recv_sems.at[j],
                                              device_id=(*chips[j], c), device_id_type=MESH)
            cp.wait_send()
            cp.wait_recv()

    return pl.pallas_call(
        body, name=name,
        in_specs=[HBM_SPEC, HBM_SPEC, SEM_SPEC, SEM_SPEC, _hbm_spec()],
        out_specs=HBM_SPEC,
        out_shape=pltpu.HBM(land.shape, land.dtype),
        input_output_aliases={1: 0},
        compiler_params=pltpu.CompilerParams(has_side_effects=SPLIT_COPY),
    )(_in_hbm(part), land, send_sems, recv_sems, after)


class _GradReducer:
    def __init__(self, c_arr):
        self._c_arr = c_arr
        self._swapping = []
        self._exchanging = {}
        self._joining = {}
        self._tokens = []

    def begin(self, name, layer, g):
        tag = f"{name}_{layer}"
        ssem, rsem, land, token = _swap_start(f"rs_swap_start_{tag}", g)
        self._swapping.append((name, layer, g, ssem, rsem, land))
        self._tokens.append(token)

    def advance(self, after):
        for name, layer, g, ssem, rsem, land in self._swapping:
            tag = f"{name}_{layer}"
            theirs = _swap_wait(f"rs_swap_wait_{tag}", g, land, ssem, rsem, after)
            part, own = _add_my_half(f"rs_add_{tag}", g, theirs, self._c_arr)
            ssems, rsems, land2, token = _exchange_start(f"rs_xchg_start_{tag}", part, own)
            self._exchanging[(name, layer)] = (part, ssems, rsems, land2)
            self._tokens.append(token)
        self._swapping = []

    def deps(self):
        tokens, self._tokens = self._tokens, []
        return tokens

    def reduce(self, name, n_layers, after):
        buf = None
        for layer in range(n_layers):
            part, ssems, rsems, land = self._exchanging.pop((name, layer))
            tag = f"{name}_{layer}"
            landed = _exchange_wait(f"rs_xchg_wait_{tag}", part, land, ssems, rsems, after)
            buf = _sum_chips(f"rs_sum_{tag}", landed, self._c_arr, layer, n_layers, buf)
        ssem, rsem, buf, token = _join_start(f"rs_join_start_{name}", buf)
        self._joining[name] = (buf, ssem, rsem)
        return token

    def reduced(self, name, after):
        buf, ssem, rsem = self._joining.pop(name)
        return _join_wait(f"rs_join_wait_{name}", buf, ssem, rsem, after)


def _sum_chips(name, r, c_arr, layer, n_layers, prev):
    ns, H, C = r.shape
    tr = min(256, H)
    nt = H // tr

    def body(c_ref, r_ref, *rest):
        o_ref = rest[-1]
        o_ref[...] = ((r_ref[0].astype(F32) + r_ref[1].astype(F32)) + r_ref[2].astype(F32)) + r_ref[3].astype(F32)

    in_specs = [pl.BlockSpec((ns, tr, C), lambda i, c_ref: (0, i, 0))]
    args = [c_arr, r]
    aliases = {}
    if prev is not None:
        in_specs.append(_hbm_spec())
        args.append(prev)
        aliases = {2: 0}
    return pl.pallas_call(
        body, name=name,
        grid_spec=pltpu.PrefetchScalarGridSpec(
            num_scalar_prefetch=1, grid=(nt,), in_specs=in_specs,
            out_specs=pl.BlockSpec((None, tr, C), lambda i, c_ref: (layer, c_ref[0] * nt + i, 0))),
        out_shape=SDS((n_layers, 2 * H, C), F32),
        input_output_aliases=aliases,
        compiler_params=_cp(1),
    )(*args)


def _join_copy(buf_ref, send_sem, recv_sem):
    x, y, c, _, _, _ = _mesh_pos()
    hr = buf_ref.shape[1] // 2
    mine = buf_ref.at[:, pl.ds(c * hr, hr), :]
    theirs = buf_ref.at[:, pl.ds((1 - c) * hr, hr), :]
    send = pltpu.make_async_remote_copy(src_ref=mine, dst_ref=mine, send_sem=send_sem, recv_sem=recv_sem,
                                        device_id=(x, y, 1 - c), device_id_type=MESH)
    arrive = pltpu.make_async_remote_copy(src_ref=theirs, dst_ref=theirs, send_sem=send_sem, recv_sem=recv_sem,
                                          device_id=(x, y, 1 - c), device_id_type=MESH)
    return send, arrive


def _join_start(name, buf):
    def body(buf_ref, send_sem, recv_sem, buf_thru, token):
        _join_copy(buf_ref, send_sem, recv_sem)[0].start()
        token[...] = jnp.zeros_like(token)

    return pl.pallas_call(
        body, name=name,
        in_specs=[HBM_SPEC],
        out_specs=[SEM_SPEC, SEM_SPEC, HBM_SPEC, TOKEN_SPEC],
        out_shape=[pltpu.SemaphoreType.DMA(()), pltpu.SemaphoreType.DMA(()), pltpu.HBM(buf.shape, buf.dtype),
                   TOKEN_SHAPE],
        input_output_aliases={0: 2},
        compiler_params=pltpu.CompilerParams(has_side_effects=SPLIT_COPY),
    )(_in_hbm(buf))


def _join_wait(name, buf, send_sem, recv_sem, after):
    def body(buf_ref, send_sem, recv_sem, after_ref, buf_out):
        send, arrive = _join_copy(buf_ref, send_sem, recv_sem)
        send.wait_send()
        arrive.wait_recv()

    return pl.pallas_call(
        body, name=name,
        in_specs=[HBM_SPEC, SEM_SPEC, SEM_SPEC, _hbm_spec()],
        out_specs=HBM_SPEC,
        out_shape=pltpu.HBM(buf.shape, buf.dtype),
        input_output_aliases={0: 0},
        compiler_params=pltpu.CompilerParams(has_side_effects=SPLIT_COPY),
    )(buf, send_sem, recv_sem, after)


def _small_copy(k, buf_ref, land_ref, send_sems, recv_sems):
    x, y, c = lax.axis_index("x"), lax.axis_index("y"), lax.axis_index("c")
    me = 4 * x + 2 * y + c
    peer = (x ^ ((k >> 2) & 1), y ^ ((k >> 1) & 1), c ^ (k & 1))
    cp = pltpu.make_async_remote_copy(src_ref=buf_ref, dst_ref=land_ref.at[me], send_sem=send_sems.at[k - 1],
                                      recv_sem=recv_sems.at[k - 1], device_id=peer, device_id_type=MESH)
    return me, peer, cp


def _small_start(buf, deps):
    land = jnp.broadcast_to(buf[None], (N_DEV,) + buf.shape)
    n_dep = len(deps)

    def body(buf_ref, land_ref, *rest):
        send_sems, recv_sems, _, token = rest[n_dep:]
        for k in range(1, N_DEV):
            _small_copy(k, buf_ref, land_ref, send_sems, recv_sems)[2].start()
        token[...] = jnp.zeros_like(token)

    sems = pltpu.SemaphoreType.DMA((N_DEV - 1,))
    return pl.pallas_call(
        body, name="small_gather_start",
        in_specs=[HBM_SPEC, HBM_SPEC] + [_hbm_spec()] * n_dep,
        out_specs=[SEM_SPEC, SEM_SPEC, HBM_SPEC, TOKEN_SPEC],
        out_shape=[sems, sems, pltpu.HBM(land.shape, land.dtype), TOKEN_SHAPE],
        input_output_aliases={1: 2},
        compiler_params=pltpu.CompilerParams(has_side_effects=SPLIT_COPY),
    )(_in_hbm(buf), _in_hbm(land), *deps)


def _small_wait(buf, land, send_sems, recv_sems, after):
    def body(buf_ref, land_ref, send_sems, recv_sems, after_ref, land_out):
        for k in range(1, N_DEV):
            me, peer, cp = _small_copy(k, buf_ref, land_ref, send_sems, recv_sems)
            cp.wait_send()
            got = land_ref.at[me ^ k]
            pltpu.make_async_remote_copy(src_ref=got, dst_ref=got, send_sem=send_sems.at[k - 1],
                                         recv_sem=recv_sems.at[k - 1], device_id=peer,
                                         device_id_type=MESH).wait_recv()

    return pl.pallas_call(
        body, name="small_gather_wait",
        in_specs=[HBM_SPEC, HBM_SPEC, SEM_SPEC, SEM_SPEC, _hbm_spec()],
        out_specs=HBM_SPEC,
        out_shape=pltpu.HBM(land.shape, land.dtype),
        input_output_aliases={1: 0},
        compiler_params=pltpu.CompilerParams(has_side_effects=SPLIT_COPY),
    )(_in_hbm(buf), land, send_sems, recv_sems, after)


def _sum_devices(land):
    n, R, C = land.shape

    def body(land_ref, out_ref):
        acc = land_ref[0]
        for d in range(1, n):
            acc = acc + land_ref[d]
        out_ref[...] = acc

    return pl.pallas_call(
        body, name="small_sum",
        in_specs=[pl.BlockSpec(memory_space=pltpu.VMEM)],
        out_specs=pl.BlockSpec(memory_space=pltpu.VMEM),
        out_shape=SDS((R, C), land.dtype),
        compiler_params=pltpu.CompilerParams(vmem_limit_bytes=V7X_VMEM_LIMIT),
    )(land)


def _deinterleave(t, d):
    if d == 1:
        return t
    S, W = t.shape
    return t.reshape(S // d, d, W).transpose(1, 0, 2).reshape(S, W)


def _interleave(t, d):
    if d == 1:
        return t
    S, W = t.shape
    return t.reshape(d, S // d, W).transpose(1, 0, 2).reshape(S, W)


def _to_patterns(t, off):
    return jnp.stack([_deinterleave(t[:, off + PW * g:off + PW * (g + 1)], PATTERN_DILATION[g])
                      for g in range(N_PATTERNS)])


def _from_patterns(t3):
    return jnp.stack([_interleave(t3[g], PATTERN_DILATION[g]) for g in range(N_PATTERNS)])


def _pack_rows(vectors):
    flat = jnp.concatenate([v.reshape(-1) for v in vectors])
    n = flat.shape[0]
    padded = -(-n // 1024) * 1024
    return jnp.pad(flat, (0, padded - n)).reshape(padded // 128, 128)


def _unpack_rows(buf, shapes):
    flat = buf.reshape(-1)
    out, off = [], 0
    for s in shapes:
        n = 1
        for dim in s:
            n *= dim
        out.append(flat[off:off + n].reshape(s))
        off += n
    return out


def _layer_forward(l, x, prm, wg):
    S, D = x.shape
    w_in = wg.get("w_in", l, x, prefetch_next=l > 0)
    p, h = _norm_matmul(f"in_proj_{l}", x, prm["attn_norm"][l], w_in, F32, deps=wg.deps())
    if l == 0:
        wg.prefetch_after("w_in", l, p)
    y_a = _sgu_fwd(f"sgu_fwd_{l}", p, prm["sgu_wt"][l], prm["sgu_bb"][l])
    y_b = _conv_fwd(f"conv_fwd_{l}", p, prm["conv_w"][l])
    os, lses = [], []
    for g in range(N_PATTERNS):
        o_g, lse_g = _attn_fwd(f"attn_fwd_{l}_{g}", p, g, prm["q_gain"][l], prm["k_gain"][l], prm["bd"])
        os.append(o_g)
        lses.append(lse_g)
    y_c = _mix_fwd(f"mix_fwd_{l}", os, lses)
    ycat = jnp.concatenate([y_a, y_b, y_c], axis=1)
    tmb, tnb = min(1024, S), min(1024, D)
    w_out = wg.get("w_out", l, ycat)
    kq = N_CHIPS * w_out.shape[1]
    x1 = _matmul(
        f"out_proj_{l}", ycat, w_out.reshape(kq, D), (S, D), F32, grid=(S // tmb, D // tnb, 1),
        a_spec=pl.BlockSpec((tmb, kq), lambda i, j, k: (i, 0)),
        b_spec=pl.BlockSpec((kq, tnb), lambda i, j, k: (0, j)),
        o_spec=pl.BlockSpec((tmb, tnb), lambda i, j, k: (i, j)),
        contract=(1, 0), acc_shape=(tmb, tnb),
        extras=(x,), extra_specs=(pl.BlockSpec((tmb, tnb), lambda i, j, k: (i, j)),),
        epi=lambda r, res: r + res, deps=wg.deps())
    w_mlp_in = wg.get("w_mlp_in", l, x1)
    a, h2 = _norm_matmul(f"mlp_in_{l}", x1, prm["mlp_norm"][l], w_mlp_in, BF16, deps=wg.deps())
    w_mlp_out = wg.get("w_mlp_out", l, a)
    dff4 = w_mlp_out.shape[1]
    tk = min(2048, dff4)
    kpc = dff4 // tk
    x2 = _matmul(
        f"mlp_out_{l}", a, w_mlp_out, (S, D), F32, grid=(S // tmb, D // tnb, N_CHIPS * kpc),
        a_spec=pl.BlockSpec((tmb, tk), lambda i, j, k: (i, k)),
        b_spec=pl.BlockSpec((None, tk, tnb), lambda i, j, k: (k // kpc, k % kpc, j)),
        o_spec=pl.BlockSpec((tmb, tnb), lambda i, j, k: (i, j)),
        contract=(1, 0), acc_shape=(tmb, tnb), a_pre=_relu2_bf16,
        extras=(x1,), extra_specs=(pl.BlockSpec((tmb, tnb), lambda i, j, k: (i, j)),),
        epi=lambda r, res: r + res, deps=wg.deps())
    saved = dict(x=x, p=p, h=h, os=os, lses=lses, ycat=ycat, x1=x1, a=a, h2=h2)
    return x2, saved


def _layer_backward(l, dx2, dx2b, sv, prm, wg, sink):
    S, D = dx2.shape
    w_in, w_out = wg.get("w_in", l), wg.get("w_out", l)
    w_mlp_in, w_mlp_out = wg.get("w_mlp_in", l), wg.get("w_mlp_out", l)
    dff4 = w_mlp_in.shape[-1]
    dff = N_CHIPS * dff4
    tm = min(512, S)
    tk = min(1024, S)
    nks = S // tk

    tmb, tnb = min(1024, S), min(1024, D)
    da = _matmul(
        f"mlp_out_bwd_{l}", dx2b, w_mlp_out, (S, dff), BF16, grid=(S // tmb, N_CHIPS, 1),
        a_spec=pl.BlockSpec((tmb, D), lambda i, j, k: (i, 0)),
        b_spec=pl.BlockSpec((None, dff4, D), lambda i, j, k: (j, 0, 0)),
        o_spec=pl.BlockSpec((tmb, dff4), lambda i, j, k: (i, j)),
        contract=(1, 1), acc_shape=(tmb, dff4),
        extras=(sv["a"],), extra_specs=(pl.BlockSpec((tmb, dff4), lambda i, j, k: (i, j)),),
        epi=lambda r, act: r * (2.0 * jnp.maximum(act.astype(F32), 0.0)), deps=sink.deps())
    tmw = min(1024, dff4)
    mpc = dff4 // tmw
    g_w2 = _matmul(
        f"mlp_out_dw_{l}", sv["a"], dx2b, (N_CHIPS, dff4, D), F32, grid=(N_CHIPS * mpc, 1, nks),
        a_spec=pl.BlockSpec((tk, tmw), lambda i, j, k: (k, i)),
        b_spec=pl.BlockSpec((tk, D), lambda i, j, k: (k, 0)),
        o_spec=pl.BlockSpec((None, tmw, D), lambda i, j, k: (i // mpc, i % mpc, 0)),
        contract=(0, 0), acc_shape=(tmw, D), a_pre=_relu2_bf16)
    sink.begin("w_mlp_out", l, g_w2)
    dh2 = _matmul(
        f"mlp_in_bwd_{l}", da, w_mlp_in, (S, D), F32, grid=(S // tmb, D // tnb, N_CHIPS),
        a_spec=pl.BlockSpec((tmb, dff4), lambda i, j, k: (i, k)),
        b_spec=pl.BlockSpec((None, tnb, dff4), lambda i, j, k: (k, j, 0)),
        o_spec=pl.BlockSpec((tmb, tnb), lambda i, j, k: (i, j)),
        contract=(1, 1), acc_shape=(tmb, tnb), deps=sink.deps())
    sink.advance(dh2)
    tmd = min(1024, D)
    g_w1 = _matmul(
        f"mlp_in_dw_{l}", sv["h2"], da, (N_CHIPS, D, dff4), F32, grid=(N_CHIPS, D // tmd, nks),
        a_spec=pl.BlockSpec((tk, tmd), lambda i, j, k: (k, j)),
        b_spec=pl.BlockSpec((tk, dff4), lambda i, j, k: (k, i)),
        o_spec=pl.BlockSpec((None, tmd, dff4), lambda i, j, k: (i, j, 0)),
        contract=(0, 0), acc_shape=(tmd, dff4))
    sink.begin("w_mlp_in", l, g_w1)
    dx1, dx1b, g_mlp_norm = _rmsnorm_bwd(f"mlp_norm_bwd_{l}", dh2, sv["x1"], prm["mlp_norm"][l], dx2,
                                         deps=sink.deps())

    rq = w_out.shape[1]
    dycat = _matmul(
        f"out_proj_bwd_{l}", dx1b, w_out, (S, N_CHIPS * rq), F32, grid=(S // tmb, N_CHIPS, 1),
        a_spec=pl.BlockSpec((tmb, D), lambda i, j, k: (i, 0)),
        b_spec=pl.BlockSpec((None, rq, D), lambda i, j, k: (j, 0, 0)),
        o_spec=pl.BlockSpec((tmb, rq), lambda i, j, k: (i, j)),
        contract=(1, 1), acc_shape=(tmb, rq))
    sink.advance(dycat)
    g_wout = _matmul(
        f"out_proj_dw_{l}", sv["ycat"], dx1b, (N_CHIPS, rq, D), F32, grid=(N_CHIPS, 1, nks),
        a_spec=pl.BlockSpec((tk, rq), lambda i, j, k: (k, i)),
        b_spec=pl.BlockSpec((tk, D), lambda i, j, k: (k, 0)),
        o_spec=pl.BlockSpec((None, rq, D), lambda i, j, k: (i, 0, 0)),
        contract=(0, 0), acc_shape=(rq, D))
    sink.begin("w_out", l, g_wout)

    p = sv["p"]
    du, dv_a, g_sgu_w, db_lanes = _sgu_bwd(f"sgu_bwd_{l}", p, dycat, prm["sgu_wt"][l], prm["sgu_wtt"][l],
                                           prm["sgu_bb"][l])
    g_sgu_b = db_lanes[:, :A_HEADS].T
    db, dc, dxb, g_conv = _conv_bwd(f"conv_bwd_{l}", p, dycat, prm["conv_w"][l])
    do3, c3 = _mix_bwd(f"mix_bwd_{l}", sv["os"], sv["lses"], dycat, prm["bd"])
    dqs, dks, dvs, dgqs, dgks = [], [], [], [], []
    for g in range(N_PATTERNS):
        dq, dk, dv, dgq, dgk = _attn_bwd(f"attn_bwd_{l}_{g}", p, g, sv["lses"][g], do3, c3,
                                         prm["q_gain"][l], prm["k_gain"][l], prm["bd"])
        dqs.append(dq)
        dks.append(dk)
        dvs.append(dv)
        dgqs.append(dgq)
        dgks.append(dgk)
    g_q = jnp.concatenate(dgqs, axis=1).reshape(N_PATTERNS * PW // HEAD_DIM, HEAD_DIM).sum(axis=0)
    g_k = jnp.concatenate(dgks, axis=1).reshape(N_PATTERNS * PW // HEAD_DIM, HEAD_DIM).sum(axis=0)
    dp = jnp.concatenate([du, dv_a, db, dc, dxb] + [t.astype(BF16) for t in dqs + dks + dvs], axis=1)

    ns_in = w_in.shape[-1]
    g_win = _matmul(
        f"in_proj_dw_{l}", sv["h"], dp, (N_CHIPS, D, ns_in), F32, grid=(N_CHIPS, D // tmd, nks),
        a_spec=pl.BlockSpec((tk, tmd), lambda i, j, k: (k, j)),
        b_spec=pl.BlockSpec((tk, ns_in), lambda i, j, k: (k, i)),
        o_spec=pl.BlockSpec((None, tmd, ns_in), lambda i, j, k: (i, j, 0)),
        contract=(0, 0), acc_shape=(tmd, ns_in))
    sink.begin("w_in", l, g_win)
    dh = _matmul(
        f"in_proj_bwd_{l}", dp, w_in, (S, D), F32, grid=(S // tmb, D // tnb, N_CHIPS),
        a_spec=pl.BlockSpec((tmb, ns_in), lambda i, j, k: (i, k)),
        b_spec=pl.BlockSpec((None, tnb, ns_in), lambda i, j, k: (k, j, 0)),
        o_spec=pl.BlockSpec((tmb, tnb), lambda i, j, k: (i, j)),
        contract=(1, 1), acc_shape=(tmb, tnb), deps=sink.deps())
    sink.advance(dh)
    dx0, dx0b, g_attn_norm = _rmsnorm_bwd(f"attn_norm_bwd_{l}", dh, sv["x"], prm["attn_norm"][l], dx1,
                                          deps=sink.deps())

    big = dict(w_in=g_win, w_out=g_wout, w_mlp_in=g_w1, w_mlp_out=g_w2)
    small = dict(attn_norm=g_attn_norm.reshape(-1), sgu_w=g_sgu_w, sgu_b=g_sgu_b, conv_w=g_conv,
                 q_norm=g_q, k_norm=g_k, mlp_norm=g_mlp_norm.reshape(-1))
    return dx0, dx0b, big, small


BIG = ("w_in", "w_out", "w_mlp_in", "w_mlp_out")
SMALL_REPLICATED = ("attn_norm", "sgu_w", "sgu_b", "q_norm", "k_norm", "mlp_norm")


def _local_step(x, target, prm, wg, n_layers, sink):
    saved = []
    h = x
    for l in range(n_layers):
        h, sv = _layer_forward(l, h, prm, wg)
        saved.append(sv)
    dy, dyb, colsq = _loss_kernel(h, target)
    loss = 0.5 * jnp.sum(colsq) / x.shape[1]
    bigs, smalls = [None] * n_layers, [None] * n_layers
    for l in reversed(range(n_layers)):
        dy, dyb, bigs[l], smalls[l] = _layer_backward(l, dy, dyb, saved[l], prm, wg, sink)
    return loss, dy, bigs, smalls


def _prepare_params(attn_norm, sgu_w, sgu_b, conv_full, q_norm, k_norm, mlp_norm):
    n_layers = attn_norm.shape[0]
    tri = jnp.tril(sgu_w)
    idx = jnp.arange(PW)
    bd = (idx[:, None] // HEAD_DIM == idx[None, :] // HEAD_DIM).astype(BF16)
    return dict(
        attn_norm=[attn_norm[l][None, :] for l in range(n_layers)],
        mlp_norm=[mlp_norm[l][None, :] for l in range(n_layers)],
        sgu_wt=[tri[l].astype(BF16) for l in range(n_layers)],
        sgu_wtt=[tri[l].transpose(0, 2, 1).astype(BF16) for l in range(n_layers)],
        sgu_bb=[jnp.repeat(sgu_b[l].T, HEAD_DIM, axis=1) for l in range(n_layers)],
        conv_w=[conv_full[l] for l in range(n_layers)],
        q_gain=[jnp.tile(q_norm[l], PW // HEAD_DIM)[None, :] for l in range(n_layers)],
        k_gain=[jnp.tile(k_norm[l], PW // HEAD_DIM)[None, :] for l in range(n_layers)],
        bd=bd,
    )


def kernel(x, attn_norm, w_in, sgu_w, sgu_b, conv_w, q_norm, k_norm, w_out, mlp_norm, w_mlp_in, w_mlp_out, loss_target, m_attn_norm, m_w_in, m_sgu_w, m_sgu_b, m_conv_w, m_q_norm, m_k_norm, m_w_out, m_mlp_norm, m_w_mlp_in, m_w_mlp_out, v_attn_norm, v_w_in, v_sgu_w, v_sgu_b, v_conv_w, v_q_norm, v_k_norm, v_w_out, v_mlp_norm, v_w_mlp_in, v_w_mlp_out):
    n_layers = attn_norm.shape[0]
    weights = dict(attn_norm=attn_norm, w_in=w_in, sgu_w=sgu_w, sgu_b=sgu_b, conv_w=conv_w, q_norm=q_norm,
                   k_norm=k_norm, w_out=w_out, mlp_norm=mlp_norm, w_mlp_in=w_mlp_in, w_mlp_out=w_mlp_out)
    mom_m = dict(attn_norm=m_attn_norm, w_in=m_w_in, sgu_w=m_sgu_w, sgu_b=m_sgu_b, conv_w=m_conv_w,
                 q_norm=m_q_norm, k_norm=m_k_norm, w_out=m_w_out, mlp_norm=m_mlp_norm, w_mlp_in=m_w_mlp_in,
                 w_mlp_out=m_w_mlp_out)
    mom_v = dict(attn_norm=v_attn_norm, w_in=v_w_in, sgu_w=v_sgu_w, sgu_b=v_sgu_b, conv_w=v_conv_w,
                 q_norm=v_q_norm, k_norm=v_k_norm, w_out=v_w_out, mlp_norm=v_mlp_norm, w_mlp_in=v_w_mlp_in,
                 w_mlp_out=v_w_mlp_out)
    order = ("attn_norm", "w_in", "sgu_w", "sgu_b", "conv_w", "q_norm", "k_norm", "w_out", "mlp_norm",
             "w_mlp_in", "w_mlp_out")
    chip = 2 * lax.axis_index("x") + lax.axis_index("y")
    c_arr = jnp.stack([lax.axis_index("c"), chip]).astype(jnp.int32)

    conv_cols = conv_w.shape[-1]
    chip_arr = chip.astype(jnp.int32).reshape(1)
    conv_pack = jnp.pad(conv_w.reshape(-1), (0, 2048 - conv_w.size)).reshape(1, 16, 128)
    wg = _GatheredWeights()
    wg.start([("conv_w", 0), ("w_in", 0)],
             [_place_shard("place_conv_w", conv_pack, 0, chip_arr, F32),
              _place_shard("place_w_in_0", weights["w_in"], 0, chip_arr, BF16)])
    keys = [(n, l) for l in range(n_layers) for n in BIG if (n, l) != ("w_in", 0)]
    first = wg.deps()
    wg.start(keys, [_place_shard(f"place_{n}_{l}", weights[n], l, chip_arr, BF16, deps=first) for n, l in keys])
    conv_full = wg.get("conv_w", 0, wg.deps()[-1]).reshape(N_CHIPS, 2048)[:, :conv_w.size].reshape(N_CHIPS, n_layers, 3, conv_cols)
    conv_full = conv_full.transpose(1, 2, 0, 3).reshape(n_layers, 3, N_CHIPS * conv_cols)
    prm = _prepare_params(attn_norm, sgu_w, sgu_b, conv_full, q_norm, k_norm, mlp_norm)

    sink = _GradReducer(c_arr)
    loss_local, grad_x, _, smalls = _local_step(x[0], loss_target[0], prm, wg, n_layers, sink)
    loss = lax.psum(loss_local, ("x", "y", "c"))

    small_names = SMALL_REPLICATED + ("conv_w",)
    small_shapes = [(n_layers,) + tuple(smalls[0][n].shape) for n in small_names]
    packed = _pack_rows([jnp.stack([smalls[l][n] for l in range(n_layers)]) for n in small_names])
    small_send, small_recv, small_land, small_token = _small_start(packed, sink.deps())

    grads, delta, new_m, new_v = {}, {}, {}, {}

    def update(n, after):
        shp = weights[n].shape
        two_d = (shp[0] * shp[1], shp[2])
        d, nm, nv, g = _adamw(f"adamw_{n}", weights[n].reshape(two_d), sink.reduced(n, after).reshape(two_d),
                              mom_m[n].reshape(two_d), mom_v[n].reshape(two_d))
        grads[n], delta[n], new_m[n], new_v[n] = g.reshape(shp), d.reshape(shp), nm.reshape(shp), nv.reshape(shp)

    token = small_token
    for n in ("w_mlp_out", "w_mlp_in", "w_out"):
        token = sink.reduce(n, n_layers, token)
    update("w_mlp_out", token)
    token = sink.reduce("w_in", n_layers, delta["w_mlp_out"])
    update("w_mlp_in", token)
    update("w_out", delta["w_mlp_in"])
    update("w_in", delta["w_out"])
    small_land = _small_wait(packed, small_land, small_send, small_recv, delta["w_in"])
    grads.update(zip(small_names, _unpack_rows(_sum_devices(small_land), small_shapes)))
    grads["conv_w"] = lax.dynamic_slice_in_dim(grads["conv_w"], chip * conv_cols, conv_cols, axis=2)
    smalls_all = SMALL_REPLICATED + ("conv_w",)
    shapes = [weights[n].shape for n in smalls_all]
    d, nm, nv, _ = _adamw("adamw_small",
                          _pack_rows([weights[n] for n in smalls_all]), _pack_rows([grads[n] for n in smalls_all]),
                          _pack_rows([mom_m[n] for n in smalls_all]), _pack_rows([mom_v[n] for n in smalls_all]))
    for n, dd, mm, vv in zip(smalls_all, _unpack_rows(d, shapes), _unpack_rows(nm, shapes), _unpack_rows(nv, shapes)):
        delta[n], new_m[n], new_v[n] = dd, mm, vv

    return (loss, grad_x[None], *[grads[n] for n in order], *[delta[n] for n in order],
            *[new_m[n] for n in order], *[new_v[n] for n in order])
```

```python
import jax
import jax.numpy as jnp
from jax import lax
from jax.experimental import pallas as pl
from jax.experimental.pallas import tpu as pltpu

F32 = jnp.float32
BF16 = jnp.bfloat16
SDS = jax.ShapeDtypeStruct

EPS = 1e-6
HEAD_DIM = 64
A_HEADS = 8
A_WIDTH = 512
CHUNK = 128
B_WIDTH = 768
C_WIDTH = 768
N_PATTERNS = 3
PATTERN_DILATION = (1, 4, 16)
PW = 256
D_IN_PROJ = 5632
OFF_AU, OFF_AV, OFF_BB, OFF_BC, OFF_BX, OFF_Q, OFF_K, OFF_V = 0, 512, 1024, 1792, 2560, 3328, 4096, 4864
N_CHIPS = 4
N_DEV = 8
BLK = 128

ADAM_LR, ADAM_B1, ADAM_B2, ADAM_EPS, ADAM_WD, ADAM_STEP = 0.001, 0.9, 0.999, 1e-08, 0.01, 10

V7X_VMEM_LIMIT = 56 * 1024 * 1024
MESH = pl.DeviceIdType.MESH
NEG = -1e30


def _cp(n_axes):
    return pltpu.CompilerParams(dimension_semantics=("arbitrary",) * n_axes, vmem_limit_bytes=V7X_VMEM_LIMIT)


def _hbm_spec():
    return pl.BlockSpec(memory_space=pl.ANY)


def _norm_matmul(name, x, g, wg, out_dtype, deps=()):
    S, D = x.shape
    ns, _, Ns = wg.shape
    tm = min(512, S)
    n_dep = len(deps)

    def body(x_ref, g_ref, w_ref, *rest):
        o_ref, h_ref, hs_ref = rest[n_dep:]
        @pl.when(pl.program_id(1) == 0)
        def _():
            xv = x_ref[...]
            y = xv * lax.rsqrt(jnp.mean(xv * xv, axis=-1, keepdims=True) + EPS) * g_ref[...]
            hb = y.astype(BF16)
            hs_ref[...] = hb
            h_ref[...] = hb
        o_ref[...] = jnp.dot(hs_ref[...], w_ref[...], preferred_element_type=F32).astype(o_ref.dtype)

    return pl.pallas_call(
        body, name=name, grid=(S // tm, ns),
        in_specs=[pl.BlockSpec((tm, D), lambda i, s: (i, 0)),
                  pl.BlockSpec((1, D), lambda i, s: (0, 0)),
                  pl.BlockSpec((None, D, Ns), lambda i, s: (s, 0, 0))] + [_hbm_spec()] * n_dep,
        out_specs=[pl.BlockSpec((tm, Ns), lambda i, s: (i, s)),
                   pl.BlockSpec((tm, D), lambda i, s: (i, 0))],
        out_shape=[SDS((S, ns * Ns), out_dtype), SDS((S, D), BF16)],
        scratch_shapes=[pltpu.VMEM((tm, D), BF16)],
        compiler_params=_cp(2),
    )(x, g, wg, *deps)


def _matmul(name, a, b, out_shape, out_dtype, *, grid, a_spec, b_spec, o_spec, contract, acc_shape,
            extras=(), extra_specs=(), a_pre=None, epi=None, deps=()):
    nk = grid[2]
    n_ex = len(extras)
    n_dep = len(deps)
    dims = (((contract[0],), (contract[1],)), ((), ()))

    def product(a_ref, b_ref):
        av = a_ref[...]
        if a_pre is not None:
            av = a_pre(av)
        return lax.dot_general(av, b_ref[...], dims, preferred_element_type=F32)

    def finish(r, ex, o_ref):
        if epi is not None:
            r = epi(r, *[e[...] for e in ex])
        o_ref[...] = r.astype(o_ref.dtype)

    def body_single(a_ref, b_ref, *rest):
        finish(product(a_ref, b_ref), rest[:n_ex], rest[n_ex + n_dep])

    def body(a_ref, b_ref, *rest):
        ex = rest[:n_ex]
        o_ref = rest[n_ex + n_dep]
        acc_ref = rest[n_ex + n_dep + 1]
        k = pl.program_id(2)

        @pl.when(k == 0)
        def _():
            acc_ref[...] = product(a_ref, b_ref)

        @pl.when((k > 0) & (k < nk - 1))
        def _():
            acc_ref[...] += product(a_ref, b_ref)

        @pl.when(k == nk - 1)
        def _():
            finish(acc_ref[...] + product(a_ref, b_ref), ex, o_ref)

    return pl.pallas_call(
        body_single if nk == 1 else body, name=name, grid=grid,
        in_specs=[a_spec, b_spec, *extra_specs] + [_hbm_spec()] * n_dep,
        out_specs=o_spec,
        out_shape=SDS(out_shape, out_dtype),
        scratch_shapes=[] if nk == 1 else [pltpu.VMEM(acc_shape, F32)],
        compiler_params=_cp(3),
    )(a, b, *extras, *deps)


def _relu2_bf16(t):
    r = jnp.maximum(t.astype(F32), 0.0)
    return (r * r).astype(BF16)


def _loss_kernel(y, t):
    S, D = y.shape
    tm = min(256, S)

    def body(y_ref, t_ref, dy_ref, dyb_ref, l_ref):
        @pl.when(pl.program_id(0) == 0)
        def _():
            l_ref[...] = jnp.zeros_like(l_ref)
        e = y_ref[...] - t_ref[...]
        l_ref[...] += jnp.sum(e * e, axis=0, keepdims=True)
        dy = e * (1.0 / D)
        dy_ref[...] = dy
        dyb_ref[...] = dy.astype(BF16)

    row = pl.BlockSpec((tm, D), lambda i: (i, 0))
    return pl.pallas_call(
        body, name="loss_head", grid=(S // tm,),
        in_specs=[row, row],
        out_specs=[row, row, pl.BlockSpec((1, D), lambda i: (0, 0))],
        out_shape=[SDS((S, D), F32), SDS((S, D), BF16), SDS((1, D), F32)],
        compiler_params=_cp(1),
    )(y, t)


def _rmsnorm_bwd(name, dh, x, g, dres, deps=()):
    S, D = x.shape
    tm = min(256, S)
    n_dep = len(deps)

    def body(dh_ref, x_ref, g_ref, dres_ref, *rest):
        dx_ref, dxb_ref, dg_ref = rest[n_dep:]
        @pl.when(pl.program_id(0) == 0)
        def _():
            dg_ref[...] = jnp.zeros_like(dg_ref)
        xv = x_ref[...]
        dhv = dh_ref[...]
        rstd = lax.rsqrt(jnp.mean(xv * xv, axis=-1, keepdims=True) + EPS)
        xhat = xv * rstd
        dg_ref[...] += jnp.sum(dhv * xhat, axis=0, keepdims=True)
        dxn = dhv * g_ref[...]
        dx = dres_ref[...] + rstd * (dxn - xhat * jnp.mean(dxn * xhat, axis=-1, keepdims=True))
        dx_ref[...] = dx
        dxb_ref[...] = dx.astype(BF16)

    row = pl.BlockSpec((tm, D), lambda i: (i, 0))
    vec = pl.BlockSpec((1, D), lambda i: (0, 0))
    return pl.pallas_call(
        body, name=name, grid=(S // tm,),
        in_specs=[row, row, vec, row] + [_hbm_spec()] * n_dep,
        out_specs=[row, row, vec],
        out_shape=[SDS((S, D), F32), SDS((S, D), BF16), SDS((1, D), F32)],
        compiler_params=_cp(1),
    )(dh, x, g, dres, *deps)


def _adamw(name, w, g, m, v):
    R, C = w.shape
    tr = 256 if R % 256 == 0 else R
    c1 = 1.0 - ADAM_B1 ** ADAM_STEP
    c2 = 1.0 - ADAM_B2 ** ADAM_STEP

    def body(w_ref, g_ref, m_ref, v_ref, d_ref, nm_ref, nv_ref, g_out_ref):
        gv = g_ref[...]
        nm = ADAM_B1 * m_ref[...] + (1.0 - ADAM_B1) * gv
        nv = ADAM_B2 * v_ref[...] + (1.0 - ADAM_B2) * (gv * gv)
        m_hat = nm / c1
        v_hat = nv / c2
        d_ref[...] = -ADAM_LR * (m_hat / (jnp.sqrt(v_hat) + ADAM_EPS) + ADAM_WD * w_ref[...])
        nm_ref[...] = nm
        nv_ref[...] = nv
        g_out_ref[...] = gv

    blk = pl.BlockSpec((tr, C), lambda i: (i, 0))
    return pl.pallas_call(
        body, name=name, grid=(R // tr,),
        in_specs=[blk] * 4, out_specs=[blk] * 4,
        out_shape=[SDS((R, C), F32)] * 4,
        compiler_params=_cp(1),
    )(w, g, m, v)


SGU_STEP_ROWS = 512


def _pair_select(lane, lo, hi):
    return jnp.where(lane < HEAD_DIM, lo, hi)


def _sgu_fwd(name, p, wt, bb):
    S = p.shape[0]

    rows = min(SGU_STEP_ROWS, S)

    def body(u_ref, v_ref, wt_ref, bb_ref, o_ref):
        lane = lax.broadcasted_iota(jnp.int32, (CHUNK, 128), 1)
        for ci in range(rows // CHUNK):
            rs = slice(CHUNK * ci, CHUNK * (ci + 1))
            for pp in range(A_HEADS // 2):
                cs = slice(128 * pp, 128 * (pp + 1))
                vb = v_ref[rs, cs].astype(BF16)
                mixed = _pair_select(lane,
                                     jnp.dot(wt_ref[2 * pp], vb, preferred_element_type=F32),
                                     jnp.dot(wt_ref[2 * pp + 1], vb, preferred_element_type=F32)) + bb_ref[:, cs]
                o_ref[rs, cs] = (u_ref[rs, cs] * mixed).astype(o_ref.dtype)

    return pl.pallas_call(
        body, name=name, grid=(S // rows,),
        in_specs=[pl.BlockSpec((rows, A_WIDTH), lambda c: (c, OFF_AU // A_WIDTH)),
                  pl.BlockSpec((rows, A_WIDTH), lambda c: (c, OFF_AV // A_WIDTH)),
                  pl.BlockSpec((A_HEADS, CHUNK, CHUNK), lambda c: (0, 0, 0)),
                  pl.BlockSpec((CHUNK, A_WIDTH), lambda c: (0, 0))],
        out_specs=pl.BlockSpec((rows, A_WIDTH), lambda c: (c, 0)),
        out_shape=SDS((S, A_WIDTH), BF16),
        compiler_params=_cp(1),
    )(p, p, wt, bb)


def _sgu_bwd(name, p, dycat, wt, wtt, bb):
    S = p.shape[0]
    rows = min(SGU_STEP_ROWS, S)

    def body(u_ref, v_ref, dy_ref, wt_ref, wtt_ref, bb_ref, du_ref, dv_ref, dw_ref, db_ref, dbacc_ref):
        c = pl.program_id(0)

        @pl.when(c == 0)
        def _():
            dw_ref[...] = jnp.zeros_like(dw_ref)
            dbacc_ref[...] = jnp.zeros_like(dbacc_ref)

        lane = lax.broadcasted_iota(jnp.int32, (CHUNK, 128), 1)
        row = lax.broadcasted_iota(jnp.int32, (CHUNK, 128), 0)
        causal = row >= lane
        nt = (((1,), (1,)), ((), ()))
        for pp in range(A_HEADS // 2):
            cs = slice(128 * pp, 128 * (pp + 1))
            dw_lo = jnp.zeros((CHUNK, CHUNK), F32)
            dw_hi = jnp.zeros((CHUNK, CHUNK), F32)
            dm_sum = jnp.zeros((CHUNK, 128), F32)
            for ci in range(rows // CHUNK):
                rs = slice(CHUNK * ci, CHUNK * (ci + 1))
                vb = v_ref[rs, cs].astype(BF16)
                dy = dy_ref[rs, cs]
                mixed = _pair_select(lane,
                                     jnp.dot(wt_ref[2 * pp], vb, preferred_element_type=F32),
                                     jnp.dot(wt_ref[2 * pp + 1], vb, preferred_element_type=F32)) + bb_ref[:, cs]
                du_ref[rs, cs] = (dy * mixed).astype(du_ref.dtype)
                dm = dy * u_ref[rs, cs]
                dmb = dm.astype(BF16)
                dv = _pair_select(lane,
                                  jnp.dot(wtt_ref[2 * pp], dmb, preferred_element_type=F32),
                                  jnp.dot(wtt_ref[2 * pp + 1], dmb, preferred_element_type=F32))
                dv_ref[rs, cs] = dv.astype(dv_ref.dtype)
                dm_sum += dm
                dm_lo = jnp.where(lane < HEAD_DIM, dm, 0.0).astype(BF16)
                dm_hi = jnp.where(lane >= HEAD_DIM, dm, 0.0).astype(BF16)
                dw_lo += lax.dot_general(dm_lo, vb, nt, preferred_element_type=F32)
                dw_hi += lax.dot_general(dm_hi, vb, nt, preferred_element_type=F32)
            dbacc_ref[:, cs] += dm_sum
            dw_ref[2 * pp] += jnp.where(causal, dw_lo, 0.0)
            dw_ref[2 * pp + 1] += jnp.where(causal, dw_hi, 0.0)

        @pl.when(c == S // rows - 1)
        def _():
            out = jnp.zeros((CHUNK, 128), F32)
            for pp in range(A_HEADS // 2):
                acc = dbacc_ref[:, 128 * pp:128 * (pp + 1)]
                s_lo = jnp.sum(jnp.where(lane < HEAD_DIM, acc, 0.0), axis=1, keepdims=True)
                s_hi = jnp.sum(jnp.where(lane >= HEAD_DIM, acc, 0.0), axis=1, keepdims=True)
                out = jnp.where(lane == 2 * pp, s_lo, out)
                out = jnp.where(lane == 2 * pp + 1, s_hi, out)
            db_ref[...] = out

    chunk = lambda col: pl.BlockSpec((rows, A_WIDTH), lambda c: (c, col))
    wspec = pl.BlockSpec((A_HEADS, CHUNK, CHUNK), lambda c: (0, 0, 0))
    return pl.pallas_call(
        body, name=name, grid=(S // rows,),
        in_specs=[chunk(OFF_AU // A_WIDTH), chunk(OFF_AV // A_WIDTH), chunk(0), wspec, wspec,
                  pl.BlockSpec((CHUNK, A_WIDTH), lambda c: (0, 0))],
        out_specs=[chunk(0), chunk(0), wspec, pl.BlockSpec((CHUNK, 128), lambda c: (0, 0))],
        out_shape=[SDS((S, A_WIDTH), BF16), SDS((S, A_WIDTH), BF16),
                   SDS((A_HEADS, CHUNK, CHUNK), F32), SDS((CHUNK, 128), F32)],
        scratch_shapes=[pltpu.VMEM((CHUNK, A_WIDTH), F32)],
        compiler_params=_cp(1),
    )(p, p, dycat, wt, wtt, bb)


CONV_HALO = 8
CONV_COLS = 256
CONV_ROWS = 1024


def _shift_down(a, halo, k):
    T = a.shape[0]
    row = lax.broadcasted_iota(jnp.int32, a.shape, 0)
    out = pltpu.roll(a, k, 0)
    for r in range(k):
        out = jnp.where(row == r, halo[CONV_HALO - k + r:CONV_HALO - k + r + 1, :], out)
    return out


def _shift_up(a, halo, k):
    T = a.shape[0]
    row = lax.broadcasted_iota(jnp.int32, a.shape, 0)
    out = pltpu.roll(a, T - k, 0)
    for r in range(k):
        out = jnp.where(row == T - k + r, halo[r:r + 1, :], out)
    return out


def _conv_specs(S, T):
    hb = T // CONV_HALO
    last = S // CONV_HALO - 1
    tile = lambda col0: pl.BlockSpec((T, CONV_COLS), lambda j, i: (i, col0 + j))
    prev = lambda col0: pl.BlockSpec((CONV_HALO, CONV_COLS), lambda j, i: (jnp.maximum(i * hb - 1, 0), col0 + j))
    nxt = lambda col0: pl.BlockSpec((CONV_HALO, CONV_COLS), lambda j, i: (jnp.minimum((i + 1) * hb, last), col0 + j))
    return tile, prev, nxt


def _conv_fwd(name, p, w):
    S = p.shape[0]
    T = min(CONV_ROWS, S)
    tile, prev, _ = _conv_specs(S, T)
    cb, cc, cx = OFF_BB // CONV_COLS, OFF_BC // CONV_COLS, OFF_BX // CONV_COLS

    def body(b_ref, c_ref, x_ref, ch_ref, xh_ref, w_ref, o_ref):
        i = pl.program_id(1)
        z = c_ref[...] * x_ref[...]
        zh = jnp.where(i > 0, ch_ref[...] * xh_ref[...], 0.0)
        z1 = _shift_down(z, zh, 1)
        z2 = _shift_down(z, zh, 2)
        conv = w_ref[0:1, :] * z2 + w_ref[1:2, :] * z1 + w_ref[2:3, :] * z
        o_ref[...] = (b_ref[...] * conv).astype(o_ref.dtype)

    return pl.pallas_call(
        body, name=name, grid=(B_WIDTH // CONV_COLS, S // T),
        in_specs=[tile(cb), tile(cc), tile(cx), prev(cc), prev(cx),
                  pl.BlockSpec((3, CONV_COLS), lambda j, i: (0, j))],
        out_specs=tile(0),
        out_shape=SDS((S, B_WIDTH), BF16),
        compiler_params=_cp(2),
    )(p, p, p, p, p, w)


def _conv_bwd(name, p, dycat, w):
    S = p.shape[0]
    T = min(CONV_ROWS, S)
    tile, prev, nxt = _conv_specs(S, T)
    cb, cc, cx = OFF_BB // CONV_COLS, OFF_BC // CONV_COLS, OFF_BX // CONV_COLS
    cdy = A_WIDTH // CONV_COLS
    n_i = S // T

    def body(b_ref, c_ref, x_ref, dy_ref, ch_ref, xh_ref, bn_ref, dyn_ref, w_ref,
             db_ref, dc_ref, dx_ref, dw_ref):
        i = pl.program_id(1)

        @pl.when(i == 0)
        def _():
            dw_ref[...] = jnp.zeros_like(dw_ref)

        cv = c_ref[...]
        xv = x_ref[...]
        z = cv * xv
        zh = jnp.where(i > 0, ch_ref[...] * xh_ref[...], 0.0)
        z1 = _shift_down(z, zh, 1)
        z2 = _shift_down(z, zh, 2)
        w0, w1, w2 = w_ref[0:1, :], w_ref[1:2, :], w_ref[2:3, :]
        conv = w0 * z2 + w1 * z1 + w2 * z
        dy = dy_ref[...]
        db_ref[...] = (dy * conv).astype(db_ref.dtype)
        dconv = dy * b_ref[...]
        dconv_n = jnp.where(i < n_i - 1, dyn_ref[...] * bn_ref[...], 0.0)
        dz = w2 * dconv + w1 * _shift_up(dconv, dconv_n, 1) + w0 * _shift_up(dconv, dconv_n, 2)
        dc_ref[...] = (dz * xv).astype(dc_ref.dtype)
        dx_ref[...] = (dz * cv).astype(dx_ref.dtype)
        dw_ref[0:1, :] += jnp.sum(dconv * z2, axis=0, keepdims=True)
        dw_ref[1:2, :] += jnp.sum(dconv * z1, axis=0, keepdims=True)
        dw_ref[2:3, :] += jnp.sum(dconv * z, axis=0, keepdims=True)

    wspec = pl.BlockSpec((3, CONV_COLS), lambda j, i: (0, j))
    return pl.pallas_call(
        body, name=name, grid=(B_WIDTH // CONV_COLS, n_i),
        in_specs=[tile(cb), tile(cc), tile(cx), tile(cdy), prev(cc), prev(cx), nxt(cb), nxt(cdy), wspec],
        out_specs=[tile(0), tile(0), tile(0), wspec],
        out_shape=[SDS((S, B_WIDTH), BF16)] * 3 + [SDS((3, B_WIDTH), F32)],
        compiler_params=_cp(2),
    )(p, p, p, dycat, p, p, p, dycat, w)


def _seg_sum(t, bd):
    hi = t.astype(BF16)
    lo = (t - hi.astype(F32)).astype(BF16)
    return jnp.dot(hi, bd, preferred_element_type=F32) + jnp.dot(lo, bd, preferred_element_type=F32)


def _head_norm(x, g, bd):
    rstd = lax.rsqrt(_seg_sum(x * x, bd) * (1.0 / HEAD_DIM) + EPS)
    xhat = x * rstd
    return xhat * g, xhat, rstd


def _head_norm_bwd(dy, g, xhat, rstd, bd):
    dxh = dy * g
    return rstd * (dxh - xhat * (_seg_sum(dxh * xhat, bd) * (1.0 / HEAD_DIM)))


def _band_mask(has_prev):
    row = lax.broadcasted_iota(jnp.int32, (BLK, 2 * BLK), 0)
    col = lax.broadcasted_iota(jnp.int32, (BLK, 2 * BLK), 1)
    first_key = jnp.where(has_prev, 0, BLK)
    return (col >= row) & (col <= row + BLK) & (col >= first_key)


def _first_of_segment(g, n, n_blocks):
    per_seg = lax.shift_right_logical(jnp.int32(n_blocks), 2 * g)
    return (n & (per_seg - 1)) == 0


def _residue_rows(r, d):
    return slice(None) if d == 1 else pl.ds(r, BLK, stride=d)


STRIDED_LANES = 128


def _step_width(d):
    return PW if d == 1 else STRIDED_LANES


def _n_stack(lane):
    return lane.shape[1] // HEAD_DIM


def _for_residues(d, fn):
    if d == 1:
        fn(0)
    else:
        def two(i, carry):
            fn(2 * i)
            fn(2 * i + 1)
            return carry
        lax.fori_loop(0, d // 2, two, 0)


def _head_mask(lane, j):
    return (lane >= HEAD_DIM * j) & (lane < HEAD_DIM * (j + 1))


def _stack_heads(x, lane):
    return jnp.concatenate([jnp.where(_head_mask(lane, j), x, 0.0) for j in range(_n_stack(lane))], axis=0)


def _unstack_heads(y, lane):
    out = y[:BLK]
    for j in range(1, _n_stack(lane)):
        out = jnp.where(lane >= HEAD_DIM * j, y[BLK * j:BLK * (j + 1)], out)
    return out


def _head_columns(v, lane):
    return jnp.concatenate([jnp.max(jnp.where(_head_mask(lane, j), v, NEG), axis=1, keepdims=True)
                            for j in range(_n_stack(lane))], axis=0)


def _attn_fwd(name, p, g, gq, gk, bd):
    S = p.shape[0]
    d = PATTERN_DILATION[g]
    rows = BLK * d
    hw = _step_width(d)
    nt = (((1,), (1,)), ((), ()))

    def body(q_ref, kc_ref, kp_ref, vc_ref, vp_ref, gq_ref, gk_ref, bd_ref, o_ref, lse_ref):
        has_prev = pl.program_id(1) > 0
        bdv = bd_ref[...]
        band = jnp.concatenate([_band_mask(has_prev)] * (hw // HEAD_DIM), axis=0)
        lane = lax.broadcasted_iota(jnp.int32, (1, hw), 1)

        def residue(r):
            rr = _residue_rows(r, d)
            qn, _, _ = _head_norm(q_ref[rr, :], gq_ref[...], bdv)
            kn, _, _ = _head_norm(jnp.concatenate([kp_ref[rr, :], kc_ref[rr, :]], axis=0), gk_ref[...], bdv)
            knb = kn.astype(BF16)
            vb = jnp.concatenate([vp_ref[rr, :], vc_ref[rr, :]], axis=0).astype(BF16)
            qs = _stack_heads(qn, lane).astype(BF16)
            s = lax.dot_general(qs, knb, nt, preferred_element_type=F32) * (HEAD_DIM ** -0.5)
            s = jnp.where(band, s, NEG)
            m = jnp.max(s, axis=1, keepdims=True)
            e = jnp.exp(s - m)
            den = jnp.sum(e, axis=1, keepdims=True)
            pv = jnp.dot(e.astype(BF16), vb, preferred_element_type=F32)
            o_ref[rr, :] = _unstack_heads(pv / den, lane)
            lse_ref[rr, :] = _unstack_heads(jnp.broadcast_to(m + jnp.log(den), pv.shape), lane)

        _for_residues(d, residue)

    per = PW // hw
    cq, ck, cv = (OFF_Q + PW * g) // hw, (OFF_K + PW * g) // hw, (OFF_V + PW * g) // hw
    cur = lambda col: pl.BlockSpec((rows, hw), lambda h, n: (n, col + h))
    prv = lambda col: pl.BlockSpec((rows, hw), lambda h, n: (jnp.maximum(n - 1, 0), col + h))
    vec = pl.BlockSpec((1, hw), lambda h, n: (0, h))
    return pl.pallas_call(
        body, name=name, grid=(per, S // rows),
        in_specs=[cur(cq), cur(ck), prv(ck), cur(cv), prv(cv), vec, vec, pl.BlockSpec((hw, hw), lambda h, n: (0, 0))],
        out_specs=[cur(0), cur(0)],
        out_shape=[SDS((S, PW), F32)] * 2,
        compiler_params=_cp(2),
    )(p, p, p, p, p, gq, gk, bd)


def _attn_bwd(name, p, g, lse, do3, c3, gq, gk, bd):
    S = p.shape[0]
    d = PATTERN_DILATION[g]
    rows = BLK * d
    nblk = S // rows
    hw = _step_width(d)
    nt = (((1,), (1,)), ((), ()))
    tn = (((0,), (0,)), ((), ()))

    def body(q_ref, kc_ref, kp_ref, vc_ref, vp_ref, lse_ref, do_ref, c_ref, gq_ref, gk_ref, bd_ref,
             dq_ref, dk_ref, dv_ref, dgq_ref, dgk_ref, ck_ref, cv_ref, dq_keep_ref):
        n = pl.program_id(1)

        @pl.when(n == 0)
        def _():
            ck_ref[...] = jnp.zeros_like(ck_ref)
            cv_ref[...] = jnp.zeros_like(cv_ref)
            dgq_ref[...] = jnp.zeros_like(dgq_ref)
            dgk_ref[...] = jnp.zeros_like(dgk_ref)

        @pl.when(n == nblk)
        def _():
            dq_ref[...] = dq_keep_ref[...]
            dk_ref[...] = ck_ref[...]
            dv_ref[...] = cv_ref[...]

        bdv = bd_ref[...]
        gqv = gq_ref[...]
        gkv = gk_ref[...]
        band = jnp.concatenate([_band_mask(n > 0)] * (hw // HEAD_DIM), axis=0)
        lane = lax.broadcasted_iota(jnp.int32, (1, hw), 1)

        def residue(r):
            rr = _residue_rows(r, d)
            qn, qhat, qrstd = _head_norm(q_ref[rr, :], gqv, bdv)
            kn, khat, krstd = _head_norm(jnp.concatenate([kp_ref[rr, :], kc_ref[rr, :]], axis=0), gkv, bdv)
            knb = kn.astype(BF16)
            vb = jnp.concatenate([vp_ref[rr, :], vc_ref[rr, :]], axis=0).astype(BF16)
            qs = _stack_heads(qn, lane).astype(BF16)
            dos = _stack_heads(do_ref[rr, :], lane).astype(BF16)
            s = lax.dot_general(qs, knb, nt, preferred_element_type=F32) * (HEAD_DIM ** -0.5)
            prob = jnp.where(band, jnp.exp(s - _head_columns(lse_ref[rr, :], lane)), 0.0)
            dp = lax.dot_general(dos, vb, nt, preferred_element_type=F32)
            ds = (prob * (dp + _head_columns(c_ref[rr, :], lane)) * (HEAD_DIM ** -0.5)).astype(BF16)
            dqn = _unstack_heads(jnp.dot(ds, knb, preferred_element_type=F32), lane)
            dkn = lax.dot_general(ds, qs, tn, preferred_element_type=F32)
            dvv = lax.dot_general(prob.astype(BF16), dos, tn, preferred_element_type=F32)

            dq = _head_norm_bwd(dqn, gqv, qhat, qrstd, bdv)
            dq_ref[rr, :] = dq
            dq_keep_ref[rr, :] = dq
            dk2 = _head_norm_bwd(dkn, gkv, khat, krstd, bdv)
            dgq_ref[...] += jnp.sum(dqn * qhat, axis=0, keepdims=True)
            dgk_ref[...] += jnp.sum(dkn * khat, axis=0, keepdims=True)
            dk_ref[rr, :] = ck_ref[rr, :] + dk2[:BLK]
            dv_ref[rr, :] = cv_ref[rr, :] + dvv[:BLK]
            ck_ref[rr, :] = dk2[BLK:]
            cv_ref[rr, :] = dvv[BLK:]

        @pl.when(n < nblk)
        def _():
            _for_residues(d, residue)

    last = nblk - 1
    per = PW // hw
    cq, ck, cv = (OFF_Q + PW * g) // hw, (OFF_K + PW * g) // hw, (OFF_V + PW * g) // hw
    cur = lambda col: pl.BlockSpec((rows, hw), lambda h, n: (jnp.minimum(n, last), col + h))
    prv = lambda col: pl.BlockSpec((rows, hw), lambda h, n: (jnp.maximum(jnp.minimum(n, last) - 1, 0), col + h))
    cur3 = pl.BlockSpec((None, rows, hw), lambda h, n: (g, jnp.minimum(n, last), h))
    done = pl.BlockSpec((rows, hw), lambda h, n: (jnp.maximum(n - 1, 0), h))
    vec = pl.BlockSpec((1, hw), lambda h, n: (0, h))
    return pl.pallas_call(
        body, name=name, grid=(per, nblk + 1),
        in_specs=[cur(cq), cur(ck), prv(ck), cur(cv), prv(cv), cur(0), cur3, cur3, vec, vec,
                  pl.BlockSpec((hw, hw), lambda h, n: (0, 0))],
        out_specs=[cur(0), done, done, vec, vec],
        out_shape=[SDS((S, PW), F32)] * 3 + [SDS((1, PW), F32)] * 2,
        scratch_shapes=[pltpu.VMEM((rows, hw), F32)] * 3,
        compiler_params=_cp(2),
    )(p, p, p, p, p, lse, do3, c3, gq, gk, bd)


def _mix_fwd(name, os, lses):
    S = os[0].shape[0]
    tm = min(512, S)

    def body(o0, o1, o2, l0, l1, l2, y_ref):
        o = [o0[...], o1[...], o2[...]]
        l = [l0[...], l1[...], l2[...]]
        m = jnp.maximum(jnp.maximum(l[0], l[1]), l[2])
        e = [jnp.exp(t - m) for t in l]
        inv = 1.0 / (e[0] + e[1] + e[2])
        for g in range(N_PATTERNS):
            y_ref[:, PW * g:PW * (g + 1)] = (o[g] * (e[g] * inv)).astype(y_ref.dtype)

    blk = pl.BlockSpec((tm, PW), lambda i: (i, 0))
    return pl.pallas_call(
        body, name=name, grid=(S // tm,),
        in_specs=[blk] * 6,
        out_specs=pl.BlockSpec((tm, C_WIDTH), lambda i: (i, 0)),
        out_shape=SDS((S, C_WIDTH), BF16),
        compiler_params=_cp(1),
    )(*os, *lses)


def _mix_bwd(name, os, lses, dycat, bd):
    S = os[0].shape[0]
    tm = min(512, S)
    c0 = (A_WIDTH + B_WIDTH) // PW

    def body(o0, o1, o2, l0, l1, l2, dy0_ref, dy1_ref, dy2_ref, bd_ref, do_ref, c_ref):
        bdv = bd_ref[...]
        o = [o0[...], o1[...], o2[...]]
        l = [l0[...], l1[...], l2[...]]
        dys = [dy0_ref[...], dy1_ref[...], dy2_ref[...]]
        m = jnp.maximum(jnp.maximum(l[0], l[1]), l[2])
        e = [jnp.exp(t - m) for t in l]
        inv = 1.0 / (e[0] + e[1] + e[2])
        alpha = [t * inv for t in e]
        da = [_seg_sum(dys[g] * o[g], bdv) for g in range(N_PATTERNS)]
        mean_da = alpha[0] * da[0] + alpha[1] * da[1] + alpha[2] * da[2]
        for g in range(N_PATTERNS):
            do_ref[g] = dys[g] * alpha[g]
            c_ref[g] = -alpha[g] * mean_da

    blk = pl.BlockSpec((tm, PW), lambda i: (i, 0))
    blk3 = pl.BlockSpec((N_PATTERNS, tm, PW), lambda i: (0, i, 0))
    dyspec = lambda g: pl.BlockSpec((tm, PW), lambda i: (i, c0 + g))
    return pl.pallas_call(
        body, name=name, grid=(S // tm,),
        in_specs=[blk] * 6 + [dyspec(0), dyspec(1), dyspec(2), pl.BlockSpec((PW, PW), lambda i: (0, 0))],
        out_specs=[blk3, blk3],
        out_shape=[SDS((N_PATTERNS, S, PW), F32)] * 2,
        compiler_params=_cp(1),
    )(*os, *lses, dycat, dycat, dycat, bd)


def _mesh_pos():
    x, y, c = lax.axis_index("x"), lax.axis_index("y"), lax.axis_index("c")
    chips = [(1 - x, y), (x, 1 - y), (1 - x, 1 - y)]
    chip_idx = [2 * cx + cy for cx, cy in chips]
    return x, y, c, 2 * x + y, chips, chip_idx


def _place_shard(name, w, layer, chip_arr, out_dtype, deps=()):
    _, R, C = w.shape
    tr = min(256, R)

    def body(chip_ref, w_ref, *rest):
        o_ref = rest[-1]
        o_ref[...] = w_ref[...].astype(o_ref.dtype)

    return pl.pallas_call(
        body, name=name,
        grid_spec=pltpu.PrefetchScalarGridSpec(
            num_scalar_prefetch=1, grid=(R // tr,),
            in_specs=[pl.BlockSpec((None, tr, C), lambda i, chip_ref: (layer, i, 0))] + [_hbm_spec()] * len(deps),
            out_specs=pl.BlockSpec((None, tr, C), lambda i, chip_ref: (chip_ref[0], i, 0))),
        out_shape=SDS((N_CHIPS, R, C), out_dtype),
        compiler_params=_cp(1),
    )(chip_arr, w, *deps)


HBM_SPEC = pl.BlockSpec(memory_space=pltpu.HBM)
SEM_SPEC = pl.BlockSpec(memory_space=pltpu.SEMAPHORE)
SPLIT_COPY = pltpu.SideEffectType.DATAFLOW_SIDE_EFFECTING
N_PEER_CHIPS = N_CHIPS - 1
TOKEN_SHAPE = SDS((8, 128), F32)
TOKEN_SPEC = pl.BlockSpec(memory_space=pltpu.VMEM)


def _in_hbm(a):
    return pltpu.with_memory_space_constraint(a, pltpu.HBM)


def _gather_start(name, bufs):
    T = len(bufs)

    def body(*refs):
        ins = refs[:T]
        send_sems, recv_sems = refs[T:2 * T], refs[2 * T:3 * T]
        token = refs[4 * T]
        x, y, c, me, chips, chip_idx = _mesh_pos()
        for t in range(T):
            hr = ins[t].shape[1] // 2
            mine = ins[t].at[me, pl.ds(c * hr, hr), :]
            for j in range(N_PEER_CHIPS):
                pltpu.make_async_remote_copy(src_ref=mine, dst_ref=mine, send_sem=send_sems[t].at[j],
                                             recv_sem=recv_sems[t].at[j], device_id=(*chips[j], c),
                                             device_id_type=MESH).start()
        token[...] = jnp.zeros_like(token)

    sems = [pltpu.SemaphoreType.DMA((N_PEER_CHIPS,))] * T
    out = pl.pallas_call(
        body, name=name,
        in_specs=[HBM_SPEC] * T,
        out_specs=[SEM_SPEC] * (2 * T) + [HBM_SPEC] * T + [TOKEN_SPEC],
        out_shape=sems + sems + [pltpu.HBM(b.shape, b.dtype) for b in bufs] + [TOKEN_SHAPE],
        input_output_aliases={t: 2 * T + t for t in range(T)},
        compiler_params=pltpu.CompilerParams(has_side_effects=SPLIT_COPY),
    )(*[_in_hbm(b) for b in bufs])
    return out[:T], out[T:2 * T], out[2 * T:3 * T], out[3 * T]


def _gather_wait(name, buf, send_sem, recv_sem, after):
    n_in = 3 if after is None else 4

    def body(*refs):
        buf_ref, ssem, rsem = refs[:3]
        x, y, c, me, chips, chip_idx = _mesh_pos()
        hr = buf_ref.shape[1] // 2
        mine = buf_ref.at[me, pl.ds(c * hr, hr), :]
        for j in range(N_PEER_CHIPS):
            got = buf_ref.at[chip_idx[j], pl.ds(c * hr, hr), :]
            cp = pltpu.make_async_remote_copy(src_ref=mine, dst_ref=got, send_sem=ssem.at[j], recv_sem=rsem.at[j],
                                              device_id=(*chips[j], c), device_id_type=MESH)
            cp.wait_send()
            cp.wait_recv()

    args = [buf, send_sem, recv_sem] + ([] if after is None else [after])
    return pl.pallas_call(
        body, name=name,
        in_specs=[HBM_SPEC, SEM_SPEC, SEM_SPEC] + [_hbm_spec()] * (n_in - 3),
        out_specs=HBM_SPEC,
        out_shape=pltpu.HBM(buf.shape, buf.dtype),
        input_output_aliases={0: 0},
        compiler_params=pltpu.CompilerParams(has_side_effects=SPLIT_COPY),
    )(*args)


def _forward_start(name, buf):
    def body(buf_ref, send_sems, recv_sems, buf_thru, token):
        x, y, c, me, chips, chip_idx = _mesh_pos()
        hr = buf_ref.shape[1] // 2
        for j in range(N_PEER_CHIPS):
            got = buf_ref.at[chip_idx[j], pl.ds(c * hr, hr), :]
            pltpu.make_async_remote_copy(src_ref=got, dst_ref=got, send_sem=send_sems.at[j], recv_sem=recv_sems.at[j],
                                         device_id=(x, y, 1 - c), device_id_type=MESH).start()
        token[...] = jnp.zeros_like(token)

    sems = pltpu.SemaphoreType.DMA((N_PEER_CHIPS,))
    return pl.pallas_call(
        body, name=name,
        in_specs=[HBM_SPEC],
        out_specs=[SEM_SPEC, SEM_SPEC, HBM_SPEC, TOKEN_SPEC],
        out_shape=[sems, sems, pltpu.HBM(buf.shape, buf.dtype), TOKEN_SHAPE],
        input_output_aliases={0: 2},
        compiler_params=pltpu.CompilerParams(has_side_effects=SPLIT_COPY),
    )(_in_hbm(buf))


def _forward_wait(name, buf, send_sems, recv_sems, after):
    n_in = 3 if after is None else 4

    def body(*refs):
        buf_ref, ssems, rsems = refs[:3]
        x, y, c, me, chips, chip_idx = _mesh_pos()
        hr = buf_ref.shape[1] // 2
        for j in range(N_PEER_CHIPS):
            sent = buf_ref.at[chip_idx[j], pl.ds(c * hr, hr), :]
            theirs = buf_ref.at[chip_idx[j], pl.ds((1 - c) * hr, hr), :]
            cp = pltpu.make_async_remote_copy(src_ref=sent, dst_ref=theirs, send_sem=ssems.at[j],
                                              recv_sem=rsems.at[j], device_id=(x, y, 1 - c), device_id_type=MESH)
            cp.wait_send()
            cp.wait_recv()

    args = [buf, send_sems, recv_sems] + ([] if after is None else [after])
    return pl.pallas_call(
        body, name=name,
        in_specs=[HBM_SPEC, SEM_SPEC, SEM_SPEC] + [_hbm_spec()] * (n_in - 3),
        out_specs=HBM_SPEC,
        out_shape=pltpu.HBM(buf.shape, buf.dtype),
        input_output_aliases={0: 0},
        compiler_params=pltpu.CompilerParams(has_side_effects=SPLIT_COPY),
    )(*args)


class _GatheredWeights:
    def __init__(self):
        self._order = []
        self._pending = {}
        self._forwarding = {}
        self._ready = {}
        self._tokens = []

    def start(self, keys, bufs):
        send_sems, recv_sems, thru, token = _gather_start(f"gather_start_{len(self._order)}", bufs)
        self._tokens.append(token)
        self._order.extend(keys)
        self._pending.update({k: (b, s, r) for k, b, s, r in zip(keys, thru, send_sems, recv_sems)})

    def _prefetch(self, key, after):
        if key in self._pending:
            buf, ssem, rsem = self._pending.pop(key)
            tag = f"{key[0]}_{key[1]}"
            buf = _gather_wait(f"gather_wait_{tag}", buf, ssem, rsem, after)
            ssems, rsems, buf, token = _forward_start(f"gather_fwd_start_{tag}", buf)
            self._forwarding[key] = (buf, ssems, rsems)
            self._tokens.append(token)

    def get(self, name, layer, after=None, prefetch_next=True):
        key = (name, layer)
        if key not in self._ready:
            self._prefetch(key, after)
            buf, ssems, rsems = self._forwarding.pop(key)
            self._ready[key] = _forward_wait(f"gather_fwd_wait_{name}_{layer}", buf, ssems, rsems, after)
            if prefetch_next:
                self.prefetch_after(name, layer, after)
        return self._ready[key]

    def prefetch_after(self, name, layer, after):
        nxt = self._order.index((name, layer)) + 1
        if nxt < len(self._order):
            self._prefetch(self._order[nxt], after)

    def deps(self):
        tokens, self._tokens = self._tokens, []
        return tokens


def _swap_copy(g_ref, land_ref, send_sem, recv_sem):
    x, y, c, _, _, _ = _mesh_pos()
    hr = g_ref.shape[1] // 2
    return pltpu.make_async_remote_copy(src_ref=g_ref.at[:, pl.ds((1 - c) * hr, hr), :], dst_ref=land_ref,
                                        send_sem=send_sem, recv_sem=recv_sem, device_id=(x, y, 1 - c),
                                        device_id_type=MESH)


def _swap_start(name, g):
    land_shape = (g.shape[0], g.shape[1] // 2, g.shape[2])

    def body(g_ref, land_ref, send_sem, recv_sem, land_thru, token):
        _swap_copy(g_ref, land_ref, send_sem, recv_sem).start()
        token[...] = jnp.zeros_like(token)

    return pl.pallas_call(
        body, name=name,
        in_specs=[HBM_SPEC, HBM_SPEC],
        out_specs=[SEM_SPEC, SEM_SPEC, HBM_SPEC, TOKEN_SPEC],
        out_shape=[pltpu.SemaphoreType.DMA(()), pltpu.SemaphoreType.DMA(()), pltpu.HBM(land_shape, g.dtype),
                   TOKEN_SHAPE],
        input_output_aliases={1: 2},
        compiler_params=pltpu.CompilerParams(has_side_effects=SPLIT_COPY),
    )(_in_hbm(g), _in_hbm(lax.empty(land_shape, g.dtype)))


def _swap_wait(name, g, land, send_sem, recv_sem, after):
    def body(g_ref, land_ref, send_sem, recv_sem, after_ref, land_out):
        cp = _swap_copy(g_ref, land_ref, send_sem, recv_sem)
        cp.wait_send()
        cp.wait_recv()

    return pl.pallas_call(
        body, name=name,
        in_specs=[HBM_SPEC, HBM_SPEC, SEM_SPEC, SEM_SPEC, _hbm_spec()],
        out_specs=HBM_SPEC,
        out_shape=pltpu.HBM(land.shape, land.dtype),
        input_output_aliases={1: 0},
        compiler_params=pltpu.CompilerParams(has_side_effects=SPLIT_COPY),
    )(_in_hbm(g), land, send_sem, recv_sem, after)


def _add_my_half(name, g, r, pos_arr):
    ns, R, C = g.shape
    hr = R // 2
    tr = min(256, hr)
    nt = hr // tr

    def body(pos_ref, g_ref, r_ref, o_ref, land_ref):
        t = (g_ref[...] + r_ref[...]).astype(o_ref.dtype)
        o_ref[...] = t

        @pl.when(pl.program_id(1) == pos_ref[1])
        def _():
            land_ref[...] = t

    blk = pl.BlockSpec((None, tr, C), lambda i, s, pos_ref: (s, i, 0))
    return pl.pallas_call(
        body, name=name,
        grid_spec=pltpu.PrefetchScalarGridSpec(
            num_scalar_prefetch=1, grid=(nt, ns),
            in_specs=[pl.BlockSpec((None, tr, C), lambda i, s, pos_ref: (s, pos_ref[0] * nt + i, 0)), blk],
            out_specs=[blk, pl.BlockSpec((None, tr, C), lambda i, s, pos_ref: (pos_ref[1], i, 0))]),
        out_shape=[SDS((ns, hr, C), BF16)] * 2,
        compiler_params=_cp(2),
    )(pos_arr, g, r)


def _exchange_start(name, part, land):
    def body(part_ref, land_ref, send_sems, recv_sems, land_thru, token):
        x, y, c, me, chips, chip_idx = _mesh_pos()
        for j in range(N_PEER_CHIPS):
            pltpu.make_async_remote_copy(src_ref=part_ref.at[chip_idx[j]], dst_ref=land_ref.at[me],
                                         send_sem=send_sems.at[j], recv_sem=recv_sems.at[j],
                                         device_id=(*chips[j], c), device_id_type=MESH).start()
        token[...] = jnp.zeros_like(token)

    sems = pltpu.SemaphoreType.DMA((N_PEER_CHIPS,))
    return pl.pallas_call(
        body, name=name,
        in_specs=[HBM_SPEC, HBM_SPEC],
        out_specs=[SEM_SPEC, SEM_SPEC, HBM_SPEC, TOKEN_SPEC],
        out_shape=[sems, sems, pltpu.HBM(land.shape, land.dtype), TOKEN_SHAPE],
        input_output_aliases={1: 2},
        compiler_params=pltpu.CompilerParams(has_side_effects=SPLIT_COPY),
    )(_in_hbm(part), _in_hbm(land))


def _exchange_wait(name, part, land, send_sems, recv_sems, after):
    def body(part_ref, land_ref, send_sems, recv_sems, after_ref, land_out):
        x, y, c, me, chips, chip_idx = _mesh_pos()
        for j in range(N_PEER_CHIPS):
            cp = pltpu.make_async_remote_copy(src_ref=part_ref.at[chip_idx[j]], dst_ref=land_ref.at[chip_idx[j]],
                                              send_sem=send_sems.at[j], recv_sem=recv_sems.at[j],
                                              device_id=(*chips[j], c), device_id_type=MESH)
            cp.wait_send()
            cp.wait_recv()

    return pl.pallas_call(
        body, name=name,
        in_specs=[HBM_SPEC, HBM_SPEC, SEM_SPEC, SEM_SPEC, _hbm_spec()],
        out_specs=HBM_SPEC,
        out_shape=pltpu.HBM(land.shape, land.dtype),
        input_output_aliases={1: 0},
        compiler_params=pltpu.CompilerParams(has_side_effects=SPLIT_COPY),
    )(_in_hbm(part), land, send_sems, recv_sems, after)


class _GradReducer:
    def __init__(self, c_arr):
        self._c_arr = c_arr
        self._swapping = []
        self._exchanging = {}
        self._joining = {}
        self._tokens = []

    def begin(self, name, layer, g):
        tag = f"{name}_{layer}"
        ssem, rsem, land, token = _swap_start(f"rs_swap_start_{tag}", g)
        self._swapping.append((name, layer, g, ssem, rsem, land))
        self._tokens.append(token)

    def advance(self, after):
        for name, layer, g, ssem, rsem, land in self._swapping:
            tag = f"{name}_{layer}"
            theirs = _swap_wait(f"rs_swap_wait_{tag}", g, land, ssem, rsem, after)
            part, own = _add_my_half(f"rs_add_{tag}", g, theirs, self._c_arr)
            ssems, rsems, land2, token = _exchange_start(f"rs_xchg_start_{tag}", part, own)
            self._exchanging[(name, layer)] = (part, ssems, rsems, land2)
            self._tokens.append(token)
        self._swapping = []

    def deps(self):
        tokens, self._tokens = self._tokens, []
        return tokens

    def reduce(self, name, n_layers, after):
        buf = None
        for layer in range(n_layers):
            part, ssems, rsems, land = self._exchanging.pop((name, layer))
            tag = f"{name}_{layer}"
            landed = _exchange_wait(f"rs_xchg_wait_{tag}", part, land, ssems, rsems, after)
            buf = _sum_chips(f"rs_sum_{tag}", landed, self._c_arr, layer, n_layers, buf)
        ssem, rsem, buf, token = _join_start(f"rs_join_start_{name}", buf)
        self._joining[name] = (buf, ssem, rsem)
        return token

    def reduced(self, name, after):
        buf, ssem, rsem = self._joining.pop(name)
        return _join_wait(f"rs_join_wait_{name}", buf, ssem, rsem, after)


def _sum_chips(name, r, c_arr, layer, n_layers, prev):
    ns, H, C = r.shape
    tr = min(256, H)
    nt = H // tr

    def body(c_ref, r_ref, *rest):
        o_ref = rest[-1]
        o_ref[...] = ((r_ref[0].astype(F32) + r_ref[1].astype(F32)) + r_ref[2].astype(F32)) + r_ref[3].astype(F32)

    in_specs = [pl.BlockSpec((ns, tr, C), lambda i, c_ref: (0, i, 0))]
    args = [c_arr, r]
    aliases = {}
    if prev is not None:
        in_specs.append(_hbm_spec())
        args.append(prev)
        aliases = {2: 0}
    return pl.pallas_call(
        body, name=name,
        grid_spec=pltpu.PrefetchScalarGridSpec(
            num_scalar_prefetch=1, grid=(nt,), in_specs=in_specs,
            out_specs=pl.BlockSpec((None, tr, C), lambda i, c_ref: (layer, c_ref[0] * nt + i, 0))),
        out_shape=SDS((n_layers, 2 * H, C), F32),
        input_output_aliases=aliases,
        compiler_params=_cp(1),
    )(*args)


def _join_copy(buf_ref, send_sem, recv_sem):
    x, y, c, _, _, _ = _mesh_pos()
    hr = buf_ref.shape[1] // 2
    mine = buf_ref.at[:, pl.ds(c * hr, hr), :]
    theirs = buf_ref.at[:, pl.ds((1 - c) * hr, hr), :]
    send = pltpu.make_async_remote_copy(src_ref=mine, dst_ref=mine, send_sem=send_sem, recv_sem=recv_sem,
                                        device_id=(x, y, 1 - c), device_id_type=MESH)
    arrive = pltpu.make_async_remote_copy(src_ref=theirs, dst_ref=theirs, send_sem=send_sem, recv_sem=recv_sem,
                                          device_id=(x, y, 1 - c), device_id_type=MESH)
    return send, arrive


def _join_start(name, buf):
    def body(buf_ref, send_sem, recv_sem, buf_thru, token):
        _join_copy(buf_ref, send_sem, recv_sem)[0].start()
        token[...] = jnp.zeros_like(token)

    return pl.pallas_call(
        body, name=name,
        in_specs=[HBM_SPEC],
        out_specs=[SEM_SPEC, SEM_SPEC, HBM_SPEC, TOKEN_SPEC],
        out_shape=[pltpu.SemaphoreType.DMA(()), pltpu.SemaphoreType.DMA(()), pltpu.HBM(buf.shape, buf.dtype),
                   TOKEN_SHAPE],
        input_output_aliases={0: 2},
        compiler_params=pltpu.CompilerParams(has_side_effects=SPLIT_COPY),
    )(_in_hbm(buf))


def _join_wait(name, buf, send_sem, recv_sem, after):
    def body(buf_ref, send_sem, recv_sem, after_ref, buf_out):
        send, arrive = _join_copy(buf_ref, send_sem, recv_sem)
        send.wait_send()
        arrive.wait_recv()

    return pl.pallas_call(
        body, name=name,
        in_specs=[HBM_SPEC, SEM_SPEC, SEM_SPEC, _hbm_spec()],
        out_specs=HBM_SPEC,
        out_shape=pltpu.HBM(buf.shape, buf.dtype),
        input_output_aliases={0: 0},
        compiler_params=pltpu.CompilerParams(has_side_effects=SPLIT_COPY),
    )(buf, send_sem, recv_sem, after)


def _small_copy(k, buf_ref, land_ref, send_sems, recv_sems):
    x, y, c = lax.axis_index("x"), lax.axis_index("y"), lax.axis_index("c")
    me = 4 * x + 2 * y + c
    peer = (x ^ ((k >> 2) & 1), y ^ ((k >> 1) & 1), c ^ (k & 1))
    cp = pltpu.make_async_remote_copy(src_ref=buf_ref, dst_ref=land_ref.at[me], send_sem=send_sems.at[k - 1],
                                      recv_sem=recv_sems.at[k - 1], device_id=peer, device_id_type=MESH)
    return me, peer, cp


def _small_start(buf, deps):
    land = jnp.broadcast_to(buf[None], (N_DEV,) + buf.shape)
    n_dep = len(deps)

    def body(buf_ref, land_ref, *rest):
        send_sems, recv_sems, _, token = rest[n_dep:]
        for k in range(1, N_DEV):
            _small_copy(k, buf_ref, land_ref, send_sems, recv_sems)[2].start()
        token[...] = jnp.zeros_like(token)

    sems = pltpu.SemaphoreType.DMA((N_DEV - 1,))
    return pl.pallas_call(
        body, name="small_gather_start",
        in_specs=[HBM_SPEC, HBM_SPEC] + [_hbm_spec()] * n_dep,
        out_specs=[SEM_SPEC, SEM_SPEC, HBM_SPEC, TOKEN_SPEC],
        out_shape=[sems, sems, pltpu.HBM(land.shape, land.dtype), TOKEN_SHAPE],
        input_output_aliases={1: 2},
        compiler_params=pltpu.CompilerParams(has_side_effects=SPLIT_COPY),
    )(_in_hbm(buf), _in_hbm(land), *deps)


def _small_wait(buf, land, send_sems, recv_sems, after):
    def body(buf_ref, land_ref, send_sems, recv_sems, after_ref, land_out):
        for k in range(1, N_DEV):
            me, peer, cp = _small_copy(k, buf_ref, land_ref, send_sems, recv_sems)
            cp.wait_send()
            got = land_ref.at[me ^ k]
            pltpu.make_async_remote_copy(src_ref=got, dst_ref=got, send_sem=send_sems.at[k - 1],
                                         recv_sem=recv_sems.at[k - 1], device_id=peer,
                                         device_id_type=MESH).wait_recv()

    return pl.pallas_call(
        body, name="small_gather_wait",
        in_specs=[HBM_SPEC, HBM_SPEC, SEM_SPEC, SEM_SPEC, _hbm_spec()],
        out_specs=HBM_SPEC,
        out_shape=pltpu.HBM(land.shape, land.dtype),
        input_output_aliases={1: 0},
        compiler_params=pltpu.CompilerParams(has_side_effects=SPLIT_COPY),
    )(_in_hbm(buf), land, send_sems, recv_sems, after)


def _sum_devices(land):
    n, R, C = land.shape

    def body(land_ref, out_ref):
        acc = land_ref[0]
        for d in range(1, n):
            acc = acc + land_ref[d]
        out_ref[...] = acc

    return pl.pallas_call(
        body, name="small_sum",
        in_specs=[pl.BlockSpec(memory_space=pltpu.VMEM)],
        out_specs=pl.BlockSpec(memory_space=pltpu.VMEM),
        out_shape=SDS((R, C), land.dtype),
        compiler_params=pltpu.CompilerParams(vmem_limit_bytes=V7X_VMEM_LIMIT),
    )(land)


def _deinterleave(t, d):
    if d == 1:
        return t
    S, W = t.shape
    return t.reshape(S // d, d, W).transpose(1, 0, 2).reshape(S, W)


def _interleave(t, d):
    if d == 1:
        return t
    S, W = t.shape
    return t.reshape(d, S // d, W).transpose(1, 0, 2).reshape(S, W)


def _to_patterns(t, off):
    return jnp.stack([_deinterleave(t[:, off + PW * g:off + PW * (g + 1)], PATTERN_DILATION[g])
                      for g in range(N_PATTERNS)])


def _from_patterns(t3):
    return jnp.stack([_interleave(t3[g], PATTERN_DILATION[g]) for g in range(N_PATTERNS)])


def _pack_rows(vectors):
    flat = jnp.concatenate([v.reshape(-1) for v in vectors])
    n = flat.shape[0]
    padded = -(-n // 1024) * 1024
    return jnp.pad(flat, (0, padded - n)).reshape(padded // 128, 128)


def _unpack_rows(buf, shapes):
    flat = buf.reshape(-1)
    out, off = [], 0
    for s in shapes:
        n = 1
        for dim in s:
            n *= dim
        out.append(flat[off:off + n].reshape(s))
        off += n
    return out


def _layer_forward(l, x, prm, wg):
    S, D = x.shape
    w_in = wg.get("w_in", l, x, prefetch_next=l > 0)
    p, h = _norm_matmul(f"in_proj_{l}", x, prm["attn_norm"][l], w_in, F32, deps=wg.deps())
    if l == 0:
        wg.prefetch_after("w_in", l, p)
    y_a = _sgu_fwd(f"sgu_fwd_{l}", p, prm["sgu_wt"][l], prm["sgu_bb"][l])
    y_b = _conv_fwd(f"conv_fwd_{l}", p, prm["conv_w"][l])
    os, lses = [], []
    for g in range(N_PATTERNS):
        o_g, lse_g = _attn_fwd(f"attn_fwd_{l}_{g}", p, g, prm["q_gain"][l], prm["k_gain"][l], prm["bd"])
        os.append(o_g)
        lses.append(lse_g)
    y_c = _mix_fwd(f"mix_fwd_{l}", os, lses)
    ycat = jnp.concatenate([y_a, y_b, y_c], axis=1)
    tmb, tnb = min(1024, S), min(1024, D)
    w_out = wg.get("w_out", l, ycat)
    kq = N_CHIPS * w_out.shape[1]
    x1 = _matmul(
        f"out_proj_{l}", ycat, w_out.reshape(kq, D), (S, D), F32, grid=(S // tmb, D // tnb, 1),
        a_spec=pl.BlockSpec((tmb, kq), lambda i, j, k: (i, 0)),
        b_spec=pl.BlockSpec((kq, tnb), lambda i, j, k: (0, j)),
        o_spec=pl.BlockSpec((tmb, tnb), lambda i, j, k: (i, j)),
        contract=(1, 0), acc_shape=(tmb, tnb),
        extras=(x,), extra_specs=(pl.BlockSpec((tmb, tnb), lambda i, j, k: (i, j)),),
        epi=lambda r, res: r + res, deps=wg.deps())
    w_mlp_in = wg.get("w_mlp_in", l, x1)
    a, h2 = _norm_matmul(f"mlp_in_{l}", x1, prm["mlp_norm"][l], w_mlp_in, BF16, deps=wg.deps())
    w_mlp_out = wg.get("w_mlp_out", l, a)
    dff4 = w_mlp_out.shape[1]
    tk = min(2048, dff4)
    kpc = dff4 // tk
    x2 = _matmul(
        f"mlp_out_{l}", a, w_mlp_out, (S, D), F32, grid=(S // tmb, D // tnb, N_CHIPS * kpc),
        a_spec=pl.BlockSpec((tmb, tk), lambda i, j, k: (i, k)),
        b_spec=pl.BlockSpec((None, tk, tnb), lambda i, j, k: (k // kpc, k % kpc, j)),
        o_spec=pl.BlockSpec((tmb, tnb), lambda i, j, k: (i, j)),
        contract=(1, 0), acc_shape=(tmb, tnb), a_pre=_relu2_bf16,
        extras=(x1,), extra_specs=(pl.BlockSpec((tmb, tnb), lambda i, j, k: (i, j)),),
        epi=lambda r, res: r + res, deps=wg.deps())
    saved = dict(x=x, p=p, h=h, os=os, lses=lses, ycat=ycat, x1=x1, a=a, h2=h2)
    return x2, saved


def _layer_backward(l, dx2, dx2b, sv, prm, wg, sink):
    S, D = dx2.shape
    w_in, w_out = wg.get("w_in", l), wg.get("w_out", l)
    w_mlp_in, w_mlp_out = wg.get("w_mlp_in", l), wg.get("w_mlp_out", l)
    dff4 = w_mlp_in.shape[-1]
    dff = N_CHIPS * dff4
    tm = min(512, S)
    tk = min(1024, S)
    nks = S // tk

    tmb, tnb = min(1024, S), min(1024, D)
    da = _matmul(
        f"mlp_out_bwd_{l}", dx2b, w_mlp_out, (S, dff), BF16, grid=(S // tmb, N_CHIPS, 1),
        a_spec=pl.BlockSpec((tmb, D), lambda i, j, k: (i, 0)),
        b_spec=pl.BlockSpec((None, dff4, D), lambda i, j, k: (j, 0, 0)),
        o_spec=pl.BlockSpec((tmb, dff4), lambda i, j, k: (i, j)),
        contract=(1, 1), acc_shape=(tmb, dff4),
        extras=(sv["a"],), extra_specs=(pl.BlockSpec((tmb, dff4), lambda i, j, k: (i, j)),),
        epi=lambda r, act: r * (2.0 * jnp.maximum(act.astype(F32), 0.0)), deps=sink.deps())
    tmw = min(512, dff4)
    mpc = dff4 // tmw
    g_w2 = _matmul(
        f"mlp_out_dw_{l}", sv["a"], dx2b, (N_CHIPS, dff4, D), F32, grid=(N_CHIPS * mpc, D // tnb, 1),
        a_spec=pl.BlockSpec((S, tmw), lambda i, j, k: (0, i)),
        b_spec=pl.BlockSpec((S, tnb), lambda i, j, k: (0, j)),
        o_spec=pl.BlockSpec((None, tmw, tnb), lambda i, j, k: (i // mpc, i % mpc, j)),
        contract=(0, 0), acc_shape=(tmw, tnb), a_pre=_relu2_bf16)
    sink.begin("w_mlp_out", l, g_w2)
    dh2 = _matmul(
        f"mlp_in_bwd_{l}", da, w_mlp_in, (S, D), F32, grid=(S // tmb, D // tnb, N_CHIPS),
        a_spec=pl.BlockSpec((tmb, dff4), lambda i, j, k: (i, k)),
        b_spec=pl.BlockSpec((None, tnb, dff4), lambda i, j, k: (k, j, 0)),
        o_spec=pl.BlockSpec((tmb, tnb), lambda i, j, k: (i, j)),
        contract=(1, 1), acc_shape=(tmb, tnb), deps=sink.deps())
    sink.advance(dh2)
    tmd = min(1024, D)
    nd = D // tmd
    tnf = min(1024, dff4)
    nf = dff4 // tnf
    g_w1 = _matmul(
        f"mlp_in_dw_{l}", sv["h2"], da, (N_CHIPS, D, dff4), F32, grid=(N_CHIPS * nd, nf, 1),
        a_spec=pl.BlockSpec((S, tmd), lambda i, j, k: (0, i % nd)),
        b_spec=pl.BlockSpec((S, tnf), lambda i, j, k: (0, (i // nd) * nf + j)),
        o_spec=pl.BlockSpec((None, tmd, tnf), lambda i, j, k: (i // nd, i % nd, j)),
        contract=(0, 0), acc_shape=(tmd, tnf))
    sink.begin("w_mlp_in", l, g_w1)
    dx1, dx1b, g_mlp_norm = _rmsnorm_bwd(f"mlp_norm_bwd_{l}", dh2, sv["x1"], prm["mlp_norm"][l], dx2,
                                         deps=sink.deps())

    rq = w_out.shape[1]
    dycat = _matmul(
        f"out_proj_bwd_{l}", dx1b, w_out, (S, N_CHIPS * rq), F32, grid=(S // tmb, N_CHIPS, 1),
        a_spec=pl.BlockSpec((tmb, D), lambda i, j, k: (i, 0)),
        b_spec=pl.BlockSpec((None, rq, D), lambda i, j, k: (j, 0, 0)),
        o_spec=pl.BlockSpec((tmb, rq), lambda i, j, k: (i, j)),
        contract=(1, 1), acc_shape=(tmb, rq))
    sink.advance(dycat)
    g_wout = _matmul(
        f"out_proj_dw_{l}", sv["ycat"], dx1b, (N_CHIPS, rq, D), F32, grid=(N_CHIPS, D // tnb, 1),
        a_spec=pl.BlockSpec((S, rq), lambda i, j, k: (0, i)),
        b_spec=pl.BlockSpec((S, tnb), lambda i, j, k: (0, j)),
        o_spec=pl.BlockSpec((None, rq, tnb), lambda i, j, k: (i, 0, j)),
        contract=(0, 0), acc_shape=(rq, tnb))
    sink.begin("w_out", l, g_wout)

    p = sv["p"]
    du, dv_a, g_sgu_w, db_lanes = _sgu_bwd(f"sgu_bwd_{l}", p, dycat, prm["sgu_wt"][l], prm["sgu_wtt"][l],
                                           prm["sgu_bb"][l])
    g_sgu_b = db_lanes[:, :A_HEADS].T
    db, dc, dxb, g_conv = _conv_bwd(f"conv_bwd_{l}", p, dycat, prm["conv_w"][l])
    do3, c3 = _mix_bwd(f"mix_bwd_{l}", sv["os"], sv["lses"], dycat, prm["bd"])
    dqs, dks, dvs, dgqs, dgks = [], [], [], [], []
    for g in range(N_PATTERNS):
        dq, dk, dv, dgq, dgk = _attn_bwd(f"attn_bwd_{l}_{g}", p, g, sv["lses"][g], do3, c3,
                                         prm["q_gain"][l], prm["k_gain"][l], prm["bd"])
        dqs.append(dq)
        dks.append(dk)
        dvs.append(dv)
        dgqs.append(dgq)
        dgks.append(dgk)
    g_q = jnp.concatenate(dgqs, axis=1).reshape(N_PATTERNS * PW // HEAD_DIM, HEAD_DIM).sum(axis=0)
    g_k = jnp.concatenate(dgks, axis=1).reshape(N_PATTERNS * PW // HEAD_DIM, HEAD_DIM).sum(axis=0)
    dp = jnp.concatenate([du, dv_a, db, dc, dxb] + [t.astype(BF16) for t in dqs + dks + dvs], axis=1)

    ns_in = w_in.shape[-1]
    tmh = min(512, D)
    nh = D // tmh
    g_win = _matmul(
        f"in_proj_dw_{l}", sv["h"], dp, (N_CHIPS, D, ns_in), F32, grid=(N_CHIPS * nh, 1, 1),
        a_spec=pl.BlockSpec((S, tmh), lambda i, j, k: (0, i % nh)),
        b_spec=pl.BlockSpec((S, ns_in), lambda i, j, k: (0, i // nh)),
        o_spec=pl.BlockSpec((None, tmh, ns_in), lambda i, j, k: (i // nh, i % nh, 0)),
        contract=(0, 0), acc_shape=(tmh, ns_in))
    sink.begin("w_in", l, g_win)
    dh = _matmul(
        f"in_proj_bwd_{l}", dp, w_in, (S, D), F32, grid=(S // tmb, D // tnb, N_CHIPS),
        a_spec=pl.BlockSpec((tmb, ns_in), lambda i, j, k: (i, k)),
        b_spec=pl.BlockSpec((None, tnb, ns_in), lambda i, j, k: (k, j, 0)),
        o_spec=pl.BlockSpec((tmb, tnb), lambda i, j, k: (i, j)),
        contract=(1, 1), acc_shape=(tmb, tnb), deps=sink.deps())
    sink.advance(dh)
    dx0, dx0b, g_attn_norm = _rmsnorm_bwd(f"attn_norm_bwd_{l}", dh, sv["x"], prm["attn_norm"][l], dx1,
                                          deps=sink.deps())

    big = dict(w_in=g_win, w_out=g_wout, w_mlp_in=g_w1, w_mlp_out=g_w2)
    small = dict(attn_norm=g_attn_norm.reshape(-1), sgu_w=g_sgu_w, sgu_b=g_sgu_b, conv_w=g_conv,
                 q_norm=g_q, k_norm=g_k, mlp_norm=g_mlp_norm.reshape(-1))
    return dx0, dx0b, big, small


BIG = ("w_in", "w_out", "w_mlp_in", "w_mlp_out")
SMALL_REPLICATED = ("attn_norm", "sgu_w", "sgu_b", "q_norm", "k_norm", "mlp_norm")


def _local_step(x, target, prm, wg, n_layers, sink):
    saved = []
    h = x
    for l in range(n_layers):
        h, sv = _layer_forward(l, h, prm, wg)
        saved.append(sv)
    dy, dyb, colsq = _loss_kernel(h, target)
    loss = 0.5 * jnp.sum(colsq) / x.shape[1]
    bigs, smalls = [None] * n_layers, [None] * n_layers
    for l in reversed(range(n_layers)):
        dy, dyb, bigs[l], smalls[l] = _layer_backward(l, dy, dyb, saved[l], prm, wg, sink)
    return loss, dy, bigs, smalls


def _prepare_params(attn_norm, sgu_w, sgu_b, conv_full, q_norm, k_norm, mlp_norm):
    n_layers = attn_norm.shape[0]
    tri = jnp.tril(sgu_w)
    idx = jnp.arange(PW)
    bd = (idx[:, None] // HEAD_DIM == idx[None, :] // HEAD_DIM).astype(BF16)
    return dict(
        attn_norm=[attn_norm[l][None, :] for l in range(n_layers)],
        mlp_norm=[mlp_norm[l][None, :] for l in range(n_layers)],
        sgu_wt=[tri[l].astype(BF16) for l in range(n_layers)],
        sgu_wtt=[tri[l].transpose(0, 2, 1).astype(BF16) for l in range(n_layers)],
        sgu_bb=[jnp.repeat(sgu_b[l].T, HEAD_DIM, axis=1) for l in range(n_layers)],
        conv_w=[conv_full[l] for l in range(n_layers)],
        q_gain=[jnp.tile(q_norm[l], PW // HEAD_DIM)[None, :] for l in range(n_layers)],
        k_gain=[jnp.tile(k_norm[l], PW // HEAD_DIM)[None, :] for l in range(n_layers)],
        bd=bd,
    )


def kernel(x, attn_norm, w_in, sgu_w, sgu_b, conv_w, q_norm, k_norm, w_out, mlp_norm, w_mlp_in, w_mlp_out, loss_target, m_attn_norm, m_w_in, m_sgu_w, m_sgu_b, m_conv_w, m_q_norm, m_k_norm, m_w_out, m_mlp_norm, m_w_mlp_in, m_w_mlp_out, v_attn_norm, v_w_in, v_sgu_w, v_sgu_b, v_conv_w, v_q_norm, v_k_norm, v_w_out, v_mlp_norm, v_w_mlp_in, v_w_mlp_out):
    n_layers = attn_norm.shape[0]
    weights = dict(attn_norm=attn_norm, w_in=w_in, sgu_w=sgu_w, sgu_b=sgu_b, conv_w=conv_w, q_norm=q_norm,
                   k_norm=k_norm, w_out=w_out, mlp_norm=mlp_norm, w_mlp_in=w_mlp_in, w_mlp_out=w_mlp_out)
    mom_m = dict(attn_norm=m_attn_norm, w_in=m_w_in, sgu_w=m_sgu_w, sgu_b=m_sgu_b, conv_w=m_conv_w,
                 q_norm=m_q_norm, k_norm=m_k_norm, w_out=m_w_out, mlp_norm=m_mlp_norm, w_mlp_in=m_w_mlp_in,
                 w_mlp_out=m_w_mlp_out)
    mom_v = dict(attn_norm=v_attn_norm, w_in=v_w_in, sgu_w=v_sgu_w, sgu_b=v_sgu_b, conv_w=v_conv_w,
                 q_norm=v_q_norm, k_norm=v_k_norm, w_out=v_w_out, mlp_norm=v_mlp_norm, w_mlp_in=v_w_mlp_in,
                 w_mlp_out=v_w_mlp_out)
    order = ("attn_norm", "w_in", "sgu_w", "sgu_b", "conv_w", "q_norm", "k_norm", "w_out", "mlp_norm",
             "w_mlp_in", "w_mlp_out")
    chip = 2 * lax.axis_index("x") + lax.axis_index("y")
    c_arr = jnp.stack([lax.axis_index("c"), chip]).astype(jnp.int32)

    conv_cols = conv_w.shape[-1]
    chip_arr = chip.astype(jnp.int32).reshape(1)
    conv_pack = jnp.pad(conv_w.reshape(-1), (0, 2048 - conv_w.size)).reshape(1, 16, 128)
    wg = _GatheredWeights()
    wg.start([("conv_w", 0), ("w_in", 0)],
             [_place_shard("place_conv_w", conv_pack, 0, chip_arr, F32),
              _place_shard("place_w_in_0", weights["w_in"], 0, chip_arr, BF16)])
    keys = [(n, l) for l in range(n_layers) for n in BIG if (n, l) != ("w_in", 0)]
    first = wg.deps()
    wg.start(keys, [_place_shard(f"place_{n}_{l}", weights[n], l, chip_arr, BF16, deps=first) for n, l in keys])
    conv_full = wg.get("conv_w", 0, wg.deps()[-1]).reshape(N_CHIPS, 2048)[:, :conv_w.size].reshape(N_CHIPS, n_layers, 3, conv_cols)
    conv_full = conv_full.transpose(1, 2, 0, 3).reshape(n_layers, 3, N_CHIPS * conv_cols)
    prm = _prepare_params(attn_norm, sgu_w, sgu_b, conv_full, q_norm, k_norm, mlp_norm)

    sink = _GradReducer(c_arr)
    loss_local, grad_x, _, smalls = _local_step(x[0], loss_target[0], prm, wg, n_layers, sink)
    loss = lax.psum(loss_local, ("x", "y", "c"))

    small_names = SMALL_REPLICATED + ("conv_w",)
    small_shapes = [(n_layers,) + tuple(smalls[0][n].shape) for n in small_names]
    packed = _pack_rows([jnp.stack([smalls[l][n] for l in range(n_layers)]) for n in small_names])
    small_send, small_recv, small_land, small_token = _small_start(packed, sink.deps())

    grads, delta, new_m, new_v = {}, {}, {}, {}

    def update(n, after):
        shp = weights[n].shape
        two_d = (shp[0] * shp[1], shp[2])
        d, nm, nv, g = _adamw(f"adamw_{n}", weights[n].reshape(two_d), sink.reduced(n, after).reshape(two_d),
                              mom_m[n].reshape(two_d), mom_v[n].reshape(two_d))
        grads[n], delta[n], new_m[n], new_v[n] = g.reshape(shp), d.reshape(shp), nm.reshape(shp), nv.reshape(shp)

    token = small_token
    for n in ("w_mlp_out", "w_mlp_in", "w_out"):
        token = sink.reduce(n, n_layers, token)
    update("w_mlp_out", token)
    token = sink.reduce("w_in", n_layers, delta["w_mlp_out"])
    update("w_mlp_in", token)
    update("w_out", delta["w_mlp_in"])
    update("w_in", delta["w_out"])
    small_land = _small_wait(packed, small_land, small_send, small_recv, delta["w_in"])
    grads.update(zip(small_names, _unpack_rows(_sum_devices(small_land), small_shapes)))
    grads["conv_w"] = lax.dynamic_slice_in_dim(grads["conv_w"], chip * conv_cols, conv_cols, axis=2)
    smalls_all = SMALL_REPLICATED + ("conv_w",)
    shapes = [weights[n].shape for n in smalls_all]
    d, nm, nv, _ = _adamw("adamw_small",
                          _pack_rows([weights[n] for n in smalls_all]), _pack_rows([grads[n] for n in smalls_all]),
                          _pack_rows([mom_m[n] for n in smalls_all]), _pack_rows([mom_v[n] for n in smalls_all]))
    for n, dd, mm, vv in zip(smalls_all, _unpack_rows(d, shapes), _unpack_rows(nm, shapes), _unpack_rows(nv, shapes)):
        delta[n], new_m[n], new_v[n] = dd, mm, vv

    return (loss, grad_x[None], *[grads[n] for n in order], *[delta[n] for n in order],
            *[new_m[n] for n in order], *[new_v[n] for n in order])
```

```python
import jax
import jax.numpy as jnp
from jax import lax
from jax.experimental import pallas as pl
from jax.experimental.pallas import tpu as pltpu

F32 = jnp.float32
BF16 = jnp.bfloat16
SDS = jax.ShapeDtypeStruct

EPS = 1e-6
HEAD_DIM = 64
A_HEADS = 8
A_WIDTH = 512
CHUNK = 128
B_WIDTH = 768
C_WIDTH = 768
N_PATTERNS = 3
PATTERN_DILATION = (1, 4, 16)
PW = 256
D_IN_PROJ = 5632
OFF_AU, OFF_AV, OFF_BB, OFF_BC, OFF_BX, OFF_Q, OFF_K, OFF_V = 0, 512, 1024, 1792, 2560, 3328, 4096, 4864
N_CHIPS = 4
N_DEV = 8
BLK = 128

ADAM_LR, ADAM_B1, ADAM_B2, ADAM_EPS, ADAM_WD, ADAM_STEP = 0.001, 0.9, 0.999, 1e-08, 0.01, 10

V7X_VMEM_LIMIT = 56 * 1024 * 1024
MESH = pl.DeviceIdType.MESH
NEG = -1e30


def _cp(n_axes):
    return pltpu.CompilerParams(dimension_semantics=("arbitrary",) * n_axes, vmem_limit_bytes=V7X_VMEM_LIMIT)


def _hbm_spec():
    return pl.BlockSpec(memory_space=pl.ANY)


def _norm_matmul(name, x, g, wg, out_dtype, deps=(), relu2_out=False):
    S, D = x.shape
    ns, _, Ns = wg.shape
    tm = min(512, S)
    n_dep = len(deps)
    n_out = 3 if relu2_out else 2

    def body(x_ref, g_ref, w_ref, *rest):
        outs = rest[n_dep:n_dep + n_out]
        o_ref, h_ref = outs[0], outs[1]
        hs_ref = rest[n_dep + n_out]

        @pl.when(pl.program_id(1) == 0)
        def _():
            xv = x_ref[...]
            y = xv * lax.rsqrt(jnp.mean(xv * xv, axis=-1, keepdims=True) + EPS) * g_ref[...]
            hb = y.astype(BF16)
            hs_ref[...] = hb
            h_ref[...] = hb
        acc = jnp.dot(hs_ref[...], w_ref[...], preferred_element_type=F32)
        o_ref[...] = acc.astype(o_ref.dtype)
        if relu2_out:
            r = jnp.maximum(acc, 0.0)
            outs[2][...] = (r * r).astype(BF16)

    tile = pl.BlockSpec((tm, Ns), lambda i, s: (i, s))
    return pl.pallas_call(
        body, name=name, grid=(S // tm, ns),
        in_specs=[pl.BlockSpec((tm, D), lambda i, s: (i, 0)),
                  pl.BlockSpec((1, D), lambda i, s: (0, 0)),
                  pl.BlockSpec((None, D, Ns), lambda i, s: (s, 0, 0))] + [_hbm_spec()] * n_dep,
        out_specs=[tile, pl.BlockSpec((tm, D), lambda i, s: (i, 0))] + [tile] * (n_out - 2),
        out_shape=[SDS((S, ns * Ns), out_dtype), SDS((S, D), BF16)] + [SDS((S, ns * Ns), BF16)] * (n_out - 2),
        scratch_shapes=[pltpu.VMEM((tm, D), BF16)],
        compiler_params=_cp(2),
    )(x, g, wg, *deps)


def _matmul(name, a, b, out_shape, out_dtype, *, grid, a_spec, b_spec, o_spec, contract, acc_shape,
            extras=(), extra_specs=(), a_pre=None, epi=None, deps=()):
    nk = grid[2]
    n_ex = len(extras)
    n_dep = len(deps)
    dims = (((contract[0],), (contract[1],)), ((), ()))

    def product(a_ref, b_ref):
        av = a_ref[...]
        if a_pre is not None:
            av = a_pre(av)
        return lax.dot_general(av, b_ref[...], dims, preferred_element_type=F32)

    def finish(r, ex, o_ref):
        if epi is not None:
            r = epi(r, *[e[...] for e in ex])
        o_ref[...] = r.astype(o_ref.dtype)

    def body_single(a_ref, b_ref, *rest):
        finish(product(a_ref, b_ref), rest[:n_ex], rest[n_ex + n_dep])

    def body(a_ref, b_ref, *rest):
        ex = rest[:n_ex]
        o_ref = rest[n_ex + n_dep]
        acc_ref = rest[n_ex + n_dep + 1]
        k = pl.program_id(2)

        @pl.when(k == 0)
        def _():
            acc_ref[...] = product(a_ref, b_ref)

        @pl.when((k > 0) & (k < nk - 1))
        def _():
            acc_ref[...] += product(a_ref, b_ref)

        @pl.when(k == nk - 1)
        def _():
            finish(acc_ref[...] + product(a_ref, b_ref), ex, o_ref)

    return pl.pallas_call(
        body_single if nk == 1 else body, name=name, grid=grid,
        in_specs=[a_spec, b_spec, *extra_specs] + [_hbm_spec()] * n_dep,
        out_specs=o_spec,
        out_shape=SDS(out_shape, out_dtype),
        scratch_shapes=[] if nk == 1 else [pltpu.VMEM(acc_shape, F32)],
        compiler_params=_cp(3),
    )(a, b, *extras, *deps)


def _relu2_bf16(t):
    r = jnp.maximum(t.astype(F32), 0.0)
    return (r * r).astype(BF16)


def _loss_kernel(y, t):
    S, D = y.shape
    tm = min(256, S)

    def body(y_ref, t_ref, dy_ref, dyb_ref, l_ref):
        @pl.when(pl.program_id(0) == 0)
        def _():
            l_ref[...] = jnp.zeros_like(l_ref)
        e = y_ref[...] - t_ref[...]
        l_ref[...] += jnp.sum(e * e, axis=0, keepdims=True)
        dy = e * (1.0 / D)
        dy_ref[...] = dy
        dyb_ref[...] = dy.astype(BF16)

    row = pl.BlockSpec((tm, D), lambda i: (i, 0))
    return pl.pallas_call(
        body, name="loss_head", grid=(S // tm,),
        in_specs=[row, row],
        out_specs=[row, row, pl.BlockSpec((1, D), lambda i: (0, 0))],
        out_shape=[SDS((S, D), F32), SDS((S, D), BF16), SDS((1, D), F32)],
        compiler_params=_cp(1),
    )(y, t)


def _rmsnorm_bwd(name, dh, x, g, dres, deps=()):
    S, D = x.shape
    tm = min(256, S)
    n_dep = len(deps)

    def body(dh_ref, x_ref, g_ref, dres_ref, *rest):
        dx_ref, dxb_ref, dg_ref = rest[n_dep:]
        @pl.when(pl.program_id(0) == 0)
        def _():
            dg_ref[...] = jnp.zeros_like(dg_ref)
        xv = x_ref[...]
        dhv = dh_ref[...]
        rstd = lax.rsqrt(jnp.mean(xv * xv, axis=-1, keepdims=True) + EPS)
        xhat = xv * rstd
        dg_ref[...] += jnp.sum(dhv * xhat, axis=0, keepdims=True)
        dxn = dhv * g_ref[...]
        dx = dres_ref[...] + rstd * (dxn - xhat * jnp.mean(dxn * xhat, axis=-1, keepdims=True))
        dx_ref[...] = dx
        dxb_ref[...] = dx.astype(BF16)

    row = pl.BlockSpec((tm, D), lambda i: (i, 0))
    vec = pl.BlockSpec((1, D), lambda i: (0, 0))
    return pl.pallas_call(
        body, name=name, grid=(S // tm,),
        in_specs=[row, row, vec, row] + [_hbm_spec()] * n_dep,
        out_specs=[row, row, vec],
        out_shape=[SDS((S, D), F32), SDS((S, D), BF16), SDS((1, D), F32)],
        compiler_params=_cp(1),
    )(dh, x, g, dres, *deps)


def _adamw(name, w, g, m, v):
    R, C = w.shape
    tr = 256 if R % 256 == 0 else R
    c1 = 1.0 - ADAM_B1 ** ADAM_STEP
    c2 = 1.0 - ADAM_B2 ** ADAM_STEP

    def body(w_ref, g_ref, m_ref, v_ref, d_ref, nm_ref, nv_ref, g_out_ref):
        gv = g_ref[...]
        nm = ADAM_B1 * m_ref[...] + (1.0 - ADAM_B1) * gv
        nv = ADAM_B2 * v_ref[...] + (1.0 - ADAM_B2) * (gv * gv)
        m_hat = nm / c1
        v_hat = nv / c2
        d_ref[...] = -ADAM_LR * (m_hat / (jnp.sqrt(v_hat) + ADAM_EPS) + ADAM_WD * w_ref[...])
        nm_ref[...] = nm
        nv_ref[...] = nv
        g_out_ref[...] = gv

    blk = pl.BlockSpec((tr, C), lambda i: (i, 0))
    return pl.pallas_call(
        body, name=name, grid=(R // tr,),
        in_specs=[blk] * 4, out_specs=[blk] * 4,
        out_shape=[SDS((R, C), F32)] * 4,
        compiler_params=_cp(1),
    )(w, g, m, v)


SGU_STEP_ROWS = 512


def _pair_select(lane, lo, hi):
    return jnp.where(lane < HEAD_DIM, lo, hi)


def _sgu_fwd(name, p, wt, bb):
    S = p.shape[0]

    rows = min(SGU_STEP_ROWS, S)

    def body(u_ref, v_ref, wt_ref, bb_ref, o_ref):
        lane = lax.broadcasted_iota(jnp.int32, (CHUNK, 128), 1)
        for ci in range(rows // CHUNK):
            rs = slice(CHUNK * ci, CHUNK * (ci + 1))
            for pp in range(A_HEADS // 2):
                cs = slice(128 * pp, 128 * (pp + 1))
                vb = v_ref[rs, cs].astype(BF16)
                mixed = _pair_select(lane,
                                     jnp.dot(wt_ref[2 * pp], vb, preferred_element_type=F32),
                                     jnp.dot(wt_ref[2 * pp + 1], vb, preferred_element_type=F32)) + bb_ref[:, cs]
                o_ref[rs, cs] = (u_ref[rs, cs] * mixed).astype(o_ref.dtype)

    return pl.pallas_call(
        body, name=name, grid=(S // rows,),
        in_specs=[pl.BlockSpec((rows, A_WIDTH), lambda c: (c, OFF_AU // A_WIDTH)),
                  pl.BlockSpec((rows, A_WIDTH), lambda c: (c, OFF_AV // A_WIDTH)),
                  pl.BlockSpec((A_HEADS, CHUNK, CHUNK), lambda c: (0, 0, 0)),
                  pl.BlockSpec((CHUNK, A_WIDTH), lambda c: (0, 0))],
        out_specs=pl.BlockSpec((rows, A_WIDTH), lambda c: (c, 0)),
        out_shape=SDS((S, A_WIDTH), BF16),
        compiler_params=_cp(1),
    )(p, p, wt, bb)


def _sgu_bwd(name, p, dycat, wt, wtt, bb):
    S = p.shape[0]
    rows = min(SGU_STEP_ROWS, S)

    def body(u_ref, v_ref, dy_ref, wt_ref, wtt_ref, bb_ref, du_ref, dv_ref, dw_ref, db_ref, dbacc_ref):
        c = pl.program_id(0)

        @pl.when(c == 0)
        def _():
            dw_ref[...] = jnp.zeros_like(dw_ref)
            dbacc_ref[...] = jnp.zeros_like(dbacc_ref)

        lane = lax.broadcasted_iota(jnp.int32, (CHUNK, 128), 1)
        row = lax.broadcasted_iota(jnp.int32, (CHUNK, 128), 0)
        causal = row >= lane
        nt = (((1,), (1,)), ((), ()))
        for pp in range(A_HEADS // 2):
            cs = slice(128 * pp, 128 * (pp + 1))
            dw_lo = jnp.zeros((CHUNK, CHUNK), F32)
            dw_hi = jnp.zeros((CHUNK, CHUNK), F32)
            dm_sum = jnp.zeros((CHUNK, 128), F32)
            for ci in range(rows // CHUNK):
                rs = slice(CHUNK * ci, CHUNK * (ci + 1))
                vb = v_ref[rs, cs].astype(BF16)
                dy = dy_ref[rs, cs]
                mixed = _pair_select(lane,
                                     jnp.dot(wt_ref[2 * pp], vb, preferred_element_type=F32),
                                     jnp.dot(wt_ref[2 * pp + 1], vb, preferred_element_type=F32)) + bb_ref[:, cs]
                du_ref[rs, cs] = (dy * mixed).astype(du_ref.dtype)
                dm = dy * u_ref[rs, cs]
                dmb = dm.astype(BF16)
                dv = _pair_select(lane,
                                  jnp.dot(wtt_ref[2 * pp], dmb, preferred_element_type=F32),
                                  jnp.dot(wtt_ref[2 * pp + 1], dmb, preferred_element_type=F32))
                dv_ref[rs, cs] = dv.astype(dv_ref.dtype)
                dm_sum += dm
                dm_lo = jnp.where(lane < HEAD_DIM, dm, 0.0).astype(BF16)
                dm_hi = jnp.where(lane >= HEAD_DIM, dm, 0.0).astype(BF16)
                dw_lo += lax.dot_general(dm_lo, vb, nt, preferred_element_type=F32)
                dw_hi += lax.dot_general(dm_hi, vb, nt, preferred_element_type=F32)
            dbacc_ref[:, cs] += dm_sum
            dw_ref[2 * pp] += jnp.where(causal, dw_lo, 0.0)
            dw_ref[2 * pp + 1] += jnp.where(causal, dw_hi, 0.0)

        @pl.when(c == S // rows - 1)
        def _():
            out = jnp.zeros((CHUNK, 128), F32)
            for pp in range(A_HEADS // 2):
                acc = dbacc_ref[:, 128 * pp:128 * (pp + 1)]
                s_lo = jnp.sum(jnp.where(lane < HEAD_DIM, acc, 0.0), axis=1, keepdims=True)
                s_hi = jnp.sum(jnp.where(lane >= HEAD_DIM, acc, 0.0), axis=1, keepdims=True)
                out = jnp.where(lane == 2 * pp, s_lo, out)
                out = jnp.where(lane == 2 * pp + 1, s_hi, out)
            db_ref[...] = out

    chunk = lambda col: pl.BlockSpec((rows, A_WIDTH), lambda c: (c, col))
    wspec = pl.BlockSpec((A_HEADS, CHUNK, CHUNK), lambda c: (0, 0, 0))
    return pl.pallas_call(
        body, name=name, grid=(S // rows,),
        in_specs=[chunk(OFF_AU // A_WIDTH), chunk(OFF_AV // A_WIDTH), chunk(0), wspec, wspec,
                  pl.BlockSpec((CHUNK, A_WIDTH), lambda c: (0, 0))],
        out_specs=[chunk(0), chunk(0), wspec, pl.BlockSpec((CHUNK, 128), lambda c: (0, 0))],
        out_shape=[SDS((S, A_WIDTH), BF16), SDS((S, A_WIDTH), BF16),
                   SDS((A_HEADS, CHUNK, CHUNK), F32), SDS((CHUNK, 128), F32)],
        scratch_shapes=[pltpu.VMEM((CHUNK, A_WIDTH), F32)],
        compiler_params=_cp(1),
    )(p, p, dycat, wt, wtt, bb)


CONV_HALO = 8
CONV_COLS = 256
CONV_ROWS = 1024


def _shift_down(a, halo, k):
    T = a.shape[0]
    row = lax.broadcasted_iota(jnp.int32, a.shape, 0)
    out = pltpu.roll(a, k, 0)
    for r in range(k):
        out = jnp.where(row == r, halo[CONV_HALO - k + r:CONV_HALO - k + r + 1, :], out)
    return out


def _shift_up(a, halo, k):
    T = a.shape[0]
    row = lax.broadcasted_iota(jnp.int32, a.shape, 0)
    out = pltpu.roll(a, T - k, 0)
    for r in range(k):
        out = jnp.where(row == T - k + r, halo[r:r + 1, :], out)
    return out


def _conv_specs(S, T):
    hb = T // CONV_HALO
    last = S // CONV_HALO - 1
    tile = lambda col0: pl.BlockSpec((T, CONV_COLS), lambda j, i: (i, col0 + j))
    prev = lambda col0: pl.BlockSpec((CONV_HALO, CONV_COLS), lambda j, i: (jnp.maximum(i * hb - 1, 0), col0 + j))
    nxt = lambda col0: pl.BlockSpec((CONV_HALO, CONV_COLS), lambda j, i: (jnp.minimum((i + 1) * hb, last), col0 + j))
    return tile, prev, nxt


def _conv_fwd(name, p, w):
    S = p.shape[0]
    T = min(CONV_ROWS, S)
    tile, prev, _ = _conv_specs(S, T)
    cb, cc, cx = OFF_BB // CONV_COLS, OFF_BC // CONV_COLS, OFF_BX // CONV_COLS

    def body(b_ref, c_ref, x_ref, ch_ref, xh_ref, w_ref, o_ref):
        i = pl.program_id(1)
        z = c_ref[...] * x_ref[...]
        zh = jnp.where(i > 0, ch_ref[...] * xh_ref[...], 0.0)
        z1 = _shift_down(z, zh, 1)
        z2 = _shift_down(z, zh, 2)
        conv = w_ref[0:1, :] * z2 + w_ref[1:2, :] * z1 + w_ref[2:3, :] * z
        o_ref[...] = (b_ref[...] * conv).astype(o_ref.dtype)

    return pl.pallas_call(
        body, name=name, grid=(B_WIDTH // CONV_COLS, S // T),
        in_specs=[tile(cb), tile(cc), tile(cx), prev(cc), prev(cx),
                  pl.BlockSpec((3, CONV_COLS), lambda j, i: (0, j))],
        out_specs=tile(0),
        out_shape=SDS((S, B_WIDTH), BF16),
        compiler_params=_cp(2),
    )(p, p, p, p, p, w)


def _conv_bwd(name, p, dycat, w):
    S = p.shape[0]
    T = min(CONV_ROWS, S)
    tile, prev, nxt = _conv_specs(S, T)
    cb, cc, cx = OFF_BB // CONV_COLS, OFF_BC // CONV_COLS, OFF_BX // CONV_COLS
    cdy = A_WIDTH // CONV_COLS
    n_i = S // T

    def body(b_ref, c_ref, x_ref, dy_ref, ch_ref, xh_ref, bn_ref, dyn_ref, w_ref,
             db_ref, dc_ref, dx_ref, dw_ref):
        i = pl.program_id(1)

        @pl.when(i == 0)
        def _():
            dw_ref[...] = jnp.zeros_like(dw_ref)

        cv = c_ref[...]
        xv = x_ref[...]
        z = cv * xv
        zh = jnp.where(i > 0, ch_ref[...] * xh_ref[...], 0.0)
        z1 = _shift_down(z, zh, 1)
        z2 = _shift_down(z, zh, 2)
        w0, w1, w2 = w_ref[0:1, :], w_ref[1:2, :], w_ref[2:3, :]
        conv = w0 * z2 + w1 * z1 + w2 * z
        dy = dy_ref[...]
        db_ref[...] = (dy * conv).astype(db_ref.dtype)
        dconv = dy * b_ref[...]
        dconv_n = jnp.where(i < n_i - 1, dyn_ref[...] * bn_ref[...], 0.0)
        dz = w2 * dconv + w1 * _shift_up(dconv, dconv_n, 1) + w0 * _shift_up(dconv, dconv_n, 2)
        dc_ref[...] = (dz * xv).astype(dc_ref.dtype)
        dx_ref[...] = (dz * cv).astype(dx_ref.dtype)
        dw_ref[0:1, :] += jnp.sum(dconv * z2, axis=0, keepdims=True)
        dw_ref[1:2, :] += jnp.sum(dconv * z1, axis=0, keepdims=True)
        dw_ref[2:3, :] += jnp.sum(dconv * z, axis=0, keepdims=True)

    wspec = pl.BlockSpec((3, CONV_COLS), lambda j, i: (0, j))
    return pl.pallas_call(
        body, name=name, grid=(B_WIDTH // CONV_COLS, n_i),
        in_specs=[tile(cb), tile(cc), tile(cx), tile(cdy), prev(cc), prev(cx), nxt(cb), nxt(cdy), wspec],
        out_specs=[tile(0), tile(0), tile(0), wspec],
        out_shape=[SDS((S, B_WIDTH), BF16)] * 3 + [SDS((3, B_WIDTH), F32)],
        compiler_params=_cp(2),
    )(p, p, p, dycat, p, p, p, dycat, w)


def _seg_sum(t, bd):
    hi = t.astype(BF16)
    lo = (t - hi.astype(F32)).astype(BF16)
    return jnp.dot(hi, bd, preferred_element_type=F32) + jnp.dot(lo, bd, preferred_element_type=F32)


def _head_norm(x, g, bd):
    rstd = lax.rsqrt(_seg_sum(x * x, bd) * (1.0 / HEAD_DIM) + EPS)
    xhat = x * rstd
    return xhat * g, xhat, rstd


def _head_norm_bwd(dy, g, xhat, rstd, bd):
    dxh = dy * g
    return rstd * (dxh - xhat * (_seg_sum(dxh * xhat, bd) * (1.0 / HEAD_DIM)))


def _band_mask(has_prev):
    row = lax.broadcasted_iota(jnp.int32, (BLK, 2 * BLK), 0)
    col = lax.broadcasted_iota(jnp.int32, (BLK, 2 * BLK), 1)
    first_key = jnp.where(has_prev, 0, BLK)
    return (col >= row) & (col <= row + BLK) & (col >= first_key)


def _first_of_segment(g, n, n_blocks):
    per_seg = lax.shift_right_logical(jnp.int32(n_blocks), 2 * g)
    return (n & (per_seg - 1)) == 0


def _residue_rows(r, d):
    return slice(None) if d == 1 else pl.ds(r, BLK, stride=d)


STRIDED_LANES = 128


def _step_width(d):
    return PW if d == 1 else STRIDED_LANES


def _n_stack(lane):
    return lane.shape[1] // HEAD_DIM


def _for_residues(d, fn):
    if d == 1:
        fn(0)
    else:
        def two(i, carry):
            fn(2 * i)
            fn(2 * i + 1)
            return carry
        lax.fori_loop(0, d // 2, two, 0)


def _head_mask(lane, j):
    return (lane >= HEAD_DIM * j) & (lane < HEAD_DIM * (j + 1))


def _stack_heads(x, lane):
    return jnp.concatenate([jnp.where(_head_mask(lane, j), x, 0.0) for j in range(_n_stack(lane))], axis=0)


def _unstack_heads(y, lane):
    out = y[:BLK]
    for j in range(1, _n_stack(lane)):
        out = jnp.where(lane >= HEAD_DIM * j, y[BLK * j:BLK * (j + 1)], out)
    return out


def _head_columns(v, lane):
    return jnp.concatenate([jnp.max(jnp.where(_head_mask(lane, j), v, NEG), axis=1, keepdims=True)
                            for j in range(_n_stack(lane))], axis=0)


def _attn_fwd(name, p, g, gq, gk, bd):
    S = p.shape[0]
    d = PATTERN_DILATION[g]
    rows = BLK * d
    hw = _step_width(d)
    nt = (((1,), (1,)), ((), ()))

    def body(q_ref, kc_ref, kp_ref, vc_ref, vp_ref, gq_ref, gk_ref, bd_ref, o_ref, lse_ref):
        has_prev = pl.program_id(1) > 0
        bdv = bd_ref[...]
        band = jnp.concatenate([_band_mask(has_prev)] * (hw // HEAD_DIM), axis=0)
        lane = lax.broadcasted_iota(jnp.int32, (1, hw), 1)

        def residue(r):
            rr = _residue_rows(r, d)
            qn, _, _ = _head_norm(q_ref[rr, :], gq_ref[...], bdv)
            kn, _, _ = _head_norm(jnp.concatenate([kp_ref[rr, :], kc_ref[rr, :]], axis=0), gk_ref[...], bdv)
            knb = kn.astype(BF16)
            vb = jnp.concatenate([vp_ref[rr, :], vc_ref[rr, :]], axis=0).astype(BF16)
            qs = _stack_heads(qn, lane).astype(BF16)
            s = lax.dot_general(qs, knb, nt, preferred_element_type=F32) * (HEAD_DIM ** -0.5)
            s = jnp.where(band, s, NEG)
            m = jnp.max(s, axis=1, keepdims=True)
            e = jnp.exp(s - m)
            den = jnp.sum(e, axis=1, keepdims=True)
            pv = jnp.dot(e.astype(BF16), vb, preferred_element_type=F32)
            o_ref[rr, :] = _unstack_heads(pv / den, lane)
            lse_ref[rr, :] = _unstack_heads(jnp.broadcast_to(m + jnp.log(den), pv.shape), lane)

        _for_residues(d, residue)

    per = PW // hw
    cq, ck, cv = (OFF_Q + PW * g) // hw, (OFF_K + PW * g) // hw, (OFF_V + PW * g) // hw
    cur = lambda col: pl.BlockSpec((rows, hw), lambda h, n: (n, col + h))
    prv = lambda col: pl.BlockSpec((rows, hw), lambda h, n: (jnp.maximum(n - 1, 0), col + h))
    vec = pl.BlockSpec((1, hw), lambda h, n: (0, h))
    return pl.pallas_call(
        body, name=name, grid=(per, S // rows),
        in_specs=[cur(cq), cur(ck), prv(ck), cur(cv), prv(cv), vec, vec, pl.BlockSpec((hw, hw), lambda h, n: (0, 0))],
        out_specs=[cur(0), cur(0)],
        out_shape=[SDS((S, PW), F32)] * 2,
        compiler_params=_cp(2),
    )(p, p, p, p, p, gq, gk, bd)


def _attn_bwd(name, p, g, lse, do3, c3, gq, gk, bd):
    S = p.shape[0]
    d = PATTERN_DILATION[g]
    rows = BLK * d
    nblk = S // rows
    hw = _step_width(d)
    nt = (((1,), (1,)), ((), ()))
    tn = (((0,), (0,)), ((), ()))

    def body(q_ref, kc_ref, kp_ref, vc_ref, vp_ref, lse_ref, do_ref, c_ref, gq_ref, gk_ref, bd_ref,
             dq_ref, dk_ref, dv_ref, dgq_ref, dgk_ref, ck_ref, cv_ref, dq_keep_ref):
        n = pl.program_id(1)

        @pl.when(n == 0)
        def _():
            ck_ref[...] = jnp.zeros_like(ck_ref)
            cv_ref[...] = jnp.zeros_like(cv_ref)
            dgq_ref[...] = jnp.zeros_like(dgq_ref)
            dgk_ref[...] = jnp.zeros_like(dgk_ref)

        @pl.when(n == nblk)
        def _():
            dq_ref[...] = dq_keep_ref[...]
            dk_ref[...] = ck_ref[...]
            dv_ref[...] = cv_ref[...]

        bdv = bd_ref[...]
        gqv = gq_ref[...]
        gkv = gk_ref[...]
        band = jnp.concatenate([_band_mask(n > 0)] * (hw // HEAD_DIM), axis=0)
        lane = lax.broadcasted_iota(jnp.int32, (1, hw), 1)

        def residue(r):
            rr = _residue_rows(r, d)
            qn, qhat, qrstd = _head_norm(q_ref[rr, :], gqv, bdv)
            kn, khat, krstd = _head_norm(jnp.concatenate([kp_ref[rr, :], kc_ref[rr, :]], axis=0), gkv, bdv)
            knb = kn.astype(BF16)
            vb = jnp.concatenate([vp_ref[rr, :], vc_ref[rr, :]], axis=0).astype(BF16)
            qs = _stack_heads(qn, lane).astype(BF16)
            dos = _stack_heads(do_ref[rr, :], lane).astype(BF16)
            s = lax.dot_general(qs, knb, nt, preferred_element_type=F32) * (HEAD_DIM ** -0.5)
            prob = jnp.where(band, jnp.exp(s - _head_columns(lse_ref[rr, :], lane)), 0.0)
            dp = lax.dot_general(dos, vb, nt, preferred_element_type=F32)
            ds = (prob * (dp + _head_columns(c_ref[rr, :], lane)) * (HEAD_DIM ** -0.5)).astype(BF16)
            dqn = _unstack_heads(jnp.dot(ds, knb, preferred_element_type=F32), lane)
            dkn = lax.dot_general(ds, qs, tn, preferred_element_type=F32)
            dvv = lax.dot_general(prob.astype(BF16), dos, tn, preferred_element_type=F32)

            dq = _head_norm_bwd(dqn, gqv, qhat, qrstd, bdv)
            dq_ref[rr, :] = dq
            dq_keep_ref[rr, :] = dq
            dk2 = _head_norm_bwd(dkn, gkv, khat, krstd, bdv)
            dgq_ref[...] += jnp.sum(dqn * qhat, axis=0, keepdims=True)
            dgk_ref[...] += jnp.sum(dkn * khat, axis=0, keepdims=True)
            dk_ref[rr, :] = ck_ref[rr, :] + dk2[:BLK]
            dv_ref[rr, :] = cv_ref[rr, :] + dvv[:BLK]
            ck_ref[rr, :] = dk2[BLK:]
            cv_ref[rr, :] = dvv[BLK:]

        @pl.when(n < nblk)
        def _():
            _for_residues(d, residue)

    last = nblk - 1
    per = PW // hw
    cq, ck, cv = (OFF_Q + PW * g) // hw, (OFF_K + PW * g) // hw, (OFF_V + PW * g) // hw
    cur = lambda col: pl.BlockSpec((rows, hw), lambda h, n: (jnp.minimum(n, last), col + h))
    prv = lambda col: pl.BlockSpec((rows, hw), lambda h, n: (jnp.maximum(jnp.minimum(n, last) - 1, 0), col + h))
    cur3 = pl.BlockSpec((None, rows, hw), lambda h, n: (g, jnp.minimum(n, last), h))
    done = pl.BlockSpec((rows, hw), lambda h, n: (jnp.maximum(n - 1, 0), h))
    vec = pl.BlockSpec((1, hw), lambda h, n: (0, h))
    return pl.pallas_call(
        body, name=name, grid=(per, nblk + 1),
        in_specs=[cur(cq), cur(ck), prv(ck), cur(cv), prv(cv), cur(0), cur3, cur3, vec, vec,
                  pl.BlockSpec((hw, hw), lambda h, n: (0, 0))],
        out_specs=[cur(0), done, done, vec, vec],
        out_shape=[SDS((S, PW), F32)] * 3 + [SDS((1, PW), F32)] * 2,
        scratch_shapes=[pltpu.VMEM((rows, hw), F32)] * 3,
        compiler_params=_cp(2),
    )(p, p, p, p, p, lse, do3, c3, gq, gk, bd)


def _mix_fwd(name, os, lses):
    S = os[0].shape[0]
    tm = min(512, S)

    def body(o0, o1, o2, l0, l1, l2, y_ref):
        o = [o0[...], o1[...], o2[...]]
        l = [l0[...], l1[...], l2[...]]
        m = jnp.maximum(jnp.maximum(l[0], l[1]), l[2])
        e = [jnp.exp(t - m) for t in l]
        inv = 1.0 / (e[0] + e[1] + e[2])
        for g in range(N_PATTERNS):
            y_ref[:, PW * g:PW * (g + 1)] = (o[g] * (e[g] * inv)).astype(y_ref.dtype)

    blk = pl.BlockSpec((tm, PW), lambda i: (i, 0))
    return pl.pallas_call(
        body, name=name, grid=(S // tm,),
        in_specs=[blk] * 6,
        out_specs=pl.BlockSpec((tm, C_WIDTH), lambda i: (i, 0)),
        out_shape=SDS((S, C_WIDTH), BF16),
        compiler_params=_cp(1),
    )(*os, *lses)


def _mix_bwd(name, os, lses, dycat, bd):
    S = os[0].shape[0]
    tm = min(512, S)
    c0 = (A_WIDTH + B_WIDTH) // PW

    def body(o0, o1, o2, l0, l1, l2, dy0_ref, dy1_ref, dy2_ref, bd_ref, do_ref, c_ref):
        bdv = bd_ref[...]
        o = [o0[...], o1[...], o2[...]]
        l = [l0[...], l1[...], l2[...]]
        dys = [dy0_ref[...], dy1_ref[...], dy2_ref[...]]
        m = jnp.maximum(jnp.maximum(l[0], l[1]), l[2])
        e = [jnp.exp(t - m) for t in l]
        inv = 1.0 / (e[0] + e[1] + e[2])
        alpha = [t * inv for t in e]
        da = [_seg_sum(dys[g] * o[g], bdv) for g in range(N_PATTERNS)]
        mean_da = alpha[0] * da[0] + alpha[1] * da[1] + alpha[2] * da[2]
        for g in range(N_PATTERNS):
            do_ref[g] = dys[g] * alpha[g]
            c_ref[g] = -alpha[g] * mean_da

    blk = pl.BlockSpec((tm, PW), lambda i: (i, 0))
    blk3 = pl.BlockSpec((N_PATTERNS, tm, PW), lambda i: (0, i, 0))
    dyspec = lambda g: pl.BlockSpec((tm, PW), lambda i: (i, c0 + g))
    return pl.pallas_call(
        body, name=name, grid=(S // tm,),
        in_specs=[blk] * 6 + [dyspec(0), dyspec(1), dyspec(2), pl.BlockSpec((PW, PW), lambda i: (0, 0))],
        out_specs=[blk3, blk3],
        out_shape=[SDS((N_PATTERNS, S, PW), F32)] * 2,
        compiler_params=_cp(1),
    )(*os, *lses, dycat, dycat, dycat, bd)


def _mesh_pos():
    x, y, c = lax.axis_index("x"), lax.axis_index("y"), lax.axis_index("c")
    chips = [(1 - x, y), (x, 1 - y), (1 - x, 1 - y)]
    chip_idx = [2 * cx + cy for cx, cy in chips]
    return x, y, c, 2 * x + y, chips, chip_idx


def _place_shard(name, w, layer, chip_arr, out_dtype, deps=()):
    _, R, C = w.shape
    tr = min(256, R)

    def body(chip_ref, w_ref, *rest):
        o_ref = rest[-1]
        o_ref[...] = w_ref[...].astype(o_ref.dtype)

    return pl.pallas_call(
        body, name=name,
        grid_spec=pltpu.PrefetchScalarGridSpec(
            num_scalar_prefetch=1, grid=(R // tr,),
            in_specs=[pl.BlockSpec((None, tr, C), lambda i, chip_ref: (layer, i, 0))] + [_hbm_spec()] * len(deps),
            out_specs=pl.BlockSpec((None, tr, C), lambda i, chip_ref: (chip_ref[0], i, 0))),
        out_shape=SDS((N_CHIPS, R, C), out_dtype),
        compiler_params=_cp(1),
    )(chip_arr, w, *deps)


HBM_SPEC = pl.BlockSpec(memory_space=pltpu.HBM)
SEM_SPEC = pl.BlockSpec(memory_space=pltpu.SEMAPHORE)
SPLIT_COPY = pltpu.SideEffectType.DATAFLOW_SIDE_EFFECTING
N_PEER_CHIPS = N_CHIPS - 1
TOKEN_SHAPE = SDS((8, 128), F32)
TOKEN_SPEC = pl.BlockSpec(memory_space=pltpu.VMEM)


def _in_hbm(a):
    return pltpu.with_memory_space_constraint(a, pltpu.HBM)


def _gather_start(name, bufs):
    T = len(bufs)

    def body(*refs):
        ins = refs[:T]
        send_sems, recv_sems = refs[T:2 * T], refs[2 * T:3 * T]
        token = refs[4 * T]
        x, y, c, me, chips, chip_idx = _mesh_pos()
        for t in range(T):
            hr = ins[t].shape[1] // 2
            mine = ins[t].at[me, pl.ds(c * hr, hr), :]
            for j in range(N_PEER_CHIPS):
                pltpu.make_async_remote_copy(src_ref=mine, dst_ref=mine, send_sem=send_sems[t].at[j],
                                             recv_sem=recv_sems[t].at[j], device_id=(*chips[j], c),
                                             device_id_type=MESH).start()
        token[...] = jnp.zeros_like(token)

    sems = [pltpu.SemaphoreType.DMA((N_PEER_CHIPS,))] * T
    out = pl.pallas_call(
        body, name=name,
        in_specs=[HBM_SPEC] * T,
        out_specs=[SEM_SPEC] * (2 * T) + [HBM_SPEC] * T + [TOKEN_SPEC],
        out_shape=sems + sems + [pltpu.HBM(b.shape, b.dtype) for b in bufs] + [TOKEN_SHAPE],
        input_output_aliases={t: 2 * T + t for t in range(T)},
        compiler_params=pltpu.CompilerParams(has_side_effects=SPLIT_COPY),
    )(*[_in_hbm(b) for b in bufs])
    return out[:T], out[T:2 * T], out[2 * T:3 * T], out[3 * T]


def _gather_wait(name, buf, send_sem, recv_sem, after):
    n_in = 3 if after is None else 4

    def body(*refs):
        buf_ref, ssem, rsem = refs[:3]
        x, y, c, me, chips, chip_idx = _mesh_pos()
        hr = buf_ref.shape[1] // 2
        mine = buf_ref.at[me, pl.ds(c * hr, hr), :]
        for j in range(N_PEER_CHIPS):
            got = buf_ref.at[chip_idx[j], pl.ds(c * hr, hr), :]
            cp = pltpu.make_async_remote_copy(src_ref=mine, dst_ref=got, send_sem=ssem.at[j], recv_sem=rsem.at[j],
                                              device_id=(*chips[j], c), device_id_type=MESH)
            cp.wait_send()
            cp.wait_recv()

    args = [buf, send_sem, recv_sem] + ([] if after is None else [after])
    return pl.pallas_call(
        body, name=name,
        in_specs=[HBM_SPEC, SEM_SPEC, SEM_SPEC] + [_hbm_spec()] * (n_in - 3),
        out_specs=HBM_SPEC,
        out_shape=pltpu.HBM(buf.shape, buf.dtype),
        input_output_aliases={0: 0},
        compiler_params=pltpu.CompilerParams(has_side_effects=SPLIT_COPY),
    )(*args)


def _forward_start(name, buf):
    def body(buf_ref, send_sems, recv_sems, buf_thru, token):
        x, y, c, me, chips, chip_idx = _mesh_pos()
        hr = buf_ref.shape[1] // 2
        for j in range(N_PEER_CHIPS):
            got = buf_ref.at[chip_idx[j], pl.ds(c * hr, hr), :]
            pltpu.make_async_remote_copy(src_ref=got, dst_ref=got, send_sem=send_sems.at[j], recv_sem=recv_sems.at[j],
                                         device_id=(x, y, 1 - c), device_id_type=MESH).start()
        token[...] = jnp.zeros_like(token)

    sems = pltpu.SemaphoreType.DMA((N_PEER_CHIPS,))
    return pl.pallas_call(
        body, name=name,
        in_specs=[HBM_SPEC],
        out_specs=[SEM_SPEC, SEM_SPEC, HBM_SPEC, TOKEN_SPEC],
        out_shape=[sems, sems, pltpu.HBM(buf.shape, buf.dtype), TOKEN_SHAPE],
        input_output_aliases={0: 2},
        compiler_params=pltpu.CompilerParams(has_side_effects=SPLIT_COPY),
    )(_in_hbm(buf))


def _forward_wait(name, buf, send_sems, recv_sems, after):
    n_in = 3 if after is None else 4

    def body(*refs):
        buf_ref, ssems, rsems = refs[:3]
        x, y, c, me, chips, chip_idx = _mesh_pos()
        hr = buf_ref.shape[1] // 2
        for j in range(N_PEER_CHIPS):
            sent = buf_ref.at[chip_idx[j], pl.ds(c * hr, hr), :]
            theirs = buf_ref.at[chip_idx[j], pl.ds((1 - c) * hr, hr), :]
            cp = pltpu.make_async_remote_copy(src_ref=sent, dst_ref=theirs, send_sem=ssems.at[j],
                                              recv_sem=rsems.at[j], device_id=(x, y, 1 - c), device_id_type=MESH)
            cp.wait_send()
            cp.wait_recv()

    args = [buf, send_sems, recv_sems] + ([] if after is None else [after])
    return pl.pallas_call(
        body, name=name,
        in_specs=[HBM_SPEC, SEM_SPEC, SEM_SPEC] + [_hbm_spec()] * (n_in - 3),
        out_specs=HBM_SPEC,
        out_shape=pltpu.HBM(buf.shape, buf.dtype),
        input_output_aliases={0: 0},
        compiler_params=pltpu.CompilerParams(has_side_effects=SPLIT_COPY),
    )(*args)


class _GatheredWeights:
    def __init__(self):
        self._order = []
        self._pending = {}
        self._forwarding = {}
        self._ready = {}
        self._tokens = []

    def start(self, keys, bufs):
        send_sems, recv_sems, thru, token = _gather_start(f"gather_start_{len(self._order)}", bufs)
        self._tokens.append(token)
        self._order.extend(keys)
        self._pending.update({k: (b, s, r) for k, b, s, r in zip(keys, thru, send_sems, recv_sems)})

    def _prefetch(self, key, after):
        if key in self._pending:
            buf, ssem, rsem = self._pending.pop(key)
            tag = f"{key[0]}_{key[1]}"
            buf = _gather_wait(f"gather_wait_{tag}", buf, ssem, rsem, after)
            ssems, rsems, buf, token = _forward_start(f"gather_fwd_start_{tag}", buf)
            self._forwarding[key] = (buf, ssems, rsems)
            self._tokens.append(token)

    def get(self, name, layer, after=None, prefetch_next=True):
        key = (name, layer)
        if key not in self._ready:
            self._prefetch(key, after)
            buf, ssems, rsems = self._forwarding.pop(key)
            self._ready[key] = _forward_wait(f"gather_fwd_wait_{name}_{layer}", buf, ssems, rsems, after)
            if prefetch_next:
                self.prefetch_after(name, layer, after)
        return self._ready[key]

    def prefetch_after(self, name, layer, after):
        nxt = self._order.index((name, layer)) + 1
        if nxt < len(self._order):
            self._prefetch(self._order[nxt], after)

    def deps(self):
        tokens, self._tokens = self._tokens, []
        return tokens


def _swap_copy(g_ref, land_ref, send_sem, recv_sem):
    x, y, c, _, _, _ = _mesh_pos()
    hr = g_ref.shape[1] // 2
    return pltpu.make_async_remote_copy(src_ref=g_ref.at[:, pl.ds((1 - c) * hr, hr), :], dst_ref=land_ref,
                                        send_sem=send_sem, recv_sem=recv_sem, device_id=(x, y, 1 - c),
                                        device_id_type=MESH)


def _swap_start(name, g):
    land_shape = (g.shape[0], g.shape[1] // 2, g.shape[2])

    def body(g_ref, land_ref, send_sem, recv_sem, land_thru, token):
        _swap_copy(g_ref, land_ref, send_sem, recv_sem).start()
        token[...] = jnp.zeros_like(token)

    return pl.pallas_call(
        body, name=name,
        in_specs=[HBM_SPEC, HBM_SPEC],
        out_specs=[SEM_SPEC, SEM_SPEC, HBM_SPEC, TOKEN_SPEC],
        out_shape=[pltpu.SemaphoreType.DMA(()), pltpu.SemaphoreType.DMA(()), pltpu.HBM(land_shape, g.dtype),
                   TOKEN_SHAPE],
        input_output_aliases={1: 2},
        compiler_params=pltpu.CompilerParams(has_side_effects=SPLIT_COPY),
    )(_in_hbm(g), _in_hbm(lax.empty(land_shape, g.dtype)))


def _swap_wait(name, g, land, send_sem, recv_sem, after):
    def body(g_ref, land_ref, send_sem, recv_sem, after_ref, land_out):
        cp = _swap_copy(g_ref, land_ref, send_sem, recv_sem)
        cp.wait_send()
        cp.wait_recv()

    return pl.pallas_call(
        body, name=name,
        in_specs=[HBM_SPEC, HBM_SPEC, SEM_SPEC, SEM_SPEC, _hbm_spec()],
        out_specs=HBM_SPEC,
        out_shape=pltpu.HBM(land.shape, land.dtype),
        input_output_aliases={1: 0},
        compiler_params=pltpu.CompilerParams(has_side_effects=SPLIT_COPY),
    )(_in_hbm(g), land, send_sem, recv_sem, after)


def _add_my_half(name, g, r, pos_arr):
    ns, R, C = g.shape
    hr = R // 2
    tr = min(256, hr)
    nt = hr // tr

    def body(pos_ref, g_ref, r_ref, o_ref, land_ref):
        t = (g_ref[...] + r_ref[...]).astype(o_ref.dtype)
        o_ref[...] = t

        @pl.when(pl.program_id(1) == pos_ref[1])
        def _():
            land_ref[...] = t

    blk = pl.BlockSpec((None, tr, C), lambda i, s, pos_ref: (s, i, 0))
    return pl.pallas_call(
        body, name=name,
        grid_spec=pltpu.PrefetchScalarGridSpec(
            num_scalar_prefetch=1, grid=(nt, ns),
            in_specs=[pl.BlockSpec((None, tr, C), lambda i, s, pos_ref: (s, pos_ref[0] * nt + i, 0)), blk],
            out_specs=[blk, pl.BlockSpec((None, tr, C), lambda i, s, pos_ref: (pos_ref[1], i, 0))]),
        out_shape=[SDS((ns, hr, C), BF16)] * 2,
        compiler_params=_cp(2),
    )(pos_arr, g, r)


def _exchange_start(name, part, land):
    def body(part_ref, land_ref, send_sems, recv_sems, land_thru, token):
        x, y, c, me, chips, chip_idx = _mesh_pos()
        for j in range(N_PEER_CHIPS):
            pltpu.make_async_remote_copy(src_ref=part_ref.at[chip_idx[j]], dst_ref=land_ref.at[me],
                                         send_sem=send_sems.at[j], recv_sem=recv_sems.at[j],
                                         device_id=(*chips[j], c), device_id_type=MESH).start()
        token[...] = jnp.zeros_like(token)

    sems = pltpu.SemaphoreType.DMA((N_PEER_CHIPS,))
    return pl.pallas_call(
        body, name=name,
        in_specs=[HBM_SPEC, HBM_SPEC],
        out_specs=[SEM_SPEC, SEM_SPEC, HBM_SPEC, TOKEN_SPEC],
        out_shape=[sems, sems, pltpu.HBM(land.shape, land.dtype), TOKEN_SHAPE],
        input_output_aliases={1: 2},
        compiler_params=pltpu.CompilerParams(has_side_effects=SPLIT_COPY),
    )(_in_hbm(part), _in_hbm(land))


def _exchange_wait(name, part, land, send_sems, recv_sems, after):
    def body(part_ref, land_ref, send_sems, recv_sems, after_ref, land_out):
        x, y, c, me, chips, chip_idx = _mesh_pos()
        for j in range(N_PEER_CHIPS):
            cp = pltpu.make_async_remote_copy(src_ref=part_ref.at[chip_idx[j]], dst_ref=land_ref.at[chip_idx[j]],
                                              send_sem=send_sems.at[j], recv_sem=recv_sems.at[j],
                                              device_id=(*chips[j], c), device_id_type=MESH)
            cp.wait_send()
            cp.wait_recv()

    return pl.pallas_call(
        body, name=name,
        in_specs=[HBM_SPEC, HBM_SPEC, SEM_SPEC, SEM_SPEC, _hbm_spec()],
        out_specs=HBM_SPEC,
        out_shape=pltpu.HBM(land.shape, land.dtype),
        input_output_aliases={1: 0},
        compiler_params=pltpu.CompilerParams(has_side_effects=SPLIT_COPY),
    )(_in_hbm(part), land, send_sems, recv_sems, after)


class _GradReducer:
    def __init__(self, c_arr):
        self._c_arr = c_arr
        self._swapping = []
        self._exchanging = {}
        self._joining = {}
        self._tokens = []

    def begin(self, name, layer, g):
        tag = f"{name}_{layer}"
        ssem, rsem, land, token = _swap_start(f"rs_swap_start_{tag}", g)
        self._swapping.append((name, layer, g, ssem, rsem, land))
        self._tokens.append(token)

    def advance(self, after):
        for name, layer, g, ssem, rsem, land in self._swapping:
            tag = f"{name}_{layer}"
            theirs = _swap_wait(f"rs_swap_wait_{tag}", g, land, ssem, rsem, after)
            part, own = _add_my_half(f"rs_add_{tag}", g, theirs, self._c_arr)
            ssems, rsems, land2, token = _exchange_start(f"rs_xchg_start_{tag}", part, own)
            self._exchanging[(name, layer)] = (part, ssems, rsems, land2)
            self._tokens.append(token)
        self._swapping = []

    def deps(self):
        tokens, self._tokens = self._tokens, []
        return tokens

    def reduce(self, name, n_layers, after):
        buf = None
        for layer in range(n_layers):
            part, ssems, rsems, land = self._exchanging.pop((name, layer))
            tag = f"{name}_{layer}"
            landed = _exchange_wait(f"rs_xchg_wait_{tag}", part, land, ssems, rsems, after)
            buf = _sum_chips(f"rs_sum_{tag}", landed, self._c_arr, layer, n_layers, buf)
        ssem, rsem, buf, token = _join_start(f"rs_join_start_{name}", buf)
        self._joining[name] = (buf, ssem, rsem)
        return token

    def reduced(self, name, after):
        buf, ssem, rsem = self._joining.pop(name)
        return _join_wait(f"rs_join_wait_{name}", buf, ssem, rsem, after)


def _sum_chips(name, r, c_arr, layer, n_layers, prev):
    ns, H, C = r.shape
    tr = min(256, H)
    nt = H // tr

    def body(c_ref, r_ref, *rest):
        o_ref = rest[-1]
        o_ref[...] = ((r_ref[0].astype(F32) + r_ref[1].astype(F32)) + r_ref[2].astype(F32)) + r_ref[3].astype(F32)

    in_specs = [pl.BlockSpec((ns, tr, C), lambda i, c_ref: (0, i, 0))]
    args = [c_arr, r]
    aliases = {}
    if prev is not None:
        in_specs.append(_hbm_spec())
        args.append(prev)
        aliases = {2: 0}
    return pl.pallas_call(
        body, name=name,
        grid_spec=pltpu.PrefetchScalarGridSpec(
            num_scalar_prefetch=1, grid=(nt,), in_specs=in_specs,
            out_specs=pl.BlockSpec((None, tr, C), lambda i, c_ref: (layer, c_ref[0] * nt + i, 0))),
        out_shape=SDS((n_layers, 2 * H, C), F32),
        input_output_aliases=aliases,
        compiler_params=_cp(1),
    )(*args)


def _join_copy(buf_ref, send_sem, recv_sem):
    x, y, c, _, _, _ = _mesh_pos()
    hr = buf_ref.shape[1] // 2
    mine = buf_ref.at[:, pl.ds(c * hr, hr), :]
    theirs = buf_ref.at[:, pl.ds((1 - c) * hr, hr), :]
    send = pltpu.make_async_remote_copy(src_ref=mine, dst_ref=mine, send_sem=send_sem, recv_sem=recv_sem,
                                        device_id=(x, y, 1 - c), device_id_type=MESH)
    arrive = pltpu.make_async_remote_copy(src_ref=theirs, dst_ref=theirs, send_sem=send_sem, recv_sem=recv_sem,
                                          device_id=(x, y, 1 - c), device_id_type=MESH)
    return send, arrive


def _join_start(name, buf):
    def body(buf_ref, send_sem, recv_sem, buf_thru, token):
        _join_copy(buf_ref, send_sem, recv_sem)[0].start()
        token[...] = jnp.zeros_like(token)

    return pl.pallas_call(
        body, name=name,
        in_specs=[HBM_SPEC],
        out_specs=[SEM_SPEC, SEM_SPEC, HBM_SPEC, TOKEN_SPEC],
        out_shape=[pltpu.SemaphoreType.DMA(()), pltpu.SemaphoreType.DMA(()), pltpu.HBM(buf.shape, buf.dtype),
                   TOKEN_SHAPE],
        input_output_aliases={0: 2},
        compiler_params=pltpu.CompilerParams(has_side_effects=SPLIT_COPY),
    )(_in_hbm(buf))


def _join_wait(name, buf, send_sem, recv_sem, after):
    def body(buf_ref, send_sem, recv_sem, after_ref, buf_out):
        send, arrive = _join_copy(buf_ref, send_sem, recv_sem)
        send.wait_send()
        arrive.wait_recv()

    return pl.pallas_call(
        body, name=name,
        in_specs=[HBM_SPEC, SEM_SPEC, SEM_SPEC, _hbm_spec()],
        out_specs=HBM_SPEC,
        out_shape=pltpu.HBM(buf.shape, buf.dtype),
        input_output_aliases={0: 0},
        compiler_params=pltpu.CompilerParams(has_side_effects=SPLIT_COPY),
    )(buf, send_sem, recv_sem, after)


def _small_copy(k, buf_ref, land_ref, send_sems, recv_sems):
    x, y, c = lax.axis_index("x"), lax.axis_index("y"), lax.axis_index("c")
    me = 4 * x + 2 * y + c
    peer = (x ^ ((k >> 2) & 1), y ^ ((k >> 1) & 1), c ^ (k & 1))
    cp = pltpu.make_async_remote_copy(src_ref=buf_ref, dst_ref=land_ref.at[me], send_sem=send_sems.at[k - 1],
                                      recv_sem=recv_sems.at[k - 1], device_id=peer, device_id_type=MESH)
    return me, peer, cp


def _small_start(buf, deps):
    land = jnp.broadcast_to(buf[None], (N_DEV,) + buf.shape)
    n_dep = len(deps)

    def body(buf_ref, land_ref, *rest):
        send_sems, recv_sems, _, token = rest[n_dep:]
        for k in range(1, N_DEV):
            _small_copy(k, buf_ref, land_ref, send_sems, recv_sems)[2].start()
        token[...] = jnp.zeros_like(token)

    sems = pltpu.SemaphoreType.DMA((N_DEV - 1,))
    return pl.pallas_call(
        body, name="small_gather_start",
        in_specs=[HBM_SPEC, HBM_SPEC] + [_hbm_spec()] * n_dep,
        out_specs=[SEM_SPEC, SEM_SPEC, HBM_SPEC, TOKEN_SPEC],
        out_shape=[sems, sems, pltpu.HBM(land.shape, land.dtype), TOKEN_SHAPE],
        input_output_aliases={1: 2},
        compiler_params=pltpu.CompilerParams(has_side_effects=SPLIT_COPY),
    )(_in_hbm(buf), _in_hbm(land), *deps)


def _small_wait(buf, land, send_sems, recv_sems, after):
    def body(buf_ref, land_ref, send_sems, recv_sems, after_ref, land_out):
        for k in range(1, N_DEV):
            me, peer, cp = _small_copy(k, buf_ref, land_ref, send_sems, recv_sems)
            cp.wait_send()
            got = land_ref.at[me ^ k]
            pltpu.make_async_remote_copy(src_ref=got, dst_ref=got, send_sem=send_sems.at[k - 1],
                                         recv_sem=recv_sems.at[k - 1], device_id=peer,
                                         device_id_type=MESH).wait_recv()

    return pl.pallas_call(
        body, name="small_gather_wait",
        in_specs=[HBM_SPEC, HBM_SPEC, SEM_SPEC, SEM_SPEC, _hbm_spec()],
        out_specs=HBM_SPEC,
        out_shape=pltpu.HBM(land.shape, land.dtype),
        input_output_aliases={1: 0},
        compiler_params=pltpu.CompilerParams(has_side_effects=SPLIT_COPY),
    )(_in_hbm(buf), land, send_sems, recv_sems, after)


def _sum_devices(land):
    n, R, C = land.shape

    def body(land_ref, out_ref):
        acc = land_ref[0]
        for d in range(1, n):
            acc = acc + land_ref[d]
        out_ref[...] = acc

    return pl.pallas_call(
        body, name="small_sum",
        in_specs=[pl.BlockSpec(memory_space=pltpu.VMEM)],
        out_specs=pl.BlockSpec(memory_space=pltpu.VMEM),
        out_shape=SDS((R, C), land.dtype),
        compiler_params=pltpu.CompilerParams(vmem_limit_bytes=V7X_VMEM_LIMIT),
    )(land)


def _deinterleave(t, d):
    if d == 1:
        return t
    S, W = t.shape
    return t.reshape(S // d, d, W).transpose(1, 0, 2).reshape(S, W)


def _interleave(t, d):
    if d == 1:
        return t
    S, W = t.shape
    return t.reshape(d, S // d, W).transpose(1, 0, 2).reshape(S, W)


def _to_patterns(t, off):
    return jnp.stack([_deinterleave(t[:, off + PW * g:off + PW * (g + 1)], PATTERN_DILATION[g])
                      for g in range(N_PATTERNS)])


def _from_patterns(t3):
    return jnp.stack([_interleave(t3[g], PATTERN_DILATION[g]) for g in range(N_PATTERNS)])


def _pack_rows(vectors):
    flat = jnp.concatenate([v.reshape(-1) for v in vectors])
    n = flat.shape[0]
    padded = -(-n // 1024) * 1024
    return jnp.pad(flat, (0, padded - n)).reshape(padded // 128, 128)


def _unpack_rows(buf, shapes):
    flat = buf.reshape(-1)
    out, off = [], 0
    for s in shapes:
        n = 1
        for dim in s:
            n *= dim
        out.append(flat[off:off + n].reshape(s))
        off += n
    return out


def _layer_forward(l, x, prm, wg):
    S, D = x.shape
    w_in = wg.get("w_in", l, x, prefetch_next=l > 0)
    p, h = _norm_matmul(f"in_proj_{l}", x, prm["attn_norm"][l], w_in, F32, deps=wg.deps())
    if l == 0:
        wg.prefetch_after("w_in", l, p)
    y_a = _sgu_fwd(f"sgu_fwd_{l}", p, prm["sgu_wt"][l], prm["sgu_bb"][l])
    y_b = _conv_fwd(f"conv_fwd_{l}", p, prm["conv_w"][l])
    os, lses = [], []
    for g in range(N_PATTERNS):
        o_g, lse_g = _attn_fwd(f"attn_fwd_{l}_{g}", p, g, prm["q_gain"][l], prm["k_gain"][l], prm["bd"])
        os.append(o_g)
        lses.append(lse_g)
    y_c = _mix_fwd(f"mix_fwd_{l}", os, lses)
    ycat = jnp.concatenate([y_a, y_b, y_c], axis=1)
    tmb, tnb = min(1024, S), min(1024, D)
    w_out = wg.get("w_out", l, ycat)
    kq = N_CHIPS * w_out.shape[1]
    x1 = _matmul(
        f"out_proj_{l}", ycat, w_out.reshape(kq, D), (S, D), F32, grid=(S // tmb, D // tnb, 1),
        a_spec=pl.BlockSpec((tmb, kq), lambda i, j, k: (i, 0)),
        b_spec=pl.BlockSpec((kq, tnb), lambda i, j, k: (0, j)),
        o_spec=pl.BlockSpec((tmb, tnb), lambda i, j, k: (i, j)),
        contract=(1, 0), acc_shape=(tmb, tnb),
        extras=(x,), extra_specs=(pl.BlockSpec((tmb, tnb), lambda i, j, k: (i, j)),),
        epi=lambda r, res: r + res, deps=wg.deps())
    w_mlp_in = wg.get("w_mlp_in", l, x1)
    a, h2, r = _norm_matmul(f"mlp_in_{l}", x1, prm["mlp_norm"][l], w_mlp_in, BF16, deps=wg.deps(), relu2_out=True)
    w_mlp_out = wg.get("w_mlp_out", l, a)
    dff4 = w_mlp_out.shape[1]
    tk = min(2048, dff4)
    kpc = dff4 // tk
    x2 = _matmul(
        f"mlp_out_{l}", r, w_mlp_out, (S, D), F32, grid=(S // tmb, D // tnb, N_CHIPS * kpc),
        a_spec=pl.BlockSpec((tmb, tk), lambda i, j, k: (i, k)),
        b_spec=pl.BlockSpec((None, tk, tnb), lambda i, j, k: (k // kpc, k % kpc, j)),
        o_spec=pl.BlockSpec((tmb, tnb), lambda i, j, k: (i, j)),
        contract=(1, 0), acc_shape=(tmb, tnb),
        extras=(x1,), extra_specs=(pl.BlockSpec((tmb, tnb), lambda i, j, k: (i, j)),),
        epi=lambda acc, res: acc + res, deps=wg.deps())
    saved = dict(x=x, p=p, h=h, os=os, lses=lses, ycat=ycat, x1=x1, a=a, r=r, h2=h2)
    return x2, saved


def _layer_backward(l, dx2, dx2b, sv, prm, wg, sink):
    S, D = dx2.shape
    w_in, w_out = wg.get("w_in", l), wg.get("w_out", l)
    w_mlp_in, w_mlp_out = wg.get("w_mlp_in", l), wg.get("w_mlp_out", l)
    dff4 = w_mlp_in.shape[-1]
    dff = N_CHIPS * dff4
    tm = min(512, S)
    tk = min(1024, S)
    nks = S // tk

    tmb, tnb = min(1024, S), min(1024, D)
    da = _matmul(
        f"mlp_out_bwd_{l}", dx2b, w_mlp_out, (S, dff), BF16, grid=(S // tmb, N_CHIPS, 1),
        a_spec=pl.BlockSpec((tmb, D), lambda i, j, k: (i, 0)),
        b_spec=pl.BlockSpec((None, dff4, D), lambda i, j, k: (j, 0, 0)),
        o_spec=pl.BlockSpec((tmb, dff4), lambda i, j, k: (i, j)),
        contract=(1, 1), acc_shape=(tmb, dff4),
        extras=(sv["a"],), extra_specs=(pl.BlockSpec((tmb, dff4), lambda i, j, k: (i, j)),),
        epi=lambda r, act: r * (2.0 * jnp.maximum(act.astype(F32), 0.0)), deps=sink.deps())
    tmw = min(1024, dff4)
    mpc = dff4 // tmw
    g_w2 = _matmul(
        f"mlp_out_dw_{l}", sv["r"], dx2b, (N_CHIPS, dff4, D), F32, grid=(N_CHIPS * mpc, D // tnb, 1),
        a_spec=pl.BlockSpec((S, tmw), lambda i, j, k: (0, i)),
        b_spec=pl.BlockSpec((S, tnb), lambda i, j, k: (0, j)),
        o_spec=pl.BlockSpec((None, tmw, tnb), lambda i, j, k: (i // mpc, i % mpc, j)),
        contract=(0, 0), acc_shape=(tmw, tnb))
    sink.begin("w_mlp_out", l, g_w2)
    dh2 = _matmul(
        f"mlp_in_bwd_{l}", da, w_mlp_in, (S, D), F32, grid=(S // tmb, D // tnb, N_CHIPS),
        a_spec=pl.BlockSpec((tmb, dff4), lambda i, j, k: (i, k)),
        b_spec=pl.BlockSpec((None, tnb, dff4), lambda i, j, k: (k, j, 0)),
        o_spec=pl.BlockSpec((tmb, tnb), lambda i, j, k: (i, j)),
        contract=(1, 1), acc_shape=(tmb, tnb), deps=sink.deps())
    sink.advance(dh2)
    tmd = min(1024, D)
    nd = D // tmd
    tnf = min(1024, dff4)
    nf = dff4 // tnf
    g_w1 = _matmul(
        f"mlp_in_dw_{l}", sv["h2"], da, (N_CHIPS, D, dff4), F32, grid=(N_CHIPS * nd, nf, 1),
        a_spec=pl.BlockSpec((S, tmd), lambda i, j, k: (0, i % nd)),
        b_spec=pl.BlockSpec((S, tnf), lambda i, j, k: (0, (i // nd) * nf + j)),
        o_spec=pl.BlockSpec((None, tmd, tnf), lambda i, j, k: (i // nd, i % nd, j)),
        contract=(0, 0), acc_shape=(tmd, tnf))
    sink.begin("w_mlp_in", l, g_w1)
    dx1, dx1b, g_mlp_norm = _rmsnorm_bwd(f"mlp_norm_bwd_{l}", dh2, sv["x1"], prm["mlp_norm"][l], dx2,
                                         deps=sink.deps())

    rq = w_out.shape[1]
    dycat = _matmul(
        f"out_proj_bwd_{l}", dx1b, w_out, (S, N_CHIPS * rq), F32, grid=(S // tmb, N_CHIPS, 1),
        a_spec=pl.BlockSpec((tmb, D), lambda i, j, k: (i, 0)),
        b_spec=pl.BlockSpec((None, rq, D), lambda i, j, k: (j, 0, 0)),
        o_spec=pl.BlockSpec((tmb, rq), lambda i, j, k: (i, j)),
        contract=(1, 1), acc_shape=(tmb, rq))
    sink.advance(dycat)
    g_wout = _matmul(
        f"out_proj_dw_{l}", sv["ycat"], dx1b, (N_CHIPS, rq, D), F32, grid=(N_CHIPS, D // tnb, 1),
        a_spec=pl.BlockSpec((S, rq), lambda i, j, k: (0, i)),
        b_spec=pl.BlockSpec((S, tnb), lambda i, j, k: (0, j)),
        o_spec=pl.BlockSpec((None, rq, tnb), lambda i, j, k: (i, 0, j)),
        contract=(0, 0), acc_shape=(rq, tnb))
    sink.begin("w_out", l, g_wout)

    p = sv["p"]
    du, dv_a, g_sgu_w, db_lanes = _sgu_bwd(f"sgu_bwd_{l}", p, dycat, prm["sgu_wt"][l], prm["sgu_wtt"][l],
                                           prm["sgu_bb"][l])
    g_sgu_b = db_lanes[:, :A_HEADS].T
    db, dc, dxb, g_conv = _conv_bwd(f"conv_bwd_{l}", p, dycat, prm["conv_w"][l])
    do3, c3 = _mix_bwd(f"mix_bwd_{l}", sv["os"], sv["lses"], dycat, prm["bd"])
    dqs, dks, dvs, dgqs, dgks = [], [], [], [], []
    for g in range(N_PATTERNS):
        dq, dk, dv, dgq, dgk = _attn_bwd(f"attn_bwd_{l}_{g}", p, g, sv["lses"][g], do3, c3,
                                         prm["q_gain"][l], prm["k_gain"][l], prm["bd"])
        dqs.append(dq)
        dks.append(dk)
        dvs.append(dv)
        dgqs.append(dgq)
        dgks.append(dgk)
    g_q = jnp.concatenate(dgqs, axis=1).reshape(N_PATTERNS * PW // HEAD_DIM, HEAD_DIM).sum(axis=0)
    g_k = jnp.concatenate(dgks, axis=1).reshape(N_PATTERNS * PW // HEAD_DIM, HEAD_DIM).sum(axis=0)
    dp = jnp.concatenate([du, dv_a, db, dc, dxb] + [t.astype(BF16) for t in dqs + dks + dvs], axis=1)

    ns_in = w_in.shape[-1]
    tmh = min(512, D)
    nh = D // tmh
    g_win = _matmul(
        f"in_proj_dw_{l}", sv["h"], dp, (N_CHIPS, D, ns_in), F32, grid=(N_CHIPS * nh, 1, 1),
        a_spec=pl.BlockSpec((S, tmh), lambda i, j, k: (0, i % nh)),
        b_spec=pl.BlockSpec((S, ns_in), lambda i, j, k: (0, i // nh)),
        o_spec=pl.BlockSpec((None, tmh, ns_in), lambda i, j, k: (i // nh, i % nh, 0)),
        contract=(0, 0), acc_shape=(tmh, ns_in))
    sink.begin("w_in", l, g_win)
    dh = _matmul(
        f"in_proj_bwd_{l}", dp, w_in, (S, D), F32, grid=(S // tmb, D // tnb, N_CHIPS),
        a_spec=pl.BlockSpec((tmb, ns_in), lambda i, j, k: (i, k)),
        b_spec=pl.BlockSpec((None, tnb, ns_in), lambda i, j, k: (k, j, 0)),
        o_spec=pl.BlockSpec((tmb, tnb), lambda i, j, k: (i, j)),
        contract=(1, 1), acc_shape=(tmb, tnb), deps=sink.deps())
    sink.advance(dh)
    dx0, dx0b, g_attn_norm = _rmsnorm_bwd(f"attn_norm_bwd_{l}", dh, sv["x"], prm["attn_norm"][l], dx1,
                                          deps=sink.deps())

    big = dict(w_in=g_win, w_out=g_wout, w_mlp_in=g_w1, w_mlp_out=g_w2)
    small = dict(attn_norm=g_attn_norm.reshape(-1), sgu_w=g_sgu_w, sgu_b=g_sgu_b, conv_w=g_conv,
                 q_norm=g_q, k_norm=g_k, mlp_norm=g_mlp_norm.reshape(-1))
    return dx0, dx0b, big, small


BIG = ("w_in", "w_out", "w_mlp_in", "w_mlp_out")
SMALL_REPLICATED = ("attn_norm", "sgu_w", "sgu_b", "q_norm", "k_norm", "mlp_norm")


def _local_step(x, target, prm, wg, n_layers, sink):
    saved = []
    h = x
    for l in range(n_layers):
        h, sv = _layer_forward(l, h, prm, wg)
        saved.append(sv)
    dy, dyb, colsq = _loss_kernel(h, target)
    loss = 0.5 * jnp.sum(colsq) / x.shape[1]
    bigs, smalls = [None] * n_layers, [None] * n_layers
    for l in reversed(range(n_layers)):
        dy, dyb, bigs[l], smalls[l] = _layer_backward(l, dy, dyb, saved[l], prm, wg, sink)
    return loss, dy, bigs, smalls


def _prepare_params(attn_norm, sgu_w, sgu_b, conv_full, q_norm, k_norm, mlp_norm):
    n_layers = attn_norm.shape[0]
    tri = jnp.tril(sgu_w)
    idx = jnp.arange(PW)
    bd = (idx[:, None] // HEAD_DIM == idx[None, :] // HEAD_DIM).astype(BF16)
    return dict(
        attn_norm=[attn_norm[l][None, :] for l in range(n_layers)],
        mlp_norm=[mlp_norm[l][None, :] for l in range(n_layers)],
        sgu_wt=[tri[l].astype(BF16) for l in range(n_layers)],
        sgu_wtt=[tri[l].transpose(0, 2, 1).astype(BF16) for l in range(n_layers)],
        sgu_bb=[jnp.repeat(sgu_b[l].T, HEAD_DIM, axis=1) for l in range(n_layers)],
        conv_w=[conv_full[l] for l in range(n_layers)],
        q_gain=[jnp.tile(q_norm[l], PW // HEAD_DIM)[None, :] for l in range(n_layers)],
        k_gain=[jnp.tile(k_norm[l], PW // HEAD_DIM)[None, :] for l in range(n_layers)],
        bd=bd,
    )


def kernel(x, attn_norm, w_in, sgu_w, sgu_b, conv_w, q_norm, k_norm, w_out, mlp_norm, w_mlp_in, w_mlp_out, loss_target, m_attn_norm, m_w_in, m_sgu_w, m_sgu_b, m_conv_w, m_q_norm, m_k_norm, m_w_out, m_mlp_norm, m_w_mlp_in, m_w_mlp_out, v_attn_norm, v_w_in, v_sgu_w, v_sgu_b, v_conv_w, v_q_norm, v_k_norm, v_w_out, v_mlp_norm, v_w_mlp_in, v_w_mlp_out):
    n_layers = attn_norm.shape[0]
    weights = dict(attn_norm=attn_norm, w_in=w_in, sgu_w=sgu_w, sgu_b=sgu_b, conv_w=conv_w, q_norm=q_norm,
                   k_norm=k_norm, w_out=w_out, mlp_norm=mlp_norm, w_mlp_in=w_mlp_in, w_mlp_out=w_mlp_out)
    mom_m = dict(attn_norm=m_attn_norm, w_in=m_w_in, sgu_w=m_sgu_w, sgu_b=m_sgu_b, conv_w=m_conv_w,
                 q_norm=m_q_norm, k_norm=m_k_norm, w_out=m_w_out, mlp_norm=m_mlp_norm, w_mlp_in=m_w_mlp_in,
                 w_mlp_out=m_w_mlp_out)
    mom_v = dict(attn_norm=v_attn_norm, w_in=v_w_in, sgu_w=v_sgu_w, sgu_b=v_sgu_b, conv_w=v_conv_w,
                 q_norm=v_q_norm, k_norm=v_k_norm, w_out=v_w_out, mlp_norm=v_mlp_norm, w_mlp_in=v_w_mlp_in,
                 w_mlp_out=v_w_mlp_out)
    order = ("attn_norm", "w_in", "sgu_w", "sgu_b", "conv_w", "q_norm", "k_norm", "w_out", "mlp_norm",
             "w_mlp_in", "w_mlp_out")
    chip = 2 * lax.axis_index("x") + lax.axis_index("y")
    c_arr = jnp.stack([lax.axis_index("c"), chip]).astype(jnp.int32)

    conv_cols = conv_w.shape[-1]
    chip_arr = chip.astype(jnp.int32).reshape(1)
    conv_pack = jnp.pad(conv_w.reshape(-1), (0, 2048 - conv_w.size)).reshape(1, 16, 128)
    wg = _GatheredWeights()
    wg.start([("conv_w", 0), ("w_in", 0)],
             [_place_shard("place_conv_w", conv_pack, 0, chip_arr, F32),
              _place_shard("place_w_in_0", weights["w_in"], 0, chip_arr, BF16)])
    keys = [(n, l) for l in range(n_layers) for n in BIG if (n, l) != ("w_in", 0)]
    first = wg.deps()
    wg.start(keys, [_place_shard(f"place_{n}_{l}", weights[n], l, chip_arr, BF16, deps=first) for n, l in keys])
    conv_full = wg.get("conv_w", 0, wg.deps()[-1]).reshape(N_CHIPS, 2048)[:, :conv_w.size].reshape(N_CHIPS, n_layers, 3, conv_cols)
    conv_full = conv_full.transpose(1, 2, 0, 3).reshape(n_layers, 3, N_CHIPS * conv_cols)
    prm = _prepare_params(attn_norm, sgu_w, sgu_b, conv_full, q_norm, k_norm, mlp_norm)

    sink = _GradReducer(c_arr)
    loss_local, grad_x, _, smalls = _local_step(x[0], loss_target[0], prm, wg, n_layers, sink)
    loss = lax.psum(loss_local, ("x", "y", "c"))

    small_names = SMALL_REPLICATED + ("conv_w",)
    small_shapes = [(n_layers,) + tuple(smalls[0][n].shape) for n in small_names]
    packed = _pack_rows([jnp.stack([smalls[l][n] for l in range(n_layers)]) for n in small_names])
    small_send, small_recv, small_land, small_token = _small_start(packed, sink.deps())

    grads, delta, new_m, new_v = {}, {}, {}, {}

    def update(n, after):
        shp = weights[n].shape
        two_d = (shp[0] * shp[1], shp[2])
        d, nm, nv, g = _adamw(f"adamw_{n}", weights[n].reshape(two_d), sink.reduced(n, after).reshape(two_d),
                              mom_m[n].reshape(two_d), mom_v[n].reshape(two_d))
        grads[n], delta[n], new_m[n], new_v[n] = g.reshape(shp), d.reshape(shp), nm.reshape(shp), nv.reshape(shp)

    token = small_token
    for n in ("w_mlp_out", "w_mlp_in", "w_out"):
        token = sink.reduce(n, n_layers, token)
    update("w_mlp_out", token)
    token = sink.reduce("w_in", n_layers, delta["w_mlp_out"])
    update("w_mlp_in", token)
    update("w_out", delta["w_mlp_in"])
    update("w_in", delta["w_out"])
    small_land = _small_wait(packed, small_land, small_send, small_recv, delta["w_in"])
    grads.update(zip(small_names, _unpack_rows(_sum_devices(small_land), small_shapes)))
    grads["conv_w"] = lax.dynamic_slice_in_dim(grads["conv_w"], chip * conv_cols, conv_cols, axis=2)
    smalls_all = SMALL_REPLICATED + ("conv_w",)
    shapes = [weights[n].shape for n in smalls_all]
    d, nm, nv, _ = _adamw("adamw_small",
                          _pack_rows([weights[n] for n in smalls_all]), _pack_rows([grads[n] for n in smalls_all]),
                          _pack_rows([mom_m[n] for n in smalls_all]), _pack_rows([mom_v[n] for n in smalls_all]))
    for n, dd, mm, vv in zip(smalls_all, _unpack_rows(d, shapes), _unpack_rows(nm, shapes), _unpack_rows(nv, shapes)):
        delta[n], new_m[n], new_v[n] = dd, mm, vv

    return (loss, grad_x[None], *[grads[n] for n in order], *[delta[n] for n in order],
            *[new_m[n] for n in order], *[new_v[n] for n in order])
```

```python
import jax
import jax.numpy as jnp
from jax import lax
from jax.experimental import pallas as pl
from jax.experimental.pallas import tpu as pltpu

F32 = jnp.float32
BF16 = jnp.bfloat16
SDS = jax.ShapeDtypeStruct

EPS = 1e-6
HEAD_DIM = 64
A_HEADS = 8
A_WIDTH = 512
CHUNK = 128
B_WIDTH = 768
C_WIDTH = 768
N_PATTERNS = 3
PATTERN_DILATION = (1, 4, 16)
PW = 256
D_IN_PROJ = 5632
OFF_AU, OFF_AV, OFF_BB, OFF_BC, OFF_BX, OFF_Q, OFF_K, OFF_V = 0, 512, 1024, 1792, 2560, 3328, 4096, 4864
N_CHIPS = 4
N_DEV = 8
BLK = 128

ADAM_LR, ADAM_B1, ADAM_B2, ADAM_EPS, ADAM_WD, ADAM_STEP = 0.001, 0.9, 0.999, 1e-08, 0.01, 10

V7X_VMEM_LIMIT = 56 * 1024 * 1024
MESH = pl.DeviceIdType.MESH
NEG = -1e30


def _cp(n_axes):
    return pltpu.CompilerParams(dimension_semantics=("arbitrary",) * n_axes, vmem_limit_bytes=V7X_VMEM_LIMIT)


def _hbm_spec():
    return pl.BlockSpec(memory_space=pl.ANY)


def _norm_matmul(name, x, g, wg, out_dtype, deps=(), post=None):
    S, D = x.shape
    ns, _, Ns = wg.shape
    tm = min(512, S)
    n_dep = len(deps)

    def body(x_ref, g_ref, w_ref, *rest):
        o_ref, h_ref, hs_ref = rest[n_dep:]

        @pl.when(pl.program_id(1) == 0)
        def _():
            xv = x_ref[...]
            y = xv * lax.rsqrt(jnp.mean(xv * xv, axis=-1, keepdims=True) + EPS) * g_ref[...]
            hb = y.astype(BF16)
            hs_ref[...] = hb
            h_ref[...] = hb
        acc = jnp.dot(hs_ref[...], w_ref[...], preferred_element_type=F32)
        o_ref[...] = (acc if post is None else post(acc)).astype(o_ref.dtype)

    return pl.pallas_call(
        body, name=name, grid=(S // tm, ns),
        in_specs=[pl.BlockSpec((tm, D), lambda i, s: (i, 0)),
                  pl.BlockSpec((1, D), lambda i, s: (0, 0)),
                  pl.BlockSpec((None, D, Ns), lambda i, s: (s, 0, 0))] + [_hbm_spec()] * n_dep,
        out_specs=[pl.BlockSpec((tm, Ns), lambda i, s: (i, s)),
                   pl.BlockSpec((tm, D), lambda i, s: (i, 0))],
        out_shape=[SDS((S, ns * Ns), out_dtype), SDS((S, D), BF16)],
        scratch_shapes=[pltpu.VMEM((tm, D), BF16)],
        compiler_params=_cp(2),
    )(x, g, wg, *deps)


def _relu2(t):
    r = jnp.maximum(t, 0.0)
    return r * r


def _matmul(name, a, b, out_shape, out_dtype, *, grid, a_spec, b_spec, o_spec, contract, acc_shape,
            extras=(), extra_specs=(), epi=None, deps=()):
    nk = grid[2]
    n_ex = len(extras)
    n_dep = len(deps)
    dims = (((contract[0],), (contract[1],)), ((), ()))

    def product(a_ref, b_ref):
        return lax.dot_general(a_ref[...], b_ref[...], dims, preferred_element_type=F32)

    def finish(r, ex, o_ref):
        if epi is not None:
            r = epi(r, *[e[...] for e in ex])
        o_ref[...] = r.astype(o_ref.dtype)

    def body_single(a_ref, b_ref, *rest):
        finish(product(a_ref, b_ref), rest[:n_ex], rest[n_ex + n_dep])

    def body(a_ref, b_ref, *rest):
        ex = rest[:n_ex]
        o_ref = rest[n_ex + n_dep]
        acc_ref = rest[n_ex + n_dep + 1]
        k = pl.program_id(2)

        @pl.when(k == 0)
        def _():
            acc_ref[...] = product(a_ref, b_ref)

        @pl.when((k > 0) & (k < nk - 1))
        def _():
            acc_ref[...] += product(a_ref, b_ref)

        @pl.when(k == nk - 1)
        def _():
            finish(acc_ref[...] + product(a_ref, b_ref), ex, o_ref)

    return pl.pallas_call(
        body_single if nk == 1 else body, name=name, grid=grid,
        in_specs=[a_spec, b_spec, *extra_specs] + [_hbm_spec()] * n_dep,
        out_specs=o_spec,
        out_shape=SDS(out_shape, out_dtype),
        scratch_shapes=[] if nk == 1 else [pltpu.VMEM(acc_shape, F32)],
        compiler_params=_cp(3),
    )(a, b, *extras, *deps)


def _loss_kernel(y, t):
    S, D = y.shape
    tm = min(256, S)

    def body(y_ref, t_ref, dy_ref, dyb_ref, l_ref):
        @pl.when(pl.program_id(0) == 0)
        def _():
            l_ref[...] = jnp.zeros_like(l_ref)
        e = y_ref[...] - t_ref[...]
        l_ref[...] += jnp.sum(e * e, axis=0, keepdims=True)
        dy = e * (1.0 / D)
        dy_ref[...] = dy
        dyb_ref[...] = dy.astype(BF16)

    row = pl.BlockSpec((tm, D), lambda i: (i, 0))
    return pl.pallas_call(
        body, name="loss_head", grid=(S // tm,),
        in_specs=[row, row],
        out_specs=[row, row, pl.BlockSpec((1, D), lambda i: (0, 0))],
        out_shape=[SDS((S, D), F32), SDS((S, D), BF16), SDS((1, D), F32)],
        compiler_params=_cp(1),
    )(y, t)


def _rmsnorm_bwd(name, dh, x, g, dres, deps=()):
    S, D = x.shape
    tm = min(256, S)
    n_dep = len(deps)

    def body(dh_ref, x_ref, g_ref, dres_ref, *rest):
        dx_ref, dxb_ref, dg_ref = rest[n_dep:]
        @pl.when(pl.program_id(0) == 0)
        def _():
            dg_ref[...] = jnp.zeros_like(dg_ref)
        xv = x_ref[...]
        dhv = dh_ref[...]
        rstd = lax.rsqrt(jnp.mean(xv * xv, axis=-1, keepdims=True) + EPS)
        xhat = xv * rstd
        dg_ref[...] += jnp.sum(dhv * xhat, axis=0, keepdims=True)
        dxn = dhv * g_ref[...]
        dx = dres_ref[...] + rstd * (dxn - xhat * jnp.mean(dxn * xhat, axis=-1, keepdims=True))
        dx_ref[...] = dx
        dxb_ref[...] = dx.astype(BF16)

    row = pl.BlockSpec((tm, D), lambda i: (i, 0))
    vec = pl.BlockSpec((1, D), lambda i: (0, 0))
    return pl.pallas_call(
        body, name=name, grid=(S // tm,),
        in_specs=[row, row, vec, row] + [_hbm_spec()] * n_dep,
        out_specs=[row, row, vec],
        out_shape=[SDS((S, D), F32), SDS((S, D), BF16), SDS((1, D), F32)],
        compiler_params=_cp(1),
    )(dh, x, g, dres, *deps)


def _adamw(name, w, g, m, v):
    R, C = w.shape
    tr = 256 if R % 256 == 0 else R
    c1 = 1.0 - ADAM_B1 ** ADAM_STEP
    c2 = 1.0 - ADAM_B2 ** ADAM_STEP

    def body(w_ref, g_ref, m_ref, v_ref, d_ref, nm_ref, nv_ref, g_out_ref):
        gv = g_ref[...]
        nm = ADAM_B1 * m_ref[...] + (1.0 - ADAM_B1) * gv
        nv = ADAM_B2 * v_ref[...] + (1.0 - ADAM_B2) * (gv * gv)
        m_hat = nm / c1
        v_hat = nv / c2
        d_ref[...] = -ADAM_LR * (m_hat / (jnp.sqrt(v_hat) + ADAM_EPS) + ADAM_WD * w_ref[...])
        nm_ref[...] = nm
        nv_ref[...] = nv
        g_out_ref[...] = gv

    blk = pl.BlockSpec((tr, C), lambda i: (i, 0))
    return pl.pallas_call(
        body, name=name, grid=(R // tr,),
        in_specs=[blk] * 4, out_specs=[blk] * 4,
        out_shape=[SDS((R, C), F32)] * 4,
        compiler_params=_cp(1),
    )(w, g, m, v)


SGU_STEP_ROWS = 512


def _pair_select(lane, lo, hi):
    return jnp.where(lane < HEAD_DIM, lo, hi)


def _sgu_fwd(name, p, wt, bb):
    S = p.shape[0]

    rows = min(SGU_STEP_ROWS, S)

    def body(u_ref, v_ref, wt_ref, bb_ref, o_ref):
        lane = lax.broadcasted_iota(jnp.int32, (CHUNK, 128), 1)
        for ci in range(rows // CHUNK):
            rs = slice(CHUNK * ci, CHUNK * (ci + 1))
            for pp in range(A_HEADS // 2):
                cs = slice(128 * pp, 128 * (pp + 1))
                vb = v_ref[rs, cs].astype(BF16)
                mixed = _pair_select(lane,
                                     jnp.dot(wt_ref[2 * pp], vb, preferred_element_type=F32),
                                     jnp.dot(wt_ref[2 * pp + 1], vb, preferred_element_type=F32)) + bb_ref[:, cs]
                o_ref[rs, cs] = (u_ref[rs, cs] * mixed).astype(o_ref.dtype)

    return pl.pallas_call(
        body, name=name, grid=(S // rows,),
        in_specs=[pl.BlockSpec((rows, A_WIDTH), lambda c: (c, OFF_AU // A_WIDTH)),
                  pl.BlockSpec((rows, A_WIDTH), lambda c: (c, OFF_AV // A_WIDTH)),
                  pl.BlockSpec((A_HEADS, CHUNK, CHUNK), lambda c: (0, 0, 0)),
                  pl.BlockSpec((CHUNK, A_WIDTH), lambda c: (0, 0))],
        out_specs=pl.BlockSpec((rows, A_WIDTH), lambda c: (c, 0)),
        out_shape=SDS((S, A_WIDTH), BF16),
        compiler_params=_cp(1),
    )(p, p, wt, bb)


def _sgu_bwd(name, p, dycat, wt, wtt, bb):
    S = p.shape[0]
    rows = min(SGU_STEP_ROWS, S)

    def body(u_ref, v_ref, dy_ref, wt_ref, wtt_ref, bb_ref, du_ref, dv_ref, dw_ref, db_ref, dbacc_ref):
        c = pl.program_id(0)

        @pl.when(c == 0)
        def _():
            dw_ref[...] = jnp.zeros_like(dw_ref)
            dbacc_ref[...] = jnp.zeros_like(dbacc_ref)

        lane = lax.broadcasted_iota(jnp.int32, (CHUNK, 128), 1)
        row = lax.broadcasted_iota(jnp.int32, (CHUNK, 128), 0)
        causal = row >= lane
        nt = (((1,), (1,)), ((), ()))
        for pp in range(A_HEADS // 2):
            cs = slice(128 * pp, 128 * (pp + 1))
            dw_lo = jnp.zeros((CHUNK, CHUNK), F32)
            dw_hi = jnp.zeros((CHUNK, CHUNK), F32)
            dm_sum = jnp.zeros((CHUNK, 128), F32)
            for ci in range(rows // CHUNK):
                rs = slice(CHUNK * ci, CHUNK * (ci + 1))
                vb = v_ref[rs, cs].astype(BF16)
                dy = dy_ref[rs, cs]
                mixed = _pair_select(lane,
                                     jnp.dot(wt_ref[2 * pp], vb, preferred_element_type=F32),
                                     jnp.dot(wt_ref[2 * pp + 1], vb, preferred_element_type=F32)) + bb_ref[:, cs]
                du_ref[rs, cs] = (dy * mixed).astype(du_ref.dtype)
                dm = dy * u_ref[rs, cs]
                dmb = dm.astype(BF16)
                dv = _pair_select(lane,
                                  jnp.dot(wtt_ref[2 * pp], dmb, preferred_element_type=F32),
                                  jnp.dot(wtt_ref[2 * pp + 1], dmb, preferred_element_type=F32))
                dv_ref[rs, cs] = dv.astype(dv_ref.dtype)
                dm_sum += dm
                dm_lo = jnp.where(lane < HEAD_DIM, dm, 0.0).astype(BF16)
                dm_hi = jnp.where(lane >= HEAD_DIM, dm, 0.0).astype(BF16)
                dw_lo += lax.dot_general(dm_lo, vb, nt, preferred_element_type=F32)
                dw_hi += lax.dot_general(dm_hi, vb, nt, preferred_element_type=F32)
            dbacc_ref[:, cs] += dm_sum
            dw_ref[2 * pp] += jnp.where(causal, dw_lo, 0.0)
            dw_ref[2 * pp + 1] += jnp.where(causal, dw_hi, 0.0)

        @pl.when(c == S // rows - 1)
        def _():
            out = jnp.zeros((CHUNK, 128), F32)
            for pp in range(A_HEADS // 2):
                acc = dbacc_ref[:, 128 * pp:128 * (pp + 1)]
                s_lo = jnp.sum(jnp.where(lane < HEAD_DIM, acc, 0.0), axis=1, keepdims=True)
                s_hi = jnp.sum(jnp.where(lane >= HEAD_DIM, acc, 0.0), axis=1, keepdims=True)
                out = jnp.where(lane == 2 * pp, s_lo, out)
                out = jnp.where(lane == 2 * pp + 1, s_hi, out)
            db_ref[...] = out

    chunk = lambda col: pl.BlockSpec((rows, A_WIDTH), lambda c: (c, col))
    wspec = pl.BlockSpec((A_HEADS, CHUNK, CHUNK), lambda c: (0, 0, 0))
    return pl.pallas_call(
        body, name=name, grid=(S // rows,),
        in_specs=[chunk(OFF_AU // A_WIDTH), chunk(OFF_AV // A_WIDTH), chunk(0), wspec, wspec,
                  pl.BlockSpec((CHUNK, A_WIDTH), lambda c: (0, 0))],
        out_specs=[chunk(0), chunk(0), wspec, pl.BlockSpec((CHUNK, 128), lambda c: (0, 0))],
        out_shape=[SDS((S, A_WIDTH), BF16), SDS((S, A_WIDTH), BF16),
                   SDS((A_HEADS, CHUNK, CHUNK), F32), SDS((CHUNK, 128), F32)],
        scratch_shapes=[pltpu.VMEM((CHUNK, A_WIDTH), F32)],
        compiler_params=_cp(1),
    )(p, p, dycat, wt, wtt, bb)


CONV_HALO = 8
CONV_COLS = 256
CONV_ROWS = 1024


def _shift_down(a, halo, k):
    T = a.shape[0]
    row = lax.broadcasted_iota(jnp.int32, a.shape, 0)
    out = pltpu.roll(a, k, 0)
    for r in range(k):
        out = jnp.where(row == r, halo[CONV_HALO - k + r:CONV_HALO - k + r + 1, :], out)
    return out


def _shift_up(a, halo, k):
    T = a.shape[0]
    row = lax.broadcasted_iota(jnp.int32, a.shape, 0)
    out = pltpu.roll(a, T - k, 0)
    for r in range(k):
        out = jnp.where(row == T - k + r, halo[r:r + 1, :], out)
    return out


def _conv_specs(S, T):
    hb = T // CONV_HALO
    last = S // CONV_HALO - 1
    tile = lambda col0: pl.BlockSpec((T, CONV_COLS), lambda j, i: (i, col0 + j))
    prev = lambda col0: pl.BlockSpec((CONV_HALO, CONV_COLS), lambda j, i: (jnp.maximum(i * hb - 1, 0), col0 + j))
    nxt = lambda col0: pl.BlockSpec((CONV_HALO, CONV_COLS), lambda j, i: (jnp.minimum((i + 1) * hb, last), col0 + j))
    return tile, prev, nxt


def _conv_fwd(name, p, w):
    S = p.shape[0]
    T = min(CONV_ROWS, S)
    tile, prev, _ = _conv_specs(S, T)
    cb, cc, cx = OFF_BB // CONV_COLS, OFF_BC // CONV_COLS, OFF_BX // CONV_COLS

    def body(b_ref, c_ref, x_ref, ch_ref, xh_ref, w_ref, o_ref):
        i = pl.program_id(1)
        z = c_ref[...] * x_ref[...]
        zh = jnp.where(i > 0, ch_ref[...] * xh_ref[...], 0.0)
        z1 = _shift_down(z, zh, 1)
        z2 = _shift_down(z, zh, 2)
        conv = w_ref[0:1, :] * z2 + w_ref[1:2, :] * z1 + w_ref[2:3, :] * z
        o_ref[...] = (b_ref[...] * conv).astype(o_ref.dtype)

    return pl.pallas_call(
        body, name=name, grid=(B_WIDTH // CONV_COLS, S // T),
        in_specs=[tile(cb), tile(cc), tile(cx), prev(cc), prev(cx),
                  pl.BlockSpec((3, CONV_COLS), lambda j, i: (0, j))],
        out_specs=tile(0),
        out_shape=SDS((S, B_WIDTH), BF16),
        compiler_params=_cp(2),
    )(p, p, p, p, p, w)


def _conv_bwd(name, p, dycat, w):
    S = p.shape[0]
    T = min(CONV_ROWS, S)
    tile, prev, nxt = _conv_specs(S, T)
    cb, cc, cx = OFF_BB // CONV_COLS, OFF_BC // CONV_COLS, OFF_BX // CONV_COLS
    cdy = A_WIDTH // CONV_COLS
    n_i = S // T

    def body(b_ref, c_ref, x_ref, dy_ref, ch_ref, xh_ref, bn_ref, dyn_ref, w_ref,
             db_ref, dc_ref, dx_ref, dw_ref):
        i = pl.program_id(1)

        @pl.when(i == 0)
        def _():
            dw_ref[...] = jnp.zeros_like(dw_ref)

        cv = c_ref[...]
        xv = x_ref[...]
        z = cv * xv
        zh = jnp.where(i > 0, ch_ref[...] * xh_ref[...], 0.0)
        z1 = _shift_down(z, zh, 1)
        z2 = _shift_down(z, zh, 2)
        w0, w1, w2 = w_ref[0:1, :], w_ref[1:2, :], w_ref[2:3, :]
        conv = w0 * z2 + w1 * z1 + w2 * z
        dy = dy_ref[...]
        db_ref[...] = (dy * conv).astype(db_ref.dtype)
        dconv = dy * b_ref[...]
        dconv_n = jnp.where(i < n_i - 1, dyn_ref[...] * bn_ref[...], 0.0)
        dz = w2 * dconv + w1 * _shift_up(dconv, dconv_n, 1) + w0 * _shift_up(dconv, dconv_n, 2)
        dc_ref[...] = (dz * xv).astype(dc_ref.dtype)
        dx_ref[...] = (dz * cv).astype(dx_ref.dtype)
        dw_ref[0:1, :] += jnp.sum(dconv * z2, axis=0, keepdims=True)
        dw_ref[1:2, :] += jnp.sum(dconv * z1, axis=0, keepdims=True)
        dw_ref[2:3, :] += jnp.sum(dconv * z, axis=0, keepdims=True)

    wspec = pl.BlockSpec((3, CONV_COLS), lambda j, i: (0, j))
    return pl.pallas_call(
        body, name=name, grid=(B_WIDTH // CONV_COLS, n_i),
        in_specs=[tile(cb), tile(cc), tile(cx), tile(cdy), prev(cc), prev(cx), nxt(cb), nxt(cdy), wspec],
        out_specs=[tile(0), tile(0), tile(0), wspec],
        out_shape=[SDS((S, B_WIDTH), BF16)] * 3 + [SDS((3, B_WIDTH), F32)],
        compiler_params=_cp(2),
    )(p, p, p, dycat, p, p, p, dycat, w)


def _seg_sum(t, bd):
    hi = t.astype(BF16)
    lo = (t - hi.astype(F32)).astype(BF16)
    return jnp.dot(hi, bd, preferred_element_type=F32) + jnp.dot(lo, bd, preferred_element_type=F32)


def _head_norm(x, g, bd):
    rstd = lax.rsqrt(_seg_sum(x * x, bd) * (1.0 / HEAD_DIM) + EPS)
    xhat = x * rstd
    return xhat * g, xhat, rstd


def _head_norm_bwd(dy, g, xhat, rstd, bd):
    dxh = dy * g
    return rstd * (dxh - xhat * (_seg_sum(dxh * xhat, bd) * (1.0 / HEAD_DIM)))


def _band_mask(has_prev):
    row = lax.broadcasted_iota(jnp.int32, (BLK, 2 * BLK), 0)
    col = lax.broadcasted_iota(jnp.int32, (BLK, 2 * BLK), 1)
    first_key = jnp.where(has_prev, 0, BLK)
    return (col >= row) & (col <= row + BLK) & (col >= first_key)


def _residue_rows(r, d):
    return slice(None) if d == 1 else pl.ds(r, BLK, stride=d)


STRIDED_LANES = 128


def _step_width(d):
    return PW if d == 1 else STRIDED_LANES


def _n_stack(lane):
    return lane.shape[1] // HEAD_DIM


def _for_residues(d, fn):
    if d == 1:
        fn(0)
    else:
        def two(i, carry):
            fn(2 * i)
            fn(2 * i + 1)
            return carry
        lax.fori_loop(0, d // 2, two, 0)


def _head_mask(lane, j):
    return (lane >= HEAD_DIM * j) & (lane < HEAD_DIM * (j + 1))


def _stack_heads(x, lane):
    return jnp.concatenate([jnp.where(_head_mask(lane, j), x, 0.0) for j in range(_n_stack(lane))], axis=0)


def _unstack_heads(y, lane):
    out = y[:BLK]
    for j in range(1, _n_stack(lane)):
        out = jnp.where(lane >= HEAD_DIM * j, y[BLK * j:BLK * (j + 1)], out)
    return out


def _head_columns(v, lane):
    return jnp.concatenate([jnp.max(jnp.where(_head_mask(lane, j), v, NEG), axis=1, keepdims=True)
                            for j in range(_n_stack(lane))], axis=0)


def _attn_fwd(name, p, g, gq, gk, bd):
    S = p.shape[0]
    d = PATTERN_DILATION[g]
    rows = BLK * d
    hw = _step_width(d)
    nt = (((1,), (1,)), ((), ()))

    def body(q_ref, kc_ref, kp_ref, vc_ref, vp_ref, gq_ref, gk_ref, bd_ref, o_ref, lse_ref):
        has_prev = pl.program_id(1) > 0
        bdv = bd_ref[...]
        band = jnp.concatenate([_band_mask(has_prev)] * (hw // HEAD_DIM), axis=0)
        lane = lax.broadcasted_iota(jnp.int32, (1, hw), 1)

        def residue(r):
            rr = _residue_rows(r, d)
            qn, _, _ = _head_norm(q_ref[rr, :], gq_ref[...], bdv)
            kn, _, _ = _head_norm(jnp.concatenate([kp_ref[rr, :], kc_ref[rr, :]], axis=0), gk_ref[...], bdv)
            knb = kn.astype(BF16)
            vb = jnp.concatenate([vp_ref[rr, :], vc_ref[rr, :]], axis=0).astype(BF16)
            qs = _stack_heads(qn, lane).astype(BF16)
            s = lax.dot_general(qs, knb, nt, preferred_element_type=F32) * (HEAD_DIM ** -0.5)
            s = jnp.where(band, s, NEG)
            m = jnp.max(s, axis=1, keepdims=True)
            e = jnp.exp(s - m)
            den = jnp.sum(e, axis=1, keepdims=True)
            pv = jnp.dot(e.astype(BF16), vb, preferred_element_type=F32)
            o_ref[rr, :] = _unstack_heads(pv / den, lane)
            lse_ref[rr, :] = _unstack_heads(jnp.broadcast_to(m + jnp.log(den), pv.shape), lane)

        _for_residues(d, residue)

    per = PW // hw
    cq, ck, cv = (OFF_Q + PW * g) // hw, (OFF_K + PW * g) // hw, (OFF_V + PW * g) // hw
    cur = lambda col: pl.BlockSpec((rows, hw), lambda h, n: (n, col + h))
    prv = lambda col: pl.BlockSpec((rows, hw), lambda h, n: (jnp.maximum(n - 1, 0), col + h))
    vec = pl.BlockSpec((1, hw), lambda h, n: (0, h))
    return pl.pallas_call(
        body, name=name, grid=(per, S // rows),
        in_specs=[cur(cq), cur(ck), prv(ck), cur(cv), prv(cv), vec, vec, pl.BlockSpec((hw, hw), lambda h, n: (0, 0))],
        out_specs=[cur(0), cur(0)],
        out_shape=[SDS((S, PW), F32)] * 2,
        compiler_params=_cp(2),
    )(p, p, p, p, p, gq, gk, bd)


def _attn_bwd(name, p, g, lse, do3, c3, gq, gk, bd):
    S = p.shape[0]
    d = PATTERN_DILATION[g]
    rows = BLK * d
    nblk = S // rows
    hw = _step_width(d)
    nt = (((1,), (1,)), ((), ()))
    tn = (((0,), (0,)), ((), ()))

    def body(q_ref, kc_ref, kp_ref, vc_ref, vp_ref, lse_ref, do_ref, c_ref, gq_ref, gk_ref, bd_ref,
             dq_ref, dk_ref, dv_ref, dgq_ref, dgk_ref, ck_ref, cv_ref, dq_keep_ref):
        n = pl.program_id(1)

        @pl.when(n == 0)
        def _():
            ck_ref[...] = jnp.zeros_like(ck_ref)
            cv_ref[...] = jnp.zeros_like(cv_ref)
            dgq_ref[...] = jnp.zeros_like(dgq_ref)
            dgk_ref[...] = jnp.zeros_like(dgk_ref)

        @pl.when(n == nblk)
        def _():
            dq_ref[...] = dq_keep_ref[...]
            dk_ref[...] = ck_ref[...]
            dv_ref[...] = cv_ref[...]

        bdv = bd_ref[...]
        gqv = gq_ref[...]
        gkv = gk_ref[...]
        band = jnp.concatenate([_band_mask(n > 0)] * (hw // HEAD_DIM), axis=0)
        lane = lax.broadcasted_iota(jnp.int32, (1, hw), 1)

        def residue(r):
            rr = _residue_rows(r, d)
            qn, qhat, qrstd = _head_norm(q_ref[rr, :], gqv, bdv)
            kn, khat, krstd = _head_norm(jnp.concatenate([kp_ref[rr, :], kc_ref[rr, :]], axis=0), gkv, bdv)
            knb = kn.astype(BF16)
            vb = jnp.concatenate([vp_ref[rr, :], vc_ref[rr, :]], axis=0).astype(BF16)
            qs = _stack_heads(qn, lane).astype(BF16)
            dos = _stack_heads(do_ref[rr, :], lane).astype(BF16)
            s = lax.dot_general(qs, knb, nt, preferred_element_type=F32) * (HEAD_DIM ** -0.5)
            prob = jnp.where(band, jnp.exp(s - _head_columns(lse_ref[rr, :], lane)), 0.0)
            dp = lax.dot_general(dos, vb, nt, preferred_element_type=F32)
            ds = (prob * (dp + _head_columns(c_ref[rr, :], lane)) * (HEAD_DIM ** -0.5)).astype(BF16)
            dqn = _unstack_heads(jnp.dot(ds, knb, preferred_element_type=F32), lane)
            dkn = lax.dot_general(ds, qs, tn, preferred_element_type=F32)
            dvv = lax.dot_general(prob.astype(BF16), dos, tn, preferred_element_type=F32)

            dq = _head_norm_bwd(dqn, gqv, qhat, qrstd, bdv)
            dq_ref[rr, :] = dq
            dq_keep_ref[rr, :] = dq
            dk2 = _head_norm_bwd(dkn, gkv, khat, krstd, bdv)
            dgq_ref[...] += jnp.sum(dqn * qhat, axis=0, keepdims=True)
            dgk_ref[...] += jnp.sum(dkn * khat, axis=0, keepdims=True)
            dk_ref[rr, :] = ck_ref[rr, :] + dk2[:BLK]
            dv_ref[rr, :] = cv_ref[rr, :] + dvv[:BLK]
            ck_ref[rr, :] = dk2[BLK:]
            cv_ref[rr, :] = dvv[BLK:]

        @pl.when(n < nblk)
        def _():
            _for_residues(d, residue)

    last = nblk - 1
    per = PW // hw
    cq, ck, cv = (OFF_Q + PW * g) // hw, (OFF_K + PW * g) // hw, (OFF_V + PW * g) // hw
    cur = lambda col: pl.BlockSpec((rows, hw), lambda h, n: (jnp.minimum(n, last), col + h))
    prv = lambda col: pl.BlockSpec((rows, hw), lambda h, n: (jnp.maximum(jnp.minimum(n, last) - 1, 0), col + h))
    cur3 = pl.BlockSpec((None, rows, hw), lambda h, n: (g, jnp.minimum(n, last), h))
    done = pl.BlockSpec((rows, hw), lambda h, n: (jnp.maximum(n - 1, 0), h))
    vec = pl.BlockSpec((1, hw), lambda h, n: (0, h))
    return pl.pallas_call(
        body, name=name, grid=(per, nblk + 1),
        in_specs=[cur(cq), cur(ck), prv(ck), cur(cv), prv(cv), cur(0), cur3, cur3, vec, vec,
                  pl.BlockSpec((hw, hw), lambda h, n: (0, 0))],
        out_specs=[cur(0), done, done, vec, vec],
        out_shape=[SDS((S, PW), F32)] * 3 + [SDS((1, PW), F32)] * 2,
        scratch_shapes=[pltpu.VMEM((rows, hw), F32)] * 3,
        compiler_params=_cp(2),
    )(p, p, p, p, p, lse, do3, c3, gq, gk, bd)


def _mix_fwd(name, os, lses):
    S = os[0].shape[0]
    tm = min(512, S)

    def body(o0, o1, o2, l0, l1, l2, y_ref):
        o = [o0[...], o1[...], o2[...]]
        l = [l0[...], l1[...], l2[...]]
        m = jnp.maximum(jnp.maximum(l[0], l[1]), l[2])
        e = [jnp.exp(t - m) for t in l]
        inv = 1.0 / (e[0] + e[1] + e[2])
        for g in range(N_PATTERNS):
            y_ref[:, PW * g:PW * (g + 1)] = (o[g] * (e[g] * inv)).astype(y_ref.dtype)

    blk = pl.BlockSpec((tm, PW), lambda i: (i, 0))
    return pl.pallas_call(
        body, name=name, grid=(S // tm,),
        in_specs=[blk] * 6,
        out_specs=pl.BlockSpec((tm, C_WIDTH), lambda i: (i, 0)),
        out_shape=SDS((S, C_WIDTH), BF16),
        compiler_params=_cp(1),
    )(*os, *lses)


def _mix_bwd(name, os, lses, dycat, bd):
    S = os[0].shape[0]
    tm = min(512, S)
    c0 = (A_WIDTH + B_WIDTH) // PW

    def body(o0, o1, o2, l0, l1, l2, dy0_ref, dy1_ref, dy2_ref, bd_ref, do_ref, c_ref):
        bdv = bd_ref[...]
        o = [o0[...], o1[...], o2[...]]
        l = [l0[...], l1[...], l2[...]]
        dys = [dy0_ref[...], dy1_ref[...], dy2_ref[...]]
        m = jnp.maximum(jnp.maximum(l[0], l[1]), l[2])
        e = [jnp.exp(t - m) for t in l]
        inv = 1.0 / (e[0] + e[1] + e[2])
        alpha = [t * inv for t in e]
        da = [_seg_sum(dys[g] * o[g], bdv) for g in range(N_PATTERNS)]
        mean_da = alpha[0] * da[0] + alpha[1] * da[1] + alpha[2] * da[2]
        for g in range(N_PATTERNS):
            do_ref[g] = dys[g] * alpha[g]
            c_ref[g] = -alpha[g] * mean_da

    blk = pl.BlockSpec((tm, PW), lambda i: (i, 0))
    blk3 = pl.BlockSpec((N_PATTERNS, tm, PW), lambda i: (0, i, 0))
    dyspec = lambda g: pl.BlockSpec((tm, PW), lambda i: (i, c0 + g))
    return pl.pallas_call(
        body, name=name, grid=(S // tm,),
        in_specs=[blk] * 6 + [dyspec(0), dyspec(1), dyspec(2), pl.BlockSpec((PW, PW), lambda i: (0, 0))],
        out_specs=[blk3, blk3],
        out_shape=[SDS((N_PATTERNS, S, PW), F32)] * 2,
        compiler_params=_cp(1),
    )(*os, *lses, dycat, dycat, dycat, bd)


def _mesh_pos():
    x, y, c = lax.axis_index("x"), lax.axis_index("y"), lax.axis_index("c")
    chips = [(1 - x, y), (x, 1 - y), (1 - x, 1 - y)]
    chip_idx = [2 * cx + cy for cx, cy in chips]
    return x, y, c, 2 * x + y, chips, chip_idx


def _place_shard(name, w, layer, chip_arr, out_dtype, deps=()):
    _, R, C = w.shape
    tr = min(256, R)

    def body(chip_ref, w_ref, *rest):
        o_ref = rest[-1]
        o_ref[...] = w_ref[...].astype(o_ref.dtype)

    return pl.pallas_call(
        body, name=name,
        grid_spec=pltpu.PrefetchScalarGridSpec(
            num_scalar_prefetch=1, grid=(R // tr,),
            in_specs=[pl.BlockSpec((None, tr, C), lambda i, chip_ref: (layer, i, 0))] + [_hbm_spec()] * len(deps),
            out_specs=pl.BlockSpec((None, tr, C), lambda i, chip_ref: (chip_ref[0], i, 0))),
        out_shape=SDS((N_CHIPS, R, C), out_dtype),
        compiler_params=_cp(1),
    )(chip_arr, w, *deps)


HBM_SPEC = pl.BlockSpec(memory_space=pltpu.HBM)
SEM_SPEC = pl.BlockSpec(memory_space=pltpu.SEMAPHORE)
SPLIT_COPY = pltpu.SideEffectType.DATAFLOW_SIDE_EFFECTING
N_PEER_CHIPS = N_CHIPS - 1
TOKEN_SHAPE = SDS((8, 128), F32)
TOKEN_SPEC = pl.BlockSpec(memory_space=pltpu.VMEM)


def _in_hbm(a):
    return pltpu.with_memory_space_constraint(a, pltpu.HBM)


def _gather_start(name, bufs):
    T = len(bufs)

    def body(*refs):
        ins = refs[:T]
        send_sems, recv_sems = refs[T:2 * T], refs[2 * T:3 * T]
        token = refs[4 * T]
        x, y, c, me, chips, chip_idx = _mesh_pos()
        for t in range(T):
            hr = ins[t].shape[1] // 2
            mine = ins[t].at[me, pl.ds(c * hr, hr), :]
            for j in range(N_PEER_CHIPS):
                pltpu.make_async_remote_copy(src_ref=mine, dst_ref=mine, send_sem=send_sems[t].at[j],
                                             recv_sem=recv_sems[t].at[j], device_id=(*chips[j], c),
                                             device_id_type=MESH).start()
        token[...] = jnp.zeros_like(token)

    sems = [pltpu.SemaphoreType.DMA((N_PEER_CHIPS,))] * T
    out = pl.pallas_call(
        body, name=name,
        in_specs=[HBM_SPEC] * T,
        out_specs=[SEM_SPEC] * (2 * T) + [HBM_SPEC] * T + [TOKEN_SPEC],
        out_shape=sems + sems + [pltpu.HBM(b.shape, b.dtype) for b in bufs] + [TOKEN_SHAPE],
        input_output_aliases={t: 2 * T + t for t in range(T)},
        compiler_params=pltpu.CompilerParams(has_side_effects=SPLIT_COPY),
    )(*[_in_hbm(b) for b in bufs])
    return out[:T], out[T:2 * T], out[2 * T:3 * T], out[3 * T]


def _gather_wait(name, buf, send_sem, recv_sem, after):
    n_in = 3 if after is None else 4

    def body(*refs):
        buf_ref, ssem, rsem = refs[:3]
        x, y, c, me, chips, chip_idx = _mesh_pos()
        hr = buf_ref.shape[1] // 2
        mine = buf_ref.at[me, pl.ds(c * hr, hr), :]
        for j in range(N_PEER_CHIPS):
            got = buf_ref.at[chip_idx[j], pl.ds(c * hr, hr), :]
            cp = pltpu.make_async_remote_copy(src_ref=mine, dst_ref=got, send_sem=ssem.at[j], recv_sem=rsem.at[j],
                                              device_id=(*chips[j], c), device_id_type=MESH)
            cp.wait_send()
            cp.wait_recv()

    args = [buf, send_sem, recv_sem] + ([] if after is None else [after])
    return pl.pallas_call(
        body, name=name,
        in_specs=[HBM_SPEC, SEM_SPEC, SEM_SPEC] + [_hbm_spec()] * (n_in - 3),
        out_specs=HBM_SPEC,
        out_shape=pltpu.HBM(buf.shape, buf.dtype),
        input_output_aliases={0: 0},
        compiler_params=pltpu.CompilerParams(has_side_effects=SPLIT_COPY),
    )(*args)


def _forward_start(name, buf):
    def body(buf_ref, send_sems, recv_sems, buf_thru, token):
        x, y, c, me, chips, chip_idx = _mesh_pos()
        hr = buf_ref.shape[1] // 2
        for j in range(N_PEER_CHIPS):
            got = buf_ref.at[chip_idx[j], pl.ds(c * hr, hr), :]
            pltpu.make_async_remote_copy(src_ref=got, dst_ref=got, send_sem=send_sems.at[j], recv_sem=recv_sems.at[j],
                                         device_id=(x, y, 1 - c), device_id_type=MESH).start()
        token[...] = jnp.zeros_like(token)

    sems = pltpu.SemaphoreType.DMA((N_PEER_CHIPS,))
    return pl.pallas_call(
        body, name=name,
        in_specs=[HBM_SPEC],
        out_specs=[SEM_SPEC, SEM_SPEC, HBM_SPEC, TOKEN_SPEC],
        out_shape=[sems, sems, pltpu.HBM(buf.shape, buf.dtype), TOKEN_SHAPE],
        input_output_aliases={0: 2},
        compiler_params=pltpu.CompilerParams(has_side_effects=SPLIT_COPY),
    )(_in_hbm(buf))


def _forward_wait(name, buf, send_sems, recv_sems, after):
    n_in = 3 if after is None else 4

    def body(*refs):
        buf_ref, ssems, rsems = refs[:3]
        x, y, c, me, chips, chip_idx = _mesh_pos()
        hr = buf_ref.shape[1] // 2
        for j in range(N_PEER_CHIPS):
            sent = buf_ref.at[chip_idx[j], pl.ds(c * hr, hr), :]
            theirs = buf_ref.at[chip_idx[j], pl.ds((1 - c) * hr, hr), :]
            cp = pltpu.make_async_remote_copy(src_ref=sent, dst_ref=theirs, send_sem=ssems.at[j],
                                              recv_sem=rsems.at[j], device_id=(x, y, 1 - c), device_id_type=MESH)
            cp.wait_send()
            cp.wait_recv()

    args = [buf, send_sems, recv_sems] + ([] if after is None else [after])
    return pl.pallas_call(
        body, name=name,
        in_specs=[HBM_SPEC, SEM_SPEC, SEM_SPEC] + [_hbm_spec()] * (n_in - 3),
        out_specs=HBM_SPEC,
        out_shape=pltpu.HBM(buf.shape, buf.dtype),
        input_output_aliases={0: 0},
        compiler_params=pltpu.CompilerParams(has_side_effects=SPLIT_COPY),
    )(*args)


class _GatheredWeights:
    def __init__(self):
        self._order = []
        self._pending = {}
        self._forwarding = {}
        self._ready = {}
        self._tokens = []

    def start(self, keys, bufs):
        send_sems, recv_sems, thru, token = _gather_start(f"gather_start_{len(self._order)}", bufs)
        self._tokens.append(token)
        self._order.extend(keys)
        self._pending.update({k: (b, s, r) for k, b, s, r in zip(keys, thru, send_sems, recv_sems)})

    def _prefetch(self, key, after):
        if key in self._pending:
            buf, ssem, rsem = self._pending.pop(key)
            tag = f"{key[0]}_{key[1]}"
            buf = _gather_wait(f"gather_wait_{tag}", buf, ssem, rsem, after)
            ssems, rsems, buf, token = _forward_start(f"gather_fwd_start_{tag}", buf)
            self._forwarding[key] = (buf, ssems, rsems)
            self._tokens.append(token)

    def get(self, name, layer, after=None, prefetch_next=True):
        key = (name, layer)
        if key not in self._ready:
            self._prefetch(key, after)
            buf, ssems, rsems = self._forwarding.pop(key)
            self._ready[key] = _forward_wait(f"gather_fwd_wait_{name}_{layer}", buf, ssems, rsems, after)
            if prefetch_next:
                self.prefetch_after(name, layer, after)
        return self._ready[key]

    def prefetch_after(self, name, layer, after):
        nxt = self._order.index((name, layer)) + 1
        if nxt < len(self._order):
            self._prefetch(self._order[nxt], after)

    def deps(self):
        tokens, self._tokens = self._tokens, []
        return tokens


def _swap_copy(g_ref, land_ref, send_sem, recv_sem):
    x, y, c, _, _, _ = _mesh_pos()
    hr = g_ref.shape[1] // 2
    return pltpu.make_async_remote_copy(src_ref=g_ref.at[:, pl.ds((1 - c) * hr, hr), :], dst_ref=land_ref,
                                        send_sem=send_sem, recv_sem=recv_sem, device_id=(x, y, 1 - c),
                                        device_id_type=MESH)


def _swap_start(name, g):
    land_shape = (g.shape[0], g.shape[1] // 2, g.shape[2])

    def body(g_ref, land_ref, send_sem, recv_sem, land_thru, token):
        _swap_copy(g_ref, land_ref, send_sem, recv_sem).start()
        token[...] = jnp.zeros_like(token)

    return pl.pallas_call(
        body, name=name,
        in_specs=[HBM_SPEC, HBM_SPEC],
        out_specs=[SEM_SPEC, SEM_SPEC, HBM_SPEC, TOKEN_SPEC],
        out_shape=[pltpu.SemaphoreType.DMA(()), pltpu.SemaphoreType.DMA(()), pltpu.HBM(land_shape, g.dtype),
                   TOKEN_SHAPE],
        input_output_aliases={1: 2},
        compiler_params=pltpu.CompilerParams(has_side_effects=SPLIT_COPY),
    )(_in_hbm(g), _in_hbm(lax.empty(land_shape, g.dtype)))


def _swap_wait(name, g, land, send_sem, recv_sem, after):
    def body(g_ref, land_ref, send_sem, recv_sem, after_ref, land_out):
        cp = _swap_copy(g_ref, land_ref, send_sem, recv_sem)
        cp.wait_send()
        cp.wait_recv()

    return pl.pallas_call(
        body, name=name,
        in_specs=[HBM_SPEC, HBM_SPEC, SEM_SPEC, SEM_SPEC, _hbm_spec()],
        out_specs=HBM_SPEC,
        out_shape=pltpu.HBM(land.shape, land.dtype),
        input_output_aliases={1: 0},
        compiler_params=pltpu.CompilerParams(has_side_effects=SPLIT_COPY),
    )(_in_hbm(g), land, send_sem, recv_sem, after)


def _add_my_half(name, g, r, pos_arr):
    ns, R, C = g.shape
    hr = R // 2
    tr = min(256, hr)
    nt = hr // tr

    def body(pos_ref, g_ref, r_ref, o_ref, land_ref):
        t = (g_ref[...] + r_ref[...]).astype(o_ref.dtype)
        o_ref[...] = t

        @pl.when(pl.program_id(1) == pos_ref[1])
        def _():
            land_ref[...] = t

    blk = pl.BlockSpec((None, tr, C), lambda i, s, pos_ref: (s, i, 0))
    return pl.pallas_call(
        body, name=name,
        grid_spec=pltpu.PrefetchScalarGridSpec(
            num_scalar_prefetch=1, grid=(nt, ns),
            in_specs=[pl.BlockSpec((None, tr, C), lambda i, s, pos_ref: (s, pos_ref[0] * nt + i, 0)), blk],
            out_specs=[blk, pl.BlockSpec((None, tr, C), lambda i, s, pos_ref: (pos_ref[1], i, 0))]),
        out_shape=[SDS((ns, hr, C), BF16)] * 2,
        compiler_params=_cp(2),
    )(pos_arr, g, r)


def _exchange_start(name, part, land):
    def body(part_ref, land_ref, send_sems, recv_sems, land_thru, token):
        x, y, c, me, chips, chip_idx = _mesh_pos()
        for j in range(N_PEER_CHIPS):
            pltpu.make_async_remote_copy(src_ref=part_ref.at[chip_idx[j]], dst_ref=land_ref.at[me],
                                         send_sem=send_sems.at[j], recv_sem=recv_sems.at[j],
                                         device_id=(*chips[j], c), device_id_type=MESH).start()
        token[...] = jnp.zeros_like(token)

    sems = pltpu.SemaphoreType.DMA((N_PEER_CHIPS,))
    return pl.pallas_call(
        body, name=name,
        in_specs=[HBM_SPEC, HBM_SPEC],
        out_specs=[SEM_SPEC, SEM_SPEC, HBM_SPEC, TOKEN_SPEC],
        out_shape=[sems, sems, pltpu.HBM(land.shape, land.dtype), TOKEN_SHAPE],
        input_output_aliases={1: 2},
        compiler_params=pltpu.CompilerParams(has_side_effects=SPLIT_COPY),
    )(_in_hbm(part), _in_hbm(land))


def _exchange_wait(name, part, land, send_sems, recv_sems, after):
    def body(part_ref, land_ref, send_sems, recv_sems, after_ref, land_out):
        x, y, c, me, chips, chip_idx = _mesh_pos()
        for j in range(N_PEER_CHIPS):
            cp = pltpu.make_async_remote_copy(src_ref=part_ref.at[chip_idx[j]], dst_ref=land_ref.at[chip_idx[j]],
                                              send_sem=send_sems.at[j], recv_sem=recv_sems.at[j],
                                              device_id=(*chips[j], c), device_id_type=MESH)
            cp.wait_send()
            cp.wait_recv()

    return pl.pallas_call(
        body, name=name,
        in_specs=[HBM_SPEC, HBM_SPEC, SEM_SPEC, SEM_SPEC, _hbm_spec()],
        out_specs=HBM_SPEC,
        out_shape=pltpu.HBM(land.shape, land.dtype),
        input_output_aliases={1: 0},
        compiler_params=pltpu.CompilerParams(has_side_effects=SPLIT_COPY),
    )(_in_hbm(part), land, send_sems, recv_sems, after)


class _GradReducer:
    def __init__(self, c_arr):
        self._c_arr = c_arr
        self._swapping = []
        self._exchanging = {}
        self._joining = {}
        self._tokens = []

    def begin(self, name, layer, g):
        tag = f"{name}_{layer}"
        ssem, rsem, land, token = _swap_start(f"rs_swap_start_{tag}", g)
        self._swapping.append((name, layer, g, ssem, rsem, land))
        self._tokens.append(token)

    def advance(self, after):
        for name, layer, g, ssem, rsem, land in self._swapping:
            tag = f"{name}_{layer}"
            theirs = _swap_wait(f"rs_swap_wait_{tag}", g, land, ssem, rsem, after)
            part, own = _add_my_half(f"rs_add_{tag}", g, theirs, self._c_arr)
            ssems, rsems, land2, token = _exchange_start(f"rs_xchg_start_{tag}", part, own)
            self._exchanging[(name, layer)] = (part, ssems, rsems, land2)
            self._tokens.append(token)
        self._swapping = []

    def deps(self):
        tokens, self._tokens = self._tokens, []
        return tokens

    def reduce(self, name, n_layers, after):
        buf = None
        for layer in range(n_layers):
            part, ssems, rsems, land = self._exchanging.pop((name, layer))
            tag = f"{name}_{layer}"
            landed = _exchange_wait(f"rs_xchg_wait_{tag}", part, land, ssems, rsems, after)
            buf = _sum_chips(f"rs_sum_{tag}", landed, self._c_arr, layer, n_layers, buf)
        ssem, rsem, buf, token = _join_start(f"rs_join_start_{name}", buf)
        self._joining[name] = (buf, ssem, rsem)
        return token

    def reduced(self, name, after):
        buf, ssem, rsem = self._joining.pop(name)
        return _join_wait(f"rs_join_wait_{name}", buf, ssem, rsem, after)


def _sum_chips(name, r, c_arr, layer, n_layers, prev):
    ns, H, C = r.shape
    tr = min(256, H)
    nt = H // tr

    def body(c_ref, r_ref, *rest):
        o_ref = rest[-1]
        o_ref[...] = ((r_ref[0].astype(F32) + r_ref[1].astype(F32)) + r_ref[2].astype(F32)) + r_ref[3].astype(F32)

    in_specs = [pl.BlockSpec((ns, tr, C), lambda i, c_ref: (0, i, 0))]
    args = [c_arr, r]
    aliases = {}
    if prev is not None:
        in_specs.append(_hbm_spec())
        args.append(prev)
        aliases = {2: 0}
    return pl.pallas_call(
        body, name=name,
        grid_spec=pltpu.PrefetchScalarGridSpec(
            num_scalar_prefetch=1, grid=(nt,), in_specs=in_specs,
            out_specs=pl.BlockSpec((None, tr, C), lambda i, c_ref: (layer, c_ref[0] * nt + i, 0))),
        out_shape=SDS((n_layers, 2 * H, C), F32),
        input_output_aliases=aliases,
        compiler_params=_cp(1),
    )(*args)


def _join_copy(buf_ref, send_sem, recv_sem):
    x, y, c, _, _, _ = _mesh_pos()
    hr = buf_ref.shape[1] // 2
    mine = buf_ref.at[:, pl.ds(c * hr, hr), :]
    theirs = buf_ref.at[:, pl.ds((1 - c) * hr, hr), :]
    send = pltpu.make_async_remote_copy(src_ref=mine, dst_ref=mine, send_sem=send_sem, recv_sem=recv_sem,
                                        device_id=(x, y, 1 - c), device_id_type=MESH)
    arrive = pltpu.make_async_remote_copy(src_ref=theirs, dst_ref=theirs, send_sem=send_sem, recv_sem=recv_sem,
                                          device_id=(x, y, 1 - c), device_id_type=MESH)
    return send, arrive


def _join_start(name, buf):
    def body(buf_ref, send_sem, recv_sem, buf_thru, token):
        _join_copy(buf_ref, send_sem, recv_sem)[0].start()
        token[...] = jnp.zeros_like(token)

    return pl.pallas_call(
        body, name=name,
        in_specs=[HBM_SPEC],
        out_specs=[SEM_SPEC, SEM_SPEC, HBM_SPEC, TOKEN_SPEC],
        out_shape=[pltpu.SemaphoreType.DMA(()), pltpu.SemaphoreType.DMA(()), pltpu.HBM(buf.shape, buf.dtype),
                   TOKEN_SHAPE],
        input_output_aliases={0: 2},
        compiler_params=pltpu.CompilerParams(has_side_effects=SPLIT_COPY),
    )(_in_hbm(buf))


def _join_wait(name, buf, send_sem, recv_sem, after):
    def body(buf_ref, send_sem, recv_sem, after_ref, buf_out):
        send, arrive = _join_copy(buf_ref, send_sem, recv_sem)
        send.wait_send()
        arrive.wait_recv()

    return pl.pallas_call(
        body, name=name,
        in_specs=[HBM_SPEC, SEM_SPEC, SEM_SPEC, _hbm_spec()],
        out_specs=HBM_SPEC,
        out_shape=pltpu.HBM(buf.shape, buf.dtype),
        input_output_aliases={0: 0},
        compiler_params=pltpu.CompilerParams(has_side_effects=SPLIT_COPY),
    )(buf, send_sem, recv_sem, after)


def _small_copy(k, buf_ref, land_ref, send_sems, recv_sems):
    x, y, c = lax.axis_index("x"), lax.axis_index("y"), lax.axis_index("c")
    me = 4 * x + 2 * y + c
    peer = (x ^ ((k >> 2) & 1), y ^ ((k >> 1) & 1), c ^ (k & 1))
    cp = pltpu.make_async_remote_copy(src_ref=buf_ref, dst_ref=land_ref.at[me], send_sem=send_sems.at[k - 1],
                                      recv_sem=recv_sems.at[k - 1], device_id=peer, device_id_type=MESH)
    return me, peer, cp


def _small_start(buf, deps):
    land = jnp.broadcast_to(buf[None], (N_DEV,) + buf.shape)
    n_dep = len(deps)

    def body(buf_ref, land_ref, *rest):
        send_sems, recv_sems, _, token = rest[n_dep:]
        for k in range(1, N_DEV):
            _small_copy(k, buf_ref, land_ref, send_sems, recv_sems)[2].start()
        token[...] = jnp.zeros_like(token)

    sems = pltpu.SemaphoreType.DMA((N_DEV - 1,))
    return pl.pallas_call(
        body, name="small_gather_start",
        in_specs=[HBM_SPEC, HBM_SPEC] + [_hbm_spec()] * n_dep,
        out_specs=[SEM_SPEC, SEM_SPEC, HBM_SPEC, TOKEN_SPEC],
        out_shape=[sems, sems, pltpu.HBM(land.shape, land.dtype), TOKEN_SHAPE],
        input_output_aliases={1: 2},
        compiler_params=pltpu.CompilerParams(has_side_effects=SPLIT_COPY),
    )(_in_hbm(buf), _in_hbm(land), *deps)


def _small_wait(buf, land, send_sems, recv_sems, after):
    def body(buf_ref, land_ref, send_sems, recv_sems, after_ref, land_out):
        for k in range(1, N_DEV):
            me, peer, cp = _small_copy(k, buf_ref, land_ref, send_sems, recv_sems)
            cp.wait_send()
            got = land_ref.at[me ^ k]
            pltpu.make_async_remote_copy(src_ref=got, dst_ref=got, send_sem=send_sems.at[k - 1],
                                         recv_sem=recv_sems.at[k - 1], device_id=peer,
                                         device_id_type=MESH).wait_recv()

    return pl.pallas_call(
        body, name="small_gather_wait",
        in_specs=[HBM_SPEC, HBM_SPEC, SEM_SPEC, SEM_SPEC, _hbm_spec()],
        out_specs=HBM_SPEC,
        out_shape=pltpu.HBM(land.shape, land.dtype),
        input_output_aliases={1: 0},
        compiler_params=pltpu.CompilerParams(has_side_effects=SPLIT_COPY),
    )(_in_hbm(buf), land, send_sems, recv_sems, after)


def _sum_devices(land):
    n, R, C = land.shape

    def body(land_ref, out_ref):
        acc = land_ref[0]
        for d in range(1, n):
            acc = acc + land_ref[d]
        out_ref[...] = acc

    return pl.pallas_call(
        body, name="small_sum",
        in_specs=[pl.BlockSpec(memory_space=pltpu.VMEM)],
        out_specs=pl.BlockSpec(memory_space=pltpu.VMEM),
        out_shape=SDS((R, C), land.dtype),
        compiler_params=pltpu.CompilerParams(vmem_limit_bytes=V7X_VMEM_LIMIT),
    )(land)


def _pack_rows(vectors):
    flat = jnp.concatenate([v.reshape(-1) for v in vectors])
    n = flat.shape[0]
    padded = -(-n // 1024) * 1024
    return jnp.pad(flat, (0, padded - n)).reshape(padded // 128, 128)


def _unpack_rows(buf, shapes):
    flat = buf.reshape(-1)
    out, off = [], 0
    for s in shapes:
        n = 1
        for dim in s:
            n *= dim
        out.append(flat[off:off + n].reshape(s))
        off += n
    return out


def _layer_forward(l, x, prm, wg):
    S, D = x.shape
    w_in = wg.get("w_in", l, x, prefetch_next=l > 0)
    p, h = _norm_matmul(f"in_proj_{l}", x, prm["attn_norm"][l], w_in, F32, deps=wg.deps())
    if l == 0:
        wg.prefetch_after("w_in", l, p)
    y_a = _sgu_fwd(f"sgu_fwd_{l}", p, prm["sgu_wt"][l], prm["sgu_bb"][l])
    y_b = _conv_fwd(f"conv_fwd_{l}", p, prm["conv_w"][l])
    os, lses = [], []
    for g in range(N_PATTERNS):
        o_g, lse_g = _attn_fwd(f"attn_fwd_{l}_{g}", p, g, prm["q_gain"][l], prm["k_gain"][l], prm["bd"])
        os.append(o_g)
        lses.append(lse_g)
    y_c = _mix_fwd(f"mix_fwd_{l}", os, lses)
    ycat = jnp.concatenate([y_a, y_b, y_c], axis=1)
    tmb, tnb = min(1024, S), min(1024, D)
    w_out = wg.get("w_out", l, ycat)
    kq = N_CHIPS * w_out.shape[1]
    x1 = _matmul(
        f"out_proj_{l}", ycat, w_out.reshape(kq, D), (S, D), F32, grid=(S // tmb, D // tnb, 1),
        a_spec=pl.BlockSpec((tmb, kq), lambda i, j, k: (i, 0)),
        b_spec=pl.BlockSpec((kq, tnb), lambda i, j, k: (0, j)),
        o_spec=pl.BlockSpec((tmb, tnb), lambda i, j, k: (i, j)),
        contract=(1, 0), acc_shape=(tmb, tnb),
        extras=(x,), extra_specs=(pl.BlockSpec((tmb, tnb), lambda i, j, k: (i, j)),),
        epi=lambda r, res: r + res, deps=wg.deps())
    w_mlp_in = wg.get("w_mlp_in", l, x1)
    r, h2 = _norm_matmul(f"mlp_in_{l}", x1, prm["mlp_norm"][l], w_mlp_in, BF16, deps=wg.deps(), post=_relu2)
    w_mlp_out = wg.get("w_mlp_out", l, r)
    dff4 = w_mlp_out.shape[1]
    tk = min(2048, dff4)
    kpc = dff4 // tk
    x2 = _matmul(
        f"mlp_out_{l}", r, w_mlp_out, (S, D), F32, grid=(S // tmb, D // tnb, N_CHIPS * kpc),
        a_spec=pl.BlockSpec((tmb, tk), lambda i, j, k: (i, k)),
        b_spec=pl.BlockSpec((None, tk, tnb), lambda i, j, k: (k // kpc, k % kpc, j)),
        o_spec=pl.BlockSpec((tmb, tnb), lambda i, j, k: (i, j)),
        contract=(1, 0), acc_shape=(tmb, tnb),
        extras=(x1,), extra_specs=(pl.BlockSpec((tmb, tnb), lambda i, j, k: (i, j)),),
        epi=lambda acc, res: acc + res, deps=wg.deps())
    saved = dict(x=x, p=p, h=h, os=os, lses=lses, ycat=ycat, x1=x1, r=r, h2=h2)
    return x2, saved


def _layer_backward(l, dx2, dx2b, sv, prm, wg, sink):
    S, D = dx2.shape
    w_in, w_out = wg.get("w_in", l), wg.get("w_out", l)
    w_mlp_in, w_mlp_out = wg.get("w_mlp_in", l), wg.get("w_mlp_out", l)
    dff4 = w_mlp_in.shape[-1]
    dff = N_CHIPS * dff4

    tmb, tnb = min(1024, S), min(1024, D)
    da = _matmul(
        f"mlp_out_bwd_{l}", dx2b, w_mlp_out, (S, dff), BF16, grid=(S // tmb, N_CHIPS, 1),
        a_spec=pl.BlockSpec((tmb, D), lambda i, j, k: (i, 0)),
        b_spec=pl.BlockSpec((None, dff4, D), lambda i, j, k: (j, 0, 0)),
        o_spec=pl.BlockSpec((tmb, dff4), lambda i, j, k: (i, j)),
        contract=(1, 1), acc_shape=(tmb, dff4),
        extras=(sv["r"],), extra_specs=(pl.BlockSpec((tmb, dff4), lambda i, j, k: (i, j)),),
        epi=lambda acc, r: acc * (2.0 * jnp.sqrt(r.astype(F32))), deps=sink.deps())
    tmw = min(1024, dff4)
    mpc = dff4 // tmw
    g_w2 = _matmul(
        f"mlp_out_dw_{l}", sv["r"], dx2b, (N_CHIPS, dff4, D), F32, grid=(N_CHIPS * mpc, D // tnb, 1),
        a_spec=pl.BlockSpec((S, tmw), lambda i, j, k: (0, i)),
        b_spec=pl.BlockSpec((S, tnb), lambda i, j, k: (0, j)),
        o_spec=pl.BlockSpec((None, tmw, tnb), lambda i, j, k: (i // mpc, i % mpc, j)),
        contract=(0, 0), acc_shape=(tmw, tnb))
    sink.begin("w_mlp_out", l, g_w2)
    dh2 = _matmul(
        f"mlp_in_bwd_{l}", da, w_mlp_in, (S, D), F32, grid=(S // tmb, D // tnb, N_CHIPS),
        a_spec=pl.BlockSpec((tmb, dff4), lambda i, j, k: (i, k)),
        b_spec=pl.BlockSpec((None, tnb, dff4), lambda i, j, k: (k, j, 0)),
        o_spec=pl.BlockSpec((tmb, tnb), lambda i, j, k: (i, j)),
        contract=(1, 1), acc_shape=(tmb, tnb), deps=sink.deps())
    sink.advance(dh2)
    tmd = min(1024, D)
    nd = D // tmd
    tnf = min(1024, dff4)
    nf = dff4 // tnf
    g_w1 = _matmul(
        f"mlp_in_dw_{l}", sv["h2"], da, (N_CHIPS, D, dff4), F32, grid=(N_CHIPS * nd, nf, 1),
        a_spec=pl.BlockSpec((S, tmd), lambda i, j, k: (0, i % nd)),
        b_spec=pl.BlockSpec((S, tnf), lambda i, j, k: (0, (i // nd) * nf + j)),
        o_spec=pl.BlockSpec((None, tmd, tnf), lambda i, j, k: (i // nd, i % nd, j)),
        contract=(0, 0), acc_shape=(tmd, tnf))
    sink.begin("w_mlp_in", l, g_w1)
    dx1, dx1b, g_mlp_norm = _rmsnorm_bwd(f"mlp_norm_bwd_{l}", dh2, sv["x1"], prm["mlp_norm"][l], dx2,
                                         deps=sink.deps())

    rq = w_out.shape[1]
    dycat = _matmul(
        f"out_proj_bwd_{l}", dx1b, w_out, (S, N_CHIPS * rq), F32, grid=(S // tmb, N_CHIPS, 1),
        a_spec=pl.BlockSpec((tmb, D), lambda i, j, k: (i, 0)),
        b_spec=pl.BlockSpec((None, rq, D), lambda i, j, k: (j, 0, 0)),
        o_spec=pl.BlockSpec((tmb, rq), lambda i, j, k: (i, j)),
        contract=(1, 1), acc_shape=(tmb, rq))
    sink.advance(dycat)
    g_wout = _matmul(
        f"out_proj_dw_{l}", sv["ycat"], dx1b, (N_CHIPS, rq, D), F32, grid=(N_CHIPS, D // tnb, 1),
        a_spec=pl.BlockSpec((S, rq), lambda i, j, k: (0, i)),
        b_spec=pl.BlockSpec((S, tnb), lambda i, j, k: (0, j)),
        o_spec=pl.BlockSpec((None, rq, tnb), lambda i, j, k: (i, 0, j)),
        contract=(0, 0), acc_shape=(rq, tnb))
    sink.begin("w_out", l, g_wout)

    p = sv["p"]
    du, dv_a, g_sgu_w, db_lanes = _sgu_bwd(f"sgu_bwd_{l}", p, dycat, prm["sgu_wt"][l], prm["sgu_wtt"][l],
                                           prm["sgu_bb"][l])
    g_sgu_b = db_lanes[:, :A_HEADS].T
    db, dc, dxb, g_conv = _conv_bwd(f"conv_bwd_{l}", p, dycat, prm["conv_w"][l])
    do3, c3 = _mix_bwd(f"mix_bwd_{l}", sv["os"], sv["lses"], dycat, prm["bd"])
    dqs, dks, dvs, dgqs, dgks = [], [], [], [], []
    for g in range(N_PATTERNS):
        dq, dk, dv, dgq, dgk = _attn_bwd(f"attn_bwd_{l}_{g}", p, g, sv["lses"][g], do3, c3,
                                         prm["q_gain"][l], prm["k_gain"][l], prm["bd"])
        dqs.append(dq)
        dks.append(dk)
        dvs.append(dv)
        dgqs.append(dgq)
        dgks.append(dgk)
    g_q = jnp.concatenate(dgqs, axis=1).reshape(N_PATTERNS * PW // HEAD_DIM, HEAD_DIM).sum(axis=0)
    g_k = jnp.concatenate(dgks, axis=1).reshape(N_PATTERNS * PW // HEAD_DIM, HEAD_DIM).sum(axis=0)
    dp = jnp.concatenate([du, dv_a, db, dc, dxb] + [t.astype(BF16) for t in dqs + dks + dvs], axis=1)

    ns_in = w_in.shape[-1]
    tmh = min(512, D)
    nh = D // tmh
    g_win = _matmul(
        f"in_proj_dw_{l}", sv["h"], dp, (N_CHIPS, D, ns_in), F32, grid=(N_CHIPS * nh, 1, 1),
        a_spec=pl.BlockSpec((S, tmh), lambda i, j, k: (0, i % nh)),
        b_spec=pl.BlockSpec((S, ns_in), lambda i, j, k: (0, i // nh)),
        o_spec=pl.BlockSpec((None, tmh, ns_in), lambda i, j, k: (i // nh, i % nh, 0)),
        contract=(0, 0), acc_shape=(tmh, ns_in))
    sink.begin("w_in", l, g_win)
    dh = _matmul(
        f"in_proj_bwd_{l}", dp, w_in, (S, D), F32, grid=(S // tmb, D // tnb, N_CHIPS),
        a_spec=pl.BlockSpec((tmb, ns_in), lambda i, j, k: (i, k)),
        b_spec=pl.BlockSpec((None, tnb, ns_in), lambda i, j, k: (k, j, 0)),
        o_spec=pl.BlockSpec((tmb, tnb), lambda i, j, k: (i, j)),
        contract=(1, 1), acc_shape=(tmb, tnb), deps=sink.deps())
    sink.advance(dh)
    dx0, dx0b, g_attn_norm = _rmsnorm_bwd(f"attn_norm_bwd_{l}", dh, sv["x"], prm["attn_norm"][l], dx1,
                                          deps=sink.deps())

    big = dict(w_in=g_win, w_out=g_wout, w_mlp_in=g_w1, w_mlp_out=g_w2)
    small = dict(attn_norm=g_attn_norm.reshape(-1), sgu_w=g_sgu_w, sgu_b=g_sgu_b, conv_w=g_conv,
                 q_norm=g_q, k_norm=g_k, mlp_norm=g_mlp_norm.reshape(-1))
    return dx0, dx0b, big, small


BIG = ("w_in", "w_out", "w_mlp_in", "w_mlp_out")
SMALL_REPLICATED = ("attn_norm", "sgu_w", "sgu_b", "q_norm", "k_norm", "mlp_norm")


def _local_step(x, target, prm, wg, n_layers, sink):
    saved = []
    h = x
    for l in range(n_layers):
        h, sv = _layer_forward(l, h, prm, wg)
        saved.append(sv)
    dy, dyb, colsq = _loss_kernel(h, target)
    loss = 0.5 * jnp.sum(colsq) / x.shape[1]
    bigs, smalls = [None] * n_layers, [None] * n_layers
    for l in reversed(range(n_layers)):
        dy, dyb, bigs[l], smalls[l] = _layer_backward(l, dy, dyb, saved[l], prm, wg, sink)
    return loss, dy, bigs, smalls


def _prepare_params(attn_norm, sgu_w, sgu_b, conv_full, q_norm, k_norm, mlp_norm):
    n_layers = attn_norm.shape[0]
    tri = jnp.tril(sgu_w)
    idx = jnp.arange(PW)
    bd = (idx[:, None] // HEAD_DIM == idx[None, :] // HEAD_DIM).astype(BF16)
    return dict(
        attn_norm=[attn_norm[l][None, :] for l in range(n_layers)],
        mlp_norm=[mlp_norm[l][None, :] for l in range(n_layers)],
        sgu_wt=[tri[l].astype(BF16) for l in range(n_layers)],
        sgu_wtt=[tri[l].transpose(0, 2, 1).astype(BF16) for l in range(n_layers)],
        sgu_bb=[jnp.repeat(sgu_b[l].T, HEAD_DIM, axis=1) for l in range(n_layers)],
        conv_w=[conv_full[l] for l in range(n_layers)],
        q_gain=[jnp.tile(q_norm[l], PW // HEAD_DIM)[None, :] for l in range(n_layers)],
        k_gain=[jnp.tile(k_norm[l], PW // HEAD_DIM)[None, :] for l in range(n_layers)],
        bd=bd,
    )


def kernel(x, attn_norm, w_in, sgu_w, sgu_b, conv_w, q_norm, k_norm, w_out, mlp_norm, w_mlp_in, w_mlp_out, loss_target, m_attn_norm, m_w_in, m_sgu_w, m_sgu_b, m_conv_w, m_q_norm, m_k_norm, m_w_out, m_mlp_norm, m_w_mlp_in, m_w_mlp_out, v_attn_norm, v_w_in, v_sgu_w, v_sgu_b, v_conv_w, v_q_norm, v_k_norm, v_w_out, v_mlp_norm, v_w_mlp_in, v_w_mlp_out):
    n_layers = attn_norm.shape[0]
    weights = dict(attn_norm=attn_norm, w_in=w_in, sgu_w=sgu_w, sgu_b=sgu_b, conv_w=conv_w, q_norm=q_norm,
                   k_norm=k_norm, w_out=w_out, mlp_norm=mlp_norm, w_mlp_in=w_mlp_in, w_mlp_out=w_mlp_out)
    mom_m = dict(attn_norm=m_attn_norm, w_in=m_w_in, sgu_w=m_sgu_w, sgu_b=m_sgu_b, conv_w=m_conv_w,
                 q_norm=m_q_norm, k_norm=m_k_norm, w_out=m_w_out, mlp_norm=m_mlp_norm, w_mlp_in=m_w_mlp_in,
                 w_mlp_out=m_w_mlp_out)
    mom_v = dict(attn_norm=v_attn_norm, w_in=v_w_in, sgu_w=v_sgu_w, sgu_b=v_sgu_b, conv_w=v_conv_w,
                 q_norm=v_q_norm, k_norm=v_k_norm, w_out=v_w_out, mlp_norm=v_mlp_norm, w_mlp_in=v_w_mlp_in,
                 w_mlp_out=v_w_mlp_out)
    order = ("attn_norm", "w_in", "sgu_w", "sgu_b", "conv_w", "q_norm", "k_norm", "w_out", "mlp_norm",
             "w_mlp_in", "w_mlp_out")
    chip = 2 * lax.axis_index("x") + lax.axis_index("y")
    c_arr = jnp.stack([lax.axis_index("c"), chip]).astype(jnp.int32)

    conv_cols = conv_w.shape[-1]
    chip_arr = chip.astype(jnp.int32).reshape(1)
    conv_pack = jnp.pad(conv_w.reshape(-1), (0, 2048 - conv_w.size)).reshape(1, 16, 128)
    wg = _GatheredWeights()
    wg.start([("conv_w", 0), ("w_in", 0)],
             [_place_shard("place_conv_w", conv_pack, 0, chip_arr, F32),
              _place_shard("place_w_in_0", weights["w_in"], 0, chip_arr, BF16)])
    keys = [(n, l) for l in range(n_layers) for n in BIG if (n, l) != ("w_in", 0)]
    first = wg.deps()
    wg.start(keys, [_place_shard(f"place_{n}_{l}", weights[n], l, chip_arr, BF16, deps=first) for n, l in keys])
    conv_full = wg.get("conv_w", 0, wg.deps()[-1]).reshape(N_CHIPS, 2048)[:, :conv_w.size].reshape(N_CHIPS, n_layers, 3, conv_cols)
    conv_full = conv_full.transpose(1, 2, 0, 3).reshape(n_layers, 3, N_CHIPS * conv_cols)
    prm = _prepare_params(attn_norm, sgu_w, sgu_b, conv_full, q_norm, k_norm, mlp_norm)

    sink = _GradReducer(c_arr)
    loss_local, grad_x, _, smalls = _local_step(x[0], loss_target[0], prm, wg, n_layers, sink)
    loss = lax.psum(loss_local, ("x", "y", "c"))

    small_names = SMALL_REPLICATED + ("conv_w",)
    small_shapes = [(n_layers,) + tuple(smalls[0][n].shape) for n in small_names]
    packed = _pack_rows([jnp.stack([smalls[l][n] for l in range(n_layers)]) for n in small_names])
    small_send, small_recv, small_land, small_token = _small_start(packed, sink.deps())

    grads, delta, new_m, new_v = {}, {}, {}, {}

    def update(n, after):
        shp = weights[n].shape
        two_d = (shp[0] * shp[1], shp[2])
        d, nm, nv, g = _adamw(f"adamw_{n}", weights[n].reshape(two_d), sink.reduced(n, after).reshape(two_d),
                              mom_m[n].reshape(two_d), mom_v[n].reshape(two_d))
        grads[n], delta[n], new_m[n], new_v[n] = g.reshape(shp), d.reshape(shp), nm.reshape(shp), nv.reshape(shp)

    token = small_token
    for n in ("w_mlp_out", "w_mlp_in", "w_out"):
        token = sink.reduce(n, n_layers, token)
    update("w_mlp_out", token)
    token = sink.reduce("w_in", n_layers, delta["w_mlp_out"])
    update("w_mlp_in", token)
    update("w_out", delta["w_mlp_in"])
    update("w_in", delta["w_out"])
    small_land = _small_wait(packed, small_land, small_send, small_recv, delta["w_in"])
    grads.update(zip(small_names, _unpack_rows(_sum_devices(small_land), small_shapes)))
    grads["conv_w"] = lax.dynamic_slice_in_dim(grads["conv_w"], chip * conv_cols, conv_cols, axis=2)
    smalls_all = SMALL_REPLICATED + ("conv_w",)
    shapes = [weights[n].shape for n in smalls_all]
    d, nm, nv, _ = _adamw("adamw_small",
                          _pack_rows([weights[n] for n in smalls_all]), _pack_rows([grads[n] for n in smalls_all]),
                          _pack_rows([mom_m[n] for n in smalls_all]), _pack_rows([mom_v[n] for n in smalls_all]))
    for n, dd, mm, vv in zip(smalls_all, _unpack_rows(d, shapes), _unpack_rows(nm, shapes), _unpack_rows(nv, shapes)):
        delta[n], new_m[n], new_v[n] = dd, mm, vv

    return (loss, grad_x[None], *[grads[n] for n in order], *[delta[n] for n in order],
            *[new_m[n] for n in order], *[new_v[n] for n in order])
```

```python
import jax
import jax.numpy as jnp
from jax import lax
from jax.experimental import pallas as pl
from jax.experimental.pallas import tpu as pltpu

F32 = jnp.float32
BF16 = jnp.bfloat16
SDS = jax.ShapeDtypeStruct

EPS = 1e-6
HEAD_DIM = 64
A_HEADS = 8
A_WIDTH = 512
CHUNK = 128
B_WIDTH = 768
C_WIDTH = 768
N_PATTERNS = 3
PATTERN_DILATION = (1, 4, 16)
PW = 256
D_IN_PROJ = 5632
OFF_AU, OFF_AV, OFF_BB, OFF_BC, OFF_BX, OFF_Q, OFF_K, OFF_V = 0, 512, 1024, 1792, 2560, 3328, 4096, 4864
N_CHIPS = 4
N_DEV = 8
BLK = 128

ADAM_LR, ADAM_B1, ADAM_B2, ADAM_EPS, ADAM_WD, ADAM_STEP = 0.001, 0.9, 0.999, 1e-08, 0.01, 10

V7X_VMEM_LIMIT = 56 * 1024 * 1024
MESH = pl.DeviceIdType.MESH
NEG = -1e30


def _cp(n_axes):
    return pltpu.CompilerParams(dimension_semantics=("arbitrary",) * n_axes, vmem_limit_bytes=V7X_VMEM_LIMIT)


def _hbm_spec():
    return pl.BlockSpec(memory_space=pl.ANY)


def _norm_matmul(name, x, g, wg, out_dtype, deps=(), post=None):
    S, D = x.shape
    ns, _, Ns = wg.shape
    tm = min(512, S)
    n_dep = len(deps)

    def body(x_ref, g_ref, w_ref, *rest):
        o_ref, h_ref, hs_ref = rest[n_dep:]

        @pl.when(pl.program_id(1) == 0)
        def _():
            xv = x_ref[...]
            y = xv * lax.rsqrt(jnp.mean(xv * xv, axis=-1, keepdims=True) + EPS) * g_ref[...]
            hb = y.astype(BF16)
            hs_ref[...] = hb
            h_ref[...] = hb
        acc = jnp.dot(hs_ref[...], w_ref[...], preferred_element_type=F32)
        o_ref[...] = (acc if post is None else post(acc)).astype(o_ref.dtype)

    return pl.pallas_call(
        body, name=name, grid=(S // tm, ns),
        in_specs=[pl.BlockSpec((tm, D), lambda i, s: (i, 0)),
                  pl.BlockSpec((1, D), lambda i, s: (0, 0)),
                  pl.BlockSpec((None, D, Ns), lambda i, s: (s, 0, 0))] + [_hbm_spec()] * n_dep,
        out_specs=[pl.BlockSpec((tm, Ns), lambda i, s: (i, s)),
                   pl.BlockSpec((tm, D), lambda i, s: (i, 0))],
        out_shape=[SDS((S, ns * Ns), out_dtype), SDS((S, D), BF16)],
        scratch_shapes=[pltpu.VMEM((tm, D), BF16)],
        compiler_params=_cp(2),
    )(x, g, wg, *deps)


def _relu(t):
    return jnp.maximum(t, 0.0)


def _square(t):
    return t * t


def _matmul(name, a, b, out_shape, out_dtype, *, grid, a_spec, b_spec, o_spec, contract, acc_shape,
            extras=(), extra_specs=(), a_pre=None, epi=None, deps=()):
    nk = grid[2]
    n_ex = len(extras)
    n_dep = len(deps)
    dims = (((contract[0],), (contract[1],)), ((), ()))

    def product(a_ref, b_ref):
        av = a_ref[...] if a_pre is None else a_pre(a_ref[...])
        return lax.dot_general(av, b_ref[...], dims, preferred_element_type=F32)

    def finish(r, ex, o_ref):
        if epi is not None:
            r = epi(r, *[e[...] for e in ex])
        o_ref[...] = r.astype(o_ref.dtype)

    def body_single(a_ref, b_ref, *rest):
        finish(product(a_ref, b_ref), rest[:n_ex], rest[n_ex + n_dep])

    def body(a_ref, b_ref, *rest):
        ex = rest[:n_ex]
        o_ref = rest[n_ex + n_dep]
        acc_ref = rest[n_ex + n_dep + 1]
        k = pl.program_id(2)

        @pl.when(k == 0)
        def _():
            acc_ref[...] = product(a_ref, b_ref)

        @pl.when((k > 0) & (k < nk - 1))
        def _():
            acc_ref[...] += product(a_ref, b_ref)

        @pl.when(k == nk - 1)
        def _():
            finish(acc_ref[...] + product(a_ref, b_ref), ex, o_ref)

    return pl.pallas_call(
        body_single if nk == 1 else body, name=name, grid=grid,
        in_specs=[a_spec, b_spec, *extra_specs] + [_hbm_spec()] * n_dep,
        out_specs=o_spec,
        out_shape=SDS(out_shape, out_dtype),
        scratch_shapes=[] if nk == 1 else [pltpu.VMEM(acc_shape, F32)],
        compiler_params=_cp(3),
    )(a, b, *extras, *deps)


def _loss_kernel(y, t):
    S, D = y.shape
    tm = min(256, S)

    def body(y_ref, t_ref, dy_ref, dyb_ref, l_ref):
        @pl.when(pl.program_id(0) == 0)
        def _():
            l_ref[...] = jnp.zeros_like(l_ref)
        e = y_ref[...] - t_ref[...]
        l_ref[...] += jnp.sum(e * e, axis=0, keepdims=True)
        dy = e * (1.0 / D)
        dy_ref[...] = dy
        dyb_ref[...] = dy.astype(BF16)

    row = pl.BlockSpec((tm, D), lambda i: (i, 0))
    return pl.pallas_call(
        body, name="loss_head", grid=(S // tm,),
        in_specs=[row, row],
        out_specs=[row, row, pl.BlockSpec((1, D), lambda i: (0, 0))],
        out_shape=[SDS((S, D), F32), SDS((S, D), BF16), SDS((1, D), F32)],
        compiler_params=_cp(1),
    )(y, t)


def _rmsnorm_bwd(name, dh, x, g, dres, deps=()):
    S, D = x.shape
    tm = min(256, S)
    n_dep = len(deps)

    def body(dh_ref, x_ref, g_ref, dres_ref, *rest):
        dx_ref, dxb_ref, dg_ref = rest[n_dep:]
        @pl.when(pl.program_id(0) == 0)
        def _():
            dg_ref[...] = jnp.zeros_like(dg_ref)
        xv = x_ref[...]
        dhv = dh_ref[...]
        rstd = lax.rsqrt(jnp.mean(xv * xv, axis=-1, keepdims=True) + EPS)
        xhat = xv * rstd
        dg_ref[...] += jnp.sum(dhv * xhat, axis=0, keepdims=True)
        dxn = dhv * g_ref[...]
        dx = dres_ref[...] + rstd * (dxn - xhat * jnp.mean(dxn * xhat, axis=-1, keepdims=True))
        dx_ref[...] = dx
        dxb_ref[...] = dx.astype(BF16)

    row = pl.BlockSpec((tm, D), lambda i: (i, 0))
    vec = pl.BlockSpec((1, D), lambda i: (0, 0))
    return pl.pallas_call(
        body, name=name, grid=(S // tm,),
        in_specs=[row, row, vec, row] + [_hbm_spec()] * n_dep,
        out_specs=[row, row, vec],
        out_shape=[SDS((S, D), F32), SDS((S, D), BF16), SDS((1, D), F32)],
        compiler_params=_cp(1),
    )(dh, x, g, dres, *deps)


def _adamw(name, w, g, m, v):
    R, C = w.shape
    tr = 256 if R % 256 == 0 else R
    c1 = 1.0 - ADAM_B1 ** ADAM_STEP
    c2 = 1.0 - ADAM_B2 ** ADAM_STEP

    def body(w_ref, g_ref, m_ref, v_ref, d_ref, nm_ref, nv_ref, g_out_ref):
        gv = g_ref[...]
        nm = ADAM_B1 * m_ref[...] + (1.0 - ADAM_B1) * gv
        nv = ADAM_B2 * v_ref[...] + (1.0 - ADAM_B2) * (gv * gv)
        m_hat = nm / c1
        v_hat = nv / c2
        d_ref[...] = -ADAM_LR * (m_hat / (jnp.sqrt(v_hat) + ADAM_EPS) + ADAM_WD * w_ref[...])
        nm_ref[...] = nm
        nv_ref[...] = nv
        g_out_ref[...] = gv

    blk = pl.BlockSpec((tr, C), lambda i: (i, 0))
    return pl.pallas_call(
        body, name=name, grid=(R // tr,),
        in_specs=[blk] * 4, out_specs=[blk] * 4,
        out_shape=[SDS((R, C), F32)] * 4,
        compiler_params=_cp(1),
    )(w, g, m, v)


SGU_STEP_ROWS = 512


def _pair_select(lane, lo, hi):
    return jnp.where(lane < HEAD_DIM, lo, hi)


def _sgu_fwd(name, p, wt, bb):
    S = p.shape[0]

    rows = min(SGU_STEP_ROWS, S)

    def body(u_ref, v_ref, wt_ref, bb_ref, o_ref):
        lane = lax.broadcasted_iota(jnp.int32, (CHUNK, 128), 1)
        for ci in range(rows // CHUNK):
            rs = slice(CHUNK * ci, CHUNK * (ci + 1))
            for pp in range(A_HEADS // 2):
                cs = slice(128 * pp, 128 * (pp + 1))
                vb = v_ref[rs, cs].astype(BF16)
                mixed = _pair_select(lane,
                                     jnp.dot(wt_ref[2 * pp], vb, preferred_element_type=F32),
                                     jnp.dot(wt_ref[2 * pp + 1], vb, preferred_element_type=F32)) + bb_ref[:, cs]
                o_ref[rs, cs] = (u_ref[rs, cs] * mixed).astype(o_ref.dtype)

    return pl.pallas_call(
        body, name=name, grid=(S // rows,),
        in_specs=[pl.BlockSpec((rows, A_WIDTH), lambda c: (c, OFF_AU // A_WIDTH)),
                  pl.BlockSpec((rows, A_WIDTH), lambda c: (c, OFF_AV // A_WIDTH)),
                  pl.BlockSpec((A_HEADS, CHUNK, CHUNK), lambda c: (0, 0, 0)),
                  pl.BlockSpec((CHUNK, A_WIDTH), lambda c: (0, 0))],
        out_specs=pl.BlockSpec((rows, A_WIDTH), lambda c: (c, 0)),
        out_shape=SDS((S, A_WIDTH), BF16),
        compiler_params=_cp(1),
    )(p, p, wt, bb)


def _sgu_bwd(name, p, dycat, wt, wtt, bb):
    S = p.shape[0]
    rows = min(SGU_STEP_ROWS, S)

    def body(u_ref, v_ref, dy_ref, wt_ref, wtt_ref, bb_ref, du_ref, dv_ref, dw_ref, db_ref, dbacc_ref):
        c = pl.program_id(0)

        @pl.when(c == 0)
        def _():
            dw_ref[...] = jnp.zeros_like(dw_ref)
            dbacc_ref[...] = jnp.zeros_like(dbacc_ref)

        lane = lax.broadcasted_iota(jnp.int32, (CHUNK, 128), 1)
        row = lax.broadcasted_iota(jnp.int32, (CHUNK, 128), 0)
        causal = row >= lane
        nt = (((1,), (1,)), ((), ()))
        for pp in range(A_HEADS // 2):
            cs = slice(128 * pp, 128 * (pp + 1))
            dw_lo = jnp.zeros((CHUNK, CHUNK), F32)
            dw_hi = jnp.zeros((CHUNK, CHUNK), F32)
            dm_sum = jnp.zeros((CHUNK, 128), F32)
            for ci in range(rows // CHUNK):
                rs = slice(CHUNK * ci, CHUNK * (ci + 1))
                vb = v_ref[rs, cs].astype(BF16)
                dy = dy_ref[rs, cs]
                mixed = _pair_select(lane,
                                     jnp.dot(wt_ref[2 * pp], vb, preferred_element_type=F32),
                                     jnp.dot(wt_ref[2 * pp + 1], vb, preferred_element_type=F32)) + bb_ref[:, cs]
                du_ref[rs, cs] = (dy * mixed).astype(du_ref.dtype)
                dm = dy * u_ref[rs, cs]
                dmb = dm.astype(BF16)
                dv = _pair_select(lane,
                                  jnp.dot(wtt_ref[2 * pp], dmb, preferred_element_type=F32),
                                  jnp.dot(wtt_ref[2 * pp + 1], dmb, preferred_element_type=F32))
                dv_ref[rs, cs] = dv.astype(dv_ref.dtype)
                dm_sum += dm
                dm_lo = jnp.where(lane < HEAD_DIM, dm, 0.0).astype(BF16)
                dm_hi = jnp.where(lane >= HEAD_DIM, dm, 0.0).astype(BF16)
                dw_lo += lax.dot_general(dm_lo, vb, nt, preferred_element_type=F32)
                dw_hi += lax.dot_general(dm_hi, vb, nt, preferred_element_type=F32)
            dbacc_ref[:, cs] += dm_sum
            dw_ref[2 * pp] += jnp.where(causal, dw_lo, 0.0)
            dw_ref[2 * pp + 1] += jnp.where(causal, dw_hi, 0.0)

        @pl.when(c == S // rows - 1)
        def _():
            out = jnp.zeros((CHUNK, 128), F32)
            for pp in range(A_HEADS // 2):
                acc = dbacc_ref[:, 128 * pp:128 * (pp + 1)]
                s_lo = jnp.sum(jnp.where(lane < HEAD_DIM, acc, 0.0), axis=1, keepdims=True)
                s_hi = jnp.sum(jnp.where(lane >= HEAD_DIM, acc, 0.0), axis=1, keepdims=True)
                out = jnp.where(lane == 2 * pp, s_lo, out)
                out = jnp.where(lane == 2 * pp + 1, s_hi, out)
            db_ref[...] = out

    chunk = lambda col: pl.BlockSpec((rows, A_WIDTH), lambda c: (c, col))
    wspec = pl.BlockSpec((A_HEADS, CHUNK, CHUNK), lambda c: (0, 0, 0))
    return pl.pallas_call(
        body, name=name, grid=(S // rows,),
        in_specs=[chunk(OFF_AU // A_WIDTH), chunk(OFF_AV // A_WIDTH), chunk(0), wspec, wspec,
                  pl.BlockSpec((CHUNK, A_WIDTH), lambda c: (0, 0))],
        out_specs=[chunk(0), chunk(0), wspec, pl.BlockSpec((CHUNK, 128), lambda c: (0, 0))],
        out_shape=[SDS((S, A_WIDTH), BF16), SDS((S, A_WIDTH), BF16),
                   SDS((A_HEADS, CHUNK, CHUNK), F32), SDS((CHUNK, 128), F32)],
        scratch_shapes=[pltpu.VMEM((CHUNK, A_WIDTH), F32)],
        compiler_params=_cp(1),
    )(p, p, dycat, wt, wtt, bb)


CONV_HALO = 8
CONV_COLS = 256
CONV_ROWS = 1024


def _shift_down(a, halo, k):
    T = a.shape[0]
    row = lax.broadcasted_iota(jnp.int32, a.shape, 0)
    out = pltpu.roll(a, k, 0)
    for r in range(k):
        out = jnp.where(row == r, halo[CONV_HALO - k + r:CONV_HALO - k + r + 1, :], out)
    return out


def _shift_up(a, halo, k):
    T = a.shape[0]
    row = lax.broadcasted_iota(jnp.int32, a.shape, 0)
    out = pltpu.roll(a, T - k, 0)
    for r in range(k):
        out = jnp.where(row == T - k + r, halo[r:r + 1, :], out)
    return out


def _conv_specs(S, T):
    hb = T // CONV_HALO
    last = S // CONV_HALO - 1
    tile = lambda col0: pl.BlockSpec((T, CONV_COLS), lambda j, i: (i, col0 + j))
    prev = lambda col0: pl.BlockSpec((CONV_HALO, CONV_COLS), lambda j, i: (jnp.maximum(i * hb - 1, 0), col0 + j))
    nxt = lambda col0: pl.BlockSpec((CONV_HALO, CONV_COLS), lambda j, i: (jnp.minimum((i + 1) * hb, last), col0 + j))
    return tile, prev, nxt


def _conv_fwd(name, p, w):
    S = p.shape[0]
    T = min(CONV_ROWS, S)
    tile, prev, _ = _conv_specs(S, T)
    cb, cc, cx = OFF_BB // CONV_COLS, OFF_BC // CONV_COLS, OFF_BX // CONV_COLS

    def body(b_ref, c_ref, x_ref, ch_ref, xh_ref, w_ref, o_ref):
        i = pl.program_id(1)
        z = c_ref[...] * x_ref[...]
        zh = jnp.where(i > 0, ch_ref[...] * xh_ref[...], 0.0)
        z1 = _shift_down(z, zh, 1)
        z2 = _shift_down(z, zh, 2)
        conv = w_ref[0:1, :] * z2 + w_ref[1:2, :] * z1 + w_ref[2:3, :] * z
        o_ref[...] = (b_ref[...] * conv).astype(o_ref.dtype)

    return pl.pallas_call(
        body, name=name, grid=(B_WIDTH // CONV_COLS, S // T),
        in_specs=[tile(cb), tile(cc), tile(cx), prev(cc), prev(cx),
                  pl.BlockSpec((3, CONV_COLS), lambda j, i: (0, j))],
        out_specs=tile(0),
        out_shape=SDS((S, B_WIDTH), BF16),
        compiler_params=_cp(2),
    )(p, p, p, p, p, w)


def _conv_bwd(name, p, dycat, w):
    S = p.shape[0]
    T = min(CONV_ROWS, S)
    tile, prev, nxt = _conv_specs(S, T)
    cb, cc, cx = OFF_BB // CONV_COLS, OFF_BC // CONV_COLS, OFF_BX // CONV_COLS
    cdy = A_WIDTH // CONV_COLS
    n_i = S // T

    def body(b_ref, c_ref, x_ref, dy_ref, ch_ref, xh_ref, bn_ref, dyn_ref, w_ref,
             db_ref, dc_ref, dx_ref, dw_ref):
        i = pl.program_id(1)

        @pl.when(i == 0)
        def _():
            dw_ref[...] = jnp.zeros_like(dw_ref)

        cv = c_ref[...]
        xv = x_ref[...]
        z = cv * xv
        zh = jnp.where(i > 0, ch_ref[...] * xh_ref[...], 0.0)
        z1 = _shift_down(z, zh, 1)
        z2 = _shift_down(z, zh, 2)
        w0, w1, w2 = w_ref[0:1, :], w_ref[1:2, :], w_ref[2:3, :]
        conv = w0 * z2 + w1 * z1 + w2 * z
        dy = dy_ref[...]
        db_ref[...] = (dy * conv).astype(db_ref.dtype)
        dconv = dy * b_ref[...]
        dconv_n = jnp.where(i < n_i - 1, dyn_ref[...] * bn_ref[...], 0.0)
        dz = w2 * dconv + w1 * _shift_up(dconv, dconv_n, 1) + w0 * _shift_up(dconv, dconv_n, 2)
        dc_ref[...] = (dz * xv).astype(dc_ref.dtype)
        dx_ref[...] = (dz * cv).astype(dx_ref.dtype)
        dw_ref[0:1, :] += jnp.sum(dconv * z2, axis=0, keepdims=True)
        dw_ref[1:2, :] += jnp.sum(dconv * z1, axis=0, keepdims=True)
        dw_ref[2:3, :] += jnp.sum(dconv * z, axis=0, keepdims=True)

    wspec = pl.BlockSpec((3, CONV_COLS), lambda j, i: (0, j))
    return pl.pallas_call(
        body, name=name, grid=(B_WIDTH // CONV_COLS, n_i),
        in_specs=[tile(cb), tile(cc), tile(cx), tile(cdy), prev(cc), prev(cx), nxt(cb), nxt(cdy), wspec],
        out_specs=[tile(0), tile(0), tile(0), wspec],
        out_shape=[SDS((S, B_WIDTH), BF16)] * 3 + [SDS((3, B_WIDTH), F32)],
        compiler_params=_cp(2),
    )(p, p, p, dycat, p, p, p, dycat, w)


def _seg_sum(t, bd):
    hi = t.astype(BF16)
    lo = (t - hi.astype(F32)).astype(BF16)
    return jnp.dot(hi, bd, preferred_element_type=F32) + jnp.dot(lo, bd, preferred_element_type=F32)


def _head_norm(x, g, bd):
    rstd = lax.rsqrt(_seg_sum(x * x, bd) * (1.0 / HEAD_DIM) + EPS)
    xhat = x * rstd
    return xhat * g, xhat, rstd


def _head_norm_bwd(dy, g, xhat, rstd, bd):
    dxh = dy * g
    return rstd * (dxh - xhat * (_seg_sum(dxh * xhat, bd) * (1.0 / HEAD_DIM)))


def _band_mask(has_prev):
    row = lax.broadcasted_iota(jnp.int32, (BLK, 2 * BLK), 0)
    col = lax.broadcasted_iota(jnp.int32, (BLK, 2 * BLK), 1)
    first_key = jnp.where(has_prev, 0, BLK)
    return (col >= row) & (col <= row + BLK) & (col >= first_key)


def _residue_rows(r, d):
    return slice(None) if d == 1 else pl.ds(r, BLK, stride=d)


STRIDED_LANES = 128


def _step_width(d):
    return PW if d == 1 else STRIDED_LANES


def _n_stack(lane):
    return lane.shape[1] // HEAD_DIM


def _for_residues(d, fn):
    if d == 1:
        fn(0)
    else:
        def two(i, carry):
            fn(2 * i)
            fn(2 * i + 1)
            return carry
        lax.fori_loop(0, d // 2, two, 0)


def _head_mask(lane, j):
    return (lane >= HEAD_DIM * j) & (lane < HEAD_DIM * (j + 1))


def _stack_heads(x, lane):
    return jnp.concatenate([jnp.where(_head_mask(lane, j), x, 0.0) for j in range(_n_stack(lane))], axis=0)


def _unstack_heads(y, lane):
    out = y[:BLK]
    for j in range(1, _n_stack(lane)):
        out = jnp.where(lane >= HEAD_DIM * j, y[BLK * j:BLK * (j + 1)], out)
    return out


def _head_columns(v, lane):
    return jnp.concatenate([jnp.max(jnp.where(_head_mask(lane, j), v, NEG), axis=1, keepdims=True)
                            for j in range(_n_stack(lane))], axis=0)


def _attn_fwd(name, p, g, gq, gk, bd):
    S = p.shape[0]
    d = PATTERN_DILATION[g]
    rows = BLK * d
    hw = _step_width(d)
    nt = (((1,), (1,)), ((), ()))

    def body(q_ref, kc_ref, kp_ref, vc_ref, vp_ref, gq_ref, gk_ref, bd_ref, o_ref, lse_ref):
        has_prev = pl.program_id(1) > 0
        bdv = bd_ref[...]
        band = jnp.concatenate([_band_mask(has_prev)] * (hw // HEAD_DIM), axis=0)
        lane = lax.broadcasted_iota(jnp.int32, (1, hw), 1)

        def residue(r):
            rr = _residue_rows(r, d)
            qn, _, _ = _head_norm(q_ref[rr, :], gq_ref[...], bdv)
            kn, _, _ = _head_norm(jnp.concatenate([kp_ref[rr, :], kc_ref[rr, :]], axis=0), gk_ref[...], bdv)
            knb = kn.astype(BF16)
            vb = jnp.concatenate([vp_ref[rr, :], vc_ref[rr, :]], axis=0).astype(BF16)
            qs = _stack_heads(qn, lane).astype(BF16)
            s = lax.dot_general(qs, knb, nt, preferred_element_type=F32) * (HEAD_DIM ** -0.5)
            s = jnp.where(band, s, NEG)
            m = jnp.max(s, axis=1, keepdims=True)
            e = jnp.exp(s - m)
            den = jnp.sum(e, axis=1, keepdims=True)
            pv = jnp.dot(e.astype(BF16), vb, preferred_element_type=F32)
            o_ref[rr, :] = _unstack_heads(pv / den, lane)
            lse_ref[rr, :] = _unstack_heads(jnp.broadcast_to(m + jnp.log(den), pv.shape), lane)

        _for_residues(d, residue)

    per = PW // hw
    cq, ck, cv = (OFF_Q + PW * g) // hw, (OFF_K + PW * g) // hw, (OFF_V + PW * g) // hw
    cur = lambda col: pl.BlockSpec((rows, hw), lambda h, n: (n, col + h))
    prv = lambda col: pl.BlockSpec((rows, hw), lambda h, n: (jnp.maximum(n - 1, 0), col + h))
    vec = pl.BlockSpec((1, hw), lambda h, n: (0, h))
    return pl.pallas_call(
        body, name=name, grid=(per, S // rows),
        in_specs=[cur(cq), cur(ck), prv(ck), cur(cv), prv(cv), vec, vec, pl.BlockSpec((hw, hw), lambda h, n: (0, 0))],
        out_specs=[cur(0), cur(0)],
        out_shape=[SDS((S, PW), F32)] * 2,
        compiler_params=_cp(2),
    )(p, p, p, p, p, gq, gk, bd)


def _attn_bwd(name, p, g, lse, do3, c3, gq, gk, bd):
    S = p.shape[0]
    d = PATTERN_DILATION[g]
    rows = BLK * d
    nblk = S // rows
    hw = _step_width(d)
    nt = (((1,), (1,)), ((), ()))
    tn = (((0,), (0,)), ((), ()))

    def body(q_ref, kc_ref, kp_ref, vc_ref, vp_ref, lse_ref, do_ref, c_ref, gq_ref, gk_ref, bd_ref,
             dq_ref, dk_ref, dv_ref, dgq_ref, dgk_ref, ck_ref, cv_ref, dq_keep_ref):
        n = pl.program_id(1)

        @pl.when(n == 0)
        def _():
            ck_ref[...] = jnp.zeros_like(ck_ref)
            cv_ref[...] = jnp.zeros_like(cv_ref)
            dgq_ref[...] = jnp.zeros_like(dgq_ref)
            dgk_ref[...] = jnp.zeros_like(dgk_ref)

        @pl.when(n == nblk)
        def _():
            dq_ref[...] = dq_keep_ref[...]
            dk_ref[...] = ck_ref[...]
            dv_ref[...] = cv_ref[...]

        bdv = bd_ref[...]
        gqv = gq_ref[...]
        gkv = gk_ref[...]
        band = jnp.concatenate([_band_mask(n > 0)] * (hw // HEAD_DIM), axis=0)
        lane = lax.broadcasted_iota(jnp.int32, (1, hw), 1)

        def residue(r):
            rr = _residue_rows(r, d)
            qn, qhat, qrstd = _head_norm(q_ref[rr, :], gqv, bdv)
            kn, khat, krstd = _head_norm(jnp.concatenate([kp_ref[rr, :], kc_ref[rr, :]], axis=0), gkv, bdv)
            knb = kn.astype(BF16)
            vb = jnp.concatenate([vp_ref[rr, :], vc_ref[rr, :]], axis=0).astype(BF16)
            qs = _stack_heads(qn, lane).astype(BF16)
            dos = _stack_heads(do_ref[rr, :], lane).astype(BF16)
            s = lax.dot_general(qs, knb, nt, preferred_element_type=F32) * (HEAD_DIM ** -0.5)
            prob = jnp.where(band, jnp.exp(s - _head_columns(lse_ref[rr, :], lane)), 0.0)
            dp = lax.dot_general(dos, vb, nt, preferred_element_type=F32)
            ds = (prob * (dp + _head_columns(c_ref[rr, :], lane)) * (HEAD_DIM ** -0.5)).astype(BF16)
            dqn = _unstack_heads(jnp.dot(ds, knb, preferred_element_type=F32), lane)
            dkn = lax.dot_general(ds, qs, tn, preferred_element_type=F32)
            dvv = lax.dot_general(prob.astype(BF16), dos, tn, preferred_element_type=F32)

            dq = _head_norm_bwd(dqn, gqv, qhat, qrstd, bdv)
            dq_ref[rr, :] = dq
            dq_keep_ref[rr, :] = dq
            dk2 = _head_norm_bwd(dkn, gkv, khat, krstd, bdv)
            dgq_ref[...] += jnp.sum(dqn * qhat, axis=0, keepdims=True)
            dgk_ref[...] += jnp.sum(dkn * khat, axis=0, keepdims=True)
            dk_ref[rr, :] = ck_ref[rr, :] + dk2[:BLK]
            dv_ref[rr, :] = cv_ref[rr, :] + dvv[:BLK]
            ck_ref[rr, :] = dk2[BLK:]
            cv_ref[rr, :] = dvv[BLK:]

        @pl.when(n < nblk)
        def _():
            _for_residues(d, residue)

    last = nblk - 1
    per = PW // hw
    cq, ck, cv = (OFF_Q + PW * g) // hw, (OFF_K + PW * g) // hw, (OFF_V + PW * g) // hw
    cur = lambda col: pl.BlockSpec((rows, hw), lambda h, n: (jnp.minimum(n, last), col + h))
    prv = lambda col: pl.BlockSpec((rows, hw), lambda h, n: (jnp.maximum(jnp.minimum(n, last) - 1, 0), col + h))
    cur3 = pl.BlockSpec((None, rows, hw), lambda h, n: (g, jnp.minimum(n, last), h))
    done = pl.BlockSpec((rows, hw), lambda h, n: (jnp.maximum(n - 1, 0), h))
    vec = pl.BlockSpec((1, hw), lambda h, n: (0, h))
    return pl.pallas_call(
        body, name=name, grid=(per, nblk + 1),
        in_specs=[cur(cq), cur(ck), prv(ck), cur(cv), prv(cv), cur(0), cur3, cur3, vec, vec,
                  pl.BlockSpec((hw, hw), lambda h, n: (0, 0))],
        out_specs=[cur(0), done, done, vec, vec],
        out_shape=[SDS((S, PW), F32)] * 3 + [SDS((1, PW), F32)] * 2,
        scratch_shapes=[pltpu.VMEM((rows, hw), F32)] * 3,
        compiler_params=_cp(2),
    )(p, p, p, p, p, lse, do3, c3, gq, gk, bd)


def _mix_fwd(name, os, lses):
    S = os[0].shape[0]
    tm = min(512, S)

    def body(o0, o1, o2, l0, l1, l2, y_ref):
        o = [o0[...], o1[...], o2[...]]
        l = [l0[...], l1[...], l2[...]]
        m = jnp.maximum(jnp.maximum(l[0], l[1]), l[2])
        e = [jnp.exp(t - m) for t in l]
        inv = 1.0 / (e[0] + e[1] + e[2])
        for g in range(N_PATTERNS):
            y_ref[:, PW * g:PW * (g + 1)] = (o[g] * (e[g] * inv)).astype(y_ref.dtype)

    blk = pl.BlockSpec((tm, PW), lambda i: (i, 0))
    return pl.pallas_call(
        body, name=name, grid=(S // tm,),
        in_specs=[blk] * 6,
        out_specs=pl.BlockSpec((tm, C_WIDTH), lambda i: (i, 0)),
        out_shape=SDS((S, C_WIDTH), BF16),
        compiler_params=_cp(1),
    )(*os, *lses)


def _mix_bwd(name, os, lses, dycat, bd):
    S = os[0].shape[0]
    tm = min(512, S)
    c0 = (A_WIDTH + B_WIDTH) // PW

    def body(o0, o1, o2, l0, l1, l2, dy0_ref, dy1_ref, dy2_ref, bd_ref, do_ref, c_ref):
        bdv = bd_ref[...]
        o = [o0[...], o1[...], o2[...]]
        l = [l0[...], l1[...], l2[...]]
        dys = [dy0_ref[...], dy1_ref[...], dy2_ref[...]]
        m = jnp.maximum(jnp.maximum(l[0], l[1]), l[2])
        e = [jnp.exp(t - m) for t in l]
        inv = 1.0 / (e[0] + e[1] + e[2])
        alpha = [t * inv for t in e]
        da = [_seg_sum(dys[g] * o[g], bdv) for g in range(N_PATTERNS)]
        mean_da = alpha[0] * da[0] + alpha[1] * da[1] + alpha[2] * da[2]
        for g in range(N_PATTERNS):
            do_ref[g] = dys[g] * alpha[g]
            c_ref[g] = -alpha[g] * mean_da

    blk = pl.BlockSpec((tm, PW), lambda i: (i, 0))
    blk3 = pl.BlockSpec((N_PATTERNS, tm, PW), lambda i: (0, i, 0))
    dyspec = lambda g: pl.BlockSpec((tm, PW), lambda i: (i, c0 + g))
    return pl.pallas_call(
        body, name=name, grid=(S // tm,),
        in_specs=[blk] * 6 + [dyspec(0), dyspec(1), dyspec(2), pl.BlockSpec((PW, PW), lambda i: (0, 0))],
        out_specs=[blk3, blk3],
        out_shape=[SDS((N_PATTERNS, S, PW), F32)] * 2,
        compiler_params=_cp(1),
    )(*os, *lses, dycat, dycat, dycat, bd)


def _mesh_pos():
    x, y, c = lax.axis_index("x"), lax.axis_index("y"), lax.axis_index("c")
    chips = [(1 - x, y), (x, 1 - y), (1 - x, 1 - y)]
    chip_idx = [2 * cx + cy for cx, cy in chips]
    return x, y, c, 2 * x + y, chips, chip_idx


def _place_shard(name, w, layer, chip_arr, out_dtype, deps=()):
    _, R, C = w.shape
    tr = min(256, R)

    def body(chip_ref, w_ref, *rest):
        o_ref = rest[-1]
        o_ref[...] = w_ref[...].astype(o_ref.dtype)

    return pl.pallas_call(
        body, name=name,
        grid_spec=pltpu.PrefetchScalarGridSpec(
            num_scalar_prefetch=1, grid=(R // tr,),
            in_specs=[pl.BlockSpec((None, tr, C), lambda i, chip_ref: (layer, i, 0))] + [_hbm_spec()] * len(deps),
            out_specs=pl.BlockSpec((None, tr, C), lambda i, chip_ref: (chip_ref[0], i, 0))),
        out_shape=SDS((N_CHIPS, R, C), out_dtype),
        compiler_params=_cp(1),
    )(chip_arr, w, *deps)


HBM_SPEC = pl.BlockSpec(memory_space=pltpu.HBM)
SEM_SPEC = pl.BlockSpec(memory_space=pltpu.SEMAPHORE)
SPLIT_COPY = pltpu.SideEffectType.DATAFLOW_SIDE_EFFECTING
N_PEER_CHIPS = N_CHIPS - 1
TOKEN_SHAPE = SDS((8, 128), F32)
TOKEN_SPEC = pl.BlockSpec(memory_space=pltpu.VMEM)


def _in_hbm(a):
    return pltpu.with_memory_space_constraint(a, pltpu.HBM)


def _gather_start(name, bufs):
    T = len(bufs)

    def body(*refs):
        ins = refs[:T]
        send_sems, recv_sems = refs[T:2 * T], refs[2 * T:3 * T]
        token = refs[4 * T]
        x, y, c, me, chips, chip_idx = _mesh_pos()
        for t in range(T):
            hr = ins[t].shape[1] // 2
            mine = ins[t].at[me, pl.ds(c * hr, hr), :]
            for j in range(N_PEER_CHIPS):
                pltpu.make_async_remote_copy(src_ref=mine, dst_ref=mine, send_sem=send_sems[t].at[j],
                                             recv_sem=recv_sems[t].at[j], device_id=(*chips[j], c),
                                             device_id_type=MESH).start()
        token[...] = jnp.zeros_like(token)

    sems = [pltpu.SemaphoreType.DMA((N_PEER_CHIPS,))] * T
    out = pl.pallas_call(
        body, name=name,
        in_specs=[HBM_SPEC] * T,
        out_specs=[SEM_SPEC] * (2 * T) + [HBM_SPEC] * T + [TOKEN_SPEC],
        out_shape=sems + sems + [pltpu.HBM(b.shape, b.dtype) for b in bufs] + [TOKEN_SHAPE],
        input_output_aliases={t: 2 * T + t for t in range(T)},
        compiler_params=pltpu.CompilerParams(has_side_effects=SPLIT_COPY),
    )(*[_in_hbm(b) for b in bufs])
    return out[:T], out[T:2 * T], out[2 * T:3 * T], out[3 * T]


def _gather_wait(name, buf, send_sem, recv_sem, after):
    n_in = 3 if after is None else 4

    def body(*refs):
        buf_ref, ssem, rsem = refs[:3]
        x, y, c, me, chips, chip_idx = _mesh_pos()
        hr = buf_ref.shape[1] // 2
        mine = buf_ref.at[me, pl.ds(c * hr, hr), :]
        for j in range(N_PEER_CHIPS):
            got = buf_ref.at[chip_idx[j], pl.ds(c * hr, hr), :]
            cp = pltpu.make_async_remote_copy(src_ref=mine, dst_ref=got, send_sem=ssem.at[j], recv_sem=rsem.at[j],
                                              device_id=(*chips[j], c), device_id_type=MESH)
            cp.wait_send()
            cp.wait_recv()

    args = [buf, send_sem, recv_sem] + ([] if after is None else [after])
    return pl.pallas_call(
        body, name=name,
        in_specs=[HBM_SPEC, SEM_SPEC, SEM_SPEC] + [_hbm_spec()] * (n_in - 3),
        out_specs=HBM_SPEC,
        out_shape=pltpu.HBM(buf.shape, buf.dtype),
        input_output_aliases={0: 0},
        compiler_params=pltpu.CompilerParams(has_side_effects=SPLIT_COPY),
    )(*args)


def _forward_start(name, buf):
    def body(buf_ref, send_sems, recv_sems, buf_thru, token):
        x, y, c, me, chips, chip_idx = _mesh_pos()
        hr = buf_ref.shape[1] // 2
        for j in range(N_PEER_CHIPS):
            got = buf_ref.at[chip_idx[j], pl.ds(c * hr, hr), :]
            pltpu.make_async_remote_copy(src_ref=got, dst_ref=got, send_sem=send_sems.at[j], recv_sem=recv_sems.at[j],
                                         device_id=(x, y, 1 - c), device_id_type=MESH).start()
        token[...] = jnp.zeros_like(token)

    sems = pltpu.SemaphoreType.DMA((N_PEER_CHIPS,))
    return pl.pallas_call(
        body, name=name,
        in_specs=[HBM_SPEC],
        out_specs=[SEM_SPEC, SEM_SPEC, HBM_SPEC, TOKEN_SPEC],
        out_shape=[sems, sems, pltpu.HBM(buf.shape, buf.dtype), TOKEN_SHAPE],
        input_output_aliases={0: 2},
        compiler_params=pltpu.CompilerParams(has_side_effects=SPLIT_COPY),
    )(_in_hbm(buf))


def _forward_wait(name, buf, send_sems, recv_sems, after):
    n_in = 3 if after is None else 4

    def body(*refs):
        buf_ref, ssems, rsems = refs[:3]
        x, y, c, me, chips, chip_idx = _mesh_pos()
        hr = buf_ref.shape[1] // 2
        for j in range(N_PEER_CHIPS):
            sent = buf_ref.at[chip_idx[j], pl.ds(c * hr, hr), :]
            theirs = buf_ref.at[chip_idx[j], pl.ds((1 - c) * hr, hr), :]
            cp = pltpu.make_async_remote_copy(src_ref=sent, dst_ref=theirs, send_sem=ssems.at[j],
                                              recv_sem=rsems.at[j], device_id=(x, y, 1 - c), device_id_type=MESH)
            cp.wait_send()
            cp.wait_recv()

    args = [buf, send_sems, recv_sems] + ([] if after is None else [after])
    return pl.pallas_call(
        body, name=name,
        in_specs=[HBM_SPEC, SEM_SPEC, SEM_SPEC] + [_hbm_spec()] * (n_in - 3),
        out_specs=HBM_SPEC,
        out_shape=pltpu.HBM(buf.shape, buf.dtype),
        input_output_aliases={0: 0},
        compiler_params=pltpu.CompilerParams(has_side_effects=SPLIT_COPY),
    )(*args)


class _GatheredWeights:
    def __init__(self):
        self._order = []
        self._pending = {}
        self._forwarding = {}
        self._ready = {}
        self._tokens = []

    def start(self, keys, bufs):
        send_sems, recv_sems, thru, token = _gather_start(f"gather_start_{len(self._order)}", bufs)
        self._tokens.append(token)
        self._order.extend(keys)
        self._pending.update({k: (b, s, r) for k, b, s, r in zip(keys, thru, send_sems, recv_sems)})

    def _prefetch(self, key, after):
        if key in self._pending:
            buf, ssem, rsem = self._pending.pop(key)
            tag = f"{key[0]}_{key[1]}"
            buf = _gather_wait(f"gather_wait_{tag}", buf, ssem, rsem, after)
            ssems, rsems, buf, token = _forward_start(f"gather_fwd_start_{tag}", buf)
            self._forwarding[key] = (buf, ssems, rsems)
            self._tokens.append(token)

    def get(self, name, layer, after=None, prefetch_next=True):
        key = (name, layer)
        if key not in self._ready:
            self._prefetch(key, after)
            buf, ssems, rsems = self._forwarding.pop(key)
            self._ready[key] = _forward_wait(f"gather_fwd_wait_{name}_{layer}", buf, ssems, rsems, after)
            if prefetch_next:
                self.prefetch_after(name, layer, after)
        return self._ready[key]

    def prefetch_after(self, name, layer, after):
        nxt = self._order.index((name, layer)) + 1
        if nxt < len(self._order):
            self._prefetch(self._order[nxt], after)

    def deps(self):
        tokens, self._tokens = self._tokens, []
        return tokens


def _swap_copy(g_ref, land_ref, send_sem, recv_sem):
    x, y, c, _, _, _ = _mesh_pos()
    hr = g_ref.shape[1] // 2
    return pltpu.make_async_remote_copy(src_ref=g_ref.at[:, pl.ds((1 - c) * hr, hr), :], dst_ref=land_ref,
                                        send_sem=send_sem, recv_sem=recv_sem, device_id=(x, y, 1 - c),
                                        device_id_type=MESH)


def _swap_start(name, g):
    land_shape = (g.shape[0], g.shape[1] // 2, g.shape[2])

    def body(g_ref, land_ref, send_sem, recv_sem, land_thru, token):
        _swap_copy(g_ref, land_ref, send_sem, recv_sem).start()
        token[...] = jnp.zeros_like(token)

    return pl.pallas_call(
        body, name=name,
        in_specs=[HBM_SPEC, HBM_SPEC],
        out_specs=[SEM_SPEC, SEM_SPEC, HBM_SPEC, TOKEN_SPEC],
        out_shape=[pltpu.SemaphoreType.DMA(()), pltpu.SemaphoreType.DMA(()), pltpu.HBM(land_shape, g.dtype),
                   TOKEN_SHAPE],
        input_output_aliases={1: 2},
        compiler_params=pltpu.CompilerParams(has_side_effects=SPLIT_COPY),
    )(_in_hbm(g), _in_hbm(lax.empty(land_shape, g.dtype)))


def _swap_wait(name, g, land, send_sem, recv_sem, after):
    def body(g_ref, land_ref, send_sem, recv_sem, after_ref, land_out):
        cp = _swap_copy(g_ref, land_ref, send_sem, recv_sem)
        cp.wait_send()
        cp.wait_recv()

    return pl.pallas_call(
        body, name=name,
        in_specs=[HBM_SPEC, HBM_SPEC, SEM_SPEC, SEM_SPEC, _hbm_spec()],
        out_specs=HBM_SPEC,
        out_shape=pltpu.HBM(land.shape, land.dtype),
        input_output_aliases={1: 0},
        compiler_params=pltpu.CompilerParams(has_side_effects=SPLIT_COPY),
    )(_in_hbm(g), land, send_sem, recv_sem, after)


def _add_my_half(name, g, r, pos_arr):
    ns, R, C = g.shape
    hr = R // 2
    tr = min(256, hr)
    nt = hr // tr

    def body(pos_ref, g_ref, r_ref, o_ref, land_ref):
        t = (g_ref[...] + r_ref[...]).astype(o_ref.dtype)
        o_ref[...] = t

        @pl.when(pl.program_id(1) == pos_ref[1])
        def _():
            land_ref[...] = t

    blk = pl.BlockSpec((None, tr, C), lambda i, s, pos_ref: (s, i, 0))
    return pl.pallas_call(
        body, name=name,
        grid_spec=pltpu.PrefetchScalarGridSpec(
            num_scalar_prefetch=1, grid=(nt, ns),
            in_specs=[pl.BlockSpec((None, tr, C), lambda i, s, pos_ref: (s, pos_ref[0] * nt + i, 0)), blk],
            out_specs=[blk, pl.BlockSpec((None, tr, C), lambda i, s, pos_ref: (pos_ref[1], i, 0))]),
        out_shape=[SDS((ns, hr, C), BF16)] * 2,
        compiler_params=_cp(2),
    )(pos_arr, g, r)


def _exchange_start(name, part, land):
    def body(part_ref, land_ref, send_sems, recv_sems, land_thru, token):
        x, y, c, me, chips, chip_idx = _mesh_pos()
        for j in range(N_PEER_CHIPS):
            pltpu.make_async_remote_copy(src_ref=part_ref.at[chip_idx[j]], dst_ref=land_ref.at[me],
                                         send_sem=send_sems.at[j], recv_sem=recv_sems.at[j],
                                         device_id=(*chips[j], c), device_id_type=MESH).start()
        token[...] = jnp.zeros_like(token)

    sems = pltpu.SemaphoreType.DMA((N_PEER_CHIPS,))
    return pl.pallas_call(
        body, name=name,
        in_specs=[HBM_SPEC, HBM_SPEC],
        out_specs=[SEM_SPEC, SEM_SPEC, HBM_SPEC, TOKEN_SPEC],
        out_shape=[sems, sems, pltpu.HBM(land.shape, land.dtype), TOKEN_SHAPE],
        input_output_aliases={1: 2},
        compiler_params=pltpu.CompilerParams(has_side_effects=SPLIT_COPY),
    )(_in_hbm(part), _in_hbm(land))


def _exchange_wait(name, part, land, send_sems, recv_sems, after):
    def body(part_ref, land_ref, send_sems, recv_sems, after_ref, land_out):
        x, y, c, me, chips, chip_idx = _mesh_pos()
        for j in range(N_PEER_CHIPS):
            cp = pltpu.make_async_remote_copy(src_ref=part_ref.at[chip_idx[j]], dst_ref=land_ref.at[chip_idx[j]],
                                              send_sem=send_sems.at[j], recv_sem=recv_sems.at[j],
                                              device_id=(*chips[j], c), device_id_type=MESH)
            cp.wait_send()
            cp.wait_recv()

    return pl.pallas_call(
        body, name=name,
        in_specs=[HBM_SPEC, HBM_SPEC, SEM_SPEC, SEM_SPEC, _hbm_spec()],
        out_specs=HBM_SPEC,
        out_shape=pltpu.HBM(land.shape, land.dtype),
        input_output_aliases={1: 0},
        compiler_params=pltpu.CompilerParams(has_side_effects=SPLIT_COPY),
    )(_in_hbm(part), land, send_sems, recv_sems, after)


class _GradReducer:
    def __init__(self, c_arr):
        self._c_arr = c_arr
        self._swapping = []
        self._exchanging = {}
        self._joining = {}
        self._tokens = []

    def begin(self, name, layer, g):
        tag = f"{name}_{layer}"
        ssem, rsem, land, token = _swap_start(f"rs_swap_start_{tag}", g)
        self._swapping.append((name, layer, g, ssem, rsem, land))
        self._tokens.append(token)

    def advance(self, after):
        for name, layer, g, ssem, rsem, land in self._swapping:
            tag = f"{name}_{layer}"
            theirs = _swap_wait(f"rs_swap_wait_{tag}", g, land, ssem, rsem, after)
            part, own = _add_my_half(f"rs_add_{tag}", g, theirs, self._c_arr)
            ssems, rsems, land2, token = _exchange_start(f"rs_xchg_start_{tag}", part, own)
            self._exchanging[(name, layer)] = (part, ssems, rsems, land2)
            self._tokens.append(token)
        self._swapping = []

    def deps(self):
        tokens, self._tokens = self._tokens, []
        return tokens

    def reduce(self, name, n_layers, after):
        buf = None
        for layer in range(n_layers):
            part, ssems, rsems, land = self._exchanging.pop((name, layer))
            tag = f"{name}_{layer}"
            landed = _exchange_wait(f"rs_xchg_wait_{tag}", part, land, ssems, rsems, after)
            buf = _sum_chips(f"rs_sum_{tag}", landed, self._c_arr, layer, n_layers, buf)
        ssem, rsem, buf, token = _join_start(f"rs_join_start_{name}", buf)
        self._joining[name] = (buf, ssem, rsem)
        return token

    def reduced(self, name, after):
        buf, ssem, rsem = self._joining.pop(name)
        return _join_wait(f"rs_join_wait_{name}", buf, ssem, rsem, after)


def _sum_chips(name, r, c_arr, layer, n_layers, prev):
    ns, H, C = r.shape
    tr = min(256, H)
    nt = H // tr

    def body(c_ref, r_ref, *rest):
        o_ref = rest[-1]
        o_ref[...] = ((r_ref[0].astype(F32) + r_ref[1].astype(F32)) + r_ref[2].astype(F32)) + r_ref[3].astype(F32)

    in_specs = [pl.BlockSpec((ns, tr, C), lambda i, c_ref: (0, i, 0))]
    args = [c_arr, r]
    aliases = {}
    if prev is not None:
        in_specs.append(_hbm_spec())
        args.append(prev)
        aliases = {2: 0}
    return pl.pallas_call(
        body, name=name,
        grid_spec=pltpu.PrefetchScalarGridSpec(
            num_scalar_prefetch=1, grid=(nt,), in_specs=in_specs,
            out_specs=pl.BlockSpec((None, tr, C), lambda i, c_ref: (layer, c_ref[0] * nt + i, 0))),
        out_shape=SDS((n_layers, 2 * H, C), F32),
        input_output_aliases=aliases,
        compiler_params=_cp(1),
    )(*args)


def _join_copy(buf_ref, send_sem, recv_sem):
    x, y, c, _, _, _ = _mesh_pos()
    hr = buf_ref.shape[1] // 2
    mine = buf_ref.at[:, pl.ds(c * hr, hr), :]
    theirs = buf_ref.at[:, pl.ds((1 - c) * hr, hr), :]
    send = pltpu.make_async_remote_copy(src_ref=mine, dst_ref=mine, send_sem=send_sem, recv_sem=recv_sem,
                                        device_id=(x, y, 1 - c), device_id_type=MESH)
    arrive = pltpu.make_async_remote_copy(src_ref=theirs, dst_ref=theirs, send_sem=send_sem, recv_sem=recv_sem,
                                          device_id=(x, y, 1 - c), device_id_type=MESH)
    return send, arrive


def _join_start(name, buf):
    def body(buf_ref, send_sem, recv_sem, buf_thru, token):
        _join_copy(buf_ref, send_sem, recv_sem)[0].start()
        token[...] = jnp.zeros_like(token)

    return pl.pallas_call(
        body, name=name,
        in_specs=[HBM_SPEC],
        out_specs=[SEM_SPEC, SEM_SPEC, HBM_SPEC, TOKEN_SPEC],
        out_shape=[pltpu.SemaphoreType.DMA(()), pltpu.SemaphoreType.DMA(()), pltpu.HBM(buf.shape, buf.dtype),
                   TOKEN_SHAPE],
        input_output_aliases={0: 2},
        compiler_params=pltpu.CompilerParams(has_side_effects=SPLIT_COPY),
    )(_in_hbm(buf))


def _join_wait(name, buf, send_sem, recv_sem, after):
    def body(buf_ref, send_sem, recv_sem, after_ref, buf_out):
        send, arrive = _join_copy(buf_ref, send_sem, recv_sem)
        send.wait_send()
        arrive.wait_recv()

    return pl.pallas_call(
        body, name=name,
        in_specs=[HBM_SPEC, SEM_SPEC, SEM_SPEC, _hbm_spec()],
        out_specs=HBM_SPEC,
        out_shape=pltpu.HBM(buf.shape, buf.dtype),
        input_output_aliases={0: 0},
        compiler_params=pltpu.CompilerParams(has_side_effects=SPLIT_COPY),
    )(buf, send_sem, recv_sem, after)


def _small_copy(k, buf_ref, land_ref, send_sems, recv_sems):
    x, y, c = lax.axis_index("x"), lax.axis_index("y"), lax.axis_index("c")
    me = 4 * x + 2 * y + c
    peer = (x ^ ((k >> 2) & 1), y ^ ((k >> 1) & 1), c ^ (k & 1))
    cp = pltpu.make_async_remote_copy(src_ref=buf_ref, dst_ref=land_ref.at[me], send_sem=send_sems.at[k - 1],
                                      recv_sem=recv_sems.at[k - 1], device_id=peer, device_id_type=MESH)
    return me, peer, cp


def _small_start(buf, deps):
    land = jnp.broadcast_to(buf[None], (N_DEV,) + buf.shape)
    n_dep = len(deps)

    def body(buf_ref, land_ref, *rest):
        send_sems, recv_sems, _, token = rest[n_dep:]
        for k in range(1, N_DEV):
            _small_copy(k, buf_ref, land_ref, send_sems, recv_sems)[2].start()
        token[...] = jnp.zeros_like(token)

    sems = pltpu.SemaphoreType.DMA((N_DEV - 1,))
    return pl.pallas_call(
        body, name="small_gather_start",
        in_specs=[HBM_SPEC, HBM_SPEC] + [_hbm_spec()] * n_dep,
        out_specs=[SEM_SPEC, SEM_SPEC, HBM_SPEC, TOKEN_SPEC],
        out_shape=[sems, sems, pltpu.HBM(land.shape, land.dtype), TOKEN_SHAPE],
        input_output_aliases={1: 2},
        compiler_params=pltpu.CompilerParams(has_side_effects=SPLIT_COPY),
    )(_in_hbm(buf), _in_hbm(land), *deps)


def _small_wait(buf, land, send_sems, recv_sems, after):
    def body(buf_ref, land_ref, send_sems, recv_sems, after_ref, land_out):
        for k in range(1, N_DEV):
            me, peer, cp = _small_copy(k, buf_ref, land_ref, send_sems, recv_sems)
            cp.wait_send()
            got = land_ref.at[me ^ k]
            pltpu.make_async_remote_copy(src_ref=got, dst_ref=got, send_sem=send_sems.at[k - 1],
                                         recv_sem=recv_sems.at[k - 1], device_id=peer,
                                         device_id_type=MESH).wait_recv()

    return pl.pallas_call(
        body, name="small_gather_wait",
        in_specs=[HBM_SPEC, HBM_SPEC, SEM_SPEC, SEM_SPEC, _hbm_spec()],
        out_specs=HBM_SPEC,
        out_shape=pltpu.HBM(land.shape, land.dtype),
        input_output_aliases={1: 0},
        compiler_params=pltpu.CompilerParams(has_side_effects=SPLIT_COPY),
    )(_in_hbm(buf), land, send_sems, recv_sems, after)


def _sum_devices(land):
    n, R, C = land.shape

    def body(land_ref, out_ref):
        acc = land_ref[0]
        for d in range(1, n):
            acc = acc + land_ref[d]
        out_ref[...] = acc

    return pl.pallas_call(
        body, name="small_sum",
        in_specs=[pl.BlockSpec(memory_space=pltpu.VMEM)],
        out_specs=pl.BlockSpec(memory_space=pltpu.VMEM),
        out_shape=SDS((R, C), land.dtype),
        compiler_params=pltpu.CompilerParams(vmem_limit_bytes=V7X_VMEM_LIMIT),
    )(land)


def _pack_rows(vectors):
    flat = jnp.concatenate([v.reshape(-1) for v in vectors])
    n = flat.shape[0]
    padded = -(-n // 1024) * 1024
    return jnp.pad(flat, (0, padded - n)).reshape(padded // 128, 128)


def _unpack_rows(buf, shapes):
    flat = buf.reshape(-1)
    out, off = [], 0
    for s in shapes:
        n = 1
        for dim in s:
            n *= dim
        out.append(flat[off:off + n].reshape(s))
        off += n
    return out


def _layer_forward(l, x, prm, wg):
    S, D = x.shape
    w_in = wg.get("w_in", l, x, prefetch_next=l > 0)
    p, h = _norm_matmul(f"in_proj_{l}", x, prm["attn_norm"][l], w_in, F32, deps=wg.deps())
    if l == 0:
        wg.prefetch_after("w_in", l, p)
    y_a = _sgu_fwd(f"sgu_fwd_{l}", p, prm["sgu_wt"][l], prm["sgu_bb"][l])
    y_b = _conv_fwd(f"conv_fwd_{l}", p, prm["conv_w"][l])
    os, lses = [], []
    for g in range(N_PATTERNS):
        o_g, lse_g = _attn_fwd(f"attn_fwd_{l}_{g}", p, g, prm["q_gain"][l], prm["k_gain"][l], prm["bd"])
        os.append(o_g)
        lses.append(lse_g)
    y_c = _mix_fwd(f"mix_fwd_{l}", os, lses)
    ycat = jnp.concatenate([y_a, y_b, y_c], axis=1)
    tmb, tnb = min(1024, S), min(1024, D)
    w_out = wg.get("w_out", l, ycat)
    kq = N_CHIPS * w_out.shape[1]
    x1 = _matmul(
        f"out_proj_{l}", ycat, w_out.reshape(kq, D), (S, D), F32, grid=(S // tmb, D // tnb, 1),
        a_spec=pl.BlockSpec((tmb, kq), lambda i, j, k: (i, 0)),
        b_spec=pl.BlockSpec((kq, tnb), lambda i, j, k: (0, j)),
        o_spec=pl.BlockSpec((tmb, tnb), lambda i, j, k: (i, j)),
        contract=(1, 0), acc_shape=(tmb, tnb),
        extras=(x,), extra_specs=(pl.BlockSpec((tmb, tnb), lambda i, j, k: (i, j)),),
        epi=lambda r, res: r + res, deps=wg.deps())
    w_mlp_in = wg.get("w_mlp_in", l, x1)
    r, h2 = _norm_matmul(f"mlp_in_{l}", x1, prm["mlp_norm"][l], w_mlp_in, BF16, deps=wg.deps(), post=_relu)
    w_mlp_out = wg.get("w_mlp_out", l, r)
    dff4 = w_mlp_out.shape[1]
    tk = min(2048, dff4)
    kpc = dff4 // tk
    x2 = _matmul(
        f"mlp_out_{l}", r, w_mlp_out, (S, D), F32, grid=(S // tmb, D // tnb, N_CHIPS * kpc),
        a_spec=pl.BlockSpec((tmb, tk), lambda i, j, k: (i, k)),
        b_spec=pl.BlockSpec((None, tk, tnb), lambda i, j, k: (k // kpc, k % kpc, j)),
        o_spec=pl.BlockSpec((tmb, tnb), lambda i, j, k: (i, j)),
        contract=(1, 0), acc_shape=(tmb, tnb), a_pre=_square,
        extras=(x1,), extra_specs=(pl.BlockSpec((tmb, tnb), lambda i, j, k: (i, j)),),
        epi=lambda acc, res: acc + res, deps=wg.deps())
    saved = dict(x=x, p=p, h=h, os=os, lses=lses, ycat=ycat, x1=x1, r=r, h2=h2)
    return x2, saved


def _layer_backward(l, dx2, dx2b, sv, prm, wg, sink):
    S, D = dx2.shape
    w_in, w_out = wg.get("w_in", l), wg.get("w_out", l)
    w_mlp_in, w_mlp_out = wg.get("w_mlp_in", l), wg.get("w_mlp_out", l)
    dff4 = w_mlp_in.shape[-1]
    dff = N_CHIPS * dff4

    tmb, tnb = min(1024, S), min(1024, D)
    da = _matmul(
        f"mlp_out_bwd_{l}", dx2b, w_mlp_out, (S, dff), BF16, grid=(S // tmb, N_CHIPS, 1),
        a_spec=pl.BlockSpec((tmb, D), lambda i, j, k: (i, 0)),
        b_spec=pl.BlockSpec((None, dff4, D), lambda i, j, k: (j, 0, 0)),
        o_spec=pl.BlockSpec((tmb, dff4), lambda i, j, k: (i, j)),
        contract=(1, 1), acc_shape=(tmb, dff4),
        extras=(sv["r"],), extra_specs=(pl.BlockSpec((tmb, dff4), lambda i, j, k: (i, j)),),
        epi=lambda acc, r: acc * (2.0 * r.astype(F32)), deps=sink.deps())
    tmw = min(512, dff4)
    mpc = dff4 // tmw
    g_w2 = _matmul(
        f"mlp_out_dw_{l}", sv["r"], dx2b, (N_CHIPS, dff4, D), F32, grid=(N_CHIPS * mpc, D // tnb, 1),
        a_spec=pl.BlockSpec((S, tmw), lambda i, j, k: (0, i)),
        b_spec=pl.BlockSpec((S, tnb), lambda i, j, k: (0, j)),
        o_spec=pl.BlockSpec((None, tmw, tnb), lambda i, j, k: (i // mpc, i % mpc, j)),
        contract=(0, 0), acc_shape=(tmw, tnb), a_pre=_square)
    sink.begin("w_mlp_out", l, g_w2)
    dh2 = _matmul(
        f"mlp_in_bwd_{l}", da, w_mlp_in, (S, D), F32, grid=(S // tmb, D // tnb, N_CHIPS),
        a_spec=pl.BlockSpec((tmb, dff4), lambda i, j, k: (i, k)),
        b_spec=pl.BlockSpec((None, tnb, dff4), lambda i, j, k: (k, j, 0)),
        o_spec=pl.BlockSpec((tmb, tnb), lambda i, j, k: (i, j)),
        contract=(1, 1), acc_shape=(tmb, tnb), deps=sink.deps())
    sink.advance(dh2)
    tmd = min(1024, D)
    nd = D // tmd
    tnf = min(1024, dff4)
    nf = dff4 // tnf
    g_w1 = _matmul(
        f"mlp_in_dw_{l}", sv["h2"], da, (N_CHIPS, D, dff4), F32, grid=(N_CHIPS * nd, nf, 1),
        a_spec=pl.BlockSpec((S, tmd), lambda i, j, k: (0, i % nd)),
        b_spec=pl.BlockSpec((S, tnf), lambda i, j, k: (0, (i // nd) * nf + j)),
        o_spec=pl.BlockSpec((None, tmd, tnf), lambda i, j, k: (i // nd, i % nd, j)),
        contract=(0, 0), acc_shape=(tmd, tnf))
    sink.begin("w_mlp_in", l, g_w1)
    dx1, dx1b, g_mlp_norm = _rmsnorm_bwd(f"mlp_norm_bwd_{l}", dh2, sv["x1"], prm["mlp_norm"][l], dx2,
                                         deps=sink.deps())

    rq = w_out.shape[1]
    dycat = _matmul(
        f"out_proj_bwd_{l}", dx1b, w_out, (S, N_CHIPS * rq), F32, grid=(S // tmb, N_CHIPS, 1),
        a_spec=pl.BlockSpec((tmb, D), lambda i, j, k: (i, 0)),
        b_spec=pl.BlockSpec((None, rq, D), lambda i, j, k: (j, 0, 0)),
        o_spec=pl.BlockSpec((tmb, rq), lambda i, j, k: (i, j)),
        contract=(1, 1), acc_shape=(tmb, rq))
    sink.advance(dycat)
    g_wout = _matmul(
        f"out_proj_dw_{l}", sv["ycat"], dx1b, (N_CHIPS, rq, D), F32, grid=(N_CHIPS, D // tnb, 1),
        a_spec=pl.BlockSpec((S, rq), lambda i, j, k: (0, i)),
        b_spec=pl.BlockSpec((S, tnb), lambda i, j, k: (0, j)),
        o_spec=pl.BlockSpec((None, rq, tnb), lambda i, j, k: (i, 0, j)),
        contract=(0, 0), acc_shape=(rq, tnb))
    sink.begin("w_out", l, g_wout)

    p = sv["p"]
    du, dv_a, g_sgu_w, db_lanes = _sgu_bwd(f"sgu_bwd_{l}", p, dycat, prm["sgu_wt"][l], prm["sgu_wtt"][l],
                                           prm["sgu_bb"][l])
    g_sgu_b = db_lanes[:, :A_HEADS].T
    db, dc, dxb, g_conv = _conv_bwd(f"conv_bwd_{l}", p, dycat, prm["conv_w"][l])
    do3, c3 = _mix_bwd(f"mix_bwd_{l}", sv["os"], sv["lses"], dycat, prm["bd"])
    dqs, dks, dvs, dgqs, dgks = [], [], [], [], []
    for g in range(N_PATTERNS):
        dq, dk, dv, dgq, dgk = _attn_bwd(f"attn_bwd_{l}_{g}", p, g, sv["lses"][g], do3, c3,
                                         prm["q_gain"][l], prm["k_gain"][l], prm["bd"])
        dqs.append(dq)
        dks.append(dk)
        dvs.append(dv)
        dgqs.append(dgq)
        dgks.append(dgk)
    g_q = jnp.concatenate(dgqs, axis=1).reshape(N_PATTERNS * PW // HEAD_DIM, HEAD_DIM).sum(axis=0)
    g_k = jnp.concatenate(dgks, axis=1).reshape(N_PATTERNS * PW // HEAD_DIM, HEAD_DIM).sum(axis=0)
    dp = jnp.concatenate([du, dv_a, db, dc, dxb] + [t.astype(BF16) for t in dqs + dks + dvs], axis=1)

    ns_in = w_in.shape[-1]
    tmh = min(512, D)
    nh = D // tmh
    g_win = _matmul(
        f"in_proj_dw_{l}", sv["h"], dp, (N_CHIPS, D, ns_in), F32, grid=(N_CHIPS * nh, 1, 1),
        a_spec=pl.BlockSpec((S, tmh), lambda i, j, k: (0, i % nh)),
        b_spec=pl.BlockSpec((S, ns_in), lambda i, j, k: (0, i // nh)),
        o_spec=pl.BlockSpec((None, tmh, ns_in), lambda i, j, k: (i // nh, i % nh, 0)),
        contract=(0, 0), acc_shape=(tmh, ns_in))
    sink.begin("w_in", l, g_win)
    dh = _matmul(
        f"in_proj_bwd_{l}", dp, w_in, (S, D), F32, grid=(S // tmb, D // tnb, N_CHIPS),
        a_spec=pl.BlockSpec((tmb, ns_in), lambda i, j, k: (i, k)),
        b_spec=pl.BlockSpec((None, tnb, ns_in), lambda i, j, k: (k, j, 0)),
        o_spec=pl.BlockSpec((tmb, tnb), lambda i, j, k: (i, j)),
        contract=(1, 1), acc_shape=(tmb, tnb), deps=sink.deps())
    sink.advance(dh)
    dx0, dx0b, g_attn_norm = _rmsnorm_bwd(f"attn_norm_bwd_{l}", dh, sv["x"], prm["attn_norm"][l], dx1,
                                          deps=sink.deps())

    big = dict(w_in=g_win, w_out=g_wout, w_mlp_in=g_w1, w_mlp_out=g_w2)
    small = dict(attn_norm=g_attn_norm.reshape(-1), sgu_w=g_sgu_w, sgu_b=g_sgu_b, conv_w=g_conv,
                 q_norm=g_q, k_norm=g_k, mlp_norm=g_mlp_norm.reshape(-1))
    return dx0, dx0b, big, small


BIG = ("w_in", "w_out", "w_mlp_in", "w_mlp_out")
SMALL_REPLICATED = ("attn_norm", "sgu_w", "sgu_b", "q_norm", "k_norm", "mlp_norm")


def _local_step(x, target, prm, wg, n_layers, sink):
    saved = []
    h = x
    for l in range(n_layers):
        h, sv = _layer_forward(l, h, prm, wg)
        saved.append(sv)
    dy, dyb, colsq = _loss_kernel(h, target)
    loss = 0.5 * jnp.sum(colsq) / x.shape[1]
    bigs, smalls = [None] * n_layers, [None] * n_layers
    for l in reversed(range(n_layers)):
        dy, dyb, bigs[l], smalls[l] = _layer_backward(l, dy, dyb, saved[l], prm, wg, sink)
    return loss, dy, bigs, smalls


def _prepare_params(attn_norm, sgu_w, sgu_b, conv_full, q_norm, k_norm, mlp_norm):
    n_layers = attn_norm.shape[0]
    tri = jnp.tril(sgu_w)
    idx = jnp.arange(PW)
    bd = (idx[:, None] // HEAD_DIM == idx[None, :] // HEAD_DIM).astype(BF16)
    return dict(
        attn_norm=[attn_norm[l][None, :] for l in range(n_layers)],
        mlp_norm=[mlp_norm[l][None, :] for l in range(n_layers)],
        sgu_wt=[tri[l].astype(BF16) for l in range(n_layers)],
        sgu_wtt=[tri[l].transpose(0, 2, 1).astype(BF16) for l in range(n_layers)],
        sgu_bb=[jnp.repeat(sgu_b[l].T, HEAD_DIM, axis=1) for l in range(n_layers)],
        conv_w=[conv_full[l] for l in range(n_layers)],
        q_gain=[jnp.tile(q_norm[l], PW // HEAD_DIM)[None, :] for l in range(n_layers)],
        k_gain=[jnp.tile(k_norm[l], PW // HEAD_DIM)[None, :] for l in range(n_layers)],
        bd=bd,
    )


def kernel(x, attn_norm, w_in, sgu_w, sgu_b, conv_w, q_norm, k_norm, w_out, mlp_norm, w_mlp_in, w_mlp_out, loss_target, m_attn_norm, m_w_in, m_sgu_w, m_sgu_b, m_conv_w, m_q_norm, m_k_norm, m_w_out, m_mlp_norm, m_w_mlp_in, m_w_mlp_out, v_attn_norm, v_w_in, v_sgu_w, v_sgu_b, v_conv_w, v_q_norm, v_k_norm, v_w_out, v_mlp_norm, v_w_mlp_in, v_w_mlp_out):
    n_layers = attn_norm.shape[0]
    weights = dict(attn_norm=attn_norm, w_in=w_in, sgu_w=sgu_w, sgu_b=sgu_b, conv_w=conv_w, q_norm=q_norm,
                   k_norm=k_norm, w_out=w_out, mlp_norm=mlp_norm, w_mlp_in=w_mlp_in, w_mlp_out=w_mlp_out)
    mom_m = dict(attn_norm=m_attn_norm, w_in=m_w_in, sgu_w=m_sgu_w, sgu_b=m_sgu_b, conv_w=m_conv_w,
                 q_norm=m_q_norm, k_norm=m_k_norm, w_out=m_w_out, mlp_norm=m_mlp_norm, w_mlp_in=m_w_mlp_in,
                 w_mlp_out=m_w_mlp_out)
    mom_v = dict(attn_norm=v_attn_norm, w_in=v_w_in, sgu_w=v_sgu_w, sgu_b=v_sgu_b, conv_w=v_conv_w,
                 q_norm=v_q_norm, k_norm=v_k_norm, w_out=v_w_out, mlp_norm=v_mlp_norm, w_mlp_in=v_w_mlp_in,
                 w_mlp_out=v_w_mlp_out)
    order = ("attn_norm", "w_in", "sgu_w", "sgu_b", "conv_w", "q_norm", "k_norm", "w_out", "mlp_norm",
             "w_mlp_in", "w_mlp_out")
    chip = 2 * lax.axis_index("x") + lax.axis_index("y")
    c_arr = jnp.stack([lax.axis_index("c"), chip]).astype(jnp.int32)

    conv_cols = conv_w.shape[-1]
    chip_arr = chip.astype(jnp.int32).reshape(1)
    conv_pack = jnp.pad(conv_w.reshape(-1), (0, 2048 - conv_w.size)).reshape(1, 16, 128)
    wg = _GatheredWeights()
    wg.start([("conv_w", 0), ("w_in", 0)],
             [_place_shard("place_conv_w", conv_pack, 0, chip_arr, F32),
              _place_shard("place_w_in_0", weights["w_in"], 0, chip_arr, BF16)])
    keys = [(n, l) for l in range(n_layers) for n in BIG if (n, l) != ("w_in", 0)]
    first = wg.deps()
    wg.start(keys, [_place_shard(f"place_{n}_{l}", weights[n], l, chip_arr, BF16, deps=first) for n, l in keys])
    conv_full = wg.get("conv_w", 0, wg.deps()[-1]).reshape(N_CHIPS, 2048)[:, :conv_w.size].reshape(N_CHIPS, n_layers, 3, conv_cols)
    conv_full = conv_full.transpose(1, 2, 0, 3).reshape(n_layers, 3, N_CHIPS * conv_cols)
    prm = _prepare_params(attn_norm, sgu_w, sgu_b, conv_full, q_norm, k_norm, mlp_norm)

    sink = _GradReducer(c_arr)
    loss_local, grad_x, _, smalls = _local_step(x[0], loss_target[0], prm, wg, n_layers, sink)
    loss = lax.psum(loss_local, ("x", "y", "c"))

    small_names = SMALL_REPLICATED + ("conv_w",)
    small_shapes = [(n_layers,) + tuple(smalls[0][n].shape) for n in small_names]
    packed = _pack_rows([jnp.stack([smalls[l][n] for l in range(n_layers)]) for n in small_names])
    small_send, small_recv, small_land, small_token = _small_start(packed, sink.deps())

    grads, delta, new_m, new_v = {}, {}, {}, {}

    def update(n, after):
        shp = weights[n].shape
        two_d = (shp[0] * shp[1], shp[2])
        d, nm, nv, g = _adamw(f"adamw_{n}", weights[n].reshape(two_d), sink.reduced(n, after).reshape(two_d),
                              mom_m[n].reshape(two_d), mom_v[n].reshape(two_d))
        grads[n], delta[n], new_m[n], new_v[n] = g.reshape(shp), d.reshape(shp), nm.reshape(shp), nv.reshape(shp)

    token = small_token
    for n in ("w_mlp_out", "w_mlp_in", "w_out"):
        token = sink.reduce(n, n_layers, token)
    update("w_mlp_out", token)
    token = sink.reduce("w_in", n_layers, delta["w_mlp_out"])
    update("w_mlp_in", token)
    update("w_out", delta["w_mlp_in"])
    update("w_in", delta["w_out"])
    small_land = _small_wait(packed, small_land, small_send, small_recv, delta["w_in"])
    grads.update(zip(small_names, _unpack_rows(_sum_devices(small_land), small_shapes)))
    grads["conv_w"] = lax.dynamic_slice_in_dim(grads["conv_w"], chip * conv_cols, conv_cols, axis=2)
    smalls_all = SMALL_REPLICATED + ("conv_w",)
    shapes = [weights[n].shape for n in smalls_all]
    d, nm, nv, _ = _adamw("adamw_small",
                          _pack_rows([weights[n] for n in smalls_all]), _pack_rows([grads[n] for n in smalls_all]),
                          _pack_rows([mom_m[n] for n in smalls_all]), _pack_rows([mom_v[n] for n in smalls_all]))
    for n, dd, mm, vv in zip(smalls_all, _unpack_rows(d, shapes), _unpack_rows(nm, shapes), _unpack_rows(nv, shapes)):
        delta[n], new_m[n], new_v[n] = dd, mm, vv

    return (loss, grad_x[None], *[grads[n] for n in order], *[delta[n] for n in order],
            *[new_m[n] for n in order], *[new_v[n] for n in order])
```

```python
import jax
import jax.numpy as jnp
from jax import lax
from jax.experimental import pallas as pl
from jax.experimental.pallas import tpu as pltpu

F32 = jnp.float32
BF16 = jnp.bfloat16
SDS = jax.ShapeDtypeStruct

EPS = 1e-6
HEAD_DIM = 64
A_HEADS = 8
A_WIDTH = 512
CHUNK = 128
B_WIDTH = 768
C_WIDTH = 768
N_PATTERNS = 3
PATTERN_DILATION = (1, 4, 16)
PW = 256
D_IN_PROJ = 5632
OFF_AU, OFF_AV, OFF_BB, OFF_BC, OFF_BX, OFF_Q, OFF_K, OFF_V = 0, 512, 1024, 1792, 2560, 3328, 4096, 4864
N_CHIPS = 4
N_DEV = 8
BLK = 128

ADAM_LR, ADAM_B1, ADAM_B2, ADAM_EPS, ADAM_WD, ADAM_STEP = 0.001, 0.9, 0.999, 1e-08, 0.01, 10

V7X_VMEM_LIMIT = 56 * 1024 * 1024
MESH = pl.DeviceIdType.MESH
NEG = -1e30


def _cp(n_axes):
    return pltpu.CompilerParams(dimension_semantics=("arbitrary",) * n_axes, vmem_limit_bytes=V7X_VMEM_LIMIT)


def _hbm_spec():
    return pl.BlockSpec(memory_space=pl.ANY)


def _norm_matmul(name, x, g, wg, out_dtype, deps=(), post=None):
    S, D = x.shape
    ns, _, Ns = wg.shape
    tm = min(512, S)
    n_dep = len(deps)

    def body(x_ref, g_ref, w_ref, *rest):
        o_ref, h_ref, hs_ref = rest[n_dep:]

        @pl.when(pl.program_id(1) == 0)
        def _():
            xv = x_ref[...]
            y = xv * lax.rsqrt(jnp.mean(xv * xv, axis=-1, keepdims=True) + EPS) * g_ref[...]
            hb = y.astype(BF16)
            hs_ref[...] = hb
            h_ref[...] = hb
        acc = jnp.dot(hs_ref[...], w_ref[...], preferred_element_type=F32)
        o_ref[...] = (acc if post is None else post(acc)).astype(o_ref.dtype)

    return pl.pallas_call(
        body, name=name, grid=(S // tm, ns),
        in_specs=[pl.BlockSpec((tm, D), lambda i, s: (i, 0)),
                  pl.BlockSpec((1, D), lambda i, s: (0, 0)),
                  pl.BlockSpec((None, D, Ns), lambda i, s: (s, 0, 0))] + [_hbm_spec()] * n_dep,
        out_specs=[pl.BlockSpec((tm, Ns), lambda i, s: (i, s)),
                   pl.BlockSpec((tm, D), lambda i, s: (i, 0))],
        out_shape=[SDS((S, ns * Ns), out_dtype), SDS((S, D), BF16)],
        scratch_shapes=[pltpu.VMEM((tm, D), BF16)],
        compiler_params=_cp(2),
    )(x, g, wg, *deps)


def _relu(t):
    return jnp.maximum(t, 0.0)


def _square(t):
    return t * t


def _matmul(name, a, b, out_shape, out_dtype, *, grid, a_spec, b_spec, o_spec, contract, acc_shape,
            extras=(), extra_specs=(), a_pre=None, epi=None, deps=()):
    nk = grid[2]
    n_ex = len(extras)
    n_dep = len(deps)
    dims = (((contract[0],), (contract[1],)), ((), ()))

    def product(a_ref, b_ref):
        av = a_ref[...] if a_pre is None else a_pre(a_ref[...])
        return lax.dot_general(av, b_ref[...], dims, preferred_element_type=F32)

    def finish(r, ex, o_ref):
        if epi is not None:
            r = epi(r, *[e[...] for e in ex])
        o_ref[...] = r.astype(o_ref.dtype)

    def body_single(a_ref, b_ref, *rest):
        finish(product(a_ref, b_ref), rest[:n_ex], rest[n_ex + n_dep])

    def body(a_ref, b_ref, *rest):
        ex = rest[:n_ex]
        o_ref = rest[n_ex + n_dep]
        acc_ref = rest[n_ex + n_dep + 1]
        k = pl.program_id(2)

        @pl.when(k == 0)
        def _():
            acc_ref[...] = product(a_ref, b_ref)

        @pl.when((k > 0) & (k < nk - 1))
        def _():
            acc_ref[...] += product(a_ref, b_ref)

        @pl.when(k == nk - 1)
        def _():
            finish(acc_ref[...] + product(a_ref, b_ref), ex, o_ref)

    return pl.pallas_call(
        body_single if nk == 1 else body, name=name, grid=grid,
        in_specs=[a_spec, b_spec, *extra_specs] + [_hbm_spec()] * n_dep,
        out_specs=o_spec,
        out_shape=SDS(out_shape, out_dtype),
        scratch_shapes=[] if nk == 1 else [pltpu.VMEM(acc_shape, F32)],
        compiler_params=_cp(3),
    )(a, b, *extras, *deps)


def _loss_kernel(y, t):
    S, D = y.shape
    tm = min(256, S)

    def body(y_ref, t_ref, dy_ref, dyb_ref, l_ref):
        @pl.when(pl.program_id(0) == 0)
        def _():
            l_ref[...] = jnp.zeros_like(l_ref)
        e = y_ref[...] - t_ref[...]
        l_ref[...] += jnp.sum(e * e, axis=0, keepdims=True)
        dy = e * (1.0 / D)
        dy_ref[...] = dy
        dyb_ref[...] = dy.astype(BF16)

    row = pl.BlockSpec((tm, D), lambda i: (i, 0))
    return pl.pallas_call(
        body, name="loss_head", grid=(S // tm,),
        in_specs=[row, row],
        out_specs=[row, row, pl.BlockSpec((1, D), lambda i: (0, 0))],
        out_shape=[SDS((S, D), F32), SDS((S, D), BF16), SDS((1, D), F32)],
        compiler_params=_cp(1),
    )(y, t)


def _rmsnorm_fwd(name, x, g):
    S, D = x.shape
    tm = min(512, S)

    def body(x_ref, g_ref, h_ref):
        xv = x_ref[...]
        y = xv * lax.rsqrt(jnp.mean(xv * xv, axis=-1, keepdims=True) + EPS) * g_ref[...]
        h_ref[...] = y.astype(h_ref.dtype)

    row = pl.BlockSpec((tm, D), lambda i: (i, 0))
    return pl.pallas_call(
        body, name=name, grid=(S // tm,),
        in_specs=[row, pl.BlockSpec((1, D), lambda i: (0, 0))],
        out_specs=row,
        out_shape=SDS((S, D), BF16),
        compiler_params=_cp(1),
    )(x, g)


def _rmsnorm_bwd(name, dh, x, g, dres, deps=()):
    S, D = x.shape
    tm = min(256, S)
    n_dep = len(deps)

    def body(dh_ref, x_ref, g_ref, dres_ref, *rest):
        dx_ref, dxb_ref, dg_ref = rest[n_dep:]
        @pl.when(pl.program_id(0) == 0)
        def _():
            dg_ref[...] = jnp.zeros_like(dg_ref)
        xv = x_ref[...]
        dhv = dh_ref[...]
        rstd = lax.rsqrt(jnp.mean(xv * xv, axis=-1, keepdims=True) + EPS)
        xhat = xv * rstd
        dg_ref[...] += jnp.sum(dhv * xhat, axis=0, keepdims=True)
        dxn = dhv * g_ref[...]
        dx = dres_ref[...] + rstd * (dxn - xhat * jnp.mean(dxn * xhat, axis=-1, keepdims=True))
        dx_ref[...] = dx
        dxb_ref[...] = dx.astype(BF16)

    row = pl.BlockSpec((tm, D), lambda i: (i, 0))
    vec = pl.BlockSpec((1, D), lambda i: (0, 0))
    return pl.pallas_call(
        body, name=name, grid=(S // tm,),
        in_specs=[row, row, vec, row] + [_hbm_spec()] * n_dep,
        out_specs=[row, row, vec],
        out_shape=[SDS((S, D), F32), SDS((S, D), BF16), SDS((1, D), F32)],
        compiler_params=_cp(1),
    )(dh, x, g, dres, *deps)


def _adamw(name, w, g, m, v):
    R, C = w.shape
    tr = 256 if R % 256 == 0 else R
    c1 = 1.0 - ADAM_B1 ** ADAM_STEP
    c2 = 1.0 - ADAM_B2 ** ADAM_STEP

    def body(w_ref, g_ref, m_ref, v_ref, d_ref, nm_ref, nv_ref, g_out_ref):
        gv = g_ref[...]
        nm = ADAM_B1 * m_ref[...] + (1.0 - ADAM_B1) * gv
        nv = ADAM_B2 * v_ref[...] + (1.0 - ADAM_B2) * (gv * gv)
        m_hat = nm / c1
        v_hat = nv / c2
        d_ref[...] = -ADAM_LR * (m_hat / (jnp.sqrt(v_hat) + ADAM_EPS) + ADAM_WD * w_ref[...])
        nm_ref[...] = nm
        nv_ref[...] = nv
        g_out_ref[...] = gv

    blk = pl.BlockSpec((tr, C), lambda i: (i, 0))
    return pl.pallas_call(
        body, name=name, grid=(R // tr,),
        in_specs=[blk] * 4, out_specs=[blk] * 4,
        out_shape=[SDS((R, C), F32)] * 4,
        compiler_params=_cp(1),
    )(w, g, m, v)


SGU_STEP_ROWS = 512


def _pair_select(lane, lo, hi):
    return jnp.where(lane < HEAD_DIM, lo, hi)


def _sgu_fwd(name, p, wt, bb):
    S = p.shape[0]

    rows = min(SGU_STEP_ROWS, S)

    def body(u_ref, v_ref, wt_ref, bb_ref, o_ref):
        lane = lax.broadcasted_iota(jnp.int32, (CHUNK, 128), 1)
        for ci in range(rows // CHUNK):
            rs = slice(CHUNK * ci, CHUNK * (ci + 1))
            for pp in range(A_HEADS // 2):
                cs = slice(128 * pp, 128 * (pp + 1))
                vb = v_ref[rs, cs].astype(BF16)
                mixed = _pair_select(lane,
                                     jnp.dot(wt_ref[2 * pp], vb, preferred_element_type=F32),
                                     jnp.dot(wt_ref[2 * pp + 1], vb, preferred_element_type=F32)) + bb_ref[:, cs]
                o_ref[rs, cs] = (u_ref[rs, cs] * mixed).astype(o_ref.dtype)

    return pl.pallas_call(
        body, name=name, grid=(S // rows,),
        in_specs=[pl.BlockSpec((rows, A_WIDTH), lambda c: (c, OFF_AU // A_WIDTH)),
                  pl.BlockSpec((rows, A_WIDTH), lambda c: (c, OFF_AV // A_WIDTH)),
                  pl.BlockSpec((A_HEADS, CHUNK, CHUNK), lambda c: (0, 0, 0)),
                  pl.BlockSpec((CHUNK, A_WIDTH), lambda c: (0, 0))],
        out_specs=pl.BlockSpec((rows, A_WIDTH), lambda c: (c, 0)),
        out_shape=SDS((S, A_WIDTH), BF16),
        compiler_params=_cp(1),
    )(p, p, wt, bb)


def _sgu_bwd(name, p, dycat, wt, wtt, bb):
    S = p.shape[0]
    rows = min(SGU_STEP_ROWS, S)

    def body(u_ref, v_ref, dy_ref, wt_ref, wtt_ref, bb_ref, du_ref, dv_ref, dw_ref, db_ref, dbacc_ref):
        c = pl.program_id(0)

        @pl.when(c == 0)
        def _():
            dw_ref[...] = jnp.zeros_like(dw_ref)
            dbacc_ref[...] = jnp.zeros_like(dbacc_ref)

        lane = lax.broadcasted_iota(jnp.int32, (CHUNK, 128), 1)
        row = lax.broadcasted_iota(jnp.int32, (CHUNK, 128), 0)
        causal = row >= lane
        nt = (((1,), (1,)), ((), ()))
        for pp in range(A_HEADS // 2):
            cs = slice(128 * pp, 128 * (pp + 1))
            dw_lo = jnp.zeros((CHUNK, CHUNK), F32)
            dw_hi = jnp.zeros((CHUNK, CHUNK), F32)
            dm_sum = jnp.zeros((CHUNK, 128), F32)
            for ci in range(rows // CHUNK):
                rs = slice(CHUNK * ci, CHUNK * (ci + 1))
                vb = v_ref[rs, cs].astype(BF16)
                dy = dy_ref[rs, cs]
                mixed = _pair_select(lane,
                                     jnp.dot(wt_ref[2 * pp], vb, preferred_element_type=F32),
                                     jnp.dot(wt_ref[2 * pp + 1], vb, preferred_element_type=F32)) + bb_ref[:, cs]
                du_ref[rs, cs] = (dy * mixed).astype(du_ref.dtype)
                dm = dy * u_ref[rs, cs]
                dmb = dm.astype(BF16)
                dv = _pair_select(lane,
                                  jnp.dot(wtt_ref[2 * pp], dmb, preferred_element_type=F32),
                                  jnp.dot(wtt_ref[2 * pp + 1], dmb, preferred_element_type=F32))
                dv_ref[rs, cs] = dv.astype(dv_ref.dtype)
                dm_sum += dm
                dm_lo = jnp.where(lane < HEAD_DIM, dm, 0.0).astype(BF16)
                dm_hi = jnp.where(lane >= HEAD_DIM, dm, 0.0).astype(BF16)
                dw_lo += lax.dot_general(dm_lo, vb, nt, preferred_element_type=F32)
                dw_hi += lax.dot_general(dm_hi, vb, nt, preferred_element_type=F32)
            dbacc_ref[:, cs] += dm_sum
            dw_ref[2 * pp] += jnp.where(causal, dw_lo, 0.0)
            dw_ref[2 * pp + 1] += jnp.where(causal, dw_hi, 0.0)

        @pl.when(c == S // rows - 1)
        def _():
            out = jnp.zeros((CHUNK, 128), F32)
            for pp in range(A_HEADS // 2):
                acc = dbacc_ref[:, 128 * pp:128 * (pp + 1)]
                s_lo = jnp.sum(jnp.where(lane < HEAD_DIM, acc, 0.0), axis=1, keepdims=True)
                s_hi = jnp.sum(jnp.where(lane >= HEAD_DIM, acc, 0.0), axis=1, keepdims=True)
                out = jnp.where(lane == 2 * pp, s_lo, out)
                out = jnp.where(lane == 2 * pp + 1, s_hi, out)
            db_ref[...] = out

    chunk = lambda col: pl.BlockSpec((rows, A_WIDTH), lambda c: (c, col))
    wspec = pl.BlockSpec((A_HEADS, CHUNK, CHUNK), lambda c: (0, 0, 0))
    return pl.pallas_call(
        body, name=name, grid=(S // rows,),
        in_specs=[chunk(OFF_AU // A_WIDTH), chunk(OFF_AV // A_WIDTH), chunk(0), wspec, wspec,
                  pl.BlockSpec((CHUNK, A_WIDTH), lambda c: (0, 0))],
        out_specs=[chunk(0), chunk(0), wspec, pl.BlockSpec((CHUNK, 128), lambda c: (0, 0))],
        out_shape=[SDS((S, A_WIDTH), BF16), SDS((S, A_WIDTH), BF16),
                   SDS((A_HEADS, CHUNK, CHUNK), F32), SDS((CHUNK, 128), F32)],
        scratch_shapes=[pltpu.VMEM((CHUNK, A_WIDTH), F32)],
        compiler_params=_cp(1),
    )(p, p, dycat, wt, wtt, bb)


CONV_HALO = 8
CONV_COLS = 256
CONV_ROWS = 1024


def _shift_down(a, halo, k):
    T = a.shape[0]
    row = lax.broadcasted_iota(jnp.int32, a.shape, 0)
    out = pltpu.roll(a, k, 0)
    for r in range(k):
        out = jnp.where(row == r, halo[CONV_HALO - k + r:CONV_HALO - k + r + 1, :], out)
    return out


def _shift_up(a, halo, k):
    T = a.shape[0]
    row = lax.broadcasted_iota(jnp.int32, a.shape, 0)
    out = pltpu.roll(a, T - k, 0)
    for r in range(k):
        out = jnp.where(row == T - k + r, halo[r:r + 1, :], out)
    return out


def _conv_specs(S, T):
    hb = T // CONV_HALO
    last = S // CONV_HALO - 1
    tile = lambda col0: pl.BlockSpec((T, CONV_COLS), lambda j, i: (i, col0 + j))
    prev = lambda col0: pl.BlockSpec((CONV_HALO, CONV_COLS), lambda j, i: (jnp.maximum(i * hb - 1, 0), col0 + j))
    nxt = lambda col0: pl.BlockSpec((CONV_HALO, CONV_COLS), lambda j, i: (jnp.minimum((i + 1) * hb, last), col0 + j))
    return tile, prev, nxt


def _conv_fwd(name, p, w):
    S = p.shape[0]
    T = min(CONV_ROWS, S)
    tile, prev, _ = _conv_specs(S, T)
    cb, cc, cx = OFF_BB // CONV_COLS, OFF_BC // CONV_COLS, OFF_BX // CONV_COLS

    def body(b_ref, c_ref, x_ref, ch_ref, xh_ref, w_ref, o_ref):
        i = pl.program_id(1)
        z = c_ref[...] * x_ref[...]
        zh = jnp.where(i > 0, ch_ref[...] * xh_ref[...], 0.0)
        z1 = _shift_down(z, zh, 1)
        z2 = _shift_down(z, zh, 2)
        conv = w_ref[0:1, :] * z2 + w_ref[1:2, :] * z1 + w_ref[2:3, :] * z
        o_ref[...] = (b_ref[...] * conv).astype(o_ref.dtype)

    return pl.pallas_call(
        body, name=name, grid=(B_WIDTH // CONV_COLS, S // T),
        in_specs=[tile(cb), tile(cc), tile(cx), prev(cc), prev(cx),
                  pl.BlockSpec((3, CONV_COLS), lambda j, i: (0, j))],
        out_specs=tile(0),
        out_shape=SDS((S, B_WIDTH), BF16),
        compiler_params=_cp(2),
    )(p, p, p, p, p, w)


def _conv_bwd(name, p, dycat, w):
    S = p.shape[0]
    T = min(CONV_ROWS, S)
    tile, prev, nxt = _conv_specs(S, T)
    cb, cc, cx = OFF_BB // CONV_COLS, OFF_BC // CONV_COLS, OFF_BX // CONV_COLS
    cdy = A_WIDTH // CONV_COLS
    n_i = S // T

    def body(b_ref, c_ref, x_ref, dy_ref, ch_ref, xh_ref, bn_ref, dyn_ref, w_ref,
             db_ref, dc_ref, dx_ref, dw_ref):
        i = pl.program_id(1)

        @pl.when(i == 0)
        def _():
            dw_ref[...] = jnp.zeros_like(dw_ref)

        cv = c_ref[...]
        xv = x_ref[...]
        z = cv * xv
        zh = jnp.where(i > 0, ch_ref[...] * xh_ref[...], 0.0)
        z1 = _shift_down(z, zh, 1)
        z2 = _shift_down(z, zh, 2)
        w0, w1, w2 = w_ref[0:1, :], w_ref[1:2, :], w_ref[2:3, :]
        conv = w0 * z2 + w1 * z1 + w2 * z
        dy = dy_ref[...]
        db_ref[...] = (dy * conv).astype(db_ref.dtype)
        dconv = dy * b_ref[...]
        dconv_n = jnp.where(i < n_i - 1, dyn_ref[...] * bn_ref[...], 0.0)
        dz = w2 * dconv + w1 * _shift_up(dconv, dconv_n, 1) + w0 * _shift_up(dconv, dconv_n, 2)
        dc_ref[...] = (dz * xv).astype(dc_ref.dtype)
        dx_ref[...] = (dz * cv).astype(dx_ref.dtype)
        dw_ref[0:1, :] += jnp.sum(dconv * z2, axis=0, keepdims=True)
        dw_ref[1:2, :] += jnp.sum(dconv * z1, axis=0, keepdims=True)
        dw_ref[2:3, :] += jnp.sum(dconv * z, axis=0, keepdims=True)

    wspec = pl.BlockSpec((3, CONV_COLS), lambda j, i: (0, j))
    return pl.pallas_call(
        body, name=name, grid=(B_WIDTH // CONV_COLS, n_i),
        in_specs=[tile(cb), tile(cc), tile(cx), tile(cdy), prev(cc), prev(cx), nxt(cb), nxt(cdy), wspec],
        out_specs=[tile(0), tile(0), tile(0), wspec],
        out_shape=[SDS((S, B_WIDTH), BF16)] * 3 + [SDS((3, B_WIDTH), F32)],
        compiler_params=_cp(2),
    )(p, p, p, dycat, p, p, p, dycat, w)


def _seg_sum(t, bd):
    hi = t.astype(BF16)
    lo = (t - hi.astype(F32)).astype(BF16)
    return jnp.dot(hi, bd, preferred_element_type=F32) + jnp.dot(lo, bd, preferred_element_type=F32)


def _head_norm(x, g, bd):
    rstd = lax.rsqrt(_seg_sum(x * x, bd) * (1.0 / HEAD_DIM) + EPS)
    xhat = x * rstd
    return xhat * g, xhat, rstd


def _head_norm_bwd(dy, g, xhat, rstd, bd):
    dxh = dy * g
    return rstd * (dxh - xhat * (_seg_sum(dxh * xhat, bd) * (1.0 / HEAD_DIM)))


def _band_mask(has_prev):
    row = lax.broadcasted_iota(jnp.int32, (BLK, 2 * BLK), 0)
    col = lax.broadcasted_iota(jnp.int32, (BLK, 2 * BLK), 1)
    first_key = jnp.where(has_prev, 0, BLK)
    return (col >= row) & (col <= row + BLK) & (col >= first_key)


def _residue_rows(r, d):
    return slice(None) if d == 1 else pl.ds(r, BLK, stride=d)


STRIDED_LANES = 128


def _step_width(d):
    return PW if d == 1 else STRIDED_LANES


def _n_stack(lane):
    return lane.shape[1] // HEAD_DIM


def _for_residues(d, fn):
    if d == 1:
        fn(0)
    else:
        def two(i, carry):
            fn(2 * i)
            fn(2 * i + 1)
            return carry
        lax.fori_loop(0, d // 2, two, 0)


def _head_mask(lane, j):
    return (lane >= HEAD_DIM * j) & (lane < HEAD_DIM * (j + 1))


def _stack_heads(x, lane):
    return jnp.concatenate([jnp.where(_head_mask(lane, j), x, 0.0) for j in range(_n_stack(lane))], axis=0)


def _unstack_heads(y, lane):
    out = y[:BLK]
    for j in range(1, _n_stack(lane)):
        out = jnp.where(lane >= HEAD_DIM * j, y[BLK * j:BLK * (j + 1)], out)
    return out


def _head_columns(v, lane):
    return jnp.concatenate([jnp.max(jnp.where(_head_mask(lane, j), v, NEG), axis=1, keepdims=True)
                            for j in range(_n_stack(lane))], axis=0)


def _attn_fwd(name, p, g, gq, gk, bd):
    S = p.shape[0]
    d = PATTERN_DILATION[g]
    rows = BLK * d
    hw = _step_width(d)
    nt = (((1,), (1,)), ((), ()))

    def body(q_ref, kc_ref, kp_ref, vc_ref, vp_ref, gq_ref, gk_ref, bd_ref, o_ref, lse_ref):
        has_prev = pl.program_id(1) > 0
        bdv = bd_ref[...]
        band = jnp.concatenate([_band_mask(has_prev)] * (hw // HEAD_DIM), axis=0)
        lane = lax.broadcasted_iota(jnp.int32, (1, hw), 1)

        def residue(r):
            rr = _residue_rows(r, d)
            qn, _, _ = _head_norm(q_ref[rr, :], gq_ref[...], bdv)
            kn, _, _ = _head_norm(jnp.concatenate([kp_ref[rr, :], kc_ref[rr, :]], axis=0), gk_ref[...], bdv)
            knb = kn.astype(BF16)
            vb = jnp.concatenate([vp_ref[rr, :], vc_ref[rr, :]], axis=0).astype(BF16)
            qs = _stack_heads(qn, lane).astype(BF16)
            s = lax.dot_general(qs, knb, nt, preferred_element_type=F32) * (HEAD_DIM ** -0.5)
            s = jnp.where(band, s, NEG)
            m = jnp.max(s, axis=1, keepdims=True)
            e = jnp.exp(s - m)
            den = jnp.sum(e, axis=1, keepdims=True)
            pv = jnp.dot(e.astype(BF16), vb, preferred_element_type=F32)
            o_ref[rr, :] = _unstack_heads(pv / den, lane)
            lse_ref[rr, :] = _unstack_heads(jnp.broadcast_to(m + jnp.log(den), pv.shape), lane)

        _for_residues(d, residue)

    per = PW // hw
    cq, ck, cv = (OFF_Q + PW * g) // hw, (OFF_K + PW * g) // hw, (OFF_V + PW * g) // hw
    cur = lambda col: pl.BlockSpec((rows, hw), lambda h, n: (n, col + h))
    prv = lambda col: pl.BlockSpec((rows, hw), lambda h, n: (jnp.maximum(n - 1, 0), col + h))
    vec = pl.BlockSpec((1, hw), lambda h, n: (0, h))
    return pl.pallas_call(
        body, name=name, grid=(per, S // rows),
        in_specs=[cur(cq), cur(ck), prv(ck), cur(cv), prv(cv), vec, vec, pl.BlockSpec((hw, hw), lambda h, n: (0, 0))],
        out_specs=[cur(0), cur(0)],
        out_shape=[SDS((S, PW), F32)] * 2,
        compiler_params=_cp(2),
    )(p, p, p, p, p, gq, gk, bd)


def _attn_bwd(name, p, g, lse, do3, c3, gq, gk, bd):
    S = p.shape[0]
    d = PATTERN_DILATION[g]
    rows = BLK * d
    nblk = S // rows
    hw = _step_width(d)
    nt = (((1,), (1,)), ((), ()))
    tn = (((0,), (0,)), ((), ()))

    def body(q_ref, kc_ref, kp_ref, vc_ref, vp_ref, lse_ref, do_ref, c_ref, gq_ref, gk_ref, bd_ref,
             dq_ref, dk_ref, dv_ref, dgq_ref, dgk_ref, ck_ref, cv_ref, dq_keep_ref):
        n = pl.program_id(1)

        @pl.when(n == 0)
        def _():
            ck_ref[...] = jnp.zeros_like(ck_ref)
            cv_ref[...] = jnp.zeros_like(cv_ref)
            dgq_ref[...] = jnp.zeros_like(dgq_ref)
            dgk_ref[...] = jnp.zeros_like(dgk_ref)

        @pl.when(n == nblk)
        def _():
            dq_ref[...] = dq_keep_ref[...]
            dk_ref[...] = ck_ref[...]
            dv_ref[...] = cv_ref[...]

        bdv = bd_ref[...]
        gqv = gq_ref[...]
        gkv = gk_ref[...]
        band = jnp.concatenate([_band_mask(n > 0)] * (hw // HEAD_DIM), axis=0)
        lane = lax.broadcasted_iota(jnp.int32, (1, hw), 1)

        def residue(r):
            rr = _residue_rows(r, d)
            qn, qhat, qrstd = _head_norm(q_ref[rr, :], gqv, bdv)
            kn, khat, krstd = _head_norm(jnp.concatenate([kp_ref[rr, :], kc_ref[rr, :]], axis=0), gkv, bdv)
            knb = kn.astype(BF16)
            vb = jnp.concatenate([vp_ref[rr, :], vc_ref[rr, :]], axis=0).astype(BF16)
            qs = _stack_heads(qn, lane).astype(BF16)
            dos = _stack_heads(do_ref[rr, :], lane).astype(BF16)
            s = lax.dot_general(qs, knb, nt, preferred_element_type=F32) * (HEAD_DIM ** -0.5)
            prob = jnp.where(band, jnp.exp(s - _head_columns(lse_ref[rr, :], lane)), 0.0)
            dp = lax.dot_general(dos, vb, nt, preferred_element_type=F32)
            ds = (prob * (dp + _head_columns(c_ref[rr, :], lane)) * (HEAD_DIM ** -0.5)).astype(BF16)
            dqn = _unstack_heads(jnp.dot(ds, knb, preferred_element_type=F32), lane)
            dkn = lax.dot_general(ds, qs, tn, preferred_element_type=F32)
            dvv = lax.dot_general(prob.astype(BF16), dos, tn, preferred_element_type=F32)

            dq = _head_norm_bwd(dqn, gqv, qhat, qrstd, bdv)
            dq_ref[rr, :] = dq
            dq_keep_ref[rr, :] = dq
            dk2 = _head_norm_bwd(dkn, gkv, khat, krstd, bdv)
            dgq_ref[...] += jnp.sum(dqn * qhat, axis=0, keepdims=True)
            dgk_ref[...] += jnp.sum(dkn * khat, axis=0, keepdims=True)
            dk_ref[rr, :] = ck_ref[rr, :] + dk2[:BLK]
            dv_ref[rr, :] = cv_ref[rr, :] + dvv[:BLK]
            ck_ref[rr, :] = dk2[BLK:]
            cv_ref[rr, :] = dvv[BLK:]

        @pl.when(n < nblk)
        def _():
            _for_residues(d, residue)

    last = nblk - 1
    per = PW // hw
    cq, ck, cv = (OFF_Q + PW * g) // hw, (OFF_K + PW * g) // hw, (OFF_V + PW * g) // hw
    cur = lambda col: pl.BlockSpec((rows, hw), lambda h, n: (jnp.minimum(n, last), col + h))
    prv = lambda col: pl.BlockSpec((rows, hw), lambda h, n: (jnp.maximum(jnp.minimum(n, last) - 1, 0), col + h))
    cur3 = pl.BlockSpec((None, rows, hw), lambda h, n: (g, jnp.minimum(n, last), h))
    done = pl.BlockSpec((rows, hw), lambda h, n: (jnp.maximum(n - 1, 0), h))
    vec = pl.BlockSpec((1, hw), lambda h, n: (0, h))
    return pl.pallas_call(
        body, name=name, grid=(per, nblk + 1),
        in_specs=[cur(cq), cur(ck), prv(ck), cur(cv), prv(cv), cur(0), cur3, cur3, vec, vec,
                  pl.BlockSpec((hw, hw), lambda h, n: (0, 0))],
        out_specs=[cur(0), done, done, vec, vec],
        out_shape=[SDS((S, PW), F32)] * 3 + [SDS((1, PW), F32)] * 2,
        scratch_shapes=[pltpu.VMEM((rows, hw), F32)] * 3,
        compiler_params=_cp(2),
    )(p, p, p, p, p, lse, do3, c3, gq, gk, bd)


def _mix_fwd(name, os, lses):
    S = os[0].shape[0]
    tm = min(512, S)

    def body(o0, o1, o2, l0, l1, l2, y_ref):
        o = [o0[...], o1[...], o2[...]]
        l = [l0[...], l1[...], l2[...]]
        m = jnp.maximum(jnp.maximum(l[0], l[1]), l[2])
        e = [jnp.exp(t - m) for t in l]
        inv = 1.0 / (e[0] + e[1] + e[2])
        for g in range(N_PATTERNS):
            y_ref[:, PW * g:PW * (g + 1)] = (o[g] * (e[g] * inv)).astype(y_ref.dtype)

    blk = pl.BlockSpec((tm, PW), lambda i: (i, 0))
    return pl.pallas_call(
        body, name=name, grid=(S // tm,),
        in_specs=[blk] * 6,
        out_specs=pl.BlockSpec((tm, C_WIDTH), lambda i: (i, 0)),
        out_shape=SDS((S, C_WIDTH), BF16),
        compiler_params=_cp(1),
    )(*os, *lses)


def _mix_bwd(name, os, lses, dycat, bd):
    S = os[0].shape[0]
    tm = min(512, S)
    c0 = (A_WIDTH + B_WIDTH) // PW

    def body(o0, o1, o2, l0, l1, l2, dy0_ref, dy1_ref, dy2_ref, bd_ref, do_ref, c_ref):
        bdv = bd_ref[...]
        o = [o0[...], o1[...], o2[...]]
        l = [l0[...], l1[...], l2[...]]
        dys = [dy0_ref[...], dy1_ref[...], dy2_ref[...]]
        m = jnp.maximum(jnp.maximum(l[0], l[1]), l[2])
        e = [jnp.exp(t - m) for t in l]
        inv = 1.0 / (e[0] + e[1] + e[2])
        alpha = [t * inv for t in e]
        da = [_seg_sum(dys[g] * o[g], bdv) for g in range(N_PATTERNS)]
        mean_da = alpha[0] * da[0] + alpha[1] * da[1] + alpha[2] * da[2]
        for g in range(N_PATTERNS):
            do_ref[g] = dys[g] * alpha[g]
            c_ref[g] = -alpha[g] * mean_da

    blk = pl.BlockSpec((tm, PW), lambda i: (i, 0))
    blk3 = pl.BlockSpec((N_PATTERNS, tm, PW), lambda i: (0, i, 0))
    dyspec = lambda g: pl.BlockSpec((tm, PW), lambda i: (i, c0 + g))
    return pl.pallas_call(
        body, name=name, grid=(S // tm,),
        in_specs=[blk] * 6 + [dyspec(0), dyspec(1), dyspec(2), pl.BlockSpec((PW, PW), lambda i: (0, 0))],
        out_specs=[blk3, blk3],
        out_shape=[SDS((N_PATTERNS, S, PW), F32)] * 2,
        compiler_params=_cp(1),
    )(*os, *lses, dycat, dycat, dycat, bd)


def _mesh_pos():
    x, y, c = lax.axis_index("x"), lax.axis_index("y"), lax.axis_index("c")
    chips = [(1 - x, y), (x, 1 - y), (1 - x, 1 - y)]
    chip_idx = [2 * cx + cy for cx, cy in chips]
    return x, y, c, 2 * x + y, chips, chip_idx


def _place_shard(name, w, layer, chip_arr, out_dtype, deps=()):
    _, R, C = w.shape
    tr = min(256, R)

    def body(chip_ref, w_ref, *rest):
        o_ref = rest[-1]
        o_ref[...] = w_ref[...].astype(o_ref.dtype)

    return pl.pallas_call(
        body, name=name,
        grid_spec=pltpu.PrefetchScalarGridSpec(
            num_scalar_prefetch=1, grid=(R // tr,),
            in_specs=[pl.BlockSpec((None, tr, C), lambda i, chip_ref: (layer, i, 0))] + [_hbm_spec()] * len(deps),
            out_specs=pl.BlockSpec((None, tr, C), lambda i, chip_ref: (chip_ref[0], i, 0))),
        out_shape=SDS((N_CHIPS, R, C), out_dtype),
        compiler_params=_cp(1),
    )(chip_arr, w, *deps)


HBM_SPEC = pl.BlockSpec(memory_space=pltpu.HBM)
SEM_SPEC = pl.BlockSpec(memory_space=pltpu.SEMAPHORE)
SPLIT_COPY = pltpu.SideEffectType.DATAFLOW_SIDE_EFFECTING
N_PEER_CHIPS = N_CHIPS - 1
TOKEN_SHAPE = SDS((8, 128), F32)
TOKEN_SPEC = pl.BlockSpec(memory_space=pltpu.VMEM)


def _in_hbm(a):
    return pltpu.with_memory_space_constraint(a, pltpu.HBM)


def _gather_start(name, bufs):
    T = len(bufs)

    def body(*refs):
        ins = refs[:T]
        send_sems, recv_sems = refs[T:2 * T], refs[2 * T:3 * T]
        token = refs[4 * T]
        x, y, c, me, chips, chip_idx = _mesh_pos()
        for t in range(T):
            hr = ins[t].shape[1] // 2
            mine = ins[t].at[me, pl.ds(c * hr, hr), :]
            for j in range(N_PEER_CHIPS):
                pltpu.make_async_remote_copy(src_ref=mine, dst_ref=mine, send_sem=send_sems[t].at[j],
                                             recv_sem=recv_sems[t].at[j], device_id=(*chips[j], c),
                                             device_id_type=MESH).start()
        token[...] = jnp.zeros_like(token)

    sems = [pltpu.SemaphoreType.DMA((N_PEER_CHIPS,))] * T
    out = pl.pallas_call(
        body, name=name,
        in_specs=[HBM_SPEC] * T,
        out_specs=[SEM_SPEC] * (2 * T) + [HBM_SPEC] * T + [TOKEN_SPEC],
        out_shape=sems + sems + [pltpu.HBM(b.shape, b.dtype) for b in bufs] + [TOKEN_SHAPE],
        input_output_aliases={t: 2 * T + t for t in range(T)},
        compiler_params=pltpu.CompilerParams(has_side_effects=SPLIT_COPY),
    )(*[_in_hbm(b) for b in bufs])
    return out[:T], out[T:2 * T], out[2 * T:3 * T], out[3 * T]


def _gather_wait(name, buf, send_sem, recv_sem, after):
    n_in = 3 if after is None else 4

    def body(*refs):
        buf_ref, ssem, rsem = refs[:3]
        x, y, c, me, chips, chip_idx = _mesh_pos()
        hr = buf_ref.shape[1] // 2
        mine = buf_ref.at[me, pl.ds(c * hr, hr), :]
        for j in range(N_PEER_CHIPS):
            got = buf_ref.at[chip_idx[j], pl.ds(c * hr, hr), :]
            cp = pltpu.make_async_remote_copy(src_ref=mine, dst_ref=got, send_sem=ssem.at[j], recv_sem=rsem.at[j],
                                              device_id=(*chips[j], c), device_id_type=MESH)
            cp.wait_send()
            cp.wait_recv()

    args = [buf, send_sem, recv_sem] + ([] if after is None else [after])
    return pl.pallas_call(
        body, name=name,
        in_specs=[HBM_SPEC, SEM_SPEC, SEM_SPEC] + [_hbm_spec()] * (n_in - 3),
        out_specs=HBM_SPEC,
        out_shape=pltpu.HBM(buf.shape, buf.dtype),
        input_output_aliases={0: 0},
        compiler_params=pltpu.CompilerParams(has_side_effects=SPLIT_COPY),
    )(*args)


def _forward_start(name, buf):
    def body(buf_ref, send_sems, recv_sems, buf_thru, token):
        x, y, c, me, chips, chip_idx = _mesh_pos()
        hr = buf_ref.shape[1] // 2
        for j in range(N_PEER_CHIPS):
            got = buf_ref.at[chip_idx[j], pl.ds(c * hr, hr), :]
            pltpu.make_async_remote_copy(src_ref=got, dst_ref=got, send_sem=send_sems.at[j], recv_sem=recv_sems.at[j],
                                         device_id=(x, y, 1 - c), device_id_type=MESH).start()
        token[...] = jnp.zeros_like(token)

    sems = pltpu.SemaphoreType.DMA((N_PEER_CHIPS,))
    return pl.pallas_call(
        body, name=name,
        in_specs=[HBM_SPEC],
        out_specs=[SEM_SPEC, SEM_SPEC, HBM_SPEC, TOKEN_SPEC],
        out_shape=[sems, sems, pltpu.HBM(buf.shape, buf.dtype), TOKEN_SHAPE],
        input_output_aliases={0: 2},
        compiler_params=pltpu.CompilerParams(has_side_effects=SPLIT_COPY),
    )(_in_hbm(buf))


def _forward_wait(name, buf, send_sems, recv_sems, after):
    n_in = 3 if after is None else 4

    def body(*refs):
        buf_ref, ssems, rsems = refs[:3]
        x, y, c, me, chips, chip_idx = _mesh_pos()
        hr = buf_ref.shape[1] // 2
        for j in range(N_PEER_CHIPS):
            sent = buf_ref.at[chip_idx[j], pl.ds(c * hr, hr), :]
            theirs = buf_ref.at[chip_idx[j], pl.ds((1 - c) * hr, hr), :]
            cp = pltpu.make_async_remote_copy(src_ref=sent, dst_ref=theirs, send_sem=ssems.at[j],
                                              recv_sem=rsems.at[j], device_id=(x, y, 1 - c), device_id_type=MESH)
            cp.wait_send()
            cp.wait_recv()

    args = [buf, send_sems, recv_sems] + ([] if after is None else [after])
    return pl.pallas_call(
        body, name=name,
        in_specs=[HBM_SPEC, SEM_SPEC, SEM_SPEC] + [_hbm_spec()] * (n_in - 3),
        out_specs=HBM_SPEC,
        out_shape=pltpu.HBM(buf.shape, buf.dtype),
        input_output_aliases={0: 0},
        compiler_params=pltpu.CompilerParams(has_side_effects=SPLIT_COPY),
    )(*args)


class _GatheredWeights:
    def __init__(self):
        self._order = []
        self._pending = {}
        self._forwarding = {}
        self._ready = {}
        self._tokens = []

    def start(self, keys, bufs):
        send_sems, recv_sems, thru, token = _gather_start(f"gather_start_{len(self._order)}", bufs)
        self._tokens.append(token)
        self._order.extend(keys)
        self._pending.update({k: (b, s, r) for k, b, s, r in zip(keys, thru, send_sems, recv_sems)})

    def _prefetch(self, key, after):
        if key in self._pending:
            buf, ssem, rsem = self._pending.pop(key)
            tag = f"{key[0]}_{key[1]}"
            buf = _gather_wait(f"gather_wait_{tag}", buf, ssem, rsem, after)
            ssems, rsems, buf, token = _forward_start(f"gather_fwd_start_{tag}", buf)
            self._forwarding[key] = (buf, ssems, rsems)
            self._tokens.append(token)

    def get(self, name, layer, after=None, prefetch_next=True):
        key = (name, layer)
        if key not in self._ready:
            self._prefetch(key, after)
            buf, ssems, rsems = self._forwarding.pop(key)
            self._ready[key] = _forward_wait(f"gather_fwd_wait_{name}_{layer}", buf, ssems, rsems, after)
            if prefetch_next:
                self.prefetch_after(name, layer, after)
        return self._ready[key]

    def prefetch_after(self, name, layer, after):
        nxt = self._order.index((name, layer)) + 1
        if nxt < len(self._order):
            self._prefetch(self._order[nxt], after)

    def deps(self):
        tokens, self._tokens = self._tokens, []
        return tokens


def _swap_copy(g_ref, land_ref, send_sem, recv_sem):
    x, y, c, _, _, _ = _mesh_pos()
    hr = g_ref.shape[1] // 2
    return pltpu.make_async_remote_copy(src_ref=g_ref.at[:, pl.ds((1 - c) * hr, hr), :], dst_ref=land_ref,
                                        send_sem=send_sem, recv_sem=recv_sem, device_id=(x, y, 1 - c),
                                        device_id_type=MESH)


def _swap_start(name, g):
    land_shape = (g.shape[0], g.shape[1] // 2, g.shape[2])

    def body(g_ref, land_ref, send_sem, recv_sem, land_thru, token):
        _swap_copy(g_ref, land_ref, send_sem, recv_sem).start()
        token[...] = jnp.zeros_like(token)

    return pl.pallas_call(
        body, name=name,
        in_specs=[HBM_SPEC, HBM_SPEC],
        out_specs=[SEM_SPEC, SEM_SPEC, HBM_SPEC, TOKEN_SPEC],
        out_shape=[pltpu.SemaphoreType.DMA(()), pltpu.SemaphoreType.DMA(()), pltpu.HBM(land_shape, g.dtype),
                   TOKEN_SHAPE],
        input_output_aliases={1: 2},
        compiler_params=pltpu.CompilerParams(has_side_effects=SPLIT_COPY),
    )(_in_hbm(g), _in_hbm(lax.empty(land_shape, g.dtype)))


def _swap_wait(name, g, land, send_sem, recv_sem, after):
    def body(g_ref, land_ref, send_sem, recv_sem, after_ref, land_out):
        cp = _swap_copy(g_ref, land_ref, send_sem, recv_sem)
        cp.wait_send()
        cp.wait_recv()

    return pl.pallas_call(
        body, name=name,
        in_specs=[HBM_SPEC, HBM_SPEC, SEM_SPEC, SEM_SPEC, _hbm_spec()],
        out_specs=HBM_SPEC,
        out_shape=pltpu.HBM(land.shape, land.dtype),
        input_output_aliases={1: 0},
        compiler_params=pltpu.CompilerParams(has_side_effects=SPLIT_COPY),
    )(_in_hbm(g), land, send_sem, recv_sem, after)


def _add_my_half(name, g, r, pos_arr):
    ns, R, C = g.shape
    hr = R // 2
    tr = min(256, hr)
    nt = hr // tr

    def body(pos_ref, g_ref, r_ref, o_ref, land_ref):
        t = (g_ref[...] + r_ref[...]).astype(o_ref.dtype)
        o_ref[...] = t

        @pl.when(pl.program_id(1) == pos_ref[1])
        def _():
            land_ref[...] = t

    blk = pl.BlockSpec((None, tr, C), lambda i, s, pos_ref: (s, i, 0))
    return pl.pallas_call(
        body, name=name,
        grid_spec=pltpu.PrefetchScalarGridSpec(
            num_scalar_prefetch=1, grid=(nt, ns),
            in_specs=[pl.BlockSpec((None, tr, C), lambda i, s, pos_ref: (s, pos_ref[0] * nt + i, 0)), blk],
            out_specs=[blk, pl.BlockSpec((None, tr, C), lambda i, s, pos_ref: (pos_ref[1], i, 0))]),
        out_shape=[SDS((ns, hr, C), BF16)] * 2,
        compiler_params=_cp(2),
    )(pos_arr, g, r)


def _exchange_start(name, part, land):
    def body(part_ref, land_ref, send_sems, recv_sems, land_thru, token):
        x, y, c, me, chips, chip_idx = _mesh_pos()
        for j in range(N_PEER_CHIPS):
            pltpu.make_async_remote_copy(src_ref=part_ref.at[chip_idx[j]], dst_ref=land_ref.at[me],
                                         send_sem=send_sems.at[j], recv_sem=recv_sems.at[j],
                                         device_id=(*chips[j], c), device_id_type=MESH).start()
        token[...] = jnp.zeros_like(token)

    sems = pltpu.SemaphoreType.DMA((N_PEER_CHIPS,))
    return pl.pallas_call(
        body, name=name,
        in_specs=[HBM_SPEC, HBM_SPEC],
        out_specs=[SEM_SPEC, SEM_SPEC, HBM_SPEC, TOKEN_SPEC],
        out_shape=[sems, sems, pltpu.HBM(land.shape, land.dtype), TOKEN_SHAPE],
        input_output_aliases={1: 2},
        compiler_params=pltpu.CompilerParams(has_side_effects=SPLIT_COPY),
    )(_in_hbm(part), _in_hbm(land))


def _exchange_wait(name, part, land, send_sems, recv_sems, after):
    def body(part_ref, land_ref, send_sems, recv_sems, after_ref, land_out):
        x, y, c, me, chips, chip_idx = _mesh_pos()
        for j in range(N_PEER_CHIPS):
            cp = pltpu.make_async_remote_copy(src_ref=part_ref.at[chip_idx[j]], dst_ref=land_ref.at[chip_idx[j]],
                                              send_sem=send_sems.at[j], recv_sem=recv_sems.at[j],
                                              device_id=(*chips[j], c), device_id_type=MESH)
            cp.wait_send()
            cp.wait_recv()

    return pl.pallas_call(
        body, name=name,
        in_specs=[HBM_SPEC, HBM_SPEC, SEM_SPEC, SEM_SPEC, _hbm_spec()],
        out_specs=HBM_SPEC,
        out_shape=pltpu.HBM(land.shape, land.dtype),
        input_output_aliases={1: 0},
        compiler_params=pltpu.CompilerParams(has_side_effects=SPLIT_COPY),
    )(_in_hbm(part), land, send_sems, recv_sems, after)


class _GradReducer:
    def __init__(self, c_arr):
        self._c_arr = c_arr
        self._swapping = []
        self._exchanging = {}
        self._joining = {}
        self._tokens = []

    def begin(self, name, layer, g):
        tag = f"{name}_{layer}"
        ssem, rsem, land, token = _swap_start(f"rs_swap_start_{tag}", g)
        self._swapping.append((name, layer, g, ssem, rsem, land))
        self._tokens.append(token)

    def advance(self, after):
        for name, layer, g, ssem, rsem, land in self._swapping:
            tag = f"{name}_{layer}"
            theirs = _swap_wait(f"rs_swap_wait_{tag}", g, land, ssem, rsem, after)
            part, own = _add_my_half(f"rs_add_{tag}", g, theirs, self._c_arr)
            ssems, rsems, land2, token = _exchange_start(f"rs_xchg_start_{tag}", part, own)
            self._exchanging[(name, layer)] = (part, ssems, rsems, land2)
            self._tokens.append(token)
        self._swapping = []

    def deps(self):
        tokens, self._tokens = self._tokens, []
        return tokens

    def reduce(self, name, n_layers, after):
        buf = None
        for layer in range(n_layers):
            part, ssems, rsems, land = self._exchanging.pop((name, layer))
            tag = f"{name}_{layer}"
            landed = _exchange_wait(f"rs_xchg_wait_{tag}", part, land, ssems, rsems, after)
            buf = _sum_chips(f"rs_sum_{tag}", landed, self._c_arr, layer, n_layers, buf)
        ssem, rsem, buf, token = _join_start(f"rs_join_start_{name}", buf)
        self._joining[name] = (buf, ssem, rsem)
        return token

    def reduced(self, name, after):
        buf, ssem, rsem = self._joining.pop(name)
        return _join_wait(f"rs_join_wait_{name}", buf, ssem, rsem, after)


def _sum_chips(name, r, c_arr, layer, n_layers, prev):
    ns, H, C = r.shape
    tr = min(256, H)
    nt = H // tr

    def body(c_ref, r_ref, *rest):
        o_ref = rest[-1]
        o_ref[...] = ((r_ref[0].astype(F32) + r_ref[1].astype(F32)) + r_ref[2].astype(F32)) + r_ref[3].astype(F32)

    in_specs = [pl.BlockSpec((ns, tr, C), lambda i, c_ref: (0, i, 0))]
    args = [c_arr, r]
    aliases = {}
    if prev is not None:
        in_specs.append(_hbm_spec())
        args.append(prev)
        aliases = {2: 0}
    return pl.pallas_call(
        body, name=name,
        grid_spec=pltpu.PrefetchScalarGridSpec(
            num_scalar_prefetch=1, grid=(nt,), in_specs=in_specs,
            out_specs=pl.BlockSpec((None, tr, C), lambda i, c_ref: (layer, c_ref[0] * nt + i, 0))),
        out_shape=SDS((n_layers, 2 * H, C), F32),
        input_output_aliases=aliases,
        compiler_params=_cp(1),
    )(*args)


def _join_copy(buf_ref, send_sem, recv_sem):
    x, y, c, _, _, _ = _mesh_pos()
    hr = buf_ref.shape[1] // 2
    mine = buf_ref.at[:, pl.ds(c * hr, hr), :]
    theirs = buf_ref.at[:, pl.ds((1 - c) * hr, hr), :]
    send = pltpu.make_async_remote_copy(src_ref=mine, dst_ref=mine, send_sem=send_sem, recv_sem=recv_sem,
                                        device_id=(x, y, 1 - c), device_id_type=MESH)
    arrive = pltpu.make_async_remote_copy(src_ref=theirs, dst_ref=theirs, send_sem=send_sem, recv_sem=recv_sem,
                                          device_id=(x, y, 1 - c), device_id_type=MESH)
    return send, arrive


def _join_start(name, buf):
    def body(buf_ref, send_sem, recv_sem, buf_thru, token):
        _join_copy(buf_ref, send_sem, recv_sem)[0].start()
        token[...] = jnp.zeros_like(token)

    return pl.pallas_call(
        body, name=name,
        in_specs=[HBM_SPEC],
        out_specs=[SEM_SPEC, SEM_SPEC, HBM_SPEC, TOKEN_SPEC],
        out_shape=[pltpu.SemaphoreType.DMA(()), pltpu.SemaphoreType.DMA(()), pltpu.HBM(buf.shape, buf.dtype),
                   TOKEN_SHAPE],
        input_output_aliases={0: 2},
        compiler_params=pltpu.CompilerParams(has_side_effects=SPLIT_COPY),
    )(_in_hbm(buf))


def _join_wait(name, buf, send_sem, recv_sem, after):
    def body(buf_ref, send_sem, recv_sem, after_ref, buf_out):
        send, arrive = _join_copy(buf_ref, send_sem, recv_sem)
        send.wait_send()
        arrive.wait_recv()

    return pl.pallas_call(
        body, name=name,
        in_specs=[HBM_SPEC, SEM_SPEC, SEM_SPEC, _hbm_spec()],
        out_specs=HBM_SPEC,
        out_shape=pltpu.HBM(buf.shape, buf.dtype),
        input_output_aliases={0: 0},
        compiler_params=pltpu.CompilerParams(has_side_effects=SPLIT_COPY),
    )(buf, send_sem, recv_sem, after)


def _small_copy(k, buf_ref, land_ref, send_sems, recv_sems):
    x, y, c = lax.axis_index("x"), lax.axis_index("y"), lax.axis_index("c")
    me = 4 * x + 2 * y + c
    peer = (x ^ ((k >> 2) & 1), y ^ ((k >> 1) & 1), c ^ (k & 1))
    cp = pltpu.make_async_remote_copy(src_ref=buf_ref, dst_ref=land_ref.at[me], send_sem=send_sems.at[k - 1],
                                      recv_sem=recv_sems.at[k - 1], device_id=peer, device_id_type=MESH)
    return me, peer, cp


def _small_start(buf, deps):
    land = jnp.broadcast_to(buf[None], (N_DEV,) + buf.shape)
    n_dep = len(deps)

    def body(buf_ref, land_ref, *rest):
        send_sems, recv_sems, _, token = rest[n_dep:]
        for k in range(1, N_DEV):
            _small_copy(k, buf_ref, land_ref, send_sems, recv_sems)[2].start()
        token[...] = jnp.zeros_like(token)

    sems = pltpu.SemaphoreType.DMA((N_DEV - 1,))
    return pl.pallas_call(
        body, name="small_gather_start",
        in_specs=[HBM_SPEC, HBM_SPEC] + [_hbm_spec()] * n_dep,
        out_specs=[SEM_SPEC, SEM_SPEC, HBM_SPEC, TOKEN_SPEC],
        out_shape=[sems, sems, pltpu.HBM(land.shape, land.dtype), TOKEN_SHAPE],
        input_output_aliases={1: 2},
        compiler_params=pltpu.CompilerParams(has_side_effects=SPLIT_COPY),
    )(_in_hbm(buf), _in_hbm(land), *deps)


def _small_wait(buf, land, send_sems, recv_sems, after):
    def body(buf_ref, land_ref, send_sems, recv_sems, after_ref, land_out):
        for k in range(1, N_DEV):
            me, peer, cp = _small_copy(k, buf_ref, land_ref, send_sems, recv_sems)
            cp.wait_send()
            got = land_ref.at[me ^ k]
            pltpu.make_async_remote_copy(src_ref=got, dst_ref=got, send_sem=send_sems.at[k - 1],
                                         recv_sem=recv_sems.at[k - 1], device_id=peer,
                                         device_id_type=MESH).wait_recv()

    return pl.pallas_call(
        body, name="small_gather_wait",
        in_specs=[HBM_SPEC, HBM_SPEC, SEM_SPEC, SEM_SPEC, _hbm_spec()],
        out_specs=HBM_SPEC,
        out_shape=pltpu.HBM(land.shape, land.dtype),
        input_output_aliases={1: 0},
        compiler_params=pltpu.CompilerParams(has_side_effects=SPLIT_COPY),
    )(_in_hbm(buf), land, send_sems, recv_sems, after)


def _sum_devices(land):
    n, R, C = land.shape

    def body(land_ref, out_ref):
        acc = land_ref[0]
        for d in range(1, n):
            acc = acc + land_ref[d]
        out_ref[...] = acc

    return pl.pallas_call(
        body, name="small_sum",
        in_specs=[pl.BlockSpec(memory_space=pltpu.VMEM)],
        out_specs=pl.BlockSpec(memory_space=pltpu.VMEM),
        out_shape=SDS((R, C), land.dtype),
        compiler_params=pltpu.CompilerParams(vmem_limit_bytes=V7X_VMEM_LIMIT),
    )(land)


def _pack_rows(vectors):
    flat = jnp.concatenate([v.reshape(-1) for v in vectors])
    n = flat.shape[0]
    padded = -(-n // 1024) * 1024
    return jnp.pad(flat, (0, padded - n)).reshape(padded // 128, 128)


def _unpack_rows(buf, shapes):
    flat = buf.reshape(-1)
    out, off = [], 0
    for s in shapes:
        n = 1
        for dim in s:
            n *= dim
        out.append(flat[off:off + n].reshape(s))
        off += n
    return out


def _layer_forward(l, x, prm, wg):
    S, D = x.shape
    h = _rmsnorm_fwd(f"attn_norm_{l}", x, prm["attn_norm"][l])
    w_in = wg.get("w_in", l, h, prefetch_next=l > 0)
    ns_in = w_in.shape[-1]
    tmi = min(1024, S)
    p = _matmul(
        f"in_proj_{l}", h, w_in, (S, N_CHIPS * ns_in), F32, grid=(S // tmi, N_CHIPS, 1),
        a_spec=pl.BlockSpec((tmi, D), lambda i, j, k: (i, 0)),
        b_spec=pl.BlockSpec((None, D, ns_in), lambda i, j, k: (j, 0, 0)),
        o_spec=pl.BlockSpec((tmi, ns_in), lambda i, j, k: (i, j)),
        contract=(1, 0), acc_shape=(tmi, ns_in), deps=wg.deps())
    if l == 0:
        wg.prefetch_after("w_in", l, p)
    y_a = _sgu_fwd(f"sgu_fwd_{l}", p, prm["sgu_wt"][l], prm["sgu_bb"][l])
    y_b = _conv_fwd(f"conv_fwd_{l}", p, prm["conv_w"][l])
    os, lses = [], []
    for g in range(N_PATTERNS):
        o_g, lse_g = _attn_fwd(f"attn_fwd_{l}_{g}", p, g, prm["q_gain"][l], prm["k_gain"][l], prm["bd"])
        os.append(o_g)
        lses.append(lse_g)
    y_c = _mix_fwd(f"mix_fwd_{l}", os, lses)
    ycat = jnp.concatenate([y_a, y_b, y_c], axis=1)
    tmb, tnb = min(1024, S), min(1024, D)
    w_out = wg.get("w_out", l, ycat)
    kq = N_CHIPS * w_out.shape[1]
    x1 = _matmul(
        f"out_proj_{l}", ycat, w_out.reshape(kq, D), (S, D), F32, grid=(S // tmb, D // tnb, 1),
        a_spec=pl.BlockSpec((tmb, kq), lambda i, j, k: (i, 0)),
        b_spec=pl.BlockSpec((kq, tnb), lambda i, j, k: (0, j)),
        o_spec=pl.BlockSpec((tmb, tnb), lambda i, j, k: (i, j)),
        contract=(1, 0), acc_shape=(tmb, tnb),
        extras=(x,), extra_specs=(pl.BlockSpec((tmb, tnb), lambda i, j, k: (i, j)),),
        epi=lambda r, res: r + res, deps=wg.deps())
    w_mlp_in = wg.get("w_mlp_in", l, x1)
    r, h2 = _norm_matmul(f"mlp_in_{l}", x1, prm["mlp_norm"][l], w_mlp_in, BF16, deps=wg.deps(), post=_relu)
    w_mlp_out = wg.get("w_mlp_out", l, r)
    dff4 = w_mlp_out.shape[1]
    tk = min(2048, dff4)
    kpc = dff4 // tk
    x2 = _matmul(
        f"mlp_out_{l}", r, w_mlp_out, (S, D), F32, grid=(S // tmb, D // tnb, N_CHIPS * kpc),
        a_spec=pl.BlockSpec((tmb, tk), lambda i, j, k: (i, k)),
        b_spec=pl.BlockSpec((None, tk, tnb), lambda i, j, k: (k // kpc, k % kpc, j)),
        o_spec=pl.BlockSpec((tmb, tnb), lambda i, j, k: (i, j)),
        contract=(1, 0), acc_shape=(tmb, tnb), a_pre=_square,
        extras=(x1,), extra_specs=(pl.BlockSpec((tmb, tnb), lambda i, j, k: (i, j)),),
        epi=lambda acc, res: acc + res, deps=wg.deps())
    saved = dict(x=x, p=p, h=h, os=os, lses=lses, ycat=ycat, x1=x1, r=r, h2=h2)
    return x2, saved


def _layer_backward(l, dx2, dx2b, sv, prm, wg, sink):
    S, D = dx2.shape
    w_in, w_out = wg.get("w_in", l), wg.get("w_out", l)
    w_mlp_in, w_mlp_out = wg.get("w_mlp_in", l), wg.get("w_mlp_out", l)
    dff4 = w_mlp_in.shape[-1]
    dff = N_CHIPS * dff4

    tmb, tnb = min(1024, S), min(1024, D)
    da = _matmul(
        f"mlp_out_bwd_{l}", dx2b, w_mlp_out, (S, dff), BF16, grid=(S // tmb, N_CHIPS, 1),
        a_spec=pl.BlockSpec((tmb, D), lambda i, j, k: (i, 0)),
        b_spec=pl.BlockSpec((None, dff4, D), lambda i, j, k: (j, 0, 0)),
        o_spec=pl.BlockSpec((tmb, dff4), lambda i, j, k: (i, j)),
        contract=(1, 1), acc_shape=(tmb, dff4),
        extras=(sv["r"],), extra_specs=(pl.BlockSpec((tmb, dff4), lambda i, j, k: (i, j)),),
        epi=lambda acc, r: acc * (2.0 * r.astype(F32)), deps=sink.deps())
    tmw = min(1024, dff4)
    mpc = dff4 // tmw
    g_w2 = _matmul(
        f"mlp_out_dw_{l}", sv["r"], dx2b, (N_CHIPS, dff4, D), F32, grid=(N_CHIPS * mpc, D // tnb, 1),
        a_spec=pl.BlockSpec((S, tmw), lambda i, j, k: (0, i)),
        b_spec=pl.BlockSpec((S, tnb), lambda i, j, k: (0, j)),
        o_spec=pl.BlockSpec((None, tmw, tnb), lambda i, j, k: (i // mpc, i % mpc, j)),
        contract=(0, 0), acc_shape=(tmw, tnb), a_pre=_square)
    sink.begin("w_mlp_out", l, g_w2)
    dh2 = _matmul(
        f"mlp_in_bwd_{l}", da, w_mlp_in, (S, D), F32, grid=(S // tmb, D // tnb, N_CHIPS),
        a_spec=pl.BlockSpec((tmb, dff4), lambda i, j, k: (i, k)),
        b_spec=pl.BlockSpec((None, tnb, dff4), lambda i, j, k: (k, j, 0)),
        o_spec=pl.BlockSpec((tmb, tnb), lambda i, j, k: (i, j)),
        contract=(1, 1), acc_shape=(tmb, tnb), deps=sink.deps())
    sink.advance(dh2)
    tmd = min(1024, D)
    nd = D // tmd
    tnf = min(1024, dff4)
    nf = dff4 // tnf
    g_w1 = _matmul(
        f"mlp_in_dw_{l}", sv["h2"], da, (N_CHIPS, D, dff4), F32, grid=(N_CHIPS * nd, nf, 1),
        a_spec=pl.BlockSpec((S, tmd), lambda i, j, k: (0, i % nd)),
        b_spec=pl.BlockSpec((S, tnf), lambda i, j, k: (0, (i // nd) * nf + j)),
        o_spec=pl.BlockSpec((None, tmd, tnf), lambda i, j, k: (i // nd, i % nd, j)),
        contract=(0, 0), acc_shape=(tmd, tnf))
    sink.begin("w_mlp_in", l, g_w1)
    dx1, dx1b, g_mlp_norm = _rmsnorm_bwd(f"mlp_norm_bwd_{l}", dh2, sv["x1"], prm["mlp_norm"][l], dx2,
                                         deps=sink.deps())

    rq = w_out.shape[1]
    dycat = _matmul(
        f"out_proj_bwd_{l}", dx1b, w_out, (S, N_CHIPS * rq), F32, grid=(S // tmb, N_CHIPS, 1),
        a_spec=pl.BlockSpec((tmb, D), lambda i, j, k: (i, 0)),
        b_spec=pl.BlockSpec((None, rq, D), lambda i, j, k: (j, 0, 0)),
        o_spec=pl.BlockSpec((tmb, rq), lambda i, j, k: (i, j)),
        contract=(1, 1), acc_shape=(tmb, rq))
    sink.advance(dycat)
    g_wout = _matmul(
        f"out_proj_dw_{l}", sv["ycat"], dx1b, (N_CHIPS, rq, D), F32, grid=(N_CHIPS, D // tnb, 1),
        a_spec=pl.BlockSpec((S, rq), lambda i, j, k: (0, i)),
        b_spec=pl.BlockSpec((S, tnb), lambda i, j, k: (0, j)),
        o_spec=pl.BlockSpec((None, rq, tnb), lambda i, j, k: (i, 0, j)),
        contract=(0, 0), acc_shape=(rq, tnb))
    sink.begin("w_out", l, g_wout)

    p = sv["p"]
    du, dv_a, g_sgu_w, db_lanes = _sgu_bwd(f"sgu_bwd_{l}", p, dycat, prm["sgu_wt"][l], prm["sgu_wtt"][l],
                                           prm["sgu_bb"][l])
    g_sgu_b = db_lanes[:, :A_HEADS].T
    db, dc, dxb, g_conv = _conv_bwd(f"conv_bwd_{l}", p, dycat, prm["conv_w"][l])
    do3, c3 = _mix_bwd(f"mix_bwd_{l}", sv["os"], sv["lses"], dycat, prm["bd"])
    dqs, dks, dvs, dgqs, dgks = [], [], [], [], []
    for g in range(N_PATTERNS):
        dq, dk, dv, dgq, dgk = _attn_bwd(f"attn_bwd_{l}_{g}", p, g, sv["lses"][g], do3, c3,
                                         prm["q_gain"][l], prm["k_gain"][l], prm["bd"])
        dqs.append(dq)
        dks.append(dk)
        dvs.append(dv)
        dgqs.append(dgq)
        dgks.append(dgk)
    g_q = jnp.concatenate(dgqs, axis=1).reshape(N_PATTERNS * PW // HEAD_DIM, HEAD_DIM).sum(axis=0)
    g_k = jnp.concatenate(dgks, axis=1).reshape(N_PATTERNS * PW // HEAD_DIM, HEAD_DIM).sum(axis=0)
    dp = jnp.concatenate([du, dv_a, db, dc, dxb] + [t.astype(BF16) for t in dqs + dks + dvs], axis=1)

    ns_in = w_in.shape[-1]
    tmh = min(512, D)
    nh = D // tmh
    g_win = _matmul(
        f"in_proj_dw_{l}", sv["h"], dp, (N_CHIPS, D, ns_in), F32, grid=(N_CHIPS * nh, 1, 1),
        a_spec=pl.BlockSpec((S, tmh), lambda i, j, k: (0, i % nh)),
        b_spec=pl.BlockSpec((S, ns_in), lambda i, j, k: (0, i // nh)),
        o_spec=pl.BlockSpec((None, tmh, ns_in), lambda i, j, k: (i // nh, i % nh, 0)),
        contract=(0, 0), acc_shape=(tmh, ns_in))
    sink.begin("w_in", l, g_win)
    dh = _matmul(
        f"in_proj_bwd_{l}", dp, w_in, (S, D), F32, grid=(S // tmb, D // tnb, N_CHIPS),
        a_spec=pl.BlockSpec((tmb, ns_in), lambda i, j, k: (i, k)),
        b_spec=pl.BlockSpec((None, tnb, ns_in), lambda i, j, k: (k, j, 0)),
        o_spec=pl.BlockSpec((tmb, tnb), lambda i, j, k: (i, j)),
        contract=(1, 1), acc_shape=(tmb, tnb), deps=sink.deps())
    sink.advance(dh)
    dx0, dx0b, g_attn_norm = _rmsnorm_bwd(f"attn_norm_bwd_{l}", dh, sv["x"], prm["attn_norm"][l], dx1,
                                          deps=sink.deps())

    big = dict(w_in=g_win, w_out=g_wout, w_mlp_in=g_w1, w_mlp_out=g_w2)
    small = dict(attn_norm=g_attn_norm.reshape(-1), sgu_w=g_sgu_w, sgu_b=g_sgu_b, conv_w=g_conv,
                 q_norm=g_q, k_norm=g_k, mlp_norm=g_mlp_norm.reshape(-1))
    return dx0, dx0b, big, small


BIG = ("w_in", "w_out", "w_mlp_in", "w_mlp_out")
SMALL_REPLICATED = ("attn_norm", "sgu_w", "sgu_b", "q_norm", "k_norm", "mlp_norm")


def _local_step(x, target, prm, wg, n_layers, sink):
    saved = []
    h = x
    for l in range(n_layers):
        h, sv = _layer_forward(l, h, prm, wg)
        saved.append(sv)
    dy, dyb, colsq = _loss_kernel(h, target)
    loss = 0.5 * jnp.sum(colsq) / x.shape[1]
    bigs, smalls = [None] * n_layers, [None] * n_layers
    for l in reversed(range(n_layers)):
        dy, dyb, bigs[l], smalls[l] = _layer_backward(l, dy, dyb, saved[l], prm, wg, sink)
    return loss, dy, bigs, smalls


def _prepare_params(attn_norm, sgu_w, sgu_b, conv_full, q_norm, k_norm, mlp_norm):
    n_layers = attn_norm.shape[0]
    tri = jnp.tril(sgu_w)
    idx = jnp.arange(PW)
    bd = (idx[:, None] // HEAD_DIM == idx[None, :] // HEAD_DIM).astype(BF16)
    return dict(
        attn_norm=[attn_norm[l][None, :] for l in range(n_layers)],
        mlp_norm=[mlp_norm[l][None, :] for l in range(n_layers)],
        sgu_wt=[tri[l].astype(BF16) for l in range(n_layers)],
        sgu_wtt=[tri[l].transpose(0, 2, 1).astype(BF16) for l in range(n_layers)],
        sgu_bb=[jnp.repeat(sgu_b[l].T, HEAD_DIM, axis=1) for l in range(n_layers)],
        conv_w=[conv_full[l] for l in range(n_layers)],
        q_gain=[jnp.tile(q_norm[l], PW // HEAD_DIM)[None, :] for l in range(n_layers)],
        k_gain=[jnp.tile(k_norm[l], PW // HEAD_DIM)[None, :] for l in range(n_layers)],
        bd=bd,
    )


def kernel(x, attn_norm, w_in, sgu_w, sgu_b, conv_w, q_norm, k_norm, w_out, mlp_norm, w_mlp_in, w_mlp_out, loss_target, m_attn_norm, m_w_in, m_sgu_w, m_sgu_b, m_conv_w, m_q_norm, m_k_norm, m_w_out, m_mlp_norm, m_w_mlp_in, m_w_mlp_out, v_attn_norm, v_w_in, v_sgu_w, v_sgu_b, v_conv_w, v_q_norm, v_k_norm, v_w_out, v_mlp_norm, v_w_mlp_in, v_w_mlp_out):
    n_layers = attn_norm.shape[0]
    weights = dict(attn_norm=attn_norm, w_in=w_in, sgu_w=sgu_w, sgu_b=sgu_b, conv_w=conv_w, q_norm=q_norm,
                   k_norm=k_norm, w_out=w_out, mlp_norm=mlp_norm, w_mlp_in=w_mlp_in, w_mlp_out=w_mlp_out)
    mom_m = dict(attn_norm=m_attn_norm, w_in=m_w_in, sgu_w=m_sgu_w, sgu_b=m_sgu_b, conv_w=m_conv_w,
                 q_norm=m_q_norm, k_norm=m_k_norm, w_out=m_w_out, mlp_norm=m_mlp_norm, w_mlp_in=m_w_mlp_in,
                 w_mlp_out=m_w_mlp_out)
    mom_v = dict(attn_norm=v_attn_norm, w_in=v_w_in, sgu_w=v_sgu_w, sgu_b=v_sgu_b, conv_w=v_conv_w,
                 q_norm=v_q_norm, k_norm=v_k_norm, w_out=v_w_out, mlp_norm=v_mlp_norm, w_mlp_in=v_w_mlp_in,
                 w_mlp_out=v_w_mlp_out)
    order = ("attn_norm", "w_in", "sgu_w", "sgu_b", "conv_w", "q_norm", "k_norm", "w_out", "mlp_norm",
             "w_mlp_in", "w_mlp_out")
    chip = 2 * lax.axis_index("x") + lax.axis_index("y")
    c_arr = jnp.stack([lax.axis_index("c"), chip]).astype(jnp.int32)

    conv_cols = conv_w.shape[-1]
    chip_arr = chip.astype(jnp.int32).reshape(1)
    conv_pack = jnp.pad(conv_w.reshape(-1), (0, 2048 - conv_w.size)).reshape(1, 16, 128)
    wg = _GatheredWeights()
    wg.start([("conv_w", 0), ("w_in", 0)],
             [_place_shard("place_conv_w", conv_pack, 0, chip_arr, F32),
              _place_shard("place_w_in_0", weights["w_in"], 0, chip_arr, BF16)])
    keys = [(n, l) for l in range(n_layers) for n in BIG if (n, l) != ("w_in", 0)]
    first = wg.deps()
    wg.start(keys, [_place_shard(f"place_{n}_{l}", weights[n], l, chip_arr, BF16, deps=first) for n, l in keys])
    conv_full = wg.get("conv_w", 0, wg.deps()[-1]).reshape(N_CHIPS, 2048)[:, :conv_w.size].reshape(N_CHIPS, n_layers, 3, conv_cols)
    conv_full = conv_full.transpose(1, 2, 0, 3).reshape(n_layers, 3, N_CHIPS * conv_cols)
    prm = _prepare_params(attn_norm, sgu_w, sgu_b, conv_full, q_norm, k_norm, mlp_norm)

    sink = _GradReducer(c_arr)
    loss_local, grad_x, _, smalls = _local_step(x[0], loss_target[0], prm, wg, n_layers, sink)
    loss = lax.psum(loss_local, ("x", "y", "c"))

    small_names = SMALL_REPLICATED + ("conv_w",)
    small_shapes = [(n_layers,) + tuple(smalls[0][n].shape) for n in small_names]
    packed = _pack_rows([jnp.stack([smalls[l][n] for l in range(n_layers)]) for n in small_names])
    small_send, small_recv, small_land, small_token = _small_start(packed, sink.deps())

    grads, delta, new_m, new_v = {}, {}, {}, {}

    def update(n, after):
        shp = weights[n].shape
        two_d = (shp[0] * shp[1], shp[2])
        d, nm, nv, g = _adamw(f"adamw_{n}", weights[n].reshape(two_d), sink.reduced(n, after).reshape(two_d),
                              mom_m[n].reshape(two_d), mom_v[n].reshape(two_d))
        grads[n], delta[n], new_m[n], new_v[n] = g.reshape(shp), d.reshape(shp), nm.reshape(shp), nv.reshape(shp)

    token = small_token
    for n in ("w_mlp_out", "w_mlp_in", "w_out"):
        token = sink.reduce(n, n_layers, token)
    update("w_mlp_out", token)
    token = sink.reduce("w_in", n_layers, delta["w_mlp_out"])
    update("w_mlp_in", token)
    update("w_out", delta["w_mlp_in"])
    update("w_in", delta["w_out"])
    small_land = _small_wait(packed, small_land, small_send, small_recv, delta["w_in"])
    grads.update(zip(small_names, _unpack_rows(_sum_devices(small_land), small_shapes)))
    grads["conv_w"] = lax.dynamic_slice_in_dim(grads["conv_w"], chip * conv_cols, conv_cols, axis=2)
    smalls_all = SMALL_REPLICATED + ("conv_w",)
    shapes = [weights[n].shape for n in smalls_all]
    d, nm, nv, _ = _adamw("adamw_small",
                          _pack_rows([weights[n] for n in smalls_all]), _pack_rows([grads[n] for n in smalls_all]),
                          _pack_rows([mom_m[n] for n in smalls_all]), _pack_rows([mom_v[n] for n in smalls_all]))
    for n, dd, mm, vv in zip(smalls_all, _unpack_rows(d, shapes), _unpack_rows(nm, shapes), _unpack_rows(nv, shapes)):
        delta[n], new_m[n], new_v[n] = dd, mm, vv

    return (loss, grad_x[None], *[grads[n] for n in order], *[delta[n] for n in order],
            *[new_m[n] for n in order], *[new_v[n] for n in order])
```

```python
import jax
import jax.numpy as jnp
from jax import lax
from jax.experimental import pallas as pl
from jax.experimental.pallas import tpu as pltpu

F32 = jnp.float32
BF16 = jnp.bfloat16
SDS = jax.ShapeDtypeStruct

EPS = 1e-6
HEAD_DIM = 64
A_HEADS = 8
A_WIDTH = 512
CHUNK = 128
B_WIDTH = 768
C_WIDTH = 768
N_PATTERNS = 3
PATTERN_DILATION = (1, 4, 16)
PW = 256
D_IN_PROJ = 5632
OFF_AU, OFF_AV, OFF_BB, OFF_BC, OFF_BX, OFF_Q, OFF_K, OFF_V = 0, 512, 1024, 1792, 2560, 3328, 4096, 4864
N_CHIPS = 4
N_DEV = 8
BLK = 128

ADAM_LR, ADAM_B1, ADAM_B2, ADAM_EPS, ADAM_WD, ADAM_STEP = 0.001, 0.9, 0.999, 1e-08, 0.01, 10

V7X_VMEM_LIMIT = 56 * 1024 * 1024
MESH = pl.DeviceIdType.MESH
NEG = -1e30


def _cp(n_axes):
    return pltpu.CompilerParams(dimension_semantics=("arbitrary",) * n_axes, vmem_limit_bytes=V7X_VMEM_LIMIT)


def _hbm_spec():
    return pl.BlockSpec(memory_space=pl.ANY)


def _norm_matmul(name, x, g, wg, out_dtype, deps=(), post=None):
    S, D = x.shape
    ns, _, Ns = wg.shape
    tm = min(512, S)
    n_dep = len(deps)

    def body(x_ref, g_ref, w_ref, *rest):
        o_ref, h_ref, hs_ref = rest[n_dep:]

        @pl.when(pl.program_id(1) == 0)
        def _():
            xv = x_ref[...]
            y = xv * lax.rsqrt(jnp.mean(xv * xv, axis=-1, keepdims=True) + EPS) * g_ref[...]
            hb = y.astype(BF16)
            hs_ref[...] = hb
            h_ref[...] = hb
        acc = jnp.dot(hs_ref[...], w_ref[...], preferred_element_type=F32)
        o_ref[...] = (acc if post is None else post(acc)).astype(o_ref.dtype)

    return pl.pallas_call(
        body, name=name, grid=(S // tm, ns),
        in_specs=[pl.BlockSpec((tm, D), lambda i, s: (i, 0)),
                  pl.BlockSpec((1, D), lambda i, s: (0, 0)),
                  pl.BlockSpec((None, D, Ns), lambda i, s: (s, 0, 0))] + [_hbm_spec()] * n_dep,
        out_specs=[pl.BlockSpec((tm, Ns), lambda i, s: (i, s)),
                   pl.BlockSpec((tm, D), lambda i, s: (i, 0))],
        out_shape=[SDS((S, ns * Ns), out_dtype), SDS((S, D), BF16)],
        scratch_shapes=[pltpu.VMEM((tm, D), BF16)],
        compiler_params=_cp(2),
    )(x, g, wg, *deps)


def _relu(t):
    return jnp.maximum(t, 0.0)


def _square(t):
    return t * t


def _matmul(name, a, b, out_shape, out_dtype, *, grid, a_spec, b_spec, o_spec, contract, acc_shape,
            extras=(), extra_specs=(), a_pre=None, epi=None, deps=()):
    nk = grid[2]
    n_ex = len(extras)
    n_dep = len(deps)
    dims = (((contract[0],), (contract[1],)), ((), ()))

    def product(a_ref, b_ref):
        av = a_ref[...] if a_pre is None else a_pre(a_ref[...])
        return lax.dot_general(av, b_ref[...], dims, preferred_element_type=F32)

    def finish(r, ex, o_ref):
        if epi is not None:
            r = epi(r, *[e[...] for e in ex])
        o_ref[...] = r.astype(o_ref.dtype)

    def body_single(a_ref, b_ref, *rest):
        finish(product(a_ref, b_ref), rest[:n_ex], rest[n_ex + n_dep])

    def body(a_ref, b_ref, *rest):
        ex = rest[:n_ex]
        o_ref = rest[n_ex + n_dep]
        acc_ref = rest[n_ex + n_dep + 1]
        k = pl.program_id(2)

        @pl.when(k == 0)
        def _():
            acc_ref[...] = product(a_ref, b_ref)

        @pl.when((k > 0) & (k < nk - 1))
        def _():
            acc_ref[...] += product(a_ref, b_ref)

        @pl.when(k == nk - 1)
        def _():
            finish(acc_ref[...] + product(a_ref, b_ref), ex, o_ref)

    return pl.pallas_call(
        body_single if nk == 1 else body, name=name, grid=grid,
        in_specs=[a_spec, b_spec, *extra_specs] + [_hbm_spec()] * n_dep,
        out_specs=o_spec,
        out_shape=SDS(out_shape, out_dtype),
        scratch_shapes=[] if nk == 1 else [pltpu.VMEM(acc_shape, F32)],
        compiler_params=_cp(3),
    )(a, b, *extras, *deps)


def _loss_kernel(y, t):
    S, D = y.shape
    tm = min(256, S)

    def body(y_ref, t_ref, dy_ref, dyb_ref, l_ref):
        @pl.when(pl.program_id(0) == 0)
        def _():
            l_ref[...] = jnp.zeros_like(l_ref)
        e = y_ref[...] - t_ref[...]
        l_ref[...] += jnp.sum(e * e, axis=0, keepdims=True)
        dy = e * (1.0 / D)
        dy_ref[...] = dy
        dyb_ref[...] = dy.astype(BF16)

    row = pl.BlockSpec((tm, D), lambda i: (i, 0))
    return pl.pallas_call(
        body, name="loss_head", grid=(S // tm,),
        in_specs=[row, row],
        out_specs=[row, row, pl.BlockSpec((1, D), lambda i: (0, 0))],
        out_shape=[SDS((S, D), F32), SDS((S, D), BF16), SDS((1, D), F32)],
        compiler_params=_cp(1),
    )(y, t)


def _rmsnorm_fwd(name, x, g):
    S, D = x.shape
    tm = min(512, S)

    def body(x_ref, g_ref, h_ref):
        xv = x_ref[...]
        y = xv * lax.rsqrt(jnp.mean(xv * xv, axis=-1, keepdims=True) + EPS) * g_ref[...]
        h_ref[...] = y.astype(h_ref.dtype)

    row = pl.BlockSpec((tm, D), lambda i: (i, 0))
    return pl.pallas_call(
        body, name=name, grid=(S // tm,),
        in_specs=[row, pl.BlockSpec((1, D), lambda i: (0, 0))],
        out_specs=row,
        out_shape=SDS((S, D), BF16),
        compiler_params=_cp(1),
    )(x, g)


def _rmsnorm_bwd(name, dh, x, g, dres, deps=()):
    S, D = x.shape
    tm = min(256, S)
    n_dep = len(deps)

    def body(dh_ref, x_ref, g_ref, dres_ref, *rest):
        dx_ref, dxb_ref, dg_ref = rest[n_dep:]
        @pl.when(pl.program_id(0) == 0)
        def _():
            dg_ref[...] = jnp.zeros_like(dg_ref)
        xv = x_ref[...]
        dhv = dh_ref[...]
        rstd = lax.rsqrt(jnp.mean(xv * xv, axis=-1, keepdims=True) + EPS)
        xhat = xv * rstd
        dg_ref[...] += jnp.sum(dhv * xhat, axis=0, keepdims=True)
        dxn = dhv * g_ref[...]
        dx = dres_ref[...] + rstd * (dxn - xhat * jnp.mean(dxn * xhat, axis=-1, keepdims=True))
        dx_ref[...] = dx
        dxb_ref[...] = dx.astype(BF16)

    row = pl.BlockSpec((tm, D), lambda i: (i, 0))
    vec = pl.BlockSpec((1, D), lambda i: (0, 0))
    return pl.pallas_call(
        body, name=name, grid=(S // tm,),
        in_specs=[row, row, vec, row] + [_hbm_spec()] * n_dep,
        out_specs=[row, row, vec],
        out_shape=[SDS((S, D), F32), SDS((S, D), BF16), SDS((1, D), F32)],
        compiler_params=_cp(1),
    )(dh, x, g, dres, *deps)


def _adamw(name, w, g, m, v):
    R, C = w.shape
    tr = 256 if R % 256 == 0 else R
    c1 = 1.0 - ADAM_B1 ** ADAM_STEP
    c2 = 1.0 - ADAM_B2 ** ADAM_STEP

    def body(w_ref, g_ref, m_ref, v_ref, d_ref, nm_ref, nv_ref, g_out_ref):
        gv = g_ref[...]
        nm = ADAM_B1 * m_ref[...] + (1.0 - ADAM_B1) * gv
        nv = ADAM_B2 * v_ref[...] + (1.0 - ADAM_B2) * (gv * gv)
        m_hat = nm / c1
        v_hat = nv / c2
        d_ref[...] = -ADAM_LR * (m_hat / (jnp.sqrt(v_hat) + ADAM_EPS) + ADAM_WD * w_ref[...])
        nm_ref[...] = nm
        nv_ref[...] = nv
        g_out_ref[...] = gv

    blk = pl.BlockSpec((tr, C), lambda i: (i, 0))
    return pl.pallas_call(
        body, name=name, grid=(R // tr,),
        in_specs=[blk] * 4, out_specs=[blk] * 4,
        out_shape=[SDS((R, C), F32)] * 4,
        compiler_params=_cp(1),
    )(w, g, m, v)


SGU_STEP_ROWS = 512


def _pair_select(lane, lo, hi):
    return jnp.where(lane < HEAD_DIM, lo, hi)


def _sgu_fwd(name, p, wt, bb):
    S = p.shape[0]

    rows = min(SGU_STEP_ROWS, S)

    def body(u_ref, v_ref, wt_ref, bb_ref, o_ref):
        lane = lax.broadcasted_iota(jnp.int32, (CHUNK, 128), 1)
        for ci in range(rows // CHUNK):
            rs = slice(CHUNK * ci, CHUNK * (ci + 1))
            for pp in range(A_HEADS // 2):
                cs = slice(128 * pp, 128 * (pp + 1))
                vb = v_ref[rs, cs].astype(BF16)
                mixed = _pair_select(lane,
                                     jnp.dot(wt_ref[2 * pp], vb, preferred_element_type=F32),
                                     jnp.dot(wt_ref[2 * pp + 1], vb, preferred_element_type=F32)) + bb_ref[:, cs]
                o_ref[rs, cs] = (u_ref[rs, cs] * mixed).astype(o_ref.dtype)

    return pl.pallas_call(
        body, name=name, grid=(S // rows,),
        in_specs=[pl.BlockSpec((rows, A_WIDTH), lambda c: (c, OFF_AU // A_WIDTH)),
                  pl.BlockSpec((rows, A_WIDTH), lambda c: (c, OFF_AV // A_WIDTH)),
                  pl.BlockSpec((A_HEADS, CHUNK, CHUNK), lambda c: (0, 0, 0)),
                  pl.BlockSpec((CHUNK, A_WIDTH), lambda c: (0, 0))],
        out_specs=pl.BlockSpec((rows, A_WIDTH), lambda c: (c, 0)),
        out_shape=SDS((S, A_WIDTH), BF16),
        compiler_params=_cp(1),
    )(p, p, wt, bb)


def _sgu_bwd(name, p, dycat, wt, wtt, bb):
    S = p.shape[0]
    rows = min(SGU_STEP_ROWS, S)

    def body(u_ref, v_ref, dy_ref, wt_ref, wtt_ref, bb_ref, du_ref, dv_ref, dw_ref, db_ref, dbacc_ref):
        c = pl.program_id(0)

        @pl.when(c == 0)
        def _():
            dw_ref[...] = jnp.zeros_like(dw_ref)
            dbacc_ref[...] = jnp.zeros_like(dbacc_ref)

        lane = lax.broadcasted_iota(jnp.int32, (CHUNK, 128), 1)
        row = lax.broadcasted_iota(jnp.int32, (CHUNK, 128), 0)
        causal = row >= lane
        nt = (((1,), (1,)), ((), ()))
        for pp in range(A_HEADS // 2):
            cs = slice(128 * pp, 128 * (pp + 1))
            dw_lo = jnp.zeros((CHUNK, CHUNK), F32)
            dw_hi = jnp.zeros((CHUNK, CHUNK), F32)
            dm_sum = jnp.zeros((CHUNK, 128), F32)
            for ci in range(rows // CHUNK):
                rs = slice(CHUNK * ci, CHUNK * (ci + 1))
                vb = v_ref[rs, cs].astype(BF16)
                dy = dy_ref[rs, cs]
                mixed = _pair_select(lane,
                                     jnp.dot(wt_ref[2 * pp], vb, preferred_element_type=F32),
                                     jnp.dot(wt_ref[2 * pp + 1], vb, preferred_element_type=F32)) + bb_ref[:, cs]
                du_ref[rs, cs] = (dy * mixed).astype(du_ref.dtype)
                dm = dy * u_ref[rs, cs]
                dmb = dm.astype(BF16)
                dv = _pair_select(lane,
                                  jnp.dot(wtt_ref[2 * pp], dmb, preferred_element_type=F32),
                                  jnp.dot(wtt_ref[2 * pp + 1], dmb, preferred_element_type=F32))
                dv_ref[rs, cs] = dv.astype(dv_ref.dtype)
                dm_sum += dm
                dm_lo = jnp.where(lane < HEAD_DIM, dm, 0.0).astype(BF16)
                dm_hi = jnp.where(lane >= HEAD_DIM, dm, 0.0).astype(BF16)
                dw_lo += lax.dot_general(dm_lo, vb, nt, preferred_element_type=F32)
                dw_hi += lax.dot_general(dm_hi, vb, nt, preferred_element_type=F32)
            dbacc_ref[:, cs] += dm_sum
            dw_ref[2 * pp] += jnp.where(causal, dw_lo, 0.0)
            dw_ref[2 * pp + 1] += jnp.where(causal, dw_hi, 0.0)

        @pl.when(c == S // rows - 1)
        def _():
            out = jnp.zeros((CHUNK, 128), F32)
            for pp in range(A_HEADS // 2):
                acc = dbacc_ref[:, 128 * pp:128 * (pp + 1)]
                s_lo = jnp.sum(jnp.where(lane < HEAD_DIM, acc, 0.0), axis=1, keepdims=True)
                s_hi = jnp.sum(jnp.where(lane >= HEAD_DIM, acc, 0.0), axis=1, keepdims=True)
                out = jnp.where(lane == 2 * pp, s_lo, out)
                out = jnp.where(lane == 2 * pp + 1, s_hi, out)
            db_ref[...] = out

    chunk = lambda col: pl.BlockSpec((rows, A_WIDTH), lambda c: (c, col))
    wspec = pl.BlockSpec((A_HEADS, CHUNK, CHUNK), lambda c: (0, 0, 0))
    return pl.pallas_call(
        body, name=name, grid=(S // rows,),
        in_specs=[chunk(OFF_AU // A_WIDTH), chunk(OFF_AV // A_WIDTH), chunk(0), wspec, wspec,
                  pl.BlockSpec((CHUNK, A_WIDTH), lambda c: (0, 0))],
        out_specs=[chunk(0), chunk(0), wspec, pl.BlockSpec((CHUNK, 128), lambda c: (0, 0))],
        out_shape=[SDS((S, A_WIDTH), BF16), SDS((S, A_WIDTH), BF16),
                   SDS((A_HEADS, CHUNK, CHUNK), F32), SDS((CHUNK, 128), F32)],
        scratch_shapes=[pltpu.VMEM((CHUNK, A_WIDTH), F32)],
        compiler_params=_cp(1),
    )(p, p, dycat, wt, wtt, bb)


CONV_HALO = 8
CONV_COLS = 256
CONV_ROWS = 1024


def _shift_down(a, halo, k):
    T = a.shape[0]
    row = lax.broadcasted_iota(jnp.int32, a.shape, 0)
    out = pltpu.roll(a, k, 0)
    for r in range(k):
        out = jnp.where(row == r, halo[CONV_HALO - k + r:CONV_HALO - k + r + 1, :], out)
    return out


def _shift_up(a, halo, k):
    T = a.shape[0]
    row = lax.broadcasted_iota(jnp.int32, a.shape, 0)
    out = pltpu.roll(a, T - k, 0)
    for r in range(k):
        out = jnp.where(row == T - k + r, halo[r:r + 1, :], out)
    return out


def _conv_specs(S, T):
    hb = T // CONV_HALO
    last = S // CONV_HALO - 1
    tile = lambda col0: pl.BlockSpec((T, CONV_COLS), lambda j, i: (i, col0 + j))
    prev = lambda col0: pl.BlockSpec((CONV_HALO, CONV_COLS), lambda j, i: (jnp.maximum(i * hb - 1, 0), col0 + j))
    nxt = lambda col0: pl.BlockSpec((CONV_HALO, CONV_COLS), lambda j, i: (jnp.minimum((i + 1) * hb, last), col0 + j))
    return tile, prev, nxt


def _conv_fwd(name, p, w):
    S = p.shape[0]
    T = min(CONV_ROWS, S)
    tile, prev, _ = _conv_specs(S, T)
    cb, cc, cx = OFF_BB // CONV_COLS, OFF_BC // CONV_COLS, OFF_BX // CONV_COLS

    def body(b_ref, c_ref, x_ref, ch_ref, xh_ref, w_ref, o_ref):
        i = pl.program_id(1)
        z = c_ref[...] * x_ref[...]
        zh = jnp.where(i > 0, ch_ref[...] * xh_ref[...], 0.0)
        z1 = _shift_down(z, zh, 1)
        z2 = _shift_down(z, zh, 2)
        conv = w_ref[0:1, :] * z2 + w_ref[1:2, :] * z1 + w_ref[2:3, :] * z
        o_ref[...] = (b_ref[...] * conv).astype(o_ref.dtype)

    return pl.pallas_call(
        body, name=name, grid=(B_WIDTH // CONV_COLS, S // T),
        in_specs=[tile(cb), tile(cc), tile(cx), prev(cc), prev(cx),
                  pl.BlockSpec((3, CONV_COLS), lambda j, i: (0, j))],
        out_specs=tile(0),
        out_shape=SDS((S, B_WIDTH), BF16),
        compiler_params=_cp(2),
    )(p, p, p, p, p, w)


def _conv_bwd(name, p, dycat, w):
    S = p.shape[0]
    T = min(CONV_ROWS, S)
    tile, prev, nxt = _conv_specs(S, T)
    cb, cc, cx = OFF_BB // CONV_COLS, OFF_BC // CONV_COLS, OFF_BX // CONV_COLS
    cdy = A_WIDTH // CONV_COLS
    n_i = S // T

    def body(b_ref, c_ref, x_ref, dy_ref, ch_ref, xh_ref, bn_ref, dyn_ref, w_ref,
             db_ref, dc_ref, dx_ref, dw_ref):
        i = pl.program_id(1)

        @pl.when(i == 0)
        def _():
            dw_ref[...] = jnp.zeros_like(dw_ref)

        cv = c_ref[...]
        xv = x_ref[...]
        z = cv * xv
        zh = jnp.where(i > 0, ch_ref[...] * xh_ref[...], 0.0)
        z1 = _shift_down(z, zh, 1)
        z2 = _shift_down(z, zh, 2)
        w0, w1, w2 = w_ref[0:1, :], w_ref[1:2, :], w_ref[2:3, :]
        conv = w0 * z2 + w1 * z1 + w2 * z
        dy = dy_ref[...]
        db_ref[...] = (dy * conv).astype(db_ref.dtype)
        dconv = dy * b_ref[...]
        dconv_n = jnp.where(i < n_i - 1, dyn_ref[...] * bn_ref[...], 0.0)
        dz = w2 * dconv + w1 * _shift_up(dconv, dconv_n, 1) + w0 * _shift_up(dconv, dconv_n, 2)
        dc_ref[...] = (dz * xv).astype(dc_ref.dtype)
        dx_ref[...] = (dz * cv).astype(dx_ref.dtype)
        dw_ref[0:1, :] += jnp.sum(dconv * z2, axis=0, keepdims=True)
        dw_ref[1:2, :] += jnp.sum(dconv * z1, axis=0, keepdims=True)
        dw_ref[2:3, :] += jnp.sum(dconv * z, axis=0, keepdims=True)

    wspec = pl.BlockSpec((3, CONV_COLS), lambda j, i: (0, j))
    return pl.pallas_call(
        body, name=name, grid=(B_WIDTH // CONV_COLS, n_i),
        in_specs=[tile(cb), tile(cc), tile(cx), tile(cdy), prev(cc), prev(cx), nxt(cb), nxt(cdy), wspec],
        out_specs=[tile(0), tile(0), tile(0), wspec],
        out_shape=[SDS((S, B_WIDTH), BF16)] * 3 + [SDS((3, B_WIDTH), F32)],
        compiler_params=_cp(2),
    )(p, p, p, dycat, p, p, p, dycat, w)


def _seg_sum(t, bd):
    hi = t.astype(BF16)
    lo = (t - hi.astype(F32)).astype(BF16)
    return jnp.dot(hi, bd, preferred_element_type=F32) + jnp.dot(lo, bd, preferred_element_type=F32)


def _head_norm(x, g, bd):
    rstd = lax.rsqrt(_seg_sum(x * x, bd) * (1.0 / HEAD_DIM) + EPS)
    xhat = x * rstd
    return xhat * g, xhat, rstd


def _head_norm_bwd(dy, g, xhat, rstd, bd):
    dxh = dy * g
    return rstd * (dxh - xhat * (_seg_sum(dxh * xhat, bd) * (1.0 / HEAD_DIM)))


def _band_mask(has_prev):
    row = lax.broadcasted_iota(jnp.int32, (BLK, 2 * BLK), 0)
    col = lax.broadcasted_iota(jnp.int32, (BLK, 2 * BLK), 1)
    first_key = jnp.where(has_prev, 0, BLK)
    return (col >= row) & (col <= row + BLK) & (col >= first_key)


def _residue_rows(r, d):
    return slice(None) if d == 1 else pl.ds(r, BLK, stride=d)


STRIDED_LANES = 128


def _step_width(d):
    return PW if d == 1 else STRIDED_LANES


def _n_stack(lane):
    return lane.shape[1] // HEAD_DIM


def _for_residues(d, fn):
    if d == 1:
        fn(0)
    else:
        def two(i, carry):
            fn(2 * i)
            fn(2 * i + 1)
            return carry
        lax.fori_loop(0, d // 2, two, 0)


def _head_mask(lane, j):
    return (lane >= HEAD_DIM * j) & (lane < HEAD_DIM * (j + 1))


def _stack_heads(x, lane):
    return jnp.concatenate([jnp.where(_head_mask(lane, j), x, 0.0) for j in range(_n_stack(lane))], axis=0)


def _unstack_heads(y, lane):
    out = y[:BLK]
    for j in range(1, _n_stack(lane)):
        out = jnp.where(lane >= HEAD_DIM * j, y[BLK * j:BLK * (j + 1)], out)
    return out


def _head_columns(v, lane):
    return jnp.concatenate([jnp.max(jnp.where(_head_mask(lane, j), v, NEG), axis=1, keepdims=True)
                            for j in range(_n_stack(lane))], axis=0)


def _attn_fwd(name, p, g, gq, gk, bd):
    S = p.shape[0]
    d = PATTERN_DILATION[g]
    rows = BLK * d
    hw = _step_width(d)
    nt = (((1,), (1,)), ((), ()))

    def body(q_ref, kc_ref, kp_ref, vc_ref, vp_ref, gq_ref, gk_ref, bd_ref, o_ref, lse_ref):
        has_prev = pl.program_id(1) > 0
        bdv = bd_ref[...]
        band = jnp.concatenate([_band_mask(has_prev)] * (hw // HEAD_DIM), axis=0)
        lane = lax.broadcasted_iota(jnp.int32, (1, hw), 1)

        def residue(r):
            rr = _residue_rows(r, d)
            qn, _, _ = _head_norm(q_ref[rr, :], gq_ref[...], bdv)
            kn, _, _ = _head_norm(jnp.concatenate([kp_ref[rr, :], kc_ref[rr, :]], axis=0), gk_ref[...], bdv)
            knb = kn.astype(BF16)
            vb = jnp.concatenate([vp_ref[rr, :], vc_ref[rr, :]], axis=0).astype(BF16)
            qs = _stack_heads(qn, lane).astype(BF16)
            s = lax.dot_general(qs, knb, nt, preferred_element_type=F32) * (HEAD_DIM ** -0.5)
            s = jnp.where(band, s, NEG)
            m = jnp.max(s, axis=1, keepdims=True)
            e = jnp.exp(s - m)
            den = jnp.sum(e, axis=1, keepdims=True)
            pv = jnp.dot(e.astype(BF16), vb, preferred_element_type=F32)
            o_ref[rr, :] = _unstack_heads(pv / den, lane)
            lse_ref[rr, :] = _unstack_heads(jnp.broadcast_to(m + jnp.log(den), pv.shape), lane)

        _for_residues(d, residue)

    per = PW // hw
    cq, ck, cv = (OFF_Q + PW * g) // hw, (OFF_K + PW * g) // hw, (OFF_V + PW * g) // hw
    cur = lambda col: pl.BlockSpec((rows, hw), lambda h, n: (n, col + h))
    prv = lambda col: pl.BlockSpec((rows, hw), lambda h, n: (jnp.maximum(n - 1, 0), col + h))
    vec = pl.BlockSpec((1, hw), lambda h, n: (0, h))
    return pl.pallas_call(
        body, name=name, grid=(per, S // rows),
        in_specs=[cur(cq), cur(ck), prv(ck), cur(cv), prv(cv), vec, vec, pl.BlockSpec((hw, hw), lambda h, n: (0, 0))],
        out_specs=[cur(0), cur(0)],
        out_shape=[SDS((S, PW), F32)] * 2,
        compiler_params=_cp(2),
    )(p, p, p, p, p, gq, gk, bd)


def _attn_bwd(name, p, g, lse, do3, c3, gq, gk, bd):
    S = p.shape[0]
    d = PATTERN_DILATION[g]
    rows = BLK * d
    nblk = S // rows
    hw = _step_width(d)
    nt = (((1,), (1,)), ((), ()))
    tn = (((0,), (0,)), ((), ()))

    def body(q_ref, kc_ref, kp_ref, vc_ref, vp_ref, lse_ref, do_ref, c_ref, gq_ref, gk_ref, bd_ref,
             dq_ref, dk_ref, dv_ref, dgq_ref, dgk_ref, ck_ref, cv_ref, dq_keep_ref):
        n = pl.program_id(1)

        @pl.when(n == 0)
        def _():
            ck_ref[...] = jnp.zeros_like(ck_ref)
            cv_ref[...] = jnp.zeros_like(cv_ref)
            dgq_ref[...] = jnp.zeros_like(dgq_ref)
            dgk_ref[...] = jnp.zeros_like(dgk_ref)

        @pl.when(n == nblk)
        def _():
            dq_ref[...] = dq_keep_ref[...]
            dk_ref[...] = ck_ref[...]
            dv_ref[...] = cv_ref[...]

        bdv = bd_ref[...]
        gqv = gq_ref[...]
        gkv = gk_ref[...]
        band = jnp.concatenate([_band_mask(n > 0)] * (hw // HEAD_DIM), axis=0)
        lane = lax.broadcasted_iota(jnp.int32, (1, hw), 1)

        def residue(r):
            rr = _residue_rows(r, d)
            qn, qhat, qrstd = _head_norm(q_ref[rr, :], gqv, bdv)
            kn, khat, krstd = _head_norm(jnp.concatenate([kp_ref[rr, :], kc_ref[rr, :]], axis=0), gkv, bdv)
            knb = kn.astype(BF16)
            vb = jnp.concatenate([vp_ref[rr, :], vc_ref[rr, :]], axis=0).astype(BF16)
            qs = _stack_heads(qn, lane).astype(BF16)
            dos = _stack_heads(do_ref[rr, :], lane).astype(BF16)
            s = lax.dot_general(qs, knb, nt, preferred_element_type=F32) * (HEAD_DIM ** -0.5)
            prob = jnp.where(band, jnp.exp(s - _head_columns(lse_ref[rr, :], lane)), 0.0)
            dp = lax.dot_general(dos, vb, nt, preferred_element_type=F32)
            ds = (prob * (dp + _head_columns(c_ref[rr, :], lane)) * (HEAD_DIM ** -0.5)).astype(BF16)
            dqn = _unstack_heads(jnp.dot(ds, knb, preferred_element_type=F32), lane)
            dkn = lax.dot_general(ds, qs, tn, preferred_element_type=F32)
            dvv = lax.dot_general(prob.astype(BF16), dos, tn, preferred_element_type=F32)

            dq = _head_norm_bwd(dqn, gqv, qhat, qrstd, bdv)
            dq_ref[rr, :] = dq
            dq_keep_ref[rr, :] = dq
            dk2 = _head_norm_bwd(dkn, gkv, khat, krstd, bdv)
            dgq_ref[...] += jnp.sum(dqn * qhat, axis=0, keepdims=True)
            dgk_ref[...] += jnp.sum(dkn * khat, axis=0, keepdims=True)
            dk_ref[rr, :] = ck_ref[rr, :] + dk2[:BLK]
            dv_ref[rr, :] = cv_ref[rr, :] + dvv[:BLK]
            ck_ref[rr, :] = dk2[BLK:]
            cv_ref[rr, :] = dvv[BLK:]

        @pl.when(n < nblk)
        def _():
            _for_residues(d, residue)

    last = nblk - 1
    per = PW // hw
    cq, ck, cv = (OFF_Q + PW * g) // hw, (OFF_K + PW * g) // hw, (OFF_V + PW * g) // hw
    cur = lambda col: pl.BlockSpec((rows, hw), lambda h, n: (jnp.minimum(n, last), col + h))
    prv = lambda col: pl.BlockSpec((rows, hw), lambda h, n: (jnp.maximum(jnp.minimum(n, last) - 1, 0), col + h))
    cur3 = pl.BlockSpec((None, rows, hw), lambda h, n: (g, jnp.minimum(n, last), h))
    done = pl.BlockSpec((rows, hw), lambda h, n: (jnp.maximum(n - 1, 0), h))
    vec = pl.BlockSpec((1, hw), lambda h, n: (0, h))
    return pl.pallas_call(
        body, name=name, grid=(per, nblk + 1),
        in_specs=[cur(cq), cur(ck), prv(ck), cur(cv), prv(cv), cur(0), cur3, cur3, vec, vec,
                  pl.BlockSpec((hw, hw), lambda h, n: (0, 0))],
        out_specs=[cur(0), done, done, vec, vec],
        out_shape=[SDS((S, PW), F32)] * 3 + [SDS((1, PW), F32)] * 2,
        scratch_shapes=[pltpu.VMEM((rows, hw), F32)] * 3,
        compiler_params=_cp(2),
    )(p, p, p, p, p, lse, do3, c3, gq, gk, bd)


def _mix_fwd(name, os, lses):
    S = os[0].shape[0]
    tm = min(512, S)

    def body(o0, o1, o2, l0, l1, l2, y_ref):
        o = [o0[...], o1[...], o2[...]]
        l = [l0[...], l1[...], l2[...]]
        m = jnp.maximum(jnp.maximum(l[0], l[1]), l[2])
        e = [jnp.exp(t - m) for t in l]
        inv = 1.0 / (e[0] + e[1] + e[2])
        for g in range(N_PATTERNS):
            y_ref[:, PW * g:PW * (g + 1)] = (o[g] * (e[g] * inv)).astype(y_ref.dtype)

    blk = pl.BlockSpec((tm, PW), lambda i: (i, 0))
    return pl.pallas_call(
        body, name=name, grid=(S // tm,),
        in_specs=[blk] * 6,
        out_specs=pl.BlockSpec((tm, C_WIDTH), lambda i: (i, 0)),
        out_shape=SDS((S, C_WIDTH), BF16),
        compiler_params=_cp(1),
    )(*os, *lses)


def _mix_bwd(name, os, lses, dycat, bd):
    S = os[0].shape[0]
    tm = min(512, S)
    c0 = (A_WIDTH + B_WIDTH) // PW

    def body(o0, o1, o2, l0, l1, l2, dy0_ref, dy1_ref, dy2_ref, bd_ref, do_ref, c_ref):
        bdv = bd_ref[...]
        o = [o0[...], o1[...], o2[...]]
        l = [l0[...], l1[...], l2[...]]
        dys = [dy0_ref[...], dy1_ref[...], dy2_ref[...]]
        m = jnp.maximum(jnp.maximum(l[0], l[1]), l[2])
        e = [jnp.exp(t - m) for t in l]
        inv = 1.0 / (e[0] + e[1] + e[2])
        alpha = [t * inv for t in e]
        da = [_seg_sum(dys[g] * o[g], bdv) for g in range(N_PATTERNS)]
        mean_da = alpha[0] * da[0] + alpha[1] * da[1] + alpha[2] * da[2]
        for g in range(N_PATTERNS):
            do_ref[g] = dys[g] * alpha[g]
            c_ref[g] = -alpha[g] * mean_da

    blk = pl.BlockSpec((tm, PW), lambda i: (i, 0))
    blk3 = pl.BlockSpec((N_PATTERNS, tm, PW), lambda i: (0, i, 0))
    dyspec = lambda g: pl.BlockSpec((tm, PW), lambda i: (i, c0 + g))
    return pl.pallas_call(
        body, name=name, grid=(S // tm,),
        in_specs=[blk] * 6 + [dyspec(0), dyspec(1), dyspec(2), pl.BlockSpec((PW, PW), lambda i: (0, 0))],
        out_specs=[blk3, blk3],
        out_shape=[SDS((N_PATTERNS, S, PW), F32)] * 2,
        compiler_params=_cp(1),
    )(*os, *lses, dycat, dycat, dycat, bd)


def _mesh_pos():
    x, y, c = lax.axis_index("x"), lax.axis_index("y"), lax.axis_index("c")
    chips = [(1 - x, y), (x, 1 - y), (1 - x, 1 - y)]
    chip_idx = [2 * cx + cy for cx, cy in chips]
    return x, y, c, 2 * x + y, chips, chip_idx


def _place_shard(name, w, layer, chip_arr, out_dtype, deps=()):
    _, R, C = w.shape
    tr = min(256, R)

    def body(chip_ref, w_ref, *rest):
        o_ref = rest[-1]
        o_ref[...] = w_ref[...].astype(o_ref.dtype)

    return pl.pallas_call(
        body, name=name,
        grid_spec=pltpu.PrefetchScalarGridSpec(
            num_scalar_prefetch=1, grid=(R // tr,),
            in_specs=[pl.BlockSpec((None, tr, C), lambda i, chip_ref: (layer, i, 0))] + [_hbm_spec()] * len(deps),
            out_specs=pl.BlockSpec((None, tr, C), lambda i, chip_ref: (chip_ref[0], i, 0))),
        out_shape=SDS((N_CHIPS, R, C), out_dtype),
        compiler_params=_cp(1),
    )(chip_arr, w, *deps)


HBM_SPEC = pl.BlockSpec(memory_space=pltpu.HBM)
SEM_SPEC = pl.BlockSpec(memory_space=pltpu.SEMAPHORE)
SPLIT_COPY = pltpu.SideEffectType.DATAFLOW_SIDE_EFFECTING
N_PEER_CHIPS = N_CHIPS - 1
TOKEN_SHAPE = SDS((8, 128), F32)
TOKEN_SPEC = pl.BlockSpec(memory_space=pltpu.VMEM)


def _in_hbm(a):
    return pltpu.with_memory_space_constraint(a, pltpu.HBM)


def _gather_start(name, bufs):
    T = len(bufs)

    def body(*refs):
        ins = refs[:T]
        send_sems, recv_sems = refs[T:2 * T], refs[2 * T:3 * T]
        token = refs[4 * T]
        x, y, c, me, chips, chip_idx = _mesh_pos()
        for t in range(T):
            hr = ins[t].shape[1] // 2
            mine = ins[t].at[me, pl.ds(c * hr, hr), :]
            for j in range(N_PEER_CHIPS):
                pltpu.make_async_remote_copy(src_ref=mine, dst_ref=mine, send_sem=send_sems[t].at[j],
                                             recv_sem=recv_sems[t].at[j], device_id=(*chips[j], c),
                                             device_id_type=MESH).start()
        token[...] = jnp.zeros_like(token)

    sems = [pltpu.SemaphoreType.DMA((N_PEER_CHIPS,))] * T
    out = pl.pallas_call(
        body, name=name,
        in_specs=[HBM_SPEC] * T,
        out_specs=[SEM_SPEC] * (2 * T) + [HBM_SPEC] * T + [TOKEN_SPEC],
        out_shape=sems + sems + [pltpu.HBM(b.shape, b.dtype) for b in bufs] + [TOKEN_SHAPE],
        input_output_aliases={t: 2 * T + t for t in range(T)},
        compiler_params=pltpu.CompilerParams(has_side_effects=SPLIT_COPY),
    )(*[_in_hbm(b) for b in bufs])
    return out[:T], out[T:2 * T], out[2 * T:3 * T], out[3 * T]


def _gather_wait(name, buf, send_sem, recv_sem, after):
    n_in = 3 if after is None else 4

    def body(*refs):
        buf_ref, ssem, rsem = refs[:3]
        x, y, c, me, chips, chip_idx = _mesh_pos()
        hr = buf_ref.shape[1] // 2
        mine = buf_ref.at[me, pl.ds(c * hr, hr), :]
        for j in range(N_PEER_CHIPS):
            got = buf_ref.at[chip_idx[j], pl.ds(c * hr, hr), :]
            cp = pltpu.make_async_remote_copy(src_ref=mine, dst_ref=got, send_sem=ssem.at[j], recv_sem=rsem.at[j],
                                              device_id=(*chips[j], c), device_id_type=MESH)
            cp.wait_send()
            cp.wait_recv()

    args = [buf, send_sem, recv_sem] + ([] if after is None else [after])
    return pl.pallas_call(
        body, name=name,
        in_specs=[HBM_SPEC, SEM_SPEC, SEM_SPEC] + [_hbm_spec()] * (n_in - 3),
        out_specs=HBM_SPEC,
        out_shape=pltpu.HBM(buf.shape, buf.dtype),
        input_output_aliases={0: 0},
        compiler_params=pltpu.CompilerParams(has_side_effects=SPLIT_COPY),
    )(*args)


def _forward_start(name, buf):
    def body(buf_ref, send_sems, recv_sems, buf_thru, token):
        x, y, c, me, chips, chip_idx = _mesh_pos()
        hr = buf_ref.shape[1] // 2
        for j in range(N_PEER_CHIPS):
            got = buf_ref.at[chip_idx[j], pl.ds(c * hr, hr), :]
            pltpu.make_async_remote_copy(src_ref=got, dst_ref=got, send_sem=send_sems.at[j], recv_sem=recv_sems.at[j],
                                         device_id=(x, y, 1 - c), device_id_type=MESH).start()
        token[...] = jnp.zeros_like(token)

    sems = pltpu.SemaphoreType.DMA((N_PEER_CHIPS,))
    return pl.pallas_call(
        body, name=name,
        in_specs=[HBM_SPEC],
        out_specs=[SEM_SPEC, SEM_SPEC, HBM_SPEC, TOKEN_SPEC],
        out_shape=[sems, sems, pltpu.HBM(buf.shape, buf.dtype), TOKEN_SHAPE],
        input_output_aliases={0: 2},
        compiler_params=pltpu.CompilerParams(has_side_effects=SPLIT_COPY),
    )(_in_hbm(buf))


def _forward_wait(name, buf, send_sems, recv_sems, after):
    n_in = 3 if after is None else 4

    def body(*refs):
        buf_ref, ssems, rsems = refs[:3]
        x, y, c, me, chips, chip_idx = _mesh_pos()
        hr = buf_ref.shape[1] // 2
        for j in range(N_PEER_CHIPS):
            sent = buf_ref.at[chip_idx[j], pl.ds(c * hr, hr), :]
            theirs = buf_ref.at[chip_idx[j], pl.ds((1 - c) * hr, hr), :]
            cp = pltpu.make_async_remote_copy(src_ref=sent, dst_ref=theirs, send_sem=ssems.at[j],
                                              recv_sem=rsems.at[j], device_id=(x, y, 1 - c), device_id_type=MESH)
            cp.wait_send()
            cp.wait_recv()

    args = [buf, send_sems, recv_sems] + ([] if after is None else [after])
    return pl.pallas_call(
        body, name=name,
        in_specs=[HBM_SPEC, SEM_SPEC, SEM_SPEC] + [_hbm_spec()] * (n_in - 3),
        out_specs=HBM_SPEC,
        out_shape=pltpu.HBM(buf.shape, buf.dtype),
        input_output_aliases={0: 0},
        compiler_params=pltpu.CompilerParams(has_side_effects=SPLIT_COPY),
    )(*args)


class _GatheredWeights:
    def __init__(self):
        self._order = []
        self._pending = {}
        self._forwarding = {}
        self._ready = {}
        self._tokens = []

    def start(self, keys, bufs):
        send_sems, recv_sems, thru, token = _gather_start(f"gather_start_{len(self._order)}", bufs)
        self._tokens.append(token)
        self._order.extend(keys)
        self._pending.update({k: (b, s, r) for k, b, s, r in zip(keys, thru, send_sems, recv_sems)})

    def _prefetch(self, key, after):
        if key in self._pending:
            buf, ssem, rsem = self._pending.pop(key)
            tag = f"{key[0]}_{key[1]}"
            buf = _gather_wait(f"gather_wait_{tag}", buf, ssem, rsem, after)
            ssems, rsems, buf, token = _forward_start(f"gather_fwd_start_{tag}", buf)
            self._forwarding[key] = (buf, ssems, rsems)
            self._tokens.append(token)

    def get(self, name, layer, after=None, prefetch_next=True):
        key = (name, layer)
        if key not in self._ready:
            self._prefetch(key, after)
            buf, ssems, rsems = self._forwarding.pop(key)
            self._ready[key] = _forward_wait(f"gather_fwd_wait_{name}_{layer}", buf, ssems, rsems, after)
            if prefetch_next:
                self.prefetch_after(name, layer, after)
        return self._ready[key]

    def prefetch_after(self, name, layer, after):
        nxt = self._order.index((name, layer)) + 1
        if nxt < len(self._order):
            self._prefetch(self._order[nxt], after)

    def deps(self):
        tokens, self._tokens = self._tokens, []
        return tokens


def _swap_copy(g_ref, land_ref, send_sem, recv_sem):
    x, y, c, _, _, _ = _mesh_pos()
    hr = g_ref.shape[1] // 2
    return pltpu.make_async_remote_copy(src_ref=g_ref.at[:, pl.ds((1 - c) * hr, hr), :], dst_ref=land_ref,
                                        send_sem=send_sem, recv_sem=recv_sem, device_id=(x, y, 1 - c),
                                        device_id_type=MESH)


def _swap_start(name, g):
    land_shape = (g.shape[0], g.shape[1] // 2, g.shape[2])

    def body(g_ref, land_ref, send_sem, recv_sem, land_thru, token):
        _swap_copy(g_ref, land_ref, send_sem, recv_sem).start()
        token[...] = jnp.zeros_like(token)

    return pl.pallas_call(
        body, name=name,
        in_specs=[HBM_SPEC, HBM_SPEC],
        out_specs=[SEM_SPEC, SEM_SPEC, HBM_SPEC, TOKEN_SPEC],
        out_shape=[pltpu.SemaphoreType.DMA(()), pltpu.SemaphoreType.DMA(()), pltpu.HBM(land_shape, g.dtype),
                   TOKEN_SHAPE],
        input_output_aliases={1: 2},
        compiler_params=pltpu.CompilerParams(has_side_effects=SPLIT_COPY),
    )(_in_hbm(g), _in_hbm(lax.empty(land_shape, g.dtype)))


def _swap_wait(name, g, land, send_sem, recv_sem, after):
    def body(g_ref, land_ref, send_sem, recv_sem, after_ref, land_out):
        cp = _swap_copy(g_ref, land_ref, send_sem, recv_sem)
        cp.wait_send()
        cp.wait_recv()

    return pl.pallas_call(
        body, name=name,
        in_specs=[HBM_SPEC, HBM_SPEC, SEM_SPEC, SEM_SPEC, _hbm_spec()],
        out_specs=HBM_SPEC,
        out_shape=pltpu.HBM(land.shape, land.dtype),
        input_output_aliases={1: 0},
        compiler_params=pltpu.CompilerParams(has_side_effects=SPLIT_COPY),
    )(_in_hbm(g), land, send_sem, recv_sem, after)


def _add_my_half(name, g, r, pos_arr):
    ns, R, C = g.shape
    hr = R // 2
    tr = min(256, hr)
    nt = hr // tr

    def body(pos_ref, g_ref, r_ref, o_ref, land_ref):
        t = (g_ref[...] + r_ref[...]).astype(o_ref.dtype)
        o_ref[...] = t

        @pl.when(pl.program_id(1) == pos_ref[1])
        def _():
            land_ref[...] = t

    blk = pl.BlockSpec((None, tr, C), lambda i, s, pos_ref: (s, i, 0))
    return pl.pallas_call(
        body, name=name,
        grid_spec=pltpu.PrefetchScalarGridSpec(
            num_scalar_prefetch=1, grid=(nt, ns),
            in_specs=[pl.BlockSpec((None, tr, C), lambda i, s, pos_ref: (s, pos_ref[0] * nt + i, 0)), blk],
            out_specs=[blk, pl.BlockSpec((None, tr, C), lambda i, s, pos_ref: (pos_ref[1], i, 0))]),
        out_shape=[SDS((ns, hr, C), BF16)] * 2,
        compiler_params=_cp(2),
    )(pos_arr, g, r)


def _exchange_start(name, part, land):
    def body(part_ref, land_ref, send_sems, recv_sems, land_thru, token):
        x, y, c, me, chips, chip_idx = _mesh_pos()
        for j in range(N_PEER_CHIPS):
            pltpu.make_async_remote_copy(src_ref=part_ref.at[chip_idx[j]], dst_ref=land_ref.at[me],
                                         send_sem=send_sems.at[j], recv_sem=recv_sems.at[j],
                                         device_id=(*chips[j], c), device_id_type=MESH).start()
        token[...] = jnp.zeros_like(token)

    sems = pltpu.SemaphoreType.DMA((N_PEER_CHIPS,))
    return pl.pallas_call(
        body, name=name,
        in_specs=[HBM_SPEC, HBM_SPEC],
        out_specs=[SEM_SPEC, SEM_SPEC, HBM_SPEC, TOKEN_SPEC],
        out_shape=[sems, sems, pltpu.HBM(land.shape, land.dtype), TOKEN_SHAPE],
        input_output_aliases={1: 2},
        compiler_params=pltpu.CompilerParams(has_side_effects=SPLIT_COPY),
    )(_in_hbm(part), _in_hbm(land))


def _exchange_wait(name, part, land, send_sems, recv_sems, after):
    def body(part_ref, land_ref, send_sems, recv_sems, after_ref, land_out):
        x, y, c, me, chips, chip_idx = _mesh_pos()
        for j in range(N_PEER_CHIPS):
            cp = pltpu.make_async_remote_copy(src_ref=part_ref.at[chip_idx[j]], dst_ref=land_ref.at[chip_idx[j]],
                                              send_sem=send_sems.at[j], recv_sem=recv_sems.at[j],
                                              device_id=(*chips[j], c), device_id_type=MESH)
            cp.wait_send()
            cp.wait_recv()

    return pl.pallas_call(
        body, name=name,
        in_specs=[HBM_SPEC, HBM_SPEC, SEM_SPEC, SEM_SPEC, _hbm_spec()],
        out_specs=HBM_SPEC,
        out_shape=pltpu.HBM(land.shape, land.dtype),
        input_output_aliases={1: 0},
        compiler_params=pltpu.CompilerParams(has_side_effects=SPLIT_COPY),
    )(_in_hbm(part), land, send_sems, recv_sems, after)


class _GradReducer:
    def __init__(self, c_arr):
        self._c_arr = c_arr
        self._swapping = []
        self._exchanging = {}
        self._joining = {}
        self._tokens = []

    def begin(self, name, layer, g):
        tag = f"{name}_{layer}"
        ssem, rsem, land, token = _swap_start(f"rs_swap_start_{tag}", g)
        self._swapping.append((name, layer, g, ssem, rsem, land))
        self._tokens.append(token)

    def advance(self, after):
        for name, layer, g, ssem, rsem, land in self._swapping:
            tag = f"{name}_{layer}"
            theirs = _swap_wait(f"rs_swap_wait_{tag}", g, land, ssem, rsem, after)
            part, own = _add_my_half(f"rs_add_{tag}", g, theirs, self._c_arr)
            ssems, rsems, land2, token = _exchange_start(f"rs_xchg_start_{tag}", part, own)
            self._exchanging[(name, layer)] = (part, ssems, rsems, land2)
            self._tokens.append(token)
        self._swapping = []

    def deps(self):
        tokens, self._tokens = self._tokens, []
        return tokens

    def reduce(self, name, n_layers, after):
        buf = None
        for layer in range(n_layers):
            part, ssems, rsems, land = self._exchanging.pop((name, layer))
            tag = f"{name}_{layer}"
            landed = _exchange_wait(f"rs_xchg_wait_{tag}", part, land, ssems, rsems, after)
            buf = _sum_chips(f"rs_sum_{tag}", landed, self._c_arr, layer, n_layers, buf)
        ssem, rsem, buf, token = _join_start(f"rs_join_start_{name}", buf)
        self._joining[name] = (buf, ssem, rsem)
        return token

    def reduced(self, name, after):
        buf, ssem, rsem = self._joining.pop(name)
        return _join_wait(f"rs_join_wait_{name}", buf, ssem, rsem, after)


def _sum_chips(name, r, c_arr, layer, n_layers, prev):
    ns, H, C = r.shape
    tr = min(256, H)
    nt = H // tr

    def body(c_ref, r_ref, *rest):
        o_ref = rest[-1]
        o_ref[...] = ((r_ref[0].astype(F32) + r_ref[1].astype(F32)) + r_ref[2].astype(F32)) + r_ref[3].astype(F32)

    in_specs = [pl.BlockSpec((ns, tr, C), lambda i, c_ref: (0, i, 0))]
    args = [c_arr, r]
    aliases = {}
    if prev is not None:
        in_specs.append(_hbm_spec())
        args.append(prev)
        aliases = {2: 0}
    return pl.pallas_call(
        body, name=name,
        grid_spec=pltpu.PrefetchScalarGridSpec(
            num_scalar_prefetch=1, grid=(nt,), in_specs=in_specs,
            out_specs=pl.BlockSpec((None, tr, C), lambda i, c_ref: (layer, c_ref[0] * nt + i, 0))),
        out_shape=SDS((n_layers, 2 * H, C), F32),
        input_output_aliases=aliases,
        compiler_params=_cp(1),
    )(*args)


def _join_copy(buf_ref, send_sem, recv_sem):
    x, y, c, _, _, _ = _mesh_pos()
    hr = buf_ref.shape[1] // 2
    mine = buf_ref.at[:, pl.ds(c * hr, hr), :]
    theirs = buf_ref.at[:, pl.ds((1 - c) * hr, hr), :]
    send = pltpu.make_async_remote_copy(src_ref=mine, dst_ref=mine, send_sem=send_sem, recv_sem=recv_sem,
                                        device_id=(x, y, 1 - c), device_id_type=MESH)
    arrive = pltpu.make_async_remote_copy(src_ref=theirs, dst_ref=theirs, send_sem=send_sem, recv_sem=recv_sem,
                                          device_id=(x, y, 1 - c), device_id_type=MESH)
    return send, arrive


def _join_start(name, buf):
    def body(buf_ref, send_sem, recv_sem, buf_thru, token):
        _join_copy(buf_ref, send_sem, recv_sem)[0].start()
        token[...] = jnp.zeros_like(token)

    return pl.pallas_call(
        body, name=name,
        in_specs=[HBM_SPEC],
        out_specs=[SEM_SPEC, SEM_SPEC, HBM_SPEC, TOKEN_SPEC],
        out_shape=[pltpu.SemaphoreType.DMA(()), pltpu.SemaphoreType.DMA(()), pltpu.HBM(buf.shape, buf.dtype),
                   TOKEN_SHAPE],
        input_output_aliases={0: 2},
        compiler_params=pltpu.CompilerParams(has_side_effects=SPLIT_COPY),
    )(_in_hbm(buf))


def _join_wait(name, buf, send_sem, recv_sem, after):
    def body(buf_ref, send_sem, recv_sem, after_ref, buf_out):
        send, arrive = _join_copy(buf_ref, send_sem, recv_sem)
        send.wait_send()
        arrive.wait_recv()

    return pl.pallas_call(
        body, name=name,
        in_specs=[HBM_SPEC, SEM_SPEC, SEM_SPEC, _hbm_spec()],
        out_specs=HBM_SPEC,
        out_shape=pltpu.HBM(buf.shape, buf.dtype),
        input_output_aliases={0: 0},
        compiler_params=pltpu.CompilerParams(has_side_effects=SPLIT_COPY),
    )(buf, send_sem, recv_sem, after)


def _small_copy(k, buf_ref, land_ref, send_sems, recv_sems):
    x, y, c = lax.axis_index("x"), lax.axis_index("y"), lax.axis_index("c")
    me = 4 * x + 2 * y + c
    peer = (x ^ ((k >> 2) & 1), y ^ ((k >> 1) & 1), c ^ (k & 1))
    cp = pltpu.make_async_remote_copy(src_ref=buf_ref, dst_ref=land_ref.at[me], send_sem=send_sems.at[k - 1],
                                      recv_sem=recv_sems.at[k - 1], device_id=peer, device_id_type=MESH)
    return me, peer, cp


def _small_start(buf, deps):
    land = jnp.broadcast_to(buf[None], (N_DEV,) + buf.shape)
    n_dep = len(deps)

    def body(buf_ref, land_ref, *rest):
        send_sems, recv_sems, _, token = rest[n_dep:]
        for k in range(1, N_DEV):
            _small_copy(k, buf_ref, land_ref, send_sems, recv_sems)[2].start()
        token[...] = jnp.zeros_like(token)

    sems = pltpu.SemaphoreType.DMA((N_DEV - 1,))
    return pl.pallas_call(
        body, name="small_gather_start",
        in_specs=[HBM_SPEC, HBM_SPEC] + [_hbm_spec()] * n_dep,
        out_specs=[SEM_SPEC, SEM_SPEC, HBM_SPEC, TOKEN_SPEC],
        out_shape=[sems, sems, pltpu.HBM(land.shape, land.dtype), TOKEN_SHAPE],
        input_output_aliases={1: 2},
        compiler_params=pltpu.CompilerParams(has_side_effects=SPLIT_COPY),
    )(_in_hbm(buf), _in_hbm(land), *deps)


def _small_wait(buf, land, send_sems, recv_sems, after):
    def body(buf_ref, land_ref, send_sems, recv_sems, after_ref, land_out):
        for k in range(1, N_DEV):
            me, peer, cp = _small_copy(k, buf_ref, land_ref, send_sems, recv_sems)
            cp.wait_send()
            got = land_ref.at[me ^ k]
            pltpu.make_async_remote_copy(src_ref=got, dst_ref=got, send_sem=send_sems.at[k - 1],
                                         recv_sem=recv_sems.at[k - 1], device_id=peer,
                                         device_id_type=MESH).wait_recv()

    return pl.pallas_call(
        body, name="small_gather_wait",
        in_specs=[HBM_SPEC, HBM_SPEC, SEM_SPEC, SEM_SPEC, _hbm_spec()],
        out_specs=HBM_SPEC,
        out_shape=pltpu.HBM(land.shape, land.dtype),
        input_output_aliases={1: 0},
        compiler_params=pltpu.CompilerParams(has_side_effects=SPLIT_COPY),
    )(_in_hbm(buf), land, send_sems, recv_sems, after)


def _sum_devices(land):
    n, R, C = land.shape

    def body(land_ref, out_ref):
        acc = land_ref[0]
        for d in range(1, n):
            acc = acc + land_ref[d]
        out_ref[...] = acc

    return pl.pallas_call(
        body, name="small_sum",
        in_specs=[pl.BlockSpec(memory_space=pltpu.VMEM)],
        out_specs=pl.BlockSpec(memory_space=pltpu.VMEM),
        out_shape=SDS((R, C), land.dtype),
        compiler_params=pltpu.CompilerParams(vmem_limit_bytes=V7X_VMEM_LIMIT),
    )(land)


def _pack_rows(vectors):
    flat = jnp.concatenate([v.reshape(-1) for v in vectors])
    n = flat.shape[0]
    padded = -(-n // 1024) * 1024
    return jnp.pad(flat, (0, padded - n)).reshape(padded // 128, 128)


def _unpack_rows(buf, shapes):
    flat = buf.reshape(-1)
    out, off = [], 0
    for s in shapes:
        n = 1
        for dim in s:
            n *= dim
        out.append(flat[off:off + n].reshape(s))
        off += n
    return out


def _layer_forward(l, x, prm, wg):
    S, D = x.shape
    h = _rmsnorm_fwd(f"attn_norm_{l}", x, prm["attn_norm"][l])
    w_in = wg.get("w_in", l, h, prefetch_next=l > 0)
    ns_in = w_in.shape[-1]
    tmi = min(1024, S)
    p = _matmul(
        f"in_proj_{l}", h, w_in, (S, N_CHIPS * ns_in), F32, grid=(S // tmi, N_CHIPS, 1),
        a_spec=pl.BlockSpec((tmi, D), lambda i, j, k: (i, 0)),
        b_spec=pl.BlockSpec((None, D, ns_in), lambda i, j, k: (j, 0, 0)),
        o_spec=pl.BlockSpec((tmi, ns_in), lambda i, j, k: (i, j)),
        contract=(1, 0), acc_shape=(tmi, ns_in), deps=wg.deps())
    if l == 0:
        wg.prefetch_after("w_in", l, p)
    y_a = _sgu_fwd(f"sgu_fwd_{l}", p, prm["sgu_wt"][l], prm["sgu_bb"][l])
    y_b = _conv_fwd(f"conv_fwd_{l}", p, prm["conv_w"][l])
    os, lses = [], []
    for g in range(N_PATTERNS):
        o_g, lse_g = _attn_fwd(f"attn_fwd_{l}_{g}", p, g, prm["q_gain"][l], prm["k_gain"][l], prm["bd"])
        os.append(o_g)
        lses.append(lse_g)
    y_c = _mix_fwd(f"mix_fwd_{l}", os, lses)
    ycat = jnp.concatenate([y_a, y_b, y_c], axis=1)
    tmb, tnb = min(1024, S), min(1024, D)
    w_out = wg.get("w_out", l, ycat)
    kq = N_CHIPS * w_out.shape[1]
    tmo, tno = min(512, S), D
    x1 = _matmul(
        f"out_proj_{l}", ycat, w_out.reshape(kq, D), (S, D), F32, grid=(S // tmo, D // tno, 1),
        a_spec=pl.BlockSpec((tmo, kq), lambda i, j, k: (i, 0)),
        b_spec=pl.BlockSpec((kq, tno), lambda i, j, k: (0, j)),
        o_spec=pl.BlockSpec((tmo, tno), lambda i, j, k: (i, j)),
        contract=(1, 0), acc_shape=(tmo, tno),
        extras=(x,), extra_specs=(pl.BlockSpec((tmo, tno), lambda i, j, k: (i, j)),),
        epi=lambda r, res: r + res, deps=wg.deps())
    w_mlp_in = wg.get("w_mlp_in", l, x1)
    r, h2 = _norm_matmul(f"mlp_in_{l}", x1, prm["mlp_norm"][l], w_mlp_in, BF16, deps=wg.deps(), post=_relu)
    w_mlp_out = wg.get("w_mlp_out", l, r)
    dff4 = w_mlp_out.shape[1]
    tk = min(2048, dff4)
    kpc = dff4 // tk
    x2 = _matmul(
        f"mlp_out_{l}", r, w_mlp_out, (S, D), F32, grid=(S // tmb, D // tnb, N_CHIPS * kpc),
        a_spec=pl.BlockSpec((tmb, tk), lambda i, j, k: (i, k)),
        b_spec=pl.BlockSpec((None, tk, tnb), lambda i, j, k: (k // kpc, k % kpc, j)),
        o_spec=pl.BlockSpec((tmb, tnb), lambda i, j, k: (i, j)),
        contract=(1, 0), acc_shape=(tmb, tnb), a_pre=_square,
        extras=(x1,), extra_specs=(pl.BlockSpec((tmb, tnb), lambda i, j, k: (i, j)),),
        epi=lambda acc, res: acc + res, deps=wg.deps())
    saved = dict(x=x, p=p, h=h, os=os, lses=lses, ycat=ycat, x1=x1, r=r, h2=h2)
    return x2, saved


def _layer_backward(l, dx2, dx2b, sv, prm, wg, sink):
    S, D = dx2.shape
    w_in, w_out = wg.get("w_in", l), wg.get("w_out", l)
    w_mlp_in, w_mlp_out = wg.get("w_mlp_in", l), wg.get("w_mlp_out", l)
    dff4 = w_mlp_in.shape[-1]
    dff = N_CHIPS * dff4

    tmb, tnb = min(1024, S), min(1024, D)
    da = _matmul(
        f"mlp_out_bwd_{l}", dx2b, w_mlp_out, (S, dff), BF16, grid=(S // tmb, N_CHIPS, 1),
        a_spec=pl.BlockSpec((tmb, D), lambda i, j, k: (i, 0)),
        b_spec=pl.BlockSpec((None, dff4, D), lambda i, j, k: (j, 0, 0)),
        o_spec=pl.BlockSpec((tmb, dff4), lambda i, j, k: (i, j)),
        contract=(1, 1), acc_shape=(tmb, dff4),
        extras=(sv["r"],), extra_specs=(pl.BlockSpec((tmb, dff4), lambda i, j, k: (i, j)),),
        epi=lambda acc, r: acc * (2.0 * r.astype(F32)), deps=sink.deps())
    tmw = min(1024, dff4)
    mpc = dff4 // tmw
    g_w2 = _matmul(
        f"mlp_out_dw_{l}", sv["r"], dx2b, (N_CHIPS, dff4, D), F32, grid=(N_CHIPS * mpc, D // tnb, 1),
        a_spec=pl.BlockSpec((S, tmw), lambda i, j, k: (0, i)),
        b_spec=pl.BlockSpec((S, tnb), lambda i, j, k: (0, j)),
        o_spec=pl.BlockSpec((None, tmw, tnb), lambda i, j, k: (i // mpc, i % mpc, j)),
        contract=(0, 0), acc_shape=(tmw, tnb), a_pre=_square)
    sink.begin("w_mlp_out", l, g_w2)
    tnx = D
    dh2 = _matmul(
        f"mlp_in_bwd_{l}", da, w_mlp_in, (S, D), F32, grid=(S // tmb, D // tnx, N_CHIPS),
        a_spec=pl.BlockSpec((tmb, dff4), lambda i, j, k: (i, k)),
        b_spec=pl.BlockSpec((None, tnx, dff4), lambda i, j, k: (k, j, 0)),
        o_spec=pl.BlockSpec((tmb, tnx), lambda i, j, k: (i, j)),
        contract=(1, 1), acc_shape=(tmb, tnx), deps=sink.deps())
    sink.advance(dh2)
    tmd = min(1024, D)
    nd = D // tmd
    tnf = min(1024, dff4)
    nf = dff4 // tnf
    g_w1 = _matmul(
        f"mlp_in_dw_{l}", sv["h2"], da, (N_CHIPS, D, dff4), F32, grid=(N_CHIPS * nd, nf, 1),
        a_spec=pl.BlockSpec((S, tmd), lambda i, j, k: (0, i % nd)),
        b_spec=pl.BlockSpec((S, tnf), lambda i, j, k: (0, (i // nd) * nf + j)),
        o_spec=pl.BlockSpec((None, tmd, tnf), lambda i, j, k: (i // nd, i % nd, j)),
        contract=(0, 0), acc_shape=(tmd, tnf))
    sink.begin("w_mlp_in", l, g_w1)
    dx1, dx1b, g_mlp_norm = _rmsnorm_bwd(f"mlp_norm_bwd_{l}", dh2, sv["x1"], prm["mlp_norm"][l], dx2,
                                         deps=sink.deps())

    rq = w_out.shape[1]
    kq = N_CHIPS * rq
    dycat = _matmul(
        f"out_proj_bwd_{l}", dx1b, w_out.reshape(kq, D), (S, kq), F32, grid=(S // tmb, 1, 1),
        a_spec=pl.BlockSpec((tmb, D), lambda i, j, k: (i, 0)),
        b_spec=pl.BlockSpec((kq, D), lambda i, j, k: (0, 0)),
        o_spec=pl.BlockSpec((tmb, kq), lambda i, j, k: (i, 0)),
        contract=(1, 1), acc_shape=(tmb, kq))
    sink.advance(dycat)
    g_wout = _matmul(
        f"out_proj_dw_{l}", sv["ycat"], dx1b, (N_CHIPS, rq, D), F32, grid=(N_CHIPS, 1, 1),
        a_spec=pl.BlockSpec((S, rq), lambda i, j, k: (0, i)),
        b_spec=pl.BlockSpec((S, D), lambda i, j, k: (0, 0)),
        o_spec=pl.BlockSpec((None, rq, D), lambda i, j, k: (i, 0, 0)),
        contract=(0, 0), acc_shape=(rq, D))
    sink.begin("w_out", l, g_wout)

    p = sv["p"]
    du, dv_a, g_sgu_w, db_lanes = _sgu_bwd(f"sgu_bwd_{l}", p, dycat, prm["sgu_wt"][l], prm["sgu_wtt"][l],
                                           prm["sgu_bb"][l])
    g_sgu_b = db_lanes[:, :A_HEADS].T
    db, dc, dxb, g_conv = _conv_bwd(f"conv_bwd_{l}", p, dycat, prm["conv_w"][l])
    do3, c3 = _mix_bwd(f"mix_bwd_{l}", sv["os"], sv["lses"], dycat, prm["bd"])
    dqs, dks, dvs, dgqs, dgks = [], [], [], [], []
    for g in range(N_PATTERNS):
        dq, dk, dv, dgq, dgk = _attn_bwd(f"attn_bwd_{l}_{g}", p, g, sv["lses"][g], do3, c3,
                                         prm["q_gain"][l], prm["k_gain"][l], prm["bd"])
        dqs.append(dq)
        dks.append(dk)
        dvs.append(dv)
        dgqs.append(dgq)
        dgks.append(dgk)
    g_q = jnp.concatenate(dgqs, axis=1).reshape(N_PATTERNS * PW // HEAD_DIM, HEAD_DIM).sum(axis=0)
    g_k = jnp.concatenate(dgks, axis=1).reshape(N_PATTERNS * PW // HEAD_DIM, HEAD_DIM).sum(axis=0)
    dp = jnp.concatenate([du, dv_a, db, dc, dxb] + [t.astype(BF16) for t in dqs + dks + dvs], axis=1)

    ns_in = w_in.shape[-1]
    tmh = min(512, D)
    nh = D // tmh
    g_win = _matmul(
        f"in_proj_dw_{l}", sv["h"], dp, (N_CHIPS, D, ns_in), F32, grid=(N_CHIPS * nh, 1, 1),
        a_spec=pl.BlockSpec((S, tmh), lambda i, j, k: (0, i % nh)),
        b_spec=pl.BlockSpec((S, ns_in), lambda i, j, k: (0, i // nh)),
        o_spec=pl.BlockSpec((None, tmh, ns_in), lambda i, j, k: (i // nh, i % nh, 0)),
        contract=(0, 0), acc_shape=(tmh, ns_in))
    sink.begin("w_in", l, g_win)
    dh = _matmul(
        f"in_proj_bwd_{l}", dp, w_in, (S, D), F32, grid=(S // tmb, D // tnx, N_CHIPS),
        a_spec=pl.BlockSpec((tmb, ns_in), lambda i, j, k: (i, k)),
        b_spec=pl.BlockSpec((None, tnx, ns_in), lambda i, j, k: (k, j, 0)),
        o_spec=pl.BlockSpec((tmb, tnx), lambda i, j, k: (i, j)),
        contract=(1, 1), acc_shape=(tmb, tnx), deps=sink.deps())
    sink.advance(dh)
    dx0, dx0b, g_attn_norm = _rmsnorm_bwd(f"attn_norm_bwd_{l}", dh, sv["x"], prm["attn_norm"][l], dx1,
                                          deps=sink.deps())

    big = dict(w_in=g_win, w_out=g_wout, w_mlp_in=g_w1, w_mlp_out=g_w2)
    small = dict(attn_norm=g_attn_norm.reshape(-1), sgu_w=g_sgu_w, sgu_b=g_sgu_b, conv_w=g_conv,
                 q_norm=g_q, k_norm=g_k, mlp_norm=g_mlp_norm.reshape(-1))
    return dx0, dx0b, big, small


BIG = ("w_in", "w_out", "w_mlp_in", "w_mlp_out")
SMALL_REPLICATED = ("attn_norm", "sgu_w", "sgu_b", "q_norm", "k_norm", "mlp_norm")


def _local_step(x, target, prm, wg, n_layers, sink):
    saved = []
    h = x
    for l in range(n_layers):
        h, sv = _layer_forward(l, h, prm, wg)
        saved.append(sv)
    dy, dyb, colsq = _loss_kernel(h, target)
    loss = 0.5 * jnp.sum(colsq) / x.shape[1]
    bigs, smalls = [None] * n_layers, [None] * n_layers
    for l in reversed(range(n_layers)):
        dy, dyb, bigs[l], smalls[l] = _layer_backward(l, dy, dyb, saved[l], prm, wg, sink)
    return loss, dy, bigs, smalls


def _prepare_params(attn_norm, sgu_w, sgu_b, conv_full, q_norm, k_norm, mlp_norm):
    n_layers = attn_norm.shape[0]
    tri = jnp.tril(sgu_w)
    idx = jnp.arange(PW)
    bd = (idx[:, None] // HEAD_DIM == idx[None, :] // HEAD_DIM).astype(BF16)
    return dict(
        attn_norm=[attn_norm[l][None, :] for l in range(n_layers)],
        mlp_norm=[mlp_norm[l][None, :] for l in range(n_layers)],
        sgu_wt=[tri[l].astype(BF16) for l in range(n_layers)],
        sgu_wtt=[tri[l].transpose(0, 2, 1).astype(BF16) for l in range(n_layers)],
        sgu_bb=[jnp.repeat(sgu_b[l].T, HEAD_DIM, axis=1) for l in range(n_layers)],
        conv_w=[conv_full[l] for l in range(n_layers)],
        q_gain=[jnp.tile(q_norm[l], PW // HEAD_DIM)[None, :] for l in range(n_layers)],
        k_gain=[jnp.tile(k_norm[l], PW // HEAD_DIM)[None, :] for l in range(n_layers)],
        bd=bd,
    )


def kernel(x, attn_norm, w_in, sgu_w, sgu_b, conv_w, q_norm, k_norm, w_out, mlp_norm, w_mlp_in, w_mlp_out, loss_target, m_attn_norm, m_w_in, m_sgu_w, m_sgu_b, m_conv_w, m_q_norm, m_k_norm, m_w_out, m_mlp_norm, m_w_mlp_in, m_w_mlp_out, v_attn_norm, v_w_in, v_sgu_w, v_sgu_b, v_conv_w, v_q_norm, v_k_norm, v_w_out, v_mlp_norm, v_w_mlp_in, v_w_mlp_out):
    n_layers = attn_norm.shape[0]
    weights = dict(attn_norm=attn_norm, w_in=w_in, sgu_w=sgu_w, sgu_b=sgu_b, conv_w=conv_w, q_norm=q_norm,
                   k_norm=k_norm, w_out=w_out, mlp_norm=mlp_norm, w_mlp_in=w_mlp_in, w_mlp_out=w_mlp_out)
    mom_m = dict(attn_norm=m_attn_norm, w_in=m_w_in, sgu_w=m_sgu_w, sgu_b=m_sgu_b, conv_w=m_conv_w,
                 q_norm=m_q_norm, k_norm=m_k_norm, w_out=m_w_out, mlp_norm=m_mlp_norm, w_mlp_in=m_w_mlp_in,
                 w_mlp_out=m_w_mlp_out)
    mom_v = dict(attn_norm=v_attn_norm, w_in=v_w_in, sgu_w=v_sgu_w, sgu_b=v_sgu_b, conv_w=v_conv_w,
                 q_norm=v_q_norm, k_norm=v_k_norm, w_out=v_w_out, mlp_norm=v_mlp_norm, w_mlp_in=v_w_mlp_in,
                 w_mlp_out=v_w_mlp_out)
    order = ("attn_norm", "w_in", "sgu_w", "sgu_b", "conv_w", "q_norm", "k_norm", "w_out", "mlp_norm",
             "w_mlp_in", "w_mlp_out")
    chip = 2 * lax.axis_index("x") + lax.axis_index("y")
    c_arr = jnp.stack([lax.axis_index("c"), chip]).astype(jnp.int32)

    conv_cols = conv_w.shape[-1]
    chip_arr = chip.astype(jnp.int32).reshape(1)
    conv_pack = jnp.pad(conv_w.reshape(-1), (0, 2048 - conv_w.size)).reshape(1, 16, 128)
    wg = _GatheredWeights()
    wg.start([("conv_w", 0), ("w_in", 0)],
             [_place_shard("place_conv_w", conv_pack, 0, chip_arr, F32),
              _place_shard("place_w_in_0", weights["w_in"], 0, chip_arr, BF16)])
    keys = [(n, l) for l in range(n_layers) for n in BIG if (n, l) != ("w_in", 0)]
    first = wg.deps()
    wg.start(keys, [_place_shard(f"place_{n}_{l}", weights[n], l, chip_arr, BF16, deps=first) for n, l in keys])
    conv_full = wg.get("conv_w", 0, wg.deps()[-1]).reshape(N_CHIPS, 2048)[:, :conv_w.size].reshape(N_CHIPS, n_layers, 3, conv_cols)
    conv_full = conv_full.transpose(1, 2, 0, 3).reshape(n_layers, 3, N_CHIPS * conv_cols)
    prm = _prepare_params(attn_norm, sgu_w, sgu_b, conv_full, q_norm, k_norm, mlp_norm)

    sink = _GradReducer(c_arr)
    loss_local, grad_x, _, smalls = _local_step(x[0], loss_target[0], prm, wg, n_layers, sink)
    loss = lax.psum(loss_local, ("x", "y", "c"))

    small_names = SMALL_REPLICATED + ("conv_w",)
    small_shapes = [(n_layers,) + tuple(smalls[0][n].shape) for n in small_names]
    packed = _pack_rows([jnp.stack([smalls[l][n] for l in range(n_layers)]) for n in small_names])
    small_send, small_recv, small_land, small_token = _small_start(packed, sink.deps())

    grads, delta, new_m, new_v = {}, {}, {}, {}

    def update(n, after):
        shp = weights[n].shape
        two_d = (shp[0] * shp[1], shp[2])
        d, nm, nv, g = _adamw(f"adamw_{n}", weights[n].reshape(two_d), sink.reduced(n, after).reshape(two_d),
                              mom_m[n].reshape(two_d), mom_v[n].reshape(two_d))
        grads[n], delta[n], new_m[n], new_v[n] = g.reshape(shp), d.reshape(shp), nm.reshape(shp), nv.reshape(shp)

    token = small_token
    for n in ("w_mlp_out", "w_mlp_in", "w_out"):
        token = sink.reduce(n, n_layers, token)
    update("w_mlp_out", token)
    token = sink.reduce("w_in", n_layers, delta["w_mlp_out"])
    update("w_mlp_in", token)
    update("w_out", delta["w_mlp_in"])
    update("w_in", delta["w_out"])
    small_land = _small_wait(packed, small_land, small_send, small_recv, delta["w_in"])
    grads.update(zip(small_names, _unpack_rows(_sum_devices(small_land), small_shapes)))
    grads["conv_w"] = lax.dynamic_slice_in_dim(grads["conv_w"], chip * conv_cols, conv_cols, axis=2)
    smalls_all = SMALL_REPLICATED + ("conv_w",)
    shapes = [weights[n].shape for n in smalls_all]
    d, nm, nv, _ = _adamw("adamw_small",
                          _pack_rows([weights[n] for n in smalls_all]), _pack_rows([grads[n] for n in smalls_all]),
                          _pack_rows([mom_m[n] for n in smalls_all]), _pack_rows([mom_v[n] for n in smalls_all]))
    for n, dd, mm, vv in zip(smalls_all, _unpack_rows(d, shapes), _unpack_rows(nm, shapes), _unpack_rows(nv, shapes)):
        delta[n], new_m[n], new_v[n] = dd, mm, vv

    return (loss, grad_x[None], *[grads[n] for n in order], *[delta[n] for n in order],
            *[new_m[n] for n in order], *[new_v[n] for n in order])
```

```python
import jax
import jax.numpy as jnp
from jax import lax
from jax.experimental import pallas as pl
from jax.experimental.pallas import tpu as pltpu

F32 = jnp.float32
BF16 = jnp.bfloat16
SDS = jax.ShapeDtypeStruct

EPS = 1e-6
HEAD_DIM = 64
A_HEADS = 8
A_WIDTH = 512
CHUNK = 128
B_WIDTH = 768
C_WIDTH = 768
N_PATTERNS = 3
PATTERN_DILATION = (1, 4, 16)
PW = 256
D_IN_PROJ = 5632
OFF_AU, OFF_AV, OFF_BB, OFF_BC, OFF_BX, OFF_Q, OFF_K, OFF_V = 0, 512, 1024, 1792, 2560, 3328, 4096, 4864
N_CHIPS = 4
N_DEV = 8
BLK = 128

ADAM_LR, ADAM_B1, ADAM_B2, ADAM_EPS, ADAM_WD, ADAM_STEP = 0.001, 0.9, 0.999, 1e-08, 0.01, 10

V7X_VMEM_LIMIT = 56 * 1024 * 1024
MESH = pl.DeviceIdType.MESH
NEG = -1e30


def _cp(n_axes):
    return pltpu.CompilerParams(dimension_semantics=("arbitrary",) * n_axes, vmem_limit_bytes=V7X_VMEM_LIMIT)


def _hbm_spec():
    return pl.BlockSpec(memory_space=pl.ANY)


def _relu(t):
    return jnp.maximum(t, 0.0)


def _square(t):
    return t * t


def _matmul(name, a, b, out_shape, out_dtype, *, grid, a_spec, b_spec, o_spec, contract, acc_shape,
            extras=(), extra_specs=(), a_pre=None, epi=None, deps=()):
    nk = grid[2]
    n_ex = len(extras)
    n_dep = len(deps)
    dims = (((contract[0],), (contract[1],)), ((), ()))

    def product(a_ref, b_ref):
        av = a_ref[...] if a_pre is None else a_pre(a_ref[...])
        return lax.dot_general(av, b_ref[...], dims, preferred_element_type=F32)

    def finish(r, ex, o_ref):
        if epi is not None:
            r = epi(r, *[e[...] for e in ex])
        o_ref[...] = r.astype(o_ref.dtype)

    def body_single(a_ref, b_ref, *rest):
        finish(product(a_ref, b_ref), rest[:n_ex], rest[n_ex + n_dep])

    def body(a_ref, b_ref, *rest):
        ex = rest[:n_ex]
        o_ref = rest[n_ex + n_dep]
        acc_ref = rest[n_ex + n_dep + 1]
        k = pl.program_id(2)

        @pl.when(k == 0)
        def _():
            acc_ref[...] = product(a_ref, b_ref)

        @pl.when((k > 0) & (k < nk - 1))
        def _():
            acc_ref[...] += product(a_ref, b_ref)

        @pl.when(k == nk - 1)
        def _():
            finish(acc_ref[...] + product(a_ref, b_ref), ex, o_ref)

    return pl.pallas_call(
        body_single if nk == 1 else body, name=name, grid=grid,
        in_specs=[a_spec, b_spec, *extra_specs] + [_hbm_spec()] * n_dep,
        out_specs=o_spec,
        out_shape=SDS(out_shape, out_dtype),
        scratch_shapes=[] if nk == 1 else [pltpu.VMEM(acc_shape, F32)],
        compiler_params=_cp(3),
    )(a, b, *extras, *deps)


def _loss_kernel(y, t):
    S, D = y.shape
    tm = min(256, S)

    def body(y_ref, t_ref, dy_ref, dyb_ref, l_ref):
        @pl.when(pl.program_id(0) == 0)
        def _():
            l_ref[...] = jnp.zeros_like(l_ref)
        e = y_ref[...] - t_ref[...]
        l_ref[...] += jnp.sum(e * e, axis=0, keepdims=True)
        dy = e * (1.0 / D)
        dy_ref[...] = dy
        dyb_ref[...] = dy.astype(BF16)

    row = pl.BlockSpec((tm, D), lambda i: (i, 0))
    return pl.pallas_call(
        body, name="loss_head", grid=(S // tm,),
        in_specs=[row, row],
        out_specs=[row, row, pl.BlockSpec((1, D), lambda i: (0, 0))],
        out_shape=[SDS((S, D), F32), SDS((S, D), BF16), SDS((1, D), F32)],
        compiler_params=_cp(1),
    )(y, t)


def _rmsnorm_fwd(name, x, g):
    S, D = x.shape
    tm = min(512, S)

    def body(x_ref, g_ref, h_ref):
        xv = x_ref[...]
        y = xv * lax.rsqrt(jnp.mean(xv * xv, axis=-1, keepdims=True) + EPS) * g_ref[...]
        h_ref[...] = y.astype(h_ref.dtype)

    row = pl.BlockSpec((tm, D), lambda i: (i, 0))
    return pl.pallas_call(
        body, name=name, grid=(S // tm,),
        in_specs=[row, pl.BlockSpec((1, D), lambda i: (0, 0))],
        out_specs=row,
        out_shape=SDS((S, D), BF16),
        compiler_params=_cp(1),
    )(x, g)


def _rmsnorm_bwd(name, dh, x, g, dres, deps=()):
    S, D = x.shape
    tm = min(256, S)
    n_dep = len(deps)

    def body(dh_ref, x_ref, g_ref, dres_ref, *rest):
        dx_ref, dxb_ref, dg_ref = rest[n_dep:]
        @pl.when(pl.program_id(0) == 0)
        def _():
            dg_ref[...] = jnp.zeros_like(dg_ref)
        xv = x_ref[...]
        dhv = dh_ref[...]
        rstd = lax.rsqrt(jnp.mean(xv * xv, axis=-1, keepdims=True) + EPS)
        xhat = xv * rstd
        dg_ref[...] += jnp.sum(dhv * xhat, axis=0, keepdims=True)
        dxn = dhv * g_ref[...]
        dx = dres_ref[...] + rstd * (dxn - xhat * jnp.mean(dxn * xhat, axis=-1, keepdims=True))
        dx_ref[...] = dx
        dxb_ref[...] = dx.astype(BF16)

    row = pl.BlockSpec((tm, D), lambda i: (i, 0))
    vec = pl.BlockSpec((1, D), lambda i: (0, 0))
    return pl.pallas_call(
        body, name=name, grid=(S // tm,),
        in_specs=[row, row, vec, row] + [_hbm_spec()] * n_dep,
        out_specs=[row, row, vec],
        out_shape=[SDS((S, D), F32), SDS((S, D), BF16), SDS((1, D), F32)],
        compiler_params=_cp(1),
    )(dh, x, g, dres, *deps)


def _adamw(name, w, g, m, v):
    R, C = w.shape
    tr = 256 if R % 256 == 0 else R
    c1 = 1.0 - ADAM_B1 ** ADAM_STEP
    c2 = 1.0 - ADAM_B2 ** ADAM_STEP

    def body(w_ref, g_ref, m_ref, v_ref, d_ref, nm_ref, nv_ref, g_out_ref):
        gv = g_ref[...]
        nm = ADAM_B1 * m_ref[...] + (1.0 - ADAM_B1) * gv
        nv = ADAM_B2 * v_ref[...] + (1.0 - ADAM_B2) * (gv * gv)
        m_hat = nm / c1
        v_hat = nv / c2
        d_ref[...] = -ADAM_LR * (m_hat / (jnp.sqrt(v_hat) + ADAM_EPS) + ADAM_WD * w_ref[...])
        nm_ref[...] = nm
        nv_ref[...] = nv
        g_out_ref[...] = gv

    blk = pl.BlockSpec((tr, C), lambda i: (i, 0))
    return pl.pallas_call(
        body, name=name, grid=(R // tr,),
        in_specs=[blk] * 4, out_specs=[blk] * 4,
        out_shape=[SDS((R, C), F32)] * 4,
        compiler_params=_cp(1),
    )(w, g, m, v)


SGU_STEP_ROWS = 512


def _pair_select(lane, lo, hi):
    return jnp.where(lane < HEAD_DIM, lo, hi)


def _sgu_fwd(name, p, wt, bb):
    S = p.shape[0]

    rows = min(SGU_STEP_ROWS, S)

    def body(u_ref, v_ref, wt_ref, bb_ref, o_ref):
        lane = lax.broadcasted_iota(jnp.int32, (CHUNK, 128), 1)
        for ci in range(rows // CHUNK):
            rs = slice(CHUNK * ci, CHUNK * (ci + 1))
            for pp in range(A_HEADS // 2):
                cs = slice(128 * pp, 128 * (pp + 1))
                vb = v_ref[rs, cs].astype(BF16)
                mixed = _pair_select(lane,
                                     jnp.dot(wt_ref[2 * pp], vb, preferred_element_type=F32),
                                     jnp.dot(wt_ref[2 * pp + 1], vb, preferred_element_type=F32)) + bb_ref[:, cs]
                o_ref[rs, cs] = (u_ref[rs, cs] * mixed).astype(o_ref.dtype)

    return pl.pallas_call(
        body, name=name, grid=(S // rows,),
        in_specs=[pl.BlockSpec((rows, A_WIDTH), lambda c: (c, OFF_AU // A_WIDTH)),
                  pl.BlockSpec((rows, A_WIDTH), lambda c: (c, OFF_AV // A_WIDTH)),
                  pl.BlockSpec((A_HEADS, CHUNK, CHUNK), lambda c: (0, 0, 0)),
                  pl.BlockSpec((CHUNK, A_WIDTH), lambda c: (0, 0))],
        out_specs=pl.BlockSpec((rows, A_WIDTH), lambda c: (c, 0)),
        out_shape=SDS((S, A_WIDTH), BF16),
        compiler_params=_cp(1),
    )(p, p, wt, bb)


def _sgu_bwd(name, p, dycat, wt, wtt, bb):
    S = p.shape[0]
    rows = min(SGU_STEP_ROWS, S)

    def body(u_ref, v_ref, dy_ref, wt_ref, wtt_ref, bb_ref, du_ref, dv_ref, dw_ref, db_ref, dbacc_ref):
        c = pl.program_id(0)

        @pl.when(c == 0)
        def _():
            dw_ref[...] = jnp.zeros_like(dw_ref)
            dbacc_ref[...] = jnp.zeros_like(dbacc_ref)

        lane = lax.broadcasted_iota(jnp.int32, (CHUNK, 128), 1)
        row = lax.broadcasted_iota(jnp.int32, (CHUNK, 128), 0)
        causal = row >= lane
        nt = (((1,), (1,)), ((), ()))
        for pp in range(A_HEADS // 2):
            cs = slice(128 * pp, 128 * (pp + 1))
            dw_lo = jnp.zeros((CHUNK, CHUNK), F32)
            dw_hi = jnp.zeros((CHUNK, CHUNK), F32)
            dm_sum = jnp.zeros((CHUNK, 128), F32)
            for ci in range(rows // CHUNK):
                rs = slice(CHUNK * ci, CHUNK * (ci + 1))
                vb = v_ref[rs, cs].astype(BF16)
                dy = dy_ref[rs, cs]
                mixed = _pair_select(lane,
                                     jnp.dot(wt_ref[2 * pp], vb, preferred_element_type=F32),
                                     jnp.dot(wt_ref[2 * pp + 1], vb, preferred_element_type=F32)) + bb_ref[:, cs]
                du_ref[rs, cs] = (dy * mixed).astype(du_ref.dtype)
                dm = dy * u_ref[rs, cs]
                dmb = dm.astype(BF16)
                dv = _pair_select(lane,
                                  jnp.dot(wtt_ref[2 * pp], dmb, preferred_element_type=F32),
                                  jnp.dot(wtt_ref[2 * pp + 1], dmb, preferred_element_type=F32))
                dv_ref[rs, cs] = dv.astype(dv_ref.dtype)
                dm_sum += dm
                dm_lo = jnp.where(lane < HEAD_DIM, dm, 0.0).astype(BF16)
                dm_hi = jnp.where(lane >= HEAD_DIM, dm, 0.0).astype(BF16)
                dw_lo += lax.dot_general(dm_lo, vb, nt, preferred_element_type=F32)
                dw_hi += lax.dot_general(dm_hi, vb, nt, preferred_element_type=F32)
            dbacc_ref[:, cs] += dm_sum
            dw_ref[2 * pp] += jnp.where(causal, dw_lo, 0.0)
            dw_ref[2 * pp + 1] += jnp.where(causal, dw_hi, 0.0)

        @pl.when(c == S // rows - 1)
        def _():
            out = jnp.zeros((CHUNK, 128), F32)
            for pp in range(A_HEADS // 2):
                acc = dbacc_ref[:, 128 * pp:128 * (pp + 1)]
                s_lo = jnp.sum(jnp.where(lane < HEAD_DIM, acc, 0.0), axis=1, keepdims=True)
                s_hi = jnp.sum(jnp.where(lane >= HEAD_DIM, acc, 0.0), axis=1, keepdims=True)
                out = jnp.where(lane == 2 * pp, s_lo, out)
                out = jnp.where(lane == 2 * pp + 1, s_hi, out)
            db_ref[...] = out

    chunk = lambda col: pl.BlockSpec((rows, A_WIDTH), lambda c: (c, col))
    wspec = pl.BlockSpec((A_HEADS, CHUNK, CHUNK), lambda c: (0, 0, 0))
    return pl.pallas_call(
        body, name=name, grid=(S // rows,),
        in_specs=[chunk(OFF_AU // A_WIDTH), chunk(OFF_AV // A_WIDTH), chunk(0), wspec, wspec,
                  pl.BlockSpec((CHUNK, A_WIDTH), lambda c: (0, 0))],
        out_specs=[chunk(0), chunk(0), wspec, pl.BlockSpec((CHUNK, 128), lambda c: (0, 0))],
        out_shape=[SDS((S, A_WIDTH), BF16), SDS((S, A_WIDTH), BF16),
                   SDS((A_HEADS, CHUNK, CHUNK), F32), SDS((CHUNK, 128), F32)],
        scratch_shapes=[pltpu.VMEM((CHUNK, A_WIDTH), F32)],
        compiler_params=_cp(1),
    )(p, p, dycat, wt, wtt, bb)


CONV_HALO = 8
CONV_COLS = 256
CONV_ROWS = 1024


def _shift_down(a, halo, k):
    T = a.shape[0]
    row = lax.broadcasted_iota(jnp.int32, a.shape, 0)
    out = pltpu.roll(a, k, 0)
    for r in range(k):
        out = jnp.where(row == r, halo[CONV_HALO - k + r:CONV_HALO - k + r + 1, :], out)
    return out


def _shift_up(a, halo, k):
    T = a.shape[0]
    row = lax.broadcasted_iota(jnp.int32, a.shape, 0)
    out = pltpu.roll(a, T - k, 0)
    for r in range(k):
        out = jnp.where(row == T - k + r, halo[r:r + 1, :], out)
    return out


def _conv_specs(S, T):
    hb = T // CONV_HALO
    last = S // CONV_HALO - 1
    tile = lambda col0: pl.BlockSpec((T, CONV_COLS), lambda j, i: (i, col0 + j))
    prev = lambda col0: pl.BlockSpec((CONV_HALO, CONV_COLS), lambda j, i: (jnp.maximum(i * hb - 1, 0), col0 + j))
    nxt = lambda col0: pl.BlockSpec((CONV_HALO, CONV_COLS), lambda j, i: (jnp.minimum((i + 1) * hb, last), col0 + j))
    return tile, prev, nxt


def _conv_fwd(name, p, w):
    S = p.shape[0]
    T = min(CONV_ROWS, S)
    tile, prev, _ = _conv_specs(S, T)
    cb, cc, cx = OFF_BB // CONV_COLS, OFF_BC // CONV_COLS, OFF_BX // CONV_COLS

    def body(b_ref, c_ref, x_ref, ch_ref, xh_ref, w_ref, o_ref):
        i = pl.program_id(1)
        z = c_ref[...] * x_ref[...]
        zh = jnp.where(i > 0, ch_ref[...] * xh_ref[...], 0.0)
        z1 = _shift_down(z, zh, 1)
        z2 = _shift_down(z, zh, 2)
        conv = w_ref[0:1, :] * z2 + w_ref[1:2, :] * z1 + w_ref[2:3, :] * z
        o_ref[...] = (b_ref[...] * conv).astype(o_ref.dtype)

    return pl.pallas_call(
        body, name=name, grid=(B_WIDTH // CONV_COLS, S // T),
        in_specs=[tile(cb), tile(cc), tile(cx), prev(cc), prev(cx),
                  pl.BlockSpec((3, CONV_COLS), lambda j, i: (0, j))],
        out_specs=tile(0),
        out_shape=SDS((S, B_WIDTH), BF16),
        compiler_params=_cp(2),
    )(p, p, p, p, p, w)


def _conv_bwd(name, p, dycat, w):
    S = p.shape[0]
    T = min(CONV_ROWS, S)
    tile, prev, nxt = _conv_specs(S, T)
    cb, cc, cx = OFF_BB // CONV_COLS, OFF_BC // CONV_COLS, OFF_BX // CONV_COLS
    cdy = A_WIDTH // CONV_COLS
    n_i = S // T

    def body(b_ref, c_ref, x_ref, dy_ref, ch_ref, xh_ref, bn_ref, dyn_ref, w_ref,
             db_ref, dc_ref, dx_ref, dw_ref):
        i = pl.program_id(1)

        @pl.when(i == 0)
        def _():
            dw_ref[...] = jnp.zeros_like(dw_ref)

        cv = c_ref[...]
        xv = x_ref[...]
        z = cv * xv
        zh = jnp.where(i > 0, ch_ref[...] * xh_ref[...], 0.0)
        z1 = _shift_down(z, zh, 1)
        z2 = _shift_down(z, zh, 2)
        w0, w1, w2 = w_ref[0:1, :], w_ref[1:2, :], w_ref[2:3, :]
        conv = w0 * z2 + w1 * z1 + w2 * z
        dy = dy_ref[...]
        db_ref[...] = (dy * conv).astype(db_ref.dtype)
        dconv = dy * b_ref[...]
        dconv_n = jnp.where(i < n_i - 1, dyn_ref[...] * bn_ref[...], 0.0)
        dz = w2 * dconv + w1 * _shift_up(dconv, dconv_n, 1) + w0 * _shift_up(dconv, dconv_n, 2)
        dc_ref[...] = (dz * xv).astype(dc_ref.dtype)
        dx_ref[...] = (dz * cv).astype(dx_ref.dtype)
        dw_ref[0:1, :] += jnp.sum(dconv * z2, axis=0, keepdims=True)
        dw_ref[1:2, :] += jnp.sum(dconv * z1, axis=0, keepdims=True)
        dw_ref[2:3, :] += jnp.sum(dconv * z, axis=0, keepdims=True)

    wspec = pl.BlockSpec((3, CONV_COLS), lambda j, i: (0, j))
    return pl.pallas_call(
        body, name=name, grid=(B_WIDTH // CONV_COLS, n_i),
        in_specs=[tile(cb), tile(cc), tile(cx), tile(cdy), prev(cc), prev(cx), nxt(cb), nxt(cdy), wspec],
        out_specs=[tile(0), tile(0), tile(0), wspec],
        out_shape=[SDS((S, B_WIDTH), BF16)] * 3 + [SDS((3, B_WIDTH), F32)],
        compiler_params=_cp(2),
    )(p, p, p, dycat, p, p, p, dycat, w)


def _seg_sum(t, bd):
    hi = t.astype(BF16)
    lo = (t - hi.astype(F32)).astype(BF16)
    return jnp.dot(hi, bd, preferred_element_type=F32) + jnp.dot(lo, bd, preferred_element_type=F32)


def _head_norm(x, g, bd):
    rstd = lax.rsqrt(_seg_sum(x * x, bd) * (1.0 / HEAD_DIM) + EPS)
    xhat = x * rstd
    return xhat * g, xhat, rstd


def _head_norm_bwd(dy, g, xhat, rstd, bd):
    dxh = dy * g
    return rstd * (dxh - xhat * (_seg_sum(dxh * xhat, bd) * (1.0 / HEAD_DIM)))


def _band_mask(has_prev):
    row = lax.broadcasted_iota(jnp.int32, (BLK, 2 * BLK), 0)
    col = lax.broadcasted_iota(jnp.int32, (BLK, 2 * BLK), 1)
    first_key = jnp.where(has_prev, 0, BLK)
    return (col >= row) & (col <= row + BLK) & (col >= first_key)


def _residue_rows(r, d):
    return slice(None) if d == 1 else pl.ds(r, BLK, stride=d)


STRIDED_LANES = 128


def _step_width(d):
    return PW if d == 1 else STRIDED_LANES


def _n_stack(lane):
    return lane.shape[1] // HEAD_DIM


def _for_residues(d, fn):
    if d == 1:
        fn(0)
    else:
        def two(i, carry):
            fn(2 * i)
            fn(2 * i + 1)
            return carry
        lax.fori_loop(0, d // 2, two, 0)


def _head_mask(lane, j):
    return (lane >= HEAD_DIM * j) & (lane < HEAD_DIM * (j + 1))


def _stack_heads(x, lane):
    return jnp.concatenate([jnp.where(_head_mask(lane, j), x, 0.0) for j in range(_n_stack(lane))], axis=0)


def _unstack_heads(y, lane):
    out = y[:BLK]
    for j in range(1, _n_stack(lane)):
        out = jnp.where(lane >= HEAD_DIM * j, y[BLK * j:BLK * (j + 1)], out)
    return out


def _head_columns(v, lane):
    return jnp.concatenate([jnp.max(jnp.where(_head_mask(lane, j), v, NEG), axis=1, keepdims=True)
                            for j in range(_n_stack(lane))], axis=0)


def _attn_fwd(name, p, g, gq, gk, bd):
    S = p.shape[0]
    d = PATTERN_DILATION[g]
    rows = BLK * d
    hw = _step_width(d)
    nt = (((1,), (1,)), ((), ()))

    def body(q_ref, kc_ref, kp_ref, vc_ref, vp_ref, gq_ref, gk_ref, bd_ref, o_ref, lse_ref):
        has_prev = pl.program_id(1) > 0
        bdv = bd_ref[...]
        band = jnp.concatenate([_band_mask(has_prev)] * (hw // HEAD_DIM), axis=0)
        lane = lax.broadcasted_iota(jnp.int32, (1, hw), 1)

        def residue(r):
            rr = _residue_rows(r, d)
            qn, _, _ = _head_norm(q_ref[rr, :], gq_ref[...], bdv)
            kn, _, _ = _head_norm(jnp.concatenate([kp_ref[rr, :], kc_ref[rr, :]], axis=0), gk_ref[...], bdv)
            knb = kn.astype(BF16)
            vb = jnp.concatenate([vp_ref[rr, :], vc_ref[rr, :]], axis=0).astype(BF16)
            qs = _stack_heads(qn, lane).astype(BF16)
            s = lax.dot_general(qs, knb, nt, preferred_element_type=F32) * (HEAD_DIM ** -0.5)
            s = jnp.where(band, s, NEG)
            m = jnp.max(s, axis=1, keepdims=True)
            e = jnp.exp(s - m)
            den = jnp.sum(e, axis=1, keepdims=True)
            pv = jnp.dot(e.astype(BF16), vb, preferred_element_type=F32)
            o_ref[rr, :] = _unstack_heads(pv / den, lane)
            lse_ref[rr, :] = _unstack_heads(jnp.broadcast_to(m + jnp.log(den), pv.shape), lane)

        _for_residues(d, residue)

    per = PW // hw
    cq, ck, cv = (OFF_Q + PW * g) // hw, (OFF_K + PW * g) // hw, (OFF_V + PW * g) // hw
    cur = lambda col: pl.BlockSpec((rows, hw), lambda h, n: (n, col + h))
    prv = lambda col: pl.BlockSpec((rows, hw), lambda h, n: (jnp.maximum(n - 1, 0), col + h))
    vec = pl.BlockSpec((1, hw), lambda h, n: (0, h))
    return pl.pallas_call(
        body, name=name, grid=(per, S // rows),
        in_specs=[cur(cq), cur(ck), prv(ck), cur(cv), prv(cv), vec, vec, pl.BlockSpec((hw, hw), lambda h, n: (0, 0))],
        out_specs=[cur(0), cur(0)],
        out_shape=[SDS((S, PW), F32)] * 2,
        compiler_params=_cp(2),
    )(p, p, p, p, p, gq, gk, bd)


def _attn_bwd(name, p, g, lse, do3, c3, gq, gk, bd):
    S = p.shape[0]
    d = PATTERN_DILATION[g]
    rows = BLK * d
    nblk = S // rows
    hw = _step_width(d)
    nt = (((1,), (1,)), ((), ()))
    tn = (((0,), (0,)), ((), ()))

    def body(q_ref, kc_ref, kp_ref, vc_ref, vp_ref, lse_ref, do_ref, c_ref, gq_ref, gk_ref, bd_ref,
             dq_ref, dk_ref, dv_ref, dgq_ref, dgk_ref, ck_ref, cv_ref, dq_keep_ref):
        n = pl.program_id(1)

        @pl.when(n == 0)
        def _():
            ck_ref[...] = jnp.zeros_like(ck_ref)
            cv_ref[...] = jnp.zeros_like(cv_ref)
            dgq_ref[...] = jnp.zeros_like(dgq_ref)
            dgk_ref[...] = jnp.zeros_like(dgk_ref)

        @pl.when(n == nblk)
        def _():
            dq_ref[...] = dq_keep_ref[...]
            dk_ref[...] = ck_ref[...]
            dv_ref[...] = cv_ref[...]

        bdv = bd_ref[...]
        gqv = gq_ref[...]
        gkv = gk_ref[...]
        band = jnp.concatenate([_band_mask(n > 0)] * (hw // HEAD_DIM), axis=0)
        lane = lax.broadcasted_iota(jnp.int32, (1, hw), 1)

        def residue(r):
            rr = _residue_rows(r, d)
            qn, qhat, qrstd = _head_norm(q_ref[rr, :], gqv, bdv)
            kn, khat, krstd = _head_norm(jnp.concatenate([kp_ref[rr, :], kc_ref[rr, :]], axis=0), gkv, bdv)
            knb = kn.astype(BF16)
            vb = jnp.concatenate([vp_ref[rr, :], vc_ref[rr, :]], axis=0).astype(BF16)
            qs = _stack_heads(qn, lane).astype(BF16)
            dos = _stack_heads(do_ref[rr, :], lane).astype(BF16)
            s = lax.dot_general(qs, knb, nt, preferred_element_type=F32) * (HEAD_DIM ** -0.5)
            prob = jnp.where(band, jnp.exp(s - _head_columns(lse_ref[rr, :], lane)), 0.0)
            dp = lax.dot_general(dos, vb, nt, preferred_element_type=F32)
            ds = (prob * (dp + _head_columns(c_ref[rr, :], lane)) * (HEAD_DIM ** -0.5)).astype(BF16)
            dqn = _unstack_heads(jnp.dot(ds, knb, preferred_element_type=F32), lane)
            dkn = lax.dot_general(ds, qs, tn, preferred_element_type=F32)
            dvv = lax.dot_general(prob.astype(BF16), dos, tn, preferred_element_type=F32)

            dq = _head_norm_bwd(dqn, gqv, qhat, qrstd, bdv)
            dq_ref[rr, :] = dq
            dq_keep_ref[rr, :] = dq
            dk2 = _head_norm_bwd(dkn, gkv, khat, krstd, bdv)
            dgq_ref[...] += jnp.sum(dqn * qhat, axis=0, keepdims=True)
            dgk_ref[...] += jnp.sum(dkn * khat, axis=0, keepdims=True)
            dk_ref[rr, :] = ck_ref[rr, :] + dk2[:BLK]
            dv_ref[rr, :] = cv_ref[rr, :] + dvv[:BLK]
            ck_ref[rr, :] = dk2[BLK:]
            cv_ref[rr, :] = dvv[BLK:]

        @pl.when(n < nblk)
        def _():
            _for_residues(d, residue)

    last = nblk - 1
    per = PW // hw
    cq, ck, cv = (OFF_Q + PW * g) // hw, (OFF_K + PW * g) // hw, (OFF_V + PW * g) // hw
    cur = lambda col: pl.BlockSpec((rows, hw), lambda h, n: (jnp.minimum(n, last), col + h))
    prv = lambda col: pl.BlockSpec((rows, hw), lambda h, n: (jnp.maximum(jnp.minimum(n, last) - 1, 0), col + h))
    cur3 = pl.BlockSpec((None, rows, hw), lambda h, n: (g, jnp.minimum(n, last), h))
    done = pl.BlockSpec((rows, hw), lambda h, n: (jnp.maximum(n - 1, 0), h))
    vec = pl.BlockSpec((1, hw), lambda h, n: (0, h))
    return pl.pallas_call(
        body, name=name, grid=(per, nblk + 1),
        in_specs=[cur(cq), cur(ck), prv(ck), cur(cv), prv(cv), cur(0), cur3, cur3, vec, vec,
                  pl.BlockSpec((hw, hw), lambda h, n: (0, 0))],
        out_specs=[cur(0), done, done, vec, vec],
        out_shape=[SDS((S, PW), F32)] * 3 + [SDS((1, PW), F32)] * 2,
        scratch_shapes=[pltpu.VMEM((rows, hw), F32)] * 3,
        compiler_params=_cp(2),
    )(p, p, p, p, p, lse, do3, c3, gq, gk, bd)


def _mix_fwd(name, os, lses):
    S = os[0].shape[0]
    tm = min(512, S)

    def body(o0, o1, o2, l0, l1, l2, y_ref):
        o = [o0[...], o1[...], o2[...]]
        l = [l0[...], l1[...], l2[...]]
        m = jnp.maximum(jnp.maximum(l[0], l[1]), l[2])
        e = [jnp.exp(t - m) for t in l]
        inv = 1.0 / (e[0] + e[1] + e[2])
        for g in range(N_PATTERNS):
            y_ref[:, PW * g:PW * (g + 1)] = (o[g] * (e[g] * inv)).astype(y_ref.dtype)

    blk = pl.BlockSpec((tm, PW), lambda i: (i, 0))
    return pl.pallas_call(
        body, name=name, grid=(S // tm,),
        in_specs=[blk] * 6,
        out_specs=pl.BlockSpec((tm, C_WIDTH), lambda i: (i, 0)),
        out_shape=SDS((S, C_WIDTH), BF16),
        compiler_params=_cp(1),
    )(*os, *lses)


def _mix_bwd(name, os, lses, dycat, bd):
    S = os[0].shape[0]
    tm = min(512, S)
    c0 = (A_WIDTH + B_WIDTH) // PW

    def body(o0, o1, o2, l0, l1, l2, dy0_ref, dy1_ref, dy2_ref, bd_ref, do_ref, c_ref):
        bdv = bd_ref[...]
        o = [o0[...], o1[...], o2[...]]
        l = [l0[...], l1[...], l2[...]]
        dys = [dy0_ref[...], dy1_ref[...], dy2_ref[...]]
        m = jnp.maximum(jnp.maximum(l[0], l[1]), l[2])
        e = [jnp.exp(t - m) for t in l]
        inv = 1.0 / (e[0] + e[1] + e[2])
        alpha = [t * inv for t in e]
        da = [_seg_sum(dys[g] * o[g], bdv) for g in range(N_PATTERNS)]
        mean_da = alpha[0] * da[0] + alpha[1] * da[1] + alpha[2] * da[2]
        for g in range(N_PATTERNS):
            do_ref[g] = dys[g] * alpha[g]
            c_ref[g] = -alpha[g] * mean_da

    blk = pl.BlockSpec((tm, PW), lambda i: (i, 0))
    blk3 = pl.BlockSpec((N_PATTERNS, tm, PW), lambda i: (0, i, 0))
    dyspec = lambda g: pl.BlockSpec((tm, PW), lambda i: (i, c0 + g))
    return pl.pallas_call(
        body, name=name, grid=(S // tm,),
        in_specs=[blk] * 6 + [dyspec(0), dyspec(1), dyspec(2), pl.BlockSpec((PW, PW), lambda i: (0, 0))],
        out_specs=[blk3, blk3],
        out_shape=[SDS((N_PATTERNS, S, PW), F32)] * 2,
        compiler_params=_cp(1),
    )(*os, *lses, dycat, dycat, dycat, bd)


def _mesh_pos():
    x, y, c = lax.axis_index("x"), lax.axis_index("y"), lax.axis_index("c")
    chips = [(1 - x, y), (x, 1 - y), (1 - x, 1 - y)]
    chip_idx = [2 * cx + cy for cx, cy in chips]
    return x, y, c, 2 * x + y, chips, chip_idx


def _place_shard(name, w, layer, chip_arr, out_dtype, deps=()):
    _, R, C = w.shape
    tr = min(256, R)

    def body(chip_ref, w_ref, *rest):
        o_ref = rest[-1]
        o_ref[...] = w_ref[...].astype(o_ref.dtype)

    return pl.pallas_call(
        body, name=name,
        grid_spec=pltpu.PrefetchScalarGridSpec(
            num_scalar_prefetch=1, grid=(R // tr,),
            in_specs=[pl.BlockSpec((None, tr, C), lambda i, chip_ref: (layer, i, 0))] + [_hbm_spec()] * len(deps),
            out_specs=pl.BlockSpec((None, tr, C), lambda i, chip_ref: (chip_ref[0], i, 0))),
        out_shape=SDS((N_CHIPS, R, C), out_dtype),
        compiler_params=_cp(1),
    )(chip_arr, w, *deps)


HBM_SPEC = pl.BlockSpec(memory_space=pltpu.HBM)
SEM_SPEC = pl.BlockSpec(memory_space=pltpu.SEMAPHORE)
SPLIT_COPY = pltpu.SideEffectType.DATAFLOW_SIDE_EFFECTING
N_PEER_CHIPS = N_CHIPS - 1
TOKEN_SHAPE = SDS((8, 128), F32)
TOKEN_SPEC = pl.BlockSpec(memory_space=pltpu.VMEM)


def _in_hbm(a):
    return pltpu.with_memory_space_constraint(a, pltpu.HBM)


def _gather_start(name, bufs):
    T = len(bufs)

    def body(*refs):
        ins = refs[:T]
        send_sems, recv_sems = refs[T:2 * T], refs[2 * T:3 * T]
        token = refs[4 * T]
        x, y, c, me, chips, chip_idx = _mesh_pos()
        for t in range(T):
            hr = ins[t].shape[1] // 2
            mine = ins[t].at[me, pl.ds(c * hr, hr), :]
            for j in range(N_PEER_CHIPS):
                pltpu.make_async_remote_copy(src_ref=mine, dst_ref=mine, send_sem=send_sems[t].at[j],
                                             recv_sem=recv_sems[t].at[j], device_id=(*chips[j], c),
                                             device_id_type=MESH).start()
        token[...] = jnp.zeros_like(token)

    sems = [pltpu.SemaphoreType.DMA((N_PEER_CHIPS,))] * T
    out = pl.pallas_call(
        body, name=name,
        in_specs=[HBM_SPEC] * T,
        out_specs=[SEM_SPEC] * (2 * T) + [HBM_SPEC] * T + [TOKEN_SPEC],
        out_shape=sems + sems + [pltpu.HBM(b.shape, b.dtype) for b in bufs] + [TOKEN_SHAPE],
        input_output_aliases={t: 2 * T + t for t in range(T)},
        compiler_params=pltpu.CompilerParams(has_side_effects=SPLIT_COPY),
    )(*[_in_hbm(b) for b in bufs])
    return out[:T], out[T:2 * T], out[2 * T:3 * T], out[3 * T]


def _gather_wait(name, buf, send_sem, recv_sem, after):
    n_in = 3 if after is None else 4

    def body(*refs):
        buf_ref, ssem, rsem = refs[:3]
        x, y, c, me, chips, chip_idx = _mesh_pos()
        hr = buf_ref.shape[1] // 2
        mine = buf_ref.at[me, pl.ds(c * hr, hr), :]
        for j in range(N_PEER_CHIPS):
            got = buf_ref.at[chip_idx[j], pl.ds(c * hr, hr), :]
            cp = pltpu.make_async_remote_copy(src_ref=mine, dst_ref=got, send_sem=ssem.at[j], recv_sem=rsem.at[j],
                                              device_id=(*chips[j], c), device_id_type=MESH)
            cp.wait_send()
            cp.wait_recv()

    args = [buf, send_sem, recv_sem] + ([] if after is None else [after])
    return pl.pallas_call(
        body, name=name,
        in_specs=[HBM_SPEC, SEM_SPEC, SEM_SPEC] + [_hbm_spec()] * (n_in - 3),
        out_specs=HBM_SPEC,
        out_shape=pltpu.HBM(buf.shape, buf.dtype),
        input_output_aliases={0: 0},
        compiler_params=pltpu.CompilerParams(has_side_effects=SPLIT_COPY),
    )(*args)


def _forward_start(name, buf):
    def body(buf_ref, send_sems, recv_sems, buf_thru, token):
        x, y, c, me, chips, chip_idx = _mesh_pos()
        hr = buf_ref.shape[1] // 2
        for j in range(N_PEER_CHIPS):
            got = buf_ref.at[chip_idx[j], pl.ds(c * hr, hr), :]
            pltpu.make_async_remote_copy(src_ref=got, dst_ref=got, send_sem=send_sems.at[j], recv_sem=recv_sems.at[j],
                                         device_id=(x, y, 1 - c), device_id_type=MESH).start()
        token[...] = jnp.zeros_like(token)

    sems = pltpu.SemaphoreType.DMA((N_PEER_CHIPS,))
    return pl.pallas_call(
        body, name=name,
        in_specs=[HBM_SPEC],
        out_specs=[SEM_SPEC, SEM_SPEC, HBM_SPEC, TOKEN_SPEC],
        out_shape=[sems, sems, pltpu.HBM(buf.shape, buf.dtype), TOKEN_SHAPE],
        input_output_aliases={0: 2},
        compiler_params=pltpu.CompilerParams(has_side_effects=SPLIT_COPY),
    )(_in_hbm(buf))


def _forward_wait(name, buf, send_sems, recv_sems, after):
    n_in = 3 if after is None else 4

    def body(*refs):
        buf_ref, ssems, rsems = refs[:3]
        x, y, c, me, chips, chip_idx = _mesh_pos()
        hr = buf_ref.shape[1] // 2
        for j in range(N_PEER_CHIPS):
            sent = buf_ref.at[chip_idx[j], pl.ds(c * hr, hr), :]
            theirs = buf_ref.at[chip_idx[j], pl.ds((1 - c) * hr, hr), :]
            cp = pltpu.make_async_remote_copy(src_ref=sent, dst_ref=theirs, send_sem=ssems.at[j],
                                              recv_sem=rsems.at[j], device_id=(x, y, 1 - c), device_id_type=MESH)
            cp.wait_send()
            cp.wait_recv()

    args = [buf, send_sems, recv_sems] + ([] if after is None else [after])
    return pl.pallas_call(
        body, name=name,
        in_specs=[HBM_SPEC, SEM_SPEC, SEM_SPEC] + [_hbm_spec()] * (n_in - 3),
        out_specs=HBM_SPEC,
        out_shape=pltpu.HBM(buf.shape, buf.dtype),
        input_output_aliases={0: 0},
        compiler_params=pltpu.CompilerParams(has_side_effects=SPLIT_COPY),
    )(*args)


class _GatheredWeights:
    def __init__(self):
        self._order = []
        self._pending = {}
        self._forwarding = {}
        self._ready = {}
        self._tokens = []

    def start(self, keys, bufs):
        send_sems, recv_sems, thru, token = _gather_start(f"gather_start_{len(self._order)}", bufs)
        self._tokens.append(token)
        self._order.extend(keys)
        self._pending.update({k: (b, s, r) for k, b, s, r in zip(keys, thru, send_sems, recv_sems)})

    def _prefetch(self, key, after):
        if key in self._pending:
            buf, ssem, rsem = self._pending.pop(key)
            tag = f"{key[0]}_{key[1]}"
            buf = _gather_wait(f"gather_wait_{tag}", buf, ssem, rsem, after)
            ssems, rsems, buf, token = _forward_start(f"gather_fwd_start_{tag}", buf)
            self._forwarding[key] = (buf, ssems, rsems)
            self._tokens.append(token)

    def get(self, name, layer, after=None, prefetch_next=True):
        key = (name, layer)
        if key not in self._ready:
            self._prefetch(key, after)
            buf, ssems, rsems = self._forwarding.pop(key)
            self._ready[key] = _forward_wait(f"gather_fwd_wait_{name}_{layer}", buf, ssems, rsems, after)
            if prefetch_next:
                self.prefetch_after(name, layer, after)
        return self._ready[key]

    def prefetch_after(self, name, layer, after):
        nxt = self._order.index((name, layer)) + 1
        if nxt < len(self._order):
            self._prefetch(self._order[nxt], after)

    def deps(self):
        tokens, self._tokens = self._tokens, []
        return tokens


def _swap_copy(g_ref, land_ref, send_sem, recv_sem):
    x, y, c, _, _, _ = _mesh_pos()
    hr = g_ref.shape[1] // 2
    return pltpu.make_async_remote_copy(src_ref=g_ref.at[:, pl.ds((1 - c) * hr, hr), :], dst_ref=land_ref,
                                        send_sem=send_sem, recv_sem=recv_sem, device_id=(x, y, 1 - c),
                                        device_id_type=MESH)


def _swap_start(name, g):
    land_shape = (g.shape[0], g.shape[1] // 2, g.shape[2])

    def body(g_ref, land_ref, send_sem, recv_sem, land_thru, token):
        _swap_copy(g_ref, land_ref, send_sem, recv_sem).start()
        token[...] = jnp.zeros_like(token)

    return pl.pallas_call(
        body, name=name,
        in_specs=[HBM_SPEC, HBM_SPEC],
        out_specs=[SEM_SPEC, SEM_SPEC, HBM_SPEC, TOKEN_SPEC],
        out_shape=[pltpu.SemaphoreType.DMA(()), pltpu.SemaphoreType.DMA(()), pltpu.HBM(land_shape, g.dtype),
                   TOKEN_SHAPE],
        input_output_aliases={1: 2},
        compiler_params=pltpu.CompilerParams(has_side_effects=SPLIT_COPY),
    )(_in_hbm(g), _in_hbm(lax.empty(land_shape, g.dtype)))


def _swap_wait(name, g, land, send_sem, recv_sem, after):
    def body(g_ref, land_ref, send_sem, recv_sem, after_ref, land_out):
        cp = _swap_copy(g_ref, land_ref, send_sem, recv_sem)
        cp.wait_send()
        cp.wait_recv()

    return pl.pallas_call(
        body, name=name,
        in_specs=[HBM_SPEC, HBM_SPEC, SEM_SPEC, SEM_SPEC, _hbm_spec()],
        out_specs=HBM_SPEC,
        out_shape=pltpu.HBM(land.shape, land.dtype),
        input_output_aliases={1: 0},
        compiler_params=pltpu.CompilerParams(has_side_effects=SPLIT_COPY),
    )(_in_hbm(g), land, send_sem, recv_sem, after)


def _add_my_half(name, g, r, pos_arr):
    ns, R, C = g.shape
    hr = R // 2
    tr = min(256, hr)
    nt = hr // tr

    def body(pos_ref, g_ref, r_ref, o_ref, land_ref):
        t = (g_ref[...] + r_ref[...]).astype(o_ref.dtype)
        o_ref[...] = t

        @pl.when(pl.program_id(1) == pos_ref[1])
        def _():
            land_ref[...] = t

    blk = pl.BlockSpec((None, tr, C), lambda i, s, pos_ref: (s, i, 0))
    return pl.pallas_call(
        body, name=name,
        grid_spec=pltpu.PrefetchScalarGridSpec(
            num_scalar_prefetch=1, grid=(nt, ns),
            in_specs=[pl.BlockSpec((None, tr, C), lambda i, s, pos_ref: (s, pos_ref[0] * nt + i, 0)), blk],
            out_specs=[blk, pl.BlockSpec((None, tr, C), lambda i, s, pos_ref: (pos_ref[1], i, 0))]),
        out_shape=[SDS((ns, hr, C), BF16)] * 2,
        compiler_params=_cp(2),
    )(pos_arr, g, r)


def _exchange_start(name, part, land):
    def body(part_ref, land_ref, send_sems, recv_sems, land_thru, token):
        x, y, c, me, chips, chip_idx = _mesh_pos()
        for j in range(N_PEER_CHIPS):
            pltpu.make_async_remote_copy(src_ref=part_ref.at[chip_idx[j]], dst_ref=land_ref.at[me],
                                         send_sem=send_sems.at[j], recv_sem=recv_sems.at[j],
                                         device_id=(*chips[j], c), device_id_type=MESH).start()
        token[...] = jnp.zeros_like(token)

    sems = pltpu.SemaphoreType.DMA((N_PEER_CHIPS,))
    return pl.pallas_call(
        body, name=name,
        in_specs=[HBM_SPEC, HBM_SPEC],
        out_specs=[SEM_SPEC, SEM_SPEC, HBM_SPEC, TOKEN_SPEC],
        out_shape=[sems, sems, pltpu.HBM(land.shape, land.dtype), TOKEN_SHAPE],
        input_output_aliases={1: 2},
        compiler_params=pltpu.CompilerParams(has_side_effects=SPLIT_COPY),
    )(_in_hbm(part), _in_hbm(land))


def _exchange_wait(name, part, land, send_sems, recv_sems, after):
    def body(part_ref, land_ref, send_sems, recv_sems, after_ref, land_out):
        x, y, c, me, chips, chip_idx = _mesh_pos()
        for j in range(N_PEER_CHIPS):
            cp = pltpu.make_async_remote_copy(src_ref=part_ref.at[chip_idx[j]], dst_ref=land_ref.at[chip_idx[j]],
                                              send_sem=send_sems.at[j], recv_sem=recv_sems.at[j],
                                              device_id=(*chips[j], c), device_id_type=MESH)
            cp.wait_send()
            cp.wait_recv()

    return pl.pallas_call(
        body, name=name,
        in_specs=[HBM_SPEC, HBM_SPEC, SEM_SPEC, SEM_SPEC, _hbm_spec()],
        out_specs=HBM_SPEC,
        out_shape=pltpu.HBM(land.shape, land.dtype),
        input_output_aliases={1: 0},
        compiler_params=pltpu.CompilerParams(has_side_effects=SPLIT_COPY),
    )(_in_hbm(part), land, send_sems, recv_sems, after)


class _GradReducer:
    def __init__(self, c_arr):
        self._c_arr = c_arr
        self._swapping = []
        self._exchanging = {}
        self._joining = {}
        self._tokens = []

    def begin(self, name, layer, g):
        tag = f"{name}_{layer}"
        ssem, rsem, land, token = _swap_start(f"rs_swap_start_{tag}", g)
        self._swapping.append((name, layer, g, ssem, rsem, land))
        self._tokens.append(token)

    def advance(self, after):
        for name, layer, g, ssem, rsem, land in self._swapping:
            tag = f"{name}_{layer}"
            theirs = _swap_wait(f"rs_swap_wait_{tag}", g, land, ssem, rsem, after)
            part, own = _add_my_half(f"rs_add_{tag}", g, theirs, self._c_arr)
            ssems, rsems, land2, token = _exchange_start(f"rs_xchg_start_{tag}", part, own)
            self._exchanging[(name, layer)] = (part, ssems, rsems, land2)
            self._tokens.append(token)
        self._swapping = []

    def deps(self):
        tokens, self._tokens = self._tokens, []
        return tokens

    def reduce(self, name, n_layers, after):
        buf = None
        for layer in range(n_layers):
            part, ssems, rsems, land = self._exchanging.pop((name, layer))
            tag = f"{name}_{layer}"
            landed = _exchange_wait(f"rs_xchg_wait_{tag}", part, land, ssems, rsems, after)
            buf = _sum_chips(f"rs_sum_{tag}", landed, self._c_arr, layer, n_layers, buf)
        ssem, rsem, buf, token = _join_start(f"rs_join_start_{name}", buf)
        self._joining[name] = (buf, ssem, rsem)
        return token

    def reduced(self, name, after):
        buf, ssem, rsem = self._joining.pop(name)
        return _join_wait(f"rs_join_wait_{name}", buf, ssem, rsem, after)


def _sum_chips(name, r, c_arr, layer, n_layers, prev):
    ns, H, C = r.shape
    tr = min(256, H)
    nt = H // tr

    def body(c_ref, r_ref, *rest):
        o_ref = rest[-1]
        o_ref[...] = ((r_ref[0].astype(F32) + r_ref[1].astype(F32)) + r_ref[2].astype(F32)) + r_ref[3].astype(F32)

    in_specs = [pl.BlockSpec((ns, tr, C), lambda i, c_ref: (0, i, 0))]
    args = [c_arr, r]
    aliases = {}
    if prev is not None:
        in_specs.append(_hbm_spec())
        args.append(prev)
        aliases = {2: 0}
    return pl.pallas_call(
        body, name=name,
        grid_spec=pltpu.PrefetchScalarGridSpec(
            num_scalar_prefetch=1, grid=(nt,), in_specs=in_specs,
            out_specs=pl.BlockSpec((None, tr, C), lambda i, c_ref: (layer, c_ref[0] * nt + i, 0))),
        out_shape=SDS((n_layers, 2 * H, C), F32),
        input_output_aliases=aliases,
        compiler_params=_cp(1),
    )(*args)


def _join_copy(buf_ref, send_sem, recv_sem):
    x, y, c, _, _, _ = _mesh_pos()
    hr = buf_ref.shape[1] // 2
    mine = buf_ref.at[:, pl.ds(c * hr, hr), :]
    theirs = buf_ref.at[:, pl.ds((1 - c) * hr, hr), :]
    send = pltpu.make_async_remote_copy(src_ref=mine, dst_ref=mine, send_sem=send_sem, recv_sem=recv_sem,
                                        device_id=(x, y, 1 - c), device_id_type=MESH)
    arrive = pltpu.make_async_remote_copy(src_ref=theirs, dst_ref=theirs, send_sem=send_sem, recv_sem=recv_sem,
                                          device_id=(x, y, 1 - c), device_id_type=MESH)
    return send, arrive


def _join_start(name, buf):
    def body(buf_ref, send_sem, recv_sem, buf_thru, token):
        _join_copy(buf_ref, send_sem, recv_sem)[0].start()
        token[...] = jnp.zeros_like(token)

    return pl.pallas_call(
        body, name=name,
        in_specs=[HBM_SPEC],
        out_specs=[SEM_SPEC, SEM_SPEC, HBM_SPEC, TOKEN_SPEC],
        out_shape=[pltpu.SemaphoreType.DMA(()), pltpu.SemaphoreType.DMA(()), pltpu.HBM(buf.shape, buf.dtype),
                   TOKEN_SHAPE],
        input_output_aliases={0: 2},
        compiler_params=pltpu.CompilerParams(has_side_effects=SPLIT_COPY),
    )(_in_hbm(buf))


def _join_wait(name, buf, send_sem, recv_sem, after):
    def body(buf_ref, send_sem, recv_sem, after_ref, buf_out):
        send, arrive = _join_copy(buf_ref, send_sem, recv_sem)
        send.wait_send()
        arrive.wait_recv()

    return pl.pallas_call(
        body, name=name,
        in_specs=[HBM_SPEC, SEM_SPEC, SEM_SPEC, _hbm_spec()],
        out_specs=HBM_SPEC,
        out_shape=pltpu.HBM(buf.shape, buf.dtype),
        input_output_aliases={0: 0},
        compiler_params=pltpu.CompilerParams(has_side_effects=SPLIT_COPY),
    )(buf, send_sem, recv_sem, after)


def _small_copy(k, buf_ref, land_ref, send_sems, recv_sems):
    x, y, c = lax.axis_index("x"), lax.axis_index("y"), lax.axis_index("c")
    me = 4 * x + 2 * y + c
    peer = (x ^ ((k >> 2) & 1), y ^ ((k >> 1) & 1), c ^ (k & 1))
    cp = pltpu.make_async_remote_copy(src_ref=buf_ref, dst_ref=land_ref.at[me], send_sem=send_sems.at[k - 1],
                                      recv_sem=recv_sems.at[k - 1], device_id=peer, device_id_type=MESH)
    return me, peer, cp


def _small_start(buf, deps):
    land = jnp.broadcast_to(buf[None], (N_DEV,) + buf.shape)
    n_dep = len(deps)

    def body(buf_ref, land_ref, *rest):
        send_sems, recv_sems, _, token = rest[n_dep:]
        for k in range(1, N_DEV):
            _small_copy(k, buf_ref, land_ref, send_sems, recv_sems)[2].start()
        token[...] = jnp.zeros_like(token)

    sems = pltpu.SemaphoreType.DMA((N_DEV - 1,))
    return pl.pallas_call(
        body, name="small_gather_start",
        in_specs=[HBM_SPEC, HBM_SPEC] + [_hbm_spec()] * n_dep,
        out_specs=[SEM_SPEC, SEM_SPEC, HBM_SPEC, TOKEN_SPEC],
        out_shape=[sems, sems, pltpu.HBM(land.shape, land.dtype), TOKEN_SHAPE],
        input_output_aliases={1: 2},
        compiler_params=pltpu.CompilerParams(has_side_effects=SPLIT_COPY),
    )(_in_hbm(buf), _in_hbm(land), *deps)


def _small_wait(buf, land, send_sems, recv_sems, after):
    def body(buf_ref, land_ref, send_sems, recv_sems, after_ref, land_out):
        for k in range(1, N_DEV):
            me, peer, cp = _small_copy(k, buf_ref, land_ref, send_sems, recv_sems)
            cp.wait_send()
            got = land_ref.at[me ^ k]
            pltpu.make_async_remote_copy(src_ref=got, dst_ref=got, send_sem=send_sems.at[k - 1],
                                         recv_sem=recv_sems.at[k - 1], device_id=peer,
                                         device_id_type=MESH).wait_recv()

    return pl.pallas_call(
        body, name="small_gather_wait",
        in_specs=[HBM_SPEC, HBM_SPEC, SEM_SPEC, SEM_SPEC, _hbm_spec()],
        out_specs=HBM_SPEC,
        out_shape=pltpu.HBM(land.shape, land.dtype),
        input_output_aliases={1: 0},
        compiler_params=pltpu.CompilerParams(has_side_effects=SPLIT_COPY),
    )(_in_hbm(buf), land, send_sems, recv_sems, after)


def _sum_devices(land):
    n, R, C = land.shape

    def body(land_ref, out_ref):
        acc = land_ref[0]
        for d in range(1, n):
            acc = acc + land_ref[d]
        out_ref[...] = acc

    return pl.pallas_call(
        body, name="small_sum",
        in_specs=[pl.BlockSpec(memory_space=pltpu.VMEM)],
        out_specs=pl.BlockSpec(memory_space=pltpu.VMEM),
        out_shape=SDS((R, C), land.dtype),
        compiler_params=pltpu.CompilerParams(vmem_limit_bytes=V7X_VMEM_LIMIT),
    )(land)


def _pack_rows(vectors):
    flat = jnp.concatenate([v.reshape(-1) for v in vectors])
    n = flat.shape[0]
    padded = -(-n // 1024) * 1024
    return jnp.pad(flat, (0, padded - n)).reshape(padded // 128, 128)


def _unpack_rows(buf, shapes):
    flat = buf.reshape(-1)
    out, off = [], 0
    for s in shapes:
        n = 1
        for dim in s:
            n *= dim
        out.append(flat[off:off + n].reshape(s))
        off += n
    return out


def _layer_forward(l, x, prm, wg):
    S, D = x.shape
    h = _rmsnorm_fwd(f"attn_norm_{l}", x, prm["attn_norm"][l])
    w_in = wg.get("w_in", l, h, prefetch_next=l > 0)
    ns_in = w_in.shape[-1]
    tmi = min(1024, S)
    p = _matmul(
        f"in_proj_{l}", h, w_in, (S, N_CHIPS * ns_in), F32, grid=(S // tmi, N_CHIPS, 1),
        a_spec=pl.BlockSpec((tmi, D), lambda i, j, k: (i, 0)),
        b_spec=pl.BlockSpec((None, D, ns_in), lambda i, j, k: (j, 0, 0)),
        o_spec=pl.BlockSpec((tmi, ns_in), lambda i, j, k: (i, j)),
        contract=(1, 0), acc_shape=(tmi, ns_in), deps=wg.deps())
    if l == 0:
        wg.prefetch_after("w_in", l, p)
    y_a = _sgu_fwd(f"sgu_fwd_{l}", p, prm["sgu_wt"][l], prm["sgu_bb"][l])
    y_b = _conv_fwd(f"conv_fwd_{l}", p, prm["conv_w"][l])
    os, lses = [], []
    for g in range(N_PATTERNS):
        o_g, lse_g = _attn_fwd(f"attn_fwd_{l}_{g}", p, g, prm["q_gain"][l], prm["k_gain"][l], prm["bd"])
        os.append(o_g)
        lses.append(lse_g)
    y_c = _mix_fwd(f"mix_fwd_{l}", os, lses)
    ycat = jnp.concatenate([y_a, y_b, y_c], axis=1)
    tmb, tnb = min(1024, S), min(1024, D)
    w_out = wg.get("w_out", l, ycat)
    kq = N_CHIPS * w_out.shape[1]
    tmo, tno = min(512, S), D
    x1 = _matmul(
        f"out_proj_{l}", ycat, w_out.reshape(kq, D), (S, D), F32, grid=(S // tmo, D // tno, 1),
        a_spec=pl.BlockSpec((tmo, kq), lambda i, j, k: (i, 0)),
        b_spec=pl.BlockSpec((kq, tno), lambda i, j, k: (0, j)),
        o_spec=pl.BlockSpec((tmo, tno), lambda i, j, k: (i, j)),
        contract=(1, 0), acc_shape=(tmo, tno),
        extras=(x,), extra_specs=(pl.BlockSpec((tmo, tno), lambda i, j, k: (i, j)),),
        epi=lambda r, res: r + res, deps=wg.deps())
    w_mlp_in = wg.get("w_mlp_in", l, x1)
    h2 = _rmsnorm_fwd(f"mlp_norm_{l}", x1, prm["mlp_norm"][l])
    nf4 = w_mlp_in.shape[-1]
    r = _matmul(
        f"mlp_in_{l}", h2, w_mlp_in, (S, N_CHIPS * nf4), BF16, grid=(S // tmb, N_CHIPS, 1),
        a_spec=pl.BlockSpec((tmb, D), lambda i, j, k: (i, 0)),
        b_spec=pl.BlockSpec((None, D, nf4), lambda i, j, k: (j, 0, 0)),
        o_spec=pl.BlockSpec((tmb, nf4), lambda i, j, k: (i, j)),
        contract=(1, 0), acc_shape=(tmb, nf4), epi=_relu, deps=wg.deps())
    w_mlp_out = wg.get("w_mlp_out", l, r)
    dff4 = w_mlp_out.shape[1]
    tk = min(2048, dff4)
    kpc = dff4 // tk
    x2 = _matmul(
        f"mlp_out_{l}", r, w_mlp_out, (S, D), F32, grid=(S // tmb, D // tnb, N_CHIPS * kpc),
        a_spec=pl.BlockSpec((tmb, tk), lambda i, j, k: (i, k)),
        b_spec=pl.BlockSpec((None, tk, tnb), lambda i, j, k: (k // kpc, k % kpc, j)),
        o_spec=pl.BlockSpec((tmb, tnb), lambda i, j, k: (i, j)),
        contract=(1, 0), acc_shape=(tmb, tnb), a_pre=_square,
        extras=(x1,), extra_specs=(pl.BlockSpec((tmb, tnb), lambda i, j, k: (i, j)),),
        epi=lambda acc, res: acc + res, deps=wg.deps())
    saved = dict(x=x, p=p, h=h, os=os, lses=lses, ycat=ycat, x1=x1, r=r, h2=h2)
    return x2, saved


def _layer_backward(l, dx2, dx2b, sv, prm, wg, sink):
    S, D = dx2.shape
    w_in, w_out = wg.get("w_in", l), wg.get("w_out", l)
    w_mlp_in, w_mlp_out = wg.get("w_mlp_in", l), wg.get("w_mlp_out", l)
    dff4 = w_mlp_in.shape[-1]
    dff = N_CHIPS * dff4

    tmb, tnb = min(1024, S), min(1024, D)
    da = _matmul(
        f"mlp_out_bwd_{l}", dx2b, w_mlp_out, (S, dff), BF16, grid=(S // tmb, N_CHIPS, 1),
        a_spec=pl.BlockSpec((tmb, D), lambda i, j, k: (i, 0)),
        b_spec=pl.BlockSpec((None, dff4, D), lambda i, j, k: (j, 0, 0)),
        o_spec=pl.BlockSpec((tmb, dff4), lambda i, j, k: (i, j)),
        contract=(1, 1), acc_shape=(tmb, dff4),
        extras=(sv["r"],), extra_specs=(pl.BlockSpec((tmb, dff4), lambda i, j, k: (i, j)),),
        epi=lambda acc, r: acc * (2.0 * r.astype(F32)), deps=sink.deps())
    tmw = min(1024, dff4)
    mpc = dff4 // tmw
    g_w2 = _matmul(
        f"mlp_out_dw_{l}", sv["r"], dx2b, (N_CHIPS, dff4, D), F32, grid=(N_CHIPS * mpc, D // tnb, 1),
        a_spec=pl.BlockSpec((S, tmw), lambda i, j, k: (0, i)),
        b_spec=pl.BlockSpec((S, tnb), lambda i, j, k: (0, j)),
        o_spec=pl.BlockSpec((None, tmw, tnb), lambda i, j, k: (i // mpc, i % mpc, j)),
        contract=(0, 0), acc_shape=(tmw, tnb), a_pre=_square)
    sink.begin("w_mlp_out", l, g_w2)
    tnx = D
    dh2 = _matmul(
        f"mlp_in_bwd_{l}", da, w_mlp_in, (S, D), F32, grid=(S // tmb, D // tnx, N_CHIPS),
        a_spec=pl.BlockSpec((tmb, dff4), lambda i, j, k: (i, k)),
        b_spec=pl.BlockSpec((None, tnx, dff4), lambda i, j, k: (k, j, 0)),
        o_spec=pl.BlockSpec((tmb, tnx), lambda i, j, k: (i, j)),
        contract=(1, 1), acc_shape=(tmb, tnx), deps=sink.deps())
    sink.advance(dh2)
    tmd = min(1024, D)
    nd = D // tmd
    tnf = min(1024, dff4)
    nf = dff4 // tnf
    g_w1 = _matmul(
        f"mlp_in_dw_{l}", sv["h2"], da, (N_CHIPS, D, dff4), F32, grid=(N_CHIPS * nd, nf, 1),
        a_spec=pl.BlockSpec((S, tmd), lambda i, j, k: (0, i % nd)),
        b_spec=pl.BlockSpec((S, tnf), lambda i, j, k: (0, (i // nd) * nf + j)),
        o_spec=pl.BlockSpec((None, tmd, tnf), lambda i, j, k: (i // nd, i % nd, j)),
        contract=(0, 0), acc_shape=(tmd, tnf))
    sink.begin("w_mlp_in", l, g_w1)
    dx1, dx1b, g_mlp_norm = _rmsnorm_bwd(f"mlp_norm_bwd_{l}", dh2, sv["x1"], prm["mlp_norm"][l], dx2,
                                         deps=sink.deps())

    rq = w_out.shape[1]
    kq = N_CHIPS * rq
    dycat = _matmul(
        f"out_proj_bwd_{l}", dx1b, w_out.reshape(kq, D), (S, kq), F32, grid=(S // tmb, 1, 1),
        a_spec=pl.BlockSpec((tmb, D), lambda i, j, k: (i, 0)),
        b_spec=pl.BlockSpec((kq, D), lambda i, j, k: (0, 0)),
        o_spec=pl.BlockSpec((tmb, kq), lambda i, j, k: (i, 0)),
        contract=(1, 1), acc_shape=(tmb, kq))
    sink.advance(dycat)
    g_wout = _matmul(
        f"out_proj_dw_{l}", sv["ycat"], dx1b, (N_CHIPS, rq, D), F32, grid=(N_CHIPS, 1, 1),
        a_spec=pl.BlockSpec((S, rq), lambda i, j, k: (0, i)),
        b_spec=pl.BlockSpec((S, D), lambda i, j, k: (0, 0)),
        o_spec=pl.BlockSpec((None, rq, D), lambda i, j, k: (i, 0, 0)),
        contract=(0, 0), acc_shape=(rq, D))
    sink.begin("w_out", l, g_wout)

    p = sv["p"]
    du, dv_a, g_sgu_w, db_lanes = _sgu_bwd(f"sgu_bwd_{l}", p, dycat, prm["sgu_wt"][l], prm["sgu_wtt"][l],
                                           prm["sgu_bb"][l])
    g_sgu_b = db_lanes[:, :A_HEADS].T
    db, dc, dxb, g_conv = _conv_bwd(f"conv_bwd_{l}", p, dycat, prm["conv_w"][l])
    do3, c3 = _mix_bwd(f"mix_bwd_{l}", sv["os"], sv["lses"], dycat, prm["bd"])
    dqs, dks, dvs, dgqs, dgks = [], [], [], [], []
    for g in range(N_PATTERNS):
        dq, dk, dv, dgq, dgk = _attn_bwd(f"attn_bwd_{l}_{g}", p, g, sv["lses"][g], do3, c3,
                                         prm["q_gain"][l], prm["k_gain"][l], prm["bd"])
        dqs.append(dq)
        dks.append(dk)
        dvs.append(dv)
        dgqs.append(dgq)
        dgks.append(dgk)
    g_q = jnp.concatenate(dgqs, axis=1).reshape(N_PATTERNS * PW // HEAD_DIM, HEAD_DIM).sum(axis=0)
    g_k = jnp.concatenate(dgks, axis=1).reshape(N_PATTERNS * PW // HEAD_DIM, HEAD_DIM).sum(axis=0)
    dp = jnp.concatenate([du, dv_a, db, dc, dxb] + [t.astype(BF16) for t in dqs + dks + dvs], axis=1)

    ns_in = w_in.shape[-1]
    tmh = min(512, D)
    nh = D // tmh
    g_win = _matmul(
        f"in_proj_dw_{l}", sv["h"], dp, (N_CHIPS, D, ns_in), F32, grid=(N_CHIPS * nh, 1, 1),
        a_spec=pl.BlockSpec((S, tmh), lambda i, j, k: (0, i % nh)),
        b_spec=pl.BlockSpec((S, ns_in), lambda i, j, k: (0, i // nh)),
        o_spec=pl.BlockSpec((None, tmh, ns_in), lambda i, j, k: (i // nh, i % nh, 0)),
        contract=(0, 0), acc_shape=(tmh, ns_in))
    sink.begin("w_in", l, g_win)
    dh = _matmul(
        f"in_proj_bwd_{l}", dp, w_in, (S, D), F32, grid=(S // tmb, D // tnx, N_CHIPS),
        a_spec=pl.BlockSpec((tmb, ns_in), lambda i, j, k: (i, k)),
        b_spec=pl.BlockSpec((None, tnx, ns_in), lambda i, j, k: (k, j, 0)),
        o_spec=pl.BlockSpec((tmb, tnx), lambda i, j, k: (i, j)),
        contract=(1, 1), acc_shape=(tmb, tnx), deps=sink.deps())
    sink.advance(dh)
    dx0, dx0b, g_attn_norm = _rmsnorm_bwd(f"attn_norm_bwd_{l}", dh, sv["x"], prm["attn_norm"][l], dx1,
                                          deps=sink.deps())

    big = dict(w_in=g_win, w_out=g_wout, w_mlp_in=g_w1, w_mlp_out=g_w2)
    small = dict(attn_norm=g_attn_norm.reshape(-1), sgu_w=g_sgu_w, sgu_b=g_sgu_b, conv_w=g_conv,
                 q_norm=g_q, k_norm=g_k, mlp_norm=g_mlp_norm.reshape(-1))
    return dx0, dx0b, big, small


BIG = ("w_in", "w_out", "w_mlp_in", "w_mlp_out")
SMALL_REPLICATED = ("attn_norm", "sgu_w", "sgu_b", "q_norm", "k_norm", "mlp_norm")


def _local_step(x, target, prm, wg, n_layers, sink):
    saved = []
    h = x
    for l in range(n_layers):
        h, sv = _layer_forward(l, h, prm, wg)
        saved.append(sv)
    dy, dyb, colsq = _loss_kernel(h, target)
    loss = 0.5 * jnp.sum(colsq) / x.shape[1]
    bigs, smalls = [None] * n_layers, [None] * n_layers
    for l in reversed(range(n_layers)):
        dy, dyb, bigs[l], smalls[l] = _layer_backward(l, dy, dyb, saved[l], prm, wg, sink)
    return loss, dy, bigs, smalls


def _prepare_params(attn_norm, sgu_w, sgu_b, conv_full, q_norm, k_norm, mlp_norm):
    n_layers = attn_norm.shape[0]
    tri = jnp.tril(sgu_w)
    idx = jnp.arange(PW)
    bd = (idx[:, None] // HEAD_DIM == idx[None, :] // HEAD_DIM).astype(BF16)
    return dict(
        attn_norm=[attn_norm[l][None, :] for l in range(n_layers)],
        mlp_norm=[mlp_norm[l][None, :] for l in range(n_layers)],
        sgu_wt=[tri[l].astype(BF16) for l in range(n_layers)],
        sgu_wtt=[tri[l].transpose(0, 2, 1).astype(BF16) for l in range(n_layers)],
        sgu_bb=[jnp.repeat(sgu_b[l].T, HEAD_DIM, axis=1) for l in range(n_layers)],
        conv_w=[conv_full[l] for l in range(n_layers)],
        q_gain=[jnp.tile(q_norm[l], PW // HEAD_DIM)[None, :] for l in range(n_layers)],
        k_gain=[jnp.tile(k_norm[l], PW // HEAD_DIM)[None, :] for l in range(n_layers)],
        bd=bd,
    )


def kernel(x, attn_norm, w_in, sgu_w, sgu_b, conv_w, q_norm, k_norm, w_out, mlp_norm, w_mlp_in, w_mlp_out, loss_target, m_attn_norm, m_w_in, m_sgu_w, m_sgu_b, m_conv_w, m_q_norm, m_k_norm, m_w_out, m_mlp_norm, m_w_mlp_in, m_w_mlp_out, v_attn_norm, v_w_in, v_sgu_w, v_sgu_b, v_conv_w, v_q_norm, v_k_norm, v_w_out, v_mlp_norm, v_w_mlp_in, v_w_mlp_out):
    n_layers = attn_norm.shape[0]
    weights = dict(attn_norm=attn_norm, w_in=w_in, sgu_w=sgu_w, sgu_b=sgu_b, conv_w=conv_w, q_norm=q_norm,
                   k_norm=k_norm, w_out=w_out, mlp_norm=mlp_norm, w_mlp_in=w_mlp_in, w_mlp_out=w_mlp_out)
    mom_m = dict(attn_norm=m_attn_norm, w_in=m_w_in, sgu_w=m_sgu_w, sgu_b=m_sgu_b, conv_w=m_conv_w,
                 q_norm=m_q_norm, k_norm=m_k_norm, w_out=m_w_out, mlp_norm=m_mlp_norm, w_mlp_in=m_w_mlp_in,
                 w_mlp_out=m_w_mlp_out)
    mom_v = dict(attn_norm=v_attn_norm, w_in=v_w_in, sgu_w=v_sgu_w, sgu_b=v_sgu_b, conv_w=v_conv_w,
                 q_norm=v_q_norm, k_norm=v_k_norm, w_out=v_w_out, mlp_norm=v_mlp_norm, w_mlp_in=v_w_mlp_in,
                 w_mlp_out=v_w_mlp_out)
    order = ("attn_norm", "w_in", "sgu_w", "sgu_b", "conv_w", "q_norm", "k_norm", "w_out", "mlp_norm",
             "w_mlp_in", "w_mlp_out")
    chip = 2 * lax.axis_index("x") + lax.axis_index("y")
    c_arr = jnp.stack([lax.axis_index("c"), chip]).astype(jnp.int32)

    conv_cols = conv_w.shape[-1]
    chip_arr = chip.astype(jnp.int32).reshape(1)
    conv_pack = jnp.pad(conv_w.reshape(-1), (0, 2048 - conv_w.size)).reshape(1, 16, 128)
    wg = _GatheredWeights()
    wg.start([("conv_w", 0), ("w_in", 0)],
             [_place_shard("place_conv_w", conv_pack, 0, chip_arr, F32),
              _place_shard("place_w_in_0", weights["w_in"], 0, chip_arr, BF16)])
    keys = [(n, l) for l in range(n_layers) for n in BIG if (n, l) != ("w_in", 0)]
    first = wg.deps()
    wg.start(keys, [_place_shard(f"place_{n}_{l}", weights[n], l, chip_arr, BF16, deps=first) for n, l in keys])
    conv_full = wg.get("conv_w", 0, wg.deps()[-1]).reshape(N_CHIPS, 2048)[:, :conv_w.size].reshape(N_CHIPS, n_layers, 3, conv_cols)
    conv_full = conv_full.transpose(1, 2, 0, 3).reshape(n_layers, 3, N_CHIPS * conv_cols)
    prm = _prepare_params(attn_norm, sgu_w, sgu_b, conv_full, q_norm, k_norm, mlp_norm)

    sink = _GradReducer(c_arr)
    loss_local, grad_x, _, smalls = _local_step(x[0], loss_target[0], prm, wg, n_layers, sink)
    loss = lax.psum(loss_local, ("x", "y", "c"))

    small_names = SMALL_REPLICATED + ("conv_w",)
    small_shapes = [(n_layers,) + tuple(smalls[0][n].shape) for n in small_names]
    packed = _pack_rows([jnp.stack([smalls[l][n] for l in range(n_layers)]) for n in small_names])
    small_send, small_recv, small_land, small_token = _small_start(packed, sink.deps())

    grads, delta, new_m, new_v = {}, {}, {}, {}

    def update(n, after):
        shp = weights[n].shape
        two_d = (shp[0] * shp[1], shp[2])
        d, nm, nv, g = _adamw(f"adamw_{n}", weights[n].reshape(two_d), sink.reduced(n, after).reshape(two_d),
                              mom_m[n].reshape(two_d), mom_v[n].reshape(two_d))
        grads[n], delta[n], new_m[n], new_v[n] = g.reshape(shp), d.reshape(shp), nm.reshape(shp), nv.reshape(shp)

    token = small_token
    for n in ("w_mlp_out", "w_mlp_in", "w_out"):
        token = sink.reduce(n, n_layers, token)
    update("w_mlp_out", token)
    token = sink.reduce("w_in", n_layers, delta["w_mlp_out"])
    update("w_mlp_in", token)
    update("w_out", delta["w_mlp_in"])
    update("w_in", delta["w_out"])
    small_land = _small_wait(packed, small_land, small_send, small_recv, delta["w_in"])
    grads.update(zip(small_names, _unpack_rows(_sum_devices(small_land), small_shapes)))
    grads["conv_w"] = lax.dynamic_slice_in_dim(grads["conv_w"], chip * conv_cols, conv_cols, axis=2)
    smalls_all = SMALL_REPLICATED + ("conv_w",)
    shapes = [weights[n].shape for n in smalls_all]
    d, nm, nv, _ = _adamw("adamw_small",
                          _pack_rows([weights[n] for n in smalls_all]), _pack_rows([grads[n] for n in smalls_all]),
                          _pack_rows([mom_m[n] for n in smalls_all]), _pack_rows([mom_v[n] for n in smalls_all]))
    for n, dd, mm, vv in zip(smalls_all, _unpack_rows(d, shapes), _unpack_rows(nm, shapes), _unpack_rows(nv, shapes)):
        delta[n], new_m[n], new_v[n] = dd, mm, vv

    return (loss, grad_x[None], *[grads[n] for n in order], *[delta[n] for n in order],
            *[new_m[n] for n in order], *[new_v[n] for n in order])
```

```python
import jax
import jax.numpy as jnp
from jax import lax
from jax.experimental import pallas as pl
from jax.experimental.pallas import tpu as pltpu

F32 = jnp.float32
BF16 = jnp.bfloat16
SDS = jax.ShapeDtypeStruct

EPS = 1e-6
HEAD_DIM = 64
A_HEADS = 8
A_WIDTH = 512
CHUNK = 128
B_WIDTH = 768
C_WIDTH = 768
N_PATTERNS = 3
PATTERN_DILATION = (1, 4, 16)
PW = 256
D_IN_PROJ = 5632
OFF_AU, OFF_AV, OFF_BB, OFF_BC, OFF_BX, OFF_Q, OFF_K, OFF_V = 0, 512, 1024, 1792, 2560, 3328, 4096, 4864
N_CHIPS = 4
N_DEV = 8
BLK = 128

ADAM_LR, ADAM_B1, ADAM_B2, ADAM_EPS, ADAM_WD, ADAM_STEP = 0.001, 0.9, 0.999, 1e-08, 0.01, 10

V7X_VMEM_LIMIT = 56 * 1024 * 1024
MESH = pl.DeviceIdType.MESH
NEG = -1e30


def _cp(n_axes):
    return pltpu.CompilerParams(dimension_semantics=("arbitrary",) * n_axes, vmem_limit_bytes=V7X_VMEM_LIMIT)


def _hbm_spec():
    return pl.BlockSpec(memory_space=pl.ANY)


def _relu(t):
    return jnp.maximum(t, 0.0)


def _square(t):
    return t * t


def _matmul(name, a, b, out_shape, out_dtype, *, grid, a_spec, b_spec, o_spec, contract, acc_shape,
            extras=(), extra_specs=(), a_pre=None, epi=None, deps=()):
    nk = grid[2]
    n_ex = len(extras)
    n_dep = len(deps)
    dims = (((contract[0],), (contract[1],)), ((), ()))

    def product(a_ref, b_ref):
        av = a_ref[...] if a_pre is None else a_pre(a_ref[...])
        return lax.dot_general(av, b_ref[...], dims, preferred_element_type=F32)

    def finish(r, ex, o_ref):
        if epi is not None:
            r = epi(r, *[e[...] for e in ex])
        o_ref[...] = r.astype(o_ref.dtype)

    def body_single(a_ref, b_ref, *rest):
        finish(product(a_ref, b_ref), rest[:n_ex], rest[n_ex + n_dep])

    def body(a_ref, b_ref, *rest):
        ex = rest[:n_ex]
        o_ref = rest[n_ex + n_dep]
        acc_ref = rest[n_ex + n_dep + 1]
        k = pl.program_id(2)

        @pl.when(k == 0)
        def _():
            acc_ref[...] = product(a_ref, b_ref)

        @pl.when((k > 0) & (k < nk - 1))
        def _():
            acc_ref[...] += product(a_ref, b_ref)

        @pl.when(k == nk - 1)
        def _():
            finish(acc_ref[...] + product(a_ref, b_ref), ex, o_ref)

    return pl.pallas_call(
        body_single if nk == 1 else body, name=name, grid=grid,
        in_specs=[a_spec, b_spec, *extra_specs] + [_hbm_spec()] * n_dep,
        out_specs=o_spec,
        out_shape=SDS(out_shape, out_dtype),
        scratch_shapes=[] if nk == 1 else [pltpu.VMEM(acc_shape, F32)],
        compiler_params=_cp(3),
    )(a, b, *extras, *deps)


def _loss_kernel(y, t):
    S, D = y.shape
    tm = min(256, S)

    def body(y_ref, t_ref, dy_ref, dyb_ref, l_ref):
        @pl.when(pl.program_id(0) == 0)
        def _():
            l_ref[...] = jnp.zeros_like(l_ref)
        e = y_ref[...] - t_ref[...]
        l_ref[...] += jnp.sum(e * e, axis=0, keepdims=True)
        dy = e * (1.0 / D)
        dy_ref[...] = dy
        dyb_ref[...] = dy.astype(BF16)

    row = pl.BlockSpec((tm, D), lambda i: (i, 0))
    return pl.pallas_call(
        body, name="loss_head", grid=(S // tm,),
        in_specs=[row, row],
        out_specs=[row, row, pl.BlockSpec((1, D), lambda i: (0, 0))],
        out_shape=[SDS((S, D), F32), SDS((S, D), BF16), SDS((1, D), F32)],
        compiler_params=_cp(1),
    )(y, t)


def _rmsnorm_fwd(name, x, g):
    S, D = x.shape
    tm = min(512, S)

    def body(x_ref, g_ref, h_ref):
        xv = x_ref[...]
        y = xv * lax.rsqrt(jnp.mean(xv * xv, axis=-1, keepdims=True) + EPS) * g_ref[...]
        h_ref[...] = y.astype(h_ref.dtype)

    row = pl.BlockSpec((tm, D), lambda i: (i, 0))
    return pl.pallas_call(
        body, name=name, grid=(S // tm,),
        in_specs=[row, pl.BlockSpec((1, D), lambda i: (0, 0))],
        out_specs=row,
        out_shape=SDS((S, D), BF16),
        compiler_params=_cp(1),
    )(x, g)


def _rmsnorm_bwd(name, dh, x, g, dres, deps=()):
    S, D = x.shape
    tm = min(256, S)
    n_dep = len(deps)

    def body(dh_ref, x_ref, g_ref, dres_ref, *rest):
        dx_ref, dxb_ref, dg_ref = rest[n_dep:]
        @pl.when(pl.program_id(0) == 0)
        def _():
            dg_ref[...] = jnp.zeros_like(dg_ref)
        xv = x_ref[...]
        dhv = dh_ref[...]
        rstd = lax.rsqrt(jnp.mean(xv * xv, axis=-1, keepdims=True) + EPS)
        xhat = xv * rstd
        dg_ref[...] += jnp.sum(dhv * xhat, axis=0, keepdims=True)
        dxn = dhv * g_ref[...]
        dx = dres_ref[...] + rstd * (dxn - xhat * jnp.mean(dxn * xhat, axis=-1, keepdims=True))
        dx_ref[...] = dx
        dxb_ref[...] = dx.astype(BF16)

    row = pl.BlockSpec((tm, D), lambda i: (i, 0))
    vec = pl.BlockSpec((1, D), lambda i: (0, 0))
    return pl.pallas_call(
        body, name=name, grid=(S // tm,),
        in_specs=[row, row, vec, row] + [_hbm_spec()] * n_dep,
        out_specs=[row, row, vec],
        out_shape=[SDS((S, D), F32), SDS((S, D), BF16), SDS((1, D), F32)],
        compiler_params=_cp(1),
    )(dh, x, g, dres, *deps)


def _adamw(name, w, g, m, v):
    R, C = w.shape
    tr = 256 if R % 256 == 0 else R
    c1 = 1.0 - ADAM_B1 ** ADAM_STEP
    c2 = 1.0 - ADAM_B2 ** ADAM_STEP

    def body(w_ref, g_ref, m_ref, v_ref, d_ref, nm_ref, nv_ref, g_out_ref):
        gv = g_ref[...]
        nm = ADAM_B1 * m_ref[...] + (1.0 - ADAM_B1) * gv
        nv = ADAM_B2 * v_ref[...] + (1.0 - ADAM_B2) * (gv * gv)
        m_hat = nm / c1
        v_hat = nv / c2
        d_ref[...] = -ADAM_LR * (m_hat / (jnp.sqrt(v_hat) + ADAM_EPS) + ADAM_WD * w_ref[...])
        nm_ref[...] = nm
        nv_ref[...] = nv
        g_out_ref[...] = gv

    blk = pl.BlockSpec((tr, C), lambda i: (i, 0))
    return pl.pallas_call(
        body, name=name, grid=(R // tr,),
        in_specs=[blk] * 4, out_specs=[blk] * 4,
        out_shape=[SDS((R, C), F32)] * 4,
        compiler_params=_cp(1),
    )(w, g, m, v)


SGU_STEP_ROWS = 1024


def _pair_select(lane, lo, hi):
    return jnp.where(lane < HEAD_DIM, lo, hi)


def _sgu_fwd(name, p, wt, bb):
    S = p.shape[0]

    rows = min(SGU_STEP_ROWS, S)

    def body(u_ref, v_ref, wt_ref, bb_ref, o_ref):
        lane = lax.broadcasted_iota(jnp.int32, (CHUNK, 128), 1)
        for ci in range(rows // CHUNK):
            rs = slice(CHUNK * ci, CHUNK * (ci + 1))
            for pp in range(A_HEADS // 2):
                cs = slice(128 * pp, 128 * (pp + 1))
                vb = v_ref[rs, cs].astype(BF16)
                mixed = _pair_select(lane,
                                     jnp.dot(wt_ref[2 * pp], vb, preferred_element_type=F32),
                                     jnp.dot(wt_ref[2 * pp + 1], vb, preferred_element_type=F32)) + bb_ref[:, cs]
                o_ref[rs, cs] = (u_ref[rs, cs] * mixed).astype(o_ref.dtype)

    return pl.pallas_call(
        body, name=name, grid=(S // rows,),
        in_specs=[pl.BlockSpec((rows, A_WIDTH), lambda c: (c, OFF_AU // A_WIDTH)),
                  pl.BlockSpec((rows, A_WIDTH), lambda c: (c, OFF_AV // A_WIDTH)),
                  pl.BlockSpec((A_HEADS, CHUNK, CHUNK), lambda c: (0, 0, 0)),
                  pl.BlockSpec((CHUNK, A_WIDTH), lambda c: (0, 0))],
        out_specs=pl.BlockSpec((rows, A_WIDTH), lambda c: (c, 0)),
        out_shape=SDS((S, A_WIDTH), BF16),
        compiler_params=_cp(1),
    )(p, p, wt, bb)


def _sgu_bwd(name, p, dycat, wt, wtt, bb):
    S = p.shape[0]
    rows = min(SGU_STEP_ROWS, S)

    def body(u_ref, v_ref, dy_ref, wt_ref, wtt_ref, bb_ref, du_ref, dv_ref, dw_ref, db_ref, dbacc_ref):
        c = pl.program_id(0)

        @pl.when(c == 0)
        def _():
            dw_ref[...] = jnp.zeros_like(dw_ref)
            dbacc_ref[...] = jnp.zeros_like(dbacc_ref)

        lane = lax.broadcasted_iota(jnp.int32, (CHUNK, 128), 1)
        row = lax.broadcasted_iota(jnp.int32, (CHUNK, 128), 0)
        causal = row >= lane
        nt = (((1,), (1,)), ((), ()))
        for pp in range(A_HEADS // 2):
            cs = slice(128 * pp, 128 * (pp + 1))
            dw_lo = jnp.zeros((CHUNK, CHUNK), F32)
            dw_hi = jnp.zeros((CHUNK, CHUNK), F32)
            dm_sum = jnp.zeros((CHUNK, 128), F32)
            for ci in range(rows // CHUNK):
                rs = slice(CHUNK * ci, CHUNK * (ci + 1))
                vb = v_ref[rs, cs].astype(BF16)
                dy = dy_ref[rs, cs]
                mixed = _pair_select(lane,
                                     jnp.dot(wt_ref[2 * pp], vb, preferred_element_type=F32),
                                     jnp.dot(wt_ref[2 * pp + 1], vb, preferred_element_type=F32)) + bb_ref[:, cs]
                du_ref[rs, cs] = (dy * mixed).astype(du_ref.dtype)
                dm = dy * u_ref[rs, cs]
                dmb = dm.astype(BF16)
                dv = _pair_select(lane,
                                  jnp.dot(wtt_ref[2 * pp], dmb, preferred_element_type=F32),
                                  jnp.dot(wtt_ref[2 * pp + 1], dmb, preferred_element_type=F32))
                dv_ref[rs, cs] = dv.astype(dv_ref.dtype)
                dm_sum += dm
                dm_lo = jnp.where(lane < HEAD_DIM, dm, 0.0).astype(BF16)
                dm_hi = jnp.where(lane >= HEAD_DIM, dm, 0.0).astype(BF16)
                dw_lo += lax.dot_general(dm_lo, vb, nt, preferred_element_type=F32)
                dw_hi += lax.dot_general(dm_hi, vb, nt, preferred_element_type=F32)
            dbacc_ref[:, cs] += dm_sum
            dw_ref[2 * pp] += jnp.where(causal, dw_lo, 0.0)
            dw_ref[2 * pp + 1] += jnp.where(causal, dw_hi, 0.0)

        @pl.when(c == S // rows - 1)
        def _():
            out = jnp.zeros((CHUNK, 128), F32)
            for pp in range(A_HEADS // 2):
                acc = dbacc_ref[:, 128 * pp:128 * (pp + 1)]
                s_lo = jnp.sum(jnp.where(lane < HEAD_DIM, acc, 0.0), axis=1, keepdims=True)
                s_hi = jnp.sum(jnp.where(lane >= HEAD_DIM, acc, 0.0), axis=1, keepdims=True)
                out = jnp.where(lane == 2 * pp, s_lo, out)
                out = jnp.where(lane == 2 * pp + 1, s_hi, out)
            db_ref[...] = out

    chunk = lambda col: pl.BlockSpec((rows, A_WIDTH), lambda c: (c, col))
    wspec = pl.BlockSpec((A_HEADS, CHUNK, CHUNK), lambda c: (0, 0, 0))
    return pl.pallas_call(
        body, name=name, grid=(S // rows,),
        in_specs=[chunk(OFF_AU // A_WIDTH), chunk(OFF_AV // A_WIDTH), chunk(0), wspec, wspec,
                  pl.BlockSpec((CHUNK, A_WIDTH), lambda c: (0, 0))],
        out_specs=[chunk(0), chunk(0), wspec, pl.BlockSpec((CHUNK, 128), lambda c: (0, 0))],
        out_shape=[SDS((S, A_WIDTH), BF16), SDS((S, A_WIDTH), BF16),
                   SDS((A_HEADS, CHUNK, CHUNK), F32), SDS((CHUNK, 128), F32)],
        scratch_shapes=[pltpu.VMEM((CHUNK, A_WIDTH), F32)],
        compiler_params=_cp(1),
    )(p, p, dycat, wt, wtt, bb)


CONV_HALO = 8
CONV_COLS = 256
CONV_ROWS = 2048


def _shift_down(a, halo, k):
    T = a.shape[0]
    row = lax.broadcasted_iota(jnp.int32, a.shape, 0)
    out = pltpu.roll(a, k, 0)
    for r in range(k):
        out = jnp.where(row == r, halo[CONV_HALO - k + r:CONV_HALO - k + r + 1, :], out)
    return out


def _shift_up(a, halo, k):
    T = a.shape[0]
    row = lax.broadcasted_iota(jnp.int32, a.shape, 0)
    out = pltpu.roll(a, T - k, 0)
    for r in range(k):
        out = jnp.where(row == T - k + r, halo[r:r + 1, :], out)
    return out


def _conv_specs(S, T):
    hb = T // CONV_HALO
    last = S // CONV_HALO - 1
    tile = lambda col0: pl.BlockSpec((T, CONV_COLS), lambda j, i: (i, col0 + j))
    prev = lambda col0: pl.BlockSpec((CONV_HALO, CONV_COLS), lambda j, i: (jnp.maximum(i * hb - 1, 0), col0 + j))
    nxt = lambda col0: pl.BlockSpec((CONV_HALO, CONV_COLS), lambda j, i: (jnp.minimum((i + 1) * hb, last), col0 + j))
    return tile, prev, nxt


def _conv_fwd(name, p, w):
    S = p.shape[0]
    T = min(CONV_ROWS, S)
    tile, prev, _ = _conv_specs(S, T)
    cb, cc, cx = OFF_BB // CONV_COLS, OFF_BC // CONV_COLS, OFF_BX // CONV_COLS

    def body(b_ref, c_ref, x_ref, ch_ref, xh_ref, w_ref, o_ref):
        i = pl.program_id(1)
        z = c_ref[...] * x_ref[...]
        zh = jnp.where(i > 0, ch_ref[...] * xh_ref[...], 0.0)
        z1 = _shift_down(z, zh, 1)
        z2 = _shift_down(z, zh, 2)
        conv = w_ref[0:1, :] * z2 + w_ref[1:2, :] * z1 + w_ref[2:3, :] * z
        o_ref[...] = (b_ref[...] * conv).astype(o_ref.dtype)

    return pl.pallas_call(
        body, name=name, grid=(B_WIDTH // CONV_COLS, S // T),
        in_specs=[tile(cb), tile(cc), tile(cx), prev(cc), prev(cx),
                  pl.BlockSpec((3, CONV_COLS), lambda j, i: (0, j))],
        out_specs=tile(0),
        out_shape=SDS((S, B_WIDTH), BF16),
        compiler_params=_cp(2),
    )(p, p, p, p, p, w)


def _conv_bwd(name, p, dycat, w):
    S = p.shape[0]
    T = min(CONV_ROWS, S)
    tile, prev, nxt = _conv_specs(S, T)
    cb, cc, cx = OFF_BB // CONV_COLS, OFF_BC // CONV_COLS, OFF_BX // CONV_COLS
    cdy = A_WIDTH // CONV_COLS
    n_i = S // T

    def body(b_ref, c_ref, x_ref, dy_ref, ch_ref, xh_ref, bn_ref, dyn_ref, w_ref,
             db_ref, dc_ref, dx_ref, dw_ref):
        i = pl.program_id(1)

        @pl.when(i == 0)
        def _():
            dw_ref[...] = jnp.zeros_like(dw_ref)

        cv = c_ref[...]
        xv = x_ref[...]
        z = cv * xv
        zh = jnp.where(i > 0, ch_ref[...] * xh_ref[...], 0.0)
        z1 = _shift_down(z, zh, 1)
        z2 = _shift_down(z, zh, 2)
        w0, w1, w2 = w_ref[0:1, :], w_ref[1:2, :], w_ref[2:3, :]
        conv = w0 * z2 + w1 * z1 + w2 * z
        dy = dy_ref[...]
        db_ref[...] = (dy * conv).astype(db_ref.dtype)
        dconv = dy * b_ref[...]
        dconv_n = jnp.where(i < n_i - 1, dyn_ref[...] * bn_ref[...], 0.0)
        dz = w2 * dconv + w1 * _shift_up(dconv, dconv_n, 1) + w0 * _shift_up(dconv, dconv_n, 2)
        dc_ref[...] = (dz * xv).astype(dc_ref.dtype)
        dx_ref[...] = (dz * cv).astype(dx_ref.dtype)
        dw_ref[0:1, :] += jnp.sum(dconv * z2, axis=0, keepdims=True)
        dw_ref[1:2, :] += jnp.sum(dconv * z1, axis=0, keepdims=True)
        dw_ref[2:3, :] += jnp.sum(dconv * z, axis=0, keepdims=True)

    wspec = pl.BlockSpec((3, CONV_COLS), lambda j, i: (0, j))
    return pl.pallas_call(
        body, name=name, grid=(B_WIDTH // CONV_COLS, n_i),
        in_specs=[tile(cb), tile(cc), tile(cx), tile(cdy), prev(cc), prev(cx), nxt(cb), nxt(cdy), wspec],
        out_specs=[tile(0), tile(0), tile(0), wspec],
        out_shape=[SDS((S, B_WIDTH), BF16)] * 3 + [SDS((3, B_WIDTH), F32)],
        compiler_params=_cp(2),
    )(p, p, p, dycat, p, p, p, dycat, w)


def _seg_sum(t, bd):
    hi = t.astype(BF16)
    lo = (t - hi.astype(F32)).astype(BF16)
    return jnp.dot(hi, bd, preferred_element_type=F32) + jnp.dot(lo, bd, preferred_element_type=F32)


def _head_norm(x, g, bd):
    rstd = lax.rsqrt(_seg_sum(x * x, bd) * (1.0 / HEAD_DIM) + EPS)
    xhat = x * rstd
    return xhat * g, xhat, rstd


def _head_norm_bwd(dy, g, xhat, rstd, bd):
    dxh = dy * g
    return rstd * (dxh - xhat * (_seg_sum(dxh * xhat, bd) * (1.0 / HEAD_DIM)))


def _band_mask(has_prev):
    row = lax.broadcasted_iota(jnp.int32, (BLK, 2 * BLK), 0)
    col = lax.broadcasted_iota(jnp.int32, (BLK, 2 * BLK), 1)
    first_key = jnp.where(has_prev, 0, BLK)
    return (col >= row) & (col <= row + BLK) & (col >= first_key)


def _residue_rows(r, d):
    return slice(None) if d == 1 else pl.ds(r, BLK, stride=d)


STRIDED_LANES = 128


def _step_width(d):
    return PW if d == 1 else STRIDED_LANES


def _n_stack(lane):
    return lane.shape[1] // HEAD_DIM


def _for_residues(d, fn):
    if d == 1:
        fn(0)
    else:
        def two(i, carry):
            fn(2 * i)
            fn(2 * i + 1)
            return carry
        lax.fori_loop(0, d // 2, two, 0)


def _head_mask(lane, j):
    return (lane >= HEAD_DIM * j) & (lane < HEAD_DIM * (j + 1))


def _stack_heads(x, lane):
    return jnp.concatenate([jnp.where(_head_mask(lane, j), x, 0.0) for j in range(_n_stack(lane))], axis=0)


def _unstack_heads(y, lane):
    out = y[:BLK]
    for j in range(1, _n_stack(lane)):
        out = jnp.where(lane >= HEAD_DIM * j, y[BLK * j:BLK * (j + 1)], out)
    return out


def _head_columns(v, lane):
    return jnp.concatenate([jnp.max(jnp.where(_head_mask(lane, j), v, NEG), axis=1, keepdims=True)
                            for j in range(_n_stack(lane))], axis=0)


def _attn_fwd(name, p, g, gq, gk, bd):
    S = p.shape[0]
    d = PATTERN_DILATION[g]
    rows = BLK * d
    hw = _step_width(d)
    nt = (((1,), (1,)), ((), ()))

    def body(q_ref, kc_ref, kp_ref, vc_ref, vp_ref, gq_ref, gk_ref, bd_ref, o_ref, lse_ref):
        has_prev = pl.program_id(1) > 0
        bdv = bd_ref[...]
        band = jnp.concatenate([_band_mask(has_prev)] * (hw // HEAD_DIM), axis=0)
        lane = lax.broadcasted_iota(jnp.int32, (1, hw), 1)

        def residue(r):
            rr = _residue_rows(r, d)
            qn, _, _ = _head_norm(q_ref[rr, :], gq_ref[...], bdv)
            kn, _, _ = _head_norm(jnp.concatenate([kp_ref[rr, :], kc_ref[rr, :]], axis=0), gk_ref[...], bdv)
            knb = kn.astype(BF16)
            vb = jnp.concatenate([vp_ref[rr, :], vc_ref[rr, :]], axis=0).astype(BF16)
            qs = _stack_heads(qn, lane).astype(BF16)
            s = lax.dot_general(qs, knb, nt, preferred_element_type=F32) * (HEAD_DIM ** -0.5)
            s = jnp.where(band, s, NEG)
            m = jnp.max(s, axis=1, keepdims=True)
            e = jnp.exp(s - m)
            den = jnp.sum(e, axis=1, keepdims=True)
            pv = jnp.dot(e.astype(BF16), vb, preferred_element_type=F32)
            o_ref[rr, :] = _unstack_heads(pv / den, lane)
            lse_ref[rr, :] = _unstack_heads(jnp.broadcast_to(m + jnp.log(den), pv.shape), lane)

        _for_residues(d, residue)

    per = PW // hw
    cq, ck, cv = (OFF_Q + PW * g) // hw, (OFF_K + PW * g) // hw, (OFF_V + PW * g) // hw
    cur = lambda col: pl.BlockSpec((rows, hw), lambda h, n: (n, col + h))
    prv = lambda col: pl.BlockSpec((rows, hw), lambda h, n: (jnp.maximum(n - 1, 0), col + h))
    vec = pl.BlockSpec((1, hw), lambda h, n: (0, h))
    return pl.pallas_call(
        body, name=name, grid=(per, S // rows),
        in_specs=[cur(cq), cur(ck), prv(ck), cur(cv), prv(cv), vec, vec, pl.BlockSpec((hw, hw), lambda h, n: (0, 0))],
        out_specs=[cur(0), cur(0)],
        out_shape=[SDS((S, PW), F32)] * 2,
        compiler_params=_cp(2),
    )(p, p, p, p, p, gq, gk, bd)


def _attn_bwd(name, p, g, lse, do3, c3, gq, gk, bd):
    S = p.shape[0]
    d = PATTERN_DILATION[g]
    rows = BLK * d
    nblk = S // rows
    hw = _step_width(d)
    nt = (((1,), (1,)), ((), ()))
    tn = (((0,), (0,)), ((), ()))

    def body(q_ref, kc_ref, kp_ref, vc_ref, vp_ref, lse_ref, do_ref, c_ref, gq_ref, gk_ref, bd_ref,
             dq_ref, dk_ref, dv_ref, dgq_ref, dgk_ref, ck_ref, cv_ref, dq_keep_ref):
        n = pl.program_id(1)

        @pl.when(n == 0)
        def _():
            ck_ref[...] = jnp.zeros_like(ck_ref)
            cv_ref[...] = jnp.zeros_like(cv_ref)
            dgq_ref[...] = jnp.zeros_like(dgq_ref)
            dgk_ref[...] = jnp.zeros_like(dgk_ref)

        @pl.when(n == nblk)
        def _():
            dq_ref[...] = dq_keep_ref[...]
            dk_ref[...] = ck_ref[...]
            dv_ref[...] = cv_ref[...]

        bdv = bd_ref[...]
        gqv = gq_ref[...]
        gkv = gk_ref[...]
        band = jnp.concatenate([_band_mask(n > 0)] * (hw // HEAD_DIM), axis=0)
        lane = lax.broadcasted_iota(jnp.int32, (1, hw), 1)

        def residue(r):
            rr = _residue_rows(r, d)
            qn, qhat, qrstd = _head_norm(q_ref[rr, :], gqv, bdv)
            kn, khat, krstd = _head_norm(jnp.concatenate([kp_ref[rr, :], kc_ref[rr, :]], axis=0), gkv, bdv)
            knb = kn.astype(BF16)
            vb = jnp.concatenate([vp_ref[rr, :], vc_ref[rr, :]], axis=0).astype(BF16)
            qs = _stack_heads(qn, lane).astype(BF16)
            dos = _stack_heads(do_ref[rr, :], lane).astype(BF16)
            s = lax.dot_general(qs, knb, nt, preferred_element_type=F32) * (HEAD_DIM ** -0.5)
            prob = jnp.where(band, jnp.exp(s - _head_columns(lse_ref[rr, :], lane)), 0.0)
            dp = lax.dot_general(dos, vb, nt, preferred_element_type=F32)
            ds = (prob * (dp + _head_columns(c_ref[rr, :], lane)) * (HEAD_DIM ** -0.5)).astype(BF16)
            dqn = _unstack_heads(jnp.dot(ds, knb, preferred_element_type=F32), lane)
            dkn = lax.dot_general(ds, qs, tn, preferred_element_type=F32)
            dvv = lax.dot_general(prob.astype(BF16), dos, tn, preferred_element_type=F32)

            dq = _head_norm_bwd(dqn, gqv, qhat, qrstd, bdv)
            dq_ref[rr, :] = dq
            dq_keep_ref[rr, :] = dq
            dk2 = _head_norm_bwd(dkn, gkv, khat, krstd, bdv)
            dgq_ref[...] += jnp.sum(dqn * qhat, axis=0, keepdims=True)
            dgk_ref[...] += jnp.sum(dkn * khat, axis=0, keepdims=True)
            dk_ref[rr, :] = ck_ref[rr, :] + dk2[:BLK]
            dv_ref[rr, :] = cv_ref[rr, :] + dvv[:BLK]
            ck_ref[rr, :] = dk2[BLK:]
            cv_ref[rr, :] = dvv[BLK:]

        @pl.when(n < nblk)
        def _():
            _for_residues(d, residue)

    last = nblk - 1
    per = PW // hw
    cq, ck, cv = (OFF_Q + PW * g) // hw, (OFF_K + PW * g) // hw, (OFF_V + PW * g) // hw
    cur = lambda col: pl.BlockSpec((rows, hw), lambda h, n: (jnp.minimum(n, last), col + h))
    prv = lambda col: pl.BlockSpec((rows, hw), lambda h, n: (jnp.maximum(jnp.minimum(n, last) - 1, 0), col + h))
    cur3 = pl.BlockSpec((None, rows, hw), lambda h, n: (g, jnp.minimum(n, last), h))
    done = pl.BlockSpec((rows, hw), lambda h, n: (jnp.maximum(n - 1, 0), h))
    vec = pl.BlockSpec((1, hw), lambda h, n: (0, h))
    return pl.pallas_call(
        body, name=name, grid=(per, nblk + 1),
        in_specs=[cur(cq), cur(ck), prv(ck), cur(cv), prv(cv), cur(0), cur3, cur3, vec, vec,
                  pl.BlockSpec((hw, hw), lambda h, n: (0, 0))],
        out_specs=[cur(0), done, done, vec, vec],
        out_shape=[SDS((S, PW), F32)] * 3 + [SDS((1, PW), F32)] * 2,
        scratch_shapes=[pltpu.VMEM((rows, hw), F32)] * 3,
        compiler_params=_cp(2),
    )(p, p, p, p, p, lse, do3, c3, gq, gk, bd)


def _mix_fwd(name, os, lses):
    S = os[0].shape[0]
    tm = min(1024, S)

    def body(o0, o1, o2, l0, l1, l2, y_ref):
        o = [o0[...], o1[...], o2[...]]
        l = [l0[...], l1[...], l2[...]]
        m = jnp.maximum(jnp.maximum(l[0], l[1]), l[2])
        e = [jnp.exp(t - m) for t in l]
        inv = 1.0 / (e[0] + e[1] + e[2])
        for g in range(N_PATTERNS):
            y_ref[:, PW * g:PW * (g + 1)] = (o[g] * (e[g] * inv)).astype(y_ref.dtype)

    blk = pl.BlockSpec((tm, PW), lambda i: (i, 0))
    return pl.pallas_call(
        body, name=name, grid=(S // tm,),
        in_specs=[blk] * 6,
        out_specs=pl.BlockSpec((tm, C_WIDTH), lambda i: (i, 0)),
        out_shape=SDS((S, C_WIDTH), BF16),
        compiler_params=_cp(1),
    )(*os, *lses)


def _mix_bwd(name, os, lses, dycat, bd):
    S = os[0].shape[0]
    tm = min(1024, S)
    c0 = (A_WIDTH + B_WIDTH) // PW

    def body(o0, o1, o2, l0, l1, l2, dy0_ref, dy1_ref, dy2_ref, bd_ref, do_ref, c_ref):
        bdv = bd_ref[...]
        o = [o0[...], o1[...], o2[...]]
        l = [l0[...], l1[...], l2[...]]
        dys = [dy0_ref[...], dy1_ref[...], dy2_ref[...]]
        m = jnp.maximum(jnp.maximum(l[0], l[1]), l[2])
        e = [jnp.exp(t - m) for t in l]
        inv = 1.0 / (e[0] + e[1] + e[2])
        alpha = [t * inv for t in e]
        da = [_seg_sum(dys[g] * o[g], bdv) for g in range(N_PATTERNS)]
        mean_da = alpha[0] * da[0] + alpha[1] * da[1] + alpha[2] * da[2]
        for g in range(N_PATTERNS):
            do_ref[g] = dys[g] * alpha[g]
            c_ref[g] = -alpha[g] * mean_da

    blk = pl.BlockSpec((tm, PW), lambda i: (i, 0))
    blk3 = pl.BlockSpec((N_PATTERNS, tm, PW), lambda i: (0, i, 0))
    dyspec = lambda g: pl.BlockSpec((tm, PW), lambda i: (i, c0 + g))
    return pl.pallas_call(
        body, name=name, grid=(S // tm,),
        in_specs=[blk] * 6 + [dyspec(0), dyspec(1), dyspec(2), pl.BlockSpec((PW, PW), lambda i: (0, 0))],
        out_specs=[blk3, blk3],
        out_shape=[SDS((N_PATTERNS, S, PW), F32)] * 2,
        compiler_params=_cp(1),
    )(*os, *lses, dycat, dycat, dycat, bd)


def _mesh_pos():
    x, y, c = lax.axis_index("x"), lax.axis_index("y"), lax.axis_index("c")
    chips = [(1 - x, y), (x, 1 - y), (1 - x, 1 - y)]
    chip_idx = [2 * cx + cy for cx, cy in chips]
    return x, y, c, 2 * x + y, chips, chip_idx


def _place_shard(name, w, layer, chip_arr, out_dtype, deps=()):
    _, R, C = w.shape
    tr = min(256, R)

    def body(chip_ref, w_ref, *rest):
        o_ref = rest[-1]
        o_ref[...] = w_ref[...].astype(o_ref.dtype)

    return pl.pallas_call(
        body, name=name,
        grid_spec=pltpu.PrefetchScalarGridSpec(
            num_scalar_prefetch=1, grid=(R // tr,),
            in_specs=[pl.BlockSpec((None, tr, C), lambda i, chip_ref: (layer, i, 0))] + [_hbm_spec()] * len(deps),
            out_specs=pl.BlockSpec((None, tr, C), lambda i, chip_ref: (chip_ref[0], i, 0))),
        out_shape=SDS((N_CHIPS, R, C), out_dtype),
        compiler_params=_cp(1),
    )(chip_arr, w, *deps)


HBM_SPEC = pl.BlockSpec(memory_space=pltpu.HBM)
SEM_SPEC = pl.BlockSpec(memory_space=pltpu.SEMAPHORE)
SPLIT_COPY = pltpu.SideEffectType.DATAFLOW_SIDE_EFFECTING
N_PEER_CHIPS = N_CHIPS - 1
TOKEN_SHAPE = SDS((8, 128), F32)
TOKEN_SPEC = pl.BlockSpec(memory_space=pltpu.VMEM)


def _in_hbm(a):
    return pltpu.with_memory_space_constraint(a, pltpu.HBM)


def _gather_start(name, bufs):
    T = len(bufs)

    def body(*refs):
        ins = refs[:T]
        send_sems, recv_sems = refs[T:2 * T], refs[2 * T:3 * T]
        token = refs[4 * T]
        x, y, c, me, chips, chip_idx = _mesh_pos()
        for t in range(T):
            hr = ins[t].shape[1] // 2
            mine = ins[t].at[me, pl.ds(c * hr, hr), :]
            for j in range(N_PEER_CHIPS):
                pltpu.make_async_remote_copy(src_ref=mine, dst_ref=mine, send_sem=send_sems[t].at[j],
                                             recv_sem=recv_sems[t].at[j], device_id=(*chips[j], c),
                                             device_id_type=MESH).start()
        token[...] = jnp.zeros_like(token)

    sems = [pltpu.SemaphoreType.DMA((N_PEER_CHIPS,))] * T
    out = pl.pallas_call(
        body, name=name,
        in_specs=[HBM_SPEC] * T,
        out_specs=[SEM_SPEC] * (2 * T) + [HBM_SPEC] * T + [TOKEN_SPEC],
        out_shape=sems + sems + [pltpu.HBM(b.shape, b.dtype) for b in bufs] + [TOKEN_SHAPE],
        input_output_aliases={t: 2 * T + t for t in range(T)},
        compiler_params=pltpu.CompilerParams(has_side_effects=SPLIT_COPY),
    )(*[_in_hbm(b) for b in bufs])
    return out[:T], out[T:2 * T], out[2 * T:3 * T], out[3 * T]


def _gather_wait(name, buf, send_sem, recv_sem, after):
    n_in = 3 if after is None else 4

    def body(*refs):
        buf_ref, ssem, rsem = refs[:3]
        x, y, c, me, chips, chip_idx = _mesh_pos()
        hr = buf_ref.shape[1] // 2
        mine = buf_ref.at[me, pl.ds(c * hr, hr), :]
        for j in range(N_PEER_CHIPS):
            got = buf_ref.at[chip_idx[j], pl.ds(c * hr, hr), :]
            cp = pltpu.make_async_remote_copy(src_ref=mine, dst_ref=got, send_sem=ssem.at[j], recv_sem=rsem.at[j],
                                              device_id=(*chips[j], c), device_id_type=MESH)
            cp.wait_send()
            cp.wait_recv()

    args = [buf, send_sem, recv_sem] + ([] if after is None else [after])
    return pl.pallas_call(
        body, name=name,
        in_specs=[HBM_SPEC, SEM_SPEC, SEM_SPEC] + [_hbm_spec()] * (n_in - 3),
        out_specs=HBM_SPEC,
        out_shape=pltpu.HBM(buf.shape, buf.dtype),
        input_output_aliases={0: 0},
        compiler_params=pltpu.CompilerParams(has_side_effects=SPLIT_COPY),
    )(*args)


def _forward_start(name, buf):
    def body(buf_ref, send_sems, recv_sems, buf_thru, token):
        x, y, c, me, chips, chip_idx = _mesh_pos()
        hr = buf_ref.shape[1] // 2
        for j in range(N_PEER_CHIPS):
            got = buf_ref.at[chip_idx[j], pl.ds(c * hr, hr), :]
            pltpu.make_async_remote_copy(src_ref=got, dst_ref=got, send_sem=send_sems.at[j], recv_sem=recv_sems.at[j],
                                         device_id=(x, y, 1 - c), device_id_type=MESH).start()
        token[...] = jnp.zeros_like(token)

    sems = pltpu.SemaphoreType.DMA((N_PEER_CHIPS,))
    return pl.pallas_call(
        body, name=name,
        in_specs=[HBM_SPEC],
        out_specs=[SEM_SPEC, SEM_SPEC, HBM_SPEC, TOKEN_SPEC],
        out_shape=[sems, sems, pltpu.HBM(buf.shape, buf.dtype), TOKEN_SHAPE],
        input_output_aliases={0: 2},
        compiler_params=pltpu.CompilerParams(has_side_effects=SPLIT_COPY),
    )(_in_hbm(buf))


def _forward_wait(name, buf, send_sems, recv_sems, after):
    n_in = 3 if after is None else 4

    def body(*refs):
        buf_ref, ssems, rsems = refs[:3]
        x, y, c, me, chips, chip_idx = _mesh_pos()
        hr = buf_ref.shape[1] // 2
        for j in range(N_PEER_CHIPS):
            sent = buf_ref.at[chip_idx[j], pl.ds(c * hr, hr), :]
            theirs = buf_ref.at[chip_idx[j], pl.ds((1 - c) * hr, hr), :]
            cp = pltpu.make_async_remote_copy(src_ref=sent, dst_ref=theirs, send_sem=ssems.at[j],
                                              recv_sem=rsems.at[j], device_id=(x, y, 1 - c), device_id_type=MESH)
            cp.wait_send()
            cp.wait_recv()

    args = [buf, send_sems, recv_sems] + ([] if after is None else [after])
    return pl.pallas_call(
        body, name=name,
        in_specs=[HBM_SPEC, SEM_SPEC, SEM_SPEC] + [_hbm_spec()] * (n_in - 3),
        out_specs=HBM_SPEC,
        out_shape=pltpu.HBM(buf.shape, buf.dtype),
        input_output_aliases={0: 0},
        compiler_params=pltpu.CompilerParams(has_side_effects=SPLIT_COPY),
    )(*args)


class _GatheredWeights:
    def __init__(self):
        self._order = []
        self._pending = {}
        self._forwarding = {}
        self._ready = {}
        self._tokens = []

    def start(self, keys, bufs):
        send_sems, recv_sems, thru, token = _gather_start(f"gather_start_{len(self._order)}", bufs)
        self._tokens.append(token)
        self._order.extend(keys)
        self._pending.update({k: (b, s, r) for k, b, s, r in zip(keys, thru, send_sems, recv_sems)})

    def _prefetch(self, key, after):
        if key in self._pending:
            buf, ssem, rsem = self._pending.pop(key)
            tag = f"{key[0]}_{key[1]}"
            buf = _gather_wait(f"gather_wait_{tag}", buf, ssem, rsem, after)
            ssems, rsems, buf, token = _forward_start(f"gather_fwd_start_{tag}", buf)
            self._forwarding[key] = (buf, ssems, rsems)
            self._tokens.append(token)

    def get(self, name, layer, after=None, prefetch_next=True):
        key = (name, layer)
        if key not in self._ready:
            self._prefetch(key, after)
            buf, ssems, rsems = self._forwarding.pop(key)
            self._ready[key] = _forward_wait(f"gather_fwd_wait_{name}_{layer}", buf, ssems, rsems, after)
            if prefetch_next:
                self.prefetch_after(name, layer, after)
        return self._ready[key]

    def prefetch_after(self, name, layer, after):
        nxt = self._order.index((name, layer)) + 1
        if nxt < len(self._order):
            self._prefetch(self._order[nxt], after)

    def deps(self):
        tokens, self._tokens = self._tokens, []
        return tokens


def _swap_copy(g_ref, land_ref, send_sem, recv_sem):
    x, y, c, _, _, _ = _mesh_pos()
    hr = g_ref.shape[1] // 2
    return pltpu.make_async_remote_copy(src_ref=g_ref.at[:, pl.ds((1 - c) * hr, hr), :], dst_ref=land_ref,
                                        send_sem=send_sem, recv_sem=recv_sem, device_id=(x, y, 1 - c),
                                        device_id_type=MESH)


def _swap_start(name, g):
    land_shape = (g.shape[0], g.shape[1] // 2, g.shape[2])

    def body(g_ref, land_ref, send_sem, recv_sem, land_thru, token):
        _swap_copy(g_ref, land_ref, send_sem, recv_sem).start()
        token[...] = jnp.zeros_like(token)

    return pl.pallas_call(
        body, name=name,
        in_specs=[HBM_SPEC, HBM_SPEC],
        out_specs=[SEM_SPEC, SEM_SPEC, HBM_SPEC, TOKEN_SPEC],
        out_shape=[pltpu.SemaphoreType.DMA(()), pltpu.SemaphoreType.DMA(()), pltpu.HBM(land_shape, g.dtype),
                   TOKEN_SHAPE],
        input_output_aliases={1: 2},
        compiler_params=pltpu.CompilerParams(has_side_effects=SPLIT_COPY),
    )(_in_hbm(g), _in_hbm(lax.empty(land_shape, g.dtype)))


def _swap_wait(name, g, land, send_sem, recv_sem, after):
    def body(g_ref, land_ref, send_sem, recv_sem, after_ref, land_out):
        cp = _swap_copy(g_ref, land_ref, send_sem, recv_sem)
        cp.wait_send()
        cp.wait_recv()

    return pl.pallas_call(
        body, name=name,
        in_specs=[HBM_SPEC, HBM_SPEC, SEM_SPEC, SEM_SPEC, _hbm_spec()],
        out_specs=HBM_SPEC,
        out_shape=pltpu.HBM(land.shape, land.dtype),
        input_output_aliases={1: 0},
        compiler_params=pltpu.CompilerParams(has_side_effects=SPLIT_COPY),
    )(_in_hbm(g), land, send_sem, recv_sem, after)


def _add_my_half(name, g, r, pos_arr):
    ns, R, C = g.shape
    hr = R // 2
    tr = min(256, hr)
    nt = hr // tr

    def body(pos_ref, g_ref, r_ref, o_ref, land_ref):
        t = (g_ref[...] + r_ref[...]).astype(o_ref.dtype)
        o_ref[...] = t

        @pl.when(pl.program_id(1) == pos_ref[1])
        def _():
            land_ref[...] = t

    blk = pl.BlockSpec((None, tr, C), lambda i, s, pos_ref: (s, i, 0))
    return pl.pallas_call(
        body, name=name,
        grid_spec=pltpu.PrefetchScalarGridSpec(
            num_scalar_prefetch=1, grid=(nt, ns),
            in_specs=[pl.BlockSpec((None, tr, C), lambda i, s, pos_ref: (s, pos_ref[0] * nt + i, 0)), blk],
            out_specs=[blk, pl.BlockSpec((None, tr, C), lambda i, s, pos_ref: (pos_ref[1], i, 0))]),
        out_shape=[SDS((ns, hr, C), BF16)] * 2,
        compiler_params=_cp(2),
    )(pos_arr, g, r)


def _exchange_start(name, part, land):
    def body(part_ref, land_ref, send_sems, recv_sems, land_thru, token):
        x, y, c, me, chips, chip_idx = _mesh_pos()
        for j in range(N_PEER_CHIPS):
            pltpu.make_async_remote_copy(src_ref=part_ref.at[chip_idx[j]], dst_ref=land_ref.at[me],
                                         send_sem=send_sems.at[j], recv_sem=recv_sems.at[j],
                                         device_id=(*chips[j], c), device_id_type=MESH).start()
        token[...] = jnp.zeros_like(token)

    sems = pltpu.SemaphoreType.DMA((N_PEER_CHIPS,))
    return pl.pallas_call(
        body, name=name,
        in_specs=[HBM_SPEC, HBM_SPEC],
        out_specs=[SEM_SPEC, SEM_SPEC, HBM_SPEC, TOKEN_SPEC],
        out_shape=[sems, sems, pltpu.HBM(land.shape, land.dtype), TOKEN_SHAPE],
        input_output_aliases={1: 2},
        compiler_params=pltpu.CompilerParams(has_side_effects=SPLIT_COPY),
    )(_in_hbm(part), _in_hbm(land))


def _exchange_wait(name, part, land, send_sems, recv_sems, after):
    def body(part_ref, land_ref, send_sems, recv_sems, after_ref, land_out):
        x, y, c, me, chips, chip_idx = _mesh_pos()
        for j in range(N_PEER_CHIPS):
            cp = pltpu.make_async_remote_copy(src_ref=part_ref.at[chip_idx[j]], dst_ref=land_ref.at[chip_idx[j]],
                                              send_sem=send_sems.at[j], recv_sem=recv_sems.at[j],
                                              device_id=(*chips[j], c), device_id_type=MESH)
            cp.wait_send()
            cp.wait_recv()

    return pl.pallas_call(
        body, name=name,
        in_specs=[HBM_SPEC, HBM_SPEC, SEM_SPEC, SEM_SPEC, _hbm_spec()],
        out_specs=HBM_SPEC,
        out_shape=pltpu.HBM(land.shape, land.dtype),
        input_output_aliases={1: 0},
        compiler_params=pltpu.CompilerParams(has_side_effects=SPLIT_COPY),
    )(_in_hbm(part), land, send_sems, recv_sems, after)


class _GradReducer:
    def __init__(self, c_arr):
        self._c_arr = c_arr
        self._swapping = []
        self._exchanging = {}
        self._joining = {}
        self._tokens = []

    def begin(self, name, layer, g):
        tag = f"{name}_{layer}"
        ssem, rsem, land, token = _swap_start(f"rs_swap_start_{tag}", g)
        self._swapping.append((name, layer, g, ssem, rsem, land))
        self._tokens.append(token)

    def advance(self, after):
        for name, layer, g, ssem, rsem, land in self._swapping:
            tag = f"{name}_{layer}"
            theirs = _swap_wait(f"rs_swap_wait_{tag}", g, land, ssem, rsem, after)
            part, own = _add_my_half(f"rs_add_{tag}", g, theirs, self._c_arr)
            ssems, rsems, land2, token = _exchange_start(f"rs_xchg_start_{tag}", part, own)
            self._exchanging[(name, layer)] = (part, ssems, rsems, land2)
            self._tokens.append(token)
        self._swapping = []

    def deps(self):
        tokens, self._tokens = self._tokens, []
        return tokens

    def reduce(self, name, n_layers, after):
        buf = None
        for layer in range(n_layers):
            part, ssems, rsems, land = self._exchanging.pop((name, layer))
            tag = f"{name}_{layer}"
            landed = _exchange_wait(f"rs_xchg_wait_{tag}", part, land, ssems, rsems, after)
            buf = _sum_chips(f"rs_sum_{tag}", landed, self._c_arr, layer, n_layers, buf)
        ssem, rsem, buf, token = _join_start(f"rs_join_start_{name}", buf)
        self._joining[name] = (buf, ssem, rsem)
        return token

    def reduced(self, name, after):
        buf, ssem, rsem = self._joining.pop(name)
        return _join_wait(f"rs_join_wait_{name}", buf, ssem, rsem, after)


def _sum_chips(name, r, c_arr, layer, n_layers, prev):
    ns, H, C = r.shape
    tr = min(256, H)
    nt = H // tr

    def body(c_ref, r_ref, *rest):
        o_ref = rest[-1]
        o_ref[...] = ((r_ref[0].astype(F32) + r_ref[1].astype(F32)) + r_ref[2].astype(F32)) + r_ref[3].astype(F32)

    in_specs = [pl.BlockSpec((ns, tr, C), lambda i, c_ref: (0, i, 0))]
    args = [c_arr, r]
    aliases = {}
    if prev is not None:
        in_specs.append(_hbm_spec())
        args.append(prev)
        aliases = {2: 0}
    return pl.pallas_call(
        body, name=name,
        grid_spec=pltpu.PrefetchScalarGridSpec(
            num_scalar_prefetch=1, grid=(nt,), in_specs=in_specs,
            out_specs=pl.BlockSpec((None, tr, C), lambda i, c_ref: (layer, c_ref[0] * nt + i, 0))),
        out_shape=SDS((n_layers, 2 * H, C), F32),
        input_output_aliases=aliases,
        compiler_params=_cp(1),
    )(*args)


def _join_copy(buf_ref, send_sem, recv_sem):
    x, y, c, _, _, _ = _mesh_pos()
    hr = buf_ref.shape[1] // 2
    mine = buf_ref.at[:, pl.ds(c * hr, hr), :]
    theirs = buf_ref.at[:, pl.ds((1 - c) * hr, hr), :]
    send = pltpu.make_async_remote_copy(src_ref=mine, dst_ref=mine, send_sem=send_sem, recv_sem=recv_sem,
                                        device_id=(x, y, 1 - c), device_id_type=MESH)
    arrive = pltpu.make_async_remote_copy(src_ref=theirs, dst_ref=theirs, send_sem=send_sem, recv_sem=recv_sem,
                                          device_id=(x, y, 1 - c), device_id_type=MESH)
    return send, arrive


def _join_start(name, buf):
    def body(buf_ref, send_sem, recv_sem, buf_thru, token):
        _join_copy(buf_ref, send_sem, recv_sem)[0].start()
        token[...] = jnp.zeros_like(token)

    return pl.pallas_call(
        body, name=name,
        in_specs=[HBM_SPEC],
        out_specs=[SEM_SPEC, SEM_SPEC, HBM_SPEC, TOKEN_SPEC],
        out_shape=[pltpu.SemaphoreType.DMA(()), pltpu.SemaphoreType.DMA(()), pltpu.HBM(buf.shape, buf.dtype),
                   TOKEN_SHAPE],
        input_output_aliases={0: 2},
        compiler_params=pltpu.CompilerParams(has_side_effects=SPLIT_COPY),
    )(_in_hbm(buf))


def _join_wait(name, buf, send_sem, recv_sem, after):
    def body(buf_ref, send_sem, recv_sem, after_ref, buf_out):
        send, arrive = _join_copy(buf_ref, send_sem, recv_sem)
        send.wait_send()
        arrive.wait_recv()

    return pl.pallas_call(
        body, name=name,
        in_specs=[HBM_SPEC, SEM_SPEC, SEM_SPEC, _hbm_spec()],
        out_specs=HBM_SPEC,
        out_shape=pltpu.HBM(buf.shape, buf.dtype),
        input_output_aliases={0: 0},
        compiler_params=pltpu.CompilerParams(has_side_effects=SPLIT_COPY),
    )(buf, send_sem, recv_sem, after)


def _small_copy(k, buf_ref, land_ref, send_sems, recv_sems):
    x, y, c = lax.axis_index("x"), lax.axis_index("y"), lax.axis_index("c")
    me = 4 * x + 2 * y + c
    peer = (x ^ ((k >> 2) & 1), y ^ ((k >> 1) & 1), c ^ (k & 1))
    cp = pltpu.make_async_remote_copy(src_ref=buf_ref, dst_ref=land_ref.at[me], send_sem=send_sems.at[k - 1],
                                      recv_sem=recv_sems.at[k - 1], device_id=peer, device_id_type=MESH)
    return me, peer, cp


def _small_start(buf, deps):
    land = jnp.broadcast_to(buf[None], (N_DEV,) + buf.shape)
    n_dep = len(deps)

    def body(buf_ref, land_ref, *rest):
        send_sems, recv_sems, _, token = rest[n_dep:]
        for k in range(1, N_DEV):
            _small_copy(k, buf_ref, land_ref, send_sems, recv_sems)[2].start()
        token[...] = jnp.zeros_like(token)

    sems = pltpu.SemaphoreType.DMA((N_DEV - 1,))
    return pl.pallas_call(
        body, name="small_gather_start",
        in_specs=[HBM_SPEC, HBM_SPEC] + [_hbm_spec()] * n_dep,
        out_specs=[SEM_SPEC, SEM_SPEC, HBM_SPEC, TOKEN_SPEC],
        out_shape=[sems, sems, pltpu.HBM(land.shape, land.dtype), TOKEN_SHAPE],
        input_output_aliases={1: 2},
        compiler_params=pltpu.CompilerParams(has_side_effects=SPLIT_COPY),
    )(_in_hbm(buf), _in_hbm(land), *deps)


def _small_wait(buf, land, send_sems, recv_sems, after):
    def body(buf_ref, land_ref, send_sems, recv_sems, after_ref, land_out):
        for k in range(1, N_DEV):
            me, peer, cp = _small_copy(k, buf_ref, land_ref, send_sems, recv_sems)
            cp.wait_send()
            got = land_ref.at[me ^ k]
            pltpu.make_async_remote_copy(src_ref=got, dst_ref=got, send_sem=send_sems.at[k - 1],
                                         recv_sem=recv_sems.at[k - 1], device_id=peer,
                                         device_id_type=MESH).wait_recv()

    return pl.pallas_call(
        body, name="small_gather_wait",
        in_specs=[HBM_SPEC, HBM_SPEC, SEM_SPEC, SEM_SPEC, _hbm_spec()],
        out_specs=HBM_SPEC,
        out_shape=pltpu.HBM(land.shape, land.dtype),
        input_output_aliases={1: 0},
        compiler_params=pltpu.CompilerParams(has_side_effects=SPLIT_COPY),
    )(_in_hbm(buf), land, send_sems, recv_sems, after)


def _sum_devices(land):
    n, R, C = land.shape

    def body(land_ref, out_ref):
        acc = land_ref[0]
        for d in range(1, n):
            acc = acc + land_ref[d]
        out_ref[...] = acc

    return pl.pallas_call(
        body, name="small_sum",
        in_specs=[pl.BlockSpec(memory_space=pltpu.VMEM)],
        out_specs=pl.BlockSpec(memory_space=pltpu.VMEM),
        out_shape=SDS((R, C), land.dtype),
        compiler_params=pltpu.CompilerParams(vmem_limit_bytes=V7X_VMEM_LIMIT),
    )(land)


def _pack_rows(vectors):
    flat = jnp.concatenate([v.reshape(-1) for v in vectors])
    n = flat.shape[0]
    padded = -(-n // 1024) * 1024
    return jnp.pad(flat, (0, padded - n)).reshape(padded // 128, 128)


def _unpack_rows(buf, shapes):
    flat = buf.reshape(-1)
    out, off = [], 0
    for s in shapes:
        n = 1
        for dim in s:
            n *= dim
        out.append(flat[off:off + n].reshape(s))
        off += n
    return out


def _layer_forward(l, x, prm, wg):
    S, D = x.shape
    h = _rmsnorm_fwd(f"attn_norm_{l}", x, prm["attn_norm"][l])
    w_in = wg.get("w_in", l, h, prefetch_next=l > 0)
    ns_in = w_in.shape[-1]
    tmi = min(1024, S)
    p = _matmul(
        f"in_proj_{l}", h, w_in, (S, N_CHIPS * ns_in), F32, grid=(S // tmi, N_CHIPS, 1),
        a_spec=pl.BlockSpec((tmi, D), lambda i, j, k: (i, 0)),
        b_spec=pl.BlockSpec((None, D, ns_in), lambda i, j, k: (j, 0, 0)),
        o_spec=pl.BlockSpec((tmi, ns_in), lambda i, j, k: (i, j)),
        contract=(1, 0), acc_shape=(tmi, ns_in), deps=wg.deps())
    if l == 0:
        wg.prefetch_after("w_in", l, p)
    y_a = _sgu_fwd(f"sgu_fwd_{l}", p, prm["sgu_wt"][l], prm["sgu_bb"][l])
    y_b = _conv_fwd(f"conv_fwd_{l}", p, prm["conv_w"][l])
    os, lses = [], []
    for g in range(N_PATTERNS):
        o_g, lse_g = _attn_fwd(f"attn_fwd_{l}_{g}", p, g, prm["q_gain"][l], prm["k_gain"][l], prm["bd"])
        os.append(o_g)
        lses.append(lse_g)
    y_c = _mix_fwd(f"mix_fwd_{l}", os, lses)
    ycat = jnp.concatenate([y_a, y_b, y_c], axis=1)
    tmb, tnb = min(1024, S), min(1024, D)
    w_out = wg.get("w_out", l, ycat)
    kq = N_CHIPS * w_out.shape[1]
    tmo, tno = min(512, S), D
    x1 = _matmul(
        f"out_proj_{l}", ycat, w_out.reshape(kq, D), (S, D), F32, grid=(S // tmo, D // tno, 1),
        a_spec=pl.BlockSpec((tmo, kq), lambda i, j, k: (i, 0)),
        b_spec=pl.BlockSpec((kq, tno), lambda i, j, k: (0, j)),
        o_spec=pl.BlockSpec((tmo, tno), lambda i, j, k: (i, j)),
        contract=(1, 0), acc_shape=(tmo, tno),
        extras=(x,), extra_specs=(pl.BlockSpec((tmo, tno), lambda i, j, k: (i, j)),),
        epi=lambda r, res: r + res, deps=wg.deps())
    w_mlp_in = wg.get("w_mlp_in", l, x1)
    h2 = _rmsnorm_fwd(f"mlp_norm_{l}", x1, prm["mlp_norm"][l])
    nf4 = w_mlp_in.shape[-1]
    r = _matmul(
        f"mlp_in_{l}", h2, w_mlp_in, (S, N_CHIPS * nf4), BF16, grid=(S // tmb, N_CHIPS, 1),
        a_spec=pl.BlockSpec((tmb, D), lambda i, j, k: (i, 0)),
        b_spec=pl.BlockSpec((None, D, nf4), lambda i, j, k: (j, 0, 0)),
        o_spec=pl.BlockSpec((tmb, nf4), lambda i, j, k: (i, j)),
        contract=(1, 0), acc_shape=(tmb, nf4), epi=_relu, deps=wg.deps())
    w_mlp_out = wg.get("w_mlp_out", l, r)
    dff4 = w_mlp_out.shape[1]
    tk = min(2048, dff4)
    kpc = dff4 // tk
    x2 = _matmul(
        f"mlp_out_{l}", r, w_mlp_out, (S, D), F32, grid=(S // tmb, D // tnb, N_CHIPS * kpc),
        a_spec=pl.BlockSpec((tmb, tk), lambda i, j, k: (i, k)),
        b_spec=pl.BlockSpec((None, tk, tnb), lambda i, j, k: (k // kpc, k % kpc, j)),
        o_spec=pl.BlockSpec((tmb, tnb), lambda i, j, k: (i, j)),
        contract=(1, 0), acc_shape=(tmb, tnb), a_pre=_square,
        extras=(x1,), extra_specs=(pl.BlockSpec((tmb, tnb), lambda i, j, k: (i, j)),),
        epi=lambda acc, res: acc + res, deps=wg.deps())
    saved = dict(x=x, p=p, h=h, os=os, lses=lses, ycat=ycat, x1=x1, r=r, h2=h2)
    return x2, saved


def _layer_backward(l, dx2, dx2b, sv, prm, wg, sink):
    S, D = dx2.shape
    w_in, w_out = wg.get("w_in", l), wg.get("w_out", l)
    w_mlp_in, w_mlp_out = wg.get("w_mlp_in", l), wg.get("w_mlp_out", l)
    dff4 = w_mlp_in.shape[-1]
    dff = N_CHIPS * dff4

    tmb, tnb = min(1024, S), min(1024, D)
    da = _matmul(
        f"mlp_out_bwd_{l}", dx2b, w_mlp_out, (S, dff), BF16, grid=(S // tmb, N_CHIPS, 1),
        a_spec=pl.BlockSpec((tmb, D), lambda i, j, k: (i, 0)),
        b_spec=pl.BlockSpec((None, dff4, D), lambda i, j, k: (j, 0, 0)),
        o_spec=pl.BlockSpec((tmb, dff4), lambda i, j, k: (i, j)),
        contract=(1, 1), acc_shape=(tmb, dff4),
        extras=(sv["r"],), extra_specs=(pl.BlockSpec((tmb, dff4), lambda i, j, k: (i, j)),),
        epi=lambda acc, r: acc * (2.0 * r.astype(F32)), deps=sink.deps())
    tmw = min(1024, dff4)
    mpc = dff4 // tmw
    g_w2 = _matmul(
        f"mlp_out_dw_{l}", sv["r"], dx2b, (N_CHIPS, dff4, D), F32, grid=(N_CHIPS * mpc, D // tnb, 1),
        a_spec=pl.BlockSpec((S, tmw), lambda i, j, k: (0, i)),
        b_spec=pl.BlockSpec((S, tnb), lambda i, j, k: (0, j)),
        o_spec=pl.BlockSpec((None, tmw, tnb), lambda i, j, k: (i // mpc, i % mpc, j)),
        contract=(0, 0), acc_shape=(tmw, tnb), a_pre=_square)
    sink.begin("w_mlp_out", l, g_w2)
    tnx = D
    dh2 = _matmul(
        f"mlp_in_bwd_{l}", da, w_mlp_in, (S, D), F32, grid=(S // tmb, D // tnx, N_CHIPS),
        a_spec=pl.BlockSpec((tmb, dff4), lambda i, j, k: (i, k)),
        b_spec=pl.BlockSpec((None, tnx, dff4), lambda i, j, k: (k, j, 0)),
        o_spec=pl.BlockSpec((tmb, tnx), lambda i, j, k: (i, j)),
        contract=(1, 1), acc_shape=(tmb, tnx), deps=sink.deps())
    sink.advance(dh2)
    tmd = min(1024, D)
    nd = D // tmd
    tnf = min(1024, dff4)
    nf = dff4 // tnf
    g_w1 = _matmul(
        f"mlp_in_dw_{l}", sv["h2"], da, (N_CHIPS, D, dff4), F32, grid=(N_CHIPS * nd, nf, 1),
        a_spec=pl.BlockSpec((S, tmd), lambda i, j, k: (0, i % nd)),
        b_spec=pl.BlockSpec((S, tnf), lambda i, j, k: (0, (i // nd) * nf + j)),
        o_spec=pl.BlockSpec((None, tmd, tnf), lambda i, j, k: (i // nd, i % nd, j)),
        contract=(0, 0), acc_shape=(tmd, tnf))
    sink.begin("w_mlp_in", l, g_w1)
    dx1, dx1b, g_mlp_norm = _rmsnorm_bwd(f"mlp_norm_bwd_{l}", dh2, sv["x1"], prm["mlp_norm"][l], dx2,
                                         deps=sink.deps())

    rq = w_out.shape[1]
    kq = N_CHIPS * rq
    dycat = _matmul(
        f"out_proj_bwd_{l}", dx1b, w_out.reshape(kq, D), (S, kq), F32, grid=(S // tmb, 1, 1),
        a_spec=pl.BlockSpec((tmb, D), lambda i, j, k: (i, 0)),
        b_spec=pl.BlockSpec((kq, D), lambda i, j, k: (0, 0)),
        o_spec=pl.BlockSpec((tmb, kq), lambda i, j, k: (i, 0)),
        contract=(1, 1), acc_shape=(tmb, kq))
    sink.advance(dycat)
    g_wout = _matmul(
        f"out_proj_dw_{l}", sv["ycat"], dx1b, (N_CHIPS, rq, D), F32, grid=(N_CHIPS, 1, 1),
        a_spec=pl.BlockSpec((S, rq), lambda i, j, k: (0, i)),
        b_spec=pl.BlockSpec((S, D), lambda i, j, k: (0, 0)),
        o_spec=pl.BlockSpec((None, rq, D), lambda i, j, k: (i, 0, 0)),
        contract=(0, 0), acc_shape=(rq, D))
    sink.begin("w_out", l, g_wout)

    p = sv["p"]
    du, dv_a, g_sgu_w, db_lanes = _sgu_bwd(f"sgu_bwd_{l}", p, dycat, prm["sgu_wt"][l], prm["sgu_wtt"][l],
                                           prm["sgu_bb"][l])
    g_sgu_b = db_lanes[:, :A_HEADS].T
    db, dc, dxb, g_conv = _conv_bwd(f"conv_bwd_{l}", p, dycat, prm["conv_w"][l])
    do3, c3 = _mix_bwd(f"mix_bwd_{l}", sv["os"], sv["lses"], dycat, prm["bd"])
    dqs, dks, dvs, dgqs, dgks = [], [], [], [], []
    for g in range(N_PATTERNS):
        dq, dk, dv, dgq, dgk = _attn_bwd(f"attn_bwd_{l}_{g}", p, g, sv["lses"][g], do3, c3,
                                         prm["q_gain"][l], prm["k_gain"][l], prm["bd"])
        dqs.append(dq)
        dks.append(dk)
        dvs.append(dv)
        dgqs.append(dgq)
        dgks.append(dgk)
    g_q = jnp.concatenate(dgqs, axis=1).reshape(N_PATTERNS * PW // HEAD_DIM, HEAD_DIM).sum(axis=0)
    g_k = jnp.concatenate(dgks, axis=1).reshape(N_PATTERNS * PW // HEAD_DIM, HEAD_DIM).sum(axis=0)
    dp = jnp.concatenate([du, dv_a, db, dc, dxb] + [t.astype(BF16) for t in dqs + dks + dvs], axis=1)

    ns_in = w_in.shape[-1]
    tmh = min(512, D)
    nh = D // tmh
    g_win = _matmul(
        f"in_proj_dw_{l}", sv["h"], dp, (N_CHIPS, D, ns_in), F32, grid=(N_CHIPS * nh, 1, 1),
        a_spec=pl.BlockSpec((S, tmh), lambda i, j, k: (0, i % nh)),
        b_spec=pl.BlockSpec((S, ns_in), lambda i, j, k: (0, i // nh)),
        o_spec=pl.BlockSpec((None, tmh, ns_in), lambda i, j, k: (i // nh, i % nh, 0)),
        contract=(0, 0), acc_shape=(tmh, ns_in))
    sink.begin("w_in", l, g_win)
    dh = _matmul(
        f"in_proj_bwd_{l}", dp, w_in, (S, D), F32, grid=(S // tmb, D // tnx, N_CHIPS),
        a_spec=pl.BlockSpec((tmb, ns_in), lambda i, j, k: (i, k)),
        b_spec=pl.BlockSpec((None, tnx, ns_in), lambda i, j, k: (k, j, 0)),
        o_spec=pl.BlockSpec((tmb, tnx), lambda i, j, k: (i, j)),
        contract=(1, 1), acc_shape=(tmb, tnx), deps=sink.deps())
    sink.advance(dh)
    dx0, dx0b, g_attn_norm = _rmsnorm_bwd(f"attn_norm_bwd_{l}", dh, sv["x"], prm["attn_norm"][l], dx1,
                                          deps=sink.deps())

    big = dict(w_in=g_win, w_out=g_wout, w_mlp_in=g_w1, w_mlp_out=g_w2)
    small = dict(attn_norm=g_attn_norm.reshape(-1), sgu_w=g_sgu_w, sgu_b=g_sgu_b, conv_w=g_conv,
                 q_norm=g_q, k_norm=g_k, mlp_norm=g_mlp_norm.reshape(-1))
    return dx0, dx0b, big, small


BIG = ("w_in", "w_out", "w_mlp_in", "w_mlp_out")
SMALL_REPLICATED = ("attn_norm", "sgu_w", "sgu_b", "q_norm", "k_norm", "mlp_norm")


def _local_step(x, target, prm, wg, n_layers, sink):
    saved = []
    h = x
    for l in range(n_layers):
        h, sv = _layer_forward(l, h, prm, wg)
        saved.append(sv)
    dy, dyb, colsq = _loss_kernel(h, target)
    loss = 0.5 * jnp.sum(colsq) / x.shape[1]
    bigs, smalls = [None] * n_layers, [None] * n_layers
    for l in reversed(range(n_layers)):
        dy, dyb, bigs[l], smalls[l] = _layer_backward(l, dy, dyb, saved[l], prm, wg, sink)
    return loss, dy, bigs, smalls


def _prepare_params(attn_norm, sgu_w, sgu_b, conv_full, q_norm, k_norm, mlp_norm):
    n_layers = attn_norm.shape[0]
    tri = jnp.tril(sgu_w)
    idx = jnp.arange(PW)
    bd = (idx[:, None] // HEAD_DIM == idx[None, :] // HEAD_DIM).astype(BF16)
    return dict(
        attn_norm=[attn_norm[l][None, :] for l in range(n_layers)],
        mlp_norm=[mlp_norm[l][None, :] for l in range(n_layers)],
        sgu_wt=[tri[l].astype(BF16) for l in range(n_layers)],
        sgu_wtt=[tri[l].transpose(0, 2, 1).astype(BF16) for l in range(n_layers)],
        sgu_bb=[jnp.repeat(sgu_b[l].T, HEAD_DIM, axis=1) for l in range(n_layers)],
        conv_w=[conv_full[l] for l in range(n_layers)],
        q_gain=[jnp.tile(q_norm[l], PW // HEAD_DIM)[None, :] for l in range(n_layers)],
        k_gain=[jnp.tile(k_norm[l], PW // HEAD_DIM)[None, :] for l in range(n_layers)],
        bd=bd,
    )


def kernel(x, attn_norm, w_in, sgu_w, sgu_b, conv_w, q_norm, k_norm, w_out, mlp_norm, w_mlp_in, w_mlp_out, loss_target, m_attn_norm, m_w_in, m_sgu_w, m_sgu_b, m_conv_w, m_q_norm, m_k_norm, m_w_out, m_mlp_norm, m_w_mlp_in, m_w_mlp_out, v_attn_norm, v_w_in, v_sgu_w, v_sgu_b, v_conv_w, v_q_norm, v_k_norm, v_w_out, v_mlp_norm, v_w_mlp_in, v_w_mlp_out):
    n_layers = attn_norm.shape[0]
    weights = dict(attn_norm=attn_norm, w_in=w_in, sgu_w=sgu_w, sgu_b=sgu_b, conv_w=conv_w, q_norm=q_norm,
                   k_norm=k_norm, w_out=w_out, mlp_norm=mlp_norm, w_mlp_in=w_mlp_in, w_mlp_out=w_mlp_out)
    mom_m = dict(attn_norm=m_attn_norm, w_in=m_w_in, sgu_w=m_sgu_w, sgu_b=m_sgu_b, conv_w=m_conv_w,
                 q_norm=m_q_norm, k_norm=m_k_norm, w_out=m_w_out, mlp_norm=m_mlp_norm, w_mlp_in=m_w_mlp_in,
                 w_mlp_out=m_w_mlp_out)
    mom_v = dict(attn_norm=v_attn_norm, w_in=v_w_in, sgu_w=v_sgu_w, sgu_b=v_sgu_b, conv_w=v_conv_w,
                 q_norm=v_q_norm, k_norm=v_k_norm, w_out=v_w_out, mlp_norm=v_mlp_norm, w_mlp_in=v_w_mlp_in,
                 w_mlp_out=v_w_mlp_out)
    order = ("attn_norm", "w_in", "sgu_w", "sgu_b", "conv_w", "q_norm", "k_norm", "w_out", "mlp_norm",
             "w_mlp_in", "w_mlp_out")
    chip = 2 * lax.axis_index("x") + lax.axis_index("y")
    c_arr = jnp.stack([lax.axis_index("c"), chip]).astype(jnp.int32)

    conv_cols = conv_w.shape[-1]
    chip_arr = chip.astype(jnp.int32).reshape(1)
    conv_pack = jnp.pad(conv_w.reshape(-1), (0, 2048 - conv_w.size)).reshape(1, 16, 128)
    wg = _GatheredWeights()
    wg.start([("conv_w", 0), ("w_in", 0)],
             [_place_shard("place_conv_w", conv_pack, 0, chip_arr, F32),
              _place_shard("place_w_in_0", weights["w_in"], 0, chip_arr, BF16)])
    keys = [(n, l) for l in range(n_layers) for n in BIG if (n, l) != ("w_in", 0)]
    first = wg.deps()
    wg.start(keys, [_place_shard(f"place_{n}_{l}", weights[n], l, chip_arr, BF16, deps=first) for n, l in keys])
    conv_full = wg.get("conv_w", 0, wg.deps()[-1]).reshape(N_CHIPS, 2048)[:, :conv_w.size].reshape(N_CHIPS, n_layers, 3, conv_cols)
    conv_full = conv_full.transpose(1, 2, 0, 3).reshape(n_layers, 3, N_CHIPS * conv_cols)
    prm = _prepare_params(attn_norm, sgu_w, sgu_b, conv_full, q_norm, k_norm, mlp_norm)

    sink = _GradReducer(c_arr)
    loss_local, grad_x, _, smalls = _local_step(x[0], loss_target[0], prm, wg, n_layers, sink)
    loss = lax.psum(loss_local, ("x", "y", "c"))

    small_names = SMALL_REPLICATED + ("conv_w",)
    small_shapes = [(n_layers,) + tuple(smalls[0][n].shape) for n in small_names]
    packed = _pack_rows([jnp.stack([smalls[l][n] for l in range(n_layers)]) for n in small_names])
    small_send, small_recv, small_land, small_token = _small_start(packed, sink.deps())

    grads, delta, new_m, new_v = {}, {}, {}, {}

    def update(n, after):
        shp = weights[n].shape
        two_d = (shp[0] * shp[1], shp[2])
        d, nm, nv, g = _adamw(f"adamw_{n}", weights[n].reshape(two_d), sink.reduced(n, after).reshape(two_d),
                              mom_m[n].reshape(two_d), mom_v[n].reshape(two_d))
        grads[n], delta[n], new_m[n], new_v[n] = g.reshape(shp), d.reshape(shp), nm.reshape(shp), nv.reshape(shp)

    token = small_token
    for n in ("w_mlp_out", "w_mlp_in", "w_out"):
        token = sink.reduce(n, n_layers, token)
    update("w_mlp_out", token)
    token = sink.reduce("w_in", n_layers, delta["w_mlp_out"])
    update("w_mlp_in", token)
    update("w_out", delta["w_mlp_in"])
    update("w_in", delta["w_out"])
    small_land = _small_wait(packed, small_land, small_send, small_recv, delta["w_in"])
    grads.update(zip(small_names, _unpack_rows(_sum_devices(small_land), small_shapes)))
    grads["conv_w"] = lax.dynamic_slice_in_dim(grads["conv_w"], chip * conv_cols, conv_cols, axis=2)
    smalls_all = SMALL_REPLICATED + ("conv_w",)
    shapes = [weights[n].shape for n in smalls_all]
    d, nm, nv, _ = _adamw("adamw_small",
                          _pack_rows([weights[n] for n in smalls_all]), _pack_rows([grads[n] for n in smalls_all]),
                          _pack_rows([mom_m[n] for n in smalls_all]), _pack_rows([mom_v[n] for n in smalls_all]))
    for n, dd, mm, vv in zip(smalls_all, _unpack_rows(d, shapes), _unpack_rows(nm, shapes), _unpack_rows(nv, shapes)):
        delta[n], new_m[n], new_v[n] = dd, mm, vv

    return (loss, grad_x[None], *[grads[n] for n in order], *[delta[n] for n in order],
            *[new_m[n] for n in order], *[new_v[n] for n in order])
```

```python
import jax
import jax.numpy as jnp
from jax import lax
from jax.experimental import pallas as pl
from jax.experimental.pallas import tpu as pltpu

F32 = jnp.float32
BF16 = jnp.bfloat16
SDS = jax.ShapeDtypeStruct

EPS = 1e-6
HEAD_DIM = 64
A_HEADS = 8
A_WIDTH = 512
CHUNK = 128
B_WIDTH = 768
C_WIDTH = 768
N_PATTERNS = 3
PATTERN_DILATION = (1, 4, 16)
PW = 256
D_IN_PROJ = 5632
OFF_AU, OFF_AV, OFF_BB, OFF_BC, OFF_BX, OFF_Q, OFF_K, OFF_V = 0, 512, 1024, 1792, 2560, 3328, 4096, 4864
N_CHIPS = 4
N_DEV = 8
BLK = 128

ADAM_LR, ADAM_B1, ADAM_B2, ADAM_EPS, ADAM_WD, ADAM_STEP = 0.001, 0.9, 0.999, 1e-08, 0.01, 10

V7X_VMEM_LIMIT = 56 * 1024 * 1024
MESH = pl.DeviceIdType.MESH
NEG = -1e30


def _cp(n_axes):
    return pltpu.CompilerParams(dimension_semantics=("arbitrary",) * n_axes, vmem_limit_bytes=V7X_VMEM_LIMIT)


def _hbm_spec():
    return pl.BlockSpec(memory_space=pl.ANY)


def _relu(t):
    return jnp.maximum(t, 0.0)


def _square(t):
    return t * t


def _matmul(name, a, b, out_shape, out_dtype, *, grid, a_spec, b_spec, o_spec, contract, acc_shape,
            extras=(), extra_specs=(), a_pre=None, epi=None, deps=()):
    nk = grid[2]
    n_ex = len(extras)
    n_dep = len(deps)
    dims = (((contract[0],), (contract[1],)), ((), ()))

    def product(a_ref, b_ref):
        av = a_ref[...] if a_pre is None else a_pre(a_ref[...])
        return lax.dot_general(av, b_ref[...], dims, preferred_element_type=F32)

    def finish(r, ex, o_ref):
        if epi is not None:
            r = epi(r, *[e[...] for e in ex])
        o_ref[...] = r.astype(o_ref.dtype)

    def body_single(a_ref, b_ref, *rest):
        finish(product(a_ref, b_ref), rest[:n_ex], rest[n_ex + n_dep])

    def body(a_ref, b_ref, *rest):
        ex = rest[:n_ex]
        o_ref = rest[n_ex + n_dep]
        acc_ref = rest[n_ex + n_dep + 1]
        k = pl.program_id(2)

        @pl.when(k == 0)
        def _():
            acc_ref[...] = product(a_ref, b_ref)

        @pl.when((k > 0) & (k < nk - 1))
        def _():
            acc_ref[...] += product(a_ref, b_ref)

        @pl.when(k == nk - 1)
        def _():
            finish(acc_ref[...] + product(a_ref, b_ref), ex, o_ref)

    return pl.pallas_call(
        body_single if nk == 1 else body, name=name, grid=grid,
        in_specs=[a_spec, b_spec, *extra_specs] + [_hbm_spec()] * n_dep,
        out_specs=o_spec,
        out_shape=SDS(out_shape, out_dtype),
        scratch_shapes=[] if nk == 1 else [pltpu.VMEM(acc_shape, F32)],
        compiler_params=_cp(3),
    )(a, b, *extras, *deps)


def _loss_kernel(y, t):
    S, D = y.shape
    tm = min(512, S)

    def body(y_ref, t_ref, dy_ref, dyb_ref, l_ref):
        @pl.when(pl.program_id(0) == 0)
        def _():
            l_ref[...] = jnp.zeros_like(l_ref)
        e = y_ref[...] - t_ref[...]
        l_ref[...] += jnp.sum(e * e, axis=0, keepdims=True)
        dy = e * (1.0 / D)
        dy_ref[...] = dy
        dyb_ref[...] = dy.astype(BF16)

    row = pl.BlockSpec((tm, D), lambda i: (i, 0))
    return pl.pallas_call(
        body, name="loss_head", grid=(S // tm,),
        in_specs=[row, row],
        out_specs=[row, row, pl.BlockSpec((1, D), lambda i: (0, 0))],
        out_shape=[SDS((S, D), F32), SDS((S, D), BF16), SDS((1, D), F32)],
        compiler_params=_cp(1),
    )(y, t)


def _rmsnorm_fwd(name, x, g):
    S, D = x.shape
    tm = min(1024, S)

    def body(x_ref, g_ref, h_ref):
        xv = x_ref[...]
        y = xv * lax.rsqrt(jnp.mean(xv * xv, axis=-1, keepdims=True) + EPS) * g_ref[...]
        h_ref[...] = y.astype(h_ref.dtype)

    row = pl.BlockSpec((tm, D), lambda i: (i, 0))
    return pl.pallas_call(
        body, name=name, grid=(S // tm,),
        in_specs=[row, pl.BlockSpec((1, D), lambda i: (0, 0))],
        out_specs=row,
        out_shape=SDS((S, D), BF16),
        compiler_params=_cp(1),
    )(x, g)


def _rmsnorm_bwd(name, dh, x, g, dres, deps=()):
    S, D = x.shape
    tm = min(256, S)
    n_dep = len(deps)

    def body(dh_ref, x_ref, g_ref, dres_ref, *rest):
        dx_ref, dxb_ref, dg_ref = rest[n_dep:]
        @pl.when(pl.program_id(0) == 0)
        def _():
            dg_ref[...] = jnp.zeros_like(dg_ref)
        xv = x_ref[...]
        dhv = dh_ref[...]
        rstd = lax.rsqrt(jnp.mean(xv * xv, axis=-1, keepdims=True) + EPS)
        xhat = xv * rstd
        dg_ref[...] += jnp.sum(dhv * xhat, axis=0, keepdims=True)
        dxn = dhv * g_ref[...]
        dx = dres_ref[...] + rstd * (dxn - xhat * jnp.mean(dxn * xhat, axis=-1, keepdims=True))
        dx_ref[...] = dx
        dxb_ref[...] = dx.astype(BF16)

    row = pl.BlockSpec((tm, D), lambda i: (i, 0))
    vec = pl.BlockSpec((1, D), lambda i: (0, 0))
    return pl.pallas_call(
        body, name=name, grid=(S // tm,),
        in_specs=[row, row, vec, row] + [_hbm_spec()] * n_dep,
        out_specs=[row, row, vec],
        out_shape=[SDS((S, D), F32), SDS((S, D), BF16), SDS((1, D), F32)],
        compiler_params=_cp(1),
    )(dh, x, g, dres, *deps)


def _adamw(name, w, g, m, v):
    R, C = w.shape
    tr = 256 if R % 256 == 0 else R
    c1 = 1.0 - ADAM_B1 ** ADAM_STEP
    c2 = 1.0 - ADAM_B2 ** ADAM_STEP

    def body(w_ref, g_ref, m_ref, v_ref, d_ref, nm_ref, nv_ref, g_out_ref):
        gv = g_ref[...]
        nm = ADAM_B1 * m_ref[...] + (1.0 - ADAM_B1) * gv
        nv = ADAM_B2 * v_ref[...] + (1.0 - ADAM_B2) * (gv * gv)
        m_hat = nm / c1
        v_hat = nv / c2
        d_ref[...] = -ADAM_LR * (m_hat / (jnp.sqrt(v_hat) + ADAM_EPS) + ADAM_WD * w_ref[...])
        nm_ref[...] = nm
        nv_ref[...] = nv
        g_out_ref[...] = gv

    blk = pl.BlockSpec((tr, C), lambda i: (i, 0))
    return pl.pallas_call(
        body, name=name, grid=(R // tr,),
        in_specs=[blk] * 4, out_specs=[blk] * 4,
        out_shape=[SDS((R, C), F32)] * 4,
        compiler_params=_cp(1),
    )(w, g, m, v)


SGU_STEP_ROWS = 1024


def _pair_select(lane, lo, hi):
    return jnp.where(lane < HEAD_DIM, lo, hi)


def _sgu_fwd(name, p, wt, bb):
    S = p.shape[0]

    rows = min(SGU_STEP_ROWS, S)

    def body(u_ref, v_ref, wt_ref, bb_ref, o_ref):
        lane = lax.broadcasted_iota(jnp.int32, (CHUNK, 128), 1)
        for ci in range(rows // CHUNK):
            rs = slice(CHUNK * ci, CHUNK * (ci + 1))
            for pp in range(A_HEADS // 2):
                cs = slice(128 * pp, 128 * (pp + 1))
                vb = v_ref[rs, cs].astype(BF16)
                mixed = _pair_select(lane,
                                     jnp.dot(wt_ref[2 * pp], vb, preferred_element_type=F32),
                                     jnp.dot(wt_ref[2 * pp + 1], vb, preferred_element_type=F32)) + bb_ref[:, cs]
                o_ref[rs, cs] = (u_ref[rs, cs] * mixed).astype(o_ref.dtype)

    return pl.pallas_call(
        body, name=name, grid=(S // rows,),
        in_specs=[pl.BlockSpec((rows, A_WIDTH), lambda c: (c, OFF_AU // A_WIDTH)),
                  pl.BlockSpec((rows, A_WIDTH), lambda c: (c, OFF_AV // A_WIDTH)),
                  pl.BlockSpec((A_HEADS, CHUNK, CHUNK), lambda c: (0, 0, 0)),
                  pl.BlockSpec((CHUNK, A_WIDTH), lambda c: (0, 0))],
        out_specs=pl.BlockSpec((rows, A_WIDTH), lambda c: (c, 0)),
        out_shape=SDS((S, A_WIDTH), BF16),
        compiler_params=_cp(1),
    )(p, p, wt, bb)


def _sgu_bwd(name, p, dycat, wt, wtt, bb):
    S = p.shape[0]
    rows = min(SGU_STEP_ROWS, S)

    def body(u_ref, v_ref, dy_ref, wt_ref, wtt_ref, bb_ref, du_ref, dv_ref, dw_ref, db_ref, dbacc_ref):
        c = pl.program_id(0)

        @pl.when(c == 0)
        def _():
            dw_ref[...] = jnp.zeros_like(dw_ref)
            dbacc_ref[...] = jnp.zeros_like(dbacc_ref)

        lane = lax.broadcasted_iota(jnp.int32, (CHUNK, 128), 1)
        row = lax.broadcasted_iota(jnp.int32, (CHUNK, 128), 0)
        causal = row >= lane
        nt = (((1,), (1,)), ((), ()))
        for pp in range(A_HEADS // 2):
            cs = slice(128 * pp, 128 * (pp + 1))
            dw_lo = jnp.zeros((CHUNK, CHUNK), F32)
            dw_hi = jnp.zeros((CHUNK, CHUNK), F32)
            dm_sum = jnp.zeros((CHUNK, 128), F32)
            for ci in range(rows // CHUNK):
                rs = slice(CHUNK * ci, CHUNK * (ci + 1))
                vb = v_ref[rs, cs].astype(BF16)
                dy = dy_ref[rs, cs]
                mixed = _pair_select(lane,
                                     jnp.dot(wt_ref[2 * pp], vb, preferred_element_type=F32),
                                     jnp.dot(wt_ref[2 * pp + 1], vb, preferred_element_type=F32)) + bb_ref[:, cs]
                du_ref[rs, cs] = (dy * mixed).astype(du_ref.dtype)
                dm = dy * u_ref[rs, cs]
                dmb = dm.astype(BF16)
                dv = _pair_select(lane,
                                  jnp.dot(wtt_ref[2 * pp], dmb, preferred_element_type=F32),
                                  jnp.dot(wtt_ref[2 * pp + 1], dmb, preferred_element_type=F32))
                dv_ref[rs, cs] = dv.astype(dv_ref.dtype)
                dm_sum += dm
                dm_lo = jnp.where(lane < HEAD_DIM, dm, 0.0).astype(BF16)
                dm_hi = jnp.where(lane >= HEAD_DIM, dm, 0.0).astype(BF16)
                dw_lo += lax.dot_general(dm_lo, vb, nt, preferred_element_type=F32)
                dw_hi += lax.dot_general(dm_hi, vb, nt, preferred_element_type=F32)
            dbacc_ref[:, cs] += dm_sum
            dw_ref[2 * pp] += jnp.where(causal, dw_lo, 0.0)
            dw_ref[2 * pp + 1] += jnp.where(causal, dw_hi, 0.0)

        @pl.when(c == S // rows - 1)
        def _():
            out = jnp.zeros((CHUNK, 128), F32)
            for pp in range(A_HEADS // 2):
                acc = dbacc_ref[:, 128 * pp:128 * (pp + 1)]
                s_lo = jnp.sum(jnp.where(lane < HEAD_DIM, acc, 0.0), axis=1, keepdims=True)
                s_hi = jnp.sum(jnp.where(lane >= HEAD_DIM, acc, 0.0), axis=1, keepdims=True)
                out = jnp.where(lane == 2 * pp, s_lo, out)
                out = jnp.where(lane == 2 * pp + 1, s_hi, out)
            db_ref[...] = out

    chunk = lambda col: pl.BlockSpec((rows, A_WIDTH), lambda c: (c, col))
    wspec = pl.BlockSpec((A_HEADS, CHUNK, CHUNK), lambda c: (0, 0, 0))
    return pl.pallas_call(
        body, name=name, grid=(S // rows,),
        in_specs=[chunk(OFF_AU // A_WIDTH), chunk(OFF_AV // A_WIDTH), chunk(0), wspec, wspec,
                  pl.BlockSpec((CHUNK, A_WIDTH), lambda c: (0, 0))],
        out_specs=[chunk(0), chunk(0), wspec, pl.BlockSpec((CHUNK, 128), lambda c: (0, 0))],
        out_shape=[SDS((S, A_WIDTH), BF16), SDS((S, A_WIDTH), BF16),
                   SDS((A_HEADS, CHUNK, CHUNK), F32), SDS((CHUNK, 128), F32)],
        scratch_shapes=[pltpu.VMEM((CHUNK, A_WIDTH), F32)],
        compiler_params=_cp(1),
    )(p, p, dycat, wt, wtt, bb)


CONV_HALO = 8
CONV_COLS = 256
CONV_ROWS = 2048


def _shift_down(a, halo, k):
    T = a.shape[0]
    row = lax.broadcasted_iota(jnp.int32, a.shape, 0)
    out = pltpu.roll(a, k, 0)
    for r in range(k):
        out = jnp.where(row == r, halo[CONV_HALO - k + r:CONV_HALO - k + r + 1, :], out)
    return out


def _shift_up(a, halo, k):
    T = a.shape[0]
    row = lax.broadcasted_iota(jnp.int32, a.shape, 0)
    out = pltpu.roll(a, T - k, 0)
    for r in range(k):
        out = jnp.where(row == T - k + r, halo[r:r + 1, :], out)
    return out


def _conv_specs(S, T):
    hb = T // CONV_HALO
    last = S // CONV_HALO - 1
    tile = lambda col0: pl.BlockSpec((T, CONV_COLS), lambda j, i: (i, col0 + j))
    prev = lambda col0: pl.BlockSpec((CONV_HALO, CONV_COLS), lambda j, i: (jnp.maximum(i * hb - 1, 0), col0 + j))
    nxt = lambda col0: pl.BlockSpec((CONV_HALO, CONV_COLS), lambda j, i: (jnp.minimum((i + 1) * hb, last), col0 + j))
    return tile, prev, nxt


def _conv_fwd(name, p, w):
    S = p.shape[0]
    T = min(CONV_ROWS, S)
    tile, prev, _ = _conv_specs(S, T)
    cb, cc, cx = OFF_BB // CONV_COLS, OFF_BC // CONV_COLS, OFF_BX // CONV_COLS

    def body(b_ref, c_ref, x_ref, ch_ref, xh_ref, w_ref, o_ref):
        i = pl.program_id(1)
        z = c_ref[...] * x_ref[...]
        zh = jnp.where(i > 0, ch_ref[...] * xh_ref[...], 0.0)
        z1 = _shift_down(z, zh, 1)
        z2 = _shift_down(z, zh, 2)
        conv = w_ref[0:1, :] * z2 + w_ref[1:2, :] * z1 + w_ref[2:3, :] * z
        o_ref[...] = (b_ref[...] * conv).astype(o_ref.dtype)

    return pl.pallas_call(
        body, name=name, grid=(B_WIDTH // CONV_COLS, S // T),
        in_specs=[tile(cb), tile(cc), tile(cx), prev(cc), prev(cx),
                  pl.BlockSpec((3, CONV_COLS), lambda j, i: (0, j))],
        out_specs=tile(0),
        out_shape=SDS((S, B_WIDTH), BF16),
        compiler_params=_cp(2),
    )(p, p, p, p, p, w)


def _conv_bwd(name, p, dycat, w):
    S = p.shape[0]
    T = min(CONV_ROWS, S)
    tile, prev, nxt = _conv_specs(S, T)
    cb, cc, cx = OFF_BB // CONV_COLS, OFF_BC // CONV_COLS, OFF_BX // CONV_COLS
    cdy = A_WIDTH // CONV_COLS
    n_i = S // T

    def body(b_ref, c_ref, x_ref, dy_ref, ch_ref, xh_ref, bn_ref, dyn_ref, w_ref,
             db_ref, dc_ref, dx_ref, dw_ref):
        i = pl.program_id(1)

        @pl.when(i == 0)
        def _():
            dw_ref[...] = jnp.zeros_like(dw_ref)

        cv = c_ref[...]
        xv = x_ref[...]
        z = cv * xv
        zh = jnp.where(i > 0, ch_ref[...] * xh_ref[...], 0.0)
        z1 = _shift_down(z, zh, 1)
        z2 = _shift_down(z, zh, 2)
        w0, w1, w2 = w_ref[0:1, :], w_ref[1:2, :], w_ref[2:3, :]
        conv = w0 * z2 + w1 * z1 + w2 * z
        dy = dy_ref[...]
        db_ref[...] = (dy * conv).astype(db_ref.dtype)
        dconv = dy * b_ref[...]
        dconv_n = jnp.where(i < n_i - 1, dyn_ref[...] * bn_ref[...], 0.0)
        dz = w2 * dconv + w1 * _shift_up(dconv, dconv_n, 1) + w0 * _shift_up(dconv, dconv_n, 2)
        dc_ref[...] = (dz * xv).astype(dc_ref.dtype)
        dx_ref[...] = (dz * cv).astype(dx_ref.dtype)
        dw_ref[0:1, :] += jnp.sum(dconv * z2, axis=0, keepdims=True)
        dw_ref[1:2, :] += jnp.sum(dconv * z1, axis=0, keepdims=True)
        dw_ref[2:3, :] += jnp.sum(dconv * z, axis=0, keepdims=True)

    wspec = pl.BlockSpec((3, CONV_COLS), lambda j, i: (0, j))
    return pl.pallas_call(
        body, name=name, grid=(B_WIDTH // CONV_COLS, n_i),
        in_specs=[tile(cb), tile(cc), tile(cx), tile(cdy), prev(cc), prev(cx), nxt(cb), nxt(cdy), wspec],
        out_specs=[tile(0), tile(0), tile(0), wspec],
        out_shape=[SDS((S, B_WIDTH), BF16)] * 3 + [SDS((3, B_WIDTH), F32)],
        compiler_params=_cp(2),
    )(p, p, p, dycat, p, p, p, dycat, w)


def _seg_sum(t, bd):
    hi = t.astype(BF16)
    lo = (t - hi.astype(F32)).astype(BF16)
    return jnp.dot(hi, bd, preferred_element_type=F32) + jnp.dot(lo, bd, preferred_element_type=F32)


def _head_norm(x, g, bd):
    rstd = lax.rsqrt(_seg_sum(x * x, bd) * (1.0 / HEAD_DIM) + EPS)
    xhat = x * rstd
    return xhat * g, xhat, rstd


def _head_norm_bwd(dy, g, xhat, rstd, bd):
    dxh = dy * g
    return rstd * (dxh - xhat * (_seg_sum(dxh * xhat, bd) * (1.0 / HEAD_DIM)))


def _band_mask(has_prev):
    row = lax.broadcasted_iota(jnp.int32, (BLK, 2 * BLK), 0)
    col = lax.broadcasted_iota(jnp.int32, (BLK, 2 * BLK), 1)
    first_key = jnp.where(has_prev, 0, BLK)
    return (col >= row) & (col <= row + BLK) & (col >= first_key)


def _residue_rows(r, d):
    return slice(None) if d == 1 else pl.ds(r, BLK, stride=d)


STRIDED_LANES = 128


def _step_width(d):
    return PW if d == 1 else STRIDED_LANES


def _n_stack(lane):
    return lane.shape[1] // HEAD_DIM


def _for_residues(d, fn):
    if d == 1:
        fn(0)
    else:
        def two(i, carry):
            fn(2 * i)
            fn(2 * i + 1)
            return carry
        lax.fori_loop(0, d // 2, two, 0)


def _head_mask(lane, j):
    return (lane >= HEAD_DIM * j) & (lane < HEAD_DIM * (j + 1))


def _stack_heads(x, lane):
    return jnp.concatenate([jnp.where(_head_mask(lane, j), x, 0.0) for j in range(_n_stack(lane))], axis=0)


def _unstack_heads(y, lane):
    out = y[:BLK]
    for j in range(1, _n_stack(lane)):
        out = jnp.where(lane >= HEAD_DIM * j, y[BLK * j:BLK * (j + 1)], out)
    return out


def _head_columns(v, lane):
    return jnp.concatenate([jnp.max(jnp.where(_head_mask(lane, j), v, NEG), axis=1, keepdims=True)
                            for j in range(_n_stack(lane))], axis=0)


def _attn_fwd(name, p, g, gq, gk, bd):
    S = p.shape[0]
    d = PATTERN_DILATION[g]
    rows = BLK * d
    hw = _step_width(d)
    nt = (((1,), (1,)), ((), ()))

    def body(q_ref, kc_ref, kp_ref, vc_ref, vp_ref, gq_ref, gk_ref, bd_ref, o_ref, lse_ref):
        has_prev = pl.program_id(1) > 0
        bdv = bd_ref[...]
        band = jnp.concatenate([_band_mask(has_prev)] * (hw // HEAD_DIM), axis=0)
        lane = lax.broadcasted_iota(jnp.int32, (1, hw), 1)

        def residue(r):
            rr = _residue_rows(r, d)
            qn, _, _ = _head_norm(q_ref[rr, :], gq_ref[...], bdv)
            kn, _, _ = _head_norm(jnp.concatenate([kp_ref[rr, :], kc_ref[rr, :]], axis=0), gk_ref[...], bdv)
            knb = kn.astype(BF16)
            vb = jnp.concatenate([vp_ref[rr, :], vc_ref[rr, :]], axis=0).astype(BF16)
            qs = _stack_heads(qn, lane).astype(BF16)
            s = lax.dot_general(qs, knb, nt, preferred_element_type=F32) * (HEAD_DIM ** -0.5)
            s = jnp.where(band, s, NEG)
            m = jnp.max(s, axis=1, keepdims=True)
            e = jnp.exp(s - m)
            den = jnp.sum(e, axis=1, keepdims=True)
            pv = jnp.dot(e.astype(BF16), vb, preferred_element_type=F32)
            o_ref[rr, :] = _unstack_heads(pv / den, lane)
            lse_ref[rr, :] = _unstack_heads(jnp.broadcast_to(m + jnp.log(den), pv.shape), lane)

        _for_residues(d, residue)

    per = PW // hw
    cq, ck, cv = (OFF_Q + PW * g) // hw, (OFF_K + PW * g) // hw, (OFF_V + PW * g) // hw
    cur = lambda col: pl.BlockSpec((rows, hw), lambda h, n: (n, col + h))
    prv = lambda col: pl.BlockSpec((rows, hw), lambda h, n: (jnp.maximum(n - 1, 0), col + h))
    vec = pl.BlockSpec((1, hw), lambda h, n: (0, h))
    return pl.pallas_call(
        body, name=name, grid=(per, S // rows),
        in_specs=[cur(cq), cur(ck), prv(ck), cur(cv), prv(cv), vec, vec, pl.BlockSpec((hw, hw), lambda h, n: (0, 0))],
        out_specs=[cur(0), cur(0)],
        out_shape=[SDS((S, PW), F32)] * 2,
        compiler_params=_cp(2),
    )(p, p, p, p, p, gq, gk, bd)


def _attn_bwd(name, p, g, lse, do3, c3, gq, gk, bd):
    S = p.shape[0]
    d = PATTERN_DILATION[g]
    rows = BLK * d
    nblk = S // rows
    hw = _step_width(d)
    nt = (((1,), (1,)), ((), ()))
    tn = (((0,), (0,)), ((), ()))

    def body(q_ref, kc_ref, kp_ref, vc_ref, vp_ref, lse_ref, do_ref, c_ref, gq_ref, gk_ref, bd_ref,
             dq_ref, dk_ref, dv_ref, dgq_ref, dgk_ref, ck_ref, cv_ref, dq_keep_ref):
        n = pl.program_id(1)

        @pl.when(n == 0)
        def _():
            ck_ref[...] = jnp.zeros_like(ck_ref)
            cv_ref[...] = jnp.zeros_like(cv_ref)
            dgq_ref[...] = jnp.zeros_like(dgq_ref)
            dgk_ref[...] = jnp.zeros_like(dgk_ref)

        @pl.when(n == nblk)
        def _():
            dq_ref[...] = dq_keep_ref[...]
            dk_ref[...] = ck_ref[...]
            dv_ref[...] = cv_ref[...]

        bdv = bd_ref[...]
        gqv = gq_ref[...]
        gkv = gk_ref[...]
        band = jnp.concatenate([_band_mask(n > 0)] * (hw // HEAD_DIM), axis=0)
        lane = lax.broadcasted_iota(jnp.int32, (1, hw), 1)

        def residue(r):
            rr = _residue_rows(r, d)
            qn, qhat, qrstd = _head_norm(q_ref[rr, :], gqv, bdv)
            kn, khat, krstd = _head_norm(jnp.concatenate([kp_ref[rr, :], kc_ref[rr, :]], axis=0), gkv, bdv)
            knb = kn.astype(BF16)
            vb = jnp.concatenate([vp_ref[rr, :], vc_ref[rr, :]], axis=0).astype(BF16)
            qs = _stack_heads(qn, lane).astype(BF16)
            dos = _stack_heads(do_ref[rr, :], lane).astype(BF16)
            s = lax.dot_general(qs, knb, nt, preferred_element_type=F32) * (HEAD_DIM ** -0.5)
            prob = jnp.where(band, jnp.exp(s - _head_columns(lse_ref[rr, :], lane)), 0.0)
            dp = lax.dot_general(dos, vb, nt, preferred_element_type=F32)
            ds = (prob * (dp + _head_columns(c_ref[rr, :], lane)) * (HEAD_DIM ** -0.5)).astype(BF16)
            dqn = _unstack_heads(jnp.dot(ds, knb, preferred_element_type=F32), lane)
            dkn = lax.dot_general(ds, qs, tn, preferred_element_type=F32)
            dvv = lax.dot_general(prob.astype(BF16), dos, tn, preferred_element_type=F32)

            dq = _head_norm_bwd(dqn, gqv, qhat, qrstd, bdv)
            dq_ref[rr, :] = dq
            dq_keep_ref[rr, :] = dq
            dk2 = _head_norm_bwd(dkn, gkv, khat, krstd, bdv)
            dgq_ref[...] += jnp.sum(dqn * qhat, axis=0, keepdims=True)
            dgk_ref[...] += jnp.sum(dkn * khat, axis=0, keepdims=True)
            dk_ref[rr, :] = ck_ref[rr, :] + dk2[:BLK]
            dv_ref[rr, :] = cv_ref[rr, :] + dvv[:BLK]
            ck_ref[rr, :] = dk2[BLK:]
            cv_ref[rr, :] = dvv[BLK:]

        @pl.when(n < nblk)
        def _():
            _for_residues(d, residue)

    last = nblk - 1
    per = PW // hw
    cq, ck, cv = (OFF_Q + PW * g) // hw, (OFF_K + PW * g) // hw, (OFF_V + PW * g) // hw
    cur = lambda col: pl.BlockSpec((rows, hw), lambda h, n: (jnp.minimum(n, last), col + h))
    prv = lambda col: pl.BlockSpec((rows, hw), lambda h, n: (jnp.maximum(jnp.minimum(n, last) - 1, 0), col + h))
    cur3 = pl.BlockSpec((None, rows, hw), lambda h, n: (g, jnp.minimum(n, last), h))
    done = pl.BlockSpec((rows, hw), lambda h, n: (jnp.maximum(n - 1, 0), h))
    vec = pl.BlockSpec((1, hw), lambda h, n: (0, h))
    return pl.pallas_call(
        body, name=name, grid=(per, nblk + 1),
        in_specs=[cur(cq), cur(ck), prv(ck), cur(cv), prv(cv), cur(0), cur3, cur3, vec, vec,
                  pl.BlockSpec((hw, hw), lambda h, n: (0, 0))],
        out_specs=[cur(0), done, done, vec, vec],
        out_shape=[SDS((S, PW), F32)] * 3 + [SDS((1, PW), F32)] * 2,
        scratch_shapes=[pltpu.VMEM((rows, hw), F32)] * 3,
        compiler_params=_cp(2),
    )(p, p, p, p, p, lse, do3, c3, gq, gk, bd)


def _mix_fwd(name, os, lses):
    S = os[0].shape[0]
    tm = min(1024, S)

    def body(o0, o1, o2, l0, l1, l2, y_ref):
        o = [o0[...], o1[...], o2[...]]
        l = [l0[...], l1[...], l2[...]]
        m = jnp.maximum(jnp.maximum(l[0], l[1]), l[2])
        e = [jnp.exp(t - m) for t in l]
        inv = 1.0 / (e[0] + e[1] + e[2])
        for g in range(N_PATTERNS):
            y_ref[:, PW * g:PW * (g + 1)] = (o[g] * (e[g] * inv)).astype(y_ref.dtype)

    blk = pl.BlockSpec((tm, PW), lambda i: (i, 0))
    return pl.pallas_call(
        body, name=name, grid=(S // tm,),
        in_specs=[blk] * 6,
        out_specs=pl.BlockSpec((tm, C_WIDTH), lambda i: (i, 0)),
        out_shape=SDS((S, C_WIDTH), BF16),
        compiler_params=_cp(1),
    )(*os, *lses)


def _mix_bwd(name, os, lses, dycat, bd):
    S = os[0].shape[0]
    tm = min(1024, S)
    c0 = (A_WIDTH + B_WIDTH) // PW

    def body(o0, o1, o2, l0, l1, l2, dy0_ref, dy1_ref, dy2_ref, bd_ref, do_ref, c_ref):
        bdv = bd_ref[...]
        o = [o0[...], o1[...], o2[...]]
        l = [l0[...], l1[...], l2[...]]
        dys = [dy0_ref[...], dy1_ref[...], dy2_ref[...]]
        m = jnp.maximum(jnp.maximum(l[0], l[1]), l[2])
        e = [jnp.exp(t - m) for t in l]
        inv = 1.0 / (e[0] + e[1] + e[2])
        alpha = [t * inv for t in e]
        da = [_seg_sum(dys[g] * o[g], bdv) for g in range(N_PATTERNS)]
        mean_da = alpha[0] * da[0] + alpha[1] * da[1] + alpha[2] * da[2]
        for g in range(N_PATTERNS):
            do_ref[g] = dys[g] * alpha[g]
            c_ref[g] = -alpha[g] * mean_da

    blk = pl.BlockSpec((tm, PW), lambda i: (i, 0))
    blk3 = pl.BlockSpec((N_PATTERNS, tm, PW), lambda i: (0, i, 0))
    dyspec = lambda g: pl.BlockSpec((tm, PW), lambda i: (i, c0 + g))
    return pl.pallas_call(
        body, name=name, grid=(S // tm,),
        in_specs=[blk] * 6 + [dyspec(0), dyspec(1), dyspec(2), pl.BlockSpec((PW, PW), lambda i: (0, 0))],
        out_specs=[blk3, blk3],
        out_shape=[SDS((N_PATTERNS, S, PW), F32)] * 2,
        compiler_params=_cp(1),
    )(*os, *lses, dycat, dycat, dycat, bd)


def _mesh_pos():
    x, y, c = lax.axis_index("x"), lax.axis_index("y"), lax.axis_index("c")
    chips = [(1 - x, y), (x, 1 - y), (1 - x, 1 - y)]
    chip_idx = [2 * cx + cy for cx, cy in chips]
    return x, y, c, 2 * x + y, chips, chip_idx


def _place_shard(name, w, layer, chip_arr, out_dtype, deps=()):
    _, R, C = w.shape
    tr = min(256, R)

    def body(chip_ref, w_ref, *rest):
        o_ref = rest[-1]
        o_ref[...] = w_ref[...].astype(o_ref.dtype)

    return pl.pallas_call(
        body, name=name,
        grid_spec=pltpu.PrefetchScalarGridSpec(
            num_scalar_prefetch=1, grid=(R // tr,),
            in_specs=[pl.BlockSpec((None, tr, C), lambda i, chip_ref: (layer, i, 0))] + [_hbm_spec()] * len(deps),
            out_specs=pl.BlockSpec((None, tr, C), lambda i, chip_ref: (chip_ref[0], i, 0))),
        out_shape=SDS((N_CHIPS, R, C), out_dtype),
        compiler_params=_cp(1),
    )(chip_arr, w, *deps)


HBM_SPEC = pl.BlockSpec(memory_space=pltpu.HBM)
SEM_SPEC = pl.BlockSpec(memory_space=pltpu.SEMAPHORE)
SPLIT_COPY = pltpu.SideEffectType.DATAFLOW_SIDE_EFFECTING
N_PEER_CHIPS = N_CHIPS - 1
TOKEN_SHAPE = SDS((8, 128), F32)
TOKEN_SPEC = pl.BlockSpec(memory_space=pltpu.VMEM)


def _in_hbm(a):
    return pltpu.with_memory_space_constraint(a, pltpu.HBM)


def _gather_start(name, bufs):
    T = len(bufs)

    def body(*refs):
        ins = refs[:T]
        send_sems, recv_sems = refs[T:2 * T], refs[2 * T:3 * T]
        token = refs[4 * T]
        x, y, c, me, chips, chip_idx = _mesh_pos()
        for t in range(T):
            hr = ins[t].shape[1] // 2
            mine = ins[t].at[me, pl.ds(c * hr, hr), :]
            for j in range(N_PEER_CHIPS):
                pltpu.make_async_remote_copy(src_ref=mine, dst_ref=mine, send_sem=send_sems[t].at[j],
                                             recv_sem=recv_sems[t].at[j], device_id=(*chips[j], c),
                                             device_id_type=MESH).start()
        token[...] = jnp.zeros_like(token)

    sems = [pltpu.SemaphoreType.DMA((N_PEER_CHIPS,))] * T
    out = pl.pallas_call(
        body, name=name,
        in_specs=[HBM_SPEC] * T,
        out_specs=[SEM_SPEC] * (2 * T) + [HBM_SPEC] * T + [TOKEN_SPEC],
        out_shape=sems + sems + [pltpu.HBM(b.shape, b.dtype) for b in bufs] + [TOKEN_SHAPE],
        input_output_aliases={t: 2 * T + t for t in range(T)},
        compiler_params=pltpu.CompilerParams(has_side_effects=SPLIT_COPY),
    )(*[_in_hbm(b) for b in bufs])
    return out[:T], out[T:2 * T], out[2 * T:3 * T], out[3 * T]


def _gather_wait(name, buf, send_sem, recv_sem, after):
    n_in = 3 if after is None else 4

    def body(*refs):
        buf_ref, ssem, rsem = refs[:3]
        x, y, c, me, chips, chip_idx = _mesh_pos()
        hr = buf_ref.shape[1] // 2
        mine = buf_ref.at[me, pl.ds(c * hr, hr), :]
        for j in range(N_PEER_CHIPS):
            got = buf_ref.at[chip_idx[j], pl.ds(c * hr, hr), :]
            cp = pltpu.make_async_remote_copy(src_ref=mine, dst_ref=got, send_sem=ssem.at[j], recv_sem=rsem.at[j],
                                              device_id=(*chips[j], c), device_id_type=MESH)
            cp.wait_send()
            cp.wait_recv()

    args = [buf, send_sem, recv_sem] + ([] if after is None else [after])
    return pl.pallas_call(
        body, name=name,
        in_specs=[HBM_SPEC, SEM_SPEC, SEM_SPEC] + [_hbm_spec()] * (n_in - 3),
        out_specs=HBM_SPEC,
        out_shape=pltpu.HBM(buf.shape, buf.dtype),
        input_output_aliases={0: 0},
        compiler_params=pltpu.CompilerParams(has_side_effects=SPLIT_COPY),
    )(*args)


def _forward_start(name, buf):
    def body(buf_ref, send_sems, recv_sems, buf_thru, token):
        x, y, c, me, chips, chip_idx = _mesh_pos()
        hr = buf_ref.shape[1] // 2
        for j in range(N_PEER_CHIPS):
            got = buf_ref.at[chip_idx[j], pl.ds(c * hr, hr), :]
            pltpu.make_async_remote_copy(src_ref=got, dst_ref=got, send_sem=send_sems.at[j], recv_sem=recv_sems.at[j],
                                         device_id=(x, y, 1 - c), device_id_type=MESH).start()
        token[...] = jnp.zeros_like(token)

    sems = pltpu.SemaphoreType.DMA((N_PEER_CHIPS,))
    return pl.pallas_call(
        body, name=name,
        in_specs=[HBM_SPEC],
        out_specs=[SEM_SPEC, SEM_SPEC, HBM_SPEC, TOKEN_SPEC],
        out_shape=[sems, sems, pltpu.HBM(buf.shape, buf.dtype), TOKEN_SHAPE],
        input_output_aliases={0: 2},
        compiler_params=pltpu.CompilerParams(has_side_effects=SPLIT_COPY),
    )(_in_hbm(buf))


def _forward_wait(name, buf, send_sems, recv_sems, after):
    n_in = 3 if after is None else 4

    def body(*refs):
        buf_ref, ssems, rsems = refs[:3]
        x, y, c, me, chips, chip_idx = _mesh_pos()
        hr = buf_ref.shape[1] // 2
        for j in range(N_PEER_CHIPS):
            sent = buf_ref.at[chip_idx[j], pl.ds(c * hr, hr), :]
            theirs = buf_ref.at[chip_idx[j], pl.ds((1 - c) * hr, hr), :]
            cp = pltpu.make_async_remote_copy(src_ref=sent, dst_ref=theirs, send_sem=ssems.at[j],
                                              recv_sem=rsems.at[j], device_id=(x, y, 1 - c), device_id_type=MESH)
            cp.wait_send()
            cp.wait_recv()

    args = [buf, send_sems, recv_sems] + ([] if after is None else [after])
    return pl.pallas_call(
        body, name=name,
        in_specs=[HBM_SPEC, SEM_SPEC, SEM_SPEC] + [_hbm_spec()] * (n_in - 3),
        out_specs=HBM_SPEC,
        out_shape=pltpu.HBM(buf.shape, buf.dtype),
        input_output_aliases={0: 0},
        compiler_params=pltpu.CompilerParams(has_side_effects=SPLIT_COPY),
    )(*args)


class _GatheredWeights:
    def __init__(self):
        self._order = []
        self._pending = {}
        self._forwarding = {}
        self._ready = {}
        self._tokens = []

    def start(self, keys, bufs):
        send_sems, recv_sems, thru, token = _gather_start(f"gather_start_{len(self._order)}", bufs)
        self._tokens.append(token)
        self._order.extend(keys)
        self._pending.update({k: (b, s, r) for k, b, s, r in zip(keys, thru, send_sems, recv_sems)})

    def _prefetch(self, key, after):
        if key in self._pending:
            buf, ssem, rsem = self._pending.pop(key)
            tag = f"{key[0]}_{key[1]}"
            buf = _gather_wait(f"gather_wait_{tag}", buf, ssem, rsem, after)
            ssems, rsems, buf, token = _forward_start(f"gather_fwd_start_{tag}", buf)
            self._forwarding[key] = (buf, ssems, rsems)
            self._tokens.append(token)

    def get(self, name, layer, after=None, prefetch_next=True):
        key = (name, layer)
        if key not in self._ready:
            self._prefetch(key, after)
            buf, ssems, rsems = self._forwarding.pop(key)
            self._ready[key] = _forward_wait(f"gather_fwd_wait_{name}_{layer}", buf, ssems, rsems, after)
            if prefetch_next:
                self.prefetch_after(name, layer, after)
        return self._ready[key]

    def prefetch_after(self, name, layer, after):
        nxt = self._order.index((name, layer)) + 1
        if nxt < len(self._order):
            self._prefetch(self._order[nxt], after)

    def deps(self):
        tokens, self._tokens = self._tokens, []
        return tokens


def _swap_copy(g_ref, land_ref, send_sem, recv_sem):
    x, y, c, _, _, _ = _mesh_pos()
    hr = g_ref.shape[1] // 2
    return pltpu.make_async_remote_copy(src_ref=g_ref.at[:, pl.ds((1 - c) * hr, hr), :], dst_ref=land_ref,
                                        send_sem=send_sem, recv_sem=recv_sem, device_id=(x, y, 1 - c),
                                        device_id_type=MESH)


def _swap_start(name, g):
    land_shape = (g.shape[0], g.shape[1] // 2, g.shape[2])

    def body(g_ref, land_ref, send_sem, recv_sem, land_thru, token):
        _swap_copy(g_ref, land_ref, send_sem, recv_sem).start()
        token[...] = jnp.zeros_like(token)

    return pl.pallas_call(
        body, name=name,
        in_specs=[HBM_SPEC, HBM_SPEC],
        out_specs=[SEM_SPEC, SEM_SPEC, HBM_SPEC, TOKEN_SPEC],
        out_shape=[pltpu.SemaphoreType.DMA(()), pltpu.SemaphoreType.DMA(()), pltpu.HBM(land_shape, g.dtype),
                   TOKEN_SHAPE],
        input_output_aliases={1: 2},
        compiler_params=pltpu.CompilerParams(has_side_effects=SPLIT_COPY),
    )(_in_hbm(g), _in_hbm(lax.empty(land_shape, g.dtype)))


def _swap_wait(name, g, land, send_sem, recv_sem, after):
    def body(g_ref, land_ref, send_sem, recv_sem, after_ref, land_out):
        cp = _swap_copy(g_ref, land_ref, send_sem, recv_sem)
        cp.wait_send()
        cp.wait_recv()

    return pl.pallas_call(
        body, name=name,
        in_specs=[HBM_SPEC, HBM_SPEC, SEM_SPEC, SEM_SPEC, _hbm_spec()],
        out_specs=HBM_SPEC,
        out_shape=pltpu.HBM(land.shape, land.dtype),
        input_output_aliases={1: 0},
        compiler_params=pltpu.CompilerParams(has_side_effects=SPLIT_COPY),
    )(_in_hbm(g), land, send_sem, recv_sem, after)


def _add_my_half(name, g, r, pos_arr):
    ns, R, C = g.shape
    hr = R // 2
    tr = min(512, hr)
    nt = hr // tr

    def body(pos_ref, g_ref, r_ref, o_ref, land_ref):
        t = (g_ref[...] + r_ref[...]).astype(o_ref.dtype)
        o_ref[...] = t

        @pl.when(pl.program_id(1) == pos_ref[1])
        def _():
            land_ref[...] = t

    blk = pl.BlockSpec((None, tr, C), lambda i, s, pos_ref: (s, i, 0))
    return pl.pallas_call(
        body, name=name,
        grid_spec=pltpu.PrefetchScalarGridSpec(
            num_scalar_prefetch=1, grid=(nt, ns),
            in_specs=[pl.BlockSpec((None, tr, C), lambda i, s, pos_ref: (s, pos_ref[0] * nt + i, 0)), blk],
            out_specs=[blk, pl.BlockSpec((None, tr, C), lambda i, s, pos_ref: (pos_ref[1], i, 0))]),
        out_shape=[SDS((ns, hr, C), BF16)] * 2,
        compiler_params=_cp(2),
    )(pos_arr, g, r)


def _exchange_start(name, part, land):
    def body(part_ref, land_ref, send_sems, recv_sems, land_thru, token):
        x, y, c, me, chips, chip_idx = _mesh_pos()
        for j in range(N_PEER_CHIPS):
            pltpu.make_async_remote_copy(src_ref=part_ref.at[chip_idx[j]], dst_ref=land_ref.at[me],
                                         send_sem=send_sems.at[j], recv_sem=recv_sems.at[j],
                                         device_id=(*chips[j], c), device_id_type=MESH).start()
        token[...] = jnp.zeros_like(token)

    sems = pltpu.SemaphoreType.DMA((N_PEER_CHIPS,))
    return pl.pallas_call(
        body, name=name,
        in_specs=[HBM_SPEC, HBM_SPEC],
        out_specs=[SEM_SPEC, SEM_SPEC, HBM_SPEC, TOKEN_SPEC],
        out_shape=[sems, sems, pltpu.HBM(land.shape, land.dtype), TOKEN_SHAPE],
        input_output_aliases={1: 2},
        compiler_params=pltpu.CompilerParams(has_side_effects=SPLIT_COPY),
    )(_in_hbm(part), _in_hbm(land))


def _exchange_wait(name, part, land, send_sems, recv_sems, after):
    def body(part_ref, land_ref, send_sems, recv_sems, after_ref, land_out):
        x, y, c, me, chips, chip_idx = _mesh_pos()
        for j in range(N_PEER_CHIPS):
            cp = pltpu.make_async_remote_copy(src_ref=part_ref.at[chip_idx[j]], dst_ref=land_ref.at[chip_idx[j]],
                                              send_sem=send_sems.at[j], recv_sem=recv_sems.at[j],
                                              device_id=(*chips[j], c), device_id_type=MESH)
            cp.wait_send()
            cp.wait_recv()

    return pl.pallas_call(
        body, name=name,
        in_specs=[HBM_SPEC, HBM_SPEC, SEM_SPEC, SEM_SPEC, _hbm_spec()],
        out_specs=HBM_SPEC,
        out_shape=pltpu.HBM(land.shape, land.dtype),
        input_output_aliases={1: 0},
        compiler_params=pltpu.CompilerParams(has_side_effects=SPLIT_COPY),
    )(_in_hbm(part), land, send_sems, recv_sems, after)


class _GradReducer:
    def __init__(self, c_arr):
        self._c_arr = c_arr
        self._swapping = []
        self._exchanging = {}
        self._joining = {}
        self._tokens = []

    def begin(self, name, layer, g):
        tag = f"{name}_{layer}"
        ssem, rsem, land, token = _swap_start(f"rs_swap_start_{tag}", g)
        self._swapping.append((name, layer, g, ssem, rsem, land))
        self._tokens.append(token)

    def advance(self, after):
        for name, layer, g, ssem, rsem, land in self._swapping:
            tag = f"{name}_{layer}"
            theirs = _swap_wait(f"rs_swap_wait_{tag}", g, land, ssem, rsem, after)
            part, own = _add_my_half(f"rs_add_{tag}", g, theirs, self._c_arr)
            ssems, rsems, land2, token = _exchange_start(f"rs_xchg_start_{tag}", part, own)
            self._exchanging[(name, layer)] = (part, ssems, rsems, land2)
            self._tokens.append(token)
        self._swapping = []

    def deps(self):
        tokens, self._tokens = self._tokens, []
        return tokens

    def reduce(self, name, n_layers, after):
        buf = None
        for layer in range(n_layers):
            part, ssems, rsems, land = self._exchanging.pop((name, layer))
            tag = f"{name}_{layer}"
            landed = _exchange_wait(f"rs_xchg_wait_{tag}", part, land, ssems, rsems, after)
            buf = _sum_chips(f"rs_sum_{tag}", landed, self._c_arr, layer, n_layers, buf)
        ssem, rsem, buf, token = _join_start(f"rs_join_start_{name}", buf)
        self._joining[name] = (buf, ssem, rsem)
        return token

    def reduced(self, name, after):
        buf, ssem, rsem = self._joining.pop(name)
        return _join_wait(f"rs_join_wait_{name}", buf, ssem, rsem, after)


def _sum_chips(name, r, c_arr, layer, n_layers, prev):
    ns, H, C = r.shape
    tr = min(512, H)
    nt = H // tr

    def body(c_ref, r_ref, *rest):
        o_ref = rest[-1]
        o_ref[...] = ((r_ref[0].astype(F32) + r_ref[1].astype(F32)) + r_ref[2].astype(F32)) + r_ref[3].astype(F32)

    in_specs = [pl.BlockSpec((ns, tr, C), lambda i, c_ref: (0, i, 0))]
    args = [c_arr, r]
    aliases = {}
    if prev is not None:
        in_specs.append(_hbm_spec())
        args.append(prev)
        aliases = {2: 0}
    return pl.pallas_call(
        body, name=name,
        grid_spec=pltpu.PrefetchScalarGridSpec(
            num_scalar_prefetch=1, grid=(nt,), in_specs=in_specs,
            out_specs=pl.BlockSpec((None, tr, C), lambda i, c_ref: (layer, c_ref[0] * nt + i, 0))),
        out_shape=SDS((n_layers, 2 * H, C), F32),
        input_output_aliases=aliases,
        compiler_params=_cp(1),
    )(*args)


def _join_copy(buf_ref, send_sem, recv_sem):
    x, y, c, _, _, _ = _mesh_pos()
    hr = buf_ref.shape[1] // 2
    mine = buf_ref.at[:, pl.ds(c * hr, hr), :]
    theirs = buf_ref.at[:, pl.ds((1 - c) * hr, hr), :]
    send = pltpu.make_async_remote_copy(src_ref=mine, dst_ref=mine, send_sem=send_sem, recv_sem=recv_sem,
                                        device_id=(x, y, 1 - c), device_id_type=MESH)
    arrive = pltpu.make_async_remote_copy(src_ref=theirs, dst_ref=theirs, send_sem=send_sem, recv_sem=recv_sem,
                                          device_id=(x, y, 1 - c), device_id_type=MESH)
    return send, arrive


def _join_start(name, buf):
    def body(buf_ref, send_sem, recv_sem, buf_thru, token):
        _join_copy(buf_ref, send_sem, recv_sem)[0].start()
        token[...] = jnp.zeros_like(token)

    return pl.pallas_call(
        body, name=name,
        in_specs=[HBM_SPEC],
        out_specs=[SEM_SPEC, SEM_SPEC, HBM_SPEC, TOKEN_SPEC],
        out_shape=[pltpu.SemaphoreType.DMA(()), pltpu.SemaphoreType.DMA(()), pltpu.HBM(buf.shape, buf.dtype),
                   TOKEN_SHAPE],
        input_output_aliases={0: 2},
        compiler_params=pltpu.CompilerParams(has_side_effects=SPLIT_COPY),
    )(_in_hbm(buf))


def _join_wait(name, buf, send_sem, recv_sem, after):
    def body(buf_ref, send_sem, recv_sem, after_ref, buf_out):
        send, arrive = _join_copy(buf_ref, send_sem, recv_sem)
        send.wait_send()
        arrive.wait_recv()

    return pl.pallas_call(
        body, name=name,
        in_specs=[HBM_SPEC, SEM_SPEC, SEM_SPEC, _hbm_spec()],
        out_specs=HBM_SPEC,
        out_shape=pltpu.HBM(buf.shape, buf.dtype),
        input_output_aliases={0: 0},
        compiler_params=pltpu.CompilerParams(has_side_effects=SPLIT_COPY),
    )(buf, send_sem, recv_sem, after)


def _small_copy(k, buf_ref, land_ref, send_sems, recv_sems):
    x, y, c = lax.axis_index("x"), lax.axis_index("y"), lax.axis_index("c")
    me = 4 * x + 2 * y + c
    peer = (x ^ ((k >> 2) & 1), y ^ ((k >> 1) & 1), c ^ (k & 1))
    cp = pltpu.make_async_remote_copy(src_ref=buf_ref, dst_ref=land_ref.at[me], send_sem=send_sems.at[k - 1],
                                      recv_sem=recv_sems.at[k - 1], device_id=peer, device_id_type=MESH)
    return me, peer, cp


def _small_start(buf, deps):
    land = jnp.broadcast_to(buf[None], (N_DEV,) + buf.shape)
    n_dep = len(deps)

    def body(buf_ref, land_ref, *rest):
        send_sems, recv_sems, _, token = rest[n_dep:]
        for k in range(1, N_DEV):
            _small_copy(k, buf_ref, land_ref, send_sems, recv_sems)[2].start()
        token[...] = jnp.zeros_like(token)

    sems = pltpu.SemaphoreType.DMA((N_DEV - 1,))
    return pl.pallas_call(
        body, name="small_gather_start",
        in_specs=[HBM_SPEC, HBM_SPEC] + [_hbm_spec()] * n_dep,
        out_specs=[SEM_SPEC, SEM_SPEC, HBM_SPEC, TOKEN_SPEC],
        out_shape=[sems, sems, pltpu.HBM(land.shape, land.dtype), TOKEN_SHAPE],
        input_output_aliases={1: 2},
        compiler_params=pltpu.CompilerParams(has_side_effects=SPLIT_COPY),
    )(_in_hbm(buf), _in_hbm(land), *deps)


def _small_wait(buf, land, send_sems, recv_sems, after):
    def body(buf_ref, land_ref, send_sems, recv_sems, after_ref, land_out):
        for k in range(1, N_DEV):
            me, peer, cp = _small_copy(k, buf_ref, land_ref, send_sems, recv_sems)
            cp.wait_send()
            got = land_ref.at[me ^ k]
            pltpu.make_async_remote_copy(src_ref=got, dst_ref=got, send_sem=send_sems.at[k - 1],
                                         recv_sem=recv_sems.at[k - 1], device_id=peer,
                                         device_id_type=MESH).wait_recv()

    return pl.pallas_call(
        body, name="small_gather_wait",
        in_specs=[HBM_SPEC, HBM_SPEC, SEM_SPEC, SEM_SPEC, _hbm_spec()],
        out_specs=HBM_SPEC,
        out_shape=pltpu.HBM(land.shape, land.dtype),
        input_output_aliases={1: 0},
        compiler_params=pltpu.CompilerParams(has_side_effects=SPLIT_COPY),
    )(_in_hbm(buf), land, send_sems, recv_sems, after)


def _sum_devices(land):
    n, R, C = land.shape

    def body(land_ref, out_ref):
        acc = land_ref[0]
        for d in range(1, n):
            acc = acc + land_ref[d]
        out_ref[...] = acc

    return pl.pallas_call(
        body, name="small_sum",
        in_specs=[pl.BlockSpec(memory_space=pltpu.VMEM)],
        out_specs=pl.BlockSpec(memory_space=pltpu.VMEM),
        out_shape=SDS((R, C), land.dtype),
        compiler_params=pltpu.CompilerParams(vmem_limit_bytes=V7X_VMEM_LIMIT),
    )(land)


def _pack_rows(vectors):
    flat = jnp.concatenate([v.reshape(-1) for v in vectors])
    n = flat.shape[0]
    padded = -(-n // 1024) * 1024
    return jnp.pad(flat, (0, padded - n)).reshape(padded // 128, 128)


def _unpack_rows(buf, shapes):
    flat = buf.reshape(-1)
    out, off = [], 0
    for s in shapes:
        n = 1
        for dim in s:
            n *= dim
        out.append(flat[off:off + n].reshape(s))
        off += n
    return out


def _layer_forward(l, x, prm, wg):
    S, D = x.shape
    h = _rmsnorm_fwd(f"attn_norm_{l}", x, prm["attn_norm"][l])
    w_in = wg.get("w_in", l, h, prefetch_next=l > 0)
    ns_in = w_in.shape[-1]
    tmi = min(1024, S)
    p = _matmul(
        f"in_proj_{l}", h, w_in, (S, N_CHIPS * ns_in), F32, grid=(S // tmi, N_CHIPS, 1),
        a_spec=pl.BlockSpec((tmi, D), lambda i, j, k: (i, 0)),
        b_spec=pl.BlockSpec((None, D, ns_in), lambda i, j, k: (j, 0, 0)),
        o_spec=pl.BlockSpec((tmi, ns_in), lambda i, j, k: (i, j)),
        contract=(1, 0), acc_shape=(tmi, ns_in), deps=wg.deps())
    if l == 0:
        wg.prefetch_after("w_in", l, p)
    y_a = _sgu_fwd(f"sgu_fwd_{l}", p, prm["sgu_wt"][l], prm["sgu_bb"][l])
    y_b = _conv_fwd(f"conv_fwd_{l}", p, prm["conv_w"][l])
    os, lses = [], []
    for g in range(N_PATTERNS):
        o_g, lse_g = _attn_fwd(f"attn_fwd_{l}_{g}", p, g, prm["q_gain"][l], prm["k_gain"][l], prm["bd"])
        os.append(o_g)
        lses.append(lse_g)
    y_c = _mix_fwd(f"mix_fwd_{l}", os, lses)
    ycat = jnp.concatenate([y_a, y_b, y_c], axis=1)
    tmb, tnb = min(1024, S), min(1024, D)
    w_out = wg.get("w_out", l, ycat)
    kq = N_CHIPS * w_out.shape[1]
    tmo, tno = min(512, S), D
    x1 = _matmul(
        f"out_proj_{l}", ycat, w_out.reshape(kq, D), (S, D), F32, grid=(S // tmo, D // tno, 1),
        a_spec=pl.BlockSpec((tmo, kq), lambda i, j, k: (i, 0)),
        b_spec=pl.BlockSpec((kq, tno), lambda i, j, k: (0, j)),
        o_spec=pl.BlockSpec((tmo, tno), lambda i, j, k: (i, j)),
        contract=(1, 0), acc_shape=(tmo, tno),
        extras=(x,), extra_specs=(pl.BlockSpec((tmo, tno), lambda i, j, k: (i, j)),),
        epi=lambda r, res: r + res, deps=wg.deps())
    w_mlp_in = wg.get("w_mlp_in", l, x1)
    h2 = _rmsnorm_fwd(f"mlp_norm_{l}", x1, prm["mlp_norm"][l])
    nf4 = w_mlp_in.shape[-1]
    r = _matmul(
        f"mlp_in_{l}", h2, w_mlp_in, (S, N_CHIPS * nf4), BF16, grid=(S // tmb, N_CHIPS, 1),
        a_spec=pl.BlockSpec((tmb, D), lambda i, j, k: (i, 0)),
        b_spec=pl.BlockSpec((None, D, nf4), lambda i, j, k: (j, 0, 0)),
        o_spec=pl.BlockSpec((tmb, nf4), lambda i, j, k: (i, j)),
        contract=(1, 0), acc_shape=(tmb, nf4), epi=_relu, deps=wg.deps())
    w_mlp_out = wg.get("w_mlp_out", l, r)
    dff4 = w_mlp_out.shape[1]
    tk = min(2048, dff4)
    kpc = dff4 // tk
    x2 = _matmul(
        f"mlp_out_{l}", r, w_mlp_out, (S, D), F32, grid=(S // tmb, D // tnb, N_CHIPS * kpc),
        a_spec=pl.BlockSpec((tmb, tk), lambda i, j, k: (i, k)),
        b_spec=pl.BlockSpec((None, tk, tnb), lambda i, j, k: (k // kpc, k % kpc, j)),
        o_spec=pl.BlockSpec((tmb, tnb), lambda i, j, k: (i, j)),
        contract=(1, 0), acc_shape=(tmb, tnb), a_pre=_square,
        extras=(x1,), extra_specs=(pl.BlockSpec((tmb, tnb), lambda i, j, k: (i, j)),),
        epi=lambda acc, res: acc + res, deps=wg.deps())
    saved = dict(x=x, p=p, h=h, os=os, lses=lses, ycat=ycat, x1=x1, r=r, h2=h2)
    return x2, saved


def _layer_backward(l, dx2, dx2b, sv, prm, wg, sink):
    S, D = dx2.shape
    w_in, w_out = wg.get("w_in", l), wg.get("w_out", l)
    w_mlp_in, w_mlp_out = wg.get("w_mlp_in", l), wg.get("w_mlp_out", l)
    dff4 = w_mlp_in.shape[-1]
    dff = N_CHIPS * dff4

    tmb, tnb = min(1024, S), min(1024, D)
    da = _matmul(
        f"mlp_out_bwd_{l}", dx2b, w_mlp_out, (S, dff), BF16, grid=(S // tmb, N_CHIPS, 1),
        a_spec=pl.BlockSpec((tmb, D), lambda i, j, k: (i, 0)),
        b_spec=pl.BlockSpec((None, dff4, D), lambda i, j, k: (j, 0, 0)),
        o_spec=pl.BlockSpec((tmb, dff4), lambda i, j, k: (i, j)),
        contract=(1, 1), acc_shape=(tmb, dff4),
        extras=(sv["r"],), extra_specs=(pl.BlockSpec((tmb, dff4), lambda i, j, k: (i, j)),),
        epi=lambda acc, r: acc * (2.0 * r.astype(F32)), deps=sink.deps())
    tmw = min(1024, dff4)
    mpc = dff4 // tmw
    g_w2 = _matmul(
        f"mlp_out_dw_{l}", sv["r"], dx2b, (N_CHIPS, dff4, D), F32, grid=(N_CHIPS * mpc, D // tnb, 1),
        a_spec=pl.BlockSpec((S, tmw), lambda i, j, k: (0, i)),
        b_spec=pl.BlockSpec((S, tnb), lambda i, j, k: (0, j)),
        o_spec=pl.BlockSpec((None, tmw, tnb), lambda i, j, k: (i // mpc, i % mpc, j)),
        contract=(0, 0), acc_shape=(tmw, tnb), a_pre=_square)
    sink.begin("w_mlp_out", l, g_w2)
    tnx = D
    dh2 = _matmul(
        f"mlp_in_bwd_{l}", da, w_mlp_in, (S, D), F32, grid=(S // tmb, D // tnx, N_CHIPS),
        a_spec=pl.BlockSpec((tmb, dff4), lambda i, j, k: (i, k)),
        b_spec=pl.BlockSpec((None, tnx, dff4), lambda i, j, k: (k, j, 0)),
        o_spec=pl.BlockSpec((tmb, tnx), lambda i, j, k: (i, j)),
        contract=(1, 1), acc_shape=(tmb, tnx), deps=sink.deps())
    sink.advance(dh2)
    tmd = min(1024, D)
    nd = D // tmd
    tnf = min(1024, dff4)
    nf = dff4 // tnf
    g_w1 = _matmul(
        f"mlp_in_dw_{l}", sv["h2"], da, (N_CHIPS, D, dff4), F32, grid=(N_CHIPS * nd, nf, 1),
        a_spec=pl.BlockSpec((S, tmd), lambda i, j, k: (0, i % nd)),
        b_spec=pl.BlockSpec((S, tnf), lambda i, j, k: (0, (i // nd) * nf + j)),
        o_spec=pl.BlockSpec((None, tmd, tnf), lambda i, j, k: (i // nd, i % nd, j)),
        contract=(0, 0), acc_shape=(tmd, tnf))
    sink.begin("w_mlp_in", l, g_w1)
    dx1, dx1b, g_mlp_norm = _rmsnorm_bwd(f"mlp_norm_bwd_{l}", dh2, sv["x1"], prm["mlp_norm"][l], dx2,
                                         deps=sink.deps())

    rq = w_out.shape[1]
    kq = N_CHIPS * rq
    dycat = _matmul(
        f"out_proj_bwd_{l}", dx1b, w_out.reshape(kq, D), (S, kq), F32, grid=(S // tmb, 1, 1),
        a_spec=pl.BlockSpec((tmb, D), lambda i, j, k: (i, 0)),
        b_spec=pl.BlockSpec((kq, D), lambda i, j, k: (0, 0)),
        o_spec=pl.BlockSpec((tmb, kq), lambda i, j, k: (i, 0)),
        contract=(1, 1), acc_shape=(tmb, kq))
    sink.advance(dycat)
    g_wout = _matmul(
        f"out_proj_dw_{l}", sv["ycat"], dx1b, (N_CHIPS, rq, D), F32, grid=(N_CHIPS, 1, 1),
        a_spec=pl.BlockSpec((S, rq), lambda i, j, k: (0, i)),
        b_spec=pl.BlockSpec((S, D), lambda i, j, k: (0, 0)),
        o_spec=pl.BlockSpec((None, rq, D), lambda i, j, k: (i, 0, 0)),
        contract=(0, 0), acc_shape=(rq, D))
    sink.begin("w_out", l, g_wout)

    p = sv["p"]
    du, dv_a, g_sgu_w, db_lanes = _sgu_bwd(f"sgu_bwd_{l}", p, dycat, prm["sgu_wt"][l], prm["sgu_wtt"][l],
                                           prm["sgu_bb"][l])
    g_sgu_b = db_lanes[:, :A_HEADS].T
    db, dc, dxb, g_conv = _conv_bwd(f"conv_bwd_{l}", p, dycat, prm["conv_w"][l])
    do3, c3 = _mix_bwd(f"mix_bwd_{l}", sv["os"], sv["lses"], dycat, prm["bd"])
    dqs, dks, dvs, dgqs, dgks = [], [], [], [], []
    for g in range(N_PATTERNS):
        dq, dk, dv, dgq, dgk = _attn_bwd(f"attn_bwd_{l}_{g}", p, g, sv["lses"][g], do3, c3,
                                         prm["q_gain"][l], prm["k_gain"][l], prm["bd"])
        dqs.append(dq)
        dks.append(dk)
        dvs.append(dv)
        dgqs.append(dgq)
        dgks.append(dgk)
    g_q = jnp.concatenate(dgqs, axis=1).reshape(N_PATTERNS * PW // HEAD_DIM, HEAD_DIM).sum(axis=0)
    g_k = jnp.concatenate(dgks, axis=1).reshape(N_PATTERNS * PW // HEAD_DIM, HEAD_DIM).sum(axis=0)
    dp = jnp.concatenate([du, dv_a, db, dc, dxb] + [t.astype(BF16) for t in dqs + dks + dvs], axis=1)

    ns_in = w_in.shape[-1]
    tmh = min(512, D)
    nh = D // tmh
    g_win = _matmul(
        f"in_proj_dw_{l}", sv["h"], dp, (N_CHIPS, D, ns_in), F32, grid=(N_CHIPS * nh, 1, 1),
        a_spec=pl.BlockSpec((S, tmh), lambda i, j, k: (0, i % nh)),
        b_spec=pl.BlockSpec((S, ns_in), lambda i, j, k: (0, i // nh)),
        o_spec=pl.BlockSpec((None, tmh, ns_in), lambda i, j, k: (i // nh, i % nh, 0)),
        contract=(0, 0), acc_shape=(tmh, ns_in))
    sink.begin("w_in", l, g_win)
    dh = _matmul(
        f"in_proj_bwd_{l}", dp, w_in, (S, D), F32, grid=(S // tmb, D // tnx, N_CHIPS),
        a_spec=pl.BlockSpec((tmb, ns_in), lambda i, j, k: (i, k)),
        b_spec=pl.BlockSpec((None, tnx, ns_in), lambda i, j, k: (k, j, 0)),
        o_spec=pl.BlockSpec((tmb, tnx), lambda i, j, k: (i, j)),
        contract=(1, 1), acc_shape=(tmb, tnx), deps=sink.deps())
    sink.advance(dh)
    dx0, dx0b, g_attn_norm = _rmsnorm_bwd(f"attn_norm_bwd_{l}", dh, sv["x"], prm["attn_norm"][l], dx1,
                                          deps=sink.deps())

    big = dict(w_in=g_win, w_out=g_wout, w_mlp_in=g_w1, w_mlp_out=g_w2)
    small = dict(attn_norm=g_attn_norm.reshape(-1), sgu_w=g_sgu_w, sgu_b=g_sgu_b, conv_w=g_conv,
                 q_norm=g_q, k_norm=g_k, mlp_norm=g_mlp_norm.reshape(-1))
    return dx0, dx0b, big, small


BIG = ("w_in", "w_out", "w_mlp_in", "w_mlp_out")
SMALL_REPLICATED = ("attn_norm", "sgu_w", "sgu_b", "q_norm", "k_norm", "mlp_norm")


def _local_step(x, target, prm, wg, n_layers, sink):
    saved = []
    h = x
    for l in range(n_layers):
        h, sv = _layer_forward(l, h, prm, wg)
        saved.append(sv)
    dy, dyb, colsq = _loss_kernel(h, target)
    loss = 0.5 * jnp.sum(colsq) / x.shape[1]
    bigs, smalls = [None] * n_layers, [None] * n_layers
    for l in reversed(range(n_layers)):
        dy, dyb, bigs[l], smalls[l] = _layer_backward(l, dy, dyb, saved[l], prm, wg, sink)
    return loss, dy, bigs, smalls


def _prepare_params(attn_norm, sgu_w, sgu_b, conv_full, q_norm, k_norm, mlp_norm):
    n_layers = attn_norm.shape[0]
    tri = jnp.tril(sgu_w)
    idx = jnp.arange(PW)
    bd = (idx[:, None] // HEAD_DIM == idx[None, :] // HEAD_DIM).astype(BF16)
    return dict(
        attn_norm=[attn_norm[l][None, :] for l in range(n_layers)],
        mlp_norm=[mlp_norm[l][None, :] for l in range(n_layers)],
        sgu_wt=[tri[l].astype(BF16) for l in range(n_layers)],
        sgu_wtt=[tri[l].transpose(0, 2, 1).astype(BF16) for l in range(n_layers)],
        sgu_bb=[jnp.repeat(sgu_b[l].T, HEAD_DIM, axis=1) for l in range(n_layers)],
        conv_w=[conv_full[l] for l in range(n_layers)],
        q_gain=[jnp.tile(q_norm[l], PW // HEAD_DIM)[None, :] for l in range(n_layers)],
        k_gain=[jnp.tile(k_norm[l], PW // HEAD_DIM)[None, :] for l in range(n_layers)],
        bd=bd,
    )


def kernel(x, attn_norm, w_in, sgu_w, sgu_b, conv_w, q_norm, k_norm, w_out, mlp_norm, w_mlp_in, w_mlp_out, loss_target, m_attn_norm, m_w_in, m_sgu_w, m_sgu_b, m_conv_w, m_q_norm, m_k_norm, m_w_out, m_mlp_norm, m_w_mlp_in, m_w_mlp_out, v_attn_norm, v_w_in, v_sgu_w, v_sgu_b, v_conv_w, v_q_norm, v_k_norm, v_w_out, v_mlp_norm, v_w_mlp_in, v_w_mlp_out):
    n_layers = attn_norm.shape[0]
    weights = dict(attn_norm=attn_norm, w_in=w_in, sgu_w=sgu_w, sgu_b=sgu_b, conv_w=conv_w, q_norm=q_norm,
                   k_norm=k_norm, w_out=w_out, mlp_norm=mlp_norm, w_mlp_in=w_mlp_in, w_mlp_out=w_mlp_out)
    mom_m = dict(attn_norm=m_attn_norm, w_in=m_w_in, sgu_w=m_sgu_w, sgu_b=m_sgu_b, conv_w=m_conv_w,
                 q_norm=m_q_norm, k_norm=m_k_norm, w_out=m_w_out, mlp_norm=m_mlp_norm, w_mlp_in=m_w_mlp_in,
                 w_mlp_out=m_w_mlp_out)
    mom_v = dict(attn_norm=v_attn_norm, w_in=v_w_in, sgu_w=v_sgu_w, sgu_b=v_sgu_b, conv_w=v_conv_w,
                 q_norm=v_q_norm, k_norm=v_k_norm, w_out=v_w_out, mlp_norm=v_mlp_norm, w_mlp_in=v_w_mlp_in,
                 w_mlp_out=v_w_mlp_out)
    order = ("attn_norm", "w_in", "sgu_w", "sgu_b", "conv_w", "q_norm", "k_norm", "w_out", "mlp_norm",
             "w_mlp_in", "w_mlp_out")
    chip = 2 * lax.axis_index("x") + lax.axis_index("y")
    c_arr = jnp.stack([lax.axis_index("c"), chip]).astype(jnp.int32)

    conv_cols = conv_w.shape[-1]
    chip_arr = chip.astype(jnp.int32).reshape(1)
    conv_pack = jnp.pad(conv_w.reshape(-1), (0, 2048 - conv_w.size)).reshape(1, 16, 128)
    wg = _GatheredWeights()
    wg.start([("conv_w", 0), ("w_in", 0)],
             [_place_shard("place_conv_w", conv_pack, 0, chip_arr, F32),
              _place_shard("place_w_in_0", weights["w_in"], 0, chip_arr, BF16)])
    keys = [(n, l) for l in range(n_layers) for n in BIG if (n, l) != ("w_in", 0)]
    first = wg.deps()
    wg.start(keys, [_place_shard(f"place_{n}_{l}", weights[n], l, chip_arr, BF16, deps=first) for n, l in keys])
    conv_full = wg.get("conv_w", 0, wg.deps()[-1]).reshape(N_CHIPS, 2048)[:, :conv_w.size].reshape(N_CHIPS, n_layers, 3, conv_cols)
    conv_full = conv_full.transpose(1, 2, 0, 3).reshape(n_layers, 3, N_CHIPS * conv_cols)
    prm = _prepare_params(attn_norm, sgu_w, sgu_b, conv_full, q_norm, k_norm, mlp_norm)

    sink = _GradReducer(c_arr)
    loss_local, grad_x, _, smalls = _local_step(x[0], loss_target[0], prm, wg, n_layers, sink)
    loss = lax.psum(loss_local, ("x", "y", "c"))

    small_names = SMALL_REPLICATED + ("conv_w",)
    small_shapes = [(n_layers,) + tuple(smalls[0][n].shape) for n in small_names]
    packed = _pack_rows([jnp.stack([smalls[l][n] for l in range(n_layers)]) for n in small_names])
    small_send, small_recv, small_land, small_token = _small_start(packed, sink.deps())

    grads, delta, new_m, new_v = {}, {}, {}, {}

    def update(n, after):
        shp = weights[n].shape
        two_d = (shp[0] * shp[1], shp[2])
        d, nm, nv, g = _adamw(f"adamw_{n}", weights[n].reshape(two_d), sink.reduced(n, after).reshape(two_d),
                              mom_m[n].reshape(two_d), mom_v[n].reshape(two_d))
        grads[n], delta[n], new_m[n], new_v[n] = g.reshape(shp), d.reshape(shp), nm.reshape(shp), nv.reshape(shp)

    token = small_token
    for n in ("w_mlp_out", "w_mlp_in", "w_out"):
        token = sink.reduce(n, n_layers, token)
    update("w_mlp_out", token)
    token = sink.reduce("w_in", n_layers, delta["w_mlp_out"])
    update("w_mlp_in", token)
    update("w_out", delta["w_mlp_in"])
    update("w_in", delta["w_out"])
    small_land = _small_wait(packed, small_land, small_send, small_recv, delta["w_in"])
    grads.update(zip(small_names, _unpack_rows(_sum_devices(small_land), small_shapes)))
    grads["conv_w"] = lax.dynamic_slice_in_dim(grads["conv_w"], chip * conv_cols, conv_cols, axis=2)
    smalls_all = SMALL_REPLICATED + ("conv_w",)
    shapes = [weights[n].shape for n in smalls_all]
    d, nm, nv, _ = _adamw("adamw_small",
                          _pack_rows([weights[n] for n in smalls_all]), _pack_rows([grads[n] for n in smalls_all]),
                          _pack_rows([mom_m[n] for n in smalls_all]), _pack_rows([mom_v[n] for n in smalls_all]))
    for n, dd, mm, vv in zip(smalls_all, _unpack_rows(d, shapes), _unpack_rows(nm, shapes), _unpack_rows(nv, shapes)):
        delta[n], new_m[n], new_v[n] = dd, mm, vv

    return (loss, grad_x[None], *[grads[n] for n in order], *[delta[n] for n in order],
            *[new_m[n] for n in order], *[new_v[n] for n in order])
```

```python
import jax
import jax.numpy as jnp
from jax import lax
from jax.experimental import pallas as pl
from jax.experimental.pallas import tpu as pltpu

F32 = jnp.float32
BF16 = jnp.bfloat16
SDS = jax.ShapeDtypeStruct

EPS = 1e-6
HEAD_DIM = 64
A_HEADS = 8
A_WIDTH = 512
CHUNK = 128
B_WIDTH = 768
C_WIDTH = 768
N_PATTERNS = 3
PATTERN_DILATION = (1, 4, 16)
PW = 256
D_IN_PROJ = 5632
OFF_AU, OFF_AV, OFF_BB, OFF_BC, OFF_BX, OFF_Q, OFF_K, OFF_V = 0, 512, 1024, 1792, 2560, 3328, 4096, 4864
N_CHIPS = 4
N_DEV = 8
BLK = 128

ADAM_LR, ADAM_B1, ADAM_B2, ADAM_EPS, ADAM_WD, ADAM_STEP = 0.001, 0.9, 0.999, 1e-08, 0.01, 10

V7X_VMEM_LIMIT = 56 * 1024 * 1024
MESH = pl.DeviceIdType.MESH
NEG = -1e30


def _cp(n_axes):
    return pltpu.CompilerParams(dimension_semantics=("arbitrary",) * n_axes, vmem_limit_bytes=V7X_VMEM_LIMIT)


def _hbm_spec():
    return pl.BlockSpec(memory_space=pl.ANY)


def _relu(t):
    return jnp.maximum(t, 0.0)


def _square(t):
    return t * t


def _matmul(name, a, b, out_shape, out_dtype, *, grid, a_spec, b_spec, o_spec, contract, acc_shape,
            extras=(), extra_specs=(), a_pre=None, epi=None, deps=()):
    nk = grid[2]
    n_ex = len(extras)
    n_dep = len(deps)
    dims = (((contract[0],), (contract[1],)), ((), ()))

    def product(a_ref, b_ref):
        av = a_ref[...] if a_pre is None else a_pre(a_ref[...])
        return lax.dot_general(av, b_ref[...], dims, preferred_element_type=F32)

    def finish(r, ex, o_ref):
        if epi is not None:
            r = epi(r, *[e[...] for e in ex])
        o_ref[...] = r.astype(o_ref.dtype)

    def body_single(a_ref, b_ref, *rest):
        finish(product(a_ref, b_ref), rest[:n_ex], rest[n_ex + n_dep])

    def body(a_ref, b_ref, *rest):
        ex = rest[:n_ex]
        o_ref = rest[n_ex + n_dep]
        acc_ref = rest[n_ex + n_dep + 1]
        k = pl.program_id(2)

        @pl.when(k == 0)
        def _():
            acc_ref[...] = product(a_ref, b_ref)

        @pl.when((k > 0) & (k < nk - 1))
        def _():
            acc_ref[...] += product(a_ref, b_ref)

        @pl.when(k == nk - 1)
        def _():
            finish(acc_ref[...] + product(a_ref, b_ref), ex, o_ref)

    return pl.pallas_call(
        body_single if nk == 1 else body, name=name, grid=grid,
        in_specs=[a_spec, b_spec, *extra_specs] + [_hbm_spec()] * n_dep,
        out_specs=o_spec,
        out_shape=SDS(out_shape, out_dtype),
        scratch_shapes=[] if nk == 1 else [pltpu.VMEM(acc_shape, F32)],
        compiler_params=_cp(3),
    )(a, b, *extras, *deps)


def _loss_kernel(y, t):
    S, D = y.shape
    tm = min(256, S)

    def body(y_ref, t_ref, dy_ref, dyb_ref, l_ref):
        @pl.when(pl.program_id(0) == 0)
        def _():
            l_ref[...] = jnp.zeros_like(l_ref)
        e = y_ref[...] - t_ref[...]
        l_ref[...] += jnp.sum(e * e, axis=0, keepdims=True)
        dy = e * (1.0 / D)
        dy_ref[...] = dy
        dyb_ref[...] = dy.astype(BF16)

    row = pl.BlockSpec((tm, D), lambda i: (i, 0))
    return pl.pallas_call(
        body, name="loss_head", grid=(S // tm,),
        in_specs=[row, row],
        out_specs=[row, row, pl.BlockSpec((1, D), lambda i: (0, 0))],
        out_shape=[SDS((S, D), F32), SDS((S, D), BF16), SDS((1, D), F32)],
        compiler_params=_cp(1),
    )(y, t)


def _rmsnorm_fwd(name, x, g):
    S, D = x.shape
    tm = min(512, S)

    def body(x_ref, g_ref, h_ref):
        xv = x_ref[...]
        y = xv * lax.rsqrt(jnp.mean(xv * xv, axis=-1, keepdims=True) + EPS) * g_ref[...]
        h_ref[...] = y.astype(h_ref.dtype)

    row = pl.BlockSpec((tm, D), lambda i: (i, 0))
    return pl.pallas_call(
        body, name=name, grid=(S // tm,),
        in_specs=[row, pl.BlockSpec((1, D), lambda i: (0, 0))],
        out_specs=row,
        out_shape=SDS((S, D), BF16),
        compiler_params=_cp(1),
    )(x, g)


def _rmsnorm_bwd(name, dh, x, g, dres, deps=()):
    S, D = x.shape
    tm = min(256, S)
    n_dep = len(deps)

    def body(dh_ref, x_ref, g_ref, dres_ref, *rest):
        dx_ref, dxb_ref, dg_ref = rest[n_dep:]
        @pl.when(pl.program_id(0) == 0)
        def _():
            dg_ref[...] = jnp.zeros_like(dg_ref)
        xv = x_ref[...]
        dhv = dh_ref[...]
        rstd = lax.rsqrt(jnp.mean(xv * xv, axis=-1, keepdims=True) + EPS)
        xhat = xv * rstd
        dg_ref[...] += jnp.sum(dhv * xhat, axis=0, keepdims=True)
        dxn = dhv * g_ref[...]
        dx = dres_ref[...] + rstd * (dxn - xhat * jnp.mean(dxn * xhat, axis=-1, keepdims=True))
        dx_ref[...] = dx
        dxb_ref[...] = dx.astype(BF16)

    row = pl.BlockSpec((tm, D), lambda i: (i, 0))
    vec = pl.BlockSpec((1, D), lambda i: (0, 0))
    return pl.pallas_call(
        body, name=name, grid=(S // tm,),
        in_specs=[row, row, vec, row] + [_hbm_spec()] * n_dep,
        out_specs=[row, row, vec],
        out_shape=[SDS((S, D), F32), SDS((S, D), BF16), SDS((1, D), F32)],
        compiler_params=_cp(1),
    )(dh, x, g, dres, *deps)


def _adamw(name, w, g, m, v):
    R, C = w.shape
    tr = 256 if R % 256 == 0 else R
    c1 = 1.0 - ADAM_B1 ** ADAM_STEP
    c2 = 1.0 - ADAM_B2 ** ADAM_STEP

    def body(w_ref, g_ref, m_ref, v_ref, d_ref, nm_ref, nv_ref, g_out_ref):
        gv = g_ref[...]
        nm = ADAM_B1 * m_ref[...] + (1.0 - ADAM_B1) * gv
        nv = ADAM_B2 * v_ref[...] + (1.0 - ADAM_B2) * (gv * gv)
        m_hat = nm / c1
        v_hat = nv / c2
        d_ref[...] = -ADAM_LR * (m_hat / (jnp.sqrt(v_hat) + ADAM_EPS) + ADAM_WD * w_ref[...])
        nm_ref[...] = nm
        nv_ref[...] = nv
        g_out_ref[...] = gv

    blk = pl.BlockSpec((tr, C), lambda i: (i, 0))
    return pl.pallas_call(
        body, name=name, grid=(R // tr,),
        in_specs=[blk] * 4, out_specs=[blk] * 4,
        out_shape=[SDS((R, C), F32)] * 4,
        compiler_params=_cp(1),
    )(w, g, m, v)


SGU_STEP_ROWS = 1024


def _pair_select(lane, lo, hi):
    return jnp.where(lane < HEAD_DIM, lo, hi)


def _sgu_fwd(name, p, wt, bb):
    S = p.shape[0]

    rows = min(SGU_STEP_ROWS, S)

    def body(u_ref, v_ref, wt_ref, bb_ref, o_ref):
        lane = lax.broadcasted_iota(jnp.int32, (CHUNK, 128), 1)
        for ci in range(rows // CHUNK):
            rs = slice(CHUNK * ci, CHUNK * (ci + 1))
            for pp in range(A_HEADS // 2):
                cs = slice(128 * pp, 128 * (pp + 1))
                vb = v_ref[rs, cs].astype(BF16)
                mixed = _pair_select(lane,
                                     jnp.dot(wt_ref[2 * pp], vb, preferred_element_type=F32),
                                     jnp.dot(wt_ref[2 * pp + 1], vb, preferred_element_type=F32)) + bb_ref[:, cs]
                o_ref[rs, cs] = (u_ref[rs, cs] * mixed).astype(o_ref.dtype)

    return pl.pallas_call(
        body, name=name, grid=(S // rows,),
        in_specs=[pl.BlockSpec((rows, A_WIDTH), lambda c: (c, OFF_AU // A_WIDTH)),
                  pl.BlockSpec((rows, A_WIDTH), lambda c: (c, OFF_AV // A_WIDTH)),
                  pl.BlockSpec((A_HEADS, CHUNK, CHUNK), lambda c: (0, 0, 0)),
                  pl.BlockSpec((CHUNK, A_WIDTH), lambda c: (0, 0))],
        out_specs=pl.BlockSpec((rows, A_WIDTH), lambda c: (c, 0)),
        out_shape=SDS((S, A_WIDTH), BF16),
        compiler_params=_cp(1),
    )(p, p, wt, bb)


def _sgu_bwd(name, p, dycat, wt, wtt, bb):
    S = p.shape[0]
    rows = min(SGU_STEP_ROWS, S)

    def body(u_ref, v_ref, dy_ref, wt_ref, wtt_ref, bb_ref, du_ref, dv_ref, dw_ref, db_ref, dbacc_ref):
        c = pl.program_id(0)

        @pl.when(c == 0)
        def _():
            dw_ref[...] = jnp.zeros_like(dw_ref)
            dbacc_ref[...] = jnp.zeros_like(dbacc_ref)

        lane = lax.broadcasted_iota(jnp.int32, (CHUNK, 128), 1)
        row = lax.broadcasted_iota(jnp.int32, (CHUNK, 128), 0)
        causal = row >= lane
        nt = (((1,), (1,)), ((), ()))
        for pp in range(A_HEADS // 2):
            cs = slice(128 * pp, 128 * (pp + 1))
            dw_lo = jnp.zeros((CHUNK, CHUNK), F32)
            dw_hi = jnp.zeros((CHUNK, CHUNK), F32)
            dm_sum = jnp.zeros((CHUNK, 128), F32)
            for ci in range(rows // CHUNK):
                rs = slice(CHUNK * ci, CHUNK * (ci + 1))
                vb = v_ref[rs, cs].astype(BF16)
                dy = dy_ref[rs, cs]
                mixed = _pair_select(lane,
                                     jnp.dot(wt_ref[2 * pp], vb, preferred_element_type=F32),
                                     jnp.dot(wt_ref[2 * pp + 1], vb, preferred_element_type=F32)) + bb_ref[:, cs]
                du_ref[rs, cs] = (dy * mixed).astype(du_ref.dtype)
                dm = dy * u_ref[rs, cs]
                dmb = dm.astype(BF16)
                dv = _pair_select(lane,
                                  jnp.dot(wtt_ref[2 * pp], dmb, preferred_element_type=F32),
                                  jnp.dot(wtt_ref[2 * pp + 1], dmb, preferred_element_type=F32))
                dv_ref[rs, cs] = dv.astype(dv_ref.dtype)
                dm_sum += dm
                dm_lo = jnp.where(lane < HEAD_DIM, dm, 0.0).astype(BF16)
                dm_hi = jnp.where(lane >= HEAD_DIM, dm, 0.0).astype(BF16)
                dw_lo += lax.dot_general(dm_lo, vb, nt, preferred_element_type=F32)
                dw_hi += lax.dot_general(dm_hi, vb, nt, preferred_element_type=F32)
            dbacc_ref[:, cs] += dm_sum
            dw_ref[2 * pp] += jnp.where(causal, dw_lo, 0.0)
            dw_ref[2 * pp + 1] += jnp.where(causal, dw_hi, 0.0)

        @pl.when(c == S // rows - 1)
        def _():
            out = jnp.zeros((CHUNK, 128), F32)
            for pp in range(A_HEADS // 2):
                acc = dbacc_ref[:, 128 * pp:128 * (pp + 1)]
                s_lo = jnp.sum(jnp.where(lane < HEAD_DIM, acc, 0.0), axis=1, keepdims=True)
                s_hi = jnp.sum(jnp.where(lane >= HEAD_DIM, acc, 0.0), axis=1, keepdims=True)
                out = jnp.where(lane == 2 * pp, s_lo, out)
                out = jnp.where(lane == 2 * pp + 1, s_hi, out)
            db_ref[...] = out

    chunk = lambda col: pl.BlockSpec((rows, A_WIDTH), lambda c: (c, col))
    wspec = pl.BlockSpec((A_HEADS, CHUNK, CHUNK), lambda c: (0, 0, 0))
    return pl.pallas_call(
        body, name=name, grid=(S // rows,),
        in_specs=[chunk(OFF_AU // A_WIDTH), chunk(OFF_AV // A_WIDTH), chunk(0), wspec, wspec,
                  pl.BlockSpec((CHUNK, A_WIDTH), lambda c: (0, 0))],
        out_specs=[chunk(0), chunk(0), wspec, pl.BlockSpec((CHUNK, 128), lambda c: (0, 0))],
        out_shape=[SDS((S, A_WIDTH), BF16), SDS((S, A_WIDTH), BF16),
                   SDS((A_HEADS, CHUNK, CHUNK), F32), SDS((CHUNK, 128), F32)],
        scratch_shapes=[pltpu.VMEM((CHUNK, A_WIDTH), F32)],
        compiler_params=_cp(1),
    )(p, p, dycat, wt, wtt, bb)


CONV_HALO = 8
CONV_COLS = 256
CONV_ROWS = 2048


def _shift_down(a, halo, k):
    T = a.shape[0]
    row = lax.broadcasted_iota(jnp.int32, a.shape, 0)
    out = pltpu.roll(a, k, 0)
    for r in range(k):
        out = jnp.where(row == r, halo[CONV_HALO - k + r:CONV_HALO - k + r + 1, :], out)
    return out


def _shift_up(a, halo, k):
    T = a.shape[0]
    row = lax.broadcasted_iota(jnp.int32, a.shape, 0)
    out = pltpu.roll(a, T - k, 0)
    for r in range(k):
        out = jnp.where(row == T - k + r, halo[r:r + 1, :], out)
    return out


def _conv_specs(S, T):
    hb = T // CONV_HALO
    last = S // CONV_HALO - 1
    tile = lambda col0: pl.BlockSpec((T, CONV_COLS), lambda j, i: (i, col0 + j))
    prev = lambda col0: pl.BlockSpec((CONV_HALO, CONV_COLS), lambda j, i: (jnp.maximum(i * hb - 1, 0), col0 + j))
    nxt = lambda col0: pl.BlockSpec((CONV_HALO, CONV_COLS), lambda j, i: (jnp.minimum((i + 1) * hb, last), col0 + j))
    return tile, prev, nxt


def _conv_fwd(name, p, w):
    S = p.shape[0]
    T = min(CONV_ROWS, S)
    tile, prev, _ = _conv_specs(S, T)
    cb, cc, cx = OFF_BB // CONV_COLS, OFF_BC // CONV_COLS, OFF_BX // CONV_COLS

    def body(b_ref, c_ref, x_ref, ch_ref, xh_ref, w_ref, o_ref):
        i = pl.program_id(1)
        z = c_ref[...] * x_ref[...]
        zh = jnp.where(i > 0, ch_ref[...] * xh_ref[...], 0.0)
        z1 = _shift_down(z, zh, 1)
        z2 = _shift_down(z, zh, 2)
        conv = w_ref[0:1, :] * z2 + w_ref[1:2, :] * z1 + w_ref[2:3, :] * z
        o_ref[...] = (b_ref[...] * conv).astype(o_ref.dtype)

    return pl.pallas_call(
        body, name=name, grid=(B_WIDTH // CONV_COLS, S // T),
        in_specs=[tile(cb), tile(cc), tile(cx), prev(cc), prev(cx),
                  pl.BlockSpec((3, CONV_COLS), lambda j, i: (0, j))],
        out_specs=tile(0),
        out_shape=SDS((S, B_WIDTH), BF16),
        compiler_params=_cp(2),
    )(p, p, p, p, p, w)


def _conv_bwd(name, p, dycat, w):
    S = p.shape[0]
    T = min(CONV_ROWS, S)
    tile, prev, nxt = _conv_specs(S, T)
    cb, cc, cx = OFF_BB // CONV_COLS, OFF_BC // CONV_COLS, OFF_BX // CONV_COLS
    cdy = A_WIDTH // CONV_COLS
    n_i = S // T

    def body(b_ref, c_ref, x_ref, dy_ref, ch_ref, xh_ref, bn_ref, dyn_ref, w_ref,
             db_ref, dc_ref, dx_ref, dw_ref):
        i = pl.program_id(1)

        @pl.when(i == 0)
        def _():
            dw_ref[...] = jnp.zeros_like(dw_ref)

        cv = c_ref[...]
        xv = x_ref[...]
        z = cv * xv
        zh = jnp.where(i > 0, ch_ref[...] * xh_ref[...], 0.0)
        z1 = _shift_down(z, zh, 1)
        z2 = _shift_down(z, zh, 2)
        w0, w1, w2 = w_ref[0:1, :], w_ref[1:2, :], w_ref[2:3, :]
        conv = w0 * z2 + w1 * z1 + w2 * z
        dy = dy_ref[...]
        db_ref[...] = (dy * conv).astype(db_ref.dtype)
        dconv = dy * b_ref[...]
        dconv_n = jnp.where(i < n_i - 1, dyn_ref[...] * bn_ref[...], 0.0)
        dz = w2 * dconv + w1 * _shift_up(dconv, dconv_n, 1) + w0 * _shift_up(dconv, dconv_n, 2)
        dc_ref[...] = (dz * xv).astype(dc_ref.dtype)
        dx_ref[...] = (dz * cv).astype(dx_ref.dtype)
        dw_ref[0:1, :] += jnp.sum(dconv * z2, axis=0, keepdims=True)
        dw_ref[1:2, :] += jnp.sum(dconv * z1, axis=0, keepdims=True)
        dw_ref[2:3, :] += jnp.sum(dconv * z, axis=0, keepdims=True)

    wspec = pl.BlockSpec((3, CONV_COLS), lambda j, i: (0, j))
    return pl.pallas_call(
        body, name=name, grid=(B_WIDTH // CONV_COLS, n_i),
        in_specs=[tile(cb), tile(cc), tile(cx), tile(cdy), prev(cc), prev(cx), nxt(cb), nxt(cdy), wspec],
        out_specs=[tile(0), tile(0), tile(0), wspec],
        out_shape=[SDS((S, B_WIDTH), BF16)] * 3 + [SDS((3, B_WIDTH), F32)],
        compiler_params=_cp(2),
    )(p, p, p, dycat, p, p, p, dycat, w)


def _seg_sum(t, bd):
    hi = t.astype(BF16)
    lo = (t - hi.astype(F32)).astype(BF16)
    return jnp.dot(hi, bd, preferred_element_type=F32) + jnp.dot(lo, bd, preferred_element_type=F32)


def _head_norm(x, g, bd):
    rstd = lax.rsqrt(_seg_sum(x * x, bd) * (1.0 / HEAD_DIM) + EPS)
    xhat = x * rstd
    return xhat * g, xhat, rstd


def _head_norm_bwd(dy, g, xhat, rstd, bd):
    dxh = dy * g
    return rstd * (dxh - xhat * (_seg_sum(dxh * xhat, bd) * (1.0 / HEAD_DIM)))


def _band_mask(has_prev):
    row = lax.broadcasted_iota(jnp.int32, (BLK, 2 * BLK), 0)
    col = lax.broadcasted_iota(jnp.int32, (BLK, 2 * BLK), 1)
    first_key = jnp.where(has_prev, 0, BLK)
    return (col >= row) & (col <= row + BLK) & (col >= first_key)


def _residue_rows(r, d):
    return slice(None) if d == 1 else pl.ds(r, BLK, stride=d)


STRIDED_LANES = 128


def _step_width(d):
    return PW if d == 1 else STRIDED_LANES


def _n_stack(lane):
    return lane.shape[1] // HEAD_DIM


def _for_residues(d, fn):
    if d == 1:
        fn(0)
    else:
        def four(i, carry):
            for u in range(4):
                fn(4 * i + u)
            return carry
        lax.fori_loop(0, d // 4, four, 0)


def _head_mask(lane, j):
    return (lane >= HEAD_DIM * j) & (lane < HEAD_DIM * (j + 1))


def _stack_heads(x, lane):
    return jnp.concatenate([jnp.where(_head_mask(lane, j), x, 0.0) for j in range(_n_stack(lane))], axis=0)


def _unstack_heads(y, lane):
    out = y[:BLK]
    for j in range(1, _n_stack(lane)):
        out = jnp.where(lane >= HEAD_DIM * j, y[BLK * j:BLK * (j + 1)], out)
    return out


def _head_columns(v, lane):
    return jnp.concatenate([jnp.max(jnp.where(_head_mask(lane, j), v, NEG), axis=1, keepdims=True)
                            for j in range(_n_stack(lane))], axis=0)


def _attn_fwd(name, p, g, gq, gk, bd):
    S = p.shape[0]
    d = PATTERN_DILATION[g]
    rows = BLK * d
    hw = _step_width(d)
    nt = (((1,), (1,)), ((), ()))

    def body(q_ref, kc_ref, kp_ref, vc_ref, vp_ref, gq_ref, gk_ref, bd_ref, o_ref, lse_ref):
        has_prev = pl.program_id(1) > 0
        bdv = bd_ref[...]
        band = jnp.concatenate([_band_mask(has_prev)] * (hw // HEAD_DIM), axis=0)
        lane = lax.broadcasted_iota(jnp.int32, (1, hw), 1)

        def residue(r):
            rr = _residue_rows(r, d)
            qn, _, _ = _head_norm(q_ref[rr, :], gq_ref[...], bdv)
            kn, _, _ = _head_norm(jnp.concatenate([kp_ref[rr, :], kc_ref[rr, :]], axis=0), gk_ref[...], bdv)
            knb = kn.astype(BF16)
            vb = jnp.concatenate([vp_ref[rr, :], vc_ref[rr, :]], axis=0).astype(BF16)
            qs = _stack_heads(qn, lane).astype(BF16)
            s = lax.dot_general(qs, knb, nt, preferred_element_type=F32) * (HEAD_DIM ** -0.5)
            s = jnp.where(band, s, NEG)
            m = jnp.max(s, axis=1, keepdims=True)
            e = jnp.exp(s - m)
            den = jnp.sum(e, axis=1, keepdims=True)
            pv = jnp.dot(e.astype(BF16), vb, preferred_element_type=F32)
            o_ref[rr, :] = _unstack_heads(pv / den, lane)
            lse_ref[rr, :] = _unstack_heads(jnp.broadcast_to(m + jnp.log(den), pv.shape), lane)

        _for_residues(d, residue)

    per = PW // hw
    cq, ck, cv = (OFF_Q + PW * g) // hw, (OFF_K + PW * g) // hw, (OFF_V + PW * g) // hw
    cur = lambda col: pl.BlockSpec((rows, hw), lambda h, n: (n, col + h))
    prv = lambda col: pl.BlockSpec((rows, hw), lambda h, n: (jnp.maximum(n - 1, 0), col + h))
    vec = pl.BlockSpec((1, hw), lambda h, n: (0, h))
    return pl.pallas_call(
        body, name=name, grid=(per, S // rows),
        in_specs=[cur(cq), cur(ck), prv(ck), cur(cv), prv(cv), vec, vec, pl.BlockSpec((hw, hw), lambda h, n: (0, 0))],
        out_specs=[cur(0), cur(0)],
        out_shape=[SDS((S, PW), F32)] * 2,
        compiler_params=_cp(2),
    )(p, p, p, p, p, gq, gk, bd)


def _attn_bwd(name, p, g, lse, do3, c3, gq, gk, bd):
    S = p.shape[0]
    d = PATTERN_DILATION[g]
    rows = BLK * d
    nblk = S // rows
    hw = _step_width(d)
    nt = (((1,), (1,)), ((), ()))
    tn = (((0,), (0,)), ((), ()))

    def body(q_ref, kc_ref, kp_ref, vc_ref, vp_ref, lse_ref, do_ref, c_ref, gq_ref, gk_ref, bd_ref,
             dq_ref, dk_ref, dv_ref, dgq_ref, dgk_ref, ck_ref, cv_ref, dq_keep_ref):
        n = pl.program_id(1)

        @pl.when(n == 0)
        def _():
            ck_ref[...] = jnp.zeros_like(ck_ref)
            cv_ref[...] = jnp.zeros_like(cv_ref)
            dgq_ref[...] = jnp.zeros_like(dgq_ref)
            dgk_ref[...] = jnp.zeros_like(dgk_ref)

        @pl.when(n == nblk)
        def _():
            dq_ref[...] = dq_keep_ref[...]
            dk_ref[...] = ck_ref[...]
            dv_ref[...] = cv_ref[...]

        bdv = bd_ref[...]
        gqv = gq_ref[...]
        gkv = gk_ref[...]
        band = jnp.concatenate([_band_mask(n > 0)] * (hw // HEAD_DIM), axis=0)
        lane = lax.broadcasted_iota(jnp.int32, (1, hw), 1)

        def residue(r):
            rr = _residue_rows(r, d)
            qn, qhat, qrstd = _head_norm(q_ref[rr, :], gqv, bdv)
            kn, khat, krstd = _head_norm(jnp.concatenate([kp_ref[rr, :], kc_ref[rr, :]], axis=0), gkv, bdv)
            knb = kn.astype(BF16)
            vb = jnp.concatenate([vp_ref[rr, :], vc_ref[rr, :]], axis=0).astype(BF16)
            qs = _stack_heads(qn, lane).astype(BF16)
            dos = _stack_heads(do_ref[rr, :], lane).astype(BF16)
            s = lax.dot_general(qs, knb, nt, preferred_element_type=F32) * (HEAD_DIM ** -0.5)
            prob = jnp.where(band, jnp.exp(s - _head_columns(lse_ref[rr, :], lane)), 0.0)
            dp = lax.dot_general(dos, vb, nt, preferred_element_type=F32)
            ds = (prob * (dp + _head_columns(c_ref[rr, :], lane)) * (HEAD_DIM ** -0.5)).astype(BF16)
            dqn = _unstack_heads(jnp.dot(ds, knb, preferred_element_type=F32), lane)
            dkn = lax.dot_general(ds, qs, tn, preferred_element_type=F32)
            dvv = lax.dot_general(prob.astype(BF16), dos, tn, preferred_element_type=F32)

            dq = _head_norm_bwd(dqn, gqv, qhat, qrstd, bdv)
            dq_ref[rr, :] = dq
            dq_keep_ref[rr, :] = dq
            dk2 = _head_norm_bwd(dkn, gkv, khat, krstd, bdv)
            dgq_ref[...] += jnp.sum(dqn * qhat, axis=0, keepdims=True)
            dgk_ref[...] += jnp.sum(dkn * khat, axis=0, keepdims=True)
            dk_ref[rr, :] = ck_ref[rr, :] + dk2[:BLK]
            dv_ref[rr, :] = cv_ref[rr, :] + dvv[:BLK]
            ck_ref[rr, :] = dk2[BLK:]
            cv_ref[rr, :] = dvv[BLK:]

        @pl.when(n < nblk)
        def _():
            _for_residues(d, residue)

    last = nblk - 1
    per = PW // hw
    cq, ck, cv = (OFF_Q + PW * g) // hw, (OFF_K + PW * g) // hw, (OFF_V + PW * g) // hw
    cur = lambda col: pl.BlockSpec((rows, hw), lambda h, n: (jnp.minimum(n, last), col + h))
    prv = lambda col: pl.BlockSpec((rows, hw), lambda h, n: (jnp.maximum(jnp.minimum(n, last) - 1, 0), col + h))
    cur3 = pl.BlockSpec((None, rows, hw), lambda h, n: (g, jnp.minimum(n, last), h))
    done = pl.BlockSpec((rows, hw), lambda h, n: (jnp.maximum(n - 1, 0), h))
    vec = pl.BlockSpec((1, hw), lambda h, n: (0, h))
    return pl.pallas_call(
        body, name=name, grid=(per, nblk + 1),
        in_specs=[cur(cq), cur(ck), prv(ck), cur(cv), prv(cv), cur(0), cur3, cur3, vec, vec,
                  pl.BlockSpec((hw, hw), lambda h, n: (0, 0))],
        out_specs=[cur(0), done, done, vec, vec],
        out_shape=[SDS((S, PW), F32)] * 3 + [SDS((1, PW), F32)] * 2,
        scratch_shapes=[pltpu.VMEM((rows, hw), F32)] * 3,
        compiler_params=_cp(2),
    )(p, p, p, p, p, lse, do3, c3, gq, gk, bd)


def _mix_fwd(name, os, lses):
    S = os[0].shape[0]
    tm = min(1024, S)

    def body(o0, o1, o2, l0, l1, l2, y_ref):
        o = [o0[...], o1[...], o2[...]]
        l = [l0[...], l1[...], l2[...]]
        m = jnp.maximum(jnp.maximum(l[0], l[1]), l[2])
        e = [jnp.exp(t - m) for t in l]
        inv = 1.0 / (e[0] + e[1] + e[2])
        for g in range(N_PATTERNS):
            y_ref[:, PW * g:PW * (g + 1)] = (o[g] * (e[g] * inv)).astype(y_ref.dtype)

    blk = pl.BlockSpec((tm, PW), lambda i: (i, 0))
    return pl.pallas_call(
        body, name=name, grid=(S // tm,),
        in_specs=[blk] * 6,
        out_specs=pl.BlockSpec((tm, C_WIDTH), lambda i: (i, 0)),
        out_shape=SDS((S, C_WIDTH), BF16),
        compiler_params=_cp(1),
    )(*os, *lses)


def _mix_bwd(name, os, lses, dycat, bd):
    S = os[0].shape[0]
    tm = min(1024, S)
    c0 = (A_WIDTH + B_WIDTH) // PW

    def body(o0, o1, o2, l0, l1, l2, dy0_ref, dy1_ref, dy2_ref, bd_ref, do_ref, c_ref):
        bdv = bd_ref[...]
        o = [o0[...], o1[...], o2[...]]
        l = [l0[...], l1[...], l2[...]]
        dys = [dy0_ref[...], dy1_ref[...], dy2_ref[...]]
        m = jnp.maximum(jnp.maximum(l[0], l[1]), l[2])
        e = [jnp.exp(t - m) for t in l]
        inv = 1.0 / (e[0] + e[1] + e[2])
        alpha = [t * inv for t in e]
        da = [_seg_sum(dys[g] * o[g], bdv) for g in range(N_PATTERNS)]
        mean_da = alpha[0] * da[0] + alpha[1] * da[1] + alpha[2] * da[2]
        for g in range(N_PATTERNS):
            do_ref[g] = dys[g] * alpha[g]
            c_ref[g] = -alpha[g] * mean_da

    blk = pl.BlockSpec((tm, PW), lambda i: (i, 0))
    blk3 = pl.BlockSpec((N_PATTERNS, tm, PW), lambda i: (0, i, 0))
    dyspec = lambda g: pl.BlockSpec((tm, PW), lambda i: (i, c0 + g))
    return pl.pallas_call(
        body, name=name, grid=(S // tm,),
        in_specs=[blk] * 6 + [dyspec(0), dyspec(1), dyspec(2), pl.BlockSpec((PW, PW), lambda i: (0, 0))],
        out_specs=[blk3, blk3],
        out_shape=[SDS((N_PATTERNS, S, PW), F32)] * 2,
        compiler_params=_cp(1),
    )(*os, *lses, dycat, dycat, dycat, bd)


def _mesh_pos():
    x, y, c = lax.axis_index("x"), lax.axis_index("y"), lax.axis_index("c")
    chips = [(1 - x, y), (x, 1 - y), (1 - x, 1 - y)]
    chip_idx = [2 * cx + cy for cx, cy in chips]
    return x, y, c, 2 * x + y, chips, chip_idx


def _place_shard(name, w, layer, chip_arr, out_dtype, deps=()):
    _, R, C = w.shape
    tr = min(256, R)

    def body(chip_ref, w_ref, *rest):
        o_ref = rest[-1]
        o_ref[...] = w_ref[...].astype(o_ref.dtype)

    return pl.pallas_call(
        body, name=name,
        grid_spec=pltpu.PrefetchScalarGridSpec(
            num_scalar_prefetch=1, grid=(R // tr,),
            in_specs=[pl.BlockSpec((None, tr, C), lambda i, chip_ref: (layer, i, 0))] + [_hbm_spec()] * len(deps),
            out_specs=pl.BlockSpec((None, tr, C), lambda i, chip_ref: (chip_ref[0], i, 0))),
        out_shape=SDS((N_CHIPS, R, C), out_dtype),
        compiler_params=_cp(1),
    )(chip_arr, w, *deps)


HBM_SPEC = pl.BlockSpec(memory_space=pltpu.HBM)
SEM_SPEC = pl.BlockSpec(memory_space=pltpu.SEMAPHORE)
SPLIT_COPY = pltpu.SideEffectType.DATAFLOW_SIDE_EFFECTING
N_PEER_CHIPS = N_CHIPS - 1
TOKEN_SHAPE = SDS((8, 128), F32)
TOKEN_SPEC = pl.BlockSpec(memory_space=pltpu.VMEM)


def _in_hbm(a):
    return pltpu.with_memory_space_constraint(a, pltpu.HBM)


def _gather_start(name, bufs):
    T = len(bufs)

    def body(*refs):
        ins = refs[:T]
        send_sems, recv_sems = refs[T:2 * T], refs[2 * T:3 * T]
        token = refs[4 * T]
        x, y, c, me, chips, chip_idx = _mesh_pos()
        for t in range(T):
            hr = ins[t].shape[1] // 2
            mine = ins[t].at[me, pl.ds(c * hr, hr), :]
            for j in range(N_PEER_CHIPS):
                pltpu.make_async_remote_copy(src_ref=mine, dst_ref=mine, send_sem=send_sems[t].at[j],
                                             recv_sem=recv_sems[t].at[j], device_id=(*chips[j], c),
                                             device_id_type=MESH).start()
        token[...] = jnp.zeros_like(token)

    sems = [pltpu.SemaphoreType.DMA((N_PEER_CHIPS,))] * T
    out = pl.pallas_call(
        body, name=name,
        in_specs=[HBM_SPEC] * T,
        out_specs=[SEM_SPEC] * (2 * T) + [HBM_SPEC] * T + [TOKEN_SPEC],
        out_shape=sems + sems + [pltpu.HBM(b.shape, b.dtype) for b in bufs] + [TOKEN_SHAPE],
        input_output_aliases={t: 2 * T + t for t in range(T)},
        compiler_params=pltpu.CompilerParams(has_side_effects=SPLIT_COPY),
    )(*[_in_hbm(b) for b in bufs])
    return out[:T], out[T:2 * T], out[2 * T:3 * T], out[3 * T]


def _gather_wait(name, buf, send_sem, recv_sem, after):
    n_in = 3 if after is None else 4

    def body(*refs):
        buf_ref, ssem, rsem = refs[:3]
        x, y, c, me, chips, chip_idx = _mesh_pos()
        hr = buf_ref.shape[1] // 2
        mine = buf_ref.at[me, pl.ds(c * hr, hr), :]
        for j in range(N_PEER_CHIPS):
            got = buf_ref.at[chip_idx[j], pl.ds(c * hr, hr), :]
            cp = pltpu.make_async_remote_copy(src_ref=mine, dst_ref=got, send_sem=ssem.at[j], recv_sem=rsem.at[j],
                                              device_id=(*chips[j], c), device_id_type=MESH)
            cp.wait_send()
            cp.wait_recv()

    args = [buf, send_sem, recv_sem] + ([] if after is None else [after])
    return pl.pallas_call(
        body, name=name,
        in_specs=[HBM_SPEC, SEM_SPEC, SEM_SPEC] + [_hbm_spec()] * (n_in - 3),
        out_specs=HBM_SPEC,
        out_shape=pltpu.HBM(buf.shape, buf.dtype),
        input_output_aliases={0: 0},
        compiler_params=pltpu.CompilerParams(has_side_effects=SPLIT_COPY),
    )(*args)


def _forward_start(name, buf):
    def body(buf_ref, send_sems, recv_sems, buf_thru, token):
        x, y, c, me, chips, chip_idx = _mesh_pos()
        hr = buf_ref.shape[1] // 2
        for j in range(N_PEER_CHIPS):
            got = buf_ref.at[chip_idx[j], pl.ds(c * hr, hr), :]
            pltpu.make_async_remote_copy(src_ref=got, dst_ref=got, send_sem=send_sems.at[j], recv_sem=recv_sems.at[j],
                                         device_id=(x, y, 1 - c), device_id_type=MESH).start()
        token[...] = jnp.zeros_like(token)

    sems = pltpu.SemaphoreType.DMA((N_PEER_CHIPS,))
    return pl.pallas_call(
        body, name=name,
        in_specs=[HBM_SPEC],
        out_specs=[SEM_SPEC, SEM_SPEC, HBM_SPEC, TOKEN_SPEC],
        out_shape=[sems, sems, pltpu.HBM(buf.shape, buf.dtype), TOKEN_SHAPE],
        input_output_aliases={0: 2},
        compiler_params=pltpu.CompilerParams(has_side_effects=SPLIT_COPY),
    )(_in_hbm(buf))


def _forward_wait(name, buf, send_sems, recv_sems, after):
    n_in = 3 if after is None else 4

    def body(*refs):
        buf_ref, ssems, rsems = refs[:3]
        x, y, c, me, chips, chip_idx = _mesh_pos()
        hr = buf_ref.shape[1] // 2
        for j in range(N_PEER_CHIPS):
            sent = buf_ref.at[chip_idx[j], pl.ds(c * hr, hr), :]
            theirs = buf_ref.at[chip_idx[j], pl.ds((1 - c) * hr, hr), :]
            cp = pltpu.make_async_remote_copy(src_ref=sent, dst_ref=theirs, send_sem=ssems.at[j],
                                              recv_sem=rsems.at[j], device_id=(x, y, 1 - c), device_id_type=MESH)
            cp.wait_send()
            cp.wait_recv()

    args = [buf, send_sems, recv_sems] + ([] if after is None else [after])
    return pl.pallas_call(
        body, name=name,
        in_specs=[HBM_SPEC, SEM_SPEC, SEM_SPEC] + [_hbm_spec()] * (n_in - 3),
        out_specs=HBM_SPEC,
        out_shape=pltpu.HBM(buf.shape, buf.dtype),
        input_output_aliases={0: 0},
        compiler_params=pltpu.CompilerParams(has_side_effects=SPLIT_COPY),
    )(*args)


class _GatheredWeights:
    def __init__(self):
        self._order = []
        self._pending = {}
        self._forwarding = {}
        self._ready = {}
        self._tokens = []

    def start(self, keys, bufs):
        send_sems, recv_sems, thru, token = _gather_start(f"gather_start_{len(self._order)}", bufs)
        self._tokens.append(token)
        self._order.extend(keys)
        self._pending.update({k: (b, s, r) for k, b, s, r in zip(keys, thru, send_sems, recv_sems)})

    def _prefetch(self, key, after):
        if key in self._pending:
            buf, ssem, rsem = self._pending.pop(key)
            tag = f"{key[0]}_{key[1]}"
            buf = _gather_wait(f"gather_wait_{tag}", buf, ssem, rsem, after)
            ssems, rsems, buf, token = _forward_start(f"gather_fwd_start_{tag}", buf)
            self._forwarding[key] = (buf, ssems, rsems)
            self._tokens.append(token)

    def get(self, name, layer, after=None, prefetch_next=True):
        key = (name, layer)
        if key not in self._ready:
            self._prefetch(key, after)
            buf, ssems, rsems = self._forwarding.pop(key)
            self._ready[key] = _forward_wait(f"gather_fwd_wait_{name}_{layer}", buf, ssems, rsems, after)
            if prefetch_next:
                self.prefetch_after(name, layer, after)
        return self._ready[key]

    def prefetch_after(self, name, layer, after):
        nxt = self._order.index((name, layer)) + 1
        if nxt < len(self._order):
            self._prefetch(self._order[nxt], after)

    def deps(self):
        tokens, self._tokens = self._tokens, []
        return tokens


def _swap_copy(g_ref, land_ref, send_sem, recv_sem):
    x, y, c, _, _, _ = _mesh_pos()
    hr = g_ref.shape[1] // 2
    return pltpu.make_async_remote_copy(src_ref=g_ref.at[:, pl.ds((1 - c) * hr, hr), :], dst_ref=land_ref,
                                        send_sem=send_sem, recv_sem=recv_sem, device_id=(x, y, 1 - c),
                                        device_id_type=MESH)


def _swap_start(name, g):
    land_shape = (g.shape[0], g.shape[1] // 2, g.shape[2])

    def body(g_ref, land_ref, send_sem, recv_sem, land_thru, token):
        _swap_copy(g_ref, land_ref, send_sem, recv_sem).start()
        token[...] = jnp.zeros_like(token)

    return pl.pallas_call(
        body, name=name,
        in_specs=[HBM_SPEC, HBM_SPEC],
        out_specs=[SEM_SPEC, SEM_SPEC, HBM_SPEC, TOKEN_SPEC],
        out_shape=[pltpu.SemaphoreType.DMA(()), pltpu.SemaphoreType.DMA(()), pltpu.HBM(land_shape, g.dtype),
                   TOKEN_SHAPE],
        input_output_aliases={1: 2},
        compiler_params=pltpu.CompilerParams(has_side_effects=SPLIT_COPY),
    )(_in_hbm(g), _in_hbm(lax.empty(land_shape, g.dtype)))


def _swap_wait(name, g, land, send_sem, recv_sem, after):
    def body(g_ref, land_ref, send_sem, recv_sem, after_ref, land_out):
        cp = _swap_copy(g_ref, land_ref, send_sem, recv_sem)
        cp.wait_send()
        cp.wait_recv()

    return pl.pallas_call(
        body, name=name,
        in_specs=[HBM_SPEC, HBM_SPEC, SEM_SPEC, SEM_SPEC, _hbm_spec()],
        out_specs=HBM_SPEC,
        out_shape=pltpu.HBM(land.shape, land.dtype),
        input_output_aliases={1: 0},
        compiler_params=pltpu.CompilerParams(has_side_effects=SPLIT_COPY),
    )(_in_hbm(g), land, send_sem, recv_sem, after)


def _add_my_half(name, g, r, pos_arr):
    ns, R, C = g.shape
    hr = R // 2
    tr = min(256, hr)
    nt = hr // tr

    def body(pos_ref, g_ref, r_ref, o_ref, land_ref):
        t = (g_ref[...] + r_ref[...]).astype(o_ref.dtype)
        o_ref[...] = t

        @pl.when(pl.program_id(1) == pos_ref[1])
        def _():
            land_ref[...] = t

    blk = pl.BlockSpec((None, tr, C), lambda i, s, pos_ref: (s, i, 0))
    return pl.pallas_call(
        body, name=name,
        grid_spec=pltpu.PrefetchScalarGridSpec(
            num_scalar_prefetch=1, grid=(nt, ns),
            in_specs=[pl.BlockSpec((None, tr, C), lambda i, s, pos_ref: (s, pos_ref[0] * nt + i, 0)), blk],
            out_specs=[blk, pl.BlockSpec((None, tr, C), lambda i, s, pos_ref: (pos_ref[1], i, 0))]),
        out_shape=[SDS((ns, hr, C), BF16)] * 2,
        compiler_params=_cp(2),
    )(pos_arr, g, r)


def _exchange_start(name, part, land):
    def body(part_ref, land_ref, send_sems, recv_sems, land_thru, token):
        x, y, c, me, chips, chip_idx = _mesh_pos()
        for j in range(N_PEER_CHIPS):
            pltpu.make_async_remote_copy(src_ref=part_ref.at[chip_idx[j]], dst_ref=land_ref.at[me],
                                         send_sem=send_sems.at[j], recv_sem=recv_sems.at[j],
                                         device_id=(*chips[j], c), device_id_type=MESH).start()
        token[...] = jnp.zeros_like(token)

    sems = pltpu.SemaphoreType.DMA((N_PEER_CHIPS,))
    return pl.pallas_call(
        body, name=name,
        in_specs=[HBM_SPEC, HBM_SPEC],
        out_specs=[SEM_SPEC, SEM_SPEC, HBM_SPEC, TOKEN_SPEC],
        out_shape=[sems, sems, pltpu.HBM(land.shape, land.dtype), TOKEN_SHAPE],
        input_output_aliases={1: 2},
        compiler_params=pltpu.CompilerParams(has_side_effects=SPLIT_COPY),
    )(_in_hbm(part), _in_hbm(land))


def _exchange_wait(name, part, land, send_sems, recv_sems, after):
    def body(part_ref, land_ref, send_sems, recv_sems, after_ref, land_out):
        x, y, c, me, chips, chip_idx = _mesh_pos()
        for j in range(N_PEER_CHIPS):
            cp = pltpu.make_async_remote_copy(src_ref=part_ref.at[chip_idx[j]], dst_ref=land_ref.at[chip_idx[j]],
                                              send_sem=send_sems.at[j], recv_sem=recv_sems.at[j],
                                              device_id=(*chips[j], c), device_id_type=MESH)
            cp.wait_send()
            cp.wait_recv()

    return pl.pallas_call(
        body, name=name,
        in_specs=[HBM_SPEC, HBM_SPEC, SEM_SPEC, SEM_SPEC, _hbm_spec()],
        out_specs=HBM_SPEC,
        out_shape=pltpu.HBM(land.shape, land.dtype),
        input_output_aliases={1: 0},
        compiler_params=pltpu.CompilerParams(has_side_effects=SPLIT_COPY),
    )(_in_hbm(part), land, send_sems, recv_sems, after)


class _GradReducer:
    def __init__(self, c_arr):
        self._c_arr = c_arr
        self._swapping = []
        self._exchanging = {}
        self._joining = {}
        self._tokens = []

    def begin(self, name, layer, g):
        tag = f"{name}_{layer}"
        ssem, rsem, land, token = _swap_start(f"rs_swap_start_{tag}", g)
        self._swapping.append((name, layer, g, ssem, rsem, land))
        self._tokens.append(token)

    def advance(self, after):
        for name, layer, g, ssem, rsem, land in self._swapping:
            tag = f"{name}_{layer}"
            theirs = _swap_wait(f"rs_swap_wait_{tag}", g, land, ssem, rsem, after)
            part, own = _add_my_half(f"rs_add_{tag}", g, theirs, self._c_arr)
            ssems, rsems, land2, token = _exchange_start(f"rs_xchg_start_{tag}", part, own)
            self._exchanging[(name, layer)] = (part, ssems, rsems, land2)
            self._tokens.append(token)
        self._swapping = []

    def deps(self):
        tokens, self._tokens = self._tokens, []
        return tokens

    def reduce(self, name, n_layers, after):
        buf = None
        for layer in range(n_layers):
            part, ssems, rsems, land = self._exchanging.pop((name, layer))
            tag = f"{name}_{layer}"
            landed = _exchange_wait(f"rs_xchg_wait_{tag}", part, land, ssems, rsems, after)
            buf = _sum_chips(f"rs_sum_{tag}", landed, self._c_arr, layer, n_layers, buf)
        ssem, rsem, buf, token = _join_start(f"rs_join_start_{name}", buf)
        self._joining[name] = (buf, ssem, rsem)
        return token

    def reduced(self, name, after):
        buf, ssem, rsem = self._joining.pop(name)
        return _join_wait(f"rs_join_wait_{name}", buf, ssem, rsem, after)


def _sum_chips(name, r, c_arr, layer, n_layers, prev):
    ns, H, C = r.shape
    tr = min(256, H)
    nt = H // tr

    def body(c_ref, r_ref, *rest):
        o_ref = rest[-1]
        o_ref[...] = ((r_ref[0].astype(F32) + r_ref[1].astype(F32)) + r_ref[2].astype(F32)) + r_ref[3].astype(F32)

    in_specs = [pl.BlockSpec((ns, tr, C), lambda i, c_ref: (0, i, 0))]
    args = [c_arr, r]
    aliases = {}
    if prev is not None:
        in_specs.append(_hbm_spec())
        args.append(prev)
        aliases = {2: 0}
    return pl.pallas_call(
        body, name=name,
        grid_spec=pltpu.PrefetchScalarGridSpec(
            num_scalar_prefetch=1, grid=(nt,), in_specs=in_specs,
            out_specs=pl.BlockSpec((None, tr, C), lambda i, c_ref: (layer, c_ref[0] * nt + i, 0))),
        out_shape=SDS((n_layers, 2 * H, C), F32),
        input_output_aliases=aliases,
        compiler_params=_cp(1),
    )(*args)


def _join_copy(buf_ref, send_sem, recv_sem):
    x, y, c, _, _, _ = _mesh_pos()
    hr = buf_ref.shape[1] // 2
    mine = buf_ref.at[:, pl.ds(c * hr, hr), :]
    theirs = buf_ref.at[:, pl.ds((1 - c) * hr, hr), :]
    send = pltpu.make_async_remote_copy(src_ref=mine, dst_ref=mine, send_sem=send_sem, recv_sem=recv_sem,
                                        device_id=(x, y, 1 - c), device_id_type=MESH)
    arrive = pltpu.make_async_remote_copy(src_ref=theirs, dst_ref=theirs, send_sem=send_sem, recv_sem=recv_sem,
                                          device_id=(x, y, 1 - c), device_id_type=MESH)
    return send, arrive


def _join_start(name, buf):
    def body(buf_ref, send_sem, recv_sem, buf_thru, token):
        _join_copy(buf_ref, send_sem, recv_sem)[0].start()
        token[...] = jnp.zeros_like(token)

    return pl.pallas_call(
        body, name=name,
        in_specs=[HBM_SPEC],
        out_specs=[SEM_SPEC, SEM_SPEC, HBM_SPEC, TOKEN_SPEC],
        out_shape=[pltpu.SemaphoreType.DMA(()), pltpu.SemaphoreType.DMA(()), pltpu.HBM(buf.shape, buf.dtype),
                   TOKEN_SHAPE],
        input_output_aliases={0: 2},
        compiler_params=pltpu.CompilerParams(has_side_effects=SPLIT_COPY),
    )(_in_hbm(buf))


def _join_wait(name, buf, send_sem, recv_sem, after):
    def body(buf_ref, send_sem, recv_sem, after_ref, buf_out):
        send, arrive = _join_copy(buf_ref, send_sem, recv_sem)
        send.wait_send()
        arrive.wait_recv()

    return pl.pallas_call(
        body, name=name,
        in_specs=[HBM_SPEC, SEM_SPEC, SEM_SPEC, _hbm_spec()],
        out_specs=HBM_SPEC,
        out_shape=pltpu.HBM(buf.shape, buf.dtype),
        input_output_aliases={0: 0},
        compiler_params=pltpu.CompilerParams(has_side_effects=SPLIT_COPY),
    )(buf, send_sem, recv_sem, after)


def _small_copy(k, buf_ref, land_ref, send_sems, recv_sems):
    x, y, c = lax.axis_index("x"), lax.axis_index("y"), lax.axis_index("c")
    me = 4 * x + 2 * y + c
    peer = (x ^ ((k >> 2) & 1), y ^ ((k >> 1) & 1), c ^ (k & 1))
    cp = pltpu.make_async_remote_copy(src_ref=buf_ref, dst_ref=land_ref.at[me], send_sem=send_sems.at[k - 1],
                                      recv_sem=recv_sems.at[k - 1], device_id=peer, device_id_type=MESH)
    return me, peer, cp


def _small_start(buf, deps):
    land = jnp.broadcast_to(buf[None], (N_DEV,) + buf.shape)
    n_dep = len(deps)

    def body(buf_ref, land_ref, *rest):
        send_sems, recv_sems, _, token = rest[n_dep:]
        for k in range(1, N_DEV):
            _small_copy(k, buf_ref, land_ref, send_sems, recv_sems)[2].start()
        token[...] = jnp.zeros_like(token)

    sems = pltpu.SemaphoreType.DMA((N_DEV - 1,))
    return pl.pallas_call(
        body, name="small_gather_start",
        in_specs=[HBM_SPEC, HBM_SPEC] + [_hbm_spec()] * n_dep,
        out_specs=[SEM_SPEC, SEM_SPEC, HBM_SPEC, TOKEN_SPEC],
        out_shape=[sems, sems, pltpu.HBM(land.shape, land.dtype), TOKEN_SHAPE],
        input_output_aliases={1: 2},
        compiler_params=pltpu.CompilerParams(has_side_effects=SPLIT_COPY),
    )(_in_hbm(buf), _in_hbm(land), *deps)


def _small_wait(buf, land, send_sems, recv_sems, after):
    def body(buf_ref, land_ref, send_sems, recv_sems, after_ref, land_out):
        for k in range(1, N_DEV):
            me, peer, cp = _small_copy(k, buf_ref, land_ref, send_sems, recv_sems)
            cp.wait_send()
            got = land_ref.at[me ^ k]
            pltpu.make_async_remote_copy(src_ref=got, dst_ref=got, send_sem=send_sems.at[k - 1],
                                         recv_sem=recv_sems.at[k - 1], device_id=peer,
                                         device_id_type=MESH).wait_recv()

    return pl.pallas_call(
        body, name="small_gather_wait",
        in_specs=[HBM_SPEC, HBM_SPEC, SEM_SPEC, SEM_SPEC, _hbm_spec()],
        out_specs=HBM_SPEC,
        out_shape=pltpu.HBM(land.shape, land.dtype),
        input_output_aliases={1: 0},
        compiler_params=pltpu.CompilerParams(has_side_effects=SPLIT_COPY),
    )(_in_hbm(buf), land, send_sems, recv_sems, after)


def _sum_devices(land):
    n, R, C = land.shape

    def body(land_ref, out_ref):
        acc = land_ref[0]
        for d in range(1, n):
            acc = acc + land_ref[d]
        out_ref[...] = acc

    return pl.pallas_call(
        body, name="small_sum",
        in_specs=[pl.BlockSpec(memory_space=pltpu.VMEM)],
        out_specs=pl.BlockSpec(memory_space=pltpu.VMEM),
        out_shape=SDS((R, C), land.dtype),
        compiler_params=pltpu.CompilerParams(vmem_limit_bytes=V7X_VMEM_LIMIT),
    )(land)


def _pack_rows(vectors):
    flat = jnp.concatenate([v.reshape(-1) for v in vectors])
    n = flat.shape[0]
    padded = -(-n // 1024) * 1024
    return jnp.pad(flat, (0, padded - n)).reshape(padded // 128, 128)


def _unpack_rows(buf, shapes):
    flat = buf.reshape(-1)
    out, off = [], 0
    for s in shapes:
        n = 1
        for dim in s:
            n *= dim
        out.append(flat[off:off + n].reshape(s))
        off += n
    return out


def _layer_forward(l, x, prm, wg):
    S, D = x.shape
    h = _rmsnorm_fwd(f"attn_norm_{l}", x, prm["attn_norm"][l])
    w_in = wg.get("w_in", l, h, prefetch_next=l > 0)
    ns_in = w_in.shape[-1]
    tmi = min(1024, S)
    p = _matmul(
        f"in_proj_{l}", h, w_in, (S, N_CHIPS * ns_in), F32, grid=(S // tmi, N_CHIPS, 1),
        a_spec=pl.BlockSpec((tmi, D), lambda i, j, k: (i, 0)),
        b_spec=pl.BlockSpec((None, D, ns_in), lambda i, j, k: (j, 0, 0)),
        o_spec=pl.BlockSpec((tmi, ns_in), lambda i, j, k: (i, j)),
        contract=(1, 0), acc_shape=(tmi, ns_in), deps=wg.deps())
    if l == 0:
        wg.prefetch_after("w_in", l, p)
    y_a = _sgu_fwd(f"sgu_fwd_{l}", p, prm["sgu_wt"][l], prm["sgu_bb"][l])
    y_b = _conv_fwd(f"conv_fwd_{l}", p, prm["conv_w"][l])
    os, lses = [], []
    for g in range(N_PATTERNS):
        o_g, lse_g = _attn_fwd(f"attn_fwd_{l}_{g}", p, g, prm["q_gain"][l], prm["k_gain"][l], prm["bd"])
        os.append(o_g)
        lses.append(lse_g)
    y_c = _mix_fwd(f"mix_fwd_{l}", os, lses)
    ycat = jnp.concatenate([y_a, y_b, y_c], axis=1)
    tmb, tnb = min(1024, S), min(1024, D)
    w_out = wg.get("w_out", l, ycat)
    kq = N_CHIPS * w_out.shape[1]
    tmo, tno = min(512, S), D
    x1 = _matmul(
        f"out_proj_{l}", ycat, w_out.reshape(kq, D), (S, D), F32, grid=(S // tmo, D // tno, 1),
        a_spec=pl.BlockSpec((tmo, kq), lambda i, j, k: (i, 0)),
        b_spec=pl.BlockSpec((kq, tno), lambda i, j, k: (0, j)),
        o_spec=pl.BlockSpec((tmo, tno), lambda i, j, k: (i, j)),
        contract=(1, 0), acc_shape=(tmo, tno),
        extras=(x,), extra_specs=(pl.BlockSpec((tmo, tno), lambda i, j, k: (i, j)),),
        epi=lambda r, res: r + res, deps=wg.deps())
    w_mlp_in = wg.get("w_mlp_in", l, x1)
    h2 = _rmsnorm_fwd(f"mlp_norm_{l}", x1, prm["mlp_norm"][l])
    nf4 = w_mlp_in.shape[-1]
    r = _matmul(
        f"mlp_in_{l}", h2, w_mlp_in, (S, N_CHIPS * nf4), BF16, grid=(S // tmb, N_CHIPS, 1),
        a_spec=pl.BlockSpec((tmb, D), lambda i, j, k: (i, 0)),
        b_spec=pl.BlockSpec((None, D, nf4), lambda i, j, k: (j, 0, 0)),
        o_spec=pl.BlockSpec((tmb, nf4), lambda i, j, k: (i, j)),
        contract=(1, 0), acc_shape=(tmb, nf4), epi=_relu, deps=wg.deps())
    w_mlp_out = wg.get("w_mlp_out", l, r)
    dff4 = w_mlp_out.shape[1]
    tk = min(2048, dff4)
    kpc = dff4 // tk
    x2 = _matmul(
        f"mlp_out_{l}", r, w_mlp_out, (S, D), F32, grid=(S // tmb, D // tnb, N_CHIPS * kpc),
        a_spec=pl.BlockSpec((tmb, tk), lambda i, j, k: (i, k)),
        b_spec=pl.BlockSpec((None, tk, tnb), lambda i, j, k: (k // kpc, k % kpc, j)),
        o_spec=pl.BlockSpec((tmb, tnb), lambda i, j, k: (i, j)),
        contract=(1, 0), acc_shape=(tmb, tnb), a_pre=_square,
        extras=(x1,), extra_specs=(pl.BlockSpec((tmb, tnb), lambda i, j, k: (i, j)),),
        epi=lambda acc, res: acc + res, deps=wg.deps())
    saved = dict(x=x, p=p, h=h, os=os, lses=lses, ycat=ycat, x1=x1, r=r, h2=h2)
    return x2, saved


def _layer_backward(l, dx2, dx2b, sv, prm, wg, sink):
    S, D = dx2.shape
    w_in, w_out = wg.get("w_in", l), wg.get("w_out", l)
    w_mlp_in, w_mlp_out = wg.get("w_mlp_in", l), wg.get("w_mlp_out", l)
    dff4 = w_mlp_in.shape[-1]
    dff = N_CHIPS * dff4

    tmb, tnb = min(1024, S), min(1024, D)
    da = _matmul(
        f"mlp_out_bwd_{l}", dx2b, w_mlp_out, (S, dff), BF16, grid=(S // tmb, N_CHIPS, 1),
        a_spec=pl.BlockSpec((tmb, D), lambda i, j, k: (i, 0)),
        b_spec=pl.BlockSpec((None, dff4, D), lambda i, j, k: (j, 0, 0)),
        o_spec=pl.BlockSpec((tmb, dff4), lambda i, j, k: (i, j)),
        contract=(1, 1), acc_shape=(tmb, dff4),
        extras=(sv["r"],), extra_specs=(pl.BlockSpec((tmb, dff4), lambda i, j, k: (i, j)),),
        epi=lambda acc, r: acc * (2.0 * r.astype(F32)), deps=sink.deps())
    tmw = min(1024, dff4)
    mpc = dff4 // tmw
    g_w2 = _matmul(
        f"mlp_out_dw_{l}", sv["r"], dx2b, (N_CHIPS, dff4, D), F32, grid=(N_CHIPS * mpc, D // tnb, 1),
        a_spec=pl.BlockSpec((S, tmw), lambda i, j, k: (0, i)),
        b_spec=pl.BlockSpec((S, tnb), lambda i, j, k: (0, j)),
        o_spec=pl.BlockSpec((None, tmw, tnb), lambda i, j, k: (i // mpc, i % mpc, j)),
        contract=(0, 0), acc_shape=(tmw, tnb), a_pre=_square)
    sink.begin("w_mlp_out", l, g_w2)
    tnx = D
    dh2 = _matmul(
        f"mlp_in_bwd_{l}", da, w_mlp_in, (S, D), F32, grid=(S // tmb, D // tnx, N_CHIPS),
        a_spec=pl.BlockSpec((tmb, dff4), lambda i, j, k: (i, k)),
        b_spec=pl.BlockSpec((None, tnx, dff4), lambda i, j, k: (k, j, 0)),
        o_spec=pl.BlockSpec((tmb, tnx), lambda i, j, k: (i, j)),
        contract=(1, 1), acc_shape=(tmb, tnx), deps=sink.deps())
    sink.advance(dh2)
    tmd = min(1024, D)
    nd = D // tmd
    tnf = min(1024, dff4)
    nf = dff4 // tnf
    g_w1 = _matmul(
        f"mlp_in_dw_{l}", sv["h2"], da, (N_CHIPS, D, dff4), F32, grid=(N_CHIPS * nd, nf, 1),
        a_spec=pl.BlockSpec((S, tmd), lambda i, j, k: (0, i % nd)),
        b_spec=pl.BlockSpec((S, tnf), lambda i, j, k: (0, (i // nd) * nf + j)),
        o_spec=pl.BlockSpec((None, tmd, tnf), lambda i, j, k: (i // nd, i % nd, j)),
        contract=(0, 0), acc_shape=(tmd, tnf))
    sink.begin("w_mlp_in", l, g_w1)
    dx1, dx1b, g_mlp_norm = _rmsnorm_bwd(f"mlp_norm_bwd_{l}", dh2, sv["x1"], prm["mlp_norm"][l], dx2,
                                         deps=sink.deps())

    rq = w_out.shape[1]
    kq = N_CHIPS * rq
    dycat = _matmul(
        f"out_proj_bwd_{l}", dx1b, w_out.reshape(kq, D), (S, kq), F32, grid=(S // tmb, 1, 1),
        a_spec=pl.BlockSpec((tmb, D), lambda i, j, k: (i, 0)),
        b_spec=pl.BlockSpec((kq, D), lambda i, j, k: (0, 0)),
        o_spec=pl.BlockSpec((tmb, kq), lambda i, j, k: (i, 0)),
        contract=(1, 1), acc_shape=(tmb, kq))
    sink.advance(dycat)
    g_wout = _matmul(
        f"out_proj_dw_{l}", sv["ycat"], dx1b, (N_CHIPS, rq, D), F32, grid=(N_CHIPS, 1, 1),
        a_spec=pl.BlockSpec((S, rq), lambda i, j, k: (0, i)),
        b_spec=pl.BlockSpec((S, D), lambda i, j, k: (0, 0)),
        o_spec=pl.BlockSpec((None, rq, D), lambda i, j, k: (i, 0, 0)),
        contract=(0, 0), acc_shape=(rq, D))
    sink.begin("w_out", l, g_wout)

    p = sv["p"]
    du, dv_a, g_sgu_w, db_lanes = _sgu_bwd(f"sgu_bwd_{l}", p, dycat, prm["sgu_wt"][l], prm["sgu_wtt"][l],
                                           prm["sgu_bb"][l])
    g_sgu_b = db_lanes[:, :A_HEADS].T
    db, dc, dxb, g_conv = _conv_bwd(f"conv_bwd_{l}", p, dycat, prm["conv_w"][l])
    do3, c3 = _mix_bwd(f"mix_bwd_{l}", sv["os"], sv["lses"], dycat, prm["bd"])
    dqs, dks, dvs, dgqs, dgks = [], [], [], [], []
    for g in range(N_PATTERNS):
        dq, dk, dv, dgq, dgk = _attn_bwd(f"attn_bwd_{l}_{g}", p, g, sv["lses"][g], do3, c3,
                                         prm["q_gain"][l], prm["k_gain"][l], prm["bd"])
        dqs.append(dq)
        dks.append(dk)
        dvs.append(dv)
        dgqs.append(dgq)
        dgks.append(dgk)
    g_q = jnp.concatenate(dgqs, axis=1).reshape(N_PATTERNS * PW // HEAD_DIM, HEAD_DIM).sum(axis=0)
    g_k = jnp.concatenate(dgks, axis=1).reshape(N_PATTERNS * PW // HEAD_DIM, HEAD_DIM).sum(axis=0)
    dp = jnp.concatenate([du, dv_a, db, dc, dxb] + [t.astype(BF16) for t in dqs + dks + dvs], axis=1)

    ns_in = w_in.shape[-1]
    tmh = min(512, D)
    nh = D // tmh
    g_win = _matmul(
        f"in_proj_dw_{l}", sv["h"], dp, (N_CHIPS, D, ns_in), F32, grid=(N_CHIPS * nh, 1, 1),
        a_spec=pl.BlockSpec((S, tmh), lambda i, j, k: (0, i % nh)),
        b_spec=pl.BlockSpec((S, ns_in), lambda i, j, k: (0, i // nh)),
        o_spec=pl.BlockSpec((None, tmh, ns_in), lambda i, j, k: (i // nh, i % nh, 0)),
        contract=(0, 0), acc_shape=(tmh, ns_in))
    sink.begin("w_in", l, g_win)
    dh = _matmul(
        f"in_proj_bwd_{l}", dp, w_in, (S, D), F32, grid=(S // tmb, D // tnx, N_CHIPS),
        a_spec=pl.BlockSpec((tmb, ns_in), lambda i, j, k: (i, k)),
        b_spec=pl.BlockSpec((None, tnx, ns_in), lambda i, j, k: (k, j, 0)),
        o_spec=pl.BlockSpec((tmb, tnx), lambda i, j, k: (i, j)),
        contract=(1, 1), acc_shape=(tmb, tnx), deps=sink.deps())
    sink.advance(dh)
    dx0, dx0b, g_attn_norm = _rmsnorm_bwd(f"attn_norm_bwd_{l}", dh, sv["x"], prm["attn_norm"][l], dx1,
                                          deps=sink.deps())

    big = dict(w_in=g_win, w_out=g_wout, w_mlp_in=g_w1, w_mlp_out=g_w2)
    small = dict(attn_norm=g_attn_norm.reshape(-1), sgu_w=g_sgu_w, sgu_b=g_sgu_b, conv_w=g_conv,
                 q_norm=g_q, k_norm=g_k, mlp_norm=g_mlp_norm.reshape(-1))
    return dx0, dx0b, big, small


BIG = ("w_in", "w_out", "w_mlp_in", "w_mlp_out")
SMALL_REPLICATED = ("attn_norm", "sgu_w", "sgu_b", "q_norm", "k_norm", "mlp_norm")


def _local_step(x, target, prm, wg, n_layers, sink):
    saved = []
    h = x
    for l in range(n_layers):
        h, sv = _layer_forward(l, h, prm, wg)
        saved.append(sv)
    dy, dyb, colsq = _loss_kernel(h, target)
    loss = 0.5 * jnp.sum(colsq) / x.shape[1]
    bigs, smalls = [None] * n_layers, [None] * n_layers
    for l in reversed(range(n_layers)):
        dy, dyb, bigs[l], smalls[l] = _layer_backward(l, dy, dyb, saved[l], prm, wg, sink)
    return loss, dy, bigs, smalls


def _prepare_params(attn_norm, sgu_w, sgu_b, conv_full, q_norm, k_norm, mlp_norm):
    n_layers = attn_norm.shape[0]
    tri = jnp.tril(sgu_w)
    idx = jnp.arange(PW)
    bd = (idx[:, None] // HEAD_DIM == idx[None, :] // HEAD_DIM).astype(BF16)
    return dict(
        attn_norm=[attn_norm[l][None, :] for l in range(n_layers)],
        mlp_norm=[mlp_norm[l][None, :] for l in range(n_layers)],
        sgu_wt=[tri[l].astype(BF16) for l in range(n_layers)],
        sgu_wtt=[tri[l].transpose(0, 2, 1).astype(BF16) for l in range(n_layers)],
        sgu_bb=[jnp.repeat(sgu_b[l].T, HEAD_DIM, axis=1) for l in range(n_layers)],
        conv_w=[conv_full[l] for l in range(n_layers)],
        q_gain=[jnp.tile(q_norm[l], PW // HEAD_DIM)[None, :] for l in range(n_layers)],
        k_gain=[jnp.tile(k_norm[l], PW // HEAD_DIM)[None, :] for l in range(n_layers)],
        bd=bd,
    )


def kernel(x, attn_norm, w_in, sgu_w, sgu_b, conv_w, q_norm, k_norm, w_out, mlp_norm, w_mlp_in, w_mlp_out, loss_target, m_attn_norm, m_w_in, m_sgu_w, m_sgu_b, m_conv_w, m_q_norm, m_k_norm, m_w_out, m_mlp_norm, m_w_mlp_in, m_w_mlp_out, v_attn_norm, v_w_in, v_sgu_w, v_sgu_b, v_conv_w, v_q_norm, v_k_norm, v_w_out, v_mlp_norm, v_w_mlp_in, v_w_mlp_out):
    n_layers = attn_norm.shape[0]
    weights = dict(attn_norm=attn_norm, w_in=w_in, sgu_w=sgu_w, sgu_b=sgu_b, conv_w=conv_w, q_norm=q_norm,
                   k_norm=k_norm, w_out=w_out, mlp_norm=mlp_norm, w_mlp_in=w_mlp_in, w_mlp_out=w_mlp_out)
    mom_m = dict(attn_norm=m_attn_norm, w_in=m_w_in, sgu_w=m_sgu_w, sgu_b=m_sgu_b, conv_w=m_conv_w,
                 q_norm=m_q_norm, k_norm=m_k_norm, w_out=m_w_out, mlp_norm=m_mlp_norm, w_mlp_in=m_w_mlp_in,
                 w_mlp_out=m_w_mlp_out)
    mom_v = dict(attn_norm=v_attn_norm, w_in=v_w_in, sgu_w=v_sgu_w, sgu_b=v_sgu_b, conv_w=v_conv_w,
                 q_norm=v_q_norm, k_norm=v_k_norm, w_out=v_w_out, mlp_norm=v_mlp_norm, w_mlp_in=v_w_mlp_in,
                 w_mlp_out=v_w_mlp_out)
    order = ("attn_norm", "w_in", "sgu_w", "sgu_b", "conv_w", "q_norm", "k_norm", "w_out", "mlp_norm",
             "w_mlp_in", "w_mlp_out")
    chip = 2 * lax.axis_index("x") + lax.axis_index("y")
    c_arr = jnp.stack([lax.axis_index("c"), chip]).astype(jnp.int32)

    conv_cols = conv_w.shape[-1]
    chip_arr = chip.astype(jnp.int32).reshape(1)
    conv_pack = jnp.pad(conv_w.reshape(-1), (0, 2048 - conv_w.size)).reshape(1, 16, 128)
    wg = _GatheredWeights()
    wg.start([("conv_w", 0), ("w_in", 0)],
             [_place_shard("place_conv_w", conv_pack, 0, chip_arr, F32),
              _place_shard("place_w_in_0", weights["w_in"], 0, chip_arr, BF16)])
    keys = [(n, l) for l in range(n_layers) for n in BIG if (n, l) != ("w_in", 0)]
    first = wg.deps()
    wg.start(keys, [_place_shard(f"place_{n}_{l}", weights[n], l, chip_arr, BF16, deps=first) for n, l in keys])
    conv_full = wg.get("conv_w", 0, wg.deps()[-1]).reshape(N_CHIPS, 2048)[:, :conv_w.size].reshape(N_CHIPS, n_layers, 3, conv_cols)
    conv_full = conv_full.transpose(1, 2, 0, 3).reshape(n_layers, 3, N_CHIPS * conv_cols)
    prm = _prepare_params(attn_norm, sgu_w, sgu_b, conv_full, q_norm, k_norm, mlp_norm)

    sink = _GradReducer(c_arr)
    loss_local, grad_x, _, smalls = _local_step(x[0], loss_target[0], prm, wg, n_layers, sink)
    loss = lax.psum(loss_local, ("x", "y", "c"))

    small_names = SMALL_REPLICATED + ("conv_w",)
    small_shapes = [(n_layers,) + tuple(smalls[0][n].shape) for n in small_names]
    packed = _pack_rows([jnp.stack([smalls[l][n] for l in range(n_layers)]) for n in small_names])
    small_send, small_recv, small_land, small_token = _small_start(packed, sink.deps())

    grads, delta, new_m, new_v = {}, {}, {}, {}

    def update(n, after):
        shp = weights[n].shape
        two_d = (shp[0] * shp[1], shp[2])
        d, nm, nv, g = _adamw(f"adamw_{n}", weights[n].reshape(two_d), sink.reduced(n, after).reshape(two_d),
                              mom_m[n].reshape(two_d), mom_v[n].reshape(two_d))
        grads[n], delta[n], new_m[n], new_v[n] = g.reshape(shp), d.reshape(shp), nm.reshape(shp), nv.reshape(shp)

    token = small_token
    for n in ("w_mlp_out", "w_mlp_in", "w_out"):
        token = sink.reduce(n, n_layers, token)
    update("w_mlp_out", token)
    token = sink.reduce("w_in", n_layers, delta["w_mlp_out"])
    update("w_mlp_in", token)
    update("w_out", delta["w_mlp_in"])
    update("w_in", delta["w_out"])
    small_land = _small_wait(packed, small_land, small_send, small_recv, delta["w_in"])
    grads.update(zip(small_names, _unpack_rows(_sum_devices(small_land), small_shapes)))
    grads["conv_w"] = lax.dynamic_slice_in_dim(grads["conv_w"], chip * conv_cols, conv_cols, axis=2)
    smalls_all = SMALL_REPLICATED + ("conv_w",)
    shapes = [weights[n].shape for n in smalls_all]
    d, nm, nv, _ = _adamw("adamw_small",
                          _pack_rows([weights[n] for n in smalls_all]), _pack_rows([grads[n] for n in smalls_all]),
                          _pack_rows([mom_m[n] for n in smalls_all]), _pack_rows([mom_v[n] for n in smalls_all]))
    for n, dd, mm, vv in zip(smalls_all, _unpack_rows(d, shapes), _unpack_rows(nm, shapes), _unpack_rows(nv, shapes)):
        delta[n], new_m[n], new_v[n] = dd, mm, vv

    return (loss, grad_x[None], *[grads[n] for n in order], *[delta[n] for n in order],
            *[new_m[n] for n in order], *[new_v[n] for n in order])
```

```python
import jax
import jax.numpy as jnp
from jax import lax
from jax.experimental import pallas as pl
from jax.experimental.pallas import tpu as pltpu

F32 = jnp.float32
BF16 = jnp.bfloat16
SDS = jax.ShapeDtypeStruct

EPS = 1e-6
HEAD_DIM = 64
A_HEADS = 8
A_WIDTH = 512
CHUNK = 128
B_WIDTH = 768
C_WIDTH = 768
N_PATTERNS = 3
PATTERN_DILATION = (1, 4, 16)
PW = 256
D_IN_PROJ = 5632
OFF_AU, OFF_AV, OFF_BB, OFF_BC, OFF_BX, OFF_Q, OFF_K, OFF_V = 0, 512, 1024, 1792, 2560, 3328, 4096, 4864
N_CHIPS = 4
N_DEV = 8
BLK = 128

ADAM_LR, ADAM_B1, ADAM_B2, ADAM_EPS, ADAM_WD, ADAM_STEP = 0.001, 0.9, 0.999, 1e-08, 0.01, 10

V7X_VMEM_LIMIT = 56 * 1024 * 1024
MESH = pl.DeviceIdType.MESH
NEG = -1e30


def _cp(n_axes):
    return pltpu.CompilerParams(dimension_semantics=("arbitrary",) * n_axes, vmem_limit_bytes=V7X_VMEM_LIMIT)


def _hbm_spec():
    return pl.BlockSpec(memory_space=pl.ANY)


def _relu(t):
    return jnp.maximum(t, 0.0)


def _square(t):
    return t * t


def _matmul(name, a, b, out_shape, out_dtype, *, grid, a_spec, b_spec, o_spec, contract, acc_shape,
            extras=(), extra_specs=(), a_pre=None, epi=None, deps=()):
    nk = grid[2]
    n_ex = len(extras)
    n_dep = len(deps)
    dims = (((contract[0],), (contract[1],)), ((), ()))

    def product(a_ref, b_ref):
        av = a_ref[...] if a_pre is None else a_pre(a_ref[...])
        return lax.dot_general(av, b_ref[...], dims, preferred_element_type=F32)

    def finish(r, ex, o_ref):
        if epi is not None:
            r = epi(r, *[e[...] for e in ex])
        o_ref[...] = r.astype(o_ref.dtype)

    def body_single(a_ref, b_ref, *rest):
        finish(product(a_ref, b_ref), rest[:n_ex], rest[n_ex + n_dep])

    def body(a_ref, b_ref, *rest):
        ex = rest[:n_ex]
        o_ref = rest[n_ex + n_dep]
        acc_ref = rest[n_ex + n_dep + 1]
        k = pl.program_id(2)

        @pl.when(k == 0)
        def _():
            acc_ref[...] = product(a_ref, b_ref)

        @pl.when((k > 0) & (k < nk - 1))
        def _():
            acc_ref[...] += product(a_ref, b_ref)

        @pl.when(k == nk - 1)
        def _():
            finish(acc_ref[...] + product(a_ref, b_ref), ex, o_ref)

    return pl.pallas_call(
        body_single if nk == 1 else body, name=name, grid=grid,
        in_specs=[a_spec, b_spec, *extra_specs] + [_hbm_spec()] * n_dep,
        out_specs=o_spec,
        out_shape=SDS(out_shape, out_dtype),
        scratch_shapes=[] if nk == 1 else [pltpu.VMEM(acc_shape, F32)],
        compiler_params=_cp(3),
    )(a, b, *extras, *deps)


def _loss_kernel(y, t):
    S, D = y.shape
    tm = min(256, S)

    def body(y_ref, t_ref, dy_ref, dyb_ref, l_ref):
        @pl.when(pl.program_id(0) == 0)
        def _():
            l_ref[...] = jnp.zeros_like(l_ref)
        e = y_ref[...] - t_ref[...]
        l_ref[...] += jnp.sum(e * e, axis=0, keepdims=True)
        dy = e * (1.0 / D)
        dy_ref[...] = dy
        dyb_ref[...] = dy.astype(BF16)

    row = pl.BlockSpec((tm, D), lambda i: (i, 0))
    return pl.pallas_call(
        body, name="loss_head", grid=(S // tm,),
        in_specs=[row, row],
        out_specs=[row, row, pl.BlockSpec((1, D), lambda i: (0, 0))],
        out_shape=[SDS((S, D), F32), SDS((S, D), BF16), SDS((1, D), F32)],
        compiler_params=_cp(1),
    )(y, t)


def _rmsnorm_fwd(name, x, g):
    S, D = x.shape
    tm = min(512, S)

    def body(x_ref, g_ref, h_ref):
        xv = x_ref[...]
        y = xv * lax.rsqrt(jnp.mean(xv * xv, axis=-1, keepdims=True) + EPS) * g_ref[...]
        h_ref[...] = y.astype(h_ref.dtype)

    row = pl.BlockSpec((tm, D), lambda i: (i, 0))
    return pl.pallas_call(
        body, name=name, grid=(S // tm,),
        in_specs=[row, pl.BlockSpec((1, D), lambda i: (0, 0))],
        out_specs=row,
        out_shape=SDS((S, D), BF16),
        compiler_params=_cp(1),
    )(x, g)


def _rmsnorm_bwd(name, dh, x, g, dres, deps=()):
    S, D = x.shape
    tm = min(256, S)
    n_dep = len(deps)

    def body(dh_ref, x_ref, g_ref, dres_ref, *rest):
        dx_ref, dxb_ref, dg_ref = rest[n_dep:]
        @pl.when(pl.program_id(0) == 0)
        def _():
            dg_ref[...] = jnp.zeros_like(dg_ref)
        xv = x_ref[...]
        dhv = dh_ref[...]
        rstd = lax.rsqrt(jnp.mean(xv * xv, axis=-1, keepdims=True) + EPS)
        xhat = xv * rstd
        dg_ref[...] += jnp.sum(dhv * xhat, axis=0, keepdims=True)
        dxn = dhv * g_ref[...]
        dx = dres_ref[...] + rstd * (dxn - xhat * jnp.mean(dxn * xhat, axis=-1, keepdims=True))
        dx_ref[...] = dx
        dxb_ref[...] = dx.astype(BF16)

    row = pl.BlockSpec((tm, D), lambda i: (i, 0))
    vec = pl.BlockSpec((1, D), lambda i: (0, 0))
    return pl.pallas_call(
        body, name=name, grid=(S // tm,),
        in_specs=[row, row, vec, row] + [_hbm_spec()] * n_dep,
        out_specs=[row, row, vec],
        out_shape=[SDS((S, D), F32), SDS((S, D), BF16), SDS((1, D), F32)],
        compiler_params=_cp(1),
    )(dh, x, g, dres, *deps)


def _adamw(name, w, g, m, v):
    R, C = w.shape
    tr = 256 if R % 256 == 0 else R
    c1 = 1.0 - ADAM_B1 ** ADAM_STEP
    c2 = 1.0 - ADAM_B2 ** ADAM_STEP

    def body(w_ref, g_ref, m_ref, v_ref, d_ref, nm_ref, nv_ref, g_out_ref):
        gv = g_ref[...]
        nm = ADAM_B1 * m_ref[...] + (1.0 - ADAM_B1) * gv
        nv = ADAM_B2 * v_ref[...] + (1.0 - ADAM_B2) * (gv * gv)
        m_hat = nm / c1
        v_hat = nv / c2
        d_ref[...] = -ADAM_LR * (m_hat / (jnp.sqrt(v_hat) + ADAM_EPS) + ADAM_WD * w_ref[...])
        nm_ref[...] = nm
        nv_ref[...] = nv
        g_out_ref[...] = gv

    blk = pl.BlockSpec((tr, C), lambda i: (i, 0))
    return pl.pallas_call(
        body, name=name, grid=(R // tr,),
        in_specs=[blk] * 4, out_specs=[blk] * 4,
        out_shape=[SDS((R, C), F32)] * 4,
        compiler_params=_cp(1),
    )(w, g, m, v)


SGU_STEP_ROWS = 1024


def _pair_select(lane, lo, hi):
    return jnp.where(lane < HEAD_DIM, lo, hi)


def _sgu_fwd(name, p, wt, bb):
    S = p.shape[0]

    rows = min(SGU_STEP_ROWS, S)

    def body(u_ref, v_ref, wt_ref, bb_ref, o_ref):
        lane = lax.broadcasted_iota(jnp.int32, (CHUNK, 128), 1)
        for ci in range(rows // CHUNK):
            rs = slice(CHUNK * ci, CHUNK * (ci + 1))
            for pp in range(A_HEADS // 2):
                cs = slice(128 * pp, 128 * (pp + 1))
                vb = v_ref[rs, cs].astype(BF16)
                mixed = _pair_select(lane,
                                     jnp.dot(wt_ref[2 * pp], vb, preferred_element_type=F32),
                                     jnp.dot(wt_ref[2 * pp + 1], vb, preferred_element_type=F32)) + bb_ref[:, cs]
                o_ref[rs, cs] = (u_ref[rs, cs] * mixed).astype(o_ref.dtype)

    return pl.pallas_call(
        body, name=name, grid=(S // rows,),
        in_specs=[pl.BlockSpec((rows, A_WIDTH), lambda c: (c, OFF_AU // A_WIDTH)),
                  pl.BlockSpec((rows, A_WIDTH), lambda c: (c, OFF_AV // A_WIDTH)),
                  pl.BlockSpec((A_HEADS, CHUNK, CHUNK), lambda c: (0, 0, 0)),
                  pl.BlockSpec((CHUNK, A_WIDTH), lambda c: (0, 0))],
        out_specs=pl.BlockSpec((rows, A_WIDTH), lambda c: (c, 0)),
        out_shape=SDS((S, A_WIDTH), BF16),
        compiler_params=_cp(1),
    )(p, p, wt, bb)


def _sgu_bwd(name, p, dycat, wt, wtt, bb):
    S = p.shape[0]
    rows = min(SGU_STEP_ROWS, S)

    def body(u_ref, v_ref, dy_ref, wt_ref, wtt_ref, bb_ref, du_ref, dv_ref, dw_ref, db_ref, dbacc_ref):
        c = pl.program_id(0)

        @pl.when(c == 0)
        def _():
            dw_ref[...] = jnp.zeros_like(dw_ref)
            dbacc_ref[...] = jnp.zeros_like(dbacc_ref)

        lane = lax.broadcasted_iota(jnp.int32, (CHUNK, 128), 1)
        row = lax.broadcasted_iota(jnp.int32, (CHUNK, 128), 0)
        causal = row >= lane
        nt = (((1,), (1,)), ((), ()))
        for pp in range(A_HEADS // 2):
            cs = slice(128 * pp, 128 * (pp + 1))
            dw_lo = jnp.zeros((CHUNK, CHUNK), F32)
            dw_hi = jnp.zeros((CHUNK, CHUNK), F32)
            dm_sum = jnp.zeros((CHUNK, 128), F32)
            for ci in range(rows // CHUNK):
                rs = slice(CHUNK * ci, CHUNK * (ci + 1))
                vb = v_ref[rs, cs].astype(BF16)
                dy = dy_ref[rs, cs]
                mixed = _pair_select(lane,
                                     jnp.dot(wt_ref[2 * pp], vb, preferred_element_type=F32),
                                     jnp.dot(wt_ref[2 * pp + 1], vb, preferred_element_type=F32)) + bb_ref[:, cs]
                du_ref[rs, cs] = (dy * mixed).astype(du_ref.dtype)
                dm = dy * u_ref[rs, cs]
                dmb = dm.astype(BF16)
                dv = _pair_select(lane,
                                  jnp.dot(wtt_ref[2 * pp], dmb, preferred_element_type=F32),
                                  jnp.dot(wtt_ref[2 * pp + 1], dmb, preferred_element_type=F32))
                dv_ref[rs, cs] = dv.astype(dv_ref.dtype)
                dm_sum += dm
                dm_lo = jnp.where(lane < HEAD_DIM, dm, 0.0).astype(BF16)
                dm_hi = jnp.where(lane >= HEAD_DIM, dm, 0.0).astype(BF16)
                dw_lo += lax.dot_general(dm_lo, vb, nt, preferred_element_type=F32)
                dw_hi += lax.dot_general(dm_hi, vb, nt, preferred_element_type=F32)
            dbacc_ref[:, cs] += dm_sum
            dw_ref[2 * pp] += jnp.where(causal, dw_lo, 0.0)
            dw_ref[2 * pp + 1] += jnp.where(causal, dw_hi, 0.0)

        @pl.when(c == S // rows - 1)
        def _():
            out = jnp.zeros((CHUNK, 128), F32)
            for pp in range(A_HEADS // 2):
                acc = dbacc_ref[:, 128 * pp:128 * (pp + 1)]
                s_lo = jnp.sum(jnp.where(lane < HEAD_DIM, acc, 0.0), axis=1, keepdims=True)
                s_hi = jnp.sum(jnp.where(lane >= HEAD_DIM, acc, 0.0), axis=1, keepdims=True)
                out = jnp.where(lane == 2 * pp, s_lo, out)
                out = jnp.where(lane == 2 * pp + 1, s_hi, out)
            db_ref[...] = out

    chunk = lambda col: pl.BlockSpec((rows, A_WIDTH), lambda c: (c, col))
    wspec = pl.BlockSpec((A_HEADS, CHUNK, CHUNK), lambda c: (0, 0, 0))
    return pl.pallas_call(
        body, name=name, grid=(S // rows,),
        in_specs=[chunk(OFF_AU // A_WIDTH), chunk(OFF_AV // A_WIDTH), chunk(0), wspec, wspec,
                  pl.BlockSpec((CHUNK, A_WIDTH), lambda c: (0, 0))],
        out_specs=[chunk(0), chunk(0), wspec, pl.BlockSpec((CHUNK, 128), lambda c: (0, 0))],
        out_shape=[SDS((S, A_WIDTH), BF16), SDS((S, A_WIDTH), BF16),
                   SDS((A_HEADS, CHUNK, CHUNK), F32), SDS((CHUNK, 128), F32)],
        scratch_shapes=[pltpu.VMEM((CHUNK, A_WIDTH), F32)],
        compiler_params=_cp(1),
    )(p, p, dycat, wt, wtt, bb)


CONV_HALO = 8
CONV_COLS = 256
CONV_ROWS = 2048


def _shift_down(a, halo, k):
    T = a.shape[0]
    row = lax.broadcasted_iota(jnp.int32, a.shape, 0)
    out = pltpu.roll(a, k, 0)
    for r in range(k):
        out = jnp.where(row == r, halo[CONV_HALO - k + r:CONV_HALO - k + r + 1, :], out)
    return out


def _shift_up(a, halo, k):
    T = a.shape[0]
    row = lax.broadcasted_iota(jnp.int32, a.shape, 0)
    out = pltpu.roll(a, T - k, 0)
    for r in range(k):
        out = jnp.where(row == T - k + r, halo[r:r + 1, :], out)
    return out


def _conv_specs(S, T):
    hb = T // CONV_HALO
    last = S // CONV_HALO - 1
    tile = lambda col0: pl.BlockSpec((T, CONV_COLS), lambda j, i: (i, col0 + j))
    prev = lambda col0: pl.BlockSpec((CONV_HALO, CONV_COLS), lambda j, i: (jnp.maximum(i * hb - 1, 0), col0 + j))
    nxt = lambda col0: pl.BlockSpec((CONV_HALO, CONV_COLS), lambda j, i: (jnp.minimum((i + 1) * hb, last), col0 + j))
    return tile, prev, nxt


def _conv_fwd(name, p, w):
    S = p.shape[0]
    T = min(CONV_ROWS, S)
    tile, prev, _ = _conv_specs(S, T)
    cb, cc, cx = OFF_BB // CONV_COLS, OFF_BC // CONV_COLS, OFF_BX // CONV_COLS

    def body(b_ref, c_ref, x_ref, ch_ref, xh_ref, w_ref, o_ref):
        i = pl.program_id(1)
        z = c_ref[...] * x_ref[...]
        zh = jnp.where(i > 0, ch_ref[...] * xh_ref[...], 0.0)
        z1 = _shift_down(z, zh, 1)
        z2 = _shift_down(z, zh, 2)
        conv = w_ref[0:1, :] * z2 + w_ref[1:2, :] * z1 + w_ref[2:3, :] * z
        o_ref[...] = (b_ref[...] * conv).astype(o_ref.dtype)

    return pl.pallas_call(
        body, name=name, grid=(B_WIDTH // CONV_COLS, S // T),
        in_specs=[tile(cb), tile(cc), tile(cx), prev(cc), prev(cx),
                  pl.BlockSpec((3, CONV_COLS), lambda j, i: (0, j))],
        out_specs=tile(0),
        out_shape=SDS((S, B_WIDTH), BF16),
        compiler_params=_cp(2),
    )(p, p, p, p, p, w)


def _conv_bwd(name, p, dycat, w):
    S = p.shape[0]
    T = min(CONV_ROWS, S)
    tile, prev, nxt = _conv_specs(S, T)
    cb, cc, cx = OFF_BB // CONV_COLS, OFF_BC // CONV_COLS, OFF_BX // CONV_COLS
    cdy = A_WIDTH // CONV_COLS
    n_i = S // T

    def body(b_ref, c_ref, x_ref, dy_ref, ch_ref, xh_ref, bn_ref, dyn_ref, w_ref,
             db_ref, dc_ref, dx_ref, dw_ref):
        i = pl.program_id(1)

        @pl.when(i == 0)
        def _():
            dw_ref[...] = jnp.zeros_like(dw_ref)

        cv = c_ref[...]
        xv = x_ref[...]
        z = cv * xv
        zh = jnp.where(i > 0, ch_ref[...] * xh_ref[...], 0.0)
        z1 = _shift_down(z, zh, 1)
        z2 = _shift_down(z, zh, 2)
        w0, w1, w2 = w_ref[0:1, :], w_ref[1:2, :], w_ref[2:3, :]
        conv = w0 * z2 + w1 * z1 + w2 * z
        dy = dy_ref[...]
        db_ref[...] = (dy * conv).astype(db_ref.dtype)
        dconv = dy * b_ref[...]
        dconv_n = jnp.where(i < n_i - 1, dyn_ref[...] * bn_ref[...], 0.0)
        dz = w2 * dconv + w1 * _shift_up(dconv, dconv_n, 1) + w0 * _shift_up(dconv, dconv_n, 2)
        dc_ref[...] = (dz * xv).astype(dc_ref.dtype)
        dx_ref[...] = (dz * cv).astype(dx_ref.dtype)
        dw_ref[0:1, :] += jnp.sum(dconv * z2, axis=0, keepdims=True)
        dw_ref[1:2, :] += jnp.sum(dconv * z1, axis=0, keepdims=True)
        dw_ref[2:3, :] += jnp.sum(dconv * z, axis=0, keepdims=True)

    wspec = pl.BlockSpec((3, CONV_COLS), lambda j, i: (0, j))
    return pl.pallas_call(
        body, name=name, grid=(B_WIDTH // CONV_COLS, n_i),
        in_specs=[tile(cb), tile(cc), tile(cx), tile(cdy), prev(cc), prev(cx), nxt(cb), nxt(cdy), wspec],
        out_specs=[tile(0), tile(0), tile(0), wspec],
        out_shape=[SDS((S, B_WIDTH), BF16)] * 3 + [SDS((3, B_WIDTH), F32)],
        compiler_params=_cp(2),
    )(p, p, p, dycat, p, p, p, dycat, w)


def _seg_sum(t, bd):
    hi = t.astype(BF16)
    lo = (t - hi.astype(F32)).astype(BF16)
    return jnp.dot(hi, bd, preferred_element_type=F32) + jnp.dot(lo, bd, preferred_element_type=F32)


def _head_norm(x, g, bd):
    rstd = lax.rsqrt(_seg_sum(x * x, bd) * (1.0 / HEAD_DIM) + EPS)
    xhat = x * rstd
    return xhat * g, xhat, rstd


def _head_norm_bwd(dy, g, xhat, rstd, bd):
    dxh = dy * g
    return rstd * (dxh - xhat * (_seg_sum(dxh * xhat, bd) * (1.0 / HEAD_DIM)))


def _band_mask(has_prev):
    row = lax.broadcasted_iota(jnp.int32, (BLK, 2 * BLK), 0)
    col = lax.broadcasted_iota(jnp.int32, (BLK, 2 * BLK), 1)
    first_key = jnp.where(has_prev, 0, BLK)
    return (col >= row) & (col <= row + BLK) & (col >= first_key)


def _residue_rows(r, d):
    return slice(None) if d == 1 else pl.ds(r, BLK, stride=d)


STRIDED_LANES = 128
RESIDUES_PER_PASS = 8


def _step_width(d):
    return PW if d == 1 else STRIDED_LANES


def _n_stack(lane):
    return lane.shape[1] // HEAD_DIM


def _for_residues(d, fn):
    if d == 1:
        fn(0)
    else:
        per_pass = min(d, RESIDUES_PER_PASS)

        def several(i, carry):
            for u in range(per_pass):
                fn(per_pass * i + u)
            return carry
        lax.fori_loop(0, d // per_pass, several, 0)


def _head_mask(lane, j):
    return (lane >= HEAD_DIM * j) & (lane < HEAD_DIM * (j + 1))


def _stack_heads(x, lane):
    return jnp.concatenate([jnp.where(_head_mask(lane, j), x, 0.0) for j in range(_n_stack(lane))], axis=0)


def _unstack_heads(y, lane):
    out = y[:BLK]
    for j in range(1, _n_stack(lane)):
        out = jnp.where(lane >= HEAD_DIM * j, y[BLK * j:BLK * (j + 1)], out)
    return out


def _head_columns(v, lane):
    return jnp.concatenate([jnp.max(jnp.where(_head_mask(lane, j), v, NEG), axis=1, keepdims=True)
                            for j in range(_n_stack(lane))], axis=0)


def _attn_fwd(name, p, g, gq, gk, bd):
    S = p.shape[0]
    d = PATTERN_DILATION[g]
    rows = BLK * d
    hw = _step_width(d)
    nt = (((1,), (1,)), ((), ()))

    def body(q_ref, kc_ref, kp_ref, vc_ref, vp_ref, gq_ref, gk_ref, bd_ref, o_ref, lse_ref):
        has_prev = pl.program_id(1) > 0
        bdv = bd_ref[...]
        band = jnp.concatenate([_band_mask(has_prev)] * (hw // HEAD_DIM), axis=0)
        lane = lax.broadcasted_iota(jnp.int32, (1, hw), 1)

        def residue(r):
            rr = _residue_rows(r, d)
            qn, _, _ = _head_norm(q_ref[rr, :], gq_ref[...], bdv)
            kn, _, _ = _head_norm(jnp.concatenate([kp_ref[rr, :], kc_ref[rr, :]], axis=0), gk_ref[...], bdv)
            knb = kn.astype(BF16)
            vb = jnp.concatenate([vp_ref[rr, :], vc_ref[rr, :]], axis=0).astype(BF16)
            qs = _stack_heads(qn, lane).astype(BF16)
            s = lax.dot_general(qs, knb, nt, preferred_element_type=F32) * (HEAD_DIM ** -0.5)
            s = jnp.where(band, s, NEG)
            m = jnp.max(s, axis=1, keepdims=True)
            e = jnp.exp(s - m)
            den = jnp.sum(e, axis=1, keepdims=True)
            pv = jnp.dot(e.astype(BF16), vb, preferred_element_type=F32)
            o_ref[rr, :] = _unstack_heads(pv / den, lane)
            lse_ref[rr, :] = _unstack_heads(jnp.broadcast_to(m + jnp.log(den), pv.shape), lane)

        _for_residues(d, residue)

    per = PW // hw
    cq, ck, cv = (OFF_Q + PW * g) // hw, (OFF_K + PW * g) // hw, (OFF_V + PW * g) // hw
    cur = lambda col: pl.BlockSpec((rows, hw), lambda h, n: (n, col + h))
    prv = lambda col: pl.BlockSpec((rows, hw), lambda h, n: (jnp.maximum(n - 1, 0), col + h))
    vec = pl.BlockSpec((1, hw), lambda h, n: (0, h))
    return pl.pallas_call(
        body, name=name, grid=(per, S // rows),
        in_specs=[cur(cq), cur(ck), prv(ck), cur(cv), prv(cv), vec, vec, pl.BlockSpec((hw, hw), lambda h, n: (0, 0))],
        out_specs=[cur(0), cur(0)],
        out_shape=[SDS((S, PW), F32)] * 2,
        compiler_params=_cp(2),
    )(p, p, p, p, p, gq, gk, bd)


def _attn_bwd(name, p, g, lse, do3, c3, gq, gk, bd):
    S = p.shape[0]
    d = PATTERN_DILATION[g]
    rows = BLK * d
    nblk = S // rows
    hw = _step_width(d)
    nt = (((1,), (1,)), ((), ()))
    tn = (((0,), (0,)), ((), ()))

    def body(q_ref, kc_ref, kp_ref, vc_ref, vp_ref, lse_ref, do_ref, c_ref, gq_ref, gk_ref, bd_ref,
             dq_ref, dk_ref, dv_ref, dgq_ref, dgk_ref, ck_ref, cv_ref, dq_keep_ref):
        n = pl.program_id(1)

        @pl.when(n == 0)
        def _():
            ck_ref[...] = jnp.zeros_like(ck_ref)
            cv_ref[...] = jnp.zeros_like(cv_ref)
            dgq_ref[...] = jnp.zeros_like(dgq_ref)
            dgk_ref[...] = jnp.zeros_like(dgk_ref)

        @pl.when(n == nblk)
        def _():
            dq_ref[...] = dq_keep_ref[...]
            dk_ref[...] = ck_ref[...]
            dv_ref[...] = cv_ref[...]

        bdv = bd_ref[...]
        gqv = gq_ref[...]
        gkv = gk_ref[...]
        band = jnp.concatenate([_band_mask(n > 0)] * (hw // HEAD_DIM), axis=0)
        lane = lax.broadcasted_iota(jnp.int32, (1, hw), 1)

        def residue(r):
            rr = _residue_rows(r, d)
            qn, qhat, qrstd = _head_norm(q_ref[rr, :], gqv, bdv)
            kn, khat, krstd = _head_norm(jnp.concatenate([kp_ref[rr, :], kc_ref[rr, :]], axis=0), gkv, bdv)
            knb = kn.astype(BF16)
            vb = jnp.concatenate([vp_ref[rr, :], vc_ref[rr, :]], axis=0).astype(BF16)
            qs = _stack_heads(qn, lane).astype(BF16)
            dos = _stack_heads(do_ref[rr, :], lane).astype(BF16)
            s = lax.dot_general(qs, knb, nt, preferred_element_type=F32) * (HEAD_DIM ** -0.5)
            prob = jnp.where(band, jnp.exp(s - _head_columns(lse_ref[rr, :], lane)), 0.0)
            dp = lax.dot_general(dos, vb, nt, preferred_element_type=F32)
            ds = (prob * (dp + _head_columns(c_ref[rr, :], lane)) * (HEAD_DIM ** -0.5)).astype(BF16)
            dqn = _unstack_heads(jnp.dot(ds, knb, preferred_element_type=F32), lane)
            dkn = lax.dot_general(ds, qs, tn, preferred_element_type=F32)
            dvv = lax.dot_general(prob.astype(BF16), dos, tn, preferred_element_type=F32)

            dq = _head_norm_bwd(dqn, gqv, qhat, qrstd, bdv)
            dq_ref[rr, :] = dq
            dq_keep_ref[rr, :] = dq
            dk2 = _head_norm_bwd(dkn, gkv, khat, krstd, bdv)
            dgq_ref[...] += jnp.sum(dqn * qhat, axis=0, keepdims=True)
            dgk_ref[...] += jnp.sum(dkn * khat, axis=0, keepdims=True)
            dk_ref[rr, :] = ck_ref[rr, :] + dk2[:BLK]
            dv_ref[rr, :] = cv_ref[rr, :] + dvv[:BLK]
            ck_ref[rr, :] = dk2[BLK:]
            cv_ref[rr, :] = dvv[BLK:]

        @pl.when(n < nblk)
        def _():
            _for_residues(d, residue)

    last = nblk - 1
    per = PW // hw
    cq, ck, cv = (OFF_Q + PW * g) // hw, (OFF_K + PW * g) // hw, (OFF_V + PW * g) // hw
    cur = lambda col: pl.BlockSpec((rows, hw), lambda h, n: (jnp.minimum(n, last), col + h))
    prv = lambda col: pl.BlockSpec((rows, hw), lambda h, n: (jnp.maximum(jnp.minimum(n, last) - 1, 0), col + h))
    cur3 = pl.BlockSpec((None, rows, hw), lambda h, n: (g, jnp.minimum(n, last), h))
    done = pl.BlockSpec((rows, hw), lambda h, n: (jnp.maximum(n - 1, 0), h))
    vec = pl.BlockSpec((1, hw), lambda h, n: (0, h))
    return pl.pallas_call(
        body, name=name, grid=(per, nblk + 1),
        in_specs=[cur(cq), cur(ck), prv(ck), cur(cv), prv(cv), cur(0), cur3, cur3, vec, vec,
                  pl.BlockSpec((hw, hw), lambda h, n: (0, 0))],
        out_specs=[cur(0), done, done, vec, vec],
        out_shape=[SDS((S, PW), F32)] * 3 + [SDS((1, PW), F32)] * 2,
        scratch_shapes=[pltpu.VMEM((rows, hw), F32)] * 3,
        compiler_params=_cp(2),
    )(p, p, p, p, p, lse, do3, c3, gq, gk, bd)


def _mix_fwd(name, os, lses):
    S = os[0].shape[0]
    tm = min(1024, S)

    def body(o0, o1, o2, l0, l1, l2, y_ref):
        o = [o0[...], o1[...], o2[...]]
        l = [l0[...], l1[...], l2[...]]
        m = jnp.maximum(jnp.maximum(l[0], l[1]), l[2])
        e = [jnp.exp(t - m) for t in l]
        inv = 1.0 / (e[0] + e[1] + e[2])
        for g in range(N_PATTERNS):
            y_ref[:, PW * g:PW * (g + 1)] = (o[g] * (e[g] * inv)).astype(y_ref.dtype)

    blk = pl.BlockSpec((tm, PW), lambda i: (i, 0))
    return pl.pallas_call(
        body, name=name, grid=(S // tm,),
        in_specs=[blk] * 6,
        out_specs=pl.BlockSpec((tm, C_WIDTH), lambda i: (i, 0)),
        out_shape=SDS((S, C_WIDTH), BF16),
        compiler_params=_cp(1),
    )(*os, *lses)


def _mix_bwd(name, os, lses, dycat, bd):
    S = os[0].shape[0]
    tm = min(1024, S)
    c0 = (A_WIDTH + B_WIDTH) // PW

    def body(o0, o1, o2, l0, l1, l2, dy0_ref, dy1_ref, dy2_ref, bd_ref, do_ref, c_ref):
        bdv = bd_ref[...]
        o = [o0[...], o1[...], o2[...]]
        l = [l0[...], l1[...], l2[...]]
        dys = [dy0_ref[...], dy1_ref[...], dy2_ref[...]]
        m = jnp.maximum(jnp.maximum(l[0], l[1]), l[2])
        e = [jnp.exp(t - m) for t in l]
        inv = 1.0 / (e[0] + e[1] + e[2])
        alpha = [t * inv for t in e]
        da = [_seg_sum(dys[g] * o[g], bdv) for g in range(N_PATTERNS)]
        mean_da = alpha[0] * da[0] + alpha[1] * da[1] + alpha[2] * da[2]
        for g in range(N_PATTERNS):
            do_ref[g] = dys[g] * alpha[g]
            c_ref[g] = -alpha[g] * mean_da

    blk = pl.BlockSpec((tm, PW), lambda i: (i, 0))
    blk3 = pl.BlockSpec((N_PATTERNS, tm, PW), lambda i: (0, i, 0))
    dyspec = lambda g: pl.BlockSpec((tm, PW), lambda i: (i, c0 + g))
    return pl.pallas_call(
        body, name=name, grid=(S // tm,),
        in_specs=[blk] * 6 + [dyspec(0), dyspec(1), dyspec(2), pl.BlockSpec((PW, PW), lambda i: (0, 0))],
        out_specs=[blk3, blk3],
        out_shape=[SDS((N_PATTERNS, S, PW), F32)] * 2,
        compiler_params=_cp(1),
    )(*os, *lses, dycat, dycat, dycat, bd)


def _mesh_pos():
    x, y, c = lax.axis_index("x"), lax.axis_index("y"), lax.axis_index("c")
    chips = [(1 - x, y), (x, 1 - y), (1 - x, 1 - y)]
    chip_idx = [2 * cx + cy for cx, cy in chips]
    return x, y, c, 2 * x + y, chips, chip_idx


def _place_shard(name, w, layer, chip_arr, out_dtype, deps=()):
    _, R, C = w.shape
    tr = min(256, R)

    def body(chip_ref, w_ref, *rest):
        o_ref = rest[-1]
        o_ref[...] = w_ref[...].astype(o_ref.dtype)

    return pl.pallas_call(
        body, name=name,
        grid_spec=pltpu.PrefetchScalarGridSpec(
            num_scalar_prefetch=1, grid=(R // tr,),
            in_specs=[pl.BlockSpec((None, tr, C), lambda i, chip_ref: (layer, i, 0))] + [_hbm_spec()] * len(deps),
            out_specs=pl.BlockSpec((None, tr, C), lambda i, chip_ref: (chip_ref[0], i, 0))),
        out_shape=SDS((N_CHIPS, R, C), out_dtype),
        compiler_params=_cp(1),
    )(chip_arr, w, *deps)


HBM_SPEC = pl.BlockSpec(memory_space=pltpu.HBM)
SEM_SPEC = pl.BlockSpec(memory_space=pltpu.SEMAPHORE)
SPLIT_COPY = pltpu.SideEffectType.DATAFLOW_SIDE_EFFECTING
N_PEER_CHIPS = N_CHIPS - 1
TOKEN_SHAPE = SDS((8, 128), F32)
TOKEN_SPEC = pl.BlockSpec(memory_space=pltpu.VMEM)


def _in_hbm(a):
    return pltpu.with_memory_space_constraint(a, pltpu.HBM)


def _gather_start(name, bufs):
    T = len(bufs)

    def body(*refs):
        ins = refs[:T]
        send_sems, recv_sems = refs[T:2 * T], refs[2 * T:3 * T]
        token = refs[4 * T]
        x, y, c, me, chips, chip_idx = _mesh_pos()
        for t in range(T):
            hr = ins[t].shape[1] // 2
            mine = ins[t].at[me, pl.ds(c * hr, hr), :]
            for j in range(N_PEER_CHIPS):
                pltpu.make_async_remote_copy(src_ref=mine, dst_ref=mine, send_sem=send_sems[t].at[j],
                                             recv_sem=recv_sems[t].at[j], device_id=(*chips[j], c),
                                             device_id_type=MESH).start()
        token[...] = jnp.zeros_like(token)

    sems = [pltpu.SemaphoreType.DMA((N_PEER_CHIPS,))] * T
    out = pl.pallas_call(
        body, name=name,
        in_specs=[HBM_SPEC] * T,
        out_specs=[SEM_SPEC] * (2 * T) + [HBM_SPEC] * T + [TOKEN_SPEC],
        out_shape=sems + sems + [pltpu.HBM(b.shape, b.dtype) for b in bufs] + [TOKEN_SHAPE],
        input_output_aliases={t: 2 * T + t for t in range(T)},
        compiler_params=pltpu.CompilerParams(has_side_effects=SPLIT_COPY),
    )(*[_in_hbm(b) for b in bufs])
    return out[:T], out[T:2 * T], out[2 * T:3 * T], out[3 * T]


def _gather_wait(name, buf, send_sem, recv_sem, after):
    n_in = 3 if after is None else 4

    def body(*refs):
        buf_ref, ssem, rsem = refs[:3]
        x, y, c, me, chips, chip_idx = _mesh_pos()
        hr = buf_ref.shape[1] // 2
        mine = buf_ref.at[me, pl.ds(c * hr, hr), :]
        for j in range(N_PEER_CHIPS):
            got = buf_ref.at[chip_idx[j], pl.ds(c * hr, hr), :]
            cp = pltpu.make_async_remote_copy(src_ref=mine, dst_ref=got, send_sem=ssem.at[j], recv_sem=rsem.at[j],
                                              device_id=(*chips[j], c), device_id_type=MESH)
            cp.wait_send()
            cp.wait_recv()

    args = [buf, send_sem, recv_sem] + ([] if after is None else [after])
    return pl.pallas_call(
        body, name=name,
        in_specs=[HBM_SPEC, SEM_SPEC, SEM_SPEC] + [_hbm_spec()] * (n_in - 3),
        out_specs=HBM_SPEC,
        out_shape=pltpu.HBM(buf.shape, buf.dtype),
        input_output_aliases={0: 0},
        compiler_params=pltpu.CompilerParams(has_side_effects=SPLIT_COPY),
    )(*args)


def _forward_start(name, buf):
    def body(buf_ref, send_sems, recv_sems, buf_thru, token):
        x, y, c, me, chips, chip_idx = _mesh_pos()
        hr = buf_ref.shape[1] // 2
        for j in range(N_PEER_CHIPS):
            got = buf_ref.at[chip_idx[j], pl.ds(c * hr, hr), :]
            pltpu.make_async_remote_copy(src_ref=got, dst_ref=got, send_sem=send_sems.at[j], recv_sem=recv_sems.at[j],
                                         device_id=(x, y, 1 - c), device_id_type=MESH).start()
        token[...] = jnp.zeros_like(token)

    sems = pltpu.SemaphoreType.DMA((N_PEER_CHIPS,))
    return pl.pallas_call(
        body, name=name,
        in_specs=[HBM_SPEC],
        out_specs=[SEM_SPEC, SEM_SPEC, HBM_SPEC, TOKEN_SPEC],
        out_shape=[sems, sems, pltpu.HBM(buf.shape, buf.dtype), TOKEN_SHAPE],
        input_output_aliases={0: 2},
        compiler_params=pltpu.CompilerParams(has_side_effects=SPLIT_COPY),
    )(_in_hbm(buf))


def _forward_wait(name, buf, send_sems, recv_sems, after):
    n_in = 3 if after is None else 4

    def body(*refs):
        buf_ref, ssems, rsems = refs[:3]
        x, y, c, me, chips, chip_idx = _mesh_pos()
        hr = buf_ref.shape[1] // 2
        for j in range(N_PEER_CHIPS):
            sent = buf_ref.at[chip_idx[j], pl.ds(c * hr, hr), :]
            theirs = buf_ref.at[chip_idx[j], pl.ds((1 - c) * hr, hr), :]
            cp = pltpu.make_async_remote_copy(src_ref=sent, dst_ref=theirs, send_sem=ssems.at[j],
                                              recv_sem=rsems.at[j], device_id=(x, y, 1 - c), device_id_type=MESH)
            cp.wait_send()
            cp.wait_recv()

    args = [buf, send_sems, recv_sems] + ([] if after is None else [after])
    return pl.pallas_call(
        body, name=name,
        in_specs=[HBM_SPEC, SEM_SPEC, SEM_SPEC] + [_hbm_spec()] * (n_in - 3),
        out_specs=HBM_SPEC,
        out_shape=pltpu.HBM(buf.shape, buf.dtype),
        input_output_aliases={0: 0},
        compiler_params=pltpu.CompilerParams(has_side_effects=SPLIT_COPY),
    )(*args)


class _GatheredWeights:
    def __init__(self):
        self._order = []
        self._pending = {}
        self._forwarding = {}
        self._ready = {}
        self._tokens = []

    def start(self, keys, bufs):
        send_sems, recv_sems, thru, token = _gather_start(f"gather_start_{len(self._order)}", bufs)
        self._tokens.append(token)
        self._order.extend(keys)
        self._pending.update({k: (b, s, r) for k, b, s, r in zip(keys, thru, send_sems, recv_sems)})

    def _prefetch(self, key, after):
        if key in self._pending:
            buf, ssem, rsem = self._pending.pop(key)
            tag = f"{key[0]}_{key[1]}"
            buf = _gather_wait(f"gather_wait_{tag}", buf, ssem, rsem, after)
            ssems, rsems, buf, token = _forward_start(f"gather_fwd_start_{tag}", buf)
            self._forwarding[key] = (buf, ssems, rsems)
            self._tokens.append(token)

    def get(self, name, layer, after=None, prefetch_next=True):
        key = (name, layer)
        if key not in self._ready:
            self._prefetch(key, after)
            buf, ssems, rsems = self._forwarding.pop(key)
            self._ready[key] = _forward_wait(f"gather_fwd_wait_{name}_{layer}", buf, ssems, rsems, after)
            if prefetch_next:
                self.prefetch_after(name, layer, after)
        return self._ready[key]

    def prefetch_after(self, name, layer, after):
        nxt = self._order.index((name, layer)) + 1
        if nxt < len(self._order):
            self._prefetch(self._order[nxt], after)

    def deps(self):
        tokens, self._tokens = self._tokens, []
        return tokens


def _swap_copy(g_ref, land_ref, send_sem, recv_sem):
    x, y, c, _, _, _ = _mesh_pos()
    hr = g_ref.shape[1] // 2
    return pltpu.make_async_remote_copy(src_ref=g_ref.at[:, pl.ds((1 - c) * hr, hr), :], dst_ref=land_ref,
                                        send_sem=send_sem, recv_sem=recv_sem, device_id=(x, y, 1 - c),
                                        device_id_type=MESH)


def _swap_start(name, g):
    land_shape = (g.shape[0], g.shape[1] // 2, g.shape[2])

    def body(g_ref, land_ref, send_sem, recv_sem, land_thru, token):
        _swap_copy(g_ref, land_ref, send_sem, recv_sem).start()
        token[...] = jnp.zeros_like(token)

    return pl.pallas_call(
        body, name=name,
        in_specs=[HBM_SPEC, HBM_SPEC],
        out_specs=[SEM_SPEC, SEM_SPEC, HBM_SPEC, TOKEN_SPEC],
        out_shape=[pltpu.SemaphoreType.DMA(()), pltpu.SemaphoreType.DMA(()), pltpu.HBM(land_shape, g.dtype),
                   TOKEN_SHAPE],
        input_output_aliases={1: 2},
        compiler_params=pltpu.CompilerParams(has_side_effects=SPLIT_COPY),
    )(_in_hbm(g), _in_hbm(lax.empty(land_shape, g.dtype)))


def _swap_wait(name, g, land, send_sem, recv_sem, after):
    def body(g_ref, land_ref, send_sem, recv_sem, after_ref, land_out):
        cp = _swap_copy(g_ref, land_ref, send_sem, recv_sem)
        cp.wait_send()
        cp.wait_recv()

    return pl.pallas_call(
        body, name=name,
        in_specs=[HBM_SPEC, HBM_SPEC, SEM_SPEC, SEM_SPEC, _hbm_spec()],
        out_specs=HBM_SPEC,
        out_shape=pltpu.HBM(land.shape, land.dtype),
        input_output_aliases={1: 0},
        compiler_params=pltpu.CompilerParams(has_side_effects=SPLIT_COPY),
    )(_in_hbm(g), land, send_sem, recv_sem, after)


def _add_my_half(name, g, r, pos_arr):
    ns, R, C = g.shape
    hr = R // 2
    tr = min(256, hr)
    nt = hr // tr

    def body(pos_ref, g_ref, r_ref, o_ref, land_ref):
        t = (g_ref[...] + r_ref[...]).astype(o_ref.dtype)
        o_ref[...] = t

        @pl.when(pl.program_id(1) == pos_ref[1])
        def _():
            land_ref[...] = t

    blk = pl.BlockSpec((None, tr, C), lambda i, s, pos_ref: (s, i, 0))
    return pl.pallas_call(
        body, name=name,
        grid_spec=pltpu.PrefetchScalarGridSpec(
            num_scalar_prefetch=1, grid=(nt, ns),
            in_specs=[pl.BlockSpec((None, tr, C), lambda i, s, pos_ref: (s, pos_ref[0] * nt + i, 0)), blk],
            out_specs=[blk, pl.BlockSpec((None, tr, C), lambda i, s, pos_ref: (pos_ref[1], i, 0))]),
        out_shape=[SDS((ns, hr, C), BF16)] * 2,
        compiler_params=_cp(2),
    )(pos_arr, g, r)


def _exchange_start(name, part, land):
    def body(part_ref, land_ref, send_sems, recv_sems, land_thru, token):
        x, y, c, me, chips, chip_idx = _mesh_pos()
        for j in range(N_PEER_CHIPS):
            pltpu.make_async_remote_copy(src_ref=part_ref.at[chip_idx[j]], dst_ref=land_ref.at[me],
                                         send_sem=send_sems.at[j], recv_sem=recv_sems.at[j],
                                         device_id=(*chips[j], c), device_id_type=MESH).start()
        token[...] = jnp.zeros_like(token)

    sems = pltpu.SemaphoreType.DMA((N_PEER_CHIPS,))
    return pl.pallas_call(
        body, name=name,
        in_specs=[HBM_SPEC, HBM_SPEC],
        out_specs=[SEM_SPEC, SEM_SPEC, HBM_SPEC, TOKEN_SPEC],
        out_shape=[sems, sems, pltpu.HBM(land.shape, land.dtype), TOKEN_SHAPE],
        input_output_aliases={1: 2},
        compiler_params=pltpu.CompilerParams(has_side_effects=SPLIT_COPY),
    )(_in_hbm(part), _in_hbm(land))


def _exchange_wait(name, part, land, send_sems, recv_sems, after):
    def body(part_ref, land_ref, send_sems, recv_sems, after_ref, land_out):
        x, y, c, me, chips, chip_idx = _mesh_pos()
        for j in range(N_PEER_CHIPS):
            cp = pltpu.make_async_remote_copy(src_ref=part_ref.at[chip_idx[j]], dst_ref=land_ref.at[chip_idx[j]],
                                              send_sem=send_sems.at[j], recv_sem=recv_sems.at[j],
                                              device_id=(*chips[j], c), device_id_type=MESH)
            cp.wait_send()
            cp.wait_recv()

    return pl.pallas_call(
        body, name=name,
        in_specs=[HBM_SPEC, HBM_SPEC, SEM_SPEC, SEM_SPEC, _hbm_spec()],
        out_specs=HBM_SPEC,
        out_shape=pltpu.HBM(land.shape, land.dtype),
        input_output_aliases={1: 0},
        compiler_params=pltpu.CompilerParams(has_side_effects=SPLIT_COPY),
    )(_in_hbm(part), land, send_sems, recv_sems, after)


class _GradReducer:
    def __init__(self, c_arr):
        self._c_arr = c_arr
        self._swapping = []
        self._exchanging = {}
        self._joining = {}
        self._tokens = []

    def begin(self, name, layer, g):
        tag = f"{name}_{layer}"
        ssem, rsem, land, token = _swap_start(f"rs_swap_start_{tag}", g)
        self._swapping.append((name, layer, g, ssem, rsem, land))
        self._tokens.append(token)

    def advance(self, after):
        for name, layer, g, ssem, rsem, land in self._swapping:
            tag = f"{name}_{layer}"
            theirs = _swap_wait(f"rs_swap_wait_{tag}", g, land, ssem, rsem, after)
            part, own = _add_my_half(f"rs_add_{tag}", g, theirs, self._c_arr)
            ssems, rsems, land2, token = _exchange_start(f"rs_xchg_start_{tag}", part, own)
            self._exchanging[(name, layer)] = (part, ssems, rsems, land2)
            self._tokens.append(token)
        self._swapping = []

    def deps(self):
        tokens, self._tokens = self._tokens, []
        return tokens

    def reduce(self, name, n_layers, after):
        buf = None
        for layer in range(n_layers):
            part, ssems, rsems, land = self._exchanging.pop((name, layer))
            tag = f"{name}_{layer}"
            landed = _exchange_wait(f"rs_xchg_wait_{tag}", part, land, ssems, rsems, after)
            buf = _sum_chips(f"rs_sum_{tag}", landed, self._c_arr, layer, n_layers, buf)
        ssem, rsem, buf, token = _join_start(f"rs_join_start_{name}", buf)
        self._joining[name] = (buf, ssem, rsem)
        return token

    def reduced(self, name, after):
        buf, ssem, rsem = self._joining.pop(name)
        return _join_wait(f"rs_join_wait_{name}", buf, ssem, rsem, after)


def _sum_chips(name, r, c_arr, layer, n_layers, prev):
    ns, H, C = r.shape
    tr = min(256, H)
    nt = H // tr

    def body(c_ref, r_ref, *rest):
        o_ref = rest[-1]
        o_ref[...] = ((r_ref[0].astype(F32) + r_ref[1].astype(F32)) + r_ref[2].astype(F32)) + r_ref[3].astype(F32)

    in_specs = [pl.BlockSpec((ns, tr, C), lambda i, c_ref: (0, i, 0))]
    args = [c_arr, r]
    aliases = {}
    if prev is not None:
        in_specs.append(_hbm_spec())
        args.append(prev)
        aliases = {2: 0}
    return pl.pallas_call(
        body, name=name,
        grid_spec=pltpu.PrefetchScalarGridSpec(
            num_scalar_prefetch=1, grid=(nt,), in_specs=in_specs,
            out_specs=pl.BlockSpec((None, tr, C), lambda i, c_ref: (layer, c_ref[0] * nt + i, 0))),
        out_shape=SDS((n_layers, 2 * H, C), F32),
        input_output_aliases=aliases,
        compiler_params=_cp(1),
    )(*args)


def _join_copy(buf_ref, send_sem, recv_sem):
    x, y, c, _, _, _ = _mesh_pos()
    hr = buf_ref.shape[1] // 2
    mine = buf_ref.at[:, pl.ds(c * hr, hr), :]
    theirs = buf_ref.at[:, pl.ds((1 - c) * hr, hr), :]
    send = pltpu.make_async_remote_copy(src_ref=mine, dst_ref=mine, send_sem=send_sem, recv_sem=recv_sem,
                                        device_id=(x, y, 1 - c), device_id_type=MESH)
    arrive = pltpu.make_async_remote_copy(src_ref=theirs, dst_ref=theirs, send_sem=send_sem, recv_sem=recv_sem,
                                          device_id=(x, y, 1 - c), device_id_type=MESH)
    return send, arrive


def _join_start(name, buf):
    def body(buf_ref, send_sem, recv_sem, buf_thru, token):
        _join_copy(buf_ref, send_sem, recv_sem)[0].start()
        token[...] = jnp.zeros_like(token)

    return pl.pallas_call(
        body, name=name,
        in_specs=[HBM_SPEC],
        out_specs=[SEM_SPEC, SEM_SPEC, HBM_SPEC, TOKEN_SPEC],
        out_shape=[pltpu.SemaphoreType.DMA(()), pltpu.SemaphoreType.DMA(()), pltpu.HBM(buf.shape, buf.dtype),
                   TOKEN_SHAPE],
        input_output_aliases={0: 2},
        compiler_params=pltpu.CompilerParams(has_side_effects=SPLIT_COPY),
    )(_in_hbm(buf))


def _join_wait(name, buf, send_sem, recv_sem, after):
    def body(buf_ref, send_sem, recv_sem, after_ref, buf_out):
        send, arrive = _join_copy(buf_ref, send_sem, recv_sem)
        send.wait_send()
        arrive.wait_recv()

    return pl.pallas_call(
        body, name=name,
        in_specs=[HBM_SPEC, SEM_SPEC, SEM_SPEC, _hbm_spec()],
        out_specs=HBM_SPEC,
        out_shape=pltpu.HBM(buf.shape, buf.dtype),
        input_output_aliases={0: 0},
        compiler_params=pltpu.CompilerParams(has_side_effects=SPLIT_COPY),
    )(buf, send_sem, recv_sem, after)


def _small_copy(k, buf_ref, land_ref, send_sems, recv_sems):
    x, y, c = lax.axis_index("x"), lax.axis_index("y"), lax.axis_index("c")
    me = 4 * x + 2 * y + c
    peer = (x ^ ((k >> 2) & 1), y ^ ((k >> 1) & 1), c ^ (k & 1))
    cp = pltpu.make_async_remote_copy(src_ref=buf_ref, dst_ref=land_ref.at[me], send_sem=send_sems.at[k - 1],
                                      recv_sem=recv_sems.at[k - 1], device_id=peer, device_id_type=MESH)
    return me, peer, cp


def _small_start(buf, deps):
    land = jnp.broadcast_to(buf[None], (N_DEV,) + buf.shape)
    n_dep = len(deps)

    def body(buf_ref, land_ref, *rest):
        send_sems, recv_sems, _, token = rest[n_dep:]
        for k in range(1, N_DEV):
            _small_copy(k, buf_ref, land_ref, send_sems, recv_sems)[2].start()
        token[...] = jnp.zeros_like(token)

    sems = pltpu.SemaphoreType.DMA((N_DEV - 1,))
    return pl.pallas_call(
        body, name="small_gather_start",
        in_specs=[HBM_SPEC, HBM_SPEC] + [_hbm_spec()] * n_dep,
        out_specs=[SEM_SPEC, SEM_SPEC, HBM_SPEC, TOKEN_SPEC],
        out_shape=[sems, sems, pltpu.HBM(land.shape, land.dtype), TOKEN_SHAPE],
        input_output_aliases={1: 2},
        compiler_params=pltpu.CompilerParams(has_side_effects=SPLIT_COPY),
    )(_in_hbm(buf), _in_hbm(land), *deps)


def _small_wait(buf, land, send_sems, recv_sems, after):
    def body(buf_ref, land_ref, send_sems, recv_sems, after_ref, land_out):
        for k in range(1, N_DEV):
            me, peer, cp = _small_copy(k, buf_ref, land_ref, send_sems, recv_sems)
            cp.wait_send()
            got = land_ref.at[me ^ k]
            pltpu.make_async_remote_copy(src_ref=got, dst_ref=got, send_sem=send_sems.at[k - 1],
                                         recv_sem=recv_sems.at[k - 1], device_id=peer,
                                         device_id_type=MESH).wait_recv()

    return pl.pallas_call(
        body, name="small_gather_wait",
        in_specs=[HBM_SPEC, HBM_SPEC, SEM_SPEC, SEM_SPEC, _hbm_spec()],
        out_specs=HBM_SPEC,
        out_shape=pltpu.HBM(land.shape, land.dtype),
        input_output_aliases={1: 0},
        compiler_params=pltpu.CompilerParams(has_side_effects=SPLIT_COPY),
    )(_in_hbm(buf), land, send_sems, recv_sems, after)


def _sum_devices(land):
    n, R, C = land.shape

    def body(land_ref, out_ref):
        acc = land_ref[0]
        for d in range(1, n):
            acc = acc + land_ref[d]
        out_ref[...] = acc

    return pl.pallas_call(
        body, name="small_sum",
        in_specs=[pl.BlockSpec(memory_space=pltpu.VMEM)],
        out_specs=pl.BlockSpec(memory_space=pltpu.VMEM),
        out_shape=SDS((R, C), land.dtype),
        compiler_params=pltpu.CompilerParams(vmem_limit_bytes=V7X_VMEM_LIMIT),
    )(land)


def _pack_rows(vectors):
    flat = jnp.concatenate([v.reshape(-1) for v in vectors])
    n = flat.shape[0]
    padded = -(-n // 1024) * 1024
    return jnp.pad(flat, (0, padded - n)).reshape(padded // 128, 128)


def _unpack_rows(buf, shapes):
    flat = buf.reshape(-1)
    out, off = [], 0
    for s in shapes:
        n = 1
        for dim in s:
            n *= dim
        out.append(flat[off:off + n].reshape(s))
        off += n
    return out


def _layer_forward(l, x, prm, wg):
    S, D = x.shape
    h = _rmsnorm_fwd(f"attn_norm_{l}", x, prm["attn_norm"][l])
    w_in = wg.get("w_in", l, h, prefetch_next=l > 0)
    ns_in = w_in.shape[-1]
    tmi = min(1024, S)
    p = _matmul(
        f"in_proj_{l}", h, w_in, (S, N_CHIPS * ns_in), F32, grid=(S // tmi, N_CHIPS, 1),
        a_spec=pl.BlockSpec((tmi, D), lambda i, j, k: (i, 0)),
        b_spec=pl.BlockSpec((None, D, ns_in), lambda i, j, k: (j, 0, 0)),
        o_spec=pl.BlockSpec((tmi, ns_in), lambda i, j, k: (i, j)),
        contract=(1, 0), acc_shape=(tmi, ns_in), deps=wg.deps())
    if l == 0:
        wg.prefetch_after("w_in", l, p)
    y_a = _sgu_fwd(f"sgu_fwd_{l}", p, prm["sgu_wt"][l], prm["sgu_bb"][l])
    y_b = _conv_fwd(f"conv_fwd_{l}", p, prm["conv_w"][l])
    os, lses = [], []
    for g in range(N_PATTERNS):
        o_g, lse_g = _attn_fwd(f"attn_fwd_{l}_{g}", p, g, prm["q_gain"][l], prm["k_gain"][l], prm["bd"])
        os.append(o_g)
        lses.append(lse_g)
    y_c = _mix_fwd(f"mix_fwd_{l}", os, lses)
    ycat = jnp.concatenate([y_a, y_b, y_c], axis=1)
    tmb, tnb = min(1024, S), min(1024, D)
    w_out = wg.get("w_out", l, ycat)
    kq = N_CHIPS * w_out.shape[1]
    tmo, tno = min(512, S), D
    x1 = _matmul(
        f"out_proj_{l}", ycat, w_out.reshape(kq, D), (S, D), F32, grid=(S // tmo, D // tno, 1),
        a_spec=pl.BlockSpec((tmo, kq), lambda i, j, k: (i, 0)),
        b_spec=pl.BlockSpec((kq, tno), lambda i, j, k: (0, j)),
        o_spec=pl.BlockSpec((tmo, tno), lambda i, j, k: (i, j)),
        contract=(1, 0), acc_shape=(tmo, tno),
        extras=(x,), extra_specs=(pl.BlockSpec((tmo, tno), lambda i, j, k: (i, j)),),
        epi=lambda r, res: r + res, deps=wg.deps())
    w_mlp_in = wg.get("w_mlp_in", l, x1)
    h2 = _rmsnorm_fwd(f"mlp_norm_{l}", x1, prm["mlp_norm"][l])
    nf4 = w_mlp_in.shape[-1]
    r = _matmul(
        f"mlp_in_{l}", h2, w_mlp_in, (S, N_CHIPS * nf4), BF16, grid=(S // tmb, N_CHIPS, 1),
        a_spec=pl.BlockSpec((tmb, D), lambda i, j, k: (i, 0)),
        b_spec=pl.BlockSpec((None, D, nf4), lambda i, j, k: (j, 0, 0)),
        o_spec=pl.BlockSpec((tmb, nf4), lambda i, j, k: (i, j)),
        contract=(1, 0), acc_shape=(tmb, nf4), epi=_relu, deps=wg.deps())
    w_mlp_out = wg.get("w_mlp_out", l, r)
    dff4 = w_mlp_out.shape[1]
    tk = min(2048, dff4)
    kpc = dff4 // tk
    x2 = _matmul(
        f"mlp_out_{l}", r, w_mlp_out, (S, D), F32, grid=(S // tmb, D // tnb, N_CHIPS * kpc),
        a_spec=pl.BlockSpec((tmb, tk), lambda i, j, k: (i, k)),
        b_spec=pl.BlockSpec((None, tk, tnb), lambda i, j, k: (k // kpc, k % kpc, j)),
        o_spec=pl.BlockSpec((tmb, tnb), lambda i, j, k: (i, j)),
        contract=(1, 0), acc_shape=(tmb, tnb), a_pre=_square,
        extras=(x1,), extra_specs=(pl.BlockSpec((tmb, tnb), lambda i, j, k: (i, j)),),
        epi=lambda acc, res: acc + res, deps=wg.deps())
    saved = dict(x=x, p=p, h=h, os=os, lses=lses, ycat=ycat, x1=x1, r=r, h2=h2)
    return x2, saved


def _layer_backward(l, dx2, dx2b, sv, prm, wg, sink):
    S, D = dx2.shape
    w_in, w_out = wg.get("w_in", l), wg.get("w_out", l)
    w_mlp_in, w_mlp_out = wg.get("w_mlp_in", l), wg.get("w_mlp_out", l)
    dff4 = w_mlp_in.shape[-1]
    dff = N_CHIPS * dff4

    tmb, tnb = min(1024, S), min(1024, D)
    da = _matmul(
        f"mlp_out_bwd_{l}", dx2b, w_mlp_out, (S, dff), BF16, grid=(S // tmb, N_CHIPS, 1),
        a_spec=pl.BlockSpec((tmb, D), lambda i, j, k: (i, 0)),
        b_spec=pl.BlockSpec((None, dff4, D), lambda i, j, k: (j, 0, 0)),
        o_spec=pl.BlockSpec((tmb, dff4), lambda i, j, k: (i, j)),
        contract=(1, 1), acc_shape=(tmb, dff4),
        extras=(sv["r"],), extra_specs=(pl.BlockSpec((tmb, dff4), lambda i, j, k: (i, j)),),
        epi=lambda acc, r: acc * (2.0 * r.astype(F32)), deps=sink.deps())
    tmw = min(1024, dff4)
    mpc = dff4 // tmw
    g_w2 = _matmul(
        f"mlp_out_dw_{l}", sv["r"], dx2b, (N_CHIPS, dff4, D), F32, grid=(N_CHIPS * mpc, D // tnb, 1),
        a_spec=pl.BlockSpec((S, tmw), lambda i, j, k: (0, i)),
        b_spec=pl.BlockSpec((S, tnb), lambda i, j, k: (0, j)),
        o_spec=pl.BlockSpec((None, tmw, tnb), lambda i, j, k: (i // mpc, i % mpc, j)),
        contract=(0, 0), acc_shape=(tmw, tnb), a_pre=_square)
    sink.begin("w_mlp_out", l, g_w2)
    tnx = D
    dh2 = _matmul(
        f"mlp_in_bwd_{l}", da, w_mlp_in, (S, D), F32, grid=(S // tmb, D // tnx, N_CHIPS),
        a_spec=pl.BlockSpec((tmb, dff4), lambda i, j, k: (i, k)),
        b_spec=pl.BlockSpec((None, tnx, dff4), lambda i, j, k: (k, j, 0)),
        o_spec=pl.BlockSpec((tmb, tnx), lambda i, j, k: (i, j)),
        contract=(1, 1), acc_shape=(tmb, tnx), deps=sink.deps())
    sink.advance(dh2)
    tmd = min(1024, D)
    nd = D // tmd
    tnf = min(1024, dff4)
    nf = dff4 // tnf
    g_w1 = _matmul(
        f"mlp_in_dw_{l}", sv["h2"], da, (N_CHIPS, D, dff4), F32, grid=(N_CHIPS * nd, nf, 1),
        a_spec=pl.BlockSpec((S, tmd), lambda i, j, k: (0, i % nd)),
        b_spec=pl.BlockSpec((S, tnf), lambda i, j, k: (0, (i // nd) * nf + j)),
        o_spec=pl.BlockSpec((None, tmd, tnf), lambda i, j, k: (i // nd, i % nd, j)),
        contract=(0, 0), acc_shape=(tmd, tnf))
    sink.begin("w_mlp_in", l, g_w1)
    dx1, dx1b, g_mlp_norm = _rmsnorm_bwd(f"mlp_norm_bwd_{l}", dh2, sv["x1"], prm["mlp_norm"][l], dx2,
                                         deps=sink.deps())

    rq = w_out.shape[1]
    kq = N_CHIPS * rq
    dycat = _matmul(
        f"out_proj_bwd_{l}", dx1b, w_out.reshape(kq, D), (S, kq), F32, grid=(S // tmb, 1, 1),
        a_spec=pl.BlockSpec((tmb, D), lambda i, j, k: (i, 0)),
        b_spec=pl.BlockSpec((kq, D), lambda i, j, k: (0, 0)),
        o_spec=pl.BlockSpec((tmb, kq), lambda i, j, k: (i, 0)),
        contract=(1, 1), acc_shape=(tmb, kq))
    sink.advance(dycat)
    g_wout = _matmul(
        f"out_proj_dw_{l}", sv["ycat"], dx1b, (N_CHIPS, rq, D), F32, grid=(N_CHIPS, 1, 1),
        a_spec=pl.BlockSpec((S, rq), lambda i, j, k: (0, i)),
        b_spec=pl.BlockSpec((S, D), lambda i, j, k: (0, 0)),
        o_spec=pl.BlockSpec((None, rq, D), lambda i, j, k: (i, 0, 0)),
        contract=(0, 0), acc_shape=(rq, D))
    sink.begin("w_out", l, g_wout)

    p = sv["p"]
    du, dv_a, g_sgu_w, db_lanes = _sgu_bwd(f"sgu_bwd_{l}", p, dycat, prm["sgu_wt"][l], prm["sgu_wtt"][l],
                                           prm["sgu_bb"][l])
    g_sgu_b = db_lanes[:, :A_HEADS].T
    db, dc, dxb, g_conv = _conv_bwd(f"conv_bwd_{l}", p, dycat, prm["conv_w"][l])
    do3, c3 = _mix_bwd(f"mix_bwd_{l}", sv["os"], sv["lses"], dycat, prm["bd"])
    dqs, dks, dvs, dgqs, dgks = [], [], [], [], []
    for g in range(N_PATTERNS):
        dq, dk, dv, dgq, dgk = _attn_bwd(f"attn_bwd_{l}_{g}", p, g, sv["lses"][g], do3, c3,
                                         prm["q_gain"][l], prm["k_gain"][l], prm["bd"])
        dqs.append(dq)
        dks.append(dk)
        dvs.append(dv)
        dgqs.append(dgq)
        dgks.append(dgk)
    g_q = jnp.concatenate(dgqs, axis=1).reshape(N_PATTERNS * PW // HEAD_DIM, HEAD_DIM).sum(axis=0)
    g_k = jnp.concatenate(dgks, axis=1).reshape(N_PATTERNS * PW // HEAD_DIM, HEAD_DIM).sum(axis=0)
    dp = jnp.concatenate([du, dv_a, db, dc, dxb] + [t.astype(BF16) for t in dqs + dks + dvs], axis=1)

    ns_in = w_in.shape[-1]
    tmh = min(512, D)
    nh = D // tmh
    g_win = _matmul(
        f"in_proj_dw_{l}", sv["h"], dp, (N_CHIPS, D, ns_in), F32, grid=(N_CHIPS * nh, 1, 1),
        a_spec=pl.BlockSpec((S, tmh), lambda i, j, k: (0, i % nh)),
        b_spec=pl.BlockSpec((S, ns_in), lambda i, j, k: (0, i // nh)),
        o_spec=pl.BlockSpec((None, tmh, ns_in), lambda i, j, k: (i // nh, i % nh, 0)),
        contract=(0, 0), acc_shape=(tmh, ns_in))
    sink.begin("w_in", l, g_win)
    dh = _matmul(
        f"in_proj_bwd_{l}", dp, w_in, (S, D), F32, grid=(S // tmb, D // tnx, N_CHIPS),
        a_spec=pl.BlockSpec((tmb, ns_in), lambda i, j, k: (i, k)),
        b_spec=pl.BlockSpec((None, tnx, ns_in), lambda i, j, k: (k, j, 0)),
        o_spec=pl.BlockSpec((tmb, tnx), lambda i, j, k: (i, j)),
        contract=(1, 1), acc_shape=(tmb, tnx), deps=sink.deps())
    sink.advance(dh)
    dx0, dx0b, g_attn_norm = _rmsnorm_bwd(f"attn_norm_bwd_{l}", dh, sv["x"], prm["attn_norm"][l], dx1,
                                          deps=sink.deps())

    big = dict(w_in=g_win, w_out=g_wout, w_mlp_in=g_w1, w_mlp_out=g_w2)
    small = dict(attn_norm=g_attn_norm.reshape(-1), sgu_w=g_sgu_w, sgu_b=g_sgu_b, conv_w=g_conv,
                 q_norm=g_q, k_norm=g_k, mlp_norm=g_mlp_norm.reshape(-1))
    return dx0, dx0b, big, small


BIG = ("w_in", "w_out", "w_mlp_in", "w_mlp_out")
SMALL_REPLICATED = ("attn_norm", "sgu_w", "sgu_b", "q_norm", "k_norm", "mlp_norm")


def _local_step(x, target, prm, wg, n_layers, sink):
    saved = []
    h = x
    for l in range(n_layers):
        h, sv = _layer_forward(l, h, prm, wg)
        saved.append(sv)
    dy, dyb, colsq = _loss_kernel(h, target)
    loss = 0.5 * jnp.sum(colsq) / x.shape[1]
    bigs, smalls = [None] * n_layers, [None] * n_layers
    for l in reversed(range(n_layers)):
        dy, dyb, bigs[l], smalls[l] = _layer_backward(l, dy, dyb, saved[l], prm, wg, sink)
    return loss, dy, bigs, smalls


def _prepare_params(attn_norm, sgu_w, sgu_b, conv_full, q_norm, k_norm, mlp_norm):
    n_layers = attn_norm.shape[0]
    tri = jnp.tril(sgu_w)
    idx = jnp.arange(PW)
    bd = (idx[:, None] // HEAD_DIM == idx[None, :] // HEAD_DIM).astype(BF16)
    return dict(
        attn_norm=[attn_norm[l][None, :] for l in range(n_layers)],
        mlp_norm=[mlp_norm[l][None, :] for l in range(n_layers)],
        sgu_wt=[tri[l].astype(BF16) for l in range(n_layers)],
        sgu_wtt=[tri[l].transpose(0, 2, 1).astype(BF16) for l in range(n_layers)],
        sgu_bb=[jnp.repeat(sgu_b[l].T, HEAD_DIM, axis=1) for l in range(n_layers)],
        conv_w=[conv_full[l] for l in range(n_layers)],
        q_gain=[jnp.tile(q_norm[l], PW // HEAD_DIM)[None, :] for l in range(n_layers)],
        k_gain=[jnp.tile(k_norm[l], PW // HEAD_DIM)[None, :] for l in range(n_layers)],
        bd=bd,
    )


def kernel(x, attn_norm, w_in, sgu_w, sgu_b, conv_w, q_norm, k_norm, w_out, mlp_norm, w_mlp_in, w_mlp_out, loss_target, m_attn_norm, m_w_in, m_sgu_w, m_sgu_b, m_conv_w, m_q_norm, m_k_norm, m_w_out, m_mlp_norm, m_w_mlp_in, m_w_mlp_out, v_attn_norm, v_w_in, v_sgu_w, v_sgu_b, v_conv_w, v_q_norm, v_k_norm, v_w_out, v_mlp_norm, v_w_mlp_in, v_w_mlp_out):
    n_layers = attn_norm.shape[0]
    weights = dict(attn_norm=attn_norm, w_in=w_in, sgu_w=sgu_w, sgu_b=sgu_b, conv_w=conv_w, q_norm=q_norm,
                   k_norm=k_norm, w_out=w_out, mlp_norm=mlp_norm, w_mlp_in=w_mlp_in, w_mlp_out=w_mlp_out)
    mom_m = dict(attn_norm=m_attn_norm, w_in=m_w_in, sgu_w=m_sgu_w, sgu_b=m_sgu_b, conv_w=m_conv_w,
                 q_norm=m_q_norm, k_norm=m_k_norm, w_out=m_w_out, mlp_norm=m_mlp_norm, w_mlp_in=m_w_mlp_in,
                 w_mlp_out=m_w_mlp_out)
    mom_v = dict(attn_norm=v_attn_norm, w_in=v_w_in, sgu_w=v_sgu_w, sgu_b=v_sgu_b, conv_w=v_conv_w,
                 q_norm=v_q_norm, k_norm=v_k_norm, w_out=v_w_out, mlp_norm=v_mlp_norm, w_mlp_in=v_w_mlp_in,
                 w_mlp_out=v_w_mlp_out)
    order = ("attn_norm", "w_in", "sgu_w", "sgu_b", "conv_w", "q_norm", "k_norm", "w_out", "mlp_norm",
             "w_mlp_in", "w_mlp_out")
    chip = 2 * lax.axis_index("x") + lax.axis_index("y")
    c_arr = jnp.stack([lax.axis_index("c"), chip]).astype(jnp.int32)

    conv_cols = conv_w.shape[-1]
    chip_arr = chip.astype(jnp.int32).reshape(1)
    conv_pack = jnp.pad(conv_w.reshape(-1), (0, 2048 - conv_w.size)).reshape(1, 16, 128)
    wg = _GatheredWeights()
    wg.start([("conv_w", 0), ("w_in", 0)],
             [_place_shard("place_conv_w", conv_pack, 0, chip_arr, F32),
              _place_shard("place_w_in_0", weights["w_in"], 0, chip_arr, BF16)])
    keys = [(n, l) for l in range(n_layers) for n in BIG if (n, l) != ("w_in", 0)]
    first = wg.deps()
    wg.start(keys, [_place_shard(f"place_{n}_{l}", weights[n], l, chip_arr, BF16, deps=first) for n, l in keys])
    conv_full = wg.get("conv_w", 0, wg.deps()[-1]).reshape(N_CHIPS, 2048)[:, :conv_w.size].reshape(N_CHIPS, n_layers, 3, conv_cols)
    conv_full = conv_full.transpose(1, 2, 0, 3).reshape(n_layers, 3, N_CHIPS * conv_cols)
    prm = _prepare_params(attn_norm, sgu_w, sgu_b, conv_full, q_norm, k_norm, mlp_norm)

    sink = _GradReducer(c_arr)
    loss_local, grad_x, _, smalls = _local_step(x[0], loss_target[0], prm, wg, n_layers, sink)
    loss = lax.psum(loss_local, ("x", "y", "c"))

    small_names = SMALL_REPLICATED + ("conv_w",)
    small_shapes = [(n_layers,) + tuple(smalls[0][n].shape) for n in small_names]
    packed = _pack_rows([jnp.stack([smalls[l][n] for l in range(n_layers)]) for n in small_names])
    small_send, small_recv, small_land, small_token = _small_start(packed, sink.deps())

    grads, delta, new_m, new_v = {}, {}, {}, {}

    def update(n, after):
        shp = weights[n].shape
        two_d = (shp[0] * shp[1], shp[2])
        d, nm, nv, g = _adamw(f"adamw_{n}", weights[n].reshape(two_d), sink.reduced(n, after).reshape(two_d),
                              mom_m[n].reshape(two_d), mom_v[n].reshape(two_d))
        grads[n], delta[n], new_m[n], new_v[n] = g.reshape(shp), d.reshape(shp), nm.reshape(shp), nv.reshape(shp)

    token = small_token
    for n in ("w_mlp_out", "w_mlp_in", "w_out"):
        token = sink.reduce(n, n_layers, token)
    update("w_mlp_out", token)
    token = sink.reduce("w_in", n_layers, delta["w_mlp_out"])
    update("w_mlp_in", token)
    update("w_out", delta["w_mlp_in"])
    update("w_in", delta["w_out"])
    small_land = _small_wait(packed, small_land, small_send, small_recv, delta["w_in"])
    grads.update(zip(small_names, _unpack_rows(_sum_devices(small_land), small_shapes)))
    grads["conv_w"] = lax.dynamic_slice_in_dim(grads["conv_w"], chip * conv_cols, conv_cols, axis=2)
    smalls_all = SMALL_REPLICATED + ("conv_w",)
    shapes = [weights[n].shape for n in smalls_all]
    d, nm, nv, _ = _adamw("adamw_small",
                          _pack_rows([weights[n] for n in smalls_all]), _pack_rows([grads[n] for n in smalls_all]),
                          _pack_rows([mom_m[n] for n in smalls_all]), _pack_rows([mom_v[n] for n in smalls_all]))
    for n, dd, mm, vv in zip(smalls_all, _unpack_rows(d, shapes), _unpack_rows(nm, shapes), _unpack_rows(nv, shapes)):
        delta[n], new_m[n], new_v[n] = dd, mm, vv

    return (loss, grad_x[None], *[grads[n] for n in order], *[delta[n] for n in order],
            *[new_m[n] for n in order], *[new_v[n] for n in order])
```

```python
import jax
import jax.numpy as jnp
from jax import lax
from jax.experimental import pallas as pl
from jax.experimental.pallas import tpu as pltpu

F32 = jnp.float32
BF16 = jnp.bfloat16
SDS = jax.ShapeDtypeStruct

EPS = 1e-6
HEAD_DIM = 64
A_HEADS = 8
A_WIDTH = 512
CHUNK = 128
B_WIDTH = 768
C_WIDTH = 768
N_PATTERNS = 3
PATTERN_DILATION = (1, 4, 16)
PW = 256
D_IN_PROJ = 5632
OFF_AU, OFF_AV, OFF_BB, OFF_BC, OFF_BX, OFF_Q, OFF_K, OFF_V = 0, 512, 1024, 1792, 2560, 3328, 4096, 4864
N_CHIPS = 4
N_DEV = 8
BLK = 128

ADAM_LR, ADAM_B1, ADAM_B2, ADAM_EPS, ADAM_WD, ADAM_STEP = 0.001, 0.9, 0.999, 1e-08, 0.01, 10

V7X_VMEM_LIMIT = 56 * 1024 * 1024
MESH = pl.DeviceIdType.MESH
NEG = -1e30


def _cp(n_axes):
    return pltpu.CompilerParams(dimension_semantics=("arbitrary",) * n_axes, vmem_limit_bytes=V7X_VMEM_LIMIT)


def _hbm_spec():
    return pl.BlockSpec(memory_space=pl.ANY)


def _relu(t):
    return jnp.maximum(t, 0.0)


def _square(t):
    return t * t


def _residual_and_norm(acc, res, g):
    x1 = acc + res
    return x1, x1 * lax.rsqrt(jnp.mean(x1 * x1, axis=-1, keepdims=True) + EPS) * g


def _matmul(name, a, b, out_shape, out_dtype, *, grid, a_spec, b_spec, o_spec, contract, acc_shape,
            extras=(), extra_specs=(), a_pre=None, epi=None, deps=(), more_outs=()):
    nk = grid[2]
    n_ex = len(extras)
    n_dep = len(deps)
    n_out = 1 + len(more_outs)
    dims = (((contract[0],), (contract[1],)), ((), ()))

    def product(a_ref, b_ref):
        av = a_ref[...] if a_pre is None else a_pre(a_ref[...])
        return lax.dot_general(av, b_ref[...], dims, preferred_element_type=F32)

    def finish(r, ex, o_refs):
        if epi is not None:
            r = epi(r, *[e[...] for e in ex])
        vals = r if n_out > 1 else (r,)
        for o_ref, v in zip(o_refs, vals):
            o_ref[...] = v.astype(o_ref.dtype)

    def body_single(a_ref, b_ref, *rest):
        finish(product(a_ref, b_ref), rest[:n_ex], rest[n_ex + n_dep:n_ex + n_dep + n_out])

    def body(a_ref, b_ref, *rest):
        ex = rest[:n_ex]
        o_ref = rest[n_ex + n_dep:n_ex + n_dep + n_out]
        acc_ref = rest[n_ex + n_dep + n_out]
        k = pl.program_id(2)

        @pl.when(k == 0)
        def _():
            acc_ref[...] = product(a_ref, b_ref)

        @pl.when((k > 0) & (k < nk - 1))
        def _():
            acc_ref[...] += product(a_ref, b_ref)

        @pl.when(k == nk - 1)
        def _():
            finish(acc_ref[...] + product(a_ref, b_ref), ex, o_ref)

    out = pl.pallas_call(
        body_single if nk == 1 else body, name=name, grid=grid,
        in_specs=[a_spec, b_spec, *extra_specs] + [_hbm_spec()] * n_dep,
        out_specs=[o_spec] + [spec for _, _, spec in more_outs],
        out_shape=[SDS(out_shape, out_dtype)] + [SDS(shape, dtype) for shape, dtype, _ in more_outs],
        scratch_shapes=[] if nk == 1 else [pltpu.VMEM(acc_shape, F32)],
        compiler_params=_cp(3),
    )(a, b, *extras, *deps)
    return out if more_outs else out[0]


def _loss_kernel(y, t):
    S, D = y.shape
    tm = min(256, S)

    def body(y_ref, t_ref, dy_ref, dyb_ref, l_ref):
        @pl.when(pl.program_id(0) == 0)
        def _():
            l_ref[...] = jnp.zeros_like(l_ref)
        e = y_ref[...] - t_ref[...]
        l_ref[...] += jnp.sum(e * e, axis=0, keepdims=True)
        dy = e * (1.0 / D)
        dy_ref[...] = dy
        dyb_ref[...] = dy.astype(BF16)

    row = pl.BlockSpec((tm, D), lambda i: (i, 0))
    return pl.pallas_call(
        body, name="loss_head", grid=(S // tm,),
        in_specs=[row, row],
        out_specs=[row, row, pl.BlockSpec((1, D), lambda i: (0, 0))],
        out_shape=[SDS((S, D), F32), SDS((S, D), BF16), SDS((1, D), F32)],
        compiler_params=_cp(1),
    )(y, t)


def _rmsnorm_fwd(name, x, g):
    S, D = x.shape
    tm = min(512, S)

    def body(x_ref, g_ref, h_ref):
        xv = x_ref[...]
        y = xv * lax.rsqrt(jnp.mean(xv * xv, axis=-1, keepdims=True) + EPS) * g_ref[...]
        h_ref[...] = y.astype(h_ref.dtype)

    row = pl.BlockSpec((tm, D), lambda i: (i, 0))
    return pl.pallas_call(
        body, name=name, grid=(S // tm,),
        in_specs=[row, pl.BlockSpec((1, D), lambda i: (0, 0))],
        out_specs=row,
        out_shape=SDS((S, D), BF16),
        compiler_params=_cp(1),
    )(x, g)


def _rmsnorm_bwd(name, dh, x, g, dres, deps=()):
    S, D = x.shape
    tm = min(256, S)
    n_dep = len(deps)

    def body(dh_ref, x_ref, g_ref, dres_ref, *rest):
        dx_ref, dxb_ref, dg_ref = rest[n_dep:]
        @pl.when(pl.program_id(0) == 0)
        def _():
            dg_ref[...] = jnp.zeros_like(dg_ref)
        xv = x_ref[...]
        dhv = dh_ref[...]
        rstd = lax.rsqrt(jnp.mean(xv * xv, axis=-1, keepdims=True) + EPS)
        xhat = xv * rstd
        dg_ref[...] += jnp.sum(dhv * xhat, axis=0, keepdims=True)
        dxn = dhv * g_ref[...]
        dx = dres_ref[...] + rstd * (dxn - xhat * jnp.mean(dxn * xhat, axis=-1, keepdims=True))
        dx_ref[...] = dx
        dxb_ref[...] = dx.astype(BF16)

    row = pl.BlockSpec((tm, D), lambda i: (i, 0))
    vec = pl.BlockSpec((1, D), lambda i: (0, 0))
    return pl.pallas_call(
        body, name=name, grid=(S // tm,),
        in_specs=[row, row, vec, row] + [_hbm_spec()] * n_dep,
        out_specs=[row, row, vec],
        out_shape=[SDS((S, D), F32), SDS((S, D), BF16), SDS((1, D), F32)],
        compiler_params=_cp(1),
    )(dh, x, g, dres, *deps)


def _adamw(name, w, g, m, v):
    R, C = w.shape
    tr = 256 if R % 256 == 0 else R
    c1 = 1.0 - ADAM_B1 ** ADAM_STEP
    c2 = 1.0 - ADAM_B2 ** ADAM_STEP

    def body(w_ref, g_ref, m_ref, v_ref, d_ref, nm_ref, nv_ref, g_out_ref):
        gv = g_ref[...]
        nm = ADAM_B1 * m_ref[...] + (1.0 - ADAM_B1) * gv
        nv = ADAM_B2 * v_ref[...] + (1.0 - ADAM_B2) * (gv * gv)
        m_hat = nm / c1
        v_hat = nv / c2
        d_ref[...] = -ADAM_LR * (m_hat / (jnp.sqrt(v_hat) + ADAM_EPS) + ADAM_WD * w_ref[...])
        nm_ref[...] = nm
        nv_ref[...] = nv
        g_out_ref[...] = gv

    blk = pl.BlockSpec((tr, C), lambda i: (i, 0))
    return pl.pallas_call(
        body, name=name, grid=(R // tr,),
        in_specs=[blk] * 4, out_specs=[blk] * 4,
        out_shape=[SDS((R, C), F32)] * 4,
        compiler_params=_cp(1),
    )(w, g, m, v)


SGU_STEP_ROWS = 1024


def _pair_select(lane, lo, hi):
    return jnp.where(lane < HEAD_DIM, lo, hi)


def _sgu_fwd(name, p, wt, bb):
    S = p.shape[0]

    rows = min(SGU_STEP_ROWS, S)

    def body(u_ref, v_ref, wt_ref, bb_ref, o_ref):
        lane = lax.broadcasted_iota(jnp.int32, (CHUNK, 128), 1)
        for ci in range(rows // CHUNK):
            rs = slice(CHUNK * ci, CHUNK * (ci + 1))
            for pp in range(A_HEADS // 2):
                cs = slice(128 * pp, 128 * (pp + 1))
                vb = v_ref[rs, cs].astype(BF16)
                mixed = _pair_select(lane,
                                     jnp.dot(wt_ref[2 * pp], vb, preferred_element_type=F32),
                                     jnp.dot(wt_ref[2 * pp + 1], vb, preferred_element_type=F32)) + bb_ref[:, cs]
                o_ref[rs, cs] = (u_ref[rs, cs] * mixed).astype(o_ref.dtype)

    return pl.pallas_call(
        body, name=name, grid=(S // rows,),
        in_specs=[pl.BlockSpec((rows, A_WIDTH), lambda c: (c, OFF_AU // A_WIDTH)),
                  pl.BlockSpec((rows, A_WIDTH), lambda c: (c, OFF_AV // A_WIDTH)),
                  pl.BlockSpec((A_HEADS, CHUNK, CHUNK), lambda c: (0, 0, 0)),
                  pl.BlockSpec((CHUNK, A_WIDTH), lambda c: (0, 0))],
        out_specs=pl.BlockSpec((rows, A_WIDTH), lambda c: (c, 0)),
        out_shape=SDS((S, A_WIDTH), BF16),
        compiler_params=_cp(1),
    )(p, p, wt, bb)


def _sgu_bwd(name, p, dycat, wt, wtt, bb):
    S = p.shape[0]
    rows = min(SGU_STEP_ROWS, S)

    def body(u_ref, v_ref, dy_ref, wt_ref, wtt_ref, bb_ref, du_ref, dv_ref, dw_ref, db_ref, dbacc_ref):
        c = pl.program_id(0)

        @pl.when(c == 0)
        def _():
            dw_ref[...] = jnp.zeros_like(dw_ref)
            dbacc_ref[...] = jnp.zeros_like(dbacc_ref)

        lane = lax.broadcasted_iota(jnp.int32, (CHUNK, 128), 1)
        row = lax.broadcasted_iota(jnp.int32, (CHUNK, 128), 0)
        causal = row >= lane
        nt = (((1,), (1,)), ((), ()))
        for pp in range(A_HEADS // 2):
            cs = slice(128 * pp, 128 * (pp + 1))
            dw_lo = jnp.zeros((CHUNK, CHUNK), F32)
            dw_hi = jnp.zeros((CHUNK, CHUNK), F32)
            dm_sum = jnp.zeros((CHUNK, 128), F32)
            for ci in range(rows // CHUNK):
                rs = slice(CHUNK * ci, CHUNK * (ci + 1))
                vb = v_ref[rs, cs].astype(BF16)
                dy = dy_ref[rs, cs]
                mixed = _pair_select(lane,
                                     jnp.dot(wt_ref[2 * pp], vb, preferred_element_type=F32),
                                     jnp.dot(wt_ref[2 * pp + 1], vb, preferred_element_type=F32)) + bb_ref[:, cs]
                du_ref[rs, cs] = (dy * mixed).astype(du_ref.dtype)
                dm = dy * u_ref[rs, cs]
                dmb = dm.astype(BF16)
                dv = _pair_select(lane,
                                  jnp.dot(wtt_ref[2 * pp], dmb, preferred_element_type=F32),
                                  jnp.dot(wtt_ref[2 * pp + 1], dmb, preferred_element_type=F32))
                dv_ref[rs, cs] = dv.astype(dv_ref.dtype)
                dm_sum += dm
                dm_lo = jnp.where(lane < HEAD_DIM, dm, 0.0).astype(BF16)
                dm_hi = jnp.where(lane >= HEAD_DIM, dm, 0.0).astype(BF16)
                dw_lo += lax.dot_general(dm_lo, vb, nt, preferred_element_type=F32)
                dw_hi += lax.dot_general(dm_hi, vb, nt, preferred_element_type=F32)
            dbacc_ref[:, cs] += dm_sum
            dw_ref[2 * pp] += jnp.where(causal, dw_lo, 0.0)
            dw_ref[2 * pp + 1] += jnp.where(causal, dw_hi, 0.0)

        @pl.when(c == S // rows - 1)
        def _():
            out = jnp.zeros((CHUNK, 128), F32)
            for pp in range(A_HEADS // 2):
                acc = dbacc_ref[:, 128 * pp:128 * (pp + 1)]
                s_lo = jnp.sum(jnp.where(lane < HEAD_DIM, acc, 0.0), axis=1, keepdims=True)
                s_hi = jnp.sum(jnp.where(lane >= HEAD_DIM, acc, 0.0), axis=1, keepdims=True)
                out = jnp.where(lane == 2 * pp, s_lo, out)
                out = jnp.where(lane == 2 * pp + 1, s_hi, out)
            db_ref[...] = out

    chunk = lambda col: pl.BlockSpec((rows, A_WIDTH), lambda c: (c, col))
    wspec = pl.BlockSpec((A_HEADS, CHUNK, CHUNK), lambda c: (0, 0, 0))
    return pl.pallas_call(
        body, name=name, grid=(S // rows,),
        in_specs=[chunk(OFF_AU // A_WIDTH), chunk(OFF_AV // A_WIDTH), chunk(0), wspec, wspec,
                  pl.BlockSpec((CHUNK, A_WIDTH), lambda c: (0, 0))],
        out_specs=[chunk(0), chunk(0), wspec, pl.BlockSpec((CHUNK, 128), lambda c: (0, 0))],
        out_shape=[SDS((S, A_WIDTH), BF16), SDS((S, A_WIDTH), BF16),
                   SDS((A_HEADS, CHUNK, CHUNK), F32), SDS((CHUNK, 128), F32)],
        scratch_shapes=[pltpu.VMEM((CHUNK, A_WIDTH), F32)],
        compiler_params=_cp(1),
    )(p, p, dycat, wt, wtt, bb)


CONV_HALO = 8
CONV_COLS = 256
CONV_ROWS = 2048


def _shift_down(a, halo, k):
    T = a.shape[0]
    row = lax.broadcasted_iota(jnp.int32, a.shape, 0)
    out = pltpu.roll(a, k, 0)
    for r in range(k):
        out = jnp.where(row == r, halo[CONV_HALO - k + r:CONV_HALO - k + r + 1, :], out)
    return out


def _shift_up(a, halo, k):
    T = a.shape[0]
    row = lax.broadcasted_iota(jnp.int32, a.shape, 0)
    out = pltpu.roll(a, T - k, 0)
    for r in range(k):
        out = jnp.where(row == T - k + r, halo[r:r + 1, :], out)
    return out


def _conv_specs(S, T):
    hb = T // CONV_HALO
    last = S // CONV_HALO - 1
    tile = lambda col0: pl.BlockSpec((T, CONV_COLS), lambda j, i: (i, col0 + j))
    prev = lambda col0: pl.BlockSpec((CONV_HALO, CONV_COLS), lambda j, i: (jnp.maximum(i * hb - 1, 0), col0 + j))
    nxt = lambda col0: pl.BlockSpec((CONV_HALO, CONV_COLS), lambda j, i: (jnp.minimum((i + 1) * hb, last), col0 + j))
    return tile, prev, nxt


def _conv_fwd(name, p, w):
    S = p.shape[0]
    T = min(CONV_ROWS, S)
    tile, prev, _ = _conv_specs(S, T)
    cb, cc, cx = OFF_BB // CONV_COLS, OFF_BC // CONV_COLS, OFF_BX // CONV_COLS

    def body(b_ref, c_ref, x_ref, ch_ref, xh_ref, w_ref, o_ref):
        i = pl.program_id(1)
        z = c_ref[...] * x_ref[...]
        zh = jnp.where(i > 0, ch_ref[...] * xh_ref[...], 0.0)
        z1 = _shift_down(z, zh, 1)
        z2 = _shift_down(z, zh, 2)
        conv = w_ref[0:1, :] * z2 + w_ref[1:2, :] * z1 + w_ref[2:3, :] * z
        o_ref[...] = (b_ref[...] * conv).astype(o_ref.dtype)

    return pl.pallas_call(
        body, name=name, grid=(B_WIDTH // CONV_COLS, S // T),
        in_specs=[tile(cb), tile(cc), tile(cx), prev(cc), prev(cx),
                  pl.BlockSpec((3, CONV_COLS), lambda j, i: (0, j))],
        out_specs=tile(0),
        out_shape=SDS((S, B_WIDTH), BF16),
        compiler_params=_cp(2),
    )(p, p, p, p, p, w)


def _conv_bwd(name, p, dycat, w):
    S = p.shape[0]
    T = min(CONV_ROWS, S)
    tile, prev, nxt = _conv_specs(S, T)
    cb, cc, cx = OFF_BB // CONV_COLS, OFF_BC // CONV_COLS, OFF_BX // CONV_COLS
    cdy = A_WIDTH // CONV_COLS
    n_i = S // T

    def body(b_ref, c_ref, x_ref, dy_ref, ch_ref, xh_ref, bn_ref, dyn_ref, w_ref,
             db_ref, dc_ref, dx_ref, dw_ref):
        i = pl.program_id(1)

        @pl.when(i == 0)
        def _():
            dw_ref[...] = jnp.zeros_like(dw_ref)

        cv = c_ref[...]
        xv = x_ref[...]
        z = cv * xv
        zh = jnp.where(i > 0, ch_ref[...] * xh_ref[...], 0.0)
        z1 = _shift_down(z, zh, 1)
        z2 = _shift_down(z, zh, 2)
        w0, w1, w2 = w_ref[0:1, :], w_ref[1:2, :], w_ref[2:3, :]
        conv = w0 * z2 + w1 * z1 + w2 * z
        dy = dy_ref[...]
        db_ref[...] = (dy * conv).astype(db_ref.dtype)
        dconv = dy * b_ref[...]
        dconv_n = jnp.where(i < n_i - 1, dyn_ref[...] * bn_ref[...], 0.0)
        dz = w2 * dconv + w1 * _shift_up(dconv, dconv_n, 1) + w0 * _shift_up(dconv, dconv_n, 2)
        dc_ref[...] = (dz * xv).astype(dc_ref.dtype)
        dx_ref[...] = (dz * cv).astype(dx_ref.dtype)
        dw_ref[0:1, :] += jnp.sum(dconv * z2, axis=0, keepdims=True)
        dw_ref[1:2, :] += jnp.sum(dconv * z1, axis=0, keepdims=True)
        dw_ref[2:3, :] += jnp.sum(dconv * z, axis=0, keepdims=True)

    wspec = pl.BlockSpec((3, CONV_COLS), lambda j, i: (0, j))
    return pl.pallas_call(
        body, name=name, grid=(B_WIDTH // CONV_COLS, n_i),
        in_specs=[tile(cb), tile(cc), tile(cx), tile(cdy), prev(cc), prev(cx), nxt(cb), nxt(cdy), wspec],
        out_specs=[tile(0), tile(0), tile(0), wspec],
        out_shape=[SDS((S, B_WIDTH), BF16)] * 3 + [SDS((3, B_WIDTH), F32)],
        compiler_params=_cp(2),
    )(p, p, p, dycat, p, p, p, dycat, w)


def _seg_sum(t, bd):
    hi = t.astype(BF16)
    lo = (t - hi.astype(F32)).astype(BF16)
    return jnp.dot(hi, bd, preferred_element_type=F32) + jnp.dot(lo, bd, preferred_element_type=F32)


def _head_norm(x, g, bd):
    rstd = lax.rsqrt(_seg_sum(x * x, bd) * (1.0 / HEAD_DIM) + EPS)
    xhat = x * rstd
    return xhat * g, xhat, rstd


def _head_norm_bwd(dy, g, xhat, rstd, bd):
    dxh = dy * g
    return rstd * (dxh - xhat * (_seg_sum(dxh * xhat, bd) * (1.0 / HEAD_DIM)))


def _band_mask(has_prev):
    row = lax.broadcasted_iota(jnp.int32, (BLK, 2 * BLK), 0)
    col = lax.broadcasted_iota(jnp.int32, (BLK, 2 * BLK), 1)
    first_key = jnp.where(has_prev, 0, BLK)
    return (col >= row) & (col <= row + BLK) & (col >= first_key)


def _residue_rows(r, d):
    return slice(None) if d == 1 else pl.ds(r, BLK, stride=d)


STRIDED_LANES = 128
RESIDUES_PER_PASS = 8


def _step_width(d):
    return PW if d == 1 else STRIDED_LANES


def _n_stack(lane):
    return lane.shape[1] // HEAD_DIM


def _for_residues(d, fn):
    if d == 1:
        fn(0)
    else:
        per_pass = min(d, RESIDUES_PER_PASS)

        def several(i, carry):
            for u in range(per_pass):
                fn(per_pass * i + u)
            return carry
        lax.fori_loop(0, d // per_pass, several, 0)


def _head_mask(lane, j):
    return (lane >= HEAD_DIM * j) & (lane < HEAD_DIM * (j + 1))


def _stack_heads(x, lane):
    return jnp.concatenate([jnp.where(_head_mask(lane, j), x, 0.0) for j in range(_n_stack(lane))], axis=0)


def _unstack_heads(y, lane):
    out = y[:BLK]
    for j in range(1, _n_stack(lane)):
        out = jnp.where(lane >= HEAD_DIM * j, y[BLK * j:BLK * (j + 1)], out)
    return out


def _head_columns(v, lane):
    return jnp.concatenate([jnp.max(jnp.where(_head_mask(lane, j), v, NEG), axis=1, keepdims=True)
                            for j in range(_n_stack(lane))], axis=0)


def _attn_fwd(name, p, g, gq, gk, bd):
    S = p.shape[0]
    d = PATTERN_DILATION[g]
    rows = BLK * d
    hw = _step_width(d)
    nt = (((1,), (1,)), ((), ()))

    def body(q_ref, kc_ref, kp_ref, vc_ref, vp_ref, gq_ref, gk_ref, bd_ref, o_ref, lse_ref):
        has_prev = pl.program_id(1) > 0
        bdv = bd_ref[...]
        band = jnp.concatenate([_band_mask(has_prev)] * (hw // HEAD_DIM), axis=0)
        lane = lax.broadcasted_iota(jnp.int32, (1, hw), 1)

        def residue(r):
            rr = _residue_rows(r, d)
            qn, _, _ = _head_norm(q_ref[rr, :], gq_ref[...], bdv)
            kn, _, _ = _head_norm(jnp.concatenate([kp_ref[rr, :], kc_ref[rr, :]], axis=0), gk_ref[...], bdv)
            knb = kn.astype(BF16)
            vb = jnp.concatenate([vp_ref[rr, :], vc_ref[rr, :]], axis=0).astype(BF16)
            qs = _stack_heads(qn, lane).astype(BF16)
            s = lax.dot_general(qs, knb, nt, preferred_element_type=F32) * (HEAD_DIM ** -0.5)
            s = jnp.where(band, s, NEG)
            m = jnp.max(s, axis=1, keepdims=True)
            e = jnp.exp(s - m)
            den = jnp.sum(e, axis=1, keepdims=True)
            pv = jnp.dot(e.astype(BF16), vb, preferred_element_type=F32)
            o_ref[rr, :] = _unstack_heads(pv / den, lane)
            lse_ref[rr, :] = _unstack_heads(jnp.broadcast_to(m + jnp.log(den), pv.shape), lane)

        _for_residues(d, residue)

    per = PW // hw
    cq, ck, cv = (OFF_Q + PW * g) // hw, (OFF_K + PW * g) // hw, (OFF_V + PW * g) // hw
    cur = lambda col: pl.BlockSpec((rows, hw), lambda h, n: (n, col + h))
    prv = lambda col: pl.BlockSpec((rows, hw), lambda h, n: (jnp.maximum(n - 1, 0), col + h))
    vec = pl.BlockSpec((1, hw), lambda h, n: (0, h))
    return pl.pallas_call(
        body, name=name, grid=(per, S // rows),
        in_specs=[cur(cq), cur(ck), prv(ck), cur(cv), prv(cv), vec, vec, pl.BlockSpec((hw, hw), lambda h, n: (0, 0))],
        out_specs=[cur(0), cur(0)],
        out_shape=[SDS((S, PW), F32)] * 2,
        compiler_params=_cp(2),
    )(p, p, p, p, p, gq, gk, bd)


def _attn_bwd(name, p, g, lse, do3, c3, gq, gk, bd):
    S = p.shape[0]
    d = PATTERN_DILATION[g]
    rows = BLK * d
    nblk = S // rows
    hw = _step_width(d)
    nt = (((1,), (1,)), ((), ()))
    tn = (((0,), (0,)), ((), ()))

    def body(q_ref, kc_ref, kp_ref, vc_ref, vp_ref, lse_ref, do_ref, c_ref, gq_ref, gk_ref, bd_ref,
             dq_ref, dk_ref, dv_ref, dgq_ref, dgk_ref, ck_ref, cv_ref, dq_keep_ref):
        n = pl.program_id(1)

        @pl.when(n == 0)
        def _():
            ck_ref[...] = jnp.zeros_like(ck_ref)
            cv_ref[...] = jnp.zeros_like(cv_ref)
            dgq_ref[...] = jnp.zeros_like(dgq_ref)
            dgk_ref[...] = jnp.zeros_like(dgk_ref)

        @pl.when(n == nblk)
        def _():
            dq_ref[...] = dq_keep_ref[...]
            dk_ref[...] = ck_ref[...]
            dv_ref[...] = cv_ref[...]

        bdv = bd_ref[...]
        gqv = gq_ref[...]
        gkv = gk_ref[...]
        band = jnp.concatenate([_band_mask(n > 0)] * (hw // HEAD_DIM), axis=0)
        lane = lax.broadcasted_iota(jnp.int32, (1, hw), 1)

        def residue(r):
            rr = _residue_rows(r, d)
            qn, qhat, qrstd = _head_norm(q_ref[rr, :], gqv, bdv)
            kn, khat, krstd = _head_norm(jnp.concatenate([kp_ref[rr, :], kc_ref[rr, :]], axis=0), gkv, bdv)
            knb = kn.astype(BF16)
            vb = jnp.concatenate([vp_ref[rr, :], vc_ref[rr, :]], axis=0).astype(BF16)
            qs = _stack_heads(qn, lane).astype(BF16)
            dos = _stack_heads(do_ref[rr, :], lane).astype(BF16)
            s = lax.dot_general(qs, knb, nt, preferred_element_type=F32) * (HEAD_DIM ** -0.5)
            prob = jnp.where(band, jnp.exp(s - _head_columns(lse_ref[rr, :], lane)), 0.0)
            dp = lax.dot_general(dos, vb, nt, preferred_element_type=F32)
            ds = (prob * (dp + _head_columns(c_ref[rr, :], lane)) * (HEAD_DIM ** -0.5)).astype(BF16)
            dqn = _unstack_heads(jnp.dot(ds, knb, preferred_element_type=F32), lane)
            dkn = lax.dot_general(ds, qs, tn, preferred_element_type=F32)
            dvv = lax.dot_general(prob.astype(BF16), dos, tn, preferred_element_type=F32)

            dq = _head_norm_bwd(dqn, gqv, qhat, qrstd, bdv)
            dq_ref[rr, :] = dq
            dq_keep_ref[rr, :] = dq
            dk2 = _head_norm_bwd(dkn, gkv, khat, krstd, bdv)
            dgq_ref[...] += jnp.sum(dqn * qhat, axis=0, keepdims=True)
            dgk_ref[...] += jnp.sum(dkn * khat, axis=0, keepdims=True)
            dk_ref[rr, :] = ck_ref[rr, :] + dk2[:BLK]
            dv_ref[rr, :] = cv_ref[rr, :] + dvv[:BLK]
            ck_ref[rr, :] = dk2[BLK:]
            cv_ref[rr, :] = dvv[BLK:]

        @pl.when(n < nblk)
        def _():
            _for_residues(d, residue)

    last = nblk - 1
    per = PW // hw
    cq, ck, cv = (OFF_Q + PW * g) // hw, (OFF_K + PW * g) // hw, (OFF_V + PW * g) // hw
    cur = lambda col: pl.BlockSpec((rows, hw), lambda h, n: (jnp.minimum(n, last), col + h))
    prv = lambda col: pl.BlockSpec((rows, hw), lambda h, n: (jnp.maximum(jnp.minimum(n, last) - 1, 0), col + h))
    cur3 = pl.BlockSpec((None, rows, hw), lambda h, n: (g, jnp.minimum(n, last), h))
    done = pl.BlockSpec((rows, hw), lambda h, n: (jnp.maximum(n - 1, 0), h))
    vec = pl.BlockSpec((1, hw), lambda h, n: (0, h))
    return pl.pallas_call(
        body, name=name, grid=(per, nblk + 1),
        in_specs=[cur(cq), cur(ck), prv(ck), cur(cv), prv(cv), cur(0), cur3, cur3, vec, vec,
                  pl.BlockSpec((hw, hw), lambda h, n: (0, 0))],
        out_specs=[cur(0), done, done, vec, vec],
        out_shape=[SDS((S, PW), F32)] * 3 + [SDS((1, PW), F32)] * 2,
        scratch_shapes=[pltpu.VMEM((rows, hw), F32)] * 3,
        compiler_params=_cp(2),
    )(p, p, p, p, p, lse, do3, c3, gq, gk, bd)


def _mix_fwd(name, os, lses):
    S = os[0].shape[0]
    tm = min(1024, S)

    def body(o0, o1, o2, l0, l1, l2, y_ref):
        o = [o0[...], o1[...], o2[...]]
        l = [l0[...], l1[...], l2[...]]
        m = jnp.maximum(jnp.maximum(l[0], l[1]), l[2])
        e = [jnp.exp(t - m) for t in l]
        inv = 1.0 / (e[0] + e[1] + e[2])
        for g in range(N_PATTERNS):
            y_ref[:, PW * g:PW * (g + 1)] = (o[g] * (e[g] * inv)).astype(y_ref.dtype)

    blk = pl.BlockSpec((tm, PW), lambda i: (i, 0))
    return pl.pallas_call(
        body, name=name, grid=(S // tm,),
        in_specs=[blk] * 6,
        out_specs=pl.BlockSpec((tm, C_WIDTH), lambda i: (i, 0)),
        out_shape=SDS((S, C_WIDTH), BF16),
        compiler_params=_cp(1),
    )(*os, *lses)


def _mix_bwd(name, os, lses, dycat, bd):
    S = os[0].shape[0]
    tm = min(1024, S)
    c0 = (A_WIDTH + B_WIDTH) // PW

    def body(o0, o1, o2, l0, l1, l2, dy0_ref, dy1_ref, dy2_ref, bd_ref, do_ref, c_ref):
        bdv = bd_ref[...]
        o = [o0[...], o1[...], o2[...]]
        l = [l0[...], l1[...], l2[...]]
        dys = [dy0_ref[...], dy1_ref[...], dy2_ref[...]]
        m = jnp.maximum(jnp.maximum(l[0], l[1]), l[2])
        e = [jnp.exp(t - m) for t in l]
        inv = 1.0 / (e[0] + e[1] + e[2])
        alpha = [t * inv for t in e]
        da = [_seg_sum(dys[g] * o[g], bdv) for g in range(N_PATTERNS)]
        mean_da = alpha[0] * da[0] + alpha[1] * da[1] + alpha[2] * da[2]
        for g in range(N_PATTERNS):
            do_ref[g] = dys[g] * alpha[g]
            c_ref[g] = -alpha[g] * mean_da

    blk = pl.BlockSpec((tm, PW), lambda i: (i, 0))
    blk3 = pl.BlockSpec((N_PATTERNS, tm, PW), lambda i: (0, i, 0))
    dyspec = lambda g: pl.BlockSpec((tm, PW), lambda i: (i, c0 + g))
    return pl.pallas_call(
        body, name=name, grid=(S // tm,),
        in_specs=[blk] * 6 + [dyspec(0), dyspec(1), dyspec(2), pl.BlockSpec((PW, PW), lambda i: (0, 0))],
        out_specs=[blk3, blk3],
        out_shape=[SDS((N_PATTERNS, S, PW), F32)] * 2,
        compiler_params=_cp(1),
    )(*os, *lses, dycat, dycat, dycat, bd)


def _mesh_pos():
    x, y, c = lax.axis_index("x"), lax.axis_index("y"), lax.axis_index("c")
    chips = [(1 - x, y), (x, 1 - y), (1 - x, 1 - y)]
    chip_idx = [2 * cx + cy for cx, cy in chips]
    return x, y, c, 2 * x + y, chips, chip_idx


def _place_shard(name, w, layer, chip_arr, out_dtype, deps=()):
    _, R, C = w.shape
    tr = min(256, R)

    def body(chip_ref, w_ref, *rest):
        o_ref = rest[-1]
        o_ref[...] = w_ref[...].astype(o_ref.dtype)

    return pl.pallas_call(
        body, name=name,
        grid_spec=pltpu.PrefetchScalarGridSpec(
            num_scalar_prefetch=1, grid=(R // tr,),
            in_specs=[pl.BlockSpec((None, tr, C), lambda i, chip_ref: (layer, i, 0))] + [_hbm_spec()] * len(deps),
            out_specs=pl.BlockSpec((None, tr, C), lambda i, chip_ref: (chip_ref[0], i, 0))),
        out_shape=SDS((N_CHIPS, R, C), out_dtype),
        compiler_params=_cp(1),
    )(chip_arr, w, *deps)


HBM_SPEC = pl.BlockSpec(memory_space=pltpu.HBM)
SEM_SPEC = pl.BlockSpec(memory_space=pltpu.SEMAPHORE)
SPLIT_COPY = pltpu.SideEffectType.DATAFLOW_SIDE_EFFECTING
N_PEER_CHIPS = N_CHIPS - 1
TOKEN_SHAPE = SDS((8, 128), F32)
TOKEN_SPEC = pl.BlockSpec(memory_space=pltpu.VMEM)


def _in_hbm(a):
    return pltpu.with_memory_space_constraint(a, pltpu.HBM)


def _gather_start(name, bufs):
    T = len(bufs)

    def body(*refs):
        ins = refs[:T]
        send_sems, recv_sems = refs[T:2 * T], refs[2 * T:3 * T]
        token = refs[4 * T]
        x, y, c, me, chips, chip_idx = _mesh_pos()
        for t in range(T):
            hr = ins[t].shape[1] // 2
            mine = ins[t].at[me, pl.ds(c * hr, hr), :]
            for j in range(N_PEER_CHIPS):
                pltpu.make_async_remote_copy(src_ref=mine, dst_ref=mine, send_sem=send_sems[t].at[j],
                                             recv_sem=recv_sems[t].at[j], device_id=(*chips[j], c),
                                             device_id_type=MESH).start()
        token[...] = jnp.zeros_like(token)

    sems = [pltpu.SemaphoreType.DMA((N_PEER_CHIPS,))] * T
    out = pl.pallas_call(
        body, name=name,
        in_specs=[HBM_SPEC] * T,
        out_specs=[SEM_SPEC] * (2 * T) + [HBM_SPEC] * T + [TOKEN_SPEC],
        out_shape=sems + sems + [pltpu.HBM(b.shape, b.dtype) for b in bufs] + [TOKEN_SHAPE],
        input_output_aliases={t: 2 * T + t for t in range(T)},
        compiler_params=pltpu.CompilerParams(has_side_effects=SPLIT_COPY),
    )(*[_in_hbm(b) for b in bufs])
    return out[:T], out[T:2 * T], out[2 * T:3 * T], out[3 * T]


def _gather_wait(name, buf, send_sem, recv_sem, after):
    n_in = 3 if after is None else 4

    def body(*refs):
        buf_ref, ssem, rsem = refs[:3]
        x, y, c, me, chips, chip_idx = _mesh_pos()
        hr = buf_ref.shape[1] // 2
        mine = buf_ref.at[me, pl.ds(c * hr, hr), :]
        for j in range(N_PEER_CHIPS):
            got = buf_ref.at[chip_idx[j], pl.ds(c * hr, hr), :]
            cp = pltpu.make_async_remote_copy(src_ref=mine, dst_ref=got, send_sem=ssem.at[j], recv_sem=rsem.at[j],
                                              device_id=(*chips[j], c), device_id_type=MESH)
            cp.wait_send()
            cp.wait_recv()

    args = [buf, send_sem, recv_sem] + ([] if after is None else [after])
    return pl.pallas_call(
        body, name=name,
        in_specs=[HBM_SPEC, SEM_SPEC, SEM_SPEC] + [_hbm_spec()] * (n_in - 3),
        out_specs=HBM_SPEC,
        out_shape=pltpu.HBM(buf.shape, buf.dtype),
        input_output_aliases={0: 0},
        compiler_params=pltpu.CompilerParams(has_side_effects=SPLIT_COPY),
    )(*args)


def _forward_start(name, buf):
    def body(buf_ref, send_sems, recv_sems, buf_thru, token):
        x, y, c, me, chips, chip_idx = _mesh_pos()
        hr = buf_ref.shape[1] // 2
        for j in range(N_PEER_CHIPS):
            got = buf_ref.at[chip_idx[j], pl.ds(c * hr, hr), :]
            pltpu.make_async_remote_copy(src_ref=got, dst_ref=got, send_sem=send_sems.at[j], recv_sem=recv_sems.at[j],
                                         device_id=(x, y, 1 - c), device_id_type=MESH).start()
        token[...] = jnp.zeros_like(token)

    sems = pltpu.SemaphoreType.DMA((N_PEER_CHIPS,))
    return pl.pallas_call(
        body, name=name,
        in_specs=[HBM_SPEC],
        out_specs=[SEM_SPEC, SEM_SPEC, HBM_SPEC, TOKEN_SPEC],
        out_shape=[sems, sems, pltpu.HBM(buf.shape, buf.dtype), TOKEN_SHAPE],
        input_output_aliases={0: 2},
        compiler_params=pltpu.CompilerParams(has_side_effects=SPLIT_COPY),
    )(_in_hbm(buf))


def _forward_wait(name, buf, send_sems, recv_sems, after):
    n_in = 3 if after is None else 4

    def body(*refs):
        buf_ref, ssems, rsems = refs[:3]
        x, y, c, me, chips, chip_idx = _mesh_pos()
        hr = buf_ref.shape[1] // 2
        for j in range(N_PEER_CHIPS):
            sent = buf_ref.at[chip_idx[j], pl.ds(c * hr, hr), :]
            theirs = buf_ref.at[chip_idx[j], pl.ds((1 - c) * hr, hr), :]
            cp = pltpu.make_async_remote_copy(src_ref=sent, dst_ref=theirs, send_sem=ssems.at[j],
                                              recv_sem=rsems.at[j], device_id=(x, y, 1 - c), device_id_type=MESH)
            cp.wait_send()
            cp.wait_recv()

    args = [buf, send_sems, recv_sems] + ([] if after is None else [after])
    return pl.pallas_call(
        body, name=name,
        in_specs=[HBM_SPEC, SEM_SPEC, SEM_SPEC] + [_hbm_spec()] * (n_in - 3),
        out_specs=HBM_SPEC,
        out_shape=pltpu.HBM(buf.shape, buf.dtype),
        input_output_aliases={0: 0},
        compiler_params=pltpu.CompilerParams(has_side_effects=SPLIT_COPY),
    )(*args)


class _GatheredWeights:
    def __init__(self):
        self._order = []
        self._pending = {}
        self._forwarding = {}
        self._ready = {}
        self._tokens = []

    def start(self, keys, bufs):
        send_sems, recv_sems, thru, token = _gather_start(f"gather_start_{len(self._order)}", bufs)
        self._tokens.append(token)
        self._order.extend(keys)
        self._pending.update({k: (b, s, r) for k, b, s, r in zip(keys, thru, send_sems, recv_sems)})

    def _prefetch(self, key, after):
        if key in self._pending:
            buf, ssem, rsem = self._pending.pop(key)
            tag = f"{key[0]}_{key[1]}"
            buf = _gather_wait(f"gather_wait_{tag}", buf, ssem, rsem, after)
            ssems, rsems, buf, token = _forward_start(f"gather_fwd_start_{tag}", buf)
            self._forwarding[key] = (buf, ssems, rsems)
            self._tokens.append(token)

    def get(self, name, layer, after=None, prefetch_next=True):
        key = (name, layer)
        if key not in self._ready:
            self._prefetch(key, after)
            buf, ssems, rsems = self._forwarding.pop(key)
            self._ready[key] = _forward_wait(f"gather_fwd_wait_{name}_{layer}", buf, ssems, rsems, after)
            if prefetch_next:
                self.prefetch_after(name, layer, after)
        return self._ready[key]

    def prefetch_after(self, name, layer, after):
        nxt = self._order.index((name, layer)) + 1
        if nxt < len(self._order):
            self._prefetch(self._order[nxt], after)

    def deps(self):
        tokens, self._tokens = self._tokens, []
        return tokens


def _swap_copy(g_ref, land_ref, send_sem, recv_sem):
    x, y, c, _, _, _ = _mesh_pos()
    hr = g_ref.shape[1] // 2
    return pltpu.make_async_remote_copy(src_ref=g_ref.at[:, pl.ds((1 - c) * hr, hr), :], dst_ref=land_ref,
                                        send_sem=send_sem, recv_sem=recv_sem, device_id=(x, y, 1 - c),
                                        device_id_type=MESH)


def _swap_start(name, g):
    land_shape = (g.shape[0], g.shape[1] // 2, g.shape[2])

    def body(g_ref, land_ref, send_sem, recv_sem, land_thru, token):
        _swap_copy(g_ref, land_ref, send_sem, recv_sem).start()
        token[...] = jnp.zeros_like(token)

    return pl.pallas_call(
        body, name=name,
        in_specs=[HBM_SPEC, HBM_SPEC],
        out_specs=[SEM_SPEC, SEM_SPEC, HBM_SPEC, TOKEN_SPEC],
        out_shape=[pltpu.SemaphoreType.DMA(()), pltpu.SemaphoreType.DMA(()), pltpu.HBM(land_shape, g.dtype),
                   TOKEN_SHAPE],
        input_output_aliases={1: 2},
        compiler_params=pltpu.CompilerParams(has_side_effects=SPLIT_COPY),
    )(_in_hbm(g), _in_hbm(lax.empty(land_shape, g.dtype)))


def _swap_wait(name, g, land, send_sem, recv_sem, after):
    def body(g_ref, land_ref, send_sem, recv_sem, after_ref, land_out):
        cp = _swap_copy(g_ref, land_ref, send_sem, recv_sem)
        cp.wait_send()
        cp.wait_recv()

    return pl.pallas_call(
        body, name=name,
        in_specs=[HBM_SPEC, HBM_SPEC, SEM_SPEC, SEM_SPEC, _hbm_spec()],
        out_specs=HBM_SPEC,
        out_shape=pltpu.HBM(land.shape, land.dtype),
        input_output_aliases={1: 0},
        compiler_params=pltpu.CompilerParams(has_side_effects=SPLIT_COPY),
    )(_in_hbm(g), land, send_sem, recv_sem, after)


def _add_my_half(name, g, r, pos_arr):
    ns, R, C = g.shape
    hr = R // 2
    tr = min(256, hr)
    nt = hr // tr

    def body(pos_ref, g_ref, r_ref, o_ref, land_ref):
        t = (g_ref[...] + r_ref[...]).astype(o_ref.dtype)
        o_ref[...] = t

        @pl.when(pl.program_id(1) == pos_ref[1])
        def _():
            land_ref[...] = t

    blk = pl.BlockSpec((None, tr, C), lambda i, s, pos_ref: (s, i, 0))
    return pl.pallas_call(
        body, name=name,
        grid_spec=pltpu.PrefetchScalarGridSpec(
            num_scalar_prefetch=1, grid=(nt, ns),
            in_specs=[pl.BlockSpec((None, tr, C), lambda i, s, pos_ref: (s, pos_ref[0] * nt + i, 0)), blk],
            out_specs=[blk, pl.BlockSpec((None, tr, C), lambda i, s, pos_ref: (pos_ref[1], i, 0))]),
        out_shape=[SDS((ns, hr, C), BF16)] * 2,
        compiler_params=_cp(2),
    )(pos_arr, g, r)


def _exchange_start(name, part, land):
    def body(part_ref, land_ref, send_sems, recv_sems, land_thru, token):
        x, y, c, me, chips, chip_idx = _mesh_pos()
        for j in range(N_PEER_CHIPS):
            pltpu.make_async_remote_copy(src_ref=part_ref.at[chip_idx[j]], dst_ref=land_ref.at[me],
                                         send_sem=send_sems.at[j], recv_sem=recv_sems.at[j],
                                         device_id=(*chips[j], c), device_id_type=MESH).start()
        token[...] = jnp.zeros_like(token)

    sems = pltpu.SemaphoreType.DMA((N_PEER_CHIPS,))
    return pl.pallas_call(
        body, name=name,
        in_specs=[HBM_SPEC, HBM_SPEC],
        out_specs=[SEM_SPEC, SEM_SPEC, HBM_SPEC, TOKEN_SPEC],
        out_shape=[sems, sems, pltpu.HBM(land.shape, land.dtype), TOKEN_SHAPE],
        input_output_aliases={1: 2},
        compiler_params=pltpu.CompilerParams(has_side_effects=SPLIT_COPY),
    )(_in_hbm(part), _in_hbm(land))


def _exchange_wait(name, part, land, send_sems, recv_sems, after):
    def body(part_ref, land_ref, send_sems, recv_sems, after_ref, land_out):
        x, y, c, me, chips, chip_idx = _mesh_pos()
        for j in range(N_PEER_CHIPS):
            cp = pltpu.make_async_remote_copy(src_ref=part_ref.at[chip_idx[j]], dst_ref=land_ref.at[chip_idx[j]],
                                              send_sem=send_sems.at[j], recv_sem=recv_sems.at[j],
                                              device_id=(*chips[j], c), device_id_type=MESH)
            cp.wait_send()
            cp.wait_recv()

    return pl.pallas_call(
        body, name=name,
        in_specs=[HBM_SPEC, HBM_SPEC, SEM_SPEC, SEM_SPEC, _hbm_spec()],
        out_specs=HBM_SPEC,
        out_shape=pltpu.HBM(land.shape, land.dtype),
        input_output_aliases={1: 0},
        compiler_params=pltpu.CompilerParams(has_side_effects=SPLIT_COPY),
    )(_in_hbm(part), land, send_sems, recv_sems, after)


class _GradReducer:
    def __init__(self, c_arr):
        self._c_arr = c_arr
        self._swapping = []
        self._exchanging = {}
        self._joining = {}
        self._tokens = []

    def begin(self, name, layer, g):
        tag = f"{name}_{layer}"
        ssem, rsem, land, token = _swap_start(f"rs_swap_start_{tag}", g)
        self._swapping.append((name, layer, g, ssem, rsem, land))
        self._tokens.append(token)

    def advance(self, after):
        for name, layer, g, ssem, rsem, land in self._swapping:
            tag = f"{name}_{layer}"
            theirs = _swap_wait(f"rs_swap_wait_{tag}", g, land, ssem, rsem, after)
            part, own = _add_my_half(f"rs_add_{tag}", g, theirs, self._c_arr)
            ssems, rsems, land2, token = _exchange_start(f"rs_xchg_start_{tag}", part, own)
            self._exchanging[(name, layer)] = (part, ssems, rsems, land2)
            self._tokens.append(token)
        self._swapping = []

    def deps(self):
        tokens, self._tokens = self._tokens, []
        return tokens

    def reduce(self, name, n_layers, after):
        buf = None
        for layer in range(n_layers):
            part, ssems, rsems, land = self._exchanging.pop((name, layer))
            tag = f"{name}_{layer}"
            landed = _exchange_wait(f"rs_xchg_wait_{tag}", part, land, ssems, rsems, after)
            buf = _sum_chips(f"rs_sum_{tag}", landed, self._c_arr, layer, n_layers, buf)
        ssem, rsem, buf, token = _join_start(f"rs_join_start_{name}", buf)
        self._joining[name] = (buf, ssem, rsem)
        return token

    def reduced(self, name, after):
        buf, ssem, rsem = self._joining.pop(name)
        return _join_wait(f"rs_join_wait_{name}", buf, ssem, rsem, after)


def _sum_chips(name, r, c_arr, layer, n_layers, prev):
    ns, H, C = r.shape
    tr = min(256, H)
    nt = H // tr

    def body(c_ref, r_ref, *rest):
        o_ref = rest[-1]
        o_ref[...] = ((r_ref[0].astype(F32) + r_ref[1].astype(F32)) + r_ref[2].astype(F32)) + r_ref[3].astype(F32)

    in_specs = [pl.BlockSpec((ns, tr, C), lambda i, c_ref: (0, i, 0))]
    args = [c_arr, r]
    aliases = {}
    if prev is not None:
        in_specs.append(_hbm_spec())
        args.append(prev)
        aliases = {2: 0}
    return pl.pallas_call(
        body, name=name,
        grid_spec=pltpu.PrefetchScalarGridSpec(
            num_scalar_prefetch=1, grid=(nt,), in_specs=in_specs,
            out_specs=pl.BlockSpec((None, tr, C), lambda i, c_ref: (layer, c_ref[0] * nt + i, 0))),
        out_shape=SDS((n_layers, 2 * H, C), F32),
        input_output_aliases=aliases,
        compiler_params=_cp(1),
    )(*args)


def _join_copy(buf_ref, send_sem, recv_sem):
    x, y, c, _, _, _ = _mesh_pos()
    hr = buf_ref.shape[1] // 2
    mine = buf_ref.at[:, pl.ds(c * hr, hr), :]
    theirs = buf_ref.at[:, pl.ds((1 - c) * hr, hr), :]
    send = pltpu.make_async_remote_copy(src_ref=mine, dst_ref=mine, send_sem=send_sem, recv_sem=recv_sem,
                                        device_id=(x, y, 1 - c), device_id_type=MESH)
    arrive = pltpu.make_async_remote_copy(src_ref=theirs, dst_ref=theirs, send_sem=send_sem, recv_sem=recv_sem,
                                          device_id=(x, y, 1 - c), device_id_type=MESH)
    return send, arrive


def _join_start(name, buf):
    def body(buf_ref, send_sem, recv_sem, buf_thru, token):
        _join_copy(buf_ref, send_sem, recv_sem)[0].start()
        token[...] = jnp.zeros_like(token)

    return pl.pallas_call(
        body, name=name,
        in_specs=[HBM_SPEC],
        out_specs=[SEM_SPEC, SEM_SPEC, HBM_SPEC, TOKEN_SPEC],
        out_shape=[pltpu.SemaphoreType.DMA(()), pltpu.SemaphoreType.DMA(()), pltpu.HBM(buf.shape, buf.dtype),
                   TOKEN_SHAPE],
        input_output_aliases={0: 2},
        compiler_params=pltpu.CompilerParams(has_side_effects=SPLIT_COPY),
    )(_in_hbm(buf))


def _join_wait(name, buf, send_sem, recv_sem, after):
    def body(buf_ref, send_sem, recv_sem, after_ref, buf_out):
        send, arrive = _join_copy(buf_ref, send_sem, recv_sem)
        send.wait_send()
        arrive.wait_recv()

    return pl.pallas_call(
        body, name=name,
        in_specs=[HBM_SPEC, SEM_SPEC, SEM_SPEC, _hbm_spec()],
        out_specs=HBM_SPEC,
        out_shape=pltpu.HBM(buf.shape, buf.dtype),
        input_output_aliases={0: 0},
        compiler_params=pltpu.CompilerParams(has_side_effects=SPLIT_COPY),
    )(buf, send_sem, recv_sem, after)


def _small_copy(k, buf_ref, land_ref, send_sems, recv_sems):
    x, y, c = lax.axis_index("x"), lax.axis_index("y"), lax.axis_index("c")
    me = 4 * x + 2 * y + c
    peer = (x ^ ((k >> 2) & 1), y ^ ((k >> 1) & 1), c ^ (k & 1))
    cp = pltpu.make_async_remote_copy(src_ref=buf_ref, dst_ref=land_ref.at[me], send_sem=send_sems.at[k - 1],
                                      recv_sem=recv_sems.at[k - 1], device_id=peer, device_id_type=MESH)
    return me, peer, cp


def _small_start(buf, deps):
    land = jnp.broadcast_to(buf[None], (N_DEV,) + buf.shape)
    n_dep = len(deps)

    def body(buf_ref, land_ref, *rest):
        send_sems, recv_sems, _, token = rest[n_dep:]
        for k in range(1, N_DEV):
            _small_copy(k, buf_ref, land_ref, send_sems, recv_sems)[2].start()
        token[...] = jnp.zeros_like(token)

    sems = pltpu.SemaphoreType.DMA((N_DEV - 1,))
    return pl.pallas_call(
        body, name="small_gather_start",
        in_specs=[HBM_SPEC, HBM_SPEC] + [_hbm_spec()] * n_dep,
        out_specs=[SEM_SPEC, SEM_SPEC, HBM_SPEC, TOKEN_SPEC],
        out_shape=[sems, sems, pltpu.HBM(land.shape, land.dtype), TOKEN_SHAPE],
        input_output_aliases={1: 2},
        compiler_params=pltpu.CompilerParams(has_side_effects=SPLIT_COPY),
    )(_in_hbm(buf), _in_hbm(land), *deps)


def _small_wait(buf, land, send_sems, recv_sems, after):
    def body(buf_ref, land_ref, send_sems, recv_sems, after_ref, land_out):
        for k in range(1, N_DEV):
            me, peer, cp = _small_copy(k, buf_ref, land_ref, send_sems, recv_sems)
            cp.wait_send()
            got = land_ref.at[me ^ k]
            pltpu.make_async_remote_copy(src_ref=got, dst_ref=got, send_sem=send_sems.at[k - 1],
                                         recv_sem=recv_sems.at[k - 1], device_id=peer,
                                         device_id_type=MESH).wait_recv()

    return pl.pallas_call(
        body, name="small_gather_wait",
        in_specs=[HBM_SPEC, HBM_SPEC, SEM_SPEC, SEM_SPEC, _hbm_spec()],
        out_specs=HBM_SPEC,
        out_shape=pltpu.HBM(land.shape, land.dtype),
        input_output_aliases={1: 0},
        compiler_params=pltpu.CompilerParams(has_side_effects=SPLIT_COPY),
    )(_in_hbm(buf), land, send_sems, recv_sems, after)


def _sum_devices(land):
    n, R, C = land.shape

    def body(land_ref, out_ref):
        acc = land_ref[0]
        for d in range(1, n):
            acc = acc + land_ref[d]
        out_ref[...] = acc

    return pl.pallas_call(
        body, name="small_sum",
        in_specs=[pl.BlockSpec(memory_space=pltpu.VMEM)],
        out_specs=pl.BlockSpec(memory_space=pltpu.VMEM),
        out_shape=SDS((R, C), land.dtype),
        compiler_params=pltpu.CompilerParams(vmem_limit_bytes=V7X_VMEM_LIMIT),
    )(land)


def _pack_rows(vectors):
    flat = jnp.concatenate([v.reshape(-1) for v in vectors])
    n = flat.shape[0]
    padded = -(-n // 1024) * 1024
    return jnp.pad(flat, (0, padded - n)).reshape(padded // 128, 128)


def _unpack_rows(buf, shapes):
    flat = buf.reshape(-1)
    out, off = [], 0
    for s in shapes:
        n = 1
        for dim in s:
            n *= dim
        out.append(flat[off:off + n].reshape(s))
        off += n
    return out


def _layer_forward(l, x, prm, wg):
    S, D = x.shape
    h = _rmsnorm_fwd(f"attn_norm_{l}", x, prm["attn_norm"][l])
    w_in = wg.get("w_in", l, h, prefetch_next=l > 0)
    ns_in = w_in.shape[-1]
    tmi = min(1024, S)
    p = _matmul(
        f"in_proj_{l}", h, w_in, (S, N_CHIPS * ns_in), F32, grid=(S // tmi, N_CHIPS, 1),
        a_spec=pl.BlockSpec((tmi, D), lambda i, j, k: (i, 0)),
        b_spec=pl.BlockSpec((None, D, ns_in), lambda i, j, k: (j, 0, 0)),
        o_spec=pl.BlockSpec((tmi, ns_in), lambda i, j, k: (i, j)),
        contract=(1, 0), acc_shape=(tmi, ns_in), deps=wg.deps())
    if l == 0:
        wg.prefetch_after("w_in", l, p)
    y_a = _sgu_fwd(f"sgu_fwd_{l}", p, prm["sgu_wt"][l], prm["sgu_bb"][l])
    y_b = _conv_fwd(f"conv_fwd_{l}", p, prm["conv_w"][l])
    os, lses = [], []
    for g in range(N_PATTERNS):
        o_g, lse_g = _attn_fwd(f"attn_fwd_{l}_{g}", p, g, prm["q_gain"][l], prm["k_gain"][l], prm["bd"])
        os.append(o_g)
        lses.append(lse_g)
    y_c = _mix_fwd(f"mix_fwd_{l}", os, lses)
    ycat = jnp.concatenate([y_a, y_b, y_c], axis=1)
    tmb, tnb = min(1024, S), min(1024, D)
    w_out = wg.get("w_out", l, ycat)
    kq = N_CHIPS * w_out.shape[1]
    tmo = min(512, S)
    rows = pl.BlockSpec((tmo, D), lambda i, j, k: (i, 0))
    x1, h2 = _matmul(
        f"out_proj_{l}", ycat, w_out.reshape(kq, D), (S, D), F32, grid=(S // tmo, 1, 1),
        a_spec=pl.BlockSpec((tmo, kq), lambda i, j, k: (i, 0)),
        b_spec=pl.BlockSpec((kq, D), lambda i, j, k: (0, 0)),
        o_spec=rows, contract=(1, 0), acc_shape=(tmo, D),
        extras=(x, prm["mlp_norm"][l]), extra_specs=(rows, pl.BlockSpec((1, D), lambda i, j, k: (0, 0))),
        epi=_residual_and_norm, more_outs=(((S, D), BF16, rows),), deps=wg.deps())
    w_mlp_in = wg.get("w_mlp_in", l, x1)
    nf4 = w_mlp_in.shape[-1]
    r = _matmul(
        f"mlp_in_{l}", h2, w_mlp_in, (S, N_CHIPS * nf4), BF16, grid=(S // tmb, N_CHIPS, 1),
        a_spec=pl.BlockSpec((tmb, D), lambda i, j, k: (i, 0)),
        b_spec=pl.BlockSpec((None, D, nf4), lambda i, j, k: (j, 0, 0)),
        o_spec=pl.BlockSpec((tmb, nf4), lambda i, j, k: (i, j)),
        contract=(1, 0), acc_shape=(tmb, nf4), epi=_relu, deps=wg.deps())
    w_mlp_out = wg.get("w_mlp_out", l, r)
    dff4 = w_mlp_out.shape[1]
    tk = min(2048, dff4)
    kpc = dff4 // tk
    x2 = _matmul(
        f"mlp_out_{l}", r, w_mlp_out, (S, D), F32, grid=(S // tmb, D // tnb, N_CHIPS * kpc),
        a_spec=pl.BlockSpec((tmb, tk), lambda i, j, k: (i, k)),
        b_spec=pl.BlockSpec((None, tk, tnb), lambda i, j, k: (k // kpc, k % kpc, j)),
        o_spec=pl.BlockSpec((tmb, tnb), lambda i, j, k: (i, j)),
        contract=(1, 0), acc_shape=(tmb, tnb), a_pre=_square,
        extras=(x1,), extra_specs=(pl.BlockSpec((tmb, tnb), lambda i, j, k: (i, j)),),
        epi=lambda acc, res: acc + res, deps=wg.deps())
    saved = dict(x=x, p=p, h=h, os=os, lses=lses, ycat=ycat, x1=x1, r=r, h2=h2)
    return x2, saved


def _layer_backward(l, dx2, dx2b, sv, prm, wg, sink):
    S, D = dx2.shape
    w_in, w_out = wg.get("w_in", l), wg.get("w_out", l)
    w_mlp_in, w_mlp_out = wg.get("w_mlp_in", l), wg.get("w_mlp_out", l)
    dff4 = w_mlp_in.shape[-1]
    dff = N_CHIPS * dff4

    tmb, tnb = min(1024, S), min(1024, D)
    da = _matmul(
        f"mlp_out_bwd_{l}", dx2b, w_mlp_out, (S, dff), BF16, grid=(S // tmb, N_CHIPS, 1),
        a_spec=pl.BlockSpec((tmb, D), lambda i, j, k: (i, 0)),
        b_spec=pl.BlockSpec((None, dff4, D), lambda i, j, k: (j, 0, 0)),
        o_spec=pl.BlockSpec((tmb, dff4), lambda i, j, k: (i, j)),
        contract=(1, 1), acc_shape=(tmb, dff4),
        extras=(sv["r"],), extra_specs=(pl.BlockSpec((tmb, dff4), lambda i, j, k: (i, j)),),
        epi=lambda acc, r: acc * (2.0 * r.astype(F32)), deps=sink.deps())
    tmw = min(1024, dff4)
    mpc = dff4 // tmw
    g_w2 = _matmul(
        f"mlp_out_dw_{l}", sv["r"], dx2b, (N_CHIPS, dff4, D), F32, grid=(N_CHIPS * mpc, D // tnb, 1),
        a_spec=pl.BlockSpec((S, tmw), lambda i, j, k: (0, i)),
        b_spec=pl.BlockSpec((S, tnb), lambda i, j, k: (0, j)),
        o_spec=pl.BlockSpec((None, tmw, tnb), lambda i, j, k: (i // mpc, i % mpc, j)),
        contract=(0, 0), acc_shape=(tmw, tnb), a_pre=_square)
    sink.begin("w_mlp_out", l, g_w2)
    tnx = D
    dh2 = _matmul(
        f"mlp_in_bwd_{l}", da, w_mlp_in, (S, D), F32, grid=(S // tmb, D // tnx, N_CHIPS),
        a_spec=pl.BlockSpec((tmb, dff4), lambda i, j, k: (i, k)),
        b_spec=pl.BlockSpec((None, tnx, dff4), lambda i, j, k: (k, j, 0)),
        o_spec=pl.BlockSpec((tmb, tnx), lambda i, j, k: (i, j)),
        contract=(1, 1), acc_shape=(tmb, tnx), deps=sink.deps())
    sink.advance(dh2)
    tmd = min(1024, D)
    nd = D // tmd
    tnf = min(1024, dff4)
    nf = dff4 // tnf
    g_w1 = _matmul(
        f"mlp_in_dw_{l}", sv["h2"], da, (N_CHIPS, D, dff4), F32, grid=(N_CHIPS * nd, nf, 1),
        a_spec=pl.BlockSpec((S, tmd), lambda i, j, k: (0, i % nd)),
        b_spec=pl.BlockSpec((S, tnf), lambda i, j, k: (0, (i // nd) * nf + j)),
        o_spec=pl.BlockSpec((None, tmd, tnf), lambda i, j, k: (i // nd, i % nd, j)),
        contract=(0, 0), acc_shape=(tmd, tnf))
    sink.begin("w_mlp_in", l, g_w1)
    dx1, dx1b, g_mlp_norm = _rmsnorm_bwd(f"mlp_norm_bwd_{l}", dh2, sv["x1"], prm["mlp_norm"][l], dx2,
                                         deps=sink.deps())

    rq = w_out.shape[1]
    kq = N_CHIPS * rq
    dycat = _matmul(
        f"out_proj_bwd_{l}", dx1b, w_out.reshape(kq, D), (S, kq), F32, grid=(S // tmb, 1, 1),
        a_spec=pl.BlockSpec((tmb, D), lambda i, j, k: (i, 0)),
        b_spec=pl.BlockSpec((kq, D), lambda i, j, k: (0, 0)),
        o_spec=pl.BlockSpec((tmb, kq), lambda i, j, k: (i, 0)),
        contract=(1, 1), acc_shape=(tmb, kq))
    sink.advance(dycat)
    g_wout = _matmul(
        f"out_proj_dw_{l}", sv["ycat"], dx1b, (N_CHIPS, rq, D), F32, grid=(N_CHIPS, 1, 1),
        a_spec=pl.BlockSpec((S, rq), lambda i, j, k: (0, i)),
        b_spec=pl.BlockSpec((S, D), lambda i, j, k: (0, 0)),
        o_spec=pl.BlockSpec((None, rq, D), lambda i, j, k: (i, 0, 0)),
        contract=(0, 0), acc_shape=(rq, D))
    sink.begin("w_out", l, g_wout)

    p = sv["p"]
    du, dv_a, g_sgu_w, db_lanes = _sgu_bwd(f"sgu_bwd_{l}", p, dycat, prm["sgu_wt"][l], prm["sgu_wtt"][l],
                                           prm["sgu_bb"][l])
    g_sgu_b = db_lanes[:, :A_HEADS].T
    db, dc, dxb, g_conv = _conv_bwd(f"conv_bwd_{l}", p, dycat, prm["conv_w"][l])
    do3, c3 = _mix_bwd(f"mix_bwd_{l}", sv["os"], sv["lses"], dycat, prm["bd"])
    dqs, dks, dvs, dgqs, dgks = [], [], [], [], []
    for g in range(N_PATTERNS):
        dq, dk, dv, dgq, dgk = _attn_bwd(f"attn_bwd_{l}_{g}", p, g, sv["lses"][g], do3, c3,
                                         prm["q_gain"][l], prm["k_gain"][l], prm["bd"])
        dqs.append(dq)
        dks.append(dk)
        dvs.append(dv)
        dgqs.append(dgq)
        dgks.append(dgk)
    g_q = jnp.concatenate(dgqs, axis=1).reshape(N_PATTERNS * PW // HEAD_DIM, HEAD_DIM).sum(axis=0)
    g_k = jnp.concatenate(dgks, axis=1).reshape(N_PATTERNS * PW // HEAD_DIM, HEAD_DIM).sum(axis=0)
    dp = jnp.concatenate([du, dv_a, db, dc, dxb] + [t.astype(BF16) for t in dqs + dks + dvs], axis=1)

    ns_in = w_in.shape[-1]
    tmh = min(512, D)
    nh = D // tmh
    g_win = _matmul(
        f"in_proj_dw_{l}", sv["h"], dp, (N_CHIPS, D, ns_in), F32, grid=(N_CHIPS * nh, 1, 1),
        a_spec=pl.BlockSpec((S, tmh), lambda i, j, k: (0, i % nh)),
        b_spec=pl.BlockSpec((S, ns_in), lambda i, j, k: (0, i // nh)),
        o_spec=pl.BlockSpec((None, tmh, ns_in), lambda i, j, k: (i // nh, i % nh, 0)),
        contract=(0, 0), acc_shape=(tmh, ns_in))
    sink.begin("w_in", l, g_win)
    dh = _matmul(
        f"in_proj_bwd_{l}", dp, w_in, (S, D), F32, grid=(S // tmb, D // tnx, N_CHIPS),
        a_spec=pl.BlockSpec((tmb, ns_in), lambda i, j, k: (i, k)),
        b_spec=pl.BlockSpec((None, tnx, ns_in), lambda i, j, k: (k, j, 0)),
        o_spec=pl.BlockSpec((tmb, tnx), lambda i, j, k: (i, j)),
        contract=(1, 1), acc_shape=(tmb, tnx), deps=sink.deps())
    sink.advance(dh)
    dx0, dx0b, g_attn_norm = _rmsnorm_bwd(f"attn_norm_bwd_{l}", dh, sv["x"], prm["attn_norm"][l], dx1,
                                          deps=sink.deps())

    big = dict(w_in=g_win, w_out=g_wout, w_mlp_in=g_w1, w_mlp_out=g_w2)
    small = dict(attn_norm=g_attn_norm.reshape(-1), sgu_w=g_sgu_w, sgu_b=g_sgu_b, conv_w=g_conv,
                 q_norm=g_q, k_norm=g_k, mlp_norm=g_mlp_norm.reshape(-1))
    return dx0, dx0b, big, small


BIG = ("w_in", "w_out", "w_mlp_in", "w_mlp_out")
SMALL_REPLICATED = ("attn_norm", "sgu_w", "sgu_b", "q_norm", "k_norm", "mlp_norm")


def _local_step(x, target, prm, wg, n_layers, sink):
    saved = []
    h = x
    for l in range(n_layers):
        h, sv = _layer_forward(l, h, prm, wg)
        saved.append(sv)
    dy, dyb, colsq = _loss_kernel(h, target)
    loss = 0.5 * jnp.sum(colsq) / x.shape[1]
    bigs, smalls = [None] * n_layers, [None] * n_layers
    for l in reversed(range(n_layers)):
        dy, dyb, bigs[l], smalls[l] = _layer_backward(l, dy, dyb, saved[l], prm, wg, sink)
    return loss, dy, bigs, smalls


def _prepare_params(attn_norm, sgu_w, sgu_b, conv_full, q_norm, k_norm, mlp_norm):
    n_layers = attn_norm.shape[0]
    tri = jnp.tril(sgu_w)
    idx = jnp.arange(PW)
    bd = (idx[:, None] // HEAD_DIM == idx[None, :] // HEAD_DIM).astype(BF16)
    return dict(
        attn_norm=[attn_norm[l][None, :] for l in range(n_layers)],
        mlp_norm=[mlp_norm[l][None, :] for l in range(n_layers)],
        sgu_wt=[tri[l].astype(BF16) for l in range(n_layers)],
        sgu_wtt=[tri[l].transpose(0, 2, 1).astype(BF16) for l in range(n_layers)],
        sgu_bb=[jnp.repeat(sgu_b[l].T, HEAD_DIM, axis=1) for l in range(n_layers)],
        conv_w=[conv_full[l] for l in range(n_layers)],
        q_gain=[jnp.tile(q_norm[l], PW // HEAD_DIM)[None, :] for l in range(n_layers)],
        k_gain=[jnp.tile(k_norm[l], PW // HEAD_DIM)[None, :] for l in range(n_layers)],
        bd=bd,
    )


def kernel(x, attn_norm, w_in, sgu_w, sgu_b, conv_w, q_norm, k_norm, w_out, mlp_norm, w_mlp_in, w_mlp_out, loss_target, m_attn_norm, m_w_in, m_sgu_w, m_sgu_b, m_conv_w, m_q_norm, m_k_norm, m_w_out, m_mlp_norm, m_w_mlp_in, m_w_mlp_out, v_attn_norm, v_w_in, v_sgu_w, v_sgu_b, v_conv_w, v_q_norm, v_k_norm, v_w_out, v_mlp_norm, v_w_mlp_in, v_w_mlp_out):
    n_layers = attn_norm.shape[0]
    weights = dict(attn_norm=attn_norm, w_in=w_in, sgu_w=sgu_w, sgu_b=sgu_b, conv_w=conv_w, q_norm=q_norm,
                   k_norm=k_norm, w_out=w_out, mlp_norm=mlp_norm, w_mlp_in=w_mlp_in, w_mlp_out=w_mlp_out)
    mom_m = dict(attn_norm=m_attn_norm, w_in=m_w_in, sgu_w=m_sgu_w, sgu_b=m_sgu_b, conv_w=m_conv_w,
                 q_norm=m_q_norm, k_norm=m_k_norm, w_out=m_w_out, mlp_norm=m_mlp_norm, w_mlp_in=m_w_mlp_in,
                 w_mlp_out=m_w_mlp_out)
    mom_v = dict(attn_norm=v_attn_norm, w_in=v_w_in, sgu_w=v_sgu_w, sgu_b=v_sgu_b, conv_w=v_conv_w,
                 q_norm=v_q_norm, k_norm=v_k_norm, w_out=v_w_out, mlp_norm=v_mlp_norm, w_mlp_in=v_w_mlp_in,
                 w_mlp_out=v_w_mlp_out)
    order = ("attn_norm", "w_in", "sgu_w", "sgu_b", "conv_w", "q_norm", "k_norm", "w_out", "mlp_norm",
             "w_mlp_in", "w_mlp_out")
    chip = 2 * lax.axis_index("x") + lax.axis_index("y")
    c_arr = jnp.stack([lax.axis_index("c"), chip]).astype(jnp.int32)

    conv_cols = conv_w.shape[-1]
    chip_arr = chip.astype(jnp.int32).reshape(1)
    conv_pack = jnp.pad(conv_w.reshape(-1), (0, 2048 - conv_w.size)).reshape(1, 16, 128)
    wg = _GatheredWeights()
    wg.start([("conv_w", 0), ("w_in", 0)],
             [_place_shard("place_conv_w", conv_pack, 0, chip_arr, F32),
              _place_shard("place_w_in_0", weights["w_in"], 0, chip_arr, BF16)])
    keys = [(n, l) for l in range(n_layers) for n in BIG if (n, l) != ("w_in", 0)]
    first = wg.deps()
    wg.start(keys, [_place_shard(f"place_{n}_{l}", weights[n], l, chip_arr, BF16, deps=first) for n, l in keys])
    conv_full = wg.get("conv_w", 0, wg.deps()[-1]).reshape(N_CHIPS, 2048)[:, :conv_w.size].reshape(N_CHIPS, n_layers, 3, conv_cols)
    conv_full = conv_full.transpose(1, 2, 0, 3).reshape(n_layers, 3, N_CHIPS * conv_cols)
    prm = _prepare_params(attn_norm, sgu_w, sgu_b, conv_full, q_norm, k_norm, mlp_norm)

    sink = _GradReducer(c_arr)
    loss_local, grad_x, _, smalls = _local_step(x[0], loss_target[0], prm, wg, n_layers, sink)
    loss = lax.psum(loss_local, ("x", "y", "c"))

    small_names = SMALL_REPLICATED + ("conv_w",)
    small_shapes = [(n_layers,) + tuple(smalls[0][n].shape) for n in small_names]
    packed = _pack_rows([jnp.stack([smalls[l][n] for l in range(n_layers)]) for n in small_names])
    small_send, small_recv, small_land, small_token = _small_start(packed, sink.deps())

    grads, delta, new_m, new_v = {}, {}, {}, {}

    def update(n, after):
        shp = weights[n].shape
        two_d = (shp[0] * shp[1], shp[2])
        d, nm, nv, g = _adamw(f"adamw_{n}", weights[n].reshape(two_d), sink.reduced(n, after).reshape(two_d),
                              mom_m[n].reshape(two_d), mom_v[n].reshape(two_d))
        grads[n], delta[n], new_m[n], new_v[n] = g.reshape(shp), d.reshape(shp), nm.reshape(shp), nv.reshape(shp)

    token = small_token
    for n in ("w_mlp_out", "w_mlp_in", "w_out"):
        token = sink.reduce(n, n_layers, token)
    update("w_mlp_out", token)
    token = sink.reduce("w_in", n_layers, delta["w_mlp_out"])
    update("w_mlp_in", token)
    update("w_out", delta["w_mlp_in"])
    update("w_in", delta["w_out"])
    small_land = _small_wait(packed, small_land, small_send, small_recv, delta["w_in"])
    grads.update(zip(small_names, _unpack_rows(_sum_devices(small_land), small_shapes)))
    grads["conv_w"] = lax.dynamic_slice_in_dim(grads["conv_w"], chip * conv_cols, conv_cols, axis=2)
    smalls_all = SMALL_REPLICATED + ("conv_w",)
    shapes = [weights[n].shape for n in smalls_all]
    d, nm, nv, _ = _adamw("adamw_small",
                          _pack_rows([weights[n] for n in smalls_all]), _pack_rows([grads[n] for n in smalls_all]),
                          _pack_rows([mom_m[n] for n in smalls_all]), _pack_rows([mom_v[n] for n in smalls_all]))
    for n, dd, mm, vv in zip(smalls_all, _unpack_rows(d, shapes), _unpack_rows(nm, shapes), _unpack_rows(nv, shapes)):
        delta[n], new_m[n], new_v[n] = dd, mm, vv

    return (loss, grad_x[None], *[grads[n] for n in order], *[delta[n] for n in order],
            *[new_m[n] for n in order], *[new_v[n] for n in order])
```

```python
import jax
import jax.numpy as jnp
from jax import lax
from jax.experimental import pallas as pl
from jax.experimental.pallas import tpu as pltpu

F32 = jnp.float32
BF16 = jnp.bfloat16
SDS = jax.ShapeDtypeStruct

EPS = 1e-6
HEAD_DIM = 64
A_HEADS = 8
A_WIDTH = 512
CHUNK = 128
B_WIDTH = 768
C_WIDTH = 768
N_PATTERNS = 3
PATTERN_DILATION = (1, 4, 16)
PW = 256
D_IN_PROJ = 5632
OFF_AU, OFF_AV, OFF_BB, OFF_BC, OFF_BX, OFF_Q, OFF_K, OFF_V = 0, 512, 1024, 1792, 2560, 3328, 4096, 4864
N_CHIPS = 4
N_DEV = 8
BLK = 128

ADAM_LR, ADAM_B1, ADAM_B2, ADAM_EPS, ADAM_WD, ADAM_STEP = 0.001, 0.9, 0.999, 1e-08, 0.01, 10

V7X_VMEM_LIMIT = 56 * 1024 * 1024
MESH = pl.DeviceIdType.MESH
NEG = -1e30


def _cp(n_axes):
    return pltpu.CompilerParams(dimension_semantics=("arbitrary",) * n_axes, vmem_limit_bytes=V7X_VMEM_LIMIT)


def _hbm_spec():
    return pl.BlockSpec(memory_space=pl.ANY)


def _relu(t):
    return jnp.maximum(t, 0.0)


def _square(t):
    return t * t


def _residual_and_norm(acc, res, g):
    x1 = acc + res
    return x1, x1 * lax.rsqrt(jnp.mean(x1 * x1, axis=-1, keepdims=True) + EPS) * g


def _matmul(name, a, b, out_shape, out_dtype, *, grid, a_spec, b_spec, o_spec, contract, acc_shape,
            extras=(), extra_specs=(), a_pre=None, epi=None, deps=(), more_outs=()):
    nk = grid[2]
    n_ex = len(extras)
    n_dep = len(deps)
    n_out = 1 + len(more_outs)
    dims = (((contract[0],), (contract[1],)), ((), ()))

    def product(a_ref, b_ref):
        av = a_ref[...] if a_pre is None else a_pre(a_ref[...])
        return lax.dot_general(av, b_ref[...], dims, preferred_element_type=F32)

    def finish(r, ex, o_refs):
        if epi is not None:
            r = epi(r, *[e[...] for e in ex])
        vals = r if n_out > 1 else (r,)
        for o_ref, v in zip(o_refs, vals):
            o_ref[...] = v.astype(o_ref.dtype)

    def body_single(a_ref, b_ref, *rest):
        finish(product(a_ref, b_ref), rest[:n_ex], rest[n_ex + n_dep:n_ex + n_dep + n_out])

    def body(a_ref, b_ref, *rest):
        ex = rest[:n_ex]
        o_ref = rest[n_ex + n_dep:n_ex + n_dep + n_out]
        acc_ref = rest[n_ex + n_dep + n_out]
        k = pl.program_id(2)

        @pl.when(k == 0)
        def _():
            acc_ref[...] = product(a_ref, b_ref)

        @pl.when((k > 0) & (k < nk - 1))
        def _():
            acc_ref[...] += product(a_ref, b_ref)

        @pl.when(k == nk - 1)
        def _():
            finish(acc_ref[...] + product(a_ref, b_ref), ex, o_ref)

    out = pl.pallas_call(
        body_single if nk == 1 else body, name=name, grid=grid,
        in_specs=[a_spec, b_spec, *extra_specs] + [_hbm_spec()] * n_dep,
        out_specs=[o_spec] + [spec for _, _, spec in more_outs],
        out_shape=[SDS(out_shape, out_dtype)] + [SDS(shape, dtype) for shape, dtype, _ in more_outs],
        scratch_shapes=[] if nk == 1 else [pltpu.VMEM(acc_shape, F32)],
        compiler_params=_cp(3),
    )(a, b, *extras, *deps)
    return out if more_outs else out[0]


def _loss_kernel(y, t):
    S, D = y.shape
    tm = min(256, S)

    def body(y_ref, t_ref, dy_ref, dyb_ref, l_ref):
        @pl.when(pl.program_id(0) == 0)
        def _():
            l_ref[...] = jnp.zeros_like(l_ref)
        e = y_ref[...] - t_ref[...]
        l_ref[...] += jnp.sum(e * e, axis=0, keepdims=True)
        dy = e * (1.0 / D)
        dy_ref[...] = dy
        dyb_ref[...] = dy.astype(BF16)

    row = pl.BlockSpec((tm, D), lambda i: (i, 0))
    return pl.pallas_call(
        body, name="loss_head", grid=(S // tm,),
        in_specs=[row, row],
        out_specs=[row, row, pl.BlockSpec((1, D), lambda i: (0, 0))],
        out_shape=[SDS((S, D), F32), SDS((S, D), BF16), SDS((1, D), F32)],
        compiler_params=_cp(1),
    )(y, t)


def _rmsnorm_fwd(name, x, g):
    S, D = x.shape
    tm = min(512, S)

    def body(x_ref, g_ref, h_ref):
        xv = x_ref[...]
        y = xv * lax.rsqrt(jnp.mean(xv * xv, axis=-1, keepdims=True) + EPS) * g_ref[...]
        h_ref[...] = y.astype(h_ref.dtype)

    row = pl.BlockSpec((tm, D), lambda i: (i, 0))
    return pl.pallas_call(
        body, name=name, grid=(S // tm,),
        in_specs=[row, pl.BlockSpec((1, D), lambda i: (0, 0))],
        out_specs=row,
        out_shape=SDS((S, D), BF16),
        compiler_params=_cp(1),
    )(x, g)


def _rmsnorm_bwd(name, dh, x, g, dres, deps=()):
    S, D = x.shape
    tm = min(256, S)
    n_dep = len(deps)

    def body(dh_ref, x_ref, g_ref, dres_ref, *rest):
        dx_ref, dxb_ref, dg_ref = rest[n_dep:]
        @pl.when(pl.program_id(0) == 0)
        def _():
            dg_ref[...] = jnp.zeros_like(dg_ref)
        xv = x_ref[...]
        dhv = dh_ref[...]
        rstd = lax.rsqrt(jnp.mean(xv * xv, axis=-1, keepdims=True) + EPS)
        xhat = xv * rstd
        dg_ref[...] += jnp.sum(dhv * xhat, axis=0, keepdims=True)
        dxn = dhv * g_ref[...]
        dx = dres_ref[...] + rstd * (dxn - xhat * jnp.mean(dxn * xhat, axis=-1, keepdims=True))
        dx_ref[...] = dx
        dxb_ref[...] = dx.astype(BF16)

    row = pl.BlockSpec((tm, D), lambda i: (i, 0))
    vec = pl.BlockSpec((1, D), lambda i: (0, 0))
    return pl.pallas_call(
        body, name=name, grid=(S // tm,),
        in_specs=[row, row, vec, row] + [_hbm_spec()] * n_dep,
        out_specs=[row, row, vec],
        out_shape=[SDS((S, D), F32), SDS((S, D), BF16), SDS((1, D), F32)],
        compiler_params=_cp(1),
    )(dh, x, g, dres, *deps)


def _adamw(name, w, g, m, v):
    R, C = w.shape
    tr = 256 if R % 256 == 0 else R
    c1 = 1.0 - ADAM_B1 ** ADAM_STEP
    c2 = 1.0 - ADAM_B2 ** ADAM_STEP

    def body(w_ref, g_ref, m_ref, v_ref, d_ref, nm_ref, nv_ref, g_out_ref):
        gv = g_ref[...]
        nm = ADAM_B1 * m_ref[...] + (1.0 - ADAM_B1) * gv
        nv = ADAM_B2 * v_ref[...] + (1.0 - ADAM_B2) * (gv * gv)
        m_hat = nm / c1
        v_hat = nv / c2
        d_ref[...] = -ADAM_LR * (m_hat / (jnp.sqrt(v_hat) + ADAM_EPS) + ADAM_WD * w_ref[...])
        nm_ref[...] = nm
        nv_ref[...] = nv
        g_out_ref[...] = gv

    blk = pl.BlockSpec((tr, C), lambda i: (i, 0))
    return pl.pallas_call(
        body, name=name, grid=(R // tr,),
        in_specs=[blk] * 4, out_specs=[blk] * 4,
        out_shape=[SDS((R, C), F32)] * 4,
        compiler_params=_cp(1),
    )(w, g, m, v)


SGU_STEP_ROWS = 1024


def _pair_select(lane, lo, hi):
    return jnp.where(lane < HEAD_DIM, lo, hi)


def _sgu_fwd(name, p, wt, bb):
    S = p.shape[0]

    rows = min(SGU_STEP_ROWS, S)

    def body(u_ref, v_ref, wt_ref, bb_ref, o_ref):
        lane = lax.broadcasted_iota(jnp.int32, (CHUNK, 128), 1)
        for ci in range(rows // CHUNK):
            rs = slice(CHUNK * ci, CHUNK * (ci + 1))
            for pp in range(A_HEADS // 2):
                cs = slice(128 * pp, 128 * (pp + 1))
                vb = v_ref[rs, cs].astype(BF16)
                mixed = _pair_select(lane,
                                     jnp.dot(wt_ref[2 * pp], vb, preferred_element_type=F32),
                                     jnp.dot(wt_ref[2 * pp + 1], vb, preferred_element_type=F32)) + bb_ref[:, cs]
                o_ref[rs, cs] = (u_ref[rs, cs] * mixed).astype(o_ref.dtype)

    return pl.pallas_call(
        body, name=name, grid=(S // rows,),
        in_specs=[pl.BlockSpec((rows, A_WIDTH), lambda c: (c, OFF_AU // A_WIDTH)),
                  pl.BlockSpec((rows, A_WIDTH), lambda c: (c, OFF_AV // A_WIDTH)),
                  pl.BlockSpec((A_HEADS, CHUNK, CHUNK), lambda c: (0, 0, 0)),
                  pl.BlockSpec((CHUNK, A_WIDTH), lambda c: (0, 0))],
        out_specs=pl.BlockSpec((rows, A_WIDTH), lambda c: (c, 0)),
        out_shape=SDS((S, A_WIDTH), BF16),
        compiler_params=_cp(1),
    )(p, p, wt, bb)


def _sgu_bwd(name, p, dycat, wt, wtt, bb):
    S = p.shape[0]
    rows = min(SGU_STEP_ROWS, S)

    def body(u_ref, v_ref, dy_ref, wt_ref, wtt_ref, bb_ref, du_ref, dv_ref, dw_ref, db_ref, dbacc_ref):
        c = pl.program_id(0)

        @pl.when(c == 0)
        def _():
            dw_ref[...] = jnp.zeros_like(dw_ref)
            dbacc_ref[...] = jnp.zeros_like(dbacc_ref)

        lane = lax.broadcasted_iota(jnp.int32, (CHUNK, 128), 1)
        row = lax.broadcasted_iota(jnp.int32, (CHUNK, 128), 0)
        causal = row >= lane
        nt = (((1,), (1,)), ((), ()))
        for pp in range(A_HEADS // 2):
            cs = slice(128 * pp, 128 * (pp + 1))
            dw_lo = jnp.zeros((CHUNK, CHUNK), F32)
            dw_hi = jnp.zeros((CHUNK, CHUNK), F32)
            dm_sum = jnp.zeros((CHUNK, 128), F32)
            for ci in range(rows // CHUNK):
                rs = slice(CHUNK * ci, CHUNK * (ci + 1))
                vb = v_ref[rs, cs].astype(BF16)
                dy = dy_ref[rs, cs]
                mixed = _pair_select(lane,
                                     jnp.dot(wt_ref[2 * pp], vb, preferred_element_type=F32),
                                     jnp.dot(wt_ref[2 * pp + 1], vb, preferred_element_type=F32)) + bb_ref[:, cs]
                du_ref[rs, cs] = (dy * mixed).astype(du_ref.dtype)
                dm = dy * u_ref[rs, cs]
                dmb = dm.astype(BF16)
                dv = _pair_select(lane,
                                  jnp.dot(wtt_ref[2 * pp], dmb, preferred_element_type=F32),
                                  jnp.dot(wtt_ref[2 * pp + 1], dmb, preferred_element_type=F32))
                dv_ref[rs, cs] = dv.astype(dv_ref.dtype)
                dm_sum += dm
                dm_lo = jnp.where(lane < HEAD_DIM, dm, 0.0).astype(BF16)
                dm_hi = jnp.where(lane >= HEAD_DIM, dm, 0.0).astype(BF16)
                dw_lo += lax.dot_general(dm_lo, vb, nt, preferred_element_type=F32)
                dw_hi += lax.dot_general(dm_hi, vb, nt, preferred_element_type=F32)
            dbacc_ref[:, cs] += dm_sum
            dw_ref[2 * pp] += jnp.where(causal, dw_lo, 0.0)
            dw_ref[2 * pp + 1] += jnp.where(causal, dw_hi, 0.0)

        @pl.when(c == S // rows - 1)
        def _():
            out = jnp.zeros((CHUNK, 128), F32)
            for pp in range(A_HEADS // 2):
                acc = dbacc_ref[:, 128 * pp:128 * (pp + 1)]
                s_lo = jnp.sum(jnp.where(lane < HEAD_DIM, acc, 0.0), axis=1, keepdims=True)
                s_hi = jnp.sum(jnp.where(lane >= HEAD_DIM, acc, 0.0), axis=1, keepdims=True)
                out = jnp.where(lane == 2 * pp, s_lo, out)
                out = jnp.where(lane == 2 * pp + 1, s_hi, out)
            db_ref[...] = out

    chunk = lambda col: pl.BlockSpec((rows, A_WIDTH), lambda c: (c, col))
    wspec = pl.BlockSpec((A_HEADS, CHUNK, CHUNK), lambda c: (0, 0, 0))
    return pl.pallas_call(
        body, name=name, grid=(S // rows,),
        in_specs=[chunk(OFF_AU // A_WIDTH), chunk(OFF_AV // A_WIDTH), chunk(0), wspec, wspec,
                  pl.BlockSpec((CHUNK, A_WIDTH), lambda c: (0, 0))],
        out_specs=[chunk(0), chunk(0), wspec, pl.BlockSpec((CHUNK, 128), lambda c: (0, 0))],
        out_shape=[SDS((S, A_WIDTH), BF16), SDS((S, A_WIDTH), BF16),
                   SDS((A_HEADS, CHUNK, CHUNK), F32), SDS((CHUNK, 128), F32)],
        scratch_shapes=[pltpu.VMEM((CHUNK, A_WIDTH), F32)],
        compiler_params=_cp(1),
    )(p, p, dycat, wt, wtt, bb)


CONV_HALO = 8
CONV_COLS = 256
CONV_ROWS = 2048


def _shift_down(a, halo, k):
    T = a.shape[0]
    row = lax.broadcasted_iota(jnp.int32, a.shape, 0)
    out = pltpu.roll(a, k, 0)
    for r in range(k):
        out = jnp.where(row == r, halo[CONV_HALO - k + r:CONV_HALO - k + r + 1, :], out)
    return out


def _shift_up(a, halo, k):
    T = a.shape[0]
    row = lax.broadcasted_iota(jnp.int32, a.shape, 0)
    out = pltpu.roll(a, T - k, 0)
    for r in range(k):
        out = jnp.where(row == T - k + r, halo[r:r + 1, :], out)
    return out


def _conv_specs(S, T):
    hb = T // CONV_HALO
    last = S // CONV_HALO - 1
    tile = lambda col0: pl.BlockSpec((T, CONV_COLS), lambda j, i: (i, col0 + j))
    prev = lambda col0: pl.BlockSpec((CONV_HALO, CONV_COLS), lambda j, i: (jnp.maximum(i * hb - 1, 0), col0 + j))
    nxt = lambda col0: pl.BlockSpec((CONV_HALO, CONV_COLS), lambda j, i: (jnp.minimum((i + 1) * hb, last), col0 + j))
    return tile, prev, nxt


def _conv_fwd(name, p, w):
    S = p.shape[0]
    T = min(CONV_ROWS, S)
    tile, prev, _ = _conv_specs(S, T)
    cb, cc, cx = OFF_BB // CONV_COLS, OFF_BC // CONV_COLS, OFF_BX // CONV_COLS

    def body(b_ref, c_ref, x_ref, ch_ref, xh_ref, w_ref, o_ref):
        i = pl.program_id(1)
        z = c_ref[...] * x_ref[...]
        zh = jnp.where(i > 0, ch_ref[...] * xh_ref[...], 0.0)
        z1 = _shift_down(z, zh, 1)
        z2 = _shift_down(z, zh, 2)
        conv = w_ref[0:1, :] * z2 + w_ref[1:2, :] * z1 + w_ref[2:3, :] * z
        o_ref[...] = (b_ref[...] * conv).astype(o_ref.dtype)

    return pl.pallas_call(
        body, name=name, grid=(B_WIDTH // CONV_COLS, S // T),
        in_specs=[tile(cb), tile(cc), tile(cx), prev(cc), prev(cx),
                  pl.BlockSpec((3, CONV_COLS), lambda j, i: (0, j))],
        out_specs=tile(0),
        out_shape=SDS((S, B_WIDTH), BF16),
        compiler_params=_cp(2),
    )(p, p, p, p, p, w)


def _conv_bwd(name, p, dycat, w):
    S = p.shape[0]
    T = min(CONV_ROWS, S)
    tile, prev, nxt = _conv_specs(S, T)
    cb, cc, cx = OFF_BB // CONV_COLS, OFF_BC // CONV_COLS, OFF_BX // CONV_COLS
    cdy = A_WIDTH // CONV_COLS
    n_i = S // T

    def body(b_ref, c_ref, x_ref, dy_ref, ch_ref, xh_ref, bn_ref, dyn_ref, w_ref,
             db_ref, dc_ref, dx_ref, dw_ref):
        i = pl.program_id(1)

        @pl.when(i == 0)
        def _():
            dw_ref[...] = jnp.zeros_like(dw_ref)

        cv = c_ref[...]
        xv = x_ref[...]
        z = cv * xv
        zh = jnp.where(i > 0, ch_ref[...] * xh_ref[...], 0.0)
        z1 = _shift_down(z, zh, 1)
        z2 = _shift_down(z, zh, 2)
        w0, w1, w2 = w_ref[0:1, :], w_ref[1:2, :], w_ref[2:3, :]
        conv = w0 * z2 + w1 * z1 + w2 * z
        dy = dy_ref[...]
        db_ref[...] = (dy * conv).astype(db_ref.dtype)
        dconv = dy * b_ref[...]
        dconv_n = jnp.where(i < n_i - 1, dyn_ref[...] * bn_ref[...], 0.0)
        dz = w2 * dconv + w1 * _shift_up(dconv, dconv_n, 1) + w0 * _shift_up(dconv, dconv_n, 2)
        dc_ref[...] = (dz * xv).astype(dc_ref.dtype)
        dx_ref[...] = (dz * cv).astype(dx_ref.dtype)
        dw_ref[0:1, :] += jnp.sum(dconv * z2, axis=0, keepdims=True)
        dw_ref[1:2, :] += jnp.sum(dconv * z1, axis=0, keepdims=True)
        dw_ref[2:3, :] += jnp.sum(dconv * z, axis=0, keepdims=True)

    wspec = pl.BlockSpec((3, CONV_COLS), lambda j, i: (0, j))
    return pl.pallas_call(
        body, name=name, grid=(B_WIDTH // CONV_COLS, n_i),
        in_specs=[tile(cb), tile(cc), tile(cx), tile(cdy), prev(cc), prev(cx), nxt(cb), nxt(cdy), wspec],
        out_specs=[tile(0), tile(0), tile(0), wspec],
        out_shape=[SDS((S, B_WIDTH), BF16)] * 3 + [SDS((3, B_WIDTH), F32)],
        compiler_params=_cp(2),
    )(p, p, p, dycat, p, p, p, dycat, w)


def _seg_sum(t, bd):
    hi = t.astype(BF16)
    lo = (t - hi.astype(F32)).astype(BF16)
    return jnp.dot(hi, bd, preferred_element_type=F32) + jnp.dot(lo, bd, preferred_element_type=F32)


def _head_norm(x, g, bd):
    rstd = lax.rsqrt(_seg_sum(x * x, bd) * (1.0 / HEAD_DIM) + EPS)
    xhat = x * rstd
    return xhat * g, xhat, rstd


def _head_norm_bwd(dy, g, xhat, rstd, bd):
    dxh = dy * g
    return rstd * (dxh - xhat * (_seg_sum(dxh * xhat, bd) * (1.0 / HEAD_DIM)))


def _band_mask(has_prev):
    row = lax.broadcasted_iota(jnp.int32, (BLK, 2 * BLK), 0)
    col = lax.broadcasted_iota(jnp.int32, (BLK, 2 * BLK), 1)
    first_key = jnp.where(has_prev, 0, BLK)
    return (col >= row) & (col <= row + BLK) & (col >= first_key)


def _residue_rows(r, d):
    return slice(None) if d == 1 else pl.ds(r, BLK, stride=d)


STRIDED_LANES = 128
RESIDUES_PER_PASS = 8


def _step_width(d):
    return PW if d == 1 else STRIDED_LANES


def _n_stack(lane):
    return lane.shape[1] // HEAD_DIM


def _for_residues(d, fn):
    if d == 1:
        fn(0)
    else:
        per_pass = min(d, RESIDUES_PER_PASS)

        def several(i, carry):
            for u in range(per_pass):
                fn(per_pass * i + u)
            return carry
        lax.fori_loop(0, d // per_pass, several, 0)


def _head_mask(lane, j):
    return (lane >= HEAD_DIM * j) & (lane < HEAD_DIM * (j + 1))


def _stack_heads(x, lane):
    return jnp.concatenate([jnp.where(_head_mask(lane, j), x, 0.0) for j in range(_n_stack(lane))], axis=0)


def _unstack_heads(y, lane):
    out = y[:BLK]
    for j in range(1, _n_stack(lane)):
        out = jnp.where(lane >= HEAD_DIM * j, y[BLK * j:BLK * (j + 1)], out)
    return out


def _head_columns(v, lane):
    return jnp.concatenate([jnp.max(jnp.where(_head_mask(lane, j), v, NEG), axis=1, keepdims=True)
                            for j in range(_n_stack(lane))], axis=0)


def _attn_fwd(name, p, g, gq, gk, bd):
    S = p.shape[0]
    d = PATTERN_DILATION[g]
    rows = BLK * d
    hw = _step_width(d)
    nt = (((1,), (1,)), ((), ()))

    def body(q_ref, kc_ref, kp_ref, vc_ref, vp_ref, gq_ref, gk_ref, bd_ref, o_ref, lse_ref):
        has_prev = pl.program_id(1) > 0
        bdv = bd_ref[...]
        band = jnp.concatenate([_band_mask(has_prev)] * (hw // HEAD_DIM), axis=0)
        lane = lax.broadcasted_iota(jnp.int32, (1, hw), 1)

        def residue(r):
            rr = _residue_rows(r, d)
            qn, _, _ = _head_norm(q_ref[rr, :], gq_ref[...], bdv)
            kn, _, _ = _head_norm(jnp.concatenate([kp_ref[rr, :], kc_ref[rr, :]], axis=0), gk_ref[...], bdv)
            knb = kn.astype(BF16)
            vb = jnp.concatenate([vp_ref[rr, :], vc_ref[rr, :]], axis=0).astype(BF16)
            qs = _stack_heads(qn, lane).astype(BF16)
            s = lax.dot_general(qs, knb, nt, preferred_element_type=F32) * (HEAD_DIM ** -0.5)
            s = jnp.where(band, s, NEG)
            m = jnp.max(s, axis=1, keepdims=True)
            e = jnp.exp(s - m)
            den = jnp.sum(e, axis=1, keepdims=True)
            pv = jnp.dot(e.astype(BF16), vb, preferred_element_type=F32)
            o_ref[rr, :] = _unstack_heads(pv / den, lane)
            lse_ref[rr, :] = _unstack_heads(jnp.broadcast_to(m + jnp.log(den), pv.shape), lane)

        _for_residues(d, residue)

    per = PW // hw
    cq, ck, cv = (OFF_Q + PW * g) // hw, (OFF_K + PW * g) // hw, (OFF_V + PW * g) // hw
    cur = lambda col: pl.BlockSpec((rows, hw), lambda h, n: (n, col + h))
    prv = lambda col: pl.BlockSpec((rows, hw), lambda h, n: (jnp.maximum(n - 1, 0), col + h))
    vec = pl.BlockSpec((1, hw), lambda h, n: (0, h))
    return pl.pallas_call(
        body, name=name, grid=(per, S // rows),
        in_specs=[cur(cq), cur(ck), prv(ck), cur(cv), prv(cv), vec, vec, pl.BlockSpec((hw, hw), lambda h, n: (0, 0))],
        out_specs=[cur(0), cur(0)],
        out_shape=[SDS((S, PW), F32)] * 2,
        compiler_params=_cp(2),
    )(p, p, p, p, p, gq, gk, bd)


def _attn_bwd(name, p, g, lse, do3, c3, gq, gk, bd):
    S = p.shape[0]
    d = PATTERN_DILATION[g]
    rows = BLK * d
    nblk = S // rows
    hw = _step_width(d)
    nt = (((1,), (1,)), ((), ()))
    tn = (((0,), (0,)), ((), ()))

    def body(q_ref, kc_ref, kp_ref, vc_ref, vp_ref, lse_ref, do_ref, c_ref, gq_ref, gk_ref, bd_ref,
             dq_ref, dk_ref, dv_ref, dgq_ref, dgk_ref, ck_ref, cv_ref, dq_keep_ref):
        n = pl.program_id(1)

        @pl.when(n == 0)
        def _():
            ck_ref[...] = jnp.zeros_like(ck_ref)
            cv_ref[...] = jnp.zeros_like(cv_ref)
            dgq_ref[...] = jnp.zeros_like(dgq_ref)
            dgk_ref[...] = jnp.zeros_like(dgk_ref)

        @pl.when(n == nblk)
        def _():
            dq_ref[...] = dq_keep_ref[...]
            dk_ref[...] = ck_ref[...]
            dv_ref[...] = cv_ref[...]

        bdv = bd_ref[...]
        gqv = gq_ref[...]
        gkv = gk_ref[...]
        band = jnp.concatenate([_band_mask(n > 0)] * (hw // HEAD_DIM), axis=0)
        lane = lax.broadcasted_iota(jnp.int32, (1, hw), 1)

        def residue(r):
            rr = _residue_rows(r, d)
            qn, qhat, qrstd = _head_norm(q_ref[rr, :], gqv, bdv)
            kn, khat, krstd = _head_norm(jnp.concatenate([kp_ref[rr, :], kc_ref[rr, :]], axis=0), gkv, bdv)
            knb = kn.astype(BF16)
            vb = jnp.concatenate([vp_ref[rr, :], vc_ref[rr, :]], axis=0).astype(BF16)
            qs = _stack_heads(qn, lane).astype(BF16)
            dos = _stack_heads(do_ref[rr, :], lane).astype(BF16)
            s = lax.dot_general(qs, knb, nt, preferred_element_type=F32) * (HEAD_DIM ** -0.5)
            prob = jnp.where(band, jnp.exp(s - _head_columns(lse_ref[rr, :], lane)), 0.0)
            dp = lax.dot_general(dos, vb, nt, preferred_element_type=F32)
            ds = (prob * (dp + _head_columns(c_ref[rr, :], lane)) * (HEAD_DIM ** -0.5)).astype(BF16)
            dqn = _unstack_heads(jnp.dot(ds, knb, preferred_element_type=F32), lane)
            dkn = lax.dot_general(ds, qs, tn, preferred_element_type=F32)
            dvv = lax.dot_general(prob.astype(BF16), dos, tn, preferred_element_type=F32)

            dq = _head_norm_bwd(dqn, gqv, qhat, qrstd, bdv)
            dq_ref[rr, :] = dq
            dq_keep_ref[rr, :] = dq
            dk2 = _head_norm_bwd(dkn, gkv, khat, krstd, bdv)
            dgq_ref[...] += jnp.sum(dqn * qhat, axis=0, keepdims=True)
            dgk_ref[...] += jnp.sum(dkn * khat, axis=0, keepdims=True)
            dk_ref[rr, :] = ck_ref[rr, :] + dk2[:BLK]
            dv_ref[rr, :] = cv_ref[rr, :] + dvv[:BLK]
            ck_ref[rr, :] = dk2[BLK:]
            cv_ref[rr, :] = dvv[BLK:]

        @pl.when(n < nblk)
        def _():
            _for_residues(d, residue)

    last = nblk - 1
    per = PW // hw
    cq, ck, cv = (OFF_Q + PW * g) // hw, (OFF_K + PW * g) // hw, (OFF_V + PW * g) // hw
    cur = lambda col: pl.BlockSpec((rows, hw), lambda h, n: (jnp.minimum(n, last), col + h))
    prv = lambda col: pl.BlockSpec((rows, hw), lambda h, n: (jnp.maximum(jnp.minimum(n, last) - 1, 0), col + h))
    cur3 = pl.BlockSpec((None, rows, hw), lambda h, n: (g, jnp.minimum(n, last), h))
    done = pl.BlockSpec((rows, hw), lambda h, n: (jnp.maximum(n - 1, 0), h))
    vec = pl.BlockSpec((1, hw), lambda h, n: (0, h))
    return pl.pallas_call(
        body, name=name, grid=(per, nblk + 1),
        in_specs=[cur(cq), cur(ck), prv(ck), cur(cv), prv(cv), cur(0), cur3, cur3, vec, vec,
                  pl.BlockSpec((hw, hw), lambda h, n: (0, 0))],
        out_specs=[cur(0), done, done, vec, vec],
        out_shape=[SDS((S, PW), F32)] * 3 + [SDS((1, PW), F32)] * 2,
        scratch_shapes=[pltpu.VMEM((rows, hw), F32)] * 3,
        compiler_params=_cp(2),
    )(p, p, p, p, p, lse, do3, c3, gq, gk, bd)


def _mix_fwd(name, os, lses):
    S = os[0].shape[0]
    tm = min(1024, S)

    def body(o0, o1, o2, l0, l1, l2, y_ref):
        o = [o0[...], o1[...], o2[...]]
        l = [l0[...], l1[...], l2[...]]
        m = jnp.maximum(jnp.maximum(l[0], l[1]), l[2])
        e = [jnp.exp(t - m) for t in l]
        inv = 1.0 / (e[0] + e[1] + e[2])
        for g in range(N_PATTERNS):
            y_ref[:, PW * g:PW * (g + 1)] = (o[g] * (e[g] * inv)).astype(y_ref.dtype)

    blk = pl.BlockSpec((tm, PW), lambda i: (i, 0))
    return pl.pallas_call(
        body, name=name, grid=(S // tm,),
        in_specs=[blk] * 6,
        out_specs=pl.BlockSpec((tm, C_WIDTH), lambda i: (i, 0)),
        out_shape=SDS((S, C_WIDTH), BF16),
        compiler_params=_cp(1),
    )(*os, *lses)


def _mix_bwd(name, os, lses, dycat, bd):
    S = os[0].shape[0]
    tm = min(1024, S)
    c0 = (A_WIDTH + B_WIDTH) // PW

    def body(o0, o1, o2, l0, l1, l2, dy0_ref, dy1_ref, dy2_ref, bd_ref, do_ref, c_ref):
        bdv = bd_ref[...]
        o = [o0[...], o1[...], o2[...]]
        l = [l0[...], l1[...], l2[...]]
        dys = [dy0_ref[...], dy1_ref[...], dy2_ref[...]]
        m = jnp.maximum(jnp.maximum(l[0], l[1]), l[2])
        e = [jnp.exp(t - m) for t in l]
        inv = 1.0 / (e[0] + e[1] + e[2])
        alpha = [t * inv for t in e]
        da = [_seg_sum(dys[g] * o[g], bdv) for g in range(N_PATTERNS)]
        mean_da = alpha[0] * da[0] + alpha[1] * da[1] + alpha[2] * da[2]
        for g in range(N_PATTERNS):
            do_ref[g] = dys[g] * alpha[g]
            c_ref[g] = -alpha[g] * mean_da

    blk = pl.BlockSpec((tm, PW), lambda i: (i, 0))
    blk3 = pl.BlockSpec((N_PATTERNS, tm, PW), lambda i: (0, i, 0))
    dyspec = lambda g: pl.BlockSpec((tm, PW), lambda i: (i, c0 + g))
    return pl.pallas_call(
        body, name=name, grid=(S // tm,),
        in_specs=[blk] * 6 + [dyspec(0), dyspec(1), dyspec(2), pl.BlockSpec((PW, PW), lambda i: (0, 0))],
        out_specs=[blk3, blk3],
        out_shape=[SDS((N_PATTERNS, S, PW), F32)] * 2,
        compiler_params=_cp(1),
    )(*os, *lses, dycat, dycat, dycat, bd)


def _mesh_pos():
    x, y, c = lax.axis_index("x"), lax.axis_index("y"), lax.axis_index("c")
    chips = [(1 - x, y), (x, 1 - y), (1 - x, 1 - y)]
    chip_idx = [2 * cx + cy for cx, cy in chips]
    return x, y, c, 2 * x + y, chips, chip_idx


def _place_shard(name, w, layer, chip_arr, out_dtype, deps=()):
    _, R, C = w.shape
    tr = min(256, R)

    def body(chip_ref, w_ref, *rest):
        o_ref = rest[-1]
        o_ref[...] = w_ref[...].astype(o_ref.dtype)

    return pl.pallas_call(
        body, name=name,
        grid_spec=pltpu.PrefetchScalarGridSpec(
            num_scalar_prefetch=1, grid=(R // tr,),
            in_specs=[pl.BlockSpec((None, tr, C), lambda i, chip_ref: (layer, i, 0))] + [_hbm_spec()] * len(deps),
            out_specs=pl.BlockSpec((None, tr, C), lambda i, chip_ref: (chip_ref[0], i, 0))),
        out_shape=SDS((N_CHIPS, R, C), out_dtype),
        compiler_params=_cp(1),
    )(chip_arr, w, *deps)


HBM_SPEC = pl.BlockSpec(memory_space=pltpu.HBM)
SEM_SPEC = pl.BlockSpec(memory_space=pltpu.SEMAPHORE)
SPLIT_COPY = pltpu.SideEffectType.DATAFLOW_SIDE_EFFECTING
N_PEER_CHIPS = N_CHIPS - 1
TOKEN_SHAPE = SDS((8, 128), F32)
TOKEN_SPEC = pl.BlockSpec(memory_space=pltpu.VMEM)


def _in_hbm(a):
    return pltpu.with_memory_space_constraint(a, pltpu.HBM)


def _gather_start(name, bufs):
    T = len(bufs)

    def body(*refs):
        ins = refs[:T]
        send_sems, recv_sems = refs[T:2 * T], refs[2 * T:3 * T]
        token = refs[4 * T]
        x, y, c, me, chips, chip_idx = _mesh_pos()
        for t in range(T):
            hr = ins[t].shape[1] // 2
            mine = ins[t].at[me, pl.ds(c * hr, hr), :]
            for j in range(N_PEER_CHIPS):
                pltpu.make_async_remote_copy(src_ref=mine, dst_ref=mine, send_sem=send_sems[t].at[j],
                                             recv_sem=recv_sems[t].at[j], device_id=(*chips[j], c),
                                             device_id_type=MESH).start()
        token[...] = jnp.zeros_like(token)

    sems = [pltpu.SemaphoreType.DMA((N_PEER_CHIPS,))] * T
    out = pl.pallas_call(
        body, name=name,
        in_specs=[HBM_SPEC] * T,
        out_specs=[SEM_SPEC] * (2 * T) + [HBM_SPEC] * T + [TOKEN_SPEC],
        out_shape=sems + sems + [pltpu.HBM(b.shape, b.dtype) for b in bufs] + [TOKEN_SHAPE],
        input_output_aliases={t: 2 * T + t for t in range(T)},
        compiler_params=pltpu.CompilerParams(has_side_effects=SPLIT_COPY),
    )(*[_in_hbm(b) for b in bufs])
    return out[:T], out[T:2 * T], out[2 * T:3 * T], out[3 * T]


def _gather_wait(name, buf, send_sem, recv_sem, after):
    n_in = 3 if after is None else 4

    def body(*refs):
        buf_ref, ssem, rsem = refs[:3]
        x, y, c, me, chips, chip_idx = _mesh_pos()
        hr = buf_ref.shape[1] // 2
        mine = buf_ref.at[me, pl.ds(c * hr, hr), :]
        for j in range(N_PEER_CHIPS):
            got = buf_ref.at[chip_idx[j], pl.ds(c * hr, hr), :]
            cp = pltpu.make_async_remote_copy(src_ref=mine, dst_ref=got, send_sem=ssem.at[j], recv_sem=rsem.at[j],
                                              device_id=(*chips[j], c), device_id_type=MESH)
            cp.wait_send()
            cp.wait_recv()

    args = [buf, send_sem, recv_sem] + ([] if after is None else [after])
    return pl.pallas_call(
        body, name=name,
        in_specs=[HBM_SPEC, SEM_SPEC, SEM_SPEC] + [_hbm_spec()] * (n_in - 3),
        out_specs=HBM_SPEC,
        out_shape=pltpu.HBM(buf.shape, buf.dtype),
        input_output_aliases={0: 0},
        compiler_params=pltpu.CompilerParams(has_side_effects=SPLIT_COPY),
    )(*args)


def _forward_start(name, buf):
    def body(buf_ref, send_sems, recv_sems, buf_thru, token):
        x, y, c, me, chips, chip_idx = _mesh_pos()
        hr = buf_ref.shape[1] // 2
        for j in range(N_PEER_CHIPS):
            got = buf_ref.at[chip_idx[j], pl.ds(c * hr, hr), :]
            pltpu.make_async_remote_copy(src_ref=got, dst_ref=got, send_sem=send_sems.at[j], recv_sem=recv_sems.at[j],
                                         device_id=(x, y, 1 - c), device_id_type=MESH).start()
        token[...] = jnp.zeros_like(token)

    sems = pltpu.SemaphoreType.DMA((N_PEER_CHIPS,))
    return pl.pallas_call(
        body, name=name,
        in_specs=[HBM_SPEC],
        out_specs=[SEM_SPEC, SEM_SPEC, HBM_SPEC, TOKEN_SPEC],
        out_shape=[sems, sems, pltpu.HBM(buf.shape, buf.dtype), TOKEN_SHAPE],
        input_output_aliases={0: 2},
        compiler_params=pltpu.CompilerParams(has_side_effects=SPLIT_COPY),
    )(_in_hbm(buf))


def _forward_wait(name, buf, send_sems, recv_sems, after):
    n_in = 3 if after is None else 4

    def body(*refs):
        buf_ref, ssems, rsems = refs[:3]
        x, y, c, me, chips, chip_idx = _mesh_pos()
        hr = buf_ref.shape[1] // 2
        for j in range(N_PEER_CHIPS):
            sent = buf_ref.at[chip_idx[j], pl.ds(c * hr, hr), :]
            theirs = buf_ref.at[chip_idx[j], pl.ds((1 - c) * hr, hr), :]
            cp = pltpu.make_async_remote_copy(src_ref=sent, dst_ref=theirs, send_sem=ssems.at[j],
                                              recv_sem=rsems.at[j], device_id=(x, y, 1 - c), device_id_type=MESH)
            cp.wait_send()
            cp.wait_recv()

    args = [buf, send_sems, recv_sems] + ([] if after is None else [after])
    return pl.pallas_call(
        body, name=name,
        in_specs=[HBM_SPEC, SEM_SPEC, SEM_SPEC] + [_hbm_spec()] * (n_in - 3),
        out_specs=HBM_SPEC,
        out_shape=pltpu.HBM(buf.shape, buf.dtype),
        input_output_aliases={0: 0},
        compiler_params=pltpu.CompilerParams(has_side_effects=SPLIT_COPY),
    )(*args)


class _GatheredWeights:
    def __init__(self):
        self._order = []
        self._pending = {}
        self._forwarding = {}
        self._ready = {}
        self._tokens = []

    def start(self, keys, bufs):
        send_sems, recv_sems, thru, token = _gather_start(f"gather_start_{len(self._order)}", bufs)
        self._tokens.append(token)
        self._order.extend(keys)
        self._pending.update({k: (b, s, r) for k, b, s, r in zip(keys, thru, send_sems, recv_sems)})

    def _prefetch(self, key, after):
        if key in self._pending:
            buf, ssem, rsem = self._pending.pop(key)
            tag = f"{key[0]}_{key[1]}"
            buf = _gather_wait(f"gather_wait_{tag}", buf, ssem, rsem, after)
            ssems, rsems, buf, token = _forward_start(f"gather_fwd_start_{tag}", buf)
            self._forwarding[key] = (buf, ssems, rsems)
            self._tokens.append(token)

    def get(self, name, layer, after=None, prefetch_next=True):
        key = (name, layer)
        if key not in self._ready:
            self._prefetch(key, after)
            buf, ssems, rsems = self._forwarding.pop(key)
            self._ready[key] = _forward_wait(f"gather_fwd_wait_{name}_{layer}", buf, ssems, rsems, after)
            if prefetch_next:
                self.prefetch_after(name, layer, after)
        return self._ready[key]

    def prefetch_after(self, name, layer, after):
        nxt = self._order.index((name, layer)) + 1
        if nxt < len(self._order):
            self._prefetch(self._order[nxt], after)

    def deps(self):
        tokens, self._tokens = self._tokens, []
        return tokens


def _swap_copy(g_ref, land_ref, send_sem, recv_sem):
    x, y, c, _, _, _ = _mesh_pos()
    hr = g_ref.shape[1] // 2
    return pltpu.make_async_remote_copy(src_ref=g_ref.at[:, pl.ds((1 - c) * hr, hr), :], dst_ref=land_ref,
                                        send_sem=send_sem, recv_sem=recv_sem, device_id=(x, y, 1 - c),
                                        device_id_type=MESH)


def _swap_start(name, g):
    land_shape = (g.shape[0], g.shape[1] // 2, g.shape[2])

    def body(g_ref, land_ref, send_sem, recv_sem, land_thru, token):
        _swap_copy(g_ref, land_ref, send_sem, recv_sem).start()
        token[...] = jnp.zeros_like(token)

    return pl.pallas_call(
        body, name=name,
        in_specs=[HBM_SPEC, HBM_SPEC],
        out_specs=[SEM_SPEC, SEM_SPEC, HBM_SPEC, TOKEN_SPEC],
        out_shape=[pltpu.SemaphoreType.DMA(()), pltpu.SemaphoreType.DMA(()), pltpu.HBM(land_shape, g.dtype),
                   TOKEN_SHAPE],
        input_output_aliases={1: 2},
        compiler_params=pltpu.CompilerParams(has_side_effects=SPLIT_COPY),
    )(_in_hbm(g), _in_hbm(lax.empty(land_shape, g.dtype)))


def _swap_wait(name, g, land, send_sem, recv_sem, after):
    def body(g_ref, land_ref, send_sem, recv_sem, after_ref, land_out):
        cp = _swap_copy(g_ref, land_ref, send_sem, recv_sem)
        cp.wait_send()
        cp.wait_recv()

    return pl.pallas_call(
        body, name=name,
        in_specs=[HBM_SPEC, HBM_SPEC, SEM_SPEC, SEM_SPEC, _hbm_spec()],
        out_specs=HBM_SPEC,
        out_shape=pltpu.HBM(land.shape, land.dtype),
        input_output_aliases={1: 0},
        compiler_params=pltpu.CompilerParams(has_side_effects=SPLIT_COPY),
    )(_in_hbm(g), land, send_sem, recv_sem, after)


def _add_my_half(name, g, r, pos_arr):
    ns, R, C = g.shape
    hr = R // 2
    tr = min(256, hr)
    nt = hr // tr

    def body(pos_ref, g_ref, r_ref, o_ref, land_ref):
        t = (g_ref[...] + r_ref[...]).astype(o_ref.dtype)
        o_ref[...] = t

        @pl.when(pl.program_id(1) == pos_ref[1])
        def _():
            land_ref[...] = t

    blk = pl.BlockSpec((None, tr, C), lambda i, s, pos_ref: (s, i, 0))
    return pl.pallas_call(
        body, name=name,
        grid_spec=pltpu.PrefetchScalarGridSpec(
            num_scalar_prefetch=1, grid=(nt, ns),
            in_specs=[pl.BlockSpec((None, tr, C), lambda i, s, pos_ref: (s, pos_ref[0] * nt + i, 0)), blk],
            out_specs=[blk, pl.BlockSpec((None, tr, C), lambda i, s, pos_ref: (pos_ref[1], i, 0))]),
        out_shape=[SDS((ns, hr, C), BF16)] * 2,
        compiler_params=_cp(2),
    )(pos_arr, g, r)


def _exchange_start(name, part, land):
    def body(part_ref, land_ref, send_sems, recv_sems, land_thru, token):
        x, y, c, me, chips, chip_idx = _mesh_pos()
        for j in range(N_PEER_CHIPS):
            pltpu.make_async_remote_copy(src_ref=part_ref.at[chip_idx[j]], dst_ref=land_ref.at[me],
                                         send_sem=send_sems.at[j], recv_sem=recv_sems.at[j],
                                         device_id=(*chips[j], c), device_id_type=MESH).start()
        token[...] = jnp.zeros_like(token)

    sems = pltpu.SemaphoreType.DMA((N_PEER_CHIPS,))
    return pl.pallas_call(
        body, name=name,
        in_specs=[HBM_SPEC, HBM_SPEC],
        out_specs=[SEM_SPEC, SEM_SPEC, HBM_SPEC, TOKEN_SPEC],
        out_shape=[sems, sems, pltpu.HBM(land.shape, land.dtype), TOKEN_SHAPE],
        input_output_aliases={1: 2},
        compiler_params=pltpu.CompilerParams(has_side_effects=SPLIT_COPY),
    )(_in_hbm(part), _in_hbm(land))


def _exchange_wait(name, part, land, send_sems, recv_sems, after):
    def body(part_ref, land_ref, send_sems, recv_sems, after_ref, land_out):
        x, y, c, me, chips, chip_idx = _mesh_pos()
        for j in range(N_PEER_CHIPS):
            cp = pltpu.make_async_remote_copy(src_ref=part_ref.at[chip_idx[j]], dst_ref=land_ref.at[chip_idx[j]],
                                              send_sem=send_sems.at[j], recv_sem=recv_sems.at[j],
                                              device_id=(*chips[j], c), device_id_type=MESH)
            cp.wait_send()
            cp.wait_recv()

    return pl.pallas_call(
        body, name=name,
        in_specs=[HBM_SPEC, HBM_SPEC, SEM_SPEC, SEM_SPEC, _hbm_spec()],
        out_specs=HBM_SPEC,
        out_shape=pltpu.HBM(land.shape, land.dtype),
        input_output_aliases={1: 0},
        compiler_params=pltpu.CompilerParams(has_side_effects=SPLIT_COPY),
    )(_in_hbm(part), land, send_sems, recv_sems, after)


class _GradReducer:
    def __init__(self, c_arr):
        self._c_arr = c_arr
        self._swapping = []
        self._exchanging = {}
        self._joining = {}
        self._tokens = []

    def begin(self, name, layer, g):
        tag = f"{name}_{layer}"
        ssem, rsem, land, token = _swap_start(f"rs_swap_start_{tag}", g)
        self._swapping.append((name, layer, g, ssem, rsem, land))
        self._tokens.append(token)

    def advance(self, after):
        for name, layer, g, ssem, rsem, land in self._swapping:
            tag = f"{name}_{layer}"
            theirs = _swap_wait(f"rs_swap_wait_{tag}", g, land, ssem, rsem, after)
            part, own = _add_my_half(f"rs_add_{tag}", g, theirs, self._c_arr)
            ssems, rsems, land2, token = _exchange_start(f"rs_xchg_start_{tag}", part, own)
            self._exchanging[(name, layer)] = (part, ssems, rsems, land2)
            self._tokens.append(token)
        self._swapping = []

    def deps(self):
        tokens, self._tokens = self._tokens, []
        return tokens

    def reduce(self, name, n_layers, after):
        buf = None
        for layer in range(n_layers):
            part, ssems, rsems, land = self._exchanging.pop((name, layer))
            tag = f"{name}_{layer}"
            landed = _exchange_wait(f"rs_xchg_wait_{tag}", part, land, ssems, rsems, after)
            buf = _sum_chips(f"rs_sum_{tag}", landed, self._c_arr, layer, n_layers, buf)
        ssem, rsem, buf, token = _join_start(f"rs_join_start_{name}", buf)
        self._joining[name] = (buf, ssem, rsem)
        return token

    def reduced(self, name, after):
        buf, ssem, rsem = self._joining.pop(name)
        return _join_wait(f"rs_join_wait_{name}", buf, ssem, rsem, after)


def _sum_chips(name, r, c_arr, layer, n_layers, prev):
    ns, H, C = r.shape
    tr = min(256, H)
    nt = H // tr

    def body(c_ref, r_ref, *rest):
        o_ref = rest[-1]
        o_ref[...] = ((r_ref[0].astype(F32) + r_ref[1].astype(F32)) + r_ref[2].astype(F32)) + r_ref[3].astype(F32)

    in_specs = [pl.BlockSpec((ns, tr, C), lambda i, c_ref: (0, i, 0))]
    args = [c_arr, r]
    aliases = {}
    if prev is not None:
        in_specs.append(_hbm_spec())
        args.append(prev)
        aliases = {2: 0}
    return pl.pallas_call(
        body, name=name,
        grid_spec=pltpu.PrefetchScalarGridSpec(
            num_scalar_prefetch=1, grid=(nt,), in_specs=in_specs,
            out_specs=pl.BlockSpec((None, tr, C), lambda i, c_ref: (layer, c_ref[0] * nt + i, 0))),
        out_shape=SDS((n_layers, 2 * H, C), F32),
        input_output_aliases=aliases,
        compiler_params=_cp(1),
    )(*args)


def _join_copy(buf_ref, send_sem, recv_sem):
    x, y, c, _, _, _ = _mesh_pos()
    hr = buf_ref.shape[1] // 2
    mine = buf_ref.at[:, pl.ds(c * hr, hr), :]
    theirs = buf_ref.at[:, pl.ds((1 - c) * hr, hr), :]
    send = pltpu.make_async_remote_copy(src_ref=mine, dst_ref=mine, send_sem=send_sem, recv_sem=recv_sem,
                                        device_id=(x, y, 1 - c), device_id_type=MESH)
    arrive = pltpu.make_async_remote_copy(src_ref=theirs, dst_ref=theirs, send_sem=send_sem, recv_sem=recv_sem,
                                          device_id=(x, y, 1 - c), device_id_type=MESH)
    return send, arrive


def _join_start(name, buf):
    def body(buf_ref, send_sem, recv_sem, buf_thru, token):
        _join_copy(buf_ref, send_sem, recv_sem)[0].start()
        token[...] = jnp.zeros_like(token)

    return pl.pallas_call(
        body, name=name,
        in_specs=[HBM_SPEC],
        out_specs=[SEM_SPEC, SEM_SPEC, HBM_SPEC, TOKEN_SPEC],
        out_shape=[pltpu.SemaphoreType.DMA(()), pltpu.SemaphoreType.DMA(()), pltpu.HBM(buf.shape, buf.dtype),
                   TOKEN_SHAPE],
        input_output_aliases={0: 2},
        compiler_params=pltpu.CompilerParams(has_side_effects=SPLIT_COPY),
    )(_in_hbm(buf))


def _join_wait(name, buf, send_sem, recv_sem, after):
    def body(buf_ref, send_sem, recv_sem, after_ref, buf_out):
        send, arrive = _join_copy(buf_ref, send_sem, recv_sem)
        send.wait_send()
        arrive.wait_recv()

    return pl.pallas_call(
        body, name=name,
        in_specs=[HBM_SPEC, SEM_SPEC, SEM_SPEC, _hbm_spec()],
        out_specs=HBM_SPEC,
        out_shape=pltpu.HBM(buf.shape, buf.dtype),
        input_output_aliases={0: 0},
        compiler_params=pltpu.CompilerParams(has_side_effects=SPLIT_COPY),
    )(buf, send_sem, recv_sem, after)


def _small_copy(k, buf_ref, land_ref, send_sems, recv_sems):
    x, y, c = lax.axis_index("x"), lax.axis_index("y"), lax.axis_index("c")
    me = 4 * x + 2 * y + c
    peer = (x ^ ((k >> 2) & 1), y ^ ((k >> 1) & 1), c ^ (k & 1))
    cp = pltpu.make_async_remote_copy(src_ref=buf_ref, dst_ref=land_ref.at[me], send_sem=send_sems.at[k - 1],
                                      recv_sem=recv_sems.at[k - 1], device_id=peer, device_id_type=MESH)
    return me, peer, cp


def _small_start(buf, deps):
    land = jnp.broadcast_to(buf[None], (N_DEV,) + buf.shape)
    n_dep = len(deps)

    def body(buf_ref, land_ref, *rest):
        send_sems, recv_sems, _, token = rest[n_dep:]
        for k in range(1, N_DEV):
            _small_copy(k, buf_ref, land_ref, send_sems, recv_sems)[2].start()
        token[...] = jnp.zeros_like(token)

    sems = pltpu.SemaphoreType.DMA((N_DEV - 1,))
    return pl.pallas_call(
        body, name="small_gather_start",
        in_specs=[HBM_SPEC, HBM_SPEC] + [_hbm_spec()] * n_dep,
        out_specs=[SEM_SPEC, SEM_SPEC, HBM_SPEC, TOKEN_SPEC],
        out_shape=[sems, sems, pltpu.HBM(land.shape, land.dtype), TOKEN_SHAPE],
        input_output_aliases={1: 2},
        compiler_params=pltpu.CompilerParams(has_side_effects=SPLIT_COPY),
    )(_in_hbm(buf), _in_hbm(land), *deps)


def _small_wait(buf, land, send_sems, recv_sems, after):
    def body(buf_ref, land_ref, send_sems, recv_sems, after_ref, land_out):
        for k in range(1, N_DEV):
            me, peer, cp = _small_copy(k, buf_ref, land_ref, send_sems, recv_sems)
            cp.wait_send()
            got = land_ref.at[me ^ k]
            pltpu.make_async_remote_copy(src_ref=got, dst_ref=got, send_sem=send_sems.at[k - 1],
                                         recv_sem=recv_sems.at[k - 1], device_id=peer,
                                         device_id_type=MESH).wait_recv()

    return pl.pallas_call(
        body, name="small_gather_wait",
        in_specs=[HBM_SPEC, HBM_SPEC, SEM_SPEC, SEM_SPEC, _hbm_spec()],
        out_specs=HBM_SPEC,
        out_shape=pltpu.HBM(land.shape, land.dtype),
        input_output_aliases={1: 0},
        compiler_params=pltpu.CompilerParams(has_side_effects=SPLIT_COPY),
    )(_in_hbm(buf), land, send_sems, recv_sems, after)


def _sum_devices(land):
    n, R, C = land.shape

    def body(land_ref, out_ref):
        acc = land_ref[0]
        for d in range(1, n):
            acc = acc + land_ref[d]
        out_ref[...] = acc

    return pl.pallas_call(
        body, name="small_sum",
        in_specs=[pl.BlockSpec(memory_space=pltpu.VMEM)],
        out_specs=pl.BlockSpec(memory_space=pltpu.VMEM),
        out_shape=SDS((R, C), land.dtype),
        compiler_params=pltpu.CompilerParams(vmem_limit_bytes=V7X_VMEM_LIMIT),
    )(land)


def _pack_rows(vectors):
    flat = jnp.concatenate([v.reshape(-1) for v in vectors])
    n = flat.shape[0]
    padded = -(-n // 1024) * 1024
    return jnp.pad(flat, (0, padded - n)).reshape(padded // 128, 128)


def _unpack_rows(buf, shapes):
    flat = buf.reshape(-1)
    out, off = [], 0
    for s in shapes:
        n = 1
        for dim in s:
            n *= dim
        out.append(flat[off:off + n].reshape(s))
        off += n
    return out


def _layer_forward(l, x, prm, wg, target=None):
    S, D = x.shape
    h = _rmsnorm_fwd(f"attn_norm_{l}", x, prm["attn_norm"][l])
    w_in = wg.get("w_in", l, h, prefetch_next=l > 0)
    ns_in = w_in.shape[-1]
    tmi = min(1024, S)
    p = _matmul(
        f"in_proj_{l}", h, w_in, (S, N_CHIPS * ns_in), F32, grid=(S // tmi, N_CHIPS, 1),
        a_spec=pl.BlockSpec((tmi, D), lambda i, j, k: (i, 0)),
        b_spec=pl.BlockSpec((None, D, ns_in), lambda i, j, k: (j, 0, 0)),
        o_spec=pl.BlockSpec((tmi, ns_in), lambda i, j, k: (i, j)),
        contract=(1, 0), acc_shape=(tmi, ns_in), deps=wg.deps())
    if l == 0:
        wg.prefetch_after("w_in", l, p)
    y_a = _sgu_fwd(f"sgu_fwd_{l}", p, prm["sgu_wt"][l], prm["sgu_bb"][l])
    y_b = _conv_fwd(f"conv_fwd_{l}", p, prm["conv_w"][l])
    os, lses = [], []
    for g in range(N_PATTERNS):
        o_g, lse_g = _attn_fwd(f"attn_fwd_{l}_{g}", p, g, prm["q_gain"][l], prm["k_gain"][l], prm["bd"])
        os.append(o_g)
        lses.append(lse_g)
    y_c = _mix_fwd(f"mix_fwd_{l}", os, lses)
    ycat = jnp.concatenate([y_a, y_b, y_c], axis=1)
    tmb, tnb = min(1024, S), min(1024, D)
    w_out = wg.get("w_out", l, ycat)
    kq = N_CHIPS * w_out.shape[1]
    tmo = min(512, S)
    rows = pl.BlockSpec((tmo, D), lambda i, j, k: (i, 0))
    x1, h2 = _matmul(
        f"out_proj_{l}", ycat, w_out.reshape(kq, D), (S, D), F32, grid=(S // tmo, 1, 1),
        a_spec=pl.BlockSpec((tmo, kq), lambda i, j, k: (i, 0)),
        b_spec=pl.BlockSpec((kq, D), lambda i, j, k: (0, 0)),
        o_spec=rows, contract=(1, 0), acc_shape=(tmo, D),
        extras=(x, prm["mlp_norm"][l]), extra_specs=(rows, pl.BlockSpec((1, D), lambda i, j, k: (0, 0))),
        epi=_residual_and_norm, more_outs=(((S, D), BF16, rows),), deps=wg.deps())
    w_mlp_in = wg.get("w_mlp_in", l, x1)
    nf4 = w_mlp_in.shape[-1]
    r = _matmul(
        f"mlp_in_{l}", h2, w_mlp_in, (S, N_CHIPS * nf4), BF16, grid=(S // tmb, N_CHIPS, 1),
        a_spec=pl.BlockSpec((tmb, D), lambda i, j, k: (i, 0)),
        b_spec=pl.BlockSpec((None, D, nf4), lambda i, j, k: (j, 0, 0)),
        o_spec=pl.BlockSpec((tmb, nf4), lambda i, j, k: (i, j)),
        contract=(1, 0), acc_shape=(tmb, nf4), epi=_relu, deps=wg.deps())
    w_mlp_out = wg.get("w_mlp_out", l, r)
    dff4 = w_mlp_out.shape[1]
    tk = min(2048, dff4)
    kpc = dff4 // tk
    saved = dict(x=x, p=p, h=h, os=os, lses=lses, ycat=ycat, x1=x1, r=r, h2=h2)
    if target is not None:
        tml = min(512, S)
        tile = pl.BlockSpec((tml, tnb), lambda i, j, k: (i, j))

        def loss_head(acc, res, tgt):
            e = acc + res - tgt
            dy = e * (1.0 / D)
            return dy, dy, jnp.sum(e * e, axis=0, keepdims=True)

        dy, dyb, sq = _matmul(
            f"mlp_out_{l}", r, w_mlp_out, (S, D), F32, grid=(S // tml, D // tnb, N_CHIPS * kpc),
            a_spec=pl.BlockSpec((tml, tk), lambda i, j, k: (i, k)),
            b_spec=pl.BlockSpec((None, tk, tnb), lambda i, j, k: (k // kpc, k % kpc, j)),
            o_spec=tile, contract=(1, 0), acc_shape=(tml, tnb), a_pre=_square,
            extras=(x1, target), extra_specs=(tile, tile), epi=loss_head,
            more_outs=(((S, D), BF16, tile),
                       ((S // tml, 1, D), F32, pl.BlockSpec((None, 1, tnb), lambda i, j, k: (i, 0, j)))),
            deps=wg.deps())
        return (dy, dyb, sq), saved
    x2 = _matmul(
        f"mlp_out_{l}", r, w_mlp_out, (S, D), F32, grid=(S // tmb, D // tnb, N_CHIPS * kpc),
        a_spec=pl.BlockSpec((tmb, tk), lambda i, j, k: (i, k)),
        b_spec=pl.BlockSpec((None, tk, tnb), lambda i, j, k: (k // kpc, k % kpc, j)),
        o_spec=pl.BlockSpec((tmb, tnb), lambda i, j, k: (i, j)),
        contract=(1, 0), acc_shape=(tmb, tnb), a_pre=_square,
        extras=(x1,), extra_specs=(pl.BlockSpec((tmb, tnb), lambda i, j, k: (i, j)),),
        epi=lambda acc, res: acc + res, deps=wg.deps())
    return x2, saved


def _layer_backward(l, dx2, dx2b, sv, prm, wg, sink):
    S, D = dx2.shape
    w_in, w_out = wg.get("w_in", l), wg.get("w_out", l)
    w_mlp_in, w_mlp_out = wg.get("w_mlp_in", l), wg.get("w_mlp_out", l)
    dff4 = w_mlp_in.shape[-1]
    dff = N_CHIPS * dff4

    tmb, tnb = min(1024, S), min(1024, D)
    da = _matmul(
        f"mlp_out_bwd_{l}", dx2b, w_mlp_out, (S, dff), BF16, grid=(S // tmb, N_CHIPS, 1),
        a_spec=pl.BlockSpec((tmb, D), lambda i, j, k: (i, 0)),
        b_spec=pl.BlockSpec((None, dff4, D), lambda i, j, k: (j, 0, 0)),
        o_spec=pl.BlockSpec((tmb, dff4), lambda i, j, k: (i, j)),
        contract=(1, 1), acc_shape=(tmb, dff4),
        extras=(sv["r"],), extra_specs=(pl.BlockSpec((tmb, dff4), lambda i, j, k: (i, j)),),
        epi=lambda acc, r: acc * (2.0 * r.astype(F32)), deps=sink.deps())
    tmw = min(1024, dff4)
    mpc = dff4 // tmw
    g_w2 = _matmul(
        f"mlp_out_dw_{l}", sv["r"], dx2b, (N_CHIPS, dff4, D), F32, grid=(N_CHIPS * mpc, D // tnb, 1),
        a_spec=pl.BlockSpec((S, tmw), lambda i, j, k: (0, i)),
        b_spec=pl.BlockSpec((S, tnb), lambda i, j, k: (0, j)),
        o_spec=pl.BlockSpec((None, tmw, tnb), lambda i, j, k: (i // mpc, i % mpc, j)),
        contract=(0, 0), acc_shape=(tmw, tnb), a_pre=_square)
    sink.begin("w_mlp_out", l, g_w2)
    tnx = D
    dh2 = _matmul(
        f"mlp_in_bwd_{l}", da, w_mlp_in, (S, D), F32, grid=(S // tmb, D // tnx, N_CHIPS),
        a_spec=pl.BlockSpec((tmb, dff4), lambda i, j, k: (i, k)),
        b_spec=pl.BlockSpec((None, tnx, dff4), lambda i, j, k: (k, j, 0)),
        o_spec=pl.BlockSpec((tmb, tnx), lambda i, j, k: (i, j)),
        contract=(1, 1), acc_shape=(tmb, tnx), deps=sink.deps())
    sink.advance(dh2)
    tmd = min(1024, D)
    nd = D // tmd
    tnf = min(1024, dff4)
    nf = dff4 // tnf
    g_w1 = _matmul(
        f"mlp_in_dw_{l}", sv["h2"], da, (N_CHIPS, D, dff4), F32, grid=(N_CHIPS * nd, nf, 1),
        a_spec=pl.BlockSpec((S, tmd), lambda i, j, k: (0, i % nd)),
        b_spec=pl.BlockSpec((S, tnf), lambda i, j, k: (0, (i // nd) * nf + j)),
        o_spec=pl.BlockSpec((None, tmd, tnf), lambda i, j, k: (i // nd, i % nd, j)),
        contract=(0, 0), acc_shape=(tmd, tnf))
    sink.begin("w_mlp_in", l, g_w1)
    dx1, dx1b, g_mlp_norm = _rmsnorm_bwd(f"mlp_norm_bwd_{l}", dh2, sv["x1"], prm["mlp_norm"][l], dx2,
                                         deps=sink.deps())

    rq = w_out.shape[1]
    kq = N_CHIPS * rq
    dycat = _matmul(
        f"out_proj_bwd_{l}", dx1b, w_out.reshape(kq, D), (S, kq), F32, grid=(S // tmb, 1, 1),
        a_spec=pl.BlockSpec((tmb, D), lambda i, j, k: (i, 0)),
        b_spec=pl.BlockSpec((kq, D), lambda i, j, k: (0, 0)),
        o_spec=pl.BlockSpec((tmb, kq), lambda i, j, k: (i, 0)),
        contract=(1, 1), acc_shape=(tmb, kq))
    sink.advance(dycat)
    g_wout = _matmul(
        f"out_proj_dw_{l}", sv["ycat"], dx1b, (N_CHIPS, rq, D), F32, grid=(N_CHIPS, 1, 1),
        a_spec=pl.BlockSpec((S, rq), lambda i, j, k: (0, i)),
        b_spec=pl.BlockSpec((S, D), lambda i, j, k: (0, 0)),
        o_spec=pl.BlockSpec((None, rq, D), lambda i, j, k: (i, 0, 0)),
        contract=(0, 0), acc_shape=(rq, D))
    sink.begin("w_out", l, g_wout)

    p = sv["p"]
    du, dv_a, g_sgu_w, db_lanes = _sgu_bwd(f"sgu_bwd_{l}", p, dycat, prm["sgu_wt"][l], prm["sgu_wtt"][l],
                                           prm["sgu_bb"][l])
    g_sgu_b = db_lanes[:, :A_HEADS].T
    db, dc, dxb, g_conv = _conv_bwd(f"conv_bwd_{l}", p, dycat, prm["conv_w"][l])
    do3, c3 = _mix_bwd(f"mix_bwd_{l}", sv["os"], sv["lses"], dycat, prm["bd"])
    dqs, dks, dvs, dgqs, dgks = [], [], [], [], []
    for g in range(N_PATTERNS):
        dq, dk, dv, dgq, dgk = _attn_bwd(f"attn_bwd_{l}_{g}", p, g, sv["lses"][g], do3, c3,
                                         prm["q_gain"][l], prm["k_gain"][l], prm["bd"])
        dqs.append(dq)
        dks.append(dk)
        dvs.append(dv)
        dgqs.append(dgq)
        dgks.append(dgk)
    g_q = jnp.concatenate(dgqs, axis=1).reshape(N_PATTERNS * PW // HEAD_DIM, HEAD_DIM).sum(axis=0)
    g_k = jnp.concatenate(dgks, axis=1).reshape(N_PATTERNS * PW // HEAD_DIM, HEAD_DIM).sum(axis=0)
    dp = jnp.concatenate([du, dv_a, db, dc, dxb] + [t.astype(BF16) for t in dqs + dks + dvs], axis=1)

    ns_in = w_in.shape[-1]
    tmh = min(512, D)
    nh = D // tmh
    g_win = _matmul(
        f"in_proj_dw_{l}", sv["h"], dp, (N_CHIPS, D, ns_in), F32, grid=(N_CHIPS * nh, 1, 1),
        a_spec=pl.BlockSpec((S, tmh), lambda i, j, k: (0, i % nh)),
        b_spec=pl.BlockSpec((S, ns_in), lambda i, j, k: (0, i // nh)),
        o_spec=pl.BlockSpec((None, tmh, ns_in), lambda i, j, k: (i // nh, i % nh, 0)),
        contract=(0, 0), acc_shape=(tmh, ns_in))
    sink.begin("w_in", l, g_win)
    dh = _matmul(
        f"in_proj_bwd_{l}", dp, w_in, (S, D), F32, grid=(S // tmb, D // tnx, N_CHIPS),
        a_spec=pl.BlockSpec((tmb, ns_in), lambda i, j, k: (i, k)),
        b_spec=pl.BlockSpec((None, tnx, ns_in), lambda i, j, k: (k, j, 0)),
        o_spec=pl.BlockSpec((tmb, tnx), lambda i, j, k: (i, j)),
        contract=(1, 1), acc_shape=(tmb, tnx), deps=sink.deps())
    sink.advance(dh)
    dx0, dx0b, g_attn_norm = _rmsnorm_bwd(f"attn_norm_bwd_{l}", dh, sv["x"], prm["attn_norm"][l], dx1,
                                          deps=sink.deps())

    big = dict(w_in=g_win, w_out=g_wout, w_mlp_in=g_w1, w_mlp_out=g_w2)
    small = dict(attn_norm=g_attn_norm.reshape(-1), sgu_w=g_sgu_w, sgu_b=g_sgu_b, conv_w=g_conv,
                 q_norm=g_q, k_norm=g_k, mlp_norm=g_mlp_norm.reshape(-1))
    return dx0, dx0b, big, small


BIG = ("w_in", "w_out", "w_mlp_in", "w_mlp_out")
SMALL_REPLICATED = ("attn_norm", "sgu_w", "sgu_b", "q_norm", "k_norm", "mlp_norm")


def _local_step(x, target, prm, wg, n_layers, sink):
    saved = []
    h = x
    for l in range(n_layers):
        h, sv = _layer_forward(l, h, prm, wg, target if l == n_layers - 1 else None)
        saved.append(sv)
    dy, dyb, colsq = h
    loss = 0.5 * jnp.sum(colsq) / x.shape[1]
    bigs, smalls = [None] * n_layers, [None] * n_layers
    for l in reversed(range(n_layers)):
        dy, dyb, bigs[l], smalls[l] = _layer_backward(l, dy, dyb, saved[l], prm, wg, sink)
    return loss, dy, bigs, smalls


def _prepare_params(attn_norm, sgu_w, sgu_b, conv_full, q_norm, k_norm, mlp_norm):
    n_layers = attn_norm.shape[0]
    tri = jnp.tril(sgu_w)
    idx = jnp.arange(PW)
    bd = (idx[:, None] // HEAD_DIM == idx[None, :] // HEAD_DIM).astype(BF16)
    return dict(
        attn_norm=[attn_norm[l][None, :] for l in range(n_layers)],
        mlp_norm=[mlp_norm[l][None, :] for l in range(n_layers)],
        sgu_wt=[tri[l].astype(BF16) for l in range(n_layers)],
        sgu_wtt=[tri[l].transpose(0, 2, 1).astype(BF16) for l in range(n_layers)],
        sgu_bb=[jnp.repeat(sgu_b[l].T, HEAD_DIM, axis=1) for l in range(n_layers)],
        conv_w=[conv_full[l] for l in range(n_layers)],
        q_gain=[jnp.tile(q_norm[l], PW // HEAD_DIM)[None, :] for l in range(n_layers)],
        k_gain=[jnp.tile(k_norm[l], PW // HEAD_DIM)[None, :] for l in range(n_layers)],
        bd=bd,
    )


def kernel(x, attn_norm, w_in, sgu_w, sgu_b, conv_w, q_norm, k_norm, w_out, mlp_norm, w_mlp_in, w_mlp_out, loss_target, m_attn_norm, m_w_in, m_sgu_w, m_sgu_b, m_conv_w, m_q_norm, m_k_norm, m_w_out, m_mlp_norm, m_w_mlp_in, m_w_mlp_out, v_attn_norm, v_w_in, v_sgu_w, v_sgu_b, v_conv_w, v_q_norm, v_k_norm, v_w_out, v_mlp_norm, v_w_mlp_in, v_w_mlp_out):
    n_layers = attn_norm.shape[0]
    weights = dict(attn_norm=attn_norm, w_in=w_in, sgu_w=sgu_w, sgu_b=sgu_b, conv_w=conv_w, q_norm=q_norm,
                   k_norm=k_norm, w_out=w_out, mlp_norm=mlp_norm, w_mlp_in=w_mlp_in, w_mlp_out=w_mlp_out)
    mom_m = dict(attn_norm=m_attn_norm, w_in=m_w_in, sgu_w=m_sgu_w, sgu_b=m_sgu_b, conv_w=m_conv_w,
                 q_norm=m_q_norm, k_norm=m_k_norm, w_out=m_w_out, mlp_norm=m_mlp_norm, w_mlp_in=m_w_mlp_in,
                 w_mlp_out=m_w_mlp_out)
    mom_v = dict(attn_norm=v_attn_norm, w_in=v_w_in, sgu_w=v_sgu_w, sgu_b=v_sgu_b, conv_w=v_conv_w,
                 q_norm=v_q_norm, k_norm=v_k_norm, w_out=v_w_out, mlp_norm=v_mlp_norm, w_mlp_in=v_w_mlp_in,
                 w_mlp_out=v_w_mlp_out)
    order = ("attn_norm", "w_in", "sgu_w", "sgu_b", "conv_w", "q_norm", "k_norm", "w_out", "mlp_norm",
             "w_mlp_in", "w_mlp_out")
    chip = 2 * lax.axis_index("x") + lax.axis_index("y")
    c_arr = jnp.stack([lax.axis_index("c"), chip]).astype(jnp.int32)

    conv_cols = conv_w.shape[-1]
    chip_arr = chip.astype(jnp.int32).reshape(1)
    conv_pack = jnp.pad(conv_w.reshape(-1), (0, 2048 - conv_w.size)).reshape(1, 16, 128)
    wg = _GatheredWeights()
    wg.start([("conv_w", 0), ("w_in", 0)],
             [_place_shard("place_conv_w", conv_pack, 0, chip_arr, F32),
              _place_shard("place_w_in_0", weights["w_in"], 0, chip_arr, BF16)])
    keys = [(n, l) for l in range(n_layers) for n in BIG if (n, l) != ("w_in", 0)]
    first = wg.deps()
    wg.start(keys, [_place_shard(f"place_{n}_{l}", weights[n], l, chip_arr, BF16, deps=first) for n, l in keys])
    conv_full = wg.get("conv_w", 0, wg.deps()[-1]).reshape(N_CHIPS, 2048)[:, :conv_w.size].reshape(N_CHIPS, n_layers, 3, conv_cols)
    conv_full = conv_full.transpose(1, 2, 0, 3).reshape(n_layers, 3, N_CHIPS * conv_cols)
    prm = _prepare_params(attn_norm, sgu_w, sgu_b, conv_full, q_norm, k_norm, mlp_norm)

    sink = _GradReducer(c_arr)
    loss_local, grad_x, _, smalls = _local_step(x[0], loss_target[0], prm, wg, n_layers, sink)
    loss = lax.psum(loss_local, ("x", "y", "c"))

    small_names = SMALL_REPLICATED + ("conv_w",)
    small_shapes = [(n_layers,) + tuple(smalls[0][n].shape) for n in small_names]
    packed = _pack_rows([jnp.stack([smalls[l][n] for l in range(n_layers)]) for n in small_names])
    small_send, small_recv, small_land, small_token = _small_start(packed, sink.deps())

    grads, delta, new_m, new_v = {}, {}, {}, {}

    def update(n, after):
        shp = weights[n].shape
        two_d = (shp[0] * shp[1], shp[2])
        d, nm, nv, g = _adamw(f"adamw_{n}", weights[n].reshape(two_d), sink.reduced(n, after).reshape(two_d),
                              mom_m[n].reshape(two_d), mom_v[n].reshape(two_d))
        grads[n], delta[n], new_m[n], new_v[n] = g.reshape(shp), d.reshape(shp), nm.reshape(shp), nv.reshape(shp)

    token = small_token
    for n in ("w_mlp_out", "w_mlp_in", "w_out"):
        token = sink.reduce(n, n_layers, token)
    update("w_mlp_out", token)
    token = sink.reduce("w_in", n_layers, delta["w_mlp_out"])
    update("w_mlp_in", token)
    update("w_out", delta["w_mlp_in"])
    update("w_in", delta["w_out"])
    small_land = _small_wait(packed, small_land, small_send, small_recv, delta["w_in"])
    grads.update(zip(small_names, _unpack_rows(_sum_devices(small_land), small_shapes)))
    grads["conv_w"] = lax.dynamic_slice_in_dim(grads["conv_w"], chip * conv_cols, conv_cols, axis=2)
    smalls_all = SMALL_REPLICATED + ("conv_w",)
    shapes = [weights[n].shape for n in smalls_all]
    d, nm, nv, _ = _adamw("adamw_small",
                          _pack_rows([weights[n] for n in smalls_all]), _pack_rows([grads[n] for n in smalls_all]),
                          _pack_rows([mom_m[n] for n in smalls_all]), _pack_rows([mom_v[n] for n in smalls_all]))
    for n, dd, mm, vv in zip(smalls_all, _unpack_rows(d, shapes), _unpack_rows(nm, shapes), _unpack_rows(nv, shapes)):
        delta[n], new_m[n], new_v[n] = dd, mm, vv

    return (loss, grad_x[None], *[grads[n] for n in order], *[delta[n] for n in order],
            *[new_m[n] for n in order], *[new_v[n] for n in order])
```
